```python
import jax, jax.numpy as jnp
from jax import lax
import numpy as np

D_MODEL = 1024
BATCH = 8
SEQ = 16384
DEPTH = 2

LRU_WIDTH = D_MODEL
LRU_HEADS = 8
LRU_HEAD_DIM = LRU_WIDTH // LRU_HEADS
LRU_CONV = 4
LRU_C = 8.0
SC_WIDTH = D_MODEL
SC_HEADS = 8
SC_CONV = 3
POOL_WIDTH = 2 * D_MODEL
POOL_WINDOWS = (2, 4, 8, 16)
POOL_GROUPS = len(POOL_WINDOWS)
POOL_GROUP_DIM = POOL_WIDTH // POOL_GROUPS
EVEN_IN = 2 * LRU_WIDTH + 4 * SC_WIDTH
EVEN_SPLITS = (LRU_WIDTH, 2 * LRU_WIDTH, 2 * LRU_WIDTH + SC_WIDTH,
               2 * LRU_WIDTH + 2 * SC_WIDTH, 2 * LRU_WIDTH + 3 * SC_WIDTH)
N_EVEN = (DEPTH + 1) // 2
N_ODD = DEPTH // 2
EPS = 1e-6

kernel_name = "hybrid_rglru_shortconv_pool_adaln"


def rmsnorm(x, g):
    xf = x.astype(jnp.float32)
    y = xf * lax.rsqrt(jnp.mean(xf * xf, axis=-1, keepdims=True) + EPS)
    return (y * g.astype(jnp.float32)).astype(x.dtype)


def causal_dwconv(x, w, b=None):
    k_width = w.shape[0]
    s = x.shape[1]
    xp = jnp.pad(x, ((0, 0), (k_width - 1, 0), (0, 0)))
    y = xp[:, 0:s] * w[0]
    for k in range(1, k_width):
        y = y + xp[:, k:k + s] * w[k]
    if b is not None:
        y = y + b
    return y


def rg_lru(x, w_a, b_a, w_x, b_x, lam):
    bsz, s, width = x.shape
    xh = x.reshape(bsz, s, LRU_HEADS, LRU_HEAD_DIM)
    r = jax.nn.sigmoid(jnp.einsum('bshi,hij->bshj', xh, w_a).reshape(bsz, s, width) + b_a)
    i = jax.nn.sigmoid(jnp.einsum('bshi,hij->bshj', xh, w_x).reshape(bsz, s, width) + b_x)
    log_a = -LRU_C * r.astype(jnp.float32) * jax.nn.softplus(-lam.astype(jnp.float32))
    a = jnp.exp(log_a)
    mult = jnp.sqrt(-jnp.expm1(2.0 * log_a))
    reset = (jnp.arange(s) == 0)[None, :, None]
    mult = jnp.where(reset, jnp.ones_like(mult), mult)
    u = mult * (i.astype(jnp.float32) * x.astype(jnp.float32))

    def combine(left, right):
        a_l, b_l = left
        a_r, b_r = right
        return a_l * a_r, a_r * b_l + b_r

    _, h = lax.associative_scan(combine, (a, u), axis=1)
    return h.astype(x.dtype)


def causal_window_mean(x, window):
    s = x.shape[1]
    cs = jnp.cumsum(x.astype(jnp.float32), axis=1)
    lag = jnp.pad(cs, ((0, 0), (window, 0), (0, 0)))[:, :s]
    count = jnp.minimum(jnp.arange(1, s + 1), window).astype(jnp.float32)[None, :, None]
    return ((cs - lag) / count).astype(x.dtype)


def even_mixer(h, w_in, conv_w, conv_b, w_a, b_a, w_x, b_x, lam, sc_conv_w, w_out):
    proj = h @ w_in
    xa, ga, gb_post, gc_pre, v, gb = jnp.split(proj, EVEN_SPLITS, axis=-1)
    ya = rg_lru(causal_dwconv(xa, conv_w, conv_b), w_a, b_a, w_x, b_x, lam)
    yb = gb_post * causal_dwconv(gc_pre * v, sc_conv_w)
    y = jnp.concatenate([ya * jax.nn.silu(ga), yb * jax.nn.silu(gb)], axis=-1)
    return y @ w_out


def odd_mixer(h, w_in, w_grp, b_grp, scale, w_out):
    bsz, s, _ = h.shape
    proj = h @ w_in
    v, g = jnp.split(proj, 2, axis=-1)
    vg = v.reshape(bsz, s, POOL_GROUPS, POOL_GROUP_DIM)
    pooled = jnp.stack([causal_window_mean(vg[:, :, k], POOL_WINDOWS[k])
                        for k in range(POOL_GROUPS)], axis=2)
    mixed = jnp.einsum('bsgc,gcd->bsgd', pooled - vg, w_grp).reshape(bsz, s, POOL_WIDTH) + b_grp
    y = mixed * scale * jax.nn.silu(g)
    return y @ w_out


def _fwd_setup_inputs(seed: int = 0) -> dict:
    key = jax.random.key(seed)
    ks = jax.random.split(key, 21)
    f32 = jnp.float32
    nrm = lambda k, shape, s: jax.random.normal(k, shape, f32) * s
    d = D_MODEL
    u = jax.random.uniform(ks[12], (N_EVEN, LRU_WIDTH), f32, minval=0.9, maxval=0.999)
    return {
        "x": nrm(ks[0], (BATCH, SEQ, d), 1.0),
        "c": nrm(ks[1], (BATCH, d), 1.0),
        "norm_g": 1.0 + nrm(ks[2], (DEPTH, d), 0.05),
        "mod_w": nrm(ks[3], (DEPTH, d, 3 * d), 0.5 * d ** -0.5),
        "mod_b": nrm(ks[4], (DEPTH, 3 * d), 0.02),
        "hy_w_in": nrm(ks[5], (N_EVEN, d, EVEN_IN), d ** -0.5),
        "hy_conv_w": nrm(ks[6], (N_EVEN, LRU_CONV, LRU_WIDTH), LRU_CONV ** -0.5),
        "hy_conv_b": nrm(ks[7], (N_EVEN, LRU_WIDTH), 0.02),
        "lru_w_a": nrm(ks[8], (N_EVEN, LRU_HEADS, LRU_HEAD_DIM, LRU_HEAD_DIM), LRU_HEAD_DIM ** -0.5),
        "lru_b_a": nrm(ks[9], (N_EVEN, LRU_WIDTH), 0.02),
        "lru_w_x": nrm(ks[10], (N_EVEN, LRU_HEADS, LRU_HEAD_DIM, LRU_HEAD_DIM), LRU_HEAD_DIM ** -0.5),
        "lru_b_x": nrm(ks[11], (N_EVEN, LRU_WIDTH), 0.02),
        "lru_lambda": jnp.log(u) - jnp.log1p(-u),
        "sc_conv_w": nrm(ks[13], (N_EVEN, SC_CONV, SC_WIDTH), SC_CONV ** -0.5),
        "hy_w_out": nrm(ks[14], (N_EVEN, LRU_WIDTH + SC_WIDTH, d), (LRU_WIDTH + SC_WIDTH) ** -0.5),
        "pool_w_in": nrm(ks[15], (N_ODD, d, 2 * POOL_WIDTH), d ** -0.5),
        "pool_w_grp": nrm(ks[16], (N_ODD, POOL_GROUPS, POOL_GROUP_DIM, POOL_GROUP_DIM), POOL_GROUP_DIM ** -0.5),
        "pool_b_grp": nrm(ks[17], (N_ODD, POOL_WIDTH), 0.02),
        "pool_scale": 1.0 + nrm(ks[18], (N_ODD, POOL_WIDTH), 0.1),
        "pool_w_out": nrm(ks[19], (N_ODD, POOL_WIDTH, d), POOL_WIDTH ** -0.5),
        "final_g": 1.0 + nrm(ks[20], (d,), 0.05),
    }


def _fwd_reference(x, c, norm_g, mod_w, mod_b, hy_w_in, hy_conv_w, hy_conv_b, lru_w_a, lru_b_a,
              lru_w_x, lru_b_x, lru_lambda, sc_conv_w, hy_w_out, pool_w_in, pool_w_grp,
              pool_b_grp, pool_scale, pool_w_out, final_g):
    c_act = jax.nn.silu(c)
    for layer in range(DEPTH):
        mod = c_act @ mod_w[layer] + mod_b[layer]
        shift, scale, gate = jnp.split(mod, 3, axis=-1)
        h = rmsnorm(x, norm_g[layer]) * (1.0 + scale[:, None, :]) + shift[:, None, :]
        if layer % 2 == 0:
            e = layer // 2
            y = even_mixer(h, hy_w_in[e], hy_conv_w[e], hy_conv_b[e], lru_w_a[e], lru_b_a[e],
                           lru_w_x[e], lru_b_x[e], lru_lambda[e], sc_conv_w[e], hy_w_out[e])
        else:
            o = layer // 2
            y = odd_mixer(h, pool_w_in[o], pool_w_grp[o], pool_b_grp[o], pool_scale[o], pool_w_out[o])
        x = x + gate[:, None, :] * y
    return rmsnorm(x, final_g)


import jax as _jax
import jax.numpy as _jnp

TWIN_FORMAT = 'train_step'
FWD_PARAMS = ['x', 'c', 'norm_g', 'mod_w', 'mod_b', 'hy_w_in', 'hy_conv_w', 'hy_conv_b', 'lru_w_a', 'lru_b_a', 'lru_w_x', 'lru_b_x', 'lru_lambda', 'sc_conv_w', 'hy_w_out', 'pool_w_in', 'pool_w_grp', 'pool_b_grp', 'pool_scale', 'pool_w_out', 'final_g']
TWIN_WEIGHTS = ['norm_g', 'mod_w', 'mod_b', 'hy_w_in', 'hy_conv_w', 'hy_conv_b', 'lru_w_a', 'lru_b_a', 'lru_w_x', 'lru_b_x', 'lru_lambda', 'sc_conv_w', 'hy_w_out', 'pool_w_in', 'pool_w_grp', 'pool_b_grp', 'pool_scale', 'pool_w_out', 'final_g']
TWIN_DIFF_INPUT = 'x'
TWIN_INPUTS = ['x', 'c', 'norm_g', 'mod_w', 'mod_b', 'hy_w_in', 'hy_conv_w', 'hy_conv_b', 'lru_w_a', 'lru_b_a', 'lru_w_x', 'lru_b_x', 'lru_lambda', 'sc_conv_w', 'hy_w_out', 'pool_w_in', 'pool_w_grp', 'pool_b_grp', 'pool_scale', 'pool_w_out', 'final_g', 'loss_target', 'm_norm_g', 'm_mod_w', 'm_mod_b', 'm_hy_w_in', 'm_hy_conv_w', 'm_hy_conv_b', 'm_lru_w_a', 'm_lru_b_a', 'm_lru_w_x', 'm_lru_b_x', 'm_lru_lambda', 'm_sc_conv_w', 'm_hy_w_out', 'm_pool_w_in', 'm_pool_w_grp', 'm_pool_b_grp', 'm_pool_scale', 'm_pool_w_out', 'm_final_g', 'v_norm_g', 'v_mod_w', 'v_mod_b', 'v_hy_w_in', 'v_hy_conv_w', 'v_hy_conv_b', 'v_lru_w_a', 'v_lru_b_a', 'v_lru_w_x', 'v_lru_b_x', 'v_lru_lambda', 'v_sc_conv_w', 'v_hy_w_out', 'v_pool_w_in', 'v_pool_w_grp', 'v_pool_b_grp', 'v_pool_scale', 'v_pool_w_out', 'v_final_g']
TWIN_OUTPUTS = ['loss', 'grad_x', 'grad_norm_g', 'grad_mod_w', 'grad_mod_b', 'grad_hy_w_in', 'grad_hy_conv_w', 'grad_hy_conv_b', 'grad_lru_w_a', 'grad_lru_b_a', 'grad_lru_w_x', 'grad_lru_b_x', 'grad_lru_lambda', 'grad_sc_conv_w', 'grad_hy_w_out', 'grad_pool_w_in', 'grad_pool_w_grp', 'grad_pool_b_grp', 'grad_pool_scale', 'grad_pool_w_out', 'grad_final_g', 'delta_norm_g', 'delta_mod_w', 'delta_mod_b', 'delta_hy_w_in', 'delta_hy_conv_w', 'delta_hy_conv_b', 'delta_lru_w_a', 'delta_lru_b_a', 'delta_lru_w_x', 'delta_lru_b_x', 'delta_lru_lambda', 'delta_sc_conv_w', 'delta_hy_w_out', 'delta_pool_w_in', 'delta_pool_w_grp', 'delta_pool_b_grp', 'delta_pool_scale', 'delta_pool_w_out', 'delta_final_g', 'new_m_norm_g', 'new_m_mod_w', 'new_m_mod_b', 'new_m_hy_w_in', 'new_m_hy_conv_w', 'new_m_hy_conv_b', 'new_m_lru_w_a', 'new_m_lru_b_a', 'new_m_lru_w_x', 'new_m_lru_b_x', 'new_m_lru_lambda', 'new_m_sc_conv_w', 'new_m_hy_w_out', 'new_m_pool_w_in', 'new_m_pool_w_grp', 'new_m_pool_b_grp', 'new_m_pool_scale', 'new_m_pool_w_out', 'new_m_final_g', 'new_v_norm_g', 'new_v_mod_w', 'new_v_mod_b', 'new_v_hy_w_in', 'new_v_hy_conv_w', 'new_v_hy_conv_b', 'new_v_lru_w_a', 'new_v_lru_b_a', 'new_v_lru_w_x', 'new_v_lru_b_x', 'new_v_lru_lambda', 'new_v_sc_conv_w', 'new_v_hy_w_out', 'new_v_pool_w_in', 'new_v_pool_w_grp', 'new_v_pool_b_grp', 'new_v_pool_scale', 'new_v_pool_w_out', 'new_v_final_g']
TWIN_LEAF_KINDS = {'loss': 'loss', 'grad_x': 'grad_x', 'grad_norm_g': 'grad_w', 'grad_mod_w': 'grad_w', 'grad_mod_b': 'grad_w', 'grad_hy_w_in': 'grad_w', 'grad_hy_conv_w': 'grad_w', 'grad_hy_conv_b': 'grad_w', 'grad_lru_w_a': 'grad_w', 'grad_lru_b_a': 'grad_w', 'grad_lru_w_x': 'grad_w', 'grad_lru_b_x': 'grad_w', 'grad_lru_lambda': 'grad_w', 'grad_sc_conv_w': 'grad_w', 'grad_hy_w_out': 'grad_w', 'grad_pool_w_in': 'grad_w', 'grad_pool_w_grp': 'grad_w', 'grad_pool_b_grp': 'grad_w', 'grad_pool_scale': 'grad_w', 'grad_pool_w_out': 'grad_w', 'grad_final_g': 'grad_w', 'delta_norm_g': 'delta_w', 'delta_mod_w': 'delta_w', 'delta_mod_b': 'delta_w', 'delta_hy_w_in': 'delta_w', 'delta_hy_conv_w': 'delta_w', 'delta_hy_conv_b': 'delta_w', 'delta_lru_w_a': 'delta_w', 'delta_lru_b_a': 'delta_w', 'delta_lru_w_x': 'delta_w', 'delta_lru_b_x': 'delta_w', 'delta_lru_lambda': 'delta_w', 'delta_sc_conv_w': 'delta_w', 'delta_hy_w_out': 'delta_w', 'delta_pool_w_in': 'delta_w', 'delta_pool_w_grp': 'delta_w', 'delta_pool_b_grp': 'delta_w', 'delta_pool_scale': 'delta_w', 'delta_pool_w_out': 'delta_w', 'delta_final_g': 'delta_w', 'new_m_norm_g': 'new_m', 'new_m_mod_w': 'new_m', 'new_m_mod_b': 'new_m', 'new_m_hy_w_in': 'new_m', 'new_m_hy_conv_w': 'new_m', 'new_m_hy_conv_b': 'new_m', 'new_m_lru_w_a': 'new_m', 'new_m_lru_b_a': 'new_m', 'new_m_lru_w_x': 'new_m', 'new_m_lru_b_x': 'new_m', 'new_m_lru_lambda': 'new_m', 'new_m_sc_conv_w': 'new_m', 'new_m_hy_w_out': 'new_m', 'new_m_pool_w_in': 'new_m', 'new_m_pool_w_grp': 'new_m', 'new_m_pool_b_grp': 'new_m', 'new_m_pool_scale': 'new_m', 'new_m_pool_w_out': 'new_m', 'new_m_final_g': 'new_m', 'new_v_norm_g': 'new_v', 'new_v_mod_w': 'new_v', 'new_v_mod_b': 'new_v', 'new_v_hy_w_in': 'new_v', 'new_v_hy_conv_w': 'new_v', 'new_v_hy_conv_b': 'new_v', 'new_v_lru_w_a': 'new_v', 'new_v_lru_b_a': 'new_v', 'new_v_lru_w_x': 'new_v', 'new_v_lru_b_x': 'new_v', 'new_v_lru_lambda': 'new_v', 'new_v_sc_conv_w': 'new_v', 'new_v_hy_w_out': 'new_v', 'new_v_pool_w_in': 'new_v', 'new_v_pool_w_grp': 'new_v', 'new_v_pool_b_grp': 'new_v', 'new_v_pool_scale': 'new_v', 'new_v_pool_w_out': 'new_v', 'new_v_final_g': 'new_v'}


def _forward(args):
    return _fwd_reference(*[args[k] for k in FWD_PARAMS])


def _output_shape():
    def fwd():
        inp = _fwd_setup_inputs(0)
        return _fwd_reference(*[inp[k] for k in FWD_PARAMS])
    out = _jax.eval_shape(fwd)
    return out.shape, out.dtype

N_MICROBATCH = 1
ADAM_LR = 0.001
ADAM_B1 = 0.9
ADAM_B2 = 0.999
ADAM_EPS = 1e-08
ADAM_WD = 0.01
ADAM_STEP = 10
PER_EXAMPLE_BATCH_AXIS = {'x': 0, 'c': 0, 'loss_target': 0}
SHARED_INPUTS = []
_WEIGHT_DTYPES = {'norm_g': _jnp.float32, 'mod_w': _jnp.float32, 'mod_b': _jnp.float32, 'hy_w_in': _jnp.float32, 'hy_conv_w': _jnp.float32, 'hy_conv_b': _jnp.float32, 'lru_w_a': _jnp.float32, 'lru_b_a': _jnp.float32, 'lru_w_x': _jnp.float32, 'lru_b_x': _jnp.float32, 'lru_lambda': _jnp.float32, 'sc_conv_w': _jnp.float32, 'hy_w_out': _jnp.float32, 'pool_w_in': _jnp.float32, 'pool_w_grp': _jnp.float32, 'pool_b_grp': _jnp.float32, 'pool_scale': _jnp.float32, 'pool_w_out': _jnp.float32, 'final_g': _jnp.float32}
MOMENT_SCALE = {'norm_g': 1.134917e-01, 'mod_w': 1.383497e-01, 'mod_b': 2.796263e-01, 'hy_w_in': 6.058659e-02, 'hy_conv_w': 5.394474e-02, 'hy_conv_b': 1.418320e-01, 'lru_w_a': 8.466496e-03, 'lru_b_a': 1.057416e-02, 'lru_w_x': 1.511873e-02, 'lru_b_x': 1.915011e-02, 'lru_lambda': 2.582054e-02, 'sc_conv_w': 6.371597e-02, 'hy_w_out': 8.256747e-02, 'pool_w_in': 4.557837e-02, 'pool_w_grp': 4.484444e-02, 'pool_b_grp': 5.286093e-02, 'pool_scale': 4.547955e-02, 'pool_w_out': 6.346010e-02, 'final_g': 1.284078e+02}


def _to_microbatches(a, axis):
    t = _jnp.moveaxis(a, axis, 0)
    t = t.reshape((N_MICROBATCH, t.shape[0] // N_MICROBATCH) + t.shape[1:])
    return _jnp.moveaxis(t, 1, axis + 1)


def setup_inputs(seed: int = 0) -> dict:
    inp = _fwd_setup_inputs(seed)
    key = _jax.random.fold_in(_jax.random.key(seed), 7919)
    shape, _ = _output_shape()
    out = dict(inp)
    out["loss_target"] = _jax.random.normal(_jax.random.fold_in(key, 0), shape, _jnp.float32)
    for i, name in enumerate(TWIN_WEIGHTS):
        w = inp[name].astype(_jnp.float32)
        if MOMENT_SCALE is None:
            s = _jnp.sqrt(_jnp.mean(_jnp.square(w)) + 1e-30)
        else:
            s = MOMENT_SCALE[name]
        km, kv = _jax.random.split(_jax.random.fold_in(key, i + 1))
        out[name] = w
        out["m_" + name] = s * _jax.random.normal(km, w.shape, _jnp.float32)
        out["v_" + name] = (s * s) * _jax.random.uniform(kv, w.shape, _jnp.float32, 0.5, 1.5)
    if N_MICROBATCH > 1:
        for name, axis in PER_EXAMPLE_BATCH_AXIS.items():
            out[name] = _to_microbatches(out[name], axis)
    return {'x': out['x'], 'c': out['c'], 'norm_g': out['norm_g'], 'mod_w': out['mod_w'], 'mod_b': out['mod_b'], 'hy_w_in': out['hy_w_in'], 'hy_conv_w': out['hy_conv_w'], 'hy_conv_b': out['hy_conv_b'], 'lru_w_a': out['lru_w_a'], 'lru_b_a': out['lru_b_a'], 'lru_w_x': out['lru_w_x'], 'lru_b_x': out['lru_b_x'], 'lru_lambda': out['lru_lambda'], 'sc_conv_w': out['sc_conv_w'], 'hy_w_out': out['hy_w_out'], 'pool_w_in': out['pool_w_in'], 'pool_w_grp': out['pool_w_grp'], 'pool_b_grp': out['pool_b_grp'], 'pool_scale': out['pool_scale'], 'pool_w_out': out['pool_w_out'], 'final_g': out['final_g'], 'loss_target': out['loss_target'], 'm_norm_g': out['m_norm_g'], 'm_mod_w': out['m_mod_w'], 'm_mod_b': out['m_mod_b'], 'm_hy_w_in': out['m_hy_w_in'], 'm_hy_conv_w': out['m_hy_conv_w'], 'm_hy_conv_b': out['m_hy_conv_b'], 'm_lru_w_a': out['m_lru_w_a'], 'm_lru_b_a': out['m_lru_b_a'], 'm_lru_w_x': out['m_lru_w_x'], 'm_lru_b_x': out['m_lru_b_x'], 'm_lru_lambda': out['m_lru_lambda'], 'm_sc_conv_w': out['m_sc_conv_w'], 'm_hy_w_out': out['m_hy_w_out'], 'm_pool_w_in': out['m_pool_w_in'], 'm_pool_w_grp': out['m_pool_w_grp'], 'm_pool_b_grp': out['m_pool_b_grp'], 'm_pool_scale': out['m_pool_scale'], 'm_pool_w_out': out['m_pool_w_out'], 'm_final_g': out['m_final_g'], 'v_norm_g': out['v_norm_g'], 'v_mod_w': out['v_mod_w'], 'v_mod_b': out['v_mod_b'], 'v_hy_w_in': out['v_hy_w_in'], 'v_hy_conv_w': out['v_hy_conv_w'], 'v_hy_conv_b': out['v_hy_conv_b'], 'v_lru_w_a': out['v_lru_w_a'], 'v_lru_b_a': out['v_lru_b_a'], 'v_lru_w_x': out['v_lru_w_x'], 'v_lru_b_x': out['v_lru_b_x'], 'v_lru_lambda': out['v_lru_lambda'], 'v_sc_conv_w': out['v_sc_conv_w'], 'v_hy_w_out': out['v_hy_w_out'], 'v_pool_w_in': out['v_pool_w_in'], 'v_pool_w_grp': out['v_pool_w_grp'], 'v_pool_b_grp': out['v_pool_b_grp'], 'v_pool_scale': out['v_pool_scale'], 'v_pool_w_out': out['v_pool_w_out'], 'v_final_g': out['v_final_g']}


def _loss(weights, diff, rest, loss_target):
    with _jax.named_scope("forward"):
        args = {**rest, TWIN_DIFF_INPUT: diff, **{k: w.astype(_WEIGHT_DTYPES[k]) for k, w in weights.items()}}
        y = _forward(args)
    with _jax.named_scope("loss_head"):
        err = _jnp.square(y.astype(_jnp.float32) - loss_target)
        return 0.5 * _jnp.sum(_jnp.mean(err, axis=-1)) if err.ndim else 0.5 * err


def _adamw(w, g, m, v):
    m = ADAM_B1 * m + (1.0 - ADAM_B1) * g
    v = ADAM_B2 * v + (1.0 - ADAM_B2) * _jnp.square(g)
    m_hat = m / (1.0 - ADAM_B1 ** ADAM_STEP)
    v_hat = v / (1.0 - ADAM_B2 ** ADAM_STEP)
    delta = -ADAM_LR * (m_hat / (_jnp.sqrt(v_hat) + ADAM_EPS) + ADAM_WD * w)
    return delta, m, v


def reference(x, c, norm_g, mod_w, mod_b, hy_w_in, hy_conv_w, hy_conv_b, lru_w_a, lru_b_a, lru_w_x, lru_b_x, lru_lambda, sc_conv_w, hy_w_out, pool_w_in, pool_w_grp, pool_b_grp, pool_scale, pool_w_out, final_g, loss_target, m_norm_g, m_mod_w, m_mod_b, m_hy_w_in, m_hy_conv_w, m_hy_conv_b, m_lru_w_a, m_lru_b_a, m_lru_w_x, m_lru_b_x, m_lru_lambda, m_sc_conv_w, m_hy_w_out, m_pool_w_in, m_pool_w_grp, m_pool_b_grp, m_pool_scale, m_pool_w_out, m_final_g, v_norm_g, v_mod_w, v_mod_b, v_hy_w_in, v_hy_conv_w, v_hy_conv_b, v_lru_w_a, v_lru_b_a, v_lru_w_x, v_lru_b_x, v_lru_lambda, v_sc_conv_w, v_hy_w_out, v_pool_w_in, v_pool_w_grp, v_pool_b_grp, v_pool_scale, v_pool_w_out, v_final_g):
    given = dict(x=x, c=c, norm_g=norm_g, mod_w=mod_w, mod_b=mod_b, hy_w_in=hy_w_in, hy_conv_w=hy_conv_w, hy_conv_b=hy_conv_b, lru_w_a=lru_w_a, lru_b_a=lru_b_a, lru_w_x=lru_w_x, lru_b_x=lru_b_x, lru_lambda=lru_lambda, sc_conv_w=sc_conv_w, hy_w_out=hy_w_out, pool_w_in=pool_w_in, pool_w_grp=pool_w_grp, pool_b_grp=pool_b_grp, pool_scale=pool_scale, pool_w_out=pool_w_out, final_g=final_g, loss_target=loss_target, m_norm_g=m_norm_g, m_mod_w=m_mod_w, m_mod_b=m_mod_b, m_hy_w_in=m_hy_w_in, m_hy_conv_w=m_hy_conv_w, m_hy_conv_b=m_hy_conv_b, m_lru_w_a=m_lru_w_a, m_lru_b_a=m_lru_b_a, m_lru_w_x=m_lru_w_x, m_lru_b_x=m_lru_b_x, m_lru_lambda=m_lru_lambda, m_sc_conv_w=m_sc_conv_w, m_hy_w_out=m_hy_w_out, m_pool_w_in=m_pool_w_in, m_pool_w_grp=m_pool_w_grp, m_pool_b_grp=m_pool_b_grp, m_pool_scale=m_pool_scale, m_pool_w_out=m_pool_w_out, m_final_g=m_final_g, v_norm_g=v_norm_g, v_mod_w=v_mod_w, v_mod_b=v_mod_b, v_hy_w_in=v_hy_w_in, v_hy_conv_w=v_hy_conv_w, v_hy_conv_b=v_hy_conv_b, v_lru_w_a=v_lru_w_a, v_lru_b_a=v_lru_b_a, v_lru_w_x=v_lru_w_x, v_lru_b_x=v_lru_b_x, v_lru_lambda=v_lru_lambda, v_sc_conv_w=v_sc_conv_w, v_hy_w_out=v_hy_w_out, v_pool_w_in=v_pool_w_in, v_pool_w_grp=v_pool_w_grp, v_pool_b_grp=v_pool_b_grp, v_pool_scale=v_pool_scale, v_pool_w_out=v_pool_w_out, v_final_g=v_final_g)
    weights = {n: given[n] for n in TWIN_WEIGHTS}
    shared = {n: given[n] for n in SHARED_INPUTS}
    per_example = {n: given[n] for n in ['x', 'c']}
    grad_fn = _jax.value_and_grad(_loss, argnums=(0, 1))

    def one_microbatch(ex, loss_target):
        ex = dict(ex)
        diff = ex.pop(TWIN_DIFF_INPUT)
        return grad_fn(weights, diff, {**shared, **ex}, loss_target)

    if N_MICROBATCH == 1:
        loss, (grad_w, grad_x) = one_microbatch(per_example, given["loss_target"])
    else:
        def body(carry, xs):
            loss_sum, grad_sum = carry
            l_k, (gw_k, gx_k) = one_microbatch(xs[0], xs[1])
            with _jax.named_scope("update"):
                return (loss_sum + l_k, _jax.tree.map(_jnp.add, grad_sum, gw_k)), gx_k

        init = (_jnp.zeros((), _jnp.float32), _jax.tree.map(_jnp.zeros_like, weights))
        (loss, grad_w), grad_x = _jax.lax.scan(body, init, (per_example, given["loss_target"]))
    with _jax.named_scope("update"):
        delta_w, new_m, new_v = {}, {}, {}
        for n in TWIN_WEIGHTS:
            delta_w[n], new_m[n], new_v[n] = _adamw(weights[n], grad_w[n], given["m_" + n], given["v_" + n])
    return (loss, grad_x, *[grad_w[n] for n in TWIN_WEIGHTS], *[delta_w[n] for n in TWIN_WEIGHTS],
            *[new_m[n] for n in TWIN_WEIGHTS], *[new_v[n] for n in TWIN_WEIGHTS])
```

```python
import jax
import jax.numpy as jnp
from jax import lax
from jax.experimental import pallas as pl
from jax.experimental.pallas import tpu as pltpu

F32, BF16 = jnp.float32, jnp.bfloat16
D = 1024
RMS_EPS = 1e-6
LRU_C = 8.0
LRU_HEADS, LRU_HEAD_DIM = 8, 128
POOL_WINDOWS = (2, 4, 8, 16)
POOL_GROUP_DIM = 512
ADAM_LR, ADAM_B1, ADAM_B2, ADAM_EPS, ADAM_WD, ADAM_STEP = 0.001, 0.9, 0.999, 1e-08, 0.01, 10
MESH = pl.DeviceIdType.MESH
N_DEV, N_CHIP = 8, 4
SUBLANES = 8
BF16_ROWS = 16
POOL_HALO = 16
TS_PROJ, TS_MIX, TS_WGRAD, TS_DGRAD = 1024, 256, 1024, 256
SMALL_ROWS = 64
ANY = pl.BlockSpec(memory_space=pl.ANY)
VMEM = pl.BlockSpec(memory_space=pltpu.VMEM)
NT = (((1,), (1,)), ((), ()))
TN = (((0,), (0,)), ((), ()))


def _cp(sem=None, vmem_mb=56):
    kw = dict(vmem_limit_bytes=vmem_mb * 2 ** 20)
    if sem is not None:
        kw["dimension_semantics"] = sem
    return pltpu.CompilerParams(**kw)


def _tile(n, t):
    return min(n, t)


def _pos():
    return lax.axis_index("x"), lax.axis_index("y"), lax.axis_index("c")


def _flip(v, f):
    return 1 - v if f else v


def _sigmoid(z):
    return 1.0 / (1.0 + jnp.exp(-z))


def _rows(n, c):
    return lax.broadcasted_iota(jnp.int32, (n, c), 0)


def _down(a, d):
    return a if d == 0 else pltpu.roll(a, d, 0)


def _up(a, d):
    return a if d == 0 else pltpu.roll(a, a.shape[0] - d, 0)


def _scan_fwd(a, u):
    n = a.shape[0]
    rows = _rows(n, a.shape[1])
    p, g, d = a, u, 1
    while d < n:
        keep = rows >= d
        g = g + p * jnp.where(keep, _down(g, d), 0.0)
        if 2 * d < n:
            p = p * jnp.where(keep, _down(p, d), 1.0)
        d *= 2
    return g


def _scan_rev(alpha, b):
    n = alpha.shape[0]
    rows = _rows(n, alpha.shape[1])
    p, g, d = alpha, b, 1
    while d < n:
        keep = rows < n - d
        g = g + p * jnp.where(keep, _up(g, d), 0.0)
        if 2 * d < n:
            p = p * jnp.where(keep, _up(p, d), 1.0)
        d *= 2
    return g


def _conv_taps(ext, halo, n, width):
    return [_down(ext, width - 1 - k)[halo:halo + n] for k in range(width)]


def _lru_gates(xc, wa_ref, ba, wx_ref, bx):
    xb = xc.astype(BF16)
    pa, px = [], []
    for h in range(LRU_HEADS):
        xh = xb[:, h * LRU_HEAD_DIM:(h + 1) * LRU_HEAD_DIM]
        pa.append(jnp.dot(xh, wa_ref[h], preferred_element_type=F32))
        px.append(jnp.dot(xh, wx_ref[h], preferred_element_type=F32))
    r = _sigmoid(jnp.concatenate(pa, axis=1) + ba)
    ig = _sigmoid(jnp.concatenate(px, axis=1) + bx)
    return r, ig


def _softplus_neg(lam):
    return jnp.maximum(-lam, 0.0) + jnp.log1p(jnp.exp(-jnp.abs(lam)))


def _lru_decay(r, sp, first):
    big_l = (-LRU_C) * r * sp
    a = jnp.exp(big_l)
    th = jnp.tanh(big_l)
    m = jnp.sqrt(-2.0 * th / (1.0 - th))
    return a, jnp.where(first, 1.0, m)


def _pool_inv_counts(t0, n):
    t = (t0 + lax.broadcasted_iota(jnp.int32, (n, 1), 0) + 1).astype(F32)
    return [1.0 / jnp.minimum(t, float(w)) for w in POOL_WINDOWS]


def _window_sums(ext, shift):
    gd = POOL_GROUP_DIM
    out = []
    s = ext
    for k in range(len(POOL_WINDOWS)):
        s = s + shift(s, 2 ** k)
        out.append(s[:, 0:gd])
        if k + 1 < len(POOL_WINDOWS):
            s = s[:, gd:]
    return out


SW_ROWS, SW_COLS = 16, 2 * D
SW_CONV, SW_SC, SW_POOL_B, SW_POOL_S = 0, 4, 8, 9


def _mod_fwd(c8, mod_w, mod_b, conv_w, sc_w, pool_b, pool_s):
    nw = mod_w.shape[2]
    cq, pq = conv_w.shape[1], pool_b.shape[1]

    def body(c_ref, w_ref, b_ref, cw_ref, sw_ref, pb_ref, ps_ref, ca_ref, mod_ref, small_ref,
             cslot, mslot, msend, pslot, psend, s1, r1, s2, r2, s3, r3):
        x, y, c = _pos()
        me = 4 * x + 2 * y + c
        chip = 2 * x + y
        first = []
        for r in range(1, N_DEV):
            fx, fy, fc = (r >> 2) & 1, (r >> 1) & 1, r & 1
            cp = pltpu.make_async_remote_copy(
                src_ref=c_ref, dst_ref=cslot.at[me], send_sem=s1.at[r - 1], recv_sem=r1.at[r - 1],
                device_id=(_flip(x, fx), _flip(y, fy), _flip(c, fc)), device_id_type=MESH)
            cp.start()
            first.append(cp)
        cslot[me] = c_ref[...]
        for cp in first:
            cp.wait()
        rows = _rows(SUBLANES, D)
        call = jnp.zeros((SUBLANES, D), F32)
        for d in range(N_DEV):
            call = jnp.where(rows == d, cslot[d], call)
        ca = call * _sigmoid(call)
        ca_ref[...] = ca
        for l in range(2):
            msend[l] = jnp.dot(ca, w_ref[l], precision=lax.Precision.HIGHEST, preferred_element_type=F32)
        psend[...] = jnp.zeros_like(psend)
        psend[SW_CONV:SW_CONV + 4, 0:cq] = cw_ref[...]
        psend[SW_SC:SW_SC + 3, 0:cq] = sw_ref[...]
        psend[SW_POOL_B:SW_POOL_B + 1, :] = pb_ref[...]
        psend[SW_POOL_S:SW_POOL_S + 1, :] = ps_ref[...]
        second = []
        for q, (fx, fy) in enumerate(((1, 0), (0, 1), (1, 1))):
            peer = (_flip(x, fx), _flip(y, fy), c)
            for src, dst, ss, rs in ((msend, mslot, s2, r2), (psend, pslot, s3, r3)):
                cp = pltpu.make_async_remote_copy(src_ref=src, dst_ref=dst.at[chip], send_sem=ss.at[q], recv_sem=rs.at[q],
                                                  device_id=peer, device_id_type=MESH)
                cp.start()
                second.append(cp)
        mslot[chip] = msend[...]
        pslot[chip] = psend[...]
        for cp in second:
            cp.wait()
        small_ref[...] = jnp.zeros_like(small_ref)
        for j in range(N_CHIP):
            for l in range(2):
                mod_ref[l, :, j * nw:(j + 1) * nw] = mslot[j, l] + b_ref[l:l + 1, j * nw:(j + 1) * nw]
            small_ref[0:SUBLANES, j * cq:(j + 1) * cq] = pslot[j, 0:SUBLANES, 0:cq]
            small_ref[SUBLANES:SW_ROWS, j * pq:(j + 1) * pq] = pslot[j, SUBLANES:SW_ROWS, :]

    args = (c8, mod_w, mod_b, conv_w, sc_w, pool_b, pool_s)
    dma3 = pltpu.SemaphoreType.DMA((N_CHIP - 1,))
    return pl.pallas_call(
        body, name="mod_fwd",
        in_specs=[VMEM] * len(args), out_specs=[VMEM] * 3,
        out_shape=[jax.ShapeDtypeStruct((SUBLANES, D), F32), jax.ShapeDtypeStruct((2, SUBLANES, N_CHIP * nw), F32),
                   jax.ShapeDtypeStruct((SW_ROWS, SW_COLS), F32)],
        scratch_shapes=[pltpu.VMEM((N_DEV, SUBLANES, D), F32), pltpu.VMEM((N_CHIP, 2, SUBLANES, nw), F32),
                        pltpu.VMEM((2, SUBLANES, nw), F32), pltpu.VMEM((N_CHIP, SW_ROWS, pq), F32),
                        pltpu.VMEM((SW_ROWS, pq), F32),
                        pltpu.SemaphoreType.DMA((N_DEV - 1,)), pltpu.SemaphoreType.DMA((N_DEV - 1,)),
                        dma3, dma3, dma3, dma3],
        compiler_params=_cp(),
    )(*args)


def _wcast(ws):
    def body(*refs):
        n = len(refs) // 2
        for a in range(n):
            refs[n + a][...] = refs[a][...].astype(BF16)

    return pl.pallas_call(
        body, name="wcast", in_specs=[VMEM] * len(ws), out_specs=[VMEM] * len(ws),
        out_shape=[jax.ShapeDtypeStruct(w.shape, BF16) for w in ws], compiler_params=_cp(),
    )(*ws)


def _wgather(shards):
    n = len(shards)

    def body(*refs):
        ins, outs = refs[:n], refs[n:2 * n]
        lsem, ssem, rsem, fssem, frsem = refs[2 * n:]
        x, y, c = _pos()
        chip = 2 * x + y
        sib = (x, y, 1 - c)
        flips = ((1, 0), (0, 1), (1, 1))
        local = [pltpu.make_async_copy(ins[a], outs[a].at[chip], lsem.at[a]) for a in range(n)]
        for cp in local:
            cp.start()

        def half(a, which):
            hr = shards[a].shape[0] // 2
            return pl.ds(pl.multiple_of(which * hr, BF16_ROWS), hr)

        sends = []
        for a in range(n):
            for q, (fx, fy) in enumerate(flips):
                cp = pltpu.make_async_remote_copy(
                    src_ref=ins[a].at[half(a, c), :], dst_ref=outs[a].at[chip, half(a, c), :],
                    send_sem=ssem.at[3 * a + q], recv_sem=rsem.at[3 * a + q],
                    device_id=(_flip(x, fx), _flip(y, fy), c), device_id_type=MESH)
                cp.start()
                sends.append(cp)
        passed = []
        for a in range(n):
            for q, (fx, fy) in enumerate(flips):
                src_chip = 2 * _flip(x, fx) + _flip(y, fy)
                landed = outs[a].at[src_chip, half(a, c), :]
                pltpu.make_async_remote_copy(
                    src_ref=landed, dst_ref=landed, send_sem=ssem.at[3 * a + q], recv_sem=rsem.at[3 * a + q],
                    device_id=sib, device_id_type=MESH).wait_recv()
                cp = pltpu.make_async_remote_copy(
                    src_ref=landed, dst_ref=landed, send_sem=fssem.at[3 * a + q], recv_sem=frsem.at[3 * a + q],
                    device_id=sib, device_id_type=MESH)
                cp.start()
                passed.append(cp)
        for a in range(n):
            for q, (fx, fy) in enumerate(flips):
                src_chip = 2 * _flip(x, fx) + _flip(y, fy)
                other = outs[a].at[src_chip, half(a, 1 - c), :]
                pltpu.make_async_remote_copy(
                    src_ref=other, dst_ref=other, send_sem=fssem.at[3 * a + q], recv_sem=frsem.at[3 * a + q],
                    device_id=sib, device_id_type=MESH).wait_recv()
        for cp in sends + passed:
            cp.wait_send()
        for cp in local:
            cp.wait()

    return pl.pallas_call(
        body, name="wgather", in_specs=[ANY] * n, out_specs=[ANY] * n,
        out_shape=[jax.ShapeDtypeStruct((N_CHIP,) + s.shape, BF16) for s in shards],
        scratch_shapes=[pltpu.SemaphoreType.DMA((n,))] + [pltpu.SemaphoreType.DMA((3 * n,))] * 4,
        compiler_params=_cp(),
    )(*shards)


def _norm_proj(x, g, sc, sh, w, name):
    s_len, nb = x.shape[0], w.shape[2]
    ts = _tile(s_len, TS_PROJ)

    def body(x_ref, g_ref, sc_ref, sh_ref, w_ref, h_ref, p_ref):
        @pl.when(pl.program_id(1) == 0)
        def _():
            xv = x_ref[...]
            r = lax.rsqrt(jnp.mean(xv * xv, axis=-1, keepdims=True) + RMS_EPS)
            h_ref[...] = (xv * r * (g_ref[...] * (1.0 + sc_ref[...])) + sh_ref[...]).astype(BF16)

        p_ref[...] = jnp.dot(h_ref[...], w_ref[...], preferred_element_type=F32).astype(BF16)

    vec = pl.BlockSpec((1, D), lambda i, j: (0, 0))
    return pl.pallas_call(
        body, name=name, grid=(s_len // ts, N_CHIP),
        in_specs=[pl.BlockSpec((ts, D), lambda i, j: (i, 0)), vec, vec, vec,
                  pl.BlockSpec((None, D, nb), lambda i, j: (j, 0, 0))],
        out_specs=[pl.BlockSpec((ts, D), lambda i, j: (i, 0)), pl.BlockSpec((ts, nb), lambda i, j: (i, j))],
        out_shape=[jax.ShapeDtypeStruct((s_len, D), BF16), jax.ShapeDtypeStruct((s_len, N_CHIP * nb), BF16)],
        compiler_params=_cp(("parallel", "arbitrary")),
    )(x, g, sc, sh, w)


def _l0_mix(proj, x, gate, cw, cb, wa, ba, wx, bx, lam, sw, wo):
    s_len = x.shape[0]
    ts = _tile(s_len, TS_MIX)
    hl = SUBLANES

    def body(p_ref, x_ref, gate_ref, cw_ref, cb_ref, wa_ref, ba_ref, wx_ref, bx_ref, lam_ref, sw_ref, wo_ref,
             x1_ref, h_ref, y_ref, cxa, czz, chh):
        i = pl.program_id(0)

        @pl.when(i == 0)
        def _():
            cxa[...] = jnp.zeros_like(cxa)
            czz[...] = jnp.zeros_like(czz)
            chh[...] = jnp.zeros_like(chh)

        xa, ga, gbp, gcp, v, gb = [p_ref[:, k * D:(k + 1) * D].astype(F32) for k in range(6)]
        rows = _rows(ts, D)
        taps = _conv_taps(jnp.concatenate([cxa[...], xa], axis=0), hl, ts, 4)
        xc = cb_ref[...] + sum(cw_ref[k:k + 1, :] * taps[k] for k in range(4))
        r, ig = _lru_gates(xc, wa_ref, ba_ref[...], wx_ref, bx_ref[...])
        a, m = _lru_decay(r, _softplus_neg(lam_ref[...]), (rows == 0) & (i == 0))
        u = m * ig * xc + jnp.where(rows == 0, a * chh[hl - 1:hl, :], 0.0)
        h = _scan_fwd(a, u)
        z = gcp * v
        ztaps = _conv_taps(jnp.concatenate([czz[...], z], axis=0), hl, ts, 3)
        yb = gbp * sum(sw_ref[k:k + 1, :] * ztaps[k] for k in range(3))
        y = jnp.concatenate([h * (ga * _sigmoid(ga)), yb * (gb * _sigmoid(gb))], axis=1).astype(BF16)
        y_ref[...] = y
        x1_ref[...] = x_ref[...] + gate_ref[...] * jnp.dot(y, wo_ref[...], preferred_element_type=F32)
        h_ref[...] = h.astype(BF16)
        cxa[...] = xa[ts - hl:, :]
        czz[...] = z[ts - hl:, :]
        chh[...] = h[ts - hl:, :]

    def full(a):
        return pl.BlockSpec(a.shape, lambda i: (0,) * a.ndim)

    row = lambda w: pl.BlockSpec((ts, w), lambda i: (i, 0))
    return pl.pallas_call(
        body, name="l0_mix", grid=(s_len // ts,),
        in_specs=[row(6 * D), row(D)] + [full(a) for a in (gate, cw, cb, wa, ba, wx, bx, lam, sw, wo)],
        out_specs=[row(D), row(D), row(2 * D)],
        out_shape=[jax.ShapeDtypeStruct((s_len, D), F32), jax.ShapeDtypeStruct((s_len, D), BF16),
                   jax.ShapeDtypeStruct((s_len, 2 * D), BF16)],
        scratch_shapes=[pltpu.VMEM((hl, D), F32)] * 3,
        compiler_params=_cp(("arbitrary",)),
    )(proj, x, gate, cw, cb, wa, ba, wx, bx, lam, sw, wo)


def _l1_mix(proj, x1, tgt, gate, wg, bg, scale, wo, gf):
    s_len = x1.shape[0]
    ts = _tile(s_len, TS_MIX)
    pw, gd, hl = 2 * D, POOL_GROUP_DIM, POOL_HALO

    def body(p_ref, x_ref, t_ref, gate_ref, wg_ref, bg_ref, sc_ref, wo_ref, gf_ref,
             d_ref, mx_ref, y_ref, dx_ref, loss_ref, dgf_ref, cv):
        i = pl.program_id(0)

        @pl.when(i == 0)
        def _():
            cv[...] = jnp.zeros_like(cv)
            loss_ref[...] = jnp.zeros_like(loss_ref)
            dgf_ref[...] = jnp.zeros_like(dgf_ref)

        v = p_ref[:, 0:pw].astype(F32)
        gg = p_ref[:, pw:2 * pw].astype(F32)
        sums = _window_sums(jnp.concatenate([cv[...], v], axis=0), _down)
        inv = _pool_inv_counts(i * ts, ts)
        dd = [sums[k][hl:hl + ts] * inv[k] - v[:, k * gd:(k + 1) * gd] for k in range(4)]
        mixed = jnp.concatenate(
            [jnp.dot(dd[k].astype(BF16), wg_ref[k], preferred_element_type=F32) for k in range(4)], axis=1) + bg_ref[...]
        d_ref[...] = jnp.concatenate(dd, axis=1).astype(BF16)
        mx_ref[...] = mixed.astype(BF16)
        y = (mixed * sc_ref[...] * (gg * _sigmoid(gg))).astype(BF16)
        y_ref[...] = y
        x2 = x_ref[...] + gate_ref[...] * jnp.dot(y, wo_ref[...], preferred_element_type=F32)
        r2 = lax.rsqrt(jnp.mean(x2 * x2, axis=-1, keepdims=True) + RMS_EPS)
        n2 = x2 * r2
        err = n2 * gf_ref[...] - t_ref[...]
        loss_ref[...] += jnp.sum(err * err, axis=0, keepdims=True)
        dyf = err * (1.0 / D)
        dgf_ref[...] += jnp.sum(dyf * n2, axis=0, keepdims=True)
        dn = dyf * gf_ref[...]
        dx_ref[...] = r2 * (dn - n2 * jnp.mean(dn * n2, axis=-1, keepdims=True))
        cv[...] = v[ts - hl:, :]

    def full(a):
        return pl.BlockSpec(a.shape, lambda i: (0,) * a.ndim)

    row = lambda w: pl.BlockSpec((ts, w), lambda i: (i, 0))
    acc = pl.BlockSpec((1, D), lambda i: (0, 0))
    return pl.pallas_call(
        body, name="l1_mix", grid=(s_len // ts,),
        in_specs=[row(2 * pw), row(D), row(D)] + [full(a) for a in (gate, wg, bg, scale, wo, gf)],
        out_specs=[row(pw), row(pw), row(pw), row(D), acc, acc],
        out_shape=[jax.ShapeDtypeStruct((s_len, pw), BF16)] * 3 + [jax.ShapeDtypeStruct((s_len, D), F32)]
        + [jax.ShapeDtypeStruct((1, D), F32)] * 2,
        scratch_shapes=[pltpu.VMEM((hl, pw), F32)],
        compiler_params=_cp(("arbitrary",)),
    )(proj, x1, tgt, gate, wg, bg, scale, wo, gf)


def _l1_bwd_mix(dx2, proj, mixed, gate, wg, scale, wo):
    s_len = dx2.shape[0]
    ts = _tile(s_len, TS_MIX)
    n_t = s_len // ts
    pw, gd, hl = 2 * D, POOL_GROUP_DIM, POOL_HALO

    def body(dx_ref, gg_ref, mx_ref, gate_ref, wg_ref, sc_ref, wo_ref, dp_ref, dmx_ref, dsc_ref, dbg_ref, cq):
        i = pl.program_id(0)

        @pl.when(i == 0)
        def _():
            cq[...] = jnp.zeros_like(cq)
            dsc_ref[...] = jnp.zeros_like(dsc_ref)
            dbg_ref[...] = jnp.zeros_like(dbg_ref)

        dy = lax.dot_general((gate_ref[...] * dx_ref[...]).astype(BF16), wo_ref[...], NT, preferred_element_type=F32)
        gg = gg_ref[...].astype(F32)
        mixed = mx_ref[...].astype(F32)
        s = _sigmoid(gg)
        sg = gg * s
        dmixed = dy * sc_ref[...] * sg
        dsc_ref[...] += jnp.sum(dy * mixed * sg, axis=0, keepdims=True)
        dbg_ref[...] += jnp.sum(dmixed, axis=0, keepdims=True)
        dmb = dmixed.astype(BF16)
        dmx_ref[...] = dmb
        dp_ref[:, pw:2 * pw] = (dy * sc_ref[...] * mixed * (s * (1.0 + gg * (1.0 - s)))).astype(BF16)
        inv = _pool_inv_counts((n_t - 1 - i) * ts, ts)
        dd = [lax.dot_general(dmb[:, k * gd:(k + 1) * gd], wg_ref[k], NT, preferred_element_type=F32) for k in range(4)]
        q = jnp.concatenate([dd[k] * inv[k] for k in range(4)], axis=1)
        sums = _window_sums(jnp.concatenate([q, cq[...]], axis=0), _up)
        dp_ref[:, 0:pw] = jnp.concatenate([sums[k][0:ts] - dd[k] for k in range(4)], axis=1).astype(BF16)
        cq[...] = q[0:hl, :]

    def full(a):
        return pl.BlockSpec(a.shape, lambda i: (0,) * a.ndim)

    rev = lambda w, j=0: pl.BlockSpec((ts, w), lambda i: (n_t - 1 - i, j))
    acc = pl.BlockSpec((1, pw), lambda i: (0, 0))
    return pl.pallas_call(
        body, name="l1_bwd_mix", grid=(n_t,),
        in_specs=[rev(D), rev(pw, 1), rev(pw)] + [full(a) for a in (gate, wg, scale, wo)],
        out_specs=[rev(2 * pw), rev(pw), acc, acc],
        out_shape=[jax.ShapeDtypeStruct((s_len, 2 * pw), BF16), jax.ShapeDtypeStruct((s_len, pw), BF16),
                   jax.ShapeDtypeStruct((1, pw), F32), jax.ShapeDtypeStruct((1, pw), F32)],
        scratch_shapes=[pltpu.VMEM((hl, pw), F32)],
        compiler_params=_cp(("arbitrary",)),
    )(dx2, proj, mixed, gate, wg, scale, wo)


def _l0_bwd_mix(dx1, proj, hst, gate, cw, cb, wa, ba, wx, bx, lam, sw, wo):
    s_len = dx1.shape[0]
    ts = _tile(s_len, TS_MIX)
    n_t = s_len // ts
    hl, hb = SUBLANES, BF16_ROWS

    def body(dx_ref, p_ref, ph_ref, h_ref, hh_ref, gate_ref, cw_ref, cb_ref, wa_ref, ba_ref, wx_ref, bx_ref,
             lam_ref, sw_ref, wo_ref, dp_ref, xc_ref, dpa_ref, dpx_ref, sm_ref, cg, cdxc, cdcz):
        i = pl.program_id(0)
        ri = n_t - 1 - i

        @pl.when(i == 0)
        def _():
            cg[...] = jnp.zeros_like(cg)
            cdxc[...] = jnp.zeros_like(cdxc)
            cdcz[...] = jnp.zeros_like(cdcz)
            sm_ref[...] = jnp.zeros_like(sm_ref)

        has_prev = (ri > 0).astype(F32)
        xa, ga, gbp, gcp, v, gb = [p_ref[:, k * D:(k + 1) * D].astype(F32) for k in range(6)]
        prev = lambda k: ph_ref[:, k * D:(k + 1) * D].astype(F32)[hb - hl:hb] * has_prev
        rows = _rows(ts, D)
        first = (rows == 0) & (ri == 0)
        xtaps = _conv_taps(jnp.concatenate([prev(0), xa], axis=0), hl, ts, 4)
        xc = cb_ref[...] + sum(cw_ref[k:k + 1, :] * xtaps[k] for k in range(4))
        r, ig = _lru_gates(xc, wa_ref, ba_ref[...], wx_ref, bx_ref[...])
        sp = _softplus_neg(lam_ref[...])
        a, m = _lru_decay(r, sp, first)
        z = gcp * v
        ztaps = _conv_taps(jnp.concatenate([prev(3) * prev(4), z], axis=0), hl, ts, 3)
        cz = sum(sw_ref[k:k + 1, :] * ztaps[k] for k in range(3))
        h = h_ref[...].astype(F32)
        hprev = _down(jnp.concatenate([hh_ref[...].astype(F32)[hb - hl:hb] * has_prev, h], axis=0), 1)[hl:hl + ts]
        dy = lax.dot_general((gate_ref[...] * dx_ref[...]).astype(BF16), wo_ref[...], NT, preferred_element_type=F32)
        dya_pre, dyb_pre = dy[:, 0:D], dy[:, D:2 * D]
        s_a, s_b = _sigmoid(ga), _sigmoid(gb)
        dp_ref[:, D:2 * D] = (dya_pre * h * (s_a * (1.0 + ga * (1.0 - s_a)))).astype(BF16)
        dp_ref[:, 5 * D:6 * D] = (dyb_pre * (gbp * cz) * (s_b * (1.0 + gb * (1.0 - s_b)))).astype(BF16)
        dya = dya_pre * (ga * s_a)
        dyb = dyb_pre * (gb * s_b)
        dp_ref[:, 2 * D:3 * D] = (dyb * cz).astype(BF16)
        dcz = dyb * gbp
        for k in range(3):
            sm_ref[8 + k:9 + k, :] += jnp.sum(dcz * ztaps[k], axis=0, keepdims=True)
        dcz_ext = jnp.concatenate([dcz, cdcz[...]], axis=0)
        dz = sum(sw_ref[k:k + 1, :] * _up(dcz_ext, 2 - k)[0:ts] for k in range(3))
        dp_ref[:, 3 * D:4 * D] = (dz * v).astype(BF16)
        dp_ref[:, 4 * D:5 * D] = (dz * gcp).astype(BF16)
        cdcz[...] = dcz[0:hl, :]
        alpha = jnp.where(rows < ts - 1, _up(a, 1), 0.0)
        dh = _scan_rev(alpha, dya + jnp.where(rows == ts - 1, cg[0:1, :], 0.0))
        cg[...] = a[0:hl, :] * dh[0:hl, :]
        da = dh * hprev
        dm = dh * ig * xc
        di = dh * m * xc
        dxc = dh * m * ig
        dl = da * a - jnp.where(first, 0.0, dm * (a * a) / m)
        sm_ref[7:8, :] += jnp.sum(dl * r, axis=0, keepdims=True) * (-LRU_C)
        dpa = (dl * sp) * (-LRU_C) * r * (1.0 - r)
        dpx = di * ig * (1.0 - ig)
        sm_ref[5:6, :] += jnp.sum(dpa, axis=0, keepdims=True)
        sm_ref[6:7, :] += jnp.sum(dpx, axis=0, keepdims=True)
        dpa_b, dpx_b = dpa.astype(BF16), dpx.astype(BF16)
        dpa_ref[...] = dpa_b
        dpx_ref[...] = dpx_b
        xc_ref[...] = xc.astype(BF16)
        back = []
        for hd in range(LRU_HEADS):
            sl = slice(hd * LRU_HEAD_DIM, (hd + 1) * LRU_HEAD_DIM)
            back.append(lax.dot_general(dpa_b[:, sl], wa_ref[hd], NT, preferred_element_type=F32)
                        + lax.dot_general(dpx_b[:, sl], wx_ref[hd], NT, preferred_element_type=F32))
        dxc = dxc + jnp.concatenate(back, axis=1)
        sm_ref[4:5, :] += jnp.sum(dxc, axis=0, keepdims=True)
        for k in range(4):
            sm_ref[k:k + 1, :] += jnp.sum(dxc * xtaps[k], axis=0, keepdims=True)
        dxc_ext = jnp.concatenate([dxc, cdxc[...]], axis=0)
        dp_ref[:, 0:D] = sum(cw_ref[k:k + 1, :] * _up(dxc_ext, 3 - k)[0:ts] for k in range(4)).astype(BF16)
        cdxc[...] = dxc[0:hl, :]

    def full(a):
        return pl.BlockSpec(a.shape, lambda i: (0,) * a.ndim)

    rev = lambda w: pl.BlockSpec((ts, w), lambda i: (n_t - 1 - i, 0))
    halo = lambda w: pl.BlockSpec((hb, w), lambda i: (jnp.maximum((n_t - 1 - i) * (ts // hb) - 1, 0), 0))
    return pl.pallas_call(
        body, name="l0_bwd_mix", grid=(n_t,),
        in_specs=[rev(D), rev(6 * D), halo(6 * D), rev(D), halo(D)]
        + [full(a) for a in (gate, cw, cb, wa, ba, wx, bx, lam, sw, wo)],
        out_specs=[rev(6 * D), rev(D), rev(D), rev(D), pl.BlockSpec((2 * SUBLANES, D), lambda i: (0, 0))],
        out_shape=[jax.ShapeDtypeStruct((s_len, 6 * D), BF16)] + [jax.ShapeDtypeStruct((s_len, D), BF16)] * 3
        + [jax.ShapeDtypeStruct((2 * SUBLANES, D), F32)],
        scratch_shapes=[pltpu.VMEM((hl, D), F32)] * 3,
        compiler_params=_cp(("arbitrary",)),
    )(dx1, proj, proj, hst, hst, gate, cw, cb, wa, ba, wx, bx, lam, sw, wo)


def _dgrad_norm(dproj, w, x, dres, g, sc, name):
    s_len, nb = x.shape[0], w.shape[2]
    ts = _tile(s_len, TS_DGRAD)

    def body(dp_ref, w_ref, x_ref, dr_ref, g_ref, sc_ref, dx_ref, s1_ref, s2_ref):
        @pl.when(pl.program_id(0) == 0)
        def _():
            s1_ref[...] = jnp.zeros_like(s1_ref)
            s2_ref[...] = jnp.zeros_like(s2_ref)

        dh = sum(lax.dot_general(dp_ref[:, k * nb:(k + 1) * nb], w_ref[k], NT, preferred_element_type=F32)
                 for k in range(N_CHIP))
        xv = x_ref[...]
        r = lax.rsqrt(jnp.mean(xv * xv, axis=-1, keepdims=True) + RMS_EPS)
        n = xv * r
        s1_ref[...] += jnp.sum(dh, axis=0, keepdims=True)
        s2_ref[...] += jnp.sum(dh * n, axis=0, keepdims=True)
        dn = dh * (g_ref[...] * (1.0 + sc_ref[...]))
        dx_ref[...] = dr_ref[...] + r * (dn - n * jnp.mean(dn * n, axis=-1, keepdims=True))

    row = lambda wd: pl.BlockSpec((ts, wd), lambda i: (i, 0))
    vec = pl.BlockSpec((1, D), lambda i: (0, 0))
    return pl.pallas_call(
        body, name=name, grid=(s_len // ts,),
        in_specs=[row(N_CHIP * nb), pl.BlockSpec(w.shape, lambda i: (0, 0, 0)), row(D), row(D), vec, vec],
        out_specs=[row(D), vec, vec],
        out_shape=[jax.ShapeDtypeStruct((s_len, D), F32)] + [jax.ShapeDtypeStruct((1, D), F32)] * 2,
        compiler_params=_cp(("arbitrary",)),
    )(dproj, w, x, dres, g, sc)


def _wgrad(a, b, groups, ka, nb, a_col, b_col, name):
    s_len = a.shape[0]
    ts = _tile(s_len, TS_WGRAD)

    def body(a_ref, b_ref, o_ref):
        @pl.when(pl.program_id(1) == 0)
        def _():
            o_ref[...] = jnp.zeros_like(o_ref)

        o_ref[...] += lax.dot_general(a_ref[...].astype(BF16), b_ref[...].astype(BF16), TN, preferred_element_type=F32)

    return pl.pallas_call(
        body, name=name, grid=(groups, s_len // ts),
        in_specs=[pl.BlockSpec((ts, ka), lambda g, s: (s, a_col(g))), pl.BlockSpec((ts, nb), lambda g, s: (s, b_col(g)))],
        out_specs=pl.BlockSpec((None, ka, nb), lambda g, s: (g, 0, 0)),
        out_shape=jax.ShapeDtypeStruct((groups, ka, nb), F32),
        compiler_params=_cp(("parallel", "arbitrary")),
    )(a, b)


def _wo_final(mt, wo, gate, name):
    rb = mt.shape[1]

    def body(m_ref, w_ref, gate_ref, dw_ref, dg_ref):
        @pl.when(pl.program_id(0) == 0)
        def _():
            dg_ref[...] = jnp.zeros_like(dg_ref)

        mv = m_ref[...]
        dw_ref[...] = mv * gate_ref[...]
        dg_ref[...] += jnp.sum(mv * w_ref[...].astype(F32), axis=0, keepdims=True)

    blk = pl.BlockSpec((None, rb, D), lambda k: (k, 0, 0))
    vec = pl.BlockSpec((1, D), lambda k: (0, 0))
    return pl.pallas_call(
        body, name=name, grid=(N_CHIP,), in_specs=[blk, blk, vec], out_specs=[blk, vec],
        out_shape=[jax.ShapeDtypeStruct(mt.shape, F32), jax.ShapeDtypeStruct((1, D), F32)],
        compiler_params=_cp(("arbitrary",)),
    )(mt, wo, gate)


ROW_NORM_G, ROW_CONV_W, ROW_CONV_B, ROW_B_A, ROW_B_X, ROW_LAMBDA, ROW_SC_W, ROW_POOL_B, ROW_POOL_S, ROW_FINAL_G = (
    0, 2, 6, 7, 8, 9, 10, 13, 15, 17)


def _small_pack(s1_0, s2_0, s1_1, s2_1, sm0, dsc1, dbg1, dgf, losscols, dgate0, dgate1, norm_g, sc0, sc1, lam):
    def body(s1_0r, s2_0r, s1_1r, s2_1r, sm, dsc, dbg, dgfr, lcols, dg0, dg1, ng, sc0r, sc1r, lamr, buf, dmod, loss):
        buf[...] = jnp.zeros_like(buf)
        buf[0:1, :] = s2_0r[...] * (1.0 + sc0r[...])
        buf[1:2, :] = s2_1r[...] * (1.0 + sc1r[...])
        buf[ROW_CONV_W:ROW_CONV_W + 4, :] = sm[0:4, :]
        buf[ROW_CONV_B:ROW_CONV_B + 1, :] = sm[4:5, :]
        buf[ROW_B_A:ROW_B_A + 1, :] = sm[5:6, :]
        buf[ROW_B_X:ROW_B_X + 1, :] = sm[6:7, :]
        buf[ROW_LAMBDA:ROW_LAMBDA + 1, :] = -sm[7:8, :] * _sigmoid(-lamr[...])
        buf[ROW_SC_W:ROW_SC_W + 3, :] = sm[8:11, :]
        for k in range(2):
            buf[ROW_POOL_B + k:ROW_POOL_B + k + 1, :] = dbg[:, k * D:(k + 1) * D]
            buf[ROW_POOL_S + k:ROW_POOL_S + k + 1, :] = dsc[:, k * D:(k + 1) * D]
        buf[ROW_FINAL_G:ROW_FINAL_G + 1, :] = dgfr[...]
        pieces = (s1_0r[...], s2_0r[...] * ng[0:1, :], dg0[...], s1_1r[...], s2_1r[...] * ng[1:2, :], dg1[...])
        for k, pc in enumerate(pieces):
            dmod[:, k * D:(k + 1) * D] = jnp.broadcast_to(pc, (SUBLANES, D))
        loss[...] = jnp.broadcast_to(jnp.sum(lcols[...], axis=1, keepdims=True) * (0.5 / D), loss.shape)

    args = (s1_0, s2_0, s1_1, s2_1, sm0, dsc1, dbg1, dgf, losscols, dgate0, dgate1, norm_g, sc0, sc1, lam)
    return pl.pallas_call(
        body, name="small_pack", in_specs=[VMEM] * len(args), out_specs=[VMEM] * 3,
        out_shape=[jax.ShapeDtypeStruct((SMALL_ROWS, D), F32), jax.ShapeDtypeStruct((SUBLANES, 6 * D), F32),
                   jax.ShapeDtypeStruct((SUBLANES, 128), F32)],
        compiler_params=_cp(),
    )(*args)


def _small_comm(buf_a, buf_b, dmod8):
    ra, rb = buf_a.shape[0] // N_DEV, buf_b.shape[0] // N_DEV
    wb = buf_b.shape[1]

    def body(a_ref, b_ref, dm_ref, oa_ref, ob_ref, odm_ref, ina, inb, dslot, sa, sb, s1, r1, s2, r2):
        x, y, c = _pos()
        me = 4 * x + 2 * y + c
        peers = []
        for r in range(1, N_DEV):
            fx, fy, fc = (r >> 2) & 1, (r >> 1) & 1, r & 1
            px, py, pc = _flip(x, fx), _flip(y, fy), _flip(c, fc)
            peers.append(((px, py, pc), 4 * px + 2 * py + pc))
        seg_a = lambda d: pl.ds(pl.multiple_of(d * ra, SUBLANES), ra)
        seg_b = lambda d: pl.ds(pl.multiple_of(d * rb, SUBLANES), rb)
        first = []
        for r, (peer, pid) in enumerate(peers):
            for k, (src, dst) in enumerate(((a_ref.at[seg_a(pid), :], ina.at[r]), (b_ref.at[seg_b(pid), :], inb.at[r]),
                                            (dm_ref, dslot.at[me]))):
                cp = pltpu.make_async_remote_copy(src_ref=src, dst_ref=dst, send_sem=s1.at[3 * r + k],
                                                  recv_sem=r1.at[3 * r + k], device_id=peer, device_id_type=MESH)
                cp.start()
                first.append(cp)
        dslot[me] = dm_ref[...]
        for cp in first:
            cp.wait()
        acc_a, acc_b = a_ref[seg_a(me), :], b_ref[seg_b(me), :]
        for r in range(N_DEV - 1):
            acc_a = acc_a + ina[r]
            acc_b = acc_b + inb[r]
        sa[...] = acc_a
        sb[...] = acc_b
        oa_ref[seg_a(me), :] = acc_a
        ob_ref[seg_b(me), :] = acc_b
        second = []
        for r, (peer, pid) in enumerate(peers):
            for k, (src, dst) in enumerate(((sa, oa_ref.at[seg_a(me), :]), (sb, ob_ref.at[seg_b(me), :]))):
                cp = pltpu.make_async_remote_copy(src_ref=src, dst_ref=dst, send_sem=s2.at[2 * r + k],
                                                  recv_sem=r2.at[2 * r + k], device_id=peer, device_id_type=MESH)
                cp.start()
                second.append(cp)
        rows = _rows(SUBLANES, dm_ref.shape[1])
        dm_all = jnp.zeros(dm_ref.shape, F32)
        for d in range(N_DEV):
            dm_all = jnp.where(rows == d, dslot[d], dm_all)
        odm_ref[...] = dm_all
        for cp in second:
            cp.wait()

    nrel = N_DEV - 1
    return pl.pallas_call(
        body, name="small_comm", in_specs=[VMEM] * 3, out_specs=[VMEM] * 3,
        out_shape=[jax.ShapeDtypeStruct(buf_a.shape, F32), jax.ShapeDtypeStruct(buf_b.shape, F32),
                   jax.ShapeDtypeStruct(dmod8.shape, F32)],
        scratch_shapes=[pltpu.VMEM((nrel, ra, D), F32), pltpu.VMEM((nrel, rb, wb), F32),
                        pltpu.VMEM((N_DEV,) + dmod8.shape, F32), pltpu.VMEM((ra, D), F32), pltpu.VMEM((rb, wb), F32),
                        pltpu.SemaphoreType.DMA((3 * nrel,)), pltpu.SemaphoreType.DMA((3 * nrel,)),
                        pltpu.SemaphoreType.DMA((2 * nrel,)), pltpu.SemaphoreType.DMA((2 * nrel,))],
        compiler_params=_cp(),
    )(buf_a, buf_b, dmod8)


def _adam(w, g, m, v):
    m2 = ADAM_B1 * m + (1.0 - ADAM_B1) * g
    v2 = ADAM_B2 * v + (1.0 - ADAM_B2) * (g * g)
    m_hat = m2 / (1.0 - ADAM_B1 ** ADAM_STEP)
    v_hat = v2 / (1.0 - ADAM_B2 ** ADAM_STEP)
    return -ADAM_LR * (m_hat / (jnp.sqrt(v_hat) + ADAM_EPS) + ADAM_WD * w), m2, v2


def _small_adam(red_a, red_b, dm_all, params):
    n = len(params)

    def body(*refs):
        ra, rb, dm = refs[:3]
        wmv = refs[3:3 + 3 * n]
        outs = refs[3 + 3 * n:]
        x, y, _ = _pos()
        chip = 2 * x + y

        def shard(row0, nrows, width):
            per_row = D // width
            cands = []
            for k in range(N_CHIP):
                if nrows == 1 or per_row >= N_CHIP:
                    cands.append(ra[row0:row0 + nrows, k * width:(k + 1) * width])
                else:
                    rr, cc = divmod(k * width, D)
                    cands.append(ra[row0 + rr:row0 + rr + 1, cc:cc + width])
            g = cands[0]
            for k in range(1, N_CHIP):
                g = jnp.where(chip == k, cands[k], g)
            return g

        dms = jnp.sum(dm[...], axis=0, keepdims=True)
        hw = LRU_HEADS * LRU_HEAD_DIM
        grads = [
            ra[ROW_NORM_G:ROW_NORM_G + 2, :],
            None,
            shard(ROW_CONV_W, 4, D // N_CHIP),
            ra[ROW_CONV_B:ROW_CONV_B + 1, :],
            rb[0:hw, :],
            ra[ROW_B_A:ROW_B_A + 1, :],
            rb[hw:2 * hw, :],
            ra[ROW_B_X:ROW_B_X + 1, :],
            ra[ROW_LAMBDA:ROW_LAMBDA + 1, :],
            shard(ROW_SC_W, 3, D // N_CHIP),
            shard(ROW_POOL_B, 2, 2 * D // N_CHIP),
            shard(ROW_POOL_S, 2, 2 * D // N_CHIP),
            ra[ROW_FINAL_G:ROW_FINAL_G + 1, :],
        ]
        for p in range(n):
            w_ref, m_ref, v_ref = wmv[3 * p:3 * p + 3]
            g_out, d_out, m_out, v_out = outs[4 * p:4 * p + 4]
            if grads[p] is None:
                for l in range(2):
                    g = dms[:, l * 3 * D:(l + 1) * 3 * D]
                    dl, m2, v2 = _adam(w_ref[l:l + 1, :], g, m_ref[l:l + 1, :], v_ref[l:l + 1, :])
                    g_out[l:l + 1, :] = g
                    d_out[l:l + 1, :] = dl
                    m_out[l:l + 1, :] = m2
                    v_out[l:l + 1, :] = v2
            else:
                g = grads[p]
                dl, m2, v2 = _adam(w_ref[...], g, m_ref[...], v_ref[...])
                g_out[...] = g
                d_out[...] = dl
                m_out[...] = m2
                v_out[...] = v2

    flat = [a for p in params for a in p]
    return pl.pallas_call(
        body, name="small_adam", in_specs=[VMEM] * (3 + len(flat)), out_specs=[VMEM] * (4 * n),
        out_shape=[jax.ShapeDtypeStruct(p[0].shape, F32) for p in params for _ in range(4)],
        compiler_params=_cp(),
    )(red_a, red_b, dm_all, *flat)


def _modw_adam(ca_t, dm_sh, w, m, v):
    nw = w.shape[2]

    def body(c_ref, d_ref, w_ref, m_ref, v_ref, g_out, d_out, m_out, v_out):
        g = jnp.dot(c_ref[...], d_ref[...], precision=lax.Precision.HIGHEST, preferred_element_type=F32)
        dl, m2, v2 = _adam(w_ref[...], g, m_ref[...], v_ref[...])
        g_out[...] = g
        d_out[...] = dl
        m_out[...] = m2
        v_out[...] = v2

    blk = pl.BlockSpec((None, D, nw), lambda l: (l, 0, 0))
    return pl.pallas_call(
        body, name="modw_adam", grid=(2,),
        in_specs=[pl.BlockSpec((D, SUBLANES), lambda l: (0, 0)), pl.BlockSpec((None, SUBLANES, nw), lambda l: (l, 0, 0)),
                  blk, blk, blk],
        out_specs=[blk] * 4, out_shape=[jax.ShapeDtypeStruct(w.shape, F32)] * 4,
        compiler_params=_cp(("arbitrary",)),
    )(ca_t, dm_sh, w, m, v)


def _half_rows(r):
    return r // 2


def _sib_send_halves(gs):
    n = len(gs)

    def body(*refs):
        ins, outs, ssem, rsem = refs[:n], refs[n:2 * n], refs[2 * n], refs[2 * n + 1]
        x, y, c = _pos()
        cps = []
        for a in range(n):
            hr = _half_rows(gs[a].shape[1])
            cp = pltpu.make_async_remote_copy(
                src_ref=ins[a].at[:, pl.ds(pl.multiple_of((1 - c) * hr, SUBLANES), hr), :], dst_ref=outs[a],
                send_sem=ssem.at[a], recv_sem=rsem.at[a], device_id=(x, y, 1 - c), device_id_type=MESH)
            cp.start()
            cps.append(cp)
        for cp in cps:
            cp.wait()

    return pl.pallas_call(
        body, name="grad_sib_halves", in_specs=[ANY] * n, out_specs=[ANY] * n,
        out_shape=[jax.ShapeDtypeStruct((N_CHIP, _half_rows(g.shape[1]), g.shape[2]), F32) for g in gs],
        scratch_shapes=[pltpu.SemaphoreType.DMA((n,)), pltpu.SemaphoreType.DMA((n,))],
        compiler_params=_cp(),
    )(*gs)


def _add_half(g, got, cidx, name):
    _, hr, cc = got.shape
    rb = min(hr, 256)

    def body(c_ref, g_ref, r_ref, o_ref):
        o_ref[...] = g_ref[...] + r_ref[...]

    blk = pl.BlockSpec((None, rb, cc), lambda k, j, c_ref: (k, j, 0))
    return pl.pallas_call(
        body, name=name,
        grid_spec=pltpu.PrefetchScalarGridSpec(
            num_scalar_prefetch=1, grid=(N_CHIP, hr // rb),
            in_specs=[pl.BlockSpec((None, rb, cc), lambda k, j, c_ref: (k, c_ref[0] * (hr // rb) + j, 0)), blk],
            out_specs=blk),
        out_shape=jax.ShapeDtypeStruct(got.shape, F32),
        compiler_params=_cp(("parallel", "parallel")),
    )(cidx, g, got)


def _chip_scatter(ps):
    n = len(ps)

    def body(*refs):
        ins, outs, ssem, rsem = refs[:n], refs[n:2 * n], refs[2 * n], refs[2 * n + 1]
        x, y, c = _pos()
        cps = []
        for a in range(n):
            for q, (fx, fy) in enumerate(((1, 0), (0, 1), (1, 1))):
                px, py = _flip(x, fx), _flip(y, fy)
                cp = pltpu.make_async_remote_copy(
                    src_ref=ins[a].at[2 * px + py], dst_ref=outs[a].at[q],
                    send_sem=ssem.at[3 * a + q], recv_sem=rsem.at[3 * a + q], device_id=(px, py, c), device_id_type=MESH)
                cp.start()
                cps.append(cp)
        for cp in cps:
            cp.wait()

    return pl.pallas_call(
        body, name="grad_chip_scatter", in_specs=[ANY] * n, out_specs=[ANY] * n,
        out_shape=[jax.ShapeDtypeStruct((N_CHIP - 1,) + p.shape[1:], F32) for p in ps],
        scratch_shapes=[pltpu.SemaphoreType.DMA((3 * n,)), pltpu.SemaphoreType.DMA((3 * n,))],
        compiler_params=_cp(),
    )(*ps)


def _add_owner(p, got, chipidx, name):
    _, hr, cc = p.shape
    rb = min(hr, 256)

    def body(k_ref, p_ref, r_ref, o_ref):
        o_ref[...] = ((p_ref[...] + r_ref[0]) + r_ref[1]) + r_ref[2]

    return pl.pallas_call(
        body, name=name,
        grid_spec=pltpu.PrefetchScalarGridSpec(
            num_scalar_prefetch=1, grid=(hr // rb,),
            in_specs=[pl.BlockSpec((None, rb, cc), lambda j, k_ref: (k_ref[0], j, 0)),
                      pl.BlockSpec((N_CHIP - 1, rb, cc), lambda j, k_ref: (0, j, 0))],
            out_specs=pl.BlockSpec((rb, cc), lambda j, k_ref: (j, 0))),
        out_shape=jax.ShapeDtypeStruct((hr, cc), F32),
        compiler_params=_cp(("parallel",)),
    )(chipidx, p, got)


def _sib_exchange(ts_):
    n = len(ts_)

    def body(*refs):
        ins, outs, lsem, ssem, rsem = refs[:n], refs[n:2 * n], refs[2 * n], refs[2 * n + 1], refs[2 * n + 2]
        x, y, c = _pos()
        local, cps = [], []
        for a in range(n):
            lc = pltpu.make_async_copy(ins[a], outs[a].at[c], lsem.at[a])
            lc.start()
            local.append(lc)
            cp = pltpu.make_async_remote_copy(src_ref=ins[a], dst_ref=outs[a].at[c], send_sem=ssem.at[a],
                                              recv_sem=rsem.at[a], device_id=(x, y, 1 - c), device_id_type=MESH)
            cp.start()
            cps.append(cp)
        for cp in cps:
            cp.wait()
        for lc in local:
            lc.wait()

    return pl.pallas_call(
        body, name="grad_sib_exchange", in_specs=[ANY] * n, out_specs=[ANY] * n,
        out_shape=[jax.ShapeDtypeStruct((2,) + t.shape, F32) for t in ts_],
        scratch_shapes=[pltpu.SemaphoreType.DMA((n,))] * 3,
        compiler_params=_cp(),
    )(*ts_)


def _adam_2d(w, g, m, v, name):
    rr, cc = w.shape
    rb = min(rr, 256)

    def body(w_ref, g_ref, m_ref, v_ref, g_out, d_out, m_out, v_out):
        g = g_ref[...]
        dl, m2, v2 = _adam(w_ref[...], g, m_ref[...], v_ref[...])
        g_out[...] = g
        d_out[...] = dl
        m_out[...] = m2
        v_out[...] = v2

    blk = pl.BlockSpec((rb, cc), lambda j: (j, 0))
    return pl.pallas_call(
        body, name=name, grid=(rr // rb,), in_specs=[blk] * 4, out_specs=[blk] * 4,
        out_shape=[jax.ShapeDtypeStruct((rr, cc), F32)] * 4, compiler_params=_cp(("parallel",)),
    )(w, g, m, v)


def kernel(x, c, norm_g, mod_w, mod_b, hy_w_in, hy_conv_w, hy_conv_b, lru_w_a, lru_b_a, lru_w_x, lru_b_x, lru_lambda, sc_conv_w, hy_w_out, pool_w_in, pool_w_grp, pool_b_grp, pool_scale, pool_w_out, final_g, loss_target, m_norm_g, m_mod_w, m_mod_b, m_hy_w_in, m_hy_conv_w, m_hy_conv_b, m_lru_w_a, m_lru_b_a, m_lru_w_x, m_lru_b_x, m_lru_lambda, m_sc_conv_w, m_hy_w_out, m_pool_w_in, m_pool_w_grp, m_pool_b_grp, m_pool_scale, m_pool_w_out, m_final_g, v_norm_g, v_mod_w, v_mod_b, v_hy_w_in, v_hy_conv_w, v_hy_conv_b, v_lru_w_a, v_lru_b_a, v_lru_w_x, v_lru_b_x, v_lru_lambda, v_sc_conv_w, v_hy_w_out, v_pool_w_in, v_pool_w_grp, v_pool_b_grp, v_pool_scale, v_pool_w_out, v_final_g):
    ax, ay, ac = _pos()
    me = 4 * ax + 2 * ay + ac
    chip = 2 * ax + ay
    xs = x[0]
    tgt = loss_target[0]
    gd = POOL_GROUP_DIM

    ca_all, mod_all, small_w = _mod_fwd(jnp.broadcast_to(c, (SUBLANES, D)), mod_w, mod_b,
                                        hy_conv_w[0], sc_conv_w[0], pool_b_grp, pool_scale)
    mod_me = lax.dynamic_index_in_dim(mod_all, me, axis=1, keepdims=False)
    sh0, sc0, gt0 = (mod_me[0:1, k * D:(k + 1) * D] for k in range(3))
    sh1, sc1, gt1 = (mod_me[1:2, k * D:(k + 1) * D] for k in range(3))
    cw = small_w[SW_CONV:SW_CONV + 4, 0:D]
    sw = small_w[SW_SC:SW_SC + 3, 0:D]
    pool_b = small_w[SW_POOL_B:SW_POOL_B + 1, :]
    pool_s = small_w[SW_POOL_S:SW_POOL_S + 1, :]
    g0, g1, gf = norm_g[0:1], norm_g[1:2], final_g.reshape(1, D)
    cb, ba, bx, lam = hy_conv_b, lru_b_a, lru_b_x, lru_lambda

    big = [hy_w_in[0], hy_w_out[0], pool_w_in[0], pool_w_grp[0].reshape(4 * 128, gd), pool_w_out[0]]
    w_in0, w_out0, w_in1, w_grp, w_out1 = _wgather(_wcast(big))
    w_grp = w_grp.reshape(N_CHIP, 4, 128, gd).transpose(1, 0, 2, 3).reshape(4, gd, gd)
    wa_b, wx_b = _wcast([lru_w_a[0], lru_w_x[0]])

    h0, proj0 = _norm_proj(xs, g0, sc0, sh0, w_in0, "l0_proj")
    x1, hst, y0 = _l0_mix(proj0, xs, gt0, cw, cb, wa_b, ba, wx_b, bx, lam, sw, w_out0.reshape(2 * D, D))
    h1, proj1 = _norm_proj(x1, g1, sc1, sh1, w_in1, "l1_proj")
    dpool, mixed, y1, dx2, losscols, dgf = _l1_mix(proj1, x1, tgt, gt1, w_grp, pool_b, pool_s,
                                                    w_out1.reshape(2 * D, D), gf)

    dproj1, dmixed, dsc1, dbg1 = _l1_bwd_mix(dx2, proj1, mixed, gt1, w_grp, pool_s, w_out1.reshape(2 * D, D))
    mt1 = _wgrad(y1, dx2, N_CHIP, gd, D, lambda g: g, lambda g: 0, "l1_wgrad_out")
    d_wgrp = _wgrad(dpool, dmixed, 4, gd, gd, lambda g: g, lambda g: g, "l1_wgrad_grp")
    d_win1 = _wgrad(h1, dproj1, N_CHIP, D, D, lambda g: 0, lambda g: g, "l1_wgrad_in")
    dx1, s1_1, s2_1 = _dgrad_norm(dproj1, w_in1, x1, dx2, g1, sc1, "l1_bwd_proj")
    d_wout1, dgate1 = _wo_final(mt1, w_out1, gt1, "l1_wo_final")

    dproj0, xc, dpa, dpx, sm0 = _l0_bwd_mix(dx1, proj0, hst, gt0, cw, cb, wa_b, ba, wx_b, bx, lam, sw,
                                            w_out0.reshape(2 * D, D))
    mt0 = _wgrad(y0, dx1, N_CHIP, gd, D, lambda g: g, lambda g: 0, "l0_wgrad_out")
    d_win0 = _wgrad(h0, dproj0, N_CHIP, D, 6 * D // N_CHIP, lambda g: 0, lambda g: g, "l0_wgrad_in")
    d_wa = _wgrad(xc, dpa, LRU_HEADS, LRU_HEAD_DIM, LRU_HEAD_DIM, lambda g: g, lambda g: g, "l0_wgrad_a")
    d_wx = _wgrad(xc, dpx, LRU_HEADS, LRU_HEAD_DIM, LRU_HEAD_DIM, lambda g: g, lambda g: g, "l0_wgrad_x")
    grad_x, s1_0, s2_0 = _dgrad_norm(dproj0, w_in0, xs, dx1, g0, sc0, "l0_bwd_proj")
    d_wout0, dgate0 = _wo_final(mt0, w_out0, gt0, "l0_wo_final")

    buf_a, dmod8, loss8 = _small_pack(s1_0, s2_0, s1_1, s2_1, sm0, dsc1, dbg1, dgf, losscols, dgate0, dgate1,
                                      norm_g, sc0, sc1, lam)
    hw = LRU_HEADS * LRU_HEAD_DIM
    buf_b = jnp.concatenate([d_wa.reshape(hw, LRU_HEAD_DIM), d_wx.reshape(hw, LRU_HEAD_DIM)], axis=0)
    red_a, red_b, dm_all = _small_comm(buf_a, buf_b, dmod8)
    small = [(norm_g, m_norm_g, v_norm_g), (mod_b, m_mod_b, v_mod_b),
             (hy_conv_w[0], m_hy_conv_w[0], v_hy_conv_w[0]), (hy_conv_b, m_hy_conv_b, v_hy_conv_b),
             tuple(a.reshape(hw, LRU_HEAD_DIM) for a in (lru_w_a, m_lru_w_a, v_lru_w_a)),
             (lru_b_a, m_lru_b_a, v_lru_b_a),
             tuple(a.reshape(hw, LRU_HEAD_DIM) for a in (lru_w_x, m_lru_w_x, v_lru_w_x)),
             (lru_b_x, m_lru_b_x, v_lru_b_x), (lru_lambda, m_lru_lambda, v_lru_lambda),
             (sc_conv_w[0], m_sc_conv_w[0], v_sc_conv_w[0]), (pool_b_grp, m_pool_b_grp, v_pool_b_grp),
             (pool_scale, m_pool_scale, v_pool_scale),
             tuple(a.reshape(1, D) for a in (final_g, m_final_g, v_final_g))]
    small_names = ["norm_g", "mod_b", "hy_conv_w", "hy_conv_b", "lru_w_a", "lru_b_a", "lru_w_x", "lru_b_x",
                   "lru_lambda", "sc_conv_w", "pool_b_grp", "pool_scale", "final_g"]
    small_out = _small_adam(red_a, red_b, dm_all, small)
    res = {}
    shapes = dict(norm_g=norm_g, mod_b=mod_b, hy_conv_w=hy_conv_w, hy_conv_b=hy_conv_b, lru_w_a=lru_w_a, lru_b_a=lru_b_a,
                  lru_w_x=lru_w_x, lru_b_x=lru_b_x, lru_lambda=lru_lambda, sc_conv_w=sc_conv_w, pool_b_grp=pool_b_grp,
                  pool_scale=pool_scale, final_g=final_g)
    for p, nm in enumerate(small_names):
        res[nm] = tuple(o.reshape(shapes[nm].shape) for o in small_out[4 * p:4 * p + 4])

    nw = mod_w.shape[2]
    dm_sh = jnp.stack([lax.dynamic_slice_in_dim(dm_all[:, l * 3 * D:(l + 1) * 3 * D], chip * nw, nw, axis=1)
                       for l in range(2)])
    res["mod_w"] = tuple(_modw_adam(ca_all.T, dm_sh, mod_w, m_mod_w, v_mod_w))

    d_wgrp = d_wgrp.reshape(4, N_CHIP, 128, gd).transpose(1, 0, 2, 3).reshape(N_CHIP, 4 * 128, gd)
    grads = [d_win0, d_wout0, d_win1, d_wgrp, d_wout1]
    cidx = ac.reshape(1).astype(jnp.int32)
    kidx = chip.reshape(1).astype(jnp.int32)
    got = _sib_send_halves(grads)
    parts = [_add_half(g, r, cidx, f"grad_add_half_{a}") for a, (g, r) in enumerate(zip(grads, got))]
    got2 = _chip_scatter(parts)
    halves = [_add_owner(p, r, kidx, f"grad_add_owner_{a}") for a, (p, r) in enumerate(zip(parts, got2))]
    fulls = _sib_exchange(halves)
    big_names = ["hy_w_in", "hy_w_out", "pool_w_in", "pool_w_grp", "pool_w_out"]
    big_wmv = [(hy_w_in, m_hy_w_in, v_hy_w_in), (hy_w_out, m_hy_w_out, v_hy_w_out), (pool_w_in, m_pool_w_in, v_pool_w_in),
               (pool_w_grp, m_pool_w_grp, v_pool_w_grp), (pool_w_out, m_pool_w_out, v_pool_w_out)]
    for a, nm in enumerate(big_names):
        rr, cc = big[a].shape
        w, m, v = (t.reshape(rr, cc) for t in big_wmv[a])
        outs = _adam_2d(w, fulls[a].reshape(rr, cc), m, v, f"adam_{nm}")
        res[nm] = tuple(o.reshape(big_wmv[a][0].shape) for o in outs)

    loss = lax.psum(loss8[0, 0], ("x", "y", "c"))
    order = ["norm_g", "mod_w", "mod_b", "hy_w_in", "hy_conv_w", "hy_conv_b", "lru_w_a", "lru_b_a", "lru_w_x", "lru_b_x",
             "lru_lambda", "sc_conv_w", "hy_w_out", "pool_w_in", "pool_w_grp", "pool_b_grp", "pool_scale", "pool_w_out",
             "final_g"]
    return (loss, grad_x[None], *[res[nm][0] for nm in order], *[res[nm][1] for nm in order],
            *[res[nm][2] for nm in order], *[res[nm][3] for nm in order])
```

```python
import jax
import jax.numpy as jnp
from jax import lax
from jax.experimental import pallas as pl
from jax.experimental.pallas import tpu as pltpu

F32, BF16 = jnp.float32, jnp.bfloat16
D = 1024
RMS_EPS = 1e-6
LRU_C = 8.0
LRU_HEADS, LRU_HEAD_DIM = 8, 128
POOL_WINDOWS = (2, 4, 8, 16)
POOL_GROUP_DIM = 512
ADAM_LR, ADAM_B1, ADAM_B2, ADAM_EPS, ADAM_WD, ADAM_STEP = 0.001, 0.9, 0.999, 1e-08, 0.01, 10
MESH = pl.DeviceIdType.MESH
N_DEV, N_CHIP = 8, 4
SUBLANES = 8
BF16_ROWS = 16
POOL_HALO = 16
TS_PROJ, TS_MIX, TS_WGRAD, TS_DGRAD = 1024, 256, 1024, 256
SMALL_ROWS = 64
GRAD_WIRE_DTYPE = BF16
ANY = pl.BlockSpec(memory_space=pl.ANY)
VMEM = pl.BlockSpec(memory_space=pltpu.VMEM)
NT = (((1,), (1,)), ((), ()))
TN = (((0,), (0,)), ((), ()))


def _cp(sem=None, vmem_mb=56):
    kw = dict(vmem_limit_bytes=vmem_mb * 2 ** 20)
    if sem is not None:
        kw["dimension_semantics"] = sem
    return pltpu.CompilerParams(**kw)


def _tile(n, t):
    return min(n, t)


def _pos():
    return lax.axis_index("x"), lax.axis_index("y"), lax.axis_index("c")


def _flip(v, f):
    return 1 - v if f else v


def _sigmoid(z):
    return 1.0 / (1.0 + jnp.exp(-z))


def _rows(n, c):
    return lax.broadcasted_iota(jnp.int32, (n, c), 0)


def _down(a, d):
    return a if d == 0 else pltpu.roll(a, d, 0)


def _up(a, d):
    return a if d == 0 else pltpu.roll(a, a.shape[0] - d, 0)


def _scan_fwd(a, u):
    n = a.shape[0]
    rows = _rows(n, a.shape[1])
    p, g, d = a, u, 1
    while d < n:
        keep = rows >= d
        g = g + p * jnp.where(keep, _down(g, d), 0.0)
        if 2 * d < n:
            p = p * jnp.where(keep, _down(p, d), 1.0)
        d *= 2
    return g


def _scan_rev(alpha, b):
    n = alpha.shape[0]
    rows = _rows(n, alpha.shape[1])
    p, g, d = alpha, b, 1
    while d < n:
        keep = rows < n - d
        g = g + p * jnp.where(keep, _up(g, d), 0.0)
        if 2 * d < n:
            p = p * jnp.where(keep, _up(p, d), 1.0)
        d *= 2
    return g


def _conv_taps(ext, halo, n, width):
    return [_down(ext, width - 1 - k)[halo:halo + n] for k in range(width)]


def _lru_gates(xc, wa_ref, ba, wx_ref, bx):
    xb = xc.astype(BF16)
    pa, px = [], []
    for h in range(LRU_HEADS):
        xh = xb[:, h * LRU_HEAD_DIM:(h + 1) * LRU_HEAD_DIM]
        pa.append(jnp.dot(xh, wa_ref[h], preferred_element_type=F32))
        px.append(jnp.dot(xh, wx_ref[h], preferred_element_type=F32))
    r = _sigmoid(jnp.concatenate(pa, axis=1) + ba)
    ig = _sigmoid(jnp.concatenate(px, axis=1) + bx)
    return r, ig


def _softplus_neg(lam):
    return jnp.maximum(-lam, 0.0) + jnp.log1p(jnp.exp(-jnp.abs(lam)))


def _lru_decay(r, sp, first):
    big_l = (-LRU_C) * r * sp
    a = jnp.exp(big_l)
    th = jnp.tanh(big_l)
    m = jnp.sqrt(-2.0 * th / (1.0 - th))
    return a, jnp.where(first, 1.0, m)


def _pool_inv_counts(t0, n):
    t = (t0 + lax.broadcasted_iota(jnp.int32, (n, 1), 0) + 1).astype(F32)
    return [1.0 / jnp.minimum(t, float(w)) for w in POOL_WINDOWS]


def _window_sums(ext, shift):
    gd = POOL_GROUP_DIM
    out = []
    s = ext
    for k in range(len(POOL_WINDOWS)):
        s = s + shift(s, 2 ** k)
        out.append(s[:, 0:gd])
        if k + 1 < len(POOL_WINDOWS):
            s = s[:, gd:]
    return out


SW_ROWS, SW_COLS = 16, 2 * D
SW_CONV, SW_SC, SW_POOL_B, SW_POOL_S = 0, 4, 8, 9


def _mod_fwd(c8, mod_w, mod_b, conv_w, sc_w, pool_b, pool_s):
    nw = mod_w.shape[2]
    cq, pq = conv_w.shape[1], pool_b.shape[1]

    def body(c_ref, w_ref, b_ref, cw_ref, sw_ref, pb_ref, ps_ref, ca_ref, mod_ref, small_ref,
             cslot, mslot, msend, pslot, psend, s1, r1, s2, r2, s3, r3):
        x, y, c = _pos()
        me = 4 * x + 2 * y + c
        chip = 2 * x + y
        first = []
        for r in range(1, N_DEV):
            fx, fy, fc = (r >> 2) & 1, (r >> 1) & 1, r & 1
            cp = pltpu.make_async_remote_copy(
                src_ref=c_ref, dst_ref=cslot.at[me], send_sem=s1.at[r - 1], recv_sem=r1.at[r - 1],
                device_id=(_flip(x, fx), _flip(y, fy), _flip(c, fc)), device_id_type=MESH)
            cp.start()
            first.append(cp)
        cslot[me] = c_ref[...]
        for cp in first:
            cp.wait()
        rows = _rows(SUBLANES, D)
        call = jnp.zeros((SUBLANES, D), F32)
        for d in range(N_DEV):
            call = jnp.where(rows == d, cslot[d], call)
        ca = call * _sigmoid(call)
        ca_ref[...] = ca
        for l in range(2):
            msend[l] = jnp.dot(ca, w_ref[l], precision=lax.Precision.HIGHEST, preferred_element_type=F32)
        psend[...] = jnp.zeros_like(psend)
        psend[SW_CONV:SW_CONV + 4, 0:cq] = cw_ref[...]
        psend[SW_SC:SW_SC + 3, 0:cq] = sw_ref[...]
        psend[SW_POOL_B:SW_POOL_B + 1, :] = pb_ref[...]
        psend[SW_POOL_S:SW_POOL_S + 1, :] = ps_ref[...]
        second = []
        for q, (fx, fy) in enumerate(((1, 0), (0, 1), (1, 1))):
            peer = (_flip(x, fx), _flip(y, fy), c)
            for src, dst, ss, rs in ((msend, mslot, s2, r2), (psend, pslot, s3, r3)):
                cp = pltpu.make_async_remote_copy(src_ref=src, dst_ref=dst.at[chip], send_sem=ss.at[q], recv_sem=rs.at[q],
                                                  device_id=peer, device_id_type=MESH)
                cp.start()
                second.append(cp)
        mslot[chip] = msend[...]
        pslot[chip] = psend[...]
        for cp in second:
            cp.wait()
        small_ref[...] = jnp.zeros_like(small_ref)
        for j in range(N_CHIP):
            for l in range(2):
                mod_ref[l, :, j * nw:(j + 1) * nw] = mslot[j, l] + b_ref[l:l + 1, j * nw:(j + 1) * nw]
            small_ref[0:SUBLANES, j * cq:(j + 1) * cq] = pslot[j, 0:SUBLANES, 0:cq]
            small_ref[SUBLANES:SW_ROWS, j * pq:(j + 1) * pq] = pslot[j, SUBLANES:SW_ROWS, :]

    args = (c8, mod_w, mod_b, conv_w, sc_w, pool_b, pool_s)
    dma3 = pltpu.SemaphoreType.DMA((N_CHIP - 1,))
    return pl.pallas_call(
        body, name="mod_fwd",
        in_specs=[VMEM] * len(args), out_specs=[VMEM] * 3,
        out_shape=[jax.ShapeDtypeStruct((SUBLANES, D), F32), jax.ShapeDtypeStruct((2, SUBLANES, N_CHIP * nw), F32),
                   jax.ShapeDtypeStruct((SW_ROWS, SW_COLS), F32)],
        scratch_shapes=[pltpu.VMEM((N_DEV, SUBLANES, D), F32), pltpu.VMEM((N_CHIP, 2, SUBLANES, nw), F32),
                        pltpu.VMEM((2, SUBLANES, nw), F32), pltpu.VMEM((N_CHIP, SW_ROWS, pq), F32),
                        pltpu.VMEM((SW_ROWS, pq), F32),
                        pltpu.SemaphoreType.DMA((N_DEV - 1,)), pltpu.SemaphoreType.DMA((N_DEV - 1,)),
                        dma3, dma3, dma3, dma3],
        compiler_params=_cp(),
    )(*args)


def _wcast(ws):
    def body(*refs):
        n = len(refs) // 2
        for a in range(n):
            refs[n + a][...] = refs[a][...].astype(BF16)

    return pl.pallas_call(
        body, name="wcast", in_specs=[VMEM] * len(ws), out_specs=[VMEM] * len(ws),
        out_shape=[jax.ShapeDtypeStruct(w.shape, BF16) for w in ws], compiler_params=_cp(),
    )(*ws)


def _wcast_own_block(w, kidx, name):
    rr, cc = w.shape
    rb = min(rr, 256)

    def body(k_ref, w_ref, o_ref):
        o_ref[...] = w_ref[...].astype(BF16)

    return pl.pallas_call(
        body, name=name,
        grid_spec=pltpu.PrefetchScalarGridSpec(
            num_scalar_prefetch=1, grid=(rr // rb,),
            in_specs=[pl.BlockSpec((rb, cc), lambda j, k_ref: (j, 0))],
            out_specs=pl.BlockSpec((None, rb, cc), lambda j, k_ref: (k_ref[0], j, 0))),
        out_shape=jax.ShapeDtypeStruct((N_CHIP, rr, cc), BF16),
        compiler_params=_cp(("parallel",)),
    )(kidx, w)


def _wgather(bufs):
    n = len(bufs)

    def body(*refs):
        outs = refs[n:2 * n]
        ssem, rsem, fssem, frsem = refs[2 * n:]
        x, y, c = _pos()
        chip = 2 * x + y
        sib = (x, y, 1 - c)
        flips = ((1, 0), (0, 1), (1, 1))

        def half(a, which):
            hr = bufs[a].shape[1] // 2
            return pl.ds(pl.multiple_of(which * hr, BF16_ROWS), hr)

        sends = []
        for a in range(n):
            mine = outs[a].at[chip, half(a, c), :]
            for q, (fx, fy) in enumerate(flips):
                cp = pltpu.make_async_remote_copy(
                    src_ref=mine, dst_ref=mine, send_sem=ssem.at[3 * a + q], recv_sem=rsem.at[3 * a + q],
                    device_id=(_flip(x, fx), _flip(y, fy), c), device_id_type=MESH)
                cp.start()
                sends.append(cp)
        passed = []
        for a in range(n):
            for q, (fx, fy) in enumerate(flips):
                src_chip = 2 * _flip(x, fx) + _flip(y, fy)
                landed = outs[a].at[src_chip, half(a, c), :]
                pltpu.make_async_remote_copy(
                    src_ref=landed, dst_ref=landed, send_sem=ssem.at[3 * a + q], recv_sem=rsem.at[3 * a + q],
                    device_id=sib, device_id_type=MESH).wait_recv()
                cp = pltpu.make_async_remote_copy(
                    src_ref=landed, dst_ref=landed, send_sem=fssem.at[3 * a + q], recv_sem=frsem.at[3 * a + q],
                    device_id=sib, device_id_type=MESH)
                cp.start()
                passed.append(cp)
        for a in range(n):
            for q, (fx, fy) in enumerate(flips):
                src_chip = 2 * _flip(x, fx) + _flip(y, fy)
                other = outs[a].at[src_chip, half(a, 1 - c), :]
                pltpu.make_async_remote_copy(
                    src_ref=other, dst_ref=other, send_sem=fssem.at[3 * a + q], recv_sem=frsem.at[3 * a + q],
                    device_id=sib, device_id_type=MESH).wait_recv()
        for cp in sends + passed:
            cp.wait_send()

    return pl.pallas_call(
        body, name="wgather", in_specs=[ANY] * n, out_specs=[ANY] * n,
        out_shape=[jax.ShapeDtypeStruct(b.shape, BF16) for b in bufs],
        input_output_aliases={a: a for a in range(n)},
        scratch_shapes=[pltpu.SemaphoreType.DMA((3 * n,))] * 4,
        compiler_params=_cp(),
    )(*bufs)


def _norm_proj(x, g, sc, sh, w, name):
    s_len, nb = x.shape[0], w.shape[2]
    ts = _tile(s_len, TS_PROJ)

    def body(x_ref, g_ref, sc_ref, sh_ref, w_ref, h_ref, p_ref):
        @pl.when(pl.program_id(1) == 0)
        def _():
            xv = x_ref[...]
            r = lax.rsqrt(jnp.mean(xv * xv, axis=-1, keepdims=True) + RMS_EPS)
            h_ref[...] = (xv * r * (g_ref[...] * (1.0 + sc_ref[...])) + sh_ref[...]).astype(BF16)

        p_ref[...] = jnp.dot(h_ref[...], w_ref[...], preferred_element_type=F32).astype(BF16)

    vec = pl.BlockSpec((1, D), lambda i, j: (0, 0))
    return pl.pallas_call(
        body, name=name, grid=(s_len // ts, N_CHIP),
        in_specs=[pl.BlockSpec((ts, D), lambda i, j: (i, 0)), vec, vec, vec,
                  pl.BlockSpec((None, D, nb), lambda i, j: (j, 0, 0))],
        out_specs=[pl.BlockSpec((ts, D), lambda i, j: (i, 0)), pl.BlockSpec((ts, nb), lambda i, j: (i, j))],
        out_shape=[jax.ShapeDtypeStruct((s_len, D), BF16), jax.ShapeDtypeStruct((s_len, N_CHIP * nb), BF16)],
        compiler_params=_cp(("parallel", "arbitrary")),
    )(x, g, sc, sh, w)


def _l0_mix(proj, x, gate, cw, cb, wa, ba, wx, bx, lam, sw, wo):
    s_len = x.shape[0]
    ts = _tile(s_len, TS_MIX)
    hl = SUBLANES

    def body(p_ref, x_ref, gate_ref, cw_ref, cb_ref, wa_ref, ba_ref, wx_ref, bx_ref, lam_ref, sw_ref, wo_ref,
             x1_ref, h_ref, y_ref, cxa, czz, chh):
        i = pl.program_id(0)

        @pl.when(i == 0)
        def _():
            cxa[...] = jnp.zeros_like(cxa)
            czz[...] = jnp.zeros_like(czz)
            chh[...] = jnp.zeros_like(chh)

        xa, ga, gbp, gcp, v, gb = [p_ref[:, k * D:(k + 1) * D].astype(F32) for k in range(6)]
        rows = _rows(ts, D)
        taps = _conv_taps(jnp.concatenate([cxa[...], xa], axis=0), hl, ts, 4)
        xc = cb_ref[...] + sum(cw_ref[k:k + 1, :] * taps[k] for k in range(4))
        r, ig = _lru_gates(xc, wa_ref, ba_ref[...], wx_ref, bx_ref[...])
        a, m = _lru_decay(r, _softplus_neg(lam_ref[...]), (rows == 0) & (i == 0))
        u = m * ig * xc + jnp.where(rows == 0, a * chh[hl - 1:hl, :], 0.0)
        h = _scan_fwd(a, u)
        z = gcp * v
        ztaps = _conv_taps(jnp.concatenate([czz[...], z], axis=0), hl, ts, 3)
        yb = gbp * sum(sw_ref[k:k + 1, :] * ztaps[k] for k in range(3))
        y = jnp.concatenate([h * (ga * _sigmoid(ga)), yb * (gb * _sigmoid(gb))], axis=1).astype(BF16)
        y_ref[...] = y
        x1_ref[...] = x_ref[...] + gate_ref[...] * jnp.dot(y, wo_ref[...], preferred_element_type=F32)
        h_ref[...] = h.astype(BF16)
        cxa[...] = xa[ts - hl:, :]
        czz[...] = z[ts - hl:, :]
        chh[...] = h[ts - hl:, :]

    def full(a):
        return pl.BlockSpec(a.shape, lambda i: (0,) * a.ndim)

    row = lambda w: pl.BlockSpec((ts, w), lambda i: (i, 0))
    return pl.pallas_call(
        body, name="l0_mix", grid=(s_len // ts,),
        in_specs=[row(6 * D), row(D)] + [full(a) for a in (gate, cw, cb, wa, ba, wx, bx, lam, sw, wo)],
        out_specs=[row(D), row(D), row(2 * D)],
        out_shape=[jax.ShapeDtypeStruct((s_len, D), F32), jax.ShapeDtypeStruct((s_len, D), BF16),
                   jax.ShapeDtypeStruct((s_len, 2 * D), BF16)],
        scratch_shapes=[pltpu.VMEM((hl, D), F32)] * 3,
        compiler_params=_cp(("arbitrary",)),
    )(proj, x, gate, cw, cb, wa, ba, wx, bx, lam, sw, wo)


def _l1_mix(proj, x1, tgt, gate, wg, bg, scale, wo, gf):
    s_len = x1.shape[0]
    ts = _tile(s_len, TS_MIX)
    pw, gd, hl = 2 * D, POOL_GROUP_DIM, POOL_HALO

    def body(p_ref, x_ref, t_ref, gate_ref, wg_ref, bg_ref, sc_ref, wo_ref, gf_ref,
             d_ref, mx_ref, y_ref, dx_ref, loss_ref, dgf_ref, cv):
        i = pl.program_id(0)

        @pl.when(i == 0)
        def _():
            cv[...] = jnp.zeros_like(cv)
            loss_ref[...] = jnp.zeros_like(loss_ref)
            dgf_ref[...] = jnp.zeros_like(dgf_ref)

        v = p_ref[:, 0:pw].astype(F32)
        gg = p_ref[:, pw:2 * pw].astype(F32)
        sums = _window_sums(jnp.concatenate([cv[...], v], axis=0), _down)
        inv = _pool_inv_counts(i * ts, ts)
        dd = [sums[k][hl:hl + ts] * inv[k] - v[:, k * gd:(k + 1) * gd] for k in range(4)]
        mixed = jnp.concatenate(
            [jnp.dot(dd[k].astype(BF16), wg_ref[k], preferred_element_type=F32) for k in range(4)], axis=1) + bg_ref[...]
        d_ref[...] = jnp.concatenate(dd, axis=1).astype(BF16)
        mx_ref[...] = mixed.astype(BF16)
        y = (mixed * sc_ref[...] * (gg * _sigmoid(gg))).astype(BF16)
        y_ref[...] = y
        x2 = x_ref[...] + gate_ref[...] * jnp.dot(y, wo_ref[...], preferred_element_type=F32)
        r2 = lax.rsqrt(jnp.mean(x2 * x2, axis=-1, keepdims=True) + RMS_EPS)
        n2 = x2 * r2
        err = n2 * gf_ref[...] - t_ref[...]
        loss_ref[...] += jnp.sum(err * err, axis=0, keepdims=True)
        dyf = err * (1.0 / D)
        dgf_ref[...] += jnp.sum(dyf * n2, axis=0, keepdims=True)
        dn = dyf * gf_ref[...]
        dx_ref[...] = r2 * (dn - n2 * jnp.mean(dn * n2, axis=-1, keepdims=True))
        cv[...] = v[ts - hl:, :]

    def full(a):
        return pl.BlockSpec(a.shape, lambda i: (0,) * a.ndim)

    row = lambda w: pl.BlockSpec((ts, w), lambda i: (i, 0))
    acc = pl.BlockSpec((1, D), lambda i: (0, 0))
    return pl.pallas_call(
        body, name="l1_mix", grid=(s_len // ts,),
        in_specs=[row(2 * pw), row(D), row(D)] + [full(a) for a in (gate, wg, bg, scale, wo, gf)],
        out_specs=[row(pw), row(pw), row(pw), row(D), acc, acc],
        out_shape=[jax.ShapeDtypeStruct((s_len, pw), BF16)] * 3 + [jax.ShapeDtypeStruct((s_len, D), F32)]
        + [jax.ShapeDtypeStruct((1, D), F32)] * 2,
        scratch_shapes=[pltpu.VMEM((hl, pw), F32)],
        compiler_params=_cp(("arbitrary",)),
    )(proj, x1, tgt, gate, wg, bg, scale, wo, gf)


def _l1_bwd_mix(dx2, proj, mixed, gate, wg, scale, wo):
    s_len = dx2.shape[0]
    ts = _tile(s_len, TS_MIX)
    n_t = s_len // ts
    pw, gd, hl = 2 * D, POOL_GROUP_DIM, POOL_HALO

    def body(dx_ref, gg_ref, mx_ref, gate_ref, wg_ref, sc_ref, wo_ref, dp_ref, dmx_ref, dsc_ref, dbg_ref, cq):
        i = pl.program_id(0)

        @pl.when(i == 0)
        def _():
            cq[...] = jnp.zeros_like(cq)
            dsc_ref[...] = jnp.zeros_like(dsc_ref)
            dbg_ref[...] = jnp.zeros_like(dbg_ref)

        dy = lax.dot_general((gate_ref[...] * dx_ref[...]).astype(BF16), wo_ref[...], NT, preferred_element_type=F32)
        gg = gg_ref[...].astype(F32)
        mixed = mx_ref[...].astype(F32)
        s = _sigmoid(gg)
        sg = gg * s
        dmixed = dy * sc_ref[...] * sg
        dsc_ref[...] += jnp.sum(dy * mixed * sg, axis=0, keepdims=True)
        dbg_ref[...] += jnp.sum(dmixed, axis=0, keepdims=True)
        dmb = dmixed.astype(BF16)
        dmx_ref[...] = dmb
        dp_ref[:, pw:2 * pw] = (dy * sc_ref[...] * mixed * (s * (1.0 + gg * (1.0 - s)))).astype(BF16)
        inv = _pool_inv_counts((n_t - 1 - i) * ts, ts)
        dd = [lax.dot_general(dmb[:, k * gd:(k + 1) * gd], wg_ref[k], NT, preferred_element_type=F32) for k in range(4)]
        q = jnp.concatenate([dd[k] * inv[k] for k in range(4)], axis=1)
        sums = _window_sums(jnp.concatenate([q, cq[...]], axis=0), _up)
        dp_ref[:, 0:pw] = jnp.concatenate([sums[k][0:ts] - dd[k] for k in range(4)], axis=1).astype(BF16)
        cq[...] = q[0:hl, :]

    def full(a):
        return pl.BlockSpec(a.shape, lambda i: (0,) * a.ndim)

    rev = lambda w, j=0: pl.BlockSpec((ts, w), lambda i: (n_t - 1 - i, j))
    acc = pl.BlockSpec((1, pw), lambda i: (0, 0))
    return pl.pallas_call(
        body, name="l1_bwd_mix", grid=(n_t,),
        in_specs=[rev(D), rev(pw, 1), rev(pw)] + [full(a) for a in (gate, wg, scale, wo)],
        out_specs=[rev(2 * pw), rev(pw), acc, acc],
        out_shape=[jax.ShapeDtypeStruct((s_len, 2 * pw), BF16), jax.ShapeDtypeStruct((s_len, pw), BF16),
                   jax.ShapeDtypeStruct((1, pw), F32), jax.ShapeDtypeStruct((1, pw), F32)],
        scratch_shapes=[pltpu.VMEM((hl, pw), F32)],
        compiler_params=_cp(("arbitrary",)),
    )(dx2, proj, mixed, gate, wg, scale, wo)


def _l0_bwd_mix(dx1, proj, hst, gate, cw, cb, wa, ba, wx, bx, lam, sw, wo):
    s_len = dx1.shape[0]
    ts = _tile(s_len, TS_MIX)
    n_t = s_len // ts
    hl, hb = SUBLANES, BF16_ROWS

    def body(dx_ref, p_ref, ph_ref, h_ref, hh_ref, gate_ref, cw_ref, cb_ref, wa_ref, ba_ref, wx_ref, bx_ref,
             lam_ref, sw_ref, wo_ref, dp_ref, xc_ref, dpa_ref, dpx_ref, sm_ref, cg, cdxc, cdcz):
        i = pl.program_id(0)
        ri = n_t - 1 - i

        @pl.when(i == 0)
        def _():
            cg[...] = jnp.zeros_like(cg)
            cdxc[...] = jnp.zeros_like(cdxc)
            cdcz[...] = jnp.zeros_like(cdcz)
            sm_ref[...] = jnp.zeros_like(sm_ref)

        has_prev = (ri > 0).astype(F32)
        xa, ga, gbp, gcp, v, gb = [p_ref[:, k * D:(k + 1) * D].astype(F32) for k in range(6)]
        prev = lambda k: ph_ref[:, k * D:(k + 1) * D].astype(F32)[hb - hl:hb] * has_prev
        rows = _rows(ts, D)
        first = (rows == 0) & (ri == 0)
        xtaps = _conv_taps(jnp.concatenate([prev(0), xa], axis=0), hl, ts, 4)
        xc = cb_ref[...] + sum(cw_ref[k:k + 1, :] * xtaps[k] for k in range(4))
        r, ig = _lru_gates(xc, wa_ref, ba_ref[...], wx_ref, bx_ref[...])
        sp = _softplus_neg(lam_ref[...])
        a, m = _lru_decay(r, sp, first)
        z = gcp * v
        ztaps = _conv_taps(jnp.concatenate([prev(3) * prev(4), z], axis=0), hl, ts, 3)
        cz = sum(sw_ref[k:k + 1, :] * ztaps[k] for k in range(3))
        h = h_ref[...].astype(F32)
        hprev = _down(jnp.concatenate([hh_ref[...].astype(F32)[hb - hl:hb] * has_prev, h], axis=0), 1)[hl:hl + ts]
        dy = lax.dot_general((gate_ref[...] * dx_ref[...]).astype(BF16), wo_ref[...], NT, preferred_element_type=F32)
        dya_pre, dyb_pre = dy[:, 0:D], dy[:, D:2 * D]
        s_a, s_b = _sigmoid(ga), _sigmoid(gb)
        dp_ref[:, D:2 * D] = (dya_pre * h * (s_a * (1.0 + ga * (1.0 - s_a)))).astype(BF16)
        dp_ref[:, 5 * D:6 * D] = (dyb_pre * (gbp * cz) * (s_b * (1.0 + gb * (1.0 - s_b)))).astype(BF16)
        dya = dya_pre * (ga * s_a)
        dyb = dyb_pre * (gb * s_b)
        dp_ref[:, 2 * D:3 * D] = (dyb * cz).astype(BF16)
        dcz = dyb * gbp
        for k in range(3):
            sm_ref[8 + k:9 + k, :] += jnp.sum(dcz * ztaps[k], axis=0, keepdims=True)
        dcz_ext = jnp.concatenate([dcz, cdcz[...]], axis=0)
        dz = sum(sw_ref[k:k + 1, :] * _up(dcz_ext, 2 - k)[0:ts] for k in range(3))
        dp_ref[:, 3 * D:4 * D] = (dz * v).astype(BF16)
        dp_ref[:, 4 * D:5 * D] = (dz * gcp).astype(BF16)
        cdcz[...] = dcz[0:hl, :]
        alpha = jnp.where(rows < ts - 1, _up(a, 1), 0.0)
        dh = _scan_rev(alpha, dya + jnp.where(rows == ts - 1, cg[0:1, :], 0.0))
        cg[...] = a[0:hl, :] * dh[0:hl, :]
        da = dh * hprev
        dm = dh * ig * xc
        di = dh * m * xc
        dxc = dh * m * ig
        dl = da * a - jnp.where(first, 0.0, dm * (a * a) / m)
        sm_ref[7:8, :] += jnp.sum(dl * r, axis=0, keepdims=True) * (-LRU_C)
        dpa = (dl * sp) * (-LRU_C) * r * (1.0 - r)
        dpx = di * ig * (1.0 - ig)
        sm_ref[5:6, :] += jnp.sum(dpa, axis=0, keepdims=True)
        sm_ref[6:7, :] += jnp.sum(dpx, axis=0, keepdims=True)
        dpa_b, dpx_b = dpa.astype(BF16), dpx.astype(BF16)
        dpa_ref[...] = dpa_b
        dpx_ref[...] = dpx_b
        xc_ref[...] = xc.astype(BF16)
        back = []
        for hd in range(LRU_HEADS):
            sl = slice(hd * LRU_HEAD_DIM, (hd + 1) * LRU_HEAD_DIM)
            back.append(lax.dot_general(dpa_b[:, sl], wa_ref[hd], NT, preferred_element_type=F32)
                        + lax.dot_general(dpx_b[:, sl], wx_ref[hd], NT, preferred_element_type=F32))
        dxc = dxc + jnp.concatenate(back, axis=1)
        sm_ref[4:5, :] += jnp.sum(dxc, axis=0, keepdims=True)
        for k in range(4):
            sm_ref[k:k + 1, :] += jnp.sum(dxc * xtaps[k], axis=0, keepdims=True)
        dxc_ext = jnp.concatenate([dxc, cdxc[...]], axis=0)
        dp_ref[:, 0:D] = sum(cw_ref[k:k + 1, :] * _up(dxc_ext, 3 - k)[0:ts] for k in range(4)).astype(BF16)
        cdxc[...] = dxc[0:hl, :]

    def full(a):
        return pl.BlockSpec(a.shape, lambda i: (0,) * a.ndim)

    rev = lambda w: pl.BlockSpec((ts, w), lambda i: (n_t - 1 - i, 0))
    halo = lambda w: pl.BlockSpec((hb, w), lambda i: (jnp.maximum((n_t - 1 - i) * (ts // hb) - 1, 0), 0))
    return pl.pallas_call(
        body, name="l0_bwd_mix", grid=(n_t,),
        in_specs=[rev(D), rev(6 * D), halo(6 * D), rev(D), halo(D)]
        + [full(a) for a in (gate, cw, cb, wa, ba, wx, bx, lam, sw, wo)],
        out_specs=[rev(6 * D), rev(D), rev(D), rev(D), pl.BlockSpec((2 * SUBLANES, D), lambda i: (0, 0))],
        out_shape=[jax.ShapeDtypeStruct((s_len, 6 * D), BF16)] + [jax.ShapeDtypeStruct((s_len, D), BF16)] * 3
        + [jax.ShapeDtypeStruct((2 * SUBLANES, D), F32)],
        scratch_shapes=[pltpu.VMEM((hl, D), F32)] * 3,
        compiler_params=_cp(("arbitrary",)),
    )(dx1, proj, proj, hst, hst, gate, cw, cb, wa, ba, wx, bx, lam, sw, wo)


def _dgrad_norm(dproj, w, x, dres, g, sc, name):
    s_len, nb = x.shape[0], w.shape[2]
    ts = _tile(s_len, TS_DGRAD)

    def body(dp_ref, w_ref, x_ref, dr_ref, g_ref, sc_ref, dx_ref, s1_ref, s2_ref):
        @pl.when(pl.program_id(0) == 0)
        def _():
            s1_ref[...] = jnp.zeros_like(s1_ref)
            s2_ref[...] = jnp.zeros_like(s2_ref)

        dh = sum(lax.dot_general(dp_ref[:, k * nb:(k + 1) * nb], w_ref[k], NT, preferred_element_type=F32)
                 for k in range(N_CHIP))
        xv = x_ref[...]
        r = lax.rsqrt(jnp.mean(xv * xv, axis=-1, keepdims=True) + RMS_EPS)
        n = xv * r
        s1_ref[...] += jnp.sum(dh, axis=0, keepdims=True)
        s2_ref[...] += jnp.sum(dh * n, axis=0, keepdims=True)
        dn = dh * (g_ref[...] * (1.0 + sc_ref[...]))
        dx_ref[...] = dr_ref[...] + r * (dn - n * jnp.mean(dn * n, axis=-1, keepdims=True))

    row = lambda wd: pl.BlockSpec((ts, wd), lambda i: (i, 0))
    vec = pl.BlockSpec((1, D), lambda i: (0, 0))
    return pl.pallas_call(
        body, name=name, grid=(s_len // ts,),
        in_specs=[row(N_CHIP * nb), pl.BlockSpec(w.shape, lambda i: (0, 0, 0)), row(D), row(D), vec, vec],
        out_specs=[row(D), vec, vec],
        out_shape=[jax.ShapeDtypeStruct((s_len, D), F32)] + [jax.ShapeDtypeStruct((1, D), F32)] * 2,
        compiler_params=_cp(("arbitrary",)),
    )(dproj, w, x, dres, g, sc)


def _wgrad(a, b, groups, ka, nb, a_col, b_col, name):
    s_len = a.shape[0]
    ts = _tile(s_len, TS_WGRAD)

    def body(a_ref, b_ref, o_ref):
        @pl.when(pl.program_id(1) == 0)
        def _():
            o_ref[...] = jnp.zeros_like(o_ref)

        o_ref[...] += lax.dot_general(a_ref[...].astype(BF16), b_ref[...].astype(BF16), TN, preferred_element_type=F32)

    return pl.pallas_call(
        body, name=name, grid=(groups, s_len // ts),
        in_specs=[pl.BlockSpec((ts, ka), lambda g, s: (s, a_col(g))), pl.BlockSpec((ts, nb), lambda g, s: (s, b_col(g)))],
        out_specs=pl.BlockSpec((None, ka, nb), lambda g, s: (g, 0, 0)),
        out_shape=jax.ShapeDtypeStruct((groups, ka, nb), F32),
        compiler_params=_cp(("parallel", "arbitrary")),
    )(a, b)


def _wgrad_rows(a, b, groups, name):
    s_len, nb = b.shape
    ka = a.shape[1] // groups
    ts = _tile(s_len, TS_WGRAD)

    def body(a_ref, b_ref, o_ref):
        @pl.when(pl.program_id(0) == 0)
        def _():
            o_ref[...] = jnp.zeros_like(o_ref)

        bb = b_ref[...].astype(BF16)
        for g in range(groups):
            o_ref[g] += lax.dot_general(a_ref[:, g * ka:(g + 1) * ka], bb, TN, preferred_element_type=F32)

    return pl.pallas_call(
        body, name=name, grid=(s_len // ts,),
        in_specs=[pl.BlockSpec((ts, groups * ka), lambda s: (s, 0)), pl.BlockSpec((ts, nb), lambda s: (s, 0))],
        out_specs=pl.BlockSpec((groups, ka, nb), lambda s: (0, 0, 0)),
        out_shape=jax.ShapeDtypeStruct((groups, ka, nb), F32),
        compiler_params=_cp(("arbitrary",)),
    )(a, b)


def _wgrad_heads(xc, dpa, dpx):
    s_len = xc.shape[0]
    ts = _tile(s_len, TS_WGRAD)
    hd = LRU_HEAD_DIM

    def body(x_ref, a_ref, b_ref, oa_ref, ob_ref):
        @pl.when(pl.program_id(0) == 0)
        def _():
            oa_ref[...] = jnp.zeros_like(oa_ref)
            ob_ref[...] = jnp.zeros_like(ob_ref)

        for h in range(LRU_HEADS):
            sl = slice(h * hd, (h + 1) * hd)
            oa_ref[h] += lax.dot_general(x_ref[:, sl], a_ref[:, sl], TN, preferred_element_type=F32)
            ob_ref[h] += lax.dot_general(x_ref[:, sl], b_ref[:, sl], TN, preferred_element_type=F32)

    row = pl.BlockSpec((ts, D), lambda s: (s, 0))
    acc = pl.BlockSpec((LRU_HEADS, hd, hd), lambda s: (0, 0, 0))
    return pl.pallas_call(
        body, name="l0_wgrad_heads", grid=(s_len // ts,), in_specs=[row] * 3, out_specs=[acc] * 2,
        out_shape=[jax.ShapeDtypeStruct((LRU_HEADS, hd, hd), F32)] * 2,
        compiler_params=_cp(("arbitrary",)),
    )(xc, dpa, dpx)


def _wo_final(mt, wo, gate, name):
    rb = mt.shape[1]

    def body(m_ref, w_ref, gate_ref, dw_ref, dg_ref):
        @pl.when(pl.program_id(0) == 0)
        def _():
            dg_ref[...] = jnp.zeros_like(dg_ref)

        mv = m_ref[...]
        dw_ref[...] = mv * gate_ref[...]
        dg_ref[...] += jnp.sum(mv * w_ref[...].astype(F32), axis=0, keepdims=True)

    blk = pl.BlockSpec((None, rb, D), lambda k: (k, 0, 0))
    vec = pl.BlockSpec((1, D), lambda k: (0, 0))
    return pl.pallas_call(
        body, name=name, grid=(N_CHIP,), in_specs=[blk, blk, vec], out_specs=[blk, vec],
        out_shape=[jax.ShapeDtypeStruct(mt.shape, F32), jax.ShapeDtypeStruct((1, D), F32)],
        compiler_params=_cp(("arbitrary",)),
    )(mt, wo, gate)


ROW_NORM_G, ROW_CONV_W, ROW_CONV_B, ROW_B_A, ROW_B_X, ROW_LAMBDA, ROW_SC_W, ROW_POOL_B, ROW_POOL_S, ROW_FINAL_G = (
    0, 2, 6, 7, 8, 9, 10, 13, 15, 17)


def _small_pack(s1_0, s2_0, s1_1, s2_1, sm0, dsc1, dbg1, dgf, losscols, dgate0, dgate1, norm_g, sc0, sc1, lam):
    def body(s1_0r, s2_0r, s1_1r, s2_1r, sm, dsc, dbg, dgfr, lcols, dg0, dg1, ng, sc0r, sc1r, lamr, buf, dmod, loss):
        buf[...] = jnp.zeros_like(buf)
        buf[0:1, :] = s2_0r[...] * (1.0 + sc0r[...])
        buf[1:2, :] = s2_1r[...] * (1.0 + sc1r[...])
        buf[ROW_CONV_W:ROW_CONV_W + 4, :] = sm[0:4, :]
        buf[ROW_CONV_B:ROW_CONV_B + 1, :] = sm[4:5, :]
        buf[ROW_B_A:ROW_B_A + 1, :] = sm[5:6, :]
        buf[ROW_B_X:ROW_B_X + 1, :] = sm[6:7, :]
        buf[ROW_LAMBDA:ROW_LAMBDA + 1, :] = -sm[7:8, :] * _sigmoid(-lamr[...])
        buf[ROW_SC_W:ROW_SC_W + 3, :] = sm[8:11, :]
        for k in range(2):
            buf[ROW_POOL_B + k:ROW_POOL_B + k + 1, :] = dbg[:, k * D:(k + 1) * D]
            buf[ROW_POOL_S + k:ROW_POOL_S + k + 1, :] = dsc[:, k * D:(k + 1) * D]
        buf[ROW_FINAL_G:ROW_FINAL_G + 1, :] = dgfr[...]
        pieces = (s1_0r[...], s2_0r[...] * ng[0:1, :], dg0[...], s1_1r[...], s2_1r[...] * ng[1:2, :], dg1[...])
        for k, pc in enumerate(pieces):
            dmod[:, k * D:(k + 1) * D] = jnp.broadcast_to(pc, (SUBLANES, D))
        loss[...] = jnp.broadcast_to(jnp.sum(lcols[...], axis=1, keepdims=True) * (0.5 / D), loss.shape)

    args = (s1_0, s2_0, s1_1, s2_1, sm0, dsc1, dbg1, dgf, losscols, dgate0, dgate1, norm_g, sc0, sc1, lam)
    return pl.pallas_call(
        body, name="small_pack", in_specs=[VMEM] * len(args), out_specs=[VMEM] * 3,
        out_shape=[jax.ShapeDtypeStruct((SMALL_ROWS, D), F32), jax.ShapeDtypeStruct((SUBLANES, 6 * D), F32),
                   jax.ShapeDtypeStruct((SUBLANES, 128), F32)],
        compiler_params=_cp(),
    )(*args)


def _small_comm(buf_a, buf_b, dmod8):
    ra, rb = buf_a.shape[0] // N_DEV, buf_b.shape[0] // N_DEV
    wb = buf_b.shape[1]

    def body(a_ref, b_ref, dm_ref, oa_ref, ob_ref, odm_ref, ina, inb, dslot, sa, sb, s1, r1, s2, r2):
        x, y, c = _pos()
        me = 4 * x + 2 * y + c
        peers = []
        for r in range(1, N_DEV):
            fx, fy, fc = (r >> 2) & 1, (r >> 1) & 1, r & 1
            px, py, pc = _flip(x, fx), _flip(y, fy), _flip(c, fc)
            peers.append(((px, py, pc), 4 * px + 2 * py + pc))
        seg_a = lambda d: pl.ds(pl.multiple_of(d * ra, SUBLANES), ra)
        seg_b = lambda d: pl.ds(pl.multiple_of(d * rb, SUBLANES), rb)
        first = []
        for r, (peer, pid) in enumerate(peers):
            for k, (src, dst) in enumerate(((a_ref.at[seg_a(pid), :], ina.at[r]), (b_ref.at[seg_b(pid), :], inb.at[r]),
                                            (dm_ref, dslot.at[me]))):
                cp = pltpu.make_async_remote_copy(src_ref=src, dst_ref=dst, send_sem=s1.at[3 * r + k],
                                                  recv_sem=r1.at[3 * r + k], device_id=peer, device_id_type=MESH)
                cp.start()
                first.append(cp)
        dslot[me] = dm_ref[...]
        for cp in first:
            cp.wait()
        acc_a, acc_b = a_ref[seg_a(me), :], b_ref[seg_b(me), :]
        for r in range(N_DEV - 1):
            acc_a = acc_a + ina[r]
            acc_b = acc_b + inb[r]
        sa[...] = acc_a
        sb[...] = acc_b
        oa_ref[seg_a(me), :] = acc_a
        ob_ref[seg_b(me), :] = acc_b
        second = []
        for r, (peer, pid) in enumerate(peers):
            for k, (src, dst) in enumerate(((sa, oa_ref.at[seg_a(me), :]), (sb, ob_ref.at[seg_b(me), :]))):
                cp = pltpu.make_async_remote_copy(src_ref=src, dst_ref=dst, send_sem=s2.at[2 * r + k],
                                                  recv_sem=r2.at[2 * r + k], device_id=peer, device_id_type=MESH)
                cp.start()
                second.append(cp)
        rows = _rows(SUBLANES, dm_ref.shape[1])
        dm_all = jnp.zeros(dm_ref.shape, F32)
        for d in range(N_DEV):
            dm_all = jnp.where(rows == d, dslot[d], dm_all)
        odm_ref[...] = dm_all
        for cp in second:
            cp.wait()

    nrel = N_DEV - 1
    return pl.pallas_call(
        body, name="small_comm", in_specs=[VMEM] * 3, out_specs=[VMEM] * 3,
        out_shape=[jax.ShapeDtypeStruct(buf_a.shape, F32), jax.ShapeDtypeStruct(buf_b.shape, F32),
                   jax.ShapeDtypeStruct(dmod8.shape, F32)],
        scratch_shapes=[pltpu.VMEM((nrel, ra, D), F32), pltpu.VMEM((nrel, rb, wb), F32),
                        pltpu.VMEM((N_DEV,) + dmod8.shape, F32), pltpu.VMEM((ra, D), F32), pltpu.VMEM((rb, wb), F32),
                        pltpu.SemaphoreType.DMA((3 * nrel,)), pltpu.SemaphoreType.DMA((3 * nrel,)),
                        pltpu.SemaphoreType.DMA((2 * nrel,)), pltpu.SemaphoreType.DMA((2 * nrel,))],
        compiler_params=_cp(),
    )(buf_a, buf_b, dmod8)


def _adam(w, g, m, v):
    m2 = ADAM_B1 * m + (1.0 - ADAM_B1) * g
    v2 = ADAM_B2 * v + (1.0 - ADAM_B2) * (g * g)
    m_hat = m2 / (1.0 - ADAM_B1 ** ADAM_STEP)
    v_hat = v2 / (1.0 - ADAM_B2 ** ADAM_STEP)
    return -ADAM_LR * (m_hat / (jnp.sqrt(v_hat) + ADAM_EPS) + ADAM_WD * w), m2, v2


def _small_adam(red_a, red_b, dm_all, params):
    n = len(params)

    def body(*refs):
        ra, rb, dm = refs[:3]
        wmv = refs[3:3 + 3 * n]
        outs = refs[3 + 3 * n:]
        x, y, _ = _pos()
        chip = 2 * x + y

        def shard(row0, nrows, width):
            per_row = D // width
            cands = []
            for k in range(N_CHIP):
                if nrows == 1 or per_row >= N_CHIP:
                    cands.append(ra[row0:row0 + nrows, k * width:(k + 1) * width])
                else:
                    rr, cc = divmod(k * width, D)
                    cands.append(ra[row0 + rr:row0 + rr + 1, cc:cc + width])
            g = cands[0]
            for k in range(1, N_CHIP):
                g = jnp.where(chip == k, cands[k], g)
            return g

        dms = jnp.sum(dm[...], axis=0, keepdims=True)
        hw = LRU_HEADS * LRU_HEAD_DIM
        grads = [
            ra[ROW_NORM_G:ROW_NORM_G + 2, :],
            None,
            shard(ROW_CONV_W, 4, D // N_CHIP),
            ra[ROW_CONV_B:ROW_CONV_B + 1, :],
            rb[0:hw, :],
            ra[ROW_B_A:ROW_B_A + 1, :],
            rb[hw:2 * hw, :],
            ra[ROW_B_X:ROW_B_X + 1, :],
            ra[ROW_LAMBDA:ROW_LAMBDA + 1, :],
            shard(ROW_SC_W, 3, D // N_CHIP),
            shard(ROW_POOL_B, 2, 2 * D // N_CHIP),
            shard(ROW_POOL_S, 2, 2 * D // N_CHIP),
            ra[ROW_FINAL_G:ROW_FINAL_G + 1, :],
        ]
        for p in range(n):
            w_ref, m_ref, v_ref = wmv[3 * p:3 * p + 3]
            g_out, d_out, m_out, v_out = outs[4 * p:4 * p + 4]
            if grads[p] is None:
                for l in range(2):
                    g = dms[:, l * 3 * D:(l + 1) * 3 * D]
                    dl, m2, v2 = _adam(w_ref[l:l + 1, :], g, m_ref[l:l + 1, :], v_ref[l:l + 1, :])
                    g_out[l:l + 1, :] = g
                    d_out[l:l + 1, :] = dl
                    m_out[l:l + 1, :] = m2
                    v_out[l:l + 1, :] = v2
            else:
                g = grads[p]
                dl, m2, v2 = _adam(w_ref[...], g, m_ref[...], v_ref[...])
                g_out[...] = g
                d_out[...] = dl
                m_out[...] = m2
                v_out[...] = v2

    flat = [a for p in params for a in p]
    return pl.pallas_call(
        body, name="small_adam", in_specs=[VMEM] * (3 + len(flat)), out_specs=[VMEM] * (4 * n),
        out_shape=[jax.ShapeDtypeStruct(p[0].shape, F32) for p in params for _ in range(4)],
        compiler_params=_cp(),
    )(red_a, red_b, dm_all, *flat)


def _modw_adam(ca_t, dm_sh, w, m, v):
    nw = w.shape[2]

    def body(c_ref, d_ref, w_ref, m_ref, v_ref, g_out, d_out, m_out, v_out):
        g = jnp.dot(c_ref[...], d_ref[...], precision=lax.Precision.HIGHEST, preferred_element_type=F32)
        dl, m2, v2 = _adam(w_ref[...], g, m_ref[...], v_ref[...])
        g_out[...] = g
        d_out[...] = dl
        m_out[...] = m2
        v_out[...] = v2

    blk = pl.BlockSpec((None, D, nw), lambda l: (l, 0, 0))
    return pl.pallas_call(
        body, name="modw_adam", grid=(2,),
        in_specs=[pl.BlockSpec((D, SUBLANES), lambda l: (0, 0)), pl.BlockSpec((None, SUBLANES, nw), lambda l: (l, 0, 0)),
                  blk, blk, blk],
        out_specs=[blk] * 4, out_shape=[jax.ShapeDtypeStruct(w.shape, F32)] * 4,
        compiler_params=_cp(("arbitrary",)),
    )(ca_t, dm_sh, w, m, v)


def _half_rows(r):
    return r // 2


def _sib_send_halves(gs):
    n = len(gs)

    def body(*refs):
        ins, outs, ssem, rsem = refs[:n], refs[n:2 * n], refs[2 * n], refs[2 * n + 1]
        x, y, c = _pos()
        cps = []
        for a in range(n):
            hr = _half_rows(gs[a].shape[1])
            cp = pltpu.make_async_remote_copy(
                src_ref=ins[a].at[:, pl.ds(pl.multiple_of((1 - c) * hr, SUBLANES), hr), :], dst_ref=outs[a],
                send_sem=ssem.at[a], recv_sem=rsem.at[a], device_id=(x, y, 1 - c), device_id_type=MESH)
            cp.start()
            cps.append(cp)
        for cp in cps:
            cp.wait()

    return pl.pallas_call(
        body, name="grad_sib_halves", in_specs=[ANY] * n, out_specs=[ANY] * n,
        out_shape=[jax.ShapeDtypeStruct((N_CHIP, _half_rows(g.shape[1]), g.shape[2]), F32) for g in gs],
        scratch_shapes=[pltpu.SemaphoreType.DMA((n,)), pltpu.SemaphoreType.DMA((n,))],
        compiler_params=_cp(),
    )(*gs)


def _add_half(g, got, cidx, name):
    _, hr, cc = got.shape
    rb = min(hr, 256)

    def body(c_ref, g_ref, r_ref, o_ref):
        o_ref[...] = (g_ref[...] + r_ref[...]).astype(o_ref.dtype)

    blk = pl.BlockSpec((None, rb, cc), lambda k, j, c_ref: (k, j, 0))
    return pl.pallas_call(
        body, name=name,
        grid_spec=pltpu.PrefetchScalarGridSpec(
            num_scalar_prefetch=1, grid=(N_CHIP, hr // rb),
            in_specs=[pl.BlockSpec((None, rb, cc), lambda k, j, c_ref: (k, c_ref[0] * (hr // rb) + j, 0)), blk],
            out_specs=blk),
        out_shape=jax.ShapeDtypeStruct(got.shape, GRAD_WIRE_DTYPE),
        compiler_params=_cp(("parallel", "parallel")),
    )(cidx, g, got)


def _chip_scatter(ps):
    n = len(ps)

    def body(*refs):
        ins, outs, ssem, rsem = refs[:n], refs[n:2 * n], refs[2 * n], refs[2 * n + 1]
        x, y, c = _pos()
        cps = []
        for a in range(n):
            for q, (fx, fy) in enumerate(((1, 0), (0, 1), (1, 1))):
                px, py = _flip(x, fx), _flip(y, fy)
                cp = pltpu.make_async_remote_copy(
                    src_ref=ins[a].at[2 * px + py], dst_ref=outs[a].at[q],
                    send_sem=ssem.at[3 * a + q], recv_sem=rsem.at[3 * a + q], device_id=(px, py, c), device_id_type=MESH)
                cp.start()
                cps.append(cp)
        for cp in cps:
            cp.wait()

    return pl.pallas_call(
        body, name="grad_chip_scatter", in_specs=[ANY] * n, out_specs=[ANY] * n,
        out_shape=[jax.ShapeDtypeStruct((N_CHIP - 1,) + p.shape[1:], p.dtype) for p in ps],
        scratch_shapes=[pltpu.SemaphoreType.DMA((3 * n,)), pltpu.SemaphoreType.DMA((3 * n,))],
        compiler_params=_cp(),
    )(*ps)


def _add_owner(p, got, chipidx, name):
    _, hr, cc = p.shape
    rb = min(hr, 256)

    def body(k_ref, p_ref, r_ref, o_ref):
        o_ref[...] = ((p_ref[...].astype(F32) + r_ref[0].astype(F32)) + r_ref[1].astype(F32)) + r_ref[2].astype(F32)

    return pl.pallas_call(
        body, name=name,
        grid_spec=pltpu.PrefetchScalarGridSpec(
            num_scalar_prefetch=1, grid=(hr // rb,),
            in_specs=[pl.BlockSpec((None, rb, cc), lambda j, k_ref: (k_ref[0], j, 0)),
                      pl.BlockSpec((N_CHIP - 1, rb, cc), lambda j, k_ref: (0, j, 0))],
            out_specs=pl.BlockSpec((rb, cc), lambda j, k_ref: (j, 0))),
        out_shape=jax.ShapeDtypeStruct((hr, cc), F32),
        compiler_params=_cp(("parallel",)),
    )(chipidx, p, got)


def _sib_exchange(ts_):
    n = len(ts_)

    def body(*refs):
        ins, outs, ssem, rsem = refs[:n], refs[n:2 * n], refs[2 * n], refs[2 * n + 1]
        x, y, c = _pos()
        cps = []
        for a in range(n):
            cp = pltpu.make_async_remote_copy(src_ref=ins[a], dst_ref=outs[a], send_sem=ssem.at[a],
                                              recv_sem=rsem.at[a], device_id=(x, y, 1 - c), device_id_type=MESH)
            cp.start()
            cps.append(cp)
        for cp in cps:
            cp.wait()

    return pl.pallas_call(
        body, name="grad_sib_exchange", in_specs=[ANY] * n, out_specs=[ANY] * n,
        out_shape=[jax.ShapeDtypeStruct(t.shape, F32) for t in ts_],
        scratch_shapes=[pltpu.SemaphoreType.DMA((n,))] * 2,
        compiler_params=_cp(),
    )(*ts_)


def _adam_2d(w, g_own, g_sib, m, v, cidx, name):
    rr, cc = w.shape
    hr = rr // 2
    rb = min(hr, 256)
    nb = hr // rb

    def body(c_ref, w_ref, go_ref, gs_ref, m_ref, v_ref, g_out, d_out, m_out, v_out):
        g = jnp.where(pl.program_id(0) == c_ref[0], go_ref[...], gs_ref[...])
        dl, m2, v2 = _adam(w_ref[...], g, m_ref[...], v_ref[...])
        g_out[...] = g
        d_out[...] = dl
        m_out[...] = m2
        v_out[...] = v2

    blk = pl.BlockSpec((rb, cc), lambda h, j, c_ref: (h * nb + j, 0))
    hblk = pl.BlockSpec((rb, cc), lambda h, j, c_ref: (j, 0))
    return pl.pallas_call(
        body, name=name,
        grid_spec=pltpu.PrefetchScalarGridSpec(
            num_scalar_prefetch=1, grid=(2, nb), in_specs=[blk, hblk, hblk, blk, blk], out_specs=[blk] * 4),
        out_shape=[jax.ShapeDtypeStruct((rr, cc), F32)] * 4, compiler_params=_cp(("parallel", "parallel")),
    )(cidx, w, g_own, g_sib, m, v)


def kernel(x, c, norm_g, mod_w, mod_b, hy_w_in, hy_conv_w, hy_conv_b, lru_w_a, lru_b_a, lru_w_x, lru_b_x, lru_lambda, sc_conv_w, hy_w_out, pool_w_in, pool_w_grp, pool_b_grp, pool_scale, pool_w_out, final_g, loss_target, m_norm_g, m_mod_w, m_mod_b, m_hy_w_in, m_hy_conv_w, m_hy_conv_b, m_lru_w_a, m_lru_b_a, m_lru_w_x, m_lru_b_x, m_lru_lambda, m_sc_conv_w, m_hy_w_out, m_pool_w_in, m_pool_w_grp, m_pool_b_grp, m_pool_scale, m_pool_w_out, m_final_g, v_norm_g, v_mod_w, v_mod_b, v_hy_w_in, v_hy_conv_w, v_hy_conv_b, v_lru_w_a, v_lru_b_a, v_lru_w_x, v_lru_b_x, v_lru_lambda, v_sc_conv_w, v_hy_w_out, v_pool_w_in, v_pool_w_grp, v_pool_b_grp, v_pool_scale, v_pool_w_out, v_final_g):
    ax, ay, ac = _pos()
    me = 4 * ax + 2 * ay + ac
    chip = 2 * ax + ay
    xs = x[0]
    tgt = loss_target[0]
    gd = POOL_GROUP_DIM

    ca_all, mod_all, small_w = _mod_fwd(jnp.broadcast_to(c, (SUBLANES, D)), mod_w, mod_b,
                                        hy_conv_w[0], sc_conv_w[0], pool_b_grp, pool_scale)
    mod_me = lax.dynamic_index_in_dim(mod_all, me, axis=1, keepdims=False)
    sh0, sc0, gt0 = (mod_me[0:1, k * D:(k + 1) * D] for k in range(3))
    sh1, sc1, gt1 = (mod_me[1:2, k * D:(k + 1) * D] for k in range(3))
    cw = small_w[SW_CONV:SW_CONV + 4, 0:D]
    sw = small_w[SW_SC:SW_SC + 3, 0:D]
    pool_b = small_w[SW_POOL_B:SW_POOL_B + 1, :]
    pool_s = small_w[SW_POOL_S:SW_POOL_S + 1, :]
    g0, g1, gf = norm_g[0:1], norm_g[1:2], final_g.reshape(1, D)
    cb, ba, bx, lam = hy_conv_b, lru_b_a, lru_b_x, lru_lambda

    big = [hy_w_in[0], hy_w_out[0], pool_w_in[0], pool_w_grp[0].reshape(4 * 128, gd), pool_w_out[0]]
    cidx = ac.reshape(1).astype(jnp.int32)
    kidx = chip.reshape(1).astype(jnp.int32)
    w_in0, w_out0, w_in1, w_grp, w_out1 = _wgather(
        [_wcast_own_block(w, kidx, f"wcast_own_block_{a}") for a, w in enumerate(big)])
    w_grp =w_grp.reshape(N_CHIP, 4, 128, gd).transpose(1, 0, 2, 3).reshape(4, gd, gd)
    wa_b, wx_b = _wcast([lru_w_a[0], lru_w_x[0]])

    h0, proj0 = _norm_proj(xs, g0, sc0, sh0, w_in0, "l0_proj")
    x1, hst, y0 = _l0_mix(proj0, xs, gt0, cw, cb, wa_b, ba, wx_b, bx, lam, sw, w_out0.reshape(2 * D, D))
    h1, proj1 = _norm_proj(x1, g1, sc1, sh1, w_in1, "l1_proj")
    dpool, mixed, y1, dx2, losscols, dgf = _l1_mix(proj1, x1, tgt, gt1, w_grp, pool_b, pool_s,
                                                    w_out1.reshape(2 * D, D), gf)

    dproj1, dmixed, dsc1, dbg1 = _l1_bwd_mix(dx2, proj1, mixed, gt1, w_grp, pool_s, w_out1.reshape(2 * D, D))
    mt1 = _wgrad_rows(y1, dx2, N_CHIP, "l1_wgrad_out")
    d_wgrp = _wgrad(dpool, dmixed, 4, gd, gd, lambda g: g, lambda g: g, "l1_wgrad_grp")
    d_win1 = _wgrad(h1, dproj1, N_CHIP, D, D, lambda g: 0, lambda g: g, "l1_wgrad_in")
    dx1, s1_1, s2_1 = _dgrad_norm(dproj1, w_in1, x1, dx2, g1, sc1, "l1_bwd_proj")
    d_wout1, dgate1 = _wo_final(mt1, w_out1, gt1, "l1_wo_final")

    dproj0, xc, dpa, dpx, sm0 = _l0_bwd_mix(dx1, proj0, hst, gt0, cw, cb, wa_b, ba, wx_b, bx, lam, sw,
                                            w_out0.reshape(2 * D, D))
    mt0 = _wgrad_rows(y0, dx1, N_CHIP, "l0_wgrad_out")
    d_win0 = _wgrad(h0, dproj0, N_CHIP, D, 6 * D // N_CHIP, lambda g: 0, lambda g: g, "l0_wgrad_in")
    d_wa, d_wx = _wgrad_heads(xc, dpa, dpx)
    grad_x, s1_0, s2_0 = _dgrad_norm(dproj0, w_in0, xs, dx1, g0, sc0, "l0_bwd_proj")
    d_wout0, dgate0 = _wo_final(mt0, w_out0, gt0, "l0_wo_final")

    buf_a, dmod8, loss8 = _small_pack(s1_0, s2_0, s1_1, s2_1, sm0, dsc1, dbg1, dgf, losscols, dgate0, dgate1,
                                      norm_g, sc0, sc1, lam)
    hw = LRU_HEADS * LRU_HEAD_DIM
    buf_b = jnp.concatenate([d_wa.reshape(hw, LRU_HEAD_DIM), d_wx.reshape(hw, LRU_HEAD_DIM)], axis=0)
    red_a, red_b, dm_all = _small_comm(buf_a, buf_b, dmod8)
    small = [(norm_g, m_norm_g, v_norm_g), (mod_b, m_mod_b, v_mod_b),
             (hy_conv_w[0], m_hy_conv_w[0], v_hy_conv_w[0]), (hy_conv_b, m_hy_conv_b, v_hy_conv_b),
             tuple(a.reshape(hw, LRU_HEAD_DIM) for a in (lru_w_a, m_lru_w_a, v_lru_w_a)),
             (lru_b_a, m_lru_b_a, v_lru_b_a),
             tuple(a.reshape(hw, LRU_HEAD_DIM) for a in (lru_w_x, m_lru_w_x, v_lru_w_x)),
             (lru_b_x, m_lru_b_x, v_lru_b_x), (lru_lambda, m_lru_lambda, v_lru_lambda),
             (sc_conv_w[0], m_sc_conv_w[0], v_sc_conv_w[0]), (pool_b_grp, m_pool_b_grp, v_pool_b_grp),
             (pool_scale, m_pool_scale, v_pool_scale),
             tuple(a.reshape(1, D) for a in (final_g, m_final_g, v_final_g))]
    small_names = ["norm_g", "mod_b", "hy_conv_w", "hy_conv_b", "lru_w_a", "lru_b_a", "lru_w_x", "lru_b_x",
                   "lru_lambda", "sc_conv_w", "pool_b_grp", "pool_scale", "final_g"]
    small_out = _small_adam(red_a, red_b, dm_all, small)
    res = {}
    shapes = dict(norm_g=norm_g, mod_b=mod_b, hy_conv_w=hy_conv_w, hy_conv_b=hy_conv_b, lru_w_a=lru_w_a, lru_b_a=lru_b_a,
                  lru_w_x=lru_w_x, lru_b_x=lru_b_x, lru_lambda=lru_lambda, sc_conv_w=sc_conv_w, pool_b_grp=pool_b_grp,
                  pool_scale=pool_scale, final_g=final_g)
    for p, nm in enumerate(small_names):
        res[nm] = tuple(o.reshape(shapes[nm].shape) for o in small_out[4 * p:4 * p + 4])

    nw = mod_w.shape[2]
    dm_sh = jnp.stack([lax.dynamic_slice_in_dim(dm_all[:, l * 3 * D:(l + 1) * 3 * D], chip * nw, nw, axis=1)
                       for l in range(2)])
    res["mod_w"] = tuple(_modw_adam(ca_all.T, dm_sh, mod_w, m_mod_w, v_mod_w))

    d_wgrp = d_wgrp.reshape(4, N_CHIP, 128, gd).transpose(1, 0, 2, 3).reshape(N_CHIP, 4 * 128, gd)
    grads = [d_win0, d_wout0, d_win1, d_wgrp, d_wout1]
    got = _sib_send_halves(grads)
    parts = [_add_half(g, r, cidx, f"grad_add_half_{a}") for a, (g, r) in enumerate(zip(grads, got))]
    got2 = _chip_scatter(parts)
    halves = [_add_owner(p, r, kidx, f"grad_add_owner_{a}") for a, (p, r) in enumerate(zip(parts, got2))]
    sib_halves = _sib_exchange(halves)
    big_names = ["hy_w_in", "hy_w_out", "pool_w_in", "pool_w_grp", "pool_w_out"]
    big_wmv = [(hy_w_in, m_hy_w_in, v_hy_w_in), (hy_w_out, m_hy_w_out, v_hy_w_out), (pool_w_in, m_pool_w_in, v_pool_w_in),
               (pool_w_grp, m_pool_w_grp, v_pool_w_grp), (pool_w_out, m_pool_w_out, v_pool_w_out)]
    for a, nm in enumerate(big_names):
        rr, cc = big[a].shape
        w, m, v = (t.reshape(rr, cc) for t in big_wmv[a])
        outs = _adam_2d(w, halves[a], sib_halves[a], m, v, cidx, f"adam_{nm}")
        res[nm] = tuple(o.reshape(big_wmv[a][0].shape) for o in outs)

    loss = lax.psum(loss8[0, 0], ("x", "y", "c"))
    order = ["norm_g", "mod_w", "mod_b", "hy_w_in", "hy_conv_w", "hy_conv_b", "lru_w_a", "lru_b_a", "lru_w_x", "lru_b_x",
             "lru_lambda", "sc_conv_w", "hy_w_out", "pool_w_in", "pool_w_grp", "pool_b_grp", "pool_scale", "pool_w_out",
             "final_g"]
    return (loss, grad_x[None], *[res[nm][0] for nm in order], *[res[nm][1] for nm in order],
            *[res[nm][2] for nm in order], *[res[nm][3] for nm in order])
```

```python
import jax
import jax.numpy as jnp
from jax import lax
from jax.experimental import pallas as pl
from jax.experimental.pallas import tpu as pltpu

F32, BF16 = jnp.float32, jnp.bfloat16
D = 1024
RMS_EPS = 1e-6
LRU_C = 8.0
LRU_HEADS, LRU_HEAD_DIM = 8, 128
POOL_WINDOWS = (2, 4, 8, 16)
POOL_GROUP_DIM = 512
ADAM_LR, ADAM_B1, ADAM_B2, ADAM_EPS, ADAM_WD, ADAM_STEP = 0.001, 0.9, 0.999, 1e-08, 0.01, 10
MESH = pl.DeviceIdType.MESH
N_DEV, N_CHIP = 8, 4
SUBLANES = 8
BF16_ROWS = 16
POOL_HALO = 16
TS_PROJ, TS_MIX, TS_WGRAD, TS_DGRAD = 1024, 256, 1024, 256
SMALL_ROWS = 64
GRAD_WIRE_DTYPE = BF16
ANY = pl.BlockSpec(memory_space=pl.ANY)
VMEM = pl.BlockSpec(memory_space=pltpu.VMEM)
NT = (((1,), (1,)), ((), ()))
TN = (((0,), (0,)), ((), ()))


def _cp(sem=None, vmem_mb=56):
    kw = dict(vmem_limit_bytes=vmem_mb * 2 ** 20)
    if sem is not None:
        kw["dimension_semantics"] = sem
    return pltpu.CompilerParams(**kw)


def _tile(n, t):
    return min(n, t)


def _pos():
    return lax.axis_index("x"), lax.axis_index("y"), lax.axis_index("c")


def _flip(v, f):
    return 1 - v if f else v


def _sigmoid(z):
    return 0.5 * jnp.tanh(0.5 * z) + 0.5


def _rows(n, c):
    return lax.broadcasted_iota(jnp.int32, (n, c), 0)


def _down(a, d):
    return a if d == 0 else pltpu.roll(a, d, 0)


def _up(a, d):
    return a if d == 0 else pltpu.roll(a, a.shape[0] - d, 0)


def _scan_fwd(a, u, carry):
    n, c = a.shape
    sub = _rows(SUBLANES, c)
    out = []
    for k in range(n // SUBLANES):
        p = a[k * SUBLANES:(k + 1) * SUBLANES]
        g = u[k * SUBLANES:(k + 1) * SUBLANES]
        for d in (1, 2, 4):
            keep = sub >= d
            g = g + p * jnp.where(keep, pltpu.roll(g, d, 0), 0.0)
            p = p * jnp.where(keep, pltpu.roll(p, d, 0), 1.0)
        h = g + p * carry
        carry = h[SUBLANES - 1:SUBLANES, :]
        out.append(h)
    return jnp.concatenate(out, axis=0)


def _scan_rev(alpha, b, carry):
    n, c = alpha.shape
    sub = _rows(SUBLANES, c)
    out = []
    for k in reversed(range(n // SUBLANES)):
        p = alpha[k * SUBLANES:(k + 1) * SUBLANES]
        g = b[k * SUBLANES:(k + 1) * SUBLANES]
        for d in (1, 2, 4):
            keep = sub < SUBLANES - d
            g = g + p * jnp.where(keep, pltpu.roll(g, SUBLANES - d, 0), 0.0)
            p = p * jnp.where(keep, pltpu.roll(p, SUBLANES - d, 0), 1.0)
        h = g + p * carry
        carry = h[0:1, :]
        out.append(h)
    return jnp.concatenate(out[::-1], axis=0)


def _conv_taps(ext, halo, n, width):
    return [_down(ext, width - 1 - k)[halo:halo + n] for k in range(width)]


def _lru_gates(xc, wa_ref, ba, wx_ref, bx):
    xb = xc.astype(BF16)
    pa, px = [], []
    for h in range(LRU_HEADS):
        xh = xb[:, h * LRU_HEAD_DIM:(h + 1) * LRU_HEAD_DIM]
        pa.append(jnp.dot(xh, wa_ref[h], preferred_element_type=F32))
        px.append(jnp.dot(xh, wx_ref[h], preferred_element_type=F32))
    r = _sigmoid(jnp.concatenate(pa, axis=1) + ba)
    ig = _sigmoid(jnp.concatenate(px, axis=1) + bx)
    return r, ig


def _softplus_neg(lam):
    return jnp.maximum(-lam, 0.0) + jnp.log1p(jnp.exp(-jnp.abs(lam)))


def _lru_decay(r, sp, first):
    big_l = (-LRU_C) * r * sp
    a = jnp.exp(big_l)
    th = jnp.tanh(big_l)
    m = jnp.sqrt(-2.0 * th / (1.0 - th))
    return a, jnp.where(first, 1.0, m)


def _pool_inv_counts(t0, n):
    t = (t0 + lax.broadcasted_iota(jnp.int32, (n, 1), 0) + 1).astype(F32)
    return [1.0 / jnp.minimum(t, float(w)) for w in POOL_WINDOWS]


def _window_sums(ext, shift):
    gd = POOL_GROUP_DIM
    out = []
    s = ext
    for k in range(len(POOL_WINDOWS)):
        s = s + shift(s, 2 ** k)
        out.append(s[:, 0:gd])
        if k + 1 < len(POOL_WINDOWS):
            s = s[:, gd:]
    return out


SW_ROWS, SW_COLS = 16, 2 * D
SW_CONV, SW_SC, SW_POOL_B, SW_POOL_S = 0, 4, 8, 9


def _mod_fwd(c8, mod_w, mod_b, conv_w, sc_w, pool_b, pool_s):
    nw = mod_w.shape[2]
    cq, pq = conv_w.shape[1], pool_b.shape[1]

    def body(c_ref, w_ref, b_ref, cw_ref, sw_ref, pb_ref, ps_ref, ca_ref, mod_ref, small_ref,
             cslot, mslot, msend, pslot, psend, s1, r1, s2, r2, s3, r3):
        x, y, c = _pos()
        me = 4 * x + 2 * y + c
        chip = 2 * x + y
        first = []
        for r in range(1, N_DEV):
            fx, fy, fc = (r >> 2) & 1, (r >> 1) & 1, r & 1
            cp = pltpu.make_async_remote_copy(
                src_ref=c_ref, dst_ref=cslot.at[me], send_sem=s1.at[r - 1], recv_sem=r1.at[r - 1],
                device_id=(_flip(x, fx), _flip(y, fy), _flip(c, fc)), device_id_type=MESH)
            cp.start()
            first.append(cp)
        cslot[me] = c_ref[...]
        for cp in first:
            cp.wait()
        rows = _rows(SUBLANES, D)
        call = jnp.zeros((SUBLANES, D), F32)
        for d in range(N_DEV):
            call = jnp.where(rows == d, cslot[d], call)
        ca = call * _sigmoid(call)
        ca_ref[...] = ca
        for l in range(2):
            msend[l] = jnp.dot(ca, w_ref[l], precision=lax.Precision.HIGHEST, preferred_element_type=F32)
        psend[...] = jnp.zeros_like(psend)
        psend[SW_CONV:SW_CONV + 4, 0:cq] = cw_ref[...]
        psend[SW_SC:SW_SC + 3, 0:cq] = sw_ref[...]
        psend[SW_POOL_B:SW_POOL_B + 1, :] = pb_ref[...]
        psend[SW_POOL_S:SW_POOL_S + 1, :] = ps_ref[...]
        second = []
        for q, (fx, fy) in enumerate(((1, 0), (0, 1), (1, 1))):
            peer = (_flip(x, fx), _flip(y, fy), c)
            for src, dst, ss, rs in ((msend, mslot, s2, r2), (psend, pslot, s3, r3)):
                cp = pltpu.make_async_remote_copy(src_ref=src, dst_ref=dst.at[chip], send_sem=ss.at[q], recv_sem=rs.at[q],
                                                  device_id=peer, device_id_type=MESH)
                cp.start()
                second.append(cp)
        mslot[chip] = msend[...]
        pslot[chip] = psend[...]
        for cp in second:
            cp.wait()
        small_ref[...] = jnp.zeros_like(small_ref)
        for j in range(N_CHIP):
            for l in range(2):
                mod_ref[l, :, j * nw:(j + 1) * nw] = mslot[j, l] + b_ref[l:l + 1, j * nw:(j + 1) * nw]
            small_ref[0:SUBLANES, j * cq:(j + 1) * cq] = pslot[j, 0:SUBLANES, 0:cq]
            small_ref[SUBLANES:SW_ROWS, j * pq:(j + 1) * pq] = pslot[j, SUBLANES:SW_ROWS, :]

    args = (c8, mod_w, mod_b, conv_w, sc_w, pool_b, pool_s)
    dma3 = pltpu.SemaphoreType.DMA((N_CHIP - 1,))
    return pl.pallas_call(
        body, name="mod_fwd",
        in_specs=[VMEM] * len(args), out_specs=[VMEM] * 3,
        out_shape=[jax.ShapeDtypeStruct((SUBLANES, D), F32), jax.ShapeDtypeStruct((2, SUBLANES, N_CHIP * nw), F32),
                   jax.ShapeDtypeStruct((SW_ROWS, SW_COLS), F32)],
        scratch_shapes=[pltpu.VMEM((N_DEV, SUBLANES, D), F32), pltpu.VMEM((N_CHIP, 2, SUBLANES, nw), F32),
                        pltpu.VMEM((2, SUBLANES, nw), F32), pltpu.VMEM((N_CHIP, SW_ROWS, pq), F32),
                        pltpu.VMEM((SW_ROWS, pq), F32),
                        pltpu.SemaphoreType.DMA((N_DEV - 1,)), pltpu.SemaphoreType.DMA((N_DEV - 1,)),
                        dma3, dma3, dma3, dma3],
        compiler_params=_cp(),
    )(*args)


def _wcast(ws):
    def body(*refs):
        n = len(refs) // 2
        for a in range(n):
            refs[n + a][...] = refs[a][...].astype(BF16)

    return pl.pallas_call(
        body, name="wcast", in_specs=[VMEM] * len(ws), out_specs=[VMEM] * len(ws),
        out_shape=[jax.ShapeDtypeStruct(w.shape, BF16) for w in ws], compiler_params=_cp(),
    )(*ws)


def _wcast_own_block(w, kidx, name):
    rr, cc = w.shape
    rb = min(rr, 256)

    def body(k_ref, w_ref, o_ref):
        o_ref[...] = w_ref[...].astype(BF16)

    return pl.pallas_call(
        body, name=name,
        grid_spec=pltpu.PrefetchScalarGridSpec(
            num_scalar_prefetch=1, grid=(rr // rb,),
            in_specs=[pl.BlockSpec((rb, cc), lambda j, k_ref: (j, 0))],
            out_specs=pl.BlockSpec((None, rb, cc), lambda j, k_ref: (k_ref[0], j, 0))),
        out_shape=jax.ShapeDtypeStruct((N_CHIP, rr, cc), BF16),
        compiler_params=_cp(("parallel",)),
    )(kidx, w)


def _wgather(bufs):
    n = len(bufs)

    def body(*refs):
        outs = refs[n:2 * n]
        ssem, rsem, fssem, frsem = refs[2 * n:]
        x, y, c = _pos()
        chip = 2 * x + y
        sib = (x, y, 1 - c)
        flips = ((1, 0), (0, 1), (1, 1))

        def half(a, which):
            hr = bufs[a].shape[1] // 2
            return pl.ds(pl.multiple_of(which * hr, BF16_ROWS), hr)

        sends = []
        for a in range(n):
            mine = outs[a].at[chip, half(a, c), :]
            for q, (fx, fy) in enumerate(flips):
                cp = pltpu.make_async_remote_copy(
                    src_ref=mine, dst_ref=mine, send_sem=ssem.at[3 * a + q], recv_sem=rsem.at[3 * a + q],
                    device_id=(_flip(x, fx), _flip(y, fy), c), device_id_type=MESH)
                cp.start()
                sends.append(cp)
        passed = []
        for a in range(n):
            for q, (fx, fy) in enumerate(flips):
                src_chip = 2 * _flip(x, fx) + _flip(y, fy)
                landed = outs[a].at[src_chip, half(a, c), :]
                pltpu.make_async_remote_copy(
                    src_ref=landed, dst_ref=landed, send_sem=ssem.at[3 * a + q], recv_sem=rsem.at[3 * a + q],
                    device_id=sib, device_id_type=MESH).wait_recv()
                cp = pltpu.make_async_remote_copy(
                    src_ref=landed, dst_ref=landed, send_sem=fssem.at[3 * a + q], recv_sem=frsem.at[3 * a + q],
                    device_id=sib, device_id_type=MESH)
                cp.start()
                passed.append(cp)
        for a in range(n):
            for q, (fx, fy) in enumerate(flips):
                src_chip = 2 * _flip(x, fx) + _flip(y, fy)
                other = outs[a].at[src_chip, half(a, 1 - c), :]
                pltpu.make_async_remote_copy(
                    src_ref=other, dst_ref=other, send_sem=fssem.at[3 * a + q], recv_sem=frsem.at[3 * a + q],
                    device_id=sib, device_id_type=MESH).wait_recv()
        for cp in sends + passed:
            cp.wait_send()

    return pl.pallas_call(
        body, name="wgather", in_specs=[ANY] * n, out_specs=[ANY] * n,
        out_shape=[jax.ShapeDtypeStruct(b.shape, BF16) for b in bufs],
        input_output_aliases={a: a for a in range(n)},
        scratch_shapes=[pltpu.SemaphoreType.DMA((3 * n,))] * 4,
        compiler_params=_cp(),
    )(*bufs)


def _norm_proj(x, g, sc, sh, w, name):
    s_len, nb = x.shape[0], w.shape[2]
    ts = _tile(s_len, TS_PROJ)

    def body(x_ref, g_ref, sc_ref, sh_ref, w_ref, h_ref, p_ref):
        @pl.when(pl.program_id(1) == 0)
        def _():
            xv = x_ref[...]
            r = lax.rsqrt(jnp.mean(xv * xv, axis=-1, keepdims=True) + RMS_EPS)
            h_ref[...] = (xv * r * (g_ref[...] * (1.0 + sc_ref[...])) + sh_ref[...]).astype(BF16)

        p_ref[...] = jnp.dot(h_ref[...], w_ref[...], preferred_element_type=F32).astype(BF16)

    vec = pl.BlockSpec((1, D), lambda i, j: (0, 0))
    return pl.pallas_call(
        body, name=name, grid=(s_len // ts, N_CHIP),
        in_specs=[pl.BlockSpec((ts, D), lambda i, j: (i, 0)), vec, vec, vec,
                  pl.BlockSpec((None, D, nb), lambda i, j: (j, 0, 0))],
        out_specs=[pl.BlockSpec((ts, D), lambda i, j: (i, 0)), pl.BlockSpec((ts, nb), lambda i, j: (i, j))],
        out_shape=[jax.ShapeDtypeStruct((s_len, D), BF16), jax.ShapeDtypeStruct((s_len, N_CHIP * nb), BF16)],
        compiler_params=_cp(("parallel", "arbitrary")),
    )(x, g, sc, sh, w)


def _l0_mix(proj, x, gate, cw, cb, wa, ba, wx, bx, lam, sw, wo):
    s_len = x.shape[0]
    ts = _tile(s_len, TS_MIX)
    hl = SUBLANES

    def body(p_ref, x_ref, gate_ref, cw_ref, cb_ref, wa_ref, ba_ref, wx_ref, bx_ref, lam_ref, sw_ref, wo_ref,
             x1_ref, h_ref, y_ref, cxa, czz, chh):
        i = pl.program_id(0)

        @pl.when(i == 0)
        def _():
            cxa[...] = jnp.zeros_like(cxa)
            czz[...] = jnp.zeros_like(czz)
            chh[...] = jnp.zeros_like(chh)

        xa, ga, gbp, gcp, v, gb = [p_ref[:, k * D:(k + 1) * D].astype(F32) for k in range(6)]
        rows = _rows(ts, D)
        taps = _conv_taps(jnp.concatenate([cxa[...], xa], axis=0), hl, ts, 4)
        xc = cb_ref[...] + sum(cw_ref[k:k + 1, :] * taps[k] for k in range(4))
        r, ig = _lru_gates(xc, wa_ref, ba_ref[...], wx_ref, bx_ref[...])
        a, m = _lru_decay(r, _softplus_neg(lam_ref[...]), (rows == 0) & (i == 0))
        h = _scan_fwd(a, m * ig * xc, chh[hl - 1:hl, :])
        z = gcp * v
        ztaps = _conv_taps(jnp.concatenate([czz[...], z], axis=0), hl, ts, 3)
        yb = gbp * sum(sw_ref[k:k + 1, :] * ztaps[k] for k in range(3))
        y = jnp.concatenate([h * (ga * _sigmoid(ga)), yb * (gb * _sigmoid(gb))], axis=1).astype(BF16)
        y_ref[...] = y
        x1_ref[...] = x_ref[...] + gate_ref[...] * jnp.dot(y, wo_ref[...], preferred_element_type=F32)
        h_ref[...] = h.astype(BF16)
        cxa[...] = xa[ts - hl:, :]
        czz[...] = z[ts - hl:, :]
        chh[...] = h[ts - hl:, :]

    def full(a):
        return pl.BlockSpec(a.shape, lambda i: (0,) * a.ndim)

    row = lambda w: pl.BlockSpec((ts, w), lambda i: (i, 0))
    return pl.pallas_call(
        body, name="l0_mix", grid=(s_len // ts,),
        in_specs=[row(6 * D), row(D)] + [full(a) for a in (gate, cw, cb, wa, ba, wx, bx, lam, sw, wo)],
        out_specs=[row(D), row(D), row(2 * D)],
        out_shape=[jax.ShapeDtypeStruct((s_len, D), F32), jax.ShapeDtypeStruct((s_len, D), BF16),
                   jax.ShapeDtypeStruct((s_len, 2 * D), BF16)],
        scratch_shapes=[pltpu.VMEM((hl, D), F32)] * 3,
        compiler_params=_cp(("arbitrary",)),
    )(proj, x, gate, cw, cb, wa, ba, wx, bx, lam, sw, wo)


def _l1_mix(proj, x1, tgt, gate, wg, bg, scale, wo, gf):
    s_len = x1.shape[0]
    ts = _tile(s_len, TS_MIX)
    pw, gd, hl = 2 * D, POOL_GROUP_DIM, POOL_HALO

    def body(p_ref, x_ref, t_ref, gate_ref, wg_ref, bg_ref, sc_ref, wo_ref, gf_ref,
             d_ref, mx_ref, y_ref, dx_ref, loss_ref, dgf_ref, cv):
        i = pl.program_id(0)

        @pl.when(i == 0)
        def _():
            cv[...] = jnp.zeros_like(cv)
            loss_ref[...] = jnp.zeros_like(loss_ref)
            dgf_ref[...] = jnp.zeros_like(dgf_ref)

        v = p_ref[:, 0:pw].astype(F32)
        gg = p_ref[:, pw:2 * pw].astype(F32)
        sums = _window_sums(jnp.concatenate([cv[...], v], axis=0), _down)
        inv = _pool_inv_counts(i * ts, ts)
        dd = [sums[k][hl:hl + ts] * inv[k] - v[:, k * gd:(k + 1) * gd] for k in range(4)]
        mixed = jnp.concatenate(
            [jnp.dot(dd[k].astype(BF16), wg_ref[k], preferred_element_type=F32) for k in range(4)], axis=1) + bg_ref[...]
        d_ref[...] = jnp.concatenate(dd, axis=1).astype(BF16)
        mx_ref[...] = mixed.astype(BF16)
        y = (mixed * sc_ref[...] * (gg * _sigmoid(gg))).astype(BF16)
        y_ref[...] = y
        x2 = x_ref[...] + gate_ref[...] * jnp.dot(y, wo_ref[...], preferred_element_type=F32)
        r2 = lax.rsqrt(jnp.mean(x2 * x2, axis=-1, keepdims=True) + RMS_EPS)
        n2 = x2 * r2
        err = n2 * gf_ref[...] - t_ref[...]
        loss_ref[...] += jnp.sum(err * err, axis=0, keepdims=True)
        dyf = err * (1.0 / D)
        dgf_ref[...] += jnp.sum(dyf * n2, axis=0, keepdims=True)
        dn = dyf * gf_ref[...]
        dx_ref[...] = r2 * (dn - n2 * jnp.mean(dn * n2, axis=-1, keepdims=True))
        cv[...] = v[ts - hl:, :]

    def full(a):
        return pl.BlockSpec(a.shape, lambda i: (0,) * a.ndim)

    row = lambda w: pl.BlockSpec((ts, w), lambda i: (i, 0))
    acc = pl.BlockSpec((1, D), lambda i: (0, 0))
    return pl.pallas_call(
        body, name="l1_mix", grid=(s_len // ts,),
        in_specs=[row(2 * pw), row(D), row(D)] + [full(a) for a in (gate, wg, bg, scale, wo, gf)],
        out_specs=[row(pw), row(pw), row(pw), row(D), acc, acc],
        out_shape=[jax.ShapeDtypeStruct((s_len, pw), BF16)] * 3 + [jax.ShapeDtypeStruct((s_len, D), F32)]
        + [jax.ShapeDtypeStruct((1, D), F32)] * 2,
        scratch_shapes=[pltpu.VMEM((hl, pw), F32)],
        compiler_params=_cp(("arbitrary",)),
    )(proj, x1, tgt, gate, wg, bg, scale, wo, gf)


def _l1_bwd_mix(dx2, proj, mixed, gate, wg, scale, wo):
    s_len = dx2.shape[0]
    ts = _tile(s_len, TS_MIX)
    n_t = s_len // ts
    pw, gd, hl = 2 * D, POOL_GROUP_DIM, POOL_HALO

    def body(dx_ref, gg_ref, mx_ref, gate_ref, wg_ref, sc_ref, wo_ref, dp_ref, dmx_ref, dsc_ref, dbg_ref, cq):
        i = pl.program_id(0)

        @pl.when(i == 0)
        def _():
            cq[...] = jnp.zeros_like(cq)
            dsc_ref[...] = jnp.zeros_like(dsc_ref)
            dbg_ref[...] = jnp.zeros_like(dbg_ref)

        dy = lax.dot_general((gate_ref[...] * dx_ref[...]).astype(BF16), wo_ref[...], NT, preferred_element_type=F32)
        gg = gg_ref[...].astype(F32)
        mixed = mx_ref[...].astype(F32)
        s = _sigmoid(gg)
        sg = gg * s
        dmixed = dy * sc_ref[...] * sg
        dsc_ref[...] += jnp.sum(dy * mixed * sg, axis=0, keepdims=True)
        dbg_ref[...] += jnp.sum(dmixed, axis=0, keepdims=True)
        dmb = dmixed.astype(BF16)
        dmx_ref[...] = dmb
        dp_ref[:, pw:2 * pw] = (dy * sc_ref[...] * mixed * (s * (1.0 + gg * (1.0 - s)))).astype(BF16)
        inv = _pool_inv_counts((n_t - 1 - i) * ts, ts)
        dd = [lax.dot_general(dmb[:, k * gd:(k + 1) * gd], wg_ref[k], NT, preferred_element_type=F32) for k in range(4)]
        q = jnp.concatenate([dd[k] * inv[k] for k in range(4)], axis=1)
        sums = _window_sums(jnp.concatenate([q, cq[...]], axis=0), _up)
        dp_ref[:, 0:pw] = jnp.concatenate([sums[k][0:ts] - dd[k] for k in range(4)], axis=1).astype(BF16)
        cq[...] = q[0:hl, :]

    def full(a):
        return pl.BlockSpec(a.shape, lambda i: (0,) * a.ndim)

    rev = lambda w, j=0: pl.BlockSpec((ts, w), lambda i: (n_t - 1 - i, j))
    acc = pl.BlockSpec((1, pw), lambda i: (0, 0))
    return pl.pallas_call(
        body, name="l1_bwd_mix", grid=(n_t,),
        in_specs=[rev(D), rev(pw, 1), rev(pw)] + [full(a) for a in (gate, wg, scale, wo)],
        out_specs=[rev(2 * pw), rev(pw), acc, acc],
        out_shape=[jax.ShapeDtypeStruct((s_len, 2 * pw), BF16), jax.ShapeDtypeStruct((s_len, pw), BF16),
                   jax.ShapeDtypeStruct((1, pw), F32), jax.ShapeDtypeStruct((1, pw), F32)],
        scratch_shapes=[pltpu.VMEM((hl, pw), F32)],
        compiler_params=_cp(("arbitrary",)),
    )(dx2, proj, mixed, gate, wg, scale, wo)


def _l0_bwd_mix(dx1, proj, hst, gate, cw, cb, wa, ba, wx, bx, lam, sw, wo):
    s_len = dx1.shape[0]
    ts = _tile(s_len, TS_MIX)
    n_t = s_len // ts
    hl, hb = SUBLANES, BF16_ROWS

    def body(dx_ref, p_ref, ph_ref, h_ref, hh_ref, gate_ref, cw_ref, cb_ref, wa_ref, ba_ref, wx_ref, bx_ref,
             lam_ref, sw_ref, wo_ref, dp_ref, xc_ref, dpa_ref, dpx_ref, sm_ref, cg, cdxc, cdcz, ca):
        i = pl.program_id(0)
        ri = n_t - 1 - i

        @pl.when(i == 0)
        def _():
            cg[...] = jnp.zeros_like(cg)
            ca[...] = jnp.zeros_like(ca)
            cdxc[...] = jnp.zeros_like(cdxc)
            cdcz[...] = jnp.zeros_like(cdcz)
            sm_ref[...] = jnp.zeros_like(sm_ref)

        has_prev = (ri > 0).astype(F32)
        xa, ga, gbp, gcp, v, gb = [p_ref[:, k * D:(k + 1) * D].astype(F32) for k in range(6)]
        prev = lambda k: ph_ref[:, k * D:(k + 1) * D].astype(F32)[hb - hl:hb] * has_prev
        rows = _rows(ts, D)
        first = (rows == 0) & (ri == 0)
        xtaps = _conv_taps(jnp.concatenate([prev(0), xa], axis=0), hl, ts, 4)
        xc = cb_ref[...] + sum(cw_ref[k:k + 1, :] * xtaps[k] for k in range(4))
        r, ig = _lru_gates(xc, wa_ref, ba_ref[...], wx_ref, bx_ref[...])
        sp = _softplus_neg(lam_ref[...])
        a, m = _lru_decay(r, sp, first)
        z = gcp * v
        ztaps = _conv_taps(jnp.concatenate([prev(3) * prev(4), z], axis=0), hl, ts, 3)
        cz = sum(sw_ref[k:k + 1, :] * ztaps[k] for k in range(3))
        h = h_ref[...].astype(F32)
        hprev = _down(jnp.concatenate([hh_ref[...].astype(F32)[hb - hl:hb] * has_prev, h], axis=0), 1)[hl:hl + ts]
        dy = lax.dot_general((gate_ref[...] * dx_ref[...]).astype(BF16), wo_ref[...], NT, preferred_element_type=F32)
        dya_pre, dyb_pre = dy[:, 0:D], dy[:, D:2 * D]
        s_a, s_b = _sigmoid(ga), _sigmoid(gb)
        dp_ref[:, D:2 * D] = (dya_pre * h * (s_a * (1.0 + ga * (1.0 - s_a)))).astype(BF16)
        dp_ref[:, 5 * D:6 * D] = (dyb_pre * (gbp * cz) * (s_b * (1.0 + gb * (1.0 - s_b)))).astype(BF16)
        dya = dya_pre * (ga * s_a)
        dyb = dyb_pre * (gb * s_b)
        dp_ref[:, 2 * D:3 * D] = (dyb * cz).astype(BF16)
        dcz = dyb * gbp
        for k in range(3):
            sm_ref[8 + k:9 + k, :] += jnp.sum(dcz * ztaps[k], axis=0, keepdims=True)
        dcz_ext = jnp.concatenate([dcz, cdcz[...]], axis=0)
        dz = sum(sw_ref[k:k + 1, :] * _up(dcz_ext, 2 - k)[0:ts] for k in range(3))
        dp_ref[:, 3 * D:4 * D] = (dz * v).astype(BF16)
        dp_ref[:, 4 * D:5 * D] = (dz * gcp).astype(BF16)
        cdcz[...] = dcz[0:hl, :]
        alpha = _up(jnp.concatenate([a, ca[...]], axis=0), 1)[0:ts]
        dh = _scan_rev(alpha, dya, cg[0:1, :])
        cg[...] = dh[0:hl, :]
        ca[...] = a[0:hl, :]
        da = dh * hprev
        dm = dh * ig * xc
        di = dh * m * xc
        dxc = dh * m * ig
        dl = da * a - jnp.where(first, 0.0, dm * (a * a) / m)
        sm_ref[7:8, :] += jnp.sum(dl * r, axis=0, keepdims=True) * (-LRU_C)
        dpa = (dl * sp) * (-LRU_C) * r * (1.0 - r)
        dpx = di * ig * (1.0 - ig)
        sm_ref[5:6, :] += jnp.sum(dpa, axis=0, keepdims=True)
        sm_ref[6:7, :] += jnp.sum(dpx, axis=0, keepdims=True)
        dpa_b, dpx_b = dpa.astype(BF16), dpx.astype(BF16)
        dpa_ref[...] = dpa_b
        dpx_ref[...] = dpx_b
        xc_ref[...] = xc.astype(BF16)
        back = []
        for hd in range(LRU_HEADS):
            sl = slice(hd * LRU_HEAD_DIM, (hd + 1) * LRU_HEAD_DIM)
            back.append(lax.dot_general(dpa_b[:, sl], wa_ref[hd], NT, preferred_element_type=F32)
                        + lax.dot_general(dpx_b[:, sl], wx_ref[hd], NT, preferred_element_type=F32))
        dxc = dxc + jnp.concatenate(back, axis=1)
        sm_ref[4:5, :] += jnp.sum(dxc, axis=0, keepdims=True)
        for k in range(4):
            sm_ref[k:k + 1, :] += jnp.sum(dxc * xtaps[k], axis=0, keepdims=True)
        dxc_ext = jnp.concatenate([dxc, cdxc[...]], axis=0)
        dp_ref[:, 0:D] = sum(cw_ref[k:k + 1, :] * _up(dxc_ext, 3 - k)[0:ts] for k in range(4)).astype(BF16)
        cdxc[...] = dxc[0:hl, :]

    def full(a):
        return pl.BlockSpec(a.shape, lambda i: (0,) * a.ndim)

    rev = lambda w: pl.BlockSpec((ts, w), lambda i: (n_t - 1 - i, 0))
    halo = lambda w: pl.BlockSpec((hb, w), lambda i: (jnp.maximum((n_t - 1 - i) * (ts // hb) - 1, 0), 0))
    return pl.pallas_call(
        body, name="l0_bwd_mix", grid=(n_t,),
        in_specs=[rev(D), rev(6 * D), halo(6 * D), rev(D), halo(D)]
        + [full(a) for a in (gate, cw, cb, wa, ba, wx, bx, lam, sw, wo)],
        out_specs=[rev(6 * D), rev(D), rev(D), rev(D), pl.BlockSpec((2 * SUBLANES, D), lambda i: (0, 0))],
        out_shape=[jax.ShapeDtypeStruct((s_len, 6 * D), BF16)] + [jax.ShapeDtypeStruct((s_len, D), BF16)] * 3
        + [jax.ShapeDtypeStruct((2 * SUBLANES, D), F32)],
        scratch_shapes=[pltpu.VMEM((hl, D), F32)] * 4,
        compiler_params=_cp(("arbitrary",)),
    )(dx1, proj, proj, hst, hst, gate, cw, cb, wa, ba, wx, bx, lam, sw, wo)


def _dgrad_norm(dproj, w, x, dres, g, sc, name):
    s_len, nb = x.shape[0], w.shape[2]
    ts = _tile(s_len, TS_DGRAD)

    def body(dp_ref, w_ref, x_ref, dr_ref, g_ref, sc_ref, dx_ref, s1_ref, s2_ref):
        @pl.when(pl.program_id(0) == 0)
        def _():
            s1_ref[...] = jnp.zeros_like(s1_ref)
            s2_ref[...] = jnp.zeros_like(s2_ref)

        dh = sum(lax.dot_general(dp_ref[:, k * nb:(k + 1) * nb], w_ref[k], NT, preferred_element_type=F32)
                 for k in range(N_CHIP))
        xv = x_ref[...]
        r = lax.rsqrt(jnp.mean(xv * xv, axis=-1, keepdims=True) + RMS_EPS)
        n = xv * r
        s1_ref[...] += jnp.sum(dh, axis=0, keepdims=True)
        s2_ref[...] += jnp.sum(dh * n, axis=0, keepdims=True)
        dn = dh * (g_ref[...] * (1.0 + sc_ref[...]))
        dx_ref[...] = dr_ref[...] + r * (dn - n * jnp.mean(dn * n, axis=-1, keepdims=True))

    row = lambda wd: pl.BlockSpec((ts, wd), lambda i: (i, 0))
    vec = pl.BlockSpec((1, D), lambda i: (0, 0))
    return pl.pallas_call(
        body, name=name, grid=(s_len // ts,),
        in_specs=[row(N_CHIP * nb), pl.BlockSpec(w.shape, lambda i: (0, 0, 0)), row(D), row(D), vec, vec],
        out_specs=[row(D), vec, vec],
        out_shape=[jax.ShapeDtypeStruct((s_len, D), F32)] + [jax.ShapeDtypeStruct((1, D), F32)] * 2,
        compiler_params=_cp(("arbitrary",)),
    )(dproj, w, x, dres, g, sc)


def _wgrad(a, b, groups, ka, nb, a_col, b_col, name):
    s_len = a.shape[0]
    ts = _tile(s_len, TS_WGRAD)

    def body(a_ref, b_ref, o_ref):
        @pl.when(pl.program_id(1) == 0)
        def _():
            o_ref[...] = jnp.zeros_like(o_ref)

        o_ref[...] += lax.dot_general(a_ref[...].astype(BF16), b_ref[...].astype(BF16), TN, preferred_element_type=F32)

    return pl.pallas_call(
        body, name=name, grid=(groups, s_len // ts),
        in_specs=[pl.BlockSpec((ts, ka), lambda g, s: (s, a_col(g))), pl.BlockSpec((ts, nb), lambda g, s: (s, b_col(g)))],
        out_specs=pl.BlockSpec((None, ka, nb), lambda g, s: (g, 0, 0)),
        out_shape=jax.ShapeDtypeStruct((groups, ka, nb), F32),
        compiler_params=_cp(("parallel", "arbitrary")),
    )(a, b)


def _wgrad_rows(a, b, groups, name):
    s_len, nb = b.shape
    ka = a.shape[1] // groups
    ts = _tile(s_len, TS_WGRAD)

    def body(a_ref, b_ref, o_ref):
        @pl.when(pl.program_id(0) == 0)
        def _():
            o_ref[...] = jnp.zeros_like(o_ref)

        bb = b_ref[...].astype(BF16)
        for g in range(groups):
            o_ref[g] += lax.dot_general(a_ref[:, g * ka:(g + 1) * ka], bb, TN, preferred_element_type=F32)

    return pl.pallas_call(
        body, name=name, grid=(s_len // ts,),
        in_specs=[pl.BlockSpec((ts, groups * ka), lambda s: (s, 0)), pl.BlockSpec((ts, nb), lambda s: (s, 0))],
        out_specs=pl.BlockSpec((groups, ka, nb), lambda s: (0, 0, 0)),
        out_shape=jax.ShapeDtypeStruct((groups, ka, nb), F32),
        compiler_params=_cp(("arbitrary",)),
    )(a, b)


def _wgrad_heads(xc, dpa, dpx):
    s_len = xc.shape[0]
    ts = _tile(s_len, TS_WGRAD)
    hd = LRU_HEAD_DIM

    def body(x_ref, a_ref, b_ref, oa_ref, ob_ref):
        @pl.when(pl.program_id(0) == 0)
        def _():
            oa_ref[...] = jnp.zeros_like(oa_ref)
            ob_ref[...] = jnp.zeros_like(ob_ref)

        for h in range(LRU_HEADS):
            sl = slice(h * hd, (h + 1) * hd)
            oa_ref[h] += lax.dot_general(x_ref[:, sl], a_ref[:, sl], TN, preferred_element_type=F32)
            ob_ref[h] += lax.dot_general(x_ref[:, sl], b_ref[:, sl], TN, preferred_element_type=F32)

    row = pl.BlockSpec((ts, D), lambda s: (s, 0))
    acc = pl.BlockSpec((LRU_HEADS, hd, hd), lambda s: (0, 0, 0))
    return pl.pallas_call(
        body, name="l0_wgrad_heads", grid=(s_len // ts,), in_specs=[row] * 3, out_specs=[acc] * 2,
        out_shape=[jax.ShapeDtypeStruct((LRU_HEADS, hd, hd), F32)] * 2,
        compiler_params=_cp(("arbitrary",)),
    )(xc, dpa, dpx)


def _wo_final(mt, wo, gate, name):
    rb = mt.shape[1]

    def body(m_ref, w_ref, gate_ref, dw_ref, dg_ref):
        @pl.when(pl.program_id(0) == 0)
        def _():
            dg_ref[...] = jnp.zeros_like(dg_ref)

        mv = m_ref[...]
        dw_ref[...] = mv * gate_ref[...]
        dg_ref[...] += jnp.sum(mv * w_ref[...].astype(F32), axis=0, keepdims=True)

    blk = pl.BlockSpec((None, rb, D), lambda k: (k, 0, 0))
    vec = pl.BlockSpec((1, D), lambda k: (0, 0))
    return pl.pallas_call(
        body, name=name, grid=(N_CHIP,), in_specs=[blk, blk, vec], out_specs=[blk, vec],
        out_shape=[jax.ShapeDtypeStruct(mt.shape, F32), jax.ShapeDtypeStruct((1, D), F32)],
        compiler_params=_cp(("arbitrary",)),
    )(mt, wo, gate)


ROW_NORM_G, ROW_CONV_W, ROW_CONV_B, ROW_B_A, ROW_B_X, ROW_LAMBDA, ROW_SC_W, ROW_POOL_B, ROW_POOL_S, ROW_FINAL_G = (
    0, 2, 6, 7, 8, 9, 10, 13, 15, 17)


def _small_pack(s1_0, s2_0, s1_1, s2_1, sm0, dsc1, dbg1, dgf, losscols, dgate0, dgate1, norm_g, sc0, sc1, lam):
    def body(s1_0r, s2_0r, s1_1r, s2_1r, sm, dsc, dbg, dgfr, lcols, dg0, dg1, ng, sc0r, sc1r, lamr, buf, dmod, loss):
        buf[...] = jnp.zeros_like(buf)
        buf[0:1, :] = s2_0r[...] * (1.0 + sc0r[...])
        buf[1:2, :] = s2_1r[...] * (1.0 + sc1r[...])
        buf[ROW_CONV_W:ROW_CONV_W + 4, :] = sm[0:4, :]
        buf[ROW_CONV_B:ROW_CONV_B + 1, :] = sm[4:5, :]
        buf[ROW_B_A:ROW_B_A + 1, :] = sm[5:6, :]
        buf[ROW_B_X:ROW_B_X + 1, :] = sm[6:7, :]
        buf[ROW_LAMBDA:ROW_LAMBDA + 1, :] = -sm[7:8, :] * _sigmoid(-lamr[...])
        buf[ROW_SC_W:ROW_SC_W + 3, :] = sm[8:11, :]
        for k in range(2):
            buf[ROW_POOL_B + k:ROW_POOL_B + k + 1, :] = dbg[:, k * D:(k + 1) * D]
            buf[ROW_POOL_S + k:ROW_POOL_S + k + 1, :] = dsc[:, k * D:(k + 1) * D]
        buf[ROW_FINAL_G:ROW_FINAL_G + 1, :] = dgfr[...]
        pieces = (s1_0r[...], s2_0r[...] * ng[0:1, :], dg0[...], s1_1r[...], s2_1r[...] * ng[1:2, :], dg1[...])
        for k, pc in enumerate(pieces):
            dmod[:, k * D:(k + 1) * D] = jnp.broadcast_to(pc, (SUBLANES, D))
        loss[...] = jnp.broadcast_to(jnp.sum(lcols[...], axis=1, keepdims=True) * (0.5 / D), loss.shape)

    args = (s1_0, s2_0, s1_1, s2_1, sm0, dsc1, dbg1, dgf, losscols, dgate0, dgate1, norm_g, sc0, sc1, lam)
    return pl.pallas_call(
        body, name="small_pack", in_specs=[VMEM] * len(args), out_specs=[VMEM] * 3,
        out_shape=[jax.ShapeDtypeStruct((SMALL_ROWS, D), F32), jax.ShapeDtypeStruct((SUBLANES, 6 * D), F32),
                   jax.ShapeDtypeStruct((SUBLANES, 128), F32)],
        compiler_params=_cp(),
    )(*args)


def _small_comm(buf_a, buf_b, dmod8):
    ra, rb = buf_a.shape[0] // N_DEV, buf_b.shape[0] // N_DEV
    wb = buf_b.shape[1]

    def body(a_ref, b_ref, dm_ref, oa_ref, ob_ref, odm_ref, ina, inb, dslot, sa, sb, s1, r1, s2, r2):
        x, y, c = _pos()
        me = 4 * x + 2 * y + c
        peers = []
        for r in range(1, N_DEV):
            fx, fy, fc = (r >> 2) & 1, (r >> 1) & 1, r & 1
            px, py, pc = _flip(x, fx), _flip(y, fy), _flip(c, fc)
            peers.append(((px, py, pc), 4 * px + 2 * py + pc))
        seg_a = lambda d: pl.ds(pl.multiple_of(d * ra, SUBLANES), ra)
        seg_b = lambda d: pl.ds(pl.multiple_of(d * rb, SUBLANES), rb)
        first = []
        for r, (peer, pid) in enumerate(peers):
            for k, (src, dst) in enumerate(((a_ref.at[seg_a(pid), :], ina.at[r]), (b_ref.at[seg_b(pid), :], inb.at[r]),
                                            (dm_ref, dslot.at[me]))):
                cp = pltpu.make_async_remote_copy(src_ref=src, dst_ref=dst, send_sem=s1.at[3 * r + k],
                                                  recv_sem=r1.at[3 * r + k], device_id=peer, device_id_type=MESH)
                cp.start()
                first.append(cp)
        dslot[me] = dm_ref[...]
        for cp in first:
            cp.wait()
        acc_a, acc_b = a_ref[seg_a(me), :], b_ref[seg_b(me), :]
        for r in range(N_DEV - 1):
            acc_a = acc_a + ina[r]
            acc_b = acc_b + inb[r]
        sa[...] = acc_a
        sb[...] = acc_b
        oa_ref[seg_a(me), :] = acc_a
        ob_ref[seg_b(me), :] = acc_b
        second = []
        for r, (peer, pid) in enumerate(peers):
            for k, (src, dst) in enumerate(((sa, oa_ref.at[seg_a(me), :]), (sb, ob_ref.at[seg_b(me), :]))):
                cp = pltpu.make_async_remote_copy(src_ref=src, dst_ref=dst, send_sem=s2.at[2 * r + k],
                                                  recv_sem=r2.at[2 * r + k], device_id=peer, device_id_type=MESH)
                cp.start()
                second.append(cp)
        rows = _rows(SUBLANES, dm_ref.shape[1])
        dm_all = jnp.zeros(dm_ref.shape, F32)
        for d in range(N_DEV):
            dm_all = jnp.where(rows == d, dslot[d], dm_all)
        odm_ref[...] = dm_all
        for cp in second:
            cp.wait()

    nrel = N_DEV - 1
    return pl.pallas_call(
        body, name="small_comm", in_specs=[VMEM] * 3, out_specs=[VMEM] * 3,
        out_shape=[jax.ShapeDtypeStruct(buf_a.shape, F32), jax.ShapeDtypeStruct(buf_b.shape, F32),
                   jax.ShapeDtypeStruct(dmod8.shape, F32)],
        scratch_shapes=[pltpu.VMEM((nrel, ra, D), F32), pltpu.VMEM((nrel, rb, wb), F32),
                        pltpu.VMEM((N_DEV,) + dmod8.shape, F32), pltpu.VMEM((ra, D), F32), pltpu.VMEM((rb, wb), F32),
                        pltpu.SemaphoreType.DMA((3 * nrel,)), pltpu.SemaphoreType.DMA((3 * nrel,)),
                        pltpu.SemaphoreType.DMA((2 * nrel,)), pltpu.SemaphoreType.DMA((2 * nrel,))],
        compiler_params=_cp(),
    )(buf_a, buf_b, dmod8)


def _adam(w, g, m, v):
    m2 = ADAM_B1 * m + (1.0 - ADAM_B1) * g
    v2 = ADAM_B2 * v + (1.0 - ADAM_B2) * (g * g)
    m_hat = m2 / (1.0 - ADAM_B1 ** ADAM_STEP)
    v_hat = v2 / (1.0 - ADAM_B2 ** ADAM_STEP)
    return -ADAM_LR * (m_hat / (jnp.sqrt(v_hat) + ADAM_EPS) + ADAM_WD * w), m2, v2


def _small_adam(red_a, red_b, dm_all, params):
    n = len(params)

    def body(*refs):
        ra, rb, dm = refs[:3]
        wmv = refs[3:3 + 3 * n]
        outs = refs[3 + 3 * n:]
        x, y, _ = _pos()
        chip = 2 * x + y

        def shard(row0, nrows, width):
            per_row = D // width
            cands = []
            for k in range(N_CHIP):
                if nrows == 1 or per_row >= N_CHIP:
                    cands.append(ra[row0:row0 + nrows, k * width:(k + 1) * width])
                else:
                    rr, cc = divmod(k * width, D)
                    cands.append(ra[row0 + rr:row0 + rr + 1, cc:cc + width])
            g = cands[0]
            for k in range(1, N_CHIP):
                g = jnp.where(chip == k, cands[k], g)
            return g

        dms = jnp.sum(dm[...], axis=0, keepdims=True)
        hw = LRU_HEADS * LRU_HEAD_DIM
        grads = [
            ra[ROW_NORM_G:ROW_NORM_G + 2, :],
            None,
            shard(ROW_CONV_W, 4, D // N_CHIP),
            ra[ROW_CONV_B:ROW_CONV_B + 1, :],
            rb[0:hw, :],
            ra[ROW_B_A:ROW_B_A + 1, :],
            rb[hw:2 * hw, :],
            ra[ROW_B_X:ROW_B_X + 1, :],
            ra[ROW_LAMBDA:ROW_LAMBDA + 1, :],
            shard(ROW_SC_W, 3, D // N_CHIP),
            shard(ROW_POOL_B, 2, 2 * D // N_CHIP),
            shard(ROW_POOL_S, 2, 2 * D // N_CHIP),
            ra[ROW_FINAL_G:ROW_FINAL_G + 1, :],
        ]
        for p in range(n):
            w_ref, m_ref, v_ref = wmv[3 * p:3 * p + 3]
            g_out, d_out, m_out, v_out = outs[4 * p:4 * p + 4]
            if grads[p] is None:
                for l in range(2):
                    g = dms[:, l * 3 * D:(l + 1) * 3 * D]
                    dl, m2, v2 = _adam(w_ref[l:l + 1, :], g, m_ref[l:l + 1, :], v_ref[l:l + 1, :])
                    g_out[l:l + 1, :] = g
                    d_out[l:l + 1, :] = dl
                    m_out[l:l + 1, :] = m2
                    v_out[l:l + 1, :] = v2
            else:
                g = grads[p]
                dl, m2, v2 = _adam(w_ref[...], g, m_ref[...], v_ref[...])
                g_out[...] = g
                d_out[...] = dl
                m_out[...] = m2
                v_out[...] = v2

    flat = [a for p in params for a in p]
    return pl.pallas_call(
        body, name="small_adam", in_specs=[VMEM] * (3 + len(flat)), out_specs=[VMEM] * (4 * n),
        out_shape=[jax.ShapeDtypeStruct(p[0].shape, F32) for p in params for _ in range(4)],
        compiler_params=_cp(),
    )(red_a, red_b, dm_all, *flat)


def _modw_adam(ca_t, dm_sh, w, m, v):
    nw = w.shape[2]

    def body(c_ref, d_ref, w_ref, m_ref, v_ref, g_out, d_out, m_out, v_out):
        g = jnp.dot(c_ref[...], d_ref[...], precision=lax.Precision.HIGHEST, preferred_element_type=F32)
        dl, m2, v2 = _adam(w_ref[...], g, m_ref[...], v_ref[...])
        g_out[...] = g
        d_out[...] = dl
        m_out[...] = m2
        v_out[...] = v2

    blk = pl.BlockSpec((None, D, nw), lambda l: (l, 0, 0))
    return pl.pallas_call(
        body, name="modw_adam", grid=(2,),
        in_specs=[pl.BlockSpec((D, SUBLANES), lambda l: (0, 0)), pl.BlockSpec((None, SUBLANES, nw), lambda l: (l, 0, 0)),
                  blk, blk, blk],
        out_specs=[blk] * 4, out_shape=[jax.ShapeDtypeStruct(w.shape, F32)] * 4,
        compiler_params=_cp(("arbitrary",)),
    )(ca_t, dm_sh, w, m, v)


def _half_rows(r):
    return r // 2


def _sib_send_halves(gs):
    n = len(gs)

    def body(*refs):
        ins, outs, ssem, rsem = refs[:n], refs[n:2 * n], refs[2 * n], refs[2 * n + 1]
        x, y, c = _pos()
        cps = []
        for a in range(n):
            hr = _half_rows(gs[a].shape[1])
            cp = pltpu.make_async_remote_copy(
                src_ref=ins[a].at[:, pl.ds(pl.multiple_of((1 - c) * hr, SUBLANES), hr), :], dst_ref=outs[a],
                send_sem=ssem.at[a], recv_sem=rsem.at[a], device_id=(x, y, 1 - c), device_id_type=MESH)
            cp.start()
            cps.append(cp)
        for cp in cps:
            cp.wait()

    return pl.pallas_call(
        body, name="grad_sib_halves", in_specs=[ANY] * n, out_specs=[ANY] * n,
        out_shape=[jax.ShapeDtypeStruct((N_CHIP, _half_rows(g.shape[1]), g.shape[2]), F32) for g in gs],
        scratch_shapes=[pltpu.SemaphoreType.DMA((n,)), pltpu.SemaphoreType.DMA((n,))],
        compiler_params=_cp(),
    )(*gs)


def _add_half(g, got, cidx, name):
    _, hr, cc = got.shape
    rb = min(hr, 256)

    def body(c_ref, g_ref, r_ref, o_ref):
        o_ref[...] = (g_ref[...] + r_ref[...]).astype(o_ref.dtype)

    blk = pl.BlockSpec((None, rb, cc), lambda k, j, c_ref: (k, j, 0))
    return pl.pallas_call(
        body, name=name,
        grid_spec=pltpu.PrefetchScalarGridSpec(
            num_scalar_prefetch=1, grid=(N_CHIP, hr // rb),
            in_specs=[pl.BlockSpec((None, rb, cc), lambda k, j, c_ref: (k, c_ref[0] * (hr // rb) + j, 0)), blk],
            out_specs=blk),
        out_shape=jax.ShapeDtypeStruct(got.shape, GRAD_WIRE_DTYPE),
        compiler_params=_cp(("parallel", "parallel")),
    )(cidx, g, got)


def _chip_scatter(ps):
    n = len(ps)

    def body(*refs):
        ins, outs, ssem, rsem = refs[:n], refs[n:2 * n], refs[2 * n], refs[2 * n + 1]
        x, y, c = _pos()
        cps = []
        for a in range(n):
            for q, (fx, fy) in enumerate(((1, 0), (0, 1), (1, 1))):
                px, py = _flip(x, fx), _flip(y, fy)
                cp = pltpu.make_async_remote_copy(
                    src_ref=ins[a].at[2 * px + py], dst_ref=outs[a].at[q],
                    send_sem=ssem.at[3 * a + q], recv_sem=rsem.at[3 * a + q], device_id=(px, py, c), device_id_type=MESH)
                cp.start()
                cps.append(cp)
        for cp in cps:
            cp.wait()

    return pl.pallas_call(
        body, name="grad_chip_scatter", in_specs=[ANY] * n, out_specs=[ANY] * n,
        out_shape=[jax.ShapeDtypeStruct((N_CHIP - 1,) + p.shape[1:], p.dtype) for p in ps],
        scratch_shapes=[pltpu.SemaphoreType.DMA((3 * n,)), pltpu.SemaphoreType.DMA((3 * n,))],
        compiler_params=_cp(),
    )(*ps)


def _add_owner(p, got, chipidx, name):
    _, hr, cc = p.shape
    rb = min(hr, 256)

    def body(k_ref, p_ref, r_ref, o_ref):
        o_ref[...] = ((p_ref[...].astype(F32) + r_ref[0].astype(F32)) + r_ref[1].astype(F32)) + r_ref[2].astype(F32)

    return pl.pallas_call(
        body, name=name,
        grid_spec=pltpu.PrefetchScalarGridSpec(
            num_scalar_prefetch=1, grid=(hr // rb,),
            in_specs=[pl.BlockSpec((None, rb, cc), lambda j, k_ref: (k_ref[0], j, 0)),
                      pl.BlockSpec((N_CHIP - 1, rb, cc), lambda j, k_ref: (0, j, 0))],
            out_specs=pl.BlockSpec((rb, cc), lambda j, k_ref: (j, 0))),
        out_shape=jax.ShapeDtypeStruct((hr, cc), F32),
        compiler_params=_cp(("parallel",)),
    )(chipidx, p, got)


def _sib_exchange(ts_):
    n = len(ts_)

    def body(*refs):
        ins, outs, ssem, rsem = refs[:n], refs[n:2 * n], refs[2 * n], refs[2 * n + 1]
        x, y, c = _pos()
        cps = []
        for a in range(n):
            cp = pltpu.make_async_remote_copy(src_ref=ins[a], dst_ref=outs[a], send_sem=ssem.at[a],
                                              recv_sem=rsem.at[a], device_id=(x, y, 1 - c), device_id_type=MESH)
            cp.start()
            cps.append(cp)
        for cp in cps:
            cp.wait()

    return pl.pallas_call(
        body, name="grad_sib_exchange", in_specs=[ANY] * n, out_specs=[ANY] * n,
        out_shape=[jax.ShapeDtypeStruct(t.shape, F32) for t in ts_],
        scratch_shapes=[pltpu.SemaphoreType.DMA((n,))] * 2,
        compiler_params=_cp(),
    )(*ts_)


def _adam_2d(w, g_own, g_sib, m, v, cidx, name):
    rr, cc = w.shape
    hr = rr // 2
    rb = min(hr, 256)
    nb = hr // rb

    def body(c_ref, w_ref, go_ref, gs_ref, m_ref, v_ref, g_out, d_out, m_out, v_out):
        g = jnp.where(pl.program_id(0) == c_ref[0], go_ref[...], gs_ref[...])
        dl, m2, v2 = _adam(w_ref[...], g, m_ref[...], v_ref[...])
        g_out[...] = g
        d_out[...] = dl
        m_out[...] = m2
        v_out[...] = v2

    blk = pl.BlockSpec((rb, cc), lambda h, j, c_ref: (h * nb + j, 0))
    hblk = pl.BlockSpec((rb, cc), lambda h, j, c_ref: (j, 0))
    return pl.pallas_call(
        body, name=name,
        grid_spec=pltpu.PrefetchScalarGridSpec(
            num_scalar_prefetch=1, grid=(2, nb), in_specs=[blk, hblk, hblk, blk, blk], out_specs=[blk] * 4),
        out_shape=[jax.ShapeDtypeStruct((rr, cc), F32)] * 4, compiler_params=_cp(("parallel", "parallel")),
    )(cidx, w, g_own, g_sib, m, v)


def kernel(x, c, norm_g, mod_w, mod_b, hy_w_in, hy_conv_w, hy_conv_b, lru_w_a, lru_b_a, lru_w_x, lru_b_x, lru_lambda, sc_conv_w, hy_w_out, pool_w_in, pool_w_grp, pool_b_grp, pool_scale, pool_w_out, final_g, loss_target, m_norm_g, m_mod_w, m_mod_b, m_hy_w_in, m_hy_conv_w, m_hy_conv_b, m_lru_w_a, m_lru_b_a, m_lru_w_x, m_lru_b_x, m_lru_lambda, m_sc_conv_w, m_hy_w_out, m_pool_w_in, m_pool_w_grp, m_pool_b_grp, m_pool_scale, m_pool_w_out, m_final_g, v_norm_g, v_mod_w, v_mod_b, v_hy_w_in, v_hy_conv_w, v_hy_conv_b, v_lru_w_a, v_lru_b_a, v_lru_w_x, v_lru_b_x, v_lru_lambda, v_sc_conv_w, v_hy_w_out, v_pool_w_in, v_pool_w_grp, v_pool_b_grp, v_pool_scale, v_pool_w_out, v_final_g):
    ax, ay, ac = _pos()
    me = 4 * ax + 2 * ay + ac
    chip = 2 * ax + ay
    xs = x[0]
    tgt = loss_target[0]
    gd = POOL_GROUP_DIM

    ca_all, mod_all, small_w = _mod_fwd(jnp.broadcast_to(c, (SUBLANES, D)), mod_w, mod_b,
                                        hy_conv_w[0], sc_conv_w[0], pool_b_grp, pool_scale)
    mod_me = lax.dynamic_index_in_dim(mod_all, me, axis=1, keepdims=False)
    sh0, sc0, gt0 = (mod_me[0:1, k * D:(k + 1) * D] for k in range(3))
    sh1, sc1, gt1 = (mod_me[1:2, k * D:(k + 1) * D] for k in range(3))
    cw = small_w[SW_CONV:SW_CONV + 4, 0:D]
    sw = small_w[SW_SC:SW_SC + 3, 0:D]
    pool_b = small_w[SW_POOL_B:SW_POOL_B + 1, :]
    pool_s = small_w[SW_POOL_S:SW_POOL_S + 1, :]
    g0, g1, gf = norm_g[0:1], norm_g[1:2], final_g.reshape(1, D)
    cb, ba, bx, lam = hy_conv_b, lru_b_a, lru_b_x, lru_lambda

    big = [hy_w_in[0], hy_w_out[0], pool_w_in[0], pool_w_grp[0].reshape(4 * 128, gd), pool_w_out[0]]
    cidx = ac.reshape(1).astype(jnp.int32)
    kidx = chip.reshape(1).astype(jnp.int32)
    w_in0, w_out0, w_in1, w_grp, w_out1 = _wgather(
        [_wcast_own_block(w, kidx, f"wcast_own_block_{a}") for a, w in enumerate(big)])
    w_grp =w_grp.reshape(N_CHIP, 4, 128, gd).transpose(1, 0, 2, 3).reshape(4, gd, gd)
    wa_b, wx_b = _wcast([lru_w_a[0], lru_w_x[0]])

    h0, proj0 = _norm_proj(xs, g0, sc0, sh0, w_in0, "l0_proj")
    x1, hst, y0 = _l0_mix(proj0, xs, gt0, cw, cb, wa_b, ba, wx_b, bx, lam, sw, w_out0.reshape(2 * D, D))
    h1, proj1 = _norm_proj(x1, g1, sc1, sh1, w_in1, "l1_proj")
    dpool, mixed, y1, dx2, losscols, dgf = _l1_mix(proj1, x1, tgt, gt1, w_grp, pool_b, pool_s,
                                                    w_out1.reshape(2 * D, D), gf)

    dproj1, dmixed, dsc1, dbg1 = _l1_bwd_mix(dx2, proj1, mixed, gt1, w_grp, pool_s, w_out1.reshape(2 * D, D))
    mt1 = _wgrad_rows(y1, dx2, N_CHIP, "l1_wgrad_out")
    d_wgrp = _wgrad(dpool, dmixed, 4, gd, gd, lambda g: g, lambda g: g, "l1_wgrad_grp")
    d_win1 = _wgrad(h1, dproj1, N_CHIP, D, D, lambda g: 0, lambda g: g, "l1_wgrad_in")
    dx1, s1_1, s2_1 = _dgrad_norm(dproj1, w_in1, x1, dx2, g1, sc1, "l1_bwd_proj")
    d_wout1, dgate1 = _wo_final(mt1, w_out1, gt1, "l1_wo_final")

    dproj0, xc, dpa, dpx, sm0 = _l0_bwd_mix(dx1, proj0, hst, gt0, cw, cb, wa_b, ba, wx_b, bx, lam, sw,
                                            w_out0.reshape(2 * D, D))
    mt0 = _wgrad_rows(y0, dx1, N_CHIP, "l0_wgrad_out")
    d_win0 = _wgrad(h0, dproj0, N_CHIP, D, 6 * D // N_CHIP, lambda g: 0, lambda g: g, "l0_wgrad_in")
    d_wa, d_wx = _wgrad_heads(xc, dpa, dpx)
    grad_x, s1_0, s2_0 = _dgrad_norm(dproj0, w_in0, xs, dx1, g0, sc0, "l0_bwd_proj")
    d_wout0, dgate0 = _wo_final(mt0, w_out0, gt0, "l0_wo_final")

    buf_a, dmod8, loss8 = _small_pack(s1_0, s2_0, s1_1, s2_1, sm0, dsc1, dbg1, dgf, losscols, dgate0, dgate1,
                                      norm_g, sc0, sc1, lam)
    hw = LRU_HEADS * LRU_HEAD_DIM
    buf_b = jnp.concatenate([d_wa.reshape(hw, LRU_HEAD_DIM), d_wx.reshape(hw, LRU_HEAD_DIM)], axis=0)
    red_a, red_b, dm_all = _small_comm(buf_a, buf_b, dmod8)
    small = [(norm_g, m_norm_g, v_norm_g), (mod_b, m_mod_b, v_mod_b),
             (hy_conv_w[0], m_hy_conv_w[0], v_hy_conv_w[0]), (hy_conv_b, m_hy_conv_b, v_hy_conv_b),
             tuple(a.reshape(hw, LRU_HEAD_DIM) for a in (lru_w_a, m_lru_w_a, v_lru_w_a)),
             (lru_b_a, m_lru_b_a, v_lru_b_a),
             tuple(a.reshape(hw, LRU_HEAD_DIM) for a in (lru_w_x, m_lru_w_x, v_lru_w_x)),
             (lru_b_x, m_lru_b_x, v_lru_b_x), (lru_lambda, m_lru_lambda, v_lru_lambda),
             (sc_conv_w[0], m_sc_conv_w[0], v_sc_conv_w[0]), (pool_b_grp, m_pool_b_grp, v_pool_b_grp),
             (pool_scale, m_pool_scale, v_pool_scale),
             tuple(a.reshape(1, D) for a in (final_g, m_final_g, v_final_g))]
    small_names = ["norm_g", "mod_b", "hy_conv_w", "hy_conv_b", "lru_w_a", "lru_b_a", "lru_w_x", "lru_b_x",
                   "lru_lambda", "sc_conv_w", "pool_b_grp", "pool_scale", "final_g"]
    small_out = _small_adam(red_a, red_b, dm_all, small)
    res = {}
    shapes = dict(norm_g=norm_g, mod_b=mod_b, hy_conv_w=hy_conv_w, hy_conv_b=hy_conv_b, lru_w_a=lru_w_a, lru_b_a=lru_b_a,
                  lru_w_x=lru_w_x, lru_b_x=lru_b_x, lru_lambda=lru_lambda, sc_conv_w=sc_conv_w, pool_b_grp=pool_b_grp,
                  pool_scale=pool_scale, final_g=final_g)
    for p, nm in enumerate(small_names):
        res[nm] = tuple(o.reshape(shapes[nm].shape) for o in small_out[4 * p:4 * p + 4])

    nw = mod_w.shape[2]
    dm_sh = jnp.stack([lax.dynamic_slice_in_dim(dm_all[:, l * 3 * D:(l + 1) * 3 * D], chip * nw, nw, axis=1)
                       for l in range(2)])
    res["mod_w"] = tuple(_modw_adam(ca_all.T, dm_sh, mod_w, m_mod_w, v_mod_w))

    d_wgrp = d_wgrp.reshape(4, N_CHIP, 128, gd).transpose(1, 0, 2, 3).reshape(N_CHIP, 4 * 128, gd)
    grads = [d_win0, d_wout0, d_win1, d_wgrp, d_wout1]
    got = _sib_send_halves(grads)
    parts = [_add_half(g, r, cidx, f"grad_add_half_{a}") for a, (g, r) in enumerate(zip(grads, got))]
    got2 = _chip_scatter(parts)
    halves = [_add_owner(p, r, kidx, f"grad_add_owner_{a}") for a, (p, r) in enumerate(zip(parts, got2))]
    sib_halves = _sib_exchange(halves)
    big_names = ["hy_w_in", "hy_w_out", "pool_w_in", "pool_w_grp", "pool_w_out"]
    big_wmv = [(hy_w_in, m_hy_w_in, v_hy_w_in), (hy_w_out, m_hy_w_out, v_hy_w_out), (pool_w_in, m_pool_w_in, v_pool_w_in),
               (pool_w_grp, m_pool_w_grp, v_pool_w_grp), (pool_w_out, m_pool_w_out, v_pool_w_out)]
    for a, nm in enumerate(big_names):
        rr, cc = big[a].shape
        w, m, v = (t.reshape(rr, cc) for t in big_wmv[a])
        outs = _adam_2d(w, halves[a], sib_halves[a], m, v, cidx, f"adam_{nm}")
        res[nm] = tuple(o.reshape(big_wmv[a][0].shape) for o in outs)

    loss = lax.psum(loss8[0, 0], ("x", "y", "c"))
    order = ["norm_g", "mod_w", "mod_b", "hy_w_in", "hy_conv_w", "hy_conv_b", "lru_w_a", "lru_b_a", "lru_w_x", "lru_b_x",
             "lru_lambda", "sc_conv_w", "hy_w_out", "pool_w_in", "pool_w_grp", "pool_b_grp", "pool_scale", "pool_w_out",
             "final_g"]
    return (loss, grad_x[None], *[res[nm][0] for nm in order], *[res[nm][1] for nm in order],
            *[res[nm][2] for nm in order], *[res[nm][3] for nm in order])
```

```python
import jax
import jax.numpy as jnp
from jax import lax
from jax.experimental import pallas as pl
from jax.experimental.pallas import tpu as pltpu

F32, BF16 = jnp.float32, jnp.bfloat16
D = 1024
RMS_EPS = 1e-6
SQRT_FLOOR = 1e-30
LRU_C = 8.0
LRU_HEADS, LRU_HEAD_DIM = 8, 128
POOL_WINDOWS = (2, 4, 8, 16)
POOL_GROUP_DIM = 512
ADAM_LR, ADAM_B1, ADAM_B2, ADAM_EPS, ADAM_WD, ADAM_STEP = 0.001, 0.9, 0.999, 1e-08, 0.01, 10
MESH = pl.DeviceIdType.MESH
N_DEV, N_CHIP = 8, 4
SUBLANES = 8
BF16_ROWS = 16
POOL_HALO = 16
TS_PROJ, TS_MIX, TS_WGRAD, TS_DGRAD = 1024, 256, 1024, 256
SMALL_ROWS = 64
GRAD_WIRE_DTYPE = BF16
ANY = pl.BlockSpec(memory_space=pl.ANY)
VMEM = pl.BlockSpec(memory_space=pltpu.VMEM)
NT = (((1,), (1,)), ((), ()))
TN = (((0,), (0,)), ((), ()))


def _cp(sem=None, vmem_mb=56):
    kw = dict(vmem_limit_bytes=vmem_mb * 2 ** 20)
    if sem is not None:
        kw["dimension_semantics"] = sem
    return pltpu.CompilerParams(**kw)


def _tile(n, t):
    return min(n, t)


def _pos():
    return lax.axis_index("x"), lax.axis_index("y"), lax.axis_index("c")


def _flip(v, f):
    return 1 - v if f else v


def _sigmoid(z):
    return 0.5 * jnp.tanh(0.5 * z) + 0.5


def _rows(n, c):
    return lax.broadcasted_iota(jnp.int32, (n, c), 0)


def _down(a, d):
    return a if d == 0 else pltpu.roll(a, d, 0)


def _up(a, d):
    return a if d == 0 else pltpu.roll(a, a.shape[0] - d, 0)


def _scan_fwd_steps(a, u, carry):
    n, c = a.shape
    sub = _rows(SUBLANES, c)
    out = []
    for k in range(n // SUBLANES):
        p = a[k * SUBLANES:(k + 1) * SUBLANES]
        g = u[k * SUBLANES:(k + 1) * SUBLANES]
        for d in (1, 2, 4):
            keep = sub >= d
            g = g + p * jnp.where(keep, pltpu.roll(g, d, 0), 0.0)
            p = p * jnp.where(keep, pltpu.roll(p, d, 0), 1.0)
        h = g + p * carry
        carry = h[SUBLANES - 1:SUBLANES, :]
        out.append(h)
        yield
    return jnp.concatenate(out, axis=0)


def _scan_rev_steps(alpha, b, carry):
    n, c = alpha.shape
    sub = _rows(SUBLANES, c)
    out = []
    for k in reversed(range(n // SUBLANES)):
        p = alpha[k * SUBLANES:(k + 1) * SUBLANES]
        g = b[k * SUBLANES:(k + 1) * SUBLANES]
        for d in (1, 2, 4):
            keep = sub < SUBLANES - d
            g = g + p * jnp.where(keep, pltpu.roll(g, SUBLANES - d, 0), 0.0)
            p = p * jnp.where(keep, pltpu.roll(p, SUBLANES - d, 0), 1.0)
        h = g + p * carry
        carry = h[0:1, :]
        out.append(h)
        yield
    return jnp.concatenate(out[::-1], axis=0)


def _run(steps):
    while True:
        try:
            next(steps)
        except StopIteration as done:
            return done.value


def _paired(progress, pieces):
    n, done = len(pieces), 1
    pieces[0]()
    for frac in progress:
        while done < n and done <= frac * n:
            pieces[done]()
            done += 1
    while done < n:
        pieces[done]()
        done += 1


def _conv_taps(ext, halo, n, width):
    return [_down(ext, width - 1 - k)[halo:halo + n] for k in range(width)]


def _lru_gates(xc, wa_ref, ba, wx_ref, bx):
    xb = xc.astype(BF16)
    pa, px = [], []
    for h in range(LRU_HEADS):
        xh = xb[:, h * LRU_HEAD_DIM:(h + 1) * LRU_HEAD_DIM]
        pa.append(jnp.dot(xh, wa_ref[h], preferred_element_type=F32))
        px.append(jnp.dot(xh, wx_ref[h], preferred_element_type=F32))
    r = _sigmoid(jnp.concatenate(pa, axis=1) + ba)
    ig = _sigmoid(jnp.concatenate(px, axis=1) + bx)
    return r, ig


def _softplus_neg(lam):
    return jnp.maximum(-lam, 0.0) + jnp.log1p(jnp.exp(-jnp.abs(lam)))


def _recip_1_to_2(d):
    r0 = pl.reciprocal(d, approx=True)
    return r0 * (2.0 - d * r0)


def _lru_decay(r, sp, first):
    big_l = (-LRU_C) * r * sp
    a = jnp.exp(big_l)
    th = jnp.tanh(big_l)
    q = (-2.0 * th) * _recip_1_to_2(1.0 - th)
    rs = lax.rsqrt(jnp.maximum(q, SQRT_FLOOR))
    return a, jnp.where(first, 1.0, q * rs), rs


def _pool_inv_counts(t0, n):
    t = (t0 + lax.broadcasted_iota(jnp.int32, (n, 1), 0) + 1).astype(F32)
    return [1.0 / jnp.minimum(t, float(w)) for w in POOL_WINDOWS]


def _window_sums(ext, shift):
    gd = POOL_GROUP_DIM
    out = []
    s = ext
    for k in range(len(POOL_WINDOWS)):
        s = s + shift(s, 2 ** k)
        out.append(s[:, 0:gd])
        if k + 1 < len(POOL_WINDOWS):
            s = s[:, gd:]
    return out


SW_ROWS, SW_COLS = 16, 2 * D
SW_CONV, SW_SC, SW_POOL_B, SW_POOL_S = 0, 4, 8, 9


def _mod_fwd(c8, mod_w, mod_b, conv_w, sc_w, pool_b, pool_s):
    nw = mod_w.shape[2]
    cq, pq = conv_w.shape[1], pool_b.shape[1]

    def body(c_ref, w_ref, b_ref, cw_ref, sw_ref, pb_ref, ps_ref, ca_ref, mod_ref, small_ref,
             cslot, mslot, msend, pslot, psend, s1, r1, s2, r2, s3, r3):
        x, y, c = _pos()
        me = 4 * x + 2 * y + c
        chip = 2 * x + y
        first = []
        for r in range(1, N_DEV):
            fx, fy, fc = (r >> 2) & 1, (r >> 1) & 1, r & 1
            cp = pltpu.make_async_remote_copy(
                src_ref=c_ref, dst_ref=cslot.at[me], send_sem=s1.at[r - 1], recv_sem=r1.at[r - 1],
                device_id=(_flip(x, fx), _flip(y, fy), _flip(c, fc)), device_id_type=MESH)
            cp.start()
            first.append(cp)
        cslot[me] = c_ref[...]
        for cp in first:
            cp.wait()
        rows = _rows(SUBLANES, D)
        call = jnp.zeros((SUBLANES, D), F32)
        for d in range(N_DEV):
            call = jnp.where(rows == d, cslot[d], call)
        ca = call * _sigmoid(call)
        ca_ref[...] = ca
        for l in range(2):
            msend[l] = jnp.dot(ca, w_ref[l], precision=lax.Precision.HIGHEST, preferred_element_type=F32)
        psend[...] = jnp.zeros_like(psend)
        psend[SW_CONV:SW_CONV + 4, 0:cq] = cw_ref[...]
        psend[SW_SC:SW_SC + 3, 0:cq] = sw_ref[...]
        psend[SW_POOL_B:SW_POOL_B + 1, :] = pb_ref[...]
        psend[SW_POOL_S:SW_POOL_S + 1, :] = ps_ref[...]
        second = []
        for q, (fx, fy) in enumerate(((1, 0), (0, 1), (1, 1))):
            peer = (_flip(x, fx), _flip(y, fy), c)
            for src, dst, ss, rs in ((msend, mslot, s2, r2), (psend, pslot, s3, r3)):
                cp = pltpu.make_async_remote_copy(src_ref=src, dst_ref=dst.at[chip], send_sem=ss.at[q], recv_sem=rs.at[q],
                                                  device_id=peer, device_id_type=MESH)
                cp.start()
                second.append(cp)
        mslot[chip] = msend[...]
        pslot[chip] = psend[...]
        for cp in second:
            cp.wait()
        small_ref[...] = jnp.zeros_like(small_ref)
        for j in range(N_CHIP):
            for l in range(2):
                mod_ref[l, :, j * nw:(j + 1) * nw] = mslot[j, l] + b_ref[l:l + 1, j * nw:(j + 1) * nw]
            small_ref[0:SUBLANES, j * cq:(j + 1) * cq] = pslot[j, 0:SUBLANES, 0:cq]
            small_ref[SUBLANES:SW_ROWS, j * pq:(j + 1) * pq] = pslot[j, SUBLANES:SW_ROWS, :]

    args = (c8, mod_w, mod_b, conv_w, sc_w, pool_b, pool_s)
    dma3 = pltpu.SemaphoreType.DMA((N_CHIP - 1,))
    return pl.pallas_call(
        body, name="mod_fwd",
        in_specs=[VMEM] * len(args), out_specs=[VMEM] * 3,
        out_shape=[jax.ShapeDtypeStruct((SUBLANES, D), F32), jax.ShapeDtypeStruct((2, SUBLANES, N_CHIP * nw), F32),
                   jax.ShapeDtypeStruct((SW_ROWS, SW_COLS), F32)],
        scratch_shapes=[pltpu.VMEM((N_DEV, SUBLANES, D), F32), pltpu.VMEM((N_CHIP, 2, SUBLANES, nw), F32),
                        pltpu.VMEM((2, SUBLANES, nw), F32), pltpu.VMEM((N_CHIP, SW_ROWS, pq), F32),
                        pltpu.VMEM((SW_ROWS, pq), F32),
                        pltpu.SemaphoreType.DMA((N_DEV - 1,)), pltpu.SemaphoreType.DMA((N_DEV - 1,)),
                        dma3, dma3, dma3, dma3],
        compiler_params=_cp(),
    )(*args)


def _wcast(ws):
    def body(*refs):
        n = len(refs) // 2
        for a in range(n):
            refs[n + a][...] = refs[a][...].astype(BF16)

    return pl.pallas_call(
        body, name="wcast", in_specs=[VMEM] * len(ws), out_specs=[VMEM] * len(ws),
        out_shape=[jax.ShapeDtypeStruct(w.shape, BF16) for w in ws], compiler_params=_cp(),
    )(*ws)


def _wcast_own_block(w, kidx, name):
    rr, cc = w.shape
    rb = min(rr, 256)

    def body(k_ref, w_ref, o_ref):
        o_ref[...] = w_ref[...].astype(BF16)

    return pl.pallas_call(
        body, name=name,
        grid_spec=pltpu.PrefetchScalarGridSpec(
            num_scalar_prefetch=1, grid=(rr // rb,),
            in_specs=[pl.BlockSpec((rb, cc), lambda j, k_ref: (j, 0))],
            out_specs=pl.BlockSpec((None, rb, cc), lambda j, k_ref: (k_ref[0], j, 0))),
        out_shape=jax.ShapeDtypeStruct((N_CHIP, rr, cc), BF16),
        compiler_params=_cp(("parallel",)),
    )(kidx, w)


def _wgather(bufs):
    n = len(bufs)

    def body(*refs):
        outs = refs[n:2 * n]
        ssem, rsem, fssem, frsem = refs[2 * n:]
        x, y, c = _pos()
        chip = 2 * x + y
        sib = (x, y, 1 - c)
        flips = ((1, 0), (0, 1), (1, 1))

        def half(a, which):
            hr = bufs[a].shape[1] // 2
            return pl.ds(pl.multiple_of(which * hr, BF16_ROWS), hr)

        sends = []
        for a in range(n):
            mine = outs[a].at[chip, half(a, c), :]
            for q, (fx, fy) in enumerate(flips):
                cp = pltpu.make_async_remote_copy(
                    src_ref=mine, dst_ref=mine, send_sem=ssem.at[3 * a + q], recv_sem=rsem.at[3 * a + q],
                    device_id=(_flip(x, fx), _flip(y, fy), c), device_id_type=MESH)
                cp.start()
                sends.append(cp)
        passed = []
        for a in range(n):
            for q, (fx, fy) in enumerate(flips):
                src_chip = 2 * _flip(x, fx) + _flip(y, fy)
                landed = outs[a].at[src_chip, half(a, c), :]
                pltpu.make_async_remote_copy(
                    src_ref=landed, dst_ref=landed, send_sem=ssem.at[3 * a + q], recv_sem=rsem.at[3 * a + q],
                    device_id=sib, device_id_type=MESH).wait_recv()
                cp = pltpu.make_async_remote_copy(
                    src_ref=landed, dst_ref=landed, send_sem=fssem.at[3 * a + q], recv_sem=frsem.at[3 * a + q],
                    device_id=sib, device_id_type=MESH)
                cp.start()
                passed.append(cp)
        for a in range(n):
            for q, (fx, fy) in enumerate(flips):
                src_chip = 2 * _flip(x, fx) + _flip(y, fy)
                other = outs[a].at[src_chip, half(a, 1 - c), :]
                pltpu.make_async_remote_copy(
                    src_ref=other, dst_ref=other, send_sem=fssem.at[3 * a + q], recv_sem=frsem.at[3 * a + q],
                    device_id=sib, device_id_type=MESH).wait_recv()
        for cp in sends + passed:
            cp.wait_send()

    return pl.pallas_call(
        body, name="wgather", in_specs=[ANY] * n, out_specs=[ANY] * n,
        out_shape=[jax.ShapeDtypeStruct(b.shape, BF16) for b in bufs],
        input_output_aliases={a: a for a in range(n)},
        scratch_shapes=[pltpu.SemaphoreType.DMA((3 * n,))] * 4,
        compiler_params=_cp(),
    )(*bufs)


def _norm_proj(x, g, sc, sh, w, name):
    s_len, nb = x.shape[0], w.shape[2]
    ts = _tile(s_len, TS_PROJ)

    def body(x_ref, g_ref, sc_ref, sh_ref, w_ref, h_ref, p_ref):
        @pl.when(pl.program_id(1) == 0)
        def _():
            xv = x_ref[...]
            r = lax.rsqrt(jnp.mean(xv * xv, axis=-1, keepdims=True) + RMS_EPS)
            h_ref[...] = (xv * r * (g_ref[...] * (1.0 + sc_ref[...])) + sh_ref[...]).astype(BF16)

        p_ref[...] = jnp.dot(h_ref[...], w_ref[...], preferred_element_type=F32).astype(BF16)

    vec = pl.BlockSpec((1, D), lambda i, j: (0, 0))
    return pl.pallas_call(
        body, name=name, grid=(s_len // ts, N_CHIP),
        in_specs=[pl.BlockSpec((ts, D), lambda i, j: (i, 0)), vec, vec, vec,
                  pl.BlockSpec((None, D, nb), lambda i, j: (j, 0, 0))],
        out_specs=[pl.BlockSpec((ts, D), lambda i, j: (i, 0)), pl.BlockSpec((ts, nb), lambda i, j: (i, j))],
        out_shape=[jax.ShapeDtypeStruct((s_len, D), BF16), jax.ShapeDtypeStruct((s_len, N_CHIP * nb), BF16)],
        compiler_params=_cp(("parallel", "arbitrary")),
    )(x, g, sc, sh, w)


def _l0_fwd(x, g, sc, sh, w_in, gate, cw, cb, wa, ba, wx, bx, lam, sw, wo):
    s_len, nb = x.shape[0], w_in.shape[2]
    ts = _tile(s_len, TS_MIX)
    n_t = s_len // ts
    hl = SUBLANES

    def body(xa_ref, xb_ref, g_ref, sc_ref, sh_ref, win_ref, gate_ref, cw_ref, cb_ref, wa_ref, ba_ref, wx_ref, bx_ref,
             lam_ref, sw_ref, wo_ref, x1_ref, h_ref, y_ref, h0_ref, p_ref, pcur, pnext, cxa, czz, chh):
        i = pl.program_id(0)

        @pl.when(i == 0)
        def _():
            cxa[...] = jnp.zeros_like(cxa)
            czz[...] = jnp.zeros_like(czz)
            chh[...] = jnp.zeros_like(chh)
            pnext[...] = jnp.zeros_like(pnext)

        pcur[...] = pnext[...]
        xv = xa_ref[...]
        rinv = lax.rsqrt(jnp.mean(xv * xv, axis=-1, keepdims=True) + RMS_EPS)
        h0 = (xv * rinv * (g_ref[...] * (1.0 + sc_ref[...])) + sh_ref[...]).astype(BF16)
        h0_ref[...] = h0

        def project(k):
            def emit():
                pk = jnp.dot(h0, win_ref[k], preferred_element_type=F32).astype(BF16)
                p_ref[:, k * nb:(k + 1) * nb] = pk
                pnext[:, k * nb:(k + 1) * nb] = pk
            return emit

        def mixer():
            piece = lambda k: pcur[:, k * D:(k + 1) * D].astype(F32)
            xa = piece(0)
            rows = _rows(ts, D)
            taps = _conv_taps(jnp.concatenate([cxa[...], xa], axis=0), hl, ts, 4)
            xc = cb_ref[...] + sum(cw_ref[k:k + 1, :] * taps[k] for k in range(4))
            r, ig = _lru_gates(xc, wa_ref, ba_ref[...], wx_ref, bx_ref[...])
            a, m, _ = _lru_decay(r, _softplus_neg(lam_ref[...]), (rows == 0) & (i == 1))
            yield 0.26
            h = _run(_scan_fwd_steps(a, m * ig * xc, chh[hl - 1:hl, :]))
            yield 0.51
            gcp, v = piece(3), piece(4)
            z = gcp * v
            ztaps = _conv_taps(jnp.concatenate([czz[...], z], axis=0), hl, ts, 3)
            yb = piece(2) * sum(sw_ref[k:k + 1, :] * ztaps[k] for k in range(3))
            ga, gb = piece(1), piece(5)
            y = jnp.concatenate([h * (ga * _sigmoid(ga)), yb * (gb * _sigmoid(gb))], axis=1).astype(BF16)
            yield 0.76
            y_ref[...] = y
            x1_ref[...] = xb_ref[...] + gate_ref[...] * jnp.dot(y, wo_ref[...], preferred_element_type=F32)
            h_ref[...] = h.astype(BF16)
            cxa[...] = xa[ts - hl:, :]
            czz[...] = z[ts - hl:, :]
            chh[...] = jnp.where(i > 0, h[ts - hl:, :], 0.0)

        _paired(mixer(), [project(k) for k in range(N_CHIP)])

    def full(a):
        return pl.BlockSpec(a.shape, lambda i: (0,) * a.ndim)

    ahead = lambda w: pl.BlockSpec((ts, w), lambda i: (jnp.minimum(i, n_t - 1), 0))
    behind = lambda w: pl.BlockSpec((ts, w), lambda i: (jnp.maximum(i - 1, 0), 0))
    args = (x, x, g, sc, sh, w_in, gate, cw, cb, wa, ba, wx, bx, lam, sw, wo)
    return pl.pallas_call(
        body, name="l0_fwd", grid=(n_t + 1,),
        in_specs=[ahead(D), behind(D)] + [full(a) for a in args[2:]],
        out_specs=[behind(D), behind(D), behind(2 * D), ahead(D), ahead(N_CHIP * nb)],
        out_shape=[jax.ShapeDtypeStruct((s_len, D), F32), jax.ShapeDtypeStruct((s_len, D), BF16),
                   jax.ShapeDtypeStruct((s_len, 2 * D), BF16), jax.ShapeDtypeStruct((s_len, D), BF16),
                   jax.ShapeDtypeStruct((s_len, N_CHIP * nb), BF16)],
        scratch_shapes=[pltpu.VMEM((ts, N_CHIP * nb), BF16)] * 2 + [pltpu.VMEM((hl, D), F32)] * 3,
        compiler_params=_cp(("arbitrary",)),
    )(*args)


def _l1_mix(proj, x1, tgt, gate, wg, bg, scale, wo, gf):
    s_len = x1.shape[0]
    ts = _tile(s_len, TS_MIX)
    pw, gd, hl = 2 * D, POOL_GROUP_DIM, POOL_HALO

    def body(p_ref, x_ref, t_ref, gate_ref, wg_ref, bg_ref, sc_ref, wo_ref, gf_ref,
             d_ref, mx_ref, y_ref, dx_ref, loss_ref, dgf_ref, cv):
        i = pl.program_id(0)

        @pl.when(i == 0)
        def _():
            cv[...] = jnp.zeros_like(cv)
            loss_ref[...] = jnp.zeros_like(loss_ref)
            dgf_ref[...] = jnp.zeros_like(dgf_ref)

        v = p_ref[:, 0:pw].astype(F32)
        gg = p_ref[:, pw:2 * pw].astype(F32)
        sums = _window_sums(jnp.concatenate([cv[...], v], axis=0), _down)
        inv = _pool_inv_counts(i * ts, ts)
        dd = [sums[k][hl:hl + ts] * inv[k] - v[:, k * gd:(k + 1) * gd] for k in range(4)]
        mixed = jnp.concatenate(
            [jnp.dot(dd[k].astype(BF16), wg_ref[k], preferred_element_type=F32) for k in range(4)], axis=1) + bg_ref[...]
        d_ref[...] = jnp.concatenate(dd, axis=1).astype(BF16)
        mx_ref[...] = mixed.astype(BF16)
        y = (mixed * sc_ref[...] * (gg * _sigmoid(gg))).astype(BF16)
        y_ref[...] = y
        x2 = x_ref[...] + gate_ref[...] * jnp.dot(y, wo_ref[...], preferred_element_type=F32)
        r2 = lax.rsqrt(jnp.mean(x2 * x2, axis=-1, keepdims=True) + RMS_EPS)
        n2 = x2 * r2
        err = n2 * gf_ref[...] - t_ref[...]
        loss_ref[...] += jnp.sum(err * err, axis=0, keepdims=True)
        dyf = err * (1.0 / D)
        dgf_ref[...] += jnp.sum(dyf * n2, axis=0, keepdims=True)
        dn = dyf * gf_ref[...]
        dx_ref[...] = r2 * (dn - n2 * jnp.mean(dn * n2, axis=-1, keepdims=True))
        cv[...] = v[ts - hl:, :]

    def full(a):
        return pl.BlockSpec(a.shape, lambda i: (0,) * a.ndim)

    row = lambda w: pl.BlockSpec((ts, w), lambda i: (i, 0))
    acc = pl.BlockSpec((1, D), lambda i: (0, 0))
    return pl.pallas_call(
        body, name="l1_mix", grid=(s_len // ts,),
        in_specs=[row(2 * pw), row(D), row(D)] + [full(a) for a in (gate, wg, bg, scale, wo, gf)],
        out_specs=[row(pw), row(pw), row(pw), row(D), acc, acc],
        out_shape=[jax.ShapeDtypeStruct((s_len, pw), BF16)] * 3 + [jax.ShapeDtypeStruct((s_len, D), F32)]
        + [jax.ShapeDtypeStruct((1, D), F32)] * 2,
        scratch_shapes=[pltpu.VMEM((hl, pw), F32)],
        compiler_params=_cp(("arbitrary",)),
    )(proj, x1, tgt, gate, wg, bg, scale, wo, gf)


def _l1_bwd_mix(dx2, proj, mixed, gate, wg, scale, wo):
    s_len = dx2.shape[0]
    ts = _tile(s_len, TS_MIX)
    n_t = s_len // ts
    pw, gd, hl = 2 * D, POOL_GROUP_DIM, POOL_HALO

    def body(dx_ref, gg_ref, mx_ref, gate_ref, wg_ref, sc_ref, wo_ref, dp_ref, dmx_ref, dsc_ref, dbg_ref, cq):
        i = pl.program_id(0)

        @pl.when(i == 0)
        def _():
            cq[...] = jnp.zeros_like(cq)
            dsc_ref[...] = jnp.zeros_like(dsc_ref)
            dbg_ref[...] = jnp.zeros_like(dbg_ref)

        dy = lax.dot_general((gate_ref[...] * dx_ref[...]).astype(BF16), wo_ref[...], NT, preferred_element_type=F32)
        gg = gg_ref[...].astype(F32)
        mixed = mx_ref[...].astype(F32)
        s = _sigmoid(gg)
        sg = gg * s
        dmixed = dy * sc_ref[...] * sg
        dsc_ref[...] += jnp.sum(dy * mixed * sg, axis=0, keepdims=True)
        dbg_ref[...] += jnp.sum(dmixed, axis=0, keepdims=True)
        dmb = dmixed.astype(BF16)
        dmx_ref[...] = dmb
        dp_ref[:, pw:2 * pw] = (dy * sc_ref[...] * mixed * (s * (1.0 + gg * (1.0 - s)))).astype(BF16)
        inv = _pool_inv_counts((n_t - 1 - i) * ts, ts)
        dd = [lax.dot_general(dmb[:, k * gd:(k + 1) * gd], wg_ref[k], NT, preferred_element_type=F32) for k in range(4)]
        q = jnp.concatenate([dd[k] * inv[k] for k in range(4)], axis=1)
        sums = _window_sums(jnp.concatenate([q, cq[...]], axis=0), _up)
        dp_ref[:, 0:pw] = jnp.concatenate([sums[k][0:ts] - dd[k] for k in range(4)], axis=1).astype(BF16)
        cq[...] = q[0:hl, :]

    def full(a):
        return pl.BlockSpec(a.shape, lambda i: (0,) * a.ndim)

    rev = lambda w, j=0: pl.BlockSpec((ts, w), lambda i: (n_t - 1 - i, j))
    acc = pl.BlockSpec((1, pw), lambda i: (0, 0))
    return pl.pallas_call(
        body, name="l1_bwd_mix", grid=(n_t,),
        in_specs=[rev(D), rev(pw, 1), rev(pw)] + [full(a) for a in (gate, wg, scale, wo)],
        out_specs=[rev(2 * pw), rev(pw), acc, acc],
        out_shape=[jax.ShapeDtypeStruct((s_len, 2 * pw), BF16), jax.ShapeDtypeStruct((s_len, pw), BF16),
                   jax.ShapeDtypeStruct((1, pw), F32), jax.ShapeDtypeStruct((1, pw), F32)],
        scratch_shapes=[pltpu.VMEM((hl, pw), F32)],
        compiler_params=_cp(("arbitrary",)),
    )(dx2, proj, mixed, gate, wg, scale, wo)


def _l0_bwd_mix(dx1, proj, hst, gate, cw, cb, wa, ba, wx, bx, lam, sw, wo):
    s_len = dx1.shape[0]
    ts = _tile(s_len, TS_MIX)
    n_t = s_len // ts
    hl, hb = SUBLANES, BF16_ROWS

    def body(dx_ref, p_ref, ph_ref, h_ref, hh_ref, gate_ref, cw_ref, cb_ref, wa_ref, ba_ref, wx_ref, bx_ref,
             lam_ref, sw_ref, wo_ref, dp_ref, xc_ref, dpa_ref, dpx_ref, sm_ref, cg, cdxc, cdcz, ca):
        i = pl.program_id(0)
        ri = n_t - 1 - i

        @pl.when(i == 0)
        def _():
            cg[...] = jnp.zeros_like(cg)
            ca[...] = jnp.zeros_like(ca)
            cdxc[...] = jnp.zeros_like(cdxc)
            cdcz[...] = jnp.zeros_like(cdcz)
            sm_ref[...] = jnp.zeros_like(sm_ref)

        has_prev = (ri > 0).astype(F32)
        xa, ga, gbp, gcp, v, gb = [p_ref[:, k * D:(k + 1) * D].astype(F32) for k in range(6)]
        prev = lambda k: ph_ref[:, k * D:(k + 1) * D].astype(F32)[hb - hl:hb] * has_prev
        rows = _rows(ts, D)
        first = (rows == 0) & (ri == 0)
        xtaps = _conv_taps(jnp.concatenate([prev(0), xa], axis=0), hl, ts, 4)
        xc = cb_ref[...] + sum(cw_ref[k:k + 1, :] * xtaps[k] for k in range(4))
        r, ig = _lru_gates(xc, wa_ref, ba_ref[...], wx_ref, bx_ref[...])
        sp = _softplus_neg(lam_ref[...])
        a, m, inv_m = _lru_decay(r, sp, first)
        z = gcp * v
        ztaps = _conv_taps(jnp.concatenate([prev(3) * prev(4), z], axis=0), hl, ts, 3)
        cz = sum(sw_ref[k:k + 1, :] * ztaps[k] for k in range(3))
        h = h_ref[...].astype(F32)
        hprev = _down(jnp.concatenate([hh_ref[...].astype(F32)[hb - hl:hb] * has_prev, h], axis=0), 1)[hl:hl + ts]
        dy = lax.dot_general((gate_ref[...] * dx_ref[...]).astype(BF16), wo_ref[...], NT, preferred_element_type=F32)
        dya_pre, dyb_pre = dy[:, 0:D], dy[:, D:2 * D]
        s_a, s_b = _sigmoid(ga), _sigmoid(gb)
        dp_ref[:, D:2 * D] = (dya_pre * h * (s_a * (1.0 + ga * (1.0 - s_a)))).astype(BF16)
        dp_ref[:, 5 * D:6 * D] = (dyb_pre * (gbp * cz) * (s_b * (1.0 + gb * (1.0 - s_b)))).astype(BF16)
        dya = dya_pre * (ga * s_a)
        dyb = dyb_pre * (gb * s_b)
        dp_ref[:, 2 * D:3 * D] = (dyb * cz).astype(BF16)
        dcz = dyb * gbp
        for k in range(3):
            sm_ref[8 + k:9 + k, :] += jnp.sum(dcz * ztaps[k], axis=0, keepdims=True)
        dcz_ext = jnp.concatenate([dcz, cdcz[...]], axis=0)
        dz = sum(sw_ref[k:k + 1, :] * _up(dcz_ext, 2 - k)[0:ts] for k in range(3))
        dp_ref[:, 3 * D:4 * D] = (dz * v).astype(BF16)
        dp_ref[:, 4 * D:5 * D] = (dz * gcp).astype(BF16)
        cdcz[...] = dcz[0:hl, :]
        alpha = _up(jnp.concatenate([a, ca[...]], axis=0), 1)[0:ts]
        dh = _run(_scan_rev_steps(alpha, dya, cg[0:1, :]))
        cg[...] = dh[0:hl, :]
        ca[...] = a[0:hl, :]
        da = dh * hprev
        dm = dh * ig * xc
        di = dh * m * xc
        dxc = dh * m * ig
        dl = da * a - jnp.where(first, 0.0, dm * (a * a) * inv_m)
        sm_ref[7:8, :] += jnp.sum(dl * r, axis=0, keepdims=True) * (-LRU_C)
        dpa = (dl * sp) * (-LRU_C) * r * (1.0 - r)
        dpx = di * ig * (1.0 - ig)
        sm_ref[5:6, :] += jnp.sum(dpa, axis=0, keepdims=True)
        sm_ref[6:7, :] += jnp.sum(dpx, axis=0, keepdims=True)
        dpa_b, dpx_b = dpa.astype(BF16), dpx.astype(BF16)
        dpa_ref[...] = dpa_b
        dpx_ref[...] = dpx_b
        xc_ref[...] = xc.astype(BF16)
        back = []
        for hd in range(LRU_HEADS):
            sl = slice(hd * LRU_HEAD_DIM, (hd + 1) * LRU_HEAD_DIM)
            back.append(lax.dot_general(dpa_b[:, sl], wa_ref[hd], NT, preferred_element_type=F32)
                        + lax.dot_general(dpx_b[:, sl], wx_ref[hd], NT, preferred_element_type=F32))
        dxc = dxc + jnp.concatenate(back, axis=1)
        sm_ref[4:5, :] += jnp.sum(dxc, axis=0, keepdims=True)
        for k in range(4):
            sm_ref[k:k + 1, :] += jnp.sum(dxc * xtaps[k], axis=0, keepdims=True)
        dxc_ext = jnp.concatenate([dxc, cdxc[...]], axis=0)
        dp_ref[:, 0:D] = sum(cw_ref[k:k + 1, :] * _up(dxc_ext, 3 - k)[0:ts] for k in range(4)).astype(BF16)
        cdxc[...] = dxc[0:hl, :]

    def full(a):
        return pl.BlockSpec(a.shape, lambda i: (0,) * a.ndim)

    rev = lambda w: pl.BlockSpec((ts, w), lambda i: (n_t - 1 - i, 0))
    halo = lambda w: pl.BlockSpec((hb, w), lambda i: (jnp.maximum((n_t - 1 - i) * (ts // hb) - 1, 0), 0))
    return pl.pallas_call(
        body, name="l0_bwd_mix", grid=(n_t,),
        in_specs=[rev(D), rev(6 * D), halo(6 * D), rev(D), halo(D)]
        + [full(a) for a in (gate, cw, cb, wa, ba, wx, bx, lam, sw, wo)],
        out_specs=[rev(6 * D), rev(D), rev(D), rev(D), pl.BlockSpec((2 * SUBLANES, D), lambda i: (0, 0))],
        out_shape=[jax.ShapeDtypeStruct((s_len, 6 * D), BF16)] + [jax.ShapeDtypeStruct((s_len, D), BF16)] * 3
        + [jax.ShapeDtypeStruct((2 * SUBLANES, D), F32)],
        scratch_shapes=[pltpu.VMEM((hl, D), F32)] * 4,
        compiler_params=_cp(("arbitrary",)),
    )(dx1, proj, proj, hst, hst, gate, cw, cb, wa, ba, wx, bx, lam, sw, wo)


def _dgrad_norm(dproj, w, x, dres, g, sc, name):
    s_len, nb = x.shape[0], w.shape[2]
    ts = _tile(s_len, TS_DGRAD)

    def body(dp_ref, w_ref, x_ref, dr_ref, g_ref, sc_ref, dx_ref, s1_ref, s2_ref):
        @pl.when(pl.program_id(0) == 0)
        def _():
            s1_ref[...] = jnp.zeros_like(s1_ref)
            s2_ref[...] = jnp.zeros_like(s2_ref)

        dh = sum(lax.dot_general(dp_ref[:, k * nb:(k + 1) * nb], w_ref[k], NT, preferred_element_type=F32)
                 for k in range(N_CHIP))
        xv = x_ref[...]
        r = lax.rsqrt(jnp.mean(xv * xv, axis=-1, keepdims=True) + RMS_EPS)
        n = xv * r
        s1_ref[...] += jnp.sum(dh, axis=0, keepdims=True)
        s2_ref[...] += jnp.sum(dh * n, axis=0, keepdims=True)
        dn = dh * (g_ref[...] * (1.0 + sc_ref[...]))
        dx_ref[...] = dr_ref[...] + r * (dn - n * jnp.mean(dn * n, axis=-1, keepdims=True))

    row = lambda wd: pl.BlockSpec((ts, wd), lambda i: (i, 0))
    vec = pl.BlockSpec((1, D), lambda i: (0, 0))
    return pl.pallas_call(
        body, name=name, grid=(s_len // ts,),
        in_specs=[row(N_CHIP * nb), pl.BlockSpec(w.shape, lambda i: (0, 0, 0)), row(D), row(D), vec, vec],
        out_specs=[row(D), vec, vec],
        out_shape=[jax.ShapeDtypeStruct((s_len, D), F32)] + [jax.ShapeDtypeStruct((1, D), F32)] * 2,
        compiler_params=_cp(("arbitrary",)),
    )(dproj, w, x, dres, g, sc)


def _wgrad(a, b, groups, ka, nb, a_col, b_col, name):
    s_len = a.shape[0]
    ts = _tile(s_len, TS_WGRAD)

    def body(a_ref, b_ref, o_ref):
        @pl.when(pl.program_id(1) == 0)
        def _():
            o_ref[...] = jnp.zeros_like(o_ref)

        o_ref[...] += lax.dot_general(a_ref[...].astype(BF16), b_ref[...].astype(BF16), TN, preferred_element_type=F32)

    return pl.pallas_call(
        body, name=name, grid=(groups, s_len // ts),
        in_specs=[pl.BlockSpec((ts, ka), lambda g, s: (s, a_col(g))), pl.BlockSpec((ts, nb), lambda g, s: (s, b_col(g)))],
        out_specs=pl.BlockSpec((None, ka, nb), lambda g, s: (g, 0, 0)),
        out_shape=jax.ShapeDtypeStruct((groups, ka, nb), F32),
        compiler_params=_cp(("parallel", "arbitrary")),
    )(a, b)


def _wgrad_rows(a, b, groups, name):
    s_len, nb = b.shape
    ka = a.shape[1] // groups
    ts = _tile(s_len, TS_WGRAD)

    def body(a_ref, b_ref, o_ref):
        @pl.when(pl.program_id(0) == 0)
        def _():
            o_ref[...] = jnp.zeros_like(o_ref)

        bb = b_ref[...].astype(BF16)
        for g in range(groups):
            o_ref[g] += lax.dot_general(a_ref[:, g * ka:(g + 1) * ka], bb, TN, preferred_element_type=F32)

    return pl.pallas_call(
        body, name=name, grid=(s_len // ts,),
        in_specs=[pl.BlockSpec((ts, groups * ka), lambda s: (s, 0)), pl.BlockSpec((ts, nb), lambda s: (s, 0))],
        out_specs=pl.BlockSpec((groups, ka, nb), lambda s: (0, 0, 0)),
        out_shape=jax.ShapeDtypeStruct((groups, ka, nb), F32),
        compiler_params=_cp(("arbitrary",)),
    )(a, b)


def _wgrad_heads(xc, dpa, dpx):
    s_len = xc.shape[0]
    ts = _tile(s_len, TS_WGRAD)
    hd = LRU_HEAD_DIM

    def body(x_ref, a_ref, b_ref, oa_ref, ob_ref):
        @pl.when(pl.program_id(0) == 0)
        def _():
            oa_ref[...] = jnp.zeros_like(oa_ref)
            ob_ref[...] = jnp.zeros_like(ob_ref)

        for h in range(LRU_HEADS):
            sl = slice(h * hd, (h + 1) * hd)
            oa_ref[h] += lax.dot_general(x_ref[:, sl], a_ref[:, sl], TN, preferred_element_type=F32)
            ob_ref[h] += lax.dot_general(x_ref[:, sl], b_ref[:, sl], TN, preferred_element_type=F32)

    row = pl.BlockSpec((ts, D), lambda s: (s, 0))
    acc = pl.BlockSpec((LRU_HEADS, hd, hd), lambda s: (0, 0, 0))
    return pl.pallas_call(
        body, name="l0_wgrad_heads", grid=(s_len // ts,), in_specs=[row] * 3, out_specs=[acc] * 2,
        out_shape=[jax.ShapeDtypeStruct((LRU_HEADS, hd, hd), F32)] * 2,
        compiler_params=_cp(("arbitrary",)),
    )(xc, dpa, dpx)


def _wo_final(mt, wo, gate, name):
    rb = mt.shape[1]

    def body(m_ref, w_ref, gate_ref, dw_ref, dg_ref):
        @pl.when(pl.program_id(0) == 0)
        def _():
            dg_ref[...] = jnp.zeros_like(dg_ref)

        mv = m_ref[...]
        dw_ref[...] = mv * gate_ref[...]
        dg_ref[...] += jnp.sum(mv * w_ref[...].astype(F32), axis=0, keepdims=True)

    blk = pl.BlockSpec((None, rb, D), lambda k: (k, 0, 0))
    vec = pl.BlockSpec((1, D), lambda k: (0, 0))
    return pl.pallas_call(
        body, name=name, grid=(N_CHIP,), in_specs=[blk, blk, vec], out_specs=[blk, vec],
        out_shape=[jax.ShapeDtypeStruct(mt.shape, F32), jax.ShapeDtypeStruct((1, D), F32)],
        compiler_params=_cp(("arbitrary",)),
    )(mt, wo, gate)


ROW_NORM_G, ROW_CONV_W, ROW_CONV_B, ROW_B_A, ROW_B_X, ROW_LAMBDA, ROW_SC_W, ROW_POOL_B, ROW_POOL_S, ROW_FINAL_G = (
    0, 2, 6, 7, 8, 9, 10, 13, 15, 17)


def _small_pack(s1_0, s2_0, s1_1, s2_1, sm0, dsc1, dbg1, dgf, losscols, dgate0, dgate1, norm_g, sc0, sc1, lam):
    def body(s1_0r, s2_0r, s1_1r, s2_1r, sm, dsc, dbg, dgfr, lcols, dg0, dg1, ng, sc0r, sc1r, lamr, buf, dmod, loss):
        buf[...] = jnp.zeros_like(buf)
        buf[0:1, :] = s2_0r[...] * (1.0 + sc0r[...])
        buf[1:2, :] = s2_1r[...] * (1.0 + sc1r[...])
        buf[ROW_CONV_W:ROW_CONV_W + 4, :] = sm[0:4, :]
        buf[ROW_CONV_B:ROW_CONV_B + 1, :] = sm[4:5, :]
        buf[ROW_B_A:ROW_B_A + 1, :] = sm[5:6, :]
        buf[ROW_B_X:ROW_B_X + 1, :] = sm[6:7, :]
        buf[ROW_LAMBDA:ROW_LAMBDA + 1, :] = -sm[7:8, :] * _sigmoid(-lamr[...])
        buf[ROW_SC_W:ROW_SC_W + 3, :] = sm[8:11, :]
        for k in range(2):
            buf[ROW_POOL_B + k:ROW_POOL_B + k + 1, :] = dbg[:, k * D:(k + 1) * D]
            buf[ROW_POOL_S + k:ROW_POOL_S + k + 1, :] = dsc[:, k * D:(k + 1) * D]
        buf[ROW_FINAL_G:ROW_FINAL_G + 1, :] = dgfr[...]
        pieces = (s1_0r[...], s2_0r[...] * ng[0:1, :], dg0[...], s1_1r[...], s2_1r[...] * ng[1:2, :], dg1[...])
        for k, pc in enumerate(pieces):
            dmod[:, k * D:(k + 1) * D] = jnp.broadcast_to(pc, (SUBLANES, D))
        loss[...] = jnp.broadcast_to(jnp.sum(lcols[...], axis=1, keepdims=True) * (0.5 / D), loss.shape)

    args = (s1_0, s2_0, s1_1, s2_1, sm0, dsc1, dbg1, dgf, losscols, dgate0, dgate1, norm_g, sc0, sc1, lam)
    return pl.pallas_call(
        body, name="small_pack", in_specs=[VMEM] * len(args), out_specs=[VMEM] * 3,
        out_shape=[jax.ShapeDtypeStruct((SMALL_ROWS, D), F32), jax.ShapeDtypeStruct((SUBLANES, 6 * D), F32),
                   jax.ShapeDtypeStruct((SUBLANES, 128), F32)],
        compiler_params=_cp(),
    )(*args)


def _small_comm(buf_a, buf_b, dmod8):
    ra, rb = buf_a.shape[0] // N_DEV, buf_b.shape[0] // N_DEV
    wb = buf_b.shape[1]

    def body(a_ref, b_ref, dm_ref, oa_ref, ob_ref, odm_ref, ina, inb, dslot, sa, sb, s1, r1, s2, r2):
        x, y, c = _pos()
        me = 4 * x + 2 * y + c
        peers = []
        for r in range(1, N_DEV):
            fx, fy, fc = (r >> 2) & 1, (r >> 1) & 1, r & 1
            px, py, pc = _flip(x, fx), _flip(y, fy), _flip(c, fc)
            peers.append(((px, py, pc), 4 * px + 2 * py + pc))
        seg_a = lambda d: pl.ds(pl.multiple_of(d * ra, SUBLANES), ra)
        seg_b = lambda d: pl.ds(pl.multiple_of(d * rb, SUBLANES), rb)
        first = []
        for r, (peer, pid) in enumerate(peers):
            for k, (src, dst) in enumerate(((a_ref.at[seg_a(pid), :], ina.at[r]), (b_ref.at[seg_b(pid), :], inb.at[r]),
                                            (dm_ref, dslot.at[me]))):
                cp = pltpu.make_async_remote_copy(src_ref=src, dst_ref=dst, send_sem=s1.at[3 * r + k],
                                                  recv_sem=r1.at[3 * r + k], device_id=peer, device_id_type=MESH)
                cp.start()
                first.append(cp)
        dslot[me] = dm_ref[...]
        for cp in first:
            cp.wait()
        acc_a, acc_b = a_ref[seg_a(me), :], b_ref[seg_b(me), :]
        for r in range(N_DEV - 1):
            acc_a = acc_a + ina[r]
            acc_b = acc_b + inb[r]
        sa[...] = acc_a
        sb[...] = acc_b
        oa_ref[seg_a(me), :] = acc_a
        ob_ref[seg_b(me), :] = acc_b
        second = []
        for r, (peer, pid) in enumerate(peers):
            for k, (src, dst) in enumerate(((sa, oa_ref.at[seg_a(me), :]), (sb, ob_ref.at[seg_b(me), :]))):
                cp = pltpu.make_async_remote_copy(src_ref=src, dst_ref=dst, send_sem=s2.at[2 * r + k],
                                                  recv_sem=r2.at[2 * r + k], device_id=peer, device_id_type=MESH)
                cp.start()
                second.append(cp)
        rows = _rows(SUBLANES, dm_ref.shape[1])
        dm_all = jnp.zeros(dm_ref.shape, F32)
        for d in range(N_DEV):
            dm_all = jnp.where(rows == d, dslot[d], dm_all)
        odm_ref[...] = dm_all
        for cp in second:
            cp.wait()

    nrel = N_DEV - 1
    return pl.pallas_call(
        body, name="small_comm", in_specs=[VMEM] * 3, out_specs=[VMEM] * 3,
        out_shape=[jax.ShapeDtypeStruct(buf_a.shape, F32), jax.ShapeDtypeStruct(buf_b.shape, F32),
                   jax.ShapeDtypeStruct(dmod8.shape, F32)],
        scratch_shapes=[pltpu.VMEM((nrel, ra, D), F32), pltpu.VMEM((nrel, rb, wb), F32),
                        pltpu.VMEM((N_DEV,) + dmod8.shape, F32), pltpu.VMEM((ra, D), F32), pltpu.VMEM((rb, wb), F32),
                        pltpu.SemaphoreType.DMA((3 * nrel,)), pltpu.SemaphoreType.DMA((3 * nrel,)),
                        pltpu.SemaphoreType.DMA((2 * nrel,)), pltpu.SemaphoreType.DMA((2 * nrel,))],
        compiler_params=_cp(),
    )(buf_a, buf_b, dmod8)


def _adam(w, g, m, v):
    m2 = ADAM_B1 * m + (1.0 - ADAM_B1) * g
    v2 = ADAM_B2 * v + (1.0 - ADAM_B2) * (g * g)
    m_hat = m2 / (1.0 - ADAM_B1 ** ADAM_STEP)
    v_hat = v2 / (1.0 - ADAM_B2 ** ADAM_STEP)
    return -ADAM_LR * (m_hat / (jnp.sqrt(v_hat) + ADAM_EPS) + ADAM_WD * w), m2, v2


def _small_adam(red_a, red_b, dm_all, params):
    n = len(params)

    def body(*refs):
        ra, rb, dm = refs[:3]
        wmv = refs[3:3 + 3 * n]
        outs = refs[3 + 3 * n:]
        x, y, _ = _pos()
        chip = 2 * x + y

        def shard(row0, nrows, width):
            per_row = D // width
            cands = []
            for k in range(N_CHIP):
                if nrows == 1 or per_row >= N_CHIP:
                    cands.append(ra[row0:row0 + nrows, k * width:(k + 1) * width])
                else:
                    rr, cc = divmod(k * width, D)
                    cands.append(ra[row0 + rr:row0 + rr + 1, cc:cc + width])
            g = cands[0]
            for k in range(1, N_CHIP):
                g = jnp.where(chip == k, cands[k], g)
            return g

        dms = jnp.sum(dm[...], axis=0, keepdims=True)
        hw = LRU_HEADS * LRU_HEAD_DIM
        grads = [
            ra[ROW_NORM_G:ROW_NORM_G + 2, :],
            None,
            shard(ROW_CONV_W, 4, D // N_CHIP),
            ra[ROW_CONV_B:ROW_CONV_B + 1, :],
            rb[0:hw, :],
            ra[ROW_B_A:ROW_B_A + 1, :],
            rb[hw:2 * hw, :],
            ra[ROW_B_X:ROW_B_X + 1, :],
            ra[ROW_LAMBDA:ROW_LAMBDA + 1, :],
            shard(ROW_SC_W, 3, D // N_CHIP),
            shard(ROW_POOL_B, 2, 2 * D // N_CHIP),
            shard(ROW_POOL_S, 2, 2 * D // N_CHIP),
            ra[ROW_FINAL_G:ROW_FINAL_G + 1, :],
        ]
        for p in range(n):
            w_ref, m_ref, v_ref = wmv[3 * p:3 * p + 3]
            g_out, d_out, m_out, v_out = outs[4 * p:4 * p + 4]
            if grads[p] is None:
                for l in range(2):
                    g = dms[:, l * 3 * D:(l + 1) * 3 * D]
                    dl, m2, v2 = _adam(w_ref[l:l + 1, :], g, m_ref[l:l + 1, :], v_ref[l:l + 1, :])
                    g_out[l:l + 1, :] = g
                    d_out[l:l + 1, :] = dl
                    m_out[l:l + 1, :] = m2
                    v_out[l:l + 1, :] = v2
            else:
                g = grads[p]
                dl, m2, v2 = _adam(w_ref[...], g, m_ref[...], v_ref[...])
                g_out[...] = g
                d_out[...] = dl
                m_out[...] = m2
                v_out[...] = v2

    flat = [a for p in params for a in p]
    return pl.pallas_call(
        body, name="small_adam", in_specs=[VMEM] * (3 + len(flat)), out_specs=[VMEM] * (4 * n),
        out_shape=[jax.ShapeDtypeStruct(p[0].shape, F32) for p in params for _ in range(4)],
        compiler_params=_cp(),
    )(red_a, red_b, dm_all, *flat)


def _modw_adam(ca_t, dm_sh, w, m, v):
    nw = w.shape[2]

    def body(c_ref, d_ref, w_ref, m_ref, v_ref, g_out, d_out, m_out, v_out):
        g = jnp.dot(c_ref[...], d_ref[...], precision=lax.Precision.HIGHEST, preferred_element_type=F32)
        dl, m2, v2 = _adam(w_ref[...], g, m_ref[...], v_ref[...])
        g_out[...] = g
        d_out[...] = dl
        m_out[...] = m2
        v_out[...] = v2

    blk = pl.BlockSpec((None, D, nw), lambda l: (l, 0, 0))
    return pl.pallas_call(
        body, name="modw_adam", grid=(2,),
        in_specs=[pl.BlockSpec((D, SUBLANES), lambda l: (0, 0)), pl.BlockSpec((None, SUBLANES, nw), lambda l: (l, 0, 0)),
                  blk, blk, blk],
        out_specs=[blk] * 4, out_shape=[jax.ShapeDtypeStruct(w.shape, F32)] * 4,
        compiler_params=_cp(("arbitrary",)),
    )(ca_t, dm_sh, w, m, v)


def _half_rows(r):
    return r // 2


def _sib_send_halves(gs):
    n = len(gs)

    def body(*refs):
        ins, outs, ssem, rsem = refs[:n], refs[n:2 * n], refs[2 * n], refs[2 * n + 1]
        x, y, c = _pos()
        cps = []
        for a in range(n):
            hr = _half_rows(gs[a].shape[1])
            cp = pltpu.make_async_remote_copy(
                src_ref=ins[a].at[:, pl.ds(pl.multiple_of((1 - c) * hr, SUBLANES), hr), :], dst_ref=outs[a],
                send_sem=ssem.at[a], recv_sem=rsem.at[a], device_id=(x, y, 1 - c), device_id_type=MESH)
            cp.start()
            cps.append(cp)
        for cp in cps:
            cp.wait()

    return pl.pallas_call(
        body, name="grad_sib_halves", in_specs=[ANY] * n, out_specs=[ANY] * n,
        out_shape=[jax.ShapeDtypeStruct((N_CHIP, _half_rows(g.shape[1]), g.shape[2]), F32) for g in gs],
        scratch_shapes=[pltpu.SemaphoreType.DMA((n,)), pltpu.SemaphoreType.DMA((n,))],
        compiler_params=_cp(),
    )(*gs)


def _add_half(g, got, cidx, name):
    _, hr, cc = got.shape
    rb = min(hr, 256)

    def body(c_ref, g_ref, r_ref, o_ref):
        o_ref[...] = (g_ref[...] + r_ref[...]).astype(o_ref.dtype)

    blk = pl.BlockSpec((None, rb, cc), lambda k, j, c_ref: (k, j, 0))
    return pl.pallas_call(
        body, name=name,
        grid_spec=pltpu.PrefetchScalarGridSpec(
            num_scalar_prefetch=1, grid=(N_CHIP, hr // rb),
            in_specs=[pl.BlockSpec((None, rb, cc), lambda k, j, c_ref: (k, c_ref[0] * (hr // rb) + j, 0)), blk],
            out_specs=blk),
        out_shape=jax.ShapeDtypeStruct(got.shape, GRAD_WIRE_DTYPE),
        compiler_params=_cp(("parallel", "parallel")),
    )(cidx, g, got)


def _chip_scatter(ps):
    n = len(ps)

    def body(*refs):
        ins, outs, ssem, rsem = refs[:n], refs[n:2 * n], refs[2 * n], refs[2 * n + 1]
        x, y, c = _pos()
        cps = []
        for a in range(n):
            for q, (fx, fy) in enumerate(((1, 0), (0, 1), (1, 1))):
                px, py = _flip(x, fx), _flip(y, fy)
                cp = pltpu.make_async_remote_copy(
                    src_ref=ins[a].at[2 * px + py], dst_ref=outs[a].at[q],
                    send_sem=ssem.at[3 * a + q], recv_sem=rsem.at[3 * a + q], device_id=(px, py, c), device_id_type=MESH)
                cp.start()
                cps.append(cp)
        for cp in cps:
            cp.wait()

    return pl.pallas_call(
        body, name="grad_chip_scatter", in_specs=[ANY] * n, out_specs=[ANY] * n,
        out_shape=[jax.ShapeDtypeStruct((N_CHIP - 1,) + p.shape[1:], p.dtype) for p in ps],
        scratch_shapes=[pltpu.SemaphoreType.DMA((3 * n,)), pltpu.SemaphoreType.DMA((3 * n,))],
        compiler_params=_cp(),
    )(*ps)


def _add_owner(p, got, chipidx, name):
    _, hr, cc = p.shape
    rb = min(hr, 256)

    def body(k_ref, p_ref, r_ref, o_ref):
        o_ref[...] = ((p_ref[...].astype(F32) + r_ref[0].astype(F32)) + r_ref[1].astype(F32)) + r_ref[2].astype(F32)

    return pl.pallas_call(
        body, name=name,
        grid_spec=pltpu.PrefetchScalarGridSpec(
            num_scalar_prefetch=1, grid=(hr // rb,),
            in_specs=[pl.BlockSpec((None, rb, cc), lambda j, k_ref: (k_ref[0], j, 0)),
                      pl.BlockSpec((N_CHIP - 1, rb, cc), lambda j, k_ref: (0, j, 0))],
            out_specs=pl.BlockSpec((rb, cc), lambda j, k_ref: (j, 0))),
        out_shape=jax.ShapeDtypeStruct((hr, cc), F32),
        compiler_params=_cp(("parallel",)),
    )(chipidx, p, got)


def _sib_exchange(ts_):
    n = len(ts_)

    def body(*refs):
        ins, outs, ssem, rsem = refs[:n], refs[n:2 * n], refs[2 * n], refs[2 * n + 1]
        x, y, c = _pos()
        cps = []
        for a in range(n):
            cp = pltpu.make_async_remote_copy(src_ref=ins[a], dst_ref=outs[a], send_sem=ssem.at[a],
                                              recv_sem=rsem.at[a], device_id=(x, y, 1 - c), device_id_type=MESH)
            cp.start()
            cps.append(cp)
        for cp in cps:
            cp.wait()

    return pl.pallas_call(
        body, name="grad_sib_exchange", in_specs=[ANY] * n, out_specs=[ANY] * n,
        out_shape=[jax.ShapeDtypeStruct(t.shape, F32) for t in ts_],
        scratch_shapes=[pltpu.SemaphoreType.DMA((n,))] * 2,
        compiler_params=_cp(),
    )(*ts_)


def _adam_2d(w, g_own, g_sib, m, v, cidx, name):
    rr, cc = w.shape
    hr = rr // 2
    rb = min(hr, 256)
    nb = hr // rb

    def body(c_ref, w_ref, go_ref, gs_ref, m_ref, v_ref, g_out, d_out, m_out, v_out):
        g = jnp.where(pl.program_id(0) == c_ref[0], go_ref[...], gs_ref[...])
        dl, m2, v2 = _adam(w_ref[...], g, m_ref[...], v_ref[...])
        g_out[...] = g
        d_out[...] = dl
        m_out[...] = m2
        v_out[...] = v2

    blk = pl.BlockSpec((rb, cc), lambda h, j, c_ref: (h * nb + j, 0))
    hblk = pl.BlockSpec((rb, cc), lambda h, j, c_ref: (j, 0))
    return pl.pallas_call(
        body, name=name,
        grid_spec=pltpu.PrefetchScalarGridSpec(
            num_scalar_prefetch=1, grid=(2, nb), in_specs=[blk, hblk, hblk, blk, blk], out_specs=[blk] * 4),
        out_shape=[jax.ShapeDtypeStruct((rr, cc), F32)] * 4, compiler_params=_cp(("parallel", "parallel")),
    )(cidx, w, g_own, g_sib, m, v)


def kernel(x, c, norm_g, mod_w, mod_b, hy_w_in, hy_conv_w, hy_conv_b, lru_w_a, lru_b_a, lru_w_x, lru_b_x, lru_lambda, sc_conv_w, hy_w_out, pool_w_in, pool_w_grp, pool_b_grp, pool_scale, pool_w_out, final_g, loss_target, m_norm_g, m_mod_w, m_mod_b, m_hy_w_in, m_hy_conv_w, m_hy_conv_b, m_lru_w_a, m_lru_b_a, m_lru_w_x, m_lru_b_x, m_lru_lambda, m_sc_conv_w, m_hy_w_out, m_pool_w_in, m_pool_w_grp, m_pool_b_grp, m_pool_scale, m_pool_w_out, m_final_g, v_norm_g, v_mod_w, v_mod_b, v_hy_w_in, v_hy_conv_w, v_hy_conv_b, v_lru_w_a, v_lru_b_a, v_lru_w_x, v_lru_b_x, v_lru_lambda, v_sc_conv_w, v_hy_w_out, v_pool_w_in, v_pool_w_grp, v_pool_b_grp, v_pool_scale, v_pool_w_out, v_final_g):
    ax, ay, ac = _pos()
    me = 4 * ax + 2 * ay + ac
    chip = 2 * ax + ay
    xs = x[0]
    tgt = loss_target[0]
    gd = POOL_GROUP_DIM

    ca_all, mod_all, small_w = _mod_fwd(jnp.broadcast_to(c, (SUBLANES, D)), mod_w, mod_b,
                                        hy_conv_w[0], sc_conv_w[0], pool_b_grp, pool_scale)
    mod_me = lax.dynamic_index_in_dim(mod_all, me, axis=1, keepdims=False)
    sh0, sc0, gt0 = (mod_me[0:1, k * D:(k + 1) * D] for k in range(3))
    sh1, sc1, gt1 = (mod_me[1:2, k * D:(k + 1) * D] for k in range(3))
    cw = small_w[SW_CONV:SW_CONV + 4, 0:D]
    sw = small_w[SW_SC:SW_SC + 3, 0:D]
    pool_b = small_w[SW_POOL_B:SW_POOL_B + 1, :]
    pool_s = small_w[SW_POOL_S:SW_POOL_S + 1, :]
    g0, g1, gf = norm_g[0:1], norm_g[1:2], final_g.reshape(1, D)
    cb, ba, bx, lam = hy_conv_b, lru_b_a, lru_b_x, lru_lambda

    big = [hy_w_in[0], hy_w_out[0], pool_w_in[0], pool_w_grp[0].reshape(4 * 128, gd), pool_w_out[0]]
    cidx = ac.reshape(1).astype(jnp.int32)
    kidx = chip.reshape(1).astype(jnp.int32)
    w_in0, w_out0, w_in1, w_grp, w_out1 = _wgather(
        [_wcast_own_block(w, kidx, f"wcast_own_block_{a}") for a, w in enumerate(big)])
    w_grp =w_grp.reshape(N_CHIP, 4, 128, gd).transpose(1, 0, 2, 3).reshape(4, gd, gd)
    wa_b, wx_b = _wcast([lru_w_a[0], lru_w_x[0]])

    x1, hst, y0, h0, proj0 = _l0_fwd(xs, g0, sc0, sh0, w_in0, gt0, cw, cb, wa_b, ba, wx_b, bx, lam, sw,
                                     w_out0.reshape(2 * D, D))
    h1, proj1 = _norm_proj(x1, g1, sc1, sh1, w_in1, "l1_proj")
    dpool, mixed, y1, dx2, losscols, dgf = _l1_mix(proj1, x1, tgt, gt1, w_grp, pool_b, pool_s,
                                                    w_out1.reshape(2 * D, D), gf)

    dproj1, dmixed, dsc1, dbg1 = _l1_bwd_mix(dx2, proj1, mixed, gt1, w_grp, pool_s, w_out1.reshape(2 * D, D))
    mt1 = _wgrad_rows(y1, dx2, N_CHIP, "l1_wgrad_out")
    d_wgrp = _wgrad(dpool, dmixed, 4, gd, gd, lambda g: g, lambda g: g, "l1_wgrad_grp")
    d_win1 = _wgrad(h1, dproj1, N_CHIP, D, D, lambda g: 0, lambda g: g, "l1_wgrad_in")
    dx1, s1_1, s2_1 = _dgrad_norm(dproj1, w_in1, x1, dx2, g1, sc1, "l1_bwd_proj")
    d_wout1, dgate1 = _wo_final(mt1, w_out1, gt1, "l1_wo_final")

    dproj0, xc, dpa, dpx, sm0 = _l0_bwd_mix(dx1, proj0, hst, gt0, cw, cb, wa_b, ba, wx_b, bx, lam, sw,
                                            w_out0.reshape(2 * D, D))
    mt0 = _wgrad_rows(y0, dx1, N_CHIP, "l0_wgrad_out")
    d_win0 = _wgrad(h0, dproj0, N_CHIP, D, 6 * D // N_CHIP, lambda g: 0, lambda g: g, "l0_wgrad_in")
    d_wa, d_wx = _wgrad_heads(xc, dpa, dpx)
    grad_x, s1_0, s2_0 = _dgrad_norm(dproj0, w_in0, xs, dx1, g0, sc0, "l0_bwd_proj")
    d_wout0, dgate0 = _wo_final(mt0, w_out0, gt0, "l0_wo_final")

    buf_a, dmod8, loss8 = _small_pack(s1_0, s2_0, s1_1, s2_1, sm0, dsc1, dbg1, dgf, losscols, dgate0, dgate1,
                                      norm_g, sc0, sc1, lam)
    hw = LRU_HEADS * LRU_HEAD_DIM
    buf_b = jnp.concatenate([d_wa.reshape(hw, LRU_HEAD_DIM), d_wx.reshape(hw, LRU_HEAD_DIM)], axis=0)
    red_a, red_b, dm_all = _small_comm(buf_a, buf_b, dmod8)
    small = [(norm_g, m_norm_g, v_norm_g), (mod_b, m_mod_b, v_mod_b),
             (hy_conv_w[0], m_hy_conv_w[0], v_hy_conv_w[0]), (hy_conv_b, m_hy_conv_b, v_hy_conv_b),
             tuple(a.reshape(hw, LRU_HEAD_DIM) for a in (lru_w_a, m_lru_w_a, v_lru_w_a)),
             (lru_b_a, m_lru_b_a, v_lru_b_a),
             tuple(a.reshape(hw, LRU_HEAD_DIM) for a in (lru_w_x, m_lru_w_x, v_lru_w_x)),
             (lru_b_x, m_lru_b_x, v_lru_b_x), (lru_lambda, m_lru_lambda, v_lru_lambda),
             (sc_conv_w[0], m_sc_conv_w[0], v_sc_conv_w[0]), (pool_b_grp, m_pool_b_grp, v_pool_b_grp),
             (pool_scale, m_pool_scale, v_pool_scale),
             tuple(a.reshape(1, D) for a in (final_g, m_final_g, v_final_g))]
    small_names = ["norm_g", "mod_b", "hy_conv_w", "hy_conv_b", "lru_w_a", "lru_b_a", "lru_w_x", "lru_b_x",
                   "lru_lambda", "sc_conv_w", "pool_b_grp", "pool_scale", "final_g"]
    small_out = _small_adam(red_a, red_b, dm_all, small)
    res = {}
    shapes = dict(norm_g=norm_g, mod_b=mod_b, hy_conv_w=hy_conv_w, hy_conv_b=hy_conv_b, lru_w_a=lru_w_a, lru_b_a=lru_b_a,
                  lru_w_x=lru_w_x, lru_b_x=lru_b_x, lru_lambda=lru_lambda, sc_conv_w=sc_conv_w, pool_b_grp=pool_b_grp,
                  pool_scale=pool_scale, final_g=final_g)
    for p, nm in enumerate(small_names):
        res[nm] = tuple(o.reshape(shapes[nm].shape) for o in small_out[4 * p:4 * p + 4])

    nw = mod_w.shape[2]
    dm_sh = jnp.stack([lax.dynamic_slice_in_dim(dm_all[:, l * 3 * D:(l + 1) * 3 * D], chip * nw, nw, axis=1)
                       for l in range(2)])
    res["mod_w"] = tuple(_modw_adam(ca_all.T, dm_sh, mod_w, m_mod_w, v_mod_w))

    d_wgrp = d_wgrp.reshape(4, N_CHIP, 128, gd).transpose(1, 0, 2, 3).reshape(N_CHIP, 4 * 128, gd)
    grads = [d_win0, d_wout0, d_win1, d_wgrp, d_wout1]
    got = _sib_send_halves(grads)
    parts = [_add_half(g, r, cidx, f"grad_add_half_{a}") for a, (g, r) in enumerate(zip(grads, got))]
    got2 = _chip_scatter(parts)
    halves = [_add_owner(p, r, kidx, f"grad_add_owner_{a}") for a, (p, r) in enumerate(zip(parts, got2))]
    sib_halves = _sib_exchange(halves)
    big_names = ["hy_w_in", "hy_w_out", "pool_w_in", "pool_w_grp", "pool_w_out"]
    big_wmv = [(hy_w_in, m_hy_w_in, v_hy_w_in), (hy_w_out, m_hy_w_out, v_hy_w_out), (pool_w_in, m_pool_w_in, v_pool_w_in),
               (pool_w_grp, m_pool_w_grp, v_pool_w_grp), (pool_w_out, m_pool_w_out, v_pool_w_out)]
    for a, nm in enumerate(big_names):
        rr, cc = big[a].shape
        w, m, v = (t.reshape(rr, cc) for t in big_wmv[a])
        outs = _adam_2d(w, halves[a], sib_halves[a], m, v, cidx, f"adam_{nm}")
        res[nm] = tuple(o.reshape(big_wmv[a][0].shape) for o in outs)

    loss = lax.psum(loss8[0, 0], ("x", "y", "c"))
    order = ["norm_g", "mod_w", "mod_b", "hy_w_in", "hy_conv_w", "hy_conv_b", "lru_w_a", "lru_b_a", "lru_w_x", "lru_b_x",
             "lru_lambda", "sc_conv_w", "hy_w_out", "pool_w_in", "pool_w_grp", "pool_b_grp", "pool_scale", "pool_w_out",
             "final_g"]
    return (loss, grad_x[None], *[res[nm][0] for nm in order], *[res[nm][1] for nm in order],
            *[res[nm][2] for nm in order], *[res[nm][3] for nm in order])
```

```python
import jax
import jax.numpy as jnp
from jax import lax
from jax.experimental import pallas as pl
from jax.experimental.pallas import tpu as pltpu

F32, BF16 = jnp.float32, jnp.bfloat16
D = 1024
RMS_EPS = 1e-6
SQRT_FLOOR = 1e-30
LRU_C = 8.0
LRU_HEADS, LRU_HEAD_DIM = 8, 128
POOL_WINDOWS = (2, 4, 8, 16)
POOL_GROUP_DIM = 512
ADAM_LR, ADAM_B1, ADAM_B2, ADAM_EPS, ADAM_WD, ADAM_STEP = 0.001, 0.9, 0.999, 1e-08, 0.01, 10
MESH = pl.DeviceIdType.MESH
N_DEV, N_CHIP = 8, 4
SUBLANES = 8
BF16_ROWS = 16
POOL_HALO = 16
TS_PROJ, TS_MIX, TS_WGRAD, TS_DGRAD = 1024, 256, 1024, 256
SMALL_ROWS = 64
GRAD_WIRE_DTYPE = BF16
ANY = pl.BlockSpec(memory_space=pl.ANY)
VMEM = pl.BlockSpec(memory_space=pltpu.VMEM)
NT = (((1,), (1,)), ((), ()))
TN = (((0,), (0,)), ((), ()))


def _cp(sem=None, vmem_mb=56):
    kw = dict(vmem_limit_bytes=vmem_mb * 2 ** 20)
    if sem is not None:
        kw["dimension_semantics"] = sem
    return pltpu.CompilerParams(**kw)


def _tile(n, t):
    return min(n, t)


def _pos():
    return lax.axis_index("x"), lax.axis_index("y"), lax.axis_index("c")


def _flip(v, f):
    return 1 - v if f else v


def _sigmoid(z):
    return 0.5 * jnp.tanh(0.5 * z) + 0.5


def _rows(n, c):
    return lax.broadcasted_iota(jnp.int32, (n, c), 0)


def _down(a, d):
    return a if d == 0 else pltpu.roll(a, d, 0)


def _up(a, d):
    return a if d == 0 else pltpu.roll(a, a.shape[0] - d, 0)


def _scan_fwd_steps(a, u, carry):
    n, c = a.shape
    sub = _rows(SUBLANES, c)
    out = []
    for k in range(n // SUBLANES):
        p = a[k * SUBLANES:(k + 1) * SUBLANES]
        g = u[k * SUBLANES:(k + 1) * SUBLANES]
        for d in (1, 2, 4):
            keep = sub >= d
            g = g + p * jnp.where(keep, pltpu.roll(g, d, 0), 0.0)
            p = p * jnp.where(keep, pltpu.roll(p, d, 0), 1.0)
        h = g + p * carry
        carry = h[SUBLANES - 1:SUBLANES, :]
        out.append(h)
        yield
    return jnp.concatenate(out, axis=0)


def _scan_rev_steps(alpha, b, carry):
    n, c = alpha.shape
    sub = _rows(SUBLANES, c)
    out = []
    for k in reversed(range(n // SUBLANES)):
        p = alpha[k * SUBLANES:(k + 1) * SUBLANES]
        g = b[k * SUBLANES:(k + 1) * SUBLANES]
        for d in (1, 2, 4):
            keep = sub < SUBLANES - d
            g = g + p * jnp.where(keep, pltpu.roll(g, SUBLANES - d, 0), 0.0)
            p = p * jnp.where(keep, pltpu.roll(p, SUBLANES - d, 0), 1.0)
        h = g + p * carry
        carry = h[0:1, :]
        out.append(h)
        yield
    return jnp.concatenate(out[::-1], axis=0)


def _run(steps):
    while True:
        try:
            next(steps)
        except StopIteration as done:
            return done.value


def _paired(progress, pieces):
    n, done = len(pieces), 1
    pieces[0]()
    for frac in progress:
        while done < n and done <= frac * n:
            pieces[done]()
            done += 1
    while done < n:
        pieces[done]()
        done += 1


def _conv_taps(ext, halo, n, width):
    return [_down(ext, width - 1 - k)[halo:halo + n] for k in range(width)]


def _lru_gates(xc, wa_ref, ba, wx_ref, bx):
    xb = xc.astype(BF16)
    pa, px = [], []
    for h in range(LRU_HEADS):
        xh = xb[:, h * LRU_HEAD_DIM:(h + 1) * LRU_HEAD_DIM]
        pa.append(jnp.dot(xh, wa_ref[h], preferred_element_type=F32))
        px.append(jnp.dot(xh, wx_ref[h], preferred_element_type=F32))
    r = _sigmoid(jnp.concatenate(pa, axis=1) + ba)
    ig = _sigmoid(jnp.concatenate(px, axis=1) + bx)
    return r, ig


def _softplus_neg(lam):
    return jnp.maximum(-lam, 0.0) + jnp.log1p(jnp.exp(-jnp.abs(lam)))


def _recip_1_to_2(d):
    r0 = pl.reciprocal(d, approx=True)
    return r0 * (2.0 - d * r0)


def _lru_decay(r, sp, first):
    big_l = (-LRU_C) * r * sp
    a = jnp.exp(big_l)
    th = jnp.tanh(big_l)
    q = (-2.0 * th) * _recip_1_to_2(1.0 - th)
    rs = lax.rsqrt(jnp.maximum(q, SQRT_FLOOR))
    return a, jnp.where(first, 1.0, q * rs), rs


def _pool_inv_counts(t0, n):
    t = (t0 + lax.broadcasted_iota(jnp.int32, (n, 1), 0) + 1).astype(F32)
    return [1.0 / jnp.minimum(t, float(w)) for w in POOL_WINDOWS]


def _window_sums(ext, shift):
    gd = POOL_GROUP_DIM
    out = []
    s = ext
    for k in range(len(POOL_WINDOWS)):
        s = s + shift(s, 2 ** k)
        out.append(s[:, 0:gd])
        if k + 1 < len(POOL_WINDOWS):
            s = s[:, gd:]
    return out


SW_ROWS, SW_COLS = 16, 2 * D
SW_CONV, SW_SC, SW_POOL_B, SW_POOL_S = 0, 4, 8, 9


def _mod_fwd(c8, mod_w, mod_b, conv_w, sc_w, pool_b, pool_s):
    nw = mod_w.shape[2]
    cq, pq = conv_w.shape[1], pool_b.shape[1]

    def body(c_ref, w_ref, b_ref, cw_ref, sw_ref, pb_ref, ps_ref, ca_ref, mod_ref, small_ref,
             cslot, mslot, msend, pslot, psend, s1, r1, s2, r2, s3, r3):
        x, y, c = _pos()
        me = 4 * x + 2 * y + c
        chip = 2 * x + y
        first = []
        for r in range(1, N_DEV):
            fx, fy, fc = (r >> 2) & 1, (r >> 1) & 1, r & 1
            cp = pltpu.make_async_remote_copy(
                src_ref=c_ref, dst_ref=cslot.at[me], send_sem=s1.at[r - 1], recv_sem=r1.at[r - 1],
                device_id=(_flip(x, fx), _flip(y, fy), _flip(c, fc)), device_id_type=MESH)
            cp.start()
            first.append(cp)
        cslot[me] = c_ref[...]
        for cp in first:
            cp.wait()
        rows = _rows(SUBLANES, D)
        call = jnp.zeros((SUBLANES, D), F32)
        for d in range(N_DEV):
            call = jnp.where(rows == d, cslot[d], call)
        ca = call * _sigmoid(call)
        ca_ref[...] = ca
        for l in range(2):
            msend[l] = jnp.dot(ca, w_ref[l], precision=lax.Precision.HIGHEST, preferred_element_type=F32)
        psend[...] = jnp.zeros_like(psend)
        psend[SW_CONV:SW_CONV + 4, 0:cq] = cw_ref[...]
        psend[SW_SC:SW_SC + 3, 0:cq] = sw_ref[...]
        psend[SW_POOL_B:SW_POOL_B + 1, :] = pb_ref[...]
        psend[SW_POOL_S:SW_POOL_S + 1, :] = ps_ref[...]
        second = []
        for q, (fx, fy) in enumerate(((1, 0), (0, 1), (1, 1))):
            peer = (_flip(x, fx), _flip(y, fy), c)
            for src, dst, ss, rs in ((msend, mslot, s2, r2), (psend, pslot, s3, r3)):
                cp = pltpu.make_async_remote_copy(src_ref=src, dst_ref=dst.at[chip], send_sem=ss.at[q], recv_sem=rs.at[q],
                                                  device_id=peer, device_id_type=MESH)
                cp.start()
                second.append(cp)
        mslot[chip] = msend[...]
        pslot[chip] = psend[...]
        for cp in second:
            cp.wait()
        small_ref[...] = jnp.zeros_like(small_ref)
        for j in range(N_CHIP):
            for l in range(2):
                mod_ref[l, :, j * nw:(j + 1) * nw] = mslot[j, l] + b_ref[l:l + 1, j * nw:(j + 1) * nw]
            small_ref[0:SUBLANES, j * cq:(j + 1) * cq] = pslot[j, 0:SUBLANES, 0:cq]
            small_ref[SUBLANES:SW_ROWS, j * pq:(j + 1) * pq] = pslot[j, SUBLANES:SW_ROWS, :]

    args = (c8, mod_w, mod_b, conv_w, sc_w, pool_b, pool_s)
    dma3 = pltpu.SemaphoreType.DMA((N_CHIP - 1,))
    return pl.pallas_call(
        body, name="mod_fwd",
        in_specs=[VMEM] * len(args), out_specs=[VMEM] * 3,
        out_shape=[jax.ShapeDtypeStruct((SUBLANES, D), F32), jax.ShapeDtypeStruct((2, SUBLANES, N_CHIP * nw), F32),
                   jax.ShapeDtypeStruct((SW_ROWS, SW_COLS), F32)],
        scratch_shapes=[pltpu.VMEM((N_DEV, SUBLANES, D), F32), pltpu.VMEM((N_CHIP, 2, SUBLANES, nw), F32),
                        pltpu.VMEM((2, SUBLANES, nw), F32), pltpu.VMEM((N_CHIP, SW_ROWS, pq), F32),
                        pltpu.VMEM((SW_ROWS, pq), F32),
                        pltpu.SemaphoreType.DMA((N_DEV - 1,)), pltpu.SemaphoreType.DMA((N_DEV - 1,)),
                        dma3, dma3, dma3, dma3],
        compiler_params=_cp(),
    )(*args)


def _wcast(ws):
    def body(*refs):
        n = len(refs) // 2
        for a in range(n):
            refs[n + a][...] = refs[a][...].astype(BF16)

    return pl.pallas_call(
        body, name="wcast", in_specs=[VMEM] * len(ws), out_specs=[VMEM] * len(ws),
        out_shape=[jax.ShapeDtypeStruct(w.shape, BF16) for w in ws], compiler_params=_cp(),
    )(*ws)


def _wcast_own_block(w, kidx, name):
    rr, cc = w.shape
    rb = min(rr, 256)

    def body(k_ref, w_ref, o_ref):
        o_ref[...] = w_ref[...].astype(BF16)

    return pl.pallas_call(
        body, name=name,
        grid_spec=pltpu.PrefetchScalarGridSpec(
            num_scalar_prefetch=1, grid=(rr // rb,),
            in_specs=[pl.BlockSpec((rb, cc), lambda j, k_ref: (j, 0))],
            out_specs=pl.BlockSpec((None, rb, cc), lambda j, k_ref: (k_ref[0], j, 0))),
        out_shape=jax.ShapeDtypeStruct((N_CHIP, rr, cc), BF16),
        compiler_params=_cp(("parallel",)),
    )(kidx, w)


def _wgather(bufs):
    n = len(bufs)

    def body(*refs):
        outs = refs[n:2 * n]
        ssem, rsem, fssem, frsem = refs[2 * n:]
        x, y, c = _pos()
        chip = 2 * x + y
        sib = (x, y, 1 - c)
        flips = ((1, 0), (0, 1), (1, 1))

        def half(a, which):
            hr = bufs[a].shape[1] // 2
            return pl.ds(pl.multiple_of(which * hr, BF16_ROWS), hr)

        sends = []
        for a in range(n):
            mine = outs[a].at[chip, half(a, c), :]
            for q, (fx, fy) in enumerate(flips):
                cp = pltpu.make_async_remote_copy(
                    src_ref=mine, dst_ref=mine, send_sem=ssem.at[3 * a + q], recv_sem=rsem.at[3 * a + q],
                    device_id=(_flip(x, fx), _flip(y, fy), c), device_id_type=MESH)
                cp.start()
                sends.append(cp)
        passed = []
        for a in range(n):
            for q, (fx, fy) in enumerate(flips):
                src_chip = 2 * _flip(x, fx) + _flip(y, fy)
                landed = outs[a].at[src_chip, half(a, c), :]
                pltpu.make_async_remote_copy(
                    src_ref=landed, dst_ref=landed, send_sem=ssem.at[3 * a + q], recv_sem=rsem.at[3 * a + q],
                    device_id=sib, device_id_type=MESH).wait_recv()
                cp = pltpu.make_async_remote_copy(
                    src_ref=landed, dst_ref=landed, send_sem=fssem.at[3 * a + q], recv_sem=frsem.at[3 * a + q],
                    device_id=sib, device_id_type=MESH)
                cp.start()
                passed.append(cp)
        for a in range(n):
            for q, (fx, fy) in enumerate(flips):
                src_chip = 2 * _flip(x, fx) + _flip(y, fy)
                other = outs[a].at[src_chip, half(a, 1 - c), :]
                pltpu.make_async_remote_copy(
                    src_ref=other, dst_ref=other, send_sem=fssem.at[3 * a + q], recv_sem=frsem.at[3 * a + q],
                    device_id=sib, device_id_type=MESH).wait_recv()
        for cp in sends + passed:
            cp.wait_send()

    return pl.pallas_call(
        body, name="wgather", in_specs=[ANY] * n, out_specs=[ANY] * n,
        out_shape=[jax.ShapeDtypeStruct(b.shape, BF16) for b in bufs],
        input_output_aliases={a: a for a in range(n)},
        scratch_shapes=[pltpu.SemaphoreType.DMA((3 * n,))] * 4,
        compiler_params=_cp(),
    )(*bufs)


def _norm_proj(x, g, sc, sh, w, name):
    s_len, nb = x.shape[0], w.shape[2]
    ts = _tile(s_len, TS_PROJ)

    def body(x_ref, g_ref, sc_ref, sh_ref, w_ref, h_ref, p_ref):
        @pl.when(pl.program_id(1) == 0)
        def _():
            xv = x_ref[...]
            r = lax.rsqrt(jnp.mean(xv * xv, axis=-1, keepdims=True) + RMS_EPS)
            h_ref[...] = (xv * r * (g_ref[...] * (1.0 + sc_ref[...])) + sh_ref[...]).astype(BF16)

        p_ref[...] = jnp.dot(h_ref[...], w_ref[...], preferred_element_type=F32).astype(BF16)

    vec = pl.BlockSpec((1, D), lambda i, j: (0, 0))
    return pl.pallas_call(
        body, name=name, grid=(s_len // ts, N_CHIP),
        in_specs=[pl.BlockSpec((ts, D), lambda i, j: (i, 0)), vec, vec, vec,
                  pl.BlockSpec((None, D, nb), lambda i, j: (j, 0, 0))],
        out_specs=[pl.BlockSpec((ts, D), lambda i, j: (i, 0)), pl.BlockSpec((ts, nb), lambda i, j: (i, j))],
        out_shape=[jax.ShapeDtypeStruct((s_len, D), BF16), jax.ShapeDtypeStruct((s_len, N_CHIP * nb), BF16)],
        compiler_params=_cp(("parallel", "arbitrary")),
    )(x, g, sc, sh, w)


def _l0_fwd(x, g, sc, sh, w_in, gate, cw, cb, wa, ba, wx, bx, lam, sw, wo):
    s_len, nb = x.shape[0], w_in.shape[2]
    ts = _tile(s_len, TS_MIX)
    n_t = s_len // ts
    hl = SUBLANES

    def body(xa_ref, xb_ref, g_ref, sc_ref, sh_ref, win_ref, gate_ref, cw_ref, cb_ref, wa_ref, ba_ref, wx_ref, bx_ref,
             lam_ref, sw_ref, wo_ref, x1_ref, h_ref, y_ref, h0_ref, p_ref, pcur, pnext, cxa, czz, chh):
        i = pl.program_id(0)

        @pl.when(i == 0)
        def _():
            cxa[...] = jnp.zeros_like(cxa)
            czz[...] = jnp.zeros_like(czz)
            chh[...] = jnp.zeros_like(chh)
            pnext[...] = jnp.zeros_like(pnext)

        pcur[...] = pnext[...]
        xv = xa_ref[...]
        rinv = lax.rsqrt(jnp.mean(xv * xv, axis=-1, keepdims=True) + RMS_EPS)
        h0 = (xv * rinv * (g_ref[...] * (1.0 + sc_ref[...])) + sh_ref[...]).astype(BF16)
        h0_ref[...] = h0

        def project(k):
            def emit():
                pk = jnp.dot(h0, win_ref[k], preferred_element_type=F32).astype(BF16)
                p_ref[:, k * nb:(k + 1) * nb] = pk
                pnext[:, k * nb:(k + 1) * nb] = pk
            return emit

        def mixer():
            piece = lambda k: pcur[:, k * D:(k + 1) * D].astype(F32)
            xa = piece(0)
            rows = _rows(ts, D)
            taps = _conv_taps(jnp.concatenate([cxa[...], xa], axis=0), hl, ts, 4)
            xc = cb_ref[...] + sum(cw_ref[k:k + 1, :] * taps[k] for k in range(4))
            r, ig = _lru_gates(xc, wa_ref, ba_ref[...], wx_ref, bx_ref[...])
            a, m, _ = _lru_decay(r, _softplus_neg(lam_ref[...]), (rows == 0) & (i == 1))
            yield 0.26
            h = _run(_scan_fwd_steps(a, m * ig * xc, chh[hl - 1:hl, :]))
            yield 0.51
            gcp, v = piece(3), piece(4)
            z = gcp * v
            ztaps = _conv_taps(jnp.concatenate([czz[...], z], axis=0), hl, ts, 3)
            yb = piece(2) * sum(sw_ref[k:k + 1, :] * ztaps[k] for k in range(3))
            ga, gb = piece(1), piece(5)
            y = jnp.concatenate([h * (ga * _sigmoid(ga)), yb * (gb * _sigmoid(gb))], axis=1).astype(BF16)
            yield 0.76
            y_ref[...] = y
            x1_ref[...] = xb_ref[...] + gate_ref[...] * jnp.dot(y, wo_ref[...], preferred_element_type=F32)
            h_ref[...] = h.astype(BF16)
            cxa[...] = xa[ts - hl:, :]
            czz[...] = z[ts - hl:, :]
            chh[...] = jnp.where(i > 0, h[ts - hl:, :], 0.0)

        _paired(mixer(), [project(k) for k in range(N_CHIP)])

    def full(a):
        return pl.BlockSpec(a.shape, lambda i: (0,) * a.ndim)

    ahead = lambda w: pl.BlockSpec((ts, w), lambda i: (jnp.minimum(i, n_t - 1), 0))
    behind = lambda w: pl.BlockSpec((ts, w), lambda i: (jnp.maximum(i - 1, 0), 0))
    args = (x, x, g, sc, sh, w_in, gate, cw, cb, wa, ba, wx, bx, lam, sw, wo)
    return pl.pallas_call(
        body, name="l0_fwd", grid=(n_t + 1,),
        in_specs=[ahead(D), behind(D)] + [full(a) for a in args[2:]],
        out_specs=[behind(D), behind(D), behind(2 * D), ahead(D), ahead(N_CHIP * nb)],
        out_shape=[jax.ShapeDtypeStruct((s_len, D), F32), jax.ShapeDtypeStruct((s_len, D), BF16),
                   jax.ShapeDtypeStruct((s_len, 2 * D), BF16), jax.ShapeDtypeStruct((s_len, D), BF16),
                   jax.ShapeDtypeStruct((s_len, N_CHIP * nb), BF16)],
        scratch_shapes=[pltpu.VMEM((ts, N_CHIP * nb), BF16)] * 2 + [pltpu.VMEM((hl, D), F32)] * 3,
        compiler_params=_cp(("arbitrary",)),
    )(*args)


def _l1_mix(proj, x1, tgt, gate, wg, bg, scale, wo, gf):
    s_len = x1.shape[0]
    ts = _tile(s_len, TS_MIX)
    pw, gd, hl = 2 * D, POOL_GROUP_DIM, POOL_HALO

    def body(p_ref, x_ref, t_ref, gate_ref, wg_ref, bg_ref, sc_ref, wo_ref, gf_ref,
             d_ref, mx_ref, y_ref, dx_ref, loss_ref, dgf_ref, cv):
        i = pl.program_id(0)

        @pl.when(i == 0)
        def _():
            cv[...] = jnp.zeros_like(cv)
            loss_ref[...] = jnp.zeros_like(loss_ref)
            dgf_ref[...] = jnp.zeros_like(dgf_ref)

        v = p_ref[:, 0:pw].astype(F32)
        gg = p_ref[:, pw:2 * pw].astype(F32)
        sums = _window_sums(jnp.concatenate([cv[...], v], axis=0), _down)
        inv = _pool_inv_counts(i * ts, ts)
        dd = [sums[k][hl:hl + ts] * inv[k] - v[:, k * gd:(k + 1) * gd] for k in range(4)]
        mixed = jnp.concatenate(
            [jnp.dot(dd[k].astype(BF16), wg_ref[k], preferred_element_type=F32) for k in range(4)], axis=1) + bg_ref[...]
        d_ref[...] = jnp.concatenate(dd, axis=1).astype(BF16)
        mx_ref[...] = mixed.astype(BF16)
        y = (mixed * sc_ref[...] * (gg * _sigmoid(gg))).astype(BF16)
        y_ref[...] = y
        x2 = x_ref[...] + gate_ref[...] * jnp.dot(y, wo_ref[...], preferred_element_type=F32)
        r2 = lax.rsqrt(jnp.mean(x2 * x2, axis=-1, keepdims=True) + RMS_EPS)
        n2 = x2 * r2
        err = n2 * gf_ref[...] - t_ref[...]
        loss_ref[...] += jnp.sum(err * err, axis=0, keepdims=True)
        dyf = err * (1.0 / D)
        dgf_ref[...] += jnp.sum(dyf * n2, axis=0, keepdims=True)
        dn = dyf * gf_ref[...]
        dx_ref[...] = r2 * (dn - n2 * jnp.mean(dn * n2, axis=-1, keepdims=True))
        cv[...] = v[ts - hl:, :]

    def full(a):
        return pl.BlockSpec(a.shape, lambda i: (0,) * a.ndim)

    row = lambda w: pl.BlockSpec((ts, w), lambda i: (i, 0))
    acc = pl.BlockSpec((1, D), lambda i: (0, 0))
    return pl.pallas_call(
        body, name="l1_mix", grid=(s_len // ts,),
        in_specs=[row(2 * pw), row(D), row(D)] + [full(a) for a in (gate, wg, bg, scale, wo, gf)],
        out_specs=[row(pw), row(pw), row(pw), row(D), acc, acc],
        out_shape=[jax.ShapeDtypeStruct((s_len, pw), BF16)] * 3 + [jax.ShapeDtypeStruct((s_len, D), F32)]
        + [jax.ShapeDtypeStruct((1, D), F32)] * 2,
        scratch_shapes=[pltpu.VMEM((hl, pw), F32)],
        compiler_params=_cp(("arbitrary",)),
    )(proj, x1, tgt, gate, wg, bg, scale, wo, gf)


def _l1_bwd_mix(dx2, proj, mixed, y, dpool, gate, wg, scale, wo):
    s_len = dx2.shape[0]
    ts = _tile(s_len, TS_MIX)
    n_t = s_len // ts
    pw, gd, hl = 2 * D, POOL_GROUP_DIM, POOL_HALO

    def body(dx_ref, gg_ref, mx_ref, y_ref, d_ref, gate_ref, wg_ref, sc_ref, wo_ref,
             dp_ref, mt_ref, dwg_ref, dsc_ref, dbg_ref, cq):
        i = pl.program_id(0)

        @pl.when(i == 0)
        def _():
            cq[...] = jnp.zeros_like(cq)
            dsc_ref[...] = jnp.zeros_like(dsc_ref)
            dbg_ref[...] = jnp.zeros_like(dbg_ref)
            mt_ref[...] = jnp.zeros_like(mt_ref)
            dwg_ref[...] = jnp.zeros_like(dwg_ref)

        dxv = dx_ref[...]
        dxb = dxv.astype(BF16)

        def wgrad_out(k):
            mt_ref[k] += lax.dot_general(y_ref[:, k * gd:(k + 1) * gd], dxb, TN, preferred_element_type=F32)

        dy = lax.dot_general((gate_ref[...] * dxv).astype(BF16), wo_ref[...], NT, preferred_element_type=F32)
        wgrad_out(0)
        gg = gg_ref[...].astype(F32)
        mixed = mx_ref[...].astype(F32)
        s = _sigmoid(gg)
        sg = gg * s
        dmixed = dy * sc_ref[...] * sg
        dsc_ref[...] += jnp.sum(dy * mixed * sg, axis=0, keepdims=True)
        dbg_ref[...] += jnp.sum(dmixed, axis=0, keepdims=True)
        dmb = dmixed.astype(BF16)
        wgrad_out(1)
        dp_ref[:, pw:2 * pw] = (dy * sc_ref[...] * mixed * (s * (1.0 + gg * (1.0 - s)))).astype(BF16)
        inv = _pool_inv_counts((n_t - 1 - i) * ts, ts)
        dd = []
        for k in range(4):
            dmk = dmb[:, k * gd:(k + 1) * gd]
            dd.append(lax.dot_general(dmk, wg_ref[k], NT, preferred_element_type=F32))
            dwg_ref[k] += lax.dot_general(d_ref[:, k * gd:(k + 1) * gd], dmk, TN, preferred_element_type=F32)
        wgrad_out(2)
        q = jnp.concatenate([dd[k] * inv[k] for k in range(4)], axis=1)
        sums = _window_sums(jnp.concatenate([q, cq[...]], axis=0), _up)
        wgrad_out(3)
        dp_ref[:, 0:pw] = jnp.concatenate([sums[k][0:ts] - dd[k] for k in range(4)], axis=1).astype(BF16)
        cq[...] = q[0:hl, :]

    def full(a):
        return pl.BlockSpec(a.shape, lambda i: (0,) * a.ndim)

    rev = lambda w, j=0: pl.BlockSpec((ts, w), lambda i: (n_t - 1 - i, j))
    acc = pl.BlockSpec((1, pw), lambda i: (0, 0))
    return pl.pallas_call(
        body, name="l1_bwd_mix", grid=(n_t,),
        in_specs=[rev(D), rev(pw, 1), rev(pw), rev(pw), rev(pw)] + [full(a) for a in (gate, wg, scale, wo)],
        out_specs=[rev(2 * pw), pl.BlockSpec((N_CHIP, gd, D), lambda i: (0, 0, 0)),
                   pl.BlockSpec((4, gd, gd), lambda i: (0, 0, 0)), acc, acc],
        out_shape=[jax.ShapeDtypeStruct((s_len, 2 * pw), BF16), jax.ShapeDtypeStruct((N_CHIP, gd, D), F32),
                   jax.ShapeDtypeStruct((4, gd, gd), F32),
                   jax.ShapeDtypeStruct((1, pw), F32), jax.ShapeDtypeStruct((1, pw), F32)],
        scratch_shapes=[pltpu.VMEM((hl, pw), F32)],
        compiler_params=_cp(("arbitrary",)),
    )(dx2, proj, mixed, y, dpool, gate, wg, scale, wo)


def _l0_bwd_mix(dx1, proj, hst, y, gate, cw, cb, wa, ba, wx, bx, lam, sw, wo):
    s_len = dx1.shape[0]
    ts = _tile(s_len, TS_MIX)
    n_t = s_len // ts
    hl, hb = SUBLANES, BF16_ROWS
    yb_w = 2 * D // N_CHIP

    def body(dx_ref, p_ref, ph_ref, h_ref, hh_ref, y_ref, gate_ref, cw_ref, cb_ref, wa_ref, ba_ref, wx_ref, bx_ref,
             lam_ref, sw_ref, wo_ref, dp_ref, mt_ref, dwa_ref, dwx_ref, sm_ref, cg, cdxc, cdcz, ca):
        i = pl.program_id(0)
        ri = n_t - 1 - i

        @pl.when(i == 0)
        def _():
            cg[...] = jnp.zeros_like(cg)
            ca[...] = jnp.zeros_like(ca)
            cdxc[...] = jnp.zeros_like(cdxc)
            cdcz[...] = jnp.zeros_like(cdcz)
            sm_ref[...] = jnp.zeros_like(sm_ref)
            mt_ref[...] = jnp.zeros_like(mt_ref)
            dwa_ref[...] = jnp.zeros_like(dwa_ref)
            dwx_ref[...] = jnp.zeros_like(dwx_ref)

        dxb = dx_ref[...].astype(BF16)

        def wgrad_out(k):
            mt_ref[k] += lax.dot_general(y_ref[:, k * yb_w:(k + 1) * yb_w], dxb, TN, preferred_element_type=F32)

        wgrad_out(0)
        has_prev = (ri > 0).astype(F32)
        xa, ga, gbp, gcp, v, gb = [p_ref[:, k * D:(k + 1) * D].astype(F32) for k in range(6)]
        prev = lambda k: ph_ref[:, k * D:(k + 1) * D].astype(F32)[hb - hl:hb] * has_prev
        rows = _rows(ts, D)
        first = (rows == 0) & (ri == 0)
        xtaps = _conv_taps(jnp.concatenate([prev(0), xa], axis=0), hl, ts, 4)
        xc = cb_ref[...] + sum(cw_ref[k:k + 1, :] * xtaps[k] for k in range(4))
        r, ig = _lru_gates(xc, wa_ref, ba_ref[...], wx_ref, bx_ref[...])
        sp = _softplus_neg(lam_ref[...])
        a, m, inv_m = _lru_decay(r, sp, first)
        z = gcp * v
        ztaps = _conv_taps(jnp.concatenate([prev(3) * prev(4), z], axis=0), hl, ts, 3)
        cz = sum(sw_ref[k:k + 1, :] * ztaps[k] for k in range(3))
        h = h_ref[...].astype(F32)
        hprev = _down(jnp.concatenate([hh_ref[...].astype(F32)[hb - hl:hb] * has_prev, h], axis=0), 1)[hl:hl + ts]
        dy = lax.dot_general((gate_ref[...] * dx_ref[...]).astype(BF16), wo_ref[...], NT, preferred_element_type=F32)
        dya_pre, dyb_pre = dy[:, 0:D], dy[:, D:2 * D]
        s_a, s_b = _sigmoid(ga), _sigmoid(gb)
        dp_ref[:, D:2 * D] = (dya_pre * h * (s_a * (1.0 + ga * (1.0 - s_a)))).astype(BF16)
        dp_ref[:, 5 * D:6 * D] = (dyb_pre * (gbp * cz) * (s_b * (1.0 + gb * (1.0 - s_b)))).astype(BF16)
        dya = dya_pre * (ga * s_a)
        dyb = dyb_pre * (gb * s_b)
        wgrad_out(1)
        dp_ref[:, 2 * D:3 * D] = (dyb * cz).astype(BF16)
        dcz = dyb * gbp
        for k in range(3):
            sm_ref[8 + k:9 + k, :] += jnp.sum(dcz * ztaps[k], axis=0, keepdims=True)
        dcz_ext = jnp.concatenate([dcz, cdcz[...]], axis=0)
        dz = sum(sw_ref[k:k + 1, :] * _up(dcz_ext, 2 - k)[0:ts] for k in range(3))
        dp_ref[:, 3 * D:4 * D] = (dz * v).astype(BF16)
        dp_ref[:, 4 * D:5 * D] = (dz * gcp).astype(BF16)
        cdcz[...] = dcz[0:hl, :]
        alpha = _up(jnp.concatenate([a, ca[...]], axis=0), 1)[0:ts]
        wgrad_out(2)
        dh = _run(_scan_rev_steps(alpha, dya, cg[0:1, :]))
        wgrad_out(3)
        cg[...] = dh[0:hl, :]
        ca[...] = a[0:hl, :]
        da = dh * hprev
        dm = dh * ig * xc
        di = dh * m * xc
        dxc = dh * m * ig
        dl = da * a - jnp.where(first, 0.0, dm * (a * a) * inv_m)
        sm_ref[7:8, :] += jnp.sum(dl * r, axis=0, keepdims=True) * (-LRU_C)
        dpa = (dl * sp) * (-LRU_C) * r * (1.0 - r)
        dpx = di * ig * (1.0 - ig)
        sm_ref[5:6, :] += jnp.sum(dpa, axis=0, keepdims=True)
        sm_ref[6:7, :] += jnp.sum(dpx, axis=0, keepdims=True)
        dpa_b, dpx_b, xc_b = dpa.astype(BF16), dpx.astype(BF16), xc.astype(BF16)
        back = []
        for hd in range(LRU_HEADS):
            sl = slice(hd * LRU_HEAD_DIM, (hd + 1) * LRU_HEAD_DIM)
            back.append(lax.dot_general(dpa_b[:, sl], wa_ref[hd], NT, preferred_element_type=F32)
                        + lax.dot_general(dpx_b[:, sl], wx_ref[hd], NT, preferred_element_type=F32))
            dwa_ref[hd] += lax.dot_general(xc_b[:, sl], dpa_b[:, sl], TN, preferred_element_type=F32)
            dwx_ref[hd] += lax.dot_general(xc_b[:, sl], dpx_b[:, sl], TN, preferred_element_type=F32)
        dxc = dxc + jnp.concatenate(back, axis=1)
        sm_ref[4:5, :] += jnp.sum(dxc, axis=0, keepdims=True)
        for k in range(4):
            sm_ref[k:k + 1, :] += jnp.sum(dxc * xtaps[k], axis=0, keepdims=True)
        dxc_ext = jnp.concatenate([dxc, cdxc[...]], axis=0)
        dp_ref[:, 0:D] = sum(cw_ref[k:k + 1, :] * _up(dxc_ext, 3 - k)[0:ts] for k in range(4)).astype(BF16)
        cdxc[...] = dxc[0:hl, :]

    def full(a):
        return pl.BlockSpec(a.shape, lambda i: (0,) * a.ndim)

    rev = lambda w: pl.BlockSpec((ts, w), lambda i: (n_t - 1 - i, 0))
    halo = lambda w: pl.BlockSpec((hb, w), lambda i: (jnp.maximum((n_t - 1 - i) * (ts // hb) - 1, 0), 0))
    return pl.pallas_call(
        body, name="l0_bwd_mix", grid=(n_t,),
        in_specs=[rev(D), rev(6 * D), halo(6 * D), rev(D), halo(D), rev(2 * D)]
        + [full(a) for a in (gate, cw, cb, wa, ba, wx, bx, lam, sw, wo)],
        out_specs=[rev(6 * D), pl.BlockSpec((N_CHIP, yb_w, D), lambda i: (0, 0, 0)),
                   pl.BlockSpec(wa.shape, lambda i: (0, 0, 0)), pl.BlockSpec(wa.shape, lambda i: (0, 0, 0)),
                   pl.BlockSpec((2 * SUBLANES, D), lambda i: (0, 0))],
        out_shape=[jax.ShapeDtypeStruct((s_len, 6 * D), BF16), jax.ShapeDtypeStruct((N_CHIP, yb_w, D), F32),
                   jax.ShapeDtypeStruct(wa.shape, F32), jax.ShapeDtypeStruct(wa.shape, F32),
                   jax.ShapeDtypeStruct((2 * SUBLANES, D), F32)],
        scratch_shapes=[pltpu.VMEM((hl, D), F32)] * 4,
        compiler_params=_cp(("arbitrary",)),
    )(dx1, proj, proj, hst, hst, y, gate, cw, cb, wa, ba, wx, bx, lam, sw, wo)


def _dgrad_norm(dproj, w, x, dres, g, sc, name):
    s_len, nb = x.shape[0], w.shape[2]
    ts = _tile(s_len, TS_DGRAD)

    def body(dp_ref, w_ref, x_ref, dr_ref, g_ref, sc_ref, dx_ref, s1_ref, s2_ref):
        @pl.when(pl.program_id(0) == 0)
        def _():
            s1_ref[...] = jnp.zeros_like(s1_ref)
            s2_ref[...] = jnp.zeros_like(s2_ref)

        dh = sum(lax.dot_general(dp_ref[:, k * nb:(k + 1) * nb], w_ref[k], NT, preferred_element_type=F32)
                 for k in range(N_CHIP))
        xv = x_ref[...]
        r = lax.rsqrt(jnp.mean(xv * xv, axis=-1, keepdims=True) + RMS_EPS)
        n = xv * r
        s1_ref[...] += jnp.sum(dh, axis=0, keepdims=True)
        s2_ref[...] += jnp.sum(dh * n, axis=0, keepdims=True)
        dn = dh * (g_ref[...] * (1.0 + sc_ref[...]))
        dx_ref[...] = dr_ref[...] + r * (dn - n * jnp.mean(dn * n, axis=-1, keepdims=True))

    row = lambda wd: pl.BlockSpec((ts, wd), lambda i: (i, 0))
    vec = pl.BlockSpec((1, D), lambda i: (0, 0))
    return pl.pallas_call(
        body, name=name, grid=(s_len // ts,),
        in_specs=[row(N_CHIP * nb), pl.BlockSpec(w.shape, lambda i: (0, 0, 0)), row(D), row(D), vec, vec],
        out_specs=[row(D), vec, vec],
        out_shape=[jax.ShapeDtypeStruct((s_len, D), F32)] + [jax.ShapeDtypeStruct((1, D), F32)] * 2,
        compiler_params=_cp(("arbitrary",)),
    )(dproj, w, x, dres, g, sc)


def _wgrad(a, b, groups, ka, nb, a_col, b_col, name):
    s_len = a.shape[0]
    ts = _tile(s_len, TS_WGRAD)

    def body(a_ref, b_ref, o_ref):
        @pl.when(pl.program_id(1) == 0)
        def _():
            o_ref[...] = jnp.zeros_like(o_ref)

        o_ref[...] += lax.dot_general(a_ref[...].astype(BF16), b_ref[...].astype(BF16), TN, preferred_element_type=F32)

    return pl.pallas_call(
        body, name=name, grid=(groups, s_len // ts),
        in_specs=[pl.BlockSpec((ts, ka), lambda g, s: (s, a_col(g))), pl.BlockSpec((ts, nb), lambda g, s: (s, b_col(g)))],
        out_specs=pl.BlockSpec((None, ka, nb), lambda g, s: (g, 0, 0)),
        out_shape=jax.ShapeDtypeStruct((groups, ka, nb), F32),
        compiler_params=_cp(("parallel", "arbitrary")),
    )(a, b)


def _wo_final(mt, wo, gate, name):
    rb = mt.shape[1]

    def body(m_ref, w_ref, gate_ref, dw_ref, dg_ref):
        @pl.when(pl.program_id(0) == 0)
        def _():
            dg_ref[...] = jnp.zeros_like(dg_ref)

        mv = m_ref[...]
        dw_ref[...] = mv * gate_ref[...]
        dg_ref[...] += jnp.sum(mv * w_ref[...].astype(F32), axis=0, keepdims=True)

    blk = pl.BlockSpec((None, rb, D), lambda k: (k, 0, 0))
    vec = pl.BlockSpec((1, D), lambda k: (0, 0))
    return pl.pallas_call(
        body, name=name, grid=(N_CHIP,), in_specs=[blk, blk, vec], out_specs=[blk, vec],
        out_shape=[jax.ShapeDtypeStruct(mt.shape, F32), jax.ShapeDtypeStruct((1, D), F32)],
        compiler_params=_cp(("arbitrary",)),
    )(mt, wo, gate)


ROW_NORM_G, ROW_CONV_W, ROW_CONV_B, ROW_B_A, ROW_B_X, ROW_LAMBDA, ROW_SC_W, ROW_POOL_B, ROW_POOL_S, ROW_FINAL_G = (
    0, 2, 6, 7, 8, 9, 10, 13, 15, 17)


def _small_pack(s1_0, s2_0, s1_1, s2_1, sm0, dsc1, dbg1, dgf, losscols, dgate0, dgate1, norm_g, sc0, sc1, lam):
    def body(s1_0r, s2_0r, s1_1r, s2_1r, sm, dsc, dbg, dgfr, lcols, dg0, dg1, ng, sc0r, sc1r, lamr, buf, dmod, loss):
        buf[...] = jnp.zeros_like(buf)
        buf[0:1, :] = s2_0r[...] * (1.0 + sc0r[...])
        buf[1:2, :] = s2_1r[...] * (1.0 + sc1r[...])
        buf[ROW_CONV_W:ROW_CONV_W + 4, :] = sm[0:4, :]
        buf[ROW_CONV_B:ROW_CONV_B + 1, :] = sm[4:5, :]
        buf[ROW_B_A:ROW_B_A + 1, :] = sm[5:6, :]
        buf[ROW_B_X:ROW_B_X + 1, :] = sm[6:7, :]
        buf[ROW_LAMBDA:ROW_LAMBDA + 1, :] = -sm[7:8, :] * _sigmoid(-lamr[...])
        buf[ROW_SC_W:ROW_SC_W + 3, :] = sm[8:11, :]
        for k in range(2):
            buf[ROW_POOL_B + k:ROW_POOL_B + k + 1, :] = dbg[:, k * D:(k + 1) * D]
            buf[ROW_POOL_S + k:ROW_POOL_S + k + 1, :] = dsc[:, k * D:(k + 1) * D]
        buf[ROW_FINAL_G:ROW_FINAL_G + 1, :] = dgfr[...]
        pieces = (s1_0r[...], s2_0r[...] * ng[0:1, :], dg0[...], s1_1r[...], s2_1r[...] * ng[1:2, :], dg1[...])
        for k, pc in enumerate(pieces):
            dmod[:, k * D:(k + 1) * D] = jnp.broadcast_to(pc, (SUBLANES, D))
        loss[...] = jnp.broadcast_to(jnp.sum(lcols[...], axis=1, keepdims=True) * (0.5 / D), loss.shape)

    args = (s1_0, s2_0, s1_1, s2_1, sm0, dsc1, dbg1, dgf, losscols, dgate0, dgate1, norm_g, sc0, sc1, lam)
    return pl.pallas_call(
        body, name="small_pack", in_specs=[VMEM] * len(args), out_specs=[VMEM] * 3,
        out_shape=[jax.ShapeDtypeStruct((SMALL_ROWS, D), F32), jax.ShapeDtypeStruct((SUBLANES, 6 * D), F32),
                   jax.ShapeDtypeStruct((SUBLANES, 128), F32)],
        compiler_params=_cp(),
    )(*args)


def _small_comm(buf_a, buf_b, dmod8):
    ra, rb = buf_a.shape[0] // N_DEV, buf_b.shape[0] // N_DEV
    wb = buf_b.shape[1]

    def body(a_ref, b_ref, dm_ref, oa_ref, ob_ref, odm_ref, ina, inb, dslot, sa, sb, s1, r1, s2, r2):
        x, y, c = _pos()
        me = 4 * x + 2 * y + c
        peers = []
        for r in range(1, N_DEV):
            fx, fy, fc = (r >> 2) & 1, (r >> 1) & 1, r & 1
            px, py, pc = _flip(x, fx), _flip(y, fy), _flip(c, fc)
            peers.append(((px, py, pc), 4 * px + 2 * py + pc))
        seg_a = lambda d: pl.ds(pl.multiple_of(d * ra, SUBLANES), ra)
        seg_b = lambda d: pl.ds(pl.multiple_of(d * rb, SUBLANES), rb)
        first = []
        for r, (peer, pid) in enumerate(peers):
            for k, (src, dst) in enumerate(((a_ref.at[seg_a(pid), :], ina.at[r]), (b_ref.at[seg_b(pid), :], inb.at[r]),
                                            (dm_ref, dslot.at[me]))):
                cp = pltpu.make_async_remote_copy(src_ref=src, dst_ref=dst, send_sem=s1.at[3 * r + k],
                                                  recv_sem=r1.at[3 * r + k], device_id=peer, device_id_type=MESH)
                cp.start()
                first.append(cp)
        dslot[me] = dm_ref[...]
        for cp in first:
            cp.wait()
        acc_a, acc_b = a_ref[seg_a(me), :], b_ref[seg_b(me), :]
        for r in range(N_DEV - 1):
            acc_a = acc_a + ina[r]
            acc_b = acc_b + inb[r]
        sa[...] = acc_a
        sb[...] = acc_b
        oa_ref[seg_a(me), :] = acc_a
        ob_ref[seg_b(me), :] = acc_b
        second = []
        for r, (peer, pid) in enumerate(peers):
            for k, (src, dst) in enumerate(((sa, oa_ref.at[seg_a(me), :]), (sb, ob_ref.at[seg_b(me), :]))):
                cp = pltpu.make_async_remote_copy(src_ref=src, dst_ref=dst, send_sem=s2.at[2 * r + k],
                                                  recv_sem=r2.at[2 * r + k], device_id=peer, device_id_type=MESH)
                cp.start()
                second.append(cp)
        rows = _rows(SUBLANES, dm_ref.shape[1])
        dm_all = jnp.zeros(dm_ref.shape, F32)
        for d in range(N_DEV):
            dm_all = jnp.where(rows == d, dslot[d], dm_all)
        odm_ref[...] = dm_all
        for cp in second:
            cp.wait()

    nrel = N_DEV - 1
    return pl.pallas_call(
        body, name="small_comm", in_specs=[VMEM] * 3, out_specs=[VMEM] * 3,
        out_shape=[jax.ShapeDtypeStruct(buf_a.shape, F32), jax.ShapeDtypeStruct(buf_b.shape, F32),
                   jax.ShapeDtypeStruct(dmod8.shape, F32)],
        scratch_shapes=[pltpu.VMEM((nrel, ra, D), F32), pltpu.VMEM((nrel, rb, wb), F32),
                        pltpu.VMEM((N_DEV,) + dmod8.shape, F32), pltpu.VMEM((ra, D), F32), pltpu.VMEM((rb, wb), F32),
                        pltpu.SemaphoreType.DMA((3 * nrel,)), pltpu.SemaphoreType.DMA((3 * nrel,)),
                        pltpu.SemaphoreType.DMA((2 * nrel,)), pltpu.SemaphoreType.DMA((2 * nrel,))],
        compiler_params=_cp(),
    )(buf_a, buf_b, dmod8)


def _adam(w, g, m, v):
    m2 = ADAM_B1 * m + (1.0 - ADAM_B1) * g
    v2 = ADAM_B2 * v + (1.0 - ADAM_B2) * (g * g)
    m_hat = m2 / (1.0 - ADAM_B1 ** ADAM_STEP)
    v_hat = v2 / (1.0 - ADAM_B2 ** ADAM_STEP)
    return -ADAM_LR * (m_hat / (jnp.sqrt(v_hat) + ADAM_EPS) + ADAM_WD * w), m2, v2


def _small_adam(red_a, red_b, dm_all, params):
    n = len(params)

    def body(*refs):
        ra, rb, dm = refs[:3]
        wmv = refs[3:3 + 3 * n]
        outs = refs[3 + 3 * n:]
        x, y, _ = _pos()
        chip = 2 * x + y

        def shard(row0, nrows, width):
            per_row = D // width
            cands = []
            for k in range(N_CHIP):
                if nrows == 1 or per_row >= N_CHIP:
                    cands.append(ra[row0:row0 + nrows, k * width:(k + 1) * width])
                else:
                    rr, cc = divmod(k * width, D)
                    cands.append(ra[row0 + rr:row0 + rr + 1, cc:cc + width])
            g = cands[0]
            for k in range(1, N_CHIP):
                g = jnp.where(chip == k, cands[k], g)
            return g

        dms = jnp.sum(dm[...], axis=0, keepdims=True)
        hw = LRU_HEADS * LRU_HEAD_DIM
        grads = [
            ra[ROW_NORM_G:ROW_NORM_G + 2, :],
            None,
            shard(ROW_CONV_W, 4, D // N_CHIP),
            ra[ROW_CONV_B:ROW_CONV_B + 1, :],
            rb[0:hw, :],
            ra[ROW_B_A:ROW_B_A + 1, :],
            rb[hw:2 * hw, :],
            ra[ROW_B_X:ROW_B_X + 1, :],
            ra[ROW_LAMBDA:ROW_LAMBDA + 1, :],
            shard(ROW_SC_W, 3, D // N_CHIP),
            shard(ROW_POOL_B, 2, 2 * D // N_CHIP),
            shard(ROW_POOL_S, 2, 2 * D // N_CHIP),
            ra[ROW_FINAL_G:ROW_FINAL_G + 1, :],
        ]
        for p in range(n):
            w_ref, m_ref, v_ref = wmv[3 * p:3 * p + 3]
            g_out, d_out, m_out, v_out = outs[4 * p:4 * p + 4]
            if grads[p] is None:
                for l in range(2):
                    g = dms[:, l * 3 * D:(l + 1) * 3 * D]
                    dl, m2, v2 = _adam(w_ref[l:l + 1, :], g, m_ref[l:l + 1, :], v_ref[l:l + 1, :])
                    g_out[l:l + 1, :] = g
                    d_out[l:l + 1, :] = dl
                    m_out[l:l + 1, :] = m2
                    v_out[l:l + 1, :] = v2
            else:
                g = grads[p]
                dl, m2, v2 = _adam(w_ref[...], g, m_ref[...], v_ref[...])
                g_out[...] = g
                d_out[...] = dl
                m_out[...] = m2
                v_out[...] = v2

    flat = [a for p in params for a in p]
    return pl.pallas_call(
        body, name="small_adam", in_specs=[VMEM] * (3 + len(flat)), out_specs=[VMEM] * (4 * n),
        out_shape=[jax.ShapeDtypeStruct(p[0].shape, F32) for p in params for _ in range(4)],
        compiler_params=_cp(),
    )(red_a, red_b, dm_all, *flat)


def _modw_adam(ca_t, dm_sh, w, m, v):
    nw = w.shape[2]

    def body(c_ref, d_ref, w_ref, m_ref, v_ref, g_out, d_out, m_out, v_out):
        g = jnp.dot(c_ref[...], d_ref[...], precision=lax.Precision.HIGHEST, preferred_element_type=F32)
        dl, m2, v2 = _adam(w_ref[...], g, m_ref[...], v_ref[...])
        g_out[...] = g
        d_out[...] = dl
        m_out[...] = m2
        v_out[...] = v2

    blk = pl.BlockSpec((None, D, nw), lambda l: (l, 0, 0))
    return pl.pallas_call(
        body, name="modw_adam", grid=(2,),
        in_specs=[pl.BlockSpec((D, SUBLANES), lambda l: (0, 0)), pl.BlockSpec((None, SUBLANES, nw), lambda l: (l, 0, 0)),
                  blk, blk, blk],
        out_specs=[blk] * 4, out_shape=[jax.ShapeDtypeStruct(w.shape, F32)] * 4,
        compiler_params=_cp(("arbitrary",)),
    )(ca_t, dm_sh, w, m, v)


def _half_rows(r):
    return r // 2


def _sib_send_halves(gs):
    n = len(gs)

    def body(*refs):
        ins, outs, ssem, rsem = refs[:n], refs[n:2 * n], refs[2 * n], refs[2 * n + 1]
        x, y, c = _pos()
        cps = []
        for a in range(n):
            hr = _half_rows(gs[a].shape[1])
            cp = pltpu.make_async_remote_copy(
                src_ref=ins[a].at[:, pl.ds(pl.multiple_of((1 - c) * hr, SUBLANES), hr), :], dst_ref=outs[a],
                send_sem=ssem.at[a], recv_sem=rsem.at[a], device_id=(x, y, 1 - c), device_id_type=MESH)
            cp.start()
            cps.append(cp)
        for cp in cps:
            cp.wait()

    return pl.pallas_call(
        body, name="grad_sib_halves", in_specs=[ANY] * n, out_specs=[ANY] * n,
        out_shape=[jax.ShapeDtypeStruct((N_CHIP, _half_rows(g.shape[1]), g.shape[2]), F32) for g in gs],
        scratch_shapes=[pltpu.SemaphoreType.DMA((n,)), pltpu.SemaphoreType.DMA((n,))],
        compiler_params=_cp(),
    )(*gs)


def _add_half(g, got, cidx, name):
    _, hr, cc = got.shape
    rb = min(hr, 256)

    def body(c_ref, g_ref, r_ref, o_ref):
        o_ref[...] = (g_ref[...] + r_ref[...]).astype(o_ref.dtype)

    blk = pl.BlockSpec((None, rb, cc), lambda k, j, c_ref: (k, j, 0))
    return pl.pallas_call(
        body, name=name,
        grid_spec=pltpu.PrefetchScalarGridSpec(
            num_scalar_prefetch=1, grid=(N_CHIP, hr // rb),
            in_specs=[pl.BlockSpec((None, rb, cc), lambda k, j, c_ref: (k, c_ref[0] * (hr // rb) + j, 0)), blk],
            out_specs=blk),
        out_shape=jax.ShapeDtypeStruct(got.shape, GRAD_WIRE_DTYPE),
        compiler_params=_cp(("parallel", "parallel")),
    )(cidx, g, got)


def _chip_scatter(ps):
    n = len(ps)

    def body(*refs):
        ins, outs, ssem, rsem = refs[:n], refs[n:2 * n], refs[2 * n], refs[2 * n + 1]
        x, y, c = _pos()
        cps = []
        for a in range(n):
            for q, (fx, fy) in enumerate(((1, 0), (0, 1), (1, 1))):
                px, py = _flip(x, fx), _flip(y, fy)
                cp = pltpu.make_async_remote_copy(
                    src_ref=ins[a].at[2 * px + py], dst_ref=outs[a].at[q],
                    send_sem=ssem.at[3 * a + q], recv_sem=rsem.at[3 * a + q], device_id=(px, py, c), device_id_type=MESH)
                cp.start()
                cps.append(cp)
        for cp in cps:
            cp.wait()

    return pl.pallas_call(
        body, name="grad_chip_scatter", in_specs=[ANY] * n, out_specs=[ANY] * n,
        out_shape=[jax.ShapeDtypeStruct((N_CHIP - 1,) + p.shape[1:], p.dtype) for p in ps],
        scratch_shapes=[pltpu.SemaphoreType.DMA((3 * n,)), pltpu.SemaphoreType.DMA((3 * n,))],
        compiler_params=_cp(),
    )(*ps)


def _add_owner(p, got, chipidx, name):
    _, hr, cc = p.shape
    rb = min(hr, 256)

    def body(k_ref, p_ref, r_ref, o_ref):
        o_ref[...] = ((p_ref[...].astype(F32) + r_ref[0].astype(F32)) + r_ref[1].astype(F32)) + r_ref[2].astype(F32)

    return pl.pallas_call(
        body, name=name,
        grid_spec=pltpu.PrefetchScalarGridSpec(
            num_scalar_prefetch=1, grid=(hr // rb,),
            in_specs=[pl.BlockSpec((None, rb, cc), lambda j, k_ref: (k_ref[0], j, 0)),
                      pl.BlockSpec((N_CHIP - 1, rb, cc), lambda j, k_ref: (0, j, 0))],
            out_specs=pl.BlockSpec((rb, cc), lambda j, k_ref: (j, 0))),
        out_shape=jax.ShapeDtypeStruct((hr, cc), F32),
        compiler_params=_cp(("parallel",)),
    )(chipidx, p, got)


def _sib_exchange(ts_):
    n = len(ts_)

    def body(*refs):
        ins, outs, ssem, rsem = refs[:n], refs[n:2 * n], refs[2 * n], refs[2 * n + 1]
        x, y, c = _pos()
        cps = []
        for a in range(n):
            cp = pltpu.make_async_remote_copy(src_ref=ins[a], dst_ref=outs[a], send_sem=ssem.at[a],
                                              recv_sem=rsem.at[a], device_id=(x, y, 1 - c), device_id_type=MESH)
            cp.start()
            cps.append(cp)
        for cp in cps:
            cp.wait()

    return pl.pallas_call(
        body, name="grad_sib_exchange", in_specs=[ANY] * n, out_specs=[ANY] * n,
        out_shape=[jax.ShapeDtypeStruct(t.shape, F32) for t in ts_],
        scratch_shapes=[pltpu.SemaphoreType.DMA((n,))] * 2,
        compiler_params=_cp(),
    )(*ts_)


def _adam_2d(w, g_own, g_sib, m, v, cidx, name):
    rr, cc = w.shape
    hr = rr // 2
    rb = min(hr, 256)
    nb = hr // rb

    def body(c_ref, w_ref, go_ref, gs_ref, m_ref, v_ref, g_out, d_out, m_out, v_out):
        g = jnp.where(pl.program_id(0) == c_ref[0], go_ref[...], gs_ref[...])
        dl, m2, v2 = _adam(w_ref[...], g, m_ref[...], v_ref[...])
        g_out[...] = g
        d_out[...] = dl
        m_out[...] = m2
        v_out[...] = v2

    blk = pl.BlockSpec((rb, cc), lambda h, j, c_ref: (h * nb + j, 0))
    hblk = pl.BlockSpec((rb, cc), lambda h, j, c_ref: (j, 0))
    return pl.pallas_call(
        body, name=name,
        grid_spec=pltpu.PrefetchScalarGridSpec(
            num_scalar_prefetch=1, grid=(2, nb), in_specs=[blk, hblk, hblk, blk, blk], out_specs=[blk] * 4),
        out_shape=[jax.ShapeDtypeStruct((rr, cc), F32)] * 4, compiler_params=_cp(("parallel", "parallel")),
    )(cidx, w, g_own, g_sib, m, v)


def kernel(x, c, norm_g, mod_w, mod_b, hy_w_in, hy_conv_w, hy_conv_b, lru_w_a, lru_b_a, lru_w_x, lru_b_x, lru_lambda, sc_conv_w, hy_w_out, pool_w_in, pool_w_grp, pool_b_grp, pool_scale, pool_w_out, final_g, loss_target, m_norm_g, m_mod_w, m_mod_b, m_hy_w_in, m_hy_conv_w, m_hy_conv_b, m_lru_w_a, m_lru_b_a, m_lru_w_x, m_lru_b_x, m_lru_lambda, m_sc_conv_w, m_hy_w_out, m_pool_w_in, m_pool_w_grp, m_pool_b_grp, m_pool_scale, m_pool_w_out, m_final_g, v_norm_g, v_mod_w, v_mod_b, v_hy_w_in, v_hy_conv_w, v_hy_conv_b, v_lru_w_a, v_lru_b_a, v_lru_w_x, v_lru_b_x, v_lru_lambda, v_sc_conv_w, v_hy_w_out, v_pool_w_in, v_pool_w_grp, v_pool_b_grp, v_pool_scale, v_pool_w_out, v_final_g):
    ax, ay, ac = _pos()
    me = 4 * ax + 2 * ay + ac
    chip = 2 * ax + ay
    xs = x[0]
    tgt = loss_target[0]
    gd = POOL_GROUP_DIM

    ca_all, mod_all, small_w = _mod_fwd(jnp.broadcast_to(c, (SUBLANES, D)), mod_w, mod_b,
                                        hy_conv_w[0], sc_conv_w[0], pool_b_grp, pool_scale)
    mod_me = lax.dynamic_index_in_dim(mod_all, me, axis=1, keepdims=False)
    sh0, sc0, gt0 = (mod_me[0:1, k * D:(k + 1) * D] for k in range(3))
    sh1, sc1, gt1 = (mod_me[1:2, k * D:(k + 1) * D] for k in range(3))
    cw = small_w[SW_CONV:SW_CONV + 4, 0:D]
    sw = small_w[SW_SC:SW_SC + 3, 0:D]
    pool_b = small_w[SW_POOL_B:SW_POOL_B + 1, :]
    pool_s = small_w[SW_POOL_S:SW_POOL_S + 1, :]
    g0, g1, gf = norm_g[0:1], norm_g[1:2], final_g.reshape(1, D)
    cb, ba, bx, lam = hy_conv_b, lru_b_a, lru_b_x, lru_lambda

    big = [hy_w_in[0], hy_w_out[0], pool_w_in[0], pool_w_grp[0].reshape(4 * 128, gd), pool_w_out[0]]
    cidx = ac.reshape(1).astype(jnp.int32)
    kidx = chip.reshape(1).astype(jnp.int32)
    w_in0, w_out0, w_in1, w_grp, w_out1 = _wgather(
        [_wcast_own_block(w, kidx, f"wcast_own_block_{a}") for a, w in enumerate(big)])
    w_grp =w_grp.reshape(N_CHIP, 4, 128, gd).transpose(1, 0, 2, 3).reshape(4, gd, gd)
    wa_b, wx_b = _wcast([lru_w_a[0], lru_w_x[0]])

    x1, hst, y0, h0, proj0 = _l0_fwd(xs, g0, sc0, sh0, w_in0, gt0, cw, cb, wa_b, ba, wx_b, bx, lam, sw,
                                     w_out0.reshape(2 * D, D))
    h1, proj1 = _norm_proj(x1, g1, sc1, sh1, w_in1, "l1_proj")
    dpool, mixed, y1, dx2, losscols, dgf = _l1_mix(proj1, x1, tgt, gt1, w_grp, pool_b, pool_s,
                                                    w_out1.reshape(2 * D, D), gf)

    dproj1, mt1, d_wgrp, dsc1, dbg1 = _l1_bwd_mix(dx2, proj1, mixed, y1, dpool, gt1, w_grp, pool_s,
                                                  w_out1.reshape(2 * D, D))
    d_win1 = _wgrad(h1, dproj1, N_CHIP, D, D, lambda g: 0, lambda g: g, "l1_wgrad_in")
    dx1, s1_1, s2_1 = _dgrad_norm(dproj1, w_in1, x1, dx2, g1, sc1, "l1_bwd_proj")
    d_wout1, dgate1 = _wo_final(mt1, w_out1, gt1, "l1_wo_final")

    dproj0, mt0, d_wa, d_wx, sm0 = _l0_bwd_mix(dx1, proj0, hst, y0, gt0, cw, cb, wa_b, ba, wx_b, bx, lam, sw,
                                               w_out0.reshape(2 * D, D))
    d_win0 = _wgrad(h0, dproj0, N_CHIP, D, 6 * D // N_CHIP, lambda g: 0, lambda g: g, "l0_wgrad_in")
    grad_x, s1_0, s2_0 = _dgrad_norm(dproj0, w_in0, xs, dx1, g0, sc0, "l0_bwd_proj")
    d_wout0, dgate0 = _wo_final(mt0, w_out0, gt0, "l0_wo_final")

    buf_a, dmod8, loss8 = _small_pack(s1_0, s2_0, s1_1, s2_1, sm0, dsc1, dbg1, dgf, losscols, dgate0, dgate1,
                                      norm_g, sc0, sc1, lam)
    hw = LRU_HEADS * LRU_HEAD_DIM
    buf_b = jnp.concatenate([d_wa.reshape(hw, LRU_HEAD_DIM), d_wx.reshape(hw, LRU_HEAD_DIM)], axis=0)
    red_a, red_b, dm_all = _small_comm(buf_a, buf_b, dmod8)
    small = [(norm_g, m_norm_g, v_norm_g), (mod_b, m_mod_b, v_mod_b),
             (hy_conv_w[0], m_hy_conv_w[0], v_hy_conv_w[0]), (hy_conv_b, m_hy_conv_b, v_hy_conv_b),
             tuple(a.reshape(hw, LRU_HEAD_DIM) for a in (lru_w_a, m_lru_w_a, v_lru_w_a)),
             (lru_b_a, m_lru_b_a, v_lru_b_a),
             tuple(a.reshape(hw, LRU_HEAD_DIM) for a in (lru_w_x, m_lru_w_x, v_lru_w_x)),
             (lru_b_x, m_lru_b_x, v_lru_b_x), (lru_lambda, m_lru_lambda, v_lru_lambda),
             (sc_conv_w[0], m_sc_conv_w[0], v_sc_conv_w[0]), (pool_b_grp, m_pool_b_grp, v_pool_b_grp),
             (pool_scale, m_pool_scale, v_pool_scale),
             tuple(a.reshape(1, D) for a in (final_g, m_final_g, v_final_g))]
    small_names = ["norm_g", "mod_b", "hy_conv_w", "hy_conv_b", "lru_w_a", "lru_b_a", "lru_w_x", "lru_b_x",
                   "lru_lambda", "sc_conv_w", "pool_b_grp", "pool_scale", "final_g"]
    small_out = _small_adam(red_a, red_b, dm_all, small)
    res = {}
    shapes = dict(norm_g=norm_g, mod_b=mod_b, hy_conv_w=hy_conv_w, hy_conv_b=hy_conv_b, lru_w_a=lru_w_a, lru_b_a=lru_b_a,
                  lru_w_x=lru_w_x, lru_b_x=lru_b_x, lru_lambda=lru_lambda, sc_conv_w=sc_conv_w, pool_b_grp=pool_b_grp,
                  pool_scale=pool_scale, final_g=final_g)
    for p, nm in enumerate(small_names):
        res[nm] = tuple(o.reshape(shapes[nm].shape) for o in small_out[4 * p:4 * p + 4])

    nw = mod_w.shape[2]
    dm_sh = jnp.stack([lax.dynamic_slice_in_dim(dm_all[:, l * 3 * D:(l + 1) * 3 * D], chip * nw, nw, axis=1)
                       for l in range(2)])
    res["mod_w"] = tuple(_modw_adam(ca_all.T, dm_sh, mod_w, m_mod_w, v_mod_w))

    d_wgrp = d_wgrp.reshape(4, N_CHIP, 128, gd).transpose(1, 0, 2, 3).reshape(N_CHIP, 4 * 128, gd)
    grads = [d_win0, d_wout0, d_win1, d_wgrp, d_wout1]
    got = _sib_send_halves(grads)
    parts = [_add_half(g, r, cidx, f"grad_add_half_{a}") for a, (g, r) in enumerate(zip(grads, got))]
    got2 = _chip_scatter(parts)
    halves = [_add_owner(p, r, kidx, f"grad_add_owner_{a}") for a, (p, r) in enumerate(zip(parts, got2))]
    sib_halves = _sib_exchange(halves)
    big_names = ["hy_w_in", "hy_w_out", "pool_w_in", "pool_w_grp", "pool_w_out"]
    big_wmv = [(hy_w_in, m_hy_w_in, v_hy_w_in), (hy_w_out, m_hy_w_out, v_hy_w_out), (pool_w_in, m_pool_w_in, v_pool_w_in),
               (pool_w_grp, m_pool_w_grp, v_pool_w_grp), (pool_w_out, m_pool_w_out, v_pool_w_out)]
    for a, nm in enumerate(big_names):
        rr, cc = big[a].shape
        w, m, v = (t.reshape(rr, cc) for t in big_wmv[a])
        outs = _adam_2d(w, halves[a], sib_halves[a], m, v, cidx, f"adam_{nm}")
        res[nm] = tuple(o.reshape(big_wmv[a][0].shape) for o in outs)

    loss = lax.psum(loss8[0, 0], ("x", "y", "c"))
    order = ["norm_g", "mod_w", "mod_b", "hy_w_in", "hy_conv_w", "hy_conv_b", "lru_w_a", "lru_b_a", "lru_w_x", "lru_b_x",
             "lru_lambda", "sc_conv_w", "hy_w_out", "pool_w_in", "pool_w_grp", "pool_b_grp", "pool_scale", "pool_w_out",
             "final_g"]
    return (loss, grad_x[None], *[res[nm][0] for nm in order], *[res[nm][1] for nm in order],
            *[res[nm][2] for nm in order], *[res[nm][3] for nm in order])
```

```python
import jax
import jax.numpy as jnp
from jax import lax
from jax.experimental import pallas as pl
from jax.experimental.pallas import tpu as pltpu
from jax.experimental.pallas import tpu_sc as plsc

F32, BF16 = jnp.float32, jnp.bfloat16
D = 1024
RMS_EPS = 1e-6
SQRT_FLOOR = 1e-30
LRU_C = 8.0
LRU_HEADS, LRU_HEAD_DIM = 8, 128
POOL_WINDOWS = (2, 4, 8, 16)
POOL_GROUP_DIM = 512
ADAM_LR, ADAM_B1, ADAM_B2, ADAM_EPS, ADAM_WD, ADAM_STEP = 0.001, 0.9, 0.999, 1e-08, 0.01, 10
MESH = pl.DeviceIdType.MESH
CID_WGATHER = 1
N_DEV, N_CHIP = 8, 4
SUBLANES = 8
BF16_ROWS = 16
POOL_HALO = 16
TS_PROJ, TS_MIX, TS_WGRAD, TS_DGRAD = 1024, 256, 1024, 256
SMALL_ROWS = 64
GRAD_WIRE_DTYPE = BF16
ANY = pl.BlockSpec(memory_space=pl.ANY)
VMEM = pl.BlockSpec(memory_space=pltpu.VMEM)
NT = (((1,), (1,)), ((), ()))
TN = (((0,), (0,)), ((), ()))


def _cp(sem=None, vmem_mb=56):
    kw = dict(vmem_limit_bytes=vmem_mb * 2 ** 20)
    if sem is not None:
        kw["dimension_semantics"] = sem
    return pltpu.CompilerParams(**kw)


def _tile(n, t):
    return min(n, t)


def _pos():
    return lax.axis_index("x"), lax.axis_index("y"), lax.axis_index("c")


def _flip(v, f):
    return 1 - v if f else v


def _sigmoid(z):
    return 0.5 * jnp.tanh(0.5 * z) + 0.5


def _rows(n, c):
    return lax.broadcasted_iota(jnp.int32, (n, c), 0)


def _down(a, d):
    return a if d == 0 else pltpu.roll(a, d, 0)


def _up(a, d):
    return a if d == 0 else pltpu.roll(a, a.shape[0] - d, 0)


def _scan_fwd_steps(a, u, carry):
    n, c = a.shape
    sub = _rows(SUBLANES, c)
    out = []
    for k in range(n // SUBLANES):
        p = a[k * SUBLANES:(k + 1) * SUBLANES]
        g = u[k * SUBLANES:(k + 1) * SUBLANES]
        for d in (1, 2, 4):
            keep = sub >= d
            g = g + p * jnp.where(keep, pltpu.roll(g, d, 0), 0.0)
            p = p * jnp.where(keep, pltpu.roll(p, d, 0), 1.0)
        h = g + p * carry
        carry = h[SUBLANES - 1:SUBLANES, :]
        out.append(h)
        yield
    return jnp.concatenate(out, axis=0)


def _scan_rev_steps(alpha, b, carry):
    n, c = alpha.shape
    sub = _rows(SUBLANES, c)
    out = []
    for k in reversed(range(n // SUBLANES)):
        p = alpha[k * SUBLANES:(k + 1) * SUBLANES]
        g = b[k * SUBLANES:(k + 1) * SUBLANES]
        for d in (1, 2, 4):
            keep = sub < SUBLANES - d
            g = g + p * jnp.where(keep, pltpu.roll(g, SUBLANES - d, 0), 0.0)
            p = p * jnp.where(keep, pltpu.roll(p, SUBLANES - d, 0), 1.0)
        h = g + p * carry
        carry = h[0:1, :]
        out.append(h)
        yield
    return jnp.concatenate(out[::-1], axis=0)


def _run(steps):
    while True:
        try:
            next(steps)
        except StopIteration as done:
            return done.value


def _paired(progress, pieces):
    n, done = len(pieces), 1
    pieces[0]()
    for frac in progress:
        while done < n and done <= frac * n:
            pieces[done]()
            done += 1
    while done < n:
        pieces[done]()
        done += 1


def _conv_taps(ext, halo, n, width):
    return [_down(ext, width - 1 - k)[halo:halo + n] for k in range(width)]


def _lru_gates(xc, wa_ref, ba, wx_ref, bx):
    xb = xc.astype(BF16)
    pa, px = [], []
    for h in range(LRU_HEADS):
        xh = xb[:, h * LRU_HEAD_DIM:(h + 1) * LRU_HEAD_DIM]
        pa.append(jnp.dot(xh, wa_ref[h], preferred_element_type=F32))
        px.append(jnp.dot(xh, wx_ref[h], preferred_element_type=F32))
    r = _sigmoid(jnp.concatenate(pa, axis=1) + ba)
    ig = _sigmoid(jnp.concatenate(px, axis=1) + bx)
    return r, ig


def _softplus_neg(lam):
    return jnp.maximum(-lam, 0.0) + jnp.log1p(jnp.exp(-jnp.abs(lam)))


def _recip_1_to_2(d):
    r0 = pl.reciprocal(d, approx=True)
    return r0 * (2.0 - d * r0)


def _lru_decay(r, sp, first):
    big_l = (-LRU_C) * r * sp
    a = jnp.exp(big_l)
    th = jnp.tanh(big_l)
    q = (-2.0 * th) * _recip_1_to_2(1.0 - th)
    rs = lax.rsqrt(jnp.maximum(q, SQRT_FLOOR))
    return a, jnp.where(first, 1.0, q * rs), rs


def _pool_inv_counts(t0, n):
    t = (t0 + lax.broadcasted_iota(jnp.int32, (n, 1), 0) + 1).astype(F32)
    return [1.0 / jnp.minimum(t, float(w)) for w in POOL_WINDOWS]


def _window_sums(ext, shift):
    gd = POOL_GROUP_DIM
    out = []
    s = ext
    for k in range(len(POOL_WINDOWS)):
        s = s + shift(s, 2 ** k)
        out.append(s[:, 0:gd])
        if k + 1 < len(POOL_WINDOWS):
            s = s[:, gd:]
    return out


SW_ROWS, SW_COLS = 16, 2 * D
SW_CONV, SW_SC, SW_POOL_B, SW_POOL_S = 0, 4, 8, 9


def _mod_fwd(c8, mod_w, mod_b, conv_w, sc_w, pool_b, pool_s):
    nw = mod_w.shape[2]
    cq, pq = conv_w.shape[1], pool_b.shape[1]

    def body(c_ref, w_ref, b_ref, cw_ref, sw_ref, pb_ref, ps_ref, ca_ref, mod_ref, small_ref,
             cslot, mslot, msend, pslot, psend, s1, r1, s2, r2, s3, r3):
        x, y, c = _pos()
        me = 4 * x + 2 * y + c
        chip = 2 * x + y
        first = []
        for r in range(1, N_DEV):
            fx, fy, fc = (r >> 2) & 1, (r >> 1) & 1, r & 1
            cp = pltpu.make_async_remote_copy(
                src_ref=c_ref, dst_ref=cslot.at[me], send_sem=s1.at[r - 1], recv_sem=r1.at[r - 1],
                device_id=(_flip(x, fx), _flip(y, fy), _flip(c, fc)), device_id_type=MESH)
            cp.start()
            first.append(cp)
        cslot[me] = c_ref[...]
        for cp in first:
            cp.wait()
        rows = _rows(SUBLANES, D)
        call = jnp.zeros((SUBLANES, D), F32)
        for d in range(N_DEV):
            call = jnp.where(rows == d, cslot[d], call)
        ca = call * _sigmoid(call)
        ca_ref[...] = ca
        for l in range(2):
            msend[l] = jnp.dot(ca, w_ref[l], precision=lax.Precision.HIGHEST, preferred_element_type=F32)
        psend[...] = jnp.zeros_like(psend)
        psend[SW_CONV:SW_CONV + 4, 0:cq] = cw_ref[...]
        psend[SW_SC:SW_SC + 3, 0:cq] = sw_ref[...]
        psend[SW_POOL_B:SW_POOL_B + 1, :] = pb_ref[...]
        psend[SW_POOL_S:SW_POOL_S + 1, :] = ps_ref[...]
        second = []
        for q, (fx, fy) in enumerate(((1, 0), (0, 1), (1, 1))):
            peer = (_flip(x, fx), _flip(y, fy), c)
            for src, dst, ss, rs in ((msend, mslot, s2, r2), (psend, pslot, s3, r3)):
                cp = pltpu.make_async_remote_copy(src_ref=src, dst_ref=dst.at[chip], send_sem=ss.at[q], recv_sem=rs.at[q],
                                                  device_id=peer, device_id_type=MESH)
                cp.start()
                second.append(cp)
        mslot[chip] = msend[...]
        pslot[chip] = psend[...]
        for cp in second:
            cp.wait()
        small_ref[...] = jnp.zeros_like(small_ref)
        for j in range(N_CHIP):
            for l in range(2):
                mod_ref[l, :, j * nw:(j + 1) * nw] = mslot[j, l] + b_ref[l:l + 1, j * nw:(j + 1) * nw]
            small_ref[0:SUBLANES, j * cq:(j + 1) * cq] = pslot[j, 0:SUBLANES, 0:cq]
            small_ref[SUBLANES:SW_ROWS, j * pq:(j + 1) * pq] = pslot[j, SUBLANES:SW_ROWS, :]

    args = (c8, mod_w, mod_b, conv_w, sc_w, pool_b, pool_s)
    dma3 = pltpu.SemaphoreType.DMA((N_CHIP - 1,))
    return pl.pallas_call(
        body, name="mod_fwd",
        in_specs=[VMEM] * len(args), out_specs=[VMEM] * 3,
        out_shape=[jax.ShapeDtypeStruct((SUBLANES, D), F32), jax.ShapeDtypeStruct((2, SUBLANES, N_CHIP * nw), F32),
                   jax.ShapeDtypeStruct((SW_ROWS, SW_COLS), F32)],
        scratch_shapes=[pltpu.VMEM((N_DEV, SUBLANES, D), F32), pltpu.VMEM((N_CHIP, 2, SUBLANES, nw), F32),
                        pltpu.VMEM((2, SUBLANES, nw), F32), pltpu.VMEM((N_CHIP, SW_ROWS, pq), F32),
                        pltpu.VMEM((SW_ROWS, pq), F32),
                        pltpu.SemaphoreType.DMA((N_DEV - 1,)), pltpu.SemaphoreType.DMA((N_DEV - 1,)),
                        dma3, dma3, dma3, dma3],
        compiler_params=_cp(),
    )(*args)


def _wcast(ws):
    def body(*refs):
        n = len(refs) // 2
        for a in range(n):
            refs[n + a][...] = refs[a][...].astype(BF16)

    return pl.pallas_call(
        body, name="wcast", in_specs=[VMEM] * len(ws), out_specs=[VMEM] * len(ws),
        out_shape=[jax.ShapeDtypeStruct(w.shape, BF16) for w in ws], compiler_params=_cp(),
    )(*ws)


def _wcast_own_block(w, kidx, name):
    rr, cc = w.shape
    rb = min(rr, 256)

    def body(k_ref, w_ref, o_ref):
        o_ref[...] = w_ref[...].astype(BF16)

    return pl.pallas_call(
        body, name=name,
        grid_spec=pltpu.PrefetchScalarGridSpec(
            num_scalar_prefetch=1, grid=(rr // rb,),
            in_specs=[pl.BlockSpec((rb, cc), lambda j, k_ref: (j, 0))],
            out_specs=pl.BlockSpec((None, rb, cc), lambda j, k_ref: (k_ref[0], j, 0))),
        out_shape=jax.ShapeDtypeStruct((N_CHIP, rr, cc), BF16),
        compiler_params=_cp(("parallel",)),
    )(kidx, w)


def _wgather_copies(outs, rows, ssem, rsem, fssem, frsem):
    n = len(outs)
    x, y, c = _pos()
    chip = 2 * x + y
    sib = (x, y, 1 - c)
    flips = ((1, 0), (0, 1), (1, 1))

    def half(a, which):
        hr = rows[a] // 2
        return pl.ds(pl.multiple_of(which * hr, BF16_ROWS), hr)

    sends = []
    for a in range(n):
        mine = outs[a].at[chip, half(a, c), :]
        for q, (fx, fy) in enumerate(flips):
            cp = pltpu.make_async_remote_copy(
                src_ref=mine, dst_ref=mine, send_sem=ssem.at[3 * a + q], recv_sem=rsem.at[3 * a + q],
                device_id=(_flip(x, fx), _flip(y, fy), c), device_id_type=MESH)
            cp.start()
            sends.append(cp)
    passed = []
    for a in range(n):
        for q, (fx, fy) in enumerate(flips):
            src_chip = 2 * _flip(x, fx) + _flip(y, fy)
            landed = outs[a].at[src_chip, half(a, c), :]
            pltpu.make_async_remote_copy(
                src_ref=landed, dst_ref=landed, send_sem=ssem.at[3 * a + q], recv_sem=rsem.at[3 * a + q],
                device_id=sib, device_id_type=MESH).wait_recv()
            cp = pltpu.make_async_remote_copy(
                src_ref=landed, dst_ref=landed, send_sem=fssem.at[3 * a + q], recv_sem=frsem.at[3 * a + q],
                device_id=sib, device_id_type=MESH)
            cp.start()
            passed.append(cp)
    for a in range(n):
        for q, (fx, fy) in enumerate(flips):
            src_chip = 2 * _flip(x, fx) + _flip(y, fy)
            other = outs[a].at[src_chip, half(a, 1 - c), :]
            pltpu.make_async_remote_copy(
                src_ref=other, dst_ref=other, send_sem=fssem.at[3 * a + q], recv_sem=frsem.at[3 * a + q],
                device_id=sib, device_id_type=MESH).wait_recv()
    for cp in sends + passed:
        cp.wait_send()


def _wgather(bufs, name):
    n = len(bufs)

    def body(*refs):
        _wgather_copies(refs[n:2 * n], [b.shape[1] for b in bufs], *refs[2 * n:])

    return pl.pallas_call(
        body, name=name, in_specs=[ANY] * n, out_specs=[ANY] * n,
        out_shape=[jax.ShapeDtypeStruct(b.shape, BF16) for b in bufs],
        input_output_aliases={a: a for a in range(n)},
        scratch_shapes=[pltpu.SemaphoreType.DMA((3 * n,))] * 4,
        compiler_params=_cp(),
    )(*bufs)


def _wgather_sequencer(bufs, name):
    n = len(bufs)
    refs = [jax.new_ref(b, memory_space=pltpu.MemorySpace.HBM) for b in bufs]
    dma = pltpu.SemaphoreType.DMA((3 * n,))

    @pl.kernel(mesh=plsc.ScalarSubcoreMesh(axis_name="sequencer", num_cores=1), name=name,
               scratch_types=(dma, dma, dma, dma), compiler_params=pltpu.CompilerParams(collective_id=CID_WGATHER))
    def launch(ssem, rsem, fssem, frsem):
        x, y, c = _pos()
        barrier = pltpu.get_barrier_semaphore()
        for peer in ((1 - x, y, c), (x, 1 - y, c), (1 - x, 1 - y, c), (x, y, 1 - c)):
            pl.semaphore_signal(barrier, inc=1, device_id=peer, device_id_type=MESH)
        pl.semaphore_wait(barrier, 4)
        _wgather_copies(refs, [b.shape[1] for b in bufs], ssem, rsem, fssem, frsem)

    launch()
    return [r[...] for r in refs]


def _norm_proj(x, g, sc, sh, w, name):
    s_len, nb = x.shape[0], w.shape[2]
    ts = _tile(s_len, TS_PROJ)

    def body(x_ref, g_ref, sc_ref, sh_ref, w_ref, h_ref, p_ref):
        @pl.when(pl.program_id(1) == 0)
        def _():
            xv = x_ref[...]
            r = lax.rsqrt(jnp.mean(xv * xv, axis=-1, keepdims=True) + RMS_EPS)
            h_ref[...] = (xv * r * (g_ref[...] * (1.0 + sc_ref[...])) + sh_ref[...]).astype(BF16)

        p_ref[...] = jnp.dot(h_ref[...], w_ref[...], preferred_element_type=F32).astype(BF16)

    vec = pl.BlockSpec((1, D), lambda i, j: (0, 0))
    return pl.pallas_call(
        body, name=name, grid=(s_len // ts, N_CHIP),
        in_specs=[pl.BlockSpec((ts, D), lambda i, j: (i, 0)), vec, vec, vec,
                  pl.BlockSpec((None, D, nb), lambda i, j: (j, 0, 0))],
        out_specs=[pl.BlockSpec((ts, D), lambda i, j: (i, 0)), pl.BlockSpec((ts, nb), lambda i, j: (i, j))],
        out_shape=[jax.ShapeDtypeStruct((s_len, D), BF16), jax.ShapeDtypeStruct((s_len, N_CHIP * nb), BF16)],
        compiler_params=_cp(("parallel", "arbitrary")),
    )(x, g, sc, sh, w)


def _l0_fwd(x, g, sc, sh, w_in, gate, cw, cb, wa, ba, wx, bx, lam, sw, wo):
    s_len, nb = x.shape[0], w_in.shape[2]
    ts = _tile(s_len, TS_MIX)
    n_t = s_len // ts
    hl = SUBLANES

    def body(xa_ref, xb_ref, g_ref, sc_ref, sh_ref, win_ref, gate_ref, cw_ref, cb_ref, wa_ref, ba_ref, wx_ref, bx_ref,
             lam_ref, sw_ref, wo_ref, x1_ref, h_ref, y_ref, h0_ref, p_ref, pcur, pnext, cxa, czz, chh):
        i = pl.program_id(0)

        @pl.when(i == 0)
        def _():
            cxa[...] = jnp.zeros_like(cxa)
            czz[...] = jnp.zeros_like(czz)
            chh[...] = jnp.zeros_like(chh)
            pnext[...] = jnp.zeros_like(pnext)

        pcur[...] = pnext[...]
        xv = xa_ref[...]
        rinv = lax.rsqrt(jnp.mean(xv * xv, axis=-1, keepdims=True) + RMS_EPS)
        h0 = (xv * rinv * (g_ref[...] * (1.0 + sc_ref[...])) + sh_ref[...]).astype(BF16)
        h0_ref[...] = h0

        def project(k):
            def emit():
                pk = jnp.dot(h0, win_ref[k], preferred_element_type=F32).astype(BF16)
                p_ref[:, k * nb:(k + 1) * nb] = pk
                pnext[:, k * nb:(k + 1) * nb] = pk
            return emit

        def mixer():
            piece = lambda k: pcur[:, k * D:(k + 1) * D].astype(F32)
            xa = piece(0)
            rows = _rows(ts, D)
            taps = _conv_taps(jnp.concatenate([cxa[...], xa], axis=0), hl, ts, 4)
            xc = cb_ref[...] + sum(cw_ref[k:k + 1, :] * taps[k] for k in range(4))
            r, ig = _lru_gates(xc, wa_ref, ba_ref[...], wx_ref, bx_ref[...])
            a, m, _ = _lru_decay(r, _softplus_neg(lam_ref[...]), (rows == 0) & (i == 1))
            yield 0.26
            h = _run(_scan_fwd_steps(a, m * ig * xc, chh[hl - 1:hl, :]))
            yield 0.51
            gcp, v = piece(3), piece(4)
            z = gcp * v
            ztaps = _conv_taps(jnp.concatenate([czz[...], z], axis=0), hl, ts, 3)
            yb = piece(2) * sum(sw_ref[k:k + 1, :] * ztaps[k] for k in range(3))
            ga, gb = piece(1), piece(5)
            y = jnp.concatenate([h * (ga * _sigmoid(ga)), yb * (gb * _sigmoid(gb))], axis=1).astype(BF16)
            yield 0.76
            y_ref[...] = y
            x1_ref[...] = xb_ref[...] + gate_ref[...] * jnp.dot(y, wo_ref[...], preferred_element_type=F32)
            h_ref[...] = h.astype(BF16)
            cxa[...] = xa[ts - hl:, :]
            czz[...] = z[ts - hl:, :]
            chh[...] = jnp.where(i > 0, h[ts - hl:, :], 0.0)

        _paired(mixer(), [project(k) for k in range(N_CHIP)])

    def full(a):
        return pl.BlockSpec(a.shape, lambda i: (0,) * a.ndim)

    ahead = lambda w: pl.BlockSpec((ts, w), lambda i: (jnp.minimum(i, n_t - 1), 0))
    behind = lambda w: pl.BlockSpec((ts, w), lambda i: (jnp.maximum(i - 1, 0), 0))
    args = (x, x, g, sc, sh, w_in, gate, cw, cb, wa, ba, wx, bx, lam, sw, wo)
    return pl.pallas_call(
        body, name="l0_fwd", grid=(n_t + 1,),
        in_specs=[ahead(D), behind(D)] + [full(a) for a in args[2:]],
        out_specs=[behind(D), behind(D), behind(2 * D), ahead(D), ahead(N_CHIP * nb)],
        out_shape=[jax.ShapeDtypeStruct((s_len, D), F32), jax.ShapeDtypeStruct((s_len, D), BF16),
                   jax.ShapeDtypeStruct((s_len, 2 * D), BF16), jax.ShapeDtypeStruct((s_len, D), BF16),
                   jax.ShapeDtypeStruct((s_len, N_CHIP * nb), BF16)],
        scratch_shapes=[pltpu.VMEM((ts, N_CHIP * nb), BF16)] * 2 + [pltpu.VMEM((hl, D), F32)] * 3,
        compiler_params=_cp(("arbitrary",)),
    )(*args)


def _l1_mix(proj, x1, tgt, gate, wg, bg, scale, wo, gf):
    s_len = x1.shape[0]
    ts = _tile(s_len, TS_MIX)
    pw, gd, hl = 2 * D, POOL_GROUP_DIM, POOL_HALO

    def body(p_ref, x_ref, t_ref, gate_ref, wg_ref, bg_ref, sc_ref, wo_ref, gf_ref,
             d_ref, mx_ref, y_ref, dx_ref, loss_ref, dgf_ref, cv):
        i = pl.program_id(0)

        @pl.when(i == 0)
        def _():
            cv[...] = jnp.zeros_like(cv)
            loss_ref[...] = jnp.zeros_like(loss_ref)
            dgf_ref[...] = jnp.zeros_like(dgf_ref)

        v = p_ref[:, 0:pw].astype(F32)
        gg = p_ref[:, pw:2 * pw].astype(F32)
        sums = _window_sums(jnp.concatenate([cv[...], v], axis=0), _down)
        inv = _pool_inv_counts(i * ts, ts)
        dd = [sums[k][hl:hl + ts] * inv[k] - v[:, k * gd:(k + 1) * gd] for k in range(4)]
        mixed = jnp.concatenate(
            [jnp.dot(dd[k].astype(BF16), wg_ref[k], preferred_element_type=F32) for k in range(4)], axis=1) + bg_ref[...]
        d_ref[...] = jnp.concatenate(dd, axis=1).astype(BF16)
        mx_ref[...] = mixed.astype(BF16)
        y = (mixed * sc_ref[...] * (gg * _sigmoid(gg))).astype(BF16)
        y_ref[...] = y
        x2 = x_ref[...] + gate_ref[...] * jnp.dot(y, wo_ref[...], preferred_element_type=F32)
        r2 = lax.rsqrt(jnp.mean(x2 * x2, axis=-1, keepdims=True) + RMS_EPS)
        n2 = x2 * r2
        err = n2 * gf_ref[...] - t_ref[...]
        loss_ref[...] += jnp.sum(err * err, axis=0, keepdims=True)
        dyf = err * (1.0 / D)
        dgf_ref[...] += jnp.sum(dyf * n2, axis=0, keepdims=True)
        dn = dyf * gf_ref[...]
        dx_ref[...] = r2 * (dn - n2 * jnp.mean(dn * n2, axis=-1, keepdims=True))
        cv[...] = v[ts - hl:, :]

    def full(a):
        return pl.BlockSpec(a.shape, lambda i: (0,) * a.ndim)

    row = lambda w: pl.BlockSpec((ts, w), lambda i: (i, 0))
    acc = pl.BlockSpec((1, D), lambda i: (0, 0))
    return pl.pallas_call(
        body, name="l1_mix", grid=(s_len // ts,),
        in_specs=[row(2 * pw), row(D), row(D)] + [full(a) for a in (gate, wg, bg, scale, wo, gf)],
        out_specs=[row(pw), row(pw), row(pw), row(D), acc, acc],
        out_shape=[jax.ShapeDtypeStruct((s_len, pw), BF16)] * 3 + [jax.ShapeDtypeStruct((s_len, D), F32)]
        + [jax.ShapeDtypeStruct((1, D), F32)] * 2,
        scratch_shapes=[pltpu.VMEM((hl, pw), F32)],
        compiler_params=_cp(("arbitrary",)),
    )(proj, x1, tgt, gate, wg, bg, scale, wo, gf)


def _l1_bwd_mix(dx2, proj, mixed, y, dpool, gate, wg, scale, wo):
    s_len = dx2.shape[0]
    ts = _tile(s_len, TS_MIX)
    n_t = s_len // ts
    pw, gd, hl = 2 * D, POOL_GROUP_DIM, POOL_HALO

    def body(dx_ref, gg_ref, mx_ref, y_ref, d_ref, gate_ref, wg_ref, sc_ref, wo_ref,
             dp_ref, mt_ref, dwg_ref, dsc_ref, dbg_ref, cq):
        i = pl.program_id(0)

        @pl.when(i == 0)
        def _():
            cq[...] = jnp.zeros_like(cq)
            dsc_ref[...] = jnp.zeros_like(dsc_ref)
            dbg_ref[...] = jnp.zeros_like(dbg_ref)
            mt_ref[...] = jnp.zeros_like(mt_ref)
            dwg_ref[...] = jnp.zeros_like(dwg_ref)

        dxv = dx_ref[...]
        dxb = dxv.astype(BF16)

        def wgrad_out(k):
            mt_ref[k] += lax.dot_general(y_ref[:, k * gd:(k + 1) * gd], dxb, TN, preferred_element_type=F32)

        dy = lax.dot_general((gate_ref[...] * dxv).astype(BF16), wo_ref[...], NT, preferred_element_type=F32)
        wgrad_out(0)
        gg = gg_ref[...].astype(F32)
        mixed = mx_ref[...].astype(F32)
        s = _sigmoid(gg)
        sg = gg * s
        dmixed = dy * sc_ref[...] * sg
        dsc_ref[...] += jnp.sum(dy * mixed * sg, axis=0, keepdims=True)
        dbg_ref[...] += jnp.sum(dmixed, axis=0, keepdims=True)
        dmb = dmixed.astype(BF16)
        wgrad_out(1)
        dp_ref[:, pw:2 * pw] = (dy * sc_ref[...] * mixed * (s * (1.0 + gg * (1.0 - s)))).astype(BF16)
        inv = _pool_inv_counts((n_t - 1 - i) * ts, ts)
        dd = []
        for k in range(4):
            dmk = dmb[:, k * gd:(k + 1) * gd]
            dd.append(lax.dot_general(dmk, wg_ref[k], NT, preferred_element_type=F32))
            dwg_ref[k] += lax.dot_general(d_ref[:, k * gd:(k + 1) * gd], dmk, TN, preferred_element_type=F32)
        wgrad_out(2)
        q = jnp.concatenate([dd[k] * inv[k] for k in range(4)], axis=1)
        sums = _window_sums(jnp.concatenate([q, cq[...]], axis=0), _up)
        wgrad_out(3)
        dp_ref[:, 0:pw] = jnp.concatenate([sums[k][0:ts] - dd[k] for k in range(4)], axis=1).astype(BF16)
        cq[...] = q[0:hl, :]

    def full(a):
        return pl.BlockSpec(a.shape, lambda i: (0,) * a.ndim)

    rev = lambda w, j=0: pl.BlockSpec((ts, w), lambda i: (n_t - 1 - i, j))
    acc = pl.BlockSpec((1, pw), lambda i: (0, 0))
    return pl.pallas_call(
        body, name="l1_bwd_mix", grid=(n_t,),
        in_specs=[rev(D), rev(pw, 1), rev(pw), rev(pw), rev(pw)] + [full(a) for a in (gate, wg, scale, wo)],
        out_specs=[rev(2 * pw), pl.BlockSpec((N_CHIP, gd, D), lambda i: (0, 0, 0)),
                   pl.BlockSpec((4, gd, gd), lambda i: (0, 0, 0)), acc, acc],
        out_shape=[jax.ShapeDtypeStruct((s_len, 2 * pw), BF16), jax.ShapeDtypeStruct((N_CHIP, gd, D), F32),
                   jax.ShapeDtypeStruct((4, gd, gd), F32),
                   jax.ShapeDtypeStruct((1, pw), F32), jax.ShapeDtypeStruct((1, pw), F32)],
        scratch_shapes=[pltpu.VMEM((hl, pw), F32)],
        compiler_params=_cp(("arbitrary",)),
    )(dx2, proj, mixed, y, dpool, gate, wg, scale, wo)


def _l0_bwd_mix(dx1, proj, hst, y, gate, cw, cb, wa, ba, wx, bx, lam, sw, wo):
    s_len = dx1.shape[0]
    ts = _tile(s_len, TS_MIX)
    n_t = s_len // ts
    hl, hb = SUBLANES, BF16_ROWS
    yb_w = 2 * D // N_CHIP

    def body(dx_ref, p_ref, ph_ref, h_ref, hh_ref, y_ref, gate_ref, cw_ref, cb_ref, wa_ref, ba_ref, wx_ref, bx_ref,
             lam_ref, sw_ref, wo_ref, dp_ref, mt_ref, dwa_ref, dwx_ref, sm_ref, cg, cdxc, cdcz, ca):
        i = pl.program_id(0)
        ri = n_t - 1 - i

        @pl.when(i == 0)
        def _():
            cg[...] = jnp.zeros_like(cg)
            ca[...] = jnp.zeros_like(ca)
            cdxc[...] = jnp.zeros_like(cdxc)
            cdcz[...] = jnp.zeros_like(cdcz)
            sm_ref[...] = jnp.zeros_like(sm_ref)
            mt_ref[...] = jnp.zeros_like(mt_ref)
            dwa_ref[...] = jnp.zeros_like(dwa_ref)
            dwx_ref[...] = jnp.zeros_like(dwx_ref)

        dxb = dx_ref[...].astype(BF16)

        def wgrad_out(k):
            mt_ref[k] += lax.dot_general(y_ref[:, k * yb_w:(k + 1) * yb_w], dxb, TN, preferred_element_type=F32)

        wgrad_out(0)
        has_prev = (ri > 0).astype(F32)
        xa, ga, gbp, gcp, v, gb = [p_ref[:, k * D:(k + 1) * D].astype(F32) for k in range(6)]
        prev = lambda k: ph_ref[:, k * D:(k + 1) * D].astype(F32)[hb - hl:hb] * has_prev
        rows = _rows(ts, D)
        first = (rows == 0) & (ri == 0)
        xtaps = _conv_taps(jnp.concatenate([prev(0), xa], axis=0), hl, ts, 4)
        xc = cb_ref[...] + sum(cw_ref[k:k + 1, :] * xtaps[k] for k in range(4))
        r, ig = _lru_gates(xc, wa_ref, ba_ref[...], wx_ref, bx_ref[...])
        sp = _softplus_neg(lam_ref[...])
        a, m, inv_m = _lru_decay(r, sp, first)
        z = gcp * v
        ztaps = _conv_taps(jnp.concatenate([prev(3) * prev(4), z], axis=0), hl, ts, 3)
        cz = sum(sw_ref[k:k + 1, :] * ztaps[k] for k in range(3))
        h = h_ref[...].astype(F32)
        hprev = _down(jnp.concatenate([hh_ref[...].astype(F32)[hb - hl:hb] * has_prev, h], axis=0), 1)[hl:hl + ts]
        dy = lax.dot_general((gate_ref[...] * dx_ref[...]).astype(BF16), wo_ref[...], NT, preferred_element_type=F32)
        dya_pre, dyb_pre = dy[:, 0:D], dy[:, D:2 * D]
        s_a, s_b = _sigmoid(ga), _sigmoid(gb)
        dp_ref[:, D:2 * D] = (dya_pre * h * (s_a * (1.0 + ga * (1.0 - s_a)))).astype(BF16)
        dp_ref[:, 5 * D:6 * D] = (dyb_pre * (gbp * cz) * (s_b * (1.0 + gb * (1.0 - s_b)))).astype(BF16)
        dya = dya_pre * (ga * s_a)
        dyb = dyb_pre * (gb * s_b)
        wgrad_out(1)
        dp_ref[:, 2 * D:3 * D] = (dyb * cz).astype(BF16)
        dcz = dyb * gbp
        for k in range(3):
            sm_ref[8 + k:9 + k, :] += jnp.sum(dcz * ztaps[k], axis=0, keepdims=True)
        dcz_ext = jnp.concatenate([dcz, cdcz[...]], axis=0)
        dz = sum(sw_ref[k:k + 1, :] * _up(dcz_ext, 2 - k)[0:ts] for k in range(3))
        dp_ref[:, 3 * D:4 * D] = (dz * v).astype(BF16)
        dp_ref[:, 4 * D:5 * D] = (dz * gcp).astype(BF16)
        cdcz[...] = dcz[0:hl, :]
        alpha = _up(jnp.concatenate([a, ca[...]], axis=0), 1)[0:ts]
        wgrad_out(2)
        dh = _run(_scan_rev_steps(alpha, dya, cg[0:1, :]))
        wgrad_out(3)
        cg[...] = dh[0:hl, :]
        ca[...] = a[0:hl, :]
        da = dh * hprev
        dm = dh * ig * xc
        di = dh * m * xc
        dxc = dh * m * ig
        dl = da * a - jnp.where(first, 0.0, dm * (a * a) * inv_m)
        sm_ref[7:8, :] += jnp.sum(dl * r, axis=0, keepdims=True) * (-LRU_C)
        dpa = (dl * sp) * (-LRU_C) * r * (1.0 - r)
        dpx = di * ig * (1.0 - ig)
        sm_ref[5:6, :] += jnp.sum(dpa, axis=0, keepdims=True)
        sm_ref[6:7, :] += jnp.sum(dpx, axis=0, keepdims=True)
        dpa_b, dpx_b, xc_b = dpa.astype(BF16), dpx.astype(BF16), xc.astype(BF16)
        back = []
        for hd in range(LRU_HEADS):
            sl = slice(hd * LRU_HEAD_DIM, (hd + 1) * LRU_HEAD_DIM)
            back.append(lax.dot_general(dpa_b[:, sl], wa_ref[hd], NT, preferred_element_type=F32)
                        + lax.dot_general(dpx_b[:, sl], wx_ref[hd], NT, preferred_element_type=F32))
            dwa_ref[hd] += lax.dot_general(xc_b[:, sl], dpa_b[:, sl], TN, preferred_element_type=F32)
            dwx_ref[hd] += lax.dot_general(xc_b[:, sl], dpx_b[:, sl], TN, preferred_element_type=F32)
        dxc = dxc + jnp.concatenate(back, axis=1)
        sm_ref[4:5, :] += jnp.sum(dxc, axis=0, keepdims=True)
        for k in range(4):
            sm_ref[k:k + 1, :] += jnp.sum(dxc * xtaps[k], axis=0, keepdims=True)
        dxc_ext = jnp.concatenate([dxc, cdxc[...]], axis=0)
        dp_ref[:, 0:D] = sum(cw_ref[k:k + 1, :] * _up(dxc_ext, 3 - k)[0:ts] for k in range(4)).astype(BF16)
        cdxc[...] = dxc[0:hl, :]

    def full(a):
        return pl.BlockSpec(a.shape, lambda i: (0,) * a.ndim)

    rev = lambda w: pl.BlockSpec((ts, w), lambda i: (n_t - 1 - i, 0))
    halo = lambda w: pl.BlockSpec((hb, w), lambda i: (jnp.maximum((n_t - 1 - i) * (ts // hb) - 1, 0), 0))
    return pl.pallas_call(
        body, name="l0_bwd_mix", grid=(n_t,),
        in_specs=[rev(D), rev(6 * D), halo(6 * D), rev(D), halo(D), rev(2 * D)]
        + [full(a) for a in (gate, cw, cb, wa, ba, wx, bx, lam, sw, wo)],
        out_specs=[rev(6 * D), pl.BlockSpec((N_CHIP, yb_w, D), lambda i: (0, 0, 0)),
                   pl.BlockSpec(wa.shape, lambda i: (0, 0, 0)), pl.BlockSpec(wa.shape, lambda i: (0, 0, 0)),
                   pl.BlockSpec((2 * SUBLANES, D), lambda i: (0, 0))],
        out_shape=[jax.ShapeDtypeStruct((s_len, 6 * D), BF16), jax.ShapeDtypeStruct((N_CHIP, yb_w, D), F32),
                   jax.ShapeDtypeStruct(wa.shape, F32), jax.ShapeDtypeStruct(wa.shape, F32),
                   jax.ShapeDtypeStruct((2 * SUBLANES, D), F32)],
        scratch_shapes=[pltpu.VMEM((hl, D), F32)] * 4,
        compiler_params=_cp(("arbitrary",)),
    )(dx1, proj, proj, hst, hst, y, gate, cw, cb, wa, ba, wx, bx, lam, sw, wo)


def _dgrad_norm(dproj, w, x, dres, g, sc, name):
    s_len, nb = x.shape[0], w.shape[2]
    ts = _tile(s_len, TS_DGRAD)

    def body(dp_ref, w_ref, x_ref, dr_ref, g_ref, sc_ref, dx_ref, s1_ref, s2_ref):
        @pl.when(pl.program_id(0) == 0)
        def _():
            s1_ref[...] = jnp.zeros_like(s1_ref)
            s2_ref[...] = jnp.zeros_like(s2_ref)

        dh = sum(lax.dot_general(dp_ref[:, k * nb:(k + 1) * nb], w_ref[k], NT, preferred_element_type=F32)
                 for k in range(N_CHIP))
        xv = x_ref[...]
        r = lax.rsqrt(jnp.mean(xv * xv, axis=-1, keepdims=True) + RMS_EPS)
        n = xv * r
        s1_ref[...] += jnp.sum(dh, axis=0, keepdims=True)
        s2_ref[...] += jnp.sum(dh * n, axis=0, keepdims=True)
        dn = dh * (g_ref[...] * (1.0 + sc_ref[...]))
        dx_ref[...] = dr_ref[...] + r * (dn - n * jnp.mean(dn * n, axis=-1, keepdims=True))

    row = lambda wd: pl.BlockSpec((ts, wd), lambda i: (i, 0))
    vec = pl.BlockSpec((1, D), lambda i: (0, 0))
    return pl.pallas_call(
        body, name=name, grid=(s_len // ts,),
        in_specs=[row(N_CHIP * nb), pl.BlockSpec(w.shape, lambda i: (0, 0, 0)), row(D), row(D), vec, vec],
        out_specs=[row(D), vec, vec],
        out_shape=[jax.ShapeDtypeStruct((s_len, D), F32)] + [jax.ShapeDtypeStruct((1, D), F32)] * 2,
        compiler_params=_cp(("arbitrary",)),
    )(dproj, w, x, dres, g, sc)


def _wgrad(a, b, groups, ka, nb, a_col, b_col, name):
    s_len = a.shape[0]
    ts = _tile(s_len, TS_WGRAD)

    def body(a_ref, b_ref, o_ref):
        @pl.when(pl.program_id(1) == 0)
        def _():
            o_ref[...] = jnp.zeros_like(o_ref)

        o_ref[...] += lax.dot_general(a_ref[...].astype(BF16), b_ref[...].astype(BF16), TN, preferred_element_type=F32)

    return pl.pallas_call(
        body, name=name, grid=(groups, s_len // ts),
        in_specs=[pl.BlockSpec((ts, ka), lambda g, s: (s, a_col(g))), pl.BlockSpec((ts, nb), lambda g, s: (s, b_col(g)))],
        out_specs=pl.BlockSpec((None, ka, nb), lambda g, s: (g, 0, 0)),
        out_shape=jax.ShapeDtypeStruct((groups, ka, nb), F32),
        compiler_params=_cp(("parallel", "arbitrary")),
    )(a, b)


def _wo_final(mt, wo, gate, name):
    rb = mt.shape[1]

    def body(m_ref, w_ref, gate_ref, dw_ref, dg_ref):
        @pl.when(pl.program_id(0) == 0)
        def _():
            dg_ref[...] = jnp.zeros_like(dg_ref)

        mv = m_ref[...]
        dw_ref[...] = mv * gate_ref[...]
        dg_ref[...] += jnp.sum(mv * w_ref[...].astype(F32), axis=0, keepdims=True)

    blk = pl.BlockSpec((None, rb, D), lambda k: (k, 0, 0))
    vec = pl.BlockSpec((1, D), lambda k: (0, 0))
    return pl.pallas_call(
        body, name=name, grid=(N_CHIP,), in_specs=[blk, blk, vec], out_specs=[blk, vec],
        out_shape=[jax.ShapeDtypeStruct(mt.shape, F32), jax.ShapeDtypeStruct((1, D), F32)],
        compiler_params=_cp(("arbitrary",)),
    )(mt, wo, gate)


ROW_NORM_G, ROW_CONV_W, ROW_CONV_B, ROW_B_A, ROW_B_X, ROW_LAMBDA, ROW_SC_W, ROW_POOL_B, ROW_POOL_S, ROW_FINAL_G = (
    0, 2, 6, 7, 8, 9, 10, 13, 15, 17)


def _small_pack(s1_0, s2_0, s1_1, s2_1, sm0, dsc1, dbg1, dgf, losscols, dgate0, dgate1, norm_g, sc0, sc1, lam):
    def body(s1_0r, s2_0r, s1_1r, s2_1r, sm, dsc, dbg, dgfr, lcols, dg0, dg1, ng, sc0r, sc1r, lamr, buf, dmod, loss):
        buf[...] = jnp.zeros_like(buf)
        buf[0:1, :] = s2_0r[...] * (1.0 + sc0r[...])
        buf[1:2, :] = s2_1r[...] * (1.0 + sc1r[...])
        buf[ROW_CONV_W:ROW_CONV_W + 4, :] = sm[0:4, :]
        buf[ROW_CONV_B:ROW_CONV_B + 1, :] = sm[4:5, :]
        buf[ROW_B_A:ROW_B_A + 1, :] = sm[5:6, :]
        buf[ROW_B_X:ROW_B_X + 1, :] = sm[6:7, :]
        buf[ROW_LAMBDA:ROW_LAMBDA + 1, :] = -sm[7:8, :] * _sigmoid(-lamr[...])
        buf[ROW_SC_W:ROW_SC_W + 3, :] = sm[8:11, :]
        for k in range(2):
            buf[ROW_POOL_B + k:ROW_POOL_B + k + 1, :] = dbg[:, k * D:(k + 1) * D]
            buf[ROW_POOL_S + k:ROW_POOL_S + k + 1, :] = dsc[:, k * D:(k + 1) * D]
        buf[ROW_FINAL_G:ROW_FINAL_G + 1, :] = dgfr[...]
        pieces = (s1_0r[...], s2_0r[...] * ng[0:1, :], dg0[...], s1_1r[...], s2_1r[...] * ng[1:2, :], dg1[...])
        for k, pc in enumerate(pieces):
            dmod[:, k * D:(k + 1) * D] = jnp.broadcast_to(pc, (SUBLANES, D))
        loss[...] = jnp.broadcast_to(jnp.sum(lcols[...], axis=1, keepdims=True) * (0.5 / D), loss.shape)

    args = (s1_0, s2_0, s1_1, s2_1, sm0, dsc1, dbg1, dgf, losscols, dgate0, dgate1, norm_g, sc0, sc1, lam)
    return pl.pallas_call(
        body, name="small_pack", in_specs=[VMEM] * len(args), out_specs=[VMEM] * 3,
        out_shape=[jax.ShapeDtypeStruct((SMALL_ROWS, D), F32), jax.ShapeDtypeStruct((SUBLANES, 6 * D), F32),
                   jax.ShapeDtypeStruct((SUBLANES, 128), F32)],
        compiler_params=_cp(),
    )(*args)


def _small_comm(buf_a, buf_b, dmod8):
    ra, rb = buf_a.shape[0] // N_DEV, buf_b.shape[0] // N_DEV
    wb = buf_b.shape[1]

    def body(a_ref, b_ref, dm_ref, oa_ref, ob_ref, odm_ref, ina, inb, dslot, sa, sb, s1, r1, s2, r2):
        x, y, c = _pos()
        me = 4 * x + 2 * y + c
        peers = []
        for r in range(1, N_DEV):
            fx, fy, fc = (r >> 2) & 1, (r >> 1) & 1, r & 1
            px, py, pc = _flip(x, fx), _flip(y, fy), _flip(c, fc)
            peers.append(((px, py, pc), 4 * px + 2 * py + pc))
        seg_a = lambda d: pl.ds(pl.multiple_of(d * ra, SUBLANES), ra)
        seg_b = lambda d: pl.ds(pl.multiple_of(d * rb, SUBLANES), rb)
        first = []
        for r, (peer, pid) in enumerate(peers):
            for k, (src, dst) in enumerate(((a_ref.at[seg_a(pid), :], ina.at[r]), (b_ref.at[seg_b(pid), :], inb.at[r]),
                                            (dm_ref, dslot.at[me]))):
                cp = pltpu.make_async_remote_copy(src_ref=src, dst_ref=dst, send_sem=s1.at[3 * r + k],
                                                  recv_sem=r1.at[3 * r + k], device_id=peer, device_id_type=MESH)
                cp.start()
                first.append(cp)
        dslot[me] = dm_ref[...]
        for cp in first:
            cp.wait()
        acc_a, acc_b = a_ref[seg_a(me), :], b_ref[seg_b(me), :]
        for r in range(N_DEV - 1):
            acc_a = acc_a + ina[r]
            acc_b = acc_b + inb[r]
        sa[...] = acc_a
        sb[...] = acc_b
        oa_ref[seg_a(me), :] = acc_a
        ob_ref[seg_b(me), :] = acc_b
        second = []
        for r, (peer, pid) in enumerate(peers):
            for k, (src, dst) in enumerate(((sa, oa_ref.at[seg_a(me), :]), (sb, ob_ref.at[seg_b(me), :]))):
                cp = pltpu.make_async_remote_copy(src_ref=src, dst_ref=dst, send_sem=s2.at[2 * r + k],
                                                  recv_sem=r2.at[2 * r + k], device_id=peer, device_id_type=MESH)
                cp.start()
                second.append(cp)
        rows = _rows(SUBLANES, dm_ref.shape[1])
        dm_all = jnp.zeros(dm_ref.shape, F32)
        for d in range(N_DEV):
            dm_all = jnp.where(rows == d, dslot[d], dm_all)
        odm_ref[...] = dm_all
        for cp in second:
            cp.wait()

    nrel = N_DEV - 1
    return pl.pallas_call(
        body, name="small_comm", in_specs=[VMEM] * 3, out_specs=[VMEM] * 3,
        out_shape=[jax.ShapeDtypeStruct(buf_a.shape, F32), jax.ShapeDtypeStruct(buf_b.shape, F32),
                   jax.ShapeDtypeStruct(dmod8.shape, F32)],
        scratch_shapes=[pltpu.VMEM((nrel, ra, D), F32), pltpu.VMEM((nrel, rb, wb), F32),
                        pltpu.VMEM((N_DEV,) + dmod8.shape, F32), pltpu.VMEM((ra, D), F32), pltpu.VMEM((rb, wb), F32),
                        pltpu.SemaphoreType.DMA((3 * nrel,)), pltpu.SemaphoreType.DMA((3 * nrel,)),
                        pltpu.SemaphoreType.DMA((2 * nrel,)), pltpu.SemaphoreType.DMA((2 * nrel,))],
        compiler_params=_cp(),
    )(buf_a, buf_b, dmod8)


def _adam(w, g, m, v):
    m2 = ADAM_B1 * m + (1.0 - ADAM_B1) * g
    v2 = ADAM_B2 * v + (1.0 - ADAM_B2) * (g * g)
    m_hat = m2 / (1.0 - ADAM_B1 ** ADAM_STEP)
    v_hat = v2 / (1.0 - ADAM_B2 ** ADAM_STEP)
    return -ADAM_LR * (m_hat / (jnp.sqrt(v_hat) + ADAM_EPS) + ADAM_WD * w), m2, v2


def _small_adam(red_a, red_b, dm_all, params):
    n = len(params)

    def body(*refs):
        ra, rb, dm = refs[:3]
        wmv = refs[3:3 + 3 * n]
        outs = refs[3 + 3 * n:]
        x, y, _ = _pos()
        chip = 2 * x + y

        def shard(row0, nrows, width):
            per_row = D // width
            cands = []
            for k in range(N_CHIP):
                if nrows == 1 or per_row >= N_CHIP:
                    cands.append(ra[row0:row0 + nrows, k * width:(k + 1) * width])
                else:
                    rr, cc = divmod(k * width, D)
                    cands.append(ra[row0 + rr:row0 + rr + 1, cc:cc + width])
            g = cands[0]
            for k in range(1, N_CHIP):
                g = jnp.where(chip == k, cands[k], g)
            return g

        dms = jnp.sum(dm[...], axis=0, keepdims=True)
        hw = LRU_HEADS * LRU_HEAD_DIM
        grads = [
            ra[ROW_NORM_G:ROW_NORM_G + 2, :],
            None,
            shard(ROW_CONV_W, 4, D // N_CHIP),
            ra[ROW_CONV_B:ROW_CONV_B + 1, :],
            rb[0:hw, :],
            ra[ROW_B_A:ROW_B_A + 1, :],
            rb[hw:2 * hw, :],
            ra[ROW_B_X:ROW_B_X + 1, :],
            ra[ROW_LAMBDA:ROW_LAMBDA + 1, :],
            shard(ROW_SC_W, 3, D // N_CHIP),
            shard(ROW_POOL_B, 2, 2 * D // N_CHIP),
            shard(ROW_POOL_S, 2, 2 * D // N_CHIP),
            ra[ROW_FINAL_G:ROW_FINAL_G + 1, :],
        ]
        for p in range(n):
            w_ref, m_ref, v_ref = wmv[3 * p:3 * p + 3]
            g_out, d_out, m_out, v_out = outs[4 * p:4 * p + 4]
            if grads[p] is None:
                for l in range(2):
                    g = dms[:, l * 3 * D:(l + 1) * 3 * D]
                    dl, m2, v2 = _adam(w_ref[l:l + 1, :], g, m_ref[l:l + 1, :], v_ref[l:l + 1, :])
                    g_out[l:l + 1, :] = g
                    d_out[l:l + 1, :] = dl
                    m_out[l:l + 1, :] = m2
                    v_out[l:l + 1, :] = v2
            else:
                g = grads[p]
                dl, m2, v2 = _adam(w_ref[...], g, m_ref[...], v_ref[...])
                g_out[...] = g
                d_out[...] = dl
                m_out[...] = m2
                v_out[...] = v2

    flat = [a for p in params for a in p]
    return pl.pallas_call(
        body, name="small_adam", in_specs=[VMEM] * (3 + len(flat)), out_specs=[VMEM] * (4 * n),
        out_shape=[jax.ShapeDtypeStruct(p[0].shape, F32) for p in params for _ in range(4)],
        compiler_params=_cp(),
    )(red_a, red_b, dm_all, *flat)


def _modw_adam(ca_t, dm_sh, w, m, v):
    nw = w.shape[2]

    def body(c_ref, d_ref, w_ref, m_ref, v_ref, g_out, d_out, m_out, v_out):
        g = jnp.dot(c_ref[...], d_ref[...], precision=lax.Precision.HIGHEST, preferred_element_type=F32)
        dl, m2, v2 = _adam(w_ref[...], g, m_ref[...], v_ref[...])
        g_out[...] = g
        d_out[...] = dl
        m_out[...] = m2
        v_out[...] = v2

    blk = pl.BlockSpec((None, D, nw), lambda l: (l, 0, 0))
    return pl.pallas_call(
        body, name="modw_adam", grid=(2,),
        in_specs=[pl.BlockSpec((D, SUBLANES), lambda l: (0, 0)), pl.BlockSpec((None, SUBLANES, nw), lambda l: (l, 0, 0)),
                  blk, blk, blk],
        out_specs=[blk] * 4, out_shape=[jax.ShapeDtypeStruct(w.shape, F32)] * 4,
        compiler_params=_cp(("arbitrary",)),
    )(ca_t, dm_sh, w, m, v)


def _half_rows(r):
    return r // 2


def _sib_send_halves(gs):
    n = len(gs)

    def body(*refs):
        ins, outs, ssem, rsem = refs[:n], refs[n:2 * n], refs[2 * n], refs[2 * n + 1]
        x, y, c = _pos()
        cps = []
        for a in range(n):
            hr = _half_rows(gs[a].shape[1])
            cp = pltpu.make_async_remote_copy(
                src_ref=ins[a].at[:, pl.ds(pl.multiple_of((1 - c) * hr, SUBLANES), hr), :], dst_ref=outs[a],
                send_sem=ssem.at[a], recv_sem=rsem.at[a], device_id=(x, y, 1 - c), device_id_type=MESH)
            cp.start()
            cps.append(cp)
        for cp in cps:
            cp.wait()

    return pl.pallas_call(
        body, name="grad_sib_halves", in_specs=[ANY] * n, out_specs=[ANY] * n,
        out_shape=[jax.ShapeDtypeStruct((N_CHIP, _half_rows(g.shape[1]), g.shape[2]), F32) for g in gs],
        scratch_shapes=[pltpu.SemaphoreType.DMA((n,)), pltpu.SemaphoreType.DMA((n,))],
        compiler_params=_cp(),
    )(*gs)


def _add_half(g, got, cidx, name):
    _, hr, cc = got.shape
    rb = min(hr, 256)

    def body(c_ref, g_ref, r_ref, o_ref):
        o_ref[...] = (g_ref[...] + r_ref[...]).astype(o_ref.dtype)

    blk = pl.BlockSpec((None, rb, cc), lambda k, j, c_ref: (k, j, 0))
    return pl.pallas_call(
        body, name=name,
        grid_spec=pltpu.PrefetchScalarGridSpec(
            num_scalar_prefetch=1, grid=(N_CHIP, hr // rb),
            in_specs=[pl.BlockSpec((None, rb, cc), lambda k, j, c_ref: (k, c_ref[0] * (hr // rb) + j, 0)), blk],
            out_specs=blk),
        out_shape=jax.ShapeDtypeStruct(got.shape, GRAD_WIRE_DTYPE),
        compiler_params=_cp(("parallel", "parallel")),
    )(cidx, g, got)


def _chip_scatter(ps):
    n = len(ps)

    def body(*refs):
        ins, outs, ssem, rsem = refs[:n], refs[n:2 * n], refs[2 * n], refs[2 * n + 1]
        x, y, c = _pos()
        cps = []
        for a in range(n):
            for q, (fx, fy) in enumerate(((1, 0), (0, 1), (1, 1))):
                px, py = _flip(x, fx), _flip(y, fy)
                cp = pltpu.make_async_remote_copy(
                    src_ref=ins[a].at[2 * px + py], dst_ref=outs[a].at[q],
                    send_sem=ssem.at[3 * a + q], recv_sem=rsem.at[3 * a + q], device_id=(px, py, c), device_id_type=MESH)
                cp.start()
                cps.append(cp)
        for cp in cps:
            cp.wait()

    return pl.pallas_call(
        body, name="grad_chip_scatter", in_specs=[ANY] * n, out_specs=[ANY] * n,
        out_shape=[jax.ShapeDtypeStruct((N_CHIP - 1,) + p.shape[1:], p.dtype) for p in ps],
        scratch_shapes=[pltpu.SemaphoreType.DMA((3 * n,)), pltpu.SemaphoreType.DMA((3 * n,))],
        compiler_params=_cp(),
    )(*ps)


def _add_owner(p, got, chipidx, name):
    _, hr, cc = p.shape
    rb = min(hr, 256)

    def body(k_ref, p_ref, r_ref, o_ref):
        o_ref[...] = ((p_ref[...].astype(F32) + r_ref[0].astype(F32)) + r_ref[1].astype(F32)) + r_ref[2].astype(F32)

    return pl.pallas_call(
        body, name=name,
        grid_spec=pltpu.PrefetchScalarGridSpec(
            num_scalar_prefetch=1, grid=(hr // rb,),
            in_specs=[pl.BlockSpec((None, rb, cc), lambda j, k_ref: (k_ref[0], j, 0)),
                      pl.BlockSpec((N_CHIP - 1, rb, cc), lambda j, k_ref: (0, j, 0))],
            out_specs=pl.BlockSpec((rb, cc), lambda j, k_ref: (j, 0))),
        out_shape=jax.ShapeDtypeStruct((hr, cc), F32),
        compiler_params=_cp(("parallel",)),
    )(chipidx, p, got)


def _sib_exchange(ts_):
    n = len(ts_)

    def body(*refs):
        ins, outs, ssem, rsem = refs[:n], refs[n:2 * n], refs[2 * n], refs[2 * n + 1]
        x, y, c = _pos()
        cps = []
        for a in range(n):
            cp = pltpu.make_async_remote_copy(src_ref=ins[a], dst_ref=outs[a], send_sem=ssem.at[a],
                                              recv_sem=rsem.at[a], device_id=(x, y, 1 - c), device_id_type=MESH)
            cp.start()
            cps.append(cp)
        for cp in cps:
            cp.wait()

    return pl.pallas_call(
        body, name="grad_sib_exchange", in_specs=[ANY] * n, out_specs=[ANY] * n,
        out_shape=[jax.ShapeDtypeStruct(t.shape, F32) for t in ts_],
        scratch_shapes=[pltpu.SemaphoreType.DMA((n,))] * 2,
        compiler_params=_cp(),
    )(*ts_)


def _adam_2d(w, g_own, g_sib, m, v, cidx, name):
    rr, cc = w.shape
    hr = rr // 2
    rb = min(hr, 256)
    nb = hr // rb

    def body(c_ref, w_ref, go_ref, gs_ref, m_ref, v_ref, g_out, d_out, m_out, v_out):
        g = jnp.where(pl.program_id(0) == c_ref[0], go_ref[...], gs_ref[...])
        dl, m2, v2 = _adam(w_ref[...], g, m_ref[...], v_ref[...])
        g_out[...] = g
        d_out[...] = dl
        m_out[...] = m2
        v_out[...] = v2

    blk = pl.BlockSpec((rb, cc), lambda h, j, c_ref: (h * nb + j, 0))
    hblk = pl.BlockSpec((rb, cc), lambda h, j, c_ref: (j, 0))
    return pl.pallas_call(
        body, name=name,
        grid_spec=pltpu.PrefetchScalarGridSpec(
            num_scalar_prefetch=1, grid=(2, nb), in_specs=[blk, hblk, hblk, blk, blk], out_specs=[blk] * 4),
        out_shape=[jax.ShapeDtypeStruct((rr, cc), F32)] * 4, compiler_params=_cp(("parallel", "parallel")),
    )(cidx, w, g_own, g_sib, m, v)


def kernel(x, c, norm_g, mod_w, mod_b, hy_w_in, hy_conv_w, hy_conv_b, lru_w_a, lru_b_a, lru_w_x, lru_b_x, lru_lambda, sc_conv_w, hy_w_out, pool_w_in, pool_w_grp, pool_b_grp, pool_scale, pool_w_out, final_g, loss_target, m_norm_g, m_mod_w, m_mod_b, m_hy_w_in, m_hy_conv_w, m_hy_conv_b, m_lru_w_a, m_lru_b_a, m_lru_w_x, m_lru_b_x, m_lru_lambda, m_sc_conv_w, m_hy_w_out, m_pool_w_in, m_pool_w_grp, m_pool_b_grp, m_pool_scale, m_pool_w_out, m_final_g, v_norm_g, v_mod_w, v_mod_b, v_hy_w_in, v_hy_conv_w, v_hy_conv_b, v_lru_w_a, v_lru_b_a, v_lru_w_x, v_lru_b_x, v_lru_lambda, v_sc_conv_w, v_hy_w_out, v_pool_w_in, v_pool_w_grp, v_pool_b_grp, v_pool_scale, v_pool_w_out, v_final_g):
    ax, ay, ac = _pos()
    me = 4 * ax + 2 * ay + ac
    chip = 2 * ax + ay
    xs = x[0]
    tgt = loss_target[0]
    gd = POOL_GROUP_DIM

    ca_all, mod_all, small_w = _mod_fwd(jnp.broadcast_to(c, (SUBLANES, D)), mod_w, mod_b,
                                        hy_conv_w[0], sc_conv_w[0], pool_b_grp, pool_scale)
    mod_me = lax.dynamic_index_in_dim(mod_all, me, axis=1, keepdims=False)
    sh0, sc0, gt0 = (mod_me[0:1, k * D:(k + 1) * D] for k in range(3))
    sh1, sc1, gt1 = (mod_me[1:2, k * D:(k + 1) * D] for k in range(3))
    cw = small_w[SW_CONV:SW_CONV + 4, 0:D]
    sw = small_w[SW_SC:SW_SC + 3, 0:D]
    pool_b = small_w[SW_POOL_B:SW_POOL_B + 1, :]
    pool_s = small_w[SW_POOL_S:SW_POOL_S + 1, :]
    g0, g1, gf = norm_g[0:1], norm_g[1:2], final_g.reshape(1, D)
    cb, ba, bx, lam = hy_conv_b, lru_b_a, lru_b_x, lru_lambda

    big = [hy_w_in[0], hy_w_out[0], pool_w_in[0], pool_w_grp[0].reshape(4 * 128, gd), pool_w_out[0]]
    cidx = ac.reshape(1).astype(jnp.int32)
    kidx = chip.reshape(1).astype(jnp.int32)
    own = [_wcast_own_block(w, kidx, f"wcast_own_block_{a}") for a, w in enumerate(big)]
    w_in0, w_out0 = _wgather([own[0], own[1]], "wgather_l0")
    w_in1, w_grp, w_out1 = _wgather_sequencer([own[2], own[3], own[4]], "wgather_l1")
    w_grp =w_grp.reshape(N_CHIP, 4, 128, gd).transpose(1, 0, 2, 3).reshape(4, gd, gd)
    wa_b, wx_b = _wcast([lru_w_a[0], lru_w_x[0]])

    x1, hst, y0, h0, proj0 = _l0_fwd(xs, g0, sc0, sh0, w_in0, gt0, cw, cb, wa_b, ba, wx_b, bx, lam, sw,
                                     w_out0.reshape(2 * D, D))
    h1, proj1 = _norm_proj(x1, g1, sc1, sh1, w_in1, "l1_proj")
    dpool, mixed, y1, dx2, losscols, dgf = _l1_mix(proj1, x1, tgt, gt1, w_grp, pool_b, pool_s,
                                                    w_out1.reshape(2 * D, D), gf)

    dproj1, mt1, d_wgrp, dsc1, dbg1 = _l1_bwd_mix(dx2, proj1, mixed, y1, dpool, gt1, w_grp, pool_s,
                                                  w_out1.reshape(2 * D, D))
    d_win1 = _wgrad(h1, dproj1, N_CHIP, D, D, lambda g: 0, lambda g: g, "l1_wgrad_in")
    dx1, s1_1, s2_1 = _dgrad_norm(dproj1, w_in1, x1, dx2, g1, sc1, "l1_bwd_proj")
    d_wout1, dgate1 = _wo_final(mt1, w_out1, gt1, "l1_wo_final")

    dproj0, mt0, d_wa, d_wx, sm0 = _l0_bwd_mix(dx1, proj0, hst, y0, gt0, cw, cb, wa_b, ba, wx_b, bx, lam, sw,
                                               w_out0.reshape(2 * D, D))
    d_win0 = _wgrad(h0, dproj0, N_CHIP, D, 6 * D // N_CHIP, lambda g: 0, lambda g: g, "l0_wgrad_in")
    grad_x, s1_0, s2_0 = _dgrad_norm(dproj0, w_in0, xs, dx1, g0, sc0, "l0_bwd_proj")
    d_wout0, dgate0 = _wo_final(mt0, w_out0, gt0, "l0_wo_final")

    buf_a, dmod8, loss8 = _small_pack(s1_0, s2_0, s1_1, s2_1, sm0, dsc1, dbg1, dgf, losscols, dgate0, dgate1,
                                      norm_g, sc0, sc1, lam)
    hw = LRU_HEADS * LRU_HEAD_DIM
    buf_b = jnp.concatenate([d_wa.reshape(hw, LRU_HEAD_DIM), d_wx.reshape(hw, LRU_HEAD_DIM)], axis=0)
    red_a, red_b, dm_all = _small_comm(buf_a, buf_b, dmod8)
    small = [(norm_g, m_norm_g, v_norm_g), (mod_b, m_mod_b, v_mod_b),
             (hy_conv_w[0], m_hy_conv_w[0], v_hy_conv_w[0]), (hy_conv_b, m_hy_conv_b, v_hy_conv_b),
             tuple(a.reshape(hw, LRU_HEAD_DIM) for a in (lru_w_a, m_lru_w_a, v_lru_w_a)),
             (lru_b_a, m_lru_b_a, v_lru_b_a),
             tuple(a.reshape(hw, LRU_HEAD_DIM) for a in (lru_w_x, m_lru_w_x, v_lru_w_x)),
             (lru_b_x, m_lru_b_x, v_lru_b_x), (lru_lambda, m_lru_lambda, v_lru_lambda),
             (sc_conv_w[0], m_sc_conv_w[0], v_sc_conv_w[0]), (pool_b_grp, m_pool_b_grp, v_pool_b_grp),
             (pool_scale, m_pool_scale, v_pool_scale),
             tuple(a.reshape(1, D) for a in (final_g, m_final_g, v_final_g))]
    small_names = ["norm_g", "mod_b", "hy_conv_w", "hy_conv_b", "lru_w_a", "lru_b_a", "lru_w_x", "lru_b_x",
                   "lru_lambda", "sc_conv_w", "pool_b_grp", "pool_scale", "final_g"]
    small_out = _small_adam(red_a, red_b, dm_all, small)
    res = {}
    shapes = dict(norm_g=norm_g, mod_b=mod_b, hy_conv_w=hy_conv_w, hy_conv_b=hy_conv_b, lru_w_a=lru_w_a, lru_b_a=lru_b_a,
                  lru_w_x=lru_w_x, lru_b_x=lru_b_x, lru_lambda=lru_lambda, sc_conv_w=sc_conv_w, pool_b_grp=pool_b_grp,
                  pool_scale=pool_scale, final_g=final_g)
    for p, nm in enumerate(small_names):
        res[nm] = tuple(o.reshape(shapes[nm].shape) for o in small_out[4 * p:4 * p + 4])

    nw = mod_w.shape[2]
    dm_sh = jnp.stack([lax.dynamic_slice_in_dim(dm_all[:, l * 3 * D:(l + 1) * 3 * D], chip * nw, nw, axis=1)
                       for l in range(2)])
    res["mod_w"] = tuple(_modw_adam(ca_all.T, dm_sh, mod_w, m_mod_w, v_mod_w))

    d_wgrp = d_wgrp.reshape(4, N_CHIP, 128, gd).transpose(1, 0, 2, 3).reshape(N_CHIP, 4 * 128, gd)
    grads = [d_win0, d_wout0, d_win1, d_wgrp, d_wout1]
    got = _sib_send_halves(grads)
    parts = [_add_half(g, r, cidx, f"grad_add_half_{a}") for a, (g, r) in enumerate(zip(grads, got))]
    got2 = _chip_scatter(parts)
    halves = [_add_owner(p, r, kidx, f"grad_add_owner_{a}") for a, (p, r) in enumerate(zip(parts, got2))]
    sib_halves = _sib_exchange(halves)
    big_names = ["hy_w_in", "hy_w_out", "pool_w_in", "pool_w_grp", "pool_w_out"]
    big_wmv = [(hy_w_in, m_hy_w_in, v_hy_w_in), (hy_w_out, m_hy_w_out, v_hy_w_out), (pool_w_in, m_pool_w_in, v_pool_w_in),
               (pool_w_grp, m_pool_w_grp, v_pool_w_grp), (pool_w_out, m_pool_w_out, v_pool_w_out)]
    for a, nm in enumerate(big_names):
        rr, cc = big[a].shape
        w, m, v = (t.reshape(rr, cc) for t in big_wmv[a])
        outs = _adam_2d(w, halves[a], sib_halves[a], m, v, cidx, f"adam_{nm}")
        res[nm] = tuple(o.reshape(big_wmv[a][0].shape) for o in outs)

    loss = lax.psum(loss8[0, 0], ("x", "y", "c"))
    order = ["norm_g", "mod_w", "mod_b", "hy_w_in", "hy_conv_w", "hy_conv_b", "lru_w_a", "lru_b_a", "lru_w_x", "lru_b_x",
             "lru_lambda", "sc_conv_w", "hy_w_out", "pool_w_in", "pool_w_grp", "pool_b_grp", "pool_scale", "pool_w_out",
             "final_g"]
    return (loss, grad_x[None], *[res[nm][0] for nm in order], *[res[nm][1] for nm in order],
            *[res[nm][2] for nm in order], *[res[nm][3] for nm in order])
```

```python
import jax
import jax.numpy as jnp
from jax import lax
from jax.experimental import pallas as pl
from jax.experimental.pallas import tpu as pltpu
from jax.experimental.pallas import tpu_sc as plsc

F32, BF16 = jnp.float32, jnp.bfloat16
D = 1024
RMS_EPS = 1e-6
SQRT_FLOOR = 1e-30
LRU_C = 8.0
LRU_HEADS, LRU_HEAD_DIM = 8, 128
POOL_WINDOWS = (2, 4, 8, 16)
POOL_GROUP_DIM = 512
ADAM_LR, ADAM_B1, ADAM_B2, ADAM_EPS, ADAM_WD, ADAM_STEP = 0.001, 0.9, 0.999, 1e-08, 0.01, 10
MESH = pl.DeviceIdType.MESH
CID_WGATHER, CID_HALVES, CID_SCATTER, CID_EXCHANGE = 1, 2, 3, 4
N_DEV, N_CHIP = 8, 4
SUBLANES = 8
BF16_ROWS = 16
POOL_HALO = 16
TS_PROJ, TS_MIX, TS_WGRAD, TS_DGRAD = 1024, 256, 1024, 256
SMALL_ROWS = 64
GRAD_WIRE_DTYPE = BF16
ANY = pl.BlockSpec(memory_space=pl.ANY)
VMEM = pl.BlockSpec(memory_space=pltpu.VMEM)
NT = (((1,), (1,)), ((), ()))
TN = (((0,), (0,)), ((), ()))


def _cp(sem=None, vmem_mb=56):
    kw = dict(vmem_limit_bytes=vmem_mb * 2 ** 20)
    if sem is not None:
        kw["dimension_semantics"] = sem
    return pltpu.CompilerParams(**kw)


def _tile(n, t):
    return min(n, t)


def _pos():
    return lax.axis_index("x"), lax.axis_index("y"), lax.axis_index("c")


def _flip(v, f):
    return 1 - v if f else v


def _sigmoid(z):
    return 0.5 * jnp.tanh(0.5 * z) + 0.5


def _rows(n, c):
    return lax.broadcasted_iota(jnp.int32, (n, c), 0)


def _down(a, d):
    return a if d == 0 else pltpu.roll(a, d, 0)


def _up(a, d):
    return a if d == 0 else pltpu.roll(a, a.shape[0] - d, 0)


def _scan_fwd_steps(a, u, carry):
    n, c = a.shape
    sub = _rows(SUBLANES, c)
    out = []
    for k in range(n // SUBLANES):
        p = a[k * SUBLANES:(k + 1) * SUBLANES]
        g = u[k * SUBLANES:(k + 1) * SUBLANES]
        for d in (1, 2, 4):
            keep = sub >= d
            g = g + p * jnp.where(keep, pltpu.roll(g, d, 0), 0.0)
            p = p * jnp.where(keep, pltpu.roll(p, d, 0), 1.0)
        h = g + p * carry
        carry = h[SUBLANES - 1:SUBLANES, :]
        out.append(h)
        yield
    return jnp.concatenate(out, axis=0)


def _scan_rev_steps(alpha, b, carry):
    n, c = alpha.shape
    sub = _rows(SUBLANES, c)
    out = []
    for k in reversed(range(n // SUBLANES)):
        p = alpha[k * SUBLANES:(k + 1) * SUBLANES]
        g = b[k * SUBLANES:(k + 1) * SUBLANES]
        for d in (1, 2, 4):
            keep = sub < SUBLANES - d
            g = g + p * jnp.where(keep, pltpu.roll(g, SUBLANES - d, 0), 0.0)
            p = p * jnp.where(keep, pltpu.roll(p, SUBLANES - d, 0), 1.0)
        h = g + p * carry
        carry = h[0:1, :]
        out.append(h)
        yield
    return jnp.concatenate(out[::-1], axis=0)


def _run(steps):
    while True:
        try:
            next(steps)
        except StopIteration as done:
            return done.value


def _paired(progress, pieces):
    n, done = len(pieces), 1
    pieces[0]()
    for frac in progress:
        while done < n and done <= frac * n:
            pieces[done]()
            done += 1
    while done < n:
        pieces[done]()
        done += 1


def _conv_taps(ext, halo, n, width):
    return [_down(ext, width - 1 - k)[halo:halo + n] for k in range(width)]


def _lru_gates(xc, wa_ref, ba, wx_ref, bx):
    xb = xc.astype(BF16)
    pa, px = [], []
    for h in range(LRU_HEADS):
        xh = xb[:, h * LRU_HEAD_DIM:(h + 1) * LRU_HEAD_DIM]
        pa.append(jnp.dot(xh, wa_ref[h], preferred_element_type=F32))
        px.append(jnp.dot(xh, wx_ref[h], preferred_element_type=F32))
    r = _sigmoid(jnp.concatenate(pa, axis=1) + ba)
    ig = _sigmoid(jnp.concatenate(px, axis=1) + bx)
    return r, ig


def _softplus_neg(lam):
    return jnp.maximum(-lam, 0.0) + jnp.log1p(jnp.exp(-jnp.abs(lam)))


def _recip_1_to_2(d):
    r0 = pl.reciprocal(d, approx=True)
    return r0 * (2.0 - d * r0)


def _lru_decay(r, sp, first):
    big_l = (-LRU_C) * r * sp
    a = jnp.exp(big_l)
    th = jnp.tanh(big_l)
    q = (-2.0 * th) * _recip_1_to_2(1.0 - th)
    rs = lax.rsqrt(jnp.maximum(q, SQRT_FLOOR))
    return a, jnp.where(first, 1.0, q * rs), rs


def _pool_inv_counts(t0, n):
    t = (t0 + lax.broadcasted_iota(jnp.int32, (n, 1), 0) + 1).astype(F32)
    return [1.0 / jnp.minimum(t, float(w)) for w in POOL_WINDOWS]


def _window_sums(ext, shift):
    gd = POOL_GROUP_DIM
    out = []
    s = ext
    for k in range(len(POOL_WINDOWS)):
        s = s + shift(s, 2 ** k)
        out.append(s[:, 0:gd])
        if k + 1 < len(POOL_WINDOWS):
            s = s[:, gd:]
    return out


SW_ROWS, SW_COLS = 16, 2 * D
SW_CONV, SW_SC, SW_POOL_B, SW_POOL_S = 0, 4, 8, 9


def _mod_fwd(c8, mod_w, mod_b, conv_w, sc_w, pool_b, pool_s):
    nw = mod_w.shape[2]
    cq, pq = conv_w.shape[1], pool_b.shape[1]

    def body(c_ref, w_ref, b_ref, cw_ref, sw_ref, pb_ref, ps_ref, ca_ref, mod_ref, small_ref,
             cslot, mslot, msend, pslot, psend, s1, r1, s2, r2, s3, r3):
        x, y, c = _pos()
        me = 4 * x + 2 * y + c
        chip = 2 * x + y
        first = []
        for r in range(1, N_DEV):
            fx, fy, fc = (r >> 2) & 1, (r >> 1) & 1, r & 1
            cp = pltpu.make_async_remote_copy(
                src_ref=c_ref, dst_ref=cslot.at[me], send_sem=s1.at[r - 1], recv_sem=r1.at[r - 1],
                device_id=(_flip(x, fx), _flip(y, fy), _flip(c, fc)), device_id_type=MESH)
            cp.start()
            first.append(cp)
        cslot[me] = c_ref[...]
        for cp in first:
            cp.wait()
        rows = _rows(SUBLANES, D)
        call = jnp.zeros((SUBLANES, D), F32)
        for d in range(N_DEV):
            call = jnp.where(rows == d, cslot[d], call)
        ca = call * _sigmoid(call)
        ca_ref[...] = ca
        for l in range(2):
            msend[l] = jnp.dot(ca, w_ref[l], precision=lax.Precision.HIGHEST, preferred_element_type=F32)
        psend[...] = jnp.zeros_like(psend)
        psend[SW_CONV:SW_CONV + 4, 0:cq] = cw_ref[...]
        psend[SW_SC:SW_SC + 3, 0:cq] = sw_ref[...]
        psend[SW_POOL_B:SW_POOL_B + 1, :] = pb_ref[...]
        psend[SW_POOL_S:SW_POOL_S + 1, :] = ps_ref[...]
        second = []
        for q, (fx, fy) in enumerate(((1, 0), (0, 1), (1, 1))):
            peer = (_flip(x, fx), _flip(y, fy), c)
            for src, dst, ss, rs in ((msend, mslot, s2, r2), (psend, pslot, s3, r3)):
                cp = pltpu.make_async_remote_copy(src_ref=src, dst_ref=dst.at[chip], send_sem=ss.at[q], recv_sem=rs.at[q],
                                                  device_id=peer, device_id_type=MESH)
                cp.start()
                second.append(cp)
        mslot[chip] = msend[...]
        pslot[chip] = psend[...]
        for cp in second:
            cp.wait()
        small_ref[...] = jnp.zeros_like(small_ref)
        for j in range(N_CHIP):
            for l in range(2):
                mod_ref[l, :, j * nw:(j + 1) * nw] = mslot[j, l] + b_ref[l:l + 1, j * nw:(j + 1) * nw]
            small_ref[0:SUBLANES, j * cq:(j + 1) * cq] = pslot[j, 0:SUBLANES, 0:cq]
            small_ref[SUBLANES:SW_ROWS, j * pq:(j + 1) * pq] = pslot[j, SUBLANES:SW_ROWS, :]

    args = (c8, mod_w, mod_b, conv_w, sc_w, pool_b, pool_s)
    dma3 = pltpu.SemaphoreType.DMA((N_CHIP - 1,))
    return pl.pallas_call(
        body, name="mod_fwd",
        in_specs=[VMEM] * len(args), out_specs=[VMEM] * 3,
        out_shape=[jax.ShapeDtypeStruct((SUBLANES, D), F32), jax.ShapeDtypeStruct((2, SUBLANES, N_CHIP * nw), F32),
                   jax.ShapeDtypeStruct((SW_ROWS, SW_COLS), F32)],
        scratch_shapes=[pltpu.VMEM((N_DEV, SUBLANES, D), F32), pltpu.VMEM((N_CHIP, 2, SUBLANES, nw), F32),
                        pltpu.VMEM((2, SUBLANES, nw), F32), pltpu.VMEM((N_CHIP, SW_ROWS, pq), F32),
                        pltpu.VMEM((SW_ROWS, pq), F32),
                        pltpu.SemaphoreType.DMA((N_DEV - 1,)), pltpu.SemaphoreType.DMA((N_DEV - 1,)),
                        dma3, dma3, dma3, dma3],
        compiler_params=_cp(),
    )(*args)


def _wcast(ws):
    def body(*refs):
        n = len(refs) // 2
        for a in range(n):
            refs[n + a][...] = refs[a][...].astype(BF16)

    return pl.pallas_call(
        body, name="wcast", in_specs=[VMEM] * len(ws), out_specs=[VMEM] * len(ws),
        out_shape=[jax.ShapeDtypeStruct(w.shape, BF16) for w in ws], compiler_params=_cp(),
    )(*ws)


def _wcast_own_block(w, kidx, name, after=None):
    rr, cc = w.shape
    rb = min(rr, 256)

    def body(k_ref, w_ref, *rest):
        rest[-1][...] = w_ref[...].astype(BF16)

    order = [] if after is None else [after]
    return pl.pallas_call(
        body, name=name,
        grid_spec=pltpu.PrefetchScalarGridSpec(
            num_scalar_prefetch=1, grid=(rr // rb,),
            in_specs=[pl.BlockSpec((rb, cc), lambda j, k_ref: (j, 0))] + [ANY] * len(order),
            out_specs=pl.BlockSpec((None, rb, cc), lambda j, k_ref: (k_ref[0], j, 0))),
        out_shape=jax.ShapeDtypeStruct((N_CHIP, rr, cc), BF16),
        compiler_params=_cp(("parallel",)),
    )(kidx, w, *order)


def _wgather_copies(outs, rows, ssem, rsem, fssem, frsem):
    n = len(outs)
    x, y, c = _pos()
    chip = 2 * x + y
    sib = (x, y, 1 - c)
    flips = ((1, 0), (0, 1), (1, 1))

    def half(a, which):
        hr = rows[a] // 2
        return pl.ds(pl.multiple_of(which * hr, BF16_ROWS), hr)

    sends = []
    for a in range(n):
        mine = outs[a].at[chip, half(a, c), :]
        for q, (fx, fy) in enumerate(flips):
            cp = pltpu.make_async_remote_copy(
                src_ref=mine, dst_ref=mine, send_sem=ssem.at[3 * a + q], recv_sem=rsem.at[3 * a + q],
                device_id=(_flip(x, fx), _flip(y, fy), c), device_id_type=MESH)
            cp.start()
            sends.append(cp)
    passed = []
    for a in range(n):
        for q, (fx, fy) in enumerate(flips):
            src_chip = 2 * _flip(x, fx) + _flip(y, fy)
            landed = outs[a].at[src_chip, half(a, c), :]
            pltpu.make_async_remote_copy(
                src_ref=landed, dst_ref=landed, send_sem=ssem.at[3 * a + q], recv_sem=rsem.at[3 * a + q],
                device_id=sib, device_id_type=MESH).wait_recv()
            cp = pltpu.make_async_remote_copy(
                src_ref=landed, dst_ref=landed, send_sem=fssem.at[3 * a + q], recv_sem=frsem.at[3 * a + q],
                device_id=sib, device_id_type=MESH)
            cp.start()
            passed.append(cp)
    for a in range(n):
        for q, (fx, fy) in enumerate(flips):
            src_chip = 2 * _flip(x, fx) + _flip(y, fy)
            other = outs[a].at[src_chip, half(a, 1 - c), :]
            pltpu.make_async_remote_copy(
                src_ref=other, dst_ref=other, send_sem=fssem.at[3 * a + q], recv_sem=frsem.at[3 * a + q],
                device_id=sib, device_id_type=MESH).wait_recv()
    for cp in sends + passed:
        cp.wait_send()


def _wgather(bufs, name):
    n = len(bufs)

    def body(*refs):
        _wgather_copies(refs[n:2 * n], [b.shape[1] for b in bufs], *refs[2 * n:])

    return pl.pallas_call(
        body, name=name, in_specs=[ANY] * n, out_specs=[ANY] * n,
        out_shape=[jax.ShapeDtypeStruct(b.shape, BF16) for b in bufs],
        input_output_aliases={a: a for a in range(n)},
        scratch_shapes=[pltpu.SemaphoreType.DMA((3 * n,))] * 4,
        compiler_params=_cp(),
    )(*bufs)


def _wgather_sequencer(bufs, name):
    n = len(bufs)
    refs = [jax.new_ref(b, memory_space=pltpu.MemorySpace.HBM) for b in bufs]
    dma = pltpu.SemaphoreType.DMA((3 * n,))

    @pl.kernel(mesh=plsc.ScalarSubcoreMesh(axis_name="sequencer", num_cores=1), name=name,
               scratch_types=(dma, dma, dma, dma), compiler_params=pltpu.CompilerParams(collective_id=CID_WGATHER))
    def launch(ssem, rsem, fssem, frsem):
        x, y, c = _pos()
        barrier = pltpu.get_barrier_semaphore()
        for peer in ((1 - x, y, c), (x, 1 - y, c), (1 - x, 1 - y, c), (x, y, 1 - c)):
            pl.semaphore_signal(barrier, inc=1, device_id=peer, device_id_type=MESH)
        pl.semaphore_wait(barrier, 4)
        _wgather_copies(refs, [b.shape[1] for b in bufs], ssem, rsem, fssem, frsem)

    launch()
    return [r[...] for r in refs]


def _norm_proj(x, g, sc, sh, w, name):
    s_len, nb = x.shape[0], w.shape[2]
    ts = _tile(s_len, TS_PROJ)

    def body(x_ref, g_ref, sc_ref, sh_ref, w_ref, h_ref, p_ref):
        @pl.when(pl.program_id(1) == 0)
        def _():
            xv = x_ref[...]
            r = lax.rsqrt(jnp.mean(xv * xv, axis=-1, keepdims=True) + RMS_EPS)
            h_ref[...] = (xv * r * (g_ref[...] * (1.0 + sc_ref[...])) + sh_ref[...]).astype(BF16)

        p_ref[...] = jnp.dot(h_ref[...], w_ref[...], preferred_element_type=F32).astype(BF16)

    vec = pl.BlockSpec((1, D), lambda i, j: (0, 0))
    return pl.pallas_call(
        body, name=name, grid=(s_len // ts, N_CHIP),
        in_specs=[pl.BlockSpec((ts, D), lambda i, j: (i, 0)), vec, vec, vec,
                  pl.BlockSpec((None, D, nb), lambda i, j: (j, 0, 0))],
        out_specs=[pl.BlockSpec((ts, D), lambda i, j: (i, 0)), pl.BlockSpec((ts, nb), lambda i, j: (i, j))],
        out_shape=[jax.ShapeDtypeStruct((s_len, D), BF16), jax.ShapeDtypeStruct((s_len, N_CHIP * nb), BF16)],
        compiler_params=_cp(("parallel", "arbitrary")),
    )(x, g, sc, sh, w)


def _l0_fwd(x, g, sc, sh, w_in, gate, cw, cb, wa, ba, wx, bx, lam, sw, wo):
    s_len, nb = x.shape[0], w_in.shape[2]
    ts = _tile(s_len, TS_MIX)
    n_t = s_len // ts
    hl = SUBLANES

    def body(xa_ref, xb_ref, g_ref, sc_ref, sh_ref, win_ref, gate_ref, cw_ref, cb_ref, wa_ref, ba_ref, wx_ref, bx_ref,
             lam_ref, sw_ref, wo_ref, x1_ref, h_ref, y_ref, h0_ref, p_ref, pcur, pnext, cxa, czz, chh):
        i = pl.program_id(0)

        @pl.when(i == 0)
        def _():
            cxa[...] = jnp.zeros_like(cxa)
            czz[...] = jnp.zeros_like(czz)
            chh[...] = jnp.zeros_like(chh)
            pnext[...] = jnp.zeros_like(pnext)

        pcur[...] = pnext[...]
        xv = xa_ref[...]
        rinv = lax.rsqrt(jnp.mean(xv * xv, axis=-1, keepdims=True) + RMS_EPS)
        h0 = (xv * rinv * (g_ref[...] * (1.0 + sc_ref[...])) + sh_ref[...]).astype(BF16)
        h0_ref[...] = h0

        def project(k):
            def emit():
                pk = jnp.dot(h0, win_ref[k], preferred_element_type=F32).astype(BF16)
                p_ref[:, k * nb:(k + 1) * nb] = pk
                pnext[:, k * nb:(k + 1) * nb] = pk
            return emit

        def mixer():
            piece = lambda k: pcur[:, k * D:(k + 1) * D].astype(F32)
            xa = piece(0)
            rows = _rows(ts, D)
            taps = _conv_taps(jnp.concatenate([cxa[...], xa], axis=0), hl, ts, 4)
            xc = cb_ref[...] + sum(cw_ref[k:k + 1, :] * taps[k] for k in range(4))
            r, ig = _lru_gates(xc, wa_ref, ba_ref[...], wx_ref, bx_ref[...])
            a, m, _ = _lru_decay(r, _softplus_neg(lam_ref[...]), (rows == 0) & (i == 1))
            yield 0.26
            h = _run(_scan_fwd_steps(a, m * ig * xc, chh[hl - 1:hl, :]))
            yield 0.51
            gcp, v = piece(3), piece(4)
            z = gcp * v
            ztaps = _conv_taps(jnp.concatenate([czz[...], z], axis=0), hl, ts, 3)
            yb = piece(2) * sum(sw_ref[k:k + 1, :] * ztaps[k] for k in range(3))
            ga, gb = piece(1), piece(5)
            y = jnp.concatenate([h * (ga * _sigmoid(ga)), yb * (gb * _sigmoid(gb))], axis=1).astype(BF16)
            yield 0.76
            y_ref[...] = y
            x1_ref[...] = xb_ref[...] + gate_ref[...] * jnp.dot(y, wo_ref[...], preferred_element_type=F32)
            h_ref[...] = h.astype(BF16)
            cxa[...] = xa[ts - hl:, :]
            czz[...] = z[ts - hl:, :]
            chh[...] = jnp.where(i > 0, h[ts - hl:, :], 0.0)

        _paired(mixer(), [project(k) for k in range(N_CHIP)])

    def full(a):
        return pl.BlockSpec(a.shape, lambda i: (0,) * a.ndim)

    ahead = lambda w: pl.BlockSpec((ts, w), lambda i: (jnp.minimum(i, n_t - 1), 0))
    behind = lambda w: pl.BlockSpec((ts, w), lambda i: (jnp.maximum(i - 1, 0), 0))
    args = (x, x, g, sc, sh, w_in, gate, cw, cb, wa, ba, wx, bx, lam, sw, wo)
    return pl.pallas_call(
        body, name="l0_fwd", grid=(n_t + 1,),
        in_specs=[ahead(D), behind(D)] + [full(a) for a in args[2:]],
        out_specs=[behind(D), behind(D), behind(2 * D), ahead(D), ahead(N_CHIP * nb)],
        out_shape=[jax.ShapeDtypeStruct((s_len, D), F32), jax.ShapeDtypeStruct((s_len, D), BF16),
                   jax.ShapeDtypeStruct((s_len, 2 * D), BF16), jax.ShapeDtypeStruct((s_len, D), BF16),
                   jax.ShapeDtypeStruct((s_len, N_CHIP * nb), BF16)],
        scratch_shapes=[pltpu.VMEM((ts, N_CHIP * nb), BF16)] * 2 + [pltpu.VMEM((hl, D), F32)] * 3,
        compiler_params=_cp(("arbitrary",)),
    )(*args)


def _l1_mix(proj, x1, tgt, gate, wg, bg, scale, wo, gf):
    s_len = x1.shape[0]
    ts = _tile(s_len, TS_MIX)
    pw, gd, hl = 2 * D, POOL_GROUP_DIM, POOL_HALO

    def body(p_ref, x_ref, t_ref, gate_ref, wg_ref, bg_ref, sc_ref, wo_ref, gf_ref,
             d_ref, mx_ref, y_ref, dx_ref, loss_ref, dgf_ref, cv):
        i = pl.program_id(0)

        @pl.when(i == 0)
        def _():
            cv[...] = jnp.zeros_like(cv)
            loss_ref[...] = jnp.zeros_like(loss_ref)
            dgf_ref[...] = jnp.zeros_like(dgf_ref)

        v = p_ref[:, 0:pw].astype(F32)
        gg = p_ref[:, pw:2 * pw].astype(F32)
        sums = _window_sums(jnp.concatenate([cv[...], v], axis=0), _down)
        inv = _pool_inv_counts(i * ts, ts)
        dd = [sums[k][hl:hl + ts] * inv[k] - v[:, k * gd:(k + 1) * gd] for k in range(4)]
        mixed = jnp.concatenate(
            [jnp.dot(dd[k].astype(BF16), wg_ref[k], preferred_element_type=F32) for k in range(4)], axis=1) + bg_ref[...]
        d_ref[...] = jnp.concatenate(dd, axis=1).astype(BF16)
        mx_ref[...] = mixed.astype(BF16)
        y = (mixed * sc_ref[...] * (gg * _sigmoid(gg))).astype(BF16)
        y_ref[...] = y
        x2 = x_ref[...] + gate_ref[...] * jnp.dot(y, wo_ref[...], preferred_element_type=F32)
        r2 = lax.rsqrt(jnp.mean(x2 * x2, axis=-1, keepdims=True) + RMS_EPS)
        n2 = x2 * r2
        err = n2 * gf_ref[...] - t_ref[...]
        loss_ref[...] += jnp.sum(err * err, axis=0, keepdims=True)
        dyf = err * (1.0 / D)
        dgf_ref[...] += jnp.sum(dyf * n2, axis=0, keepdims=True)
        dn = dyf * gf_ref[...]
        dx_ref[...] = r2 * (dn - n2 * jnp.mean(dn * n2, axis=-1, keepdims=True))
        cv[...] = v[ts - hl:, :]

    def full(a):
        return pl.BlockSpec(a.shape, lambda i: (0,) * a.ndim)

    row = lambda w: pl.BlockSpec((ts, w), lambda i: (i, 0))
    acc = pl.BlockSpec((1, D), lambda i: (0, 0))
    return pl.pallas_call(
        body, name="l1_mix", grid=(s_len // ts,),
        in_specs=[row(2 * pw), row(D), row(D)] + [full(a) for a in (gate, wg, bg, scale, wo, gf)],
        out_specs=[row(pw), row(pw), row(pw), row(D), acc, acc],
        out_shape=[jax.ShapeDtypeStruct((s_len, pw), BF16)] * 3 + [jax.ShapeDtypeStruct((s_len, D), F32)]
        + [jax.ShapeDtypeStruct((1, D), F32)] * 2,
        scratch_shapes=[pltpu.VMEM((hl, pw), F32)],
        compiler_params=_cp(("arbitrary",)),
    )(proj, x1, tgt, gate, wg, bg, scale, wo, gf)


def _l1_bwd_mix(dx2, proj, mixed, y, dpool, gate, wg, scale, wo):
    s_len = dx2.shape[0]
    ts = _tile(s_len, TS_MIX)
    n_t = s_len // ts
    pw, gd, hl = 2 * D, POOL_GROUP_DIM, POOL_HALO

    def body(dx_ref, gg_ref, mx_ref, y_ref, d_ref, gate_ref, wg_ref, sc_ref, wo_ref,
             dp_ref, mt_ref, dwg_ref, dsc_ref, dbg_ref, cq):
        i = pl.program_id(0)

        @pl.when(i == 0)
        def _():
            cq[...] = jnp.zeros_like(cq)
            dsc_ref[...] = jnp.zeros_like(dsc_ref)
            dbg_ref[...] = jnp.zeros_like(dbg_ref)
            mt_ref[...] = jnp.zeros_like(mt_ref)
            dwg_ref[...] = jnp.zeros_like(dwg_ref)

        dxv = dx_ref[...]
        dxb = dxv.astype(BF16)

        def wgrad_out(k):
            mt_ref[k] += lax.dot_general(y_ref[:, k * gd:(k + 1) * gd], dxb, TN, preferred_element_type=F32)

        dy = lax.dot_general((gate_ref[...] * dxv).astype(BF16), wo_ref[...], NT, preferred_element_type=F32)
        wgrad_out(0)
        gg = gg_ref[...].astype(F32)
        mixed = mx_ref[...].astype(F32)
        s = _sigmoid(gg)
        sg = gg * s
        dmixed = dy * sc_ref[...] * sg
        dsc_ref[...] += jnp.sum(dy * mixed * sg, axis=0, keepdims=True)
        dbg_ref[...] += jnp.sum(dmixed, axis=0, keepdims=True)
        dmb = dmixed.astype(BF16)
        wgrad_out(1)
        dp_ref[:, pw:2 * pw] = (dy * sc_ref[...] * mixed * (s * (1.0 + gg * (1.0 - s)))).astype(BF16)
        inv = _pool_inv_counts((n_t - 1 - i) * ts, ts)
        dd = []
        for k in range(4):
            dmk = dmb[:, k * gd:(k + 1) * gd]
            dd.append(lax.dot_general(dmk, wg_ref[k], NT, preferred_element_type=F32))
            dwg_ref[k] += lax.dot_general(d_ref[:, k * gd:(k + 1) * gd], dmk, TN, preferred_element_type=F32)
        wgrad_out(2)
        q = jnp.concatenate([dd[k] * inv[k] for k in range(4)], axis=1)
        sums = _window_sums(jnp.concatenate([q, cq[...]], axis=0), _up)
        wgrad_out(3)
        dp_ref[:, 0:pw] = jnp.concatenate([sums[k][0:ts] - dd[k] for k in range(4)], axis=1).astype(BF16)
        cq[...] = q[0:hl, :]

    def full(a):
        return pl.BlockSpec(a.shape, lambda i: (0,) * a.ndim)

    rev = lambda w, j=0: pl.BlockSpec((ts, w), lambda i: (n_t - 1 - i, j))
    acc = pl.BlockSpec((1, pw), lambda i: (0, 0))
    return pl.pallas_call(
        body, name="l1_bwd_mix", grid=(n_t,),
        in_specs=[rev(D), rev(pw, 1), rev(pw), rev(pw), rev(pw)] + [full(a) for a in (gate, wg, scale, wo)],
        out_specs=[rev(2 * pw), pl.BlockSpec((N_CHIP, gd, D), lambda i: (0, 0, 0)),
                   pl.BlockSpec((4, gd, gd), lambda i: (0, 0, 0)), acc, acc],
        out_shape=[jax.ShapeDtypeStruct((s_len, 2 * pw), BF16), jax.ShapeDtypeStruct((N_CHIP, gd, D), F32),
                   jax.ShapeDtypeStruct((4, gd, gd), F32),
                   jax.ShapeDtypeStruct((1, pw), F32), jax.ShapeDtypeStruct((1, pw), F32)],
        scratch_shapes=[pltpu.VMEM((hl, pw), F32)],
        compiler_params=_cp(("arbitrary",)),
    )(dx2, proj, mixed, y, dpool, gate, wg, scale, wo)


def _l0_bwd_mix(dx1, proj, hst, y, gate, cw, cb, wa, ba, wx, bx, lam, sw, wo):
    s_len = dx1.shape[0]
    ts = _tile(s_len, TS_MIX)
    n_t = s_len // ts
    hl, hb = SUBLANES, BF16_ROWS
    yb_w = 2 * D // N_CHIP

    def body(dx_ref, p_ref, ph_ref, h_ref, hh_ref, y_ref, gate_ref, cw_ref, cb_ref, wa_ref, ba_ref, wx_ref, bx_ref,
             lam_ref, sw_ref, wo_ref, dp_ref, mt_ref, dwa_ref, dwx_ref, sm_ref, cg, cdxc, cdcz, ca):
        i = pl.program_id(0)
        ri = n_t - 1 - i

        @pl.when(i == 0)
        def _():
            cg[...] = jnp.zeros_like(cg)
            ca[...] = jnp.zeros_like(ca)
            cdxc[...] = jnp.zeros_like(cdxc)
            cdcz[...] = jnp.zeros_like(cdcz)
            sm_ref[...] = jnp.zeros_like(sm_ref)
            mt_ref[...] = jnp.zeros_like(mt_ref)
            dwa_ref[...] = jnp.zeros_like(dwa_ref)
            dwx_ref[...] = jnp.zeros_like(dwx_ref)

        dxb = dx_ref[...].astype(BF16)

        def wgrad_out(k):
            mt_ref[k] += lax.dot_general(y_ref[:, k * yb_w:(k + 1) * yb_w], dxb, TN, preferred_element_type=F32)

        wgrad_out(0)
        has_prev = (ri > 0).astype(F32)
        xa, ga, gbp, gcp, v, gb = [p_ref[:, k * D:(k + 1) * D].astype(F32) for k in range(6)]
        prev = lambda k: ph_ref[:, k * D:(k + 1) * D].astype(F32)[hb - hl:hb] * has_prev
        rows = _rows(ts, D)
        first = (rows == 0) & (ri == 0)
        xtaps = _conv_taps(jnp.concatenate([prev(0), xa], axis=0), hl, ts, 4)
        xc = cb_ref[...] + sum(cw_ref[k:k + 1, :] * xtaps[k] for k in range(4))
        r, ig = _lru_gates(xc, wa_ref, ba_ref[...], wx_ref, bx_ref[...])
        sp = _softplus_neg(lam_ref[...])
        a, m, inv_m = _lru_decay(r, sp, first)
        z = gcp * v
        ztaps = _conv_taps(jnp.concatenate([prev(3) * prev(4), z], axis=0), hl, ts, 3)
        cz = sum(sw_ref[k:k + 1, :] * ztaps[k] for k in range(3))
        h = h_ref[...].astype(F32)
        hprev = _down(jnp.concatenate([hh_ref[...].astype(F32)[hb - hl:hb] * has_prev, h], axis=0), 1)[hl:hl + ts]
        dy = lax.dot_general((gate_ref[...] * dx_ref[...]).astype(BF16), wo_ref[...], NT, preferred_element_type=F32)
        dya_pre, dyb_pre = dy[:, 0:D], dy[:, D:2 * D]
        s_a, s_b = _sigmoid(ga), _sigmoid(gb)
        dp_ref[:, D:2 * D] = (dya_pre * h * (s_a * (1.0 + ga * (1.0 - s_a)))).astype(BF16)
        dp_ref[:, 5 * D:6 * D] = (dyb_pre * (gbp * cz) * (s_b * (1.0 + gb * (1.0 - s_b)))).astype(BF16)
        dya = dya_pre * (ga * s_a)
        dyb = dyb_pre * (gb * s_b)
        wgrad_out(1)
        dp_ref[:, 2 * D:3 * D] = (dyb * cz).astype(BF16)
        dcz = dyb * gbp
        for k in range(3):
            sm_ref[8 + k:9 + k, :] += jnp.sum(dcz * ztaps[k], axis=0, keepdims=True)
        dcz_ext = jnp.concatenate([dcz, cdcz[...]], axis=0)
        dz = sum(sw_ref[k:k + 1, :] * _up(dcz_ext, 2 - k)[0:ts] for k in range(3))
        dp_ref[:, 3 * D:4 * D] = (dz * v).astype(BF16)
        dp_ref[:, 4 * D:5 * D] = (dz * gcp).astype(BF16)
        cdcz[...] = dcz[0:hl, :]
        alpha = _up(jnp.concatenate([a, ca[...]], axis=0), 1)[0:ts]
        wgrad_out(2)
        dh = _run(_scan_rev_steps(alpha, dya, cg[0:1, :]))
        wgrad_out(3)
        cg[...] = dh[0:hl, :]
        ca[...] = a[0:hl, :]
        da = dh * hprev
        dm = dh * ig * xc
        di = dh * m * xc
        dxc = dh * m * ig
        dl = da * a - jnp.where(first, 0.0, dm * (a * a) * inv_m)
        sm_ref[7:8, :] += jnp.sum(dl * r, axis=0, keepdims=True) * (-LRU_C)
        dpa = (dl * sp) * (-LRU_C) * r * (1.0 - r)
        dpx = di * ig * (1.0 - ig)
        sm_ref[5:6, :] += jnp.sum(dpa, axis=0, keepdims=True)
        sm_ref[6:7, :] += jnp.sum(dpx, axis=0, keepdims=True)
        dpa_b, dpx_b, xc_b = dpa.astype(BF16), dpx.astype(BF16), xc.astype(BF16)
        back = []
        for hd in range(LRU_HEADS):
            sl = slice(hd * LRU_HEAD_DIM, (hd + 1) * LRU_HEAD_DIM)
            back.append(lax.dot_general(dpa_b[:, sl], wa_ref[hd], NT, preferred_element_type=F32)
                        + lax.dot_general(dpx_b[:, sl], wx_ref[hd], NT, preferred_element_type=F32))
            dwa_ref[hd] += lax.dot_general(xc_b[:, sl], dpa_b[:, sl], TN, preferred_element_type=F32)
            dwx_ref[hd] += lax.dot_general(xc_b[:, sl], dpx_b[:, sl], TN, preferred_element_type=F32)
        dxc = dxc + jnp.concatenate(back, axis=1)
        sm_ref[4:5, :] += jnp.sum(dxc, axis=0, keepdims=True)
        for k in range(4):
            sm_ref[k:k + 1, :] += jnp.sum(dxc * xtaps[k], axis=0, keepdims=True)
        dxc_ext = jnp.concatenate([dxc, cdxc[...]], axis=0)
        dp_ref[:, 0:D] = sum(cw_ref[k:k + 1, :] * _up(dxc_ext, 3 - k)[0:ts] for k in range(4)).astype(BF16)
        cdxc[...] = dxc[0:hl, :]

    def full(a):
        return pl.BlockSpec(a.shape, lambda i: (0,) * a.ndim)

    rev = lambda w: pl.BlockSpec((ts, w), lambda i: (n_t - 1 - i, 0))
    halo = lambda w: pl.BlockSpec((hb, w), lambda i: (jnp.maximum((n_t - 1 - i) * (ts // hb) - 1, 0), 0))
    return pl.pallas_call(
        body, name="l0_bwd_mix", grid=(n_t,),
        in_specs=[rev(D), rev(6 * D), halo(6 * D), rev(D), halo(D), rev(2 * D)]
        + [full(a) for a in (gate, cw, cb, wa, ba, wx, bx, lam, sw, wo)],
        out_specs=[rev(6 * D), pl.BlockSpec((N_CHIP, yb_w, D), lambda i: (0, 0, 0)),
                   pl.BlockSpec(wa.shape, lambda i: (0, 0, 0)), pl.BlockSpec(wa.shape, lambda i: (0, 0, 0)),
                   pl.BlockSpec((2 * SUBLANES, D), lambda i: (0, 0))],
        out_shape=[jax.ShapeDtypeStruct((s_len, 6 * D), BF16), jax.ShapeDtypeStruct((N_CHIP, yb_w, D), F32),
                   jax.ShapeDtypeStruct(wa.shape, F32), jax.ShapeDtypeStruct(wa.shape, F32),
                   jax.ShapeDtypeStruct((2 * SUBLANES, D), F32)],
        scratch_shapes=[pltpu.VMEM((hl, D), F32)] * 4,
        compiler_params=_cp(("arbitrary",)),
    )(dx1, proj, proj, hst, hst, y, gate, cw, cb, wa, ba, wx, bx, lam, sw, wo)


def _dgrad_norm(dproj, w, x, dres, g, sc, name):
    s_len, nb = x.shape[0], w.shape[2]
    ts = _tile(s_len, TS_DGRAD)

    def body(dp_ref, w_ref, x_ref, dr_ref, g_ref, sc_ref, dx_ref, s1_ref, s2_ref):
        @pl.when(pl.program_id(0) == 0)
        def _():
            s1_ref[...] = jnp.zeros_like(s1_ref)
            s2_ref[...] = jnp.zeros_like(s2_ref)

        dh = sum(lax.dot_general(dp_ref[:, k * nb:(k + 1) * nb], w_ref[k], NT, preferred_element_type=F32)
                 for k in range(N_CHIP))
        xv = x_ref[...]
        r = lax.rsqrt(jnp.mean(xv * xv, axis=-1, keepdims=True) + RMS_EPS)
        n = xv * r
        s1_ref[...] += jnp.sum(dh, axis=0, keepdims=True)
        s2_ref[...] += jnp.sum(dh * n, axis=0, keepdims=True)
        dn = dh * (g_ref[...] * (1.0 + sc_ref[...]))
        dx_ref[...] = dr_ref[...] + r * (dn - n * jnp.mean(dn * n, axis=-1, keepdims=True))

    row = lambda wd: pl.BlockSpec((ts, wd), lambda i: (i, 0))
    vec = pl.BlockSpec((1, D), lambda i: (0, 0))
    return pl.pallas_call(
        body, name=name, grid=(s_len // ts,),
        in_specs=[row(N_CHIP * nb), pl.BlockSpec(w.shape, lambda i: (0, 0, 0)), row(D), row(D), vec, vec],
        out_specs=[row(D), vec, vec],
        out_shape=[jax.ShapeDtypeStruct((s_len, D), F32)] + [jax.ShapeDtypeStruct((1, D), F32)] * 2,
        compiler_params=_cp(("arbitrary",)),
    )(dproj, w, x, dres, g, sc)


def _wgrad(a, b, groups, ka, nb, a_col, b_col, name):
    s_len = a.shape[0]
    ts = _tile(s_len, TS_WGRAD)

    def body(a_ref, b_ref, o_ref):
        @pl.when(pl.program_id(1) == 0)
        def _():
            o_ref[...] = jnp.zeros_like(o_ref)

        o_ref[...] += lax.dot_general(a_ref[...].astype(BF16), b_ref[...].astype(BF16), TN, preferred_element_type=F32)

    return pl.pallas_call(
        body, name=name, grid=(groups, s_len // ts),
        in_specs=[pl.BlockSpec((ts, ka), lambda g, s: (s, a_col(g))), pl.BlockSpec((ts, nb), lambda g, s: (s, b_col(g)))],
        out_specs=pl.BlockSpec((None, ka, nb), lambda g, s: (g, 0, 0)),
        out_shape=jax.ShapeDtypeStruct((groups, ka, nb), F32),
        compiler_params=_cp(("parallel", "arbitrary")),
    )(a, b)


def _wo_final(mt, wo, gate, name):
    rb = mt.shape[1]

    def body(m_ref, w_ref, gate_ref, dw_ref, dg_ref):
        @pl.when(pl.program_id(0) == 0)
        def _():
            dg_ref[...] = jnp.zeros_like(dg_ref)

        mv = m_ref[...]
        dw_ref[...] = mv * gate_ref[...]
        dg_ref[...] += jnp.sum(mv * w_ref[...].astype(F32), axis=0, keepdims=True)

    blk = pl.BlockSpec((None, rb, D), lambda k: (k, 0, 0))
    vec = pl.BlockSpec((1, D), lambda k: (0, 0))
    return pl.pallas_call(
        body, name=name, grid=(N_CHIP,), in_specs=[blk, blk, vec], out_specs=[blk, vec],
        out_shape=[jax.ShapeDtypeStruct(mt.shape, F32), jax.ShapeDtypeStruct((1, D), F32)],
        compiler_params=_cp(("arbitrary",)),
    )(mt, wo, gate)


ROW_NORM_G, ROW_CONV_W, ROW_CONV_B, ROW_B_A, ROW_B_X, ROW_LAMBDA, ROW_SC_W, ROW_POOL_B, ROW_POOL_S, ROW_FINAL_G = (
    0, 2, 6, 7, 8, 9, 10, 13, 15, 17)


def _small_pack(s1_0, s2_0, s1_1, s2_1, sm0, dsc1, dbg1, dgf, losscols, dgate0, dgate1, norm_g, sc0, sc1, lam):
    def body(s1_0r, s2_0r, s1_1r, s2_1r, sm, dsc, dbg, dgfr, lcols, dg0, dg1, ng, sc0r, sc1r, lamr, buf, dmod, loss):
        buf[...] = jnp.zeros_like(buf)
        buf[0:1, :] = s2_0r[...] * (1.0 + sc0r[...])
        buf[1:2, :] = s2_1r[...] * (1.0 + sc1r[...])
        buf[ROW_CONV_W:ROW_CONV_W + 4, :] = sm[0:4, :]
        buf[ROW_CONV_B:ROW_CONV_B + 1, :] = sm[4:5, :]
        buf[ROW_B_A:ROW_B_A + 1, :] = sm[5:6, :]
        buf[ROW_B_X:ROW_B_X + 1, :] = sm[6:7, :]
        buf[ROW_LAMBDA:ROW_LAMBDA + 1, :] = -sm[7:8, :] * _sigmoid(-lamr[...])
        buf[ROW_SC_W:ROW_SC_W + 3, :] = sm[8:11, :]
        for k in range(2):
            buf[ROW_POOL_B + k:ROW_POOL_B + k + 1, :] = dbg[:, k * D:(k + 1) * D]
            buf[ROW_POOL_S + k:ROW_POOL_S + k + 1, :] = dsc[:, k * D:(k + 1) * D]
        buf[ROW_FINAL_G:ROW_FINAL_G + 1, :] = dgfr[...]
        pieces = (s1_0r[...], s2_0r[...] * ng[0:1, :], dg0[...], s1_1r[...], s2_1r[...] * ng[1:2, :], dg1[...])
        for k, pc in enumerate(pieces):
            dmod[:, k * D:(k + 1) * D] = jnp.broadcast_to(pc, (SUBLANES, D))
        loss[...] = jnp.broadcast_to(jnp.sum(lcols[...], axis=1, keepdims=True) * (0.5 / D), loss.shape)

    args = (s1_0, s2_0, s1_1, s2_1, sm0, dsc1, dbg1, dgf, losscols, dgate0, dgate1, norm_g, sc0, sc1, lam)
    return pl.pallas_call(
        body, name="small_pack", in_specs=[VMEM] * len(args), out_specs=[VMEM] * 3,
        out_shape=[jax.ShapeDtypeStruct((SMALL_ROWS, D), F32), jax.ShapeDtypeStruct((SUBLANES, 6 * D), F32),
                   jax.ShapeDtypeStruct((SUBLANES, 128), F32)],
        compiler_params=_cp(),
    )(*args)


def _small_comm(buf_a, buf_b, dmod8):
    ra, rb = buf_a.shape[0] // N_DEV, buf_b.shape[0] // N_DEV
    wb = buf_b.shape[1]

    def body(a_ref, b_ref, dm_ref, oa_ref, ob_ref, odm_ref, ina, inb, dslot, sa, sb, s1, r1, s2, r2):
        x, y, c = _pos()
        me = 4 * x + 2 * y + c
        peers = []
        for r in range(1, N_DEV):
            fx, fy, fc = (r >> 2) & 1, (r >> 1) & 1, r & 1
            px, py, pc = _flip(x, fx), _flip(y, fy), _flip(c, fc)
            peers.append(((px, py, pc), 4 * px + 2 * py + pc))
        seg_a = lambda d: pl.ds(pl.multiple_of(d * ra, SUBLANES), ra)
        seg_b = lambda d: pl.ds(pl.multiple_of(d * rb, SUBLANES), rb)
        first = []
        for r, (peer, pid) in enumerate(peers):
            for k, (src, dst) in enumerate(((a_ref.at[seg_a(pid), :], ina.at[r]), (b_ref.at[seg_b(pid), :], inb.at[r]),
                                            (dm_ref, dslot.at[me]))):
                cp = pltpu.make_async_remote_copy(src_ref=src, dst_ref=dst, send_sem=s1.at[3 * r + k],
                                                  recv_sem=r1.at[3 * r + k], device_id=peer, device_id_type=MESH)
                cp.start()
                first.append(cp)
        dslot[me] = dm_ref[...]
        for cp in first:
            cp.wait()
        acc_a, acc_b = a_ref[seg_a(me), :], b_ref[seg_b(me), :]
        for r in range(N_DEV - 1):
            acc_a = acc_a + ina[r]
            acc_b = acc_b + inb[r]
        sa[...] = acc_a
        sb[...] = acc_b
        oa_ref[seg_a(me), :] = acc_a
        ob_ref[seg_b(me), :] = acc_b
        second = []
        for r, (peer, pid) in enumerate(peers):
            for k, (src, dst) in enumerate(((sa, oa_ref.at[seg_a(me), :]), (sb, ob_ref.at[seg_b(me), :]))):
                cp = pltpu.make_async_remote_copy(src_ref=src, dst_ref=dst, send_sem=s2.at[2 * r + k],
                                                  recv_sem=r2.at[2 * r + k], device_id=peer, device_id_type=MESH)
                cp.start()
                second.append(cp)
        rows = _rows(SUBLANES, dm_ref.shape[1])
        dm_all = jnp.zeros(dm_ref.shape, F32)
        for d in range(N_DEV):
            dm_all = jnp.where(rows == d, dslot[d], dm_all)
        odm_ref[...] = dm_all
        for cp in second:
            cp.wait()

    nrel = N_DEV - 1
    return pl.pallas_call(
        body, name="small_comm", in_specs=[VMEM] * 3, out_specs=[VMEM] * 3,
        out_shape=[jax.ShapeDtypeStruct(buf_a.shape, F32), jax.ShapeDtypeStruct(buf_b.shape, F32),
                   jax.ShapeDtypeStruct(dmod8.shape, F32)],
        scratch_shapes=[pltpu.VMEM((nrel, ra, D), F32), pltpu.VMEM((nrel, rb, wb), F32),
                        pltpu.VMEM((N_DEV,) + dmod8.shape, F32), pltpu.VMEM((ra, D), F32), pltpu.VMEM((rb, wb), F32),
                        pltpu.SemaphoreType.DMA((3 * nrel,)), pltpu.SemaphoreType.DMA((3 * nrel,)),
                        pltpu.SemaphoreType.DMA((2 * nrel,)), pltpu.SemaphoreType.DMA((2 * nrel,))],
        compiler_params=_cp(),
    )(buf_a, buf_b, dmod8)


def _adam(w, g, m, v):
    m2 = ADAM_B1 * m + (1.0 - ADAM_B1) * g
    v2 = ADAM_B2 * v + (1.0 - ADAM_B2) * (g * g)
    m_hat = m2 / (1.0 - ADAM_B1 ** ADAM_STEP)
    v_hat = v2 / (1.0 - ADAM_B2 ** ADAM_STEP)
    return -ADAM_LR * (m_hat / (jnp.sqrt(v_hat) + ADAM_EPS) + ADAM_WD * w), m2, v2


def _small_adam(red_a, red_b, dm_all, params):
    n = len(params)

    def body(*refs):
        ra, rb, dm = refs[:3]
        wmv = refs[3:3 + 3 * n]
        outs = refs[3 + 3 * n:]
        x, y, _ = _pos()
        chip = 2 * x + y

        def shard(row0, nrows, width):
            per_row = D // width
            cands = []
            for k in range(N_CHIP):
                if nrows == 1 or per_row >= N_CHIP:
                    cands.append(ra[row0:row0 + nrows, k * width:(k + 1) * width])
                else:
                    rr, cc = divmod(k * width, D)
                    cands.append(ra[row0 + rr:row0 + rr + 1, cc:cc + width])
            g = cands[0]
            for k in range(1, N_CHIP):
                g = jnp.where(chip == k, cands[k], g)
            return g

        dms = jnp.sum(dm[...], axis=0, keepdims=True)
        hw = LRU_HEADS * LRU_HEAD_DIM
        grads = [
            ra[ROW_NORM_G:ROW_NORM_G + 2, :],
            None,
            shard(ROW_CONV_W, 4, D // N_CHIP),
            ra[ROW_CONV_B:ROW_CONV_B + 1, :],
            rb[0:hw, :],
            ra[ROW_B_A:ROW_B_A + 1, :],
            rb[hw:2 * hw, :],
            ra[ROW_B_X:ROW_B_X + 1, :],
            ra[ROW_LAMBDA:ROW_LAMBDA + 1, :],
            shard(ROW_SC_W, 3, D // N_CHIP),
            shard(ROW_POOL_B, 2, 2 * D // N_CHIP),
            shard(ROW_POOL_S, 2, 2 * D // N_CHIP),
            ra[ROW_FINAL_G:ROW_FINAL_G + 1, :],
        ]
        for p in range(n):
            w_ref, m_ref, v_ref = wmv[3 * p:3 * p + 3]
            g_out, d_out, m_out, v_out = outs[4 * p:4 * p + 4]
            if grads[p] is None:
                for l in range(2):
                    g = dms[:, l * 3 * D:(l + 1) * 3 * D]
                    dl, m2, v2 = _adam(w_ref[l:l + 1, :], g, m_ref[l:l + 1, :], v_ref[l:l + 1, :])
                    g_out[l:l + 1, :] = g
                    d_out[l:l + 1, :] = dl
                    m_out[l:l + 1, :] = m2
                    v_out[l:l + 1, :] = v2
            else:
                g = grads[p]
                dl, m2, v2 = _adam(w_ref[...], g, m_ref[...], v_ref[...])
                g_out[...] = g
                d_out[...] = dl
                m_out[...] = m2
                v_out[...] = v2

    flat = [a for p in params for a in p]
    return pl.pallas_call(
        body, name="small_adam", in_specs=[VMEM] * (3 + len(flat)), out_specs=[VMEM] * (4 * n),
        out_shape=[jax.ShapeDtypeStruct(p[0].shape, F32) for p in params for _ in range(4)],
        compiler_params=_cp(),
    )(red_a, red_b, dm_all, *flat)


def _modw_adam(ca_t, dm_sh, w, m, v):
    nw = w.shape[2]

    def body(c_ref, d_ref, w_ref, m_ref, v_ref, g_out, d_out, m_out, v_out):
        g = jnp.dot(c_ref[...], d_ref[...], precision=lax.Precision.HIGHEST, preferred_element_type=F32)
        dl, m2, v2 = _adam(w_ref[...], g, m_ref[...], v_ref[...])
        g_out[...] = g
        d_out[...] = dl
        m_out[...] = m2
        v_out[...] = v2

    blk = pl.BlockSpec((None, D, nw), lambda l: (l, 0, 0))
    return pl.pallas_call(
        body, name="modw_adam", grid=(2,),
        in_specs=[pl.BlockSpec((D, SUBLANES), lambda l: (0, 0)), pl.BlockSpec((None, SUBLANES, nw), lambda l: (l, 0, 0)),
                  blk, blk, blk],
        out_specs=[blk] * 4, out_shape=[jax.ShapeDtypeStruct(w.shape, F32)] * 4,
        compiler_params=_cp(("arbitrary",)),
    )(ca_t, dm_sh, w, m, v)


def _half_rows(r):
    return r // 2


def _exchange(copies, name, out_type, n_sems, args, sequencer=None):
    n_in, n_out = len(args), len(out_type)

    def body(*refs):
        if sequencer is not None:
            barrier = pltpu.get_barrier_semaphore()
            peers = sequencer[1](*_pos())
            for peer in peers:
                pl.semaphore_signal(barrier, inc=1, device_id=peer, device_id_type=MESH)
            pl.semaphore_wait(barrier, len(peers))
        copies(refs[:n_in], refs[n_in:n_in + n_out], refs[n_in + n_out], refs[n_in + n_out + 1])

    sems = [pltpu.SemaphoreType.DMA((n_sems,))] * 2
    if sequencer is None:
        return pl.pallas_call(body, name=name, in_specs=[ANY] * n_in, out_specs=[ANY] * n_out, out_shape=out_type,
                              scratch_shapes=sems, compiler_params=_cp())(*args)
    return pl.kernel(body, out_type, mesh=plsc.ScalarSubcoreMesh(axis_name="sequencer", num_cores=1), name=name,
                     scratch_types=sems, compiler_params=pltpu.CompilerParams(collective_id=sequencer[0]))(*args)


def _sibling(x, y, c):
    return [(x, y, 1 - c)]


def _other_chips(x, y, c):
    return [(1 - x, y, c), (x, 1 - y, c), (1 - x, 1 - y, c)]


def _sib_send_halves(gs, name, sequencer_id=None):
    n = len(gs)

    def copies(ins, outs, ssem, rsem):
        x, y, c = _pos()
        cps = []
        for a in range(n):
            hr = _half_rows(gs[a].shape[1])
            cp = pltpu.make_async_remote_copy(
                src_ref=ins[a].at[:, pl.ds(pl.multiple_of((1 - c) * hr, SUBLANES), hr), :], dst_ref=outs[a],
                send_sem=ssem.at[a], recv_sem=rsem.at[a], device_id=(x, y, 1 - c), device_id_type=MESH)
            cp.start()
            cps.append(cp)
        for cp in cps:
            cp.wait()

    out_type = [jax.ShapeDtypeStruct((N_CHIP, _half_rows(g.shape[1]), g.shape[2]), F32) for g in gs]
    return _exchange(copies, name, out_type, n, gs, None if sequencer_id is None else (sequencer_id, _sibling))


def _add_half(g, got, cidx, name):
    _, hr, cc = got.shape
    rb = min(hr, 256)

    def body(c_ref, g_ref, r_ref, o_ref):
        o_ref[...] = (g_ref[...] + r_ref[...]).astype(o_ref.dtype)

    blk = pl.BlockSpec((None, rb, cc), lambda k, j, c_ref: (k, j, 0))
    return pl.pallas_call(
        body, name=name,
        grid_spec=pltpu.PrefetchScalarGridSpec(
            num_scalar_prefetch=1, grid=(N_CHIP, hr // rb),
            in_specs=[pl.BlockSpec((None, rb, cc), lambda k, j, c_ref: (k, c_ref[0] * (hr // rb) + j, 0)), blk],
            out_specs=blk),
        out_shape=jax.ShapeDtypeStruct(got.shape, GRAD_WIRE_DTYPE),
        compiler_params=_cp(("parallel", "parallel")),
    )(cidx, g, got)


def _chip_scatter(ps, name, sequencer_id=None):
    n = len(ps)

    def copies(ins, outs, ssem, rsem):
        x, y, c = _pos()
        cps = []
        for a in range(n):
            for q, (fx, fy) in enumerate(((1, 0), (0, 1), (1, 1))):
                px, py = _flip(x, fx), _flip(y, fy)
                cp = pltpu.make_async_remote_copy(
                    src_ref=ins[a].at[2 * px + py], dst_ref=outs[a].at[q],
                    send_sem=ssem.at[3 * a + q], recv_sem=rsem.at[3 * a + q], device_id=(px, py, c), device_id_type=MESH)
                cp.start()
                cps.append(cp)
        for cp in cps:
            cp.wait()

    out_type = [jax.ShapeDtypeStruct((N_CHIP - 1,) + p.shape[1:], p.dtype) for p in ps]
    return _exchange(copies, name, out_type, 3 * n, ps, None if sequencer_id is None else (sequencer_id, _other_chips))


def _add_owner(p, got, chipidx, name):
    _, hr, cc = p.shape
    rb = min(hr, 256)

    def body(k_ref, p_ref, r_ref, o_ref):
        o_ref[...] = ((p_ref[...].astype(F32) + r_ref[0].astype(F32)) + r_ref[1].astype(F32)) + r_ref[2].astype(F32)

    return pl.pallas_call(
        body, name=name,
        grid_spec=pltpu.PrefetchScalarGridSpec(
            num_scalar_prefetch=1, grid=(hr // rb,),
            in_specs=[pl.BlockSpec((None, rb, cc), lambda j, k_ref: (k_ref[0], j, 0)),
                      pl.BlockSpec((N_CHIP - 1, rb, cc), lambda j, k_ref: (0, j, 0))],
            out_specs=pl.BlockSpec((rb, cc), lambda j, k_ref: (j, 0))),
        out_shape=jax.ShapeDtypeStruct((hr, cc), F32),
        compiler_params=_cp(("parallel",)),
    )(chipidx, p, got)


def _sib_exchange(ts_, name, sequencer_id=None):
    n = len(ts_)

    def copies(ins, outs, ssem, rsem):
        x, y, c = _pos()
        cps = []
        for a in range(n):
            cp = pltpu.make_async_remote_copy(src_ref=ins[a], dst_ref=outs[a], send_sem=ssem.at[a],
                                              recv_sem=rsem.at[a], device_id=(x, y, 1 - c), device_id_type=MESH)
            cp.start()
            cps.append(cp)
        for cp in cps:
            cp.wait()

    out_type = [jax.ShapeDtypeStruct(t.shape, F32) for t in ts_]
    return _exchange(copies, name, out_type, n, ts_, None if sequencer_id is None else (sequencer_id, _sibling))


def _adam_2d(w, g_own, g_sib, m, v, cidx, name):
    rr, cc = w.shape
    hr = rr // 2
    rb = min(hr, 256)
    nb = hr // rb

    def body(c_ref, w_ref, go_ref, gs_ref, m_ref, v_ref, g_out, d_out, m_out, v_out):
        g = jnp.where(pl.program_id(0) == c_ref[0], go_ref[...], gs_ref[...])
        dl, m2, v2 = _adam(w_ref[...], g, m_ref[...], v_ref[...])
        g_out[...] = g
        d_out[...] = dl
        m_out[...] = m2
        v_out[...] = v2

    blk = pl.BlockSpec((rb, cc), lambda h, j, c_ref: (h * nb + j, 0))
    hblk = pl.BlockSpec((rb, cc), lambda h, j, c_ref: (j, 0))
    return pl.pallas_call(
        body, name=name,
        grid_spec=pltpu.PrefetchScalarGridSpec(
            num_scalar_prefetch=1, grid=(2, nb), in_specs=[blk, hblk, hblk, blk, blk], out_specs=[blk] * 4),
        out_shape=[jax.ShapeDtypeStruct((rr, cc), F32)] * 4, compiler_params=_cp(("parallel", "parallel")),
    )(cidx, w, g_own, g_sib, m, v)


def kernel(x, c, norm_g, mod_w, mod_b, hy_w_in, hy_conv_w, hy_conv_b, lru_w_a, lru_b_a, lru_w_x, lru_b_x, lru_lambda, sc_conv_w, hy_w_out, pool_w_in, pool_w_grp, pool_b_grp, pool_scale, pool_w_out, final_g, loss_target, m_norm_g, m_mod_w, m_mod_b, m_hy_w_in, m_hy_conv_w, m_hy_conv_b, m_lru_w_a, m_lru_b_a, m_lru_w_x, m_lru_b_x, m_lru_lambda, m_sc_conv_w, m_hy_w_out, m_pool_w_in, m_pool_w_grp, m_pool_b_grp, m_pool_scale, m_pool_w_out, m_final_g, v_norm_g, v_mod_w, v_mod_b, v_hy_w_in, v_hy_conv_w, v_hy_conv_b, v_lru_w_a, v_lru_b_a, v_lru_w_x, v_lru_b_x, v_lru_lambda, v_sc_conv_w, v_hy_w_out, v_pool_w_in, v_pool_w_grp, v_pool_b_grp, v_pool_scale, v_pool_w_out, v_final_g):
    ax, ay, ac = _pos()
    me = 4 * ax + 2 * ay + ac
    chip = 2 * ax + ay
    xs = x[0]
    tgt = loss_target[0]
    gd = POOL_GROUP_DIM

    ca_all, mod_all, small_w = _mod_fwd(jnp.broadcast_to(c, (SUBLANES, D)), mod_w, mod_b,
                                        hy_conv_w[0], sc_conv_w[0], pool_b_grp, pool_scale)
    mod_me = lax.dynamic_index_in_dim(mod_all, me, axis=1, keepdims=False)
    sh0, sc0, gt0 = (mod_me[0:1, k * D:(k + 1) * D] for k in range(3))
    sh1, sc1, gt1 = (mod_me[1:2, k * D:(k + 1) * D] for k in range(3))
    cw = small_w[SW_CONV:SW_CONV + 4, 0:D]
    sw = small_w[SW_SC:SW_SC + 3, 0:D]
    pool_b = small_w[SW_POOL_B:SW_POOL_B + 1, :]
    pool_s = small_w[SW_POOL_S:SW_POOL_S + 1, :]
    g0, g1, gf = norm_g[0:1], norm_g[1:2], final_g.reshape(1, D)
    cb, ba, bx, lam = hy_conv_b, lru_b_a, lru_b_x, lru_lambda

    big = [hy_w_in[0], hy_w_out[0], pool_w_in[0], pool_w_grp[0].reshape(4 * 128, gd), pool_w_out[0]]
    cidx = ac.reshape(1).astype(jnp.int32)
    kidx = chip.reshape(1).astype(jnp.int32)
    w_in0, w_out0 = _wgather([_wcast_own_block(w, kidx, f"wcast_own_block_{a}") for a, w in enumerate(big[:2])],
                             "wgather_l0")
    w_in1, w_grp, w_out1 = _wgather_sequencer(
        [_wcast_own_block(w, kidx, f"wcast_own_block_{a + 2}", after=w_out0) for a, w in enumerate(big[2:])], "wgather_l1")
    w_grp =w_grp.reshape(N_CHIP, 4, 128, gd).transpose(1, 0, 2, 3).reshape(4, gd, gd)
    wa_b, wx_b = _wcast([lru_w_a[0], lru_w_x[0]])

    x1, hst, y0, h0, proj0 = _l0_fwd(xs, g0, sc0, sh0, w_in0, gt0, cw, cb, wa_b, ba, wx_b, bx, lam, sw,
                                     w_out0.reshape(2 * D, D))
    h1, proj1 = _norm_proj(x1, g1, sc1, sh1, w_in1, "l1_proj")
    dpool, mixed, y1, dx2, losscols, dgf = _l1_mix(proj1, x1, tgt, gt1, w_grp, pool_b, pool_s,
                                                    w_out1.reshape(2 * D, D), gf)

    def reduce_large(grads, tag, on_sequencer):
        ids = (CID_HALVES, CID_SCATTER, CID_EXCHANGE) if on_sequencer else (None, None, None)
        got = _sib_send_halves(grads, f"grad_sib_halves_{tag}", ids[0])
        parts = [_add_half(g, r, cidx, f"grad_add_half_{tag}{a}") for a, (g, r) in enumerate(zip(grads, got))]
        got2 = _chip_scatter(parts, f"grad_chip_scatter_{tag}", ids[1])
        own = [_add_owner(p, r, kidx, f"grad_add_owner_{tag}{a}") for a, (p, r) in enumerate(zip(parts, got2))]
        return own, _sib_exchange(own, f"grad_sib_exchange_{tag}", ids[2])

    dproj1, mt1, d_wgrp, dsc1, dbg1 = _l1_bwd_mix(dx2, proj1, mixed, y1, dpool, gt1, w_grp, pool_s,
                                                  w_out1.reshape(2 * D, D))
    d_win1 = _wgrad(h1, dproj1, N_CHIP, D, D, lambda g: 0, lambda g: g, "l1_wgrad_in")
    dx1, s1_1, s2_1 = _dgrad_norm(dproj1, w_in1, x1, dx2, g1, sc1, "l1_bwd_proj")
    d_wout1, dgate1 = _wo_final(mt1, w_out1, gt1, "l1_wo_final")
    d_wgrp = d_wgrp.reshape(4, N_CHIP, 128, gd).transpose(1, 0, 2, 3).reshape(N_CHIP, 4 * 128, gd)
    halves_l1, sib_l1 = reduce_large([d_win1, d_wgrp, d_wout1], "l1", True)

    dproj0, mt0, d_wa, d_wx, sm0 = _l0_bwd_mix(dx1, proj0, hst, y0, gt0, cw, cb, wa_b, ba, wx_b, bx, lam, sw,
                                               w_out0.reshape(2 * D, D))
    d_win0 = _wgrad(h0, dproj0, N_CHIP, D, 6 * D // N_CHIP, lambda g: 0, lambda g: g, "l0_wgrad_in")
    grad_x, s1_0, s2_0 = _dgrad_norm(dproj0, w_in0, xs, dx1, g0, sc0, "l0_bwd_proj")
    d_wout0, dgate0 = _wo_final(mt0, w_out0, gt0, "l0_wo_final")
    halves_l0, sib_l0 = reduce_large([d_win0, d_wout0], "l0", False)

    buf_a, dmod8, loss8 = _small_pack(s1_0, s2_0, s1_1, s2_1, sm0, dsc1, dbg1, dgf, losscols, dgate0, dgate1,
                                      norm_g, sc0, sc1, lam)
    hw = LRU_HEADS * LRU_HEAD_DIM
    buf_b = jnp.concatenate([d_wa.reshape(hw, LRU_HEAD_DIM), d_wx.reshape(hw, LRU_HEAD_DIM)], axis=0)
    red_a, red_b, dm_all = _small_comm(buf_a, buf_b, dmod8)
    small = [(norm_g, m_norm_g, v_norm_g), (mod_b, m_mod_b, v_mod_b),
             (hy_conv_w[0], m_hy_conv_w[0], v_hy_conv_w[0]), (hy_conv_b, m_hy_conv_b, v_hy_conv_b),
             tuple(a.reshape(hw, LRU_HEAD_DIM) for a in (lru_w_a, m_lru_w_a, v_lru_w_a)),
             (lru_b_a, m_lru_b_a, v_lru_b_a),
             tuple(a.reshape(hw, LRU_HEAD_DIM) for a in (lru_w_x, m_lru_w_x, v_lru_w_x)),
             (lru_b_x, m_lru_b_x, v_lru_b_x), (lru_lambda, m_lru_lambda, v_lru_lambda),
             (sc_conv_w[0], m_sc_conv_w[0], v_sc_conv_w[0]), (pool_b_grp, m_pool_b_grp, v_pool_b_grp),
             (pool_scale, m_pool_scale, v_pool_scale),
             tuple(a.reshape(1, D) for a in (final_g, m_final_g, v_final_g))]
    small_names = ["norm_g", "mod_b", "hy_conv_w", "hy_conv_b", "lru_w_a", "lru_b_a", "lru_w_x", "lru_b_x",
                   "lru_lambda", "sc_conv_w", "pool_b_grp", "pool_scale", "final_g"]
    small_out = _small_adam(red_a, red_b, dm_all, small)
    res = {}
    shapes = dict(norm_g=norm_g, mod_b=mod_b, hy_conv_w=hy_conv_w, hy_conv_b=hy_conv_b, lru_w_a=lru_w_a, lru_b_a=lru_b_a,
                  lru_w_x=lru_w_x, lru_b_x=lru_b_x, lru_lambda=lru_lambda, sc_conv_w=sc_conv_w, pool_b_grp=pool_b_grp,
                  pool_scale=pool_scale, final_g=final_g)
    for p, nm in enumerate(small_names):
        res[nm] = tuple(o.reshape(shapes[nm].shape) for o in small_out[4 * p:4 * p + 4])

    nw = mod_w.shape[2]
    dm_sh = jnp.stack([lax.dynamic_slice_in_dim(dm_all[:, l * 3 * D:(l + 1) * 3 * D], chip * nw, nw, axis=1)
                       for l in range(2)])
    res["mod_w"] = tuple(_modw_adam(ca_all.T, dm_sh, mod_w, m_mod_w, v_mod_w))

    halves = list(halves_l0) + list(halves_l1)
    sib_halves = list(sib_l0) + list(sib_l1)
    big_names = ["hy_w_in", "hy_w_out", "pool_w_in", "pool_w_grp", "pool_w_out"]
    big_wmv = [(hy_w_in, m_hy_w_in, v_hy_w_in), (hy_w_out, m_hy_w_out, v_hy_w_out), (pool_w_in, m_pool_w_in, v_pool_w_in),
               (pool_w_grp, m_pool_w_grp, v_pool_w_grp), (pool_w_out, m_pool_w_out, v_pool_w_out)]
    for a, nm in enumerate(big_names):
        rr, cc = big[a].shape
        w, m, v = (t.reshape(rr, cc) for t in big_wmv[a])
        outs = _adam_2d(w, halves[a], sib_halves[a], m, v, cidx, f"adam_{nm}")
        res[nm] = tuple(o.reshape(big_wmv[a][0].shape) for o in outs)

    loss = lax.psum(loss8[0, 0], ("x", "y", "c"))
    order = ["norm_g", "mod_w", "mod_b", "hy_w_in", "hy_conv_w", "hy_conv_b", "lru_w_a", "lru_b_a", "lru_w_x", "lru_b_x",
             "lru_lambda", "sc_conv_w", "hy_w_out", "pool_w_in", "pool_w_grp", "pool_b_grp", "pool_scale", "pool_w_out",
             "final_g"]
    return (loss, grad_x[None], *[res[nm][0] for nm in order], *[res[nm][1] for nm in order],
            *[res[nm][2] for nm in order], *[res[nm][3] for nm in order])
```

```python
import jax
import jax.numpy as jnp
from jax import lax
from jax.experimental import pallas as pl
from jax.experimental.pallas import tpu as pltpu
from jax.experimental.pallas import tpu_sc as plsc

F32, BF16 = jnp.float32, jnp.bfloat16
D = 1024
RMS_EPS = 1e-6
SQRT_FLOOR = 1e-30
LRU_C = 8.0
LRU_HEADS, LRU_HEAD_DIM = 8, 128
POOL_WINDOWS = (2, 4, 8, 16)
POOL_GROUP_DIM = 512
ADAM_LR, ADAM_B1, ADAM_B2, ADAM_EPS, ADAM_WD, ADAM_STEP = 0.001, 0.9, 0.999, 1e-08, 0.01, 10
MESH = pl.DeviceIdType.MESH
CID_WGATHER = 1
CIDS_L1 = (2, 3, 4)
CIDS_L0 = (5, 6, 7)
N_DEV, N_CHIP = 8, 4
SUBLANES = 8
BF16_ROWS = 16
POOL_HALO = 16
TS_PROJ, TS_MIX, TS_WGRAD, TS_DGRAD = 1024, 256, 1024, 256
SMALL_ROWS = 64
GRAD_WIRE_DTYPE = BF16
ANY = pl.BlockSpec(memory_space=pl.ANY)
VMEM = pl.BlockSpec(memory_space=pltpu.VMEM)
NT = (((1,), (1,)), ((), ()))
TN = (((0,), (0,)), ((), ()))


def _cp(sem=None, vmem_mb=56):
    kw = dict(vmem_limit_bytes=vmem_mb * 2 ** 20)
    if sem is not None:
        kw["dimension_semantics"] = sem
    return pltpu.CompilerParams(**kw)


def _tile(n, t):
    return min(n, t)


def _pos():
    return lax.axis_index("x"), lax.axis_index("y"), lax.axis_index("c")


def _flip(v, f):
    return 1 - v if f else v


def _sigmoid(z):
    return 0.5 * jnp.tanh(0.5 * z) + 0.5


def _rows(n, c):
    return lax.broadcasted_iota(jnp.int32, (n, c), 0)


def _down(a, d):
    return a if d == 0 else pltpu.roll(a, d, 0)


def _up(a, d):
    return a if d == 0 else pltpu.roll(a, a.shape[0] - d, 0)


def _scan_fwd_steps(a, u, carry):
    n, c = a.shape
    sub = _rows(SUBLANES, c)
    out = []
    for k in range(n // SUBLANES):
        p = a[k * SUBLANES:(k + 1) * SUBLANES]
        g = u[k * SUBLANES:(k + 1) * SUBLANES]
        for d in (1, 2, 4):
            keep = sub >= d
            g = g + p * jnp.where(keep, pltpu.roll(g, d, 0), 0.0)
            p = p * jnp.where(keep, pltpu.roll(p, d, 0), 1.0)
        h = g + p * carry
        carry = h[SUBLANES - 1:SUBLANES, :]
        out.append(h)
        yield
    return jnp.concatenate(out, axis=0)


def _scan_rev_steps(alpha, b, carry):
    n, c = alpha.shape
    sub = _rows(SUBLANES, c)
    out = []
    for k in reversed(range(n // SUBLANES)):
        p = alpha[k * SUBLANES:(k + 1) * SUBLANES]
        g = b[k * SUBLANES:(k + 1) * SUBLANES]
        for d in (1, 2, 4):
            keep = sub < SUBLANES - d
            g = g + p * jnp.where(keep, pltpu.roll(g, SUBLANES - d, 0), 0.0)
            p = p * jnp.where(keep, pltpu.roll(p, SUBLANES - d, 0), 1.0)
        h = g + p * carry
        carry = h[0:1, :]
        out.append(h)
        yield
    return jnp.concatenate(out[::-1], axis=0)


def _run(steps):
    while True:
        try:
            next(steps)
        except StopIteration as done:
            return done.value


def _paired(progress, pieces):
    n, done = len(pieces), 1
    pieces[0]()
    for frac in progress:
        while done < n and done <= frac * n:
            pieces[done]()
            done += 1
    while done < n:
        pieces[done]()
        done += 1


def _conv_taps(ext, halo, n, width):
    return [_down(ext, width - 1 - k)[halo:halo + n] for k in range(width)]


def _lru_gates(xc, wa_ref, ba, wx_ref, bx):
    xb = xc.astype(BF16)
    pa, px = [], []
    for h in range(LRU_HEADS):
        xh = xb[:, h * LRU_HEAD_DIM:(h + 1) * LRU_HEAD_DIM]
        pa.append(jnp.dot(xh, wa_ref[h], preferred_element_type=F32))
        px.append(jnp.dot(xh, wx_ref[h], preferred_element_type=F32))
    r = _sigmoid(jnp.concatenate(pa, axis=1) + ba)
    ig = _sigmoid(jnp.concatenate(px, axis=1) + bx)
    return r, ig


def _softplus_neg(lam):
    return jnp.maximum(-lam, 0.0) + jnp.log1p(jnp.exp(-jnp.abs(lam)))


def _recip_1_to_2(d):
    r0 = pl.reciprocal(d, approx=True)
    return r0 * (2.0 - d * r0)


def _lru_decay(r, sp, first):
    big_l = (-LRU_C) * r * sp
    a = jnp.exp(big_l)
    th = jnp.tanh(big_l)
    q = (-2.0 * th) * _recip_1_to_2(1.0 - th)
    rs = lax.rsqrt(jnp.maximum(q, SQRT_FLOOR))
    return a, jnp.where(first, 1.0, q * rs), rs


def _pool_inv_counts(t0, n):
    t = (t0 + lax.broadcasted_iota(jnp.int32, (n, 1), 0) + 1).astype(F32)
    return [1.0 / jnp.minimum(t, float(w)) for w in POOL_WINDOWS]


def _window_sums(ext, shift):
    gd = POOL_GROUP_DIM
    out = []
    s = ext
    for k in range(len(POOL_WINDOWS)):
        s = s + shift(s, 2 ** k)
        out.append(s[:, 0:gd])
        if k + 1 < len(POOL_WINDOWS):
            s = s[:, gd:]
    return out


SW_ROWS, SW_COLS = 16, 2 * D
SW_CONV, SW_SC, SW_POOL_B, SW_POOL_S = 0, 4, 8, 9


def _mod_fwd(c8, mod_w, mod_b, conv_w, sc_w, pool_b, pool_s):
    nw = mod_w.shape[2]
    cq, pq = conv_w.shape[1], pool_b.shape[1]

    def body(c_ref, w_ref, b_ref, cw_ref, sw_ref, pb_ref, ps_ref, ca_ref, mod_ref, small_ref,
             cslot, mslot, msend, pslot, psend, s1, r1, s2, r2, s3, r3):
        x, y, c = _pos()
        me = 4 * x + 2 * y + c
        chip = 2 * x + y
        first = []
        for r in range(1, N_DEV):
            fx, fy, fc = (r >> 2) & 1, (r >> 1) & 1, r & 1
            cp = pltpu.make_async_remote_copy(
                src_ref=c_ref, dst_ref=cslot.at[me], send_sem=s1.at[r - 1], recv_sem=r1.at[r - 1],
                device_id=(_flip(x, fx), _flip(y, fy), _flip(c, fc)), device_id_type=MESH)
            cp.start()
            first.append(cp)
        cslot[me] = c_ref[...]
        for cp in first:
            cp.wait()
        rows = _rows(SUBLANES, D)
        call = jnp.zeros((SUBLANES, D), F32)
        for d in range(N_DEV):
            call = jnp.where(rows == d, cslot[d], call)
        ca = call * _sigmoid(call)
        ca_ref[...] = ca
        for l in range(2):
            msend[l] = jnp.dot(ca, w_ref[l], precision=lax.Precision.HIGHEST, preferred_element_type=F32)
        psend[...] = jnp.zeros_like(psend)
        psend[SW_CONV:SW_CONV + 4, 0:cq] = cw_ref[...]
        psend[SW_SC:SW_SC + 3, 0:cq] = sw_ref[...]
        psend[SW_POOL_B:SW_POOL_B + 1, :] = pb_ref[...]
        psend[SW_POOL_S:SW_POOL_S + 1, :] = ps_ref[...]
        second = []
        for q, (fx, fy) in enumerate(((1, 0), (0, 1), (1, 1))):
            peer = (_flip(x, fx), _flip(y, fy), c)
            for src, dst, ss, rs in ((msend, mslot, s2, r2), (psend, pslot, s3, r3)):
                cp = pltpu.make_async_remote_copy(src_ref=src, dst_ref=dst.at[chip], send_sem=ss.at[q], recv_sem=rs.at[q],
                                                  device_id=peer, device_id_type=MESH)
                cp.start()
                second.append(cp)
        mslot[chip] = msend[...]
        pslot[chip] = psend[...]
        for cp in second:
            cp.wait()
        small_ref[...] = jnp.zeros_like(small_ref)
        for j in range(N_CHIP):
            for l in range(2):
                mod_ref[l, :, j * nw:(j + 1) * nw] = mslot[j, l] + b_ref[l:l + 1, j * nw:(j + 1) * nw]
            small_ref[0:SUBLANES, j * cq:(j + 1) * cq] = pslot[j, 0:SUBLANES, 0:cq]
            small_ref[SUBLANES:SW_ROWS, j * pq:(j + 1) * pq] = pslot[j, SUBLANES:SW_ROWS, :]

    args = (c8, mod_w, mod_b, conv_w, sc_w, pool_b, pool_s)
    dma3 = pltpu.SemaphoreType.DMA((N_CHIP - 1,))
    return pl.pallas_call(
        body, name="mod_fwd",
        in_specs=[VMEM] * len(args), out_specs=[VMEM] * 3,
        out_shape=[jax.ShapeDtypeStruct((SUBLANES, D), F32), jax.ShapeDtypeStruct((2, SUBLANES, N_CHIP * nw), F32),
                   jax.ShapeDtypeStruct((SW_ROWS, SW_COLS), F32)],
        scratch_shapes=[pltpu.VMEM((N_DEV, SUBLANES, D), F32), pltpu.VMEM((N_CHIP, 2, SUBLANES, nw), F32),
                        pltpu.VMEM((2, SUBLANES, nw), F32), pltpu.VMEM((N_CHIP, SW_ROWS, pq), F32),
                        pltpu.VMEM((SW_ROWS, pq), F32),
                        pltpu.SemaphoreType.DMA((N_DEV - 1,)), pltpu.SemaphoreType.DMA((N_DEV - 1,)),
                        dma3, dma3, dma3, dma3],
        compiler_params=_cp(),
    )(*args)


def _wcast(ws):
    def body(*refs):
        n = len(refs) // 2
        for a in range(n):
            refs[n + a][...] = refs[a][...].astype(BF16)

    return pl.pallas_call(
        body, name="wcast", in_specs=[VMEM] * len(ws), out_specs=[VMEM] * len(ws),
        out_shape=[jax.ShapeDtypeStruct(w.shape, BF16) for w in ws], compiler_params=_cp(),
    )(*ws)


def _wcast_own_block(w, kidx, name, after=None):
    rr, cc = w.shape
    rb = min(rr, 256)

    def body(k_ref, w_ref, *rest):
        rest[-1][...] = w_ref[...].astype(BF16)

    order = [] if after is None else [after]
    return pl.pallas_call(
        body, name=name,
        grid_spec=pltpu.PrefetchScalarGridSpec(
            num_scalar_prefetch=1, grid=(rr // rb,),
            in_specs=[pl.BlockSpec((rb, cc), lambda j, k_ref: (j, 0))] + [ANY] * len(order),
            out_specs=pl.BlockSpec((None, rb, cc), lambda j, k_ref: (k_ref[0], j, 0))),
        out_shape=jax.ShapeDtypeStruct((N_CHIP, rr, cc), BF16),
        compiler_params=_cp(("parallel",)),
    )(kidx, w, *order)


def _wgather_copies(outs, rows, ssem, rsem, fssem, frsem):
    n = len(outs)
    x, y, c = _pos()
    chip = 2 * x + y
    sib = (x, y, 1 - c)
    flips = ((1, 0), (0, 1), (1, 1))

    def half(a, which):
        hr = rows[a] // 2
        return pl.ds(pl.multiple_of(which * hr, BF16_ROWS), hr)

    sends = []
    for a in range(n):
        mine = outs[a].at[chip, half(a, c), :]
        for q, (fx, fy) in enumerate(flips):
            cp = pltpu.make_async_remote_copy(
                src_ref=mine, dst_ref=mine, send_sem=ssem.at[3 * a + q], recv_sem=rsem.at[3 * a + q],
                device_id=(_flip(x, fx), _flip(y, fy), c), device_id_type=MESH)
            cp.start()
            sends.append(cp)
    passed = []
    for a in range(n):
        for q, (fx, fy) in enumerate(flips):
            src_chip = 2 * _flip(x, fx) + _flip(y, fy)
            landed = outs[a].at[src_chip, half(a, c), :]
            pltpu.make_async_remote_copy(
                src_ref=landed, dst_ref=landed, send_sem=ssem.at[3 * a + q], recv_sem=rsem.at[3 * a + q],
                device_id=sib, device_id_type=MESH).wait_recv()
            cp = pltpu.make_async_remote_copy(
                src_ref=landed, dst_ref=landed, send_sem=fssem.at[3 * a + q], recv_sem=frsem.at[3 * a + q],
                device_id=sib, device_id_type=MESH)
            cp.start()
            passed.append(cp)
    for a in range(n):
        for q, (fx, fy) in enumerate(flips):
            src_chip = 2 * _flip(x, fx) + _flip(y, fy)
            other = outs[a].at[src_chip, half(a, 1 - c), :]
            pltpu.make_async_remote_copy(
                src_ref=other, dst_ref=other, send_sem=fssem.at[3 * a + q], recv_sem=frsem.at[3 * a + q],
                device_id=sib, device_id_type=MESH).wait_recv()
    for cp in sends + passed:
        cp.wait_send()


def _wgather(bufs, name):
    n = len(bufs)

    def body(*refs):
        _wgather_copies(refs[n:2 * n], [b.shape[1] for b in bufs], *refs[2 * n:])

    return pl.pallas_call(
        body, name=name, in_specs=[ANY] * n, out_specs=[ANY] * n,
        out_shape=[jax.ShapeDtypeStruct(b.shape, BF16) for b in bufs],
        input_output_aliases={a: a for a in range(n)},
        scratch_shapes=[pltpu.SemaphoreType.DMA((3 * n,))] * 4,
        compiler_params=_cp(),
    )(*bufs)


def _wgather_sequencer(bufs, name):
    n = len(bufs)
    refs = [jax.new_ref(b, memory_space=pltpu.MemorySpace.HBM) for b in bufs]
    dma = pltpu.SemaphoreType.DMA((3 * n,))

    @pl.kernel(mesh=plsc.ScalarSubcoreMesh(axis_name="sequencer", num_cores=1), name=name,
               scratch_types=(dma, dma, dma, dma), compiler_params=pltpu.CompilerParams(collective_id=CID_WGATHER))
    def launch(ssem, rsem, fssem, frsem):
        x, y, c = _pos()
        barrier = pltpu.get_barrier_semaphore()
        for peer in ((1 - x, y, c), (x, 1 - y, c), (1 - x, 1 - y, c), (x, y, 1 - c)):
            pl.semaphore_signal(barrier, inc=1, device_id=peer, device_id_type=MESH)
        pl.semaphore_wait(barrier, 4)
        _wgather_copies(refs, [b.shape[1] for b in bufs], ssem, rsem, fssem, frsem)

    launch()
    return [r[...] for r in refs]


def _norm_proj(x, g, sc, sh, w, name):
    s_len, nb = x.shape[0], w.shape[2]
    ts = _tile(s_len, TS_PROJ)

    def body(x_ref, g_ref, sc_ref, sh_ref, w_ref, h_ref, p_ref):
        @pl.when(pl.program_id(1) == 0)
        def _():
            xv = x_ref[...]
            r = lax.rsqrt(jnp.mean(xv * xv, axis=-1, keepdims=True) + RMS_EPS)
            h_ref[...] = (xv * r * (g_ref[...] * (1.0 + sc_ref[...])) + sh_ref[...]).astype(BF16)

        p_ref[...] = jnp.dot(h_ref[...], w_ref[...], preferred_element_type=F32).astype(BF16)

    vec = pl.BlockSpec((1, D), lambda i, j: (0, 0))
    return pl.pallas_call(
        body, name=name, grid=(s_len // ts, N_CHIP),
        in_specs=[pl.BlockSpec((ts, D), lambda i, j: (i, 0)), vec, vec, vec,
                  pl.BlockSpec((None, D, nb), lambda i, j: (j, 0, 0))],
        out_specs=[pl.BlockSpec((ts, D), lambda i, j: (i, 0)), pl.BlockSpec((ts, nb), lambda i, j: (i, j))],
        out_shape=[jax.ShapeDtypeStruct((s_len, D), BF16), jax.ShapeDtypeStruct((s_len, N_CHIP * nb), BF16)],
        compiler_params=_cp(("parallel", "arbitrary")),
    )(x, g, sc, sh, w)


def _l0_fwd(x, g, sc, sh, w_in, gate, cw, cb, wa, ba, wx, bx, lam, sw, wo):
    s_len, nb = x.shape[0], w_in.shape[2]
    ts = _tile(s_len, TS_MIX)
    n_t = s_len // ts
    hl = SUBLANES

    def body(xa_ref, xb_ref, g_ref, sc_ref, sh_ref, win_ref, gate_ref, cw_ref, cb_ref, wa_ref, ba_ref, wx_ref, bx_ref,
             lam_ref, sw_ref, wo_ref, x1_ref, h_ref, y_ref, h0_ref, p_ref, pcur, pnext, cxa, czz, chh):
        i = pl.program_id(0)

        @pl.when(i == 0)
        def _():
            cxa[...] = jnp.zeros_like(cxa)
            czz[...] = jnp.zeros_like(czz)
            chh[...] = jnp.zeros_like(chh)
            pnext[...] = jnp.zeros_like(pnext)

        pcur[...] = pnext[...]
        xv = xa_ref[...]
        rinv = lax.rsqrt(jnp.mean(xv * xv, axis=-1, keepdims=True) + RMS_EPS)
        h0 = (xv * rinv * (g_ref[...] * (1.0 + sc_ref[...])) + sh_ref[...]).astype(BF16)
        h0_ref[...] = h0

        def project(k):
            def emit():
                pk = jnp.dot(h0, win_ref[k], preferred_element_type=F32).astype(BF16)
                p_ref[:, k * nb:(k + 1) * nb] = pk
                pnext[:, k * nb:(k + 1) * nb] = pk
            return emit

        def mixer():
            piece = lambda k: pcur[:, k * D:(k + 1) * D].astype(F32)
            xa = piece(0)
            rows = _rows(ts, D)
            taps = _conv_taps(jnp.concatenate([cxa[...], xa], axis=0), hl, ts, 4)
            xc = cb_ref[...] + sum(cw_ref[k:k + 1, :] * taps[k] for k in range(4))
            r, ig = _lru_gates(xc, wa_ref, ba_ref[...], wx_ref, bx_ref[...])
            a, m, _ = _lru_decay(r, _softplus_neg(lam_ref[...]), (rows == 0) & (i == 1))
            yield 0.26
            h = _run(_scan_fwd_steps(a, m * ig * xc, chh[hl - 1:hl, :]))
            yield 0.51
            gcp, v = piece(3), piece(4)
            z = gcp * v
            ztaps = _conv_taps(jnp.concatenate([czz[...], z], axis=0), hl, ts, 3)
            yb = piece(2) * sum(sw_ref[k:k + 1, :] * ztaps[k] for k in range(3))
            ga, gb = piece(1), piece(5)
            y = jnp.concatenate([h * (ga * _sigmoid(ga)), yb * (gb * _sigmoid(gb))], axis=1).astype(BF16)
            yield 0.76
            y_ref[...] = y
            x1_ref[...] = xb_ref[...] + gate_ref[...] * jnp.dot(y, wo_ref[...], preferred_element_type=F32)
            h_ref[...] = h.astype(BF16)
            cxa[...] = xa[ts - hl:, :]
            czz[...] = z[ts - hl:, :]
            chh[...] = jnp.where(i > 0, h[ts - hl:, :], 0.0)

        _paired(mixer(), [project(k) for k in range(N_CHIP)])

    def full(a):
        return pl.BlockSpec(a.shape, lambda i: (0,) * a.ndim)

    ahead = lambda w: pl.BlockSpec((ts, w), lambda i: (jnp.minimum(i, n_t - 1), 0))
    behind = lambda w: pl.BlockSpec((ts, w), lambda i: (jnp.maximum(i - 1, 0), 0))
    args = (x, x, g, sc, sh, w_in, gate, cw, cb, wa, ba, wx, bx, lam, sw, wo)
    return pl.pallas_call(
        body, name="l0_fwd", grid=(n_t + 1,),
        in_specs=[ahead(D), behind(D)] + [full(a) for a in args[2:]],
        out_specs=[behind(D), behind(D), behind(2 * D), ahead(D), ahead(N_CHIP * nb)],
        out_shape=[jax.ShapeDtypeStruct((s_len, D), F32), jax.ShapeDtypeStruct((s_len, D), BF16),
                   jax.ShapeDtypeStruct((s_len, 2 * D), BF16), jax.ShapeDtypeStruct((s_len, D), BF16),
                   jax.ShapeDtypeStruct((s_len, N_CHIP * nb), BF16)],
        scratch_shapes=[pltpu.VMEM((ts, N_CHIP * nb), BF16)] * 2 + [pltpu.VMEM((hl, D), F32)] * 3,
        compiler_params=_cp(("arbitrary",)),
    )(*args)


def _l1_mix(proj, x1, tgt, gate, wg, bg, scale, wo, gf):
    s_len = x1.shape[0]
    ts = _tile(s_len, TS_MIX)
    pw, gd, hl = 2 * D, POOL_GROUP_DIM, POOL_HALO

    def body(p_ref, x_ref, t_ref, gate_ref, wg_ref, bg_ref, sc_ref, wo_ref, gf_ref,
             d_ref, mx_ref, y_ref, dx_ref, loss_ref, dgf_ref, cv):
        i = pl.program_id(0)

        @pl.when(i == 0)
        def _():
            cv[...] = jnp.zeros_like(cv)
            loss_ref[...] = jnp.zeros_like(loss_ref)
            dgf_ref[...] = jnp.zeros_like(dgf_ref)

        v = p_ref[:, 0:pw].astype(F32)
        gg = p_ref[:, pw:2 * pw].astype(F32)
        sums = _window_sums(jnp.concatenate([cv[...], v], axis=0), _down)
        inv = _pool_inv_counts(i * ts, ts)
        dd = [sums[k][hl:hl + ts] * inv[k] - v[:, k * gd:(k + 1) * gd] for k in range(4)]
        mixed = jnp.concatenate(
            [jnp.dot(dd[k].astype(BF16), wg_ref[k], preferred_element_type=F32) for k in range(4)], axis=1) + bg_ref[...]
        d_ref[...] = jnp.concatenate(dd, axis=1).astype(BF16)
        mx_ref[...] = mixed.astype(BF16)
        y = (mixed * sc_ref[...] * (gg * _sigmoid(gg))).astype(BF16)
        y_ref[...] = y
        x2 = x_ref[...] + gate_ref[...] * jnp.dot(y, wo_ref[...], preferred_element_type=F32)
        r2 = lax.rsqrt(jnp.mean(x2 * x2, axis=-1, keepdims=True) + RMS_EPS)
        n2 = x2 * r2
        err = n2 * gf_ref[...] - t_ref[...]
        loss_ref[...] += jnp.sum(err * err, axis=0, keepdims=True)
        dyf = err * (1.0 / D)
        dgf_ref[...] += jnp.sum(dyf * n2, axis=0, keepdims=True)
        dn = dyf * gf_ref[...]
        dx_ref[...] = r2 * (dn - n2 * jnp.mean(dn * n2, axis=-1, keepdims=True))
        cv[...] = v[ts - hl:, :]

    def full(a):
        return pl.BlockSpec(a.shape, lambda i: (0,) * a.ndim)

    row = lambda w: pl.BlockSpec((ts, w), lambda i: (i, 0))
    acc = pl.BlockSpec((1, D), lambda i: (0, 0))
    return pl.pallas_call(
        body, name="l1_mix", grid=(s_len // ts,),
        in_specs=[row(2 * pw), row(D), row(D)] + [full(a) for a in (gate, wg, bg, scale, wo, gf)],
        out_specs=[row(pw), row(pw), row(pw), row(D), acc, acc],
        out_shape=[jax.ShapeDtypeStruct((s_len, pw), BF16)] * 3 + [jax.ShapeDtypeStruct((s_len, D), F32)]
        + [jax.ShapeDtypeStruct((1, D), F32)] * 2,
        scratch_shapes=[pltpu.VMEM((hl, pw), F32)],
        compiler_params=_cp(("arbitrary",)),
    )(proj, x1, tgt, gate, wg, bg, scale, wo, gf)


def _l1_bwd_mix(dx2, proj, mixed, y, dpool, gate, wg, scale, wo):
    s_len = dx2.shape[0]
    ts = _tile(s_len, TS_MIX)
    n_t = s_len // ts
    pw, gd, hl = 2 * D, POOL_GROUP_DIM, POOL_HALO

    def body(dx_ref, gg_ref, mx_ref, y_ref, d_ref, gate_ref, wg_ref, sc_ref, wo_ref,
             dp_ref, mt_ref, dwg_ref, dsc_ref, dbg_ref, cq):
        i = pl.program_id(0)

        @pl.when(i == 0)
        def _():
            cq[...] = jnp.zeros_like(cq)
            dsc_ref[...] = jnp.zeros_like(dsc_ref)
            dbg_ref[...] = jnp.zeros_like(dbg_ref)
            mt_ref[...] = jnp.zeros_like(mt_ref)
            dwg_ref[...] = jnp.zeros_like(dwg_ref)

        dxv = dx_ref[...]
        dxb = dxv.astype(BF16)

        def wgrad_out(k):
            mt_ref[k] += lax.dot_general(y_ref[:, k * gd:(k + 1) * gd], dxb, TN, preferred_element_type=F32)

        dy = lax.dot_general((gate_ref[...] * dxv).astype(BF16), wo_ref[...], NT, preferred_element_type=F32)
        wgrad_out(0)
        gg = gg_ref[...].astype(F32)
        mixed = mx_ref[...].astype(F32)
        s = _sigmoid(gg)
        sg = gg * s
        dmixed = dy * sc_ref[...] * sg
        dsc_ref[...] += jnp.sum(dy * mixed * sg, axis=0, keepdims=True)
        dbg_ref[...] += jnp.sum(dmixed, axis=0, keepdims=True)
        dmb = dmixed.astype(BF16)
        wgrad_out(1)
        dp_ref[:, pw:2 * pw] = (dy * sc_ref[...] * mixed * (s * (1.0 + gg * (1.0 - s)))).astype(BF16)
        inv = _pool_inv_counts((n_t - 1 - i) * ts, ts)
        dd = []
        for k in range(4):
            dmk = dmb[:, k * gd:(k + 1) * gd]
            dd.append(lax.dot_general(dmk, wg_ref[k], NT, preferred_element_type=F32))
            dwg_ref[k] += lax.dot_general(d_ref[:, k * gd:(k + 1) * gd], dmk, TN, preferred_element_type=F32)
        wgrad_out(2)
        q = jnp.concatenate([dd[k] * inv[k] for k in range(4)], axis=1)
        sums = _window_sums(jnp.concatenate([q, cq[...]], axis=0), _up)
        wgrad_out(3)
        dp_ref[:, 0:pw] = jnp.concatenate([sums[k][0:ts] - dd[k] for k in range(4)], axis=1).astype(BF16)
        cq[...] = q[0:hl, :]

    def full(a):
        return pl.BlockSpec(a.shape, lambda i: (0,) * a.ndim)

    rev = lambda w, j=0: pl.BlockSpec((ts, w), lambda i: (n_t - 1 - i, j))
    acc = pl.BlockSpec((1, pw), lambda i: (0, 0))
    return pl.pallas_call(
        body, name="l1_bwd_mix", grid=(n_t,),
        in_specs=[rev(D), rev(pw, 1), rev(pw), rev(pw), rev(pw)] + [full(a) for a in (gate, wg, scale, wo)],
        out_specs=[rev(2 * pw), pl.BlockSpec((N_CHIP, gd, D), lambda i: (0, 0, 0)),
                   pl.BlockSpec((4, gd, gd), lambda i: (0, 0, 0)), acc, acc],
        out_shape=[jax.ShapeDtypeStruct((s_len, 2 * pw), BF16), jax.ShapeDtypeStruct((N_CHIP, gd, D), F32),
                   jax.ShapeDtypeStruct((4, gd, gd), F32),
                   jax.ShapeDtypeStruct((1, pw), F32), jax.ShapeDtypeStruct((1, pw), F32)],
        scratch_shapes=[pltpu.VMEM((hl, pw), F32)],
        compiler_params=_cp(("arbitrary",)),
    )(dx2, proj, mixed, y, dpool, gate, wg, scale, wo)


def _l0_bwd_mix(dx1, proj, hst, y, gate, cw, cb, wa, ba, wx, bx, lam, sw, wo):
    s_len = dx1.shape[0]
    ts = _tile(s_len, TS_MIX)
    n_t = s_len // ts
    hl, hb = SUBLANES, BF16_ROWS
    yb_w = 2 * D // N_CHIP

    def body(dx_ref, p_ref, ph_ref, h_ref, hh_ref, y_ref, gate_ref, cw_ref, cb_ref, wa_ref, ba_ref, wx_ref, bx_ref,
             lam_ref, sw_ref, wo_ref, dp_ref, mt_ref, dwa_ref, dwx_ref, sm_ref, cg, cdxc, cdcz, ca):
        i = pl.program_id(0)
        ri = n_t - 1 - i

        @pl.when(i == 0)
        def _():
            cg[...] = jnp.zeros_like(cg)
            ca[...] = jnp.zeros_like(ca)
            cdxc[...] = jnp.zeros_like(cdxc)
            cdcz[...] = jnp.zeros_like(cdcz)
            sm_ref[...] = jnp.zeros_like(sm_ref)
            mt_ref[...] = jnp.zeros_like(mt_ref)
            dwa_ref[...] = jnp.zeros_like(dwa_ref)
            dwx_ref[...] = jnp.zeros_like(dwx_ref)

        dxb = dx_ref[...].astype(BF16)

        def wgrad_out(k):
            mt_ref[k] += lax.dot_general(y_ref[:, k * yb_w:(k + 1) * yb_w], dxb, TN, preferred_element_type=F32)

        wgrad_out(0)
        has_prev = (ri > 0).astype(F32)
        xa, ga, gbp, gcp, v, gb = [p_ref[:, k * D:(k + 1) * D].astype(F32) for k in range(6)]
        prev = lambda k: ph_ref[:, k * D:(k + 1) * D].astype(F32)[hb - hl:hb] * has_prev
        rows = _rows(ts, D)
        first = (rows == 0) & (ri == 0)
        xtaps = _conv_taps(jnp.concatenate([prev(0), xa], axis=0), hl, ts, 4)
        xc = cb_ref[...] + sum(cw_ref[k:k + 1, :] * xtaps[k] for k in range(4))
        r, ig = _lru_gates(xc, wa_ref, ba_ref[...], wx_ref, bx_ref[...])
        sp = _softplus_neg(lam_ref[...])
        a, m, inv_m = _lru_decay(r, sp, first)
        z = gcp * v
        ztaps = _conv_taps(jnp.concatenate([prev(3) * prev(4), z], axis=0), hl, ts, 3)
        cz = sum(sw_ref[k:k + 1, :] * ztaps[k] for k in range(3))
        h = h_ref[...].astype(F32)
        hprev = _down(jnp.concatenate([hh_ref[...].astype(F32)[hb - hl:hb] * has_prev, h], axis=0), 1)[hl:hl + ts]
        dy = lax.dot_general((gate_ref[...] * dx_ref[...]).astype(BF16), wo_ref[...], NT, preferred_element_type=F32)
        dya_pre, dyb_pre = dy[:, 0:D], dy[:, D:2 * D]
        s_a, s_b = _sigmoid(ga), _sigmoid(gb)
        dp_ref[:, D:2 * D] = (dya_pre * h * (s_a * (1.0 + ga * (1.0 - s_a)))).astype(BF16)
        dp_ref[:, 5 * D:6 * D] = (dyb_pre * (gbp * cz) * (s_b * (1.0 + gb * (1.0 - s_b)))).astype(BF16)
        dya = dya_pre * (ga * s_a)
        dyb = dyb_pre * (gb * s_b)
        wgrad_out(1)
        dp_ref[:, 2 * D:3 * D] = (dyb * cz).astype(BF16)
        dcz = dyb * gbp
        for k in range(3):
            sm_ref[8 + k:9 + k, :] += jnp.sum(dcz * ztaps[k], axis=0, keepdims=True)
        dcz_ext = jnp.concatenate([dcz, cdcz[...]], axis=0)
        dz = sum(sw_ref[k:k + 1, :] * _up(dcz_ext, 2 - k)[0:ts] for k in range(3))
        dp_ref[:, 3 * D:4 * D] = (dz * v).astype(BF16)
        dp_ref[:, 4 * D:5 * D] = (dz * gcp).astype(BF16)
        cdcz[...] = dcz[0:hl, :]
        alpha = _up(jnp.concatenate([a, ca[...]], axis=0), 1)[0:ts]
        wgrad_out(2)
        dh = _run(_scan_rev_steps(alpha, dya, cg[0:1, :]))
        wgrad_out(3)
        cg[...] = dh[0:hl, :]
        ca[...] = a[0:hl, :]
        da = dh * hprev
        dm = dh * ig * xc
        di = dh * m * xc
        dxc = dh * m * ig
        dl = da * a - jnp.where(first, 0.0, dm * (a * a) * inv_m)
        sm_ref[7:8, :] += jnp.sum(dl * r, axis=0, keepdims=True) * (-LRU_C)
        dpa = (dl * sp) * (-LRU_C) * r * (1.0 - r)
        dpx = di * ig * (1.0 - ig)
        sm_ref[5:6, :] += jnp.sum(dpa, axis=0, keepdims=True)
        sm_ref[6:7, :] += jnp.sum(dpx, axis=0, keepdims=True)
        dpa_b, dpx_b, xc_b = dpa.astype(BF16), dpx.astype(BF16), xc.astype(BF16)
        back = []
        for hd in range(LRU_HEADS):
            sl = slice(hd * LRU_HEAD_DIM, (hd + 1) * LRU_HEAD_DIM)
            back.append(lax.dot_general(dpa_b[:, sl], wa_ref[hd], NT, preferred_element_type=F32)
                        + lax.dot_general(dpx_b[:, sl], wx_ref[hd], NT, preferred_element_type=F32))
            dwa_ref[hd] += lax.dot_general(xc_b[:, sl], dpa_b[:, sl], TN, preferred_element_type=F32)
            dwx_ref[hd] += lax.dot_general(xc_b[:, sl], dpx_b[:, sl], TN, preferred_element_type=F32)
        dxc = dxc + jnp.concatenate(back, axis=1)
        sm_ref[4:5, :] += jnp.sum(dxc, axis=0, keepdims=True)
        for k in range(4):
            sm_ref[k:k + 1, :] += jnp.sum(dxc * xtaps[k], axis=0, keepdims=True)
        dxc_ext = jnp.concatenate([dxc, cdxc[...]], axis=0)
        dp_ref[:, 0:D] = sum(cw_ref[k:k + 1, :] * _up(dxc_ext, 3 - k)[0:ts] for k in range(4)).astype(BF16)
        cdxc[...] = dxc[0:hl, :]

    def full(a):
        return pl.BlockSpec(a.shape, lambda i: (0,) * a.ndim)

    rev = lambda w: pl.BlockSpec((ts, w), lambda i: (n_t - 1 - i, 0))
    halo = lambda w: pl.BlockSpec((hb, w), lambda i: (jnp.maximum((n_t - 1 - i) * (ts // hb) - 1, 0), 0))
    return pl.pallas_call(
        body, name="l0_bwd_mix", grid=(n_t,),
        in_specs=[rev(D), rev(6 * D), halo(6 * D), rev(D), halo(D), rev(2 * D)]
        + [full(a) for a in (gate, cw, cb, wa, ba, wx, bx, lam, sw, wo)],
        out_specs=[rev(6 * D), pl.BlockSpec((N_CHIP, yb_w, D), lambda i: (0, 0, 0)),
                   pl.BlockSpec(wa.shape, lambda i: (0, 0, 0)), pl.BlockSpec(wa.shape, lambda i: (0, 0, 0)),
                   pl.BlockSpec((2 * SUBLANES, D), lambda i: (0, 0))],
        out_shape=[jax.ShapeDtypeStruct((s_len, 6 * D), BF16), jax.ShapeDtypeStruct((N_CHIP, yb_w, D), F32),
                   jax.ShapeDtypeStruct(wa.shape, F32), jax.ShapeDtypeStruct(wa.shape, F32),
                   jax.ShapeDtypeStruct((2 * SUBLANES, D), F32)],
        scratch_shapes=[pltpu.VMEM((hl, D), F32)] * 4,
        compiler_params=_cp(("arbitrary",)),
    )(dx1, proj, proj, hst, hst, y, gate, cw, cb, wa, ba, wx, bx, lam, sw, wo)


def _dgrad_norm(dproj, w, x, dres, g, sc, name, after=None):
    s_len, nb = x.shape[0], w.shape[2]
    ts = _tile(s_len, TS_DGRAD)
    order = [] if after is None else [after]

    def body(dp_ref, w_ref, x_ref, dr_ref, g_ref, sc_ref, *rest):
        dx_ref, s1_ref, s2_ref = rest[len(order):]

        @pl.when(pl.program_id(0) == 0)
        def _():
            s1_ref[...] = jnp.zeros_like(s1_ref)
            s2_ref[...] = jnp.zeros_like(s2_ref)

        dh = sum(lax.dot_general(dp_ref[:, k * nb:(k + 1) * nb], w_ref[k], NT, preferred_element_type=F32)
                 for k in range(N_CHIP))
        xv = x_ref[...]
        r = lax.rsqrt(jnp.mean(xv * xv, axis=-1, keepdims=True) + RMS_EPS)
        n = xv * r
        s1_ref[...] += jnp.sum(dh, axis=0, keepdims=True)
        s2_ref[...] += jnp.sum(dh * n, axis=0, keepdims=True)
        dn = dh * (g_ref[...] * (1.0 + sc_ref[...]))
        dx_ref[...] = dr_ref[...] + r * (dn - n * jnp.mean(dn * n, axis=-1, keepdims=True))

    row = lambda wd: pl.BlockSpec((ts, wd), lambda i: (i, 0))
    vec = pl.BlockSpec((1, D), lambda i: (0, 0))
    return pl.pallas_call(
        body, name=name, grid=(s_len // ts,),
        in_specs=[row(N_CHIP * nb), pl.BlockSpec(w.shape, lambda i: (0, 0, 0)), row(D), row(D), vec, vec]
        + [ANY] * len(order),
        out_specs=[row(D), vec, vec],
        out_shape=[jax.ShapeDtypeStruct((s_len, D), F32)] + [jax.ShapeDtypeStruct((1, D), F32)] * 2,
        compiler_params=_cp(("arbitrary",)),
    )(dproj, w, x, dres, g, sc, *order)


def _wgrad(a, b, groups, ka, nb, a_col, b_col, name):
    s_len = a.shape[0]
    ts = _tile(s_len, TS_WGRAD)

    def body(a_ref, b_ref, o_ref):
        @pl.when(pl.program_id(1) == 0)
        def _():
            o_ref[...] = jnp.zeros_like(o_ref)

        o_ref[...] += lax.dot_general(a_ref[...].astype(BF16), b_ref[...].astype(BF16), TN, preferred_element_type=F32)

    return pl.pallas_call(
        body, name=name, grid=(groups, s_len // ts),
        in_specs=[pl.BlockSpec((ts, ka), lambda g, s: (s, a_col(g))), pl.BlockSpec((ts, nb), lambda g, s: (s, b_col(g)))],
        out_specs=pl.BlockSpec((None, ka, nb), lambda g, s: (g, 0, 0)),
        out_shape=jax.ShapeDtypeStruct((groups, ka, nb), F32),
        compiler_params=_cp(("parallel", "arbitrary")),
    )(a, b)


def _wo_final(mt, wo, gate, name):
    rb = mt.shape[1]

    def body(m_ref, w_ref, gate_ref, dw_ref, dg_ref):
        @pl.when(pl.program_id(0) == 0)
        def _():
            dg_ref[...] = jnp.zeros_like(dg_ref)

        mv = m_ref[...]
        dw_ref[...] = mv * gate_ref[...]
        dg_ref[...] += jnp.sum(mv * w_ref[...].astype(F32), axis=0, keepdims=True)

    blk = pl.BlockSpec((None, rb, D), lambda k: (k, 0, 0))
    vec = pl.BlockSpec((1, D), lambda k: (0, 0))
    return pl.pallas_call(
        body, name=name, grid=(N_CHIP,), in_specs=[blk, blk, vec], out_specs=[blk, vec],
        out_shape=[jax.ShapeDtypeStruct(mt.shape, F32), jax.ShapeDtypeStruct((1, D), F32)],
        compiler_params=_cp(("arbitrary",)),
    )(mt, wo, gate)


ROW_NORM_G, ROW_CONV_W, ROW_CONV_B, ROW_B_A, ROW_B_X, ROW_LAMBDA, ROW_SC_W, ROW_POOL_B, ROW_POOL_S, ROW_FINAL_G = (
    0, 2, 6, 7, 8, 9, 10, 13, 15, 17)


def _small_pack(s1_0, s2_0, s1_1, s2_1, sm0, dsc1, dbg1, dgf, losscols, dgate0, dgate1, norm_g, sc0, sc1, lam):
    def body(s1_0r, s2_0r, s1_1r, s2_1r, sm, dsc, dbg, dgfr, lcols, dg0, dg1, ng, sc0r, sc1r, lamr, buf, dmod, loss):
        buf[...] = jnp.zeros_like(buf)
        buf[0:1, :] = s2_0r[...] * (1.0 + sc0r[...])
        buf[1:2, :] = s2_1r[...] * (1.0 + sc1r[...])
        buf[ROW_CONV_W:ROW_CONV_W + 4, :] = sm[0:4, :]
        buf[ROW_CONV_B:ROW_CONV_B + 1, :] = sm[4:5, :]
        buf[ROW_B_A:ROW_B_A + 1, :] = sm[5:6, :]
        buf[ROW_B_X:ROW_B_X + 1, :] = sm[6:7, :]
        buf[ROW_LAMBDA:ROW_LAMBDA + 1, :] = -sm[7:8, :] * _sigmoid(-lamr[...])
        buf[ROW_SC_W:ROW_SC_W + 3, :] = sm[8:11, :]
        for k in range(2):
            buf[ROW_POOL_B + k:ROW_POOL_B + k + 1, :] = dbg[:, k * D:(k + 1) * D]
            buf[ROW_POOL_S + k:ROW_POOL_S + k + 1, :] = dsc[:, k * D:(k + 1) * D]
        buf[ROW_FINAL_G:ROW_FINAL_G + 1, :] = dgfr[...]
        pieces = (s1_0r[...], s2_0r[...] * ng[0:1, :], dg0[...], s1_1r[...], s2_1r[...] * ng[1:2, :], dg1[...])
        for k, pc in enumerate(pieces):
            dmod[:, k * D:(k + 1) * D] = jnp.broadcast_to(pc, (SUBLANES, D))
        loss[...] = jnp.broadcast_to(jnp.sum(lcols[...], axis=1, keepdims=True) * (0.5 / D), loss.shape)

    args = (s1_0, s2_0, s1_1, s2_1, sm0, dsc1, dbg1, dgf, losscols, dgate0, dgate1, norm_g, sc0, sc1, lam)
    return pl.pallas_call(
        body, name="small_pack", in_specs=[VMEM] * len(args), out_specs=[VMEM] * 3,
        out_shape=[jax.ShapeDtypeStruct((SMALL_ROWS, D), F32), jax.ShapeDtypeStruct((SUBLANES, 6 * D), F32),
                   jax.ShapeDtypeStruct((SUBLANES, 128), F32)],
        compiler_params=_cp(),
    )(*args)


def _small_comm(buf_a, buf_b, dmod8):
    ra, rb = buf_a.shape[0] // N_DEV, buf_b.shape[0] // N_DEV
    wb = buf_b.shape[1]

    def body(a_ref, b_ref, dm_ref, oa_ref, ob_ref, odm_ref, ina, inb, dslot, sa, sb, s1, r1, s2, r2):
        x, y, c = _pos()
        me = 4 * x + 2 * y + c
        peers = []
        for r in range(1, N_DEV):
            fx, fy, fc = (r >> 2) & 1, (r >> 1) & 1, r & 1
            px, py, pc = _flip(x, fx), _flip(y, fy), _flip(c, fc)
            peers.append(((px, py, pc), 4 * px + 2 * py + pc))
        seg_a = lambda d: pl.ds(pl.multiple_of(d * ra, SUBLANES), ra)
        seg_b = lambda d: pl.ds(pl.multiple_of(d * rb, SUBLANES), rb)
        first = []
        for r, (peer, pid) in enumerate(peers):
            for k, (src, dst) in enumerate(((a_ref.at[seg_a(pid), :], ina.at[r]), (b_ref.at[seg_b(pid), :], inb.at[r]),
                                            (dm_ref, dslot.at[me]))):
                cp = pltpu.make_async_remote_copy(src_ref=src, dst_ref=dst, send_sem=s1.at[3 * r + k],
                                                  recv_sem=r1.at[3 * r + k], device_id=peer, device_id_type=MESH)
                cp.start()
                first.append(cp)
        dslot[me] = dm_ref[...]
        for cp in first:
            cp.wait()
        acc_a, acc_b = a_ref[seg_a(me), :], b_ref[seg_b(me), :]
        for r in range(N_DEV - 1):
            acc_a = acc_a + ina[r]
            acc_b = acc_b + inb[r]
        sa[...] = acc_a
        sb[...] = acc_b
        oa_ref[seg_a(me), :] = acc_a
        ob_ref[seg_b(me), :] = acc_b
        second = []
        for r, (peer, pid) in enumerate(peers):
            for k, (src, dst) in enumerate(((sa, oa_ref.at[seg_a(me), :]), (sb, ob_ref.at[seg_b(me), :]))):
                cp = pltpu.make_async_remote_copy(src_ref=src, dst_ref=dst, send_sem=s2.at[2 * r + k],
                                                  recv_sem=r2.at[2 * r + k], device_id=peer, device_id_type=MESH)
                cp.start()
                second.append(cp)
        rows = _rows(SUBLANES, dm_ref.shape[1])
        dm_all = jnp.zeros(dm_ref.shape, F32)
        for d in range(N_DEV):
            dm_all = jnp.where(rows == d, dslot[d], dm_all)
        odm_ref[...] = dm_all
        for cp in second:
            cp.wait()

    nrel = N_DEV - 1
    return pl.pallas_call(
        body, name="small_comm", in_specs=[VMEM] * 3, out_specs=[VMEM] * 3,
        out_shape=[jax.ShapeDtypeStruct(buf_a.shape, F32), jax.ShapeDtypeStruct(buf_b.shape, F32),
                   jax.ShapeDtypeStruct(dmod8.shape, F32)],
        scratch_shapes=[pltpu.VMEM((nrel, ra, D), F32), pltpu.VMEM((nrel, rb, wb), F32),
                        pltpu.VMEM((N_DEV,) + dmod8.shape, F32), pltpu.VMEM((ra, D), F32), pltpu.VMEM((rb, wb), F32),
                        pltpu.SemaphoreType.DMA((3 * nrel,)), pltpu.SemaphoreType.DMA((3 * nrel,)),
                        pltpu.SemaphoreType.DMA((2 * nrel,)), pltpu.SemaphoreType.DMA((2 * nrel,))],
        compiler_params=_cp(),
    )(buf_a, buf_b, dmod8)


def _adam(w, g, m, v):
    m2 = ADAM_B1 * m + (1.0 - ADAM_B1) * g
    v2 = ADAM_B2 * v + (1.0 - ADAM_B2) * (g * g)
    m_hat = m2 / (1.0 - ADAM_B1 ** ADAM_STEP)
    v_hat = v2 / (1.0 - ADAM_B2 ** ADAM_STEP)
    return -ADAM_LR * (m_hat / (jnp.sqrt(v_hat) + ADAM_EPS) + ADAM_WD * w), m2, v2


def _small_adam(red_a, red_b, dm_all, params):
    n = len(params)

    def body(*refs):
        ra, rb, dm = refs[:3]
        wmv = refs[3:3 + 3 * n]
        outs = refs[3 + 3 * n:]
        x, y, _ = _pos()
        chip = 2 * x + y

        def shard(row0, nrows, width):
            per_row = D // width
            cands = []
            for k in range(N_CHIP):
                if nrows == 1 or per_row >= N_CHIP:
                    cands.append(ra[row0:row0 + nrows, k * width:(k + 1) * width])
                else:
                    rr, cc = divmod(k * width, D)
                    cands.append(ra[row0 + rr:row0 + rr + 1, cc:cc + width])
            g = cands[0]
            for k in range(1, N_CHIP):
                g = jnp.where(chip == k, cands[k], g)
            return g

        dms = jnp.sum(dm[...], axis=0, keepdims=True)
        hw = LRU_HEADS * LRU_HEAD_DIM
        grads = [
            ra[ROW_NORM_G:ROW_NORM_G + 2, :],
            None,
            shard(ROW_CONV_W, 4, D // N_CHIP),
            ra[ROW_CONV_B:ROW_CONV_B + 1, :],
            rb[0:hw, :],
            ra[ROW_B_A:ROW_B_A + 1, :],
            rb[hw:2 * hw, :],
            ra[ROW_B_X:ROW_B_X + 1, :],
            ra[ROW_LAMBDA:ROW_LAMBDA + 1, :],
            shard(ROW_SC_W, 3, D // N_CHIP),
            shard(ROW_POOL_B, 2, 2 * D // N_CHIP),
            shard(ROW_POOL_S, 2, 2 * D // N_CHIP),
            ra[ROW_FINAL_G:ROW_FINAL_G + 1, :],
        ]
        for p in range(n):
            w_ref, m_ref, v_ref = wmv[3 * p:3 * p + 3]
            g_out, d_out, m_out, v_out = outs[4 * p:4 * p + 4]
            if grads[p] is None:
                for l in range(2):
                    g = dms[:, l * 3 * D:(l + 1) * 3 * D]
                    dl, m2, v2 = _adam(w_ref[l:l + 1, :], g, m_ref[l:l + 1, :], v_ref[l:l + 1, :])
                    g_out[l:l + 1, :] = g
                    d_out[l:l + 1, :] = dl
                    m_out[l:l + 1, :] = m2
                    v_out[l:l + 1, :] = v2
            else:
                g = grads[p]
                dl, m2, v2 = _adam(w_ref[...], g, m_ref[...], v_ref[...])
                g_out[...] = g
                d_out[...] = dl
                m_out[...] = m2
                v_out[...] = v2

    flat = [a for p in params for a in p]
    return pl.pallas_call(
        body, name="small_adam", in_specs=[VMEM] * (3 + len(flat)), out_specs=[VMEM] * (4 * n),
        out_shape=[jax.ShapeDtypeStruct(p[0].shape, F32) for p in params for _ in range(4)],
        compiler_params=_cp(),
    )(red_a, red_b, dm_all, *flat)


def _modw_adam(ca_t, dm_sh, w, m, v):
    nw = w.shape[2]

    def body(c_ref, d_ref, w_ref, m_ref, v_ref, g_out, d_out, m_out, v_out):
        g = jnp.dot(c_ref[...], d_ref[...], precision=lax.Precision.HIGHEST, preferred_element_type=F32)
        dl, m2, v2 = _adam(w_ref[...], g, m_ref[...], v_ref[...])
        g_out[...] = g
        d_out[...] = dl
        m_out[...] = m2
        v_out[...] = v2

    blk = pl.BlockSpec((None, D, nw), lambda l: (l, 0, 0))
    return pl.pallas_call(
        body, name="modw_adam", grid=(2,),
        in_specs=[pl.BlockSpec((D, SUBLANES), lambda l: (0, 0)), pl.BlockSpec((None, SUBLANES, nw), lambda l: (l, 0, 0)),
                  blk, blk, blk],
        out_specs=[blk] * 4, out_shape=[jax.ShapeDtypeStruct(w.shape, F32)] * 4,
        compiler_params=_cp(("arbitrary",)),
    )(ca_t, dm_sh, w, m, v)


def _half_rows(r):
    return r // 2


def _exchange(copies, name, out_type, n_sems, args, sequencer=None):
    n_in, n_out = len(args), len(out_type)

    def body(*refs):
        if sequencer is not None:
            barrier = pltpu.get_barrier_semaphore()
            peers = sequencer[1](*_pos())
            for peer in peers:
                pl.semaphore_signal(barrier, inc=1, device_id=peer, device_id_type=MESH)
            pl.semaphore_wait(barrier, len(peers))
        copies(refs[:n_in], refs[n_in:n_in + n_out], refs[n_in + n_out], refs[n_in + n_out + 1])

    sems = [pltpu.SemaphoreType.DMA((n_sems,))] * 2
    if sequencer is None:
        return pl.pallas_call(body, name=name, in_specs=[ANY] * n_in, out_specs=[ANY] * n_out, out_shape=out_type,
                              scratch_shapes=sems, compiler_params=_cp())(*args)
    return pl.kernel(body, out_type, mesh=plsc.ScalarSubcoreMesh(axis_name="sequencer", num_cores=1), name=name,
                     scratch_types=sems, compiler_params=pltpu.CompilerParams(collective_id=sequencer[0]))(*args)


def _sibling(x, y, c):
    return [(x, y, 1 - c)]


def _other_chips(x, y, c):
    return [(1 - x, y, c), (x, 1 - y, c), (1 - x, 1 - y, c)]


def _sib_send_halves(gs, name, sequencer_id=None):
    n = len(gs)

    def copies(ins, outs, ssem, rsem):
        x, y, c = _pos()
        cps = []
        for a in range(n):
            hr = _half_rows(gs[a].shape[1])
            cp = pltpu.make_async_remote_copy(
                src_ref=ins[a].at[:, pl.ds(pl.multiple_of((1 - c) * hr, SUBLANES), hr), :], dst_ref=outs[a],
                send_sem=ssem.at[a], recv_sem=rsem.at[a], device_id=(x, y, 1 - c), device_id_type=MESH)
            cp.start()
            cps.append(cp)
        for cp in cps:
            cp.wait()

    out_type = [jax.ShapeDtypeStruct((N_CHIP, _half_rows(g.shape[1]), g.shape[2]), F32) for g in gs]
    return _exchange(copies, name, out_type, n, gs, None if sequencer_id is None else (sequencer_id, _sibling))


def _add_half(g, got, cidx, name, after=None):
    _, hr, cc = got.shape
    rb = min(hr, 256)

    def body(c_ref, g_ref, r_ref, *rest):
        rest[-1][...] = (g_ref[...] + r_ref[...]).astype(rest[-1].dtype)

    order = [] if after is None else [after]
    blk = pl.BlockSpec((None, rb, cc), lambda k, j, c_ref: (k, j, 0))
    return pl.pallas_call(
        body, name=name,
        grid_spec=pltpu.PrefetchScalarGridSpec(
            num_scalar_prefetch=1, grid=(N_CHIP, hr // rb),
            in_specs=[pl.BlockSpec((None, rb, cc), lambda k, j, c_ref: (k, c_ref[0] * (hr // rb) + j, 0)), blk]
            + [ANY] * len(order),
            out_specs=blk),
        out_shape=jax.ShapeDtypeStruct(got.shape, GRAD_WIRE_DTYPE),
        compiler_params=_cp(("parallel", "parallel")),
    )(cidx, g, got, *order)


def _chip_scatter(ps, name, sequencer_id=None):
    n = len(ps)

    def copies(ins, outs, ssem, rsem):
        x, y, c = _pos()
        cps = []
        for a in range(n):
            for q, (fx, fy) in enumerate(((1, 0), (0, 1), (1, 1))):
                px, py = _flip(x, fx), _flip(y, fy)
                cp = pltpu.make_async_remote_copy(
                    src_ref=ins[a].at[2 * px + py], dst_ref=outs[a].at[q],
                    send_sem=ssem.at[3 * a + q], recv_sem=rsem.at[3 * a + q], device_id=(px, py, c), device_id_type=MESH)
                cp.start()
                cps.append(cp)
        for cp in cps:
            cp.wait()

    out_type = [jax.ShapeDtypeStruct((N_CHIP - 1,) + p.shape[1:], p.dtype) for p in ps]
    return _exchange(copies, name, out_type, 3 * n, ps, None if sequencer_id is None else (sequencer_id, _other_chips))


def _add_owner(p, got, chipidx, name, after=None):
    _, hr, cc = p.shape
    rb = min(hr, 256)

    def body(k_ref, p_ref, r_ref, *rest):
        rest[-1][...] = ((p_ref[...].astype(F32) + r_ref[0].astype(F32)) + r_ref[1].astype(F32)) + r_ref[2].astype(F32)

    order = [] if after is None else [after]
    return pl.pallas_call(
        body, name=name,
        grid_spec=pltpu.PrefetchScalarGridSpec(
            num_scalar_prefetch=1, grid=(hr // rb,),
            in_specs=[pl.BlockSpec((None, rb, cc), lambda j, k_ref: (k_ref[0], j, 0)),
                      pl.BlockSpec((N_CHIP - 1, rb, cc), lambda j, k_ref: (0, j, 0))] + [ANY] * len(order),
            out_specs=pl.BlockSpec((rb, cc), lambda j, k_ref: (j, 0))),
        out_shape=jax.ShapeDtypeStruct((hr, cc), F32),
        compiler_params=_cp(("parallel",)),
    )(chipidx, p, got, *order)


def _sib_exchange(ts_, name, sequencer_id=None):
    n = len(ts_)

    def copies(ins, outs, ssem, rsem):
        x, y, c = _pos()
        cps = []
        for a in range(n):
            cp = pltpu.make_async_remote_copy(src_ref=ins[a], dst_ref=outs[a], send_sem=ssem.at[a],
                                              recv_sem=rsem.at[a], device_id=(x, y, 1 - c), device_id_type=MESH)
            cp.start()
            cps.append(cp)
        for cp in cps:
            cp.wait()

    out_type = [jax.ShapeDtypeStruct(t.shape, F32) for t in ts_]
    return _exchange(copies, name, out_type, n, ts_, None if sequencer_id is None else (sequencer_id, _sibling))


def _adam_2d(w, g_own, g_sib, m, v, cidx, name):
    rr, cc = w.shape
    hr = rr // 2
    rb = min(hr, 256)
    nb = hr // rb

    def body(c_ref, w_ref, go_ref, gs_ref, m_ref, v_ref, g_out, d_out, m_out, v_out):
        g = jnp.where(pl.program_id(0) == c_ref[0], go_ref[...], gs_ref[...])
        dl, m2, v2 = _adam(w_ref[...], g, m_ref[...], v_ref[...])
        g_out[...] = g
        d_out[...] = dl
        m_out[...] = m2
        v_out[...] = v2

    blk = pl.BlockSpec((rb, cc), lambda h, j, c_ref: (h * nb + j, 0))
    hblk = pl.BlockSpec((rb, cc), lambda h, j, c_ref: (j, 0))
    return pl.pallas_call(
        body, name=name,
        grid_spec=pltpu.PrefetchScalarGridSpec(
            num_scalar_prefetch=1, grid=(2, nb), in_specs=[blk, hblk, hblk, blk, blk], out_specs=[blk] * 4),
        out_shape=[jax.ShapeDtypeStruct((rr, cc), F32)] * 4, compiler_params=_cp(("parallel", "parallel")),
    )(cidx, w, g_own, g_sib, m, v)


def kernel(x, c, norm_g, mod_w, mod_b, hy_w_in, hy_conv_w, hy_conv_b, lru_w_a, lru_b_a, lru_w_x, lru_b_x, lru_lambda, sc_conv_w, hy_w_out, pool_w_in, pool_w_grp, pool_b_grp, pool_scale, pool_w_out, final_g, loss_target, m_norm_g, m_mod_w, m_mod_b, m_hy_w_in, m_hy_conv_w, m_hy_conv_b, m_lru_w_a, m_lru_b_a, m_lru_w_x, m_lru_b_x, m_lru_lambda, m_sc_conv_w, m_hy_w_out, m_pool_w_in, m_pool_w_grp, m_pool_b_grp, m_pool_scale, m_pool_w_out, m_final_g, v_norm_g, v_mod_w, v_mod_b, v_hy_w_in, v_hy_conv_w, v_hy_conv_b, v_lru_w_a, v_lru_b_a, v_lru_w_x, v_lru_b_x, v_lru_lambda, v_sc_conv_w, v_hy_w_out, v_pool_w_in, v_pool_w_grp, v_pool_b_grp, v_pool_scale, v_pool_w_out, v_final_g):
    ax, ay, ac = _pos()
    me = 4 * ax + 2 * ay + ac
    chip = 2 * ax + ay
    xs = x[0]
    tgt = loss_target[0]
    gd = POOL_GROUP_DIM

    ca_all, mod_all, small_w = _mod_fwd(jnp.broadcast_to(c, (SUBLANES, D)), mod_w, mod_b,
                                        hy_conv_w[0], sc_conv_w[0], pool_b_grp, pool_scale)
    mod_me = lax.dynamic_index_in_dim(mod_all, me, axis=1, keepdims=False)
    sh0, sc0, gt0 = (mod_me[0:1, k * D:(k + 1) * D] for k in range(3))
    sh1, sc1, gt1 = (mod_me[1:2, k * D:(k + 1) * D] for k in range(3))
    cw = small_w[SW_CONV:SW_CONV + 4, 0:D]
    sw = small_w[SW_SC:SW_SC + 3, 0:D]
    pool_b = small_w[SW_POOL_B:SW_POOL_B + 1, :]
    pool_s = small_w[SW_POOL_S:SW_POOL_S + 1, :]
    g0, g1, gf = norm_g[0:1], norm_g[1:2], final_g.reshape(1, D)
    cb, ba, bx, lam = hy_conv_b, lru_b_a, lru_b_x, lru_lambda

    big = [hy_w_in[0], hy_w_out[0], pool_w_in[0], pool_w_grp[0].reshape(4 * 128, gd), pool_w_out[0]]
    cidx = ac.reshape(1).astype(jnp.int32)
    kidx = chip.reshape(1).astype(jnp.int32)
    w_in0, w_out0 = _wgather([_wcast_own_block(w, kidx, f"wcast_own_block_{a}") for a, w in enumerate(big[:2])],
                             "wgather_l0")
    w_in1, w_grp, w_out1 = _wgather_sequencer(
        [_wcast_own_block(w, kidx, f"wcast_own_block_{a + 2}", after=w_out0) for a, w in enumerate(big[2:])], "wgather_l1")
    w_grp =w_grp.reshape(N_CHIP, 4, 128, gd).transpose(1, 0, 2, 3).reshape(4, gd, gd)
    wa_b, wx_b = _wcast([lru_w_a[0], lru_w_x[0]])

    x1, hst, y0, h0, proj0 = _l0_fwd(xs, g0, sc0, sh0, w_in0, gt0, cw, cb, wa_b, ba, wx_b, bx, lam, sw,
                                     w_out0.reshape(2 * D, D))
    h1, proj1 = _norm_proj(x1, g1, sc1, sh1, w_in1, "l1_proj")
    dpool, mixed, y1, dx2, losscols, dgf = _l1_mix(proj1, x1, tgt, gt1, w_grp, pool_b, pool_s,
                                                    w_out1.reshape(2 * D, D), gf)

    def add_halves(grads, got, tag, after):
        return [_add_half(g, r, cidx, f"grad_add_half_{tag}{a}", after) for a, (g, r) in enumerate(zip(grads, got))]

    def add_owners(parts, got, tag, ids, after):
        own = [_add_owner(p, r, kidx, f"grad_add_owner_{tag}{a}", after) for a, (p, r) in enumerate(zip(parts, got))]
        return own, _sib_exchange(own, f"grad_sib_exchange_{tag}", ids[2])

    dproj1, mt1, d_wgrp, dsc1, dbg1 = _l1_bwd_mix(dx2, proj1, mixed, y1, dpool, gt1, w_grp, pool_s,
                                                  w_out1.reshape(2 * D, D))
    d_win1 = _wgrad(h1, dproj1, N_CHIP, D, D, lambda g: 0, lambda g: g, "l1_wgrad_in")
    dx1, s1_1, s2_1 = _dgrad_norm(dproj1, w_in1, x1, dx2, g1, sc1, "l1_bwd_proj")
    d_wout1, dgate1 = _wo_final(mt1, w_out1, gt1, "l1_wo_final")
    d_wgrp = d_wgrp.reshape(4, N_CHIP, 128, gd).transpose(1, 0, 2, 3).reshape(N_CHIP, 4 * 128, gd)
    grads_l1 = [d_win1, d_wgrp, d_wout1]
    got_l1 = _sib_send_halves(grads_l1, "grad_sib_halves_l1", CIDS_L1[0])

    dproj0, mt0, d_wa, d_wx, sm0 = _l0_bwd_mix(dx1, proj0, hst, y0, gt0, cw, cb, wa_b, ba, wx_b, bx, lam, sw,
                                               w_out0.reshape(2 * D, D))
    parts_l1 = add_halves(grads_l1, got_l1, "l1", after=sm0)
    got_l1 = _chip_scatter(parts_l1, "grad_chip_scatter_l1", CIDS_L1[1])
    d_win0 = _wgrad(h0, dproj0, N_CHIP, D, 6 * D // N_CHIP, lambda g: 0, lambda g: g, "l0_wgrad_in")
    d_wout0, dgate0 = _wo_final(mt0, w_out0, gt0, "l0_wo_final")
    halves_l1, sib_l1 = add_owners(parts_l1, got_l1, "l1", CIDS_L1, after=d_win0)
    grads_l0 = [d_win0, d_wout0]
    parts_l0 = add_halves(grads_l0, _sib_send_halves(grads_l0, "grad_sib_halves_l0", CIDS_L0[0]), "l0", after=None)
    got_l0 = _chip_scatter(parts_l0, "grad_chip_scatter_l0", CIDS_L0[1])
    grad_x, s1_0, s2_0 = _dgrad_norm(dproj0, w_in0, xs, dx1, g0, sc0, "l0_bwd_proj", after=parts_l0[0])
    halves_l0, sib_l0 = add_owners(parts_l0, got_l0, "l0", CIDS_L0, after=s1_0)

    buf_a, dmod8, loss8 = _small_pack(s1_0, s2_0, s1_1, s2_1, sm0, dsc1, dbg1, dgf, losscols, dgate0, dgate1,
                                      norm_g, sc0, sc1, lam)
    hw = LRU_HEADS * LRU_HEAD_DIM
    buf_b = jnp.concatenate([d_wa.reshape(hw, LRU_HEAD_DIM), d_wx.reshape(hw, LRU_HEAD_DIM)], axis=0)
    red_a, red_b, dm_all = _small_comm(buf_a, buf_b, dmod8)
    small = [(norm_g, m_norm_g, v_norm_g), (mod_b, m_mod_b, v_mod_b),
             (hy_conv_w[0], m_hy_conv_w[0], v_hy_conv_w[0]), (hy_conv_b, m_hy_conv_b, v_hy_conv_b),
             tuple(a.reshape(hw, LRU_HEAD_DIM) for a in (lru_w_a, m_lru_w_a, v_lru_w_a)),
             (lru_b_a, m_lru_b_a, v_lru_b_a),
             tuple(a.reshape(hw, LRU_HEAD_DIM) for a in (lru_w_x, m_lru_w_x, v_lru_w_x)),
             (lru_b_x, m_lru_b_x, v_lru_b_x), (lru_lambda, m_lru_lambda, v_lru_lambda),
             (sc_conv_w[0], m_sc_conv_w[0], v_sc_conv_w[0]), (pool_b_grp, m_pool_b_grp, v_pool_b_grp),
             (pool_scale, m_pool_scale, v_pool_scale),
             tuple(a.reshape(1, D) for a in (final_g, m_final_g, v_final_g))]
    small_names = ["norm_g", "mod_b", "hy_conv_w", "hy_conv_b", "lru_w_a", "lru_b_a", "lru_w_x", "lru_b_x",
                   "lru_lambda", "sc_conv_w", "pool_b_grp", "pool_scale", "final_g"]
    small_out = _small_adam(red_a, red_b, dm_all, small)
    res = {}
    shapes = dict(norm_g=norm_g, mod_b=mod_b, hy_conv_w=hy_conv_w, hy_conv_b=hy_conv_b, lru_w_a=lru_w_a, lru_b_a=lru_b_a,
                  lru_w_x=lru_w_x, lru_b_x=lru_b_x, lru_lambda=lru_lambda, sc_conv_w=sc_conv_w, pool_b_grp=pool_b_grp,
                  pool_scale=pool_scale, final_g=final_g)
    for p, nm in enumerate(small_names):
        res[nm] = tuple(o.reshape(shapes[nm].shape) for o in small_out[4 * p:4 * p + 4])

    nw = mod_w.shape[2]
    dm_sh = jnp.stack([lax.dynamic_slice_in_dim(dm_all[:, l * 3 * D:(l + 1) * 3 * D], chip * nw, nw, axis=1)
                       for l in range(2)])
    res["mod_w"] = tuple(_modw_adam(ca_all.T, dm_sh, mod_w, m_mod_w, v_mod_w))

    halves = list(halves_l0) + list(halves_l1)
    sib_halves = list(sib_l0) + list(sib_l1)
    big_names = ["hy_w_in", "hy_w_out", "pool_w_in", "pool_w_grp", "pool_w_out"]
    big_wmv = [(hy_w_in, m_hy_w_in, v_hy_w_in), (hy_w_out, m_hy_w_out, v_hy_w_out), (pool_w_in, m_pool_w_in, v_pool_w_in),
               (pool_w_grp, m_pool_w_grp, v_pool_w_grp), (pool_w_out, m_pool_w_out, v_pool_w_out)]
    for a, nm in enumerate(big_names):
        rr, cc = big[a].shape
        w, m, v = (t.reshape(rr, cc) for t in big_wmv[a])
        outs = _adam_2d(w, halves[a], sib_halves[a], m, v, cidx, f"adam_{nm}")
        res[nm] = tuple(o.reshape(big_wmv[a][0].shape) for o in outs)

    loss = lax.psum(loss8[0, 0], ("x", "y", "c"))
    order = ["norm_g", "mod_w", "mod_b", "hy_w_in", "hy_conv_w", "hy_conv_b", "lru_w_a", "lru_b_a", "lru_w_x", "lru_b_x",
             "lru_lambda", "sc_conv_w", "hy_w_out", "pool_w_in", "pool_w_grp", "pool_b_grp", "pool_scale", "pool_w_out",
             "final_g"]
    return (loss, grad_x[None], *[res[nm][0] for nm in order], *[res[nm][1] for nm in order],
            *[res[nm][2] for nm in order], *[res[nm][3] for nm in order])
```

```python
import jax
import jax.numpy as jnp
from jax import lax
from jax.experimental import pallas as pl
from jax.experimental.pallas import tpu as pltpu
from jax.experimental.pallas import tpu_sc as plsc

F32, BF16 = jnp.float32, jnp.bfloat16
D = 1024
RMS_EPS = 1e-6
SQRT_FLOOR = 1e-30
LRU_C = 8.0
LRU_HEADS, LRU_HEAD_DIM = 8, 128
POOL_WINDOWS = (2, 4, 8, 16)
POOL_GROUP_DIM = 512
ADAM_LR, ADAM_B1, ADAM_B2, ADAM_EPS, ADAM_WD, ADAM_STEP = 0.001, 0.9, 0.999, 1e-08, 0.01, 10
MESH = pl.DeviceIdType.MESH
CID_WGATHER = 1
CIDS_L1 = (2, 3, 4)
CIDS_L0 = (5, 6, 7)
N_DEV, N_CHIP = 8, 4
SUBLANES = 8
BF16_ROWS = 16
POOL_HALO = 16
TS_PROJ, TS_MIX, TS_WGRAD, TS_DGRAD = 1024, 256, 1024, 256
SMALL_ROWS = 64
GRAD_WIRE_DTYPE = BF16
ANY = pl.BlockSpec(memory_space=pl.ANY)
VMEM = pl.BlockSpec(memory_space=pltpu.VMEM)
NT = (((1,), (1,)), ((), ()))
TN = (((0,), (0,)), ((), ()))


def _cp(sem=None, vmem_mb=56):
    kw = dict(vmem_limit_bytes=vmem_mb * 2 ** 20)
    if sem is not None:
        kw["dimension_semantics"] = sem
    return pltpu.CompilerParams(**kw)


def _tile(n, t):
    return min(n, t)


def _pos():
    return lax.axis_index("x"), lax.axis_index("y"), lax.axis_index("c")


def _flip(v, f):
    return 1 - v if f else v


def _sigmoid(z):
    return 0.5 * jnp.tanh(0.5 * z) + 0.5


def _rows(n, c):
    return lax.broadcasted_iota(jnp.int32, (n, c), 0)


def _down(a, d):
    return a if d == 0 else pltpu.roll(a, d, 0)


def _up(a, d):
    return a if d == 0 else pltpu.roll(a, a.shape[0] - d, 0)


def _scan_fwd_steps(a, u, carry):
    n, c = a.shape
    sub = _rows(SUBLANES, c)
    out = []
    for k in range(n // SUBLANES):
        p = a[k * SUBLANES:(k + 1) * SUBLANES]
        g = u[k * SUBLANES:(k + 1) * SUBLANES]
        for d in (1, 2, 4):
            keep = sub >= d
            g = g + p * jnp.where(keep, pltpu.roll(g, d, 0), 0.0)
            p = p * jnp.where(keep, pltpu.roll(p, d, 0), 1.0)
        h = g + p * carry
        carry = h[SUBLANES - 1:SUBLANES, :]
        out.append(h)
        yield
    return jnp.concatenate(out, axis=0)


def _scan_rev_steps(alpha, b, carry):
    n, c = alpha.shape
    sub = _rows(SUBLANES, c)
    out = []
    for k in reversed(range(n // SUBLANES)):
        p = alpha[k * SUBLANES:(k + 1) * SUBLANES]
        g = b[k * SUBLANES:(k + 1) * SUBLANES]
        for d in (1, 2, 4):
            keep = sub < SUBLANES - d
            g = g + p * jnp.where(keep, pltpu.roll(g, SUBLANES - d, 0), 0.0)
            p = p * jnp.where(keep, pltpu.roll(p, SUBLANES - d, 0), 1.0)
        h = g + p * carry
        carry = h[0:1, :]
        out.append(h)
        yield
    return jnp.concatenate(out[::-1], axis=0)


def _run(steps):
    while True:
        try:
            next(steps)
        except StopIteration as done:
            return done.value


def _paired(progress, pieces):
    n, done = len(pieces), 1
    pieces[0]()
    for frac in progress:
        while done < n and done <= frac * n:
            pieces[done]()
            done += 1
    while done < n:
        pieces[done]()
        done += 1


def _conv_taps(ext, halo, n, width):
    return [_down(ext, width - 1 - k)[halo:halo + n] for k in range(width)]


def _lru_gates(xc, wa_ref, ba, wx_ref, bx):
    xb = xc.astype(BF16)
    pa, px = [], []
    for h in range(LRU_HEADS):
        xh = xb[:, h * LRU_HEAD_DIM:(h + 1) * LRU_HEAD_DIM]
        pa.append(jnp.dot(xh, wa_ref[h], preferred_element_type=F32))
        px.append(jnp.dot(xh, wx_ref[h], preferred_element_type=F32))
    r = _sigmoid(jnp.concatenate(pa, axis=1) + ba)
    ig = _sigmoid(jnp.concatenate(px, axis=1) + bx)
    return r, ig


def _softplus_neg(lam):
    return jnp.maximum(-lam, 0.0) + jnp.log1p(jnp.exp(-jnp.abs(lam)))


def _recip_1_to_2(d):
    r0 = pl.reciprocal(d, approx=True)
    return r0 * (2.0 - d * r0)


def _lru_decay(r, sp, first):
    big_l = (-LRU_C) * r * sp
    a = jnp.exp(big_l)
    th = jnp.tanh(big_l)
    q = (-2.0 * th) * _recip_1_to_2(1.0 - th)
    rs = lax.rsqrt(jnp.maximum(q, SQRT_FLOOR))
    return a, jnp.where(first, 1.0, q * rs), rs


def _pool_inv_counts(t0, n):
    t = (t0 + lax.broadcasted_iota(jnp.int32, (n, 1), 0) + 1).astype(F32)
    return [1.0 / jnp.minimum(t, float(w)) for w in POOL_WINDOWS]


def _window_sums(ext, shift):
    gd = POOL_GROUP_DIM
    out = []
    s = ext
    for k in range(len(POOL_WINDOWS)):
        s = s + shift(s, 2 ** k)
        out.append(s[:, 0:gd])
        if k + 1 < len(POOL_WINDOWS):
            s = s[:, gd:]
    return out


SW_ROWS, SW_COLS = 16, 2 * D
SW_CONV, SW_SC, SW_POOL_B, SW_POOL_S = 0, 4, 8, 9


def _mod_fwd(c8, mod_w, mod_b, conv_w, sc_w, pool_b, pool_s):
    nw = mod_w.shape[2]
    cq, pq = conv_w.shape[1], pool_b.shape[1]

    def body(c_ref, w_ref, b_ref, cw_ref, sw_ref, pb_ref, ps_ref, ca_ref, mod_ref, small_ref,
             cslot, mslot, msend, pslot, psend, s1, r1, s2, r2, s3, r3):
        x, y, c = _pos()
        me = 4 * x + 2 * y + c
        chip = 2 * x + y
        first = []
        for r in range(1, N_DEV):
            fx, fy, fc = (r >> 2) & 1, (r >> 1) & 1, r & 1
            cp = pltpu.make_async_remote_copy(
                src_ref=c_ref, dst_ref=cslot.at[me], send_sem=s1.at[r - 1], recv_sem=r1.at[r - 1],
                device_id=(_flip(x, fx), _flip(y, fy), _flip(c, fc)), device_id_type=MESH)
            cp.start()
            first.append(cp)
        cslot[me] = c_ref[...]
        for cp in first:
            cp.wait()
        rows = _rows(SUBLANES, D)
        call = jnp.zeros((SUBLANES, D), F32)
        for d in range(N_DEV):
            call = jnp.where(rows == d, cslot[d], call)
        ca = call * _sigmoid(call)
        ca_ref[...] = ca
        for l in range(2):
            msend[l] = jnp.dot(ca, w_ref[l], precision=lax.Precision.HIGHEST, preferred_element_type=F32)
        psend[...] = jnp.zeros_like(psend)
        psend[SW_CONV:SW_CONV + 4, 0:cq] = cw_ref[...]
        psend[SW_SC:SW_SC + 3, 0:cq] = sw_ref[...]
        psend[SW_POOL_B:SW_POOL_B + 1, :] = pb_ref[...]
        psend[SW_POOL_S:SW_POOL_S + 1, :] = ps_ref[...]
        second = []
        for q, (fx, fy) in enumerate(((1, 0), (0, 1), (1, 1))):
            peer = (_flip(x, fx), _flip(y, fy), c)
            for src, dst, ss, rs in ((msend, mslot, s2, r2), (psend, pslot, s3, r3)):
                cp = pltpu.make_async_remote_copy(src_ref=src, dst_ref=dst.at[chip], send_sem=ss.at[q], recv_sem=rs.at[q],
                                                  device_id=peer, device_id_type=MESH)
                cp.start()
                second.append(cp)
        mslot[chip] = msend[...]
        pslot[chip] = psend[...]
        for cp in second:
            cp.wait()
        small_ref[...] = jnp.zeros_like(small_ref)
        for j in range(N_CHIP):
            for l in range(2):
                mod_ref[l, :, j * nw:(j + 1) * nw] = mslot[j, l] + b_ref[l:l + 1, j * nw:(j + 1) * nw]
            small_ref[0:SUBLANES, j * cq:(j + 1) * cq] = pslot[j, 0:SUBLANES, 0:cq]
            small_ref[SUBLANES:SW_ROWS, j * pq:(j + 1) * pq] = pslot[j, SUBLANES:SW_ROWS, :]

    args = (c8, mod_w, mod_b, conv_w, sc_w, pool_b, pool_s)
    dma3 = pltpu.SemaphoreType.DMA((N_CHIP - 1,))
    return pl.pallas_call(
        body, name="mod_fwd",
        in_specs=[VMEM] * len(args), out_specs=[VMEM] * 3,
        out_shape=[jax.ShapeDtypeStruct((SUBLANES, D), F32), jax.ShapeDtypeStruct((2, SUBLANES, N_CHIP * nw), F32),
                   jax.ShapeDtypeStruct((SW_ROWS, SW_COLS), F32)],
        scratch_shapes=[pltpu.VMEM((N_DEV, SUBLANES, D), F32), pltpu.VMEM((N_CHIP, 2, SUBLANES, nw), F32),
                        pltpu.VMEM((2, SUBLANES, nw), F32), pltpu.VMEM((N_CHIP, SW_ROWS, pq), F32),
                        pltpu.VMEM((SW_ROWS, pq), F32),
                        pltpu.SemaphoreType.DMA((N_DEV - 1,)), pltpu.SemaphoreType.DMA((N_DEV - 1,)),
                        dma3, dma3, dma3, dma3],
        compiler_params=_cp(),
    )(*args)


def _wcast(ws):
    def body(*refs):
        n = len(refs) // 2
        for a in range(n):
            refs[n + a][...] = refs[a][...].astype(BF16)

    return pl.pallas_call(
        body, name="wcast", in_specs=[VMEM] * len(ws), out_specs=[VMEM] * len(ws),
        out_shape=[jax.ShapeDtypeStruct(w.shape, BF16) for w in ws], compiler_params=_cp(),
    )(*ws)


def _wcast_own_block(w, kidx, name, after=None):
    rr, cc = w.shape
    rb = min(rr, 256)

    def body(k_ref, w_ref, *rest):
        rest[-1][...] = w_ref[...].astype(BF16)

    order = [] if after is None else [after]
    return pl.pallas_call(
        body, name=name,
        grid_spec=pltpu.PrefetchScalarGridSpec(
            num_scalar_prefetch=1, grid=(rr // rb,),
            in_specs=[pl.BlockSpec((rb, cc), lambda j, k_ref: (j, 0))] + [ANY] * len(order),
            out_specs=pl.BlockSpec((None, rb, cc), lambda j, k_ref: (k_ref[0], j, 0))),
        out_shape=jax.ShapeDtypeStruct((N_CHIP, rr, cc), BF16),
        compiler_params=_cp(("parallel",)),
    )(kidx, w, *order)


def _wgather_copies(outs, rows, ssem, rsem, fssem, frsem):
    n = len(outs)
    x, y, c = _pos()
    chip = 2 * x + y
    sib = (x, y, 1 - c)
    flips = ((1, 0), (0, 1), (1, 1))

    def half(a, which):
        hr = rows[a] // 2
        return pl.ds(pl.multiple_of(which * hr, BF16_ROWS), hr)

    sends = []
    for a in range(n):
        mine = outs[a].at[chip, half(a, c), :]
        for q, (fx, fy) in enumerate(flips):
            cp = pltpu.make_async_remote_copy(
                src_ref=mine, dst_ref=mine, send_sem=ssem.at[3 * a + q], recv_sem=rsem.at[3 * a + q],
                device_id=(_flip(x, fx), _flip(y, fy), c), device_id_type=MESH)
            cp.start()
            sends.append(cp)
    passed = []
    for a in range(n):
        for q, (fx, fy) in enumerate(flips):
            src_chip = 2 * _flip(x, fx) + _flip(y, fy)
            landed = outs[a].at[src_chip, half(a, c), :]
            pltpu.make_async_remote_copy(
                src_ref=landed, dst_ref=landed, send_sem=ssem.at[3 * a + q], recv_sem=rsem.at[3 * a + q],
                device_id=sib, device_id_type=MESH).wait_recv()
            cp = pltpu.make_async_remote_copy(
                src_ref=landed, dst_ref=landed, send_sem=fssem.at[3 * a + q], recv_sem=frsem.at[3 * a + q],
                device_id=sib, device_id_type=MESH)
            cp.start()
            passed.append(cp)
    for a in range(n):
        for q, (fx, fy) in enumerate(flips):
            src_chip = 2 * _flip(x, fx) + _flip(y, fy)
            other = outs[a].at[src_chip, half(a, 1 - c), :]
            pltpu.make_async_remote_copy(
                src_ref=other, dst_ref=other, send_sem=fssem.at[3 * a + q], recv_sem=frsem.at[3 * a + q],
                device_id=sib, device_id_type=MESH).wait_recv()
    for cp in sends + passed:
        cp.wait_send()


def _wgather(bufs, name):
    n = len(bufs)

    def body(*refs):
        _wgather_copies(refs[n:2 * n], [b.shape[1] for b in bufs], *refs[2 * n:])

    return pl.pallas_call(
        body, name=name, in_specs=[ANY] * n, out_specs=[ANY] * n,
        out_shape=[jax.ShapeDtypeStruct(b.shape, BF16) for b in bufs],
        input_output_aliases={a: a for a in range(n)},
        scratch_shapes=[pltpu.SemaphoreType.DMA((3 * n,))] * 4,
        compiler_params=_cp(),
    )(*bufs)


def _wgather_sequencer(bufs, name):
    n = len(bufs)
    refs = [jax.new_ref(b, memory_space=pltpu.MemorySpace.HBM) for b in bufs]
    dma = pltpu.SemaphoreType.DMA((3 * n,))

    @pl.kernel(mesh=plsc.ScalarSubcoreMesh(axis_name="sequencer", num_cores=1), name=name,
               scratch_types=(dma, dma, dma, dma), compiler_params=pltpu.CompilerParams(collective_id=CID_WGATHER))
    def launch(ssem, rsem, fssem, frsem):
        x, y, c = _pos()
        barrier = pltpu.get_barrier_semaphore()
        for peer in ((1 - x, y, c), (x, 1 - y, c), (1 - x, 1 - y, c), (x, y, 1 - c)):
            pl.semaphore_signal(barrier, inc=1, device_id=peer, device_id_type=MESH)
        pl.semaphore_wait(barrier, 4)
        _wgather_copies(refs, [b.shape[1] for b in bufs], ssem, rsem, fssem, frsem)

    launch()
    return [r[...] for r in refs]


def _norm_proj(x, g, sc, sh, w, name):
    s_len, nb = x.shape[0], w.shape[2]
    ts = _tile(s_len, TS_PROJ)

    def body(x_ref, g_ref, sc_ref, sh_ref, w_ref, h_ref, p_ref):
        @pl.when(pl.program_id(1) == 0)
        def _():
            xv = x_ref[...]
            r = lax.rsqrt(jnp.mean(xv * xv, axis=-1, keepdims=True) + RMS_EPS)
            h_ref[...] = (xv * r * (g_ref[...] * (1.0 + sc_ref[...])) + sh_ref[...]).astype(BF16)

        p_ref[...] = jnp.dot(h_ref[...], w_ref[...], preferred_element_type=F32).astype(BF16)

    vec = pl.BlockSpec((1, D), lambda i, j: (0, 0))
    return pl.pallas_call(
        body, name=name, grid=(s_len // ts, N_CHIP),
        in_specs=[pl.BlockSpec((ts, D), lambda i, j: (i, 0)), vec, vec, vec,
                  pl.BlockSpec((None, D, nb), lambda i, j: (j, 0, 0))],
        out_specs=[pl.BlockSpec((ts, D), lambda i, j: (i, 0)), pl.BlockSpec((ts, nb), lambda i, j: (i, j))],
        out_shape=[jax.ShapeDtypeStruct((s_len, D), BF16), jax.ShapeDtypeStruct((s_len, N_CHIP * nb), BF16)],
        compiler_params=_cp(("parallel", "arbitrary")),
    )(x, g, sc, sh, w)


def _l0_fwd(x, g, sc, sh, w_in, gate, cw, cb, wa, ba, wx, bx, lam, sw, wo):
    s_len, nb = x.shape[0], w_in.shape[2]
    ts = _tile(s_len, TS_MIX)
    n_t = s_len // ts
    hl = SUBLANES

    def body(xa_ref, xb_ref, g_ref, sc_ref, sh_ref, win_ref, gate_ref, cw_ref, cb_ref, wa_ref, ba_ref, wx_ref, bx_ref,
             lam_ref, sw_ref, wo_ref, x1_ref, h_ref, y_ref, h0_ref, p_ref, pcur, pnext, cxa, czz, chh):
        i = pl.program_id(0)

        @pl.when(i == 0)
        def _():
            cxa[...] = jnp.zeros_like(cxa)
            czz[...] = jnp.zeros_like(czz)
            chh[...] = jnp.zeros_like(chh)
            pnext[...] = jnp.zeros_like(pnext)

        pcur[...] = pnext[...]
        xv = xa_ref[...]
        rinv = lax.rsqrt(jnp.mean(xv * xv, axis=-1, keepdims=True) + RMS_EPS)
        h0 = (xv * rinv * (g_ref[...] * (1.0 + sc_ref[...])) + sh_ref[...]).astype(BF16)
        h0_ref[...] = h0

        def project(k):
            def emit():
                pk = jnp.dot(h0, win_ref[k], preferred_element_type=F32).astype(BF16)
                p_ref[:, k * nb:(k + 1) * nb] = pk
                pnext[:, k * nb:(k + 1) * nb] = pk
            return emit

        def mixer():
            piece = lambda k: pcur[:, k * D:(k + 1) * D].astype(F32)
            xa = piece(0)
            rows = _rows(ts, D)
            taps = _conv_taps(jnp.concatenate([cxa[...], xa], axis=0), hl, ts, 4)
            xc = cb_ref[...] + sum(cw_ref[k:k + 1, :] * taps[k] for k in range(4))
            r, ig = _lru_gates(xc, wa_ref, ba_ref[...], wx_ref, bx_ref[...])
            a, m, _ = _lru_decay(r, _softplus_neg(lam_ref[...]), (rows == 0) & (i == 1))
            yield 0.26
            h = _run(_scan_fwd_steps(a, m * ig * xc, chh[hl - 1:hl, :]))
            yield 0.51
            gcp, v = piece(3), piece(4)
            z = gcp * v
            ztaps = _conv_taps(jnp.concatenate([czz[...], z], axis=0), hl, ts, 3)
            yb = piece(2) * sum(sw_ref[k:k + 1, :] * ztaps[k] for k in range(3))
            ga, gb = piece(1), piece(5)
            y = jnp.concatenate([h * (ga * _sigmoid(ga)), yb * (gb * _sigmoid(gb))], axis=1).astype(BF16)
            yield 0.76
            y_ref[...] = y
            x1_ref[...] = xb_ref[...] + gate_ref[...] * jnp.dot(y, wo_ref[...], preferred_element_type=F32)
            h_ref[...] = h.astype(BF16)
            cxa[...] = xa[ts - hl:, :]
            czz[...] = z[ts - hl:, :]
            chh[...] = jnp.where(i > 0, h[ts - hl:, :], 0.0)

        _paired(mixer(), [project(k) for k in range(N_CHIP)])

    def full(a):
        return pl.BlockSpec(a.shape, lambda i: (0,) * a.ndim)

    ahead = lambda w: pl.BlockSpec((ts, w), lambda i: (jnp.minimum(i, n_t - 1), 0))
    behind = lambda w: pl.BlockSpec((ts, w), lambda i: (jnp.maximum(i - 1, 0), 0))
    args = (x, x, g, sc, sh, w_in, gate, cw, cb, wa, ba, wx, bx, lam, sw, wo)
    return pl.pallas_call(
        body, name="l0_fwd", grid=(n_t + 1,),
        in_specs=[ahead(D), behind(D)] + [full(a) for a in args[2:]],
        out_specs=[behind(D), behind(D), behind(2 * D), ahead(D), ahead(N_CHIP * nb)],
        out_shape=[jax.ShapeDtypeStruct((s_len, D), F32), jax.ShapeDtypeStruct((s_len, D), BF16),
                   jax.ShapeDtypeStruct((s_len, 2 * D), BF16), jax.ShapeDtypeStruct((s_len, D), BF16),
                   jax.ShapeDtypeStruct((s_len, N_CHIP * nb), BF16)],
        scratch_shapes=[pltpu.VMEM((ts, N_CHIP * nb), BF16)] * 2 + [pltpu.VMEM((hl, D), F32)] * 3,
        compiler_params=_cp(("arbitrary",)),
    )(*args)


def _l1_mix(proj, x1, tgt, gate, wg, bg, scale, wo, gf):
    s_len = x1.shape[0]
    ts = _tile(s_len, TS_MIX)
    pw, gd, hl = 2 * D, POOL_GROUP_DIM, POOL_HALO

    def body(p_ref, x_ref, t_ref, gate_ref, wg_ref, bg_ref, sc_ref, wo_ref, gf_ref,
             d_ref, mx_ref, y_ref, dx_ref, loss_ref, dgf_ref, cv):
        i = pl.program_id(0)

        @pl.when(i == 0)
        def _():
            cv[...] = jnp.zeros_like(cv)
            loss_ref[...] = jnp.zeros_like(loss_ref)
            dgf_ref[...] = jnp.zeros_like(dgf_ref)

        v = p_ref[:, 0:pw].astype(F32)
        gg = p_ref[:, pw:2 * pw].astype(F32)
        sums = _window_sums(jnp.concatenate([cv[...], v], axis=0), _down)
        inv = _pool_inv_counts(i * ts, ts)
        dd = [sums[k][hl:hl + ts] * inv[k] - v[:, k * gd:(k + 1) * gd] for k in range(4)]
        mixed = jnp.concatenate(
            [jnp.dot(dd[k].astype(BF16), wg_ref[k], preferred_element_type=F32) for k in range(4)], axis=1) + bg_ref[...]
        d_ref[...] = jnp.concatenate(dd, axis=1).astype(BF16)
        mx_ref[...] = mixed.astype(BF16)
        y = (mixed * sc_ref[...] * (gg * _sigmoid(gg))).astype(BF16)
        y_ref[...] = y
        x2 = x_ref[...] + gate_ref[...] * jnp.dot(y, wo_ref[...], preferred_element_type=F32)
        r2 = lax.rsqrt(jnp.mean(x2 * x2, axis=-1, keepdims=True) + RMS_EPS)
        n2 = x2 * r2
        err = n2 * gf_ref[...] - t_ref[...]
        loss_ref[...] += jnp.sum(err * err, axis=0, keepdims=True)
        dyf = err * (1.0 / D)
        dgf_ref[...] += jnp.sum(dyf * n2, axis=0, keepdims=True)
        dn = dyf * gf_ref[...]
        dx_ref[...] = r2 * (dn - n2 * jnp.mean(dn * n2, axis=-1, keepdims=True))
        cv[...] = v[ts - hl:, :]

    def full(a):
        return pl.BlockSpec(a.shape, lambda i: (0,) * a.ndim)

    row = lambda w: pl.BlockSpec((ts, w), lambda i: (i, 0))
    acc = pl.BlockSpec((1, D), lambda i: (0, 0))
    return pl.pallas_call(
        body, name="l1_mix", grid=(s_len // ts,),
        in_specs=[row(2 * pw), row(D), row(D)] + [full(a) for a in (gate, wg, bg, scale, wo, gf)],
        out_specs=[row(pw), row(pw), row(pw), row(D), acc, acc],
        out_shape=[jax.ShapeDtypeStruct((s_len, pw), BF16)] * 3 + [jax.ShapeDtypeStruct((s_len, D), F32)]
        + [jax.ShapeDtypeStruct((1, D), F32)] * 2,
        scratch_shapes=[pltpu.VMEM((hl, pw), F32)],
        compiler_params=_cp(("arbitrary",)),
    )(proj, x1, tgt, gate, wg, bg, scale, wo, gf)


def _l1_bwd_mix(dx2, proj, mixed, y, dpool, gate, wg, scale, wo):
    s_len = dx2.shape[0]
    ts = _tile(s_len, TS_MIX)
    n_t = s_len // ts
    pw, gd, hl = 2 * D, POOL_GROUP_DIM, POOL_HALO

    def body(dx_ref, gg_ref, mx_ref, y_ref, d_ref, gate_ref, wg_ref, sc_ref, wo_ref,
             dp_ref, mt_ref, dwg_ref, dsc_ref, dbg_ref, cq):
        i = pl.program_id(0)

        @pl.when(i == 0)
        def _():
            cq[...] = jnp.zeros_like(cq)
            dsc_ref[...] = jnp.zeros_like(dsc_ref)
            dbg_ref[...] = jnp.zeros_like(dbg_ref)
            mt_ref[...] = jnp.zeros_like(mt_ref)
            dwg_ref[...] = jnp.zeros_like(dwg_ref)

        dxv = dx_ref[...]
        dxb = dxv.astype(BF16)

        def wgrad_out(k):
            mt_ref[k] += lax.dot_general(y_ref[:, k * gd:(k + 1) * gd], dxb, TN, preferred_element_type=F32)

        dy = lax.dot_general((gate_ref[...] * dxv).astype(BF16), wo_ref[...], NT, preferred_element_type=F32)
        wgrad_out(0)
        gg = gg_ref[...].astype(F32)
        mixed = mx_ref[...].astype(F32)
        s = _sigmoid(gg)
        sg = gg * s
        dmixed = dy * sc_ref[...] * sg
        dsc_ref[...] += jnp.sum(dy * mixed * sg, axis=0, keepdims=True)
        dbg_ref[...] += jnp.sum(dmixed, axis=0, keepdims=True)
        dmb = dmixed.astype(BF16)
        wgrad_out(1)
        dp_ref[:, pw:2 * pw] = (dy * sc_ref[...] * mixed * (s * (1.0 + gg * (1.0 - s)))).astype(BF16)
        inv = _pool_inv_counts((n_t - 1 - i) * ts, ts)
        dd = []
        for k in range(4):
            dmk = dmb[:, k * gd:(k + 1) * gd]
            dd.append(lax.dot_general(dmk, wg_ref[k], NT, preferred_element_type=F32))
            dwg_ref[k] += lax.dot_general(d_ref[:, k * gd:(k + 1) * gd], dmk, TN, preferred_element_type=F32)
        wgrad_out(2)
        q = jnp.concatenate([dd[k] * inv[k] for k in range(4)], axis=1)
        sums = _window_sums(jnp.concatenate([q, cq[...]], axis=0), _up)
        wgrad_out(3)
        dp_ref[:, 0:pw] = jnp.concatenate([sums[k][0:ts] - dd[k] for k in range(4)], axis=1).astype(BF16)
        cq[...] = q[0:hl, :]

    def full(a):
        return pl.BlockSpec(a.shape, lambda i: (0,) * a.ndim)

    rev = lambda w, j=0: pl.BlockSpec((ts, w), lambda i: (n_t - 1 - i, j))
    acc = pl.BlockSpec((1, pw), lambda i: (0, 0))
    return pl.pallas_call(
        body, name="l1_bwd_mix", grid=(n_t,),
        in_specs=[rev(D), rev(pw, 1), rev(pw), rev(pw), rev(pw)] + [full(a) for a in (gate, wg, scale, wo)],
        out_specs=[rev(2 * pw), pl.BlockSpec((N_CHIP, gd, D), lambda i: (0, 0, 0)),
                   pl.BlockSpec((4, gd, gd), lambda i: (0, 0, 0)), acc, acc],
        out_shape=[jax.ShapeDtypeStruct((s_len, 2 * pw), BF16), jax.ShapeDtypeStruct((N_CHIP, gd, D), F32),
                   jax.ShapeDtypeStruct((4, gd, gd), F32),
                   jax.ShapeDtypeStruct((1, pw), F32), jax.ShapeDtypeStruct((1, pw), F32)],
        scratch_shapes=[pltpu.VMEM((hl, pw), F32)],
        compiler_params=_cp(("arbitrary",)),
    )(dx2, proj, mixed, y, dpool, gate, wg, scale, wo)


def _l0_bwd_mix(dx1, proj, hst, y, gate, cw, cb, wa, ba, wx, bx, lam, sw, wo):
    s_len = dx1.shape[0]
    ts = _tile(s_len, TS_MIX)
    n_t = s_len // ts
    hl, hb = SUBLANES, BF16_ROWS
    yb_w = 2 * D // N_CHIP

    def body(dx_ref, p_ref, ph_ref, h_ref, hh_ref, y_ref, gate_ref, cw_ref, cb_ref, wa_ref, ba_ref, wx_ref, bx_ref,
             lam_ref, sw_ref, wo_ref, dp_ref, mt_ref, dwa_ref, dwx_ref, sm_ref, cg, cdxc, cdcz, ca):
        i = pl.program_id(0)
        ri = n_t - 1 - i

        @pl.when(i == 0)
        def _():
            cg[...] = jnp.zeros_like(cg)
            ca[...] = jnp.zeros_like(ca)
            cdxc[...] = jnp.zeros_like(cdxc)
            cdcz[...] = jnp.zeros_like(cdcz)
            sm_ref[...] = jnp.zeros_like(sm_ref)
            mt_ref[...] = jnp.zeros_like(mt_ref)
            dwa_ref[...] = jnp.zeros_like(dwa_ref)
            dwx_ref[...] = jnp.zeros_like(dwx_ref)

        dxb = dx_ref[...].astype(BF16)

        def wgrad_out(k):
            mt_ref[k] += lax.dot_general(y_ref[:, k * yb_w:(k + 1) * yb_w], dxb, TN, preferred_element_type=F32)

        wgrad_out(0)
        has_prev = (ri > 0).astype(F32)
        xa, ga, gbp, gcp, v, gb = [p_ref[:, k * D:(k + 1) * D].astype(F32) for k in range(6)]
        prev = lambda k: ph_ref[:, k * D:(k + 1) * D].astype(F32)[hb - hl:hb] * has_prev
        rows = _rows(ts, D)
        first = (rows == 0) & (ri == 0)
        xtaps = _conv_taps(jnp.concatenate([prev(0), xa], axis=0), hl, ts, 4)
        xc = cb_ref[...] + sum(cw_ref[k:k + 1, :] * xtaps[k] for k in range(4))
        r, ig = _lru_gates(xc, wa_ref, ba_ref[...], wx_ref, bx_ref[...])
        sp = _softplus_neg(lam_ref[...])
        a, m, inv_m = _lru_decay(r, sp, first)
        z = gcp * v
        ztaps = _conv_taps(jnp.concatenate([prev(3) * prev(4), z], axis=0), hl, ts, 3)
        cz = sum(sw_ref[k:k + 1, :] * ztaps[k] for k in range(3))
        h = h_ref[...].astype(F32)
        hprev = _down(jnp.concatenate([hh_ref[...].astype(F32)[hb - hl:hb] * has_prev, h], axis=0), 1)[hl:hl + ts]
        dy = lax.dot_general((gate_ref[...] * dx_ref[...]).astype(BF16), wo_ref[...], NT, preferred_element_type=F32)
        dya_pre, dyb_pre = dy[:, 0:D], dy[:, D:2 * D]
        s_a, s_b = _sigmoid(ga), _sigmoid(gb)
        dp_ref[:, D:2 * D] = (dya_pre * h * (s_a * (1.0 + ga * (1.0 - s_a)))).astype(BF16)
        dp_ref[:, 5 * D:6 * D] = (dyb_pre * (gbp * cz) * (s_b * (1.0 + gb * (1.0 - s_b)))).astype(BF16)
        dya = dya_pre * (ga * s_a)
        dyb = dyb_pre * (gb * s_b)
        wgrad_out(1)
        dp_ref[:, 2 * D:3 * D] = (dyb * cz).astype(BF16)
        dcz = dyb * gbp
        for k in range(3):
            sm_ref[8 + k:9 + k, :] += jnp.sum(dcz * ztaps[k], axis=0, keepdims=True)
        dcz_ext = jnp.concatenate([dcz, cdcz[...]], axis=0)
        dz = sum(sw_ref[k:k + 1, :] * _up(dcz_ext, 2 - k)[0:ts] for k in range(3))
        dp_ref[:, 3 * D:4 * D] = (dz * v).astype(BF16)
        dp_ref[:, 4 * D:5 * D] = (dz * gcp).astype(BF16)
        cdcz[...] = dcz[0:hl, :]
        alpha = _up(jnp.concatenate([a, ca[...]], axis=0), 1)[0:ts]
        wgrad_out(2)
        dh = _run(_scan_rev_steps(alpha, dya, cg[0:1, :]))
        wgrad_out(3)
        cg[...] = dh[0:hl, :]
        ca[...] = a[0:hl, :]
        da = dh * hprev
        dm = dh * ig * xc
        di = dh * m * xc
        dxc = dh * m * ig
        dl = da * a - jnp.where(first, 0.0, dm * (a * a) * inv_m)
        sm_ref[7:8, :] += jnp.sum(dl * r, axis=0, keepdims=True) * (-LRU_C)
        dpa = (dl * sp) * (-LRU_C) * r * (1.0 - r)
        dpx = di * ig * (1.0 - ig)
        sm_ref[5:6, :] += jnp.sum(dpa, axis=0, keepdims=True)
        sm_ref[6:7, :] += jnp.sum(dpx, axis=0, keepdims=True)
        dpa_b, dpx_b, xc_b = dpa.astype(BF16), dpx.astype(BF16), xc.astype(BF16)
        back = []
        for hd in range(LRU_HEADS):
            sl = slice(hd * LRU_HEAD_DIM, (hd + 1) * LRU_HEAD_DIM)
            back.append(lax.dot_general(dpa_b[:, sl], wa_ref[hd], NT, preferred_element_type=F32)
                        + lax.dot_general(dpx_b[:, sl], wx_ref[hd], NT, preferred_element_type=F32))
            dwa_ref[hd] += lax.dot_general(xc_b[:, sl], dpa_b[:, sl], TN, preferred_element_type=F32)
            dwx_ref[hd] += lax.dot_general(xc_b[:, sl], dpx_b[:, sl], TN, preferred_element_type=F32)
        dxc = dxc + jnp.concatenate(back, axis=1)
        sm_ref[4:5, :] += jnp.sum(dxc, axis=0, keepdims=True)
        for k in range(4):
            sm_ref[k:k + 1, :] += jnp.sum(dxc * xtaps[k], axis=0, keepdims=True)
        dxc_ext = jnp.concatenate([dxc, cdxc[...]], axis=0)
        dp_ref[:, 0:D] = sum(cw_ref[k:k + 1, :] * _up(dxc_ext, 3 - k)[0:ts] for k in range(4)).astype(BF16)
        cdxc[...] = dxc[0:hl, :]

    def full(a):
        return pl.BlockSpec(a.shape, lambda i: (0,) * a.ndim)

    rev = lambda w: pl.BlockSpec((ts, w), lambda i: (n_t - 1 - i, 0))
    halo = lambda w: pl.BlockSpec((hb, w), lambda i: (jnp.maximum((n_t - 1 - i) * (ts // hb) - 1, 0), 0))
    return pl.pallas_call(
        body, name="l0_bwd_mix", grid=(n_t,),
        in_specs=[rev(D), rev(6 * D), halo(6 * D), rev(D), halo(D), rev(2 * D)]
        + [full(a) for a in (gate, cw, cb, wa, ba, wx, bx, lam, sw, wo)],
        out_specs=[rev(6 * D), pl.BlockSpec((N_CHIP, yb_w, D), lambda i: (0, 0, 0)),
                   pl.BlockSpec(wa.shape, lambda i: (0, 0, 0)), pl.BlockSpec(wa.shape, lambda i: (0, 0, 0)),
                   pl.BlockSpec((2 * SUBLANES, D), lambda i: (0, 0))],
        out_shape=[jax.ShapeDtypeStruct((s_len, 6 * D), BF16), jax.ShapeDtypeStruct((N_CHIP, yb_w, D), F32),
                   jax.ShapeDtypeStruct(wa.shape, F32), jax.ShapeDtypeStruct(wa.shape, F32),
                   jax.ShapeDtypeStruct((2 * SUBLANES, D), F32)],
        scratch_shapes=[pltpu.VMEM((hl, D), F32)] * 4,
        compiler_params=_cp(("arbitrary",)),
    )(dx1, proj, proj, hst, hst, y, gate, cw, cb, wa, ba, wx, bx, lam, sw, wo)


def _dgrad_norm(dproj, w, x, dres, g, sc, name, after=None):
    s_len, nb = x.shape[0], w.shape[2]
    ts = _tile(s_len, TS_DGRAD)
    order = [] if after is None else [after]

    def body(dp_ref, w_ref, x_ref, dr_ref, g_ref, sc_ref, *rest):
        dx_ref, s1_ref, s2_ref = rest[len(order):]

        @pl.when(pl.program_id(0) == 0)
        def _():
            s1_ref[...] = jnp.zeros_like(s1_ref)
            s2_ref[...] = jnp.zeros_like(s2_ref)

        dh = sum(lax.dot_general(dp_ref[:, k * nb:(k + 1) * nb], w_ref[k], NT, preferred_element_type=F32)
                 for k in range(N_CHIP))
        xv = x_ref[...]
        r = lax.rsqrt(jnp.mean(xv * xv, axis=-1, keepdims=True) + RMS_EPS)
        n = xv * r
        s1_ref[...] += jnp.sum(dh, axis=0, keepdims=True)
        s2_ref[...] += jnp.sum(dh * n, axis=0, keepdims=True)
        dn = dh * (g_ref[...] * (1.0 + sc_ref[...]))
        dx_ref[...] = dr_ref[...] + r * (dn - n * jnp.mean(dn * n, axis=-1, keepdims=True))

    row = lambda wd: pl.BlockSpec((ts, wd), lambda i: (i, 0))
    vec = pl.BlockSpec((1, D), lambda i: (0, 0))
    return pl.pallas_call(
        body, name=name, grid=(s_len // ts,),
        in_specs=[row(N_CHIP * nb), pl.BlockSpec(w.shape, lambda i: (0, 0, 0)), row(D), row(D), vec, vec]
        + [ANY] * len(order),
        out_specs=[row(D), vec, vec],
        out_shape=[jax.ShapeDtypeStruct((s_len, D), F32)] + [jax.ShapeDtypeStruct((1, D), F32)] * 2,
        compiler_params=_cp(("arbitrary",)),
    )(dproj, w, x, dres, g, sc, *order)


def _wgrad(a, b, groups, ka, nb, a_col, b_col, name, after=None):
    s_len = a.shape[0]
    ts = _tile(s_len, TS_WGRAD)
    order = [] if after is None else [after]

    def body(a_ref, b_ref, *rest):
        o_ref = rest[-1]

        @pl.when(pl.program_id(1) == 0)
        def _():
            o_ref[...] = jnp.zeros_like(o_ref)

        o_ref[...] += lax.dot_general(a_ref[...].astype(BF16), b_ref[...].astype(BF16), TN, preferred_element_type=F32)

    return pl.pallas_call(
        body, name=name, grid=(groups, s_len // ts),
        in_specs=[pl.BlockSpec((ts, ka), lambda g, s: (s, a_col(g))), pl.BlockSpec((ts, nb), lambda g, s: (s, b_col(g)))]
        + [ANY] * len(order),
        out_specs=pl.BlockSpec((None, ka, nb), lambda g, s: (g, 0, 0)),
        out_shape=jax.ShapeDtypeStruct((groups, ka, nb), F32),
        compiler_params=_cp(("parallel", "arbitrary")),
    )(a, b, *order)


def _wo_final(mt, wo, gate, name):
    rb = mt.shape[1]

    def body(m_ref, w_ref, gate_ref, dw_ref, dg_ref):
        @pl.when(pl.program_id(0) == 0)
        def _():
            dg_ref[...] = jnp.zeros_like(dg_ref)

        mv = m_ref[...]
        dw_ref[...] = mv * gate_ref[...]
        dg_ref[...] += jnp.sum(mv * w_ref[...].astype(F32), axis=0, keepdims=True)

    blk = pl.BlockSpec((None, rb, D), lambda k: (k, 0, 0))
    vec = pl.BlockSpec((1, D), lambda k: (0, 0))
    return pl.pallas_call(
        body, name=name, grid=(N_CHIP,), in_specs=[blk, blk, vec], out_specs=[blk, vec],
        out_shape=[jax.ShapeDtypeStruct(mt.shape, F32), jax.ShapeDtypeStruct((1, D), F32)],
        compiler_params=_cp(("arbitrary",)),
    )(mt, wo, gate)


ROW_NORM_G, ROW_CONV_W, ROW_CONV_B, ROW_B_A, ROW_B_X, ROW_LAMBDA, ROW_SC_W, ROW_POOL_B, ROW_POOL_S, ROW_FINAL_G = (
    0, 2, 6, 7, 8, 9, 10, 13, 15, 17)


def _small_pack(s1_0, s2_0, s1_1, s2_1, sm0, dsc1, dbg1, dgf, losscols, dgate0, dgate1, norm_g, sc0, sc1, lam):
    def body(s1_0r, s2_0r, s1_1r, s2_1r, sm, dsc, dbg, dgfr, lcols, dg0, dg1, ng, sc0r, sc1r, lamr, buf, dmod, loss):
        buf[...] = jnp.zeros_like(buf)
        buf[0:1, :] = s2_0r[...] * (1.0 + sc0r[...])
        buf[1:2, :] = s2_1r[...] * (1.0 + sc1r[...])
        buf[ROW_CONV_W:ROW_CONV_W + 4, :] = sm[0:4, :]
        buf[ROW_CONV_B:ROW_CONV_B + 1, :] = sm[4:5, :]
        buf[ROW_B_A:ROW_B_A + 1, :] = sm[5:6, :]
        buf[ROW_B_X:ROW_B_X + 1, :] = sm[6:7, :]
        buf[ROW_LAMBDA:ROW_LAMBDA + 1, :] = -sm[7:8, :] * _sigmoid(-lamr[...])
        buf[ROW_SC_W:ROW_SC_W + 3, :] = sm[8:11, :]
        for k in range(2):
            buf[ROW_POOL_B + k:ROW_POOL_B + k + 1, :] = dbg[:, k * D:(k + 1) * D]
            buf[ROW_POOL_S + k:ROW_POOL_S + k + 1, :] = dsc[:, k * D:(k + 1) * D]
        buf[ROW_FINAL_G:ROW_FINAL_G + 1, :] = dgfr[...]
        pieces = (s1_0r[...], s2_0r[...] * ng[0:1, :], dg0[...], s1_1r[...], s2_1r[...] * ng[1:2, :], dg1[...])
        for k, pc in enumerate(pieces):
            dmod[:, k * D:(k + 1) * D] = jnp.broadcast_to(pc, (SUBLANES, D))
        loss[...] = jnp.broadcast_to(jnp.sum(lcols[...], axis=1, keepdims=True) * (0.5 / D), loss.shape)

    args = (s1_0, s2_0, s1_1, s2_1, sm0, dsc1, dbg1, dgf, losscols, dgate0, dgate1, norm_g, sc0, sc1, lam)
    return pl.pallas_call(
        body, name="small_pack", in_specs=[VMEM] * len(args), out_specs=[VMEM] * 3,
        out_shape=[jax.ShapeDtypeStruct((SMALL_ROWS, D), F32), jax.ShapeDtypeStruct((SUBLANES, 6 * D), F32),
                   jax.ShapeDtypeStruct((SUBLANES, 128), F32)],
        compiler_params=_cp(),
    )(*args)


def _small_comm(buf_a, buf_b, dmod8):
    ra, rb = buf_a.shape[0] // N_DEV, buf_b.shape[0] // N_DEV
    wb = buf_b.shape[1]

    def body(a_ref, b_ref, dm_ref, oa_ref, ob_ref, odm_ref, ina, inb, dslot, sa, sb, s1, r1, s2, r2):
        x, y, c = _pos()
        me = 4 * x + 2 * y + c
        peers = []
        for r in range(1, N_DEV):
            fx, fy, fc = (r >> 2) & 1, (r >> 1) & 1, r & 1
            px, py, pc = _flip(x, fx), _flip(y, fy), _flip(c, fc)
            peers.append(((px, py, pc), 4 * px + 2 * py + pc))
        seg_a = lambda d: pl.ds(pl.multiple_of(d * ra, SUBLANES), ra)
        seg_b = lambda d: pl.ds(pl.multiple_of(d * rb, SUBLANES), rb)
        first = []
        for r, (peer, pid) in enumerate(peers):
            for k, (src, dst) in enumerate(((a_ref.at[seg_a(pid), :], ina.at[r]), (b_ref.at[seg_b(pid), :], inb.at[r]),
                                            (dm_ref, dslot.at[me]))):
                cp = pltpu.make_async_remote_copy(src_ref=src, dst_ref=dst, send_sem=s1.at[3 * r + k],
                                                  recv_sem=r1.at[3 * r + k], device_id=peer, device_id_type=MESH)
                cp.start()
                first.append(cp)
        dslot[me] = dm_ref[...]
        for cp in first:
            cp.wait()
        acc_a, acc_b = a_ref[seg_a(me), :], b_ref[seg_b(me), :]
        for r in range(N_DEV - 1):
            acc_a = acc_a + ina[r]
            acc_b = acc_b + inb[r]
        sa[...] = acc_a
        sb[...] = acc_b
        oa_ref[seg_a(me), :] = acc_a
        ob_ref[seg_b(me), :] = acc_b
        second = []
        for r, (peer, pid) in enumerate(peers):
            for k, (src, dst) in enumerate(((sa, oa_ref.at[seg_a(me), :]), (sb, ob_ref.at[seg_b(me), :]))):
                cp = pltpu.make_async_remote_copy(src_ref=src, dst_ref=dst, send_sem=s2.at[2 * r + k],
                                                  recv_sem=r2.at[2 * r + k], device_id=peer, device_id_type=MESH)
                cp.start()
                second.append(cp)
        rows = _rows(SUBLANES, dm_ref.shape[1])
        dm_all = jnp.zeros(dm_ref.shape, F32)
        for d in range(N_DEV):
            dm_all = jnp.where(rows == d, dslot[d], dm_all)
        odm_ref[...] = dm_all
        for cp in second:
            cp.wait()

    nrel = N_DEV - 1
    return pl.pallas_call(
        body, name="small_comm", in_specs=[VMEM] * 3, out_specs=[VMEM] * 3,
        out_shape=[jax.ShapeDtypeStruct(buf_a.shape, F32), jax.ShapeDtypeStruct(buf_b.shape, F32),
                   jax.ShapeDtypeStruct(dmod8.shape, F32)],
        scratch_shapes=[pltpu.VMEM((nrel, ra, D), F32), pltpu.VMEM((nrel, rb, wb), F32),
                        pltpu.VMEM((N_DEV,) + dmod8.shape, F32), pltpu.VMEM((ra, D), F32), pltpu.VMEM((rb, wb), F32),
                        pltpu.SemaphoreType.DMA((3 * nrel,)), pltpu.SemaphoreType.DMA((3 * nrel,)),
                        pltpu.SemaphoreType.DMA((2 * nrel,)), pltpu.SemaphoreType.DMA((2 * nrel,))],
        compiler_params=_cp(),
    )(buf_a, buf_b, dmod8)


def _adam(w, g, m, v):
    m2 = ADAM_B1 * m + (1.0 - ADAM_B1) * g
    v2 = ADAM_B2 * v + (1.0 - ADAM_B2) * (g * g)
    m_hat = m2 / (1.0 - ADAM_B1 ** ADAM_STEP)
    v_hat = v2 / (1.0 - ADAM_B2 ** ADAM_STEP)
    return -ADAM_LR * (m_hat / (jnp.sqrt(v_hat) + ADAM_EPS) + ADAM_WD * w), m2, v2


def _small_adam(red_a, red_b, dm_all, params):
    n = len(params)

    def body(*refs):
        ra, rb, dm = refs[:3]
        wmv = refs[3:3 + 3 * n]
        outs = refs[3 + 3 * n:]
        x, y, _ = _pos()
        chip = 2 * x + y

        def shard(row0, nrows, width):
            per_row = D // width
            cands = []
            for k in range(N_CHIP):
                if nrows == 1 or per_row >= N_CHIP:
                    cands.append(ra[row0:row0 + nrows, k * width:(k + 1) * width])
                else:
                    rr, cc = divmod(k * width, D)
                    cands.append(ra[row0 + rr:row0 + rr + 1, cc:cc + width])
            g = cands[0]
            for k in range(1, N_CHIP):
                g = jnp.where(chip == k, cands[k], g)
            return g

        dms = jnp.sum(dm[...], axis=0, keepdims=True)
        hw = LRU_HEADS * LRU_HEAD_DIM
        grads = [
            ra[ROW_NORM_G:ROW_NORM_G + 2, :],
            None,
            shard(ROW_CONV_W, 4, D // N_CHIP),
            ra[ROW_CONV_B:ROW_CONV_B + 1, :],
            rb[0:hw, :],
            ra[ROW_B_A:ROW_B_A + 1, :],
            rb[hw:2 * hw, :],
            ra[ROW_B_X:ROW_B_X + 1, :],
            ra[ROW_LAMBDA:ROW_LAMBDA + 1, :],
            shard(ROW_SC_W, 3, D // N_CHIP),
            shard(ROW_POOL_B, 2, 2 * D // N_CHIP),
            shard(ROW_POOL_S, 2, 2 * D // N_CHIP),
            ra[ROW_FINAL_G:ROW_FINAL_G + 1, :],
        ]
        for p in range(n):
            w_ref, m_ref, v_ref = wmv[3 * p:3 * p + 3]
            g_out, d_out, m_out, v_out = outs[4 * p:4 * p + 4]
            if grads[p] is None:
                for l in range(2):
                    g = dms[:, l * 3 * D:(l + 1) * 3 * D]
                    dl, m2, v2 = _adam(w_ref[l:l + 1, :], g, m_ref[l:l + 1, :], v_ref[l:l + 1, :])
                    g_out[l:l + 1, :] = g
                    d_out[l:l + 1, :] = dl
                    m_out[l:l + 1, :] = m2
                    v_out[l:l + 1, :] = v2
            else:
                g = grads[p]
                dl, m2, v2 = _adam(w_ref[...], g, m_ref[...], v_ref[...])
                g_out[...] = g
                d_out[...] = dl
                m_out[...] = m2
                v_out[...] = v2

    flat = [a for p in params for a in p]
    return pl.pallas_call(
        body, name="small_adam", in_specs=[VMEM] * (3 + len(flat)), out_specs=[VMEM] * (4 * n),
        out_shape=[jax.ShapeDtypeStruct(p[0].shape, F32) for p in params for _ in range(4)],
        compiler_params=_cp(),
    )(red_a, red_b, dm_all, *flat)


def _modw_adam(ca_t, dm_sh, w, m, v):
    nw = w.shape[2]

    def body(c_ref, d_ref, w_ref, m_ref, v_ref, g_out, d_out, m_out, v_out):
        g = jnp.dot(c_ref[...], d_ref[...], precision=lax.Precision.HIGHEST, preferred_element_type=F32)
        dl, m2, v2 = _adam(w_ref[...], g, m_ref[...], v_ref[...])
        g_out[...] = g
        d_out[...] = dl
        m_out[...] = m2
        v_out[...] = v2

    blk = pl.BlockSpec((None, D, nw), lambda l: (l, 0, 0))
    return pl.pallas_call(
        body, name="modw_adam", grid=(2,),
        in_specs=[pl.BlockSpec((D, SUBLANES), lambda l: (0, 0)), pl.BlockSpec((None, SUBLANES, nw), lambda l: (l, 0, 0)),
                  blk, blk, blk],
        out_specs=[blk] * 4, out_shape=[jax.ShapeDtypeStruct(w.shape, F32)] * 4,
        compiler_params=_cp(("arbitrary",)),
    )(ca_t, dm_sh, w, m, v)


def _half_rows(r):
    return r // 2


def _exchange(copies, name, out_type, n_sems, args, sequencer=None):
    n_in, n_out = len(args), len(out_type)

    def body(*refs):
        if sequencer is not None:
            barrier = pltpu.get_barrier_semaphore()
            peers = sequencer[1](*_pos())
            for peer in peers:
                pl.semaphore_signal(barrier, inc=1, device_id=peer, device_id_type=MESH)
            pl.semaphore_wait(barrier, len(peers))
        copies(refs[:n_in], refs[n_in:n_in + n_out], refs[n_in + n_out], refs[n_in + n_out + 1])

    sems = [pltpu.SemaphoreType.DMA((n_sems,))] * 2
    if sequencer is None:
        return pl.pallas_call(body, name=name, in_specs=[ANY] * n_in, out_specs=[ANY] * n_out, out_shape=out_type,
                              scratch_shapes=sems, compiler_params=_cp())(*args)
    return pl.kernel(body, out_type, mesh=plsc.ScalarSubcoreMesh(axis_name="sequencer", num_cores=1), name=name,
                     scratch_types=sems, compiler_params=pltpu.CompilerParams(collective_id=sequencer[0]))(*args)


def _sibling(x, y, c):
    return [(x, y, 1 - c)]


def _other_chips(x, y, c):
    return [(1 - x, y, c), (x, 1 - y, c), (1 - x, 1 - y, c)]


def _sib_send_halves(gs, name, sequencer_id=None):
    n = len(gs)

    def copies(ins, outs, ssem, rsem):
        x, y, c = _pos()
        cps = []
        for a in range(n):
            hr = _half_rows(gs[a].shape[1])
            cp = pltpu.make_async_remote_copy(
                src_ref=ins[a].at[:, pl.ds(pl.multiple_of((1 - c) * hr, SUBLANES), hr), :], dst_ref=outs[a],
                send_sem=ssem.at[a], recv_sem=rsem.at[a], device_id=(x, y, 1 - c), device_id_type=MESH)
            cp.start()
            cps.append(cp)
        for cp in cps:
            cp.wait()

    out_type = [jax.ShapeDtypeStruct((N_CHIP, _half_rows(g.shape[1]), g.shape[2]), F32) for g in gs]
    return _exchange(copies, name, out_type, n, gs, None if sequencer_id is None else (sequencer_id, _sibling))


def _add_half(g, got, cidx, name, after=None):
    _, hr, cc = got.shape
    rb = min(hr, 256)

    def body(c_ref, g_ref, r_ref, *rest):
        rest[-1][...] = (g_ref[...] + r_ref[...]).astype(rest[-1].dtype)

    order = [] if after is None else [after]
    blk = pl.BlockSpec((None, rb, cc), lambda k, j, c_ref: (k, j, 0))
    return pl.pallas_call(
        body, name=name,
        grid_spec=pltpu.PrefetchScalarGridSpec(
            num_scalar_prefetch=1, grid=(N_CHIP, hr // rb),
            in_specs=[pl.BlockSpec((None, rb, cc), lambda k, j, c_ref: (k, c_ref[0] * (hr // rb) + j, 0)), blk]
            + [ANY] * len(order),
            out_specs=blk),
        out_shape=jax.ShapeDtypeStruct(got.shape, GRAD_WIRE_DTYPE),
        compiler_params=_cp(("parallel", "parallel")),
    )(cidx, g, got, *order)


def _chip_scatter(ps, name, sequencer_id=None):
    n = len(ps)

    def copies(ins, outs, ssem, rsem):
        x, y, c = _pos()
        cps = []
        for a in range(n):
            for q, (fx, fy) in enumerate(((1, 0), (0, 1), (1, 1))):
                px, py = _flip(x, fx), _flip(y, fy)
                cp = pltpu.make_async_remote_copy(
                    src_ref=ins[a].at[2 * px + py], dst_ref=outs[a].at[q],
                    send_sem=ssem.at[3 * a + q], recv_sem=rsem.at[3 * a + q], device_id=(px, py, c), device_id_type=MESH)
                cp.start()
                cps.append(cp)
        for cp in cps:
            cp.wait()

    out_type = [jax.ShapeDtypeStruct((N_CHIP - 1,) + p.shape[1:], p.dtype) for p in ps]
    return _exchange(copies, name, out_type, 3 * n, ps, None if sequencer_id is None else (sequencer_id, _other_chips))


def _add_owner(p, got, chipidx, name, after=None):
    _, hr, cc = p.shape
    rb = min(hr, 256)

    def body(k_ref, p_ref, r_ref, *rest):
        rest[-1][...] = ((p_ref[...].astype(F32) + r_ref[0].astype(F32)) + r_ref[1].astype(F32)) + r_ref[2].astype(F32)

    order = [] if after is None else [after]
    return pl.pallas_call(
        body, name=name,
        grid_spec=pltpu.PrefetchScalarGridSpec(
            num_scalar_prefetch=1, grid=(hr // rb,),
            in_specs=[pl.BlockSpec((None, rb, cc), lambda j, k_ref: (k_ref[0], j, 0)),
                      pl.BlockSpec((N_CHIP - 1, rb, cc), lambda j, k_ref: (0, j, 0))] + [ANY] * len(order),
            out_specs=pl.BlockSpec((rb, cc), lambda j, k_ref: (j, 0))),
        out_shape=jax.ShapeDtypeStruct((hr, cc), F32),
        compiler_params=_cp(("parallel",)),
    )(chipidx, p, got, *order)


def _sib_exchange(ts_, name, sequencer_id=None):
    n = len(ts_)

    def copies(ins, outs, ssem, rsem):
        x, y, c = _pos()
        cps = []
        for a in range(n):
            cp = pltpu.make_async_remote_copy(src_ref=ins[a], dst_ref=outs[a], send_sem=ssem.at[a],
                                              recv_sem=rsem.at[a], device_id=(x, y, 1 - c), device_id_type=MESH)
            cp.start()
            cps.append(cp)
        for cp in cps:
            cp.wait()

    out_type = [jax.ShapeDtypeStruct(t.shape, F32) for t in ts_]
    return _exchange(copies, name, out_type, n, ts_, None if sequencer_id is None else (sequencer_id, _sibling))


def _adam_2d(w, g_own, g_sib, m, v, cidx, name):
    rr, cc = w.shape
    hr = rr // 2
    rb = min(hr, 256)
    nb = hr // rb

    def body(c_ref, w_ref, go_ref, gs_ref, m_ref, v_ref, g_out, d_out, m_out, v_out):
        g = jnp.where(pl.program_id(0) == c_ref[0], go_ref[...], gs_ref[...])
        dl, m2, v2 = _adam(w_ref[...], g, m_ref[...], v_ref[...])
        g_out[...] = g
        d_out[...] = dl
        m_out[...] = m2
        v_out[...] = v2

    blk = pl.BlockSpec((rb, cc), lambda h, j, c_ref: (h * nb + j, 0))
    hblk = pl.BlockSpec((rb, cc), lambda h, j, c_ref: (j, 0))
    return pl.pallas_call(
        body, name=name,
        grid_spec=pltpu.PrefetchScalarGridSpec(
            num_scalar_prefetch=1, grid=(2, nb), in_specs=[blk, hblk, hblk, blk, blk], out_specs=[blk] * 4),
        out_shape=[jax.ShapeDtypeStruct((rr, cc), F32)] * 4, compiler_params=_cp(("parallel", "parallel")),
    )(cidx, w, g_own, g_sib, m, v)


def kernel(x, c, norm_g, mod_w, mod_b, hy_w_in, hy_conv_w, hy_conv_b, lru_w_a, lru_b_a, lru_w_x, lru_b_x, lru_lambda, sc_conv_w, hy_w_out, pool_w_in, pool_w_grp, pool_b_grp, pool_scale, pool_w_out, final_g, loss_target, m_norm_g, m_mod_w, m_mod_b, m_hy_w_in, m_hy_conv_w, m_hy_conv_b, m_lru_w_a, m_lru_b_a, m_lru_w_x, m_lru_b_x, m_lru_lambda, m_sc_conv_w, m_hy_w_out, m_pool_w_in, m_pool_w_grp, m_pool_b_grp, m_pool_scale, m_pool_w_out, m_final_g, v_norm_g, v_mod_w, v_mod_b, v_hy_w_in, v_hy_conv_w, v_hy_conv_b, v_lru_w_a, v_lru_b_a, v_lru_w_x, v_lru_b_x, v_lru_lambda, v_sc_conv_w, v_hy_w_out, v_pool_w_in, v_pool_w_grp, v_pool_b_grp, v_pool_scale, v_pool_w_out, v_final_g):
    ax, ay, ac = _pos()
    me = 4 * ax + 2 * ay + ac
    chip = 2 * ax + ay
    xs = x[0]
    tgt = loss_target[0]
    gd = POOL_GROUP_DIM

    ca_all, mod_all, small_w = _mod_fwd(jnp.broadcast_to(c, (SUBLANES, D)), mod_w, mod_b,
                                        hy_conv_w[0], sc_conv_w[0], pool_b_grp, pool_scale)
    mod_me = lax.dynamic_index_in_dim(mod_all, me, axis=1, keepdims=False)
    sh0, sc0, gt0 = (mod_me[0:1, k * D:(k + 1) * D] for k in range(3))
    sh1, sc1, gt1 = (mod_me[1:2, k * D:(k + 1) * D] for k in range(3))
    cw = small_w[SW_CONV:SW_CONV + 4, 0:D]
    sw = small_w[SW_SC:SW_SC + 3, 0:D]
    pool_b = small_w[SW_POOL_B:SW_POOL_B + 1, :]
    pool_s = small_w[SW_POOL_S:SW_POOL_S + 1, :]
    g0, g1, gf = norm_g[0:1], norm_g[1:2], final_g.reshape(1, D)
    cb, ba, bx, lam = hy_conv_b, lru_b_a, lru_b_x, lru_lambda

    big = [hy_w_in[0], hy_w_out[0], pool_w_in[0], pool_w_grp[0].reshape(4 * 128, gd), pool_w_out[0]]
    cidx = ac.reshape(1).astype(jnp.int32)
    kidx = chip.reshape(1).astype(jnp.int32)
    w_in0, w_out0 = _wgather([_wcast_own_block(w, kidx, f"wcast_own_block_{a}") for a, w in enumerate(big[:2])],
                             "wgather_l0")
    w_in1, w_grp, w_out1 = _wgather_sequencer(
        [_wcast_own_block(w, kidx, f"wcast_own_block_{a + 2}", after=w_out0) for a, w in enumerate(big[2:])], "wgather_l1")
    w_grp =w_grp.reshape(N_CHIP, 4, 128, gd).transpose(1, 0, 2, 3).reshape(4, gd, gd)
    wa_b, wx_b = _wcast([lru_w_a[0], lru_w_x[0]])

    x1, hst, y0, h0, proj0 = _l0_fwd(xs, g0, sc0, sh0, w_in0, gt0, cw, cb, wa_b, ba, wx_b, bx, lam, sw,
                                     w_out0.reshape(2 * D, D))
    h1, proj1 = _norm_proj(x1, g1, sc1, sh1, w_in1, "l1_proj")
    dpool, mixed, y1, dx2, losscols, dgf = _l1_mix(proj1, x1, tgt, gt1, w_grp, pool_b, pool_s,
                                                    w_out1.reshape(2 * D, D), gf)

    def add_halves(grads, got, tag, after):
        return [_add_half(g, r, cidx, f"grad_add_half_{tag}{a}", after) for a, (g, r) in enumerate(zip(grads, got))]

    def add_owners(parts, got, tag, ids, after):
        own = [_add_owner(p, r, kidx, f"grad_add_owner_{tag}{a}", after) for a, (p, r) in enumerate(zip(parts, got))]
        return own, _sib_exchange(own, f"grad_sib_exchange_{tag}", ids[2])

    dproj1, mt1, d_wgrp, dsc1, dbg1 = _l1_bwd_mix(dx2, proj1, mixed, y1, dpool, gt1, w_grp, pool_s,
                                                  w_out1.reshape(2 * D, D))
    d_win1 = _wgrad(h1, dproj1, N_CHIP, D, D, lambda g: 0, lambda g: g, "l1_wgrad_in")
    dx1, s1_1, s2_1 = _dgrad_norm(dproj1, w_in1, x1, dx2, g1, sc1, "l1_bwd_proj")
    d_wout1, dgate1 = _wo_final(mt1, w_out1, gt1, "l1_wo_final")
    d_wgrp = d_wgrp.reshape(4, N_CHIP, 128, gd).transpose(1, 0, 2, 3).reshape(N_CHIP, 4 * 128, gd)
    grads_l1 = [d_win1, d_wgrp, d_wout1]
    got_l1 = _sib_send_halves(grads_l1, "grad_sib_halves_l1", CIDS_L1[0])

    dproj0, mt0, d_wa, d_wx, sm0 = _l0_bwd_mix(dx1, proj0, hst, y0, gt0, cw, cb, wa_b, ba, wx_b, bx, lam, sw,
                                               w_out0.reshape(2 * D, D))
    parts_l1 = add_halves(grads_l1, got_l1, "l1", after=sm0)
    got_l1 = _chip_scatter(parts_l1, "grad_chip_scatter_l1", CIDS_L1[1])
    d_win0 = _wgrad(h0, dproj0, N_CHIP, D, 6 * D // N_CHIP, lambda g: 0, lambda g: g, "l0_wgrad_in", after=parts_l1[0])
    d_wout0, dgate0 = _wo_final(mt0, w_out0, gt0, "l0_wo_final")
    halves_l1, sib_l1 = add_owners(parts_l1, got_l1, "l1", CIDS_L1, after=d_win0)
    grads_l0 = [d_win0, d_wout0]
    parts_l0 = add_halves(grads_l0, _sib_send_halves(grads_l0, "grad_sib_halves_l0", CIDS_L0[0]), "l0", after=None)
    got_l0 = _chip_scatter(parts_l0, "grad_chip_scatter_l0", CIDS_L0[1])
    grad_x, s1_0, s2_0 = _dgrad_norm(dproj0, w_in0, xs, dx1, g0, sc0, "l0_bwd_proj", after=parts_l0[0])
    halves_l0, sib_l0 = add_owners(parts_l0, got_l0, "l0", CIDS_L0, after=s1_0)

    buf_a, dmod8, loss8 = _small_pack(s1_0, s2_0, s1_1, s2_1, sm0, dsc1, dbg1, dgf, losscols, dgate0, dgate1,
                                      norm_g, sc0, sc1, lam)
    hw = LRU_HEADS * LRU_HEAD_DIM
    buf_b = jnp.concatenate([d_wa.reshape(hw, LRU_HEAD_DIM), d_wx.reshape(hw, LRU_HEAD_DIM)], axis=0)
    red_a, red_b, dm_all = _small_comm(buf_a, buf_b, dmod8)
    small = [(norm_g, m_norm_g, v_norm_g), (mod_b, m_mod_b, v_mod_b),
             (hy_conv_w[0], m_hy_conv_w[0], v_hy_conv_w[0]), (hy_conv_b, m_hy_conv_b, v_hy_conv_b),
             tuple(a.reshape(hw, LRU_HEAD_DIM) for a in (lru_w_a, m_lru_w_a, v_lru_w_a)),
             (lru_b_a, m_lru_b_a, v_lru_b_a),
             tuple(a.reshape(hw, LRU_HEAD_DIM) for a in (lru_w_x, m_lru_w_x, v_lru_w_x)),
             (lru_b_x, m_lru_b_x, v_lru_b_x), (lru_lambda, m_lru_lambda, v_lru_lambda),
             (sc_conv_w[0], m_sc_conv_w[0], v_sc_conv_w[0]), (pool_b_grp, m_pool_b_grp, v_pool_b_grp),
             (pool_scale, m_pool_scale, v_pool_scale),
             tuple(a.reshape(1, D) for a in (final_g, m_final_g, v_final_g))]
    small_names = ["norm_g", "mod_b", "hy_conv_w", "hy_conv_b", "lru_w_a", "lru_b_a", "lru_w_x", "lru_b_x",
                   "lru_lambda", "sc_conv_w", "pool_b_grp", "pool_scale", "final_g"]
    small_out = _small_adam(red_a, red_b, dm_all, small)
    res = {}
    shapes = dict(norm_g=norm_g, mod_b=mod_b, hy_conv_w=hy_conv_w, hy_conv_b=hy_conv_b, lru_w_a=lru_w_a, lru_b_a=lru_b_a,
                  lru_w_x=lru_w_x, lru_b_x=lru_b_x, lru_lambda=lru_lambda, sc_conv_w=sc_conv_w, pool_b_grp=pool_b_grp,
                  pool_scale=pool_scale, final_g=final_g)
    for p, nm in enumerate(small_names):
        res[nm] = tuple(o.reshape(shapes[nm].shape) for o in small_out[4 * p:4 * p + 4])

    nw = mod_w.shape[2]
    dm_sh = jnp.stack([lax.dynamic_slice_in_dim(dm_all[:, l * 3 * D:(l + 1) * 3 * D], chip * nw, nw, axis=1)
                       for l in range(2)])
    res["mod_w"] = tuple(_modw_adam(ca_all.T, dm_sh, mod_w, m_mod_w, v_mod_w))

    halves = list(halves_l0) + list(halves_l1)
    sib_halves = list(sib_l0) + list(sib_l1)
    big_names = ["hy_w_in", "hy_w_out", "pool_w_in", "pool_w_grp", "pool_w_out"]
    big_wmv = [(hy_w_in, m_hy_w_in, v_hy_w_in), (hy_w_out, m_hy_w_out, v_hy_w_out), (pool_w_in, m_pool_w_in, v_pool_w_in),
               (pool_w_grp, m_pool_w_grp, v_pool_w_grp), (pool_w_out, m_pool_w_out, v_pool_w_out)]
    for a, nm in enumerate(big_names):
        rr, cc = big[a].shape
        w, m, v = (t.reshape(rr, cc) for t in big_wmv[a])
        outs = _adam_2d(w, halves[a], sib_halves[a], m, v, cidx, f"adam_{nm}")
        res[nm] = tuple(o.reshape(big_wmv[a][0].shape) for o in outs)

    loss = lax.psum(loss8[0, 0], ("x", "y", "c"))
    order = ["norm_g", "mod_w", "mod_b", "hy_w_in", "hy_conv_w", "hy_conv_b", "lru_w_a", "lru_b_a", "lru_w_x", "lru_b_x",
             "lru_lambda", "sc_conv_w", "hy_w_out", "pool_w_in", "pool_w_grp", "pool_b_grp", "pool_scale", "pool_w_out",
             "final_g"]
    return (loss, grad_x[None], *[res[nm][0] for nm in order], *[res[nm][1] for nm in order],
            *[res[nm][2] for nm in order], *[res[nm][3] for nm in order])
```

```python
import jax
import jax.numpy as jnp
from jax import lax
from jax.experimental import pallas as pl
from jax.experimental.pallas import tpu as pltpu
from jax.experimental.pallas import tpu_sc as plsc

F32, BF16 = jnp.float32, jnp.bfloat16
D = 1024
RMS_EPS = 1e-6
SQRT_FLOOR = 1e-30
LRU_C = 8.0
LRU_HEADS, LRU_HEAD_DIM = 8, 128
POOL_WINDOWS = (2, 4, 8, 16)
POOL_GROUP_DIM = 512
ADAM_LR, ADAM_B1, ADAM_B2, ADAM_EPS, ADAM_WD, ADAM_STEP = 0.001, 0.9, 0.999, 1e-08, 0.01, 10
MESH = pl.DeviceIdType.MESH
CID_WGATHER = 1
CIDS_L1 = (2, 3, 4)
CIDS_L0 = (5, 6, 7)
N_DEV, N_CHIP = 8, 4
SUBLANES = 8
BF16_ROWS = 16
POOL_HALO = 16
TS_PROJ, TS_MIX, TS_WGRAD, TS_DGRAD = 1024, 256, 1024, 256
SMALL_ROWS = 64
GRAD_WIRE_DTYPE = BF16
ANY = pl.BlockSpec(memory_space=pl.ANY)
VMEM = pl.BlockSpec(memory_space=pltpu.VMEM)
NT = (((1,), (1,)), ((), ()))
TN = (((0,), (0,)), ((), ()))


def _cp(sem=None, vmem_mb=56):
    kw = dict(vmem_limit_bytes=vmem_mb * 2 ** 20)
    if sem is not None:
        kw["dimension_semantics"] = sem
    return pltpu.CompilerParams(**kw)


def _tile(n, t):
    return min(n, t)


def _pos():
    return lax.axis_index("x"), lax.axis_index("y"), lax.axis_index("c")


def _flip(v, f):
    return 1 - v if f else v


def _sigmoid(z):
    return 0.5 * jnp.tanh(0.5 * z) + 0.5


def _rows(n, c):
    return lax.broadcasted_iota(jnp.int32, (n, c), 0)


def _down(a, d):
    return a if d == 0 else pltpu.roll(a, d, 0)


def _up(a, d):
    return a if d == 0 else pltpu.roll(a, a.shape[0] - d, 0)


def _scan_fwd_steps(a, u, carry):
    n, c = a.shape
    sub = _rows(SUBLANES, c)
    out = []
    for k in range(n // SUBLANES):
        p = a[k * SUBLANES:(k + 1) * SUBLANES]
        g = u[k * SUBLANES:(k + 1) * SUBLANES]
        for d in (1, 2, 4):
            keep = sub >= d
            g = g + p * jnp.where(keep, pltpu.roll(g, d, 0), 0.0)
            p = p * jnp.where(keep, pltpu.roll(p, d, 0), 1.0)
        h = g + p * carry
        carry = h[SUBLANES - 1:SUBLANES, :]
        out.append(h)
        yield
    return jnp.concatenate(out, axis=0)


def _scan_rev_steps(alpha, b, carry):
    n, c = alpha.shape
    sub = _rows(SUBLANES, c)
    out = []
    for k in reversed(range(n // SUBLANES)):
        p = alpha[k * SUBLANES:(k + 1) * SUBLANES]
        g = b[k * SUBLANES:(k + 1) * SUBLANES]
        for d in (1, 2, 4):
            keep = sub < SUBLANES - d
            g = g + p * jnp.where(keep, pltpu.roll(g, SUBLANES - d, 0), 0.0)
            p = p * jnp.where(keep, pltpu.roll(p, SUBLANES - d, 0), 1.0)
        h = g + p * carry
        carry = h[0:1, :]
        out.append(h)
        yield
    return jnp.concatenate(out[::-1], axis=0)


def _run(steps):
    while True:
        try:
            next(steps)
        except StopIteration as done:
            return done.value


def _paired(progress, pieces):
    n, done = len(pieces), 1
    pieces[0]()
    for frac in progress:
        while done < n and done <= frac * n:
            pieces[done]()
            done += 1
    while done < n:
        pieces[done]()
        done += 1


def _conv_taps(ext, halo, n, width):
    return [_down(ext, width - 1 - k)[halo:halo + n] for k in range(width)]


def _lru_gates(xc, wa_ref, ba, wx_ref, bx):
    xb = xc.astype(BF16)
    pa, px = [], []
    for h in range(LRU_HEADS):
        xh = xb[:, h * LRU_HEAD_DIM:(h + 1) * LRU_HEAD_DIM]
        pa.append(jnp.dot(xh, wa_ref[h], preferred_element_type=F32))
        px.append(jnp.dot(xh, wx_ref[h], preferred_element_type=F32))
    r = _sigmoid(jnp.concatenate(pa, axis=1) + ba)
    ig = _sigmoid(jnp.concatenate(px, axis=1) + bx)
    return r, ig


def _softplus_neg(lam):
    return jnp.maximum(-lam, 0.0) + jnp.log1p(jnp.exp(-jnp.abs(lam)))


def _recip_1_to_2(d):
    r0 = pl.reciprocal(d, approx=True)
    return r0 * (2.0 - d * r0)


def _lru_decay(r, sp, first):
    big_l = (-LRU_C) * r * sp
    a = jnp.exp(big_l)
    th = jnp.tanh(big_l)
    q = (-2.0 * th) * _recip_1_to_2(1.0 - th)
    rs = lax.rsqrt(jnp.maximum(q, SQRT_FLOOR))
    return a, jnp.where(first, 1.0, q * rs), rs


def _pool_inv_counts(t0, n):
    t = (t0 + lax.broadcasted_iota(jnp.int32, (n, 1), 0) + 1).astype(F32)
    return [1.0 / jnp.minimum(t, float(w)) for w in POOL_WINDOWS]


def _window_sums(ext, shift):
    gd = POOL_GROUP_DIM
    out = []
    s = ext
    for k in range(len(POOL_WINDOWS)):
        s = s + shift(s, 2 ** k)
        out.append(s[:, 0:gd])
        if k + 1 < len(POOL_WINDOWS):
            s = s[:, gd:]
    return out


SW_ROWS, SW_COLS = 16, 2 * D
SW_CONV, SW_SC, SW_POOL_B, SW_POOL_S = 0, 4, 8, 9


def _mod_fwd(c8, mod_w, mod_b, conv_w, sc_w, pool_b, pool_s):
    nw = mod_w.shape[2]
    cq, pq = conv_w.shape[1], pool_b.shape[1]

    def body(c_ref, w_ref, b_ref, cw_ref, sw_ref, pb_ref, ps_ref, ca_ref, mod_ref, small_ref,
             cslot, mslot, msend, pslot, psend, s1, r1, s2, r2, s3, r3):
        x, y, c = _pos()
        me = 4 * x + 2 * y + c
        chip = 2 * x + y
        first = []
        for r in range(1, N_DEV):
            fx, fy, fc = (r >> 2) & 1, (r >> 1) & 1, r & 1
            cp = pltpu.make_async_remote_copy(
                src_ref=c_ref, dst_ref=cslot.at[me], send_sem=s1.at[r - 1], recv_sem=r1.at[r - 1],
                device_id=(_flip(x, fx), _flip(y, fy), _flip(c, fc)), device_id_type=MESH)
            cp.start()
            first.append(cp)
        cslot[me] = c_ref[...]
        for cp in first:
            cp.wait()
        rows = _rows(SUBLANES, D)
        call = jnp.zeros((SUBLANES, D), F32)
        for d in range(N_DEV):
            call = jnp.where(rows == d, cslot[d], call)
        ca = call * _sigmoid(call)
        ca_ref[...] = ca
        for l in range(2):
            msend[l] = jnp.dot(ca, w_ref[l], precision=lax.Precision.HIGHEST, preferred_element_type=F32)
        psend[...] = jnp.zeros_like(psend)
        psend[SW_CONV:SW_CONV + 4, 0:cq] = cw_ref[...]
        psend[SW_SC:SW_SC + 3, 0:cq] = sw_ref[...]
        psend[SW_POOL_B:SW_POOL_B + 1, :] = pb_ref[...]
        psend[SW_POOL_S:SW_POOL_S + 1, :] = ps_ref[...]
        second = []
        for q, (fx, fy) in enumerate(((1, 0), (0, 1), (1, 1))):
            peer = (_flip(x, fx), _flip(y, fy), c)
            for src, dst, ss, rs in ((msend, mslot, s2, r2), (psend, pslot, s3, r3)):
                cp = pltpu.make_async_remote_copy(src_ref=src, dst_ref=dst.at[chip], send_sem=ss.at[q], recv_sem=rs.at[q],
                                                  device_id=peer, device_id_type=MESH)
                cp.start()
                second.append(cp)
        mslot[chip] = msend[...]
        pslot[chip] = psend[...]
        for cp in second:
            cp.wait()
        small_ref[...] = jnp.zeros_like(small_ref)
        for j in range(N_CHIP):
            for l in range(2):
                mod_ref[l, :, j * nw:(j + 1) * nw] = mslot[j, l] + b_ref[l:l + 1, j * nw:(j + 1) * nw]
            small_ref[0:SUBLANES, j * cq:(j + 1) * cq] = pslot[j, 0:SUBLANES, 0:cq]
            small_ref[SUBLANES:SW_ROWS, j * pq:(j + 1) * pq] = pslot[j, SUBLANES:SW_ROWS, :]

    args = (c8, mod_w, mod_b, conv_w, sc_w, pool_b, pool_s)
    dma3 = pltpu.SemaphoreType.DMA((N_CHIP - 1,))
    return pl.pallas_call(
        body, name="mod_fwd",
        in_specs=[VMEM] * len(args), out_specs=[VMEM] * 3,
        out_shape=[jax.ShapeDtypeStruct((SUBLANES, D), F32), jax.ShapeDtypeStruct((2, SUBLANES, N_CHIP * nw), F32),
                   jax.ShapeDtypeStruct((SW_ROWS, SW_COLS), F32)],
        scratch_shapes=[pltpu.VMEM((N_DEV, SUBLANES, D), F32), pltpu.VMEM((N_CHIP, 2, SUBLANES, nw), F32),
                        pltpu.VMEM((2, SUBLANES, nw), F32), pltpu.VMEM((N_CHIP, SW_ROWS, pq), F32),
                        pltpu.VMEM((SW_ROWS, pq), F32),
                        pltpu.SemaphoreType.DMA((N_DEV - 1,)), pltpu.SemaphoreType.DMA((N_DEV - 1,)),
                        dma3, dma3, dma3, dma3],
        compiler_params=_cp(),
    )(*args)


def _wcast(ws):
    def body(*refs):
        n = len(refs) // 2
        for a in range(n):
            refs[n + a][...] = refs[a][...].astype(BF16)

    return pl.pallas_call(
        body, name="wcast", in_specs=[VMEM] * len(ws), out_specs=[VMEM] * len(ws),
        out_shape=[jax.ShapeDtypeStruct(w.shape, BF16) for w in ws], compiler_params=_cp(),
    )(*ws)


def _wcast_own_block(w, kidx, name, after=None):
    rr, cc = w.shape
    rb = min(rr, 256)

    def body(k_ref, w_ref, *rest):
        rest[-1][...] = w_ref[...].astype(BF16)

    order = [] if after is None else [after]
    return pl.pallas_call(
        body, name=name,
        grid_spec=pltpu.PrefetchScalarGridSpec(
            num_scalar_prefetch=1, grid=(rr // rb,),
            in_specs=[pl.BlockSpec((rb, cc), lambda j, k_ref: (j, 0))] + [ANY] * len(order),
            out_specs=pl.BlockSpec((None, rb, cc), lambda j, k_ref: (k_ref[0], j, 0))),
        out_shape=jax.ShapeDtypeStruct((N_CHIP, rr, cc), BF16),
        compiler_params=_cp(("parallel",)),
    )(kidx, w, *order)


def _wgather_copies(outs, rows, ssem, rsem, fssem, frsem):
    n = len(outs)
    x, y, c = _pos()
    chip = 2 * x + y
    sib = (x, y, 1 - c)
    flips = ((1, 0), (0, 1), (1, 1))

    def half(a, which):
        hr = rows[a] // 2
        return pl.ds(pl.multiple_of(which * hr, BF16_ROWS), hr)

    sends = []
    for a in range(n):
        mine = outs[a].at[chip, half(a, c), :]
        for q, (fx, fy) in enumerate(flips):
            cp = pltpu.make_async_remote_copy(
                src_ref=mine, dst_ref=mine, send_sem=ssem.at[3 * a + q], recv_sem=rsem.at[3 * a + q],
                device_id=(_flip(x, fx), _flip(y, fy), c), device_id_type=MESH)
            cp.start()
            sends.append(cp)
    passed = []
    for a in range(n):
        for q, (fx, fy) in enumerate(flips):
            src_chip = 2 * _flip(x, fx) + _flip(y, fy)
            landed = outs[a].at[src_chip, half(a, c), :]
            pltpu.make_async_remote_copy(
                src_ref=landed, dst_ref=landed, send_sem=ssem.at[3 * a + q], recv_sem=rsem.at[3 * a + q],
                device_id=sib, device_id_type=MESH).wait_recv()
            cp = pltpu.make_async_remote_copy(
                src_ref=landed, dst_ref=landed, send_sem=fssem.at[3 * a + q], recv_sem=frsem.at[3 * a + q],
                device_id=sib, device_id_type=MESH)
            cp.start()
            passed.append(cp)
    for a in range(n):
        for q, (fx, fy) in enumerate(flips):
            src_chip = 2 * _flip(x, fx) + _flip(y, fy)
            other = outs[a].at[src_chip, half(a, 1 - c), :]
            pltpu.make_async_remote_copy(
                src_ref=other, dst_ref=other, send_sem=fssem.at[3 * a + q], recv_sem=frsem.at[3 * a + q],
                device_id=sib, device_id_type=MESH).wait_recv()
    for cp in sends + passed:
        cp.wait_send()


def _wgather(bufs, name):
    n = len(bufs)

    def body(*refs):
        _wgather_copies(refs[n:2 * n], [b.shape[1] for b in bufs], *refs[2 * n:])

    return pl.pallas_call(
        body, name=name, in_specs=[ANY] * n, out_specs=[ANY] * n,
        out_shape=[jax.ShapeDtypeStruct(b.shape, BF16) for b in bufs],
        input_output_aliases={a: a for a in range(n)},
        scratch_shapes=[pltpu.SemaphoreType.DMA((3 * n,))] * 4,
        compiler_params=_cp(),
    )(*bufs)


def _wgather_sequencer(bufs, name):
    n = len(bufs)
    refs = [jax.new_ref(b, memory_space=pltpu.MemorySpace.HBM) for b in bufs]
    dma = pltpu.SemaphoreType.DMA((3 * n,))

    @pl.kernel(mesh=plsc.ScalarSubcoreMesh(axis_name="sequencer", num_cores=1), name=name,
               scratch_types=(dma, dma, dma, dma), compiler_params=pltpu.CompilerParams(collective_id=CID_WGATHER))
    def launch(ssem, rsem, fssem, frsem):
        x, y, c = _pos()
        barrier = pltpu.get_barrier_semaphore()
        for peer in ((1 - x, y, c), (x, 1 - y, c), (1 - x, 1 - y, c), (x, y, 1 - c)):
            pl.semaphore_signal(barrier, inc=1, device_id=peer, device_id_type=MESH)
        pl.semaphore_wait(barrier, 4)
        _wgather_copies(refs, [b.shape[1] for b in bufs], ssem, rsem, fssem, frsem)

    launch()
    return [r[...] for r in refs]


def _norm_proj(x, g, sc, sh, w, name):
    s_len, nb = x.shape[0], w.shape[2]
    ts = _tile(s_len, TS_PROJ)

    def body(x_ref, g_ref, sc_ref, sh_ref, w_ref, h_ref, p_ref):
        @pl.when(pl.program_id(1) == 0)
        def _():
            xv = x_ref[...]
            r = lax.rsqrt(jnp.mean(xv * xv, axis=-1, keepdims=True) + RMS_EPS)
            h_ref[...] = (xv * r * (g_ref[...] * (1.0 + sc_ref[...])) + sh_ref[...]).astype(BF16)

        p_ref[...] = jnp.dot(h_ref[...], w_ref[...], preferred_element_type=F32).astype(BF16)

    vec = pl.BlockSpec((1, D), lambda i, j: (0, 0))
    return pl.pallas_call(
        body, name=name, grid=(s_len // ts, N_CHIP),
        in_specs=[pl.BlockSpec((ts, D), lambda i, j: (i, 0)), vec, vec, vec,
                  pl.BlockSpec((None, D, nb), lambda i, j: (j, 0, 0))],
        out_specs=[pl.BlockSpec((ts, D), lambda i, j: (i, 0)), pl.BlockSpec((ts, nb), lambda i, j: (i, j))],
        out_shape=[jax.ShapeDtypeStruct((s_len, D), BF16), jax.ShapeDtypeStruct((s_len, N_CHIP * nb), BF16)],
        compiler_params=_cp(("parallel", "arbitrary")),
    )(x, g, sc, sh, w)


def _l0_fwd(x, g, sc, sh, w_in, gate, cw, cb, wa, ba, wx, bx, lam, sw, wo):
    s_len, nb = x.shape[0], w_in.shape[2]
    ts = _tile(s_len, TS_MIX)
    n_t = s_len // ts
    hl = SUBLANES

    def body(xa_ref, xb_ref, g_ref, sc_ref, sh_ref, win_ref, gate_ref, cw_ref, cb_ref, wa_ref, ba_ref, wx_ref, bx_ref,
             lam_ref, sw_ref, wo_ref, x1_ref, h_ref, y_ref, h0_ref, p_ref, pcur, pnext, cxa, czz, chh):
        i = pl.program_id(0)

        @pl.when(i == 0)
        def _():
            cxa[...] = jnp.zeros_like(cxa)
            czz[...] = jnp.zeros_like(czz)
            chh[...] = jnp.zeros_like(chh)
            pnext[...] = jnp.zeros_like(pnext)

        pcur[...] = pnext[...]
        xv = xa_ref[...]
        rinv = lax.rsqrt(jnp.mean(xv * xv, axis=-1, keepdims=True) + RMS_EPS)
        h0 = (xv * rinv * (g_ref[...] * (1.0 + sc_ref[...])) + sh_ref[...]).astype(BF16)
        h0_ref[...] = h0

        def project(k):
            def emit():
                pk = jnp.dot(h0, win_ref[k], preferred_element_type=F32).astype(BF16)
                p_ref[:, k * nb:(k + 1) * nb] = pk
                pnext[:, k * nb:(k + 1) * nb] = pk
            return emit

        def mixer():
            piece = lambda k: pcur[:, k * D:(k + 1) * D].astype(F32)
            xa = piece(0)
            rows = _rows(ts, D)
            taps = _conv_taps(jnp.concatenate([cxa[...], xa], axis=0), hl, ts, 4)
            xc = cb_ref[...] + sum(cw_ref[k:k + 1, :] * taps[k] for k in range(4))
            r, ig = _lru_gates(xc, wa_ref, ba_ref[...], wx_ref, bx_ref[...])
            a, m, _ = _lru_decay(r, _softplus_neg(lam_ref[...]), (rows == 0) & (i == 1))
            yield 0.26
            h = _run(_scan_fwd_steps(a, m * ig * xc, chh[hl - 1:hl, :]))
            yield 0.51
            gcp, v = piece(3), piece(4)
            z = gcp * v
            ztaps = _conv_taps(jnp.concatenate([czz[...], z], axis=0), hl, ts, 3)
            yb = piece(2) * sum(sw_ref[k:k + 1, :] * ztaps[k] for k in range(3))
            ga, gb = piece(1), piece(5)
            y = jnp.concatenate([h * (ga * _sigmoid(ga)), yb * (gb * _sigmoid(gb))], axis=1).astype(BF16)
            yield 0.76
            y_ref[...] = y
            x1_ref[...] = xb_ref[...] + gate_ref[...] * jnp.dot(y, wo_ref[...], preferred_element_type=F32)
            h_ref[...] = h.astype(BF16)
            cxa[...] = xa[ts - hl:, :]
            czz[...] = z[ts - hl:, :]
            chh[...] = jnp.where(i > 0, h[ts - hl:, :], 0.0)

        _paired(mixer(), [project(k) for k in range(N_CHIP)])

    def full(a):
        return pl.BlockSpec(a.shape, lambda i: (0,) * a.ndim)

    ahead = lambda w: pl.BlockSpec((ts, w), lambda i: (jnp.minimum(i, n_t - 1), 0))
    behind = lambda w: pl.BlockSpec((ts, w), lambda i: (jnp.maximum(i - 1, 0), 0))
    args = (x, x, g, sc, sh, w_in, gate, cw, cb, wa, ba, wx, bx, lam, sw, wo)
    return pl.pallas_call(
        body, name="l0_fwd", grid=(n_t + 1,),
        in_specs=[ahead(D), behind(D)] + [full(a) for a in args[2:]],
        out_specs=[behind(D), behind(D), behind(2 * D), ahead(D), ahead(N_CHIP * nb)],
        out_shape=[jax.ShapeDtypeStruct((s_len, D), F32), jax.ShapeDtypeStruct((s_len, D), BF16),
                   jax.ShapeDtypeStruct((s_len, 2 * D), BF16), jax.ShapeDtypeStruct((s_len, D), BF16),
                   jax.ShapeDtypeStruct((s_len, N_CHIP * nb), BF16)],
        scratch_shapes=[pltpu.VMEM((ts, N_CHIP * nb), BF16)] * 2 + [pltpu.VMEM((hl, D), F32)] * 3,
        compiler_params=_cp(("arbitrary",)),
    )(*args)


def _l1_mix(proj, x1, tgt, gate, wg, bg, scale, wo, gf):
    s_len = x1.shape[0]
    ts = _tile(s_len, TS_MIX)
    pw, gd, hl = 2 * D, POOL_GROUP_DIM, POOL_HALO

    def body(p_ref, x_ref, t_ref, gate_ref, wg_ref, bg_ref, sc_ref, wo_ref, gf_ref,
             d_ref, mx_ref, y_ref, dx_ref, loss_ref, dgf_ref, cv):
        i = pl.program_id(0)

        @pl.when(i == 0)
        def _():
            cv[...] = jnp.zeros_like(cv)
            loss_ref[...] = jnp.zeros_like(loss_ref)
            dgf_ref[...] = jnp.zeros_like(dgf_ref)

        v = p_ref[:, 0:pw].astype(F32)
        gg = p_ref[:, pw:2 * pw].astype(F32)
        sums = _window_sums(jnp.concatenate([cv[...], v], axis=0), _down)
        inv = _pool_inv_counts(i * ts, ts)
        dd = [sums[k][hl:hl + ts] * inv[k] - v[:, k * gd:(k + 1) * gd] for k in range(4)]
        mixed = jnp.concatenate(
            [jnp.dot(dd[k].astype(BF16), wg_ref[k], preferred_element_type=F32) for k in range(4)], axis=1) + bg_ref[...]
        d_ref[...] = jnp.concatenate(dd, axis=1).astype(BF16)
        mx_ref[...] = mixed.astype(BF16)
        y = (mixed * sc_ref[...] * (gg * _sigmoid(gg))).astype(BF16)
        y_ref[...] = y
        x2 = x_ref[...] + gate_ref[...] * jnp.dot(y, wo_ref[...], preferred_element_type=F32)
        r2 = lax.rsqrt(jnp.mean(x2 * x2, axis=-1, keepdims=True) + RMS_EPS)
        n2 = x2 * r2
        err = n2 * gf_ref[...] - t_ref[...]
        loss_ref[...] += jnp.sum(err * err, axis=0, keepdims=True)
        dyf = err * (1.0 / D)
        dgf_ref[...] += jnp.sum(dyf * n2, axis=0, keepdims=True)
        dn = dyf * gf_ref[...]
        dx_ref[...] = r2 * (dn - n2 * jnp.mean(dn * n2, axis=-1, keepdims=True))
        cv[...] = v[ts - hl:, :]

    def full(a):
        return pl.BlockSpec(a.shape, lambda i: (0,) * a.ndim)

    row = lambda w: pl.BlockSpec((ts, w), lambda i: (i, 0))
    acc = pl.BlockSpec((1, D), lambda i: (0, 0))
    return pl.pallas_call(
        body, name="l1_mix", grid=(s_len // ts,),
        in_specs=[row(2 * pw), row(D), row(D)] + [full(a) for a in (gate, wg, bg, scale, wo, gf)],
        out_specs=[row(pw), row(pw), row(pw), row(D), acc, acc],
        out_shape=[jax.ShapeDtypeStruct((s_len, pw), BF16)] * 3 + [jax.ShapeDtypeStruct((s_len, D), F32)]
        + [jax.ShapeDtypeStruct((1, D), F32)] * 2,
        scratch_shapes=[pltpu.VMEM((hl, pw), F32)],
        compiler_params=_cp(("arbitrary",)),
    )(proj, x1, tgt, gate, wg, bg, scale, wo, gf)


def _l1_bwd_mix(dx2, proj, mixed, y, dpool, gate, wg, scale, wo):
    s_len = dx2.shape[0]
    ts = _tile(s_len, TS_MIX)
    n_t = s_len // ts
    pw, gd, hl = 2 * D, POOL_GROUP_DIM, POOL_HALO

    def body(dx_ref, gg_ref, mx_ref, y_ref, d_ref, gate_ref, wg_ref, sc_ref, wo_ref,
             dp_ref, mt_ref, dwg_ref, dsc_ref, dbg_ref, cq):
        i = pl.program_id(0)

        @pl.when(i == 0)
        def _():
            cq[...] = jnp.zeros_like(cq)
            dsc_ref[...] = jnp.zeros_like(dsc_ref)
            dbg_ref[...] = jnp.zeros_like(dbg_ref)
            mt_ref[...] = jnp.zeros_like(mt_ref)
            dwg_ref[...] = jnp.zeros_like(dwg_ref)

        dxv = dx_ref[...]
        dxb = dxv.astype(BF16)

        def wgrad_out(k):
            mt_ref[k] += lax.dot_general(y_ref[:, k * gd:(k + 1) * gd], dxb, TN, preferred_element_type=F32)

        dy = lax.dot_general((gate_ref[...] * dxv).astype(BF16), wo_ref[...], NT, preferred_element_type=F32)
        wgrad_out(0)
        gg = gg_ref[...].astype(F32)
        mixed = mx_ref[...].astype(F32)
        s = _sigmoid(gg)
        sg = gg * s
        dmixed = dy * sc_ref[...] * sg
        dsc_ref[...] += jnp.sum(dy * mixed * sg, axis=0, keepdims=True)
        dbg_ref[...] += jnp.sum(dmixed, axis=0, keepdims=True)
        dmb = dmixed.astype(BF16)
        wgrad_out(1)
        dp_ref[:, pw:2 * pw] = (dy * sc_ref[...] * mixed * (s * (1.0 + gg * (1.0 - s)))).astype(BF16)
        inv = _pool_inv_counts((n_t - 1 - i) * ts, ts)
        dd = []
        for k in range(4):
            dmk = dmb[:, k * gd:(k + 1) * gd]
            dd.append(lax.dot_general(dmk, wg_ref[k], NT, preferred_element_type=F32))
            dwg_ref[k] += lax.dot_general(d_ref[:, k * gd:(k + 1) * gd], dmk, TN, preferred_element_type=F32)
        wgrad_out(2)
        q = jnp.concatenate([dd[k] * inv[k] for k in range(4)], axis=1)
        sums = _window_sums(jnp.concatenate([q, cq[...]], axis=0), _up)
        wgrad_out(3)
        dp_ref[:, 0:pw] = jnp.concatenate([sums[k][0:ts] - dd[k] for k in range(4)], axis=1).astype(BF16)
        cq[...] = q[0:hl, :]

    def full(a):
        return pl.BlockSpec(a.shape, lambda i: (0,) * a.ndim)

    rev = lambda w, j=0: pl.BlockSpec((ts, w), lambda i: (n_t - 1 - i, j))
    acc = pl.BlockSpec((1, pw), lambda i: (0, 0))
    return pl.pallas_call(
        body, name="l1_bwd_mix", grid=(n_t,),
        in_specs=[rev(D), rev(pw, 1), rev(pw), rev(pw), rev(pw)] + [full(a) for a in (gate, wg, scale, wo)],
        out_specs=[rev(2 * pw), pl.BlockSpec((N_CHIP, gd, D), lambda i: (0, 0, 0)),
                   pl.BlockSpec((4, gd, gd), lambda i: (0, 0, 0)), acc, acc],
        out_shape=[jax.ShapeDtypeStruct((s_len, 2 * pw), BF16), jax.ShapeDtypeStruct((N_CHIP, gd, D), F32),
                   jax.ShapeDtypeStruct((4, gd, gd), F32),
                   jax.ShapeDtypeStruct((1, pw), F32), jax.ShapeDtypeStruct((1, pw), F32)],
        scratch_shapes=[pltpu.VMEM((hl, pw), F32)],
        compiler_params=_cp(("arbitrary",)),
    )(dx2, proj, mixed, y, dpool, gate, wg, scale, wo)


def _l0_bwd_mix(dx1, proj, hst, y, gate, cw, cb, wa, ba, wx, bx, lam, sw, wo):
    s_len = dx1.shape[0]
    ts = _tile(s_len, TS_MIX)
    n_t = s_len // ts
    hl, hb = SUBLANES, BF16_ROWS
    yb_w = 2 * D // N_CHIP

    def body(dx_ref, p_ref, ph_ref, h_ref, hh_ref, y_ref, gate_ref, cw_ref, cb_ref, wa_ref, ba_ref, wx_ref, bx_ref,
             lam_ref, sw_ref, wo_ref, dp_ref, mt_ref, dwa_ref, dwx_ref, sm_ref, cg, cdxc, cdcz, ca):
        i = pl.program_id(0)
        ri = n_t - 1 - i

        @pl.when(i == 0)
        def _():
            cg[...] = jnp.zeros_like(cg)
            ca[...] = jnp.zeros_like(ca)
            cdxc[...] = jnp.zeros_like(cdxc)
            cdcz[...] = jnp.zeros_like(cdcz)
            sm_ref[...] = jnp.zeros_like(sm_ref)
            mt_ref[...] = jnp.zeros_like(mt_ref)
            dwa_ref[...] = jnp.zeros_like(dwa_ref)
            dwx_ref[...] = jnp.zeros_like(dwx_ref)

        dxb = dx_ref[...].astype(BF16)

        def wgrad_out(k):
            mt_ref[k] += lax.dot_general(y_ref[:, k * yb_w:(k + 1) * yb_w], dxb, TN, preferred_element_type=F32)

        wgrad_out(0)
        has_prev = (ri > 0).astype(F32)
        xa, ga, gbp, gcp, v, gb = [p_ref[:, k * D:(k + 1) * D].astype(F32) for k in range(6)]
        prev = lambda k: ph_ref[:, k * D:(k + 1) * D].astype(F32)[hb - hl:hb] * has_prev
        rows = _rows(ts, D)
        first = (rows == 0) & (ri == 0)
        xtaps = _conv_taps(jnp.concatenate([prev(0), xa], axis=0), hl, ts, 4)
        xc = cb_ref[...] + sum(cw_ref[k:k + 1, :] * xtaps[k] for k in range(4))
        r, ig = _lru_gates(xc, wa_ref, ba_ref[...], wx_ref, bx_ref[...])
        sp = _softplus_neg(lam_ref[...])
        a, m, inv_m = _lru_decay(r, sp, first)
        z = gcp * v
        ztaps = _conv_taps(jnp.concatenate([prev(3) * prev(4), z], axis=0), hl, ts, 3)
        cz = sum(sw_ref[k:k + 1, :] * ztaps[k] for k in range(3))
        h = h_ref[...].astype(F32)
        hprev = _down(jnp.concatenate([hh_ref[...].astype(F32)[hb - hl:hb] * has_prev, h], axis=0), 1)[hl:hl + ts]
        dy = lax.dot_general((gate_ref[...] * dx_ref[...]).astype(BF16), wo_ref[...], NT, preferred_element_type=F32)
        dya_pre, dyb_pre = dy[:, 0:D], dy[:, D:2 * D]
        s_a, s_b = _sigmoid(ga), _sigmoid(gb)
        dp_ref[:, D:2 * D] = (dya_pre * h * (s_a * (1.0 + ga * (1.0 - s_a)))).astype(BF16)
        dp_ref[:, 5 * D:6 * D] = (dyb_pre * (gbp * cz) * (s_b * (1.0 + gb * (1.0 - s_b)))).astype(BF16)
        dya = dya_pre * (ga * s_a)
        dyb = dyb_pre * (gb * s_b)
        wgrad_out(1)
        dp_ref[:, 2 * D:3 * D] = (dyb * cz).astype(BF16)
        dcz = dyb * gbp
        for k in range(3):
            sm_ref[8 + k:9 + k, :] += jnp.sum(dcz * ztaps[k], axis=0, keepdims=True)
        dcz_ext = jnp.concatenate([dcz, cdcz[...]], axis=0)
        dz = sum(sw_ref[k:k + 1, :] * _up(dcz_ext, 2 - k)[0:ts] for k in range(3))
        dp_ref[:, 3 * D:4 * D] = (dz * v).astype(BF16)
        dp_ref[:, 4 * D:5 * D] = (dz * gcp).astype(BF16)
        cdcz[...] = dcz[0:hl, :]
        alpha = _up(jnp.concatenate([a, ca[...]], axis=0), 1)[0:ts]
        wgrad_out(2)
        dh = _run(_scan_rev_steps(alpha, dya, cg[0:1, :]))
        wgrad_out(3)
        cg[...] = dh[0:hl, :]
        ca[...] = a[0:hl, :]
        da = dh * hprev
        dm = dh * ig * xc
        di = dh * m * xc
        dxc = dh * m * ig
        dl = da * a - jnp.where(first, 0.0, dm * (a * a) * inv_m)
        sm_ref[7:8, :] += jnp.sum(dl * r, axis=0, keepdims=True) * (-LRU_C)
        dpa = (dl * sp) * (-LRU_C) * r * (1.0 - r)
        dpx = di * ig * (1.0 - ig)
        sm_ref[5:6, :] += jnp.sum(dpa, axis=0, keepdims=True)
        sm_ref[6:7, :] += jnp.sum(dpx, axis=0, keepdims=True)
        dpa_b, dpx_b, xc_b = dpa.astype(BF16), dpx.astype(BF16), xc.astype(BF16)
        back = []
        for hd in range(LRU_HEADS):
            sl = slice(hd * LRU_HEAD_DIM, (hd + 1) * LRU_HEAD_DIM)
            back.append(lax.dot_general(dpa_b[:, sl], wa_ref[hd], NT, preferred_element_type=F32)
                        + lax.dot_general(dpx_b[:, sl], wx_ref[hd], NT, preferred_element_type=F32))
            dwa_ref[hd] += lax.dot_general(xc_b[:, sl], dpa_b[:, sl], TN, preferred_element_type=F32)
            dwx_ref[hd] += lax.dot_general(xc_b[:, sl], dpx_b[:, sl], TN, preferred_element_type=F32)
        dxc = dxc + jnp.concatenate(back, axis=1)
        sm_ref[4:5, :] += jnp.sum(dxc, axis=0, keepdims=True)
        for k in range(4):
            sm_ref[k:k + 1, :] += jnp.sum(dxc * xtaps[k], axis=0, keepdims=True)
        dxc_ext = jnp.concatenate([dxc, cdxc[...]], axis=0)
        dp_ref[:, 0:D] = sum(cw_ref[k:k + 1, :] * _up(dxc_ext, 3 - k)[0:ts] for k in range(4)).astype(BF16)
        cdxc[...] = dxc[0:hl, :]

    def full(a):
        return pl.BlockSpec(a.shape, lambda i: (0,) * a.ndim)

    rev = lambda w: pl.BlockSpec((ts, w), lambda i: (n_t - 1 - i, 0))
    halo = lambda w: pl.BlockSpec((hb, w), lambda i: (jnp.maximum((n_t - 1 - i) * (ts // hb) - 1, 0), 0))
    return pl.pallas_call(
        body, name="l0_bwd_mix", grid=(n_t,),
        in_specs=[rev(D), rev(6 * D), halo(6 * D), rev(D), halo(D), rev(2 * D)]
        + [full(a) for a in (gate, cw, cb, wa, ba, wx, bx, lam, sw, wo)],
        out_specs=[rev(6 * D), pl.BlockSpec((N_CHIP, yb_w, D), lambda i: (0, 0, 0)),
                   pl.BlockSpec(wa.shape, lambda i: (0, 0, 0)), pl.BlockSpec(wa.shape, lambda i: (0, 0, 0)),
                   pl.BlockSpec((2 * SUBLANES, D), lambda i: (0, 0))],
        out_shape=[jax.ShapeDtypeStruct((s_len, 6 * D), BF16), jax.ShapeDtypeStruct((N_CHIP, yb_w, D), F32),
                   jax.ShapeDtypeStruct(wa.shape, F32), jax.ShapeDtypeStruct(wa.shape, F32),
                   jax.ShapeDtypeStruct((2 * SUBLANES, D), F32)],
        scratch_shapes=[pltpu.VMEM((hl, D), F32)] * 4,
        compiler_params=_cp(("arbitrary",)),
    )(dx1, proj, proj, hst, hst, y, gate, cw, cb, wa, ba, wx, bx, lam, sw, wo)


def _dgrad_norm(dproj, w, x, dres, g, sc, name, after=None):
    s_len, nb = x.shape[0], w.shape[2]
    ts = _tile(s_len, TS_DGRAD)
    order = [] if after is None else [after]

    def body(dp_ref, w_ref, x_ref, dr_ref, g_ref, sc_ref, *rest):
        dx_ref, s1_ref, s2_ref = rest[len(order):]

        @pl.when(pl.program_id(0) == 0)
        def _():
            s1_ref[...] = jnp.zeros_like(s1_ref)
            s2_ref[...] = jnp.zeros_like(s2_ref)

        dh = sum(lax.dot_general(dp_ref[:, k * nb:(k + 1) * nb], w_ref[k], NT, preferred_element_type=F32)
                 for k in range(N_CHIP))
        xv = x_ref[...]
        r = lax.rsqrt(jnp.mean(xv * xv, axis=-1, keepdims=True) + RMS_EPS)
        n = xv * r
        s1_ref[...] += jnp.sum(dh, axis=0, keepdims=True)
        s2_ref[...] += jnp.sum(dh * n, axis=0, keepdims=True)
        dn = dh * (g_ref[...] * (1.0 + sc_ref[...]))
        dx_ref[...] = dr_ref[...] + r * (dn - n * jnp.mean(dn * n, axis=-1, keepdims=True))

    row = lambda wd: pl.BlockSpec((ts, wd), lambda i: (i, 0))
    vec = pl.BlockSpec((1, D), lambda i: (0, 0))
    return pl.pallas_call(
        body, name=name, grid=(s_len // ts,),
        in_specs=[row(N_CHIP * nb), pl.BlockSpec(w.shape, lambda i: (0, 0, 0)), row(D), row(D), vec, vec]
        + [ANY] * len(order),
        out_specs=[row(D), vec, vec],
        out_shape=[jax.ShapeDtypeStruct((s_len, D), F32)] + [jax.ShapeDtypeStruct((1, D), F32)] * 2,
        compiler_params=_cp(("arbitrary",)),
    )(dproj, w, x, dres, g, sc, *order)


def _wgrad(a, b, groups, ka, nb, a_col, b_col, name, after=None):
    s_len = a.shape[0]
    ts = _tile(s_len, TS_WGRAD)
    order = [] if after is None else [after]

    def body(a_ref, b_ref, *rest):
        o_ref = rest[-1]

        @pl.when(pl.program_id(1) == 0)
        def _():
            o_ref[...] = jnp.zeros_like(o_ref)

        o_ref[...] += lax.dot_general(a_ref[...].astype(BF16), b_ref[...].astype(BF16), TN, preferred_element_type=F32)

    return pl.pallas_call(
        body, name=name, grid=(groups, s_len // ts),
        in_specs=[pl.BlockSpec((ts, ka), lambda g, s: (s, a_col(g))), pl.BlockSpec((ts, nb), lambda g, s: (s, b_col(g)))]
        + [ANY] * len(order),
        out_specs=pl.BlockSpec((None, ka, nb), lambda g, s: (g, 0, 0)),
        out_shape=jax.ShapeDtypeStruct((groups, ka, nb), F32),
        compiler_params=_cp(("parallel", "arbitrary")),
    )(a, b, *order)


def _wo_final(mt, wo, gate, name):
    rb = mt.shape[1]

    def body(m_ref, w_ref, gate_ref, dw_ref, dg_ref):
        @pl.when(pl.program_id(0) == 0)
        def _():
            dg_ref[...] = jnp.zeros_like(dg_ref)

        mv = m_ref[...]
        dw_ref[...] = mv * gate_ref[...]
        dg_ref[...] += jnp.sum(mv * w_ref[...].astype(F32), axis=0, keepdims=True)

    blk = pl.BlockSpec((None, rb, D), lambda k: (k, 0, 0))
    vec = pl.BlockSpec((1, D), lambda k: (0, 0))
    return pl.pallas_call(
        body, name=name, grid=(N_CHIP,), in_specs=[blk, blk, vec], out_specs=[blk, vec],
        out_shape=[jax.ShapeDtypeStruct(mt.shape, F32), jax.ShapeDtypeStruct((1, D), F32)],
        compiler_params=_cp(("arbitrary",)),
    )(mt, wo, gate)


ROW_NORM_G, ROW_CONV_W, ROW_CONV_B, ROW_B_A, ROW_B_X, ROW_LAMBDA, ROW_SC_W, ROW_POOL_B, ROW_POOL_S, ROW_FINAL_G = (
    0, 2, 6, 7, 8, 9, 10, 13, 15, 17)


def _small_pack(s1_0, s2_0, s1_1, s2_1, sm0, dsc1, dbg1, dgf, losscols, dgate0, dgate1, norm_g, sc0, sc1, lam):
    def body(s1_0r, s2_0r, s1_1r, s2_1r, sm, dsc, dbg, dgfr, lcols, dg0, dg1, ng, sc0r, sc1r, lamr, buf, dmod, loss):
        buf[...] = jnp.zeros_like(buf)
        buf[0:1, :] = s2_0r[...] * (1.0 + sc0r[...])
        buf[1:2, :] = s2_1r[...] * (1.0 + sc1r[...])
        buf[ROW_CONV_W:ROW_CONV_W + 4, :] = sm[0:4, :]
        buf[ROW_CONV_B:ROW_CONV_B + 1, :] = sm[4:5, :]
        buf[ROW_B_A:ROW_B_A + 1, :] = sm[5:6, :]
        buf[ROW_B_X:ROW_B_X + 1, :] = sm[6:7, :]
        buf[ROW_LAMBDA:ROW_LAMBDA + 1, :] = -sm[7:8, :] * _sigmoid(-lamr[...])
        buf[ROW_SC_W:ROW_SC_W + 3, :] = sm[8:11, :]
        for k in range(2):
            buf[ROW_POOL_B + k:ROW_POOL_B + k + 1, :] = dbg[:, k * D:(k + 1) * D]
            buf[ROW_POOL_S + k:ROW_POOL_S + k + 1, :] = dsc[:, k * D:(k + 1) * D]
        buf[ROW_FINAL_G:ROW_FINAL_G + 1, :] = dgfr[...]
        pieces = (s1_0r[...], s2_0r[...] * ng[0:1, :], dg0[...], s1_1r[...], s2_1r[...] * ng[1:2, :], dg1[...])
        for k, pc in enumerate(pieces):
            dmod[:, k * D:(k + 1) * D] = jnp.broadcast_to(pc, (SUBLANES, D))
        loss[...] = jnp.broadcast_to(jnp.sum(lcols[...], axis=1, keepdims=True) * (0.5 / D), loss.shape)

    args = (s1_0, s2_0, s1_1, s2_1, sm0, dsc1, dbg1, dgf, losscols, dgate0, dgate1, norm_g, sc0, sc1, lam)
    return pl.pallas_call(
        body, name="small_pack", in_specs=[VMEM] * len(args), out_specs=[VMEM] * 3,
        out_shape=[jax.ShapeDtypeStruct((SMALL_ROWS, D), F32), jax.ShapeDtypeStruct((SUBLANES, 6 * D), F32),
                   jax.ShapeDtypeStruct((SUBLANES, 128), F32)],
        compiler_params=_cp(),
    )(*args)


def _small_comm(buf_a, buf_b, dmod8):
    ra, rb = buf_a.shape[0] // N_DEV, buf_b.shape[0] // N_DEV
    wb = buf_b.shape[1]

    def body(a_ref, b_ref, dm_ref, oa_ref, ob_ref, odm_ref, ina, inb, dslot, sa, sb, s1, r1, s2, r2):
        x, y, c = _pos()
        me = 4 * x + 2 * y + c
        peers = []
        for r in range(1, N_DEV):
            fx, fy, fc = (r >> 2) & 1, (r >> 1) & 1, r & 1
            px, py, pc = _flip(x, fx), _flip(y, fy), _flip(c, fc)
            peers.append(((px, py, pc), 4 * px + 2 * py + pc))
        seg_a = lambda d: pl.ds(pl.multiple_of(d * ra, SUBLANES), ra)
        seg_b = lambda d: pl.ds(pl.multiple_of(d * rb, SUBLANES), rb)
        first = []
        for r, (peer, pid) in enumerate(peers):
            for k, (src, dst) in enumerate(((a_ref.at[seg_a(pid), :], ina.at[r]), (b_ref.at[seg_b(pid), :], inb.at[r]),
                                            (dm_ref, dslot.at[me]))):
                cp = pltpu.make_async_remote_copy(src_ref=src, dst_ref=dst, send_sem=s1.at[3 * r + k],
                                                  recv_sem=r1.at[3 * r + k], device_id=peer, device_id_type=MESH)
                cp.start()
                first.append(cp)
        dslot[me] = dm_ref[...]
        for cp in first:
            cp.wait()
        acc_a, acc_b = a_ref[seg_a(me), :], b_ref[seg_b(me), :]
        for r in range(N_DEV - 1):
            acc_a = acc_a + ina[r]
            acc_b = acc_b + inb[r]
        sa[...] = acc_a
        sb[...] = acc_b
        oa_ref[seg_a(me), :] = acc_a
        ob_ref[seg_b(me), :] = acc_b
        second = []
        for r, (peer, pid) in enumerate(peers):
            for k, (src, dst) in enumerate(((sa, oa_ref.at[seg_a(me), :]), (sb, ob_ref.at[seg_b(me), :]))):
                cp = pltpu.make_async_remote_copy(src_ref=src, dst_ref=dst, send_sem=s2.at[2 * r + k],
                                                  recv_sem=r2.at[2 * r + k], device_id=peer, device_id_type=MESH)
                cp.start()
                second.append(cp)
        rows = _rows(SUBLANES, dm_ref.shape[1])
        dm_all = jnp.zeros(dm_ref.shape, F32)
        for d in range(N_DEV):
            dm_all = jnp.where(rows == d, dslot[d], dm_all)
        odm_ref[...] = dm_all
        for cp in second:
            cp.wait()

    nrel = N_DEV - 1
    return pl.pallas_call(
        body, name="small_comm", in_specs=[VMEM] * 3, out_specs=[VMEM] * 3,
        out_shape=[jax.ShapeDtypeStruct(buf_a.shape, F32), jax.ShapeDtypeStruct(buf_b.shape, F32),
                   jax.ShapeDtypeStruct(dmod8.shape, F32)],
        scratch_shapes=[pltpu.VMEM((nrel, ra, D), F32), pltpu.VMEM((nrel, rb, wb), F32),
                        pltpu.VMEM((N_DEV,) + dmod8.shape, F32), pltpu.VMEM((ra, D), F32), pltpu.VMEM((rb, wb), F32),
                        pltpu.SemaphoreType.DMA((3 * nrel,)), pltpu.SemaphoreType.DMA((3 * nrel,)),
                        pltpu.SemaphoreType.DMA((2 * nrel,)), pltpu.SemaphoreType.DMA((2 * nrel,))],
        compiler_params=_cp(),
    )(buf_a, buf_b, dmod8)


def _adam(w, g, m, v):
    m2 = ADAM_B1 * m + (1.0 - ADAM_B1) * g
    v2 = ADAM_B2 * v + (1.0 - ADAM_B2) * (g * g)
    m_hat = m2 / (1.0 - ADAM_B1 ** ADAM_STEP)
    v_hat = v2 / (1.0 - ADAM_B2 ** ADAM_STEP)
    return -ADAM_LR * (m_hat / (jnp.sqrt(v_hat) + ADAM_EPS) + ADAM_WD * w), m2, v2


def _small_adam(red_a, red_b, dm_all, params):
    n = len(params)

    def body(*refs):
        ra, rb, dm = refs[:3]
        wmv = refs[3:3 + 3 * n]
        outs = refs[3 + 3 * n:]
        x, y, _ = _pos()
        chip = 2 * x + y

        def shard(row0, nrows, width):
            per_row = D // width
            cands = []
            for k in range(N_CHIP):
                if nrows == 1 or per_row >= N_CHIP:
                    cands.append(ra[row0:row0 + nrows, k * width:(k + 1) * width])
                else:
                    rr, cc = divmod(k * width, D)
                    cands.append(ra[row0 + rr:row0 + rr + 1, cc:cc + width])
            g = cands[0]
            for k in range(1, N_CHIP):
                g = jnp.where(chip == k, cands[k], g)
            return g

        dms = jnp.sum(dm[...], axis=0, keepdims=True)
        hw = LRU_HEADS * LRU_HEAD_DIM
        grads = [
            ra[ROW_NORM_G:ROW_NORM_G + 2, :],
            None,
            shard(ROW_CONV_W, 4, D // N_CHIP),
            ra[ROW_CONV_B:ROW_CONV_B + 1, :],
            rb[0:hw, :],
            ra[ROW_B_A:ROW_B_A + 1, :],
            rb[hw:2 * hw, :],
            ra[ROW_B_X:ROW_B_X + 1, :],
            ra[ROW_LAMBDA:ROW_LAMBDA + 1, :],
            shard(ROW_SC_W, 3, D // N_CHIP),
            shard(ROW_POOL_B, 2, 2 * D // N_CHIP),
            shard(ROW_POOL_S, 2, 2 * D // N_CHIP),
            ra[ROW_FINAL_G:ROW_FINAL_G + 1, :],
        ]
        for p in range(n):
            w_ref, m_ref, v_ref = wmv[3 * p:3 * p + 3]
            g_out, d_out, m_out, v_out = outs[4 * p:4 * p + 4]
            if grads[p] is None:
                for l in range(2):
                    g = dms[:, l * 3 * D:(l + 1) * 3 * D]
                    dl, m2, v2 = _adam(w_ref[l:l + 1, :], g, m_ref[l:l + 1, :], v_ref[l:l + 1, :])
                    g_out[l:l + 1, :] = g
                    d_out[l:l + 1, :] = dl
                    m_out[l:l + 1, :] = m2
                    v_out[l:l + 1, :] = v2
            else:
                g = grads[p]
                dl, m2, v2 = _adam(w_ref[...], g, m_ref[...], v_ref[...])
                g_out[...] = g
                d_out[...] = dl
                m_out[...] = m2
                v_out[...] = v2

    flat = [a for p in params for a in p]
    return pl.pallas_call(
        body, name="small_adam", in_specs=[VMEM] * (3 + len(flat)), out_specs=[VMEM] * (4 * n),
        out_shape=[jax.ShapeDtypeStruct(p[0].shape, F32) for p in params for _ in range(4)],
        compiler_params=_cp(),
    )(red_a, red_b, dm_all, *flat)


def _modw_adam(ca_t, dm_sh, w, m, v):
    nw = w.shape[2]

    def body(c_ref, d_ref, w_ref, m_ref, v_ref, g_out, d_out, m_out, v_out):
        g = jnp.dot(c_ref[...], d_ref[...], precision=lax.Precision.HIGHEST, preferred_element_type=F32)
        dl, m2, v2 = _adam(w_ref[...], g, m_ref[...], v_ref[...])
        g_out[...] = g
        d_out[...] = dl
        m_out[...] = m2
        v_out[...] = v2

    blk = pl.BlockSpec((None, D, nw), lambda l: (l, 0, 0))
    return pl.pallas_call(
        body, name="modw_adam", grid=(2,),
        in_specs=[pl.BlockSpec((D, SUBLANES), lambda l: (0, 0)), pl.BlockSpec((None, SUBLANES, nw), lambda l: (l, 0, 0)),
                  blk, blk, blk],
        out_specs=[blk] * 4, out_shape=[jax.ShapeDtypeStruct(w.shape, F32)] * 4,
        compiler_params=_cp(("arbitrary",)),
    )(ca_t, dm_sh, w, m, v)


def _half_rows(r):
    return r // 2


def _exchange(copies, name, out_type, n_sems, args, sequencer, after=None):
    order = [] if after is None else [after]
    n_in, n_out = len(args) + len(order), len(out_type)

    def body(*refs):
        barrier = pltpu.get_barrier_semaphore()
        peers = sequencer[1](*_pos())
        for peer in peers:
            pl.semaphore_signal(barrier, inc=1, device_id=peer, device_id_type=MESH)
        pl.semaphore_wait(barrier, len(peers))
        copies(refs[:n_in], refs[n_in:n_in + n_out], refs[n_in + n_out], refs[n_in + n_out + 1])

    sems = [pltpu.SemaphoreType.DMA((n_sems,))] * 2
    return pl.kernel(body, out_type, mesh=plsc.ScalarSubcoreMesh(axis_name="sequencer", num_cores=1), name=name,
                     scratch_types=sems, compiler_params=pltpu.CompilerParams(collective_id=sequencer[0]))(*args, *order)


def _sibling(x, y, c):
    return [(x, y, 1 - c)]


def _other_chips(x, y, c):
    return [(1 - x, y, c), (x, 1 - y, c), (1 - x, 1 - y, c)]


def _sib_send_halves(gs, name, collective_id, after=None):
    n = len(gs)

    def copies(ins, outs, ssem, rsem):
        x, y, c = _pos()
        cps = []
        for a in range(n):
            hr = _half_rows(gs[a].shape[1])
            cp = pltpu.make_async_remote_copy(
                src_ref=ins[a].at[:, pl.ds(pl.multiple_of((1 - c) * hr, SUBLANES), hr), :], dst_ref=outs[a],
                send_sem=ssem.at[a], recv_sem=rsem.at[a], device_id=(x, y, 1 - c), device_id_type=MESH)
            cp.start()
            cps.append(cp)
        for cp in cps:
            cp.wait()

    out_type = [jax.ShapeDtypeStruct((N_CHIP, _half_rows(g.shape[1]), g.shape[2]), F32) for g in gs]
    return _exchange(copies, name, out_type, n, gs, (collective_id, _sibling), after)


def _add_half(g, got, cidx, name, after=None):
    _, hr, cc = got.shape
    rb = min(hr, 256)

    def body(c_ref, g_ref, r_ref, *rest):
        rest[-1][...] = (g_ref[...] + r_ref[...]).astype(rest[-1].dtype)

    order = [] if after is None else [after]
    blk = pl.BlockSpec((None, rb, cc), lambda k, j, c_ref: (k, j, 0))
    return pl.pallas_call(
        body, name=name,
        grid_spec=pltpu.PrefetchScalarGridSpec(
            num_scalar_prefetch=1, grid=(N_CHIP, hr // rb),
            in_specs=[pl.BlockSpec((None, rb, cc), lambda k, j, c_ref: (k, c_ref[0] * (hr // rb) + j, 0)), blk]
            + [ANY] * len(order),
            out_specs=blk),
        out_shape=jax.ShapeDtypeStruct(got.shape, GRAD_WIRE_DTYPE),
        compiler_params=_cp(("parallel", "parallel")),
    )(cidx, g, got, *order)


def _chip_scatter(ps, name, collective_id, after=None):
    n = len(ps)

    def copies(ins, outs, ssem, rsem):
        x, y, c = _pos()
        cps = []
        for a in range(n):
            for q, (fx, fy) in enumerate(((1, 0), (0, 1), (1, 1))):
                px, py = _flip(x, fx), _flip(y, fy)
                cp = pltpu.make_async_remote_copy(
                    src_ref=ins[a].at[2 * px + py], dst_ref=outs[a].at[q],
                    send_sem=ssem.at[3 * a + q], recv_sem=rsem.at[3 * a + q], device_id=(px, py, c), device_id_type=MESH)
                cp.start()
                cps.append(cp)
        for cp in cps:
            cp.wait()

    out_type = [jax.ShapeDtypeStruct((N_CHIP - 1,) + p.shape[1:], p.dtype) for p in ps]
    return _exchange(copies, name, out_type, 3 * n, ps, (collective_id, _other_chips), after)


def _add_owner(p, got, chipidx, name, after=None):
    _, hr, cc = p.shape
    rb = min(hr, 256)

    def body(k_ref, p_ref, r_ref, *rest):
        rest[-1][...] = ((p_ref[...].astype(F32) + r_ref[0].astype(F32)) + r_ref[1].astype(F32)) + r_ref[2].astype(F32)

    order = [] if after is None else [after]
    return pl.pallas_call(
        body, name=name,
        grid_spec=pltpu.PrefetchScalarGridSpec(
            num_scalar_prefetch=1, grid=(hr // rb,),
            in_specs=[pl.BlockSpec((None, rb, cc), lambda j, k_ref: (k_ref[0], j, 0)),
                      pl.BlockSpec((N_CHIP - 1, rb, cc), lambda j, k_ref: (0, j, 0))] + [ANY] * len(order),
            out_specs=pl.BlockSpec((rb, cc), lambda j, k_ref: (j, 0))),
        out_shape=jax.ShapeDtypeStruct((hr, cc), F32),
        compiler_params=_cp(("parallel",)),
    )(chipidx, p, got, *order)


def _sib_exchange(ts_, name, collective_id, after=None):
    n = len(ts_)

    def copies(ins, outs, ssem, rsem):
        x, y, c = _pos()
        cps = []
        for a in range(n):
            cp = pltpu.make_async_remote_copy(src_ref=ins[a], dst_ref=outs[a], send_sem=ssem.at[a],
                                              recv_sem=rsem.at[a], device_id=(x, y, 1 - c), device_id_type=MESH)
            cp.start()
            cps.append(cp)
        for cp in cps:
            cp.wait()

    out_type = [jax.ShapeDtypeStruct(t.shape, F32) for t in ts_]
    return _exchange(copies, name, out_type, n, ts_, (collective_id, _sibling), after)


def _adam_2d(w, g_own, g_sib, m, v, cidx, name):
    rr, cc = w.shape
    hr = rr // 2
    rb = min(hr, 256)
    nb = hr // rb

    def body(c_ref, w_ref, go_ref, gs_ref, m_ref, v_ref, g_out, d_out, m_out, v_out):
        g = jnp.where(pl.program_id(0) == c_ref[0], go_ref[...], gs_ref[...])
        dl, m2, v2 = _adam(w_ref[...], g, m_ref[...], v_ref[...])
        g_out[...] = g
        d_out[...] = dl
        m_out[...] = m2
        v_out[...] = v2

    blk = pl.BlockSpec((rb, cc), lambda h, j, c_ref: (h * nb + j, 0))
    hblk = pl.BlockSpec((rb, cc), lambda h, j, c_ref: (j, 0))
    return pl.pallas_call(
        body, name=name,
        grid_spec=pltpu.PrefetchScalarGridSpec(
            num_scalar_prefetch=1, grid=(2, nb), in_specs=[blk, hblk, hblk, blk, blk], out_specs=[blk] * 4),
        out_shape=[jax.ShapeDtypeStruct((rr, cc), F32)] * 4, compiler_params=_cp(("parallel", "parallel")),
    )(cidx, w, g_own, g_sib, m, v)


def kernel(x, c, norm_g, mod_w, mod_b, hy_w_in, hy_conv_w, hy_conv_b, lru_w_a, lru_b_a, lru_w_x, lru_b_x, lru_lambda, sc_conv_w, hy_w_out, pool_w_in, pool_w_grp, pool_b_grp, pool_scale, pool_w_out, final_g, loss_target, m_norm_g, m_mod_w, m_mod_b, m_hy_w_in, m_hy_conv_w, m_hy_conv_b, m_lru_w_a, m_lru_b_a, m_lru_w_x, m_lru_b_x, m_lru_lambda, m_sc_conv_w, m_hy_w_out, m_pool_w_in, m_pool_w_grp, m_pool_b_grp, m_pool_scale, m_pool_w_out, m_final_g, v_norm_g, v_mod_w, v_mod_b, v_hy_w_in, v_hy_conv_w, v_hy_conv_b, v_lru_w_a, v_lru_b_a, v_lru_w_x, v_lru_b_x, v_lru_lambda, v_sc_conv_w, v_hy_w_out, v_pool_w_in, v_pool_w_grp, v_pool_b_grp, v_pool_scale, v_pool_w_out, v_final_g):
    ax, ay, ac = _pos()
    me = 4 * ax + 2 * ay + ac
    chip = 2 * ax + ay
    xs = x[0]
    tgt = loss_target[0]
    gd = POOL_GROUP_DIM

    ca_all, mod_all, small_w = _mod_fwd(jnp.broadcast_to(c, (SUBLANES, D)), mod_w, mod_b,
                                        hy_conv_w[0], sc_conv_w[0], pool_b_grp, pool_scale)
    mod_me = lax.dynamic_index_in_dim(mod_all, me, axis=1, keepdims=False)
    sh0, sc0, gt0 = (mod_me[0:1, k * D:(k + 1) * D] for k in range(3))
    sh1, sc1, gt1 = (mod_me[1:2, k * D:(k + 1) * D] for k in range(3))
    cw = small_w[SW_CONV:SW_CONV + 4, 0:D]
    sw = small_w[SW_SC:SW_SC + 3, 0:D]
    pool_b = small_w[SW_POOL_B:SW_POOL_B + 1, :]
    pool_s = small_w[SW_POOL_S:SW_POOL_S + 1, :]
    g0, g1, gf = norm_g[0:1], norm_g[1:2], final_g.reshape(1, D)
    cb, ba, bx, lam = hy_conv_b, lru_b_a, lru_b_x, lru_lambda

    big = [hy_w_in[0], hy_w_out[0], pool_w_in[0], pool_w_grp[0].reshape(4 * 128, gd), pool_w_out[0]]
    cidx = ac.reshape(1).astype(jnp.int32)
    kidx = chip.reshape(1).astype(jnp.int32)
    w_in0, w_out0 = _wgather([_wcast_own_block(w, kidx, f"wcast_own_block_{a}") for a, w in enumerate(big[:2])],
                             "wgather_l0")
    w_in1, w_grp, w_out1 = _wgather_sequencer(
        [_wcast_own_block(w, kidx, f"wcast_own_block_{a + 2}", after=w_out0) for a, w in enumerate(big[2:])], "wgather_l1")
    w_grp =w_grp.reshape(N_CHIP, 4, 128, gd).transpose(1, 0, 2, 3).reshape(4, gd, gd)
    wa_b, wx_b = _wcast([lru_w_a[0], lru_w_x[0]])

    x1, hst, y0, h0, proj0 = _l0_fwd(xs, g0, sc0, sh0, w_in0, gt0, cw, cb, wa_b, ba, wx_b, bx, lam, sw,
                                     w_out0.reshape(2 * D, D))
    h1, proj1 = _norm_proj(x1, g1, sc1, sh1, w_in1, "l1_proj")
    dpool, mixed, y1, dx2, losscols, dgf = _l1_mix(proj1, x1, tgt, gt1, w_grp, pool_b, pool_s,
                                                    w_out1.reshape(2 * D, D), gf)

    def add_halves(grads, got, tag, after):
        parts = []
        for a, (g, r) in enumerate(zip(grads, got)):
            parts.append(_add_half(g, r, cidx, f"grad_add_half_{tag}{a}", parts[-1] if parts else after))
        return parts

    def add_owners(parts, got, tag, ids, after):
        own = []
        for a, (p, r) in enumerate(zip(parts, got)):
            own.append(_add_owner(p, r, kidx, f"grad_add_owner_{tag}{a}", own[-1] if own else after))
        return own, _sib_exchange(own, f"grad_sib_exchange_{tag}", ids[2])

    dproj1, mt1, d_wgrp, dsc1, dbg1 = _l1_bwd_mix(dx2, proj1, mixed, y1, dpool, gt1, w_grp, pool_s,
                                                  w_out1.reshape(2 * D, D))
    d_win1 = _wgrad(h1, dproj1, N_CHIP, D, D, lambda g: 0, lambda g: g, "l1_wgrad_in")
    dx1, s1_1, s2_1 = _dgrad_norm(dproj1, w_in1, x1, dx2, g1, sc1, "l1_bwd_proj")
    d_wout1, dgate1 = _wo_final(mt1, w_out1, gt1, "l1_wo_final")
    d_wgrp = d_wgrp.reshape(4, N_CHIP, 128, gd).transpose(1, 0, 2, 3).reshape(N_CHIP, 4 * 128, gd)
    grads_l1 = [d_win1, d_wgrp, d_wout1]
    got_l1 = _sib_send_halves(grads_l1, "grad_sib_halves_l1", CIDS_L1[0])

    dproj0, mt0, d_wa, d_wx, sm0 = _l0_bwd_mix(dx1, proj0, hst, y0, gt0, cw, cb, wa_b, ba, wx_b, bx, lam, sw,
                                               w_out0.reshape(2 * D, D))
    parts_l1 = add_halves(grads_l1, got_l1, "l1", after=sm0)
    got_l1 = _chip_scatter(parts_l1, "grad_chip_scatter_l1", CIDS_L1[1])
    d_win0 = _wgrad(h0, dproj0, N_CHIP, D, 6 * D // N_CHIP, lambda g: 0, lambda g: g, "l0_wgrad_in", after=parts_l1[-1])
    d_wout0, dgate0 = _wo_final(mt0, w_out0, gt0, "l0_wo_final")
    halves_l1, sib_l1 = add_owners(parts_l1, got_l1, "l1", CIDS_L1, after=d_win0)
    grads_l0 = [d_win0, d_wout0]
    parts_l0 = add_halves(grads_l0, _sib_send_halves(grads_l0, "grad_sib_halves_l0", CIDS_L0[0], after=got_l1[0]),
                          "l0", after=None)
    got_l0 = _chip_scatter(parts_l0, "grad_chip_scatter_l0", CIDS_L0[1])
    grad_x, s1_0, s2_0 = _dgrad_norm(dproj0, w_in0, xs, dx1, g0, sc0, "l0_bwd_proj", after=parts_l0[0])
    halves_l0, sib_l0 = add_owners(parts_l0, got_l0, "l0", CIDS_L0, after=s1_0)

    buf_a, dmod8, loss8 = _small_pack(s1_0, s2_0, s1_1, s2_1, sm0, dsc1, dbg1, dgf, losscols, dgate0, dgate1,
                                      norm_g, sc0, sc1, lam)
    hw = LRU_HEADS * LRU_HEAD_DIM
    buf_b = jnp.concatenate([d_wa.reshape(hw, LRU_HEAD_DIM), d_wx.reshape(hw, LRU_HEAD_DIM)], axis=0)
    red_a, red_b, dm_all = _small_comm(buf_a, buf_b, dmod8)
    small = [(norm_g, m_norm_g, v_norm_g), (mod_b, m_mod_b, v_mod_b),
             (hy_conv_w[0], m_hy_conv_w[0], v_hy_conv_w[0]), (hy_conv_b, m_hy_conv_b, v_hy_conv_b),
             tuple(a.reshape(hw, LRU_HEAD_DIM) for a in (lru_w_a, m_lru_w_a, v_lru_w_a)),
             (lru_b_a, m_lru_b_a, v_lru_b_a),
             tuple(a.reshape(hw, LRU_HEAD_DIM) for a in (lru_w_x, m_lru_w_x, v_lru_w_x)),
             (lru_b_x, m_lru_b_x, v_lru_b_x), (lru_lambda, m_lru_lambda, v_lru_lambda),
             (sc_conv_w[0], m_sc_conv_w[0], v_sc_conv_w[0]), (pool_b_grp, m_pool_b_grp, v_pool_b_grp),
             (pool_scale, m_pool_scale, v_pool_scale),
             tuple(a.reshape(1, D) for a in (final_g, m_final_g, v_final_g))]
    small_names = ["norm_g", "mod_b", "hy_conv_w", "hy_conv_b", "lru_w_a", "lru_b_a", "lru_w_x", "lru_b_x",
                   "lru_lambda", "sc_conv_w", "pool_b_grp", "pool_scale", "final_g"]
    small_out = _small_adam(red_a, red_b, dm_all, small)
    res = {}
    shapes = dict(norm_g=norm_g, mod_b=mod_b, hy_conv_w=hy_conv_w, hy_conv_b=hy_conv_b, lru_w_a=lru_w_a, lru_b_a=lru_b_a,
                  lru_w_x=lru_w_x, lru_b_x=lru_b_x, lru_lambda=lru_lambda, sc_conv_w=sc_conv_w, pool_b_grp=pool_b_grp,
                  pool_scale=pool_scale, final_g=final_g)
    for p, nm in enumerate(small_names):
        res[nm] = tuple(o.reshape(shapes[nm].shape) for o in small_out[4 * p:4 * p + 4])

    nw = mod_w.shape[2]
    dm_sh = jnp.stack([lax.dynamic_slice_in_dim(dm_all[:, l * 3 * D:(l + 1) * 3 * D], chip * nw, nw, axis=1)
                       for l in range(2)])
    res["mod_w"] = tuple(_modw_adam(ca_all.T, dm_sh, mod_w, m_mod_w, v_mod_w))

    halves = list(halves_l0) + list(halves_l1)
    sib_halves = list(sib_l0) + list(sib_l1)
    big_names = ["hy_w_in", "hy_w_out", "pool_w_in", "pool_w_grp", "pool_w_out"]
    big_wmv = [(hy_w_in, m_hy_w_in, v_hy_w_in), (hy_w_out, m_hy_w_out, v_hy_w_out), (pool_w_in, m_pool_w_in, v_pool_w_in),
               (pool_w_grp, m_pool_w_grp, v_pool_w_grp), (pool_w_out, m_pool_w_out, v_pool_w_out)]
    for a, nm in enumerate(big_names):
        rr, cc = big[a].shape
        w, m, v = (t.reshape(rr, cc) for t in big_wmv[a])
        outs = _adam_2d(w, halves[a], sib_halves[a], m, v, cidx, f"adam_{nm}")
        res[nm] = tuple(o.reshape(big_wmv[a][0].shape) for o in outs)

    loss = lax.psum(loss8[0, 0], ("x", "y", "c"))
    order = ["norm_g", "mod_w", "mod_b", "hy_w_in", "hy_conv_w", "hy_conv_b", "lru_w_a", "lru_b_a", "lru_w_x", "lru_b_x",
             "lru_lambda", "sc_conv_w", "hy_w_out", "pool_w_in", "pool_w_grp", "pool_b_grp", "pool_scale", "pool_w_out",
             "final_g"]
    return (loss, grad_x[None], *[res[nm][0] for nm in order], *[res[nm][1] for nm in order],
            *[res[nm][2] for nm in order], *[res[nm][3] for nm in order])
```

```python
import jax
import jax.numpy as jnp
from jax import lax
from jax.experimental import pallas as pl
from jax.experimental.pallas import tpu as pltpu
from jax.experimental.pallas import tpu_sc as plsc

F32, BF16 = jnp.float32, jnp.bfloat16
D = 1024
RMS_EPS = 1e-6
SQRT_FLOOR = 1e-30
LRU_C = 8.0
LRU_HEADS, LRU_HEAD_DIM = 8, 128
POOL_WINDOWS = (2, 4, 8, 16)
POOL_GROUP_DIM = 512
ADAM_LR, ADAM_B1, ADAM_B2, ADAM_EPS, ADAM_WD, ADAM_STEP = 0.001, 0.9, 0.999, 1e-08, 0.01, 10
MESH = pl.DeviceIdType.MESH
CID_WGATHER = 1
CIDS_L1 = (2, 3, 4)
CIDS_L0 = (5, 6, 7)
N_DEV, N_CHIP = 8, 4
SUBLANES = 8
BF16_ROWS = 16
POOL_HALO = 16
TS_PROJ, TS_MIX, TS_WGRAD, TS_DGRAD = 1024, 256, 1024, 256
SMALL_ROWS = 64
GRAD_WIRE_DTYPE = BF16
ANY = pl.BlockSpec(memory_space=pl.ANY)
VMEM = pl.BlockSpec(memory_space=pltpu.VMEM)
NT = (((1,), (1,)), ((), ()))
TN = (((0,), (0,)), ((), ()))


def _cp(sem=None, vmem_mb=56):
    kw = dict(vmem_limit_bytes=vmem_mb * 2 ** 20)
    if sem is not None:
        kw["dimension_semantics"] = sem
    return pltpu.CompilerParams(**kw)


def _tile(n, t):
    return min(n, t)


def _pos():
    return lax.axis_index("x"), lax.axis_index("y"), lax.axis_index("c")


def _flip(v, f):
    return 1 - v if f else v


def _sigmoid(z):
    return 0.5 * jnp.tanh(0.5 * z) + 0.5


def _rows(n, c):
    return lax.broadcasted_iota(jnp.int32, (n, c), 0)


def _down(a, d):
    return a if d == 0 else pltpu.roll(a, d, 0)


def _up(a, d):
    return a if d == 0 else pltpu.roll(a, a.shape[0] - d, 0)


def _scan_fwd_steps(a, u, carry):
    n, c = a.shape
    sub = _rows(SUBLANES, c)
    out = []
    for k in range(n // SUBLANES):
        p = a[k * SUBLANES:(k + 1) * SUBLANES]
        g = u[k * SUBLANES:(k + 1) * SUBLANES]
        for d in (1, 2, 4):
            keep = sub >= d
            g = g + p * jnp.where(keep, pltpu.roll(g, d, 0), 0.0)
            p = p * jnp.where(keep, pltpu.roll(p, d, 0), 1.0)
        h = g + p * carry
        carry = h[SUBLANES - 1:SUBLANES, :]
        out.append(h)
        yield
    return jnp.concatenate(out, axis=0)


def _scan_rev_steps(alpha, b, carry):
    n, c = alpha.shape
    sub = _rows(SUBLANES, c)
    out = []
    for k in reversed(range(n // SUBLANES)):
        p = alpha[k * SUBLANES:(k + 1) * SUBLANES]
        g = b[k * SUBLANES:(k + 1) * SUBLANES]
        for d in (1, 2, 4):
            keep = sub < SUBLANES - d
            g = g + p * jnp.where(keep, pltpu.roll(g, SUBLANES - d, 0), 0.0)
            p = p * jnp.where(keep, pltpu.roll(p, SUBLANES - d, 0), 1.0)
        h = g + p * carry
        carry = h[0:1, :]
        out.append(h)
        yield
    return jnp.concatenate(out[::-1], axis=0)


def _run(steps):
    while True:
        try:
            next(steps)
        except StopIteration as done:
            return done.value


def _paired(progress, pieces):
    n, done = len(pieces), 1
    pieces[0]()
    for frac in progress:
        while done < n and done <= frac * n:
            pieces[done]()
            done += 1
    while done < n:
        pieces[done]()
        done += 1


def _conv_taps(ext, halo, n, width):
    return [_down(ext, width - 1 - k)[halo:halo + n] for k in range(width)]


def _lru_gates(xc, wa_ref, ba, wx_ref, bx):
    xb = xc.astype(BF16)
    pa, px = [], []
    for h in range(LRU_HEADS):
        xh = xb[:, h * LRU_HEAD_DIM:(h + 1) * LRU_HEAD_DIM]
        pa.append(jnp.dot(xh, wa_ref[h], preferred_element_type=F32))
        px.append(jnp.dot(xh, wx_ref[h], preferred_element_type=F32))
    r = _sigmoid(jnp.concatenate(pa, axis=1) + ba)
    ig = _sigmoid(jnp.concatenate(px, axis=1) + bx)
    return r, ig


def _softplus_neg(lam):
    return jnp.maximum(-lam, 0.0) + jnp.log1p(jnp.exp(-jnp.abs(lam)))


def _recip_1_to_2(d):
    r0 = pl.reciprocal(d, approx=True)
    return r0 * (2.0 - d * r0)


def _lru_decay(r, sp, first):
    big_l = (-LRU_C) * r * sp
    a = jnp.exp(big_l)
    th = jnp.tanh(big_l)
    q = (-2.0 * th) * _recip_1_to_2(1.0 - th)
    rs = lax.rsqrt(jnp.maximum(q, SQRT_FLOOR))
    return a, jnp.where(first, 1.0, q * rs), rs


def _pool_inv_counts(t0, n):
    t = (t0 + lax.broadcasted_iota(jnp.int32, (n, 1), 0) + 1).astype(F32)
    return [1.0 / jnp.minimum(t, float(w)) for w in POOL_WINDOWS]


def _window_sums(ext, shift):
    gd = POOL_GROUP_DIM
    out = []
    s = ext
    for k in range(len(POOL_WINDOWS)):
        s = s + shift(s, 2 ** k)
        out.append(s[:, 0:gd])
        if k + 1 < len(POOL_WINDOWS):
            s = s[:, gd:]
    return out


SW_ROWS, SW_COLS = 16, 2 * D
SW_CONV, SW_SC, SW_POOL_B, SW_POOL_S = 0, 4, 8, 9


def _mod_fwd(c8, mod_w, mod_b, conv_w, sc_w, pool_b, pool_s):
    nw = mod_w.shape[2]
    cq, pq = conv_w.shape[1], pool_b.shape[1]

    def body(c_ref, w_ref, b_ref, cw_ref, sw_ref, pb_ref, ps_ref, ca_ref, mod_ref, small_ref,
             cslot, mslot, msend, pslot, psend, s1, r1, s2, r2, s3, r3):
        x, y, c = _pos()
        me = 4 * x + 2 * y + c
        chip = 2 * x + y
        first = []
        for r in range(1, N_DEV):
            fx, fy, fc = (r >> 2) & 1, (r >> 1) & 1, r & 1
            cp = pltpu.make_async_remote_copy(
                src_ref=c_ref, dst_ref=cslot.at[me], send_sem=s1.at[r - 1], recv_sem=r1.at[r - 1],
                device_id=(_flip(x, fx), _flip(y, fy), _flip(c, fc)), device_id_type=MESH)
            cp.start()
            first.append(cp)
        cslot[me] = c_ref[...]
        for cp in first:
            cp.wait()
        rows = _rows(SUBLANES, D)
        call = jnp.zeros((SUBLANES, D), F32)
        for d in range(N_DEV):
            call = jnp.where(rows == d, cslot[d], call)
        ca = call * _sigmoid(call)
        ca_ref[...] = ca
        for l in range(2):
            msend[l] = jnp.dot(ca, w_ref[l], precision=lax.Precision.HIGHEST, preferred_element_type=F32)
        psend[...] = jnp.zeros_like(psend)
        psend[SW_CONV:SW_CONV + 4, 0:cq] = cw_ref[...]
        psend[SW_SC:SW_SC + 3, 0:cq] = sw_ref[...]
        psend[SW_POOL_B:SW_POOL_B + 1, :] = pb_ref[...]
        psend[SW_POOL_S:SW_POOL_S + 1, :] = ps_ref[...]
        second = []
        for q, (fx, fy) in enumerate(((1, 0), (0, 1), (1, 1))):
            peer = (_flip(x, fx), _flip(y, fy), c)
            for src, dst, ss, rs in ((msend, mslot, s2, r2), (psend, pslot, s3, r3)):
                cp = pltpu.make_async_remote_copy(src_ref=src, dst_ref=dst.at[chip], send_sem=ss.at[q], recv_sem=rs.at[q],
                                                  device_id=peer, device_id_type=MESH)
                cp.start()
                second.append(cp)
        mslot[chip] = msend[...]
        pslot[chip] = psend[...]
        for cp in second:
            cp.wait()
        small_ref[...] = jnp.zeros_like(small_ref)
        for j in range(N_CHIP):
            for l in range(2):
                mod_ref[l, :, j * nw:(j + 1) * nw] = mslot[j, l] + b_ref[l:l + 1, j * nw:(j + 1) * nw]
            small_ref[0:SUBLANES, j * cq:(j + 1) * cq] = pslot[j, 0:SUBLANES, 0:cq]
            small_ref[SUBLANES:SW_ROWS, j * pq:(j + 1) * pq] = pslot[j, SUBLANES:SW_ROWS, :]

    args = (c8, mod_w, mod_b, conv_w, sc_w, pool_b, pool_s)
    dma3 = pltpu.SemaphoreType.DMA((N_CHIP - 1,))
    return pl.pallas_call(
        body, name="mod_fwd",
        in_specs=[VMEM] * len(args), out_specs=[VMEM] * 3,
        out_shape=[jax.ShapeDtypeStruct((SUBLANES, D), F32), jax.ShapeDtypeStruct((2, SUBLANES, N_CHIP * nw), F32),
                   jax.ShapeDtypeStruct((SW_ROWS, SW_COLS), F32)],
        scratch_shapes=[pltpu.VMEM((N_DEV, SUBLANES, D), F32), pltpu.VMEM((N_CHIP, 2, SUBLANES, nw), F32),
                        pltpu.VMEM((2, SUBLANES, nw), F32), pltpu.VMEM((N_CHIP, SW_ROWS, pq), F32),
                        pltpu.VMEM((SW_ROWS, pq), F32),
                        pltpu.SemaphoreType.DMA((N_DEV - 1,)), pltpu.SemaphoreType.DMA((N_DEV - 1,)),
                        dma3, dma3, dma3, dma3],
        compiler_params=_cp(),
    )(*args)


def _wcast(ws):
    def body(*refs):
        n = len(refs) // 2
        for a in range(n):
            refs[n + a][...] = refs[a][...].astype(BF16)

    return pl.pallas_call(
        body, name="wcast", in_specs=[VMEM] * len(ws), out_specs=[VMEM] * len(ws),
        out_shape=[jax.ShapeDtypeStruct(w.shape, BF16) for w in ws], compiler_params=_cp(),
    )(*ws)


def _wcast_own_block(w, kidx, name, after=None):
    rr, cc = w.shape
    rb = min(rr, 256)

    def body(k_ref, w_ref, *rest):
        rest[-1][...] = w_ref[...].astype(BF16)

    order = [] if after is None else [after]
    return pl.pallas_call(
        body, name=name,
        grid_spec=pltpu.PrefetchScalarGridSpec(
            num_scalar_prefetch=1, grid=(rr // rb,),
            in_specs=[pl.BlockSpec((rb, cc), lambda j, k_ref: (j, 0))] + [ANY] * len(order),
            out_specs=pl.BlockSpec((None, rb, cc), lambda j, k_ref: (k_ref[0], j, 0))),
        out_shape=jax.ShapeDtypeStruct((N_CHIP, rr, cc), BF16),
        compiler_params=_cp(("parallel",)),
    )(kidx, w, *order)


def _wgather_copies(outs, rows, ssem, rsem, fssem, frsem):
    n = len(outs)
    x, y, c = _pos()
    chip = 2 * x + y
    sib = (x, y, 1 - c)
    flips = ((1, 0), (0, 1), (1, 1))

    def half(a, which):
        hr = rows[a] // 2
        return pl.ds(pl.multiple_of(which * hr, BF16_ROWS), hr)

    sends = []
    for a in range(n):
        mine = outs[a].at[chip, half(a, c), :]
        for q, (fx, fy) in enumerate(flips):
            cp = pltpu.make_async_remote_copy(
                src_ref=mine, dst_ref=mine, send_sem=ssem.at[3 * a + q], recv_sem=rsem.at[3 * a + q],
                device_id=(_flip(x, fx), _flip(y, fy), c), device_id_type=MESH)
            cp.start()
            sends.append(cp)
    passed = []
    for a in range(n):
        for q, (fx, fy) in enumerate(flips):
            src_chip = 2 * _flip(x, fx) + _flip(y, fy)
            landed = outs[a].at[src_chip, half(a, c), :]
            pltpu.make_async_remote_copy(
                src_ref=landed, dst_ref=landed, send_sem=ssem.at[3 * a + q], recv_sem=rsem.at[3 * a + q],
                device_id=sib, device_id_type=MESH).wait_recv()
            cp = pltpu.make_async_remote_copy(
                src_ref=landed, dst_ref=landed, send_sem=fssem.at[3 * a + q], recv_sem=frsem.at[3 * a + q],
                device_id=sib, device_id_type=MESH)
            cp.start()
            passed.append(cp)
    for a in range(n):
        for q, (fx, fy) in enumerate(flips):
            src_chip = 2 * _flip(x, fx) + _flip(y, fy)
            other = outs[a].at[src_chip, half(a, 1 - c), :]
            pltpu.make_async_remote_copy(
                src_ref=other, dst_ref=other, send_sem=fssem.at[3 * a + q], recv_sem=frsem.at[3 * a + q],
                device_id=sib, device_id_type=MESH).wait_recv()
    for cp in sends + passed:
        cp.wait_send()


def _wgather(bufs, name):
    n = len(bufs)

    def body(*refs):
        _wgather_copies(refs[n:2 * n], [b.shape[1] for b in bufs], *refs[2 * n:])

    return pl.pallas_call(
        body, name=name, in_specs=[ANY] * n, out_specs=[ANY] * n,
        out_shape=[jax.ShapeDtypeStruct(b.shape, BF16) for b in bufs],
        input_output_aliases={a: a for a in range(n)},
        scratch_shapes=[pltpu.SemaphoreType.DMA((3 * n,))] * 4,
        compiler_params=_cp(),
    )(*bufs)


def _wgather_sequencer(bufs, name):
    n = len(bufs)
    refs = [jax.new_ref(b, memory_space=pltpu.MemorySpace.HBM) for b in bufs]
    dma = pltpu.SemaphoreType.DMA((3 * n,))

    @pl.kernel(mesh=plsc.ScalarSubcoreMesh(axis_name="sequencer", num_cores=1), name=name,
               scratch_types=(dma, dma, dma, dma), compiler_params=pltpu.CompilerParams(collective_id=CID_WGATHER))
    def launch(ssem, rsem, fssem, frsem):
        x, y, c = _pos()
        barrier = pltpu.get_barrier_semaphore()
        for peer in ((1 - x, y, c), (x, 1 - y, c), (1 - x, 1 - y, c), (x, y, 1 - c)):
            pl.semaphore_signal(barrier, inc=1, device_id=peer, device_id_type=MESH)
        pl.semaphore_wait(barrier, 4)
        _wgather_copies(refs, [b.shape[1] for b in bufs], ssem, rsem, fssem, frsem)

    launch()
    return [r[...] for r in refs]


def _l0_fwd(x, g, sc, sh, w_in, gate, cw, cb, wa, ba, wx, bx, lam, sw, wo):
    s_len, nb = x.shape[0], w_in.shape[2]
    ts = _tile(s_len, TS_MIX)
    n_t = s_len // ts
    hl = SUBLANES

    def body(xa_ref, xb_ref, g_ref, sc_ref, sh_ref, win_ref, gate_ref, cw_ref, cb_ref, wa_ref, ba_ref, wx_ref, bx_ref,
             lam_ref, sw_ref, wo_ref, x1_ref, h_ref, y_ref, h0_ref, p_ref, pcur, pnext, cxa, czz, chh):
        i = pl.program_id(0)

        @pl.when(i == 0)
        def _():
            cxa[...] = jnp.zeros_like(cxa)
            czz[...] = jnp.zeros_like(czz)
            chh[...] = jnp.zeros_like(chh)
            pnext[...] = jnp.zeros_like(pnext)

        pcur[...] = pnext[...]
        xv = xa_ref[...]
        rinv = lax.rsqrt(jnp.mean(xv * xv, axis=-1, keepdims=True) + RMS_EPS)
        h0 = (xv * rinv * (g_ref[...] * (1.0 + sc_ref[...])) + sh_ref[...]).astype(BF16)
        h0_ref[...] = h0

        def project(k):
            def emit():
                pk = jnp.dot(h0, win_ref[k], preferred_element_type=F32).astype(BF16)
                p_ref[:, k * nb:(k + 1) * nb] = pk
                pnext[:, k * nb:(k + 1) * nb] = pk
            return emit

        def mixer():
            piece = lambda k: pcur[:, k * D:(k + 1) * D].astype(F32)
            xa = piece(0)
            rows = _rows(ts, D)
            taps = _conv_taps(jnp.concatenate([cxa[...], xa], axis=0), hl, ts, 4)
            xc = cb_ref[...] + sum(cw_ref[k:k + 1, :] * taps[k] for k in range(4))
            r, ig = _lru_gates(xc, wa_ref, ba_ref[...], wx_ref, bx_ref[...])
            a, m, _ = _lru_decay(r, _softplus_neg(lam_ref[...]), (rows == 0) & (i == 1))
            yield 0.26
            h = _run(_scan_fwd_steps(a, m * ig * xc, chh[hl - 1:hl, :]))
            yield 0.51
            gcp, v = piece(3), piece(4)
            z = gcp * v
            ztaps = _conv_taps(jnp.concatenate([czz[...], z], axis=0), hl, ts, 3)
            yb = piece(2) * sum(sw_ref[k:k + 1, :] * ztaps[k] for k in range(3))
            ga, gb = piece(1), piece(5)
            y = jnp.concatenate([h * (ga * _sigmoid(ga)), yb * (gb * _sigmoid(gb))], axis=1).astype(BF16)
            yield 0.76
            y_ref[...] = y
            x1_ref[...] = xb_ref[...] + gate_ref[...] * jnp.dot(y, wo_ref[...], preferred_element_type=F32)
            h_ref[...] = h.astype(BF16)
            cxa[...] = xa[ts - hl:, :]
            czz[...] = z[ts - hl:, :]
            chh[...] = jnp.where(i > 0, h[ts - hl:, :], 0.0)

        _paired(mixer(), [project(k) for k in range(N_CHIP)])

    def full(a):
        return pl.BlockSpec(a.shape, lambda i: (0,) * a.ndim)

    ahead = lambda w: pl.BlockSpec((ts, w), lambda i: (jnp.minimum(i, n_t - 1), 0))
    behind = lambda w: pl.BlockSpec((ts, w), lambda i: (jnp.maximum(i - 1, 0), 0))
    args = (x, x, g, sc, sh, w_in, gate, cw, cb, wa, ba, wx, bx, lam, sw, wo)
    return pl.pallas_call(
        body, name="l0_fwd", grid=(n_t + 1,),
        in_specs=[ahead(D), behind(D)] + [full(a) for a in args[2:]],
        out_specs=[behind(D), behind(D), behind(2 * D), ahead(D), ahead(N_CHIP * nb)],
        out_shape=[jax.ShapeDtypeStruct((s_len, D), F32), jax.ShapeDtypeStruct((s_len, D), BF16),
                   jax.ShapeDtypeStruct((s_len, 2 * D), BF16), jax.ShapeDtypeStruct((s_len, D), BF16),
                   jax.ShapeDtypeStruct((s_len, N_CHIP * nb), BF16)],
        scratch_shapes=[pltpu.VMEM((ts, N_CHIP * nb), BF16)] * 2 + [pltpu.VMEM((hl, D), F32)] * 3,
        compiler_params=_cp(("arbitrary",)),
    )(*args)


def _l1_fwd(x1, g, sc, sh, w_in, tgt, gate, wg, bg, scale, wo, gf):
    s_len, nb = x1.shape[0], w_in.shape[2]
    ts = _tile(s_len, TS_MIX)
    n_t = s_len // ts
    pw, gd, hl = 2 * D, POOL_GROUP_DIM, POOL_HALO

    def body(xa_ref, xb_ref, t_ref, g_ref, sc_ref, sh_ref, win_ref, gate_ref, wg_ref, bg_ref, scl_ref, wo_ref, gf_ref,
             d_ref, mx_ref, y_ref, dx_ref, loss_ref, dgf_ref, h1_ref, p_ref, pcur, pnext, cv):
        i = pl.program_id(0)

        @pl.when(i == 0)
        def _():
            cv[...] = jnp.zeros_like(cv)
            loss_ref[...] = jnp.zeros_like(loss_ref)
            dgf_ref[...] = jnp.zeros_like(dgf_ref)
            pnext[...] = jnp.zeros_like(pnext)

        pcur[...] = pnext[...]
        xv = xa_ref[...]
        rinv = lax.rsqrt(jnp.mean(xv * xv, axis=-1, keepdims=True) + RMS_EPS)
        h1 = (xv * rinv * (g_ref[...] * (1.0 + sc_ref[...])) + sh_ref[...]).astype(BF16)
        h1_ref[...] = h1

        def project(k):
            def emit():
                pk = jnp.dot(h1, win_ref[k], preferred_element_type=F32).astype(BF16)
                p_ref[:, k * nb:(k + 1) * nb] = pk
                pnext[:, k * nb:(k + 1) * nb] = pk
            return emit

        def mixer():
            v = pcur[:, 0:pw].astype(F32)
            sums = _window_sums(jnp.concatenate([cv[...], v], axis=0), _down)
            inv = _pool_inv_counts(jnp.maximum(i - 1, 0) * ts, ts)
            dd = [sums[k][hl:hl + ts] * inv[k] - v[:, k * gd:(k + 1) * gd] for k in range(4)]
            d_ref[...] = jnp.concatenate(dd, axis=1).astype(BF16)
            yield 0.26
            mixed = jnp.concatenate(
                [jnp.dot(dd[k].astype(BF16), wg_ref[k], preferred_element_type=F32) for k in range(4)], axis=1) + bg_ref[...]
            mx_ref[...] = mixed.astype(BF16)
            gg = pcur[:, pw:2 * pw].astype(F32)
            y = (mixed * scl_ref[...] * (gg * _sigmoid(gg))).astype(BF16)
            y_ref[...] = y
            yield 0.51
            x2 = xb_ref[...] + gate_ref[...] * jnp.dot(y, wo_ref[...], preferred_element_type=F32)
            yield 0.76
            r2 = lax.rsqrt(jnp.mean(x2 * x2, axis=-1, keepdims=True) + RMS_EPS)
            n2 = x2 * r2
            err = n2 * gf_ref[...] - t_ref[...]
            loss_ref[...] += jnp.where(i > 0, jnp.sum(err * err, axis=0, keepdims=True), 0.0)
            dyf = err * (1.0 / D)
            dgf_ref[...] += jnp.where(i > 0, jnp.sum(dyf * n2, axis=0, keepdims=True), 0.0)
            dn = dyf * gf_ref[...]
            dx_ref[...] = r2 * (dn - n2 * jnp.mean(dn * n2, axis=-1, keepdims=True))
            cv[...] = v[ts - hl:, :]

        _paired(mixer(), [project(k) for k in range(N_CHIP)])

    def full(a):
        return pl.BlockSpec(a.shape, lambda i: (0,) * a.ndim)

    ahead = lambda w: pl.BlockSpec((ts, w), lambda i: (jnp.minimum(i, n_t - 1), 0))
    behind = lambda w: pl.BlockSpec((ts, w), lambda i: (jnp.maximum(i - 1, 0), 0))
    acc = pl.BlockSpec((1, D), lambda i: (0, 0))
    args = (x1, x1, tgt, g, sc, sh, w_in, gate, wg, bg, scale, wo, gf)
    return pl.pallas_call(
        body, name="l1_fwd", grid=(n_t + 1,),
        in_specs=[ahead(D), behind(D), behind(D)] + [full(a) for a in args[3:]],
        out_specs=[behind(pw), behind(pw), behind(pw), behind(D), acc, acc, ahead(D), ahead(N_CHIP * nb)],
        out_shape=[jax.ShapeDtypeStruct((s_len, pw), BF16)] * 3 + [jax.ShapeDtypeStruct((s_len, D), F32)]
        + [jax.ShapeDtypeStruct((1, D), F32)] * 2
        + [jax.ShapeDtypeStruct((s_len, D), BF16), jax.ShapeDtypeStruct((s_len, N_CHIP * nb), BF16)],
        scratch_shapes=[pltpu.VMEM((ts, N_CHIP * nb), BF16)] * 2 + [pltpu.VMEM((hl, pw), F32)],
        compiler_params=_cp(("arbitrary",)),
    )(*args)


def _l1_bwd_mix(dx2, proj, mixed, y, dpool, gate, wg, scale, wo):
    s_len = dx2.shape[0]
    ts = _tile(s_len, TS_MIX)
    n_t = s_len // ts
    pw, gd, hl = 2 * D, POOL_GROUP_DIM, POOL_HALO

    def body(dx_ref, gg_ref, mx_ref, y_ref, d_ref, gate_ref, wg_ref, sc_ref, wo_ref,
             dp_ref, mt_ref, dwg_ref, dsc_ref, dbg_ref, cq):
        i = pl.program_id(0)

        @pl.when(i == 0)
        def _():
            cq[...] = jnp.zeros_like(cq)
            dsc_ref[...] = jnp.zeros_like(dsc_ref)
            dbg_ref[...] = jnp.zeros_like(dbg_ref)
            mt_ref[...] = jnp.zeros_like(mt_ref)
            dwg_ref[...] = jnp.zeros_like(dwg_ref)

        dxv = dx_ref[...]
        dxb = dxv.astype(BF16)

        def wgrad_out(k):
            mt_ref[k] += lax.dot_general(y_ref[:, k * gd:(k + 1) * gd], dxb, TN, preferred_element_type=F32)

        dy = lax.dot_general((gate_ref[...] * dxv).astype(BF16), wo_ref[...], NT, preferred_element_type=F32)
        wgrad_out(0)
        gg = gg_ref[...].astype(F32)
        mixed = mx_ref[...].astype(F32)
        s = _sigmoid(gg)
        sg = gg * s
        dmixed = dy * sc_ref[...] * sg
        dsc_ref[...] += jnp.sum(dy * mixed * sg, axis=0, keepdims=True)
        dbg_ref[...] += jnp.sum(dmixed, axis=0, keepdims=True)
        dmb = dmixed.astype(BF16)
        wgrad_out(1)
        dp_ref[:, pw:2 * pw] = (dy * sc_ref[...] * mixed * (s * (1.0 + gg * (1.0 - s)))).astype(BF16)
        inv = _pool_inv_counts((n_t - 1 - i) * ts, ts)
        dd = []
        for k in range(4):
            dmk = dmb[:, k * gd:(k + 1) * gd]
            dd.append(lax.dot_general(dmk, wg_ref[k], NT, preferred_element_type=F32))
            dwg_ref[k] += lax.dot_general(d_ref[:, k * gd:(k + 1) * gd], dmk, TN, preferred_element_type=F32)
        wgrad_out(2)
        q = jnp.concatenate([dd[k] * inv[k] for k in range(4)], axis=1)
        sums = _window_sums(jnp.concatenate([q, cq[...]], axis=0), _up)
        wgrad_out(3)
        dp_ref[:, 0:pw] = jnp.concatenate([sums[k][0:ts] - dd[k] for k in range(4)], axis=1).astype(BF16)
        cq[...] = q[0:hl, :]

    def full(a):
        return pl.BlockSpec(a.shape, lambda i: (0,) * a.ndim)

    rev = lambda w, j=0: pl.BlockSpec((ts, w), lambda i: (n_t - 1 - i, j))
    acc = pl.BlockSpec((1, pw), lambda i: (0, 0))
    return pl.pallas_call(
        body, name="l1_bwd_mix", grid=(n_t,),
        in_specs=[rev(D), rev(pw, 1), rev(pw), rev(pw), rev(pw)] + [full(a) for a in (gate, wg, scale, wo)],
        out_specs=[rev(2 * pw), pl.BlockSpec((N_CHIP, gd, D), lambda i: (0, 0, 0)),
                   pl.BlockSpec((4, gd, gd), lambda i: (0, 0, 0)), acc, acc],
        out_shape=[jax.ShapeDtypeStruct((s_len, 2 * pw), BF16), jax.ShapeDtypeStruct((N_CHIP, gd, D), F32),
                   jax.ShapeDtypeStruct((4, gd, gd), F32),
                   jax.ShapeDtypeStruct((1, pw), F32), jax.ShapeDtypeStruct((1, pw), F32)],
        scratch_shapes=[pltpu.VMEM((hl, pw), F32)],
        compiler_params=_cp(("arbitrary",)),
    )(dx2, proj, mixed, y, dpool, gate, wg, scale, wo)


def _l0_bwd_mix(dx1, proj, hst, y, gate, cw, cb, wa, ba, wx, bx, lam, sw, wo):
    s_len = dx1.shape[0]
    ts = _tile(s_len, TS_MIX)
    n_t = s_len // ts
    hl, hb = SUBLANES, BF16_ROWS
    yb_w = 2 * D // N_CHIP

    def body(dx_ref, p_ref, ph_ref, h_ref, hh_ref, y_ref, gate_ref, cw_ref, cb_ref, wa_ref, ba_ref, wx_ref, bx_ref,
             lam_ref, sw_ref, wo_ref, dp_ref, mt_ref, dwa_ref, dwx_ref, sm_ref, cg, cdxc, cdcz, ca):
        i = pl.program_id(0)
        ri = n_t - 1 - i

        @pl.when(i == 0)
        def _():
            cg[...] = jnp.zeros_like(cg)
            ca[...] = jnp.zeros_like(ca)
            cdxc[...] = jnp.zeros_like(cdxc)
            cdcz[...] = jnp.zeros_like(cdcz)
            sm_ref[...] = jnp.zeros_like(sm_ref)
            mt_ref[...] = jnp.zeros_like(mt_ref)
            dwa_ref[...] = jnp.zeros_like(dwa_ref)
            dwx_ref[...] = jnp.zeros_like(dwx_ref)

        dxb = dx_ref[...].astype(BF16)

        def wgrad_out(k):
            mt_ref[k] += lax.dot_general(y_ref[:, k * yb_w:(k + 1) * yb_w], dxb, TN, preferred_element_type=F32)

        wgrad_out(0)
        has_prev = (ri > 0).astype(F32)
        xa, ga, gbp, gcp, v, gb = [p_ref[:, k * D:(k + 1) * D].astype(F32) for k in range(6)]
        prev = lambda k: ph_ref[:, k * D:(k + 1) * D].astype(F32)[hb - hl:hb] * has_prev
        rows = _rows(ts, D)
        first = (rows == 0) & (ri == 0)
        xtaps = _conv_taps(jnp.concatenate([prev(0), xa], axis=0), hl, ts, 4)
        xc = cb_ref[...] + sum(cw_ref[k:k + 1, :] * xtaps[k] for k in range(4))
        r, ig = _lru_gates(xc, wa_ref, ba_ref[...], wx_ref, bx_ref[...])
        sp = _softplus_neg(lam_ref[...])
        a, m, inv_m = _lru_decay(r, sp, first)
        z = gcp * v
        ztaps = _conv_taps(jnp.concatenate([prev(3) * prev(4), z], axis=0), hl, ts, 3)
        cz = sum(sw_ref[k:k + 1, :] * ztaps[k] for k in range(3))
        h = h_ref[...].astype(F32)
        hprev = _down(jnp.concatenate([hh_ref[...].astype(F32)[hb - hl:hb] * has_prev, h], axis=0), 1)[hl:hl + ts]
        dy = lax.dot_general((gate_ref[...] * dx_ref[...]).astype(BF16), wo_ref[...], NT, preferred_element_type=F32)
        dya_pre, dyb_pre = dy[:, 0:D], dy[:, D:2 * D]
        s_a, s_b = _sigmoid(ga), _sigmoid(gb)
        dp_ref[:, D:2 * D] = (dya_pre * h * (s_a * (1.0 + ga * (1.0 - s_a)))).astype(BF16)
        dp_ref[:, 5 * D:6 * D] = (dyb_pre * (gbp * cz) * (s_b * (1.0 + gb * (1.0 - s_b)))).astype(BF16)
        dya = dya_pre * (ga * s_a)
        dyb = dyb_pre * (gb * s_b)
        wgrad_out(1)
        dp_ref[:, 2 * D:3 * D] = (dyb * cz).astype(BF16)
        dcz = dyb * gbp
        for k in range(3):
            sm_ref[8 + k:9 + k, :] += jnp.sum(dcz * ztaps[k], axis=0, keepdims=True)
        dcz_ext = jnp.concatenate([dcz, cdcz[...]], axis=0)
        dz = sum(sw_ref[k:k + 1, :] * _up(dcz_ext, 2 - k)[0:ts] for k in range(3))
        dp_ref[:, 3 * D:4 * D] = (dz * v).astype(BF16)
        dp_ref[:, 4 * D:5 * D] = (dz * gcp).astype(BF16)
        cdcz[...] = dcz[0:hl, :]
        alpha = _up(jnp.concatenate([a, ca[...]], axis=0), 1)[0:ts]
        wgrad_out(2)
        dh = _run(_scan_rev_steps(alpha, dya, cg[0:1, :]))
        wgrad_out(3)
        cg[...] = dh[0:hl, :]
        ca[...] = a[0:hl, :]
        da = dh * hprev
        dm = dh * ig * xc
        di = dh * m * xc
        dxc = dh * m * ig
        dl = da * a - jnp.where(first, 0.0, dm * (a * a) * inv_m)
        sm_ref[7:8, :] += jnp.sum(dl * r, axis=0, keepdims=True) * (-LRU_C)
        dpa = (dl * sp) * (-LRU_C) * r * (1.0 - r)
        dpx = di * ig * (1.0 - ig)
        sm_ref[5:6, :] += jnp.sum(dpa, axis=0, keepdims=True)
        sm_ref[6:7, :] += jnp.sum(dpx, axis=0, keepdims=True)
        dpa_b, dpx_b, xc_b = dpa.astype(BF16), dpx.astype(BF16), xc.astype(BF16)
        back = []
        for hd in range(LRU_HEADS):
            sl = slice(hd * LRU_HEAD_DIM, (hd + 1) * LRU_HEAD_DIM)
            back.append(lax.dot_general(dpa_b[:, sl], wa_ref[hd], NT, preferred_element_type=F32)
                        + lax.dot_general(dpx_b[:, sl], wx_ref[hd], NT, preferred_element_type=F32))
            dwa_ref[hd] += lax.dot_general(xc_b[:, sl], dpa_b[:, sl], TN, preferred_element_type=F32)
            dwx_ref[hd] += lax.dot_general(xc_b[:, sl], dpx_b[:, sl], TN, preferred_element_type=F32)
        dxc = dxc + jnp.concatenate(back, axis=1)
        sm_ref[4:5, :] += jnp.sum(dxc, axis=0, keepdims=True)
        for k in range(4):
            sm_ref[k:k + 1, :] += jnp.sum(dxc * xtaps[k], axis=0, keepdims=True)
        dxc_ext = jnp.concatenate([dxc, cdxc[...]], axis=0)
        dp_ref[:, 0:D] = sum(cw_ref[k:k + 1, :] * _up(dxc_ext, 3 - k)[0:ts] for k in range(4)).astype(BF16)
        cdxc[...] = dxc[0:hl, :]

    def full(a):
        return pl.BlockSpec(a.shape, lambda i: (0,) * a.ndim)

    rev = lambda w: pl.BlockSpec((ts, w), lambda i: (n_t - 1 - i, 0))
    halo = lambda w: pl.BlockSpec((hb, w), lambda i: (jnp.maximum((n_t - 1 - i) * (ts // hb) - 1, 0), 0))
    return pl.pallas_call(
        body, name="l0_bwd_mix", grid=(n_t,),
        in_specs=[rev(D), rev(6 * D), halo(6 * D), rev(D), halo(D), rev(2 * D)]
        + [full(a) for a in (gate, cw, cb, wa, ba, wx, bx, lam, sw, wo)],
        out_specs=[rev(6 * D), pl.BlockSpec((N_CHIP, yb_w, D), lambda i: (0, 0, 0)),
                   pl.BlockSpec(wa.shape, lambda i: (0, 0, 0)), pl.BlockSpec(wa.shape, lambda i: (0, 0, 0)),
                   pl.BlockSpec((2 * SUBLANES, D), lambda i: (0, 0))],
        out_shape=[jax.ShapeDtypeStruct((s_len, 6 * D), BF16), jax.ShapeDtypeStruct((N_CHIP, yb_w, D), F32),
                   jax.ShapeDtypeStruct(wa.shape, F32), jax.ShapeDtypeStruct(wa.shape, F32),
                   jax.ShapeDtypeStruct((2 * SUBLANES, D), F32)],
        scratch_shapes=[pltpu.VMEM((hl, D), F32)] * 4,
        compiler_params=_cp(("arbitrary",)),
    )(dx1, proj, proj, hst, hst, y, gate, cw, cb, wa, ba, wx, bx, lam, sw, wo)


def _dgrad_norm(dproj, w, x, dres, g, sc, name, after=None):
    s_len, nb = x.shape[0], w.shape[2]
    ts = _tile(s_len, TS_DGRAD)
    order = [] if after is None else [after]

    def body(dp_ref, w_ref, x_ref, dr_ref, g_ref, sc_ref, *rest):
        dx_ref, s1_ref, s2_ref = rest[len(order):]

        @pl.when(pl.program_id(0) == 0)
        def _():
            s1_ref[...] = jnp.zeros_like(s1_ref)
            s2_ref[...] = jnp.zeros_like(s2_ref)

        dh = sum(lax.dot_general(dp_ref[:, k * nb:(k + 1) * nb], w_ref[k], NT, preferred_element_type=F32)
                 for k in range(N_CHIP))
        xv = x_ref[...]
        r = lax.rsqrt(jnp.mean(xv * xv, axis=-1, keepdims=True) + RMS_EPS)
        n = xv * r
        s1_ref[...] += jnp.sum(dh, axis=0, keepdims=True)
        s2_ref[...] += jnp.sum(dh * n, axis=0, keepdims=True)
        dn = dh * (g_ref[...] * (1.0 + sc_ref[...]))
        dx_ref[...] = dr_ref[...] + r * (dn - n * jnp.mean(dn * n, axis=-1, keepdims=True))

    row = lambda wd: pl.BlockSpec((ts, wd), lambda i: (i, 0))
    vec = pl.BlockSpec((1, D), lambda i: (0, 0))
    return pl.pallas_call(
        body, name=name, grid=(s_len // ts,),
        in_specs=[row(N_CHIP * nb), pl.BlockSpec(w.shape, lambda i: (0, 0, 0)), row(D), row(D), vec, vec]
        + [ANY] * len(order),
        out_specs=[row(D), vec, vec],
        out_shape=[jax.ShapeDtypeStruct((s_len, D), F32)] + [jax.ShapeDtypeStruct((1, D), F32)] * 2,
        compiler_params=_cp(("arbitrary",)),
    )(dproj, w, x, dres, g, sc, *order)


def _wgrad(a, b, groups, ka, nb, a_col, b_col, name, after=None):
    s_len = a.shape[0]
    ts = _tile(s_len, TS_WGRAD)
    order = [] if after is None else [after]

    def body(a_ref, b_ref, *rest):
        o_ref = rest[-1]

        @pl.when(pl.program_id(1) == 0)
        def _():
            o_ref[...] = jnp.zeros_like(o_ref)

        o_ref[...] += lax.dot_general(a_ref[...].astype(BF16), b_ref[...].astype(BF16), TN, preferred_element_type=F32)

    return pl.pallas_call(
        body, name=name, grid=(groups, s_len // ts),
        in_specs=[pl.BlockSpec((ts, ka), lambda g, s: (s, a_col(g))), pl.BlockSpec((ts, nb), lambda g, s: (s, b_col(g)))]
        + [ANY] * len(order),
        out_specs=pl.BlockSpec((None, ka, nb), lambda g, s: (g, 0, 0)),
        out_shape=jax.ShapeDtypeStruct((groups, ka, nb), F32),
        compiler_params=_cp(("parallel", "arbitrary")),
    )(a, b, *order)


def _wo_final(mt, wo, gate, name):
    rb = mt.shape[1]

    def body(m_ref, w_ref, gate_ref, dw_ref, dg_ref):
        @pl.when(pl.program_id(0) == 0)
        def _():
            dg_ref[...] = jnp.zeros_like(dg_ref)

        mv = m_ref[...]
        dw_ref[...] = mv * gate_ref[...]
        dg_ref[...] += jnp.sum(mv * w_ref[...].astype(F32), axis=0, keepdims=True)

    blk = pl.BlockSpec((None, rb, D), lambda k: (k, 0, 0))
    vec = pl.BlockSpec((1, D), lambda k: (0, 0))
    return pl.pallas_call(
        body, name=name, grid=(N_CHIP,), in_specs=[blk, blk, vec], out_specs=[blk, vec],
        out_shape=[jax.ShapeDtypeStruct(mt.shape, F32), jax.ShapeDtypeStruct((1, D), F32)],
        compiler_params=_cp(("arbitrary",)),
    )(mt, wo, gate)


ROW_NORM_G, ROW_CONV_W, ROW_CONV_B, ROW_B_A, ROW_B_X, ROW_LAMBDA, ROW_SC_W, ROW_POOL_B, ROW_POOL_S, ROW_FINAL_G = (
    0, 2, 6, 7, 8, 9, 10, 13, 15, 17)


def _small_pack(s1_0, s2_0, s1_1, s2_1, sm0, dsc1, dbg1, dgf, losscols, dgate0, dgate1, norm_g, sc0, sc1, lam):
    def body(s1_0r, s2_0r, s1_1r, s2_1r, sm, dsc, dbg, dgfr, lcols, dg0, dg1, ng, sc0r, sc1r, lamr, buf, dmod, loss):
        buf[...] = jnp.zeros_like(buf)
        buf[0:1, :] = s2_0r[...] * (1.0 + sc0r[...])
        buf[1:2, :] = s2_1r[...] * (1.0 + sc1r[...])
        buf[ROW_CONV_W:ROW_CONV_W + 4, :] = sm[0:4, :]
        buf[ROW_CONV_B:ROW_CONV_B + 1, :] = sm[4:5, :]
        buf[ROW_B_A:ROW_B_A + 1, :] = sm[5:6, :]
        buf[ROW_B_X:ROW_B_X + 1, :] = sm[6:7, :]
        buf[ROW_LAMBDA:ROW_LAMBDA + 1, :] = -sm[7:8, :] * _sigmoid(-lamr[...])
        buf[ROW_SC_W:ROW_SC_W + 3, :] = sm[8:11, :]
        for k in range(2):
            buf[ROW_POOL_B + k:ROW_POOL_B + k + 1, :] = dbg[:, k * D:(k + 1) * D]
            buf[ROW_POOL_S + k:ROW_POOL_S + k + 1, :] = dsc[:, k * D:(k + 1) * D]
        buf[ROW_FINAL_G:ROW_FINAL_G + 1, :] = dgfr[...]
        pieces = (s1_0r[...], s2_0r[...] * ng[0:1, :], dg0[...], s1_1r[...], s2_1r[...] * ng[1:2, :], dg1[...])
        for k, pc in enumerate(pieces):
            dmod[:, k * D:(k + 1) * D] = jnp.broadcast_to(pc, (SUBLANES, D))
        loss[...] = jnp.broadcast_to(jnp.sum(lcols[...], axis=1, keepdims=True) * (0.5 / D), loss.shape)

    args = (s1_0, s2_0, s1_1, s2_1, sm0, dsc1, dbg1, dgf, losscols, dgate0, dgate1, norm_g, sc0, sc1, lam)
    return pl.pallas_call(
        body, name="small_pack", in_specs=[VMEM] * len(args), out_specs=[VMEM] * 3,
        out_shape=[jax.ShapeDtypeStruct((SMALL_ROWS, D), F32), jax.ShapeDtypeStruct((SUBLANES, 6 * D), F32),
                   jax.ShapeDtypeStruct((SUBLANES, 128), F32)],
        compiler_params=_cp(),
    )(*args)


def _small_comm(buf_a, buf_b, dmod8):
    ra, rb = buf_a.shape[0] // N_DEV, buf_b.shape[0] // N_DEV
    wb = buf_b.shape[1]

    def body(a_ref, b_ref, dm_ref, oa_ref, ob_ref, odm_ref, ina, inb, dslot, sa, sb, s1, r1, s2, r2):
        x, y, c = _pos()
        me = 4 * x + 2 * y + c
        peers = []
        for r in range(1, N_DEV):
            fx, fy, fc = (r >> 2) & 1, (r >> 1) & 1, r & 1
            px, py, pc = _flip(x, fx), _flip(y, fy), _flip(c, fc)
            peers.append(((px, py, pc), 4 * px + 2 * py + pc))
        seg_a = lambda d: pl.ds(pl.multiple_of(d * ra, SUBLANES), ra)
        seg_b = lambda d: pl.ds(pl.multiple_of(d * rb, SUBLANES), rb)
        first = []
        for r, (peer, pid) in enumerate(peers):
            for k, (src, dst) in enumerate(((a_ref.at[seg_a(pid), :], ina.at[r]), (b_ref.at[seg_b(pid), :], inb.at[r]),
                                            (dm_ref, dslot.at[me]))):
                cp = pltpu.make_async_remote_copy(src_ref=src, dst_ref=dst, send_sem=s1.at[3 * r + k],
                                                  recv_sem=r1.at[3 * r + k], device_id=peer, device_id_type=MESH)
                cp.start()
                first.append(cp)
        dslot[me] = dm_ref[...]
        for cp in first:
            cp.wait()
        acc_a, acc_b = a_ref[seg_a(me), :], b_ref[seg_b(me), :]
        for r in range(N_DEV - 1):
            acc_a = acc_a + ina[r]
            acc_b = acc_b + inb[r]
        sa[...] = acc_a
        sb[...] = acc_b
        oa_ref[seg_a(me), :] = acc_a
        ob_ref[seg_b(me), :] = acc_b
        second = []
        for r, (peer, pid) in enumerate(peers):
            for k, (src, dst) in enumerate(((sa, oa_ref.at[seg_a(me), :]), (sb, ob_ref.at[seg_b(me), :]))):
                cp = pltpu.make_async_remote_copy(src_ref=src, dst_ref=dst, send_sem=s2.at[2 * r + k],
                                                  recv_sem=r2.at[2 * r + k], device_id=peer, device_id_type=MESH)
                cp.start()
                second.append(cp)
        rows = _rows(SUBLANES, dm_ref.shape[1])
        dm_all = jnp.zeros(dm_ref.shape, F32)
        for d in range(N_DEV):
            dm_all = jnp.where(rows == d, dslot[d], dm_all)
        odm_ref[...] = dm_all
        for cp in second:
            cp.wait()

    nrel = N_DEV - 1
    return pl.pallas_call(
        body, name="small_comm", in_specs=[VMEM] * 3, out_specs=[VMEM] * 3,
        out_shape=[jax.ShapeDtypeStruct(buf_a.shape, F32), jax.ShapeDtypeStruct(buf_b.shape, F32),
                   jax.ShapeDtypeStruct(dmod8.shape, F32)],
        scratch_shapes=[pltpu.VMEM((nrel, ra, D), F32), pltpu.VMEM((nrel, rb, wb), F32),
                        pltpu.VMEM((N_DEV,) + dmod8.shape, F32), pltpu.VMEM((ra, D), F32), pltpu.VMEM((rb, wb), F32),
                        pltpu.SemaphoreType.DMA((3 * nrel,)), pltpu.SemaphoreType.DMA((3 * nrel,)),
                        pltpu.SemaphoreType.DMA((2 * nrel,)), pltpu.SemaphoreType.DMA((2 * nrel,))],
        compiler_params=_cp(),
    )(buf_a, buf_b, dmod8)


def _adam(w, g, m, v):
    m2 = ADAM_B1 * m + (1.0 - ADAM_B1) * g
    v2 = ADAM_B2 * v + (1.0 - ADAM_B2) * (g * g)
    m_hat = m2 / (1.0 - ADAM_B1 ** ADAM_STEP)
    v_hat = v2 / (1.0 - ADAM_B2 ** ADAM_STEP)
    return -ADAM_LR * (m_hat / (jnp.sqrt(v_hat) + ADAM_EPS) + ADAM_WD * w), m2, v2


def _small_adam(red_a, red_b, dm_all, params):
    n = len(params)

    def body(*refs):
        ra, rb, dm = refs[:3]
        wmv = refs[3:3 + 3 * n]
        outs = refs[3 + 3 * n:]
        x, y, _ = _pos()
        chip = 2 * x + y

        def shard(row0, nrows, width):
            per_row = D // width
            cands = []
            for k in range(N_CHIP):
                if nrows == 1 or per_row >= N_CHIP:
                    cands.append(ra[row0:row0 + nrows, k * width:(k + 1) * width])
                else:
                    rr, cc = divmod(k * width, D)
                    cands.append(ra[row0 + rr:row0 + rr + 1, cc:cc + width])
            g = cands[0]
            for k in range(1, N_CHIP):
                g = jnp.where(chip == k, cands[k], g)
            return g

        dms = jnp.sum(dm[...], axis=0, keepdims=True)
        hw = LRU_HEADS * LRU_HEAD_DIM
        grads = [
            ra[ROW_NORM_G:ROW_NORM_G + 2, :],
            None,
            shard(ROW_CONV_W, 4, D // N_CHIP),
            ra[ROW_CONV_B:ROW_CONV_B + 1, :],
            rb[0:hw, :],
            ra[ROW_B_A:ROW_B_A + 1, :],
            rb[hw:2 * hw, :],
            ra[ROW_B_X:ROW_B_X + 1, :],
            ra[ROW_LAMBDA:ROW_LAMBDA + 1, :],
            shard(ROW_SC_W, 3, D // N_CHIP),
            shard(ROW_POOL_B, 2, 2 * D // N_CHIP),
            shard(ROW_POOL_S, 2, 2 * D // N_CHIP),
            ra[ROW_FINAL_G:ROW_FINAL_G + 1, :],
        ]
        for p in range(n):
            w_ref, m_ref, v_ref = wmv[3 * p:3 * p + 3]
            g_out, d_out, m_out, v_out = outs[4 * p:4 * p + 4]
            if grads[p] is None:
                for l in range(2):
                    g = dms[:, l * 3 * D:(l + 1) * 3 * D]
                    dl, m2, v2 = _adam(w_ref[l:l + 1, :], g, m_ref[l:l + 1, :], v_ref[l:l + 1, :])
                    g_out[l:l + 1, :] = g
                    d_out[l:l + 1, :] = dl
                    m_out[l:l + 1, :] = m2
                    v_out[l:l + 1, :] = v2
            else:
                g = grads[p]
                dl, m2, v2 = _adam(w_ref[...], g, m_ref[...], v_ref[...])
                g_out[...] = g
                d_out[...] = dl
                m_out[...] = m2
                v_out[...] = v2

    flat = [a for p in params for a in p]
    return pl.pallas_call(
        body, name="small_adam", in_specs=[VMEM] * (3 + len(flat)), out_specs=[VMEM] * (4 * n),
        out_shape=[jax.ShapeDtypeStruct(p[0].shape, F32) for p in params for _ in range(4)],
        compiler_params=_cp(),
    )(red_a, red_b, dm_all, *flat)


def _modw_adam(ca_t, dm_sh, w, m, v):
    nw = w.shape[2]

    def body(c_ref, d_ref, w_ref, m_ref, v_ref, g_out, d_out, m_out, v_out):
        g = jnp.dot(c_ref[...], d_ref[...], precision=lax.Precision.HIGHEST, preferred_element_type=F32)
        dl, m2, v2 = _adam(w_ref[...], g, m_ref[...], v_ref[...])
        g_out[...] = g
        d_out[...] = dl
        m_out[...] = m2
        v_out[...] = v2

    blk = pl.BlockSpec((None, D, nw), lambda l: (l, 0, 0))
    return pl.pallas_call(
        body, name="modw_adam", grid=(2,),
        in_specs=[pl.BlockSpec((D, SUBLANES), lambda l: (0, 0)), pl.BlockSpec((None, SUBLANES, nw), lambda l: (l, 0, 0)),
                  blk, blk, blk],
        out_specs=[blk] * 4, out_shape=[jax.ShapeDtypeStruct(w.shape, F32)] * 4,
        compiler_params=_cp(("arbitrary",)),
    )(ca_t, dm_sh, w, m, v)


def _half_rows(r):
    return r // 2


def _exchange(copies, name, out_type, n_sems, args, sequencer, after=None):
    order = [] if after is None else [after]
    n_in, n_out = len(args) + len(order), len(out_type)

    def body(*refs):
        barrier = pltpu.get_barrier_semaphore()
        peers = sequencer[1](*_pos())
        for peer in peers:
            pl.semaphore_signal(barrier, inc=1, device_id=peer, device_id_type=MESH)
        pl.semaphore_wait(barrier, len(peers))
        copies(refs[:n_in], refs[n_in:n_in + n_out], refs[n_in + n_out], refs[n_in + n_out + 1])

    sems = [pltpu.SemaphoreType.DMA((n_sems,))] * 2
    return pl.kernel(body, out_type, mesh=plsc.ScalarSubcoreMesh(axis_name="sequencer", num_cores=1), name=name,
                     scratch_types=sems, compiler_params=pltpu.CompilerParams(collective_id=sequencer[0]))(*args, *order)


def _sibling(x, y, c):
    return [(x, y, 1 - c)]


def _other_chips(x, y, c):
    return [(1 - x, y, c), (x, 1 - y, c), (1 - x, 1 - y, c)]


def _sib_send_halves(gs, name, collective_id, after=None):
    n = len(gs)

    def copies(ins, outs, ssem, rsem):
        x, y, c = _pos()
        cps = []
        for a in range(n):
            hr = _half_rows(gs[a].shape[1])
            cp = pltpu.make_async_remote_copy(
                src_ref=ins[a].at[:, pl.ds(pl.multiple_of((1 - c) * hr, SUBLANES), hr), :], dst_ref=outs[a],
                send_sem=ssem.at[a], recv_sem=rsem.at[a], device_id=(x, y, 1 - c), device_id_type=MESH)
            cp.start()
            cps.append(cp)
        for cp in cps:
            cp.wait()

    out_type = [jax.ShapeDtypeStruct((N_CHIP, _half_rows(g.shape[1]), g.shape[2]), F32) for g in gs]
    return _exchange(copies, name, out_type, n, gs, (collective_id, _sibling), after)


def _add_half(g, got, cidx, name, after=None):
    _, hr, cc = got.shape
    rb = min(hr, 256)

    def body(c_ref, g_ref, r_ref, *rest):
        rest[-1][...] = (g_ref[...] + r_ref[...]).astype(rest[-1].dtype)

    order = [] if after is None else [after]
    blk = pl.BlockSpec((None, rb, cc), lambda k, j, c_ref: (k, j, 0))
    return pl.pallas_call(
        body, name=name,
        grid_spec=pltpu.PrefetchScalarGridSpec(
            num_scalar_prefetch=1, grid=(N_CHIP, hr // rb),
            in_specs=[pl.BlockSpec((None, rb, cc), lambda k, j, c_ref: (k, c_ref[0] * (hr // rb) + j, 0)), blk]
            + [ANY] * len(order),
            out_specs=blk),
        out_shape=jax.ShapeDtypeStruct(got.shape, GRAD_WIRE_DTYPE),
        compiler_params=_cp(("parallel", "parallel")),
    )(cidx, g, got, *order)


def _chip_scatter(ps, name, collective_id, after=None):
    n = len(ps)

    def copies(ins, outs, ssem, rsem):
        x, y, c = _pos()
        cps = []
        for a in range(n):
            for q, (fx, fy) in enumerate(((1, 0), (0, 1), (1, 1))):
                px, py = _flip(x, fx), _flip(y, fy)
                cp = pltpu.make_async_remote_copy(
                    src_ref=ins[a].at[2 * px + py], dst_ref=outs[a].at[q],
                    send_sem=ssem.at[3 * a + q], recv_sem=rsem.at[3 * a + q], device_id=(px, py, c), device_id_type=MESH)
                cp.start()
                cps.append(cp)
        for cp in cps:
            cp.wait()

    out_type = [jax.ShapeDtypeStruct((N_CHIP - 1,) + p.shape[1:], p.dtype) for p in ps]
    return _exchange(copies, name, out_type, 3 * n, ps, (collective_id, _other_chips), after)


def _add_owner(p, got, chipidx, name, after=None):
    _, hr, cc = p.shape
    rb = min(hr, 256)

    def body(k_ref, p_ref, r_ref, *rest):
        rest[-1][...] = ((p_ref[...].astype(F32) + r_ref[0].astype(F32)) + r_ref[1].astype(F32)) + r_ref[2].astype(F32)

    order = [] if after is None else [after]
    return pl.pallas_call(
        body, name=name,
        grid_spec=pltpu.PrefetchScalarGridSpec(
            num_scalar_prefetch=1, grid=(hr // rb,),
            in_specs=[pl.BlockSpec((None, rb, cc), lambda j, k_ref: (k_ref[0], j, 0)),
                      pl.BlockSpec((N_CHIP - 1, rb, cc), lambda j, k_ref: (0, j, 0))] + [ANY] * len(order),
            out_specs=pl.BlockSpec((rb, cc), lambda j, k_ref: (j, 0))),
        out_shape=jax.ShapeDtypeStruct((hr, cc), F32),
        compiler_params=_cp(("parallel",)),
    )(chipidx, p, got, *order)


def _sib_exchange(ts_, name, collective_id, after=None):
    n = len(ts_)

    def copies(ins, outs, ssem, rsem):
        x, y, c = _pos()
        cps = []
        for a in range(n):
            cp = pltpu.make_async_remote_copy(src_ref=ins[a], dst_ref=outs[a], send_sem=ssem.at[a],
                                              recv_sem=rsem.at[a], device_id=(x, y, 1 - c), device_id_type=MESH)
            cp.start()
            cps.append(cp)
        for cp in cps:
            cp.wait()

    out_type = [jax.ShapeDtypeStruct(t.shape, F32) for t in ts_]
    return _exchange(copies, name, out_type, n, ts_, (collective_id, _sibling), after)


def _adam_2d(w, g_own, g_sib, m, v, cidx, name):
    rr, cc = w.shape
    hr = rr // 2
    rb = min(hr, 256)
    nb = hr // rb

    def body(c_ref, w_ref, go_ref, gs_ref, m_ref, v_ref, g_out, d_out, m_out, v_out):
        g = jnp.where(pl.program_id(0) == c_ref[0], go_ref[...], gs_ref[...])
        dl, m2, v2 = _adam(w_ref[...], g, m_ref[...], v_ref[...])
        g_out[...] = g
        d_out[...] = dl
        m_out[...] = m2
        v_out[...] = v2

    blk = pl.BlockSpec((rb, cc), lambda h, j, c_ref: (h * nb + j, 0))
    hblk = pl.BlockSpec((rb, cc), lambda h, j, c_ref: (j, 0))
    return pl.pallas_call(
        body, name=name,
        grid_spec=pltpu.PrefetchScalarGridSpec(
            num_scalar_prefetch=1, grid=(2, nb), in_specs=[blk, hblk, hblk, blk, blk], out_specs=[blk] * 4),
        out_shape=[jax.ShapeDtypeStruct((rr, cc), F32)] * 4, compiler_params=_cp(("parallel", "parallel")),
    )(cidx, w, g_own, g_sib, m, v)


def kernel(x, c, norm_g, mod_w, mod_b, hy_w_in, hy_conv_w, hy_conv_b, lru_w_a, lru_b_a, lru_w_x, lru_b_x, lru_lambda, sc_conv_w, hy_w_out, pool_w_in, pool_w_grp, pool_b_grp, pool_scale, pool_w_out, final_g, loss_target, m_norm_g, m_mod_w, m_mod_b, m_hy_w_in, m_hy_conv_w, m_hy_conv_b, m_lru_w_a, m_lru_b_a, m_lru_w_x, m_lru_b_x, m_lru_lambda, m_sc_conv_w, m_hy_w_out, m_pool_w_in, m_pool_w_grp, m_pool_b_grp, m_pool_scale, m_pool_w_out, m_final_g, v_norm_g, v_mod_w, v_mod_b, v_hy_w_in, v_hy_conv_w, v_hy_conv_b, v_lru_w_a, v_lru_b_a, v_lru_w_x, v_lru_b_x, v_lru_lambda, v_sc_conv_w, v_hy_w_out, v_pool_w_in, v_pool_w_grp, v_pool_b_grp, v_pool_scale, v_pool_w_out, v_final_g):
    ax, ay, ac = _pos()
    me = 4 * ax + 2 * ay + ac
    chip = 2 * ax + ay
    xs = x[0]
    tgt = loss_target[0]
    gd = POOL_GROUP_DIM

    ca_all, mod_all, small_w = _mod_fwd(jnp.broadcast_to(c, (SUBLANES, D)), mod_w, mod_b,
                                        hy_conv_w[0], sc_conv_w[0], pool_b_grp, pool_scale)
    mod_me = lax.dynamic_index_in_dim(mod_all, me, axis=1, keepdims=False)
    sh0, sc0, gt0 = (mod_me[0:1, k * D:(k + 1) * D] for k in range(3))
    sh1, sc1, gt1 = (mod_me[1:2, k * D:(k + 1) * D] for k in range(3))
    cw = small_w[SW_CONV:SW_CONV + 4, 0:D]
    sw = small_w[SW_SC:SW_SC + 3, 0:D]
    pool_b = small_w[SW_POOL_B:SW_POOL_B + 1, :]
    pool_s = small_w[SW_POOL_S:SW_POOL_S + 1, :]
    g0, g1, gf = norm_g[0:1], norm_g[1:2], final_g.reshape(1, D)
    cb, ba, bx, lam = hy_conv_b, lru_b_a, lru_b_x, lru_lambda

    big = [hy_w_in[0], hy_w_out[0], pool_w_in[0], pool_w_grp[0].reshape(4 * 128, gd), pool_w_out[0]]
    cidx = ac.reshape(1).astype(jnp.int32)
    kidx = chip.reshape(1).astype(jnp.int32)
    w_in0, w_out0 = _wgather([_wcast_own_block(w, kidx, f"wcast_own_block_{a}") for a, w in enumerate(big[:2])],
                             "wgather_l0")
    w_in1, w_grp, w_out1 = _wgather_sequencer(
        [_wcast_own_block(w, kidx, f"wcast_own_block_{a + 2}", after=w_out0) for a, w in enumerate(big[2:])], "wgather_l1")
    w_grp =w_grp.reshape(N_CHIP, 4, 128, gd).transpose(1, 0, 2, 3).reshape(4, gd, gd)
    wa_b, wx_b = _wcast([lru_w_a[0], lru_w_x[0]])

    x1, hst, y0, h0, proj0 = _l0_fwd(xs, g0, sc0, sh0, w_in0, gt0, cw, cb, wa_b, ba, wx_b, bx, lam, sw,
                                     w_out0.reshape(2 * D, D))
    dpool, mixed, y1, dx2, losscols, dgf, h1, proj1 = _l1_fwd(x1, g1, sc1, sh1, w_in1, tgt, gt1, w_grp, pool_b, pool_s,
                                                              w_out1.reshape(2 * D, D), gf)

    def add_halves(grads, got, tag, after):
        parts = []
        for a, (g, r) in enumerate(zip(grads, got)):
            parts.append(_add_half(g, r, cidx, f"grad_add_half_{tag}{a}", parts[-1] if parts else after))
        return parts

    def add_owners(parts, got, tag, ids, after):
        own = []
        for a, (p, r) in enumerate(zip(parts, got)):
            own.append(_add_owner(p, r, kidx, f"grad_add_owner_{tag}{a}", own[-1] if own else after))
        return own, _sib_exchange(own, f"grad_sib_exchange_{tag}", ids[2])

    dproj1, mt1, d_wgrp, dsc1, dbg1 = _l1_bwd_mix(dx2, proj1, mixed, y1, dpool, gt1, w_grp, pool_s,
                                                  w_out1.reshape(2 * D, D))
    d_win1 = _wgrad(h1, dproj1, N_CHIP, D, D, lambda g: 0, lambda g: g, "l1_wgrad_in")
    dx1, s1_1, s2_1 = _dgrad_norm(dproj1, w_in1, x1, dx2, g1, sc1, "l1_bwd_proj")
    d_wout1, dgate1 = _wo_final(mt1, w_out1, gt1, "l1_wo_final")
    d_wgrp = d_wgrp.reshape(4, N_CHIP, 128, gd).transpose(1, 0, 2, 3).reshape(N_CHIP, 4 * 128, gd)
    grads_l1 = [d_win1, d_wgrp, d_wout1]
    got_l1 = _sib_send_halves(grads_l1, "grad_sib_halves_l1", CIDS_L1[0])

    dproj0, mt0, d_wa, d_wx, sm0 = _l0_bwd_mix(dx1, proj0, hst, y0, gt0, cw, cb, wa_b, ba, wx_b, bx, lam, sw,
                                               w_out0.reshape(2 * D, D))
    parts_l1 = add_halves(grads_l1, got_l1, "l1", after=sm0)
    got_l1 = _chip_scatter(parts_l1, "grad_chip_scatter_l1", CIDS_L1[1])
    d_win0 = _wgrad(h0, dproj0, N_CHIP, D, 6 * D // N_CHIP, lambda g: 0, lambda g: g, "l0_wgrad_in", after=parts_l1[-1])
    d_wout0, dgate0 = _wo_final(mt0, w_out0, gt0, "l0_wo_final")
    halves_l1, sib_l1 = add_owners(parts_l1, got_l1, "l1", CIDS_L1, after=d_win0)
    grads_l0 = [d_win0, d_wout0]
    parts_l0 = add_halves(grads_l0, _sib_send_halves(grads_l0, "grad_sib_halves_l0", CIDS_L0[0], after=got_l1[0]),
                          "l0", after=None)
    got_l0 = _chip_scatter(parts_l0, "grad_chip_scatter_l0", CIDS_L0[1])
    grad_x, s1_0, s2_0 = _dgrad_norm(dproj0, w_in0, xs, dx1, g0, sc0, "l0_bwd_proj", after=parts_l0[0])
    halves_l0, sib_l0 = add_owners(parts_l0, got_l0, "l0", CIDS_L0, after=s1_0)

    buf_a, dmod8, loss8 = _small_pack(s1_0, s2_0, s1_1, s2_1, sm0, dsc1, dbg1, dgf, losscols, dgate0, dgate1,
                                      norm_g, sc0, sc1, lam)
    hw = LRU_HEADS * LRU_HEAD_DIM
    buf_b = jnp.concatenate([d_wa.reshape(hw, LRU_HEAD_DIM), d_wx.reshape(hw, LRU_HEAD_DIM)], axis=0)
    red_a, red_b, dm_all = _small_comm(buf_a, buf_b, dmod8)
    small = [(norm_g, m_norm_g, v_norm_g), (mod_b, m_mod_b, v_mod_b),
             (hy_conv_w[0], m_hy_conv_w[0], v_hy_conv_w[0]), (hy_conv_b, m_hy_conv_b, v_hy_conv_b),
             tuple(a.reshape(hw, LRU_HEAD_DIM) for a in (lru_w_a, m_lru_w_a, v_lru_w_a)),
             (lru_b_a, m_lru_b_a, v_lru_b_a),
             tuple(a.reshape(hw, LRU_HEAD_DIM) for a in (lru_w_x, m_lru_w_x, v_lru_w_x)),
             (lru_b_x, m_lru_b_x, v_lru_b_x), (lru_lambda, m_lru_lambda, v_lru_lambda),
             (sc_conv_w[0], m_sc_conv_w[0], v_sc_conv_w[0]), (pool_b_grp, m_pool_b_grp, v_pool_b_grp),
             (pool_scale, m_pool_scale, v_pool_scale),
             tuple(a.reshape(1, D) for a in (final_g, m_final_g, v_final_g))]
    small_names = ["norm_g", "mod_b", "hy_conv_w", "hy_conv_b", "lru_w_a", "lru_b_a", "lru_w_x", "lru_b_x",
                   "lru_lambda", "sc_conv_w", "pool_b_grp", "pool_scale", "final_g"]
    small_out = _small_adam(red_a, red_b, dm_all, small)
    res = {}
    shapes = dict(norm_g=norm_g, mod_b=mod_b, hy_conv_w=hy_conv_w, hy_conv_b=hy_conv_b, lru_w_a=lru_w_a, lru_b_a=lru_b_a,
                  lru_w_x=lru_w_x, lru_b_x=lru_b_x, lru_lambda=lru_lambda, sc_conv_w=sc_conv_w, pool_b_grp=pool_b_grp,
                  pool_scale=pool_scale, final_g=final_g)
    for p, nm in enumerate(small_names):
        res[nm] = tuple(o.reshape(shapes[nm].shape) for o in small_out[4 * p:4 * p + 4])

    nw = mod_w.shape[2]
    dm_sh = jnp.stack([lax.dynamic_slice_in_dim(dm_all[:, l * 3 * D:(l + 1) * 3 * D], chip * nw, nw, axis=1)
                       for l in range(2)])
    res["mod_w"] = tuple(_modw_adam(ca_all.T, dm_sh, mod_w, m_mod_w, v_mod_w))

    halves = list(halves_l0) + list(halves_l1)
    sib_halves = list(sib_l0) + list(sib_l1)
    big_names = ["hy_w_in", "hy_w_out", "pool_w_in", "pool_w_grp", "pool_w_out"]
    big_wmv = [(hy_w_in, m_hy_w_in, v_hy_w_in), (hy_w_out, m_hy_w_out, v_hy_w_out), (pool_w_in, m_pool_w_in, v_pool_w_in),
               (pool_w_grp, m_pool_w_grp, v_pool_w_grp), (pool_w_out, m_pool_w_out, v_pool_w_out)]
    for a, nm in enumerate(big_names):
        rr, cc = big[a].shape
        w, m, v = (t.reshape(rr, cc) for t in big_wmv[a])
        outs = _adam_2d(w, halves[a], sib_halves[a], m, v, cidx, f"adam_{nm}")
        res[nm] = tuple(o.reshape(big_wmv[a][0].shape) for o in outs)

    loss = lax.psum(loss8[0, 0], ("x", "y", "c"))
    order = ["norm_g", "mod_w", "mod_b", "hy_w_in", "hy_conv_w", "hy_conv_b", "lru_w_a", "lru_b_a", "lru_w_x", "lru_b_x",
             "lru_lambda", "sc_conv_w", "hy_w_out", "pool_w_in", "pool_w_grp", "pool_b_grp", "pool_scale", "pool_w_out",
             "final_g"]
    return (loss, grad_x[None], *[res[nm][0] for nm in order], *[res[nm][1] for nm in order],
            *[res[nm][2] for nm in order], *[res[nm][3] for nm in order])
```

```python
import jax
import jax.numpy as jnp
from jax import lax
from jax.experimental import pallas as pl
from jax.experimental.pallas import tpu as pltpu
from jax.experimental.pallas import tpu_sc as plsc

F32, BF16 = jnp.float32, jnp.bfloat16
D = 1024
RMS_EPS = 1e-6
SQRT_FLOOR = 1e-30
LRU_C = 8.0
LRU_HEADS, LRU_HEAD_DIM = 8, 128
POOL_WINDOWS = (2, 4, 8, 16)
POOL_GROUP_DIM = 512
ADAM_LR, ADAM_B1, ADAM_B2, ADAM_EPS, ADAM_WD, ADAM_STEP = 0.001, 0.9, 0.999, 1e-08, 0.01, 10
MESH = pl.DeviceIdType.MESH
CIDS_WGATHER = (1, 8)
CIDS_L1 = (2, 3)
CIDS_L0 = (4, 5)
N_DEV, N_CHIP = 8, 4
SUBLANES = 8
BF16_ROWS = 16
POOL_HALO = 16
TS_MIX, TS_WGRAD, TS_DGRAD = 256, 1024, 256
SMALL_ROWS = 64
GRAD_WIRE_DTYPE = BF16
ANY = pl.BlockSpec(memory_space=pl.ANY)
VMEM = pl.BlockSpec(memory_space=pltpu.VMEM)
NT = (((1,), (1,)), ((), ()))
TN = (((0,), (0,)), ((), ()))


def _cp(sem=None, vmem_mb=56):
    kw = dict(vmem_limit_bytes=vmem_mb * 2 ** 20)
    if sem is not None:
        kw["dimension_semantics"] = sem
    return pltpu.CompilerParams(**kw)


def _tile(n, t):
    return min(n, t)


def _pos():
    return lax.axis_index("x"), lax.axis_index("y"), lax.axis_index("c")


def _flip(v, f):
    return 1 - v if f else v


def _sigmoid(z):
    return 0.5 * jnp.tanh(0.5 * z) + 0.5


def _rows(n, c):
    return lax.broadcasted_iota(jnp.int32, (n, c), 0)


def _down(a, d):
    return a if d == 0 else pltpu.roll(a, d, 0)


def _up(a, d):
    return a if d == 0 else pltpu.roll(a, a.shape[0] - d, 0)


def _scan_fwd_steps(a, u, carry):
    n, c = a.shape
    sub = _rows(SUBLANES, c)
    out = []
    for k in range(n // SUBLANES):
        p = a[k * SUBLANES:(k + 1) * SUBLANES]
        g = u[k * SUBLANES:(k + 1) * SUBLANES]
        for d in (1, 2, 4):
            keep = sub >= d
            g = g + p * jnp.where(keep, pltpu.roll(g, d, 0), 0.0)
            p = p * jnp.where(keep, pltpu.roll(p, d, 0), 1.0)
        h = g + p * carry
        carry = h[SUBLANES - 1:SUBLANES, :]
        out.append(h)
        yield
    return jnp.concatenate(out, axis=0)


def _scan_rev_steps(alpha, b, carry):
    n, c = alpha.shape
    sub = _rows(SUBLANES, c)
    out = []
    for k in reversed(range(n // SUBLANES)):
        p = alpha[k * SUBLANES:(k + 1) * SUBLANES]
        g = b[k * SUBLANES:(k + 1) * SUBLANES]
        for d in (1, 2, 4):
            keep = sub < SUBLANES - d
            g = g + p * jnp.where(keep, pltpu.roll(g, SUBLANES - d, 0), 0.0)
            p = p * jnp.where(keep, pltpu.roll(p, SUBLANES - d, 0), 1.0)
        h = g + p * carry
        carry = h[0:1, :]
        out.append(h)
        yield
    return jnp.concatenate(out[::-1], axis=0)


def _run(steps):
    while True:
        try:
            next(steps)
        except StopIteration as done:
            return done.value


def _paired(progress, pieces):
    n, done = len(pieces), 1
    pieces[0]()
    for frac in progress:
        while done < n and done <= frac * n:
            pieces[done]()
            done += 1
    while done < n:
        pieces[done]()
        done += 1


def _conv_taps(ext, halo, n, width):
    return [_down(ext, width - 1 - k)[halo:halo + n] for k in range(width)]


def _lru_gates(xc, wa_ref, ba, wx_ref, bx):
    xb = xc.astype(BF16)
    pa, px = [], []
    for h in range(LRU_HEADS):
        xh = xb[:, h * LRU_HEAD_DIM:(h + 1) * LRU_HEAD_DIM]
        pa.append(jnp.dot(xh, wa_ref[h], preferred_element_type=F32))
        px.append(jnp.dot(xh, wx_ref[h], preferred_element_type=F32))
    r = _sigmoid(jnp.concatenate(pa, axis=1) + ba)
    ig = _sigmoid(jnp.concatenate(px, axis=1) + bx)
    return r, ig


def _softplus_neg(lam):
    return jnp.maximum(-lam, 0.0) + jnp.log1p(jnp.exp(-jnp.abs(lam)))


def _recip_1_to_2(d):
    r0 = pl.reciprocal(d, approx=True)
    return r0 * (2.0 - d * r0)


def _lru_decay(r, sp, first):
    big_l = (-LRU_C) * r * sp
    a = jnp.exp(big_l)
    th = jnp.tanh(big_l)
    q = (-2.0 * th) * _recip_1_to_2(1.0 - th)
    rs = lax.rsqrt(jnp.maximum(q, SQRT_FLOOR))
    return a, jnp.where(first, 1.0, q * rs), rs


def _pool_inv_counts(t0, n):
    t = (t0 + lax.broadcasted_iota(jnp.int32, (n, 1), 0) + 1).astype(F32)
    return [1.0 / jnp.minimum(t, float(w)) for w in POOL_WINDOWS]


def _window_sums(ext, shift):
    gd = POOL_GROUP_DIM
    out = []
    s = ext
    for k in range(len(POOL_WINDOWS)):
        s = s + shift(s, 2 ** k)
        out.append(s[:, 0:gd])
        if k + 1 < len(POOL_WINDOWS):
            s = s[:, gd:]
    return out


SW_ROWS, SW_COLS = 16, 2 * D
SW_CONV, SW_SC, SW_POOL_B, SW_POOL_S = 0, 4, 8, 9


def _mod_fwd(c8, mod_w, mod_b, conv_w, sc_w, pool_b, pool_s):
    nw = mod_w.shape[2]
    cq, pq = conv_w.shape[1], pool_b.shape[1]

    def body(c_ref, w_ref, b_ref, cw_ref, sw_ref, pb_ref, ps_ref, ca_ref, mod_ref, small_ref,
             cslot, mslot, msend, pslot, psend, s1, r1, s2, r2, s3, r3):
        x, y, c = _pos()
        me = 4 * x + 2 * y + c
        chip = 2 * x + y
        first = []
        for r in range(1, N_DEV):
            fx, fy, fc = (r >> 2) & 1, (r >> 1) & 1, r & 1
            cp = pltpu.make_async_remote_copy(
                src_ref=c_ref, dst_ref=cslot.at[me], send_sem=s1.at[r - 1], recv_sem=r1.at[r - 1],
                device_id=(_flip(x, fx), _flip(y, fy), _flip(c, fc)), device_id_type=MESH)
            cp.start()
            first.append(cp)
        cslot[me] = c_ref[...]
        for cp in first:
            cp.wait()
        rows = _rows(SUBLANES, D)
        call = jnp.zeros((SUBLANES, D), F32)
        for d in range(N_DEV):
            call = jnp.where(rows == d, cslot[d], call)
        ca = call * _sigmoid(call)
        ca_ref[...] = ca
        for l in range(2):
            msend[l] = jnp.dot(ca, w_ref[l], precision=lax.Precision.HIGHEST, preferred_element_type=F32)
        psend[...] = jnp.zeros_like(psend)
        psend[SW_CONV:SW_CONV + 4, 0:cq] = cw_ref[...]
        psend[SW_SC:SW_SC + 3, 0:cq] = sw_ref[...]
        psend[SW_POOL_B:SW_POOL_B + 1, :] = pb_ref[...]
        psend[SW_POOL_S:SW_POOL_S + 1, :] = ps_ref[...]
        second = []
        for q, (fx, fy) in enumerate(((1, 0), (0, 1), (1, 1))):
            peer = (_flip(x, fx), _flip(y, fy), c)
            for src, dst, ss, rs in ((msend, mslot, s2, r2), (psend, pslot, s3, r3)):
                cp = pltpu.make_async_remote_copy(src_ref=src, dst_ref=dst.at[chip], send_sem=ss.at[q], recv_sem=rs.at[q],
                                                  device_id=peer, device_id_type=MESH)
                cp.start()
                second.append(cp)
        mslot[chip] = msend[...]
        pslot[chip] = psend[...]
        for cp in second:
            cp.wait()
        small_ref[...] = jnp.zeros_like(small_ref)
        for j in range(N_CHIP):
            for l in range(2):
                mod_ref[l, :, j * nw:(j + 1) * nw] = mslot[j, l] + b_ref[l:l + 1, j * nw:(j + 1) * nw]
            small_ref[0:SUBLANES, j * cq:(j + 1) * cq] = pslot[j, 0:SUBLANES, 0:cq]
            small_ref[SUBLANES:SW_ROWS, j * pq:(j + 1) * pq] = pslot[j, SUBLANES:SW_ROWS, :]

    args = (c8, mod_w, mod_b, conv_w, sc_w, pool_b, pool_s)
    dma3 = pltpu.SemaphoreType.DMA((N_CHIP - 1,))
    return pl.pallas_call(
        body, name="mod_fwd",
        in_specs=[VMEM] * len(args), out_specs=[VMEM] * 3,
        out_shape=[jax.ShapeDtypeStruct((SUBLANES, D), F32), jax.ShapeDtypeStruct((2, SUBLANES, N_CHIP * nw), F32),
                   jax.ShapeDtypeStruct((SW_ROWS, SW_COLS), F32)],
        scratch_shapes=[pltpu.VMEM((N_DEV, SUBLANES, D), F32), pltpu.VMEM((N_CHIP, 2, SUBLANES, nw), F32),
                        pltpu.VMEM((2, SUBLANES, nw), F32), pltpu.VMEM((N_CHIP, SW_ROWS, pq), F32),
                        pltpu.VMEM((SW_ROWS, pq), F32),
                        pltpu.SemaphoreType.DMA((N_DEV - 1,)), pltpu.SemaphoreType.DMA((N_DEV - 1,)),
                        dma3, dma3, dma3, dma3],
        compiler_params=_cp(),
    )(*args)


def _wcast(ws):
    def body(*refs):
        n = len(refs) // 2
        for a in range(n):
            refs[n + a][...] = refs[a][...].astype(BF16)

    return pl.pallas_call(
        body, name="wcast", in_specs=[VMEM] * len(ws), out_specs=[VMEM] * len(ws),
        out_shape=[jax.ShapeDtypeStruct(w.shape, BF16) for w in ws], compiler_params=_cp(),
    )(*ws)


def _wcast_own_block(w, kidx, name, after=None):
    rr, cc = w.shape
    rb = min(rr, 256)

    def body(k_ref, w_ref, *rest):
        rest[-1][...] = w_ref[...].astype(BF16)

    order = [] if after is None else [after]
    return pl.pallas_call(
        body, name=name,
        grid_spec=pltpu.PrefetchScalarGridSpec(
            num_scalar_prefetch=1, grid=(rr // rb,),
            in_specs=[pl.BlockSpec((rb, cc), lambda j, k_ref: (j, 0))] + [ANY] * len(order),
            out_specs=pl.BlockSpec((None, rb, cc), lambda j, k_ref: (k_ref[0], j, 0))),
        out_shape=jax.ShapeDtypeStruct((N_CHIP, rr, cc), BF16),
        compiler_params=_cp(("parallel",)),
    )(kidx, w, *order)


def _wgather_copies(outs, rows, ssem, rsem, fssem, frsem):
    n = len(outs)
    x, y, c = _pos()
    chip = 2 * x + y
    sib = (x, y, 1 - c)
    flips = ((1, 0), (0, 1), (1, 1))

    def half(a, which):
        hr = rows[a] // 2
        return pl.ds(pl.multiple_of(which * hr, BF16_ROWS), hr)

    sends = []
    for a in range(n):
        mine = outs[a].at[chip, half(a, c), :]
        for q, (fx, fy) in enumerate(flips):
            cp = pltpu.make_async_remote_copy(
                src_ref=mine, dst_ref=mine, send_sem=ssem.at[3 * a + q], recv_sem=rsem.at[3 * a + q],
                device_id=(_flip(x, fx), _flip(y, fy), c), device_id_type=MESH)
            cp.start()
            sends.append(cp)
    passed = []
    for a in range(n):
        for q, (fx, fy) in enumerate(flips):
            src_chip = 2 * _flip(x, fx) + _flip(y, fy)
            landed = outs[a].at[src_chip, half(a, c), :]
            pltpu.make_async_remote_copy(
                src_ref=landed, dst_ref=landed, send_sem=ssem.at[3 * a + q], recv_sem=rsem.at[3 * a + q],
                device_id=sib, device_id_type=MESH).wait_recv()
            cp = pltpu.make_async_remote_copy(
                src_ref=landed, dst_ref=landed, send_sem=fssem.at[3 * a + q], recv_sem=frsem.at[3 * a + q],
                device_id=sib, device_id_type=MESH)
            cp.start()
            passed.append(cp)
    for a in range(n):
        for q, (fx, fy) in enumerate(flips):
            src_chip = 2 * _flip(x, fx) + _flip(y, fy)
            other = outs[a].at[src_chip, half(a, 1 - c), :]
            pltpu.make_async_remote_copy(
                src_ref=other, dst_ref=other, send_sem=fssem.at[3 * a + q], recv_sem=frsem.at[3 * a + q],
                device_id=sib, device_id_type=MESH).wait_recv()
    for cp in sends + passed:
        cp.wait_send()


def _wgather_sequencer(bufs, name, collective_id):
    n = len(bufs)
    refs = [jax.new_ref(b, memory_space=pltpu.MemorySpace.HBM) for b in bufs]
    dma = pltpu.SemaphoreType.DMA((3 * n,))

    @pl.kernel(mesh=plsc.ScalarSubcoreMesh(axis_name="sequencer", num_cores=1), name=name,
               scratch_types=(dma, dma, dma, dma), compiler_params=pltpu.CompilerParams(collective_id=collective_id))
    def launch(ssem, rsem, fssem, frsem):
        x, y, c = _pos()
        barrier = pltpu.get_barrier_semaphore()
        for peer in ((1 - x, y, c), (x, 1 - y, c), (1 - x, 1 - y, c), (x, y, 1 - c)):
            pl.semaphore_signal(barrier, inc=1, device_id=peer, device_id_type=MESH)
        pl.semaphore_wait(barrier, 4)
        _wgather_copies(refs, [b.shape[1] for b in bufs], ssem, rsem, fssem, frsem)

    launch()
    return [r[...] for r in refs]


def _l0_fwd(x, g, sc, sh, w_in, gate, cw, cb, wa, ba, wx, bx, lam, sw, wo):
    s_len, nb = x.shape[0], w_in.shape[2]
    ts = _tile(s_len, TS_MIX)
    n_t = s_len // ts
    hl = SUBLANES

    def body(xa_ref, xb_ref, g_ref, sc_ref, sh_ref, win_ref, gate_ref, cw_ref, cb_ref, wa_ref, ba_ref, wx_ref, bx_ref,
             lam_ref, sw_ref, wo_ref, x1_ref, h_ref, y_ref, h0_ref, p_ref, pcur, pnext, cxa, czz, chh):
        i = pl.program_id(0)

        @pl.when(i == 0)
        def _():
            cxa[...] = jnp.zeros_like(cxa)
            czz[...] = jnp.zeros_like(czz)
            chh[...] = jnp.zeros_like(chh)
            pnext[...] = jnp.zeros_like(pnext)

        pcur[...] = pnext[...]
        xv = xa_ref[...]
        rinv = lax.rsqrt(jnp.mean(xv * xv, axis=-1, keepdims=True) + RMS_EPS)
        h0 = (xv * rinv * (g_ref[...] * (1.0 + sc_ref[...])) + sh_ref[...]).astype(BF16)
        h0_ref[...] = h0

        def project(k):
            def emit():
                pk = jnp.dot(h0, win_ref[k], preferred_element_type=F32).astype(BF16)
                p_ref[:, k * nb:(k + 1) * nb] = pk
                pnext[:, k * nb:(k + 1) * nb] = pk
            return emit

        def mixer():
            piece = lambda k: pcur[:, k * D:(k + 1) * D].astype(F32)
            xa = piece(0)
            rows = _rows(ts, D)
            taps = _conv_taps(jnp.concatenate([cxa[...], xa], axis=0), hl, ts, 4)
            xc = cb_ref[...] + sum(cw_ref[k:k + 1, :] * taps[k] for k in range(4))
            r, ig = _lru_gates(xc, wa_ref, ba_ref[...], wx_ref, bx_ref[...])
            a, m, _ = _lru_decay(r, _softplus_neg(lam_ref[...]), (rows == 0) & (i == 1))
            yield 0.26
            h = _run(_scan_fwd_steps(a, m * ig * xc, chh[hl - 1:hl, :]))
            yield 0.51
            gcp, v = piece(3), piece(4)
            z = gcp * v
            ztaps = _conv_taps(jnp.concatenate([czz[...], z], axis=0), hl, ts, 3)
            yb = piece(2) * sum(sw_ref[k:k + 1, :] * ztaps[k] for k in range(3))
            ga, gb = piece(1), piece(5)
            y = jnp.concatenate([h * (ga * _sigmoid(ga)), yb * (gb * _sigmoid(gb))], axis=1).astype(BF16)
            yield 0.76
            y_ref[...] = y
            x1_ref[...] = xb_ref[...] + gate_ref[...] * jnp.dot(y, wo_ref[...], preferred_element_type=F32)
            h_ref[...] = h.astype(BF16)
            cxa[...] = xa[ts - hl:, :]
            czz[...] = z[ts - hl:, :]
            chh[...] = jnp.where(i > 0, h[ts - hl:, :], 0.0)

        _paired(mixer(), [project(k) for k in range(N_CHIP)])

    def full(a):
        return pl.BlockSpec(a.shape, lambda i: (0,) * a.ndim)

    ahead = lambda w: pl.BlockSpec((ts, w), lambda i: (jnp.minimum(i, n_t - 1), 0))
    behind = lambda w: pl.BlockSpec((ts, w), lambda i: (jnp.maximum(i - 1, 0), 0))
    args = (x, x, g, sc, sh, w_in, gate, cw, cb, wa, ba, wx, bx, lam, sw, wo)
    return pl.pallas_call(
        body, name="l0_fwd", grid=(n_t + 1,),
        in_specs=[ahead(D), behind(D)] + [full(a) for a in args[2:]],
        out_specs=[behind(D), behind(D), behind(2 * D), ahead(D), ahead(N_CHIP * nb)],
        out_shape=[jax.ShapeDtypeStruct((s_len, D), F32), jax.ShapeDtypeStruct((s_len, D), BF16),
                   jax.ShapeDtypeStruct((s_len, 2 * D), BF16), jax.ShapeDtypeStruct((s_len, D), BF16),
                   jax.ShapeDtypeStruct((s_len, N_CHIP * nb), BF16)],
        scratch_shapes=[pltpu.VMEM((ts, N_CHIP * nb), BF16)] * 2 + [pltpu.VMEM((hl, D), F32)] * 3,
        compiler_params=_cp(("arbitrary",)),
    )(*args)


def _l1_fwd(x1, g, sc, sh, w_in, tgt, gate, wg, bg, scale, wo, gf):
    s_len, nb = x1.shape[0], w_in.shape[2]
    ts = _tile(s_len, TS_MIX)
    n_t = s_len // ts
    pw, gd, hl = 2 * D, POOL_GROUP_DIM, POOL_HALO

    def body(xa_ref, xb_ref, t_ref, g_ref, sc_ref, sh_ref, win_ref, gate_ref, wg_ref, bg_ref, scl_ref, wo_ref, gf_ref,
             d_ref, mx_ref, y_ref, dx_ref, loss_ref, dgf_ref, h1_ref, p_ref, pcur, pnext, cv):
        i = pl.program_id(0)

        @pl.when(i == 0)
        def _():
            cv[...] = jnp.zeros_like(cv)
            loss_ref[...] = jnp.zeros_like(loss_ref)
            dgf_ref[...] = jnp.zeros_like(dgf_ref)
            pnext[...] = jnp.zeros_like(pnext)

        pcur[...] = pnext[...]
        xv = xa_ref[...]
        rinv = lax.rsqrt(jnp.mean(xv * xv, axis=-1, keepdims=True) + RMS_EPS)
        h1 = (xv * rinv * (g_ref[...] * (1.0 + sc_ref[...])) + sh_ref[...]).astype(BF16)
        h1_ref[...] = h1

        def project(k):
            def emit():
                pk = jnp.dot(h1, win_ref[k], preferred_element_type=F32).astype(BF16)
                p_ref[:, k * nb:(k + 1) * nb] = pk
                pnext[:, k * nb:(k + 1) * nb] = pk
            return emit

        def mixer():
            v = pcur[:, 0:pw].astype(F32)
            sums = _window_sums(jnp.concatenate([cv[...], v], axis=0), _down)
            inv = _pool_inv_counts(jnp.maximum(i - 1, 0) * ts, ts)
            dd = [sums[k][hl:hl + ts] * inv[k] - v[:, k * gd:(k + 1) * gd] for k in range(4)]
            d_ref[...] = jnp.concatenate(dd, axis=1).astype(BF16)
            yield 0.26
            mixed = jnp.concatenate(
                [jnp.dot(dd[k].astype(BF16), wg_ref[k], preferred_element_type=F32) for k in range(4)], axis=1) + bg_ref[...]
            mx_ref[...] = mixed.astype(BF16)
            gg = pcur[:, pw:2 * pw].astype(F32)
            y = (mixed * scl_ref[...] * (gg * _sigmoid(gg))).astype(BF16)
            y_ref[...] = y
            yield 0.51
            x2 = xb_ref[...] + gate_ref[...] * jnp.dot(y, wo_ref[...], preferred_element_type=F32)
            yield 0.76
            r2 = lax.rsqrt(jnp.mean(x2 * x2, axis=-1, keepdims=True) + RMS_EPS)
            n2 = x2 * r2
            err = n2 * gf_ref[...] - t_ref[...]
            loss_ref[...] += jnp.where(i > 0, jnp.sum(err * err, axis=0, keepdims=True), 0.0)
            dyf = err * (1.0 / D)
            dgf_ref[...] += jnp.where(i > 0, jnp.sum(dyf * n2, axis=0, keepdims=True), 0.0)
            dn = dyf * gf_ref[...]
            dx_ref[...] = r2 * (dn - n2 * jnp.mean(dn * n2, axis=-1, keepdims=True))
            cv[...] = v[ts - hl:, :]

        _paired(mixer(), [project(k) for k in range(N_CHIP)])

    def full(a):
        return pl.BlockSpec(a.shape, lambda i: (0,) * a.ndim)

    ahead = lambda w: pl.BlockSpec((ts, w), lambda i: (jnp.minimum(i, n_t - 1), 0))
    behind = lambda w: pl.BlockSpec((ts, w), lambda i: (jnp.maximum(i - 1, 0), 0))
    acc = pl.BlockSpec((1, D), lambda i: (0, 0))
    args = (x1, x1, tgt, g, sc, sh, w_in, gate, wg, bg, scale, wo, gf)
    return pl.pallas_call(
        body, name="l1_fwd", grid=(n_t + 1,),
        in_specs=[ahead(D), behind(D), behind(D)] + [full(a) for a in args[3:]],
        out_specs=[behind(pw), behind(pw), behind(pw), behind(D), acc, acc, ahead(D), ahead(N_CHIP * nb)],
        out_shape=[jax.ShapeDtypeStruct((s_len, pw), BF16)] * 3 + [jax.ShapeDtypeStruct((s_len, D), F32)]
        + [jax.ShapeDtypeStruct((1, D), F32)] * 2
        + [jax.ShapeDtypeStruct((s_len, D), BF16), jax.ShapeDtypeStruct((s_len, N_CHIP * nb), BF16)],
        scratch_shapes=[pltpu.VMEM((ts, N_CHIP * nb), BF16)] * 2 + [pltpu.VMEM((hl, pw), F32)],
        compiler_params=_cp(("arbitrary",)),
    )(*args)


def _l1_bwd_mix(dx2, proj, mixed, y, dpool, gate, wg, scale, wo):
    s_len = dx2.shape[0]
    ts = _tile(s_len, TS_MIX)
    n_t = s_len // ts
    pw, gd, hl = 2 * D, POOL_GROUP_DIM, POOL_HALO

    def body(dx_ref, gg_ref, mx_ref, y_ref, d_ref, gate_ref, wg_ref, sc_ref, wo_ref,
             dp_ref, mt_ref, dwg_ref, dsc_ref, dbg_ref, cq):
        i = pl.program_id(0)

        @pl.when(i == 0)
        def _():
            cq[...] = jnp.zeros_like(cq)
            dsc_ref[...] = jnp.zeros_like(dsc_ref)
            dbg_ref[...] = jnp.zeros_like(dbg_ref)
            mt_ref[...] = jnp.zeros_like(mt_ref)
            dwg_ref[...] = jnp.zeros_like(dwg_ref)

        dxv = dx_ref[...]
        dxb = dxv.astype(BF16)

        def wgrad_out(k):
            mt_ref[k] += lax.dot_general(y_ref[:, k * gd:(k + 1) * gd], dxb, TN, preferred_element_type=F32)

        dy = lax.dot_general((gate_ref[...] * dxv).astype(BF16), wo_ref[...], NT, preferred_element_type=F32)
        wgrad_out(0)
        gg = gg_ref[...].astype(F32)
        mixed = mx_ref[...].astype(F32)
        s = _sigmoid(gg)
        sg = gg * s
        dmixed = dy * sc_ref[...] * sg
        dsc_ref[...] += jnp.sum(dy * mixed * sg, axis=0, keepdims=True)
        dbg_ref[...] += jnp.sum(dmixed, axis=0, keepdims=True)
        dmb = dmixed.astype(BF16)
        wgrad_out(1)
        dp_ref[:, pw:2 * pw] = (dy * sc_ref[...] * mixed * (s * (1.0 + gg * (1.0 - s)))).astype(BF16)
        inv = _pool_inv_counts((n_t - 1 - i) * ts, ts)
        dd = []
        for k in range(4):
            dmk = dmb[:, k * gd:(k + 1) * gd]
            dd.append(lax.dot_general(dmk, wg_ref[k], NT, preferred_element_type=F32))
            dwg_ref[k] += lax.dot_general(d_ref[:, k * gd:(k + 1) * gd], dmk, TN, preferred_element_type=F32)
        wgrad_out(2)
        q = jnp.concatenate([dd[k] * inv[k] for k in range(4)], axis=1)
        sums = _window_sums(jnp.concatenate([q, cq[...]], axis=0), _up)
        wgrad_out(3)
        dp_ref[:, 0:pw] = jnp.concatenate([sums[k][0:ts] - dd[k] for k in range(4)], axis=1).astype(BF16)
        cq[...] = q[0:hl, :]

    def full(a):
        return pl.BlockSpec(a.shape, lambda i: (0,) * a.ndim)

    rev = lambda w, j=0: pl.BlockSpec((ts, w), lambda i: (n_t - 1 - i, j))
    acc = pl.BlockSpec((1, pw), lambda i: (0, 0))
    return pl.pallas_call(
        body, name="l1_bwd_mix", grid=(n_t,),
        in_specs=[rev(D), rev(pw, 1), rev(pw), rev(pw), rev(pw)] + [full(a) for a in (gate, wg, scale, wo)],
        out_specs=[rev(2 * pw), pl.BlockSpec((N_CHIP, gd, D), lambda i: (0, 0, 0)),
                   pl.BlockSpec((4, gd, gd), lambda i: (0, 0, 0)), acc, acc],
        out_shape=[jax.ShapeDtypeStruct((s_len, 2 * pw), BF16), jax.ShapeDtypeStruct((N_CHIP, gd, D), F32),
                   jax.ShapeDtypeStruct((4, gd, gd), F32),
                   jax.ShapeDtypeStruct((1, pw), F32), jax.ShapeDtypeStruct((1, pw), F32)],
        scratch_shapes=[pltpu.VMEM((hl, pw), F32)],
        compiler_params=_cp(("arbitrary",)),
    )(dx2, proj, mixed, y, dpool, gate, wg, scale, wo)


def _l0_bwd_mix(dx1, proj, hst, y, gate, cw, cb, wa, ba, wx, bx, lam, sw, wo):
    s_len = dx1.shape[0]
    ts = _tile(s_len, TS_MIX)
    n_t = s_len // ts
    hl, hb = SUBLANES, BF16_ROWS
    yb_w = 2 * D // N_CHIP

    def body(dx_ref, p_ref, ph_ref, h_ref, hh_ref, y_ref, gate_ref, cw_ref, cb_ref, wa_ref, ba_ref, wx_ref, bx_ref,
             lam_ref, sw_ref, wo_ref, dp_ref, mt_ref, dwa_ref, dwx_ref, sm_ref, cg, cdxc, cdcz, ca):
        i = pl.program_id(0)
        ri = n_t - 1 - i

        @pl.when(i == 0)
        def _():
            cg[...] = jnp.zeros_like(cg)
            ca[...] = jnp.zeros_like(ca)
            cdxc[...] = jnp.zeros_like(cdxc)
            cdcz[...] = jnp.zeros_like(cdcz)
            sm_ref[...] = jnp.zeros_like(sm_ref)
            mt_ref[...] = jnp.zeros_like(mt_ref)
            dwa_ref[...] = jnp.zeros_like(dwa_ref)
            dwx_ref[...] = jnp.zeros_like(dwx_ref)

        dxb = dx_ref[...].astype(BF16)

        def wgrad_out(k):
            mt_ref[k] += lax.dot_general(y_ref[:, k * yb_w:(k + 1) * yb_w], dxb, TN, preferred_element_type=F32)

        wgrad_out(0)
        has_prev = (ri > 0).astype(F32)
        xa, ga, gbp, gcp, v, gb = [p_ref[:, k * D:(k + 1) * D].astype(F32) for k in range(6)]
        prev = lambda k: ph_ref[:, k * D:(k + 1) * D].astype(F32)[hb - hl:hb] * has_prev
        rows = _rows(ts, D)
        first = (rows == 0) & (ri == 0)
        xtaps = _conv_taps(jnp.concatenate([prev(0), xa], axis=0), hl, ts, 4)
        xc = cb_ref[...] + sum(cw_ref[k:k + 1, :] * xtaps[k] for k in range(4))
        r, ig = _lru_gates(xc, wa_ref, ba_ref[...], wx_ref, bx_ref[...])
        sp = _softplus_neg(lam_ref[...])
        a, m, inv_m = _lru_decay(r, sp, first)
        z = gcp * v
        ztaps = _conv_taps(jnp.concatenate([prev(3) * prev(4), z], axis=0), hl, ts, 3)
        cz = sum(sw_ref[k:k + 1, :] * ztaps[k] for k in range(3))
        h = h_ref[...].astype(F32)
        hprev = _down(jnp.concatenate([hh_ref[...].astype(F32)[hb - hl:hb] * has_prev, h], axis=0), 1)[hl:hl + ts]
        dy = lax.dot_general((gate_ref[...] * dx_ref[...]).astype(BF16), wo_ref[...], NT, preferred_element_type=F32)
        dya_pre, dyb_pre = dy[:, 0:D], dy[:, D:2 * D]
        s_a, s_b = _sigmoid(ga), _sigmoid(gb)
        dp_ref[:, D:2 * D] = (dya_pre * h * (s_a * (1.0 + ga * (1.0 - s_a)))).astype(BF16)
        dp_ref[:, 5 * D:6 * D] = (dyb_pre * (gbp * cz) * (s_b * (1.0 + gb * (1.0 - s_b)))).astype(BF16)
        dya = dya_pre * (ga * s_a)
        dyb = dyb_pre * (gb * s_b)
        wgrad_out(1)
        dp_ref[:, 2 * D:3 * D] = (dyb * cz).astype(BF16)
        dcz = dyb * gbp
        for k in range(3):
            sm_ref[8 + k:9 + k, :] += jnp.sum(dcz * ztaps[k], axis=0, keepdims=True)
        dcz_ext = jnp.concatenate([dcz, cdcz[...]], axis=0)
        dz = sum(sw_ref[k:k + 1, :] * _up(dcz_ext, 2 - k)[0:ts] for k in range(3))
        dp_ref[:, 3 * D:4 * D] = (dz * v).astype(BF16)
        dp_ref[:, 4 * D:5 * D] = (dz * gcp).astype(BF16)
        cdcz[...] = dcz[0:hl, :]
        alpha = _up(jnp.concatenate([a, ca[...]], axis=0), 1)[0:ts]
        wgrad_out(2)
        dh = _run(_scan_rev_steps(alpha, dya, cg[0:1, :]))
        wgrad_out(3)
        cg[...] = dh[0:hl, :]
        ca[...] = a[0:hl, :]
        da = dh * hprev
        dm = dh * ig * xc
        di = dh * m * xc
        dxc = dh * m * ig
        dl = da * a - jnp.where(first, 0.0, dm * (a * a) * inv_m)
        sm_ref[7:8, :] += jnp.sum(dl * r, axis=0, keepdims=True) * (-LRU_C)
        dpa = (dl * sp) * (-LRU_C) * r * (1.0 - r)
        dpx = di * ig * (1.0 - ig)
        sm_ref[5:6, :] += jnp.sum(dpa, axis=0, keepdims=True)
        sm_ref[6:7, :] += jnp.sum(dpx, axis=0, keepdims=True)
        dpa_b, dpx_b, xc_b = dpa.astype(BF16), dpx.astype(BF16), xc.astype(BF16)
        back = []
        for hd in range(LRU_HEADS):
            sl = slice(hd * LRU_HEAD_DIM, (hd + 1) * LRU_HEAD_DIM)
            back.append(lax.dot_general(dpa_b[:, sl], wa_ref[hd], NT, preferred_element_type=F32)
                        + lax.dot_general(dpx_b[:, sl], wx_ref[hd], NT, preferred_element_type=F32))
            dwa_ref[hd] += lax.dot_general(xc_b[:, sl], dpa_b[:, sl], TN, preferred_element_type=F32)
            dwx_ref[hd] += lax.dot_general(xc_b[:, sl], dpx_b[:, sl], TN, preferred_element_type=F32)
        dxc = dxc + jnp.concatenate(back, axis=1)
        sm_ref[4:5, :] += jnp.sum(dxc, axis=0, keepdims=True)
        for k in range(4):
            sm_ref[k:k + 1, :] += jnp.sum(dxc * xtaps[k], axis=0, keepdims=True)
        dxc_ext = jnp.concatenate([dxc, cdxc[...]], axis=0)
        dp_ref[:, 0:D] = sum(cw_ref[k:k + 1, :] * _up(dxc_ext, 3 - k)[0:ts] for k in range(4)).astype(BF16)
        cdxc[...] = dxc[0:hl, :]

    def full(a):
        return pl.BlockSpec(a.shape, lambda i: (0,) * a.ndim)

    rev = lambda w: pl.BlockSpec((ts, w), lambda i: (n_t - 1 - i, 0))
    halo = lambda w: pl.BlockSpec((hb, w), lambda i: (jnp.maximum((n_t - 1 - i) * (ts // hb) - 1, 0), 0))
    return pl.pallas_call(
        body, name="l0_bwd_mix", grid=(n_t,),
        in_specs=[rev(D), rev(6 * D), halo(6 * D), rev(D), halo(D), rev(2 * D)]
        + [full(a) for a in (gate, cw, cb, wa, ba, wx, bx, lam, sw, wo)],
        out_specs=[rev(6 * D), pl.BlockSpec((N_CHIP, yb_w, D), lambda i: (0, 0, 0)),
                   pl.BlockSpec(wa.shape, lambda i: (0, 0, 0)), pl.BlockSpec(wa.shape, lambda i: (0, 0, 0)),
                   pl.BlockSpec((2 * SUBLANES, D), lambda i: (0, 0))],
        out_shape=[jax.ShapeDtypeStruct((s_len, 6 * D), BF16), jax.ShapeDtypeStruct((N_CHIP, yb_w, D), F32),
                   jax.ShapeDtypeStruct(wa.shape, F32), jax.ShapeDtypeStruct(wa.shape, F32),
                   jax.ShapeDtypeStruct((2 * SUBLANES, D), F32)],
        scratch_shapes=[pltpu.VMEM((hl, D), F32)] * 4,
        compiler_params=_cp(("arbitrary",)),
    )(dx1, proj, proj, hst, hst, y, gate, cw, cb, wa, ba, wx, bx, lam, sw, wo)


def _dgrad_norm(dproj, w, x, dres, g, sc, name, after=None):
    s_len, nb = x.shape[0], w.shape[2]
    ts = _tile(s_len, TS_DGRAD)
    order = [] if after is None else [after]

    def body(dp_ref, w_ref, x_ref, dr_ref, g_ref, sc_ref, *rest):
        dx_ref, s1_ref, s2_ref = rest[len(order):]

        @pl.when(pl.program_id(0) == 0)
        def _():
            s1_ref[...] = jnp.zeros_like(s1_ref)
            s2_ref[...] = jnp.zeros_like(s2_ref)

        dh = sum(lax.dot_general(dp_ref[:, k * nb:(k + 1) * nb], w_ref[k], NT, preferred_element_type=F32)
                 for k in range(N_CHIP))
        xv = x_ref[...]
        r = lax.rsqrt(jnp.mean(xv * xv, axis=-1, keepdims=True) + RMS_EPS)
        n = xv * r
        s1_ref[...] += jnp.sum(dh, axis=0, keepdims=True)
        s2_ref[...] += jnp.sum(dh * n, axis=0, keepdims=True)
        dn = dh * (g_ref[...] * (1.0 + sc_ref[...]))
        dx_ref[...] = dr_ref[...] + r * (dn - n * jnp.mean(dn * n, axis=-1, keepdims=True))

    row = lambda wd: pl.BlockSpec((ts, wd), lambda i: (i, 0))
    vec = pl.BlockSpec((1, D), lambda i: (0, 0))
    return pl.pallas_call(
        body, name=name, grid=(s_len // ts,),
        in_specs=[row(N_CHIP * nb), pl.BlockSpec(w.shape, lambda i: (0, 0, 0)), row(D), row(D), vec, vec]
        + [ANY] * len(order),
        out_specs=[row(D), vec, vec],
        out_shape=[jax.ShapeDtypeStruct((s_len, D), F32)] + [jax.ShapeDtypeStruct((1, D), F32)] * 2,
        compiler_params=_cp(("arbitrary",)),
    )(dproj, w, x, dres, g, sc, *order)


def _wgrad(a, b, groups, ka, nb, a_col, b_col, name, after=None):
    s_len = a.shape[0]
    ts = _tile(s_len, TS_WGRAD)
    order = [] if after is None else [after]

    def body(a_ref, b_ref, *rest):
        o_ref = rest[-1]

        @pl.when(pl.program_id(1) == 0)
        def _():
            o_ref[...] = jnp.zeros_like(o_ref)

        o_ref[...] += lax.dot_general(a_ref[...].astype(BF16), b_ref[...].astype(BF16), TN, preferred_element_type=F32)

    return pl.pallas_call(
        body, name=name, grid=(groups, s_len // ts),
        in_specs=[pl.BlockSpec((ts, ka), lambda g, s: (s, a_col(g))), pl.BlockSpec((ts, nb), lambda g, s: (s, b_col(g)))]
        + [ANY] * len(order),
        out_specs=pl.BlockSpec((None, ka, nb), lambda g, s: (g, 0, 0)),
        out_shape=jax.ShapeDtypeStruct((groups, ka, nb), F32),
        compiler_params=_cp(("parallel", "arbitrary")),
    )(a, b, *order)


def _wo_final(mt, wo, gate, name):
    rb = mt.shape[1]

    def body(m_ref, w_ref, gate_ref, dw_ref, dg_ref):
        @pl.when(pl.program_id(0) == 0)
        def _():
            dg_ref[...] = jnp.zeros_like(dg_ref)

        mv = m_ref[...]
        dw_ref[...] = mv * gate_ref[...]
        dg_ref[...] += jnp.sum(mv * w_ref[...].astype(F32), axis=0, keepdims=True)

    blk = pl.BlockSpec((None, rb, D), lambda k: (k, 0, 0))
    vec = pl.BlockSpec((1, D), lambda k: (0, 0))
    return pl.pallas_call(
        body, name=name, grid=(N_CHIP,), in_specs=[blk, blk, vec], out_specs=[blk, vec],
        out_shape=[jax.ShapeDtypeStruct(mt.shape, F32), jax.ShapeDtypeStruct((1, D), F32)],
        compiler_params=_cp(("arbitrary",)),
    )(mt, wo, gate)


ROW_NORM_G, ROW_CONV_W, ROW_CONV_B, ROW_B_A, ROW_B_X, ROW_LAMBDA, ROW_SC_W, ROW_POOL_B, ROW_POOL_S, ROW_FINAL_G = (
    0, 2, 6, 7, 8, 9, 10, 13, 15, 17)


def _small_pack(s1_0, s2_0, s1_1, s2_1, sm0, dsc1, dbg1, dgf, losscols, dgate0, dgate1, norm_g, sc0, sc1, lam):
    def body(s1_0r, s2_0r, s1_1r, s2_1r, sm, dsc, dbg, dgfr, lcols, dg0, dg1, ng, sc0r, sc1r, lamr, buf, dmod, loss):
        buf[...] = jnp.zeros_like(buf)
        buf[0:1, :] = s2_0r[...] * (1.0 + sc0r[...])
        buf[1:2, :] = s2_1r[...] * (1.0 + sc1r[...])
        buf[ROW_CONV_W:ROW_CONV_W + 4, :] = sm[0:4, :]
        buf[ROW_CONV_B:ROW_CONV_B + 1, :] = sm[4:5, :]
        buf[ROW_B_A:ROW_B_A + 1, :] = sm[5:6, :]
        buf[ROW_B_X:ROW_B_X + 1, :] = sm[6:7, :]
        buf[ROW_LAMBDA:ROW_LAMBDA + 1, :] = -sm[7:8, :] * _sigmoid(-lamr[...])
        buf[ROW_SC_W:ROW_SC_W + 3, :] = sm[8:11, :]
        for k in range(2):
            buf[ROW_POOL_B + k:ROW_POOL_B + k + 1, :] = dbg[:, k * D:(k + 1) * D]
            buf[ROW_POOL_S + k:ROW_POOL_S + k + 1, :] = dsc[:, k * D:(k + 1) * D]
        buf[ROW_FINAL_G:ROW_FINAL_G + 1, :] = dgfr[...]
        pieces = (s1_0r[...], s2_0r[...] * ng[0:1, :], dg0[...], s1_1r[...], s2_1r[...] * ng[1:2, :], dg1[...])
        for k, pc in enumerate(pieces):
            dmod[:, k * D:(k + 1) * D] = jnp.broadcast_to(pc, (SUBLANES, D))
        loss[...] = jnp.broadcast_to(jnp.sum(lcols[...], axis=1, keepdims=True) * (0.5 / D), loss.shape)

    args = (s1_0, s2_0, s1_1, s2_1, sm0, dsc1, dbg1, dgf, losscols, dgate0, dgate1, norm_g, sc0, sc1, lam)
    return pl.pallas_call(
        body, name="small_pack", in_specs=[VMEM] * len(args), out_specs=[VMEM] * 3,
        out_shape=[jax.ShapeDtypeStruct((SMALL_ROWS, D), F32), jax.ShapeDtypeStruct((SUBLANES, 6 * D), F32),
                   jax.ShapeDtypeStruct((SUBLANES, 128), F32)],
        compiler_params=_cp(),
    )(*args)


def _small_comm(buf_a, buf_b, dmod8):
    ra, rb = buf_a.shape[0] // N_DEV, buf_b.shape[0] // N_DEV
    wb = buf_b.shape[1]

    def body(a_ref, b_ref, dm_ref, oa_ref, ob_ref, odm_ref, ina, inb, dslot, sa, sb, s1, r1, s2, r2):
        x, y, c = _pos()
        me = 4 * x + 2 * y + c
        peers = []
        for r in range(1, N_DEV):
            fx, fy, fc = (r >> 2) & 1, (r >> 1) & 1, r & 1
            px, py, pc = _flip(x, fx), _flip(y, fy), _flip(c, fc)
            peers.append(((px, py, pc), 4 * px + 2 * py + pc))
        seg_a = lambda d: pl.ds(pl.multiple_of(d * ra, SUBLANES), ra)
        seg_b = lambda d: pl.ds(pl.multiple_of(d * rb, SUBLANES), rb)
        first = []
        for r, (peer, pid) in enumerate(peers):
            for k, (src, dst) in enumerate(((a_ref.at[seg_a(pid), :], ina.at[r]), (b_ref.at[seg_b(pid), :], inb.at[r]),
                                            (dm_ref, dslot.at[me]))):
                cp = pltpu.make_async_remote_copy(src_ref=src, dst_ref=dst, send_sem=s1.at[3 * r + k],
                                                  recv_sem=r1.at[3 * r + k], device_id=peer, device_id_type=MESH)
                cp.start()
                first.append(cp)
        dslot[me] = dm_ref[...]
        for cp in first:
            cp.wait()
        acc_a, acc_b = a_ref[seg_a(me), :], b_ref[seg_b(me), :]
        for r in range(N_DEV - 1):
            acc_a = acc_a + ina[r]
            acc_b = acc_b + inb[r]
        sa[...] = acc_a
        sb[...] = acc_b
        oa_ref[seg_a(me), :] = acc_a
        ob_ref[seg_b(me), :] = acc_b
        second = []
        for r, (peer, pid) in enumerate(peers):
            for k, (src, dst) in enumerate(((sa, oa_ref.at[seg_a(me), :]), (sb, ob_ref.at[seg_b(me), :]))):
                cp = pltpu.make_async_remote_copy(src_ref=src, dst_ref=dst, send_sem=s2.at[2 * r + k],
                                                  recv_sem=r2.at[2 * r + k], device_id=peer, device_id_type=MESH)
                cp.start()
                second.append(cp)
        rows = _rows(SUBLANES, dm_ref.shape[1])
        dm_all = jnp.zeros(dm_ref.shape, F32)
        for d in range(N_DEV):
            dm_all = jnp.where(rows == d, dslot[d], dm_all)
        odm_ref[...] = dm_all
        for cp in second:
            cp.wait()

    nrel = N_DEV - 1
    return pl.pallas_call(
        body, name="small_comm", in_specs=[VMEM] * 3, out_specs=[VMEM] * 3,
        out_shape=[jax.ShapeDtypeStruct(buf_a.shape, F32), jax.ShapeDtypeStruct(buf_b.shape, F32),
                   jax.ShapeDtypeStruct(dmod8.shape, F32)],
        scratch_shapes=[pltpu.VMEM((nrel, ra, D), F32), pltpu.VMEM((nrel, rb, wb), F32),
                        pltpu.VMEM((N_DEV,) + dmod8.shape, F32), pltpu.VMEM((ra, D), F32), pltpu.VMEM((rb, wb), F32),
                        pltpu.SemaphoreType.DMA((3 * nrel,)), pltpu.SemaphoreType.DMA((3 * nrel,)),
                        pltpu.SemaphoreType.DMA((2 * nrel,)), pltpu.SemaphoreType.DMA((2 * nrel,))],
        compiler_params=_cp(),
    )(buf_a, buf_b, dmod8)


def _adam(w, g, m, v):
    m2 = ADAM_B1 * m + (1.0 - ADAM_B1) * g
    v2 = ADAM_B2 * v + (1.0 - ADAM_B2) * (g * g)
    m_hat = m2 / (1.0 - ADAM_B1 ** ADAM_STEP)
    v_hat = v2 / (1.0 - ADAM_B2 ** ADAM_STEP)
    return -ADAM_LR * (m_hat / (jnp.sqrt(v_hat) + ADAM_EPS) + ADAM_WD * w), m2, v2


def _small_adam(red_a, red_b, dm_all, params):
    n = len(params)

    def body(*refs):
        ra, rb, dm = refs[:3]
        wmv = refs[3:3 + 3 * n]
        outs = refs[3 + 3 * n:]
        x, y, _ = _pos()
        chip = 2 * x + y

        def shard(row0, nrows, width):
            per_row = D // width
            cands = []
            for k in range(N_CHIP):
                if nrows == 1 or per_row >= N_CHIP:
                    cands.append(ra[row0:row0 + nrows, k * width:(k + 1) * width])
                else:
                    rr, cc = divmod(k * width, D)
                    cands.append(ra[row0 + rr:row0 + rr + 1, cc:cc + width])
            g = cands[0]
            for k in range(1, N_CHIP):
                g = jnp.where(chip == k, cands[k], g)
            return g

        dms = jnp.sum(dm[...], axis=0, keepdims=True)
        hw = LRU_HEADS * LRU_HEAD_DIM
        grads = [
            ra[ROW_NORM_G:ROW_NORM_G + 2, :],
            None,
            shard(ROW_CONV_W, 4, D // N_CHIP),
            ra[ROW_CONV_B:ROW_CONV_B + 1, :],
            rb[0:hw, :],
            ra[ROW_B_A:ROW_B_A + 1, :],
            rb[hw:2 * hw, :],
            ra[ROW_B_X:ROW_B_X + 1, :],
            ra[ROW_LAMBDA:ROW_LAMBDA + 1, :],
            shard(ROW_SC_W, 3, D // N_CHIP),
            shard(ROW_POOL_B, 2, 2 * D // N_CHIP),
            shard(ROW_POOL_S, 2, 2 * D // N_CHIP),
            ra[ROW_FINAL_G:ROW_FINAL_G + 1, :],
        ]
        for p in range(n):
            w_ref, m_ref, v_ref = wmv[3 * p:3 * p + 3]
            g_out, d_out, m_out, v_out = outs[4 * p:4 * p + 4]
            if grads[p] is None:
                for l in range(2):
                    g = dms[:, l * 3 * D:(l + 1) * 3 * D]
                    dl, m2, v2 = _adam(w_ref[l:l + 1, :], g, m_ref[l:l + 1, :], v_ref[l:l + 1, :])
                    g_out[l:l + 1, :] = g
                    d_out[l:l + 1, :] = dl
                    m_out[l:l + 1, :] = m2
                    v_out[l:l + 1, :] = v2
            else:
                g = grads[p]
                dl, m2, v2 = _adam(w_ref[...], g, m_ref[...], v_ref[...])
                g_out[...] = g
                d_out[...] = dl
                m_out[...] = m2
                v_out[...] = v2

    flat = [a for p in params for a in p]
    return pl.pallas_call(
        body, name="small_adam", in_specs=[VMEM] * (3 + len(flat)), out_specs=[VMEM] * (4 * n),
        out_shape=[jax.ShapeDtypeStruct(p[0].shape, F32) for p in params for _ in range(4)],
        compiler_params=_cp(),
    )(red_a, red_b, dm_all, *flat)


def _modw_adam(ca_t, dm_sh, w, m, v):
    nw = w.shape[2]

    def body(c_ref, d_ref, w_ref, m_ref, v_ref, g_out, d_out, m_out, v_out):
        g = jnp.dot(c_ref[...], d_ref[...], precision=lax.Precision.HIGHEST, preferred_element_type=F32)
        dl, m2, v2 = _adam(w_ref[...], g, m_ref[...], v_ref[...])
        g_out[...] = g
        d_out[...] = dl
        m_out[...] = m2
        v_out[...] = v2

    blk = pl.BlockSpec((None, D, nw), lambda l: (l, 0, 0))
    return pl.pallas_call(
        body, name="modw_adam", grid=(2,),
        in_specs=[pl.BlockSpec((D, SUBLANES), lambda l: (0, 0)), pl.BlockSpec((None, SUBLANES, nw), lambda l: (l, 0, 0)),
                  blk, blk, blk],
        out_specs=[blk] * 4, out_shape=[jax.ShapeDtypeStruct(w.shape, F32)] * 4,
        compiler_params=_cp(("arbitrary",)),
    )(ca_t, dm_sh, w, m, v)


def _exchange(copies, name, out_type, n_sems, args, sequencer, after=None):
    order = [] if after is None else [after]
    n_in, n_out = len(args) + len(order), len(out_type)

    def body(*refs):
        barrier = pltpu.get_barrier_semaphore()
        peers = sequencer[1](*_pos())
        for peer in peers:
            pl.semaphore_signal(barrier, inc=1, device_id=peer, device_id_type=MESH)
        pl.semaphore_wait(barrier, len(peers))
        copies(refs[:n_in], refs[n_in:n_in + n_out], refs[n_in + n_out], refs[n_in + n_out + 1])

    sems = [pltpu.SemaphoreType.DMA((n_sems,))] * 2
    return pl.kernel(body, out_type, mesh=plsc.ScalarSubcoreMesh(axis_name="sequencer", num_cores=1), name=name,
                     scratch_types=sems, compiler_params=pltpu.CompilerParams(collective_id=sequencer[0]))(*args, *order)


def _sibling(x, y, c):
    return [(x, y, 1 - c)]


def _other_chips(x, y, c):
    return [(1 - x, y, c), (x, 1 - y, c), (1 - x, 1 - y, c)]


def _to_wire(g, name, after=None):
    _, rr, cc = g.shape
    rb = min(rr, 256)

    def body(g_ref, *rest):
        rest[-1][...] = g_ref[...].astype(GRAD_WIRE_DTYPE)

    order = [] if after is None else [after]
    blk = pl.BlockSpec((None, rb, cc), lambda k, j: (k, j, 0))
    return pl.pallas_call(
        body, name=name, grid=(N_CHIP, rr // rb), in_specs=[blk] + [ANY] * len(order), out_specs=blk,
        out_shape=jax.ShapeDtypeStruct(g.shape, GRAD_WIRE_DTYPE), compiler_params=_cp(("parallel", "parallel")),
    )(g, *order)


def _chip_scatter(ps, name, collective_id, after=None):
    n = len(ps)

    def copies(ins, outs, ssem, rsem):
        x, y, c = _pos()
        cps = []
        for a in range(n):
            for q, (fx, fy) in enumerate(((1, 0), (0, 1), (1, 1))):
                px, py = _flip(x, fx), _flip(y, fy)
                cp = pltpu.make_async_remote_copy(
                    src_ref=ins[a].at[2 * px + py], dst_ref=outs[a].at[q],
                    send_sem=ssem.at[3 * a + q], recv_sem=rsem.at[3 * a + q], device_id=(px, py, c), device_id_type=MESH)
                cp.start()
                cps.append(cp)
        for cp in cps:
            cp.wait()

    out_type = [jax.ShapeDtypeStruct((N_CHIP - 1,) + p.shape[1:], p.dtype) for p in ps]
    return _exchange(copies, name, out_type, 3 * n, ps, (collective_id, _other_chips), after)


def _add_owner(p, got, chipidx, name, after=None):
    _, hr, cc = p.shape
    rb = min(hr, 256)

    def body(k_ref, p_ref, r_ref, *rest):
        rest[-1][...] = ((p_ref[...].astype(F32) + r_ref[0].astype(F32)) + r_ref[1].astype(F32)) + r_ref[2].astype(F32)

    order = [] if after is None else [after]
    return pl.pallas_call(
        body, name=name,
        grid_spec=pltpu.PrefetchScalarGridSpec(
            num_scalar_prefetch=1, grid=(hr // rb,),
            in_specs=[pl.BlockSpec((None, rb, cc), lambda j, k_ref: (k_ref[0], j, 0)),
                      pl.BlockSpec((N_CHIP - 1, rb, cc), lambda j, k_ref: (0, j, 0))] + [ANY] * len(order),
            out_specs=pl.BlockSpec((rb, cc), lambda j, k_ref: (j, 0))),
        out_shape=jax.ShapeDtypeStruct((hr, cc), F32),
        compiler_params=_cp(("parallel",)),
    )(chipidx, p, got, *order)


def _sib_exchange(ts_, name, collective_id, after=None):
    n = len(ts_)

    def copies(ins, outs, ssem, rsem):
        x, y, c = _pos()
        cps = []
        for a in range(n):
            cp = pltpu.make_async_remote_copy(src_ref=ins[a], dst_ref=outs[a], send_sem=ssem.at[a],
                                              recv_sem=rsem.at[a], device_id=(x, y, 1 - c), device_id_type=MESH)
            cp.start()
            cps.append(cp)
        for cp in cps:
            cp.wait()

    out_type = [jax.ShapeDtypeStruct(t.shape, F32) for t in ts_]
    return _exchange(copies, name, out_type, n, ts_, (collective_id, _sibling), after)


def _adam_2d(w, g_own, g_sib, m, v, name):
    rr, cc = w.shape
    rb = min(rr, 256)

    def body(w_ref, go_ref, gs_ref, m_ref, v_ref, g_out, d_out, m_out, v_out):
        g = go_ref[...] + gs_ref[...]
        dl, m2, v2 = _adam(w_ref[...], g, m_ref[...], v_ref[...])
        g_out[...] = g
        d_out[...] = dl
        m_out[...] = m2
        v_out[...] = v2

    blk = pl.BlockSpec((rb, cc), lambda j: (j, 0))
    return pl.pallas_call(
        body, name=name, grid=(rr // rb,), in_specs=[blk] * 5, out_specs=[blk] * 4,
        out_shape=[jax.ShapeDtypeStruct((rr, cc), F32)] * 4, compiler_params=_cp(("parallel",)),
    )(w, g_own, g_sib, m, v)


def kernel(x, c, norm_g, mod_w, mod_b, hy_w_in, hy_conv_w, hy_conv_b, lru_w_a, lru_b_a, lru_w_x, lru_b_x, lru_lambda, sc_conv_w, hy_w_out, pool_w_in, pool_w_grp, pool_b_grp, pool_scale, pool_w_out, final_g, loss_target, m_norm_g, m_mod_w, m_mod_b, m_hy_w_in, m_hy_conv_w, m_hy_conv_b, m_lru_w_a, m_lru_b_a, m_lru_w_x, m_lru_b_x, m_lru_lambda, m_sc_conv_w, m_hy_w_out, m_pool_w_in, m_pool_w_grp, m_pool_b_grp, m_pool_scale, m_pool_w_out, m_final_g, v_norm_g, v_mod_w, v_mod_b, v_hy_w_in, v_hy_conv_w, v_hy_conv_b, v_lru_w_a, v_lru_b_a, v_lru_w_x, v_lru_b_x, v_lru_lambda, v_sc_conv_w, v_hy_w_out, v_pool_w_in, v_pool_w_grp, v_pool_b_grp, v_pool_scale, v_pool_w_out, v_final_g):
    ax, ay, ac = _pos()
    me = 4 * ax + 2 * ay + ac
    chip = 2 * ax + ay
    xs = x[0]
    tgt = loss_target[0]
    gd = POOL_GROUP_DIM
    kidx = chip.reshape(1).astype(jnp.int32)

    big = [hy_w_in[0], hy_w_out[0], pool_w_in[0], pool_w_grp[0].reshape(4 * 128, gd), pool_w_out[0]]
    w_in0, w_out0 = _wgather_sequencer(
        [_wcast_own_block(w, kidx, f"wcast_own_block_{a}") for a, w in enumerate(big[:2])], "wgather_l0", CIDS_WGATHER[0])

    ca_all, mod_all, small_w = _mod_fwd(jnp.broadcast_to(c, (SUBLANES, D)), mod_w, mod_b,
                                        hy_conv_w[0], sc_conv_w[0], pool_b_grp, pool_scale)
    mod_me = lax.dynamic_index_in_dim(mod_all, me, axis=1, keepdims=False)
    sh0, sc0, gt0 = (mod_me[0:1, k * D:(k + 1) * D] for k in range(3))
    sh1, sc1, gt1 = (mod_me[1:2, k * D:(k + 1) * D] for k in range(3))
    cw = small_w[SW_CONV:SW_CONV + 4, 0:D]
    sw = small_w[SW_SC:SW_SC + 3, 0:D]
    pool_b = small_w[SW_POOL_B:SW_POOL_B + 1, :]
    pool_s = small_w[SW_POOL_S:SW_POOL_S + 1, :]
    g0, g1, gf = norm_g[0:1], norm_g[1:2], final_g.reshape(1, D)
    cb, ba, bx, lam = hy_conv_b, lru_b_a, lru_b_x, lru_lambda

    w_in1, w_grp, w_out1 = _wgather_sequencer(
        [_wcast_own_block(w, kidx, f"wcast_own_block_{a + 2}", after=w_out0) for a, w in enumerate(big[2:])],
        "wgather_l1", CIDS_WGATHER[1])
    w_grp =w_grp.reshape(N_CHIP, 4, 128, gd).transpose(1, 0, 2, 3).reshape(4, gd, gd)
    wa_b, wx_b = _wcast([lru_w_a[0], lru_w_x[0]])

    x1, hst, y0, h0, proj0 = _l0_fwd(xs, g0, sc0, sh0, w_in0, gt0, cw, cb, wa_b, ba, wx_b, bx, lam, sw,
                                     w_out0.reshape(2 * D, D))
    dpool, mixed, y1, dx2, losscols, dgf, h1, proj1 = _l1_fwd(x1, g1, sc1, sh1, w_in1, tgt, gt1, w_grp, pool_b, pool_s,
                                                              w_out1.reshape(2 * D, D), gf)

    def to_wire(grads, tag, after):
        wire = []
        for a, g in enumerate(grads):
            wire.append(_to_wire(g, f"grad_to_wire_{tag}{a}", wire[-1] if wire else after))
        return wire

    def add_owners(grads, got, tag, ids, after):
        own = []
        for a, (g, r) in enumerate(zip(grads, got)):
            own.append(_add_owner(g, r, kidx, f"grad_add_owner_{tag}{a}", own[-1] if own else after))
        return own, _sib_exchange(own, f"grad_sib_exchange_{tag}", ids[1])

    dproj1, mt1, d_wgrp, dsc1, dbg1 = _l1_bwd_mix(dx2, proj1, mixed, y1, dpool, gt1, w_grp, pool_s,
                                                  w_out1.reshape(2 * D, D))
    d_win1 = _wgrad(h1, dproj1, N_CHIP, D, D, lambda g: 0, lambda g: g, "l1_wgrad_in")
    dx1, s1_1, s2_1 = _dgrad_norm(dproj1, w_in1, x1, dx2, g1, sc1, "l1_bwd_proj")
    d_wout1, dgate1 = _wo_final(mt1, w_out1, gt1, "l1_wo_final")
    d_wgrp = d_wgrp.reshape(4, N_CHIP, 128, gd).transpose(1, 0, 2, 3).reshape(N_CHIP, 4 * 128, gd)
    grads_l1 = [d_win1, d_wgrp, d_wout1]
    got_l1 = _chip_scatter(to_wire(grads_l1, "l1", None), "grad_chip_scatter_l1", CIDS_L1[0])

    dproj0, mt0, d_wa, d_wx, sm0 = _l0_bwd_mix(dx1, proj0, hst, y0, gt0, cw, cb, wa_b, ba, wx_b, bx, lam, sw,
                                               w_out0.reshape(2 * D, D))
    sums_l1, sib_l1 = add_owners(grads_l1, got_l1, "l1", CIDS_L1, after=sm0)
    d_win0 = _wgrad(h0, dproj0, N_CHIP, D, 6 * D // N_CHIP, lambda g: 0, lambda g: g, "l0_wgrad_in", after=sums_l1[-1])
    d_wout0, dgate0 = _wo_final(mt0, w_out0, gt0, "l0_wo_final")
    grads_l0 = [d_win0, d_wout0]
    wire_l0 = to_wire(grads_l0, "l0", None)
    got_l0 = _chip_scatter(wire_l0, "grad_chip_scatter_l0", CIDS_L0[0], after=sib_l1[0])
    grad_x, s1_0, s2_0 = _dgrad_norm(dproj0, w_in0, xs, dx1, g0, sc0, "l0_bwd_proj", after=wire_l0[-1])
    sums_l0, sib_l0 = add_owners(grads_l0, got_l0, "l0", CIDS_L0, after=s1_0)

    buf_a, dmod8, loss8 = _small_pack(s1_0, s2_0, s1_1, s2_1, sm0, dsc1, dbg1, dgf, losscols, dgate0, dgate1,
                                      norm_g, sc0, sc1, lam)
    hw = LRU_HEADS * LRU_HEAD_DIM
    buf_b = jnp.concatenate([d_wa.reshape(hw, LRU_HEAD_DIM), d_wx.reshape(hw, LRU_HEAD_DIM)], axis=0)
    red_a, red_b, dm_all = _small_comm(buf_a, buf_b, dmod8)
    small = [(norm_g, m_norm_g, v_norm_g), (mod_b, m_mod_b, v_mod_b),
             (hy_conv_w[0], m_hy_conv_w[0], v_hy_conv_w[0]), (hy_conv_b, m_hy_conv_b, v_hy_conv_b),
             tuple(a.reshape(hw, LRU_HEAD_DIM) for a in (lru_w_a, m_lru_w_a, v_lru_w_a)),
             (lru_b_a, m_lru_b_a, v_lru_b_a),
             tuple(a.reshape(hw, LRU_HEAD_DIM) for a in (lru_w_x, m_lru_w_x, v_lru_w_x)),
             (lru_b_x, m_lru_b_x, v_lru_b_x), (lru_lambda, m_lru_lambda, v_lru_lambda),
             (sc_conv_w[0], m_sc_conv_w[0], v_sc_conv_w[0]), (pool_b_grp, m_pool_b_grp, v_pool_b_grp),
             (pool_scale, m_pool_scale, v_pool_scale),
             tuple(a.reshape(1, D) for a in (final_g, m_final_g, v_final_g))]
    small_names = ["norm_g", "mod_b", "hy_conv_w", "hy_conv_b", "lru_w_a", "lru_b_a", "lru_w_x", "lru_b_x",
                   "lru_lambda", "sc_conv_w", "pool_b_grp", "pool_scale", "final_g"]
    small_out = _small_adam(red_a, red_b, dm_all, small)
    res = {}
    shapes = dict(norm_g=norm_g, mod_b=mod_b, hy_conv_w=hy_conv_w, hy_conv_b=hy_conv_b, lru_w_a=lru_w_a, lru_b_a=lru_b_a,
                  lru_w_x=lru_w_x, lru_b_x=lru_b_x, lru_lambda=lru_lambda, sc_conv_w=sc_conv_w, pool_b_grp=pool_b_grp,
                  pool_scale=pool_scale, final_g=final_g)
    for p, nm in enumerate(small_names):
        res[nm] = tuple(o.reshape(shapes[nm].shape) for o in small_out[4 * p:4 * p + 4])

    nw = mod_w.shape[2]
    dm_sh = jnp.stack([lax.dynamic_slice_in_dim(dm_all[:, l * 3 * D:(l + 1) * 3 * D], chip * nw, nw, axis=1)
                       for l in range(2)])
    res["mod_w"] = tuple(_modw_adam(ca_all.T, dm_sh, mod_w, m_mod_w, v_mod_w))

    sums = list(sums_l0) + list(sums_l1)
    sib_sums = list(sib_l0) + list(sib_l1)
    big_names = ["hy_w_in", "hy_w_out", "pool_w_in", "pool_w_grp", "pool_w_out"]
    big_wmv = [(hy_w_in, m_hy_w_in, v_hy_w_in), (hy_w_out, m_hy_w_out, v_hy_w_out), (pool_w_in, m_pool_w_in, v_pool_w_in),
               (pool_w_grp, m_pool_w_grp, v_pool_w_grp), (pool_w_out, m_pool_w_out, v_pool_w_out)]
    for a, nm in enumerate(big_names):
        rr, cc = big[a].shape
        w, m, v = (t.reshape(rr, cc) for t in big_wmv[a])
        outs = _adam_2d(w, sums[a], sib_sums[a], m, v, f"adam_{nm}")
        res[nm] = tuple(o.reshape(big_wmv[a][0].shape) for o in outs)

    loss = lax.psum(loss8[0, 0], ("x", "y", "c"))
    order = ["norm_g", "mod_w", "mod_b", "hy_w_in", "hy_conv_w", "hy_conv_b", "lru_w_a", "lru_b_a", "lru_w_x", "lru_b_x",
             "lru_lambda", "sc_conv_w", "hy_w_out", "pool_w_in", "pool_w_grp", "pool_b_grp", "pool_scale", "pool_w_out",
             "final_g"]
    return (loss, grad_x[None], *[res[nm][0] for nm in order], *[res[nm][1] for nm in order],
            *[res[nm][2] for nm in order], *[res[nm][3] for nm in order])
```

```python
import jax
import jax.numpy as jnp
from jax import lax
from jax.experimental import pallas as pl
from jax.experimental.pallas import tpu as pltpu
from jax.experimental.pallas import tpu_sc as plsc

F32, BF16 = jnp.float32, jnp.bfloat16
D = 1024
RMS_EPS = 1e-6
SQRT_FLOOR = 1e-30
LRU_C = 8.0
LRU_HEADS, LRU_HEAD_DIM = 8, 128
POOL_WINDOWS = (2, 4, 8, 16)
POOL_GROUP_DIM = 512
ADAM_LR, ADAM_B1, ADAM_B2, ADAM_EPS, ADAM_WD, ADAM_STEP = 0.001, 0.9, 0.999, 1e-08, 0.01, 10
MESH = pl.DeviceIdType.MESH
CIDS_WGATHER = (1, 8)
CIDS_L1 = (2, 3)
CIDS_L0 = (4, 5)
N_DEV, N_CHIP = 8, 4
SUBLANES = 8
BF16_ROWS = 16
POOL_HALO = 16
TS_MIX, TS_WGRAD, TS_DGRAD = 256, 1024, 256
SMALL_ROWS = 64
GRAD_WIRE_DTYPE = BF16
ANY = pl.BlockSpec(memory_space=pl.ANY)
VMEM = pl.BlockSpec(memory_space=pltpu.VMEM)
NT = (((1,), (1,)), ((), ()))
TN = (((0,), (0,)), ((), ()))


def _cp(sem=None, vmem_mb=56):
    kw = dict(vmem_limit_bytes=vmem_mb * 2 ** 20)
    if sem is not None:
        kw["dimension_semantics"] = sem
    return pltpu.CompilerParams(**kw)


def _tile(n, t):
    return min(n, t)


def _pos():
    return lax.axis_index("x"), lax.axis_index("y"), lax.axis_index("c")


def _flip(v, f):
    return 1 - v if f else v


def _sigmoid(z):
    return 0.5 * jnp.tanh(0.5 * z) + 0.5


def _rows(n, c):
    return lax.broadcasted_iota(jnp.int32, (n, c), 0)


def _down(a, d):
    return a if d == 0 else pltpu.roll(a, d, 0)


def _up(a, d):
    return a if d == 0 else pltpu.roll(a, a.shape[0] - d, 0)


def _scan_fwd_steps(a, u, carry):
    n, c = a.shape
    sub = _rows(SUBLANES, c)
    out = []
    for k in range(n // SUBLANES):
        p = a[k * SUBLANES:(k + 1) * SUBLANES]
        g = u[k * SUBLANES:(k + 1) * SUBLANES]
        for d in (1, 2, 4):
            keep = sub >= d
            g = g + p * jnp.where(keep, pltpu.roll(g, d, 0), 0.0)
            p = p * jnp.where(keep, pltpu.roll(p, d, 0), 1.0)
        h = g + p * carry
        carry = h[SUBLANES - 1:SUBLANES, :]
        out.append(h)
        yield
    return jnp.concatenate(out, axis=0)


def _scan_rev_steps(alpha, b, carry):
    n, c = alpha.shape
    sub = _rows(SUBLANES, c)
    out = []
    for k in reversed(range(n // SUBLANES)):
        p = alpha[k * SUBLANES:(k + 1) * SUBLANES]
        g = b[k * SUBLANES:(k + 1) * SUBLANES]
        for d in (1, 2, 4):
            keep = sub < SUBLANES - d
            g = g + p * jnp.where(keep, pltpu.roll(g, SUBLANES - d, 0), 0.0)
            p = p * jnp.where(keep, pltpu.roll(p, SUBLANES - d, 0), 1.0)
        h = g + p * carry
        carry = h[0:1, :]
        out.append(h)
        yield
    return jnp.concatenate(out[::-1], axis=0)


def _run(steps):
    while True:
        try:
            next(steps)
        except StopIteration as done:
            return done.value


def _paired(progress, pieces):
    n, done = len(pieces), 1
    pieces[0]()
    for frac in progress:
        while done < n and done <= frac * n:
            pieces[done]()
            done += 1
    while done < n:
        pieces[done]()
        done += 1


def _conv_taps(ext, halo, n, width):
    return [_down(ext, width - 1 - k)[halo:halo + n] for k in range(width)]


def _lru_gates(xc, wa_ref, ba, wx_ref, bx):
    xb = xc.astype(BF16)
    pa, px = [], []
    for h in range(LRU_HEADS):
        xh = xb[:, h * LRU_HEAD_DIM:(h + 1) * LRU_HEAD_DIM]
        pa.append(jnp.dot(xh, wa_ref[h], preferred_element_type=F32))
        px.append(jnp.dot(xh, wx_ref[h], preferred_element_type=F32))
    r = _sigmoid(jnp.concatenate(pa, axis=1) + ba)
    ig = _sigmoid(jnp.concatenate(px, axis=1) + bx)
    return r, ig


def _softplus_neg(lam):
    return jnp.maximum(-lam, 0.0) + jnp.log1p(jnp.exp(-jnp.abs(lam)))


def _recip_1_to_2(d):
    r0 = pl.reciprocal(d, approx=True)
    return r0 * (2.0 - d * r0)


def _lru_decay(r, sp, first):
    big_l = (-LRU_C) * r * sp
    a = jnp.exp(big_l)
    th = jnp.tanh(big_l)
    q = (-2.0 * th) * _recip_1_to_2(1.0 - th)
    rs = lax.rsqrt(jnp.maximum(q, SQRT_FLOOR))
    return a, jnp.where(first, 1.0, q * rs), rs


def _pool_inv_counts(t0, n):
    t = (t0 + lax.broadcasted_iota(jnp.int32, (n, 1), 0) + 1).astype(F32)
    return [1.0 / jnp.minimum(t, float(w)) for w in POOL_WINDOWS]


def _window_sums(ext, shift):
    gd = POOL_GROUP_DIM
    out = []
    s = ext
    for k in range(len(POOL_WINDOWS)):
        s = s + shift(s, 2 ** k)
        out.append(s[:, 0:gd])
        if k + 1 < len(POOL_WINDOWS):
            s = s[:, gd:]
    return out


SW_ROWS, SW_COLS = 16, 2 * D
SW_CONV, SW_SC, SW_POOL_B, SW_POOL_S = 0, 4, 8, 9


def _mod_fwd(c8, mod_w, mod_b, conv_w, sc_w, pool_b, pool_s):
    nw = mod_w.shape[2]
    cq, pq = conv_w.shape[1], pool_b.shape[1]

    def body(c_ref, w_ref, b_ref, cw_ref, sw_ref, pb_ref, ps_ref, ca_ref, mod_ref, small_ref,
             cslot, mslot, msend, pslot, psend, s1, r1, s2, r2, s3, r3):
        x, y, c = _pos()
        me = 4 * x + 2 * y + c
        chip = 2 * x + y
        first = []
        for r in range(1, N_DEV):
            fx, fy, fc = (r >> 2) & 1, (r >> 1) & 1, r & 1
            cp = pltpu.make_async_remote_copy(
                src_ref=c_ref, dst_ref=cslot.at[me], send_sem=s1.at[r - 1], recv_sem=r1.at[r - 1],
                device_id=(_flip(x, fx), _flip(y, fy), _flip(c, fc)), device_id_type=MESH)
            cp.start()
            first.append(cp)
        cslot[me] = c_ref[...]
        for cp in first:
            cp.wait()
        rows = _rows(SUBLANES, D)
        call = jnp.zeros((SUBLANES, D), F32)
        for d in range(N_DEV):
            call = jnp.where(rows == d, cslot[d], call)
        ca = call * _sigmoid(call)
        ca_ref[...] = ca
        for l in range(2):
            msend[l] = jnp.dot(ca, w_ref[l], precision=lax.Precision.HIGHEST, preferred_element_type=F32)
        psend[...] = jnp.zeros_like(psend)
        psend[SW_CONV:SW_CONV + 4, 0:cq] = cw_ref[...]
        psend[SW_SC:SW_SC + 3, 0:cq] = sw_ref[...]
        psend[SW_POOL_B:SW_POOL_B + 1, :] = pb_ref[...]
        psend[SW_POOL_S:SW_POOL_S + 1, :] = ps_ref[...]
        second = []
        for q, (fx, fy) in enumerate(((1, 0), (0, 1), (1, 1))):
            peer = (_flip(x, fx), _flip(y, fy), c)
            for src, dst, ss, rs in ((msend, mslot, s2, r2), (psend, pslot, s3, r3)):
                cp = pltpu.make_async_remote_copy(src_ref=src, dst_ref=dst.at[chip], send_sem=ss.at[q], recv_sem=rs.at[q],
                                                  device_id=peer, device_id_type=MESH)
                cp.start()
                second.append(cp)
        mslot[chip] = msend[...]
        pslot[chip] = psend[...]
        for cp in second:
            cp.wait()
        small_ref[...] = jnp.zeros_like(small_ref)
        for j in range(N_CHIP):
            for l in range(2):
                mod_ref[l, :, j * nw:(j + 1) * nw] = mslot[j, l] + b_ref[l:l + 1, j * nw:(j + 1) * nw]
            small_ref[0:SUBLANES, j * cq:(j + 1) * cq] = pslot[j, 0:SUBLANES, 0:cq]
            small_ref[SUBLANES:SW_ROWS, j * pq:(j + 1) * pq] = pslot[j, SUBLANES:SW_ROWS, :]

    args = (c8, mod_w, mod_b, conv_w, sc_w, pool_b, pool_s)
    dma3 = pltpu.SemaphoreType.DMA((N_CHIP - 1,))
    return pl.pallas_call(
        body, name="mod_fwd",
        in_specs=[VMEM] * len(args), out_specs=[VMEM] * 3,
        out_shape=[jax.ShapeDtypeStruct((SUBLANES, D), F32), jax.ShapeDtypeStruct((2, SUBLANES, N_CHIP * nw), F32),
                   jax.ShapeDtypeStruct((SW_ROWS, SW_COLS), F32)],
        scratch_shapes=[pltpu.VMEM((N_DEV, SUBLANES, D), F32), pltpu.VMEM((N_CHIP, 2, SUBLANES, nw), F32),
                        pltpu.VMEM((2, SUBLANES, nw), F32), pltpu.VMEM((N_CHIP, SW_ROWS, pq), F32),
                        pltpu.VMEM((SW_ROWS, pq), F32),
                        pltpu.SemaphoreType.DMA((N_DEV - 1,)), pltpu.SemaphoreType.DMA((N_DEV - 1,)),
                        dma3, dma3, dma3, dma3],
        compiler_params=_cp(),
    )(*args)


def _wcast(ws):
    def body(*refs):
        n = len(refs) // 2
        for a in range(n):
            refs[n + a][...] = refs[a][...].astype(BF16)

    return pl.pallas_call(
        body, name="wcast", in_specs=[VMEM] * len(ws), out_specs=[VMEM] * len(ws),
        out_shape=[jax.ShapeDtypeStruct(w.shape, BF16) for w in ws], compiler_params=_cp(),
    )(*ws)


def _wcast_own_block(w, kidx, name, after=None):
    rr, cc = w.shape
    rb = min(rr, 256)

    def body(k_ref, w_ref, *rest):
        rest[-1][...] = w_ref[...].astype(BF16)

    order = [] if after is None else [after]
    return pl.pallas_call(
        body, name=name,
        grid_spec=pltpu.PrefetchScalarGridSpec(
            num_scalar_prefetch=1, grid=(rr // rb,),
            in_specs=[pl.BlockSpec((rb, cc), lambda j, k_ref: (j, 0))] + [ANY] * len(order),
            out_specs=pl.BlockSpec((None, rb, cc), lambda j, k_ref: (k_ref[0], j, 0))),
        out_shape=jax.ShapeDtypeStruct((N_CHIP, rr, cc), BF16),
        compiler_params=_cp(("parallel",)),
    )(kidx, w, *order)


def _wgather_copies(outs, rows, ssem, rsem, fssem, frsem):
    n = len(outs)
    x, y, c = _pos()
    chip = 2 * x + y
    sib = (x, y, 1 - c)
    flips = ((1, 0), (0, 1), (1, 1))

    def half(a, which):
        hr = rows[a] // 2
        return pl.ds(pl.multiple_of(which * hr, BF16_ROWS), hr)

    sends = []
    for a in range(n):
        mine = outs[a].at[chip, half(a, c), :]
        for q, (fx, fy) in enumerate(flips):
            cp = pltpu.make_async_remote_copy(
                src_ref=mine, dst_ref=mine, send_sem=ssem.at[3 * a + q], recv_sem=rsem.at[3 * a + q],
                device_id=(_flip(x, fx), _flip(y, fy), c), device_id_type=MESH)
            cp.start()
            sends.append(cp)
    passed = []
    for a in range(n):
        for q, (fx, fy) in enumerate(flips):
            src_chip = 2 * _flip(x, fx) + _flip(y, fy)
            landed = outs[a].at[src_chip, half(a, c), :]
            pltpu.make_async_remote_copy(
                src_ref=landed, dst_ref=landed, send_sem=ssem.at[3 * a + q], recv_sem=rsem.at[3 * a + q],
                device_id=sib, device_id_type=MESH).wait_recv()
            cp = pltpu.make_async_remote_copy(
                src_ref=landed, dst_ref=landed, send_sem=fssem.at[3 * a + q], recv_sem=frsem.at[3 * a + q],
                device_id=sib, device_id_type=MESH)
            cp.start()
            passed.append(cp)
    for a in range(n):
        for q, (fx, fy) in enumerate(flips):
            src_chip = 2 * _flip(x, fx) + _flip(y, fy)
            other = outs[a].at[src_chip, half(a, 1 - c), :]
            pltpu.make_async_remote_copy(
                src_ref=other, dst_ref=other, send_sem=fssem.at[3 * a + q], recv_sem=frsem.at[3 * a + q],
                device_id=sib, device_id_type=MESH).wait_recv()
    for cp in sends + passed:
        cp.wait_send()


def _wgather_sequencer(bufs, name, collective_id):
    n = len(bufs)
    refs = [jax.new_ref(b, memory_space=pltpu.MemorySpace.HBM) for b in bufs]
    dma = pltpu.SemaphoreType.DMA((3 * n,))

    @pl.kernel(mesh=plsc.ScalarSubcoreMesh(axis_name="sequencer", num_cores=1), name=name,
               scratch_types=(dma, dma, dma, dma), compiler_params=pltpu.CompilerParams(collective_id=collective_id))
    def launch(ssem, rsem, fssem, frsem):
        x, y, c = _pos()
        barrier = pltpu.get_barrier_semaphore()
        for peer in ((1 - x, y, c), (x, 1 - y, c), (1 - x, 1 - y, c), (x, y, 1 - c)):
            pl.semaphore_signal(barrier, inc=1, device_id=peer, device_id_type=MESH)
        pl.semaphore_wait(barrier, 4)
        _wgather_copies(refs, [b.shape[1] for b in bufs], ssem, rsem, fssem, frsem)

    launch()
    return [r[...] for r in refs]


def _l0_fwd(x, g, sc, sh, w_in, gate, cw, cb, wa, ba, wx, bx, lam, sw, wo):
    s_len, nb = x.shape[0], w_in.shape[2]
    ts = _tile(s_len, TS_MIX)
    n_t = s_len // ts
    hl = SUBLANES

    def body(xa_ref, xb_ref, g_ref, sc_ref, sh_ref, win_ref, gate_ref, cw_ref, cb_ref, wa_ref, ba_ref, wx_ref, bx_ref,
             lam_ref, sw_ref, wo_ref, x1_ref, h_ref, y_ref, h0_ref, p_ref, pcur, pnext, cxa, czz, chh):
        i = pl.program_id(0)

        @pl.when(i == 0)
        def _():
            cxa[...] = jnp.zeros_like(cxa)
            czz[...] = jnp.zeros_like(czz)
            chh[...] = jnp.zeros_like(chh)
            pnext[...] = jnp.zeros_like(pnext)

        pcur[...] = pnext[...]
        xv = xa_ref[...]
        rinv = lax.rsqrt(jnp.mean(xv * xv, axis=-1, keepdims=True) + RMS_EPS)
        h0 = (xv * rinv * (g_ref[...] * (1.0 + sc_ref[...])) + sh_ref[...]).astype(BF16)
        h0_ref[...] = h0

        def project(k):
            def emit():
                pk = jnp.dot(h0, win_ref[k], preferred_element_type=F32).astype(BF16)
                p_ref[:, k * nb:(k + 1) * nb] = pk
                pnext[:, k * nb:(k + 1) * nb] = pk
            return emit

        def mixer():
            piece = lambda k: pcur[:, k * D:(k + 1) * D].astype(F32)
            xa = piece(0)
            rows = _rows(ts, D)
            taps = _conv_taps(jnp.concatenate([cxa[...], xa], axis=0), hl, ts, 4)
            xc = cb_ref[...] + sum(cw_ref[k:k + 1, :] * taps[k] for k in range(4))
            r, ig = _lru_gates(xc, wa_ref, ba_ref[...], wx_ref, bx_ref[...])
            a, m, _ = _lru_decay(r, _softplus_neg(lam_ref[...]), (rows == 0) & (i == 1))
            yield 0.26
            h = _run(_scan_fwd_steps(a, m * ig * xc, chh[hl - 1:hl, :]))
            yield 0.51
            gcp, v = piece(3), piece(4)
            z = gcp * v
            ztaps = _conv_taps(jnp.concatenate([czz[...], z], axis=0), hl, ts, 3)
            yb = piece(2) * sum(sw_ref[k:k + 1, :] * ztaps[k] for k in range(3))
            ga, gb = piece(1), piece(5)
            y = jnp.concatenate([h * (ga * _sigmoid(ga)), yb * (gb * _sigmoid(gb))], axis=1).astype(BF16)
            yield 0.76
            y_ref[...] = y
            x1_ref[...] = xb_ref[...] + gate_ref[...] * jnp.dot(y, wo_ref[...], preferred_element_type=F32)
            h_ref[...] = h.astype(BF16)
            cxa[...] = xa[ts - hl:, :]
            czz[...] = z[ts - hl:, :]
            chh[...] = jnp.where(i > 0, h[ts - hl:, :], 0.0)

        _paired(mixer(), [project(k) for k in range(N_CHIP)])

    def full(a):
        return pl.BlockSpec(a.shape, lambda i: (0,) * a.ndim)

    ahead = lambda w: pl.BlockSpec((ts, w), lambda i: (jnp.minimum(i, n_t - 1), 0))
    behind = lambda w: pl.BlockSpec((ts, w), lambda i: (jnp.maximum(i - 1, 0), 0))
    args = (x, x, g, sc, sh, w_in, gate, cw, cb, wa, ba, wx, bx, lam, sw, wo)
    return pl.pallas_call(
        body, name="l0_fwd", grid=(n_t + 1,),
        in_specs=[ahead(D), behind(D)] + [full(a) for a in args[2:]],
        out_specs=[behind(D), behind(D), behind(2 * D), ahead(D), ahead(N_CHIP * nb)],
        out_shape=[jax.ShapeDtypeStruct((s_len, D), F32), jax.ShapeDtypeStruct((s_len, D), BF16),
                   jax.ShapeDtypeStruct((s_len, 2 * D), BF16), jax.ShapeDtypeStruct((s_len, D), BF16),
                   jax.ShapeDtypeStruct((s_len, N_CHIP * nb), BF16)],
        scratch_shapes=[pltpu.VMEM((ts, N_CHIP * nb), BF16)] * 2 + [pltpu.VMEM((hl, D), F32)] * 3,
        compiler_params=_cp(("arbitrary",)),
    )(*args)


def _l1_fwd(x1, g, sc, sh, w_in, tgt, gate, wg, bg, scale, wo, gf):
    s_len, nb = x1.shape[0], w_in.shape[2]
    ts = _tile(s_len, TS_MIX)
    n_t = s_len // ts
    pw, gd, hl = 2 * D, POOL_GROUP_DIM, POOL_HALO

    def body(xa_ref, xb_ref, t_ref, g_ref, sc_ref, sh_ref, win_ref, gate_ref, wg_ref, bg_ref, scl_ref, wo_ref, gf_ref,
             d_ref, mx_ref, y_ref, dx_ref, loss_ref, dgf_ref, h1_ref, p_ref, pcur, pnext, cv):
        i = pl.program_id(0)

        @pl.when(i == 0)
        def _():
            cv[...] = jnp.zeros_like(cv)
            loss_ref[...] = jnp.zeros_like(loss_ref)
            dgf_ref[...] = jnp.zeros_like(dgf_ref)
            pnext[...] = jnp.zeros_like(pnext)

        pcur[...] = pnext[...]
        xv = xa_ref[...]
        rinv = lax.rsqrt(jnp.mean(xv * xv, axis=-1, keepdims=True) + RMS_EPS)
        h1 = (xv * rinv * (g_ref[...] * (1.0 + sc_ref[...])) + sh_ref[...]).astype(BF16)
        h1_ref[...] = h1

        def project(k):
            def emit():
                pk = jnp.dot(h1, win_ref[k], preferred_element_type=F32).astype(BF16)
                p_ref[:, k * nb:(k + 1) * nb] = pk
                pnext[:, k * nb:(k + 1) * nb] = pk
            return emit

        def mixer():
            v = pcur[:, 0:pw].astype(F32)
            sums = _window_sums(jnp.concatenate([cv[...], v], axis=0), _down)
            inv = _pool_inv_counts(jnp.maximum(i - 1, 0) * ts, ts)
            dd = [sums[k][hl:hl + ts] * inv[k] - v[:, k * gd:(k + 1) * gd] for k in range(4)]
            d_ref[...] = jnp.concatenate(dd, axis=1).astype(BF16)
            yield 0.26
            mixed = jnp.concatenate(
                [jnp.dot(dd[k].astype(BF16), wg_ref[k], preferred_element_type=F32) for k in range(4)], axis=1) + bg_ref[...]
            mx_ref[...] = mixed.astype(BF16)
            gg = pcur[:, pw:2 * pw].astype(F32)
            y = (mixed * scl_ref[...] * (gg * _sigmoid(gg))).astype(BF16)
            y_ref[...] = y
            yield 0.51
            x2 = xb_ref[...] + gate_ref[...] * jnp.dot(y, wo_ref[...], preferred_element_type=F32)
            yield 0.76
            r2 = lax.rsqrt(jnp.mean(x2 * x2, axis=-1, keepdims=True) + RMS_EPS)
            n2 = x2 * r2
            err = n2 * gf_ref[...] - t_ref[...]
            loss_ref[...] += jnp.where(i > 0, jnp.sum(err * err, axis=0, keepdims=True), 0.0)
            dyf = err * (1.0 / D)
            dgf_ref[...] += jnp.where(i > 0, jnp.sum(dyf * n2, axis=0, keepdims=True), 0.0)
            dn = dyf * gf_ref[...]
            dx_ref[...] = r2 * (dn - n2 * jnp.mean(dn * n2, axis=-1, keepdims=True))
            cv[...] = v[ts - hl:, :]

        _paired(mixer(), [project(k) for k in range(N_CHIP)])

    def full(a):
        return pl.BlockSpec(a.shape, lambda i: (0,) * a.ndim)

    ahead = lambda w: pl.BlockSpec((ts, w), lambda i: (jnp.minimum(i, n_t - 1), 0))
    behind = lambda w: pl.BlockSpec((ts, w), lambda i: (jnp.maximum(i - 1, 0), 0))
    acc = pl.BlockSpec((1, D), lambda i: (0, 0))
    args = (x1, x1, tgt, g, sc, sh, w_in, gate, wg, bg, scale, wo, gf)
    return pl.pallas_call(
        body, name="l1_fwd", grid=(n_t + 1,),
        in_specs=[ahead(D), behind(D), behind(D)] + [full(a) for a in args[3:]],
        out_specs=[behind(pw), behind(pw), behind(pw), behind(D), acc, acc, ahead(D), ahead(N_CHIP * nb)],
        out_shape=[jax.ShapeDtypeStruct((s_len, pw), BF16)] * 3 + [jax.ShapeDtypeStruct((s_len, D), F32)]
        + [jax.ShapeDtypeStruct((1, D), F32)] * 2
        + [jax.ShapeDtypeStruct((s_len, D), BF16), jax.ShapeDtypeStruct((s_len, N_CHIP * nb), BF16)],
        scratch_shapes=[pltpu.VMEM((ts, N_CHIP * nb), BF16)] * 2 + [pltpu.VMEM((hl, pw), F32)],
        compiler_params=_cp(("arbitrary",)),
    )(*args)


def _l1_bwd_mix(dx2, proj, mixed, y, dpool, gate, wg, scale, wo):
    s_len = dx2.shape[0]
    ts = _tile(s_len, TS_MIX)
    n_t = s_len // ts
    pw, gd, hl = 2 * D, POOL_GROUP_DIM, POOL_HALO

    def body(dx_ref, gg_ref, mx_ref, y_ref, d_ref, gate_ref, wg_ref, sc_ref, wo_ref,
             dp_ref, mt_ref, dwg_ref, dsc_ref, dbg_ref, cq):
        i = pl.program_id(0)

        @pl.when(i == 0)
        def _():
            cq[...] = jnp.zeros_like(cq)
            dsc_ref[...] = jnp.zeros_like(dsc_ref)
            dbg_ref[...] = jnp.zeros_like(dbg_ref)
            mt_ref[...] = jnp.zeros_like(mt_ref)
            dwg_ref[...] = jnp.zeros_like(dwg_ref)

        dxv = dx_ref[...]
        dxb = dxv.astype(BF16)

        def wgrad_out(k):
            mt_ref[k] += lax.dot_general(y_ref[:, k * gd:(k + 1) * gd], dxb, TN, preferred_element_type=F32)

        dy = lax.dot_general((gate_ref[...] * dxv).astype(BF16), wo_ref[...], NT, preferred_element_type=F32)
        wgrad_out(0)
        gg = gg_ref[...].astype(F32)
        mixed = mx_ref[...].astype(F32)
        s = _sigmoid(gg)
        sg = gg * s
        dmixed = dy * sc_ref[...] * sg
        dsc_ref[...] += jnp.sum(dy * mixed * sg, axis=0, keepdims=True)
        dbg_ref[...] += jnp.sum(dmixed, axis=0, keepdims=True)
        dmb = dmixed.astype(BF16)
        wgrad_out(1)
        dp_ref[:, pw:2 * pw] = (dy * sc_ref[...] * mixed * (s * (1.0 + gg * (1.0 - s)))).astype(BF16)
        inv = _pool_inv_counts((n_t - 1 - i) * ts, ts)
        dd = []
        for k in range(4):
            dmk = dmb[:, k * gd:(k + 1) * gd]
            dd.append(lax.dot_general(dmk, wg_ref[k], NT, preferred_element_type=F32))
            dwg_ref[k] += lax.dot_general(d_ref[:, k * gd:(k + 1) * gd], dmk, TN, preferred_element_type=F32)
        wgrad_out(2)
        q = jnp.concatenate([dd[k] * inv[k] for k in range(4)], axis=1)
        sums = _window_sums(jnp.concatenate([q, cq[...]], axis=0), _up)
        wgrad_out(3)
        dp_ref[:, 0:pw] = jnp.concatenate([sums[k][0:ts] - dd[k] for k in range(4)], axis=1).astype(BF16)
        cq[...] = q[0:hl, :]

    def full(a):
        return pl.BlockSpec(a.shape, lambda i: (0,) * a.ndim)

    rev = lambda w, j=0: pl.BlockSpec((ts, w), lambda i: (n_t - 1 - i, j))
    acc = pl.BlockSpec((1, pw), lambda i: (0, 0))
    return pl.pallas_call(
        body, name="l1_bwd_mix", grid=(n_t,),
        in_specs=[rev(D), rev(pw, 1), rev(pw), rev(pw), rev(pw)] + [full(a) for a in (gate, wg, scale, wo)],
        out_specs=[rev(2 * pw), pl.BlockSpec((N_CHIP, gd, D), lambda i: (0, 0, 0)),
                   pl.BlockSpec((4, gd, gd), lambda i: (0, 0, 0)), acc, acc],
        out_shape=[jax.ShapeDtypeStruct((s_len, 2 * pw), BF16), jax.ShapeDtypeStruct((N_CHIP, gd, D), F32),
                   jax.ShapeDtypeStruct((4, gd, gd), F32),
                   jax.ShapeDtypeStruct((1, pw), F32), jax.ShapeDtypeStruct((1, pw), F32)],
        scratch_shapes=[pltpu.VMEM((hl, pw), F32)],
        compiler_params=_cp(("arbitrary",)),
    )(dx2, proj, mixed, y, dpool, gate, wg, scale, wo)


def _l0_bwd_mix(dx1, proj, hst, y, gate, cw, cb, wa, ba, wx, bx, lam, sw, wo):
    s_len = dx1.shape[0]
    ts = _tile(s_len, TS_MIX)
    n_t = s_len // ts
    hl, hb = SUBLANES, BF16_ROWS
    yb_w = 2 * D // N_CHIP

    def body(dx_ref, p_ref, ph_ref, h_ref, hh_ref, y_ref, gate_ref, cw_ref, cb_ref, wa_ref, ba_ref, wx_ref, bx_ref,
             lam_ref, sw_ref, wo_ref, dp_ref, mt_ref, dwa_ref, dwx_ref, sm_ref, cg, cdxc, cdcz, ca):
        i = pl.program_id(0)
        ri = n_t - 1 - i

        @pl.when(i == 0)
        def _():
            cg[...] = jnp.zeros_like(cg)
            ca[...] = jnp.zeros_like(ca)
            cdxc[...] = jnp.zeros_like(cdxc)
            cdcz[...] = jnp.zeros_like(cdcz)
            sm_ref[...] = jnp.zeros_like(sm_ref)
            mt_ref[...] = jnp.zeros_like(mt_ref)
            dwa_ref[...] = jnp.zeros_like(dwa_ref)
            dwx_ref[...] = jnp.zeros_like(dwx_ref)

        dxb = dx_ref[...].astype(BF16)

        def wgrad_out(k):
            mt_ref[k] += lax.dot_general(y_ref[:, k * yb_w:(k + 1) * yb_w], dxb, TN, preferred_element_type=F32)

        wgrad_out(0)
        has_prev = (ri > 0).astype(F32)
        xa, ga, gbp, gcp, v, gb = [p_ref[:, k * D:(k + 1) * D].astype(F32) for k in range(6)]
        prev = lambda k: ph_ref[:, k * D:(k + 1) * D].astype(F32)[hb - hl:hb] * has_prev
        rows = _rows(ts, D)
        first = (rows == 0) & (ri == 0)
        xtaps = _conv_taps(jnp.concatenate([prev(0), xa], axis=0), hl, ts, 4)
        xc = cb_ref[...] + sum(cw_ref[k:k + 1, :] * xtaps[k] for k in range(4))
        r, ig = _lru_gates(xc, wa_ref, ba_ref[...], wx_ref, bx_ref[...])
        sp = _softplus_neg(lam_ref[...])
        a, m, inv_m = _lru_decay(r, sp, first)
        z = gcp * v
        ztaps = _conv_taps(jnp.concatenate([prev(3) * prev(4), z], axis=0), hl, ts, 3)
        cz = sum(sw_ref[k:k + 1, :] * ztaps[k] for k in range(3))
        h = h_ref[...].astype(F32)
        hprev = _down(jnp.concatenate([hh_ref[...].astype(F32)[hb - hl:hb] * has_prev, h], axis=0), 1)[hl:hl + ts]
        dy = lax.dot_general((gate_ref[...] * dx_ref[...]).astype(BF16), wo_ref[...], NT, preferred_element_type=F32)
        dya_pre, dyb_pre = dy[:, 0:D], dy[:, D:2 * D]
        s_a, s_b = _sigmoid(ga), _sigmoid(gb)
        dp_ref[:, D:2 * D] = (dya_pre * h * (s_a * (1.0 + ga * (1.0 - s_a)))).astype(BF16)
        dp_ref[:, 5 * D:6 * D] = (dyb_pre * (gbp * cz) * (s_b * (1.0 + gb * (1.0 - s_b)))).astype(BF16)
        dya = dya_pre * (ga * s_a)
        dyb = dyb_pre * (gb * s_b)
        wgrad_out(1)
        dp_ref[:, 2 * D:3 * D] = (dyb * cz).astype(BF16)
        dcz = dyb * gbp
        for k in range(3):
            sm_ref[8 + k:9 + k, :] += jnp.sum(dcz * ztaps[k], axis=0, keepdims=True)
        dcz_ext = jnp.concatenate([dcz, cdcz[...]], axis=0)
        dz = sum(sw_ref[k:k + 1, :] * _up(dcz_ext, 2 - k)[0:ts] for k in range(3))
        dp_ref[:, 3 * D:4 * D] = (dz * v).astype(BF16)
        dp_ref[:, 4 * D:5 * D] = (dz * gcp).astype(BF16)
        cdcz[...] = dcz[0:hl, :]
        alpha = _up(jnp.concatenate([a, ca[...]], axis=0), 1)[0:ts]
        wgrad_out(2)
        dh = _run(_scan_rev_steps(alpha, dya, cg[0:1, :]))
        wgrad_out(3)
        cg[...] = dh[0:hl, :]
        ca[...] = a[0:hl, :]
        da = dh * hprev
        dm = dh * ig * xc
        di = dh * m * xc
        dxc = dh * m * ig
        dl = da * a - jnp.where(first, 0.0, dm * (a * a) * inv_m)
        sm_ref[7:8, :] += jnp.sum(dl * r, axis=0, keepdims=True) * (-LRU_C)
        dpa = (dl * sp) * (-LRU_C) * r * (1.0 - r)
        dpx = di * ig * (1.0 - ig)
        sm_ref[5:6, :] += jnp.sum(dpa, axis=0, keepdims=True)
        sm_ref[6:7, :] += jnp.sum(dpx, axis=0, keepdims=True)
        dpa_b, dpx_b, xc_b = dpa.astype(BF16), dpx.astype(BF16), xc.astype(BF16)
        back = []
        for hd in range(LRU_HEADS):
            sl = slice(hd * LRU_HEAD_DIM, (hd + 1) * LRU_HEAD_DIM)
            back.append(lax.dot_general(dpa_b[:, sl], wa_ref[hd], NT, preferred_element_type=F32)
                        + lax.dot_general(dpx_b[:, sl], wx_ref[hd], NT, preferred_element_type=F32))
            dwa_ref[hd] += lax.dot_general(xc_b[:, sl], dpa_b[:, sl], TN, preferred_element_type=F32)
            dwx_ref[hd] += lax.dot_general(xc_b[:, sl], dpx_b[:, sl], TN, preferred_element_type=F32)
        dxc = dxc + jnp.concatenate(back, axis=1)
        sm_ref[4:5, :] += jnp.sum(dxc, axis=0, keepdims=True)
        for k in range(4):
            sm_ref[k:k + 1, :] += jnp.sum(dxc * xtaps[k], axis=0, keepdims=True)
        dxc_ext = jnp.concatenate([dxc, cdxc[...]], axis=0)
        dp_ref[:, 0:D] = sum(cw_ref[k:k + 1, :] * _up(dxc_ext, 3 - k)[0:ts] for k in range(4)).astype(BF16)
        cdxc[...] = dxc[0:hl, :]

    def full(a):
        return pl.BlockSpec(a.shape, lambda i: (0,) * a.ndim)

    rev = lambda w: pl.BlockSpec((ts, w), lambda i: (n_t - 1 - i, 0))
    halo = lambda w: pl.BlockSpec((hb, w), lambda i: (jnp.maximum((n_t - 1 - i) * (ts // hb) - 1, 0), 0))
    return pl.pallas_call(
        body, name="l0_bwd_mix", grid=(n_t,),
        in_specs=[rev(D), rev(6 * D), halo(6 * D), rev(D), halo(D), rev(2 * D)]
        + [full(a) for a in (gate, cw, cb, wa, ba, wx, bx, lam, sw, wo)],
        out_specs=[rev(6 * D), pl.BlockSpec((N_CHIP, yb_w, D), lambda i: (0, 0, 0)),
                   pl.BlockSpec(wa.shape, lambda i: (0, 0, 0)), pl.BlockSpec(wa.shape, lambda i: (0, 0, 0)),
                   pl.BlockSpec((2 * SUBLANES, D), lambda i: (0, 0))],
        out_shape=[jax.ShapeDtypeStruct((s_len, 6 * D), BF16), jax.ShapeDtypeStruct((N_CHIP, yb_w, D), F32),
                   jax.ShapeDtypeStruct(wa.shape, F32), jax.ShapeDtypeStruct(wa.shape, F32),
                   jax.ShapeDtypeStruct((2 * SUBLANES, D), F32)],
        scratch_shapes=[pltpu.VMEM((hl, D), F32)] * 4,
        compiler_params=_cp(("arbitrary",)),
    )(dx1, proj, proj, hst, hst, y, gate, cw, cb, wa, ba, wx, bx, lam, sw, wo)


def _dgrad_norm(dproj, w, x, dres, g, sc, name, after=None):
    s_len, nb = x.shape[0], w.shape[2]
    ts = _tile(s_len, TS_DGRAD)
    order = [] if after is None else [after]

    def body(dp_ref, w_ref, x_ref, dr_ref, g_ref, sc_ref, *rest):
        dx_ref, s1_ref, s2_ref = rest[len(order):]

        @pl.when(pl.program_id(0) == 0)
        def _():
            s1_ref[...] = jnp.zeros_like(s1_ref)
            s2_ref[...] = jnp.zeros_like(s2_ref)

        dh = sum(lax.dot_general(dp_ref[:, k * nb:(k + 1) * nb], w_ref[k], NT, preferred_element_type=F32)
                 for k in range(N_CHIP))
        xv = x_ref[...]
        r = lax.rsqrt(jnp.mean(xv * xv, axis=-1, keepdims=True) + RMS_EPS)
        n = xv * r
        s1_ref[...] += jnp.sum(dh, axis=0, keepdims=True)
        s2_ref[...] += jnp.sum(dh * n, axis=0, keepdims=True)
        dn = dh * (g_ref[...] * (1.0 + sc_ref[...]))
        dx_ref[...] = dr_ref[...] + r * (dn - n * jnp.mean(dn * n, axis=-1, keepdims=True))

    row = lambda wd: pl.BlockSpec((ts, wd), lambda i: (i, 0))
    vec = pl.BlockSpec((1, D), lambda i: (0, 0))
    return pl.pallas_call(
        body, name=name, grid=(s_len // ts,),
        in_specs=[row(N_CHIP * nb), pl.BlockSpec(w.shape, lambda i: (0, 0, 0)), row(D), row(D), vec, vec]
        + [ANY] * len(order),
        out_specs=[row(D), vec, vec],
        out_shape=[jax.ShapeDtypeStruct((s_len, D), F32)] + [jax.ShapeDtypeStruct((1, D), F32)] * 2,
        compiler_params=_cp(("arbitrary",)),
    )(dproj, w, x, dres, g, sc, *order)


def _wgrad(a, b, groups, ka, nb, a_col, b_col, name, after=None):
    s_len = a.shape[0]
    ts = _tile(s_len, TS_WGRAD)
    n_s = s_len // ts
    order = [] if after is None else [after]

    def body(a_ref, b_ref, *rest):
        o_ref, wire_ref = rest[-2:]

        @pl.when(pl.program_id(1) == 0)
        def _():
            o_ref[...] = jnp.zeros_like(o_ref)

        o_ref[...] += lax.dot_general(a_ref[...].astype(BF16), b_ref[...].astype(BF16), TN, preferred_element_type=F32)

        @pl.when(pl.program_id(1) == n_s - 1)
        def _():
            wire_ref[...] = o_ref[...].astype(GRAD_WIRE_DTYPE)

    blk = pl.BlockSpec((None, ka, nb), lambda g, s: (g, 0, 0))
    return pl.pallas_call(
        body, name=name, grid=(groups, n_s),
        in_specs=[pl.BlockSpec((ts, ka), lambda g, s: (s, a_col(g))), pl.BlockSpec((ts, nb), lambda g, s: (s, b_col(g)))]
        + [ANY] * len(order),
        out_specs=[blk, blk],
        out_shape=[jax.ShapeDtypeStruct((groups, ka, nb), F32), jax.ShapeDtypeStruct((groups, ka, nb), GRAD_WIRE_DTYPE)],
        compiler_params=_cp(("parallel", "arbitrary")),
    )(a, b, *order)


def _wo_final(mt, wo, gate, name):
    rb = mt.shape[1]

    def body(m_ref, w_ref, gate_ref, dw_ref, wire_ref, dg_ref):
        @pl.when(pl.program_id(0) == 0)
        def _():
            dg_ref[...] = jnp.zeros_like(dg_ref)

        mv = m_ref[...]
        dw = mv * gate_ref[...]
        dw_ref[...] = dw
        wire_ref[...] = dw.astype(GRAD_WIRE_DTYPE)
        dg_ref[...] += jnp.sum(mv * w_ref[...].astype(F32), axis=0, keepdims=True)

    blk = pl.BlockSpec((None, rb, D), lambda k: (k, 0, 0))
    vec = pl.BlockSpec((1, D), lambda k: (0, 0))
    return pl.pallas_call(
        body, name=name, grid=(N_CHIP,), in_specs=[blk, blk, vec], out_specs=[blk, blk, vec],
        out_shape=[jax.ShapeDtypeStruct(mt.shape, F32), jax.ShapeDtypeStruct(mt.shape, GRAD_WIRE_DTYPE),
                   jax.ShapeDtypeStruct((1, D), F32)],
        compiler_params=_cp(("arbitrary",)),
    )(mt, wo, gate)


ROW_NORM_G, ROW_CONV_W, ROW_CONV_B, ROW_B_A, ROW_B_X, ROW_LAMBDA, ROW_SC_W, ROW_POOL_B, ROW_POOL_S, ROW_FINAL_G = (
    0, 2, 6, 7, 8, 9, 10, 13, 15, 17)
ROW_LOSS = 18


def _small_pack(s1_0, s2_0, s1_1, s2_1, sm0, dsc1, dbg1, dgf, losscols, dgate0, dgate1, norm_g, sc0, sc1, lam):
    def body(s1_0r, s2_0r, s1_1r, s2_1r, sm, dsc, dbg, dgfr, lcols, dg0, dg1, ng, sc0r, sc1r, lamr, buf, dmod):
        buf[...] = jnp.zeros_like(buf)
        buf[0:1, :] = s2_0r[...] * (1.0 + sc0r[...])
        buf[1:2, :] = s2_1r[...] * (1.0 + sc1r[...])
        buf[ROW_CONV_W:ROW_CONV_W + 4, :] = sm[0:4, :]
        buf[ROW_CONV_B:ROW_CONV_B + 1, :] = sm[4:5, :]
        buf[ROW_B_A:ROW_B_A + 1, :] = sm[5:6, :]
        buf[ROW_B_X:ROW_B_X + 1, :] = sm[6:7, :]
        buf[ROW_LAMBDA:ROW_LAMBDA + 1, :] = -sm[7:8, :] * _sigmoid(-lamr[...])
        buf[ROW_SC_W:ROW_SC_W + 3, :] = sm[8:11, :]
        for k in range(2):
            buf[ROW_POOL_B + k:ROW_POOL_B + k + 1, :] = dbg[:, k * D:(k + 1) * D]
            buf[ROW_POOL_S + k:ROW_POOL_S + k + 1, :] = dsc[:, k * D:(k + 1) * D]
        buf[ROW_FINAL_G:ROW_FINAL_G + 1, :] = dgfr[...]
        pieces = (s1_0r[...], s2_0r[...] * ng[0:1, :], dg0[...], s1_1r[...], s2_1r[...] * ng[1:2, :], dg1[...])
        for k, pc in enumerate(pieces):
            dmod[:, k * D:(k + 1) * D] = jnp.broadcast_to(pc, (SUBLANES, D))
        buf[ROW_LOSS:ROW_LOSS + 1, :] = jnp.broadcast_to(jnp.sum(lcols[...], axis=1, keepdims=True) * (0.5 / D), (1, D))

    args = (s1_0, s2_0, s1_1, s2_1, sm0, dsc1, dbg1, dgf, losscols, dgate0, dgate1, norm_g, sc0, sc1, lam)
    return pl.pallas_call(
        body, name="small_pack", in_specs=[VMEM] * len(args), out_specs=[VMEM] * 2,
        out_shape=[jax.ShapeDtypeStruct((SMALL_ROWS, D), F32), jax.ShapeDtypeStruct((SUBLANES, 6 * D), F32)],
        compiler_params=_cp(),
    )(*args)


def _small_comm(buf_a, buf_b, dmod8):
    ra, rb = buf_a.shape[0] // N_DEV, buf_b.shape[0] // N_DEV
    wb = buf_b.shape[1]

    def body(a_ref, b_ref, dm_ref, oa_ref, ob_ref, odm_ref, ina, inb, dslot, sa, sb, s1, r1, s2, r2):
        x, y, c = _pos()
        me = 4 * x + 2 * y + c
        peers = []
        for r in range(1, N_DEV):
            fx, fy, fc = (r >> 2) & 1, (r >> 1) & 1, r & 1
            px, py, pc = _flip(x, fx), _flip(y, fy), _flip(c, fc)
            peers.append(((px, py, pc), 4 * px + 2 * py + pc))
        seg_a = lambda d: pl.ds(pl.multiple_of(d * ra, SUBLANES), ra)
        seg_b = lambda d: pl.ds(pl.multiple_of(d * rb, SUBLANES), rb)
        first = []
        for r, (peer, pid) in enumerate(peers):
            for k, (src, dst) in enumerate(((a_ref.at[seg_a(pid), :], ina.at[r]), (b_ref.at[seg_b(pid), :], inb.at[r]),
                                            (dm_ref, dslot.at[me]))):
                cp = pltpu.make_async_remote_copy(src_ref=src, dst_ref=dst, send_sem=s1.at[3 * r + k],
                                                  recv_sem=r1.at[3 * r + k], device_id=peer, device_id_type=MESH)
                cp.start()
                first.append(cp)
        dslot[me] = dm_ref[...]
        for cp in first:
            cp.wait()
        acc_a, acc_b = a_ref[seg_a(me), :], b_ref[seg_b(me), :]
        for r in range(N_DEV - 1):
            acc_a = acc_a + ina[r]
            acc_b = acc_b + inb[r]
        sa[...] = acc_a
        sb[...] = acc_b
        oa_ref[seg_a(me), :] = acc_a
        ob_ref[seg_b(me), :] = acc_b
        second = []
        for r, (peer, pid) in enumerate(peers):
            for k, (src, dst) in enumerate(((sa, oa_ref.at[seg_a(me), :]), (sb, ob_ref.at[seg_b(me), :]))):
                cp = pltpu.make_async_remote_copy(src_ref=src, dst_ref=dst, send_sem=s2.at[2 * r + k],
                                                  recv_sem=r2.at[2 * r + k], device_id=peer, device_id_type=MESH)
                cp.start()
                second.append(cp)
        rows = _rows(SUBLANES, dm_ref.shape[1])
        dm_all = jnp.zeros(dm_ref.shape, F32)
        for d in range(N_DEV):
            dm_all = jnp.where(rows == d, dslot[d], dm_all)
        odm_ref[...] = dm_all
        for cp in second:
            cp.wait()

    nrel = N_DEV - 1
    return pl.pallas_call(
        body, name="small_comm", in_specs=[VMEM] * 3, out_specs=[VMEM] * 3,
        out_shape=[jax.ShapeDtypeStruct(buf_a.shape, F32), jax.ShapeDtypeStruct(buf_b.shape, F32),
                   jax.ShapeDtypeStruct(dmod8.shape, F32)],
        scratch_shapes=[pltpu.VMEM((nrel, ra, D), F32), pltpu.VMEM((nrel, rb, wb), F32),
                        pltpu.VMEM((N_DEV,) + dmod8.shape, F32), pltpu.VMEM((ra, D), F32), pltpu.VMEM((rb, wb), F32),
                        pltpu.SemaphoreType.DMA((3 * nrel,)), pltpu.SemaphoreType.DMA((3 * nrel,)),
                        pltpu.SemaphoreType.DMA((2 * nrel,)), pltpu.SemaphoreType.DMA((2 * nrel,))],
        compiler_params=_cp(),
    )(buf_a, buf_b, dmod8)


def _adam(w, g, m, v):
    m2 = ADAM_B1 * m + (1.0 - ADAM_B1) * g
    v2 = ADAM_B2 * v + (1.0 - ADAM_B2) * (g * g)
    m_hat = m2 / (1.0 - ADAM_B1 ** ADAM_STEP)
    v_hat = v2 / (1.0 - ADAM_B2 ** ADAM_STEP)
    return -ADAM_LR * (m_hat / (jnp.sqrt(v_hat) + ADAM_EPS) + ADAM_WD * w), m2, v2


def _small_adam(red_a, red_b, dm_all, params):
    n = len(params)

    def body(*refs):
        ra, rb, dm = refs[:3]
        wmv = refs[3:3 + 3 * n]
        outs = refs[3 + 3 * n:]
        x, y, _ = _pos()
        chip = 2 * x + y

        def shard(row0, nrows, width):
            per_row = D // width
            cands = []
            for k in range(N_CHIP):
                if nrows == 1 or per_row >= N_CHIP:
                    cands.append(ra[row0:row0 + nrows, k * width:(k + 1) * width])
                else:
                    rr, cc = divmod(k * width, D)
                    cands.append(ra[row0 + rr:row0 + rr + 1, cc:cc + width])
            g = cands[0]
            for k in range(1, N_CHIP):
                g = jnp.where(chip == k, cands[k], g)
            return g

        dms = jnp.sum(dm[...], axis=0, keepdims=True)
        hw = LRU_HEADS * LRU_HEAD_DIM
        grads = [
            ra[ROW_NORM_G:ROW_NORM_G + 2, :],
            None,
            shard(ROW_CONV_W, 4, D // N_CHIP),
            ra[ROW_CONV_B:ROW_CONV_B + 1, :],
            rb[0:hw, :],
            ra[ROW_B_A:ROW_B_A + 1, :],
            rb[hw:2 * hw, :],
            ra[ROW_B_X:ROW_B_X + 1, :],
            ra[ROW_LAMBDA:ROW_LAMBDA + 1, :],
            shard(ROW_SC_W, 3, D // N_CHIP),
            shard(ROW_POOL_B, 2, 2 * D // N_CHIP),
            shard(ROW_POOL_S, 2, 2 * D // N_CHIP),
            ra[ROW_FINAL_G:ROW_FINAL_G + 1, :],
        ]
        for p in range(n):
            w_ref, m_ref, v_ref = wmv[3 * p:3 * p + 3]
            g_out, d_out, m_out, v_out = outs[4 * p:4 * p + 4]
            if grads[p] is None:
                for l in range(2):
                    g = dms[:, l * 3 * D:(l + 1) * 3 * D]
                    dl, m2, v2 = _adam(w_ref[l:l + 1, :], g, m_ref[l:l + 1, :], v_ref[l:l + 1, :])
                    g_out[l:l + 1, :] = g
                    d_out[l:l + 1, :] = dl
                    m_out[l:l + 1, :] = m2
                    v_out[l:l + 1, :] = v2
            else:
                g = grads[p]
                dl, m2, v2 = _adam(w_ref[...], g, m_ref[...], v_ref[...])
                g_out[...] = g
                d_out[...] = dl
                m_out[...] = m2
                v_out[...] = v2

    flat = [a for p in params for a in p]
    return pl.pallas_call(
        body, name="small_adam", in_specs=[VMEM] * (3 + len(flat)), out_specs=[VMEM] * (4 * n),
        out_shape=[jax.ShapeDtypeStruct(p[0].shape, F32) for p in params for _ in range(4)],
        compiler_params=_cp(),
    )(red_a, red_b, dm_all, *flat)


def _modw_adam(ca_t, dm_sh, w, m, v):
    nw = w.shape[2]

    def body(c_ref, d_ref, w_ref, m_ref, v_ref, g_out, d_out, m_out, v_out):
        g = jnp.dot(c_ref[...], d_ref[...], precision=lax.Precision.HIGHEST, preferred_element_type=F32)
        dl, m2, v2 = _adam(w_ref[...], g, m_ref[...], v_ref[...])
        g_out[...] = g
        d_out[...] = dl
        m_out[...] = m2
        v_out[...] = v2

    blk = pl.BlockSpec((None, D, nw), lambda l: (l, 0, 0))
    return pl.pallas_call(
        body, name="modw_adam", grid=(2,),
        in_specs=[pl.BlockSpec((D, SUBLANES), lambda l: (0, 0)), pl.BlockSpec((None, SUBLANES, nw), lambda l: (l, 0, 0)),
                  blk, blk, blk],
        out_specs=[blk] * 4, out_shape=[jax.ShapeDtypeStruct(w.shape, F32)] * 4,
        compiler_params=_cp(("arbitrary",)),
    )(ca_t, dm_sh, w, m, v)


def _exchange(copies, name, out_type, n_sems, args, sequencer, after=None):
    order = [] if after is None else [after]
    n_in, n_out = len(args) + len(order), len(out_type)

    def body(*refs):
        barrier = pltpu.get_barrier_semaphore()
        peers = sequencer[1](*_pos())
        for peer in peers:
            pl.semaphore_signal(barrier, inc=1, device_id=peer, device_id_type=MESH)
        pl.semaphore_wait(barrier, len(peers))
        copies(refs[:n_in], refs[n_in:n_in + n_out], refs[n_in + n_out], refs[n_in + n_out + 1])

    sems = [pltpu.SemaphoreType.DMA((n_sems,))] * 2
    return pl.kernel(body, out_type, mesh=plsc.ScalarSubcoreMesh(axis_name="sequencer", num_cores=1), name=name,
                     scratch_types=sems, compiler_params=pltpu.CompilerParams(collective_id=sequencer[0]))(*args, *order)


def _sibling(x, y, c):
    return [(x, y, 1 - c)]


def _other_chips(x, y, c):
    return [(1 - x, y, c), (x, 1 - y, c), (1 - x, 1 - y, c)]


def _to_wire(g, name, after=None):
    _, rr, cc = g.shape
    rb = min(rr, 256)

    def body(g_ref, *rest):
        rest[-1][...] = g_ref[...].astype(GRAD_WIRE_DTYPE)

    order = [] if after is None else [after]
    blk = pl.BlockSpec((None, rb, cc), lambda k, j: (k, j, 0))
    return pl.pallas_call(
        body, name=name, grid=(N_CHIP, rr // rb), in_specs=[blk] + [ANY] * len(order), out_specs=blk,
        out_shape=jax.ShapeDtypeStruct(g.shape, GRAD_WIRE_DTYPE), compiler_params=_cp(("parallel", "parallel")),
    )(g, *order)


def _chip_scatter(ps, name, collective_id, after=None):
    n = len(ps)

    def copies(ins, outs, ssem, rsem):
        x, y, c = _pos()
        cps = []
        for a in range(n):
            for q, (fx, fy) in enumerate(((1, 0), (0, 1), (1, 1))):
                px, py = _flip(x, fx), _flip(y, fy)
                cp = pltpu.make_async_remote_copy(
                    src_ref=ins[a].at[2 * px + py], dst_ref=outs[a].at[q],
                    send_sem=ssem.at[3 * a + q], recv_sem=rsem.at[3 * a + q], device_id=(px, py, c), device_id_type=MESH)
                cp.start()
                cps.append(cp)
        for cp in cps:
            cp.wait()

    out_type = [jax.ShapeDtypeStruct((N_CHIP - 1,) + p.shape[1:], p.dtype) for p in ps]
    return _exchange(copies, name, out_type, 3 * n, ps, (collective_id, _other_chips), after)


def _add_owner(p, got, chipidx, name, after=None):
    _, hr, cc = p.shape
    rb = min(hr, 256)

    def body(k_ref, p_ref, r_ref, *rest):
        rest[-1][...] = ((p_ref[...].astype(F32) + r_ref[0].astype(F32)) + r_ref[1].astype(F32)) + r_ref[2].astype(F32)

    order = [] if after is None else [after]
    return pl.pallas_call(
        body, name=name,
        grid_spec=pltpu.PrefetchScalarGridSpec(
            num_scalar_prefetch=1, grid=(hr // rb,),
            in_specs=[pl.BlockSpec((None, rb, cc), lambda j, k_ref: (k_ref[0], j, 0)),
                      pl.BlockSpec((N_CHIP - 1, rb, cc), lambda j, k_ref: (0, j, 0))] + [ANY] * len(order),
            out_specs=pl.BlockSpec((rb, cc), lambda j, k_ref: (j, 0))),
        out_shape=jax.ShapeDtypeStruct((hr, cc), F32),
        compiler_params=_cp(("parallel",)),
    )(chipidx, p, got, *order)


def _sib_exchange(ts_, name, collective_id, after=None):
    n = len(ts_)

    def copies(ins, outs, ssem, rsem):
        x, y, c = _pos()
        cps = []
        for a in range(n):
            cp = pltpu.make_async_remote_copy(src_ref=ins[a], dst_ref=outs[a], send_sem=ssem.at[a],
                                              recv_sem=rsem.at[a], device_id=(x, y, 1 - c), device_id_type=MESH)
            cp.start()
            cps.append(cp)
        for cp in cps:
            cp.wait()

    out_type = [jax.ShapeDtypeStruct(t.shape, F32) for t in ts_]
    return _exchange(copies, name, out_type, n, ts_, (collective_id, _sibling), after)


def _adam_2d(w, g_own, g_sib, m, v, name):
    rr, cc = w.shape
    rb = min(rr, 256)

    def body(w_ref, go_ref, gs_ref, m_ref, v_ref, g_out, d_out, m_out, v_out):
        g = go_ref[...] + gs_ref[...]
        dl, m2, v2 = _adam(w_ref[...], g, m_ref[...], v_ref[...])
        g_out[...] = g
        d_out[...] = dl
        m_out[...] = m2
        v_out[...] = v2

    blk = pl.BlockSpec((rb, cc), lambda j: (j, 0))
    return pl.pallas_call(
        body, name=name, grid=(rr // rb,), in_specs=[blk] * 5, out_specs=[blk] * 4,
        out_shape=[jax.ShapeDtypeStruct((rr, cc), F32)] * 4, compiler_params=_cp(("parallel",)),
    )(w, g_own, g_sib, m, v)


def kernel(x, c, norm_g, mod_w, mod_b, hy_w_in, hy_conv_w, hy_conv_b, lru_w_a, lru_b_a, lru_w_x, lru_b_x, lru_lambda, sc_conv_w, hy_w_out, pool_w_in, pool_w_grp, pool_b_grp, pool_scale, pool_w_out, final_g, loss_target, m_norm_g, m_mod_w, m_mod_b, m_hy_w_in, m_hy_conv_w, m_hy_conv_b, m_lru_w_a, m_lru_b_a, m_lru_w_x, m_lru_b_x, m_lru_lambda, m_sc_conv_w, m_hy_w_out, m_pool_w_in, m_pool_w_grp, m_pool_b_grp, m_pool_scale, m_pool_w_out, m_final_g, v_norm_g, v_mod_w, v_mod_b, v_hy_w_in, v_hy_conv_w, v_hy_conv_b, v_lru_w_a, v_lru_b_a, v_lru_w_x, v_lru_b_x, v_lru_lambda, v_sc_conv_w, v_hy_w_out, v_pool_w_in, v_pool_w_grp, v_pool_b_grp, v_pool_scale, v_pool_w_out, v_final_g):
    ax, ay, ac = _pos()
    me = 4 * ax + 2 * ay + ac
    chip = 2 * ax + ay
    xs = x[0]
    tgt = loss_target[0]
    gd = POOL_GROUP_DIM
    kidx = chip.reshape(1).astype(jnp.int32)

    big = [hy_w_in[0], hy_w_out[0], pool_w_in[0], pool_w_grp[0].reshape(4 * 128, gd), pool_w_out[0]]
    w_in0, w_out0 = _wgather_sequencer(
        [_wcast_own_block(w, kidx, f"wcast_own_block_{a}") for a, w in enumerate(big[:2])], "wgather_l0", CIDS_WGATHER[0])

    ca_all, mod_all, small_w = _mod_fwd(jnp.broadcast_to(c, (SUBLANES, D)), mod_w, mod_b,
                                        hy_conv_w[0], sc_conv_w[0], pool_b_grp, pool_scale)
    mod_me = lax.dynamic_index_in_dim(mod_all, me, axis=1, keepdims=False)
    sh0, sc0, gt0 = (mod_me[0:1, k * D:(k + 1) * D] for k in range(3))
    sh1, sc1, gt1 = (mod_me[1:2, k * D:(k + 1) * D] for k in range(3))
    cw = small_w[SW_CONV:SW_CONV + 4, 0:D]
    sw = small_w[SW_SC:SW_SC + 3, 0:D]
    pool_b = small_w[SW_POOL_B:SW_POOL_B + 1, :]
    pool_s = small_w[SW_POOL_S:SW_POOL_S + 1, :]
    g0, g1, gf = norm_g[0:1], norm_g[1:2], final_g.reshape(1, D)
    cb, ba, bx, lam = hy_conv_b, lru_b_a, lru_b_x, lru_lambda

    w_in1, w_grp, w_out1 = _wgather_sequencer(
        [_wcast_own_block(w, kidx, f"wcast_own_block_{a + 2}", after=small_w) for a, w in enumerate(big[2:])],
        "wgather_l1", CIDS_WGATHER[1])
    w_grp =w_grp.reshape(N_CHIP, 4, 128, gd).transpose(1, 0, 2, 3).reshape(4, gd, gd)
    wa_b, wx_b = _wcast([lru_w_a[0], lru_w_x[0]])

    x1, hst, y0, h0, proj0 = _l0_fwd(xs, g0, sc0, sh0, w_in0, gt0, cw, cb, wa_b, ba, wx_b, bx, lam, sw,
                                     w_out0.reshape(2 * D, D))
    dpool, mixed, y1, dx2, losscols, dgf, h1, proj1 = _l1_fwd(x1, g1, sc1, sh1, w_in1, tgt, gt1, w_grp, pool_b, pool_s,
                                                              w_out1.reshape(2 * D, D), gf)

    def add_owners(grads, got, tag, ids, after):
        own = []
        for a, (g, r) in enumerate(zip(grads, got)):
            own.append(_add_owner(g, r, kidx, f"grad_add_owner_{tag}{a}", own[-1] if own else after))
        return own, _sib_exchange(own, f"grad_sib_exchange_{tag}", ids[1])

    dproj1, mt1, d_wgrp, dsc1, dbg1 = _l1_bwd_mix(dx2, proj1, mixed, y1, dpool, gt1, w_grp, pool_s,
                                                  w_out1.reshape(2 * D, D))
    d_win1, wire_win1 = _wgrad(h1, dproj1, N_CHIP, D, D, lambda g: 0, lambda g: g, "l1_wgrad_in")
    d_wout1, wire_wout1, dgate1 = _wo_final(mt1, w_out1, gt1, "l1_wo_final")
    d_wgrp = d_wgrp.reshape(4, N_CHIP, 128, gd).transpose(1, 0, 2, 3).reshape(N_CHIP, 4 * 128, gd)
    grads_l1 = [d_win1, d_wgrp, d_wout1]
    got_l1 = _chip_scatter([wire_win1, _to_wire(d_wgrp, "grad_to_wire_grp"), wire_wout1], "grad_chip_scatter_l1",
                           CIDS_L1[0])
    dx1, s1_1, s2_1 = _dgrad_norm(dproj1, w_in1, x1, dx2, g1, sc1, "l1_bwd_proj")

    dproj0, mt0, d_wa, d_wx, sm0 = _l0_bwd_mix(dx1, proj0, hst, y0, gt0, cw, cb, wa_b, ba, wx_b, bx, lam, sw,
                                               w_out0.reshape(2 * D, D))
    sums_l1, sib_l1 = add_owners(grads_l1, got_l1, "l1", CIDS_L1, after=sm0)
    d_win0, wire_win0 = _wgrad(h0, dproj0, N_CHIP, D, 6 * D // N_CHIP, lambda g: 0, lambda g: g, "l0_wgrad_in",
                               after=sums_l1[-1])
    d_wout0, wire_wout0, dgate0 = _wo_final(mt0, w_out0, gt0, "l0_wo_final")
    grads_l0 = [d_win0, d_wout0]
    got_l0 = _chip_scatter([wire_win0, wire_wout0], "grad_chip_scatter_l0", CIDS_L0[0], after=sib_l1[0])
    grad_x, s1_0, s2_0 = _dgrad_norm(dproj0, w_in0, xs, dx1, g0, sc0, "l0_bwd_proj", after=wire_win0)
    sums_l0, sib_l0 = add_owners(grads_l0, got_l0, "l0", CIDS_L0, after=s1_0)

    buf_a, dmod8 = _small_pack(s1_0, s2_0, s1_1, s2_1, sm0, dsc1, dbg1, dgf, losscols, dgate0, dgate1,
                                      norm_g, sc0, sc1, lam)
    hw = LRU_HEADS * LRU_HEAD_DIM
    buf_b = jnp.concatenate([d_wa.reshape(hw, LRU_HEAD_DIM), d_wx.reshape(hw, LRU_HEAD_DIM)], axis=0)
    red_a, red_b, dm_all = _small_comm(buf_a, buf_b, dmod8)
    small = [(norm_g, m_norm_g, v_norm_g), (mod_b, m_mod_b, v_mod_b),
             (hy_conv_w[0], m_hy_conv_w[0], v_hy_conv_w[0]), (hy_conv_b, m_hy_conv_b, v_hy_conv_b),
             tuple(a.reshape(hw, LRU_HEAD_DIM) for a in (lru_w_a, m_lru_w_a, v_lru_w_a)),
             (lru_b_a, m_lru_b_a, v_lru_b_a),
             tuple(a.reshape(hw, LRU_HEAD_DIM) for a in (lru_w_x, m_lru_w_x, v_lru_w_x)),
             (lru_b_x, m_lru_b_x, v_lru_b_x), (lru_lambda, m_lru_lambda, v_lru_lambda),
             (sc_conv_w[0], m_sc_conv_w[0], v_sc_conv_w[0]), (pool_b_grp, m_pool_b_grp, v_pool_b_grp),
             (pool_scale, m_pool_scale, v_pool_scale),
             tuple(a.reshape(1, D) for a in (final_g, m_final_g, v_final_g))]
    small_names = ["norm_g", "mod_b", "hy_conv_w", "hy_conv_b", "lru_w_a", "lru_b_a", "lru_w_x", "lru_b_x",
                   "lru_lambda", "sc_conv_w", "pool_b_grp", "pool_scale", "final_g"]
    small_out = _small_adam(red_a, red_b, dm_all, small)
    res = {}
    shapes = dict(norm_g=norm_g, mod_b=mod_b, hy_conv_w=hy_conv_w, hy_conv_b=hy_conv_b, lru_w_a=lru_w_a, lru_b_a=lru_b_a,
                  lru_w_x=lru_w_x, lru_b_x=lru_b_x, lru_lambda=lru_lambda, sc_conv_w=sc_conv_w, pool_b_grp=pool_b_grp,
                  pool_scale=pool_scale, final_g=final_g)
    for p, nm in enumerate(small_names):
        res[nm] = tuple(o.reshape(shapes[nm].shape) for o in small_out[4 * p:4 * p + 4])

    nw = mod_w.shape[2]
    dm_sh = jnp.stack([lax.dynamic_slice_in_dim(dm_all[:, l * 3 * D:(l + 1) * 3 * D], chip * nw, nw, axis=1)
                       for l in range(2)])
    res["mod_w"] = tuple(_modw_adam(ca_all.T, dm_sh, mod_w, m_mod_w, v_mod_w))

    sums = list(sums_l0) + list(sums_l1)
    sib_sums = list(sib_l0) + list(sib_l1)
    big_names = ["hy_w_in", "hy_w_out", "pool_w_in", "pool_w_grp", "pool_w_out"]
    big_wmv = [(hy_w_in, m_hy_w_in, v_hy_w_in), (hy_w_out, m_hy_w_out, v_hy_w_out), (pool_w_in, m_pool_w_in, v_pool_w_in),
               (pool_w_grp, m_pool_w_grp, v_pool_w_grp), (pool_w_out, m_pool_w_out, v_pool_w_out)]
    for a, nm in enumerate(big_names):
        rr, cc = big[a].shape
        w, m, v = (t.reshape(rr, cc) for t in big_wmv[a])
        outs = _adam_2d(w, sums[a], sib_sums[a], m, v, f"adam_{nm}")
        res[nm] = tuple(o.reshape(big_wmv[a][0].shape) for o in outs)

    loss = red_a[ROW_LOSS, 0]
    order = ["norm_g", "mod_w", "mod_b", "hy_w_in", "hy_conv_w", "hy_conv_b", "lru_w_a", "lru_b_a", "lru_w_x", "lru_b_x",
             "lru_lambda", "sc_conv_w", "hy_w_out", "pool_w_in", "pool_w_grp", "pool_b_grp", "pool_scale", "pool_w_out",
             "final_g"]
    return (loss, grad_x[None], *[res[nm][0] for nm in order], *[res[nm][1] for nm in order],
            *[res[nm][2] for nm in order], *[res[nm][3] for nm in order])
```

```python
import jax
import jax.numpy as jnp
from jax import lax
from jax.experimental import pallas as pl
from jax.experimental.pallas import tpu as pltpu
from jax.experimental.pallas import tpu_sc as plsc

F32, BF16 = jnp.float32, jnp.bfloat16
D = 1024
RMS_EPS = 1e-6
SQRT_FLOOR = 1e-30
LRU_C = 8.0
LRU_HEADS, LRU_HEAD_DIM = 8, 128
POOL_WINDOWS = (2, 4, 8, 16)
POOL_GROUP_DIM = 512
ADAM_LR, ADAM_B1, ADAM_B2, ADAM_EPS, ADAM_WD, ADAM_STEP = 0.001, 0.9, 0.999, 1e-08, 0.01, 10
MESH = pl.DeviceIdType.MESH
CIDS_WGATHER = (1, 8)
CIDS_L1 = (2, 3)
CIDS_L0 = (4, 5)
N_DEV, N_CHIP = 8, 4
SUBLANES = 8
BF16_ROWS = 16
POOL_HALO = 16
TS_MIX, TS_WGRAD, TS_DGRAD = 256, 1024, 256
SMALL_ROWS = 64
GRAD_WIRE_DTYPE = BF16
ANY = pl.BlockSpec(memory_space=pl.ANY)
VMEM = pl.BlockSpec(memory_space=pltpu.VMEM)
NT = (((1,), (1,)), ((), ()))
TN = (((0,), (0,)), ((), ()))


def _cp(sem=None, vmem_mb=56):
    kw = dict(vmem_limit_bytes=vmem_mb * 2 ** 20)
    if sem is not None:
        kw["dimension_semantics"] = sem
    return pltpu.CompilerParams(**kw)


def _tile(n, t):
    return min(n, t)


def _pos():
    return lax.axis_index("x"), lax.axis_index("y"), lax.axis_index("c")


def _flip(v, f):
    return 1 - v if f else v


def _sigmoid(z):
    return 0.5 * jnp.tanh(0.5 * z) + 0.5


def _rows(n, c):
    return lax.broadcasted_iota(jnp.int32, (n, c), 0)


def _down(a, d):
    return a if d == 0 else pltpu.roll(a, d, 0)


def _up(a, d):
    return a if d == 0 else pltpu.roll(a, a.shape[0] - d, 0)


def _scan_fwd_steps(a, u, carry):
    n, c = a.shape
    sub = _rows(SUBLANES, c)
    out = []
    for k in range(n // SUBLANES):
        p = a[k * SUBLANES:(k + 1) * SUBLANES]
        g = u[k * SUBLANES:(k + 1) * SUBLANES]
        for d in (1, 2, 4):
            keep = sub >= d
            g = g + p * jnp.where(keep, pltpu.roll(g, d, 0), 0.0)
            p = p * jnp.where(keep, pltpu.roll(p, d, 0), 1.0)
        h = g + p * carry
        carry = h[SUBLANES - 1:SUBLANES, :]
        out.append(h)
        yield
    return jnp.concatenate(out, axis=0)


def _scan_rev_steps(alpha, b, carry):
    n, c = alpha.shape
    sub = _rows(SUBLANES, c)
    out = []
    for k in reversed(range(n // SUBLANES)):
        p = alpha[k * SUBLANES:(k + 1) * SUBLANES]
        g = b[k * SUBLANES:(k + 1) * SUBLANES]
        for d in (1, 2, 4):
            keep = sub < SUBLANES - d
            g = g + p * jnp.where(keep, pltpu.roll(g, SUBLANES - d, 0), 0.0)
            p = p * jnp.where(keep, pltpu.roll(p, SUBLANES - d, 0), 1.0)
        h = g + p * carry
        carry = h[0:1, :]
        out.append(h)
        yield
    return jnp.concatenate(out[::-1], axis=0)


def _run(steps):
    while True:
        try:
            next(steps)
        except StopIteration as done:
            return done.value


def _paired(progress, pieces):
    n, done = len(pieces), 1
    pieces[0]()
    for frac in progress:
        while done < n and done <= frac * n:
            pieces[done]()
            done += 1
    while done < n:
        pieces[done]()
        done += 1


def _conv_taps(ext, halo, n, width):
    return [_down(ext, width - 1 - k)[halo:halo + n] for k in range(width)]


def _lru_gates(xc, wa_ref, ba, wx_ref, bx):
    xb = xc.astype(BF16)
    pa, px = [], []
    for h in range(LRU_HEADS):
        xh = xb[:, h * LRU_HEAD_DIM:(h + 1) * LRU_HEAD_DIM]
        pa.append(jnp.dot(xh, wa_ref[h], preferred_element_type=F32))
        px.append(jnp.dot(xh, wx_ref[h], preferred_element_type=F32))
    r = _sigmoid(jnp.concatenate(pa, axis=1) + ba)
    ig = _sigmoid(jnp.concatenate(px, axis=1) + bx)
    return r, ig


def _softplus_neg(lam):
    return jnp.maximum(-lam, 0.0) + jnp.log1p(jnp.exp(-jnp.abs(lam)))


def _recip_1_to_2(d):
    r0 = pl.reciprocal(d, approx=True)
    return r0 * (2.0 - d * r0)


def _lru_decay(r, sp, first):
    big_l = (-LRU_C) * r * sp
    a = jnp.exp(big_l)
    th = jnp.tanh(big_l)
    q = (-2.0 * th) * _recip_1_to_2(1.0 - th)
    rs = lax.rsqrt(jnp.maximum(q, SQRT_FLOOR))
    return a, jnp.where(first, 1.0, q * rs), rs


def _pool_inv_counts(t0, n):
    t = (t0 + lax.broadcasted_iota(jnp.int32, (n, 1), 0) + 1).astype(F32)
    return [1.0 / jnp.minimum(t, float(w)) for w in POOL_WINDOWS]


def _window_sums(ext, shift):
    gd = POOL_GROUP_DIM
    out = []
    s = ext
    for k in range(len(POOL_WINDOWS)):
        s = s + shift(s, 2 ** k)
        out.append(s[:, 0:gd])
        if k + 1 < len(POOL_WINDOWS):
            s = s[:, gd:]
    return out


SW_ROWS, SW_COLS = 16, 2 * D
SW_CONV, SW_SC, SW_POOL_B, SW_POOL_S = 0, 4, 8, 9


def _mod_fwd(c8, mod_w, mod_b, conv_w, sc_w, pool_b, pool_s):
    nw = mod_w.shape[2]
    cq, pq = conv_w.shape[1], pool_b.shape[1]

    def body(c_ref, w_ref, b_ref, cw_ref, sw_ref, pb_ref, ps_ref, ca_ref, mod_ref, small_ref,
             cslot, mslot, msend, pslot, psend, s1, r1, s2, r2, s3, r3):
        x, y, c = _pos()
        me = 4 * x + 2 * y + c
        chip = 2 * x + y
        first = []
        for r in range(1, N_DEV):
            fx, fy, fc = (r >> 2) & 1, (r >> 1) & 1, r & 1
            cp = pltpu.make_async_remote_copy(
                src_ref=c_ref, dst_ref=cslot.at[me], send_sem=s1.at[r - 1], recv_sem=r1.at[r - 1],
                device_id=(_flip(x, fx), _flip(y, fy), _flip(c, fc)), device_id_type=MESH)
            cp.start()
            first.append(cp)
        cslot[me] = c_ref[...]
        for cp in first:
            cp.wait()
        rows = _rows(SUBLANES, D)
        call = jnp.zeros((SUBLANES, D), F32)
        for d in range(N_DEV):
            call = jnp.where(rows == d, cslot[d], call)
        ca = call * _sigmoid(call)
        ca_ref[...] = ca
        for l in range(2):
            msend[l] = jnp.dot(ca, w_ref[l], precision=lax.Precision.HIGHEST, preferred_element_type=F32)
        psend[...] = jnp.zeros_like(psend)
        psend[SW_CONV:SW_CONV + 4, 0:cq] = cw_ref[...]
        psend[SW_SC:SW_SC + 3, 0:cq] = sw_ref[...]
        psend[SW_POOL_B:SW_POOL_B + 1, :] = pb_ref[...]
        psend[SW_POOL_S:SW_POOL_S + 1, :] = ps_ref[...]
        second = []
        for q, (fx, fy) in enumerate(((1, 0), (0, 1), (1, 1))):
            peer = (_flip(x, fx), _flip(y, fy), c)
            for src, dst, ss, rs in ((msend, mslot, s2, r2), (psend, pslot, s3, r3)):
                cp = pltpu.make_async_remote_copy(src_ref=src, dst_ref=dst.at[chip], send_sem=ss.at[q], recv_sem=rs.at[q],
                                                  device_id=peer, device_id_type=MESH)
                cp.start()
                second.append(cp)
        mslot[chip] = msend[...]
        pslot[chip] = psend[...]
        for cp in second:
            cp.wait()
        small_ref[...] = jnp.zeros_like(small_ref)
        for j in range(N_CHIP):
            for l in range(2):
                mod_ref[l, :, j * nw:(j + 1) * nw] = mslot[j, l] + b_ref[l:l + 1, j * nw:(j + 1) * nw]
            small_ref[0:SUBLANES, j * cq:(j + 1) * cq] = pslot[j, 0:SUBLANES, 0:cq]
            small_ref[SUBLANES:SW_ROWS, j * pq:(j + 1) * pq] = pslot[j, SUBLANES:SW_ROWS, :]

    args = (c8, mod_w, mod_b, conv_w, sc_w, pool_b, pool_s)
    dma3 = pltpu.SemaphoreType.DMA((N_CHIP - 1,))
    return pl.pallas_call(
        body, name="mod_fwd",
        in_specs=[VMEM] * len(args), out_specs=[VMEM] * 3,
        out_shape=[jax.ShapeDtypeStruct((SUBLANES, D), F32), jax.ShapeDtypeStruct((2, SUBLANES, N_CHIP * nw), F32),
                   jax.ShapeDtypeStruct((SW_ROWS, SW_COLS), F32)],
        scratch_shapes=[pltpu.VMEM((N_DEV, SUBLANES, D), F32), pltpu.VMEM((N_CHIP, 2, SUBLANES, nw), F32),
                        pltpu.VMEM((2, SUBLANES, nw), F32), pltpu.VMEM((N_CHIP, SW_ROWS, pq), F32),
                        pltpu.VMEM((SW_ROWS, pq), F32),
                        pltpu.SemaphoreType.DMA((N_DEV - 1,)), pltpu.SemaphoreType.DMA((N_DEV - 1,)),
                        dma3, dma3, dma3, dma3],
        compiler_params=_cp(),
    )(*args)


def _wcast(ws):
    def body(*refs):
        n = len(refs) // 2
        for a in range(n):
            refs[n + a][...] = refs[a][...].astype(BF16)

    return pl.pallas_call(
        body, name="wcast", in_specs=[VMEM] * len(ws), out_specs=[VMEM] * len(ws),
        out_shape=[jax.ShapeDtypeStruct(w.shape, BF16) for w in ws], compiler_params=_cp(),
    )(*ws)


def _wcast_own_block(w, kidx, name, after=()):
    rr, cc = w.shape
    rb = min(rr, 256)

    def body(k_ref, w_ref, *rest):
        rest[-1][...] = w_ref[...].astype(BF16)

    order = list(after)
    return pl.pallas_call(
        body, name=name,
        grid_spec=pltpu.PrefetchScalarGridSpec(
            num_scalar_prefetch=1, grid=(rr // rb,),
            in_specs=[pl.BlockSpec((rb, cc), lambda j, k_ref: (j, 0))] + [ANY] * len(order),
            out_specs=pl.BlockSpec((None, rb, cc), lambda j, k_ref: (k_ref[0], j, 0))),
        out_shape=jax.ShapeDtypeStruct((N_CHIP, rr, cc), BF16),
        compiler_params=_cp(("parallel",)),
    )(kidx, w, *order)


def _wgather_copies(outs, rows, ssem, rsem, fssem, frsem):
    n = len(outs)
    x, y, c = _pos()
    chip = 2 * x + y
    sib = (x, y, 1 - c)
    flips = ((1, 0), (0, 1), (1, 1))

    def half(a, which):
        hr = rows[a] // 2
        return pl.ds(pl.multiple_of(which * hr, BF16_ROWS), hr)

    sends = []
    for a in range(n):
        mine = outs[a].at[chip, half(a, c), :]
        for q, (fx, fy) in enumerate(flips):
            cp = pltpu.make_async_remote_copy(
                src_ref=mine, dst_ref=mine, send_sem=ssem.at[3 * a + q], recv_sem=rsem.at[3 * a + q],
                device_id=(_flip(x, fx), _flip(y, fy), c), device_id_type=MESH)
            cp.start()
            sends.append(cp)
    passed = []
    for a in range(n):
        for q, (fx, fy) in enumerate(flips):
            src_chip = 2 * _flip(x, fx) + _flip(y, fy)
            landed = outs[a].at[src_chip, half(a, c), :]
            pltpu.make_async_remote_copy(
                src_ref=landed, dst_ref=landed, send_sem=ssem.at[3 * a + q], recv_sem=rsem.at[3 * a + q],
                device_id=sib, device_id_type=MESH).wait_recv()
            cp = pltpu.make_async_remote_copy(
                src_ref=landed, dst_ref=landed, send_sem=fssem.at[3 * a + q], recv_sem=frsem.at[3 * a + q],
                device_id=sib, device_id_type=MESH)
            cp.start()
            passed.append(cp)
    for a in range(n):
        for q, (fx, fy) in enumerate(flips):
            src_chip = 2 * _flip(x, fx) + _flip(y, fy)
            other = outs[a].at[src_chip, half(a, 1 - c), :]
            pltpu.make_async_remote_copy(
                src_ref=other, dst_ref=other, send_sem=fssem.at[3 * a + q], recv_sem=frsem.at[3 * a + q],
                device_id=sib, device_id_type=MESH).wait_recv()
    for cp in sends + passed:
        cp.wait_send()


def _wgather_sequencer(bufs, name, collective_id):
    n = len(bufs)
    refs = [jax.new_ref(b, memory_space=pltpu.MemorySpace.HBM) for b in bufs]
    dma = pltpu.SemaphoreType.DMA((3 * n,))

    @pl.kernel(mesh=plsc.ScalarSubcoreMesh(axis_name="sequencer", num_cores=1), name=name,
               scratch_types=(dma, dma, dma, dma), compiler_params=pltpu.CompilerParams(collective_id=collective_id))
    def launch(ssem, rsem, fssem, frsem):
        x, y, c = _pos()
        barrier = pltpu.get_barrier_semaphore()
        for peer in ((1 - x, y, c), (x, 1 - y, c), (1 - x, 1 - y, c), (x, y, 1 - c)):
            pl.semaphore_signal(barrier, inc=1, device_id=peer, device_id_type=MESH)
        pl.semaphore_wait(barrier, 4)
        _wgather_copies(refs, [b.shape[1] for b in bufs], ssem, rsem, fssem, frsem)

    launch()
    return [r[...] for r in refs]


def _l0_fwd(x, g, sc, sh, w_in, gate, cw, cb, wa, ba, wx, bx, lam, sw, wo):
    s_len, nb = x.shape[0], w_in.shape[2]
    ts = _tile(s_len, TS_MIX)
    n_t = s_len // ts
    hl = SUBLANES

    def body(xa_ref, xb_ref, g_ref, sc_ref, sh_ref, win_ref, gate_ref, cw_ref, cb_ref, wa_ref, ba_ref, wx_ref, bx_ref,
             lam_ref, sw_ref, wo_ref, x1_ref, h_ref, y_ref, h0_ref, p_ref, pcur, pnext, cxa, czz, chh):
        i = pl.program_id(0)

        @pl.when(i == 0)
        def _():
            cxa[...] = jnp.zeros_like(cxa)
            czz[...] = jnp.zeros_like(czz)
            chh[...] = jnp.zeros_like(chh)
            pnext[...] = jnp.zeros_like(pnext)

        pcur[...] = pnext[...]
        xv = xa_ref[...]
        rinv = lax.rsqrt(jnp.mean(xv * xv, axis=-1, keepdims=True) + RMS_EPS)
        h0 = (xv * rinv * (g_ref[...] * (1.0 + sc_ref[...])) + sh_ref[...]).astype(BF16)
        h0_ref[...] = h0

        def project(k):
            def emit():
                pk = jnp.dot(h0, win_ref[k], preferred_element_type=F32).astype(BF16)
                p_ref[:, k * nb:(k + 1) * nb] = pk
                pnext[:, k * nb:(k + 1) * nb] = pk
            return emit

        def mixer():
            piece = lambda k: pcur[:, k * D:(k + 1) * D].astype(F32)
            xa = piece(0)
            rows = _rows(ts, D)
            taps = _conv_taps(jnp.concatenate([cxa[...], xa], axis=0), hl, ts, 4)
            xc = cb_ref[...] + sum(cw_ref[k:k + 1, :] * taps[k] for k in range(4))
            r, ig = _lru_gates(xc, wa_ref, ba_ref[...], wx_ref, bx_ref[...])
            a, m, _ = _lru_decay(r, _softplus_neg(lam_ref[...]), (rows == 0) & (i == 1))
            yield 0.26
            h = _run(_scan_fwd_steps(a, m * ig * xc, chh[hl - 1:hl, :]))
            yield 0.51
            gcp, v = piece(3), piece(4)
            z = gcp * v
            ztaps = _conv_taps(jnp.concatenate([czz[...], z], axis=0), hl, ts, 3)
            yb = piece(2) * sum(sw_ref[k:k + 1, :] * ztaps[k] for k in range(3))
            ga, gb = piece(1), piece(5)
            y = jnp.concatenate([h * (ga * _sigmoid(ga)), yb * (gb * _sigmoid(gb))], axis=1).astype(BF16)
            yield 0.76
            y_ref[...] = y
            x1_ref[...] = xb_ref[...] + gate_ref[...] * jnp.dot(y, wo_ref[...], preferred_element_type=F32)
            h_ref[...] = h.astype(BF16)
            cxa[...] = xa[ts - hl:, :]
            czz[...] = z[ts - hl:, :]
            chh[...] = jnp.where(i > 0, h[ts - hl:, :], 0.0)

        _paired(mixer(), [project(k) for k in range(N_CHIP)])

    def full(a):
        return pl.BlockSpec(a.shape, lambda i: (0,) * a.ndim)

    ahead = lambda w: pl.BlockSpec((ts, w), lambda i: (jnp.minimum(i, n_t - 1), 0))
    behind = lambda w: pl.BlockSpec((ts, w), lambda i: (jnp.maximum(i - 1, 0), 0))
    args = (x, x, g, sc, sh, w_in, gate, cw, cb, wa, ba, wx, bx, lam, sw, wo)
    return pl.pallas_call(
        body, name="l0_fwd", grid=(n_t + 1,),
        in_specs=[ahead(D), behind(D)] + [full(a) for a in args[2:]],
        out_specs=[behind(D), behind(D), behind(2 * D), ahead(D), ahead(N_CHIP * nb)],
        out_shape=[jax.ShapeDtypeStruct((s_len, D), F32), jax.ShapeDtypeStruct((s_len, D), BF16),
                   jax.ShapeDtypeStruct((s_len, 2 * D), BF16), jax.ShapeDtypeStruct((s_len, D), BF16),
                   jax.ShapeDtypeStruct((s_len, N_CHIP * nb), BF16)],
        scratch_shapes=[pltpu.VMEM((ts, N_CHIP * nb), BF16)] * 2 + [pltpu.VMEM((hl, D), F32)] * 3,
        compiler_params=_cp(("arbitrary",)),
    )(*args)


def _l1_fwd(x1, g, sc, sh, w_in, tgt, gate, wg, bg, scale, wo, gf):
    s_len, nb = x1.shape[0], w_in.shape[2]
    ts = _tile(s_len, TS_MIX)
    n_t = s_len // ts
    pw, gd, hl = 2 * D, POOL_GROUP_DIM, POOL_HALO

    def body(xa_ref, xb_ref, t_ref, g_ref, sc_ref, sh_ref, win_ref, gate_ref, wg_ref, bg_ref, scl_ref, wo_ref, gf_ref,
             d_ref, mx_ref, y_ref, dx_ref, loss_ref, dgf_ref, h1_ref, p_ref, pcur, pnext, cv):
        i = pl.program_id(0)

        @pl.when(i == 0)
        def _():
            cv[...] = jnp.zeros_like(cv)
            loss_ref[...] = jnp.zeros_like(loss_ref)
            dgf_ref[...] = jnp.zeros_like(dgf_ref)
            pnext[...] = jnp.zeros_like(pnext)

        pcur[...] = pnext[...]
        xv = xa_ref[...]
        rinv = lax.rsqrt(jnp.mean(xv * xv, axis=-1, keepdims=True) + RMS_EPS)
        h1 = (xv * rinv * (g_ref[...] * (1.0 + sc_ref[...])) + sh_ref[...]).astype(BF16)
        h1_ref[...] = h1

        def project(k):
            def emit():
                pk = jnp.dot(h1, win_ref[k], preferred_element_type=F32).astype(BF16)
                p_ref[:, k * nb:(k + 1) * nb] = pk
                pnext[:, k * nb:(k + 1) * nb] = pk
            return emit

        def mixer():
            v = pcur[:, 0:pw].astype(F32)
            sums = _window_sums(jnp.concatenate([cv[...], v], axis=0), _down)
            inv = _pool_inv_counts(jnp.maximum(i - 1, 0) * ts, ts)
            dd = [sums[k][hl:hl + ts] * inv[k] - v[:, k * gd:(k + 1) * gd] for k in range(4)]
            d_ref[...] = jnp.concatenate(dd, axis=1).astype(BF16)
            yield 0.26
            mixed = jnp.concatenate(
                [jnp.dot(dd[k].astype(BF16), wg_ref[k], preferred_element_type=F32) for k in range(4)], axis=1) + bg_ref[...]
            mx_ref[...] = mixed.astype(BF16)
            gg = pcur[:, pw:2 * pw].astype(F32)
            y = (mixed * scl_ref[...] * (gg * _sigmoid(gg))).astype(BF16)
            y_ref[...] = y
            yield 0.51
            x2 = xb_ref[...] + gate_ref[...] * jnp.dot(y, wo_ref[...], preferred_element_type=F32)
            yield 0.76
            r2 = lax.rsqrt(jnp.mean(x2 * x2, axis=-1, keepdims=True) + RMS_EPS)
            n2 = x2 * r2
            err = n2 * gf_ref[...] - t_ref[...]
            loss_ref[...] += jnp.where(i > 0, jnp.sum(err * err, axis=0, keepdims=True), 0.0)
            dyf = err * (1.0 / D)
            dgf_ref[...] += jnp.where(i > 0, jnp.sum(dyf * n2, axis=0, keepdims=True), 0.0)
            dn = dyf * gf_ref[...]
            dx_ref[...] = r2 * (dn - n2 * jnp.mean(dn * n2, axis=-1, keepdims=True))
            cv[...] = v[ts - hl:, :]

        _paired(mixer(), [project(k) for k in range(N_CHIP)])

    def full(a):
        return pl.BlockSpec(a.shape, lambda i: (0,) * a.ndim)

    ahead = lambda w: pl.BlockSpec((ts, w), lambda i: (jnp.minimum(i, n_t - 1), 0))
    behind = lambda w: pl.BlockSpec((ts, w), lambda i: (jnp.maximum(i - 1, 0), 0))
    acc = pl.BlockSpec((1, D), lambda i: (0, 0))
    args = (x1, x1, tgt, g, sc, sh, w_in, gate, wg, bg, scale, wo, gf)
    return pl.pallas_call(
        body, name="l1_fwd", grid=(n_t + 1,),
        in_specs=[ahead(D), behind(D), behind(D)] + [full(a) for a in args[3:]],
        out_specs=[behind(pw), behind(pw), behind(pw), behind(D), acc, acc, ahead(D), ahead(N_CHIP * nb)],
        out_shape=[jax.ShapeDtypeStruct((s_len, pw), BF16)] * 3 + [jax.ShapeDtypeStruct((s_len, D), F32)]
        + [jax.ShapeDtypeStruct((1, D), F32)] * 2
        + [jax.ShapeDtypeStruct((s_len, D), BF16), jax.ShapeDtypeStruct((s_len, N_CHIP * nb), BF16)],
        scratch_shapes=[pltpu.VMEM((ts, N_CHIP * nb), BF16)] * 2 + [pltpu.VMEM((hl, pw), F32)],
        compiler_params=_cp(("arbitrary",)),
    )(*args)


def _l1_bwd_mix(dx2, proj, mixed, y, dpool, gate, wg, scale, wo):
    s_len = dx2.shape[0]
    ts = _tile(s_len, TS_MIX)
    n_t = s_len // ts
    pw, gd, hl = 2 * D, POOL_GROUP_DIM, POOL_HALO

    def body(dx_ref, gg_ref, mx_ref, y_ref, d_ref, gate_ref, wg_ref, sc_ref, wo_ref,
             dp_ref, mt_ref, dwg_ref, dsc_ref, dbg_ref, cq):
        i = pl.program_id(0)

        @pl.when(i == 0)
        def _():
            cq[...] = jnp.zeros_like(cq)
            dsc_ref[...] = jnp.zeros_like(dsc_ref)
            dbg_ref[...] = jnp.zeros_like(dbg_ref)
            mt_ref[...] = jnp.zeros_like(mt_ref)
            dwg_ref[...] = jnp.zeros_like(dwg_ref)

        dxv = dx_ref[...]
        dxb = dxv.astype(BF16)

        def wgrad_out(k):
            mt_ref[k] += lax.dot_general(y_ref[:, k * gd:(k + 1) * gd], dxb, TN, preferred_element_type=F32)

        dy = lax.dot_general((gate_ref[...] * dxv).astype(BF16), wo_ref[...], NT, preferred_element_type=F32)
        wgrad_out(0)
        gg = gg_ref[...].astype(F32)
        mixed = mx_ref[...].astype(F32)
        s = _sigmoid(gg)
        sg = gg * s
        dmixed = dy * sc_ref[...] * sg
        dsc_ref[...] += jnp.sum(dy * mixed * sg, axis=0, keepdims=True)
        dbg_ref[...] += jnp.sum(dmixed, axis=0, keepdims=True)
        dmb = dmixed.astype(BF16)
        wgrad_out(1)
        dp_ref[:, pw:2 * pw] = (dy * sc_ref[...] * mixed * (s * (1.0 + gg * (1.0 - s)))).astype(BF16)
        inv = _pool_inv_counts((n_t - 1 - i) * ts, ts)
        dd = []
        for k in range(4):
            dmk = dmb[:, k * gd:(k + 1) * gd]
            dd.append(lax.dot_general(dmk, wg_ref[k], NT, preferred_element_type=F32))
            dwg_ref[k] += lax.dot_general(d_ref[:, k * gd:(k + 1) * gd], dmk, TN, preferred_element_type=F32)
        wgrad_out(2)
        q = jnp.concatenate([dd[k] * inv[k] for k in range(4)], axis=1)
        sums = _window_sums(jnp.concatenate([q, cq[...]], axis=0), _up)
        wgrad_out(3)
        dp_ref[:, 0:pw] = jnp.concatenate([sums[k][0:ts] - dd[k] for k in range(4)], axis=1).astype(BF16)
        cq[...] = q[0:hl, :]

    def full(a):
        return pl.BlockSpec(a.shape, lambda i: (0,) * a.ndim)

    rev = lambda w, j=0: pl.BlockSpec((ts, w), lambda i: (n_t - 1 - i, j))
    acc = pl.BlockSpec((1, pw), lambda i: (0, 0))
    return pl.pallas_call(
        body, name="l1_bwd_mix", grid=(n_t,),
        in_specs=[rev(D), rev(pw, 1), rev(pw), rev(pw), rev(pw)] + [full(a) for a in (gate, wg, scale, wo)],
        out_specs=[rev(2 * pw), pl.BlockSpec((N_CHIP, gd, D), lambda i: (0, 0, 0)),
                   pl.BlockSpec((4, gd, gd), lambda i: (0, 0, 0)), acc, acc],
        out_shape=[jax.ShapeDtypeStruct((s_len, 2 * pw), BF16), jax.ShapeDtypeStruct((N_CHIP, gd, D), F32),
                   jax.ShapeDtypeStruct((4, gd, gd), F32),
                   jax.ShapeDtypeStruct((1, pw), F32), jax.ShapeDtypeStruct((1, pw), F32)],
        scratch_shapes=[pltpu.VMEM((hl, pw), F32)],
        compiler_params=_cp(("arbitrary",)),
    )(dx2, proj, mixed, y, dpool, gate, wg, scale, wo)


def _l0_bwd_mix(dx1, proj, hst, y, gate, cw, cb, wa, ba, wx, bx, lam, sw, wo):
    s_len = dx1.shape[0]
    ts = _tile(s_len, TS_MIX)
    n_t = s_len // ts
    hl, hb = SUBLANES, BF16_ROWS
    yb_w = 2 * D // N_CHIP

    def body(dx_ref, p_ref, ph_ref, h_ref, hh_ref, y_ref, gate_ref, cw_ref, cb_ref, wa_ref, ba_ref, wx_ref, bx_ref,
             lam_ref, sw_ref, wo_ref, dp_ref, mt_ref, dwa_ref, dwx_ref, sm_ref, cg, cdxc, cdcz, ca):
        i = pl.program_id(0)
        ri = n_t - 1 - i

        @pl.when(i == 0)
        def _():
            cg[...] = jnp.zeros_like(cg)
            ca[...] = jnp.zeros_like(ca)
            cdxc[...] = jnp.zeros_like(cdxc)
            cdcz[...] = jnp.zeros_like(cdcz)
            sm_ref[...] = jnp.zeros_like(sm_ref)
            mt_ref[...] = jnp.zeros_like(mt_ref)
            dwa_ref[...] = jnp.zeros_like(dwa_ref)
            dwx_ref[...] = jnp.zeros_like(dwx_ref)

        dxb = dx_ref[...].astype(BF16)

        def wgrad_out(k):
            mt_ref[k] += lax.dot_general(y_ref[:, k * yb_w:(k + 1) * yb_w], dxb, TN, preferred_element_type=F32)

        wgrad_out(0)
        has_prev = (ri > 0).astype(F32)
        xa, ga, gbp, gcp, v, gb = [p_ref[:, k * D:(k + 1) * D].astype(F32) for k in range(6)]
        prev = lambda k: ph_ref[:, k * D:(k + 1) * D].astype(F32)[hb - hl:hb] * has_prev
        rows = _rows(ts, D)
        first = (rows == 0) & (ri == 0)
        xtaps = _conv_taps(jnp.concatenate([prev(0), xa], axis=0), hl, ts, 4)
        xc = cb_ref[...] + sum(cw_ref[k:k + 1, :] * xtaps[k] for k in range(4))
        r, ig = _lru_gates(xc, wa_ref, ba_ref[...], wx_ref, bx_ref[...])
        sp = _softplus_neg(lam_ref[...])
        a, m, inv_m = _lru_decay(r, sp, first)
        z = gcp * v
        ztaps = _conv_taps(jnp.concatenate([prev(3) * prev(4), z], axis=0), hl, ts, 3)
        cz = sum(sw_ref[k:k + 1, :] * ztaps[k] for k in range(3))
        h = h_ref[...].astype(F32)
        hprev = _down(jnp.concatenate([hh_ref[...].astype(F32)[hb - hl:hb] * has_prev, h], axis=0), 1)[hl:hl + ts]
        dy = lax.dot_general((gate_ref[...] * dx_ref[...]).astype(BF16), wo_ref[...], NT, preferred_element_type=F32)
        dya_pre, dyb_pre = dy[:, 0:D], dy[:, D:2 * D]
        s_a, s_b = _sigmoid(ga), _sigmoid(gb)
        dp_ref[:, D:2 * D] = (dya_pre * h * (s_a * (1.0 + ga * (1.0 - s_a)))).astype(BF16)
        dp_ref[:, 5 * D:6 * D] = (dyb_pre * (gbp * cz) * (s_b * (1.0 + gb * (1.0 - s_b)))).astype(BF16)
        dya = dya_pre * (ga * s_a)
        dyb = dyb_pre * (gb * s_b)
        wgrad_out(1)
        dp_ref[:, 2 * D:3 * D] = (dyb * cz).astype(BF16)
        dcz = dyb * gbp
        for k in range(3):
            sm_ref[8 + k:9 + k, :] += jnp.sum(dcz * ztaps[k], axis=0, keepdims=True)
        dcz_ext = jnp.concatenate([dcz, cdcz[...]], axis=0)
        dz = sum(sw_ref[k:k + 1, :] * _up(dcz_ext, 2 - k)[0:ts] for k in range(3))
        dp_ref[:, 3 * D:4 * D] = (dz * v).astype(BF16)
        dp_ref[:, 4 * D:5 * D] = (dz * gcp).astype(BF16)
        cdcz[...] = dcz[0:hl, :]
        alpha = _up(jnp.concatenate([a, ca[...]], axis=0), 1)[0:ts]
        wgrad_out(2)
        dh = _run(_scan_rev_steps(alpha, dya, cg[0:1, :]))
        wgrad_out(3)
        cg[...] = dh[0:hl, :]
        ca[...] = a[0:hl, :]
        da = dh * hprev
        dm = dh * ig * xc
        di = dh * m * xc
        dxc = dh * m * ig
        dl = da * a - jnp.where(first, 0.0, dm * (a * a) * inv_m)
        sm_ref[7:8, :] += jnp.sum(dl * r, axis=0, keepdims=True) * (-LRU_C)
        dpa = (dl * sp) * (-LRU_C) * r * (1.0 - r)
        dpx = di * ig * (1.0 - ig)
        sm_ref[5:6, :] += jnp.sum(dpa, axis=0, keepdims=True)
        sm_ref[6:7, :] += jnp.sum(dpx, axis=0, keepdims=True)
        dpa_b, dpx_b, xc_b = dpa.astype(BF16), dpx.astype(BF16), xc.astype(BF16)
        back = []
        for hd in range(LRU_HEADS):
            sl = slice(hd * LRU_HEAD_DIM, (hd + 1) * LRU_HEAD_DIM)
            back.append(lax.dot_general(dpa_b[:, sl], wa_ref[hd], NT, preferred_element_type=F32)
                        + lax.dot_general(dpx_b[:, sl], wx_ref[hd], NT, preferred_element_type=F32))
            dwa_ref[hd] += lax.dot_general(xc_b[:, sl], dpa_b[:, sl], TN, preferred_element_type=F32)
            dwx_ref[hd] += lax.dot_general(xc_b[:, sl], dpx_b[:, sl], TN, preferred_element_type=F32)
        dxc = dxc + jnp.concatenate(back, axis=1)
        sm_ref[4:5, :] += jnp.sum(dxc, axis=0, keepdims=True)
        for k in range(4):
            sm_ref[k:k + 1, :] += jnp.sum(dxc * xtaps[k], axis=0, keepdims=True)
        dxc_ext = jnp.concatenate([dxc, cdxc[...]], axis=0)
        dp_ref[:, 0:D] = sum(cw_ref[k:k + 1, :] * _up(dxc_ext, 3 - k)[0:ts] for k in range(4)).astype(BF16)
        cdxc[...] = dxc[0:hl, :]

    def full(a):
        return pl.BlockSpec(a.shape, lambda i: (0,) * a.ndim)

    rev = lambda w: pl.BlockSpec((ts, w), lambda i: (n_t - 1 - i, 0))
    halo = lambda w: pl.BlockSpec((hb, w), lambda i: (jnp.maximum((n_t - 1 - i) * (ts // hb) - 1, 0), 0))
    return pl.pallas_call(
        body, name="l0_bwd_mix", grid=(n_t,),
        in_specs=[rev(D), rev(6 * D), halo(6 * D), rev(D), halo(D), rev(2 * D)]
        + [full(a) for a in (gate, cw, cb, wa, ba, wx, bx, lam, sw, wo)],
        out_specs=[rev(6 * D), pl.BlockSpec((N_CHIP, yb_w, D), lambda i: (0, 0, 0)),
                   pl.BlockSpec(wa.shape, lambda i: (0, 0, 0)), pl.BlockSpec(wa.shape, lambda i: (0, 0, 0)),
                   pl.BlockSpec((2 * SUBLANES, D), lambda i: (0, 0))],
        out_shape=[jax.ShapeDtypeStruct((s_len, 6 * D), BF16), jax.ShapeDtypeStruct((N_CHIP, yb_w, D), F32),
                   jax.ShapeDtypeStruct(wa.shape, F32), jax.ShapeDtypeStruct(wa.shape, F32),
                   jax.ShapeDtypeStruct((2 * SUBLANES, D), F32)],
        scratch_shapes=[pltpu.VMEM((hl, D), F32)] * 4,
        compiler_params=_cp(("arbitrary",)),
    )(dx1, proj, proj, hst, hst, y, gate, cw, cb, wa, ba, wx, bx, lam, sw, wo)


def _dgrad_norm(dproj, w, x, dres, g, sc, name, after=None):
    s_len, nb = x.shape[0], w.shape[2]
    ts = _tile(s_len, TS_DGRAD)
    order = [] if after is None else [after]

    def body(dp_ref, w_ref, x_ref, dr_ref, g_ref, sc_ref, *rest):
        dx_ref, s1_ref, s2_ref = rest[len(order):]

        @pl.when(pl.program_id(0) == 0)
        def _():
            s1_ref[...] = jnp.zeros_like(s1_ref)
            s2_ref[...] = jnp.zeros_like(s2_ref)

        dh = sum(lax.dot_general(dp_ref[:, k * nb:(k + 1) * nb], w_ref[k], NT, preferred_element_type=F32)
                 for k in range(N_CHIP))
        xv = x_ref[...]
        r = lax.rsqrt(jnp.mean(xv * xv, axis=-1, keepdims=True) + RMS_EPS)
        n = xv * r
        s1_ref[...] += jnp.sum(dh, axis=0, keepdims=True)
        s2_ref[...] += jnp.sum(dh * n, axis=0, keepdims=True)
        dn = dh * (g_ref[...] * (1.0 + sc_ref[...]))
        dx_ref[...] = dr_ref[...] + r * (dn - n * jnp.mean(dn * n, axis=-1, keepdims=True))

    row = lambda wd: pl.BlockSpec((ts, wd), lambda i: (i, 0))
    vec = pl.BlockSpec((1, D), lambda i: (0, 0))
    return pl.pallas_call(
        body, name=name, grid=(s_len // ts,),
        in_specs=[row(N_CHIP * nb), pl.BlockSpec(w.shape, lambda i: (0, 0, 0)), row(D), row(D), vec, vec]
        + [ANY] * len(order),
        out_specs=[row(D), vec, vec],
        out_shape=[jax.ShapeDtypeStruct((s_len, D), F32)] + [jax.ShapeDtypeStruct((1, D), F32)] * 2,
        compiler_params=_cp(("arbitrary",)),
    )(dproj, w, x, dres, g, sc, *order)


def _wgrad(a, b, groups, ka, nb, a_col, b_col, name, after=None):
    s_len = a.shape[0]
    ts = _tile(s_len, TS_WGRAD)
    n_s = s_len // ts
    order = [] if after is None else [after]

    def body(a_ref, b_ref, *rest):
        o_ref, wire_ref = rest[-2:]

        @pl.when(pl.program_id(1) == 0)
        def _():
            o_ref[...] = jnp.zeros_like(o_ref)

        o_ref[...] += lax.dot_general(a_ref[...].astype(BF16), b_ref[...].astype(BF16), TN, preferred_element_type=F32)

        @pl.when(pl.program_id(1) == n_s - 1)
        def _():
            wire_ref[...] = o_ref[...].astype(GRAD_WIRE_DTYPE)

    blk = pl.BlockSpec((None, ka, nb), lambda g, s: (g, 0, 0))
    return pl.pallas_call(
        body, name=name, grid=(groups, n_s),
        in_specs=[pl.BlockSpec((ts, ka), lambda g, s: (s, a_col(g))), pl.BlockSpec((ts, nb), lambda g, s: (s, b_col(g)))]
        + [ANY] * len(order),
        out_specs=[blk, blk],
        out_shape=[jax.ShapeDtypeStruct((groups, ka, nb), F32), jax.ShapeDtypeStruct((groups, ka, nb), GRAD_WIRE_DTYPE)],
        compiler_params=_cp(("parallel", "arbitrary")),
    )(a, b, *order)


def _wo_final(mt, wo, gate, name):
    rb = mt.shape[1]

    def body(m_ref, w_ref, gate_ref, dw_ref, wire_ref, dg_ref):
        @pl.when(pl.program_id(0) == 0)
        def _():
            dg_ref[...] = jnp.zeros_like(dg_ref)

        mv = m_ref[...]
        dw = mv * gate_ref[...]
        dw_ref[...] = dw
        wire_ref[...] = dw.astype(GRAD_WIRE_DTYPE)
        dg_ref[...] += jnp.sum(mv * w_ref[...].astype(F32), axis=0, keepdims=True)

    blk = pl.BlockSpec((None, rb, D), lambda k: (k, 0, 0))
    vec = pl.BlockSpec((1, D), lambda k: (0, 0))
    return pl.pallas_call(
        body, name=name, grid=(N_CHIP,), in_specs=[blk, blk, vec], out_specs=[blk, blk, vec],
        out_shape=[jax.ShapeDtypeStruct(mt.shape, F32), jax.ShapeDtypeStruct(mt.shape, GRAD_WIRE_DTYPE),
                   jax.ShapeDtypeStruct((1, D), F32)],
        compiler_params=_cp(("arbitrary",)),
    )(mt, wo, gate)


ROW_NORM_G, ROW_CONV_W, ROW_CONV_B, ROW_B_A, ROW_B_X, ROW_LAMBDA, ROW_SC_W, ROW_POOL_B, ROW_POOL_S, ROW_FINAL_G = (
    0, 2, 6, 7, 8, 9, 10, 13, 15, 17)
ROW_LOSS = 18


def _small_pack(s1_0, s2_0, s1_1, s2_1, sm0, dsc1, dbg1, dgf, losscols, dgate0, dgate1, norm_g, sc0, sc1, lam):
    def body(s1_0r, s2_0r, s1_1r, s2_1r, sm, dsc, dbg, dgfr, lcols, dg0, dg1, ng, sc0r, sc1r, lamr, buf, dmod):
        buf[...] = jnp.zeros_like(buf)
        buf[0:1, :] = s2_0r[...] * (1.0 + sc0r[...])
        buf[1:2, :] = s2_1r[...] * (1.0 + sc1r[...])
        buf[ROW_CONV_W:ROW_CONV_W + 4, :] = sm[0:4, :]
        buf[ROW_CONV_B:ROW_CONV_B + 1, :] = sm[4:5, :]
        buf[ROW_B_A:ROW_B_A + 1, :] = sm[5:6, :]
        buf[ROW_B_X:ROW_B_X + 1, :] = sm[6:7, :]
        buf[ROW_LAMBDA:ROW_LAMBDA + 1, :] = -sm[7:8, :] * _sigmoid(-lamr[...])
        buf[ROW_SC_W:ROW_SC_W + 3, :] = sm[8:11, :]
        for k in range(2):
            buf[ROW_POOL_B + k:ROW_POOL_B + k + 1, :] = dbg[:, k * D:(k + 1) * D]
            buf[ROW_POOL_S + k:ROW_POOL_S + k + 1, :] = dsc[:, k * D:(k + 1) * D]
        buf[ROW_FINAL_G:ROW_FINAL_G + 1, :] = dgfr[...]
        pieces = (s1_0r[...], s2_0r[...] * ng[0:1, :], dg0[...], s1_1r[...], s2_1r[...] * ng[1:2, :], dg1[...])
        for k, pc in enumerate(pieces):
            dmod[:, k * D:(k + 1) * D] = jnp.broadcast_to(pc, (SUBLANES, D))
        buf[ROW_LOSS:ROW_LOSS + 1, :] = jnp.broadcast_to(jnp.sum(lcols[...], axis=1, keepdims=True) * (0.5 / D), (1, D))

    args = (s1_0, s2_0, s1_1, s2_1, sm0, dsc1, dbg1, dgf, losscols, dgate0, dgate1, norm_g, sc0, sc1, lam)
    return pl.pallas_call(
        body, name="small_pack", in_specs=[VMEM] * len(args), out_specs=[VMEM] * 2,
        out_shape=[jax.ShapeDtypeStruct((SMALL_ROWS, D), F32), jax.ShapeDtypeStruct((SUBLANES, 6 * D), F32)],
        compiler_params=_cp(),
    )(*args)


def _small_comm(buf_a, buf_b, dmod8):
    ra, rb = buf_a.shape[0] // N_DEV, buf_b.shape[0] // N_DEV
    wb = buf_b.shape[1]

    def body(a_ref, b_ref, dm_ref, oa_ref, ob_ref, odm_ref, ina, inb, dslot, sa, sb, s1, r1, s2, r2):
        x, y, c = _pos()
        me = 4 * x + 2 * y + c
        peers = []
        for r in range(1, N_DEV):
            fx, fy, fc = (r >> 2) & 1, (r >> 1) & 1, r & 1
            px, py, pc = _flip(x, fx), _flip(y, fy), _flip(c, fc)
            peers.append(((px, py, pc), 4 * px + 2 * py + pc))
        seg_a = lambda d: pl.ds(pl.multiple_of(d * ra, SUBLANES), ra)
        seg_b = lambda d: pl.ds(pl.multiple_of(d * rb, SUBLANES), rb)
        first = []
        for r, (peer, pid) in enumerate(peers):
            for k, (src, dst) in enumerate(((a_ref.at[seg_a(pid), :], ina.at[r]), (b_ref.at[seg_b(pid), :], inb.at[r]),
                                            (dm_ref, dslot.at[me]))):
                cp = pltpu.make_async_remote_copy(src_ref=src, dst_ref=dst, send_sem=s1.at[3 * r + k],
                                                  recv_sem=r1.at[3 * r + k], device_id=peer, device_id_type=MESH)
                cp.start()
                first.append(cp)
        dslot[me] = dm_ref[...]
        for cp in first:
            cp.wait()
        acc_a, acc_b = a_ref[seg_a(me), :], b_ref[seg_b(me), :]
        for r in range(N_DEV - 1):
            acc_a = acc_a + ina[r]
            acc_b = acc_b + inb[r]
        sa[...] = acc_a
        sb[...] = acc_b
        oa_ref[seg_a(me), :] = acc_a
        ob_ref[seg_b(me), :] = acc_b
        second = []
        for r, (peer, pid) in enumerate(peers):
            for k, (src, dst) in enumerate(((sa, oa_ref.at[seg_a(me), :]), (sb, ob_ref.at[seg_b(me), :]))):
                cp = pltpu.make_async_remote_copy(src_ref=src, dst_ref=dst, send_sem=s2.at[2 * r + k],
                                                  recv_sem=r2.at[2 * r + k], device_id=peer, device_id_type=MESH)
                cp.start()
                second.append(cp)
        rows = _rows(SUBLANES, dm_ref.shape[1])
        dm_all = jnp.zeros(dm_ref.shape, F32)
        for d in range(N_DEV):
            dm_all = jnp.where(rows == d, dslot[d], dm_all)
        odm_ref[...] = dm_all
        for cp in second:
            cp.wait()

    nrel = N_DEV - 1
    return pl.pallas_call(
        body, name="small_comm", in_specs=[VMEM] * 3, out_specs=[VMEM] * 3,
        out_shape=[jax.ShapeDtypeStruct(buf_a.shape, F32), jax.ShapeDtypeStruct(buf_b.shape, F32),
                   jax.ShapeDtypeStruct(dmod8.shape, F32)],
        scratch_shapes=[pltpu.VMEM((nrel, ra, D), F32), pltpu.VMEM((nrel, rb, wb), F32),
                        pltpu.VMEM((N_DEV,) + dmod8.shape, F32), pltpu.VMEM((ra, D), F32), pltpu.VMEM((rb, wb), F32),
                        pltpu.SemaphoreType.DMA((3 * nrel,)), pltpu.SemaphoreType.DMA((3 * nrel,)),
                        pltpu.SemaphoreType.DMA((2 * nrel,)), pltpu.SemaphoreType.DMA((2 * nrel,))],
        compiler_params=_cp(),
    )(buf_a, buf_b, dmod8)


def _adam(w, g, m, v):
    m2 = ADAM_B1 * m + (1.0 - ADAM_B1) * g
    v2 = ADAM_B2 * v + (1.0 - ADAM_B2) * (g * g)
    m_hat = m2 / (1.0 - ADAM_B1 ** ADAM_STEP)
    v_hat = v2 / (1.0 - ADAM_B2 ** ADAM_STEP)
    return -ADAM_LR * (m_hat / (jnp.sqrt(v_hat) + ADAM_EPS) + ADAM_WD * w), m2, v2


def _small_adam(red_a, red_b, dm_all, params):
    n = len(params)

    def body(*refs):
        ra, rb, dm = refs[:3]
        wmv = refs[3:3 + 3 * n]
        outs = refs[3 + 3 * n:]
        x, y, _ = _pos()
        chip = 2 * x + y

        def shard(row0, nrows, width):
            per_row = D // width
            cands = []
            for k in range(N_CHIP):
                if nrows == 1 or per_row >= N_CHIP:
                    cands.append(ra[row0:row0 + nrows, k * width:(k + 1) * width])
                else:
                    rr, cc = divmod(k * width, D)
                    cands.append(ra[row0 + rr:row0 + rr + 1, cc:cc + width])
            g = cands[0]
            for k in range(1, N_CHIP):
                g = jnp.where(chip == k, cands[k], g)
            return g

        dms = jnp.sum(dm[...], axis=0, keepdims=True)
        hw = LRU_HEADS * LRU_HEAD_DIM
        grads = [
            ra[ROW_NORM_G:ROW_NORM_G + 2, :],
            None,
            shard(ROW_CONV_W, 4, D // N_CHIP),
            ra[ROW_CONV_B:ROW_CONV_B + 1, :],
            rb[0:hw, :],
            ra[ROW_B_A:ROW_B_A + 1, :],
            rb[hw:2 * hw, :],
            ra[ROW_B_X:ROW_B_X + 1, :],
            ra[ROW_LAMBDA:ROW_LAMBDA + 1, :],
            shard(ROW_SC_W, 3, D // N_CHIP),
            shard(ROW_POOL_B, 2, 2 * D // N_CHIP),
            shard(ROW_POOL_S, 2, 2 * D // N_CHIP),
            ra[ROW_FINAL_G:ROW_FINAL_G + 1, :],
        ]
        for p in range(n):
            w_ref, m_ref, v_ref = wmv[3 * p:3 * p + 3]
            g_out, d_out, m_out, v_out = outs[4 * p:4 * p + 4]
            if grads[p] is None:
                for l in range(2):
                    g = dms[:, l * 3 * D:(l + 1) * 3 * D]
                    dl, m2, v2 = _adam(w_ref[l:l + 1, :], g, m_ref[l:l + 1, :], v_ref[l:l + 1, :])
                    g_out[l:l + 1, :] = g
                    d_out[l:l + 1, :] = dl
                    m_out[l:l + 1, :] = m2
                    v_out[l:l + 1, :] = v2
            else:
                g = grads[p]
                dl, m2, v2 = _adam(w_ref[...], g, m_ref[...], v_ref[...])
                g_out[...] = g
                d_out[...] = dl
                m_out[...] = m2
                v_out[...] = v2

    flat = [a for p in params for a in p]
    return pl.pallas_call(
        body, name="small_adam", in_specs=[VMEM] * (3 + len(flat)), out_specs=[VMEM] * (4 * n),
        out_shape=[jax.ShapeDtypeStruct(p[0].shape, F32) for p in params for _ in range(4)],
        compiler_params=_cp(),
    )(red_a, red_b, dm_all, *flat)


def _modw_adam(ca_t, dm_sh, w, m, v):
    nw = w.shape[2]

    def body(c_ref, d_ref, w_ref, m_ref, v_ref, g_out, d_out, m_out, v_out):
        g = jnp.dot(c_ref[...], d_ref[...], precision=lax.Precision.HIGHEST, preferred_element_type=F32)
        dl, m2, v2 = _adam(w_ref[...], g, m_ref[...], v_ref[...])
        g_out[...] = g
        d_out[...] = dl
        m_out[...] = m2
        v_out[...] = v2

    blk = pl.BlockSpec((None, D, nw), lambda l: (l, 0, 0))
    return pl.pallas_call(
        body, name="modw_adam", grid=(2,),
        in_specs=[pl.BlockSpec((D, SUBLANES), lambda l: (0, 0)), pl.BlockSpec((None, SUBLANES, nw), lambda l: (l, 0, 0)),
                  blk, blk, blk],
        out_specs=[blk] * 4, out_shape=[jax.ShapeDtypeStruct(w.shape, F32)] * 4,
        compiler_params=_cp(("arbitrary",)),
    )(ca_t, dm_sh, w, m, v)


def _exchange(copies, name, out_type, n_sems, args, sequencer, after=None):
    order = [] if after is None else [after]
    n_in, n_out = len(args) + len(order), len(out_type)

    def body(*refs):
        barrier = pltpu.get_barrier_semaphore()
        peers = sequencer[1](*_pos())
        for peer in peers:
            pl.semaphore_signal(barrier, inc=1, device_id=peer, device_id_type=MESH)
        pl.semaphore_wait(barrier, len(peers))
        copies(refs[:n_in], refs[n_in:n_in + n_out], refs[n_in + n_out], refs[n_in + n_out + 1])

    sems = [pltpu.SemaphoreType.DMA((n_sems,))] * 2
    return pl.kernel(body, out_type, mesh=plsc.ScalarSubcoreMesh(axis_name="sequencer", num_cores=1), name=name,
                     scratch_types=sems, compiler_params=pltpu.CompilerParams(collective_id=sequencer[0]))(*args, *order)


def _sibling(x, y, c):
    return [(x, y, 1 - c)]


def _other_chips(x, y, c):
    return [(1 - x, y, c), (x, 1 - y, c), (1 - x, 1 - y, c)]


def _to_wire(g, name, after=None):
    _, rr, cc = g.shape
    rb = min(rr, 256)

    def body(g_ref, *rest):
        rest[-1][...] = g_ref[...].astype(GRAD_WIRE_DTYPE)

    order = [] if after is None else [after]
    blk = pl.BlockSpec((None, rb, cc), lambda k, j: (k, j, 0))
    return pl.pallas_call(
        body, name=name, grid=(N_CHIP, rr // rb), in_specs=[blk] + [ANY] * len(order), out_specs=blk,
        out_shape=jax.ShapeDtypeStruct(g.shape, GRAD_WIRE_DTYPE), compiler_params=_cp(("parallel", "parallel")),
    )(g, *order)


def _chip_scatter(ps, name, collective_id, after=None):
    n = len(ps)

    def copies(ins, outs, ssem, rsem):
        x, y, c = _pos()
        cps = []
        for a in range(n):
            for q, (fx, fy) in enumerate(((1, 0), (0, 1), (1, 1))):
                px, py = _flip(x, fx), _flip(y, fy)
                cp = pltpu.make_async_remote_copy(
                    src_ref=ins[a].at[2 * px + py], dst_ref=outs[a].at[q],
                    send_sem=ssem.at[3 * a + q], recv_sem=rsem.at[3 * a + q], device_id=(px, py, c), device_id_type=MESH)
                cp.start()
                cps.append(cp)
        for cp in cps:
            cp.wait()

    out_type = [jax.ShapeDtypeStruct((N_CHIP - 1,) + p.shape[1:], p.dtype) for p in ps]
    return _exchange(copies, name, out_type, 3 * n, ps, (collective_id, _other_chips), after)


def _add_owner(p, got, chipidx, name, after=None):
    _, hr, cc = p.shape
    rb = min(hr, 256)

    def body(k_ref, p_ref, r_ref, *rest):
        rest[-1][...] = ((p_ref[...].astype(F32) + r_ref[0].astype(F32)) + r_ref[1].astype(F32)) + r_ref[2].astype(F32)

    order = [] if after is None else [after]
    return pl.pallas_call(
        body, name=name,
        grid_spec=pltpu.PrefetchScalarGridSpec(
            num_scalar_prefetch=1, grid=(hr // rb,),
            in_specs=[pl.BlockSpec((None, rb, cc), lambda j, k_ref: (k_ref[0], j, 0)),
                      pl.BlockSpec((N_CHIP - 1, rb, cc), lambda j, k_ref: (0, j, 0))] + [ANY] * len(order),
            out_specs=pl.BlockSpec((rb, cc), lambda j, k_ref: (j, 0))),
        out_shape=jax.ShapeDtypeStruct((hr, cc), F32),
        compiler_params=_cp(("parallel",)),
    )(chipidx, p, got, *order)


def _sib_exchange(ts_, name, collective_id, after=None):
    n = len(ts_)

    def copies(ins, outs, ssem, rsem):
        x, y, c = _pos()
        cps = []
        for a in range(n):
            cp = pltpu.make_async_remote_copy(src_ref=ins[a], dst_ref=outs[a], send_sem=ssem.at[a],
                                              recv_sem=rsem.at[a], device_id=(x, y, 1 - c), device_id_type=MESH)
            cp.start()
            cps.append(cp)
        for cp in cps:
            cp.wait()

    out_type = [jax.ShapeDtypeStruct(t.shape, F32) for t in ts_]
    return _exchange(copies, name, out_type, n, ts_, (collective_id, _sibling), after)


def _adam_2d(w, g_own, g_sib, m, v, name):
    rr, cc = w.shape
    rb = min(rr, 256)

    def body(w_ref, go_ref, gs_ref, m_ref, v_ref, g_out, d_out, m_out, v_out):
        g = go_ref[...] + gs_ref[...]
        dl, m2, v2 = _adam(w_ref[...], g, m_ref[...], v_ref[...])
        g_out[...] = g
        d_out[...] = dl
        m_out[...] = m2
        v_out[...] = v2

    blk = pl.BlockSpec((rb, cc), lambda j: (j, 0))
    return pl.pallas_call(
        body, name=name, grid=(rr // rb,), in_specs=[blk] * 5, out_specs=[blk] * 4,
        out_shape=[jax.ShapeDtypeStruct((rr, cc), F32)] * 4, compiler_params=_cp(("parallel",)),
    )(w, g_own, g_sib, m, v)


def kernel(x, c, norm_g, mod_w, mod_b, hy_w_in, hy_conv_w, hy_conv_b, lru_w_a, lru_b_a, lru_w_x, lru_b_x, lru_lambda, sc_conv_w, hy_w_out, pool_w_in, pool_w_grp, pool_b_grp, pool_scale, pool_w_out, final_g, loss_target, m_norm_g, m_mod_w, m_mod_b, m_hy_w_in, m_hy_conv_w, m_hy_conv_b, m_lru_w_a, m_lru_b_a, m_lru_w_x, m_lru_b_x, m_lru_lambda, m_sc_conv_w, m_hy_w_out, m_pool_w_in, m_pool_w_grp, m_pool_b_grp, m_pool_scale, m_pool_w_out, m_final_g, v_norm_g, v_mod_w, v_mod_b, v_hy_w_in, v_hy_conv_w, v_hy_conv_b, v_lru_w_a, v_lru_b_a, v_lru_w_x, v_lru_b_x, v_lru_lambda, v_sc_conv_w, v_hy_w_out, v_pool_w_in, v_pool_w_grp, v_pool_b_grp, v_pool_scale, v_pool_w_out, v_final_g):
    ax, ay, ac = _pos()
    me = 4 * ax + 2 * ay + ac
    chip = 2 * ax + ay
    xs = x[0]
    tgt = loss_target[0]
    gd = POOL_GROUP_DIM
    kidx = chip.reshape(1).astype(jnp.int32)

    big = [hy_w_in[0], hy_w_out[0], pool_w_in[0], pool_w_grp[0].reshape(4 * 128, gd), pool_w_out[0]]
    w_in0, w_out0 = _wgather_sequencer(
        [_wcast_own_block(w, kidx, f"wcast_own_block_{a}") for a, w in enumerate(big[:2])], "wgather_l0", CIDS_WGATHER[0])

    ca_all, mod_all, small_w = _mod_fwd(jnp.broadcast_to(c, (SUBLANES, D)), mod_w, mod_b,
                                        hy_conv_w[0], sc_conv_w[0], pool_b_grp, pool_scale)
    mod_me = lax.dynamic_index_in_dim(mod_all, me, axis=1, keepdims=False)
    sh0, sc0, gt0 = (mod_me[0:1, k * D:(k + 1) * D] for k in range(3))
    sh1, sc1, gt1 = (mod_me[1:2, k * D:(k + 1) * D] for k in range(3))
    cw = small_w[SW_CONV:SW_CONV + 4, 0:D]
    sw = small_w[SW_SC:SW_SC + 3, 0:D]
    pool_b = small_w[SW_POOL_B:SW_POOL_B + 1, :]
    pool_s = small_w[SW_POOL_S:SW_POOL_S + 1, :]
    g0, g1, gf = norm_g[0:1], norm_g[1:2], final_g.reshape(1, D)
    cb, ba, bx, lam = hy_conv_b, lru_b_a, lru_b_x, lru_lambda

    w_in1, w_grp, w_out1 = _wgather_sequencer(
        [_wcast_own_block(w, kidx, f"wcast_own_block_{a + 2}", after=(w_out0, small_w)) for a, w in enumerate(big[2:])],
        "wgather_l1", CIDS_WGATHER[1])
    w_grp =w_grp.reshape(N_CHIP, 4, 128, gd).transpose(1, 0, 2, 3).reshape(4, gd, gd)
    wa_b, wx_b = _wcast([lru_w_a[0], lru_w_x[0]])

    x1, hst, y0, h0, proj0 = _l0_fwd(xs, g0, sc0, sh0, w_in0, gt0, cw, cb, wa_b, ba, wx_b, bx, lam, sw,
                                     w_out0.reshape(2 * D, D))
    dpool, mixed, y1, dx2, losscols, dgf, h1, proj1 = _l1_fwd(x1, g1, sc1, sh1, w_in1, tgt, gt1, w_grp, pool_b, pool_s,
                                                              w_out1.reshape(2 * D, D), gf)

    def add_owners(grads, got, tag, ids, after):
        own = []
        for a, (g, r) in enumerate(zip(grads, got)):
            own.append(_add_owner(g, r, kidx, f"grad_add_owner_{tag}{a}", own[-1] if own else after))
        return own, _sib_exchange(own, f"grad_sib_exchange_{tag}", ids[1])

    dproj1, mt1, d_wgrp, dsc1, dbg1 = _l1_bwd_mix(dx2, proj1, mixed, y1, dpool, gt1, w_grp, pool_s,
                                                  w_out1.reshape(2 * D, D))
    d_win1, wire_win1 = _wgrad(h1, dproj1, N_CHIP, D, D, lambda g: 0, lambda g: g, "l1_wgrad_in")
    d_wout1, wire_wout1, dgate1 = _wo_final(mt1, w_out1, gt1, "l1_wo_final")
    d_wgrp = d_wgrp.reshape(4, N_CHIP, 128, gd).transpose(1, 0, 2, 3).reshape(N_CHIP, 4 * 128, gd)
    grads_l1 = [d_win1, d_wgrp, d_wout1]
    got_l1 = _chip_scatter([wire_win1, _to_wire(d_wgrp, "grad_to_wire_grp"), wire_wout1], "grad_chip_scatter_l1",
                           CIDS_L1[0])
    dx1, s1_1, s2_1 = _dgrad_norm(dproj1, w_in1, x1, dx2, g1, sc1, "l1_bwd_proj")

    dproj0, mt0, d_wa, d_wx, sm0 = _l0_bwd_mix(dx1, proj0, hst, y0, gt0, cw, cb, wa_b, ba, wx_b, bx, lam, sw,
                                               w_out0.reshape(2 * D, D))
    sums_l1, sib_l1 = add_owners(grads_l1, got_l1, "l1", CIDS_L1, after=sm0)
    d_win0, wire_win0 = _wgrad(h0, dproj0, N_CHIP, D, 6 * D // N_CHIP, lambda g: 0, lambda g: g, "l0_wgrad_in",
                               after=sums_l1[-1])
    d_wout0, wire_wout0, dgate0 = _wo_final(mt0, w_out0, gt0, "l0_wo_final")
    grads_l0 = [d_win0, d_wout0]
    got_l0 = _chip_scatter([wire_win0, wire_wout0], "grad_chip_scatter_l0", CIDS_L0[0], after=sib_l1[0])
    grad_x, s1_0, s2_0 = _dgrad_norm(dproj0, w_in0, xs, dx1, g0, sc0, "l0_bwd_proj", after=wire_win0)
    sums_l0, sib_l0 = add_owners(grads_l0, got_l0, "l0", CIDS_L0, after=s1_0)

    buf_a, dmod8 = _small_pack(s1_0, s2_0, s1_1, s2_1, sm0, dsc1, dbg1, dgf, losscols, dgate0, dgate1,
                                      norm_g, sc0, sc1, lam)
    hw = LRU_HEADS * LRU_HEAD_DIM
    buf_b = jnp.concatenate([d_wa.reshape(hw, LRU_HEAD_DIM), d_wx.reshape(hw, LRU_HEAD_DIM)], axis=0)
    red_a, red_b, dm_all = _small_comm(buf_a, buf_b, dmod8)
    small = [(norm_g, m_norm_g, v_norm_g), (mod_b, m_mod_b, v_mod_b),
             (hy_conv_w[0], m_hy_conv_w[0], v_hy_conv_w[0]), (hy_conv_b, m_hy_conv_b, v_hy_conv_b),
             tuple(a.reshape(hw, LRU_HEAD_DIM) for a in (lru_w_a, m_lru_w_a, v_lru_w_a)),
             (lru_b_a, m_lru_b_a, v_lru_b_a),
             tuple(a.reshape(hw, LRU_HEAD_DIM) for a in (lru_w_x, m_lru_w_x, v_lru_w_x)),
             (lru_b_x, m_lru_b_x, v_lru_b_x), (lru_lambda, m_lru_lambda, v_lru_lambda),
             (sc_conv_w[0], m_sc_conv_w[0], v_sc_conv_w[0]), (pool_b_grp, m_pool_b_grp, v_pool_b_grp),
             (pool_scale, m_pool_scale, v_pool_scale),
             tuple(a.reshape(1, D) for a in (final_g, m_final_g, v_final_g))]
    small_names = ["norm_g", "mod_b", "hy_conv_w", "hy_conv_b", "lru_w_a", "lru_b_a", "lru_w_x", "lru_b_x",
                   "lru_lambda", "sc_conv_w", "pool_b_grp", "pool_scale", "final_g"]
    small_out = _small_adam(red_a, red_b, dm_all, small)
    res = {}
    shapes = dict(norm_g=norm_g, mod_b=mod_b, hy_conv_w=hy_conv_w, hy_conv_b=hy_conv_b, lru_w_a=lru_w_a, lru_b_a=lru_b_a,
                  lru_w_x=lru_w_x, lru_b_x=lru_b_x, lru_lambda=lru_lambda, sc_conv_w=sc_conv_w, pool_b_grp=pool_b_grp,
                  pool_scale=pool_scale, final_g=final_g)
    for p, nm in enumerate(small_names):
        res[nm] = tuple(o.reshape(shapes[nm].shape) for o in small_out[4 * p:4 * p + 4])

    nw = mod_w.shape[2]
    dm_sh = jnp.stack([lax.dynamic_slice_in_dim(dm_all[:, l * 3 * D:(l + 1) * 3 * D], chip * nw, nw, axis=1)
                       for l in range(2)])
    res["mod_w"] = tuple(_modw_adam(ca_all.T, dm_sh, mod_w, m_mod_w, v_mod_w))

    sums = list(sums_l0) + list(sums_l1)
    sib_sums = list(sib_l0) + list(sib_l1)
    big_names = ["hy_w_in", "hy_w_out", "pool_w_in", "pool_w_grp", "pool_w_out"]
    big_wmv = [(hy_w_in, m_hy_w_in, v_hy_w_in), (hy_w_out, m_hy_w_out, v_hy_w_out), (pool_w_in, m_pool_w_in, v_pool_w_in),
               (pool_w_grp, m_pool_w_grp, v_pool_w_grp), (pool_w_out, m_pool_w_out, v_pool_w_out)]
    for a, nm in enumerate(big_names):
        rr, cc = big[a].shape
        w, m, v = (t.reshape(rr, cc) for t in big_wmv[a])
        outs = _adam_2d(w, sums[a], sib_sums[a], m, v, f"adam_{nm}")
        res[nm] = tuple(o.reshape(big_wmv[a][0].shape) for o in outs)

    loss = red_a[ROW_LOSS, 0]
    order = ["norm_g", "mod_w", "mod_b", "hy_w_in", "hy_conv_w", "hy_conv_b", "lru_w_a", "lru_b_a", "lru_w_x", "lru_b_x",
             "lru_lambda", "sc_conv_w", "hy_w_out", "pool_w_in", "pool_w_grp", "pool_b_grp", "pool_scale", "pool_w_out",
             "final_g"]
    return (loss, grad_x[None], *[res[nm][0] for nm in order], *[res[nm][1] for nm in order],
            *[res[nm][2] for nm in order], *[res[nm][3] for nm in order])
```

```python
import jax
import jax.numpy as jnp
from jax import lax
from jax.experimental import pallas as pl
from jax.experimental.pallas import tpu as pltpu
from jax.experimental.pallas import tpu_sc as plsc

F32, BF16 = jnp.float32, jnp.bfloat16
D = 1024
RMS_EPS = 1e-6
SQRT_FLOOR = 1e-30
LRU_C = 8.0
LRU_HEADS, LRU_HEAD_DIM = 8, 128
POOL_WINDOWS = (2, 4, 8, 16)
POOL_GROUP_DIM = 512
ADAM_LR, ADAM_B1, ADAM_B2, ADAM_EPS, ADAM_WD, ADAM_STEP = 0.001, 0.9, 0.999, 1e-08, 0.01, 10
MESH = pl.DeviceIdType.MESH
CIDS_WGATHER = (1, 8)
CIDS_L1 = (2, 3)
CIDS_L0 = (4, 5)
N_DEV, N_CHIP = 8, 4
SUBLANES = 8
BF16_ROWS = 16
POOL_HALO = 16
TS_MIX, TS_WGRAD, TS_DGRAD = 256, 1024, 512
SMALL_ROWS = 64
GRAD_WIRE_DTYPE = BF16
ANY = pl.BlockSpec(memory_space=pl.ANY)
VMEM = pl.BlockSpec(memory_space=pltpu.VMEM)
NT = (((1,), (1,)), ((), ()))
TN = (((0,), (0,)), ((), ()))


def _cp(sem=None, vmem_mb=56):
    kw = dict(vmem_limit_bytes=vmem_mb * 2 ** 20)
    if sem is not None:
        kw["dimension_semantics"] = sem
    return pltpu.CompilerParams(**kw)


def _tile(n, t):
    return min(n, t)


def _pos():
    return lax.axis_index("x"), lax.axis_index("y"), lax.axis_index("c")


def _flip(v, f):
    return 1 - v if f else v


def _sigmoid(z):
    return 0.5 * jnp.tanh(0.5 * z) + 0.5


def _rows(n, c):
    return lax.broadcasted_iota(jnp.int32, (n, c), 0)


def _down(a, d):
    return a if d == 0 else pltpu.roll(a, d, 0)


def _up(a, d):
    return a if d == 0 else pltpu.roll(a, a.shape[0] - d, 0)


def _scan_fwd_steps(a, u, carry):
    n, c = a.shape
    sub = _rows(SUBLANES, c)
    out = []
    for k in range(n // SUBLANES):
        p = a[k * SUBLANES:(k + 1) * SUBLANES]
        g = u[k * SUBLANES:(k + 1) * SUBLANES]
        for d in (1, 2, 4):
            keep = sub >= d
            g = g + p * jnp.where(keep, pltpu.roll(g, d, 0), 0.0)
            p = p * jnp.where(keep, pltpu.roll(p, d, 0), 1.0)
        h = g + p * carry
        carry = h[SUBLANES - 1:SUBLANES, :]
        out.append(h)
        yield
    return jnp.concatenate(out, axis=0)


def _scan_rev_steps(alpha, b, carry):
    n, c = alpha.shape
    sub = _rows(SUBLANES, c)
    out = []
    for k in reversed(range(n // SUBLANES)):
        p = alpha[k * SUBLANES:(k + 1) * SUBLANES]
        g = b[k * SUBLANES:(k + 1) * SUBLANES]
        for d in (1, 2, 4):
            keep = sub < SUBLANES - d
            g = g + p * jnp.where(keep, pltpu.roll(g, SUBLANES - d, 0), 0.0)
            p = p * jnp.where(keep, pltpu.roll(p, SUBLANES - d, 0), 1.0)
        h = g + p * carry
        carry = h[0:1, :]
        out.append(h)
        yield
    return jnp.concatenate(out[::-1], axis=0)


def _run(steps):
    while True:
        try:
            next(steps)
        except StopIteration as done:
            return done.value


def _paired(progress, pieces):
    n, done = len(pieces), 1
    pieces[0]()
    for frac in progress:
        while done < n and done <= frac * n:
            pieces[done]()
            done += 1
    while done < n:
        pieces[done]()
        done += 1


def _conv_taps(ext, halo, n, width):
    return [_down(ext, width - 1 - k)[halo:halo + n] for k in range(width)]


def _lru_gates(xc, wa_ref, ba, wx_ref, bx):
    xb = xc.astype(BF16)
    pa, px = [], []
    for h in range(LRU_HEADS):
        xh = xb[:, h * LRU_HEAD_DIM:(h + 1) * LRU_HEAD_DIM]
        pa.append(jnp.dot(xh, wa_ref[h], preferred_element_type=F32))
        px.append(jnp.dot(xh, wx_ref[h], preferred_element_type=F32))
    r = _sigmoid(jnp.concatenate(pa, axis=1) + ba)
    ig = _sigmoid(jnp.concatenate(px, axis=1) + bx)
    return r, ig


def _softplus_neg(lam):
    return jnp.maximum(-lam, 0.0) + jnp.log1p(jnp.exp(-jnp.abs(lam)))


def _recip_1_to_2(d):
    r0 = pl.reciprocal(d, approx=True)
    return r0 * (2.0 - d * r0)


def _lru_decay(r, sp, first):
    big_l = (-LRU_C) * r * sp
    a = jnp.exp(big_l)
    th = jnp.tanh(big_l)
    q = (-2.0 * th) * _recip_1_to_2(1.0 - th)
    rs = lax.rsqrt(jnp.maximum(q, SQRT_FLOOR))
    return a, jnp.where(first, 1.0, q * rs), rs


def _pool_inv_counts(t0, n):
    t = (t0 + lax.broadcasted_iota(jnp.int32, (n, 1), 0) + 1).astype(F32)
    return [1.0 / jnp.minimum(t, float(w)) for w in POOL_WINDOWS]


def _window_sums(ext, shift):
    gd = POOL_GROUP_DIM
    out = []
    s = ext
    for k in range(len(POOL_WINDOWS)):
        s = s + shift(s, 2 ** k)
        out.append(s[:, 0:gd])
        if k + 1 < len(POOL_WINDOWS):
            s = s[:, gd:]
    return out


SW_ROWS, SW_COLS = 16, 2 * D
SW_CONV, SW_SC, SW_POOL_B, SW_POOL_S = 0, 4, 8, 9


def _mod_fwd(c8, mod_w, mod_b, conv_w, sc_w, pool_b, pool_s):
    nw = mod_w.shape[2]
    cq, pq = conv_w.shape[1], pool_b.shape[1]

    def body(c_ref, w_ref, b_ref, cw_ref, sw_ref, pb_ref, ps_ref, ca_ref, mod_ref, small_ref,
             cslot, mslot, msend, pslot, psend, s1, r1, s2, r2, s3, r3):
        x, y, c = _pos()
        me = 4 * x + 2 * y + c
        chip = 2 * x + y
        first = []
        for r in range(1, N_DEV):
            fx, fy, fc = (r >> 2) & 1, (r >> 1) & 1, r & 1
            cp = pltpu.make_async_remote_copy(
                src_ref=c_ref, dst_ref=cslot.at[me], send_sem=s1.at[r - 1], recv_sem=r1.at[r - 1],
                device_id=(_flip(x, fx), _flip(y, fy), _flip(c, fc)), device_id_type=MESH)
            cp.start()
            first.append(cp)
        cslot[me] = c_ref[...]
        for cp in first:
            cp.wait()
        rows = _rows(SUBLANES, D)
        call = jnp.zeros((SUBLANES, D), F32)
        for d in range(N_DEV):
            call = jnp.where(rows == d, cslot[d], call)
        ca = call * _sigmoid(call)
        ca_ref[...] = ca
        for l in range(2):
            msend[l] = jnp.dot(ca, w_ref[l], precision=lax.Precision.HIGHEST, preferred_element_type=F32)
        psend[...] = jnp.zeros_like(psend)
        psend[SW_CONV:SW_CONV + 4, 0:cq] = cw_ref[...]
        psend[SW_SC:SW_SC + 3, 0:cq] = sw_ref[...]
        psend[SW_POOL_B:SW_POOL_B + 1, :] = pb_ref[...]
        psend[SW_POOL_S:SW_POOL_S + 1, :] = ps_ref[...]
        second = []
        for q, (fx, fy) in enumerate(((1, 0), (0, 1), (1, 1))):
            peer = (_flip(x, fx), _flip(y, fy), c)
            for src, dst, ss, rs in ((msend, mslot, s2, r2), (psend, pslot, s3, r3)):
                cp = pltpu.make_async_remote_copy(src_ref=src, dst_ref=dst.at[chip], send_sem=ss.at[q], recv_sem=rs.at[q],
                                                  device_id=peer, device_id_type=MESH)
                cp.start()
                second.append(cp)
        mslot[chip] = msend[...]
        pslot[chip] = psend[...]
        for cp in second:
            cp.wait()
        small_ref[...] = jnp.zeros_like(small_ref)
        for j in range(N_CHIP):
            for l in range(2):
                mod_ref[l, :, j * nw:(j + 1) * nw] = mslot[j, l] + b_ref[l:l + 1, j * nw:(j + 1) * nw]
            small_ref[0:SUBLANES, j * cq:(j + 1) * cq] = pslot[j, 0:SUBLANES, 0:cq]
            small_ref[SUBLANES:SW_ROWS, j * pq:(j + 1) * pq] = pslot[j, SUBLANES:SW_ROWS, :]

    args = (c8, mod_w, mod_b, conv_w, sc_w, pool_b, pool_s)
    dma3 = pltpu.SemaphoreType.DMA((N_CHIP - 1,))
    return pl.pallas_call(
        body, name="mod_fwd",
        in_specs=[VMEM] * len(args), out_specs=[VMEM] * 3,
        out_shape=[jax.ShapeDtypeStruct((SUBLANES, D), F32), jax.ShapeDtypeStruct((2, SUBLANES, N_CHIP * nw), F32),
                   jax.ShapeDtypeStruct((SW_ROWS, SW_COLS), F32)],
        scratch_shapes=[pltpu.VMEM((N_DEV, SUBLANES, D), F32), pltpu.VMEM((N_CHIP, 2, SUBLANES, nw), F32),
                        pltpu.VMEM((2, SUBLANES, nw), F32), pltpu.VMEM((N_CHIP, SW_ROWS, pq), F32),
                        pltpu.VMEM((SW_ROWS, pq), F32),
                        pltpu.SemaphoreType.DMA((N_DEV - 1,)), pltpu.SemaphoreType.DMA((N_DEV - 1,)),
                        dma3, dma3, dma3, dma3],
        compiler_params=_cp(),
    )(*args)


def _wcast(ws):
    def body(*refs):
        n = len(refs) // 2
        for a in range(n):
            refs[n + a][...] = refs[a][...].astype(BF16)

    return pl.pallas_call(
        body, name="wcast", in_specs=[VMEM] * len(ws), out_specs=[VMEM] * len(ws),
        out_shape=[jax.ShapeDtypeStruct(w.shape, BF16) for w in ws], compiler_params=_cp(),
    )(*ws)


def _wcast_own_block(w, kidx, name, after=()):
    rr, cc = w.shape
    rb = min(rr, 256)

    def body(k_ref, w_ref, *rest):
        rest[-1][...] = w_ref[...].astype(BF16)

    order = list(after)
    return pl.pallas_call(
        body, name=name,
        grid_spec=pltpu.PrefetchScalarGridSpec(
            num_scalar_prefetch=1, grid=(rr // rb,),
            in_specs=[pl.BlockSpec((rb, cc), lambda j, k_ref: (j, 0))] + [ANY] * len(order),
            out_specs=pl.BlockSpec((None, rb, cc), lambda j, k_ref: (k_ref[0], j, 0))),
        out_shape=jax.ShapeDtypeStruct((N_CHIP, rr, cc), BF16),
        compiler_params=_cp(("parallel",)),
    )(kidx, w, *order)


def _wgather_copies(outs, rows, ssem, rsem, fssem, frsem):
    n = len(outs)
    x, y, c = _pos()
    chip = 2 * x + y
    sib = (x, y, 1 - c)
    flips = ((1, 0), (0, 1), (1, 1))

    def half(a, which):
        hr = rows[a] // 2
        return pl.ds(pl.multiple_of(which * hr, BF16_ROWS), hr)

    sends = []
    for a in range(n):
        mine = outs[a].at[chip, half(a, c), :]
        for q, (fx, fy) in enumerate(flips):
            cp = pltpu.make_async_remote_copy(
                src_ref=mine, dst_ref=mine, send_sem=ssem.at[3 * a + q], recv_sem=rsem.at[3 * a + q],
                device_id=(_flip(x, fx), _flip(y, fy), c), device_id_type=MESH)
            cp.start()
            sends.append(cp)
    passed = []
    for a in range(n):
        for q, (fx, fy) in enumerate(flips):
            src_chip = 2 * _flip(x, fx) + _flip(y, fy)
            landed = outs[a].at[src_chip, half(a, c), :]
            pltpu.make_async_remote_copy(
                src_ref=landed, dst_ref=landed, send_sem=ssem.at[3 * a + q], recv_sem=rsem.at[3 * a + q],
                device_id=sib, device_id_type=MESH).wait_recv()
            cp = pltpu.make_async_remote_copy(
                src_ref=landed, dst_ref=landed, send_sem=fssem.at[3 * a + q], recv_sem=frsem.at[3 * a + q],
                device_id=sib, device_id_type=MESH)
            cp.start()
            passed.append(cp)
    for a in range(n):
        for q, (fx, fy) in enumerate(flips):
            src_chip = 2 * _flip(x, fx) + _flip(y, fy)
            other = outs[a].at[src_chip, half(a, 1 - c), :]
            pltpu.make_async_remote_copy(
                src_ref=other, dst_ref=other, send_sem=fssem.at[3 * a + q], recv_sem=frsem.at[3 * a + q],
                device_id=sib, device_id_type=MESH).wait_recv()
    for cp in sends + passed:
        cp.wait_send()


def _wgather_sequencer(bufs, name, collective_id):
    n = len(bufs)
    refs = [jax.new_ref(b, memory_space=pltpu.MemorySpace.HBM) for b in bufs]
    dma = pltpu.SemaphoreType.DMA((3 * n,))

    @pl.kernel(mesh=plsc.ScalarSubcoreMesh(axis_name="sequencer", num_cores=1), name=name,
               scratch_types=(dma, dma, dma, dma), compiler_params=pltpu.CompilerParams(collective_id=collective_id))
    def launch(ssem, rsem, fssem, frsem):
        x, y, c = _pos()
        barrier = pltpu.get_barrier_semaphore()
        for peer in ((1 - x, y, c), (x, 1 - y, c), (1 - x, 1 - y, c), (x, y, 1 - c)):
            pl.semaphore_signal(barrier, inc=1, device_id=peer, device_id_type=MESH)
        pl.semaphore_wait(barrier, 4)
        _wgather_copies(refs, [b.shape[1] for b in bufs], ssem, rsem, fssem, frsem)

    launch()
    return [r[...] for r in refs]


def _l0_fwd(x, g, sc, sh, w_in, gate, cw, cb, wa, ba, wx, bx, lam, sw, wo):
    s_len, nb = x.shape[0], w_in.shape[2]
    ts = _tile(s_len, TS_MIX)
    n_t = s_len // ts
    hl = SUBLANES

    def body(xa_ref, xb_ref, g_ref, sc_ref, sh_ref, win_ref, gate_ref, cw_ref, cb_ref, wa_ref, ba_ref, wx_ref, bx_ref,
             lam_ref, sw_ref, wo_ref, x1_ref, h_ref, y_ref, xc_ref, cz_ref, h0_ref, p_ref, pcur, pnext, cxa, czz, chh):
        i = pl.program_id(0)

        @pl.when(i == 0)
        def _():
            cxa[...] = jnp.zeros_like(cxa)
            czz[...] = jnp.zeros_like(czz)
            chh[...] = jnp.zeros_like(chh)
            pnext[...] = jnp.zeros_like(pnext)

        pcur[...] = pnext[...]
        xv = xa_ref[...]
        rinv = lax.rsqrt(jnp.mean(xv * xv, axis=-1, keepdims=True) + RMS_EPS)
        h0 = (xv * rinv * (g_ref[...] * (1.0 + sc_ref[...])) + sh_ref[...]).astype(BF16)
        h0_ref[...] = h0

        def project(k):
            def emit():
                pk = jnp.dot(h0, win_ref[k], preferred_element_type=F32).astype(BF16)
                p_ref[:, k * nb:(k + 1) * nb] = pk
                pnext[:, k * nb:(k + 1) * nb] = pk
            return emit

        def mixer():
            piece = lambda k: pcur[:, k * D:(k + 1) * D].astype(F32)
            xa = piece(0)
            rows = _rows(ts, D)
            taps = _conv_taps(jnp.concatenate([cxa[...], xa], axis=0), hl, ts, 4)
            xc = cb_ref[...] + sum(cw_ref[k:k + 1, :] * taps[k] for k in range(4))
            xc_ref[...] = xc.astype(BF16)
            r, ig = _lru_gates(xc, wa_ref, ba_ref[...], wx_ref, bx_ref[...])
            a, m, _ = _lru_decay(r, _softplus_neg(lam_ref[...]), (rows == 0) & (i == 1))
            yield 0.26
            h = _run(_scan_fwd_steps(a, m * ig * xc, chh[hl - 1:hl, :]))
            yield 0.51
            gcp, v = piece(3), piece(4)
            z = gcp * v
            ztaps = _conv_taps(jnp.concatenate([czz[...], z], axis=0), hl, ts, 3)
            cz = sum(sw_ref[k:k + 1, :] * ztaps[k] for k in range(3))
            cz_ref[...] = cz.astype(BF16)
            yb = piece(2) * cz
            ga, gb = piece(1), piece(5)
            y = jnp.concatenate([h * (ga * _sigmoid(ga)), yb * (gb * _sigmoid(gb))], axis=1).astype(BF16)
            yield 0.76
            y_ref[...] = y
            x1_ref[...] = xb_ref[...] + gate_ref[...] * jnp.dot(y, wo_ref[...], preferred_element_type=F32)
            h_ref[...] = h.astype(BF16)
            cxa[...] = xa[ts - hl:, :]
            czz[...] = z[ts - hl:, :]
            chh[...] = jnp.where(i > 0, h[ts - hl:, :], 0.0)

        _paired(mixer(), [project(k) for k in range(N_CHIP)])

    def full(a):
        return pl.BlockSpec(a.shape, lambda i: (0,) * a.ndim)

    ahead = lambda w: pl.BlockSpec((ts, w), lambda i: (jnp.minimum(i, n_t - 1), 0))
    behind = lambda w: pl.BlockSpec((ts, w), lambda i: (jnp.maximum(i - 1, 0), 0))
    args = (x, x, g, sc, sh, w_in, gate, cw, cb, wa, ba, wx, bx, lam, sw, wo)
    return pl.pallas_call(
        body, name="l0_fwd", grid=(n_t + 1,),
        in_specs=[ahead(D), behind(D)] + [full(a) for a in args[2:]],
        out_specs=[behind(D), behind(D), behind(2 * D), behind(D), behind(D), ahead(D), ahead(N_CHIP * nb)],
        out_shape=[jax.ShapeDtypeStruct((s_len, D), F32), jax.ShapeDtypeStruct((s_len, D), BF16),
                   jax.ShapeDtypeStruct((s_len, 2 * D), BF16), jax.ShapeDtypeStruct((s_len, D), BF16),
                   jax.ShapeDtypeStruct((s_len, D), BF16), jax.ShapeDtypeStruct((s_len, D), BF16),
                   jax.ShapeDtypeStruct((s_len, N_CHIP * nb), BF16)],
        scratch_shapes=[pltpu.VMEM((ts, N_CHIP * nb), BF16)] * 2 + [pltpu.VMEM((hl, D), F32)] * 3,
        compiler_params=_cp(("arbitrary",)),
    )(*args)


def _l1_fwd(x1, g, sc, sh, w_in, tgt, gate, wg, bg, scale, wo, gf):
    s_len, nb = x1.shape[0], w_in.shape[2]
    ts = _tile(s_len, TS_MIX)
    n_t = s_len // ts
    pw, gd, hl = 2 * D, POOL_GROUP_DIM, POOL_HALO

    def body(xa_ref, xb_ref, t_ref, g_ref, sc_ref, sh_ref, win_ref, gate_ref, wg_ref, bg_ref, scl_ref, wo_ref, gf_ref,
             d_ref, mx_ref, y_ref, dx_ref, loss_ref, dgf_ref, h1_ref, p_ref, pcur, pnext, cv):
        i = pl.program_id(0)

        @pl.when(i == 0)
        def _():
            cv[...] = jnp.zeros_like(cv)
            loss_ref[...] = jnp.zeros_like(loss_ref)
            dgf_ref[...] = jnp.zeros_like(dgf_ref)
            pnext[...] = jnp.zeros_like(pnext)

        pcur[...] = pnext[...]
        xv = xa_ref[...]
        rinv = lax.rsqrt(jnp.mean(xv * xv, axis=-1, keepdims=True) + RMS_EPS)
        h1 = (xv * rinv * (g_ref[...] * (1.0 + sc_ref[...])) + sh_ref[...]).astype(BF16)
        h1_ref[...] = h1

        def project(k):
            def emit():
                pk = jnp.dot(h1, win_ref[k], preferred_element_type=F32).astype(BF16)
                p_ref[:, k * nb:(k + 1) * nb] = pk
                pnext[:, k * nb:(k + 1) * nb] = pk
            return emit

        def mixer():
            v = pcur[:, 0:pw].astype(F32)
            sums = _window_sums(jnp.concatenate([cv[...], v], axis=0), _down)
            inv = _pool_inv_counts(jnp.maximum(i - 1, 0) * ts, ts)
            dd = [sums[k][hl:hl + ts] * inv[k] - v[:, k * gd:(k + 1) * gd] for k in range(4)]
            d_ref[...] = jnp.concatenate(dd, axis=1).astype(BF16)
            yield 0.26
            mixed = jnp.concatenate(
                [jnp.dot(dd[k].astype(BF16), wg_ref[k], preferred_element_type=F32) for k in range(4)], axis=1) + bg_ref[...]
            mx_ref[...] = mixed.astype(BF16)
            gg = pcur[:, pw:2 * pw].astype(F32)
            y = (mixed * scl_ref[...] * (gg * _sigmoid(gg))).astype(BF16)
            y_ref[...] = y
            yield 0.51
            x2 = xb_ref[...] + gate_ref[...] * jnp.dot(y, wo_ref[...], preferred_element_type=F32)
            yield 0.76
            r2 = lax.rsqrt(jnp.mean(x2 * x2, axis=-1, keepdims=True) + RMS_EPS)
            n2 = x2 * r2
            err = n2 * gf_ref[...] - t_ref[...]
            loss_ref[...] += jnp.where(i > 0, jnp.sum(err * err, axis=0, keepdims=True), 0.0)
            dyf = err * (1.0 / D)
            dgf_ref[...] += jnp.where(i > 0, jnp.sum(dyf * n2, axis=0, keepdims=True), 0.0)
            dn = dyf * gf_ref[...]
            dx_ref[...] = r2 * (dn - n2 * jnp.mean(dn * n2, axis=-1, keepdims=True))
            cv[...] = v[ts - hl:, :]

        _paired(mixer(), [project(k) for k in range(N_CHIP)])

    def full(a):
        return pl.BlockSpec(a.shape, lambda i: (0,) * a.ndim)

    ahead = lambda w: pl.BlockSpec((ts, w), lambda i: (jnp.minimum(i, n_t - 1), 0))
    behind = lambda w: pl.BlockSpec((ts, w), lambda i: (jnp.maximum(i - 1, 0), 0))
    acc = pl.BlockSpec((1, D), lambda i: (0, 0))
    args = (x1, x1, tgt, g, sc, sh, w_in, gate, wg, bg, scale, wo, gf)
    return pl.pallas_call(
        body, name="l1_fwd", grid=(n_t + 1,),
        in_specs=[ahead(D), behind(D), behind(D)] + [full(a) for a in args[3:]],
        out_specs=[behind(pw), behind(pw), behind(pw), behind(D), acc, acc, ahead(D), ahead(N_CHIP * nb)],
        out_shape=[jax.ShapeDtypeStruct((s_len, pw), BF16)] * 3 + [jax.ShapeDtypeStruct((s_len, D), F32)]
        + [jax.ShapeDtypeStruct((1, D), F32)] * 2
        + [jax.ShapeDtypeStruct((s_len, D), BF16), jax.ShapeDtypeStruct((s_len, N_CHIP * nb), BF16)],
        scratch_shapes=[pltpu.VMEM((ts, N_CHIP * nb), BF16)] * 2 + [pltpu.VMEM((hl, pw), F32)],
        compiler_params=_cp(("arbitrary",)),
    )(*args)


def _l1_bwd_mix(dx2, proj, mixed, y, dpool, gate, wg, scale, wo):
    s_len = dx2.shape[0]
    ts = _tile(s_len, TS_MIX)
    n_t = s_len // ts
    pw, gd, hl = 2 * D, POOL_GROUP_DIM, POOL_HALO

    def body(dx_ref, gg_ref, mx_ref, y_ref, d_ref, gate_ref, wg_ref, sc_ref, wo_ref,
             dp_ref, mt_ref, dwg_ref, dsc_ref, dbg_ref, cq):
        i = pl.program_id(0)

        @pl.when(i == 0)
        def _():
            cq[...] = jnp.zeros_like(cq)
            dsc_ref[...] = jnp.zeros_like(dsc_ref)
            dbg_ref[...] = jnp.zeros_like(dbg_ref)
            mt_ref[...] = jnp.zeros_like(mt_ref)
            dwg_ref[...] = jnp.zeros_like(dwg_ref)

        dxv = dx_ref[...]
        dxb = dxv.astype(BF16)

        def wgrad_out(k):
            mt_ref[k] += lax.dot_general(y_ref[:, k * gd:(k + 1) * gd], dxb, TN, preferred_element_type=F32)

        dy = lax.dot_general((gate_ref[...] * dxv).astype(BF16), wo_ref[...], NT, preferred_element_type=F32)
        wgrad_out(0)
        gg = gg_ref[...].astype(F32)
        mixed = mx_ref[...].astype(F32)
        s = _sigmoid(gg)
        sg = gg * s
        dmixed = dy * sc_ref[...] * sg
        dsc_ref[...] += jnp.sum(dy * mixed * sg, axis=0, keepdims=True)
        dbg_ref[...] += jnp.sum(dmixed, axis=0, keepdims=True)
        dmb = dmixed.astype(BF16)
        wgrad_out(1)
        dp_ref[:, pw:2 * pw] = (dy * sc_ref[...] * mixed * (s * (1.0 + gg * (1.0 - s)))).astype(BF16)
        inv = _pool_inv_counts((n_t - 1 - i) * ts, ts)
        dd = []
        for k in range(4):
            dmk = dmb[:, k * gd:(k + 1) * gd]
            dd.append(lax.dot_general(dmk, wg_ref[k], NT, preferred_element_type=F32))
            dwg_ref[k] += lax.dot_general(d_ref[:, k * gd:(k + 1) * gd], dmk, TN, preferred_element_type=F32)
        wgrad_out(2)
        q = jnp.concatenate([dd[k] * inv[k] for k in range(4)], axis=1)
        sums = _window_sums(jnp.concatenate([q, cq[...]], axis=0), _up)
        wgrad_out(3)
        dp_ref[:, 0:pw] = jnp.concatenate([sums[k][0:ts] - dd[k] for k in range(4)], axis=1).astype(BF16)
        cq[...] = q[0:hl, :]

    def full(a):
        return pl.BlockSpec(a.shape, lambda i: (0,) * a.ndim)

    rev = lambda w, j=0: pl.BlockSpec((ts, w), lambda i: (n_t - 1 - i, j))
    acc = pl.BlockSpec((1, pw), lambda i: (0, 0))
    return pl.pallas_call(
        body, name="l1_bwd_mix", grid=(n_t,),
        in_specs=[rev(D), rev(pw, 1), rev(pw), rev(pw), rev(pw)] + [full(a) for a in (gate, wg, scale, wo)],
        out_specs=[rev(2 * pw), pl.BlockSpec((N_CHIP, gd, D), lambda i: (0, 0, 0)),
                   pl.BlockSpec((4, gd, gd), lambda i: (0, 0, 0)), acc, acc],
        out_shape=[jax.ShapeDtypeStruct((s_len, 2 * pw), BF16), jax.ShapeDtypeStruct((N_CHIP, gd, D), F32),
                   jax.ShapeDtypeStruct((4, gd, gd), F32),
                   jax.ShapeDtypeStruct((1, pw), F32), jax.ShapeDtypeStruct((1, pw), F32)],
        scratch_shapes=[pltpu.VMEM((hl, pw), F32)],
        compiler_params=_cp(("arbitrary",)),
    )(dx2, proj, mixed, y, dpool, gate, wg, scale, wo)


def _l0_bwd_mix(dx1, proj, hst, y, xc, cz, gate, cw, wa, ba, wx, bx, lam, sw, wo):
    s_len = dx1.shape[0]
    ts = _tile(s_len, TS_MIX)
    n_t = s_len // ts
    hl, hb = SUBLANES, BF16_ROWS
    yb_w = 2 * D // N_CHIP

    def body(dx_ref, p_ref, h_ref, hh_ref, y_ref, xc_ref, cz_ref, gate_ref, cw_ref, wa_ref, ba_ref, wx_ref, bx_ref,
             lam_ref, sw_ref, wo_ref, dp_ref, mt_ref, dwa_ref, dwx_ref, sm_ref, cg, cdxc, cdcz, ca):
        i = pl.program_id(0)
        ri = n_t - 1 - i

        @pl.when(i == 0)
        def _():
            cg[...] = jnp.zeros_like(cg)
            ca[...] = jnp.zeros_like(ca)
            cdxc[...] = jnp.zeros_like(cdxc)
            cdcz[...] = jnp.zeros_like(cdcz)
            sm_ref[...] = jnp.zeros_like(sm_ref)
            mt_ref[...] = jnp.zeros_like(mt_ref)
            dwa_ref[...] = jnp.zeros_like(dwa_ref)
            dwx_ref[...] = jnp.zeros_like(dwx_ref)

        dxb = dx_ref[...].astype(BF16)

        def wgrad_out(k):
            mt_ref[k] += lax.dot_general(y_ref[:, k * yb_w:(k + 1) * yb_w], dxb, TN, preferred_element_type=F32)

        wgrad_out(0)
        has_prev = (ri > 0).astype(F32)
        xa, ga, gbp, gcp, v, gb = [p_ref[:, k * D:(k + 1) * D].astype(F32) for k in range(6)]
        rows = _rows(ts, D)
        first = (rows == 0) & (ri == 0)
        xc = xc_ref[...].astype(F32)
        cz = cz_ref[...].astype(F32)
        r, ig = _lru_gates(xc, wa_ref, ba_ref[...], wx_ref, bx_ref[...])
        sp = _softplus_neg(lam_ref[...])
        a, m, inv_m = _lru_decay(r, sp, first)
        z = gcp * v
        h = h_ref[...].astype(F32)
        hprev = _down(jnp.concatenate([hh_ref[...].astype(F32)[hb - hl:hb] * has_prev, h], axis=0), 1)[hl:hl + ts]
        dy = lax.dot_general((gate_ref[...] * dx_ref[...]).astype(BF16), wo_ref[...], NT, preferred_element_type=F32)
        dya_pre, dyb_pre = dy[:, 0:D], dy[:, D:2 * D]
        s_a, s_b = _sigmoid(ga), _sigmoid(gb)
        dp_ref[:, D:2 * D] = (dya_pre * h * (s_a * (1.0 + ga * (1.0 - s_a)))).astype(BF16)
        dp_ref[:, 5 * D:6 * D] = (dyb_pre * (gbp * cz) * (s_b * (1.0 + gb * (1.0 - s_b)))).astype(BF16)
        dya = dya_pre * (ga * s_a)
        dyb = dyb_pre * (gb * s_b)
        wgrad_out(1)
        dp_ref[:, 2 * D:3 * D] = (dyb * cz).astype(BF16)
        dcz = dyb * gbp
        dcz_ext = jnp.concatenate([dcz, cdcz[...]], axis=0)
        dcz_taps = [_up(dcz_ext, 2 - k)[0:ts] for k in range(3)]
        for k in range(3):
            sm_ref[8 + k:9 + k, :] += jnp.sum(z * dcz_taps[k], axis=0, keepdims=True)
        dz = sum(sw_ref[k:k + 1, :] * dcz_taps[k] for k in range(3))
        dp_ref[:, 3 * D:4 * D] = (dz * v).astype(BF16)
        dp_ref[:, 4 * D:5 * D] = (dz * gcp).astype(BF16)
        cdcz[...] = dcz[0:hl, :]
        alpha = _up(jnp.concatenate([a, ca[...]], axis=0), 1)[0:ts]
        wgrad_out(2)
        dh = _run(_scan_rev_steps(alpha, dya, cg[0:1, :]))
        wgrad_out(3)
        cg[...] = dh[0:hl, :]
        ca[...] = a[0:hl, :]
        da = dh * hprev
        dm = dh * ig * xc
        di = dh * m * xc
        dxc = dh * m * ig
        dl = da * a - jnp.where(first, 0.0, dm * (a * a) * inv_m)
        sm_ref[7:8, :] += jnp.sum(dl * r, axis=0, keepdims=True) * (-LRU_C)
        dpa = (dl * sp) * (-LRU_C) * r * (1.0 - r)
        dpx = di * ig * (1.0 - ig)
        sm_ref[5:6, :] += jnp.sum(dpa, axis=0, keepdims=True)
        sm_ref[6:7, :] += jnp.sum(dpx, axis=0, keepdims=True)
        dpa_b, dpx_b, xc_b = dpa.astype(BF16), dpx.astype(BF16), xc.astype(BF16)
        back = []
        for hd in range(LRU_HEADS):
            sl = slice(hd * LRU_HEAD_DIM, (hd + 1) * LRU_HEAD_DIM)
            back.append(lax.dot_general(dpa_b[:, sl], wa_ref[hd], NT, preferred_element_type=F32)
                        + lax.dot_general(dpx_b[:, sl], wx_ref[hd], NT, preferred_element_type=F32))
            dwa_ref[hd] += lax.dot_general(xc_b[:, sl], dpa_b[:, sl], TN, preferred_element_type=F32)
            dwx_ref[hd] += lax.dot_general(xc_b[:, sl], dpx_b[:, sl], TN, preferred_element_type=F32)
        dxc = dxc + jnp.concatenate(back, axis=1)
        sm_ref[4:5, :] += jnp.sum(dxc, axis=0, keepdims=True)
        dxc_ext = jnp.concatenate([dxc, cdxc[...]], axis=0)
        dxc_taps = [_up(dxc_ext, 3 - k)[0:ts] for k in range(4)]
        for k in range(4):
            sm_ref[k:k + 1, :] += jnp.sum(xa * dxc_taps[k], axis=0, keepdims=True)
        dp_ref[:, 0:D] = sum(cw_ref[k:k + 1, :] * dxc_taps[k] for k in range(4)).astype(BF16)
        cdxc[...] = dxc[0:hl, :]

    def full(a):
        return pl.BlockSpec(a.shape, lambda i: (0,) * a.ndim)

    rev = lambda w: pl.BlockSpec((ts, w), lambda i: (n_t - 1 - i, 0))
    halo = lambda w: pl.BlockSpec((hb, w), lambda i: (jnp.maximum((n_t - 1 - i) * (ts // hb) - 1, 0), 0))
    return pl.pallas_call(
        body, name="l0_bwd_mix", grid=(n_t,),
        in_specs=[rev(D), rev(6 * D), rev(D), halo(D), rev(2 * D), rev(D), rev(D)]
        + [full(a) for a in (gate, cw, wa, ba, wx, bx, lam, sw, wo)],
        out_specs=[rev(6 * D), pl.BlockSpec((N_CHIP, yb_w, D), lambda i: (0, 0, 0)),
                   pl.BlockSpec(wa.shape, lambda i: (0, 0, 0)), pl.BlockSpec(wa.shape, lambda i: (0, 0, 0)),
                   pl.BlockSpec((2 * SUBLANES, D), lambda i: (0, 0))],
        out_shape=[jax.ShapeDtypeStruct((s_len, 6 * D), BF16), jax.ShapeDtypeStruct((N_CHIP, yb_w, D), F32),
                   jax.ShapeDtypeStruct(wa.shape, F32), jax.ShapeDtypeStruct(wa.shape, F32),
                   jax.ShapeDtypeStruct((2 * SUBLANES, D), F32)],
        scratch_shapes=[pltpu.VMEM((hl, D), F32)] * 4,
        compiler_params=_cp(("arbitrary",)),
    )(dx1, proj, hst, hst, y, xc, cz, gate, cw, wa, ba, wx, bx, lam, sw, wo)


def _dgrad_norm(dproj, w, x, dres, g, sc, name, after=None):
    s_len, nb = x.shape[0], w.shape[2]
    ts = _tile(s_len, TS_DGRAD)
    order = [] if after is None else [after]

    def body(dp_ref, w_ref, x_ref, dr_ref, g_ref, sc_ref, *rest):
        dx_ref, s1_ref, s2_ref = rest[len(order):]

        @pl.when(pl.program_id(0) == 0)
        def _():
            s1_ref[...] = jnp.zeros_like(s1_ref)
            s2_ref[...] = jnp.zeros_like(s2_ref)

        dh = sum(lax.dot_general(dp_ref[:, k * nb:(k + 1) * nb], w_ref[k], NT, preferred_element_type=F32)
                 for k in range(N_CHIP))
        xv = x_ref[...]
        r = lax.rsqrt(jnp.mean(xv * xv, axis=-1, keepdims=True) + RMS_EPS)
        n = xv * r
        s1_ref[...] += jnp.sum(dh, axis=0, keepdims=True)
        s2_ref[...] += jnp.sum(dh * n, axis=0, keepdims=True)
        dn = dh * (g_ref[...] * (1.0 + sc_ref[...]))
        dx_ref[...] = dr_ref[...] + r * (dn - n * jnp.mean(dn * n, axis=-1, keepdims=True))

    row = lambda wd: pl.BlockSpec((ts, wd), lambda i: (i, 0))
    vec = pl.BlockSpec((1, D), lambda i: (0, 0))
    return pl.pallas_call(
        body, name=name, grid=(s_len // ts,),
        in_specs=[row(N_CHIP * nb), pl.BlockSpec(w.shape, lambda i: (0, 0, 0)), row(D), row(D), vec, vec]
        + [ANY] * len(order),
        out_specs=[row(D), vec, vec],
        out_shape=[jax.ShapeDtypeStruct((s_len, D), F32)] + [jax.ShapeDtypeStruct((1, D), F32)] * 2,
        compiler_params=_cp(("arbitrary",)),
    )(dproj, w, x, dres, g, sc, *order)


def _wgrad(a, b, groups, ka, nb, a_col, b_col, name, after=None):
    s_len = a.shape[0]
    ts = _tile(s_len, TS_WGRAD)
    n_s = s_len // ts
    order = [] if after is None else [after]

    def body(a_ref, b_ref, *rest):
        o_ref, wire_ref = rest[-2:]

        @pl.when(pl.program_id(1) == 0)
        def _():
            o_ref[...] = jnp.zeros_like(o_ref)

        o_ref[...] += lax.dot_general(a_ref[...].astype(BF16), b_ref[...].astype(BF16), TN, preferred_element_type=F32)

        @pl.when(pl.program_id(1) == n_s - 1)
        def _():
            wire_ref[...] = o_ref[...].astype(GRAD_WIRE_DTYPE)

    blk = pl.BlockSpec((None, ka, nb), lambda g, s: (g, 0, 0))
    return pl.pallas_call(
        body, name=name, grid=(groups, n_s),
        in_specs=[pl.BlockSpec((ts, ka), lambda g, s: (s, a_col(g))), pl.BlockSpec((ts, nb), lambda g, s: (s, b_col(g)))]
        + [ANY] * len(order),
        out_specs=[blk, blk],
        out_shape=[jax.ShapeDtypeStruct((groups, ka, nb), F32), jax.ShapeDtypeStruct((groups, ka, nb), GRAD_WIRE_DTYPE)],
        compiler_params=_cp(("parallel", "arbitrary")),
    )(a, b, *order)


def _wo_final(mt, wo, gate, name):
    rb = mt.shape[1]

    def body(m_ref, w_ref, gate_ref, dw_ref, wire_ref, dg_ref):
        @pl.when(pl.program_id(0) == 0)
        def _():
            dg_ref[...] = jnp.zeros_like(dg_ref)

        mv = m_ref[...]
        dw = mv * gate_ref[...]
        dw_ref[...] = dw
        wire_ref[...] = dw.astype(GRAD_WIRE_DTYPE)
        dg_ref[...] += jnp.sum(mv * w_ref[...].astype(F32), axis=0, keepdims=True)

    blk = pl.BlockSpec((None, rb, D), lambda k: (k, 0, 0))
    vec = pl.BlockSpec((1, D), lambda k: (0, 0))
    return pl.pallas_call(
        body, name=name, grid=(N_CHIP,), in_specs=[blk, blk, vec], out_specs=[blk, blk, vec],
        out_shape=[jax.ShapeDtypeStruct(mt.shape, F32), jax.ShapeDtypeStruct(mt.shape, GRAD_WIRE_DTYPE),
                   jax.ShapeDtypeStruct((1, D), F32)],
        compiler_params=_cp(("arbitrary",)),
    )(mt, wo, gate)


ROW_NORM_G, ROW_CONV_W, ROW_CONV_B, ROW_B_A, ROW_B_X, ROW_LAMBDA, ROW_SC_W, ROW_POOL_B, ROW_POOL_S, ROW_FINAL_G = (
    0, 2, 6, 7, 8, 9, 10, 13, 15, 17)
ROW_LOSS = 18


def _small_pack(s1_0, s2_0, s1_1, s2_1, sm0, dsc1, dbg1, dgf, losscols, dgate0, dgate1, norm_g, sc0, sc1, lam):
    def body(s1_0r, s2_0r, s1_1r, s2_1r, sm, dsc, dbg, dgfr, lcols, dg0, dg1, ng, sc0r, sc1r, lamr, buf, dmod):
        buf[...] = jnp.zeros_like(buf)
        buf[0:1, :] = s2_0r[...] * (1.0 + sc0r[...])
        buf[1:2, :] = s2_1r[...] * (1.0 + sc1r[...])
        buf[ROW_CONV_W:ROW_CONV_W + 4, :] = sm[0:4, :]
        buf[ROW_CONV_B:ROW_CONV_B + 1, :] = sm[4:5, :]
        buf[ROW_B_A:ROW_B_A + 1, :] = sm[5:6, :]
        buf[ROW_B_X:ROW_B_X + 1, :] = sm[6:7, :]
        buf[ROW_LAMBDA:ROW_LAMBDA + 1, :] = -sm[7:8, :] * _sigmoid(-lamr[...])
        buf[ROW_SC_W:ROW_SC_W + 3, :] = sm[8:11, :]
        for k in range(2):
            buf[ROW_POOL_B + k:ROW_POOL_B + k + 1, :] = dbg[:, k * D:(k + 1) * D]
            buf[ROW_POOL_S + k:ROW_POOL_S + k + 1, :] = dsc[:, k * D:(k + 1) * D]
        buf[ROW_FINAL_G:ROW_FINAL_G + 1, :] = dgfr[...]
        pieces = (s1_0r[...], s2_0r[...] * ng[0:1, :], dg0[...], s1_1r[...], s2_1r[...] * ng[1:2, :], dg1[...])
        for k, pc in enumerate(pieces):
            dmod[:, k * D:(k + 1) * D] = jnp.broadcast_to(pc, (SUBLANES, D))
        buf[ROW_LOSS:ROW_LOSS + 1, :] = jnp.broadcast_to(jnp.sum(lcols[...], axis=1, keepdims=True) * (0.5 / D), (1, D))

    args = (s1_0, s2_0, s1_1, s2_1, sm0, dsc1, dbg1, dgf, losscols, dgate0, dgate1, norm_g, sc0, sc1, lam)
    return pl.pallas_call(
        body, name="small_pack", in_specs=[VMEM] * len(args), out_specs=[VMEM] * 2,
        out_shape=[jax.ShapeDtypeStruct((SMALL_ROWS, D), F32), jax.ShapeDtypeStruct((SUBLANES, 6 * D), F32)],
        compiler_params=_cp(),
    )(*args)


def _small_comm(buf_a, buf_b, dmod8):
    ra, rb = buf_a.shape[0] // N_DEV, buf_b.shape[0] // N_DEV
    wb = buf_b.shape[1]

    def body(a_ref, b_ref, dm_ref, oa_ref, ob_ref, odm_ref, ina, inb, dslot, sa, sb, s1, r1, s2, r2):
        x, y, c = _pos()
        me = 4 * x + 2 * y + c
        peers = []
        for r in range(1, N_DEV):
            fx, fy, fc = (r >> 2) & 1, (r >> 1) & 1, r & 1
            px, py, pc = _flip(x, fx), _flip(y, fy), _flip(c, fc)
            peers.append(((px, py, pc), 4 * px + 2 * py + pc))
        seg_a = lambda d: pl.ds(pl.multiple_of(d * ra, SUBLANES), ra)
        seg_b = lambda d: pl.ds(pl.multiple_of(d * rb, SUBLANES), rb)
        first = []
        for r, (peer, pid) in enumerate(peers):
            for k, (src, dst) in enumerate(((a_ref.at[seg_a(pid), :], ina.at[r]), (b_ref.at[seg_b(pid), :], inb.at[r]),
                                            (dm_ref, dslot.at[me]))):
                cp = pltpu.make_async_remote_copy(src_ref=src, dst_ref=dst, send_sem=s1.at[3 * r + k],
                                                  recv_sem=r1.at[3 * r + k], device_id=peer, device_id_type=MESH)
                cp.start()
                first.append(cp)
        dslot[me] = dm_ref[...]
        for cp in first:
            cp.wait()
        acc_a, acc_b = a_ref[seg_a(me), :], b_ref[seg_b(me), :]
        for r in range(N_DEV - 1):
            acc_a = acc_a + ina[r]
            acc_b = acc_b + inb[r]
        sa[...] = acc_a
        sb[...] = acc_b
        oa_ref[seg_a(me), :] = acc_a
        ob_ref[seg_b(me), :] = acc_b
        second = []
        for r, (peer, pid) in enumerate(peers):
            for k, (src, dst) in enumerate(((sa, oa_ref.at[seg_a(me), :]), (sb, ob_ref.at[seg_b(me), :]))):
                cp = pltpu.make_async_remote_copy(src_ref=src, dst_ref=dst, send_sem=s2.at[2 * r + k],
                                                  recv_sem=r2.at[2 * r + k], device_id=peer, device_id_type=MESH)
                cp.start()
                second.append(cp)
        rows = _rows(SUBLANES, dm_ref.shape[1])
        dm_all = jnp.zeros(dm_ref.shape, F32)
        for d in range(N_DEV):
            dm_all = jnp.where(rows == d, dslot[d], dm_all)
        odm_ref[...] = dm_all
        for cp in second:
            cp.wait()

    nrel = N_DEV - 1
    return pl.pallas_call(
        body, name="small_comm", in_specs=[VMEM] * 3, out_specs=[VMEM] * 3,
        out_shape=[jax.ShapeDtypeStruct(buf_a.shape, F32), jax.ShapeDtypeStruct(buf_b.shape, F32),
                   jax.ShapeDtypeStruct(dmod8.shape, F32)],
        scratch_shapes=[pltpu.VMEM((nrel, ra, D), F32), pltpu.VMEM((nrel, rb, wb), F32),
                        pltpu.VMEM((N_DEV,) + dmod8.shape, F32), pltpu.VMEM((ra, D), F32), pltpu.VMEM((rb, wb), F32),
                        pltpu.SemaphoreType.DMA((3 * nrel,)), pltpu.SemaphoreType.DMA((3 * nrel,)),
                        pltpu.SemaphoreType.DMA((2 * nrel,)), pltpu.SemaphoreType.DMA((2 * nrel,))],
        compiler_params=_cp(),
    )(buf_a, buf_b, dmod8)


def _adam(w, g, m, v):
    m2 = ADAM_B1 * m + (1.0 - ADAM_B1) * g
    v2 = ADAM_B2 * v + (1.0 - ADAM_B2) * (g * g)
    m_hat = m2 / (1.0 - ADAM_B1 ** ADAM_STEP)
    v_hat = v2 / (1.0 - ADAM_B2 ** ADAM_STEP)
    return -ADAM_LR * (m_hat / (jnp.sqrt(v_hat) + ADAM_EPS) + ADAM_WD * w), m2, v2


def _small_adam(red_a, red_b, dm_all, params):
    n = len(params)

    def body(*refs):
        ra, rb, dm = refs[:3]
        wmv = refs[3:3 + 3 * n]
        outs = refs[3 + 3 * n:]
        x, y, _ = _pos()
        chip = 2 * x + y

        def shard(row0, nrows, width):
            per_row = D // width
            cands = []
            for k in range(N_CHIP):
                if nrows == 1 or per_row >= N_CHIP:
                    cands.append(ra[row0:row0 + nrows, k * width:(k + 1) * width])
                else:
                    rr, cc = divmod(k * width, D)
                    cands.append(ra[row0 + rr:row0 + rr + 1, cc:cc + width])
            g = cands[0]
            for k in range(1, N_CHIP):
                g = jnp.where(chip == k, cands[k], g)
            return g

        dms = jnp.sum(dm[...], axis=0, keepdims=True)
        hw = LRU_HEADS * LRU_HEAD_DIM
        grads = [
            ra[ROW_NORM_G:ROW_NORM_G + 2, :],
            None,
            shard(ROW_CONV_W, 4, D // N_CHIP),
            ra[ROW_CONV_B:ROW_CONV_B + 1, :],
            rb[0:hw, :],
            ra[ROW_B_A:ROW_B_A + 1, :],
            rb[hw:2 * hw, :],
            ra[ROW_B_X:ROW_B_X + 1, :],
            ra[ROW_LAMBDA:ROW_LAMBDA + 1, :],
            shard(ROW_SC_W, 3, D // N_CHIP),
            shard(ROW_POOL_B, 2, 2 * D // N_CHIP),
            shard(ROW_POOL_S, 2, 2 * D // N_CHIP),
            ra[ROW_FINAL_G:ROW_FINAL_G + 1, :],
        ]
        for p in range(n):
            w_ref, m_ref, v_ref = wmv[3 * p:3 * p + 3]
            g_out, d_out, m_out, v_out = outs[4 * p:4 * p + 4]
            if grads[p] is None:
                for l in range(2):
                    g = dms[:, l * 3 * D:(l + 1) * 3 * D]
                    dl, m2, v2 = _adam(w_ref[l:l + 1, :], g, m_ref[l:l + 1, :], v_ref[l:l + 1, :])
                    g_out[l:l + 1, :] = g
                    d_out[l:l + 1, :] = dl
                    m_out[l:l + 1, :] = m2
                    v_out[l:l + 1, :] = v2
            else:
                g = grads[p]
                dl, m2, v2 = _adam(w_ref[...], g, m_ref[...], v_ref[...])
                g_out[...] = g
                d_out[...] = dl
                m_out[...] = m2
                v_out[...] = v2

    flat = [a for p in params for a in p]
    return pl.pallas_call(
        body, name="small_adam", in_specs=[VMEM] * (3 + len(flat)), out_specs=[VMEM] * (4 * n),
        out_shape=[jax.ShapeDtypeStruct(p[0].shape, F32) for p in params for _ in range(4)],
        compiler_params=_cp(),
    )(red_a, red_b, dm_all, *flat)


def _modw_adam(ca_t, dm_sh, w, m, v):
    nw = w.shape[2]

    def body(c_ref, d_ref, w_ref, m_ref, v_ref, g_out, d_out, m_out, v_out):
        g = jnp.dot(c_ref[...], d_ref[...], precision=lax.Precision.HIGHEST, preferred_element_type=F32)
        dl, m2, v2 = _adam(w_ref[...], g, m_ref[...], v_ref[...])
        g_out[...] = g
        d_out[...] = dl
        m_out[...] = m2
        v_out[...] = v2

    blk = pl.BlockSpec((None, D, nw), lambda l: (l, 0, 0))
    return pl.pallas_call(
        body, name="modw_adam", grid=(2,),
        in_specs=[pl.BlockSpec((D, SUBLANES), lambda l: (0, 0)), pl.BlockSpec((None, SUBLANES, nw), lambda l: (l, 0, 0)),
                  blk, blk, blk],
        out_specs=[blk] * 4, out_shape=[jax.ShapeDtypeStruct(w.shape, F32)] * 4,
        compiler_params=_cp(("arbitrary",)),
    )(ca_t, dm_sh, w, m, v)


def _exchange(copies, name, out_type, n_sems, args, sequencer, after=None):
    order = [] if after is None else [after]
    n_in, n_out = len(args) + len(order), len(out_type)

    def body(*refs):
        barrier = pltpu.get_barrier_semaphore()
        peers = sequencer[1](*_pos())
        for peer in peers:
            pl.semaphore_signal(barrier, inc=1, device_id=peer, device_id_type=MESH)
        pl.semaphore_wait(barrier, len(peers))
        copies(refs[:n_in], refs[n_in:n_in + n_out], refs[n_in + n_out], refs[n_in + n_out + 1])

    sems = [pltpu.SemaphoreType.DMA((n_sems,))] * 2
    return pl.kernel(body, out_type, mesh=plsc.ScalarSubcoreMesh(axis_name="sequencer", num_cores=1), name=name,
                     scratch_types=sems, compiler_params=pltpu.CompilerParams(collective_id=sequencer[0]))(*args, *order)


def _sibling(x, y, c):
    return [(x, y, 1 - c)]


def _other_chips(x, y, c):
    return [(1 - x, y, c), (x, 1 - y, c), (1 - x, 1 - y, c)]


def _to_wire(g, name, after=None):
    _, rr, cc = g.shape
    rb = min(rr, 256)

    def body(g_ref, *rest):
        rest[-1][...] = g_ref[...].astype(GRAD_WIRE_DTYPE)

    order = [] if after is None else [after]
    blk = pl.BlockSpec((None, rb, cc), lambda k, j: (k, j, 0))
    return pl.pallas_call(
        body, name=name, grid=(N_CHIP, rr // rb), in_specs=[blk] + [ANY] * len(order), out_specs=blk,
        out_shape=jax.ShapeDtypeStruct(g.shape, GRAD_WIRE_DTYPE), compiler_params=_cp(("parallel", "parallel")),
    )(g, *order)


def _chip_scatter(ps, name, collective_id, after=None):
    n = len(ps)

    def copies(ins, outs, ssem, rsem):
        x, y, c = _pos()
        cps = []
        for a in range(n):
            for q, (fx, fy) in enumerate(((1, 0), (0, 1), (1, 1))):
                px, py = _flip(x, fx), _flip(y, fy)
                cp = pltpu.make_async_remote_copy(
                    src_ref=ins[a].at[2 * px + py], dst_ref=outs[a].at[q],
                    send_sem=ssem.at[3 * a + q], recv_sem=rsem.at[3 * a + q], device_id=(px, py, c), device_id_type=MESH)
                cp.start()
                cps.append(cp)
        for cp in cps:
            cp.wait()

    out_type = [jax.ShapeDtypeStruct((N_CHIP - 1,) + p.shape[1:], p.dtype) for p in ps]
    return _exchange(copies, name, out_type, 3 * n, ps, (collective_id, _other_chips), after)


def _add_owner(p, got, chipidx, name, after=None):
    _, hr, cc = p.shape
    rb = min(hr, 256)

    def body(k_ref, p_ref, r_ref, *rest):
        rest[-1][...] = ((p_ref[...].astype(F32) + r_ref[0].astype(F32)) + r_ref[1].astype(F32)) + r_ref[2].astype(F32)

    order = [] if after is None else [after]
    return pl.pallas_call(
        body, name=name,
        grid_spec=pltpu.PrefetchScalarGridSpec(
            num_scalar_prefetch=1, grid=(hr // rb,),
            in_specs=[pl.BlockSpec((None, rb, cc), lambda j, k_ref: (k_ref[0], j, 0)),
                      pl.BlockSpec((N_CHIP - 1, rb, cc), lambda j, k_ref: (0, j, 0))] + [ANY] * len(order),
            out_specs=pl.BlockSpec((rb, cc), lambda j, k_ref: (j, 0))),
        out_shape=jax.ShapeDtypeStruct((hr, cc), F32),
        compiler_params=_cp(("parallel",)),
    )(chipidx, p, got, *order)


def _sib_exchange(ts_, name, collective_id, after=None):
    n = len(ts_)

    def copies(ins, outs, ssem, rsem):
        x, y, c = _pos()
        cps = []
        for a in range(n):
            cp = pltpu.make_async_remote_copy(src_ref=ins[a], dst_ref=outs[a], send_sem=ssem.at[a],
                                              recv_sem=rsem.at[a], device_id=(x, y, 1 - c), device_id_type=MESH)
            cp.start()
            cps.append(cp)
        for cp in cps:
            cp.wait()

    out_type = [jax.ShapeDtypeStruct(t.shape, F32) for t in ts_]
    return _exchange(copies, name, out_type, n, ts_, (collective_id, _sibling), after)


def _adam_2d(w, g_own, g_sib, m, v, name):
    rr, cc = w.shape
    rb = min(rr, 256)

    def body(w_ref, go_ref, gs_ref, m_ref, v_ref, g_out, d_out, m_out, v_out):
        g = go_ref[...] + gs_ref[...]
        dl, m2, v2 = _adam(w_ref[...], g, m_ref[...], v_ref[...])
        g_out[...] = g
        d_out[...] = dl
        m_out[...] = m2
        v_out[...] = v2

    blk = pl.BlockSpec((rb, cc), lambda j: (j, 0))
    return pl.pallas_call(
        body, name=name, grid=(rr // rb,), in_specs=[blk] * 5, out_specs=[blk] * 4,
        out_shape=[jax.ShapeDtypeStruct((rr, cc), F32)] * 4, compiler_params=_cp(("parallel",)),
    )(w, g_own, g_sib, m, v)


def kernel(x, c, norm_g, mod_w, mod_b, hy_w_in, hy_conv_w, hy_conv_b, lru_w_a, lru_b_a, lru_w_x, lru_b_x, lru_lambda, sc_conv_w, hy_w_out, pool_w_in, pool_w_grp, pool_b_grp, pool_scale, pool_w_out, final_g, loss_target, m_norm_g, m_mod_w, m_mod_b, m_hy_w_in, m_hy_conv_w, m_hy_conv_b, m_lru_w_a, m_lru_b_a, m_lru_w_x, m_lru_b_x, m_lru_lambda, m_sc_conv_w, m_hy_w_out, m_pool_w_in, m_pool_w_grp, m_pool_b_grp, m_pool_scale, m_pool_w_out, m_final_g, v_norm_g, v_mod_w, v_mod_b, v_hy_w_in, v_hy_conv_w, v_hy_conv_b, v_lru_w_a, v_lru_b_a, v_lru_w_x, v_lru_b_x, v_lru_lambda, v_sc_conv_w, v_hy_w_out, v_pool_w_in, v_pool_w_grp, v_pool_b_grp, v_pool_scale, v_pool_w_out, v_final_g):
    ax, ay, ac = _pos()
    me = 4 * ax + 2 * ay + ac
    chip = 2 * ax + ay
    xs = x[0]
    tgt = loss_target[0]
    gd = POOL_GROUP_DIM
    kidx = chip.reshape(1).astype(jnp.int32)

    big = [hy_w_in[0], hy_w_out[0], pool_w_in[0], pool_w_grp[0].reshape(4 * 128, gd), pool_w_out[0]]
    w_in0, w_out0 = _wgather_sequencer(
        [_wcast_own_block(w, kidx, f"wcast_own_block_{a}") for a, w in enumerate(big[:2])], "wgather_l0", CIDS_WGATHER[0])

    ca_all, mod_all, small_w = _mod_fwd(jnp.broadcast_to(c, (SUBLANES, D)), mod_w, mod_b,
                                        hy_conv_w[0], sc_conv_w[0], pool_b_grp, pool_scale)
    mod_me = lax.dynamic_index_in_dim(mod_all, me, axis=1, keepdims=False)
    sh0, sc0, gt0 = (mod_me[0:1, k * D:(k + 1) * D] for k in range(3))
    sh1, sc1, gt1 = (mod_me[1:2, k * D:(k + 1) * D] for k in range(3))
    cw = small_w[SW_CONV:SW_CONV + 4, 0:D]
    sw = small_w[SW_SC:SW_SC + 3, 0:D]
    pool_b = small_w[SW_POOL_B:SW_POOL_B + 1, :]
    pool_s = small_w[SW_POOL_S:SW_POOL_S + 1, :]
    g0, g1, gf = norm_g[0:1], norm_g[1:2], final_g.reshape(1, D)
    cb, ba, bx, lam = hy_conv_b, lru_b_a, lru_b_x, lru_lambda

    w_in1, w_grp, w_out1 = _wgather_sequencer(
        [_wcast_own_block(w, kidx, f"wcast_own_block_{a + 2}", after=(w_out0, small_w)) for a, w in enumerate(big[2:])],
        "wgather_l1", CIDS_WGATHER[1])
    w_grp =w_grp.reshape(N_CHIP, 4, 128, gd).transpose(1, 0, 2, 3).reshape(4, gd, gd)
    wa_b, wx_b = _wcast([lru_w_a[0], lru_w_x[0]])

    x1, hst, y0, xc0, cz0, h0, proj0 = _l0_fwd(xs, g0, sc0, sh0, w_in0, gt0, cw, cb, wa_b, ba, wx_b, bx, lam, sw,
                                               w_out0.reshape(2 * D, D))
    dpool, mixed, y1, dx2, losscols, dgf, h1, proj1 = _l1_fwd(x1, g1, sc1, sh1, w_in1, tgt, gt1, w_grp, pool_b, pool_s,
                                                              w_out1.reshape(2 * D, D), gf)

    def add_owners(grads, got, tag, ids, after):
        own = []
        for a, (g, r) in enumerate(zip(grads, got)):
            own.append(_add_owner(g, r, kidx, f"grad_add_owner_{tag}{a}", own[-1] if own else after))
        return own, _sib_exchange(own, f"grad_sib_exchange_{tag}", ids[1])

    dproj1, mt1, d_wgrp, dsc1, dbg1 = _l1_bwd_mix(dx2, proj1, mixed, y1, dpool, gt1, w_grp, pool_s,
                                                  w_out1.reshape(2 * D, D))
    d_win1, wire_win1 = _wgrad(h1, dproj1, N_CHIP, D, D, lambda g: 0, lambda g: g, "l1_wgrad_in")
    d_wout1, wire_wout1, dgate1 = _wo_final(mt1, w_out1, gt1, "l1_wo_final")
    d_wgrp = d_wgrp.reshape(4, N_CHIP, 128, gd).transpose(1, 0, 2, 3).reshape(N_CHIP, 4 * 128, gd)
    grads_l1 = [d_win1, d_wgrp, d_wout1]
    got_l1 = _chip_scatter([wire_win1, _to_wire(d_wgrp, "grad_to_wire_grp"), wire_wout1], "grad_chip_scatter_l1",
                           CIDS_L1[0])
    dx1, s1_1, s2_1 = _dgrad_norm(dproj1, w_in1, x1, dx2, g1, sc1, "l1_bwd_proj")

    dproj0, mt0, d_wa, d_wx, sm0 = _l0_bwd_mix(dx1, proj0, hst, y0, xc0, cz0, gt0, cw, wa_b, ba, wx_b, bx, lam, sw,
                                               w_out0.reshape(2 * D, D))
    sums_l1, sib_l1 = add_owners(grads_l1, got_l1, "l1", CIDS_L1, after=sm0)
    d_win0, wire_win0 = _wgrad(h0, dproj0, N_CHIP, D, 6 * D // N_CHIP, lambda g: 0, lambda g: g, "l0_wgrad_in",
                               after=sums_l1[-1])
    d_wout0, wire_wout0, dgate0 = _wo_final(mt0, w_out0, gt0, "l0_wo_final")
    grads_l0 = [d_win0, d_wout0]
    got_l0 = _chip_scatter([wire_win0, wire_wout0], "grad_chip_scatter_l0", CIDS_L0[0], after=sib_l1[0])
    grad_x, s1_0, s2_0 = _dgrad_norm(dproj0, w_in0, xs, dx1, g0, sc0, "l0_bwd_proj", after=wire_win0)
    sums_l0, sib_l0 = add_owners(grads_l0, got_l0, "l0", CIDS_L0, after=s1_0)

    buf_a, dmod8 = _small_pack(s1_0, s2_0, s1_1, s2_1, sm0, dsc1, dbg1, dgf, losscols, dgate0, dgate1,
                                      norm_g, sc0, sc1, lam)
    hw = LRU_HEADS * LRU_HEAD_DIM
    buf_b = jnp.concatenate([d_wa.reshape(hw, LRU_HEAD_DIM), d_wx.reshape(hw, LRU_HEAD_DIM)], axis=0)
    red_a, red_b, dm_all = _small_comm(buf_a, buf_b, dmod8)
    small = [(norm_g, m_norm_g, v_norm_g), (mod_b, m_mod_b, v_mod_b),
             (hy_conv_w[0], m_hy_conv_w[0], v_hy_conv_w[0]), (hy_conv_b, m_hy_conv_b, v_hy_conv_b),
             tuple(a.reshape(hw, LRU_HEAD_DIM) for a in (lru_w_a, m_lru_w_a, v_lru_w_a)),
             (lru_b_a, m_lru_b_a, v_lru_b_a),
             tuple(a.reshape(hw, LRU_HEAD_DIM) for a in (lru_w_x, m_lru_w_x, v_lru_w_x)),
             (lru_b_x, m_lru_b_x, v_lru_b_x), (lru_lambda, m_lru_lambda, v_lru_lambda),
             (sc_conv_w[0], m_sc_conv_w[0], v_sc_conv_w[0]), (pool_b_grp, m_pool_b_grp, v_pool_b_grp),
             (pool_scale, m_pool_scale, v_pool_scale),
             tuple(a.reshape(1, D) for a in (final_g, m_final_g, v_final_g))]
    small_names = ["norm_g", "mod_b", "hy_conv_w", "hy_conv_b", "lru_w_a", "lru_b_a", "lru_w_x", "lru_b_x",
                   "lru_lambda", "sc_conv_w", "pool_b_grp", "pool_scale", "final_g"]
    small_out = _small_adam(red_a, red_b, dm_all, small)
    res = {}
    shapes = dict(norm_g=norm_g, mod_b=mod_b, hy_conv_w=hy_conv_w, hy_conv_b=hy_conv_b, lru_w_a=lru_w_a, lru_b_a=lru_b_a,
                  lru_w_x=lru_w_x, lru_b_x=lru_b_x, lru_lambda=lru_lambda, sc_conv_w=sc_conv_w, pool_b_grp=pool_b_grp,
                  pool_scale=pool_scale, final_g=final_g)
    for p, nm in enumerate(small_names):
        res[nm] = tuple(o.reshape(shapes[nm].shape) for o in small_out[4 * p:4 * p + 4])

    nw = mod_w.shape[2]
    dm_sh = jnp.stack([lax.dynamic_slice_in_dim(dm_all[:, l * 3 * D:(l + 1) * 3 * D], chip * nw, nw, axis=1)
                       for l in range(2)])
    res["mod_w"] = tuple(_modw_adam(ca_all.T, dm_sh, mod_w, m_mod_w, v_mod_w))

    sums = list(sums_l0) + list(sums_l1)
    sib_sums = list(sib_l0) + list(sib_l1)
    big_names = ["hy_w_in", "hy_w_out", "pool_w_in", "pool_w_grp", "pool_w_out"]
    big_wmv = [(hy_w_in, m_hy_w_in, v_hy_w_in), (hy_w_out, m_hy_w_out, v_hy_w_out), (pool_w_in, m_pool_w_in, v_pool_w_in),
               (pool_w_grp, m_pool_w_grp, v_pool_w_grp), (pool_w_out, m_pool_w_out, v_pool_w_out)]
    for a, nm in enumerate(big_names):
        rr, cc = big[a].shape
        w, m, v = (t.reshape(rr, cc) for t in big_wmv[a])
        outs = _adam_2d(w, sums[a], sib_sums[a], m, v, f"adam_{nm}")
        res[nm] = tuple(o.reshape(big_wmv[a][0].shape) for o in outs)

    loss = red_a[ROW_LOSS, 0]
    order = ["norm_g", "mod_w", "mod_b", "hy_w_in", "hy_conv_w", "hy_conv_b", "lru_w_a", "lru_b_a", "lru_w_x", "lru_b_x",
             "lru_lambda", "sc_conv_w", "hy_w_out", "pool_w_in", "pool_w_grp", "pool_b_grp", "pool_scale", "pool_w_out",
             "final_g"]
    return (loss, grad_x[None], *[res[nm][0] for nm in order], *[res[nm][1] for nm in order],
            *[res[nm][2] for nm in order], *[res[nm][3] for nm in order])
```

```python
import jax
import jax.numpy as jnp
from jax import lax
from jax.experimental import pallas as pl
from jax.experimental.pallas import tpu as pltpu
from jax.experimental.pallas import tpu_sc as plsc

F32, BF16 = jnp.float32, jnp.bfloat16
D = 1024
RMS_EPS = 1e-6
SQRT_FLOOR = 1e-30
LRU_C = 8.0
LRU_HEADS, LRU_HEAD_DIM = 8, 128
POOL_WINDOWS = (2, 4, 8, 16)
POOL_GROUP_DIM = 512
ADAM_LR, ADAM_B1, ADAM_B2, ADAM_EPS, ADAM_WD, ADAM_STEP = 0.001, 0.9, 0.999, 1e-08, 0.01, 10
MESH = pl.DeviceIdType.MESH
CIDS_WGATHER = (1, 8)
CIDS_L1 = (2, 3)
CIDS_L0 = (4, 5)
N_DEV, N_CHIP = 8, 4
SUBLANES = 8
BF16_ROWS = 16
POOL_HALO = 16
TS_MIX, TS_WGRAD, TS_DGRAD = 256, 2048, 512
SMALL_ROWS = 64
GRAD_WIRE_DTYPE = BF16
ANY = pl.BlockSpec(memory_space=pl.ANY)
VMEM = pl.BlockSpec(memory_space=pltpu.VMEM)
NT = (((1,), (1,)), ((), ()))
TN = (((0,), (0,)), ((), ()))


def _cp(sem=None, vmem_mb=56):
    kw = dict(vmem_limit_bytes=vmem_mb * 2 ** 20)
    if sem is not None:
        kw["dimension_semantics"] = sem
    return pltpu.CompilerParams(**kw)


def _tile(n, t):
    return min(n, t)


def _pos():
    return lax.axis_index("x"), lax.axis_index("y"), lax.axis_index("c")


def _flip(v, f):
    return 1 - v if f else v


def _sigmoid(z):
    return 0.5 * jnp.tanh(0.5 * z) + 0.5


def _rows(n, c):
    return lax.broadcasted_iota(jnp.int32, (n, c), 0)


def _down(a, d):
    return a if d == 0 else pltpu.roll(a, d, 0)


def _up(a, d):
    return a if d == 0 else pltpu.roll(a, a.shape[0] - d, 0)


def _scan_fwd_steps(a, u, carry):
    n, c = a.shape
    sub = _rows(SUBLANES, c)
    out = []
    for k in range(n // SUBLANES):
        p = a[k * SUBLANES:(k + 1) * SUBLANES]
        g = u[k * SUBLANES:(k + 1) * SUBLANES]
        for d in (1, 2, 4):
            keep = sub >= d
            g = g + p * jnp.where(keep, pltpu.roll(g, d, 0), 0.0)
            p = p * jnp.where(keep, pltpu.roll(p, d, 0), 1.0)
        h = g + p * carry
        carry = h[SUBLANES - 1:SUBLANES, :]
        out.append(h)
        yield
    return jnp.concatenate(out, axis=0)


def _scan_rev_steps(alpha, b, carry):
    n, c = alpha.shape
    sub = _rows(SUBLANES, c)
    out = []
    for k in reversed(range(n // SUBLANES)):
        p = alpha[k * SUBLANES:(k + 1) * SUBLANES]
        g = b[k * SUBLANES:(k + 1) * SUBLANES]
        for d in (1, 2, 4):
            keep = sub < SUBLANES - d
            g = g + p * jnp.where(keep, pltpu.roll(g, SUBLANES - d, 0), 0.0)
            p = p * jnp.where(keep, pltpu.roll(p, SUBLANES - d, 0), 1.0)
        h = g + p * carry
        carry = h[0:1, :]
        out.append(h)
        yield
    return jnp.concatenate(out[::-1], axis=0)


def _run(steps):
    while True:
        try:
            next(steps)
        except StopIteration as done:
            return done.value


def _paired(progress, pieces):
    n, done = len(pieces), 1
    pieces[0]()
    for frac in progress:
        while done < n and done <= frac * n:
            pieces[done]()
            done += 1
    while done < n:
        pieces[done]()
        done += 1


def _conv_taps(ext, halo, n, width):
    return [_down(ext, width - 1 - k)[halo:halo + n] for k in range(width)]


def _lru_gates(xc, wa_ref, ba, wx_ref, bx):
    xb = xc.astype(BF16)
    pa, px = [], []
    for h in range(LRU_HEADS):
        xh = xb[:, h * LRU_HEAD_DIM:(h + 1) * LRU_HEAD_DIM]
        pa.append(jnp.dot(xh, wa_ref[h], preferred_element_type=F32))
        px.append(jnp.dot(xh, wx_ref[h], preferred_element_type=F32))
    r = _sigmoid(jnp.concatenate(pa, axis=1) + ba)
    ig = _sigmoid(jnp.concatenate(px, axis=1) + bx)
    return r, ig


def _softplus_neg(lam):
    return jnp.maximum(-lam, 0.0) + jnp.log1p(jnp.exp(-jnp.abs(lam)))


def _recip_1_to_2(d):
    r0 = pl.reciprocal(d, approx=True)
    return r0 * (2.0 - d * r0)


def _lru_decay(r, sp, first):
    big_l = (-LRU_C) * r * sp
    a = jnp.exp(big_l)
    th = jnp.tanh(big_l)
    q = (-2.0 * th) * _recip_1_to_2(1.0 - th)
    rs = lax.rsqrt(jnp.maximum(q, SQRT_FLOOR))
    return a, jnp.where(first, 1.0, q * rs), rs


def _pool_inv_counts(t0, n):
    t = (t0 + lax.broadcasted_iota(jnp.int32, (n, 1), 0) + 1).astype(F32)
    return [1.0 / jnp.minimum(t, float(w)) for w in POOL_WINDOWS]


def _window_sums(ext, shift):
    gd = POOL_GROUP_DIM
    out = []
    s = ext
    for k in range(len(POOL_WINDOWS)):
        s = s + shift(s, 2 ** k)
        out.append(s[:, 0:gd])
        if k + 1 < len(POOL_WINDOWS):
            s = s[:, gd:]
    return out


SW_ROWS, SW_COLS = 16, 2 * D
SW_CONV, SW_SC, SW_POOL_B, SW_POOL_S = 0, 4, 8, 9


def _mod_fwd(c8, mod_w, mod_b, conv_w, sc_w, pool_b, pool_s):
    nw = mod_w.shape[2]
    cq, pq = conv_w.shape[1], pool_b.shape[1]

    def body(c_ref, w_ref, b_ref, cw_ref, sw_ref, pb_ref, ps_ref, ca_ref, mod_ref, small_ref,
             cslot, mslot, msend, pslot, psend, s1, r1, s2, r2, s3, r3):
        x, y, c = _pos()
        me = 4 * x + 2 * y + c
        chip = 2 * x + y
        first = []
        for r in range(1, N_DEV):
            fx, fy, fc = (r >> 2) & 1, (r >> 1) & 1, r & 1
            cp = pltpu.make_async_remote_copy(
                src_ref=c_ref, dst_ref=cslot.at[me], send_sem=s1.at[r - 1], recv_sem=r1.at[r - 1],
                device_id=(_flip(x, fx), _flip(y, fy), _flip(c, fc)), device_id_type=MESH)
            cp.start()
            first.append(cp)
        cslot[me] = c_ref[...]
        for cp in first:
            cp.wait()
        rows = _rows(SUBLANES, D)
        call = jnp.zeros((SUBLANES, D), F32)
        for d in range(N_DEV):
            call = jnp.where(rows == d, cslot[d], call)
        ca = call * _sigmoid(call)
        ca_ref[...] = ca
        for l in range(2):
            msend[l] = jnp.dot(ca, w_ref[l], precision=lax.Precision.HIGHEST, preferred_element_type=F32)
        psend[...] = jnp.zeros_like(psend)
        psend[SW_CONV:SW_CONV + 4, 0:cq] = cw_ref[...]
        psend[SW_SC:SW_SC + 3, 0:cq] = sw_ref[...]
        psend[SW_POOL_B:SW_POOL_B + 1, :] = pb_ref[...]
        psend[SW_POOL_S:SW_POOL_S + 1, :] = ps_ref[...]
        second = []
        for q, (fx, fy) in enumerate(((1, 0), (0, 1), (1, 1))):
            peer = (_flip(x, fx), _flip(y, fy), c)
            for src, dst, ss, rs in ((msend, mslot, s2, r2), (psend, pslot, s3, r3)):
                cp = pltpu.make_async_remote_copy(src_ref=src, dst_ref=dst.at[chip], send_sem=ss.at[q], recv_sem=rs.at[q],
                                                  device_id=peer, device_id_type=MESH)
                cp.start()
                second.append(cp)
        mslot[chip] = msend[...]
        pslot[chip] = psend[...]
        for cp in second:
            cp.wait()
        small_ref[...] = jnp.zeros_like(small_ref)
        for j in range(N_CHIP):
            for l in range(2):
                mod_ref[l, :, j * nw:(j + 1) * nw] = mslot[j, l] + b_ref[l:l + 1, j * nw:(j + 1) * nw]
            small_ref[0:SUBLANES, j * cq:(j + 1) * cq] = pslot[j, 0:SUBLANES, 0:cq]
            small_ref[SUBLANES:SW_ROWS, j * pq:(j + 1) * pq] = pslot[j, SUBLANES:SW_ROWS, :]

    args = (c8, mod_w, mod_b, conv_w, sc_w, pool_b, pool_s)
    dma3 = pltpu.SemaphoreType.DMA((N_CHIP - 1,))
    return pl.pallas_call(
        body, name="mod_fwd",
        in_specs=[VMEM] * len(args), out_specs=[VMEM] * 3,
        out_shape=[jax.ShapeDtypeStruct((SUBLANES, D), F32), jax.ShapeDtypeStruct((2, SUBLANES, N_CHIP * nw), F32),
                   jax.ShapeDtypeStruct((SW_ROWS, SW_COLS), F32)],
        scratch_shapes=[pltpu.VMEM((N_DEV, SUBLANES, D), F32), pltpu.VMEM((N_CHIP, 2, SUBLANES, nw), F32),
                        pltpu.VMEM((2, SUBLANES, nw), F32), pltpu.VMEM((N_CHIP, SW_ROWS, pq), F32),
                        pltpu.VMEM((SW_ROWS, pq), F32),
                        pltpu.SemaphoreType.DMA((N_DEV - 1,)), pltpu.SemaphoreType.DMA((N_DEV - 1,)),
                        dma3, dma3, dma3, dma3],
        compiler_params=_cp(),
    )(*args)


def _wcast(ws):
    def body(*refs):
        n = len(refs) // 2
        for a in range(n):
            refs[n + a][...] = refs[a][...].astype(BF16)

    return pl.pallas_call(
        body, name="wcast", in_specs=[VMEM] * len(ws), out_specs=[VMEM] * len(ws),
        out_shape=[jax.ShapeDtypeStruct(w.shape, BF16) for w in ws], compiler_params=_cp(),
    )(*ws)


def _wcast_own_block(w, kidx, name, after=()):
    rr, cc = w.shape
    rb = min(rr, 256)

    def body(k_ref, w_ref, *rest):
        rest[-1][...] = w_ref[...].astype(BF16)

    order = list(after)
    return pl.pallas_call(
        body, name=name,
        grid_spec=pltpu.PrefetchScalarGridSpec(
            num_scalar_prefetch=1, grid=(rr // rb,),
            in_specs=[pl.BlockSpec((rb, cc), lambda j, k_ref: (j, 0))] + [ANY] * len(order),
            out_specs=pl.BlockSpec((None, rb, cc), lambda j, k_ref: (k_ref[0], j, 0))),
        out_shape=jax.ShapeDtypeStruct((N_CHIP, rr, cc), BF16),
        compiler_params=_cp(("parallel",)),
    )(kidx, w, *order)


def _wgather_copies(outs, rows, ssem, rsem, fssem, frsem):
    n = len(outs)
    x, y, c = _pos()
    chip = 2 * x + y
    sib = (x, y, 1 - c)
    flips = ((1, 0), (0, 1), (1, 1))

    def half(a, which):
        hr = rows[a] // 2
        return pl.ds(pl.multiple_of(which * hr, BF16_ROWS), hr)

    sends = []
    for a in range(n):
        mine = outs[a].at[chip, half(a, c), :]
        for q, (fx, fy) in enumerate(flips):
            cp = pltpu.make_async_remote_copy(
                src_ref=mine, dst_ref=mine, send_sem=ssem.at[3 * a + q], recv_sem=rsem.at[3 * a + q],
                device_id=(_flip(x, fx), _flip(y, fy), c), device_id_type=MESH)
            cp.start()
            sends.append(cp)
    passed = []
    for a in range(n):
        for q, (fx, fy) in enumerate(flips):
            src_chip = 2 * _flip(x, fx) + _flip(y, fy)
            landed = outs[a].at[src_chip, half(a, c), :]
            pltpu.make_async_remote_copy(
                src_ref=landed, dst_ref=landed, send_sem=ssem.at[3 * a + q], recv_sem=rsem.at[3 * a + q],
                device_id=sib, device_id_type=MESH).wait_recv()
            cp = pltpu.make_async_remote_copy(
                src_ref=landed, dst_ref=landed, send_sem=fssem.at[3 * a + q], recv_sem=frsem.at[3 * a + q],
                device_id=sib, device_id_type=MESH)
            cp.start()
            passed.append(cp)
    for a in range(n):
        for q, (fx, fy) in enumerate(flips):
            src_chip = 2 * _flip(x, fx) + _flip(y, fy)
            other = outs[a].at[src_chip, half(a, 1 - c), :]
            pltpu.make_async_remote_copy(
                src_ref=other, dst_ref=other, send_sem=fssem.at[3 * a + q], recv_sem=frsem.at[3 * a + q],
                device_id=sib, device_id_type=MESH).wait_recv()
    for cp in sends + passed:
        cp.wait_send()


def _wgather_sequencer(bufs, name, collective_id):
    n = len(bufs)
    refs = [jax.new_ref(b, memory_space=pltpu.MemorySpace.HBM) for b in bufs]
    dma = pltpu.SemaphoreType.DMA((3 * n,))

    @pl.kernel(mesh=plsc.ScalarSubcoreMesh(axis_name="sequencer", num_cores=1), name=name,
               scratch_types=(dma, dma, dma, dma), compiler_params=pltpu.CompilerParams(collective_id=collective_id))
    def launch(ssem, rsem, fssem, frsem):
        x, y, c = _pos()
        barrier = pltpu.get_barrier_semaphore()
        for peer in ((1 - x, y, c), (x, 1 - y, c), (1 - x, 1 - y, c), (x, y, 1 - c)):
            pl.semaphore_signal(barrier, inc=1, device_id=peer, device_id_type=MESH)
        pl.semaphore_wait(barrier, 4)
        _wgather_copies(refs, [b.shape[1] for b in bufs], ssem, rsem, fssem, frsem)

    launch()
    return [r[...] for r in refs]


def _l0_fwd(x, g, sc, sh, w_in, gate, cw, cb, wa, ba, wx, bx, lam, sw, wo):
    s_len, nb = x.shape[0], w_in.shape[2]
    ts = _tile(s_len, TS_MIX)
    n_t = s_len // ts
    hl = SUBLANES

    def body(xa_ref, xb_ref, g_ref, sc_ref, sh_ref, win_ref, gate_ref, cw_ref, cb_ref, wa_ref, ba_ref, wx_ref, bx_ref,
             lam_ref, sw_ref, wo_ref, x1_ref, h_ref, y_ref, xc_ref, cz_ref, h0_ref, p_ref, pcur, pnext, cxa, czz, chh):
        i = pl.program_id(0)

        @pl.when(i == 0)
        def _():
            cxa[...] = jnp.zeros_like(cxa)
            czz[...] = jnp.zeros_like(czz)
            chh[...] = jnp.zeros_like(chh)
            pnext[...] = jnp.zeros_like(pnext)

        pcur[...] = pnext[...]
        xv = xa_ref[...]
        rinv = lax.rsqrt(jnp.mean(xv * xv, axis=-1, keepdims=True) + RMS_EPS)
        h0 = (xv * rinv * (g_ref[...] * (1.0 + sc_ref[...])) + sh_ref[...]).astype(BF16)
        h0_ref[...] = h0

        def project(k):
            def emit():
                pk = jnp.dot(h0, win_ref[k], preferred_element_type=F32).astype(BF16)
                p_ref[:, k * nb:(k + 1) * nb] = pk
                pnext[:, k * nb:(k + 1) * nb] = pk
            return emit

        def mixer():
            piece = lambda k: pcur[:, k * D:(k + 1) * D].astype(F32)
            xa = piece(0)
            rows = _rows(ts, D)
            taps = _conv_taps(jnp.concatenate([cxa[...], xa], axis=0), hl, ts, 4)
            xc = cb_ref[...] + sum(cw_ref[k:k + 1, :] * taps[k] for k in range(4))
            xc_ref[...] = xc.astype(BF16)
            r, ig = _lru_gates(xc, wa_ref, ba_ref[...], wx_ref, bx_ref[...])
            a, m, _ = _lru_decay(r, _softplus_neg(lam_ref[...]), (rows == 0) & (i == 1))
            yield 0.26
            h = _run(_scan_fwd_steps(a, m * ig * xc, chh[hl - 1:hl, :]))
            yield 0.51
            gcp, v = piece(3), piece(4)
            z = gcp * v
            ztaps = _conv_taps(jnp.concatenate([czz[...], z], axis=0), hl, ts, 3)
            cz = sum(sw_ref[k:k + 1, :] * ztaps[k] for k in range(3))
            cz_ref[...] = cz.astype(BF16)
            yb = piece(2) * cz
            ga, gb = piece(1), piece(5)
            y = jnp.concatenate([h * (ga * _sigmoid(ga)), yb * (gb * _sigmoid(gb))], axis=1).astype(BF16)
            yield 0.76
            y_ref[...] = y
            x1_ref[...] = xb_ref[...] + gate_ref[...] * jnp.dot(y, wo_ref[...], preferred_element_type=F32)
            h_ref[...] = h.astype(BF16)
            cxa[...] = xa[ts - hl:, :]
            czz[...] = z[ts - hl:, :]
            chh[...] = jnp.where(i > 0, h[ts - hl:, :], 0.0)

        _paired(mixer(), [project(k) for k in range(N_CHIP)])

    def full(a):
        return pl.BlockSpec(a.shape, lambda i: (0,) * a.ndim)

    ahead = lambda w: pl.BlockSpec((ts, w), lambda i: (jnp.minimum(i, n_t - 1), 0))
    behind = lambda w: pl.BlockSpec((ts, w), lambda i: (jnp.maximum(i - 1, 0), 0))
    args = (x, x, g, sc, sh, w_in, gate, cw, cb, wa, ba, wx, bx, lam, sw, wo)
    return pl.pallas_call(
        body, name="l0_fwd", grid=(n_t + 1,),
        in_specs=[ahead(D), behind(D)] + [full(a) for a in args[2:]],
        out_specs=[behind(D), behind(D), behind(2 * D), behind(D), behind(D), ahead(D), ahead(N_CHIP * nb)],
        out_shape=[jax.ShapeDtypeStruct((s_len, D), F32), jax.ShapeDtypeStruct((s_len, D), BF16),
                   jax.ShapeDtypeStruct((s_len, 2 * D), BF16), jax.ShapeDtypeStruct((s_len, D), BF16),
                   jax.ShapeDtypeStruct((s_len, D), BF16), jax.ShapeDtypeStruct((s_len, D), BF16),
                   jax.ShapeDtypeStruct((s_len, N_CHIP * nb), BF16)],
        scratch_shapes=[pltpu.VMEM((ts, N_CHIP * nb), BF16)] * 2 + [pltpu.VMEM((hl, D), F32)] * 3,
        compiler_params=_cp(("arbitrary",)),
    )(*args)


def _l1_fwd(x1, g, sc, sh, w_in, tgt, gate, wg, bg, scale, wo, gf):
    s_len, nb = x1.shape[0], w_in.shape[2]
    ts = _tile(s_len, TS_MIX)
    n_t = s_len // ts
    pw, gd, hl = 2 * D, POOL_GROUP_DIM, POOL_HALO

    def body(xa_ref, xb_ref, t_ref, g_ref, sc_ref, sh_ref, win_ref, gate_ref, wg_ref, bg_ref, scl_ref, wo_ref, gf_ref,
             d_ref, mx_ref, y_ref, dx_ref, loss_ref, dgf_ref, h1_ref, p_ref, pcur, pnext, cv):
        i = pl.program_id(0)

        @pl.when(i == 0)
        def _():
            cv[...] = jnp.zeros_like(cv)
            loss_ref[...] = jnp.zeros_like(loss_ref)
            dgf_ref[...] = jnp.zeros_like(dgf_ref)
            pnext[...] = jnp.zeros_like(pnext)

        pcur[...] = pnext[...]
        xv = xa_ref[...]
        rinv = lax.rsqrt(jnp.mean(xv * xv, axis=-1, keepdims=True) + RMS_EPS)
        h1 = (xv * rinv * (g_ref[...] * (1.0 + sc_ref[...])) + sh_ref[...]).astype(BF16)
        h1_ref[...] = h1

        def project(k):
            def emit():
                pk = jnp.dot(h1, win_ref[k], preferred_element_type=F32).astype(BF16)
                p_ref[:, k * nb:(k + 1) * nb] = pk
                pnext[:, k * nb:(k + 1) * nb] = pk
            return emit

        def mixer():
            v = pcur[:, 0:pw].astype(F32)
            sums = _window_sums(jnp.concatenate([cv[...], v], axis=0), _down)
            inv = _pool_inv_counts(jnp.maximum(i - 1, 0) * ts, ts)
            dd = [sums[k][hl:hl + ts] * inv[k] - v[:, k * gd:(k + 1) * gd] for k in range(4)]
            d_ref[...] = jnp.concatenate(dd, axis=1).astype(BF16)
            yield 0.26
            mixed = jnp.concatenate(
                [jnp.dot(dd[k].astype(BF16), wg_ref[k], preferred_element_type=F32) for k in range(4)], axis=1) + bg_ref[...]
            mx_ref[...] = mixed.astype(BF16)
            gg = pcur[:, pw:2 * pw].astype(F32)
            y = (mixed * scl_ref[...] * (gg * _sigmoid(gg))).astype(BF16)
            y_ref[...] = y
            yield 0.51
            x2 = xb_ref[...] + gate_ref[...] * jnp.dot(y, wo_ref[...], preferred_element_type=F32)
            yield 0.76
            r2 = lax.rsqrt(jnp.mean(x2 * x2, axis=-1, keepdims=True) + RMS_EPS)
            n2 = x2 * r2
            err = n2 * gf_ref[...] - t_ref[...]
            loss_ref[...] += jnp.where(i > 0, jnp.sum(err * err, axis=0, keepdims=True), 0.0)
            dyf = err * (1.0 / D)
            dgf_ref[...] += jnp.where(i > 0, jnp.sum(dyf * n2, axis=0, keepdims=True), 0.0)
            dn = dyf * gf_ref[...]
            dx_ref[...] = r2 * (dn - n2 * jnp.mean(dn * n2, axis=-1, keepdims=True))
            cv[...] = v[ts - hl:, :]

        _paired(mixer(), [project(k) for k in range(N_CHIP)])

    def full(a):
        return pl.BlockSpec(a.shape, lambda i: (0,) * a.ndim)

    ahead = lambda w: pl.BlockSpec((ts, w), lambda i: (jnp.minimum(i, n_t - 1), 0))
    behind = lambda w: pl.BlockSpec((ts, w), lambda i: (jnp.maximum(i - 1, 0), 0))
    acc = pl.BlockSpec((1, D), lambda i: (0, 0))
    args = (x1, x1, tgt, g, sc, sh, w_in, gate, wg, bg, scale, wo, gf)
    return pl.pallas_call(
        body, name="l1_fwd", grid=(n_t + 1,),
        in_specs=[ahead(D), behind(D), behind(D)] + [full(a) for a in args[3:]],
        out_specs=[behind(pw), behind(pw), behind(pw), behind(D), acc, acc, ahead(D), ahead(N_CHIP * nb)],
        out_shape=[jax.ShapeDtypeStruct((s_len, pw), BF16)] * 3 + [jax.ShapeDtypeStruct((s_len, D), F32)]
        + [jax.ShapeDtypeStruct((1, D), F32)] * 2
        + [jax.ShapeDtypeStruct((s_len, D), BF16), jax.ShapeDtypeStruct((s_len, N_CHIP * nb), BF16)],
        scratch_shapes=[pltpu.VMEM((ts, N_CHIP * nb), BF16)] * 2 + [pltpu.VMEM((hl, pw), F32)],
        compiler_params=_cp(("arbitrary",)),
    )(*args)


def _l1_bwd_mix(dx2, proj, mixed, y, dpool, gate, wg, scale, wo):
    s_len = dx2.shape[0]
    ts = _tile(s_len, TS_MIX)
    n_t = s_len // ts
    pw, gd, hl = 2 * D, POOL_GROUP_DIM, POOL_HALO

    def body(dx_ref, gg_ref, mx_ref, y_ref, d_ref, gate_ref, wg_ref, sc_ref, wo_ref,
             dp_ref, mt_ref, dwg_ref, dsc_ref, dbg_ref, cq):
        i = pl.program_id(0)

        @pl.when(i == 0)
        def _():
            cq[...] = jnp.zeros_like(cq)
            dsc_ref[...] = jnp.zeros_like(dsc_ref)
            dbg_ref[...] = jnp.zeros_like(dbg_ref)
            mt_ref[...] = jnp.zeros_like(mt_ref)
            dwg_ref[...] = jnp.zeros_like(dwg_ref)

        dxv = dx_ref[...]
        dxb = dxv.astype(BF16)

        def wgrad_out(k):
            mt_ref[k] += lax.dot_general(y_ref[:, k * gd:(k + 1) * gd], dxb, TN, preferred_element_type=F32)

        dy = lax.dot_general((gate_ref[...] * dxv).astype(BF16), wo_ref[...], NT, preferred_element_type=F32)
        wgrad_out(0)
        gg = gg_ref[...].astype(F32)
        mixed = mx_ref[...].astype(F32)
        s = _sigmoid(gg)
        sg = gg * s
        dys = dy * sc_ref[...]
        dym = dy * mixed
        dmixed = dys * sg
        dsc_ref[...] += jnp.sum(dym * sg, axis=0, keepdims=True)
        dbg_ref[...] += jnp.sum(dmixed, axis=0, keepdims=True)
        dmb = dmixed.astype(BF16)
        wgrad_out(1)
        dp_ref[:, pw:2 * pw] = (dym * sc_ref[...] * (s + sg * (1.0 - s))).astype(BF16)
        inv = _pool_inv_counts((n_t - 1 - i) * ts, ts)
        dd = []
        for k in range(4):
            dmk = dmb[:, k * gd:(k + 1) * gd]
            dd.append(lax.dot_general(dmk, wg_ref[k], NT, preferred_element_type=F32))
            dwg_ref[k] += lax.dot_general(d_ref[:, k * gd:(k + 1) * gd], dmk, TN, preferred_element_type=F32)
        wgrad_out(2)
        q = jnp.concatenate([dd[k] * inv[k] for k in range(4)], axis=1)
        sums = _window_sums(jnp.concatenate([q, cq[...]], axis=0), _up)
        wgrad_out(3)
        dp_ref[:, 0:pw] = jnp.concatenate([sums[k][0:ts] - dd[k] for k in range(4)], axis=1).astype(BF16)
        cq[...] = q[0:hl, :]

    def full(a):
        return pl.BlockSpec(a.shape, lambda i: (0,) * a.ndim)

    rev = lambda w, j=0: pl.BlockSpec((ts, w), lambda i: (n_t - 1 - i, j))
    acc = pl.BlockSpec((1, pw), lambda i: (0, 0))
    return pl.pallas_call(
        body, name="l1_bwd_mix", grid=(n_t,),
        in_specs=[rev(D), rev(pw, 1), rev(pw), rev(pw), rev(pw)] + [full(a) for a in (gate, wg, scale, wo)],
        out_specs=[rev(2 * pw), pl.BlockSpec((N_CHIP, gd, D), lambda i: (0, 0, 0)),
                   pl.BlockSpec((4, gd, gd), lambda i: (0, 0, 0)), acc, acc],
        out_shape=[jax.ShapeDtypeStruct((s_len, 2 * pw), BF16), jax.ShapeDtypeStruct((N_CHIP, gd, D), F32),
                   jax.ShapeDtypeStruct((4, gd, gd), F32),
                   jax.ShapeDtypeStruct((1, pw), F32), jax.ShapeDtypeStruct((1, pw), F32)],
        scratch_shapes=[pltpu.VMEM((hl, pw), F32)],
        compiler_params=_cp(("arbitrary",)),
    )(dx2, proj, mixed, y, dpool, gate, wg, scale, wo)


def _l0_bwd_mix(dx1, proj, hst, y, xc, cz, gate, cw, wa, ba, wx, bx, lam, sw, wo):
    s_len = dx1.shape[0]
    ts = _tile(s_len, TS_MIX)
    n_t = s_len // ts
    hl, hb = SUBLANES, BF16_ROWS
    yb_w = 2 * D // N_CHIP

    def body(dx_ref, p_ref, h_ref, hh_ref, y_ref, xc_ref, cz_ref, gate_ref, cw_ref, wa_ref, ba_ref, wx_ref, bx_ref,
             lam_ref, sw_ref, wo_ref, dp_ref, mt_ref, dwa_ref, dwx_ref, sm_ref, cg, cdxc, cdcz, ca):
        i = pl.program_id(0)
        ri = n_t - 1 - i

        @pl.when(i == 0)
        def _():
            cg[...] = jnp.zeros_like(cg)
            ca[...] = jnp.zeros_like(ca)
            cdxc[...] = jnp.zeros_like(cdxc)
            cdcz[...] = jnp.zeros_like(cdcz)
            sm_ref[...] = jnp.zeros_like(sm_ref)
            mt_ref[...] = jnp.zeros_like(mt_ref)
            dwa_ref[...] = jnp.zeros_like(dwa_ref)
            dwx_ref[...] = jnp.zeros_like(dwx_ref)

        dxb = dx_ref[...].astype(BF16)

        def wgrad_out(k):
            mt_ref[k] += lax.dot_general(y_ref[:, k * yb_w:(k + 1) * yb_w], dxb, TN, preferred_element_type=F32)

        wgrad_out(0)
        has_prev = (ri > 0).astype(F32)
        xa, ga, gbp, gcp, v, gb = [p_ref[:, k * D:(k + 1) * D].astype(F32) for k in range(6)]
        rows = _rows(ts, D)
        first = (rows == 0) & (ri == 0)
        xc = xc_ref[...].astype(F32)
        cz = cz_ref[...].astype(F32)
        r, ig = _lru_gates(xc, wa_ref, ba_ref[...], wx_ref, bx_ref[...])
        sp = _softplus_neg(lam_ref[...])
        a, m, inv_m = _lru_decay(r, sp, first)
        z = gcp * v
        h = h_ref[...].astype(F32)
        hprev = _down(jnp.concatenate([hh_ref[...].astype(F32)[hb - hl:hb] * has_prev, h], axis=0), 1)[hl:hl + ts]
        dy = lax.dot_general((gate_ref[...] * dx_ref[...]).astype(BF16), wo_ref[...], NT, preferred_element_type=F32)
        dya_pre, dyb_pre = dy[:, 0:D], dy[:, D:2 * D]
        s_a, s_b = _sigmoid(ga), _sigmoid(gb)
        silu_a, silu_b = ga * s_a, gb * s_b
        dp_ref[:, D:2 * D] = (dya_pre * h * (s_a + silu_a * (1.0 - s_a))).astype(BF16)
        dp_ref[:, 5 * D:6 * D] = (dyb_pre * (gbp * cz) * (s_b + silu_b * (1.0 - s_b))).astype(BF16)
        dya = dya_pre * silu_a
        dyb = dyb_pre * silu_b
        wgrad_out(1)
        dp_ref[:, 2 * D:3 * D] = (dyb * cz).astype(BF16)
        dcz = dyb * gbp
        dcz_ext = jnp.concatenate([dcz, cdcz[...]], axis=0)
        dcz_taps = [_up(dcz_ext, 2 - k)[0:ts] for k in range(3)]
        for k in range(3):
            sm_ref[8 + k:9 + k, :] += jnp.sum(z * dcz_taps[k], axis=0, keepdims=True)
        dz = sum(sw_ref[k:k + 1, :] * dcz_taps[k] for k in range(3))
        dp_ref[:, 3 * D:4 * D] = (dz * v).astype(BF16)
        dp_ref[:, 4 * D:5 * D] = (dz * gcp).astype(BF16)
        cdcz[...] = dcz[0:hl, :]
        alpha = _up(jnp.concatenate([a, ca[...]], axis=0), 1)[0:ts]
        wgrad_out(2)
        dh = _run(_scan_rev_steps(alpha, dya, cg[0:1, :]))
        wgrad_out(3)
        cg[...] = dh[0:hl, :]
        ca[...] = a[0:hl, :]
        da = dh * hprev
        dhx = dh * xc
        dm = dhx * ig
        di = dhx * m
        dxc = dh * (m * ig)
        dl = a * (da - jnp.where(first, 0.0, dm * a * inv_m))
        dlr = dl * r
        sm_ref[7:8, :] += jnp.sum(dlr, axis=0, keepdims=True) * (-LRU_C)
        dpa = dlr * (sp * (-LRU_C)) * (1.0 - r)
        dpx = di * ig * (1.0 - ig)
        sm_ref[5:6, :] += jnp.sum(dpa, axis=0, keepdims=True)
        sm_ref[6:7, :] += jnp.sum(dpx, axis=0, keepdims=True)
        dpa_b, dpx_b, xc_b = dpa.astype(BF16), dpx.astype(BF16), xc.astype(BF16)
        back = []
        for hd in range(LRU_HEADS):
            sl = slice(hd * LRU_HEAD_DIM, (hd + 1) * LRU_HEAD_DIM)
            back.append(lax.dot_general(dpa_b[:, sl], wa_ref[hd], NT, preferred_element_type=F32)
                        + lax.dot_general(dpx_b[:, sl], wx_ref[hd], NT, preferred_element_type=F32))
            dwa_ref[hd] += lax.dot_general(xc_b[:, sl], dpa_b[:, sl], TN, preferred_element_type=F32)
            dwx_ref[hd] += lax.dot_general(xc_b[:, sl], dpx_b[:, sl], TN, preferred_element_type=F32)
        dxc = dxc + jnp.concatenate(back, axis=1)
        sm_ref[4:5, :] += jnp.sum(dxc, axis=0, keepdims=True)
        dxc_ext = jnp.concatenate([dxc, cdxc[...]], axis=0)
        dxc_taps = [_up(dxc_ext, 3 - k)[0:ts] for k in range(4)]
        for k in range(4):
            sm_ref[k:k + 1, :] += jnp.sum(xa * dxc_taps[k], axis=0, keepdims=True)
        dp_ref[:, 0:D] = sum(cw_ref[k:k + 1, :] * dxc_taps[k] for k in range(4)).astype(BF16)
        cdxc[...] = dxc[0:hl, :]

    def full(a):
        return pl.BlockSpec(a.shape, lambda i: (0,) * a.ndim)

    rev = lambda w: pl.BlockSpec((ts, w), lambda i: (n_t - 1 - i, 0))
    halo = lambda w: pl.BlockSpec((hb, w), lambda i: (jnp.maximum((n_t - 1 - i) * (ts // hb) - 1, 0), 0))
    return pl.pallas_call(
        body, name="l0_bwd_mix", grid=(n_t,),
        in_specs=[rev(D), rev(6 * D), rev(D), halo(D), rev(2 * D), rev(D), rev(D)]
        + [full(a) for a in (gate, cw, wa, ba, wx, bx, lam, sw, wo)],
        out_specs=[rev(6 * D), pl.BlockSpec((N_CHIP, yb_w, D), lambda i: (0, 0, 0)),
                   pl.BlockSpec(wa.shape, lambda i: (0, 0, 0)), pl.BlockSpec(wa.shape, lambda i: (0, 0, 0)),
                   pl.BlockSpec((2 * SUBLANES, D), lambda i: (0, 0))],
        out_shape=[jax.ShapeDtypeStruct((s_len, 6 * D), BF16), jax.ShapeDtypeStruct((N_CHIP, yb_w, D), F32),
                   jax.ShapeDtypeStruct(wa.shape, F32), jax.ShapeDtypeStruct(wa.shape, F32),
                   jax.ShapeDtypeStruct((2 * SUBLANES, D), F32)],
        scratch_shapes=[pltpu.VMEM((hl, D), F32)] * 4,
        compiler_params=_cp(("arbitrary",)),
    )(dx1, proj, hst, hst, y, xc, cz, gate, cw, wa, ba, wx, bx, lam, sw, wo)


def _dgrad_norm(dproj, w, x, dres, g, sc, name, after=None):
    s_len, nb = x.shape[0], w.shape[2]
    ts = _tile(s_len, TS_DGRAD)
    order = [] if after is None else [after]

    def body(dp_ref, w_ref, x_ref, dr_ref, g_ref, sc_ref, *rest):
        dx_ref, s1_ref, s2_ref = rest[len(order):]

        @pl.when(pl.program_id(0) == 0)
        def _():
            s1_ref[...] = jnp.zeros_like(s1_ref)
            s2_ref[...] = jnp.zeros_like(s2_ref)

        dh = sum(lax.dot_general(dp_ref[:, k * nb:(k + 1) * nb], w_ref[k], NT, preferred_element_type=F32)
                 for k in range(N_CHIP))
        xv = x_ref[...]
        r = lax.rsqrt(jnp.mean(xv * xv, axis=-1, keepdims=True) + RMS_EPS)
        n = xv * r
        s1_ref[...] += jnp.sum(dh, axis=0, keepdims=True)
        s2_ref[...] += jnp.sum(dh * n, axis=0, keepdims=True)
        dn = dh * (g_ref[...] * (1.0 + sc_ref[...]))
        dx_ref[...] = dr_ref[...] + r * (dn - n * jnp.mean(dn * n, axis=-1, keepdims=True))

    row = lambda wd: pl.BlockSpec((ts, wd), lambda i: (i, 0))
    vec = pl.BlockSpec((1, D), lambda i: (0, 0))
    return pl.pallas_call(
        body, name=name, grid=(s_len // ts,),
        in_specs=[row(N_CHIP * nb), pl.BlockSpec(w.shape, lambda i: (0, 0, 0)), row(D), row(D), vec, vec]
        + [ANY] * len(order),
        out_specs=[row(D), vec, vec],
        out_shape=[jax.ShapeDtypeStruct((s_len, D), F32)] + [jax.ShapeDtypeStruct((1, D), F32)] * 2,
        compiler_params=_cp(("arbitrary",)),
    )(dproj, w, x, dres, g, sc, *order)


def _wgrad(a, b, groups, ka, nb, a_col, b_col, name, after=None):
    s_len = a.shape[0]
    ts = _tile(s_len, TS_WGRAD)
    n_s = s_len // ts
    order = [] if after is None else [after]

    def body(a_ref, b_ref, *rest):
        o_ref, wire_ref = rest[-2:]

        @pl.when(pl.program_id(1) == 0)
        def _():
            o_ref[...] = jnp.zeros_like(o_ref)

        o_ref[...] += lax.dot_general(a_ref[...].astype(BF16), b_ref[...].astype(BF16), TN, preferred_element_type=F32)

        @pl.when(pl.program_id(1) == n_s - 1)
        def _():
            wire_ref[...] = o_ref[...].astype(GRAD_WIRE_DTYPE)

    blk = pl.BlockSpec((None, ka, nb), lambda g, s: (g, 0, 0))
    return pl.pallas_call(
        body, name=name, grid=(groups, n_s),
        in_specs=[pl.BlockSpec((ts, ka), lambda g, s: (s, a_col(g))), pl.BlockSpec((ts, nb), lambda g, s: (s, b_col(g)))]
        + [ANY] * len(order),
        out_specs=[blk, blk],
        out_shape=[jax.ShapeDtypeStruct((groups, ka, nb), F32), jax.ShapeDtypeStruct((groups, ka, nb), GRAD_WIRE_DTYPE)],
        compiler_params=_cp(("parallel", "arbitrary")),
    )(a, b, *order)


def _wo_final(mt, wo, gate, name):
    rb = mt.shape[1]

    def body(m_ref, w_ref, gate_ref, dw_ref, wire_ref, dg_ref):
        @pl.when(pl.program_id(0) == 0)
        def _():
            dg_ref[...] = jnp.zeros_like(dg_ref)

        mv = m_ref[...]
        dw = mv * gate_ref[...]
        dw_ref[...] = dw
        wire_ref[...] = dw.astype(GRAD_WIRE_DTYPE)
        dg_ref[...] += jnp.sum(mv * w_ref[...].astype(F32), axis=0, keepdims=True)

    blk = pl.BlockSpec((None, rb, D), lambda k: (k, 0, 0))
    vec = pl.BlockSpec((1, D), lambda k: (0, 0))
    return pl.pallas_call(
        body, name=name, grid=(N_CHIP,), in_specs=[blk, blk, vec], out_specs=[blk, blk, vec],
        out_shape=[jax.ShapeDtypeStruct(mt.shape, F32), jax.ShapeDtypeStruct(mt.shape, GRAD_WIRE_DTYPE),
                   jax.ShapeDtypeStruct((1, D), F32)],
        compiler_params=_cp(("arbitrary",)),
    )(mt, wo, gate)


ROW_NORM_G, ROW_CONV_W, ROW_CONV_B, ROW_B_A, ROW_B_X, ROW_LAMBDA, ROW_SC_W, ROW_POOL_B, ROW_POOL_S, ROW_FINAL_G = (
    0, 2, 6, 7, 8, 9, 10, 13, 15, 17)
ROW_LOSS = 18


def _small_pack(s1_0, s2_0, s1_1, s2_1, sm0, dsc1, dbg1, dgf, losscols, dgate0, dgate1, norm_g, sc0, sc1, lam):
    def body(s1_0r, s2_0r, s1_1r, s2_1r, sm, dsc, dbg, dgfr, lcols, dg0, dg1, ng, sc0r, sc1r, lamr, buf, dmod):
        buf[...] = jnp.zeros_like(buf)
        buf[0:1, :] = s2_0r[...] * (1.0 + sc0r[...])
        buf[1:2, :] = s2_1r[...] * (1.0 + sc1r[...])
        buf[ROW_CONV_W:ROW_CONV_W + 4, :] = sm[0:4, :]
        buf[ROW_CONV_B:ROW_CONV_B + 1, :] = sm[4:5, :]
        buf[ROW_B_A:ROW_B_A + 1, :] = sm[5:6, :]
        buf[ROW_B_X:ROW_B_X + 1, :] = sm[6:7, :]
        buf[ROW_LAMBDA:ROW_LAMBDA + 1, :] = -sm[7:8, :] * _sigmoid(-lamr[...])
        buf[ROW_SC_W:ROW_SC_W + 3, :] = sm[8:11, :]
        for k in range(2):
            buf[ROW_POOL_B + k:ROW_POOL_B + k + 1, :] = dbg[:, k * D:(k + 1) * D]
            buf[ROW_POOL_S + k:ROW_POOL_S + k + 1, :] = dsc[:, k * D:(k + 1) * D]
        buf[ROW_FINAL_G:ROW_FINAL_G + 1, :] = dgfr[...]
        pieces = (s1_0r[...], s2_0r[...] * ng[0:1, :], dg0[...], s1_1r[...], s2_1r[...] * ng[1:2, :], dg1[...])
        for k, pc in enumerate(pieces):
            dmod[:, k * D:(k + 1) * D] = jnp.broadcast_to(pc, (SUBLANES, D))
        buf[ROW_LOSS:ROW_LOSS + 1, :] = jnp.broadcast_to(jnp.sum(lcols[...], axis=1, keepdims=True) * (0.5 / D), (1, D))

    args = (s1_0, s2_0, s1_1, s2_1, sm0, dsc1, dbg1, dgf, losscols, dgate0, dgate1, norm_g, sc0, sc1, lam)
    return pl.pallas_call(
        body, name="small_pack", in_specs=[VMEM] * len(args), out_specs=[VMEM] * 2,
        out_shape=[jax.ShapeDtypeStruct((SMALL_ROWS, D), F32), jax.ShapeDtypeStruct((SUBLANES, 6 * D), F32)],
        compiler_params=_cp(),
    )(*args)


def _small_comm(buf_a, buf_b, dmod8):
    ra, rb = buf_a.shape[0] // N_DEV, buf_b.shape[0] // N_DEV
    wb = buf_b.shape[1]

    def body(a_ref, b_ref, dm_ref, oa_ref, ob_ref, odm_ref, ina, inb, dslot, sa, sb, s1, r1, s2, r2):
        x, y, c = _pos()
        me = 4 * x + 2 * y + c
        peers = []
        for r in range(1, N_DEV):
            fx, fy, fc = (r >> 2) & 1, (r >> 1) & 1, r & 1
            px, py, pc = _flip(x, fx), _flip(y, fy), _flip(c, fc)
            peers.append(((px, py, pc), 4 * px + 2 * py + pc))
        seg_a = lambda d: pl.ds(pl.multiple_of(d * ra, SUBLANES), ra)
        seg_b = lambda d: pl.ds(pl.multiple_of(d * rb, SUBLANES), rb)
        first = []
        for r, (peer, pid) in enumerate(peers):
            for k, (src, dst) in enumerate(((a_ref.at[seg_a(pid), :], ina.at[r]), (b_ref.at[seg_b(pid), :], inb.at[r]),
                                            (dm_ref, dslot.at[me]))):
                cp = pltpu.make_async_remote_copy(src_ref=src, dst_ref=dst, send_sem=s1.at[3 * r + k],
                                                  recv_sem=r1.at[3 * r + k], device_id=peer, device_id_type=MESH)
                cp.start()
                first.append(cp)
        dslot[me] = dm_ref[...]
        for cp in first:
            cp.wait()
        acc_a, acc_b = a_ref[seg_a(me), :], b_ref[seg_b(me), :]
        for r in range(N_DEV - 1):
            acc_a = acc_a + ina[r]
            acc_b = acc_b + inb[r]
        sa[...] = acc_a
        sb[...] = acc_b
        oa_ref[seg_a(me), :] = acc_a
        ob_ref[seg_b(me), :] = acc_b
        second = []
        for r, (peer, pid) in enumerate(peers):
            for k, (src, dst) in enumerate(((sa, oa_ref.at[seg_a(me), :]), (sb, ob_ref.at[seg_b(me), :]))):
                cp = pltpu.make_async_remote_copy(src_ref=src, dst_ref=dst, send_sem=s2.at[2 * r + k],
                                                  recv_sem=r2.at[2 * r + k], device_id=peer, device_id_type=MESH)
                cp.start()
                second.append(cp)
        rows = _rows(SUBLANES, dm_ref.shape[1])
        dm_all = jnp.zeros(dm_ref.shape, F32)
        for d in range(N_DEV):
            dm_all = jnp.where(rows == d, dslot[d], dm_all)
        odm_ref[...] = dm_all
        for cp in second:
            cp.wait()

    nrel = N_DEV - 1
    return pl.pallas_call(
        body, name="small_comm", in_specs=[VMEM] * 3, out_specs=[VMEM] * 3,
        out_shape=[jax.ShapeDtypeStruct(buf_a.shape, F32), jax.ShapeDtypeStruct(buf_b.shape, F32),
                   jax.ShapeDtypeStruct(dmod8.shape, F32)],
        scratch_shapes=[pltpu.VMEM((nrel, ra, D), F32), pltpu.VMEM((nrel, rb, wb), F32),
                        pltpu.VMEM((N_DEV,) + dmod8.shape, F32), pltpu.VMEM((ra, D), F32), pltpu.VMEM((rb, wb), F32),
                        pltpu.SemaphoreType.DMA((3 * nrel,)), pltpu.SemaphoreType.DMA((3 * nrel,)),
                        pltpu.SemaphoreType.DMA((2 * nrel,)), pltpu.SemaphoreType.DMA((2 * nrel,))],
        compiler_params=_cp(),
    )(buf_a, buf_b, dmod8)


def _adam(w, g, m, v):
    m2 = ADAM_B1 * m + (1.0 - ADAM_B1) * g
    v2 = ADAM_B2 * v + (1.0 - ADAM_B2) * (g * g)
    m_hat = m2 / (1.0 - ADAM_B1 ** ADAM_STEP)
    v_hat = v2 / (1.0 - ADAM_B2 ** ADAM_STEP)
    return -ADAM_LR * (m_hat / (jnp.sqrt(v_hat) + ADAM_EPS) + ADAM_WD * w), m2, v2


def _small_adam(red_a, red_b, dm_all, params):
    n = len(params)

    def body(*refs):
        ra, rb, dm = refs[:3]
        wmv = refs[3:3 + 3 * n]
        outs = refs[3 + 3 * n:]
        x, y, _ = _pos()
        chip = 2 * x + y

        def shard(row0, nrows, width):
            per_row = D // width
            cands = []
            for k in range(N_CHIP):
                if nrows == 1 or per_row >= N_CHIP:
                    cands.append(ra[row0:row0 + nrows, k * width:(k + 1) * width])
                else:
                    rr, cc = divmod(k * width, D)
                    cands.append(ra[row0 + rr:row0 + rr + 1, cc:cc + width])
            g = cands[0]
            for k in range(1, N_CHIP):
                g = jnp.where(chip == k, cands[k], g)
            return g

        dms = jnp.sum(dm[...], axis=0, keepdims=True)
        hw = LRU_HEADS * LRU_HEAD_DIM
        grads = [
            ra[ROW_NORM_G:ROW_NORM_G + 2, :],
            None,
            shard(ROW_CONV_W, 4, D // N_CHIP),
            ra[ROW_CONV_B:ROW_CONV_B + 1, :],
            rb[0:hw, :],
            ra[ROW_B_A:ROW_B_A + 1, :],
            rb[hw:2 * hw, :],
            ra[ROW_B_X:ROW_B_X + 1, :],
            ra[ROW_LAMBDA:ROW_LAMBDA + 1, :],
            shard(ROW_SC_W, 3, D // N_CHIP),
            shard(ROW_POOL_B, 2, 2 * D // N_CHIP),
            shard(ROW_POOL_S, 2, 2 * D // N_CHIP),
            ra[ROW_FINAL_G:ROW_FINAL_G + 1, :],
        ]
        for p in range(n):
            w_ref, m_ref, v_ref = wmv[3 * p:3 * p + 3]
            g_out, d_out, m_out, v_out = outs[4 * p:4 * p + 4]
            if grads[p] is None:
                for l in range(2):
                    g = dms[:, l * 3 * D:(l + 1) * 3 * D]
                    dl, m2, v2 = _adam(w_ref[l:l + 1, :], g, m_ref[l:l + 1, :], v_ref[l:l + 1, :])
                    g_out[l:l + 1, :] = g
                    d_out[l:l + 1, :] = dl
                    m_out[l:l + 1, :] = m2
                    v_out[l:l + 1, :] = v2
            else:
                g = grads[p]
                dl, m2, v2 = _adam(w_ref[...], g, m_ref[...], v_ref[...])
                g_out[...] = g
                d_out[...] = dl
                m_out[...] = m2
                v_out[...] = v2

    flat = [a for p in params for a in p]
    return pl.pallas_call(
        body, name="small_adam", in_specs=[VMEM] * (3 + len(flat)), out_specs=[VMEM] * (4 * n),
        out_shape=[jax.ShapeDtypeStruct(p[0].shape, F32) for p in params for _ in range(4)],
        compiler_params=_cp(),
    )(red_a, red_b, dm_all, *flat)


def _modw_adam(ca_t, dm_sh, w, m, v):
    nw = w.shape[2]

    def body(c_ref, d_ref, w_ref, m_ref, v_ref, g_out, d_out, m_out, v_out):
        g = jnp.dot(c_ref[...], d_ref[...], precision=lax.Precision.HIGHEST, preferred_element_type=F32)
        dl, m2, v2 = _adam(w_ref[...], g, m_ref[...], v_ref[...])
        g_out[...] = g
        d_out[...] = dl
        m_out[...] = m2
        v_out[...] = v2

    blk = pl.BlockSpec((None, D, nw), lambda l: (l, 0, 0))
    return pl.pallas_call(
        body, name="modw_adam", grid=(2,),
        in_specs=[pl.BlockSpec((D, SUBLANES), lambda l: (0, 0)), pl.BlockSpec((None, SUBLANES, nw), lambda l: (l, 0, 0)),
                  blk, blk, blk],
        out_specs=[blk] * 4, out_shape=[jax.ShapeDtypeStruct(w.shape, F32)] * 4,
        compiler_params=_cp(("arbitrary",)),
    )(ca_t, dm_sh, w, m, v)


def _exchange(copies, name, out_type, n_sems, args, sequencer, after=None):
    order = [] if after is None else [after]
    n_in, n_out = len(args) + len(order), len(out_type)

    def body(*refs):
        barrier = pltpu.get_barrier_semaphore()
        peers = sequencer[1](*_pos())
        for peer in peers:
            pl.semaphore_signal(barrier, inc=1, device_id=peer, device_id_type=MESH)
        pl.semaphore_wait(barrier, len(peers))
        copies(refs[:n_in], refs[n_in:n_in + n_out], refs[n_in + n_out], refs[n_in + n_out + 1])

    sems = [pltpu.SemaphoreType.DMA((n_sems,))] * 2
    return pl.kernel(body, out_type, mesh=plsc.ScalarSubcoreMesh(axis_name="sequencer", num_cores=1), name=name,
                     scratch_types=sems, compiler_params=pltpu.CompilerParams(collective_id=sequencer[0]))(*args, *order)


def _sibling(x, y, c):
    return [(x, y, 1 - c)]


def _other_chips(x, y, c):
    return [(1 - x, y, c), (x, 1 - y, c), (1 - x, 1 - y, c)]


def _to_wire(g, name, after=None):
    _, rr, cc = g.shape
    rb = min(rr, 256)

    def body(g_ref, *rest):
        rest[-1][...] = g_ref[...].astype(GRAD_WIRE_DTYPE)

    order = [] if after is None else [after]
    blk = pl.BlockSpec((None, rb, cc), lambda k, j: (k, j, 0))
    return pl.pallas_call(
        body, name=name, grid=(N_CHIP, rr // rb), in_specs=[blk] + [ANY] * len(order), out_specs=blk,
        out_shape=jax.ShapeDtypeStruct(g.shape, GRAD_WIRE_DTYPE), compiler_params=_cp(("parallel", "parallel")),
    )(g, *order)


def _chip_scatter(ps, name, collective_id, after=None):
    n = len(ps)

    def copies(ins, outs, ssem, rsem):
        x, y, c = _pos()
        cps = []
        for a in range(n):
            for q, (fx, fy) in enumerate(((1, 0), (0, 1), (1, 1))):
                px, py = _flip(x, fx), _flip(y, fy)
                cp = pltpu.make_async_remote_copy(
                    src_ref=ins[a].at[2 * px + py], dst_ref=outs[a].at[q],
                    send_sem=ssem.at[3 * a + q], recv_sem=rsem.at[3 * a + q], device_id=(px, py, c), device_id_type=MESH)
                cp.start()
                cps.append(cp)
        for cp in cps:
            cp.wait()

    out_type = [jax.ShapeDtypeStruct((N_CHIP - 1,) + p.shape[1:], p.dtype) for p in ps]
    return _exchange(copies, name, out_type, 3 * n, ps, (collective_id, _other_chips), after)


def _add_owner(p, got, chipidx, name, after=None):
    _, hr, cc = p.shape
    rb = min(hr, 256)

    def body(k_ref, p_ref, r_ref, *rest):
        rest[-1][...] = ((p_ref[...].astype(F32) + r_ref[0].astype(F32)) + r_ref[1].astype(F32)) + r_ref[2].astype(F32)

    order = [] if after is None else [after]
    return pl.pallas_call(
        body, name=name,
        grid_spec=pltpu.PrefetchScalarGridSpec(
            num_scalar_prefetch=1, grid=(hr // rb,),
            in_specs=[pl.BlockSpec((None, rb, cc), lambda j, k_ref: (k_ref[0], j, 0)),
                      pl.BlockSpec((N_CHIP - 1, rb, cc), lambda j, k_ref: (0, j, 0))] + [ANY] * len(order),
            out_specs=pl.BlockSpec((rb, cc), lambda j, k_ref: (j, 0))),
        out_shape=jax.ShapeDtypeStruct((hr, cc), F32),
        compiler_params=_cp(("parallel",)),
    )(chipidx, p, got, *order)


def _sib_exchange(ts_, name, collective_id, after=None):
    n = len(ts_)

    def copies(ins, outs, ssem, rsem):
        x, y, c = _pos()
        cps = []
        for a in range(n):
            cp = pltpu.make_async_remote_copy(src_ref=ins[a], dst_ref=outs[a], send_sem=ssem.at[a],
                                              recv_sem=rsem.at[a], device_id=(x, y, 1 - c), device_id_type=MESH)
            cp.start()
            cps.append(cp)
        for cp in cps:
            cp.wait()

    out_type = [jax.ShapeDtypeStruct(t.shape, F32) for t in ts_]
    return _exchange(copies, name, out_type, n, ts_, (collective_id, _sibling), after)


def _adam_2d(w, g_own, g_sib, m, v, name):
    rr, cc = w.shape
    rb = min(rr, 256)

    def body(w_ref, go_ref, gs_ref, m_ref, v_ref, g_out, d_out, m_out, v_out):
        g = go_ref[...] + gs_ref[...]
        dl, m2, v2 = _adam(w_ref[...], g, m_ref[...], v_ref[...])
        g_out[...] = g
        d_out[...] = dl
        m_out[...] = m2
        v_out[...] = v2

    blk = pl.BlockSpec((rb, cc), lambda j: (j, 0))
    return pl.pallas_call(
        body, name=name, grid=(rr // rb,), in_specs=[blk] * 5, out_specs=[blk] * 4,
        out_shape=[jax.ShapeDtypeStruct((rr, cc), F32)] * 4, compiler_params=_cp(("parallel",)),
    )(w, g_own, g_sib, m, v)


def kernel(x, c, norm_g, mod_w, mod_b, hy_w_in, hy_conv_w, hy_conv_b, lru_w_a, lru_b_a, lru_w_x, lru_b_x, lru_lambda, sc_conv_w, hy_w_out, pool_w_in, pool_w_grp, pool_b_grp, pool_scale, pool_w_out, final_g, loss_target, m_norm_g, m_mod_w, m_mod_b, m_hy_w_in, m_hy_conv_w, m_hy_conv_b, m_lru_w_a, m_lru_b_a, m_lru_w_x, m_lru_b_x, m_lru_lambda, m_sc_conv_w, m_hy_w_out, m_pool_w_in, m_pool_w_grp, m_pool_b_grp, m_pool_scale, m_pool_w_out, m_final_g, v_norm_g, v_mod_w, v_mod_b, v_hy_w_in, v_hy_conv_w, v_hy_conv_b, v_lru_w_a, v_lru_b_a, v_lru_w_x, v_lru_b_x, v_lru_lambda, v_sc_conv_w, v_hy_w_out, v_pool_w_in, v_pool_w_grp, v_pool_b_grp, v_pool_scale, v_pool_w_out, v_final_g):
    ax, ay, ac = _pos()
    me = 4 * ax + 2 * ay + ac
    chip = 2 * ax + ay
    xs = x[0]
    tgt = loss_target[0]
    gd = POOL_GROUP_DIM
    kidx = chip.reshape(1).astype(jnp.int32)

    big = [hy_w_in[0], hy_w_out[0], pool_w_in[0], pool_w_grp[0].reshape(4 * 128, gd), pool_w_out[0]]
    w_in0, w_out0 = _wgather_sequencer(
        [_wcast_own_block(w, kidx, f"wcast_own_block_{a}") for a, w in enumerate(big[:2])], "wgather_l0", CIDS_WGATHER[0])

    ca_all, mod_all, small_w = _mod_fwd(jnp.broadcast_to(c, (SUBLANES, D)), mod_w, mod_b,
                                        hy_conv_w[0], sc_conv_w[0], pool_b_grp, pool_scale)
    mod_me = lax.dynamic_index_in_dim(mod_all, me, axis=1, keepdims=False)
    sh0, sc0, gt0 = (mod_me[0:1, k * D:(k + 1) * D] for k in range(3))
    sh1, sc1, gt1 = (mod_me[1:2, k * D:(k + 1) * D] for k in range(3))
    cw = small_w[SW_CONV:SW_CONV + 4, 0:D]
    sw = small_w[SW_SC:SW_SC + 3, 0:D]
    pool_b = small_w[SW_POOL_B:SW_POOL_B + 1, :]
    pool_s = small_w[SW_POOL_S:SW_POOL_S + 1, :]
    g0, g1, gf = norm_g[0:1], norm_g[1:2], final_g.reshape(1, D)
    cb, ba, bx, lam = hy_conv_b, lru_b_a, lru_b_x, lru_lambda

    w_in1, w_grp, w_out1 = _wgather_sequencer(
        [_wcast_own_block(w, kidx, f"wcast_own_block_{a + 2}", after=(w_out0, small_w)) for a, w in enumerate(big[2:])],
        "wgather_l1", CIDS_WGATHER[1])
    w_grp =w_grp.reshape(N_CHIP, 4, 128, gd).transpose(1, 0, 2, 3).reshape(4, gd, gd)
    wa_b, wx_b = _wcast([lru_w_a[0], lru_w_x[0]])

    x1, hst, y0, xc0, cz0, h0, proj0 = _l0_fwd(xs, g0, sc0, sh0, w_in0, gt0, cw, cb, wa_b, ba, wx_b, bx, lam, sw,
                                               w_out0.reshape(2 * D, D))
    dpool, mixed, y1, dx2, losscols, dgf, h1, proj1 = _l1_fwd(x1, g1, sc1, sh1, w_in1, tgt, gt1, w_grp, pool_b, pool_s,
                                                              w_out1.reshape(2 * D, D), gf)

    def add_owners(grads, got, tag, ids, after):
        own = []
        for a, (g, r) in enumerate(zip(grads, got)):
            own.append(_add_owner(g, r, kidx, f"grad_add_owner_{tag}{a}", own[-1] if own else after))
        return own, _sib_exchange(own, f"grad_sib_exchange_{tag}", ids[1])

    dproj1, mt1, d_wgrp, dsc1, dbg1 = _l1_bwd_mix(dx2, proj1, mixed, y1, dpool, gt1, w_grp, pool_s,
                                                  w_out1.reshape(2 * D, D))
    d_win1, wire_win1 = _wgrad(h1, dproj1, N_CHIP, D, D, lambda g: 0, lambda g: g, "l1_wgrad_in")
    d_wout1, wire_wout1, dgate1 = _wo_final(mt1, w_out1, gt1, "l1_wo_final")
    d_wgrp = d_wgrp.reshape(4, N_CHIP, 128, gd).transpose(1, 0, 2, 3).reshape(N_CHIP, 4 * 128, gd)
    grads_l1 = [d_win1, d_wgrp, d_wout1]
    got_l1 = _chip_scatter([wire_win1, _to_wire(d_wgrp, "grad_to_wire_grp"), wire_wout1], "grad_chip_scatter_l1",
                           CIDS_L1[0])
    dx1, s1_1, s2_1 = _dgrad_norm(dproj1, w_in1, x1, dx2, g1, sc1, "l1_bwd_proj")

    dproj0, mt0, d_wa, d_wx, sm0 = _l0_bwd_mix(dx1, proj0, hst, y0, xc0, cz0, gt0, cw, wa_b, ba, wx_b, bx, lam, sw,
                                               w_out0.reshape(2 * D, D))
    sums_l1, sib_l1 = add_owners(grads_l1, got_l1, "l1", CIDS_L1, after=sm0)
    d_win0, wire_win0 = _wgrad(h0, dproj0, N_CHIP, D, 6 * D // N_CHIP, lambda g: 0, lambda g: g, "l0_wgrad_in",
                               after=sums_l1[-1])
    d_wout0, wire_wout0, dgate0 = _wo_final(mt0, w_out0, gt0, "l0_wo_final")
    grads_l0 = [d_win0, d_wout0]
    got_l0 = _chip_scatter([wire_win0, wire_wout0], "grad_chip_scatter_l0", CIDS_L0[0], after=sib_l1[0])
    grad_x, s1_0, s2_0 = _dgrad_norm(dproj0, w_in0, xs, dx1, g0, sc0, "l0_bwd_proj", after=wire_win0)
    sums_l0, sib_l0 = add_owners(grads_l0, got_l0, "l0", CIDS_L0, after=s1_0)

    buf_a, dmod8 = _small_pack(s1_0, s2_0, s1_1, s2_1, sm0, dsc1, dbg1, dgf, losscols, dgate0, dgate1,
                                      norm_g, sc0, sc1, lam)
    hw = LRU_HEADS * LRU_HEAD_DIM
    buf_b = jnp.concatenate([d_wa.reshape(hw, LRU_HEAD_DIM), d_wx.reshape(hw, LRU_HEAD_DIM)], axis=0)
    red_a, red_b, dm_all = _small_comm(buf_a, buf_b, dmod8)
    small = [(norm_g, m_norm_g, v_norm_g), (mod_b, m_mod_b, v_mod_b),
             (hy_conv_w[0], m_hy_conv_w[0], v_hy_conv_w[0]), (hy_conv_b, m_hy_conv_b, v_hy_conv_b),
             tuple(a.reshape(hw, LRU_HEAD_DIM) for a in (lru_w_a, m_lru_w_a, v_lru_w_a)),
             (lru_b_a, m_lru_b_a, v_lru_b_a),
             tuple(a.reshape(hw, LRU_HEAD_DIM) for a in (lru_w_x, m_lru_w_x, v_lru_w_x)),
             (lru_b_x, m_lru_b_x, v_lru_b_x), (lru_lambda, m_lru_lambda, v_lru_lambda),
             (sc_conv_w[0], m_sc_conv_w[0], v_sc_conv_w[0]), (pool_b_grp, m_pool_b_grp, v_pool_b_grp),
             (pool_scale, m_pool_scale, v_pool_scale),
             tuple(a.reshape(1, D) for a in (final_g, m_final_g, v_final_g))]
    small_names = ["norm_g", "mod_b", "hy_conv_w", "hy_conv_b", "lru_w_a", "lru_b_a", "lru_w_x", "lru_b_x",
                   "lru_lambda", "sc_conv_w", "pool_b_grp", "pool_scale", "final_g"]
    small_out = _small_adam(red_a, red_b, dm_all, small)
    res = {}
    shapes = dict(norm_g=norm_g, mod_b=mod_b, hy_conv_w=hy_conv_w, hy_conv_b=hy_conv_b, lru_w_a=lru_w_a, lru_b_a=lru_b_a,
                  lru_w_x=lru_w_x, lru_b_x=lru_b_x, lru_lambda=lru_lambda, sc_conv_w=sc_conv_w, pool_b_grp=pool_b_grp,
                  pool_scale=pool_scale, final_g=final_g)
    for p, nm in enumerate(small_names):
        res[nm] = tuple(o.reshape(shapes[nm].shape) for o in small_out[4 * p:4 * p + 4])

    nw = mod_w.shape[2]
    dm_sh = jnp.stack([lax.dynamic_slice_in_dim(dm_all[:, l * 3 * D:(l + 1) * 3 * D], chip * nw, nw, axis=1)
                       for l in range(2)])
    res["mod_w"] = tuple(_modw_adam(ca_all.T, dm_sh, mod_w, m_mod_w, v_mod_w))

    sums = list(sums_l0) + list(sums_l1)
    sib_sums = list(sib_l0) + list(sib_l1)
    big_names = ["hy_w_in", "hy_w_out", "pool_w_in", "pool_w_grp", "pool_w_out"]
    big_wmv = [(hy_w_in, m_hy_w_in, v_hy_w_in), (hy_w_out, m_hy_w_out, v_hy_w_out), (pool_w_in, m_pool_w_in, v_pool_w_in),
               (pool_w_grp, m_pool_w_grp, v_pool_w_grp), (pool_w_out, m_pool_w_out, v_pool_w_out)]
    for a, nm in enumerate(big_names):
        rr, cc = big[a].shape
        w, m, v = (t.reshape(rr, cc) for t in big_wmv[a])
        outs = _adam_2d(w, sums[a], sib_sums[a], m, v, f"adam_{nm}")
        res[nm] = tuple(o.reshape(big_wmv[a][0].shape) for o in outs)

    loss = red_a[ROW_LOSS, 0]
    order = ["norm_g", "mod_w", "mod_b", "hy_w_in", "hy_conv_w", "hy_conv_b", "lru_w_a", "lru_b_a", "lru_w_x", "lru_b_x",
             "lru_lambda", "sc_conv_w", "hy_w_out", "pool_w_in", "pool_w_grp", "pool_b_grp", "pool_scale", "pool_w_out",
             "final_g"]
    return (loss, grad_x[None], *[res[nm][0] for nm in order], *[res[nm][1] for nm in order],
            *[res[nm][2] for nm in order], *[res[nm][3] for nm in order])
```

```python
import jax
import jax.numpy as jnp
from jax import lax
from jax.experimental import pallas as pl
from jax.experimental.pallas import tpu as pltpu
from jax.experimental.pallas import tpu_sc as plsc

F32, BF16 = jnp.float32, jnp.bfloat16
D = 1024
RMS_EPS = 1e-6
SQRT_FLOOR = 1e-30
LRU_C = 8.0
LRU_HEADS, LRU_HEAD_DIM = 8, 128
POOL_WINDOWS = (2, 4, 8, 16)
POOL_GROUP_DIM = 512
ADAM_LR, ADAM_B1, ADAM_B2, ADAM_EPS, ADAM_WD, ADAM_STEP = 0.001, 0.9, 0.999, 1e-08, 0.01, 10
MESH = pl.DeviceIdType.MESH
CIDS_WGATHER = (1, 8)
CIDS_L1 = (2, 3)
CIDS_L0 = (4, 5)
N_DEV, N_CHIP = 8, 4
SUBLANES = 8
BF16_ROWS = 16
POOL_HALO = 16
TS_MIX, TS_WGRAD, TS_DGRAD = 256, 2048, 512
SMALL_ROWS = 64
GRAD_WIRE_DTYPE = BF16
ANY = pl.BlockSpec(memory_space=pl.ANY)
VMEM = pl.BlockSpec(memory_space=pltpu.VMEM)
NT = (((1,), (1,)), ((), ()))
TN = (((0,), (0,)), ((), ()))


def _cp(sem=None, vmem_mb=56):
    kw = dict(vmem_limit_bytes=vmem_mb * 2 ** 20)
    if sem is not None:
        kw["dimension_semantics"] = sem
    return pltpu.CompilerParams(**kw)


def _tile(n, t):
    return min(n, t)


def _pos():
    return lax.axis_index("x"), lax.axis_index("y"), lax.axis_index("c")


def _flip(v, f):
    return 1 - v if f else v


def _sigmoid(z):
    return 0.5 * jnp.tanh(0.5 * z) + 0.5


def _rows(n, c):
    return lax.broadcasted_iota(jnp.int32, (n, c), 0)


def _down(a, d):
    return a if d == 0 else pltpu.roll(a, d, 0)


def _up(a, d):
    return a if d == 0 else pltpu.roll(a, a.shape[0] - d, 0)


def _scan_fwd_steps(a, u, carry):
    n, c = a.shape
    sub = _rows(SUBLANES, c)
    out = []
    for k in range(n // SUBLANES):
        p = a[k * SUBLANES:(k + 1) * SUBLANES]
        g = u[k * SUBLANES:(k + 1) * SUBLANES]
        for d in (1, 2, 4):
            keep = sub >= d
            g = g + p * jnp.where(keep, pltpu.roll(g, d, 0), 0.0)
            p = p * jnp.where(keep, pltpu.roll(p, d, 0), 1.0)
        h = g + p * carry
        carry = h[SUBLANES - 1:SUBLANES, :]
        out.append(h)
        yield
    return jnp.concatenate(out, axis=0)


def _scan_rev_steps(alpha, b, carry):
    n, c = alpha.shape
    sub = _rows(SUBLANES, c)
    out = []
    for k in reversed(range(n // SUBLANES)):
        p = alpha[k * SUBLANES:(k + 1) * SUBLANES]
        g = b[k * SUBLANES:(k + 1) * SUBLANES]
        for d in (1, 2, 4):
            keep = sub < SUBLANES - d
            g = g + p * jnp.where(keep, pltpu.roll(g, SUBLANES - d, 0), 0.0)
            p = p * jnp.where(keep, pltpu.roll(p, SUBLANES - d, 0), 1.0)
        h = g + p * carry
        carry = h[0:1, :]
        out.append(h)
        yield
    return jnp.concatenate(out[::-1], axis=0)


def _run(steps):
    while True:
        try:
            next(steps)
        except StopIteration as done:
            return done.value


def _paired(progress, pieces):
    n, done = len(pieces), 1
    pieces[0]()
    for frac in progress:
        while done < n and done <= frac * n:
            pieces[done]()
            done += 1
    while done < n:
        pieces[done]()
        done += 1


def _conv_taps(ext, halo, n, width):
    return [_down(ext, width - 1 - k)[halo:halo + n] for k in range(width)]


def _lru_gates(xc, wa_ref, ba, wx_ref, bx):
    xb = xc.astype(BF16)
    pa, px = [], []
    for h in range(LRU_HEADS):
        xh = xb[:, h * LRU_HEAD_DIM:(h + 1) * LRU_HEAD_DIM]
        pa.append(jnp.dot(xh, wa_ref[h], preferred_element_type=F32))
        px.append(jnp.dot(xh, wx_ref[h], preferred_element_type=F32))
    r = _sigmoid(jnp.concatenate(pa, axis=1) + ba)
    ig = _sigmoid(jnp.concatenate(px, axis=1) + bx)
    return r, ig


def _softplus_neg(lam):
    return jnp.maximum(-lam, 0.0) + jnp.log1p(jnp.exp(-jnp.abs(lam)))


def _recip_1_to_2(d):
    r0 = pl.reciprocal(d, approx=True)
    return r0 * (2.0 - d * r0)


def _lru_decay(r, sp, first):
    big_l = (-LRU_C) * r * sp
    a = jnp.exp(big_l)
    th = jnp.tanh(big_l)
    q = (-2.0 * th) * _recip_1_to_2(1.0 - th)
    rs = lax.rsqrt(jnp.maximum(q, SQRT_FLOOR))
    return a, jnp.where(first, 1.0, q * rs), rs


def _pool_inv_counts(t0, n):
    t = (t0 + lax.broadcasted_iota(jnp.int32, (n, 1), 0) + 1).astype(F32)
    return [1.0 / jnp.minimum(t, float(w)) for w in POOL_WINDOWS]


def _window_sums(ext, shift):
    gd = POOL_GROUP_DIM
    out = []
    s = ext
    for k in range(len(POOL_WINDOWS)):
        s = s + shift(s, 2 ** k)
        out.append(s[:, 0:gd])
        if k + 1 < len(POOL_WINDOWS):
            s = s[:, gd:]
    return out


SW_ROWS, SW_COLS = 16, 2 * D
SW_CONV, SW_SC, SW_POOL_B, SW_POOL_S = 0, 4, 8, 9


def _mod_fwd(c8, mod_w, mod_b, conv_w, sc_w, pool_b, pool_s):
    nw = mod_w.shape[2]
    cq, pq = conv_w.shape[1], pool_b.shape[1]

    def body(c_ref, w_ref, b_ref, cw_ref, sw_ref, pb_ref, ps_ref, ca_ref, mod_ref, small_ref,
             cslot, mslot, msend, pslot, psend, s1, r1, s2, r2, s3, r3):
        x, y, c = _pos()
        me = 4 * x + 2 * y + c
        chip = 2 * x + y
        first = []
        for r in range(1, N_DEV):
            fx, fy, fc = (r >> 2) & 1, (r >> 1) & 1, r & 1
            cp = pltpu.make_async_remote_copy(
                src_ref=c_ref, dst_ref=cslot.at[me], send_sem=s1.at[r - 1], recv_sem=r1.at[r - 1],
                device_id=(_flip(x, fx), _flip(y, fy), _flip(c, fc)), device_id_type=MESH)
            cp.start()
            first.append(cp)
        cslot[me] = c_ref[...]
        for cp in first:
            cp.wait()
        rows = _rows(SUBLANES, D)
        call = jnp.zeros((SUBLANES, D), F32)
        for d in range(N_DEV):
            call = jnp.where(rows == d, cslot[d], call)
        ca = call * _sigmoid(call)
        ca_ref[...] = ca
        for l in range(2):
            msend[l] = jnp.dot(ca, w_ref[l], precision=lax.Precision.HIGHEST, preferred_element_type=F32)
        psend[...] = jnp.zeros_like(psend)
        psend[SW_CONV:SW_CONV + 4, 0:cq] = cw_ref[...]
        psend[SW_SC:SW_SC + 3, 0:cq] = sw_ref[...]
        psend[SW_POOL_B:SW_POOL_B + 1, :] = pb_ref[...]
        psend[SW_POOL_S:SW_POOL_S + 1, :] = ps_ref[...]
        second = []
        for q, (fx, fy) in enumerate(((1, 0), (0, 1), (1, 1))):
            peer = (_flip(x, fx), _flip(y, fy), c)
            for src, dst, ss, rs in ((msend, mslot, s2, r2), (psend, pslot, s3, r3)):
                cp = pltpu.make_async_remote_copy(src_ref=src, dst_ref=dst.at[chip], send_sem=ss.at[q], recv_sem=rs.at[q],
                                                  device_id=peer, device_id_type=MESH)
                cp.start()
                second.append(cp)
        mslot[chip] = msend[...]
        pslot[chip] = psend[...]
        for cp in second:
            cp.wait()
        small_ref[...] = jnp.zeros_like(small_ref)
        for j in range(N_CHIP):
            for l in range(2):
                mod_ref[l, :, j * nw:(j + 1) * nw] = mslot[j, l] + b_ref[l:l + 1, j * nw:(j + 1) * nw]
            small_ref[0:SUBLANES, j * cq:(j + 1) * cq] = pslot[j, 0:SUBLANES, 0:cq]
            small_ref[SUBLANES:SW_ROWS, j * pq:(j + 1) * pq] = pslot[j, SUBLANES:SW_ROWS, :]

    args = (c8, mod_w, mod_b, conv_w, sc_w, pool_b, pool_s)
    dma3 = pltpu.SemaphoreType.DMA((N_CHIP - 1,))
    return pl.pallas_call(
        body, name="mod_fwd",
        in_specs=[VMEM] * len(args), out_specs=[VMEM] * 3,
        out_shape=[jax.ShapeDtypeStruct((SUBLANES, D), F32), jax.ShapeDtypeStruct((2, SUBLANES, N_CHIP * nw), F32),
                   jax.ShapeDtypeStruct((SW_ROWS, SW_COLS), F32)],
        scratch_shapes=[pltpu.VMEM((N_DEV, SUBLANES, D), F32), pltpu.VMEM((N_CHIP, 2, SUBLANES, nw), F32),
                        pltpu.VMEM((2, SUBLANES, nw), F32), pltpu.VMEM((N_CHIP, SW_ROWS, pq), F32),
                        pltpu.VMEM((SW_ROWS, pq), F32),
                        pltpu.SemaphoreType.DMA((N_DEV - 1,)), pltpu.SemaphoreType.DMA((N_DEV - 1,)),
                        dma3, dma3, dma3, dma3],
        compiler_params=_cp(),
    )(*args)


def _wcast(ws):
    def body(*refs):
        n = len(refs) // 2
        for a in range(n):
            refs[n + a][...] = refs[a][...].astype(BF16)

    return pl.pallas_call(
        body, name="wcast", in_specs=[VMEM] * len(ws), out_specs=[VMEM] * len(ws),
        out_shape=[jax.ShapeDtypeStruct(w.shape, BF16) for w in ws], compiler_params=_cp(),
    )(*ws)


def _wcast_own_block(w, kidx, name, after=()):
    rr, cc = w.shape
    rb = min(rr, 256)

    def body(k_ref, w_ref, *rest):
        rest[-1][...] = w_ref[...].astype(BF16)

    order = list(after)
    return pl.pallas_call(
        body, name=name,
        grid_spec=pltpu.PrefetchScalarGridSpec(
            num_scalar_prefetch=1, grid=(rr // rb,),
            in_specs=[pl.BlockSpec((rb, cc), lambda j, k_ref: (j, 0))] + [ANY] * len(order),
            out_specs=pl.BlockSpec((None, rb, cc), lambda j, k_ref: (k_ref[0], j, 0))),
        out_shape=jax.ShapeDtypeStruct((N_CHIP, rr, cc), BF16),
        compiler_params=_cp(("parallel",)),
    )(kidx, w, *order)


def _wgather_copies(outs, rows, ssem, rsem, fssem, frsem):
    n = len(outs)
    x, y, c = _pos()
    chip = 2 * x + y
    sib = (x, y, 1 - c)
    flips = ((1, 0), (0, 1), (1, 1))

    def half(a, which):
        hr = rows[a] // 2
        return pl.ds(pl.multiple_of(which * hr, BF16_ROWS), hr)

    sends = []
    for a in range(n):
        mine = outs[a].at[chip, half(a, c), :]
        for q, (fx, fy) in enumerate(flips):
            cp = pltpu.make_async_remote_copy(
                src_ref=mine, dst_ref=mine, send_sem=ssem.at[3 * a + q], recv_sem=rsem.at[3 * a + q],
                device_id=(_flip(x, fx), _flip(y, fy), c), device_id_type=MESH)
            cp.start()
            sends.append(cp)
    passed = []
    for a in range(n):
        for q, (fx, fy) in enumerate(flips):
            src_chip = 2 * _flip(x, fx) + _flip(y, fy)
            landed = outs[a].at[src_chip, half(a, c), :]
            pltpu.make_async_remote_copy(
                src_ref=landed, dst_ref=landed, send_sem=ssem.at[3 * a + q], recv_sem=rsem.at[3 * a + q],
                device_id=sib, device_id_type=MESH).wait_recv()
            cp = pltpu.make_async_remote_copy(
                src_ref=landed, dst_ref=landed, send_sem=fssem.at[3 * a + q], recv_sem=frsem.at[3 * a + q],
                device_id=sib, device_id_type=MESH)
            cp.start()
            passed.append(cp)
    for a in range(n):
        for q, (fx, fy) in enumerate(flips):
            src_chip = 2 * _flip(x, fx) + _flip(y, fy)
            other = outs[a].at[src_chip, half(a, 1 - c), :]
            pltpu.make_async_remote_copy(
                src_ref=other, dst_ref=other, send_sem=fssem.at[3 * a + q], recv_sem=frsem.at[3 * a + q],
                device_id=sib, device_id_type=MESH).wait_recv()
    for cp in sends + passed:
        cp.wait_send()


def _wgather_sequencer(bufs, name, collective_id):
    n = len(bufs)
    refs = [jax.new_ref(b, memory_space=pltpu.MemorySpace.HBM) for b in bufs]
    dma = pltpu.SemaphoreType.DMA((3 * n,))

    @pl.kernel(mesh=plsc.ScalarSubcoreMesh(axis_name="sequencer", num_cores=1), name=name,
               scratch_types=(dma, dma, dma, dma), compiler_params=pltpu.CompilerParams(collective_id=collective_id))
    def launch(ssem, rsem, fssem, frsem):
        x, y, c = _pos()
        barrier = pltpu.get_barrier_semaphore()
        for peer in ((1 - x, y, c), (x, 1 - y, c), (1 - x, 1 - y, c), (x, y, 1 - c)):
            pl.semaphore_signal(barrier, inc=1, device_id=peer, device_id_type=MESH)
        pl.semaphore_wait(barrier, 4)
        _wgather_copies(refs, [b.shape[1] for b in bufs], ssem, rsem, fssem, frsem)

    launch()
    return [r[...] for r in refs]


def _l0_fwd(x, g, sc, sh, w_in, gate, cw, cb, wa, ba, wx, bx, lam, sw, wo):
    s_len, nb = x.shape[0], w_in.shape[2]
    ts = _tile(s_len, TS_MIX)
    n_t = s_len // ts
    hl = SUBLANES

    def body(xa_ref, xb_ref, g_ref, sc_ref, sh_ref, win_ref, gate_ref, cw_ref, cb_ref, wa_ref, ba_ref, wx_ref, bx_ref,
             lam_ref, sw_ref, wo_ref, x1_ref, h_ref, y_ref, xc_ref, cz_ref, h0_ref, p_ref, pcur, pnext, cxa, czz, chh):
        i = pl.program_id(0)

        @pl.when(i == 0)
        def _():
            cxa[...] = jnp.zeros_like(cxa)
            czz[...] = jnp.zeros_like(czz)
            chh[...] = jnp.zeros_like(chh)
            pnext[...] = jnp.zeros_like(pnext)

        pcur[...] = pnext[...]
        xv = xa_ref[...]
        rinv = lax.rsqrt(jnp.mean(xv * xv, axis=-1, keepdims=True) + RMS_EPS)
        h0 = (xv * rinv * (g_ref[...] * (1.0 + sc_ref[...])) + sh_ref[...]).astype(BF16)
        h0_ref[...] = h0

        def project(k):
            def emit():
                pk = jnp.dot(h0, win_ref[k], preferred_element_type=F32).astype(BF16)
                p_ref[:, k * nb:(k + 1) * nb] = pk
                pnext[:, k * nb:(k + 1) * nb] = pk
            return emit

        def mixer():
            piece = lambda k: pcur[:, k * D:(k + 1) * D].astype(F32)
            xa = piece(0)
            rows = _rows(ts, D)
            taps = _conv_taps(jnp.concatenate([cxa[...], xa], axis=0), hl, ts, 4)
            xc = cb_ref[...] + sum(cw_ref[k:k + 1, :] * taps[k] for k in range(4))
            xc_ref[...] = xc.astype(BF16)
            r, ig = _lru_gates(xc, wa_ref, ba_ref[...], wx_ref, bx_ref[...])
            a, m, _ = _lru_decay(r, _softplus_neg(lam_ref[...]), (rows == 0) & (i == 1))
            yield 0.26
            h = _run(_scan_fwd_steps(a, m * ig * xc, chh[hl - 1:hl, :]))
            yield 0.51
            gcp, v = piece(3), piece(4)
            z = gcp * v
            ztaps = _conv_taps(jnp.concatenate([czz[...], z], axis=0), hl, ts, 3)
            cz = sum(sw_ref[k:k + 1, :] * ztaps[k] for k in range(3))
            cz_ref[...] = cz.astype(BF16)
            yb = piece(2) * cz
            ga, gb = piece(1), piece(5)
            y = jnp.concatenate([h * (ga * _sigmoid(ga)), yb * (gb * _sigmoid(gb))], axis=1).astype(BF16)
            yield 0.76
            y_ref[...] = y
            x1_ref[...] = xb_ref[...] + gate_ref[...] * jnp.dot(y, wo_ref[...], preferred_element_type=F32)
            h_ref[...] = h.astype(BF16)
            cxa[...] = xa[ts - hl:, :]
            czz[...] = z[ts - hl:, :]
            chh[...] = jnp.where(i > 0, h[ts - hl:, :], 0.0)

        _paired(mixer(), [project(k) for k in range(N_CHIP)])

    def full(a):
        return pl.BlockSpec(a.shape, lambda i: (0,) * a.ndim)

    ahead = lambda w: pl.BlockSpec((ts, w), lambda i: (jnp.minimum(i, n_t - 1), 0))
    behind = lambda w: pl.BlockSpec((ts, w), lambda i: (jnp.maximum(i - 1, 0), 0))
    args = (x, x, g, sc, sh, w_in, gate, cw, cb, wa, ba, wx, bx, lam, sw, wo)
    return pl.pallas_call(
        body, name="l0_fwd", grid=(n_t + 1,),
        in_specs=[ahead(D), behind(D)] + [full(a) for a in args[2:]],
        out_specs=[behind(D), behind(D), behind(2 * D), behind(D), behind(D), ahead(D), ahead(N_CHIP * nb)],
        out_shape=[jax.ShapeDtypeStruct((s_len, D), F32), jax.ShapeDtypeStruct((s_len, D), BF16),
                   jax.ShapeDtypeStruct((s_len, 2 * D), BF16), jax.ShapeDtypeStruct((s_len, D), BF16),
                   jax.ShapeDtypeStruct((s_len, D), BF16), jax.ShapeDtypeStruct((s_len, D), BF16),
                   jax.ShapeDtypeStruct((s_len, N_CHIP * nb), BF16)],
        scratch_shapes=[pltpu.VMEM((ts, N_CHIP * nb), BF16)] * 2 + [pltpu.VMEM((hl, D), F32)] * 3,
        compiler_params=_cp(("arbitrary",)),
    )(*args)


def _l1_fwd(x1, g, sc, sh, w_in, tgt, gate, wg, bg, scale, wo, gf):
    s_len, nb = x1.shape[0], w_in.shape[2]
    ts = _tile(s_len, TS_MIX)
    n_t = s_len // ts
    pw, gd, hl = 2 * D, POOL_GROUP_DIM, POOL_HALO

    def body(xa_ref, xb_ref, t_ref, g_ref, sc_ref, sh_ref, win_ref, gate_ref, wg_ref, bg_ref, scl_ref, wo_ref, gf_ref,
             d_ref, mx_ref, y_ref, dx_ref, loss_ref, dgf_ref, h1_ref, p_ref, pcur, pnext, cv):
        i = pl.program_id(0)

        @pl.when(i == 0)
        def _():
            cv[...] = jnp.zeros_like(cv)
            loss_ref[...] = jnp.zeros_like(loss_ref)
            dgf_ref[...] = jnp.zeros_like(dgf_ref)
            pnext[...] = jnp.zeros_like(pnext)

        pcur[...] = pnext[...]
        xv = xa_ref[...]
        rinv = lax.rsqrt(jnp.mean(xv * xv, axis=-1, keepdims=True) + RMS_EPS)
        h1 = (xv * rinv * (g_ref[...] * (1.0 + sc_ref[...])) + sh_ref[...]).astype(BF16)
        h1_ref[...] = h1

        def project(k):
            def emit():
                pk = jnp.dot(h1, win_ref[k], preferred_element_type=F32).astype(BF16)
                p_ref[:, k * nb:(k + 1) * nb] = pk
                pnext[:, k * nb:(k + 1) * nb] = pk
            return emit

        def mixer():
            v = pcur[:, 0:pw].astype(F32)
            sums = _window_sums(jnp.concatenate([cv[...], v], axis=0), _down)
            inv = _pool_inv_counts(jnp.maximum(i - 1, 0) * ts, ts)
            dd = [sums[k][hl:hl + ts] * inv[k] - v[:, k * gd:(k + 1) * gd] for k in range(4)]
            d_ref[...] = jnp.concatenate(dd, axis=1).astype(BF16)
            yield 0.26
            mixed = jnp.concatenate(
                [jnp.dot(dd[k].astype(BF16), wg_ref[k], preferred_element_type=F32) for k in range(4)], axis=1) + bg_ref[...]
            mx_ref[...] = mixed.astype(BF16)
            gg = pcur[:, pw:2 * pw].astype(F32)
            y = (mixed * scl_ref[...] * (gg * _sigmoid(gg))).astype(BF16)
            y_ref[...] = y
            yield 0.51
            x2 = xb_ref[...] + gate_ref[...] * jnp.dot(y, wo_ref[...], preferred_element_type=F32)
            yield 0.76
            r2 = lax.rsqrt(jnp.mean(x2 * x2, axis=-1, keepdims=True) + RMS_EPS)
            n2 = x2 * r2
            err = n2 * gf_ref[...] - t_ref[...]
            loss_ref[...] += jnp.where(i > 0, jnp.sum(err * err, axis=0, keepdims=True), 0.0)
            dyf = err * (1.0 / D)
            dgf_ref[...] += jnp.where(i > 0, jnp.sum(dyf * n2, axis=0, keepdims=True), 0.0)
            dn = dyf * gf_ref[...]
            dx_ref[...] = r2 * (dn - n2 * jnp.mean(dn * n2, axis=-1, keepdims=True))
            cv[...] = v[ts - hl:, :]

        _paired(mixer(), [project(k) for k in range(N_CHIP)])

    def full(a):
        return pl.BlockSpec(a.shape, lambda i: (0,) * a.ndim)

    ahead = lambda w: pl.BlockSpec((ts, w), lambda i: (jnp.minimum(i, n_t - 1), 0))
    behind = lambda w: pl.BlockSpec((ts, w), lambda i: (jnp.maximum(i - 1, 0), 0))
    acc = pl.BlockSpec((1, D), lambda i: (0, 0))
    args = (x1, x1, tgt, g, sc, sh, w_in, gate, wg, bg, scale, wo, gf)
    return pl.pallas_call(
        body, name="l1_fwd", grid=(n_t + 1,),
        in_specs=[ahead(D), behind(D), behind(D)] + [full(a) for a in args[3:]],
        out_specs=[behind(pw), behind(pw), behind(pw), behind(D), acc, acc, ahead(D), ahead(N_CHIP * nb)],
        out_shape=[jax.ShapeDtypeStruct((s_len, pw), BF16)] * 3 + [jax.ShapeDtypeStruct((s_len, D), F32)]
        + [jax.ShapeDtypeStruct((1, D), F32)] * 2
        + [jax.ShapeDtypeStruct((s_len, D), BF16), jax.ShapeDtypeStruct((s_len, N_CHIP * nb), BF16)],
        scratch_shapes=[pltpu.VMEM((ts, N_CHIP * nb), BF16)] * 2 + [pltpu.VMEM((hl, pw), F32)],
        compiler_params=_cp(("arbitrary",)),
    )(*args)


def _l1_bwd_mix(dx2, proj, mixed, y, dpool, gate, wg, scale, wo):
    s_len = dx2.shape[0]
    n_sub = 2
    ts = _tile(s_len, n_sub * TS_MIX)
    sub = ts // n_sub
    n_t = s_len // ts
    pw, gd, hl = 2 * D, POOL_GROUP_DIM, POOL_HALO

    def body(dx_ref, gg_ref, mx_ref, y_ref, d_ref, gate_ref, wg_ref, sc_ref, wo_ref,
             dp_ref, mt_ref, dwg_ref, dsc_ref, dbg_ref, cq):
        i = pl.program_id(0)

        @pl.when(i == 0)
        def _():
            cq[...] = jnp.zeros_like(cq)
            dsc_ref[...] = jnp.zeros_like(dsc_ref)
            dbg_ref[...] = jnp.zeros_like(dbg_ref)
            mt_ref[...] = jnp.zeros_like(mt_ref)
            dwg_ref[...] = jnp.zeros_like(dwg_ref)

        ahead_rows = {}

        def chain(j):
            rows = slice(j * sub, (j + 1) * sub)
            dxv = dx_ref[rows, :]
            dxb = dxv.astype(BF16)
            dy = lax.dot_general((gate_ref[...] * dxv).astype(BF16), wo_ref[...], NT, preferred_element_type=F32)
            for k in range(2):
                mt_ref[k] += lax.dot_general(y_ref[rows, k * gd:(k + 1) * gd], dxb, TN, preferred_element_type=F32)
            yield
            gg = gg_ref[rows, :].astype(F32)
            mixed = mx_ref[rows, :].astype(F32)
            s = _sigmoid(gg)
            sg = gg * s
            dym = dy * mixed
            dmixed = dy * sc_ref[...] * sg
            dsc_ref[...] += jnp.sum(dym * sg, axis=0, keepdims=True)
            dbg_ref[...] += jnp.sum(dmixed, axis=0, keepdims=True)
            dmb = dmixed.astype(BF16)
            dp_ref[rows, pw:2 * pw] = (dym * sc_ref[...] * (s + sg * (1.0 - s))).astype(BF16)
            yield
            inv = _pool_inv_counts((n_t - 1 - i) * ts + j * sub, sub)
            dd = []
            for k in range(4):
                dmk = dmb[:, k * gd:(k + 1) * gd]
                dd.append(lax.dot_general(dmk, wg_ref[k], NT, preferred_element_type=F32))
                dwg_ref[k] += lax.dot_general(d_ref[rows, k * gd:(k + 1) * gd], dmk, TN, preferred_element_type=F32)
            for k in range(2, 4):
                mt_ref[k] += lax.dot_general(y_ref[rows, k * gd:(k + 1) * gd], dxb, TN, preferred_element_type=F32)
            q = jnp.concatenate([dd[k] * inv[k] for k in range(4)], axis=1)
            ahead_rows[j] = q[0:hl, :]
            yield
            behind_q = cq[...] if j == n_sub - 1 else ahead_rows[j + 1]
            sums = _window_sums(jnp.concatenate([q, behind_q], axis=0), _up)
            dp_ref[rows, 0:pw] = jnp.concatenate([sums[k][0:sub] - dd[k] for k in range(4)], axis=1).astype(BF16)

        chains = [chain(j) for j in reversed(range(n_sub))]
        for _ in range(4):
            for ch in chains:
                next(ch, None)
        cq[...] = ahead_rows[0]

    def full(a):
        return pl.BlockSpec(a.shape, lambda i: (0,) * a.ndim)

    rev = lambda w, j=0: pl.BlockSpec((ts, w), lambda i: (n_t - 1 - i, j))
    acc = pl.BlockSpec((1, pw), lambda i: (0, 0))
    return pl.pallas_call(
        body, name="l1_bwd_mix", grid=(n_t,),
        in_specs=[rev(D), rev(pw, 1), rev(pw), rev(pw), rev(pw)] + [full(a) for a in (gate, wg, scale, wo)],
        out_specs=[rev(2 * pw), pl.BlockSpec((N_CHIP, gd, D), lambda i: (0, 0, 0)),
                   pl.BlockSpec((4, gd, gd), lambda i: (0, 0, 0)), acc, acc],
        out_shape=[jax.ShapeDtypeStruct((s_len, 2 * pw), BF16), jax.ShapeDtypeStruct((N_CHIP, gd, D), F32),
                   jax.ShapeDtypeStruct((4, gd, gd), F32),
                   jax.ShapeDtypeStruct((1, pw), F32), jax.ShapeDtypeStruct((1, pw), F32)],
        scratch_shapes=[pltpu.VMEM((hl, pw), F32)],
        compiler_params=_cp(("arbitrary",)),
    )(dx2, proj, mixed, y, dpool, gate, wg, scale, wo)


def _l0_bwd_mix(dx1, proj, hst, y, xc, cz, gate, cw, wa, ba, wx, bx, lam, sw, wo):
    s_len = dx1.shape[0]
    ts = _tile(s_len, TS_MIX)
    n_t = s_len // ts
    hl, hb = SUBLANES, BF16_ROWS
    yb_w = 2 * D // N_CHIP

    def body(dx_ref, p_ref, h_ref, hh_ref, y_ref, xc_ref, cz_ref, gate_ref, cw_ref, wa_ref, ba_ref, wx_ref, bx_ref,
             lam_ref, sw_ref, wo_ref, dp_ref, mt_ref, dwa_ref, dwx_ref, sm_ref, cg, cdxc, cdcz, ca):
        i = pl.program_id(0)
        ri = n_t - 1 - i

        @pl.when(i == 0)
        def _():
            cg[...] = jnp.zeros_like(cg)
            ca[...] = jnp.zeros_like(ca)
            cdxc[...] = jnp.zeros_like(cdxc)
            cdcz[...] = jnp.zeros_like(cdcz)
            sm_ref[...] = jnp.zeros_like(sm_ref)
            mt_ref[...] = jnp.zeros_like(mt_ref)
            dwa_ref[...] = jnp.zeros_like(dwa_ref)
            dwx_ref[...] = jnp.zeros_like(dwx_ref)

        dxb = dx_ref[...].astype(BF16)

        def wgrad_out(k):
            mt_ref[k] += lax.dot_general(y_ref[:, k * yb_w:(k + 1) * yb_w], dxb, TN, preferred_element_type=F32)

        wgrad_out(0)
        has_prev = (ri > 0).astype(F32)
        xa, ga, gbp, gcp, v, gb = [p_ref[:, k * D:(k + 1) * D].astype(F32) for k in range(6)]
        rows = _rows(ts, D)
        first = (rows == 0) & (ri == 0)
        xc = xc_ref[...].astype(F32)
        cz = cz_ref[...].astype(F32)
        r, ig = _lru_gates(xc, wa_ref, ba_ref[...], wx_ref, bx_ref[...])
        sp = _softplus_neg(lam_ref[...])
        a, m, inv_m = _lru_decay(r, sp, first)
        z = gcp * v
        h = h_ref[...].astype(F32)
        hprev = _down(jnp.concatenate([hh_ref[...].astype(F32)[hb - hl:hb] * has_prev, h], axis=0), 1)[hl:hl + ts]
        dy = lax.dot_general((gate_ref[...] * dx_ref[...]).astype(BF16), wo_ref[...], NT, preferred_element_type=F32)
        dya_pre, dyb_pre = dy[:, 0:D], dy[:, D:2 * D]
        s_a, s_b = _sigmoid(ga), _sigmoid(gb)
        silu_a, silu_b = ga * s_a, gb * s_b
        dp_ref[:, D:2 * D] = (dya_pre * h * (s_a + silu_a * (1.0 - s_a))).astype(BF16)
        dp_ref[:, 5 * D:6 * D] = (dyb_pre * (gbp * cz) * (s_b + silu_b * (1.0 - s_b))).astype(BF16)
        dya = dya_pre * silu_a
        dyb = dyb_pre * silu_b
        wgrad_out(1)
        dp_ref[:, 2 * D:3 * D] = (dyb * cz).astype(BF16)
        dcz = dyb * gbp
        dcz_ext = jnp.concatenate([dcz, cdcz[...]], axis=0)
        dcz_taps = [_up(dcz_ext, 2 - k)[0:ts] for k in range(3)]
        for k in range(3):
            sm_ref[8 + k:9 + k, :] += jnp.sum(z * dcz_taps[k], axis=0, keepdims=True)
        dz = sum(sw_ref[k:k + 1, :] * dcz_taps[k] for k in range(3))
        dp_ref[:, 3 * D:4 * D] = (dz * v).astype(BF16)
        dp_ref[:, 4 * D:5 * D] = (dz * gcp).astype(BF16)
        cdcz[...] = dcz[0:hl, :]
        alpha = _up(jnp.concatenate([a, ca[...]], axis=0), 1)[0:ts]
        wgrad_out(2)
        dh = _run(_scan_rev_steps(alpha, dya, cg[0:1, :]))
        wgrad_out(3)
        cg[...] = dh[0:hl, :]
        ca[...] = a[0:hl, :]
        da = dh * hprev
        dhx = dh * xc
        dm = dhx * ig
        di = dhx * m
        dxc = dh * (m * ig)
        dl = a * (da - jnp.where(first, 0.0, dm * a * inv_m))
        dlr = dl * r
        sm_ref[7:8, :] += jnp.sum(dlr, axis=0, keepdims=True) * (-LRU_C)
        dpa = dlr * (sp * (-LRU_C)) * (1.0 - r)
        dpx = di * ig * (1.0 - ig)
        sm_ref[5:6, :] += jnp.sum(dpa, axis=0, keepdims=True)
        sm_ref[6:7, :] += jnp.sum(dpx, axis=0, keepdims=True)
        dpa_b, dpx_b, xc_b = dpa.astype(BF16), dpx.astype(BF16), xc.astype(BF16)
        back = []
        for hd in range(LRU_HEADS):
            sl = slice(hd * LRU_HEAD_DIM, (hd + 1) * LRU_HEAD_DIM)
            back.append(lax.dot_general(dpa_b[:, sl], wa_ref[hd], NT, preferred_element_type=F32)
                        + lax.dot_general(dpx_b[:, sl], wx_ref[hd], NT, preferred_element_type=F32))
            dwa_ref[hd] += lax.dot_general(xc_b[:, sl], dpa_b[:, sl], TN, preferred_element_type=F32)
            dwx_ref[hd] += lax.dot_general(xc_b[:, sl], dpx_b[:, sl], TN, preferred_element_type=F32)
        dxc = dxc + jnp.concatenate(back, axis=1)
        sm_ref[4:5, :] += jnp.sum(dxc, axis=0, keepdims=True)
        dxc_ext = jnp.concatenate([dxc, cdxc[...]], axis=0)
        dxc_taps = [_up(dxc_ext, 3 - k)[0:ts] for k in range(4)]
        for k in range(4):
            sm_ref[k:k + 1, :] += jnp.sum(xa * dxc_taps[k], axis=0, keepdims=True)
        dp_ref[:, 0:D] = sum(cw_ref[k:k + 1, :] * dxc_taps[k] for k in range(4)).astype(BF16)
        cdxc[...] = dxc[0:hl, :]

    def full(a):
        return pl.BlockSpec(a.shape, lambda i: (0,) * a.ndim)

    rev = lambda w: pl.BlockSpec((ts, w), lambda i: (n_t - 1 - i, 0))
    halo = lambda w: pl.BlockSpec((hb, w), lambda i: (jnp.maximum((n_t - 1 - i) * (ts // hb) - 1, 0), 0))
    return pl.pallas_call(
        body, name="l0_bwd_mix", grid=(n_t,),
        in_specs=[rev(D), rev(6 * D), rev(D), halo(D), rev(2 * D), rev(D), rev(D)]
        + [full(a) for a in (gate, cw, wa, ba, wx, bx, lam, sw, wo)],
        out_specs=[rev(6 * D), pl.BlockSpec((N_CHIP, yb_w, D), lambda i: (0, 0, 0)),
                   pl.BlockSpec(wa.shape, lambda i: (0, 0, 0)), pl.BlockSpec(wa.shape, lambda i: (0, 0, 0)),
                   pl.BlockSpec((2 * SUBLANES, D), lambda i: (0, 0))],
        out_shape=[jax.ShapeDtypeStruct((s_len, 6 * D), BF16), jax.ShapeDtypeStruct((N_CHIP, yb_w, D), F32),
                   jax.ShapeDtypeStruct(wa.shape, F32), jax.ShapeDtypeStruct(wa.shape, F32),
                   jax.ShapeDtypeStruct((2 * SUBLANES, D), F32)],
        scratch_shapes=[pltpu.VMEM((hl, D), F32)] * 4,
        compiler_params=_cp(("arbitrary",)),
    )(dx1, proj, hst, hst, y, xc, cz, gate, cw, wa, ba, wx, bx, lam, sw, wo)


def _dgrad_norm(dproj, w, x, dres, g, sc, name, after=None):
    s_len, nb = x.shape[0], w.shape[2]
    ts = _tile(s_len, TS_DGRAD)
    order = [] if after is None else [after]

    def body(dp_ref, w_ref, x_ref, dr_ref, g_ref, sc_ref, *rest):
        dx_ref, s1_ref, s2_ref = rest[len(order):]

        @pl.when(pl.program_id(0) == 0)
        def _():
            s1_ref[...] = jnp.zeros_like(s1_ref)
            s2_ref[...] = jnp.zeros_like(s2_ref)

        dh = sum(lax.dot_general(dp_ref[:, k * nb:(k + 1) * nb], w_ref[k], NT, preferred_element_type=F32)
                 for k in range(N_CHIP))
        xv = x_ref[...]
        r = lax.rsqrt(jnp.mean(xv * xv, axis=-1, keepdims=True) + RMS_EPS)
        n = xv * r
        s1_ref[...] += jnp.sum(dh, axis=0, keepdims=True)
        s2_ref[...] += jnp.sum(dh * n, axis=0, keepdims=True)
        dn = dh * (g_ref[...] * (1.0 + sc_ref[...]))
        dx_ref[...] = dr_ref[...] + r * (dn - n * jnp.mean(dn * n, axis=-1, keepdims=True))

    row = lambda wd: pl.BlockSpec((ts, wd), lambda i: (i, 0))
    vec = pl.BlockSpec((1, D), lambda i: (0, 0))
    return pl.pallas_call(
        body, name=name, grid=(s_len // ts,),
        in_specs=[row(N_CHIP * nb), pl.BlockSpec(w.shape, lambda i: (0, 0, 0)), row(D), row(D), vec, vec]
        + [ANY] * len(order),
        out_specs=[row(D), vec, vec],
        out_shape=[jax.ShapeDtypeStruct((s_len, D), F32)] + [jax.ShapeDtypeStruct((1, D), F32)] * 2,
        compiler_params=_cp(("arbitrary",)),
    )(dproj, w, x, dres, g, sc, *order)


def _wgrad(a, b, groups, ka, nb, a_col, b_col, name, after=None):
    s_len = a.shape[0]
    ts = _tile(s_len, TS_WGRAD)
    n_s = s_len // ts
    order = [] if after is None else [after]

    def body(a_ref, b_ref, *rest):
        o_ref, wire_ref = rest[-2:]

        @pl.when(pl.program_id(1) == 0)
        def _():
            o_ref[...] = jnp.zeros_like(o_ref)

        o_ref[...] += lax.dot_general(a_ref[...].astype(BF16), b_ref[...].astype(BF16), TN, preferred_element_type=F32)

        @pl.when(pl.program_id(1) == n_s - 1)
        def _():
            wire_ref[...] = o_ref[...].astype(GRAD_WIRE_DTYPE)

    blk = pl.BlockSpec((None, ka, nb), lambda g, s: (g, 0, 0))
    return pl.pallas_call(
        body, name=name, grid=(groups, n_s),
        in_specs=[pl.BlockSpec((ts, ka), lambda g, s: (s, a_col(g))), pl.BlockSpec((ts, nb), lambda g, s: (s, b_col(g)))]
        + [ANY] * len(order),
        out_specs=[blk, blk],
        out_shape=[jax.ShapeDtypeStruct((groups, ka, nb), F32), jax.ShapeDtypeStruct((groups, ka, nb), GRAD_WIRE_DTYPE)],
        compiler_params=_cp(("parallel", "arbitrary")),
    )(a, b, *order)


def _wo_final(mt, wo, gate, name):
    rb = mt.shape[1]

    def body(m_ref, w_ref, gate_ref, dw_ref, wire_ref, dg_ref):
        @pl.when(pl.program_id(0) == 0)
        def _():
            dg_ref[...] = jnp.zeros_like(dg_ref)

        mv = m_ref[...]
        dw = mv * gate_ref[...]
        dw_ref[...] = dw
        wire_ref[...] = dw.astype(GRAD_WIRE_DTYPE)
        dg_ref[...] += jnp.sum(mv * w_ref[...].astype(F32), axis=0, keepdims=True)

    blk = pl.BlockSpec((None, rb, D), lambda k: (k, 0, 0))
    vec = pl.BlockSpec((1, D), lambda k: (0, 0))
    return pl.pallas_call(
        body, name=name, grid=(N_CHIP,), in_specs=[blk, blk, vec], out_specs=[blk, blk, vec],
        out_shape=[jax.ShapeDtypeStruct(mt.shape, F32), jax.ShapeDtypeStruct(mt.shape, GRAD_WIRE_DTYPE),
                   jax.ShapeDtypeStruct((1, D), F32)],
        compiler_params=_cp(("arbitrary",)),
    )(mt, wo, gate)


ROW_NORM_G, ROW_CONV_W, ROW_CONV_B, ROW_B_A, ROW_B_X, ROW_LAMBDA, ROW_SC_W, ROW_POOL_B, ROW_POOL_S, ROW_FINAL_G = (
    0, 2, 6, 7, 8, 9, 10, 13, 15, 17)
ROW_LOSS = 18


def _small_pack(s1_0, s2_0, s1_1, s2_1, sm0, dsc1, dbg1, dgf, losscols, dgate0, dgate1, norm_g, sc0, sc1, lam):
    def body(s1_0r, s2_0r, s1_1r, s2_1r, sm, dsc, dbg, dgfr, lcols, dg0, dg1, ng, sc0r, sc1r, lamr, buf, dmod):
        buf[...] = jnp.zeros_like(buf)
        buf[0:1, :] = s2_0r[...] * (1.0 + sc0r[...])
        buf[1:2, :] = s2_1r[...] * (1.0 + sc1r[...])
        buf[ROW_CONV_W:ROW_CONV_W + 4, :] = sm[0:4, :]
        buf[ROW_CONV_B:ROW_CONV_B + 1, :] = sm[4:5, :]
        buf[ROW_B_A:ROW_B_A + 1, :] = sm[5:6, :]
        buf[ROW_B_X:ROW_B_X + 1, :] = sm[6:7, :]
        buf[ROW_LAMBDA:ROW_LAMBDA + 1, :] = -sm[7:8, :] * _sigmoid(-lamr[...])
        buf[ROW_SC_W:ROW_SC_W + 3, :] = sm[8:11, :]
        for k in range(2):
            buf[ROW_POOL_B + k:ROW_POOL_B + k + 1, :] = dbg[:, k * D:(k + 1) * D]
            buf[ROW_POOL_S + k:ROW_POOL_S + k + 1, :] = dsc[:, k * D:(k + 1) * D]
        buf[ROW_FINAL_G:ROW_FINAL_G + 1, :] = dgfr[...]
        pieces = (s1_0r[...], s2_0r[...] * ng[0:1, :], dg0[...], s1_1r[...], s2_1r[...] * ng[1:2, :], dg1[...])
        for k, pc in enumerate(pieces):
            dmod[:, k * D:(k + 1) * D] = jnp.broadcast_to(pc, (SUBLANES, D))
        buf[ROW_LOSS:ROW_LOSS + 1, :] = jnp.broadcast_to(jnp.sum(lcols[...], axis=1, keepdims=True) * (0.5 / D), (1, D))

    args = (s1_0, s2_0, s1_1, s2_1, sm0, dsc1, dbg1, dgf, losscols, dgate0, dgate1, norm_g, sc0, sc1, lam)
    return pl.pallas_call(
        body, name="small_pack", in_specs=[VMEM] * len(args), out_specs=[VMEM] * 2,
        out_shape=[jax.ShapeDtypeStruct((SMALL_ROWS, D), F32), jax.ShapeDtypeStruct((SUBLANES, 6 * D), F32)],
        compiler_params=_cp(),
    )(*args)


def _small_comm(buf_a, buf_b, dmod8):
    ra, rb = buf_a.shape[0] // N_DEV, buf_b.shape[0] // N_DEV
    wb = buf_b.shape[1]

    def body(a_ref, b_ref, dm_ref, oa_ref, ob_ref, odm_ref, ina, inb, dslot, sa, sb, s1, r1, s2, r2):
        x, y, c = _pos()
        me = 4 * x + 2 * y + c
        peers = []
        for r in range(1, N_DEV):
            fx, fy, fc = (r >> 2) & 1, (r >> 1) & 1, r & 1
            px, py, pc = _flip(x, fx), _flip(y, fy), _flip(c, fc)
            peers.append(((px, py, pc), 4 * px + 2 * py + pc))
        seg_a = lambda d: pl.ds(pl.multiple_of(d * ra, SUBLANES), ra)
        seg_b = lambda d: pl.ds(pl.multiple_of(d * rb, SUBLANES), rb)
        first = []
        for r, (peer, pid) in enumerate(peers):
            for k, (src, dst) in enumerate(((a_ref.at[seg_a(pid), :], ina.at[r]), (b_ref.at[seg_b(pid), :], inb.at[r]),
                                            (dm_ref, dslot.at[me]))):
                cp = pltpu.make_async_remote_copy(src_ref=src, dst_ref=dst, send_sem=s1.at[3 * r + k],
                                                  recv_sem=r1.at[3 * r + k], device_id=peer, device_id_type=MESH)
                cp.start()
                first.append(cp)
        dslot[me] = dm_ref[...]
        for cp in first:
            cp.wait()
        acc_a, acc_b = a_ref[seg_a(me), :], b_ref[seg_b(me), :]
        for r in range(N_DEV - 1):
            acc_a = acc_a + ina[r]
            acc_b = acc_b + inb[r]
        sa[...] = acc_a
        sb[...] = acc_b
        oa_ref[seg_a(me), :] = acc_a
        ob_ref[seg_b(me), :] = acc_b
        second = []
        for r, (peer, pid) in enumerate(peers):
            for k, (src, dst) in enumerate(((sa, oa_ref.at[seg_a(me), :]), (sb, ob_ref.at[seg_b(me), :]))):
                cp = pltpu.make_async_remote_copy(src_ref=src, dst_ref=dst, send_sem=s2.at[2 * r + k],
                                                  recv_sem=r2.at[2 * r + k], device_id=peer, device_id_type=MESH)
                cp.start()
                second.append(cp)
        rows = _rows(SUBLANES, dm_ref.shape[1])
        dm_all = jnp.zeros(dm_ref.shape, F32)
        for d in range(N_DEV):
            dm_all = jnp.where(rows == d, dslot[d], dm_all)
        odm_ref[...] = dm_all
        for cp in second:
            cp.wait()

    nrel = N_DEV - 1
    return pl.pallas_call(
        body, name="small_comm", in_specs=[VMEM] * 3, out_specs=[VMEM] * 3,
        out_shape=[jax.ShapeDtypeStruct(buf_a.shape, F32), jax.ShapeDtypeStruct(buf_b.shape, F32),
                   jax.ShapeDtypeStruct(dmod8.shape, F32)],
        scratch_shapes=[pltpu.VMEM((nrel, ra, D), F32), pltpu.VMEM((nrel, rb, wb), F32),
                        pltpu.VMEM((N_DEV,) + dmod8.shape, F32), pltpu.VMEM((ra, D), F32), pltpu.VMEM((rb, wb), F32),
                        pltpu.SemaphoreType.DMA((3 * nrel,)), pltpu.SemaphoreType.DMA((3 * nrel,)),
                        pltpu.SemaphoreType.DMA((2 * nrel,)), pltpu.SemaphoreType.DMA((2 * nrel,))],
        compiler_params=_cp(),
    )(buf_a, buf_b, dmod8)


def _adam(w, g, m, v):
    m2 = ADAM_B1 * m + (1.0 - ADAM_B1) * g
    v2 = ADAM_B2 * v + (1.0 - ADAM_B2) * (g * g)
    m_hat = m2 / (1.0 - ADAM_B1 ** ADAM_STEP)
    v_hat = v2 / (1.0 - ADAM_B2 ** ADAM_STEP)
    return -ADAM_LR * (m_hat / (jnp.sqrt(v_hat) + ADAM_EPS) + ADAM_WD * w), m2, v2


def _small_adam(red_a, red_b, dm_all, params):
    n = len(params)

    def body(*refs):
        ra, rb, dm = refs[:3]
        wmv = refs[3:3 + 3 * n]
        outs = refs[3 + 3 * n:]
        x, y, _ = _pos()
        chip = 2 * x + y

        def shard(row0, nrows, width):
            per_row = D // width
            cands = []
            for k in range(N_CHIP):
                if nrows == 1 or per_row >= N_CHIP:
                    cands.append(ra[row0:row0 + nrows, k * width:(k + 1) * width])
                else:
                    rr, cc = divmod(k * width, D)
                    cands.append(ra[row0 + rr:row0 + rr + 1, cc:cc + width])
            g = cands[0]
            for k in range(1, N_CHIP):
                g = jnp.where(chip == k, cands[k], g)
            return g

        dms = jnp.sum(dm[...], axis=0, keepdims=True)
        hw = LRU_HEADS * LRU_HEAD_DIM
        grads = [
            ra[ROW_NORM_G:ROW_NORM_G + 2, :],
            None,
            shard(ROW_CONV_W, 4, D // N_CHIP),
            ra[ROW_CONV_B:ROW_CONV_B + 1, :],
            rb[0:hw, :],
            ra[ROW_B_A:ROW_B_A + 1, :],
            rb[hw:2 * hw, :],
            ra[ROW_B_X:ROW_B_X + 1, :],
            ra[ROW_LAMBDA:ROW_LAMBDA + 1, :],
            shard(ROW_SC_W, 3, D // N_CHIP),
            shard(ROW_POOL_B, 2, 2 * D // N_CHIP),
            shard(ROW_POOL_S, 2, 2 * D // N_CHIP),
            ra[ROW_FINAL_G:ROW_FINAL_G + 1, :],
        ]
        for p in range(n):
            w_ref, m_ref, v_ref = wmv[3 * p:3 * p + 3]
            g_out, d_out, m_out, v_out = outs[4 * p:4 * p + 4]
            if grads[p] is None:
                for l in range(2):
                    g = dms[:, l * 3 * D:(l + 1) * 3 * D]
                    dl, m2, v2 = _adam(w_ref[l:l + 1, :], g, m_ref[l:l + 1, :], v_ref[l:l + 1, :])
                    g_out[l:l + 1, :] = g
                    d_out[l:l + 1, :] = dl
                    m_out[l:l + 1, :] = m2
                    v_out[l:l + 1, :] = v2
            else:
                g = grads[p]
                dl, m2, v2 = _adam(w_ref[...], g, m_ref[...], v_ref[...])
                g_out[...] = g
                d_out[...] = dl
                m_out[...] = m2
                v_out[...] = v2

    flat = [a for p in params for a in p]
    return pl.pallas_call(
        body, name="small_adam", in_specs=[VMEM] * (3 + len(flat)), out_specs=[VMEM] * (4 * n),
        out_shape=[jax.ShapeDtypeStruct(p[0].shape, F32) for p in params for _ in range(4)],
        compiler_params=_cp(),
    )(red_a, red_b, dm_all, *flat)


def _modw_adam(ca_t, dm_sh, w, m, v):
    nw = w.shape[2]

    def body(c_ref, d_ref, w_ref, m_ref, v_ref, g_out, d_out, m_out, v_out):
        g = jnp.dot(c_ref[...], d_ref[...], precision=lax.Precision.HIGHEST, preferred_element_type=F32)
        dl, m2, v2 = _adam(w_ref[...], g, m_ref[...], v_ref[...])
        g_out[...] = g
        d_out[...] = dl
        m_out[...] = m2
        v_out[...] = v2

    blk = pl.BlockSpec((None, D, nw), lambda l: (l, 0, 0))
    return pl.pallas_call(
        body, name="modw_adam", grid=(2,),
        in_specs=[pl.BlockSpec((D, SUBLANES), lambda l: (0, 0)), pl.BlockSpec((None, SUBLANES, nw), lambda l: (l, 0, 0)),
                  blk, blk, blk],
        out_specs=[blk] * 4, out_shape=[jax.ShapeDtypeStruct(w.shape, F32)] * 4,
        compiler_params=_cp(("arbitrary",)),
    )(ca_t, dm_sh, w, m, v)


def _exchange(copies, name, out_type, n_sems, args, sequencer, after=None):
    order = [] if after is None else [after]
    n_in, n_out = len(args) + len(order), len(out_type)

    def body(*refs):
        barrier = pltpu.get_barrier_semaphore()
        peers = sequencer[1](*_pos())
        for peer in peers:
            pl.semaphore_signal(barrier, inc=1, device_id=peer, device_id_type=MESH)
        pl.semaphore_wait(barrier, len(peers))
        copies(refs[:n_in], refs[n_in:n_in + n_out], refs[n_in + n_out], refs[n_in + n_out + 1])

    sems = [pltpu.SemaphoreType.DMA((n_sems,))] * 2
    return pl.kernel(body, out_type, mesh=plsc.ScalarSubcoreMesh(axis_name="sequencer", num_cores=1), name=name,
                     scratch_types=sems, compiler_params=pltpu.CompilerParams(collective_id=sequencer[0]))(*args, *order)


def _sibling(x, y, c):
    return [(x, y, 1 - c)]


def _other_chips(x, y, c):
    return [(1 - x, y, c), (x, 1 - y, c), (1 - x, 1 - y, c)]


def _to_wire(g, name, after=None):
    _, rr, cc = g.shape
    rb = min(rr, 256)

    def body(g_ref, *rest):
        rest[-1][...] = g_ref[...].astype(GRAD_WIRE_DTYPE)

    order = [] if after is None else [after]
    blk = pl.BlockSpec((None, rb, cc), lambda k, j: (k, j, 0))
    return pl.pallas_call(
        body, name=name, grid=(N_CHIP, rr // rb), in_specs=[blk] + [ANY] * len(order), out_specs=blk,
        out_shape=jax.ShapeDtypeStruct(g.shape, GRAD_WIRE_DTYPE), compiler_params=_cp(("parallel", "parallel")),
    )(g, *order)


def _chip_scatter(ps, name, collective_id, after=None):
    n = len(ps)

    def copies(ins, outs, ssem, rsem):
        x, y, c = _pos()
        cps = []
        for a in range(n):
            for q, (fx, fy) in enumerate(((1, 0), (0, 1), (1, 1))):
                px, py = _flip(x, fx), _flip(y, fy)
                cp = pltpu.make_async_remote_copy(
                    src_ref=ins[a].at[2 * px + py], dst_ref=outs[a].at[q],
                    send_sem=ssem.at[3 * a + q], recv_sem=rsem.at[3 * a + q], device_id=(px, py, c), device_id_type=MESH)
                cp.start()
                cps.append(cp)
        for cp in cps:
            cp.wait()

    out_type = [jax.ShapeDtypeStruct((N_CHIP - 1,) + p.shape[1:], p.dtype) for p in ps]
    return _exchange(copies, name, out_type, 3 * n, ps, (collective_id, _other_chips), after)


def _add_owner(p, got, chipidx, name, after=None):
    _, hr, cc = p.shape
    rb = min(hr, 256)

    def body(k_ref, p_ref, r_ref, *rest):
        rest[-1][...] = ((p_ref[...].astype(F32) + r_ref[0].astype(F32)) + r_ref[1].astype(F32)) + r_ref[2].astype(F32)

    order = [] if after is None else [after]
    return pl.pallas_call(
        body, name=name,
        grid_spec=pltpu.PrefetchScalarGridSpec(
            num_scalar_prefetch=1, grid=(hr // rb,),
            in_specs=[pl.BlockSpec((None, rb, cc), lambda j, k_ref: (k_ref[0], j, 0)),
                      pl.BlockSpec((N_CHIP - 1, rb, cc), lambda j, k_ref: (0, j, 0))] + [ANY] * len(order),
            out_specs=pl.BlockSpec((rb, cc), lambda j, k_ref: (j, 0))),
        out_shape=jax.ShapeDtypeStruct((hr, cc), F32),
        compiler_params=_cp(("parallel",)),
    )(chipidx, p, got, *order)


def _sib_exchange(ts_, name, collective_id, after=None):
    n = len(ts_)

    def copies(ins, outs, ssem, rsem):
        x, y, c = _pos()
        cps = []
        for a in range(n):
            cp = pltpu.make_async_remote_copy(src_ref=ins[a], dst_ref=outs[a], send_sem=ssem.at[a],
                                              recv_sem=rsem.at[a], device_id=(x, y, 1 - c), device_id_type=MESH)
            cp.start()
            cps.append(cp)
        for cp in cps:
            cp.wait()

    out_type = [jax.ShapeDtypeStruct(t.shape, F32) for t in ts_]
    return _exchange(copies, name, out_type, n, ts_, (collective_id, _sibling), after)


def _adam_2d(w, g_own, g_sib, m, v, name):
    rr, cc = w.shape
    rb = min(rr, 256)

    def body(w_ref, go_ref, gs_ref, m_ref, v_ref, g_out, d_out, m_out, v_out):
        g = go_ref[...] + gs_ref[...]
        dl, m2, v2 = _adam(w_ref[...], g, m_ref[...], v_ref[...])
        g_out[...] = g
        d_out[...] = dl
        m_out[...] = m2
        v_out[...] = v2

    blk = pl.BlockSpec((rb, cc), lambda j: (j, 0))
    return pl.pallas_call(
        body, name=name, grid=(rr // rb,), in_specs=[blk] * 5, out_specs=[blk] * 4,
        out_shape=[jax.ShapeDtypeStruct((rr, cc), F32)] * 4, compiler_params=_cp(("parallel",)),
    )(w, g_own, g_sib, m, v)


def kernel(x, c, norm_g, mod_w, mod_b, hy_w_in, hy_conv_w, hy_conv_b, lru_w_a, lru_b_a, lru_w_x, lru_b_x, lru_lambda, sc_conv_w, hy_w_out, pool_w_in, pool_w_grp, pool_b_grp, pool_scale, pool_w_out, final_g, loss_target, m_norm_g, m_mod_w, m_mod_b, m_hy_w_in, m_hy_conv_w, m_hy_conv_b, m_lru_w_a, m_lru_b_a, m_lru_w_x, m_lru_b_x, m_lru_lambda, m_sc_conv_w, m_hy_w_out, m_pool_w_in, m_pool_w_grp, m_pool_b_grp, m_pool_scale, m_pool_w_out, m_final_g, v_norm_g, v_mod_w, v_mod_b, v_hy_w_in, v_hy_conv_w, v_hy_conv_b, v_lru_w_a, v_lru_b_a, v_lru_w_x, v_lru_b_x, v_lru_lambda, v_sc_conv_w, v_hy_w_out, v_pool_w_in, v_pool_w_grp, v_pool_b_grp, v_pool_scale, v_pool_w_out, v_final_g):
    ax, ay, ac = _pos()
    me = 4 * ax + 2 * ay + ac
    chip = 2 * ax + ay
    xs = x[0]
    tgt = loss_target[0]
    gd = POOL_GROUP_DIM
    kidx = chip.reshape(1).astype(jnp.int32)

    big = [hy_w_in[0], hy_w_out[0], pool_w_in[0], pool_w_grp[0].reshape(4 * 128, gd), pool_w_out[0]]
    w_in0, w_out0 = _wgather_sequencer(
        [_wcast_own_block(w, kidx, f"wcast_own_block_{a}") for a, w in enumerate(big[:2])], "wgather_l0", CIDS_WGATHER[0])

    ca_all, mod_all, small_w = _mod_fwd(jnp.broadcast_to(c, (SUBLANES, D)), mod_w, mod_b,
                                        hy_conv_w[0], sc_conv_w[0], pool_b_grp, pool_scale)
    mod_me = lax.dynamic_index_in_dim(mod_all, me, axis=1, keepdims=False)
    sh0, sc0, gt0 = (mod_me[0:1, k * D:(k + 1) * D] for k in range(3))
    sh1, sc1, gt1 = (mod_me[1:2, k * D:(k + 1) * D] for k in range(3))
    cw = small_w[SW_CONV:SW_CONV + 4, 0:D]
    sw = small_w[SW_SC:SW_SC + 3, 0:D]
    pool_b = small_w[SW_POOL_B:SW_POOL_B + 1, :]
    pool_s = small_w[SW_POOL_S:SW_POOL_S + 1, :]
    g0, g1, gf = norm_g[0:1], norm_g[1:2], final_g.reshape(1, D)
    cb, ba, bx, lam = hy_conv_b, lru_b_a, lru_b_x, lru_lambda

    w_in1, w_grp, w_out1 = _wgather_sequencer(
        [_wcast_own_block(w, kidx, f"wcast_own_block_{a + 2}", after=(w_out0, small_w)) for a, w in enumerate(big[2:])],
        "wgather_l1", CIDS_WGATHER[1])
    w_grp =w_grp.reshape(N_CHIP, 4, 128, gd).transpose(1, 0, 2, 3).reshape(4, gd, gd)
    wa_b, wx_b = _wcast([lru_w_a[0], lru_w_x[0]])

    x1, hst, y0, xc0, cz0, h0, proj0 = _l0_fwd(xs, g0, sc0, sh0, w_in0, gt0, cw, cb, wa_b, ba, wx_b, bx, lam, sw,
                                               w_out0.reshape(2 * D, D))
    dpool, mixed, y1, dx2, losscols, dgf, h1, proj1 = _l1_fwd(x1, g1, sc1, sh1, w_in1, tgt, gt1, w_grp, pool_b, pool_s,
                                                              w_out1.reshape(2 * D, D), gf)

    def add_owners(grads, got, tag, ids, after):
        own = []
        for a, (g, r) in enumerate(zip(grads, got)):
            own.append(_add_owner(g, r, kidx, f"grad_add_owner_{tag}{a}", own[-1] if own else after))
        return own, _sib_exchange(own, f"grad_sib_exchange_{tag}", ids[1])

    dproj1, mt1, d_wgrp, dsc1, dbg1 = _l1_bwd_mix(dx2, proj1, mixed, y1, dpool, gt1, w_grp, pool_s,
                                                  w_out1.reshape(2 * D, D))
    d_win1, wire_win1 = _wgrad(h1, dproj1, N_CHIP, D, D, lambda g: 0, lambda g: g, "l1_wgrad_in")
    d_wout1, wire_wout1, dgate1 = _wo_final(mt1, w_out1, gt1, "l1_wo_final")
    d_wgrp = d_wgrp.reshape(4, N_CHIP, 128, gd).transpose(1, 0, 2, 3).reshape(N_CHIP, 4 * 128, gd)
    grads_l1 = [d_win1, d_wgrp, d_wout1]
    got_l1 = _chip_scatter([wire_win1, _to_wire(d_wgrp, "grad_to_wire_grp"), wire_wout1], "grad_chip_scatter_l1",
                           CIDS_L1[0])
    dx1, s1_1, s2_1 = _dgrad_norm(dproj1, w_in1, x1, dx2, g1, sc1, "l1_bwd_proj")

    dproj0, mt0, d_wa, d_wx, sm0 = _l0_bwd_mix(dx1, proj0, hst, y0, xc0, cz0, gt0, cw, wa_b, ba, wx_b, bx, lam, sw,
                                               w_out0.reshape(2 * D, D))
    sums_l1, sib_l1 = add_owners(grads_l1, got_l1, "l1", CIDS_L1, after=sm0)
    d_win0, wire_win0 = _wgrad(h0, dproj0, N_CHIP, D, 6 * D // N_CHIP, lambda g: 0, lambda g: g, "l0_wgrad_in",
                               after=sums_l1[-1])
    d_wout0, wire_wout0, dgate0 = _wo_final(mt0, w_out0, gt0, "l0_wo_final")
    grads_l0 = [d_win0, d_wout0]
    got_l0 = _chip_scatter([wire_win0, wire_wout0], "grad_chip_scatter_l0", CIDS_L0[0], after=sib_l1[0])
    grad_x, s1_0, s2_0 = _dgrad_norm(dproj0, w_in0, xs, dx1, g0, sc0, "l0_bwd_proj", after=wire_win0)
    sums_l0, sib_l0 = add_owners(grads_l0, got_l0, "l0", CIDS_L0, after=s1_0)

    buf_a, dmod8 = _small_pack(s1_0, s2_0, s1_1, s2_1, sm0, dsc1, dbg1, dgf, losscols, dgate0, dgate1,
                                      norm_g, sc0, sc1, lam)
    hw = LRU_HEADS * LRU_HEAD_DIM
    buf_b = jnp.concatenate([d_wa.reshape(hw, LRU_HEAD_DIM), d_wx.reshape(hw, LRU_HEAD_DIM)], axis=0)
    red_a, red_b, dm_all = _small_comm(buf_a, buf_b, dmod8)
    small = [(norm_g, m_norm_g, v_norm_g), (mod_b, m_mod_b, v_mod_b),
             (hy_conv_w[0], m_hy_conv_w[0], v_hy_conv_w[0]), (hy_conv_b, m_hy_conv_b, v_hy_conv_b),
             tuple(a.reshape(hw, LRU_HEAD_DIM) for a in (lru_w_a, m_lru_w_a, v_lru_w_a)),
             (lru_b_a, m_lru_b_a, v_lru_b_a),
             tuple(a.reshape(hw, LRU_HEAD_DIM) for a in (lru_w_x, m_lru_w_x, v_lru_w_x)),
             (lru_b_x, m_lru_b_x, v_lru_b_x), (lru_lambda, m_lru_lambda, v_lru_lambda),
             (sc_conv_w[0], m_sc_conv_w[0], v_sc_conv_w[0]), (pool_b_grp, m_pool_b_grp, v_pool_b_grp),
             (pool_scale, m_pool_scale, v_pool_scale),
             tuple(a.reshape(1, D) for a in (final_g, m_final_g, v_final_g))]
    small_names = ["norm_g", "mod_b", "hy_conv_w", "hy_conv_b", "lru_w_a", "lru_b_a", "lru_w_x", "lru_b_x",
                   "lru_lambda", "sc_conv_w", "pool_b_grp", "pool_scale", "final_g"]
    small_out = _small_adam(red_a, red_b, dm_all, small)
    res = {}
    shapes = dict(norm_g=norm_g, mod_b=mod_b, hy_conv_w=hy_conv_w, hy_conv_b=hy_conv_b, lru_w_a=lru_w_a, lru_b_a=lru_b_a,
                  lru_w_x=lru_w_x, lru_b_x=lru_b_x, lru_lambda=lru_lambda, sc_conv_w=sc_conv_w, pool_b_grp=pool_b_grp,
                  pool_scale=pool_scale, final_g=final_g)
    for p, nm in enumerate(small_names):
        res[nm] = tuple(o.reshape(shapes[nm].shape) for o in small_out[4 * p:4 * p + 4])

    nw = mod_w.shape[2]
    dm_sh = jnp.stack([lax.dynamic_slice_in_dim(dm_all[:, l * 3 * D:(l + 1) * 3 * D], chip * nw, nw, axis=1)
                       for l in range(2)])
    res["mod_w"] = tuple(_modw_adam(ca_all.T, dm_sh, mod_w, m_mod_w, v_mod_w))

    sums = list(sums_l0) + list(sums_l1)
    sib_sums = list(sib_l0) + list(sib_l1)
    big_names = ["hy_w_in", "hy_w_out", "pool_w_in", "pool_w_grp", "pool_w_out"]
    big_wmv = [(hy_w_in, m_hy_w_in, v_hy_w_in), (hy_w_out, m_hy_w_out, v_hy_w_out), (pool_w_in, m_pool_w_in, v_pool_w_in),
               (pool_w_grp, m_pool_w_grp, v_pool_w_grp), (pool_w_out, m_pool_w_out, v_pool_w_out)]
    for a, nm in enumerate(big_names):
        rr, cc = big[a].shape
        w, m, v = (t.reshape(rr, cc) for t in big_wmv[a])
        outs = _adam_2d(w, sums[a], sib_sums[a], m, v, f"adam_{nm}")
        res[nm] = tuple(o.reshape(big_wmv[a][0].shape) for o in outs)

    loss = red_a[ROW_LOSS, 0]
    order = ["norm_g", "mod_w", "mod_b", "hy_w_in", "hy_conv_w", "hy_conv_b", "lru_w_a", "lru_b_a", "lru_w_x", "lru_b_x",
             "lru_lambda", "sc_conv_w", "hy_w_out", "pool_w_in", "pool_w_grp", "pool_b_grp", "pool_scale", "pool_w_out",
             "final_g"]
    return (loss, grad_x[None], *[res[nm][0] for nm in order], *[res[nm][1] for nm in order],
            *[res[nm][2] for nm in order], *[res[nm][3] for nm in order])
```

```python
import jax
import jax.numpy as jnp
from jax import lax
from jax.experimental import pallas as pl
from jax.experimental.pallas import tpu as pltpu
from jax.experimental.pallas import tpu_sc as plsc

F32, BF16 = jnp.float32, jnp.bfloat16
D = 1024
RMS_EPS = 1e-6
SQRT_FLOOR = 1e-30
LRU_C = 8.0
LRU_HEADS, LRU_HEAD_DIM = 8, 128
POOL_WINDOWS = (2, 4, 8, 16)
POOL_GROUP_DIM = 512
ADAM_LR, ADAM_B1, ADAM_B2, ADAM_EPS, ADAM_WD, ADAM_STEP = 0.001, 0.9, 0.999, 1e-08, 0.01, 10
MESH = pl.DeviceIdType.MESH
CIDS_WGATHER = (1, 8)
CIDS_L1 = (2, 3)
CIDS_L0 = (4, 5)
N_DEV, N_CHIP = 8, 4
SUBLANES = 8
BF16_ROWS = 16
POOL_HALO = 16
TS_MIX, TS_WGRAD, TS_DGRAD = 256, 2048, 512
SMALL_ROWS = 64
GRAD_WIRE_DTYPE = BF16
ANY = pl.BlockSpec(memory_space=pl.ANY)
VMEM = pl.BlockSpec(memory_space=pltpu.VMEM)
NT = (((1,), (1,)), ((), ()))
TN = (((0,), (0,)), ((), ()))


def _cp(sem=None, vmem_mb=56):
    kw = dict(vmem_limit_bytes=vmem_mb * 2 ** 20)
    if sem is not None:
        kw["dimension_semantics"] = sem
    return pltpu.CompilerParams(**kw)


def _tile(n, t):
    return min(n, t)


def _pos():
    return lax.axis_index("x"), lax.axis_index("y"), lax.axis_index("c")


def _flip(v, f):
    return 1 - v if f else v


def _sigmoid(z):
    return 0.5 * jnp.tanh(0.5 * z) + 0.5


def _rows(n, c):
    return lax.broadcasted_iota(jnp.int32, (n, c), 0)


def _down(a, d):
    return a if d == 0 else pltpu.roll(a, d, 0)


def _up(a, d):
    return a if d == 0 else pltpu.roll(a, a.shape[0] - d, 0)


def _scan_fwd_steps(a, u, carry):
    n, c = a.shape
    sub = _rows(SUBLANES, c)
    out = []
    for k in range(n // SUBLANES):
        p = a[k * SUBLANES:(k + 1) * SUBLANES]
        g = u[k * SUBLANES:(k + 1) * SUBLANES]
        for d in (1, 2, 4):
            keep = sub >= d
            g = g + p * jnp.where(keep, pltpu.roll(g, d, 0), 0.0)
            p = p * jnp.where(keep, pltpu.roll(p, d, 0), 1.0)
        h = g + p * carry
        carry = h[SUBLANES - 1:SUBLANES, :]
        out.append(h)
        yield
    return jnp.concatenate(out, axis=0)


def _scan_rev_steps(alpha, b, carry):
    n, c = alpha.shape
    sub = _rows(SUBLANES, c)
    out = []
    for k in reversed(range(n // SUBLANES)):
        p = alpha[k * SUBLANES:(k + 1) * SUBLANES]
        g = b[k * SUBLANES:(k + 1) * SUBLANES]
        for d in (1, 2, 4):
            keep = sub < SUBLANES - d
            g = g + p * jnp.where(keep, pltpu.roll(g, SUBLANES - d, 0), 0.0)
            p = p * jnp.where(keep, pltpu.roll(p, SUBLANES - d, 0), 1.0)
        h = g + p * carry
        carry = h[0:1, :]
        out.append(h)
        yield
    return jnp.concatenate(out[::-1], axis=0)


def _run(steps):
    while True:
        try:
            next(steps)
        except StopIteration as done:
            return done.value


def _paired(progress, pieces):
    n, done = len(pieces), 1
    pieces[0]()
    for frac in progress:
        while done < n and done <= frac * n:
            pieces[done]()
            done += 1
    while done < n:
        pieces[done]()
        done += 1


def _conv_taps(ext, halo, n, width):
    return [_down(ext, width - 1 - k)[halo:halo + n] for k in range(width)]


def _lru_gates(xc, wa_ref, ba, wx_ref, bx):
    xb = xc.astype(BF16)
    pa, px = [], []
    for h in range(LRU_HEADS):
        xh = xb[:, h * LRU_HEAD_DIM:(h + 1) * LRU_HEAD_DIM]
        pa.append(jnp.dot(xh, wa_ref[h], preferred_element_type=F32))
        px.append(jnp.dot(xh, wx_ref[h], preferred_element_type=F32))
    r = _sigmoid(jnp.concatenate(pa, axis=1) + ba)
    ig = _sigmoid(jnp.concatenate(px, axis=1) + bx)
    return r, ig


def _softplus_neg(lam):
    return jnp.maximum(-lam, 0.0) + jnp.log1p(jnp.exp(-jnp.abs(lam)))


def _recip_1_to_2(d):
    r0 = pl.reciprocal(d, approx=True)
    return r0 * (2.0 - d * r0)


def _lru_decay(r, sp, first):
    big_l = (-LRU_C) * r * sp
    a = jnp.exp(big_l)
    th = jnp.tanh(big_l)
    q = (-2.0 * th) * _recip_1_to_2(1.0 - th)
    rs = lax.rsqrt(jnp.maximum(q, SQRT_FLOOR))
    return a, jnp.where(first, 1.0, q * rs), rs


def _pool_inv_counts(t0, n):
    t = (t0 + lax.broadcasted_iota(jnp.int32, (n, 1), 0) + 1).astype(F32)
    return [1.0 / jnp.minimum(t, float(w)) for w in POOL_WINDOWS]


def _window_sums(ext, shift):
    gd = POOL_GROUP_DIM
    out = []
    s = ext
    for k in range(len(POOL_WINDOWS)):
        s = s + shift(s, 2 ** k)
        out.append(s[:, 0:gd])
        if k + 1 < len(POOL_WINDOWS):
            s = s[:, gd:]
    return out


SW_ROWS, SW_COLS = 16, 2 * D
SW_CONV, SW_SC, SW_POOL_B, SW_POOL_S = 0, 4, 8, 9


def _mod_fwd(c8, mod_w, mod_b, conv_w, sc_w, pool_b, pool_s):
    nw = mod_w.shape[2]
    cq, pq = conv_w.shape[1], pool_b.shape[1]

    def body(c_ref, w_ref, b_ref, cw_ref, sw_ref, pb_ref, ps_ref, ca_ref, mod_ref, small_ref,
             cslot, mslot, msend, pslot, psend, s1, r1, s2, r2, s3, r3):
        x, y, c = _pos()
        me = 4 * x + 2 * y + c
        chip = 2 * x + y
        first = []
        for r in range(1, N_DEV):
            fx, fy, fc = (r >> 2) & 1, (r >> 1) & 1, r & 1
            cp = pltpu.make_async_remote_copy(
                src_ref=c_ref, dst_ref=cslot.at[me], send_sem=s1.at[r - 1], recv_sem=r1.at[r - 1],
                device_id=(_flip(x, fx), _flip(y, fy), _flip(c, fc)), device_id_type=MESH)
            cp.start()
            first.append(cp)
        cslot[me] = c_ref[...]
        for cp in first:
            cp.wait()
        rows = _rows(SUBLANES, D)
        call = jnp.zeros((SUBLANES, D), F32)
        for d in range(N_DEV):
            call = jnp.where(rows == d, cslot[d], call)
        ca = call * _sigmoid(call)
        ca_ref[...] = ca
        for l in range(2):
            msend[l] = jnp.dot(ca, w_ref[l], precision=lax.Precision.HIGHEST, preferred_element_type=F32)
        psend[...] = jnp.zeros_like(psend)
        psend[SW_CONV:SW_CONV + 4, 0:cq] = cw_ref[...]
        psend[SW_SC:SW_SC + 3, 0:cq] = sw_ref[...]
        psend[SW_POOL_B:SW_POOL_B + 1, :] = pb_ref[...]
        psend[SW_POOL_S:SW_POOL_S + 1, :] = ps_ref[...]
        second = []
        for q, (fx, fy) in enumerate(((1, 0), (0, 1), (1, 1))):
            peer = (_flip(x, fx), _flip(y, fy), c)
            for src, dst, ss, rs in ((msend, mslot, s2, r2), (psend, pslot, s3, r3)):
                cp = pltpu.make_async_remote_copy(src_ref=src, dst_ref=dst.at[chip], send_sem=ss.at[q], recv_sem=rs.at[q],
                                                  device_id=peer, device_id_type=MESH)
                cp.start()
                second.append(cp)
        mslot[chip] = msend[...]
        pslot[chip] = psend[...]
        for cp in second:
            cp.wait()
        small_ref[...] = jnp.zeros_like(small_ref)
        for j in range(N_CHIP):
            for l in range(2):
                mod_ref[l, :, j * nw:(j + 1) * nw] = mslot[j, l] + b_ref[l:l + 1, j * nw:(j + 1) * nw]
            small_ref[0:SUBLANES, j * cq:(j + 1) * cq] = pslot[j, 0:SUBLANES, 0:cq]
            small_ref[SUBLANES:SW_ROWS, j * pq:(j + 1) * pq] = pslot[j, SUBLANES:SW_ROWS, :]

    args = (c8, mod_w, mod_b, conv_w, sc_w, pool_b, pool_s)
    dma3 = pltpu.SemaphoreType.DMA((N_CHIP - 1,))
    return pl.pallas_call(
        body, name="mod_fwd",
        in_specs=[VMEM] * len(args), out_specs=[VMEM] * 3,
        out_shape=[jax.ShapeDtypeStruct((SUBLANES, D), F32), jax.ShapeDtypeStruct((2, SUBLANES, N_CHIP * nw), F32),
                   jax.ShapeDtypeStruct((SW_ROWS, SW_COLS), F32)],
        scratch_shapes=[pltpu.VMEM((N_DEV, SUBLANES, D), F32), pltpu.VMEM((N_CHIP, 2, SUBLANES, nw), F32),
                        pltpu.VMEM((2, SUBLANES, nw), F32), pltpu.VMEM((N_CHIP, SW_ROWS, pq), F32),
                        pltpu.VMEM((SW_ROWS, pq), F32),
                        pltpu.SemaphoreType.DMA((N_DEV - 1,)), pltpu.SemaphoreType.DMA((N_DEV - 1,)),
                        dma3, dma3, dma3, dma3],
        compiler_params=_cp(),
    )(*args)


def _wcast(ws):
    def body(*refs):
        n = len(refs) // 2
        for a in range(n):
            refs[n + a][...] = refs[a][...].astype(BF16)

    return pl.pallas_call(
        body, name="wcast", in_specs=[VMEM] * len(ws), out_specs=[VMEM] * len(ws),
        out_shape=[jax.ShapeDtypeStruct(w.shape, BF16) for w in ws], compiler_params=_cp(),
    )(*ws)


def _wcast_own_block(w, kidx, name, after=()):
    rr, cc = w.shape
    rb = min(rr, 256)

    def body(k_ref, w_ref, *rest):
        rest[-1][...] = w_ref[...].astype(BF16)

    order = list(after)
    return pl.pallas_call(
        body, name=name,
        grid_spec=pltpu.PrefetchScalarGridSpec(
            num_scalar_prefetch=1, grid=(rr // rb,),
            in_specs=[pl.BlockSpec((rb, cc), lambda j, k_ref: (j, 0))] + [ANY] * len(order),
            out_specs=pl.BlockSpec((None, rb, cc), lambda j, k_ref: (k_ref[0], j, 0))),
        out_shape=jax.ShapeDtypeStruct((N_CHIP, rr, cc), BF16),
        compiler_params=_cp(("parallel",)),
    )(kidx, w, *order)


def _wgather_copies(outs, rows, ssem, rsem, fssem, frsem):
    n = len(outs)
    x, y, c = _pos()
    chip = 2 * x + y
    sib = (x, y, 1 - c)
    flips = ((1, 0), (0, 1), (1, 1))

    def half(a, which):
        hr = rows[a] // 2
        return pl.ds(pl.multiple_of(which * hr, BF16_ROWS), hr)

    sends = []
    for a in range(n):
        mine = outs[a].at[chip, half(a, c), :]
        for q, (fx, fy) in enumerate(flips):
            cp = pltpu.make_async_remote_copy(
                src_ref=mine, dst_ref=mine, send_sem=ssem.at[3 * a + q], recv_sem=rsem.at[3 * a + q],
                device_id=(_flip(x, fx), _flip(y, fy), c), device_id_type=MESH)
            cp.start()
            sends.append(cp)
    passed = []
    for a in range(n):
        for q, (fx, fy) in enumerate(flips):
            src_chip = 2 * _flip(x, fx) + _flip(y, fy)
            landed = outs[a].at[src_chip, half(a, c), :]
            pltpu.make_async_remote_copy(
                src_ref=landed, dst_ref=landed, send_sem=ssem.at[3 * a + q], recv_sem=rsem.at[3 * a + q],
                device_id=sib, device_id_type=MESH).wait_recv()
            cp = pltpu.make_async_remote_copy(
                src_ref=landed, dst_ref=landed, send_sem=fssem.at[3 * a + q], recv_sem=frsem.at[3 * a + q],
                device_id=sib, device_id_type=MESH)
            cp.start()
            passed.append(cp)
    for a in range(n):
        for q, (fx, fy) in enumerate(flips):
            src_chip = 2 * _flip(x, fx) + _flip(y, fy)
            other = outs[a].at[src_chip, half(a, 1 - c), :]
            pltpu.make_async_remote_copy(
                src_ref=other, dst_ref=other, send_sem=fssem.at[3 * a + q], recv_sem=frsem.at[3 * a + q],
                device_id=sib, device_id_type=MESH).wait_recv()
    for cp in sends + passed:
        cp.wait_send()


def _wgather_sequencer(bufs, name, collective_id):
    n = len(bufs)
    refs = [jax.new_ref(b, memory_space=pltpu.MemorySpace.HBM) for b in bufs]
    dma = pltpu.SemaphoreType.DMA((3 * n,))

    @pl.kernel(mesh=plsc.ScalarSubcoreMesh(axis_name="sequencer", num_cores=1), name=name,
               scratch_types=(dma, dma, dma, dma), compiler_params=pltpu.CompilerParams(collective_id=collective_id))
    def launch(ssem, rsem, fssem, frsem):
        x, y, c = _pos()
        barrier = pltpu.get_barrier_semaphore()
        for peer in ((1 - x, y, c), (x, 1 - y, c), (1 - x, 1 - y, c), (x, y, 1 - c)):
            pl.semaphore_signal(barrier, inc=1, device_id=peer, device_id_type=MESH)
        pl.semaphore_wait(barrier, 4)
        _wgather_copies(refs, [b.shape[1] for b in bufs], ssem, rsem, fssem, frsem)

    launch()
    return [r[...] for r in refs]


def _l0_fwd(x, g, sc, sh, w_in, gate, cw, cb, wa, ba, wx, bx, lam, sw, wo):
    s_len, nb = x.shape[0], w_in.shape[2]
    ts = _tile(s_len, TS_MIX)
    n_t = s_len // ts
    hl = SUBLANES

    def body(xa_ref, xb_ref, g_ref, sc_ref, sh_ref, win_ref, gate_ref, cw_ref, cb_ref, wa_ref, ba_ref, wx_ref, bx_ref,
             lam_ref, sw_ref, wo_ref, x1_ref, h_ref, y_ref, xc_ref, cz_ref, h0_ref, p_ref, pcur, pnext, cxa, czz, chh):
        i = pl.program_id(0)

        @pl.when(i == 0)
        def _():
            cxa[...] = jnp.zeros_like(cxa)
            czz[...] = jnp.zeros_like(czz)
            chh[...] = jnp.zeros_like(chh)
            pnext[...] = jnp.zeros_like(pnext)

        pcur[...] = pnext[...]
        xv = xa_ref[...]
        rinv = lax.rsqrt(jnp.mean(xv * xv, axis=-1, keepdims=True) + RMS_EPS)
        h0 = (xv * rinv * (g_ref[...] * (1.0 + sc_ref[...])) + sh_ref[...]).astype(BF16)
        h0_ref[...] = h0

        def project(k, c0, cn):
            def emit():
                pk = jnp.dot(h0, win_ref[k, :, c0:c0 + cn], preferred_element_type=F32).astype(BF16)
                p_ref[:, k * nb + c0:k * nb + c0 + cn] = pk
                pnext[:, k * nb + c0:k * nb + c0 + cn] = pk
            return emit

        def mixer():
            piece = lambda k: pcur[:, k * D:(k + 1) * D].astype(F32)
            xa = piece(0)
            rows = _rows(ts, D)
            taps = _conv_taps(jnp.concatenate([cxa[...], xa], axis=0), hl, ts, 4)
            xc = cb_ref[...] + sum(cw_ref[k:k + 1, :] * taps[k] for k in range(4))
            xc_ref[...] = xc.astype(BF16)
            r, ig = _lru_gates(xc, wa_ref, ba_ref[...], wx_ref, bx_ref[...])
            a, m, _ = _lru_decay(r, _softplus_neg(lam_ref[...]), (rows == 0) & (i == 1))
            yield 0.26
            h = _run(_scan_fwd_steps(a, m * ig * xc, chh[hl - 1:hl, :]))
            yield 0.51
            gcp, v = piece(3), piece(4)
            z = gcp * v
            ztaps = _conv_taps(jnp.concatenate([czz[...], z], axis=0), hl, ts, 3)
            cz = sum(sw_ref[k:k + 1, :] * ztaps[k] for k in range(3))
            cz_ref[...] = cz.astype(BF16)
            yb = piece(2) * cz
            ga, gb = piece(1), piece(5)
            y = jnp.concatenate([h * (ga * _sigmoid(ga)), yb * (gb * _sigmoid(gb))], axis=1).astype(BF16)
            yield 0.76
            y_ref[...] = y
            x1_ref[...] = xb_ref[...] + gate_ref[...] * jnp.dot(y, wo_ref[...], preferred_element_type=F32)
            h_ref[...] = h.astype(BF16)
            cxa[...] = xa[ts - hl:, :]
            czz[...] = z[ts - hl:, :]
            chh[...] = jnp.where(i > 0, h[ts - hl:, :], 0.0)

        _paired(mixer(), [project(k, 0, nb) for k in range(N_CHIP)])

    def full(a):
        return pl.BlockSpec(a.shape, lambda i: (0,) * a.ndim)

    ahead = lambda w: pl.BlockSpec((ts, w), lambda i: (jnp.minimum(i, n_t - 1), 0))
    behind = lambda w: pl.BlockSpec((ts, w), lambda i: (jnp.maximum(i - 1, 0), 0))
    args = (x, x, g, sc, sh, w_in, gate, cw, cb, wa, ba, wx, bx, lam, sw, wo)
    return pl.pallas_call(
        body, name="l0_fwd", grid=(n_t + 1,),
        in_specs=[ahead(D), behind(D)] + [full(a) for a in args[2:]],
        out_specs=[behind(D), behind(D), behind(2 * D), behind(D), behind(D), ahead(D), ahead(N_CHIP * nb)],
        out_shape=[jax.ShapeDtypeStruct((s_len, D), F32), jax.ShapeDtypeStruct((s_len, D), BF16),
                   jax.ShapeDtypeStruct((s_len, 2 * D), BF16), jax.ShapeDtypeStruct((s_len, D), BF16),
                   jax.ShapeDtypeStruct((s_len, D), BF16), jax.ShapeDtypeStruct((s_len, D), BF16),
                   jax.ShapeDtypeStruct((s_len, N_CHIP * nb), BF16)],
        scratch_shapes=[pltpu.VMEM((ts, N_CHIP * nb), BF16)] * 2 + [pltpu.VMEM((hl, D), F32)] * 3,
        compiler_params=_cp(("arbitrary",)),
    )(*args)


def _l1_fwd(x1, g, sc, sh, w_in, tgt, gate, wg, bg, scale, wo, gf):
    s_len, nb = x1.shape[0], w_in.shape[2]
    ts = _tile(s_len, TS_MIX)
    n_t = s_len // ts
    pw, gd, hl = 2 * D, POOL_GROUP_DIM, POOL_HALO

    def body(xa_ref, xb_ref, t_ref, g_ref, sc_ref, sh_ref, win_ref, gate_ref, wg_ref, bg_ref, scl_ref, wo_ref, gf_ref,
             d_ref, mx_ref, y_ref, dx_ref, loss_ref, dgf_ref, h1_ref, p_ref, pcur, pnext, cv):
        i = pl.program_id(0)

        @pl.when(i == 0)
        def _():
            cv[...] = jnp.zeros_like(cv)
            loss_ref[...] = jnp.zeros_like(loss_ref)
            dgf_ref[...] = jnp.zeros_like(dgf_ref)
            pnext[...] = jnp.zeros_like(pnext)

        pcur[...] = pnext[...]
        xv = xa_ref[...]
        rinv = lax.rsqrt(jnp.mean(xv * xv, axis=-1, keepdims=True) + RMS_EPS)
        h1 = (xv * rinv * (g_ref[...] * (1.0 + sc_ref[...])) + sh_ref[...]).astype(BF16)
        h1_ref[...] = h1

        def project(k):
            def emit():
                pk = jnp.dot(h1, win_ref[k], preferred_element_type=F32).astype(BF16)
                p_ref[:, k * nb:(k + 1) * nb] = pk
                pnext[:, k * nb:(k + 1) * nb] = pk
            return emit

        def mixer():
            v = pcur[:, 0:pw].astype(F32)
            sums = _window_sums(jnp.concatenate([cv[...], v], axis=0), _down)
            inv = _pool_inv_counts(jnp.maximum(i - 1, 0) * ts, ts)
            dd = [sums[k][hl:hl + ts] * inv[k] - v[:, k * gd:(k + 1) * gd] for k in range(4)]
            d_ref[...] = jnp.concatenate(dd, axis=1).astype(BF16)
            yield 0.26
            mixed = jnp.concatenate(
                [jnp.dot(dd[k].astype(BF16), wg_ref[k], preferred_element_type=F32) for k in range(4)], axis=1) + bg_ref[...]
            mx_ref[...] = mixed.astype(BF16)
            gg = pcur[:, pw:2 * pw].astype(F32)
            y = (mixed * scl_ref[...] * (gg * _sigmoid(gg))).astype(BF16)
            y_ref[...] = y
            yield 0.51
            x2 = xb_ref[...] + gate_ref[...] * jnp.dot(y, wo_ref[...], preferred_element_type=F32)
            yield 0.76
            r2 = lax.rsqrt(jnp.mean(x2 * x2, axis=-1, keepdims=True) + RMS_EPS)
            n2 = x2 * r2
            err = n2 * gf_ref[...] - t_ref[...]
            loss_ref[...] += jnp.where(i > 0, jnp.sum(err * err, axis=0, keepdims=True), 0.0)
            dyf = err * (1.0 / D)
            dgf_ref[...] += jnp.where(i > 0, jnp.sum(dyf * n2, axis=0, keepdims=True), 0.0)
            dn = dyf * gf_ref[...]
            dx_ref[...] = r2 * (dn - n2 * jnp.mean(dn * n2, axis=-1, keepdims=True))
            cv[...] = v[ts - hl:, :]

        _paired(mixer(), [project(k) for k in range(N_CHIP)])

    def full(a):
        return pl.BlockSpec(a.shape, lambda i: (0,) * a.ndim)

    ahead = lambda w: pl.BlockSpec((ts, w), lambda i: (jnp.minimum(i, n_t - 1), 0))
    behind = lambda w: pl.BlockSpec((ts, w), lambda i: (jnp.maximum(i - 1, 0), 0))
    acc = pl.BlockSpec((1, D), lambda i: (0, 0))
    args = (x1, x1, tgt, g, sc, sh, w_in, gate, wg, bg, scale, wo, gf)
    return pl.pallas_call(
        body, name="l1_fwd", grid=(n_t + 1,),
        in_specs=[ahead(D), behind(D), behind(D)] + [full(a) for a in args[3:]],
        out_specs=[behind(pw), behind(pw), behind(pw), behind(D), acc, acc, ahead(D), ahead(N_CHIP * nb)],
        out_shape=[jax.ShapeDtypeStruct((s_len, pw), BF16)] * 3 + [jax.ShapeDtypeStruct((s_len, D), F32)]
        + [jax.ShapeDtypeStruct((1, D), F32)] * 2
        + [jax.ShapeDtypeStruct((s_len, D), BF16), jax.ShapeDtypeStruct((s_len, N_CHIP * nb), BF16)],
        scratch_shapes=[pltpu.VMEM((ts, N_CHIP * nb), BF16)] * 2 + [pltpu.VMEM((hl, pw), F32)],
        compiler_params=_cp(("arbitrary",)),
    )(*args)


def _l1_bwd_mix(dx2, proj, mixed, y, dpool, gate, wg, scale, wo):
    s_len = dx2.shape[0]
    n_sub = 2
    ts = _tile(s_len, n_sub * TS_MIX)
    sub = ts // n_sub
    n_t = s_len // ts
    pw, gd, hl = 2 * D, POOL_GROUP_DIM, POOL_HALO

    def body(dx_ref, gg_ref, mx_ref, y_ref, d_ref, gate_ref, wg_ref, sc_ref, wo_ref,
             dp_ref, mt_ref, dwg_ref, dsc_ref, dbg_ref, cq):
        i = pl.program_id(0)

        @pl.when(i == 0)
        def _():
            cq[...] = jnp.zeros_like(cq)
            dsc_ref[...] = jnp.zeros_like(dsc_ref)
            dbg_ref[...] = jnp.zeros_like(dbg_ref)
            mt_ref[...] = jnp.zeros_like(mt_ref)
            dwg_ref[...] = jnp.zeros_like(dwg_ref)

        ahead_rows = {}

        def chain(j):
            rows = slice(j * sub, (j + 1) * sub)
            dxv = dx_ref[rows, :]
            dxb = dxv.astype(BF16)
            dy = lax.dot_general((gate_ref[...] * dxv).astype(BF16), wo_ref[...], NT, preferred_element_type=F32)
            for k in range(2):
                mt_ref[k] += lax.dot_general(y_ref[rows, k * gd:(k + 1) * gd], dxb, TN, preferred_element_type=F32)
            yield
            gg = gg_ref[rows, :].astype(F32)
            mixed = mx_ref[rows, :].astype(F32)
            s = _sigmoid(gg)
            sg = gg * s
            dym = dy * mixed
            dmixed = dy * sc_ref[...] * sg
            dsc_ref[...] += jnp.sum(dym * sg, axis=0, keepdims=True)
            dbg_ref[...] += jnp.sum(dmixed, axis=0, keepdims=True)
            dmb = dmixed.astype(BF16)
            dp_ref[rows, pw:2 * pw] = (dym * sc_ref[...] * (s + sg * (1.0 - s))).astype(BF16)
            yield
            inv = _pool_inv_counts((n_t - 1 - i) * ts + j * sub, sub)
            dd = []
            for k in range(4):
                dmk = dmb[:, k * gd:(k + 1) * gd]
                dd.append(lax.dot_general(dmk, wg_ref[k], NT, preferred_element_type=F32))
                dwg_ref[k] += lax.dot_general(d_ref[rows, k * gd:(k + 1) * gd], dmk, TN, preferred_element_type=F32)
            for k in range(2, 4):
                mt_ref[k] += lax.dot_general(y_ref[rows, k * gd:(k + 1) * gd], dxb, TN, preferred_element_type=F32)
            q = jnp.concatenate([dd[k] * inv[k] for k in range(4)], axis=1)
            ahead_rows[j] = q[0:hl, :]
            yield
            behind_q = cq[...] if j == n_sub - 1 else ahead_rows[j + 1]
            sums = _window_sums(jnp.concatenate([q, behind_q], axis=0), _up)
            dp_ref[rows, 0:pw] = jnp.concatenate([sums[k][0:sub] - dd[k] for k in range(4)], axis=1).astype(BF16)

        chains = [chain(j) for j in reversed(range(n_sub))]
        for _ in range(4):
            for ch in chains:
                next(ch, None)
        cq[...] = ahead_rows[0]

    def full(a):
        return pl.BlockSpec(a.shape, lambda i: (0,) * a.ndim)

    rev = lambda w, j=0: pl.BlockSpec((ts, w), lambda i: (n_t - 1 - i, j))
    acc = pl.BlockSpec((1, pw), lambda i: (0, 0))
    return pl.pallas_call(
        body, name="l1_bwd_mix", grid=(n_t,),
        in_specs=[rev(D), rev(pw, 1), rev(pw), rev(pw), rev(pw)] + [full(a) for a in (gate, wg, scale, wo)],
        out_specs=[rev(2 * pw), pl.BlockSpec((N_CHIP, gd, D), lambda i: (0, 0, 0)),
                   pl.BlockSpec((4, gd, gd), lambda i: (0, 0, 0)), acc, acc],
        out_shape=[jax.ShapeDtypeStruct((s_len, 2 * pw), BF16), jax.ShapeDtypeStruct((N_CHIP, gd, D), F32),
                   jax.ShapeDtypeStruct((4, gd, gd), F32),
                   jax.ShapeDtypeStruct((1, pw), F32), jax.ShapeDtypeStruct((1, pw), F32)],
        scratch_shapes=[pltpu.VMEM((hl, pw), F32)],
        compiler_params=_cp(("arbitrary",)),
    )(dx2, proj, mixed, y, dpool, gate, wg, scale, wo)


def _l0_bwd_mix(dx1, proj, hst, y, xc, cz, gate, cw, wa, ba, wx, bx, lam, sw, wo):
    s_len = dx1.shape[0]
    ts = _tile(s_len, TS_MIX)
    n_t = s_len // ts
    hl, hb = SUBLANES, BF16_ROWS
    yb_w = 2 * D // N_CHIP

    def body(dx_ref, p_ref, h_ref, hh_ref, y_ref, xc_ref, cz_ref, gate_ref, cw_ref, wa_ref, ba_ref, wx_ref, bx_ref,
             lam_ref, sw_ref, wo_ref, dp_ref, mt_ref, dwa_ref, dwx_ref, sm_ref, cg, cdxc, cdcz, ca):
        i = pl.program_id(0)
        ri = n_t - 1 - i

        @pl.when(i == 0)
        def _():
            cg[...] = jnp.zeros_like(cg)
            ca[...] = jnp.zeros_like(ca)
            cdxc[...] = jnp.zeros_like(cdxc)
            cdcz[...] = jnp.zeros_like(cdcz)
            sm_ref[...] = jnp.zeros_like(sm_ref)
            mt_ref[...] = jnp.zeros_like(mt_ref)
            dwa_ref[...] = jnp.zeros_like(dwa_ref)
            dwx_ref[...] = jnp.zeros_like(dwx_ref)

        dxb = dx_ref[...].astype(BF16)

        def wgrad_out(k):
            mt_ref[k] += lax.dot_general(y_ref[:, k * yb_w:(k + 1) * yb_w], dxb, TN, preferred_element_type=F32)

        wgrad_out(0)
        has_prev = (ri > 0).astype(F32)
        xa, ga, gbp, gcp, v, gb = [p_ref[:, k * D:(k + 1) * D].astype(F32) for k in range(6)]
        rows = _rows(ts, D)
        first = (rows == 0) & (ri == 0)
        xc = xc_ref[...].astype(F32)
        cz = cz_ref[...].astype(F32)
        r, ig = _lru_gates(xc, wa_ref, ba_ref[...], wx_ref, bx_ref[...])
        sp = _softplus_neg(lam_ref[...])
        a, m, inv_m = _lru_decay(r, sp, first)
        z = gcp * v
        h = h_ref[...].astype(F32)
        hprev = _down(jnp.concatenate([hh_ref[...].astype(F32)[hb - hl:hb] * has_prev, h], axis=0), 1)[hl:hl + ts]
        dy = lax.dot_general((gate_ref[...] * dx_ref[...]).astype(BF16), wo_ref[...], NT, preferred_element_type=F32)
        dya_pre, dyb_pre = dy[:, 0:D], dy[:, D:2 * D]
        s_a, s_b = _sigmoid(ga), _sigmoid(gb)
        silu_a, silu_b = ga * s_a, gb * s_b
        dp_ref[:, D:2 * D] = (dya_pre * h * (s_a + silu_a * (1.0 - s_a))).astype(BF16)
        dp_ref[:, 5 * D:6 * D] = (dyb_pre * (gbp * cz) * (s_b + silu_b * (1.0 - s_b))).astype(BF16)
        dya = dya_pre * silu_a
        dyb = dyb_pre * silu_b
        wgrad_out(1)
        dp_ref[:, 2 * D:3 * D] = (dyb * cz).astype(BF16)
        dcz = dyb * gbp
        dcz_ext = jnp.concatenate([dcz, cdcz[...]], axis=0)
        dcz_taps = [_up(dcz_ext, 2 - k)[0:ts] for k in range(3)]
        for k in range(3):
            sm_ref[8 + k:9 + k, :] += jnp.sum(z * dcz_taps[k], axis=0, keepdims=True)
        dz = sum(sw_ref[k:k + 1, :] * dcz_taps[k] for k in range(3))
        dp_ref[:, 3 * D:4 * D] = (dz * v).astype(BF16)
        dp_ref[:, 4 * D:5 * D] = (dz * gcp).astype(BF16)
        cdcz[...] = dcz[0:hl, :]
        alpha = _up(jnp.concatenate([a, ca[...]], axis=0), 1)[0:ts]
        wgrad_out(2)
        dh = _run(_scan_rev_steps(alpha, dya, cg[0:1, :]))
        wgrad_out(3)
        cg[...] = dh[0:hl, :]
        ca[...] = a[0:hl, :]
        da = dh * hprev
        dhx = dh * xc
        dm = dhx * ig
        di = dhx * m
        dxc = dh * (m * ig)
        dl = a * (da - jnp.where(first, 0.0, dm * a * inv_m))
        dlr = dl * r
        sm_ref[7:8, :] += jnp.sum(dlr, axis=0, keepdims=True) * (-LRU_C)
        dpa = dlr * (sp * (-LRU_C)) * (1.0 - r)
        dpx = di * ig * (1.0 - ig)
        sm_ref[5:6, :] += jnp.sum(dpa, axis=0, keepdims=True)
        sm_ref[6:7, :] += jnp.sum(dpx, axis=0, keepdims=True)
        dpa_b, dpx_b, xc_b = dpa.astype(BF16), dpx.astype(BF16), xc.astype(BF16)
        back = []
        for hd in range(LRU_HEADS):
            sl = slice(hd * LRU_HEAD_DIM, (hd + 1) * LRU_HEAD_DIM)
            back.append(lax.dot_general(dpa_b[:, sl], wa_ref[hd], NT, preferred_element_type=F32)
                        + lax.dot_general(dpx_b[:, sl], wx_ref[hd], NT, preferred_element_type=F32))
            dwa_ref[hd] += lax.dot_general(xc_b[:, sl], dpa_b[:, sl], TN, preferred_element_type=F32)
            dwx_ref[hd] += lax.dot_general(xc_b[:, sl], dpx_b[:, sl], TN, preferred_element_type=F32)
        dxc = dxc + jnp.concatenate(back, axis=1)
        sm_ref[4:5, :] += jnp.sum(dxc, axis=0, keepdims=True)
        dxc_ext = jnp.concatenate([dxc, cdxc[...]], axis=0)
        dxc_taps = [_up(dxc_ext, 3 - k)[0:ts] for k in range(4)]
        for k in range(4):
            sm_ref[k:k + 1, :] += jnp.sum(xa * dxc_taps[k], axis=0, keepdims=True)
        dp_ref[:, 0:D] = sum(cw_ref[k:k + 1, :] * dxc_taps[k] for k in range(4)).astype(BF16)
        cdxc[...] = dxc[0:hl, :]

    def full(a):
        return pl.BlockSpec(a.shape, lambda i: (0,) * a.ndim)

    rev = lambda w: pl.BlockSpec((ts, w), lambda i: (n_t - 1 - i, 0))
    halo = lambda w: pl.BlockSpec((hb, w), lambda i: (jnp.maximum((n_t - 1 - i) * (ts // hb) - 1, 0), 0))
    return pl.pallas_call(
        body, name="l0_bwd_mix", grid=(n_t,),
        in_specs=[rev(D), rev(6 * D), rev(D), halo(D), rev(2 * D), rev(D), rev(D)]
        + [full(a) for a in (gate, cw, wa, ba, wx, bx, lam, sw, wo)],
        out_specs=[rev(6 * D), pl.BlockSpec((N_CHIP, yb_w, D), lambda i: (0, 0, 0)),
                   pl.BlockSpec(wa.shape, lambda i: (0, 0, 0)), pl.BlockSpec(wa.shape, lambda i: (0, 0, 0)),
                   pl.BlockSpec((2 * SUBLANES, D), lambda i: (0, 0))],
        out_shape=[jax.ShapeDtypeStruct((s_len, 6 * D), BF16), jax.ShapeDtypeStruct((N_CHIP, yb_w, D), F32),
                   jax.ShapeDtypeStruct(wa.shape, F32), jax.ShapeDtypeStruct(wa.shape, F32),
                   jax.ShapeDtypeStruct((2 * SUBLANES, D), F32)],
        scratch_shapes=[pltpu.VMEM((hl, D), F32)] * 4,
        compiler_params=_cp(("arbitrary",)),
    )(dx1, proj, hst, hst, y, xc, cz, gate, cw, wa, ba, wx, bx, lam, sw, wo)


def _dgrad_norm(dproj, w, x, dres, g, sc, name, after=None):
    s_len, nb = x.shape[0], w.shape[2]
    ts = _tile(s_len, TS_DGRAD)
    order = [] if after is None else [after]

    def body(dp_ref, w_ref, x_ref, dr_ref, g_ref, sc_ref, *rest):
        dx_ref, s1_ref, s2_ref = rest[len(order):]

        @pl.when(pl.program_id(0) == 0)
        def _():
            s1_ref[...] = jnp.zeros_like(s1_ref)
            s2_ref[...] = jnp.zeros_like(s2_ref)

        dh = sum(lax.dot_general(dp_ref[:, k * nb:(k + 1) * nb], w_ref[k], NT, preferred_element_type=F32)
                 for k in range(N_CHIP))
        xv = x_ref[...]
        r = lax.rsqrt(jnp.mean(xv * xv, axis=-1, keepdims=True) + RMS_EPS)
        n = xv * r
        s1_ref[...] += jnp.sum(dh, axis=0, keepdims=True)
        s2_ref[...] += jnp.sum(dh * n, axis=0, keepdims=True)
        dn = dh * (g_ref[...] * (1.0 + sc_ref[...]))
        dx_ref[...] = dr_ref[...] + r * (dn - n * jnp.mean(dn * n, axis=-1, keepdims=True))

    row = lambda wd: pl.BlockSpec((ts, wd), lambda i: (i, 0))
    vec = pl.BlockSpec((1, D), lambda i: (0, 0))
    return pl.pallas_call(
        body, name=name, grid=(s_len // ts,),
        in_specs=[row(N_CHIP * nb), pl.BlockSpec(w.shape, lambda i: (0, 0, 0)), row(D), row(D), vec, vec]
        + [ANY] * len(order),
        out_specs=[row(D), vec, vec],
        out_shape=[jax.ShapeDtypeStruct((s_len, D), F32)] + [jax.ShapeDtypeStruct((1, D), F32)] * 2,
        compiler_params=_cp(("arbitrary",)),
    )(dproj, w, x, dres, g, sc, *order)


def _wgrad(a, b, groups, ka, nb, a_col, b_col, name, after=None):
    s_len = a.shape[0]
    ts = _tile(s_len, TS_WGRAD)
    n_s = s_len // ts
    order = [] if after is None else [after]

    def body(a_ref, b_ref, *rest):
        o_ref, wire_ref = rest[-2:]

        @pl.when(pl.program_id(1) == 0)
        def _():
            o_ref[...] = jnp.zeros_like(o_ref)

        o_ref[...] += lax.dot_general(a_ref[...].astype(BF16), b_ref[...].astype(BF16), TN, preferred_element_type=F32)

        @pl.when(pl.program_id(1) == n_s - 1)
        def _():
            wire_ref[...] = o_ref[...].astype(GRAD_WIRE_DTYPE)

    blk = pl.BlockSpec((None, ka, nb), lambda g, s: (g, 0, 0))
    return pl.pallas_call(
        body, name=name, grid=(groups, n_s),
        in_specs=[pl.BlockSpec((ts, ka), lambda g, s: (s, a_col(g))), pl.BlockSpec((ts, nb), lambda g, s: (s, b_col(g)))]
        + [ANY] * len(order),
        out_specs=[blk, blk],
        out_shape=[jax.ShapeDtypeStruct((groups, ka, nb), F32), jax.ShapeDtypeStruct((groups, ka, nb), GRAD_WIRE_DTYPE)],
        compiler_params=_cp(("parallel", "arbitrary")),
    )(a, b, *order)


def _wo_final(mt, wo, gate, name):
    rb = mt.shape[1]

    def body(m_ref, w_ref, gate_ref, dw_ref, wire_ref, dg_ref):
        @pl.when(pl.program_id(0) == 0)
        def _():
            dg_ref[...] = jnp.zeros_like(dg_ref)

        mv = m_ref[...]
        dw = mv * gate_ref[...]
        dw_ref[...] = dw
        wire_ref[...] = dw.astype(GRAD_WIRE_DTYPE)
        dg_ref[...] += jnp.sum(mv * w_ref[...].astype(F32), axis=0, keepdims=True)

    blk = pl.BlockSpec((None, rb, D), lambda k: (k, 0, 0))
    vec = pl.BlockSpec((1, D), lambda k: (0, 0))
    return pl.pallas_call(
        body, name=name, grid=(N_CHIP,), in_specs=[blk, blk, vec], out_specs=[blk, blk, vec],
        out_shape=[jax.ShapeDtypeStruct(mt.shape, F32), jax.ShapeDtypeStruct(mt.shape, GRAD_WIRE_DTYPE),
                   jax.ShapeDtypeStruct((1, D), F32)],
        compiler_params=_cp(("arbitrary",)),
    )(mt, wo, gate)


ROW_NORM_G, ROW_CONV_W, ROW_CONV_B, ROW_B_A, ROW_B_X, ROW_LAMBDA, ROW_SC_W, ROW_POOL_B, ROW_POOL_S, ROW_FINAL_G = (
    0, 2, 6, 7, 8, 9, 10, 13, 15, 17)
ROW_LOSS = 18
DMOD_W = 6 * D // SUBLANES


def _small_pack(s1_0, s2_0, s1_1, s2_1, sm0, dsc1, dbg1, dgf, losscols, dgate0, dgate1, norm_g, sc0, sc1, lam):
    def body(s1_0r, s2_0r, s1_1r, s2_1r, sm, dsc, dbg, dgfr, lcols, dg0, dg1, ng, sc0r, sc1r, lamr, buf, dmod):
        buf[...] = jnp.zeros_like(buf)
        buf[0:1, :] = s2_0r[...] * (1.0 + sc0r[...])
        buf[1:2, :] = s2_1r[...] * (1.0 + sc1r[...])
        buf[ROW_CONV_W:ROW_CONV_W + 4, :] = sm[0:4, :]
        buf[ROW_CONV_B:ROW_CONV_B + 1, :] = sm[4:5, :]
        buf[ROW_B_A:ROW_B_A + 1, :] = sm[5:6, :]
        buf[ROW_B_X:ROW_B_X + 1, :] = sm[6:7, :]
        buf[ROW_LAMBDA:ROW_LAMBDA + 1, :] = -sm[7:8, :] * _sigmoid(-lamr[...])
        buf[ROW_SC_W:ROW_SC_W + 3, :] = sm[8:11, :]
        for k in range(2):
            buf[ROW_POOL_B + k:ROW_POOL_B + k + 1, :] = dbg[:, k * D:(k + 1) * D]
            buf[ROW_POOL_S + k:ROW_POOL_S + k + 1, :] = dsc[:, k * D:(k + 1) * D]
        buf[ROW_FINAL_G:ROW_FINAL_G + 1, :] = dgfr[...]
        pieces = (s1_0r[...], s2_0r[...] * ng[0:1, :], dg0[...], s1_1r[...], s2_1r[...] * ng[1:2, :], dg1[...])
        flat = jnp.concatenate(pieces, axis=1)
        for r in range(SUBLANES):
            dmod[r:r + 1, :] = flat[:, r * DMOD_W:(r + 1) * DMOD_W]
        buf[ROW_LOSS:ROW_LOSS + 1, :] = jnp.broadcast_to(jnp.sum(lcols[...], axis=1, keepdims=True) * (0.5 / D), (1, D))

    args = (s1_0, s2_0, s1_1, s2_1, sm0, dsc1, dbg1, dgf, losscols, dgate0, dgate1, norm_g, sc0, sc1, lam)
    return pl.pallas_call(
        body, name="small_pack", in_specs=[VMEM] * len(args), out_specs=[VMEM] * 2,
        out_shape=[jax.ShapeDtypeStruct((SMALL_ROWS, D), F32), jax.ShapeDtypeStruct((SUBLANES, DMOD_W), F32)],
        compiler_params=_cp(),
    )(*args)


def _small_comm(buf_a, buf_b, dmod8):
    ra, rb = buf_a.shape[0] // N_DEV, buf_b.shape[0] // N_DEV
    wb = buf_b.shape[1]

    def body(a_ref, b_ref, dm_ref, oa_ref, ob_ref, odm_ref, ina, inb, dslot, sa, sb, s1, r1, s2, r2):
        x, y, c = _pos()
        me = 4 * x + 2 * y + c
        peers = []
        for r in range(1, N_DEV):
            fx, fy, fc = (r >> 2) & 1, (r >> 1) & 1, r & 1
            px, py, pc = _flip(x, fx), _flip(y, fy), _flip(c, fc)
            peers.append(((px, py, pc), 4 * px + 2 * py + pc))
        seg_a = lambda d: pl.ds(pl.multiple_of(d * ra, SUBLANES), ra)
        seg_b = lambda d: pl.ds(pl.multiple_of(d * rb, SUBLANES), rb)
        first = []
        for r, (peer, pid) in enumerate(peers):
            for k, (src, dst) in enumerate(((a_ref.at[seg_a(pid), :], ina.at[r]), (b_ref.at[seg_b(pid), :], inb.at[r]),
                                            (dm_ref, dslot.at[me]))):
                cp = pltpu.make_async_remote_copy(src_ref=src, dst_ref=dst, send_sem=s1.at[3 * r + k],
                                                  recv_sem=r1.at[3 * r + k], device_id=peer, device_id_type=MESH)
                cp.start()
                first.append(cp)
        dslot[me] = dm_ref[...]
        for cp in first:
            cp.wait()
        acc_a, acc_b = a_ref[seg_a(me), :], b_ref[seg_b(me), :]
        for r in range(N_DEV - 1):
            acc_a = acc_a + ina[r]
            acc_b = acc_b + inb[r]
        sa[...] = acc_a
        sb[...] = acc_b
        oa_ref[seg_a(me), :] = acc_a
        ob_ref[seg_b(me), :] = acc_b
        second = []
        for r, (peer, pid) in enumerate(peers):
            for k, (src, dst) in enumerate(((sa, oa_ref.at[seg_a(me), :]), (sb, ob_ref.at[seg_b(me), :]))):
                cp = pltpu.make_async_remote_copy(src_ref=src, dst_ref=dst, send_sem=s2.at[2 * r + k],
                                                  recv_sem=r2.at[2 * r + k], device_id=peer, device_id_type=MESH)
                cp.start()
                second.append(cp)
        odm_ref[...] = dslot[...]
        for cp in second:
            cp.wait()

    nrel = N_DEV - 1
    return pl.pallas_call(
        body, name="small_comm", in_specs=[VMEM] * 3, out_specs=[VMEM] * 3,
        out_shape=[jax.ShapeDtypeStruct(buf_a.shape, F32), jax.ShapeDtypeStruct(buf_b.shape, F32),
                   jax.ShapeDtypeStruct((N_DEV,) + dmod8.shape, F32)],
        scratch_shapes=[pltpu.VMEM((nrel, ra, D), F32), pltpu.VMEM((nrel, rb, wb), F32),
                        pltpu.VMEM((N_DEV,) + dmod8.shape, F32), pltpu.VMEM((ra, D), F32), pltpu.VMEM((rb, wb), F32),
                        pltpu.SemaphoreType.DMA((3 * nrel,)), pltpu.SemaphoreType.DMA((3 * nrel,)),
                        pltpu.SemaphoreType.DMA((2 * nrel,)), pltpu.SemaphoreType.DMA((2 * nrel,))],
        compiler_params=_cp(),
    )(buf_a, buf_b, dmod8)


def _adam(w, g, m, v):
    m2 = ADAM_B1 * m + (1.0 - ADAM_B1) * g
    v2 = ADAM_B2 * v + (1.0 - ADAM_B2) * (g * g)
    m_hat = m2 / (1.0 - ADAM_B1 ** ADAM_STEP)
    v_hat = v2 / (1.0 - ADAM_B2 ** ADAM_STEP)
    return -ADAM_LR * (m_hat / (jnp.sqrt(v_hat) + ADAM_EPS) + ADAM_WD * w), m2, v2


def _small_adam(red_a, red_b, dm_all, params):
    n = len(params)

    def body(*refs):
        ra, rb, dm = refs[:3]
        wmv = refs[3:3 + 3 * n]
        outs = refs[3 + 3 * n:]
        x, y, _ = _pos()
        chip = 2 * x + y

        def shard(row0, nrows, width):
            per_row = D // width
            cands = []
            for k in range(N_CHIP):
                if nrows == 1 or per_row >= N_CHIP:
                    cands.append(ra[row0:row0 + nrows, k * width:(k + 1) * width])
                else:
                    rr, cc = divmod(k * width, D)
                    cands.append(ra[row0 + rr:row0 + rr + 1, cc:cc + width])
            g = cands[0]
            for k in range(1, N_CHIP):
                g = jnp.where(chip == k, cands[k], g)
            return g

        dms = jnp.sum(dm[...], axis=0)
        hw = LRU_HEADS * LRU_HEAD_DIM
        grads = [
            ra[ROW_NORM_G:ROW_NORM_G + 2, :],
            None,
            shard(ROW_CONV_W, 4, D // N_CHIP),
            ra[ROW_CONV_B:ROW_CONV_B + 1, :],
            rb[0:hw, :],
            ra[ROW_B_A:ROW_B_A + 1, :],
            rb[hw:2 * hw, :],
            ra[ROW_B_X:ROW_B_X + 1, :],
            ra[ROW_LAMBDA:ROW_LAMBDA + 1, :],
            shard(ROW_SC_W, 3, D // N_CHIP),
            shard(ROW_POOL_B, 2, 2 * D // N_CHIP),
            shard(ROW_POOL_S, 2, 2 * D // N_CHIP),
            ra[ROW_FINAL_G:ROW_FINAL_G + 1, :],
        ]
        for p in range(n):
            w_ref, m_ref, v_ref = wmv[3 * p:3 * p + 3]
            g_out, d_out, m_out, v_out = outs[4 * p:4 * p + 4]
            if grads[p] is None:
                for r in range(SUBLANES):
                    l, cols = r // N_CHIP, slice((r % N_CHIP) * DMOD_W, (r % N_CHIP + 1) * DMOD_W)
                    g = dms[r:r + 1, :]
                    dl, m2, v2 = _adam(w_ref[l:l + 1, cols], g, m_ref[l:l + 1, cols], v_ref[l:l + 1, cols])
                    g_out[l:l + 1, cols] = g
                    d_out[l:l + 1, cols] = dl
                    m_out[l:l + 1, cols] = m2
                    v_out[l:l + 1, cols] = v2
            else:
                g = grads[p]
                dl, m2, v2 = _adam(w_ref[...], g, m_ref[...], v_ref[...])
                g_out[...] = g
                d_out[...] = dl
                m_out[...] = m2
                v_out[...] = v2

    flat = [a for p in params for a in p]
    return pl.pallas_call(
        body, name="small_adam", in_specs=[VMEM] * (3 + len(flat)), out_specs=[VMEM] * (4 * n),
        out_shape=[jax.ShapeDtypeStruct(p[0].shape, F32) for p in params for _ in range(4)],
        compiler_params=_cp(),
    )(red_a, red_b, dm_all, *flat)


def _modw_adam(ca_t, dm_sh, w, m, v):
    nw = w.shape[2]

    def body(c_ref, d_ref, w_ref, m_ref, v_ref, g_out, d_out, m_out, v_out):
        g = jnp.dot(c_ref[...], d_ref[...], precision=lax.Precision.HIGHEST, preferred_element_type=F32)
        dl, m2, v2 = _adam(w_ref[...], g, m_ref[...], v_ref[...])
        g_out[...] = g
        d_out[...] = dl
        m_out[...] = m2
        v_out[...] = v2

    blk = pl.BlockSpec((None, D, nw), lambda l: (l, 0, 0))
    return pl.pallas_call(
        body, name="modw_adam", grid=(2,),
        in_specs=[pl.BlockSpec((D, SUBLANES), lambda l: (0, 0)), pl.BlockSpec((None, SUBLANES, nw), lambda l: (l, 0, 0)),
                  blk, blk, blk],
        out_specs=[blk] * 4, out_shape=[jax.ShapeDtypeStruct(w.shape, F32)] * 4,
        compiler_params=_cp(("arbitrary",)),
    )(ca_t, dm_sh, w, m, v)


def _exchange(copies, name, out_type, n_sems, args, sequencer, after=None):
    order = [] if after is None else [after]
    n_in, n_out = len(args) + len(order), len(out_type)

    def body(*refs):
        barrier = pltpu.get_barrier_semaphore()
        peers = sequencer[1](*_pos())
        for peer in peers:
            pl.semaphore_signal(barrier, inc=1, device_id=peer, device_id_type=MESH)
        pl.semaphore_wait(barrier, len(peers))
        copies(refs[:n_in], refs[n_in:n_in + n_out], refs[n_in + n_out], refs[n_in + n_out + 1])

    sems = [pltpu.SemaphoreType.DMA((n_sems,))] * 2
    return pl.kernel(body, out_type, mesh=plsc.ScalarSubcoreMesh(axis_name="sequencer", num_cores=1), name=name,
                     scratch_types=sems, compiler_params=pltpu.CompilerParams(collective_id=sequencer[0]))(*args, *order)


def _sibling(x, y, c):
    return [(x, y, 1 - c)]


def _other_chips(x, y, c):
    return [(1 - x, y, c), (x, 1 - y, c), (1 - x, 1 - y, c)]


def _to_wire(g, name, after=None):
    _, rr, cc = g.shape
    rb = min(rr, 256)

    def body(g_ref, *rest):
        rest[-1][...] = g_ref[...].astype(GRAD_WIRE_DTYPE)

    order = [] if after is None else [after]
    blk = pl.BlockSpec((None, rb, cc), lambda k, j: (k, j, 0))
    return pl.pallas_call(
        body, name=name, grid=(N_CHIP, rr // rb), in_specs=[blk] + [ANY] * len(order), out_specs=blk,
        out_shape=jax.ShapeDtypeStruct(g.shape, GRAD_WIRE_DTYPE), compiler_params=_cp(("parallel", "parallel")),
    )(g, *order)


def _chip_scatter(ps, name, collective_id, after=None):
    n = len(ps)

    def copies(ins, outs, ssem, rsem):
        x, y, c = _pos()
        cps = []
        for a in range(n):
            for q, (fx, fy) in enumerate(((1, 0), (0, 1), (1, 1))):
                px, py = _flip(x, fx), _flip(y, fy)
                cp = pltpu.make_async_remote_copy(
                    src_ref=ins[a].at[2 * px + py], dst_ref=outs[a].at[q],
                    send_sem=ssem.at[3 * a + q], recv_sem=rsem.at[3 * a + q], device_id=(px, py, c), device_id_type=MESH)
                cp.start()
                cps.append(cp)
        for cp in cps:
            cp.wait()

    out_type = [jax.ShapeDtypeStruct((N_CHIP - 1,) + p.shape[1:], p.dtype) for p in ps]
    return _exchange(copies, name, out_type, 3 * n, ps, (collective_id, _other_chips), after)


def _add_owner(p, got, chipidx, name, after=None):
    _, hr, cc = p.shape
    rb = min(hr, 256)

    def body(k_ref, p_ref, r_ref, *rest):
        rest[-1][...] = ((p_ref[...].astype(F32) + r_ref[0].astype(F32)) + r_ref[1].astype(F32)) + r_ref[2].astype(F32)

    order = [] if after is None else [after]
    return pl.pallas_call(
        body, name=name,
        grid_spec=pltpu.PrefetchScalarGridSpec(
            num_scalar_prefetch=1, grid=(hr // rb,),
            in_specs=[pl.BlockSpec((None, rb, cc), lambda j, k_ref: (k_ref[0], j, 0)),
                      pl.BlockSpec((N_CHIP - 1, rb, cc), lambda j, k_ref: (0, j, 0))] + [ANY] * len(order),
            out_specs=pl.BlockSpec((rb, cc), lambda j, k_ref: (j, 0))),
        out_shape=jax.ShapeDtypeStruct((hr, cc), F32),
        compiler_params=_cp(("parallel",)),
    )(chipidx, p, got, *order)


def _sib_exchange(ts_, name, collective_id, after=None):
    n = len(ts_)

    def copies(ins, outs, ssem, rsem):
        x, y, c = _pos()
        cps = []
        for a in range(n):
            cp = pltpu.make_async_remote_copy(src_ref=ins[a], dst_ref=outs[a], send_sem=ssem.at[a],
                                              recv_sem=rsem.at[a], device_id=(x, y, 1 - c), device_id_type=MESH)
            cp.start()
            cps.append(cp)
        for cp in cps:
            cp.wait()

    out_type = [jax.ShapeDtypeStruct(t.shape, F32) for t in ts_]
    return _exchange(copies, name, out_type, n, ts_, (collective_id, _sibling), after)


def _adam_2d(w, g_own, g_sib, m, v, name):
    rr, cc = w.shape
    rb = min(rr, 256)

    def body(w_ref, go_ref, gs_ref, m_ref, v_ref, g_out, d_out, m_out, v_out):
        g = go_ref[...] + gs_ref[...]
        dl, m2, v2 = _adam(w_ref[...], g, m_ref[...], v_ref[...])
        g_out[...] = g
        d_out[...] = dl
        m_out[...] = m2
        v_out[...] = v2

    blk = pl.BlockSpec((rb, cc), lambda j: (j, 0))
    return pl.pallas_call(
        body, name=name, grid=(rr // rb,), in_specs=[blk] * 5, out_specs=[blk] * 4,
        out_shape=[jax.ShapeDtypeStruct((rr, cc), F32)] * 4, compiler_params=_cp(("parallel",)),
    )(w, g_own, g_sib, m, v)


def kernel(x, c, norm_g, mod_w, mod_b, hy_w_in, hy_conv_w, hy_conv_b, lru_w_a, lru_b_a, lru_w_x, lru_b_x, lru_lambda, sc_conv_w, hy_w_out, pool_w_in, pool_w_grp, pool_b_grp, pool_scale, pool_w_out, final_g, loss_target, m_norm_g, m_mod_w, m_mod_b, m_hy_w_in, m_hy_conv_w, m_hy_conv_b, m_lru_w_a, m_lru_b_a, m_lru_w_x, m_lru_b_x, m_lru_lambda, m_sc_conv_w, m_hy_w_out, m_pool_w_in, m_pool_w_grp, m_pool_b_grp, m_pool_scale, m_pool_w_out, m_final_g, v_norm_g, v_mod_w, v_mod_b, v_hy_w_in, v_hy_conv_w, v_hy_conv_b, v_lru_w_a, v_lru_b_a, v_lru_w_x, v_lru_b_x, v_lru_lambda, v_sc_conv_w, v_hy_w_out, v_pool_w_in, v_pool_w_grp, v_pool_b_grp, v_pool_scale, v_pool_w_out, v_final_g):
    ax, ay, ac = _pos()
    me = 4 * ax + 2 * ay + ac
    chip = 2 * ax + ay
    xs = x[0]
    tgt = loss_target[0]
    gd = POOL_GROUP_DIM
    kidx = chip.reshape(1).astype(jnp.int32)

    big = [hy_w_in[0], hy_w_out[0], pool_w_in[0], pool_w_grp[0].reshape(4 * 128, gd), pool_w_out[0]]
    w_in0, w_out0 = _wgather_sequencer(
        [_wcast_own_block(w, kidx, f"wcast_own_block_{a}") for a, w in enumerate(big[:2])], "wgather_l0", CIDS_WGATHER[0])

    ca_all, mod_all, small_w = _mod_fwd(jnp.broadcast_to(c, (SUBLANES, D)), mod_w, mod_b,
                                        hy_conv_w[0], sc_conv_w[0], pool_b_grp, pool_scale)
    mod_me = lax.dynamic_index_in_dim(mod_all, me, axis=1, keepdims=False)
    sh0, sc0, gt0 = (mod_me[0:1, k * D:(k + 1) * D] for k in range(3))
    sh1, sc1, gt1 = (mod_me[1:2, k * D:(k + 1) * D] for k in range(3))
    cw = small_w[SW_CONV:SW_CONV + 4, 0:D]
    sw = small_w[SW_SC:SW_SC + 3, 0:D]
    pool_b = small_w[SW_POOL_B:SW_POOL_B + 1, :]
    pool_s = small_w[SW_POOL_S:SW_POOL_S + 1, :]
    g0, g1, gf = norm_g[0:1], norm_g[1:2], final_g.reshape(1, D)
    cb, ba, bx, lam = hy_conv_b, lru_b_a, lru_b_x, lru_lambda

    w_in1, w_grp, w_out1 = _wgather_sequencer(
        [_wcast_own_block(w, kidx, f"wcast_own_block_{a + 2}", after=(w_out0, small_w)) for a, w in enumerate(big[2:])],
        "wgather_l1", CIDS_WGATHER[1])
    w_grp =w_grp.reshape(N_CHIP, 4, 128, gd).transpose(1, 0, 2, 3).reshape(4, gd, gd)
    wa_b, wx_b = _wcast([lru_w_a[0], lru_w_x[0]])

    x1, hst, y0, xc0, cz0, h0, proj0 = _l0_fwd(xs, g0, sc0, sh0, w_in0, gt0, cw, cb, wa_b, ba, wx_b, bx, lam, sw,
                                               w_out0.reshape(2 * D, D))
    dpool, mixed, y1, dx2, losscols, dgf, h1, proj1 = _l1_fwd(x1, g1, sc1, sh1, w_in1, tgt, gt1, w_grp, pool_b, pool_s,
                                                              w_out1.reshape(2 * D, D), gf)

    def add_owners(grads, got, tag, ids, after):
        own = []
        for a, (g, r) in enumerate(zip(grads, got)):
            own.append(_add_owner(g, r, kidx, f"grad_add_owner_{tag}{a}", own[-1] if own else after))
        return own, _sib_exchange(own, f"grad_sib_exchange_{tag}", ids[1])

    dproj1, mt1, d_wgrp, dsc1, dbg1 = _l1_bwd_mix(dx2, proj1, mixed, y1, dpool, gt1, w_grp, pool_s,
                                                  w_out1.reshape(2 * D, D))
    d_win1, wire_win1 = _wgrad(h1, dproj1, N_CHIP, D, D, lambda g: 0, lambda g: g, "l1_wgrad_in")
    d_wout1, wire_wout1, dgate1 = _wo_final(mt1, w_out1, gt1, "l1_wo_final")
    d_wgrp = d_wgrp.reshape(4, N_CHIP, 128, gd).transpose(1, 0, 2, 3).reshape(N_CHIP, 4 * 128, gd)
    grads_l1 = [d_win1, d_wgrp, d_wout1]
    got_l1 = _chip_scatter([wire_win1, _to_wire(d_wgrp, "grad_to_wire_grp"), wire_wout1], "grad_chip_scatter_l1",
                           CIDS_L1[0])
    dx1, s1_1, s2_1 = _dgrad_norm(dproj1, w_in1, x1, dx2, g1, sc1, "l1_bwd_proj")

    dproj0, mt0, d_wa, d_wx, sm0 = _l0_bwd_mix(dx1, proj0, hst, y0, xc0, cz0, gt0, cw, wa_b, ba, wx_b, bx, lam, sw,
                                               w_out0.reshape(2 * D, D))
    sums_l1, sib_l1 = add_owners(grads_l1, got_l1, "l1", CIDS_L1, after=sm0)
    d_win0, wire_win0 = _wgrad(h0, dproj0, N_CHIP, D, 6 * D // N_CHIP, lambda g: 0, lambda g: g, "l0_wgrad_in",
                               after=sums_l1[-1])
    d_wout0, wire_wout0, dgate0 = _wo_final(mt0, w_out0, gt0, "l0_wo_final")
    grads_l0 = [d_win0, d_wout0]
    got_l0 = _chip_scatter([wire_win0, wire_wout0], "grad_chip_scatter_l0", CIDS_L0[0], after=sib_l1[0])
    grad_x, s1_0, s2_0 = _dgrad_norm(dproj0, w_in0, xs, dx1, g0, sc0, "l0_bwd_proj", after=wire_win0)
    sums_l0, sib_l0 = add_owners(grads_l0, got_l0, "l0", CIDS_L0, after=s1_0)

    buf_a, dmod8 = _small_pack(s1_0, s2_0, s1_1, s2_1, sm0, dsc1, dbg1, dgf, losscols, dgate0, dgate1,
                                      norm_g, sc0, sc1, lam)
    hw = LRU_HEADS * LRU_HEAD_DIM
    buf_b = jnp.concatenate([d_wa.reshape(hw, LRU_HEAD_DIM), d_wx.reshape(hw, LRU_HEAD_DIM)], axis=0)
    red_a, red_b, dm_all = _small_comm(buf_a, buf_b, dmod8)
    small = [(norm_g, m_norm_g, v_norm_g), (mod_b, m_mod_b, v_mod_b),
             (hy_conv_w[0], m_hy_conv_w[0], v_hy_conv_w[0]), (hy_conv_b, m_hy_conv_b, v_hy_conv_b),
             tuple(a.reshape(hw, LRU_HEAD_DIM) for a in (lru_w_a, m_lru_w_a, v_lru_w_a)),
             (lru_b_a, m_lru_b_a, v_lru_b_a),
             tuple(a.reshape(hw, LRU_HEAD_DIM) for a in (lru_w_x, m_lru_w_x, v_lru_w_x)),
             (lru_b_x, m_lru_b_x, v_lru_b_x), (lru_lambda, m_lru_lambda, v_lru_lambda),
             (sc_conv_w[0], m_sc_conv_w[0], v_sc_conv_w[0]), (pool_b_grp, m_pool_b_grp, v_pool_b_grp),
             (pool_scale, m_pool_scale, v_pool_scale),
             tuple(a.reshape(1, D) for a in (final_g, m_final_g, v_final_g))]
    small_names = ["norm_g", "mod_b", "hy_conv_w", "hy_conv_b", "lru_w_a", "lru_b_a", "lru_w_x", "lru_b_x",
                   "lru_lambda", "sc_conv_w", "pool_b_grp", "pool_scale", "final_g"]
    small_out = _small_adam(red_a, red_b, dm_all, small)
    res = {}
    shapes = dict(norm_g=norm_g, mod_b=mod_b, hy_conv_w=hy_conv_w, hy_conv_b=hy_conv_b, lru_w_a=lru_w_a, lru_b_a=lru_b_a,
                  lru_w_x=lru_w_x, lru_b_x=lru_b_x, lru_lambda=lru_lambda, sc_conv_w=sc_conv_w, pool_b_grp=pool_b_grp,
                  pool_scale=pool_scale, final_g=final_g)
    for p, nm in enumerate(small_names):
        res[nm] = tuple(o.reshape(shapes[nm].shape) for o in small_out[4 * p:4 * p + 4])

    nw = mod_w.shape[2]
    assert nw == DMOD_W
    dm_sh = jnp.stack([lax.dynamic_index_in_dim(dm_all, N_CHIP * l + chip, axis=1, keepdims=False) for l in range(2)])
    res["mod_w"] = tuple(_modw_adam(ca_all.T, dm_sh, mod_w, m_mod_w, v_mod_w))

    sums = list(sums_l0) + list(sums_l1)
    sib_sums = list(sib_l0) + list(sib_l1)
    big_names = ["hy_w_in", "hy_w_out", "pool_w_in", "pool_w_grp", "pool_w_out"]
    big_wmv = [(hy_w_in, m_hy_w_in, v_hy_w_in), (hy_w_out, m_hy_w_out, v_hy_w_out), (pool_w_in, m_pool_w_in, v_pool_w_in),
               (pool_w_grp, m_pool_w_grp, v_pool_w_grp), (pool_w_out, m_pool_w_out, v_pool_w_out)]
    for a, nm in enumerate(big_names):
        rr, cc = big[a].shape
        w, m, v = (t.reshape(rr, cc) for t in big_wmv[a])
        outs = _adam_2d(w, sums[a], sib_sums[a], m, v, f"adam_{nm}")
        res[nm] = tuple(o.reshape(big_wmv[a][0].shape) for o in outs)

    loss = red_a[ROW_LOSS, 0]
    order = ["norm_g", "mod_w", "mod_b", "hy_w_in", "hy_conv_w", "hy_conv_b", "lru_w_a", "lru_b_a", "lru_w_x", "lru_b_x",
             "lru_lambda", "sc_conv_w", "hy_w_out", "pool_w_in", "pool_w_grp", "pool_b_grp", "pool_scale", "pool_w_out",
             "final_g"]
    return (loss, grad_x[None], *[res[nm][0] for nm in order], *[res[nm][1] for nm in order],
            *[res[nm][2] for nm in order], *[res[nm][3] for nm in order])
```

```python
import jax
import jax.numpy as jnp
from jax import lax
from jax.experimental import pallas as pl
from jax.experimental.pallas import tpu as pltpu
from jax.experimental.pallas import tpu_sc as plsc

F32, BF16 = jnp.float32, jnp.bfloat16
D = 1024
RMS_EPS = 1e-6
SQRT_FLOOR = 1e-30
LRU_C = 8.0
LRU_HEADS, LRU_HEAD_DIM = 8, 128
POOL_WINDOWS = (2, 4, 8, 16)
POOL_GROUP_DIM = 512
ADAM_LR, ADAM_B1, ADAM_B2, ADAM_EPS, ADAM_WD, ADAM_STEP = 0.001, 0.9, 0.999, 1e-08, 0.01, 10
MESH = pl.DeviceIdType.MESH
CIDS_WGATHER = (1, 8)
CIDS_L1 = (2, 3)
CIDS_L0 = (4, 5)
N_DEV, N_CHIP = 8, 4
SUBLANES = 8
BF16_ROWS = 16
POOL_HALO = 16
TS_MIX, TS_WGRAD, TS_DGRAD = 256, 2048, 512
SMALL_ROWS = 64
GRAD_WIRE_DTYPE = BF16
ANY = pl.BlockSpec(memory_space=pl.ANY)
VMEM = pl.BlockSpec(memory_space=pltpu.VMEM)
NT = (((1,), (1,)), ((), ()))
TN = (((0,), (0,)), ((), ()))


def _cp(sem=None, vmem_mb=56):
    kw = dict(vmem_limit_bytes=vmem_mb * 2 ** 20)
    if sem is not None:
        kw["dimension_semantics"] = sem
    return pltpu.CompilerParams(**kw)


def _tile(n, t):
    return min(n, t)


def _pos():
    return lax.axis_index("x"), lax.axis_index("y"), lax.axis_index("c")


def _flip(v, f):
    return 1 - v if f else v


def _sigmoid(z):
    return 0.5 * jnp.tanh(0.5 * z) + 0.5


def _rows(n, c):
    return lax.broadcasted_iota(jnp.int32, (n, c), 0)


def _down(a, d):
    return a if d == 0 else pltpu.roll(a, d, 0)


def _up(a, d):
    return a if d == 0 else pltpu.roll(a, a.shape[0] - d, 0)


def _scan_fwd_steps(a, u, carry):
    n, c = a.shape
    sub = _rows(SUBLANES, c)
    out = []
    for k in range(n // SUBLANES):
        p = a[k * SUBLANES:(k + 1) * SUBLANES]
        g = u[k * SUBLANES:(k + 1) * SUBLANES]
        for d in (1, 2, 4):
            keep = sub >= d
            g = g + p * jnp.where(keep, pltpu.roll(g, d, 0), 0.0)
            p = p * jnp.where(keep, pltpu.roll(p, d, 0), 1.0)
        h = g + p * carry
        carry = h[SUBLANES - 1:SUBLANES, :]
        out.append(h)
        yield
    return jnp.concatenate(out, axis=0)


def _scan_rev_steps(alpha, b, carry):
    n, c = alpha.shape
    sub = _rows(SUBLANES, c)
    out = []
    for k in reversed(range(n // SUBLANES)):
        p = alpha[k * SUBLANES:(k + 1) * SUBLANES]
        g = b[k * SUBLANES:(k + 1) * SUBLANES]
        for d in (1, 2, 4):
            keep = sub < SUBLANES - d
            g = g + p * jnp.where(keep, pltpu.roll(g, SUBLANES - d, 0), 0.0)
            p = p * jnp.where(keep, pltpu.roll(p, SUBLANES - d, 0), 1.0)
        h = g + p * carry
        carry = h[0:1, :]
        out.append(h)
        yield
    return jnp.concatenate(out[::-1], axis=0)


def _run(steps):
    while True:
        try:
            next(steps)
        except StopIteration as done:
            return done.value


def _paired(progress, pieces):
    n, done = len(pieces), 1
    pieces[0]()
    for frac in progress:
        while done < n and done <= frac * n:
            pieces[done]()
            done += 1
    while done < n:
        pieces[done]()
        done += 1


def _conv_taps(ext, halo, n, width):
    return [_down(ext, width - 1 - k)[halo:halo + n] for k in range(width)]


def _lru_gates(xc, wa_ref, ba, wx_ref, bx):
    xb = xc.astype(BF16)
    pa, px = [], []
    for h in range(LRU_HEADS):
        xh = xb[:, h * LRU_HEAD_DIM:(h + 1) * LRU_HEAD_DIM]
        pa.append(jnp.dot(xh, wa_ref[h], preferred_element_type=F32))
        px.append(jnp.dot(xh, wx_ref[h], preferred_element_type=F32))
    r = _sigmoid(jnp.concatenate(pa, axis=1) + ba)
    ig = _sigmoid(jnp.concatenate(px, axis=1) + bx)
    return r, ig


def _softplus_neg(lam):
    return jnp.maximum(-lam, 0.0) + jnp.log1p(jnp.exp(-jnp.abs(lam)))


def _recip_1_to_2(d):
    r0 = pl.reciprocal(d, approx=True)
    return r0 * (2.0 - d * r0)


def _lru_decay(r, sp, first):
    big_l = (-LRU_C) * r * sp
    a = jnp.exp(big_l)
    th = jnp.tanh(big_l)
    q = (-2.0 * th) * _recip_1_to_2(1.0 - th)
    rs = lax.rsqrt(jnp.maximum(q, SQRT_FLOOR))
    return a, jnp.where(first, 1.0, q * rs), rs


def _pool_inv_counts(t0, n):
    t = (t0 + lax.broadcasted_iota(jnp.int32, (n, 1), 0) + 1).astype(F32)
    return [1.0 / jnp.minimum(t, float(w)) for w in POOL_WINDOWS]


def _window_sums(ext, shift):
    gd = POOL_GROUP_DIM
    out = []
    s = ext
    for k in range(len(POOL_WINDOWS)):
        s = s + shift(s, 2 ** k)
        out.append(s[:, 0:gd])
        if k + 1 < len(POOL_WINDOWS):
            s = s[:, gd:]
    return out


SW_ROWS, SW_COLS = 16, 2 * D
SW_CONV, SW_SC, SW_POOL_B, SW_POOL_S = 0, 4, 8, 9


def _mod_fwd(c8, mod_w, mod_b, conv_w, sc_w, pool_b, pool_s):
    nw = mod_w.shape[2]
    cq, pq = conv_w.shape[1], pool_b.shape[1]

    def body(c_ref, w_ref, b_ref, cw_ref, sw_ref, pb_ref, ps_ref, ca_ref, mod_ref, small_ref,
             cslot, mslot, msend, pslot, psend, s1, r1, s2, r2, s3, r3):
        x, y, c = _pos()
        me = 4 * x + 2 * y + c
        chip = 2 * x + y
        first = []
        for r in range(1, N_DEV):
            fx, fy, fc = (r >> 2) & 1, (r >> 1) & 1, r & 1
            cp = pltpu.make_async_remote_copy(
                src_ref=c_ref, dst_ref=cslot.at[me], send_sem=s1.at[r - 1], recv_sem=r1.at[r - 1],
                device_id=(_flip(x, fx), _flip(y, fy), _flip(c, fc)), device_id_type=MESH)
            cp.start()
            first.append(cp)
        cslot[me] = c_ref[...]
        for cp in first:
            cp.wait()
        rows = _rows(SUBLANES, D)
        call = jnp.zeros((SUBLANES, D), F32)
        for d in range(N_DEV):
            call = jnp.where(rows == d, cslot[d], call)
        ca = call * _sigmoid(call)
        ca_ref[...] = ca
        for l in range(2):
            msend[l] = jnp.dot(ca, w_ref[l], precision=lax.Precision.HIGHEST, preferred_element_type=F32)
        psend[...] = jnp.zeros_like(psend)
        psend[SW_CONV:SW_CONV + 4, 0:cq] = cw_ref[...]
        psend[SW_SC:SW_SC + 3, 0:cq] = sw_ref[...]
        psend[SW_POOL_B:SW_POOL_B + 1, :] = pb_ref[...]
        psend[SW_POOL_S:SW_POOL_S + 1, :] = ps_ref[...]
        second = []
        for q, (fx, fy) in enumerate(((1, 0), (0, 1), (1, 1))):
            peer = (_flip(x, fx), _flip(y, fy), c)
            for src, dst, ss, rs in ((msend, mslot, s2, r2), (psend, pslot, s3, r3)):
                cp = pltpu.make_async_remote_copy(src_ref=src, dst_ref=dst.at[chip], send_sem=ss.at[q], recv_sem=rs.at[q],
                                                  device_id=peer, device_id_type=MESH)
                cp.start()
                second.append(cp)
        mslot[chip] = msend[...]
        pslot[chip] = psend[...]
        for cp in second:
            cp.wait()
        small_ref[...] = jnp.zeros_like(small_ref)
        for j in range(N_CHIP):
            for l in range(2):
                mod_ref[l, :, j * nw:(j + 1) * nw] = mslot[j, l] + b_ref[l:l + 1, j * nw:(j + 1) * nw]
            small_ref[0:SUBLANES, j * cq:(j + 1) * cq] = pslot[j, 0:SUBLANES, 0:cq]
            small_ref[SUBLANES:SW_ROWS, j * pq:(j + 1) * pq] = pslot[j, SUBLANES:SW_ROWS, :]

    args = (c8, mod_w, mod_b, conv_w, sc_w, pool_b, pool_s)
    dma3 = pltpu.SemaphoreType.DMA((N_CHIP - 1,))
    return pl.pallas_call(
        body, name="mod_fwd",
        in_specs=[VMEM] * len(args), out_specs=[VMEM] * 3,
        out_shape=[jax.ShapeDtypeStruct((SUBLANES, D), F32), jax.ShapeDtypeStruct((2, SUBLANES, N_CHIP * nw), F32),
                   jax.ShapeDtypeStruct((SW_ROWS, SW_COLS), F32)],
        scratch_shapes=[pltpu.VMEM((N_DEV, SUBLANES, D), F32), pltpu.VMEM((N_CHIP, 2, SUBLANES, nw), F32),
                        pltpu.VMEM((2, SUBLANES, nw), F32), pltpu.VMEM((N_CHIP, SW_ROWS, pq), F32),
                        pltpu.VMEM((SW_ROWS, pq), F32),
                        pltpu.SemaphoreType.DMA((N_DEV - 1,)), pltpu.SemaphoreType.DMA((N_DEV - 1,)),
                        dma3, dma3, dma3, dma3],
        compiler_params=_cp(),
    )(*args)


def _wcast(ws):
    def body(*refs):
        n = len(refs) // 2
        for a in range(n):
            refs[n + a][...] = refs[a][...].astype(BF16)

    return pl.pallas_call(
        body, name="wcast", in_specs=[VMEM] * len(ws), out_specs=[VMEM] * len(ws),
        out_shape=[jax.ShapeDtypeStruct(w.shape, BF16) for w in ws], compiler_params=_cp(),
    )(*ws)


def _wcast_own_block(w, kidx, name, after=()):
    rr, cc = w.shape
    rb = min(rr, 256)

    def body(k_ref, w_ref, *rest):
        rest[-1][...] = w_ref[...].astype(BF16)

    order = list(after)
    return pl.pallas_call(
        body, name=name,
        grid_spec=pltpu.PrefetchScalarGridSpec(
            num_scalar_prefetch=1, grid=(rr // rb,),
            in_specs=[pl.BlockSpec((rb, cc), lambda j, k_ref: (j, 0))] + [ANY] * len(order),
            out_specs=pl.BlockSpec((None, rb, cc), lambda j, k_ref: (k_ref[0], j, 0))),
        out_shape=jax.ShapeDtypeStruct((N_CHIP, rr, cc), BF16),
        compiler_params=_cp(("parallel",)),
    )(kidx, w, *order)


def _wgather_copies(outs, rows, ssem, rsem, fssem, frsem):
    n = len(outs)
    x, y, c = _pos()
    chip = 2 * x + y
    sib = (x, y, 1 - c)
    flips = ((1, 0), (0, 1), (1, 1))

    def half(a, which):
        hr = rows[a] // 2
        return pl.ds(pl.multiple_of(which * hr, BF16_ROWS), hr)

    sends = []
    for a in range(n):
        mine = outs[a].at[chip, half(a, c), :]
        for q, (fx, fy) in enumerate(flips):
            cp = pltpu.make_async_remote_copy(
                src_ref=mine, dst_ref=mine, send_sem=ssem.at[3 * a + q], recv_sem=rsem.at[3 * a + q],
                device_id=(_flip(x, fx), _flip(y, fy), c), device_id_type=MESH)
            cp.start()
            sends.append(cp)
    passed = []
    for a in range(n):
        for q, (fx, fy) in enumerate(flips):
            src_chip = 2 * _flip(x, fx) + _flip(y, fy)
            landed = outs[a].at[src_chip, half(a, c), :]
            pltpu.make_async_remote_copy(
                src_ref=landed, dst_ref=landed, send_sem=ssem.at[3 * a + q], recv_sem=rsem.at[3 * a + q],
                device_id=sib, device_id_type=MESH).wait_recv()
            cp = pltpu.make_async_remote_copy(
                src_ref=landed, dst_ref=landed, send_sem=fssem.at[3 * a + q], recv_sem=frsem.at[3 * a + q],
                device_id=sib, device_id_type=MESH)
            cp.start()
            passed.append(cp)
    for a in range(n):
        for q, (fx, fy) in enumerate(flips):
            src_chip = 2 * _flip(x, fx) + _flip(y, fy)
            other = outs[a].at[src_chip, half(a, 1 - c), :]
            pltpu.make_async_remote_copy(
                src_ref=other, dst_ref=other, send_sem=fssem.at[3 * a + q], recv_sem=frsem.at[3 * a + q],
                device_id=sib, device_id_type=MESH).wait_recv()
    for cp in sends + passed:
        cp.wait_send()


def _wgather_sequencer(bufs, name, collective_id):
    n = len(bufs)
    refs = [jax.new_ref(b, memory_space=pltpu.MemorySpace.HBM) for b in bufs]
    dma = pltpu.SemaphoreType.DMA((3 * n,))

    @pl.kernel(mesh=plsc.ScalarSubcoreMesh(axis_name="sequencer", num_cores=1), name=name,
               scratch_types=(dma, dma, dma, dma), compiler_params=pltpu.CompilerParams(collective_id=collective_id))
    def launch(ssem, rsem, fssem, frsem):
        x, y, c = _pos()
        barrier = pltpu.get_barrier_semaphore()
        for peer in ((1 - x, y, c), (x, 1 - y, c), (1 - x, 1 - y, c), (x, y, 1 - c)):
            pl.semaphore_signal(barrier, inc=1, device_id=peer, device_id_type=MESH)
        pl.semaphore_wait(barrier, 4)
        _wgather_copies(refs, [b.shape[1] for b in bufs], ssem, rsem, fssem, frsem)

    launch()
    return [r[...] for r in refs]


def _norm_mod(x, g, sc, sh, name):
    s_len = x.shape[0]
    ts = _tile(s_len, TS_WGRAD)

    def body(x_ref, g_ref, sc_ref, sh_ref, h_ref):
        xv = x_ref[...]
        rinv = lax.rsqrt(jnp.mean(xv * xv, axis=-1, keepdims=True) + RMS_EPS)
        h_ref[...] = (xv * rinv * (g_ref[...] * (1.0 + sc_ref[...])) + sh_ref[...]).astype(BF16)

    row = pl.BlockSpec((ts, D), lambda i: (i, 0))
    vec = pl.BlockSpec((1, D), lambda i: (0, 0))
    return pl.pallas_call(
        body, name=name, grid=(s_len // ts,), in_specs=[row, vec, vec, vec], out_specs=row,
        out_shape=jax.ShapeDtypeStruct((s_len, D), BF16), compiler_params=_cp(("parallel",)),
    )(x, g, sc, sh)


def _l0_fwd(h0_all, x, w_in, gate, cw, cb, wa, ba, wx, bx, lam, sw, wo):
    s_len, nb = x.shape[0], w_in.shape[2]
    ts = _tile(s_len, TS_MIX)
    n_t = s_len // ts
    hl = SUBLANES

    def body(h0_ref, xb_ref, win_ref, gate_ref, cw_ref, cb_ref, wa_ref, ba_ref, wx_ref, bx_ref,
             lam_ref, sw_ref, wo_ref, x1_ref, h_ref, y_ref, xc_ref, cz_ref, p_ref, pcur, pnext, cxa, czz, chh):
        i = pl.program_id(0)

        @pl.when(i == 0)
        def _():
            cxa[...] = jnp.zeros_like(cxa)
            czz[...] = jnp.zeros_like(czz)
            chh[...] = jnp.zeros_like(chh)
            pnext[...] = jnp.zeros_like(pnext)

        pcur[...] = pnext[...]
        h0 = h0_ref[...]

        def project(k, c0, cn):
            def emit():
                pk = jnp.dot(h0, win_ref[k, :, c0:c0 + cn], preferred_element_type=F32).astype(BF16)
                p_ref[:, k * nb + c0:k * nb + c0 + cn] = pk
                pnext[:, k * nb + c0:k * nb + c0 + cn] = pk
            return emit

        def mixer():
            piece = lambda k: pcur[:, k * D:(k + 1) * D].astype(F32)
            xa = piece(0)
            rows = _rows(ts, D)
            taps = _conv_taps(jnp.concatenate([cxa[...], xa], axis=0), hl, ts, 4)
            xc = cb_ref[...] + sum(cw_ref[k:k + 1, :] * taps[k] for k in range(4))
            xc_ref[...] = xc.astype(BF16)
            r, ig = _lru_gates(xc, wa_ref, ba_ref[...], wx_ref, bx_ref[...])
            a, m, _ = _lru_decay(r, _softplus_neg(lam_ref[...]), (rows == 0) & (i == 1))
            yield 0.26
            h = _run(_scan_fwd_steps(a, m * ig * xc, chh[hl - 1:hl, :]))
            yield 0.51
            gcp, v = piece(3), piece(4)
            z = gcp * v
            ztaps = _conv_taps(jnp.concatenate([czz[...], z], axis=0), hl, ts, 3)
            cz = sum(sw_ref[k:k + 1, :] * ztaps[k] for k in range(3))
            cz_ref[...] = cz.astype(BF16)
            yb = piece(2) * cz
            ga, gb = piece(1), piece(5)
            y = jnp.concatenate([h * (ga * _sigmoid(ga)), yb * (gb * _sigmoid(gb))], axis=1).astype(BF16)
            yield 0.76
            y_ref[...] = y
            x1_ref[...] = xb_ref[...] + gate_ref[...] * jnp.dot(y, wo_ref[...], preferred_element_type=F32)
            h_ref[...] = h.astype(BF16)
            cxa[...] = xa[ts - hl:, :]
            czz[...] = z[ts - hl:, :]
            chh[...] = jnp.where(i > 0, h[ts - hl:, :], 0.0)

        _paired(mixer(), [project(k, 0, nb) for k in range(N_CHIP)])

    def full(a):
        return pl.BlockSpec(a.shape, lambda i: (0,) * a.ndim)

    ahead = lambda w: pl.BlockSpec((ts, w), lambda i: (jnp.minimum(i, n_t - 1), 0))
    behind = lambda w: pl.BlockSpec((ts, w), lambda i: (jnp.maximum(i - 1, 0), 0))
    args = (h0_all, x, w_in, gate, cw, cb, wa, ba, wx, bx, lam, sw, wo)
    return pl.pallas_call(
        body, name="l0_fwd", grid=(n_t + 1,),
        in_specs=[ahead(D), behind(D)] + [full(a) for a in args[2:]],
        out_specs=[behind(D), behind(D), behind(2 * D), behind(D), behind(D), ahead(N_CHIP * nb)],
        out_shape=[jax.ShapeDtypeStruct((s_len, D), F32), jax.ShapeDtypeStruct((s_len, D), BF16),
                   jax.ShapeDtypeStruct((s_len, 2 * D), BF16), jax.ShapeDtypeStruct((s_len, D), BF16),
                   jax.ShapeDtypeStruct((s_len, D), BF16), jax.ShapeDtypeStruct((s_len, N_CHIP * nb), BF16)],
        scratch_shapes=[pltpu.VMEM((ts, N_CHIP * nb), BF16)] * 2 + [pltpu.VMEM((hl, D), F32)] * 3,
        compiler_params=_cp(("arbitrary",)),
    )(*args)


def _l1_fwd(x1, g, sc, sh, w_in, tgt, gate, wg, bg, scale, wo, gf):
    s_len, nb = x1.shape[0], w_in.shape[2]
    ts = _tile(s_len, TS_MIX)
    n_t = s_len // ts
    pw, gd, hl = 2 * D, POOL_GROUP_DIM, POOL_HALO

    def body(xa_ref, xb_ref, t_ref, g_ref, sc_ref, sh_ref, win_ref, gate_ref, wg_ref, bg_ref, scl_ref, wo_ref, gf_ref,
             d_ref, mx_ref, y_ref, dx_ref, loss_ref, dgf_ref, h1_ref, p_ref, pcur, pnext, cv):
        i = pl.program_id(0)

        @pl.when(i == 0)
        def _():
            cv[...] = jnp.zeros_like(cv)
            loss_ref[...] = jnp.zeros_like(loss_ref)
            dgf_ref[...] = jnp.zeros_like(dgf_ref)
            pnext[...] = jnp.zeros_like(pnext)

        pcur[...] = pnext[...]
        xv = xa_ref[...]
        rinv = lax.rsqrt(jnp.mean(xv * xv, axis=-1, keepdims=True) + RMS_EPS)
        h1 = (xv * rinv * (g_ref[...] * (1.0 + sc_ref[...])) + sh_ref[...]).astype(BF16)
        h1_ref[...] = h1

        def project(k):
            def emit():
                pk = jnp.dot(h1, win_ref[k], preferred_element_type=F32).astype(BF16)
                p_ref[:, k * nb:(k + 1) * nb] = pk
                pnext[:, k * nb:(k + 1) * nb] = pk
            return emit

        def mixer():
            v = pcur[:, 0:pw].astype(F32)
            sums = _window_sums(jnp.concatenate([cv[...], v], axis=0), _down)
            inv = _pool_inv_counts(jnp.maximum(i - 1, 0) * ts, ts)
            dd = [sums[k][hl:hl + ts] * inv[k] - v[:, k * gd:(k + 1) * gd] for k in range(4)]
            d_ref[...] = jnp.concatenate(dd, axis=1).astype(BF16)
            yield 0.26
            mixed = jnp.concatenate(
                [jnp.dot(dd[k].astype(BF16), wg_ref[k], preferred_element_type=F32) for k in range(4)], axis=1) + bg_ref[...]
            mx_ref[...] = mixed.astype(BF16)
            gg = pcur[:, pw:2 * pw].astype(F32)
            y = (mixed * scl_ref[...] * (gg * _sigmoid(gg))).astype(BF16)
            y_ref[...] = y
            yield 0.51
            x2 = xb_ref[...] + gate_ref[...] * jnp.dot(y, wo_ref[...], preferred_element_type=F32)
            yield 0.76
            r2 = lax.rsqrt(jnp.mean(x2 * x2, axis=-1, keepdims=True) + RMS_EPS)
            n2 = x2 * r2
            err = n2 * gf_ref[...] - t_ref[...]
            loss_ref[...] += jnp.where(i > 0, jnp.sum(err * err, axis=0, keepdims=True), 0.0)
            dyf = err * (1.0 / D)
            dgf_ref[...] += jnp.where(i > 0, jnp.sum(dyf * n2, axis=0, keepdims=True), 0.0)
            dn = dyf * gf_ref[...]
            dx_ref[...] = r2 * (dn - n2 * jnp.mean(dn * n2, axis=-1, keepdims=True))
            cv[...] = v[ts - hl:, :]

        _paired(mixer(), [project(k) for k in range(N_CHIP)])

    def full(a):
        return pl.BlockSpec(a.shape, lambda i: (0,) * a.ndim)

    ahead = lambda w: pl.BlockSpec((ts, w), lambda i: (jnp.minimum(i, n_t - 1), 0))
    behind = lambda w: pl.BlockSpec((ts, w), lambda i: (jnp.maximum(i - 1, 0), 0))
    acc = pl.BlockSpec((1, D), lambda i: (0, 0))
    args = (x1, x1, tgt, g, sc, sh, w_in, gate, wg, bg, scale, wo, gf)
    return pl.pallas_call(
        body, name="l1_fwd", grid=(n_t + 1,),
        in_specs=[ahead(D), behind(D), behind(D)] + [full(a) for a in args[3:]],
        out_specs=[behind(pw), behind(pw), behind(pw), behind(D), acc, acc, ahead(D), ahead(N_CHIP * nb)],
        out_shape=[jax.ShapeDtypeStruct((s_len, pw), BF16)] * 3 + [jax.ShapeDtypeStruct((s_len, D), F32)]
        + [jax.ShapeDtypeStruct((1, D), F32)] * 2
        + [jax.ShapeDtypeStruct((s_len, D), BF16), jax.ShapeDtypeStruct((s_len, N_CHIP * nb), BF16)],
        scratch_shapes=[pltpu.VMEM((ts, N_CHIP * nb), BF16)] * 2 + [pltpu.VMEM((hl, pw), F32)],
        compiler_params=_cp(("arbitrary",)),
    )(*args)


def _l1_bwd_mix(dx2, proj, mixed, y, dpool, gate, wg, scale, wo):
    s_len = dx2.shape[0]
    n_sub = 2
    ts = _tile(s_len, n_sub * TS_MIX)
    sub = ts // n_sub
    n_t = s_len // ts
    pw, gd, hl = 2 * D, POOL_GROUP_DIM, POOL_HALO

    def body(dx_ref, gg_ref, mx_ref, y_ref, d_ref, gate_ref, wg_ref, sc_ref, wo_ref,
             dp_ref, mt_ref, dwg_ref, dsc_ref, dbg_ref, cq):
        i = pl.program_id(0)

        @pl.when(i == 0)
        def _():
            cq[...] = jnp.zeros_like(cq)
            dsc_ref[...] = jnp.zeros_like(dsc_ref)
            dbg_ref[...] = jnp.zeros_like(dbg_ref)
            mt_ref[...] = jnp.zeros_like(mt_ref)
            dwg_ref[...] = jnp.zeros_like(dwg_ref)

        ahead_rows = {}

        def chain(j):
            rows = slice(j * sub, (j + 1) * sub)
            dxv = dx_ref[rows, :]
            dxb = dxv.astype(BF16)
            dy = lax.dot_general((gate_ref[...] * dxv).astype(BF16), wo_ref[...], NT, preferred_element_type=F32)
            for k in range(2):
                mt_ref[k] += lax.dot_general(y_ref[rows, k * gd:(k + 1) * gd], dxb, TN, preferred_element_type=F32)
            yield
            gg = gg_ref[rows, :].astype(F32)
            mixed = mx_ref[rows, :].astype(F32)
            s = _sigmoid(gg)
            sg = gg * s
            dym = dy * mixed
            dmixed = dy * sc_ref[...] * sg
            dsc_ref[...] += jnp.sum(dym * sg, axis=0, keepdims=True)
            dbg_ref[...] += jnp.sum(dmixed, axis=0, keepdims=True)
            dmb = dmixed.astype(BF16)
            dp_ref[rows, pw:2 * pw] = (dym * sc_ref[...] * (s + sg * (1.0 - s))).astype(BF16)
            yield
            inv = _pool_inv_counts((n_t - 1 - i) * ts + j * sub, sub)
            dd = []
            for k in range(4):
                dmk = dmb[:, k * gd:(k + 1) * gd]
                dd.append(lax.dot_general(dmk, wg_ref[k], NT, preferred_element_type=F32))
                dwg_ref[k] += lax.dot_general(d_ref[rows, k * gd:(k + 1) * gd], dmk, TN, preferred_element_type=F32)
            for k in range(2, 4):
                mt_ref[k] += lax.dot_general(y_ref[rows, k * gd:(k + 1) * gd], dxb, TN, preferred_element_type=F32)
            q = jnp.concatenate([dd[k] * inv[k] for k in range(4)], axis=1)
            ahead_rows[j] = q[0:hl, :]
            yield
            behind_q = cq[...] if j == n_sub - 1 else ahead_rows[j + 1]
            sums = _window_sums(jnp.concatenate([q, behind_q], axis=0), _up)
            dp_ref[rows, 0:pw] = jnp.concatenate([sums[k][0:sub] - dd[k] for k in range(4)], axis=1).astype(BF16)

        chains = [chain(j) for j in reversed(range(n_sub))]
        for _ in range(4):
            for ch in chains:
                next(ch, None)
        cq[...] = ahead_rows[0]

    def full(a):
        return pl.BlockSpec(a.shape, lambda i: (0,) * a.ndim)

    rev = lambda w, j=0: pl.BlockSpec((ts, w), lambda i: (n_t - 1 - i, j))
    acc = pl.BlockSpec((1, pw), lambda i: (0, 0))
    return pl.pallas_call(
        body, name="l1_bwd_mix", grid=(n_t,),
        in_specs=[rev(D), rev(pw, 1), rev(pw), rev(pw), rev(pw)] + [full(a) for a in (gate, wg, scale, wo)],
        out_specs=[rev(2 * pw), pl.BlockSpec((N_CHIP, gd, D), lambda i: (0, 0, 0)),
                   pl.BlockSpec((4, gd, gd), lambda i: (0, 0, 0)), acc, acc],
        out_shape=[jax.ShapeDtypeStruct((s_len, 2 * pw), BF16), jax.ShapeDtypeStruct((N_CHIP, gd, D), F32),
                   jax.ShapeDtypeStruct((4, gd, gd), F32),
                   jax.ShapeDtypeStruct((1, pw), F32), jax.ShapeDtypeStruct((1, pw), F32)],
        scratch_shapes=[pltpu.VMEM((hl, pw), F32)],
        compiler_params=_cp(("arbitrary",)),
    )(dx2, proj, mixed, y, dpool, gate, wg, scale, wo)


def _l0_bwd_mix(dx1, proj, hst, y, xc, cz, gate, cw, wa, ba, wx, bx, lam, sw, wo):
    s_len = dx1.shape[0]
    ts = _tile(s_len, TS_MIX)
    n_t = s_len // ts
    hl, hb = SUBLANES, BF16_ROWS
    yb_w = 2 * D // N_CHIP

    def body(dx_ref, p_ref, h_ref, hh_ref, y_ref, xc_ref, cz_ref, gate_ref, cw_ref, wa_ref, ba_ref, wx_ref, bx_ref,
             lam_ref, sw_ref, wo_ref, dp_ref, mt_ref, dwa_ref, dwx_ref, sm_ref, cg, cdxc, cdcz, ca):
        i = pl.program_id(0)
        ri = n_t - 1 - i

        @pl.when(i == 0)
        def _():
            cg[...] = jnp.zeros_like(cg)
            ca[...] = jnp.zeros_like(ca)
            cdxc[...] = jnp.zeros_like(cdxc)
            cdcz[...] = jnp.zeros_like(cdcz)
            sm_ref[...] = jnp.zeros_like(sm_ref)
            mt_ref[...] = jnp.zeros_like(mt_ref)
            dwa_ref[...] = jnp.zeros_like(dwa_ref)
            dwx_ref[...] = jnp.zeros_like(dwx_ref)

        dxb = dx_ref[...].astype(BF16)

        def wgrad_out(k):
            mt_ref[k] += lax.dot_general(y_ref[:, k * yb_w:(k + 1) * yb_w], dxb, TN, preferred_element_type=F32)

        wgrad_out(0)
        has_prev = (ri > 0).astype(F32)
        xa, ga, gbp, gcp, v, gb = [p_ref[:, k * D:(k + 1) * D].astype(F32) for k in range(6)]
        rows = _rows(ts, D)
        first = (rows == 0) & (ri == 0)
        xc = xc_ref[...].astype(F32)
        cz = cz_ref[...].astype(F32)
        r, ig = _lru_gates(xc, wa_ref, ba_ref[...], wx_ref, bx_ref[...])
        sp = _softplus_neg(lam_ref[...])
        a, m, inv_m = _lru_decay(r, sp, first)
        z = gcp * v
        h = h_ref[...].astype(F32)
        hprev = _down(jnp.concatenate([hh_ref[...].astype(F32)[hb - hl:hb] * has_prev, h], axis=0), 1)[hl:hl + ts]
        dy = lax.dot_general((gate_ref[...] * dx_ref[...]).astype(BF16), wo_ref[...], NT, preferred_element_type=F32)
        dya_pre, dyb_pre = dy[:, 0:D], dy[:, D:2 * D]
        s_a, s_b = _sigmoid(ga), _sigmoid(gb)
        silu_a, silu_b = ga * s_a, gb * s_b
        dp_ref[:, D:2 * D] = (dya_pre * h * (s_a + silu_a * (1.0 - s_a))).astype(BF16)
        dp_ref[:, 5 * D:6 * D] = (dyb_pre * (gbp * cz) * (s_b + silu_b * (1.0 - s_b))).astype(BF16)
        dya = dya_pre * silu_a
        dyb = dyb_pre * silu_b
        wgrad_out(1)
        dp_ref[:, 2 * D:3 * D] = (dyb * cz).astype(BF16)
        dcz = dyb * gbp
        dcz_ext = jnp.concatenate([dcz, cdcz[...]], axis=0)
        dcz_taps = [_up(dcz_ext, 2 - k)[0:ts] for k in range(3)]
        for k in range(3):
            sm_ref[8 + k:9 + k, :] += jnp.sum(z * dcz_taps[k], axis=0, keepdims=True)
        dz = sum(sw_ref[k:k + 1, :] * dcz_taps[k] for k in range(3))
        dp_ref[:, 3 * D:4 * D] = (dz * v).astype(BF16)
        dp_ref[:, 4 * D:5 * D] = (dz * gcp).astype(BF16)
        cdcz[...] = dcz[0:hl, :]
        alpha = _up(jnp.concatenate([a, ca[...]], axis=0), 1)[0:ts]
        wgrad_out(2)
        dh = _run(_scan_rev_steps(alpha, dya, cg[0:1, :]))
        wgrad_out(3)
        cg[...] = dh[0:hl, :]
        ca[...] = a[0:hl, :]
        da = dh * hprev
        dhx = dh * xc
        dm = dhx * ig
        di = dhx * m
        dxc = dh * (m * ig)
        dl = a * (da - jnp.where(first, 0.0, dm * a * inv_m))
        dlr = dl * r
        sm_ref[7:8, :] += jnp.sum(dlr, axis=0, keepdims=True) * (-LRU_C)
        dpa = dlr * (sp * (-LRU_C)) * (1.0 - r)
        dpx = di * ig * (1.0 - ig)
        sm_ref[5:6, :] += jnp.sum(dpa, axis=0, keepdims=True)
        sm_ref[6:7, :] += jnp.sum(dpx, axis=0, keepdims=True)
        dpa_b, dpx_b, xc_b = dpa.astype(BF16), dpx.astype(BF16), xc.astype(BF16)
        back = []
        for hd in range(LRU_HEADS):
            sl = slice(hd * LRU_HEAD_DIM, (hd + 1) * LRU_HEAD_DIM)
            back.append(lax.dot_general(dpa_b[:, sl], wa_ref[hd], NT, preferred_element_type=F32)
                        + lax.dot_general(dpx_b[:, sl], wx_ref[hd], NT, preferred_element_type=F32))
            dwa_ref[hd] += lax.dot_general(xc_b[:, sl], dpa_b[:, sl], TN, preferred_element_type=F32)
            dwx_ref[hd] += lax.dot_general(xc_b[:, sl], dpx_b[:, sl], TN, preferred_element_type=F32)
        dxc = dxc + jnp.concatenate(back, axis=1)
        sm_ref[4:5, :] += jnp.sum(dxc, axis=0, keepdims=True)
        dxc_ext = jnp.concatenate([dxc, cdxc[...]], axis=0)
        dxc_taps = [_up(dxc_ext, 3 - k)[0:ts] for k in range(4)]
        for k in range(4):
            sm_ref[k:k + 1, :] += jnp.sum(xa * dxc_taps[k], axis=0, keepdims=True)
        dp_ref[:, 0:D] = sum(cw_ref[k:k + 1, :] * dxc_taps[k] for k in range(4)).astype(BF16)
        cdxc[...] = dxc[0:hl, :]

    def full(a):
        return pl.BlockSpec(a.shape, lambda i: (0,) * a.ndim)

    rev = lambda w: pl.BlockSpec((ts, w), lambda i: (n_t - 1 - i, 0))
    halo = lambda w: pl.BlockSpec((hb, w), lambda i: (jnp.maximum((n_t - 1 - i) * (ts // hb) - 1, 0), 0))
    return pl.pallas_call(
        body, name="l0_bwd_mix", grid=(n_t,),
        in_specs=[rev(D), rev(6 * D), rev(D), halo(D), rev(2 * D), rev(D), rev(D)]
        + [full(a) for a in (gate, cw, wa, ba, wx, bx, lam, sw, wo)],
        out_specs=[rev(6 * D), pl.BlockSpec((N_CHIP, yb_w, D), lambda i: (0, 0, 0)),
                   pl.BlockSpec(wa.shape, lambda i: (0, 0, 0)), pl.BlockSpec(wa.shape, lambda i: (0, 0, 0)),
                   pl.BlockSpec((2 * SUBLANES, D), lambda i: (0, 0))],
        out_shape=[jax.ShapeDtypeStruct((s_len, 6 * D), BF16), jax.ShapeDtypeStruct((N_CHIP, yb_w, D), F32),
                   jax.ShapeDtypeStruct(wa.shape, F32), jax.ShapeDtypeStruct(wa.shape, F32),
                   jax.ShapeDtypeStruct((2 * SUBLANES, D), F32)],
        scratch_shapes=[pltpu.VMEM((hl, D), F32)] * 4,
        compiler_params=_cp(("arbitrary",)),
    )(dx1, proj, hst, hst, y, xc, cz, gate, cw, wa, ba, wx, bx, lam, sw, wo)


def _dgrad_norm(dproj, w, x, dres, g, sc, name, after=None):
    s_len, nb = x.shape[0], w.shape[2]
    ts = _tile(s_len, TS_DGRAD)
    order = [] if after is None else [after]

    def body(dp_ref, w_ref, x_ref, dr_ref, g_ref, sc_ref, *rest):
        dx_ref, s1_ref, s2_ref = rest[len(order):]

        @pl.when(pl.program_id(0) == 0)
        def _():
            s1_ref[...] = jnp.zeros_like(s1_ref)
            s2_ref[...] = jnp.zeros_like(s2_ref)

        dh = sum(lax.dot_general(dp_ref[:, k * nb:(k + 1) * nb], w_ref[k], NT, preferred_element_type=F32)
                 for k in range(N_CHIP))
        xv = x_ref[...]
        r = lax.rsqrt(jnp.mean(xv * xv, axis=-1, keepdims=True) + RMS_EPS)
        n = xv * r
        s1_ref[...] += jnp.sum(dh, axis=0, keepdims=True)
        s2_ref[...] += jnp.sum(dh * n, axis=0, keepdims=True)
        dn = dh * (g_ref[...] * (1.0 + sc_ref[...]))
        dx_ref[...] = dr_ref[...] + r * (dn - n * jnp.mean(dn * n, axis=-1, keepdims=True))

    row = lambda wd: pl.BlockSpec((ts, wd), lambda i: (i, 0))
    vec = pl.BlockSpec((1, D), lambda i: (0, 0))
    return pl.pallas_call(
        body, name=name, grid=(s_len // ts,),
        in_specs=[row(N_CHIP * nb), pl.BlockSpec(w.shape, lambda i: (0, 0, 0)), row(D), row(D), vec, vec]
        + [ANY] * len(order),
        out_specs=[row(D), vec, vec],
        out_shape=[jax.ShapeDtypeStruct((s_len, D), F32)] + [jax.ShapeDtypeStruct((1, D), F32)] * 2,
        compiler_params=_cp(("arbitrary",)),
    )(dproj, w, x, dres, g, sc, *order)


def _wgrad(a, b, groups, ka, nb, a_col, b_col, name, after=None):
    s_len = a.shape[0]
    ts = _tile(s_len, TS_WGRAD)
    n_s = s_len // ts
    order = [] if after is None else [after]

    def body(a_ref, b_ref, *rest):
        o_ref, wire_ref = rest[-2:]

        @pl.when(pl.program_id(1) == 0)
        def _():
            o_ref[...] = jnp.zeros_like(o_ref)

        o_ref[...] += lax.dot_general(a_ref[...].astype(BF16), b_ref[...].astype(BF16), TN, preferred_element_type=F32)

        @pl.when(pl.program_id(1) == n_s - 1)
        def _():
            wire_ref[...] = o_ref[...].astype(GRAD_WIRE_DTYPE)

    blk = pl.BlockSpec((None, ka, nb), lambda g, s: (g, 0, 0))
    return pl.pallas_call(
        body, name=name, grid=(groups, n_s),
        in_specs=[pl.BlockSpec((ts, ka), lambda g, s: (s, a_col(g))), pl.BlockSpec((ts, nb), lambda g, s: (s, b_col(g)))]
        + [ANY] * len(order),
        out_specs=[blk, blk],
        out_shape=[jax.ShapeDtypeStruct((groups, ka, nb), F32), jax.ShapeDtypeStruct((groups, ka, nb), GRAD_WIRE_DTYPE)],
        compiler_params=_cp(("parallel", "arbitrary")),
    )(a, b, *order)


def _wo_final(mt, wo, gate, name):
    rb = mt.shape[1]

    def body(m_ref, w_ref, gate_ref, dw_ref, wire_ref, dg_ref):
        @pl.when(pl.program_id(0) == 0)
        def _():
            dg_ref[...] = jnp.zeros_like(dg_ref)

        mv = m_ref[...]
        dw = mv * gate_ref[...]
        dw_ref[...] = dw
        wire_ref[...] = dw.astype(GRAD_WIRE_DTYPE)
        dg_ref[...] += jnp.sum(mv * w_ref[...].astype(F32), axis=0, keepdims=True)

    blk = pl.BlockSpec((None, rb, D), lambda k: (k, 0, 0))
    vec = pl.BlockSpec((1, D), lambda k: (0, 0))
    return pl.pallas_call(
        body, name=name, grid=(N_CHIP,), in_specs=[blk, blk, vec], out_specs=[blk, blk, vec],
        out_shape=[jax.ShapeDtypeStruct(mt.shape, F32), jax.ShapeDtypeStruct(mt.shape, GRAD_WIRE_DTYPE),
                   jax.ShapeDtypeStruct((1, D), F32)],
        compiler_params=_cp(("arbitrary",)),
    )(mt, wo, gate)


ROW_NORM_G, ROW_CONV_W, ROW_CONV_B, ROW_B_A, ROW_B_X, ROW_LAMBDA, ROW_SC_W, ROW_POOL_B, ROW_POOL_S, ROW_FINAL_G = (
    0, 2, 6, 7, 8, 9, 10, 13, 15, 17)
ROW_LOSS = 18
DMOD_W = 6 * D // SUBLANES


def _small_pack(s1_0, s2_0, s1_1, s2_1, sm0, dsc1, dbg1, dgf, losscols, dgate0, dgate1, norm_g, sc0, sc1, lam):
    def body(s1_0r, s2_0r, s1_1r, s2_1r, sm, dsc, dbg, dgfr, lcols, dg0, dg1, ng, sc0r, sc1r, lamr, buf, dmod):
        buf[...] = jnp.zeros_like(buf)
        buf[0:1, :] = s2_0r[...] * (1.0 + sc0r[...])
        buf[1:2, :] = s2_1r[...] * (1.0 + sc1r[...])
        buf[ROW_CONV_W:ROW_CONV_W + 4, :] = sm[0:4, :]
        buf[ROW_CONV_B:ROW_CONV_B + 1, :] = sm[4:5, :]
        buf[ROW_B_A:ROW_B_A + 1, :] = sm[5:6, :]
        buf[ROW_B_X:ROW_B_X + 1, :] = sm[6:7, :]
        buf[ROW_LAMBDA:ROW_LAMBDA + 1, :] = -sm[7:8, :] * _sigmoid(-lamr[...])
        buf[ROW_SC_W:ROW_SC_W + 3, :] = sm[8:11, :]
        for k in range(2):
            buf[ROW_POOL_B + k:ROW_POOL_B + k + 1, :] = dbg[:, k * D:(k + 1) * D]
            buf[ROW_POOL_S + k:ROW_POOL_S + k + 1, :] = dsc[:, k * D:(k + 1) * D]
        buf[ROW_FINAL_G:ROW_FINAL_G + 1, :] = dgfr[...]
        pieces = (s1_0r[...], s2_0r[...] * ng[0:1, :], dg0[...], s1_1r[...], s2_1r[...] * ng[1:2, :], dg1[...])
        flat = jnp.concatenate(pieces, axis=1)
        for r in range(SUBLANES):
            dmod[r:r + 1, :] = flat[:, r * DMOD_W:(r + 1) * DMOD_W]
        buf[ROW_LOSS:ROW_LOSS + 1, :] = jnp.broadcast_to(jnp.sum(lcols[...], axis=1, keepdims=True) * (0.5 / D), (1, D))

    args = (s1_0, s2_0, s1_1, s2_1, sm0, dsc1, dbg1, dgf, losscols, dgate0, dgate1, norm_g, sc0, sc1, lam)
    return pl.pallas_call(
        body, name="small_pack", in_specs=[VMEM] * len(args), out_specs=[VMEM] * 2,
        out_shape=[jax.ShapeDtypeStruct((SMALL_ROWS, D), F32), jax.ShapeDtypeStruct((SUBLANES, DMOD_W), F32)],
        compiler_params=_cp(),
    )(*args)


def _small_comm(buf_a, buf_b, dmod8):
    ra, rb = buf_a.shape[0] // N_DEV, buf_b.shape[0] // N_DEV
    wb = buf_b.shape[1]

    def body(a_ref, b_ref, dm_ref, oa_ref, ob_ref, odm_ref, ina, inb, dslot, sa, sb, s1, r1, s2, r2):
        x, y, c = _pos()
        me = 4 * x + 2 * y + c
        peers = []
        for r in range(1, N_DEV):
            fx, fy, fc = (r >> 2) & 1, (r >> 1) & 1, r & 1
            px, py, pc = _flip(x, fx), _flip(y, fy), _flip(c, fc)
            peers.append(((px, py, pc), 4 * px + 2 * py + pc))
        seg_a = lambda d: pl.ds(pl.multiple_of(d * ra, SUBLANES), ra)
        seg_b = lambda d: pl.ds(pl.multiple_of(d * rb, SUBLANES), rb)
        first = []
        for r, (peer, pid) in enumerate(peers):
            for k, (src, dst) in enumerate(((a_ref.at[seg_a(pid), :], ina.at[r]), (b_ref.at[seg_b(pid), :], inb.at[r]),
                                            (dm_ref, dslot.at[me]))):
                cp = pltpu.make_async_remote_copy(src_ref=src, dst_ref=dst, send_sem=s1.at[3 * r + k],
                                                  recv_sem=r1.at[3 * r + k], device_id=peer, device_id_type=MESH)
                cp.start()
                first.append(cp)
        dslot[me] = dm_ref[...]
        for cp in first:
            cp.wait()
        acc_a, acc_b = a_ref[seg_a(me), :], b_ref[seg_b(me), :]
        for r in range(N_DEV - 1):
            acc_a = acc_a + ina[r]
            acc_b = acc_b + inb[r]
        sa[...] = acc_a
        sb[...] = acc_b
        oa_ref[seg_a(me), :] = acc_a
        ob_ref[seg_b(me), :] = acc_b
        second = []
        for r, (peer, pid) in enumerate(peers):
            for k, (src, dst) in enumerate(((sa, oa_ref.at[seg_a(me), :]), (sb, ob_ref.at[seg_b(me), :]))):
                cp = pltpu.make_async_remote_copy(src_ref=src, dst_ref=dst, send_sem=s2.at[2 * r + k],
                                                  recv_sem=r2.at[2 * r + k], device_id=peer, device_id_type=MESH)
                cp.start()
                second.append(cp)
        odm_ref[...] = dslot[...]
        for cp in second:
            cp.wait()

    nrel = N_DEV - 1
    return pl.pallas_call(
        body, name="small_comm", in_specs=[VMEM] * 3, out_specs=[VMEM] * 3,
        out_shape=[jax.ShapeDtypeStruct(buf_a.shape, F32), jax.ShapeDtypeStruct(buf_b.shape, F32),
                   jax.ShapeDtypeStruct((N_DEV,) + dmod8.shape, F32)],
        scratch_shapes=[pltpu.VMEM((nrel, ra, D), F32), pltpu.VMEM((nrel, rb, wb), F32),
                        pltpu.VMEM((N_DEV,) + dmod8.shape, F32), pltpu.VMEM((ra, D), F32), pltpu.VMEM((rb, wb), F32),
                        pltpu.SemaphoreType.DMA((3 * nrel,)), pltpu.SemaphoreType.DMA((3 * nrel,)),
                        pltpu.SemaphoreType.DMA((2 * nrel,)), pltpu.SemaphoreType.DMA((2 * nrel,))],
        compiler_params=_cp(),
    )(buf_a, buf_b, dmod8)


def _adam(w, g, m, v):
    m2 = ADAM_B1 * m + (1.0 - ADAM_B1) * g
    v2 = ADAM_B2 * v + (1.0 - ADAM_B2) * (g * g)
    m_hat = m2 / (1.0 - ADAM_B1 ** ADAM_STEP)
    v_hat = v2 / (1.0 - ADAM_B2 ** ADAM_STEP)
    return -ADAM_LR * (m_hat / (jnp.sqrt(v_hat) + ADAM_EPS) + ADAM_WD * w), m2, v2


def _small_adam(red_a, red_b, dm_all, params):
    n = len(params)

    def body(*refs):
        ra, rb, dm = refs[:3]
        wmv = refs[3:3 + 3 * n]
        outs = refs[3 + 3 * n:]
        x, y, _ = _pos()
        chip = 2 * x + y

        def shard(row0, nrows, width):
            per_row = D // width
            cands = []
            for k in range(N_CHIP):
                if nrows == 1 or per_row >= N_CHIP:
                    cands.append(ra[row0:row0 + nrows, k * width:(k + 1) * width])
                else:
                    rr, cc = divmod(k * width, D)
                    cands.append(ra[row0 + rr:row0 + rr + 1, cc:cc + width])
            g = cands[0]
            for k in range(1, N_CHIP):
                g = jnp.where(chip == k, cands[k], g)
            return g

        dms = jnp.sum(dm[...], axis=0)
        hw = LRU_HEADS * LRU_HEAD_DIM
        grads = [
            ra[ROW_NORM_G:ROW_NORM_G + 2, :],
            None,
            shard(ROW_CONV_W, 4, D // N_CHIP),
            ra[ROW_CONV_B:ROW_CONV_B + 1, :],
            rb[0:hw, :],
            ra[ROW_B_A:ROW_B_A + 1, :],
            rb[hw:2 * hw, :],
            ra[ROW_B_X:ROW_B_X + 1, :],
            ra[ROW_LAMBDA:ROW_LAMBDA + 1, :],
            shard(ROW_SC_W, 3, D // N_CHIP),
            shard(ROW_POOL_B, 2, 2 * D // N_CHIP),
            shard(ROW_POOL_S, 2, 2 * D // N_CHIP),
            ra[ROW_FINAL_G:ROW_FINAL_G + 1, :],
        ]
        for p in range(n):
            w_ref, m_ref, v_ref = wmv[3 * p:3 * p + 3]
            g_out, d_out, m_out, v_out = outs[4 * p:4 * p + 4]
            if grads[p] is None:
                for r in range(SUBLANES):
                    l, cols = r // N_CHIP, slice((r % N_CHIP) * DMOD_W, (r % N_CHIP + 1) * DMOD_W)
                    g = dms[r:r + 1, :]
                    dl, m2, v2 = _adam(w_ref[l:l + 1, cols], g, m_ref[l:l + 1, cols], v_ref[l:l + 1, cols])
                    g_out[l:l + 1, cols] = g
                    d_out[l:l + 1, cols] = dl
                    m_out[l:l + 1, cols] = m2
                    v_out[l:l + 1, cols] = v2
            else:
                g = grads[p]
                dl, m2, v2 = _adam(w_ref[...], g, m_ref[...], v_ref[...])
                g_out[...] = g
                d_out[...] = dl
                m_out[...] = m2
                v_out[...] = v2

    flat = [a for p in params for a in p]
    return pl.pallas_call(
        body, name="small_adam", in_specs=[VMEM] * (3 + len(flat)), out_specs=[VMEM] * (4 * n),
        out_shape=[jax.ShapeDtypeStruct(p[0].shape, F32) for p in params for _ in range(4)],
        compiler_params=_cp(),
    )(red_a, red_b, dm_all, *flat)


def _modw_adam(ca_t, dm_sh, w, m, v):
    nw = w.shape[2]

    def body(c_ref, d_ref, w_ref, m_ref, v_ref, g_out, d_out, m_out, v_out):
        g = jnp.dot(c_ref[...], d_ref[...], precision=lax.Precision.HIGHEST, preferred_element_type=F32)
        dl, m2, v2 = _adam(w_ref[...], g, m_ref[...], v_ref[...])
        g_out[...] = g
        d_out[...] = dl
        m_out[...] = m2
        v_out[...] = v2

    blk = pl.BlockSpec((None, D, nw), lambda l: (l, 0, 0))
    return pl.pallas_call(
        body, name="modw_adam", grid=(2,),
        in_specs=[pl.BlockSpec((D, SUBLANES), lambda l: (0, 0)), pl.BlockSpec((None, SUBLANES, nw), lambda l: (l, 0, 0)),
                  blk, blk, blk],
        out_specs=[blk] * 4, out_shape=[jax.ShapeDtypeStruct(w.shape, F32)] * 4,
        compiler_params=_cp(("arbitrary",)),
    )(ca_t, dm_sh, w, m, v)


def _exchange(copies, name, out_type, n_sems, args, sequencer, after=None):
    order = [] if after is None else [after]
    n_in, n_out = len(args) + len(order), len(out_type)

    def body(*refs):
        barrier = pltpu.get_barrier_semaphore()
        peers = sequencer[1](*_pos())
        for peer in peers:
            pl.semaphore_signal(barrier, inc=1, device_id=peer, device_id_type=MESH)
        pl.semaphore_wait(barrier, len(peers))
        copies(refs[:n_in], refs[n_in:n_in + n_out], refs[n_in + n_out], refs[n_in + n_out + 1])

    sems = [pltpu.SemaphoreType.DMA((n_sems,))] * 2
    return pl.kernel(body, out_type, mesh=plsc.ScalarSubcoreMesh(axis_name="sequencer", num_cores=1), name=name,
                     scratch_types=sems, compiler_params=pltpu.CompilerParams(collective_id=sequencer[0]))(*args, *order)


def _sibling(x, y, c):
    return [(x, y, 1 - c)]


def _other_chips(x, y, c):
    return [(1 - x, y, c), (x, 1 - y, c), (1 - x, 1 - y, c)]


def _to_wire(g, name, after=None):
    _, rr, cc = g.shape
    rb = min(rr, 256)

    def body(g_ref, *rest):
        rest[-1][...] = g_ref[...].astype(GRAD_WIRE_DTYPE)

    order = [] if after is None else [after]
    blk = pl.BlockSpec((None, rb, cc), lambda k, j: (k, j, 0))
    return pl.pallas_call(
        body, name=name, grid=(N_CHIP, rr // rb), in_specs=[blk] + [ANY] * len(order), out_specs=blk,
        out_shape=jax.ShapeDtypeStruct(g.shape, GRAD_WIRE_DTYPE), compiler_params=_cp(("parallel", "parallel")),
    )(g, *order)


def _chip_scatter(ps, name, collective_id, after=None):
    n = len(ps)

    def copies(ins, outs, ssem, rsem):
        x, y, c = _pos()
        cps = []
        for a in range(n):
            for q, (fx, fy) in enumerate(((1, 0), (0, 1), (1, 1))):
                px, py = _flip(x, fx), _flip(y, fy)
                cp = pltpu.make_async_remote_copy(
                    src_ref=ins[a].at[2 * px + py], dst_ref=outs[a].at[q],
                    send_sem=ssem.at[3 * a + q], recv_sem=rsem.at[3 * a + q], device_id=(px, py, c), device_id_type=MESH)
                cp.start()
                cps.append(cp)
        for cp in cps:
            cp.wait()

    out_type = [jax.ShapeDtypeStruct((N_CHIP - 1,) + p.shape[1:], p.dtype) for p in ps]
    return _exchange(copies, name, out_type, 3 * n, ps, (collective_id, _other_chips), after)


def _add_owner(p, got, chipidx, name, after=None):
    _, hr, cc = p.shape
    rb = min(hr, 256)

    def body(k_ref, p_ref, r_ref, *rest):
        rest[-1][...] = ((p_ref[...].astype(F32) + r_ref[0].astype(F32)) + r_ref[1].astype(F32)) + r_ref[2].astype(F32)

    order = [] if after is None else [after]
    return pl.pallas_call(
        body, name=name,
        grid_spec=pltpu.PrefetchScalarGridSpec(
            num_scalar_prefetch=1, grid=(hr // rb,),
            in_specs=[pl.BlockSpec((None, rb, cc), lambda j, k_ref: (k_ref[0], j, 0)),
                      pl.BlockSpec((N_CHIP - 1, rb, cc), lambda j, k_ref: (0, j, 0))] + [ANY] * len(order),
            out_specs=pl.BlockSpec((rb, cc), lambda j, k_ref: (j, 0))),
        out_shape=jax.ShapeDtypeStruct((hr, cc), F32),
        compiler_params=_cp(("parallel",)),
    )(chipidx, p, got, *order)


def _sib_exchange(ts_, name, collective_id, after=None):
    n = len(ts_)

    def copies(ins, outs, ssem, rsem):
        x, y, c = _pos()
        cps = []
        for a in range(n):
            cp = pltpu.make_async_remote_copy(src_ref=ins[a], dst_ref=outs[a], send_sem=ssem.at[a],
                                              recv_sem=rsem.at[a], device_id=(x, y, 1 - c), device_id_type=MESH)
            cp.start()
            cps.append(cp)
        for cp in cps:
            cp.wait()

    out_type = [jax.ShapeDtypeStruct(t.shape, F32) for t in ts_]
    return _exchange(copies, name, out_type, n, ts_, (collective_id, _sibling), after)


def _adam_2d(w, g_own, g_sib, m, v, name):
    rr, cc = w.shape
    rb = min(rr, 256)

    def body(w_ref, go_ref, gs_ref, m_ref, v_ref, g_out, d_out, m_out, v_out):
        g = go_ref[...] + gs_ref[...]
        dl, m2, v2 = _adam(w_ref[...], g, m_ref[...], v_ref[...])
        g_out[...] = g
        d_out[...] = dl
        m_out[...] = m2
        v_out[...] = v2

    blk = pl.BlockSpec((rb, cc), lambda j: (j, 0))
    return pl.pallas_call(
        body, name=name, grid=(rr // rb,), in_specs=[blk] * 5, out_specs=[blk] * 4,
        out_shape=[jax.ShapeDtypeStruct((rr, cc), F32)] * 4, compiler_params=_cp(("parallel",)),
    )(w, g_own, g_sib, m, v)


def kernel(x, c, norm_g, mod_w, mod_b, hy_w_in, hy_conv_w, hy_conv_b, lru_w_a, lru_b_a, lru_w_x, lru_b_x, lru_lambda, sc_conv_w, hy_w_out, pool_w_in, pool_w_grp, pool_b_grp, pool_scale, pool_w_out, final_g, loss_target, m_norm_g, m_mod_w, m_mod_b, m_hy_w_in, m_hy_conv_w, m_hy_conv_b, m_lru_w_a, m_lru_b_a, m_lru_w_x, m_lru_b_x, m_lru_lambda, m_sc_conv_w, m_hy_w_out, m_pool_w_in, m_pool_w_grp, m_pool_b_grp, m_pool_scale, m_pool_w_out, m_final_g, v_norm_g, v_mod_w, v_mod_b, v_hy_w_in, v_hy_conv_w, v_hy_conv_b, v_lru_w_a, v_lru_b_a, v_lru_w_x, v_lru_b_x, v_lru_lambda, v_sc_conv_w, v_hy_w_out, v_pool_w_in, v_pool_w_grp, v_pool_b_grp, v_pool_scale, v_pool_w_out, v_final_g):
    ax, ay, ac = _pos()
    me = 4 * ax + 2 * ay + ac
    chip = 2 * ax + ay
    xs = x[0]
    tgt = loss_target[0]
    gd = POOL_GROUP_DIM
    kidx = chip.reshape(1).astype(jnp.int32)

    big = [hy_w_in[0], hy_w_out[0], pool_w_in[0], pool_w_grp[0].reshape(4 * 128, gd), pool_w_out[0]]
    w_in0, w_out0 = _wgather_sequencer(
        [_wcast_own_block(w, kidx, f"wcast_own_block_{a}") for a, w in enumerate(big[:2])], "wgather_l0", CIDS_WGATHER[0])

    ca_all, mod_all, small_w = _mod_fwd(jnp.broadcast_to(c, (SUBLANES, D)), mod_w, mod_b,
                                        hy_conv_w[0], sc_conv_w[0], pool_b_grp, pool_scale)
    mod_me = lax.dynamic_index_in_dim(mod_all, me, axis=1, keepdims=False)
    sh0, sc0, gt0 = (mod_me[0:1, k * D:(k + 1) * D] for k in range(3))
    sh1, sc1, gt1 = (mod_me[1:2, k * D:(k + 1) * D] for k in range(3))
    cw = small_w[SW_CONV:SW_CONV + 4, 0:D]
    sw = small_w[SW_SC:SW_SC + 3, 0:D]
    pool_b = small_w[SW_POOL_B:SW_POOL_B + 1, :]
    pool_s = small_w[SW_POOL_S:SW_POOL_S + 1, :]
    g0, g1, gf = norm_g[0:1], norm_g[1:2], final_g.reshape(1, D)
    cb, ba, bx, lam = hy_conv_b, lru_b_a, lru_b_x, lru_lambda

    h0 = _norm_mod(xs, g0, sc0, sh0, "l0_norm")
    wa_b, wx_b = _wcast([lru_w_a[0], lru_w_x[0]])
    w_in1, w_grp, w_out1 = _wgather_sequencer(
        [_wcast_own_block(w, kidx, f"wcast_own_block_{a + 2}", after=(w_out0, h0)) for a, w in enumerate(big[2:])],
        "wgather_l1", CIDS_WGATHER[1])
    w_grp =w_grp.reshape(N_CHIP, 4, 128, gd).transpose(1, 0, 2, 3).reshape(4, gd, gd)

    x1, hst, y0, xc0, cz0, proj0 = _l0_fwd(h0, xs, w_in0, gt0, cw, cb, wa_b, ba, wx_b, bx, lam, sw,
                                           w_out0.reshape(2 * D, D))
    dpool, mixed, y1, dx2, losscols, dgf, h1, proj1 = _l1_fwd(x1, g1, sc1, sh1, w_in1, tgt, gt1, w_grp, pool_b, pool_s,
                                                              w_out1.reshape(2 * D, D), gf)

    def add_owners(grads, got, tag, ids, after):
        own = []
        for a, (g, r) in enumerate(zip(grads, got)):
            own.append(_add_owner(g, r, kidx, f"grad_add_owner_{tag}{a}", own[-1] if own else after))
        return own, _sib_exchange(own, f"grad_sib_exchange_{tag}", ids[1])

    dproj1, mt1, d_wgrp, dsc1, dbg1 = _l1_bwd_mix(dx2, proj1, mixed, y1, dpool, gt1, w_grp, pool_s,
                                                  w_out1.reshape(2 * D, D))
    d_win1, wire_win1 = _wgrad(h1, dproj1, N_CHIP, D, D, lambda g: 0, lambda g: g, "l1_wgrad_in")
    d_wout1, wire_wout1, dgate1 = _wo_final(mt1, w_out1, gt1, "l1_wo_final")
    d_wgrp = d_wgrp.reshape(4, N_CHIP, 128, gd).transpose(1, 0, 2, 3).reshape(N_CHIP, 4 * 128, gd)
    grads_l1 = [d_win1, d_wgrp, d_wout1]
    got_l1 = _chip_scatter([wire_win1, _to_wire(d_wgrp, "grad_to_wire_grp"), wire_wout1], "grad_chip_scatter_l1",
                           CIDS_L1[0])
    dx1, s1_1, s2_1 = _dgrad_norm(dproj1, w_in1, x1, dx2, g1, sc1, "l1_bwd_proj")

    dproj0, mt0, d_wa, d_wx, sm0 = _l0_bwd_mix(dx1, proj0, hst, y0, xc0, cz0, gt0, cw, wa_b, ba, wx_b, bx, lam, sw,
                                               w_out0.reshape(2 * D, D))
    sums_l1, sib_l1 = add_owners(grads_l1, got_l1, "l1", CIDS_L1, after=sm0)
    d_win0, wire_win0 = _wgrad(h0, dproj0, N_CHIP, D, 6 * D // N_CHIP, lambda g: 0, lambda g: g, "l0_wgrad_in",
                               after=sums_l1[-1])
    d_wout0, wire_wout0, dgate0 = _wo_final(mt0, w_out0, gt0, "l0_wo_final")
    grads_l0 = [d_win0, d_wout0]
    got_l0 = _chip_scatter([wire_win0, wire_wout0], "grad_chip_scatter_l0", CIDS_L0[0], after=sib_l1[0])
    grad_x, s1_0, s2_0 = _dgrad_norm(dproj0, w_in0, xs, dx1, g0, sc0, "l0_bwd_proj", after=wire_win0)
    sums_l0, sib_l0 = add_owners(grads_l0, got_l0, "l0", CIDS_L0, after=s1_0)

    buf_a, dmod8 = _small_pack(s1_0, s2_0, s1_1, s2_1, sm0, dsc1, dbg1, dgf, losscols, dgate0, dgate1,
                                      norm_g, sc0, sc1, lam)
    hw = LRU_HEADS * LRU_HEAD_DIM
    buf_b = jnp.concatenate([d_wa.reshape(hw, LRU_HEAD_DIM), d_wx.reshape(hw, LRU_HEAD_DIM)], axis=0)
    red_a, red_b, dm_all = _small_comm(buf_a, buf_b, dmod8)
    small = [(norm_g, m_norm_g, v_norm_g), (mod_b, m_mod_b, v_mod_b),
             (hy_conv_w[0], m_hy_conv_w[0], v_hy_conv_w[0]), (hy_conv_b, m_hy_conv_b, v_hy_conv_b),
             tuple(a.reshape(hw, LRU_HEAD_DIM) for a in (lru_w_a, m_lru_w_a, v_lru_w_a)),
             (lru_b_a, m_lru_b_a, v_lru_b_a),
             tuple(a.reshape(hw, LRU_HEAD_DIM) for a in (lru_w_x, m_lru_w_x, v_lru_w_x)),
             (lru_b_x, m_lru_b_x, v_lru_b_x), (lru_lambda, m_lru_lambda, v_lru_lambda),
             (sc_conv_w[0], m_sc_conv_w[0], v_sc_conv_w[0]), (pool_b_grp, m_pool_b_grp, v_pool_b_grp),
             (pool_scale, m_pool_scale, v_pool_scale),
             tuple(a.reshape(1, D) for a in (final_g, m_final_g, v_final_g))]
    small_names = ["norm_g", "mod_b", "hy_conv_w", "hy_conv_b", "lru_w_a", "lru_b_a", "lru_w_x", "lru_b_x",
                   "lru_lambda", "sc_conv_w", "pool_b_grp", "pool_scale", "final_g"]
    small_out = _small_adam(red_a, red_b, dm_all, small)
    res = {}
    shapes = dict(norm_g=norm_g, mod_b=mod_b, hy_conv_w=hy_conv_w, hy_conv_b=hy_conv_b, lru_w_a=lru_w_a, lru_b_a=lru_b_a,
                  lru_w_x=lru_w_x, lru_b_x=lru_b_x, lru_lambda=lru_lambda, sc_conv_w=sc_conv_w, pool_b_grp=pool_b_grp,
                  pool_scale=pool_scale, final_g=final_g)
    for p, nm in enumerate(small_names):
        res[nm] = tuple(o.reshape(shapes[nm].shape) for o in small_out[4 * p:4 * p + 4])

    nw = mod_w.shape[2]
    assert nw == DMOD_W
    dm_sh = jnp.stack([lax.dynamic_index_in_dim(dm_all, N_CHIP * l + chip, axis=1, keepdims=False) for l in range(2)])
    res["mod_w"] = tuple(_modw_adam(ca_all.T, dm_sh, mod_w, m_mod_w, v_mod_w))

    sums = list(sums_l0) + list(sums_l1)
    sib_sums = list(sib_l0) + list(sib_l1)
    big_names = ["hy_w_in", "hy_w_out", "pool_w_in", "pool_w_grp", "pool_w_out"]
    big_wmv = [(hy_w_in, m_hy_w_in, v_hy_w_in), (hy_w_out, m_hy_w_out, v_hy_w_out), (pool_w_in, m_pool_w_in, v_pool_w_in),
               (pool_w_grp, m_pool_w_grp, v_pool_w_grp), (pool_w_out, m_pool_w_out, v_pool_w_out)]
    for a, nm in enumerate(big_names):
        rr, cc = big[a].shape
        w, m, v = (t.reshape(rr, cc) for t in big_wmv[a])
        outs = _adam_2d(w, sums[a], sib_sums[a], m, v, f"adam_{nm}")
        res[nm] = tuple(o.reshape(big_wmv[a][0].shape) for o in outs)

    loss = red_a[ROW_LOSS, 0]
    order = ["norm_g", "mod_w", "mod_b", "hy_w_in", "hy_conv_w", "hy_conv_b", "lru_w_a", "lru_b_a", "lru_w_x", "lru_b_x",
             "lru_lambda", "sc_conv_w", "hy_w_out", "pool_w_in", "pool_w_grp", "pool_b_grp", "pool_scale", "pool_w_out",
             "final_g"]
    return (loss, grad_x[None], *[res[nm][0] for nm in order], *[res[nm][1] for nm in order],
            *[res[nm][2] for nm in order], *[res[nm][3] for nm in order])
```

```python
import jax
import jax.numpy as jnp
from jax import lax
from jax.experimental import pallas as pl
from jax.experimental.pallas import tpu as pltpu
from jax.experimental.pallas import tpu_sc as plsc

F32, BF16 = jnp.float32, jnp.bfloat16
D = 1024
RMS_EPS = 1e-6
SQRT_FLOOR = 1e-30
LRU_C = 8.0
LRU_HEADS, LRU_HEAD_DIM = 8, 128
POOL_WINDOWS = (2, 4, 8, 16)
POOL_GROUP_DIM = 512
ADAM_LR, ADAM_B1, ADAM_B2, ADAM_EPS, ADAM_WD, ADAM_STEP = 0.001, 0.9, 0.999, 1e-08, 0.01, 10
MESH = pl.DeviceIdType.MESH
CIDS_WGATHER = (1, 8)
CIDS_L1 = (2, 3)
CIDS_L0 = (4, 5)
N_DEV, N_CHIP = 8, 4
SUBLANES = 8
BF16_ROWS = 16
POOL_HALO = 16
TS_MIX, TS_WGRAD, TS_DGRAD = 256, 2048, 512
SMALL_ROWS = 64
GRAD_WIRE_DTYPE = BF16
ANY = pl.BlockSpec(memory_space=pl.ANY)
VMEM = pl.BlockSpec(memory_space=pltpu.VMEM)
NT = (((1,), (1,)), ((), ()))
TN = (((0,), (0,)), ((), ()))


def _cp(sem=None, vmem_mb=56):
    kw = dict(vmem_limit_bytes=vmem_mb * 2 ** 20)
    if sem is not None:
        kw["dimension_semantics"] = sem
    return pltpu.CompilerParams(**kw)


def _tile(n, t):
    return min(n, t)


def _pos():
    return lax.axis_index("x"), lax.axis_index("y"), lax.axis_index("c")


def _flip(v, f):
    return 1 - v if f else v


def _sigmoid(z):
    return 0.5 * jnp.tanh(0.5 * z) + 0.5


def _rows(n, c):
    return lax.broadcasted_iota(jnp.int32, (n, c), 0)


def _down(a, d):
    return a if d == 0 else pltpu.roll(a, d, 0)


def _up(a, d):
    return a if d == 0 else pltpu.roll(a, a.shape[0] - d, 0)


def _scan_fwd_steps(a, u, carry):
    n, c = a.shape
    sub = _rows(SUBLANES, c)
    out = []
    for k in range(n // SUBLANES):
        p = a[k * SUBLANES:(k + 1) * SUBLANES]
        g = u[k * SUBLANES:(k + 1) * SUBLANES]
        for d in (1, 2, 4):
            keep = sub >= d
            g = g + p * jnp.where(keep, pltpu.roll(g, d, 0), 0.0)
            p = p * jnp.where(keep, pltpu.roll(p, d, 0), 1.0)
        h = g + p * carry
        carry = h[SUBLANES - 1:SUBLANES, :]
        out.append(h)
        yield
    return jnp.concatenate(out, axis=0)


def _scan_rev_steps(alpha, b, carry):
    n, c = alpha.shape
    sub = _rows(SUBLANES, c)
    out = []
    for k in reversed(range(n // SUBLANES)):
        p = alpha[k * SUBLANES:(k + 1) * SUBLANES]
        g = b[k * SUBLANES:(k + 1) * SUBLANES]
        for d in (1, 2, 4):
            keep = sub < SUBLANES - d
            g = g + p * jnp.where(keep, pltpu.roll(g, SUBLANES - d, 0), 0.0)
            p = p * jnp.where(keep, pltpu.roll(p, SUBLANES - d, 0), 1.0)
        h = g + p * carry
        carry = h[0:1, :]
        out.append(h)
        yield
    return jnp.concatenate(out[::-1], axis=0)


def _run(steps):
    while True:
        try:
            next(steps)
        except StopIteration as done:
            return done.value


def _paired(progress, pieces):
    n, done = len(pieces), 1
    pieces[0]()
    for frac in progress:
        while done < n and done <= frac * n:
            pieces[done]()
            done += 1
    while done < n:
        pieces[done]()
        done += 1


def _conv_taps(ext, halo, n, width):
    return [_down(ext, width - 1 - k)[halo:halo + n] for k in range(width)]


def _lru_gates(xc, wa_ref, ba, wx_ref, bx):
    xb = xc.astype(BF16)
    pa, px = [], []
    for h in range(LRU_HEADS):
        xh = xb[:, h * LRU_HEAD_DIM:(h + 1) * LRU_HEAD_DIM]
        pa.append(jnp.dot(xh, wa_ref[h], preferred_element_type=F32))
        px.append(jnp.dot(xh, wx_ref[h], preferred_element_type=F32))
    r = _sigmoid(jnp.concatenate(pa, axis=1) + ba)
    ig = _sigmoid(jnp.concatenate(px, axis=1) + bx)
    return r, ig


def _softplus_neg(lam):
    return jnp.maximum(-lam, 0.0) + jnp.log1p(jnp.exp(-jnp.abs(lam)))


def _recip_1_to_2(d):
    r0 = pl.reciprocal(d, approx=True)
    return r0 * (2.0 - d * r0)


def _lru_decay(r, sp, first):
    big_l = (-LRU_C) * r * sp
    a = jnp.exp(big_l)
    th = jnp.tanh(big_l)
    q = (-2.0 * th) * _recip_1_to_2(1.0 - th)
    rs = lax.rsqrt(jnp.maximum(q, SQRT_FLOOR))
    return a, jnp.where(first, 1.0, q * rs), rs


def _pool_inv_counts(t0, n):
    t = (t0 + lax.broadcasted_iota(jnp.int32, (n, 1), 0) + 1).astype(F32)
    return [1.0 / jnp.minimum(t, float(w)) for w in POOL_WINDOWS]


def _window_sums(ext, shift):
    gd = POOL_GROUP_DIM
    out = []
    s = ext
    for k in range(len(POOL_WINDOWS)):
        s = s + shift(s, 2 ** k)
        out.append(s[:, 0:gd])
        if k + 1 < len(POOL_WINDOWS):
            s = s[:, gd:]
    return out


SW_ROWS, SW_COLS = 16, 2 * D
SW_CONV, SW_SC, SW_POOL_B, SW_POOL_S = 0, 4, 8, 9


def _mod_fwd(c8, mod_w, mod_b, conv_w, sc_w, pool_b, pool_s):
    nw = mod_w.shape[2]
    cq, pq = conv_w.shape[1], pool_b.shape[1]

    def body(c_ref, w_ref, b_ref, cw_ref, sw_ref, pb_ref, ps_ref, ca_ref, mod_ref, small_ref,
             cslot, mslot, msend, pslot, psend, s1, r1, s2, r2, s3, r3):
        x, y, c = _pos()
        me = 4 * x + 2 * y + c
        chip = 2 * x + y
        first = []
        for r in range(1, N_DEV):
            fx, fy, fc = (r >> 2) & 1, (r >> 1) & 1, r & 1
            cp = pltpu.make_async_remote_copy(
                src_ref=c_ref, dst_ref=cslot.at[me], send_sem=s1.at[r - 1], recv_sem=r1.at[r - 1],
                device_id=(_flip(x, fx), _flip(y, fy), _flip(c, fc)), device_id_type=MESH)
            cp.start()
            first.append(cp)
        cslot[me] = c_ref[...]
        for cp in first:
            cp.wait()
        rows = _rows(SUBLANES, D)
        call = jnp.zeros((SUBLANES, D), F32)
        for d in range(N_DEV):
            call = jnp.where(rows == d, cslot[d], call)
        ca = call * _sigmoid(call)
        ca_ref[...] = ca
        for l in range(2):
            msend[l] = jnp.dot(ca, w_ref[l], precision=lax.Precision.HIGHEST, preferred_element_type=F32)
        psend[...] = jnp.zeros_like(psend)
        psend[SW_CONV:SW_CONV + 4, 0:cq] = cw_ref[...]
        psend[SW_SC:SW_SC + 3, 0:cq] = sw_ref[...]
        psend[SW_POOL_B:SW_POOL_B + 1, :] = pb_ref[...]
        psend[SW_POOL_S:SW_POOL_S + 1, :] = ps_ref[...]
        second = []
        for q, (fx, fy) in enumerate(((1, 0), (0, 1), (1, 1))):
            peer = (_flip(x, fx), _flip(y, fy), c)
            for src, dst, ss, rs in ((msend, mslot, s2, r2), (psend, pslot, s3, r3)):
                cp = pltpu.make_async_remote_copy(src_ref=src, dst_ref=dst.at[chip], send_sem=ss.at[q], recv_sem=rs.at[q],
                                                  device_id=peer, device_id_type=MESH)
                cp.start()
                second.append(cp)
        mslot[chip] = msend[...]
        pslot[chip] = psend[...]
        for cp in second:
            cp.wait()
        small_ref[...] = jnp.zeros_like(small_ref)
        for j in range(N_CHIP):
            for l in range(2):
                mod_ref[l, :, j * nw:(j + 1) * nw] = mslot[j, l] + b_ref[l:l + 1, j * nw:(j + 1) * nw]
            small_ref[0:SUBLANES, j * cq:(j + 1) * cq] = pslot[j, 0:SUBLANES, 0:cq]
            small_ref[SUBLANES:SW_ROWS, j * pq:(j + 1) * pq] = pslot[j, SUBLANES:SW_ROWS, :]

    args = (c8, mod_w, mod_b, conv_w, sc_w, pool_b, pool_s)
    dma3 = pltpu.SemaphoreType.DMA((N_CHIP - 1,))
    return pl.pallas_call(
        body, name="mod_fwd",
        in_specs=[VMEM] * len(args), out_specs=[VMEM] * 3,
        out_shape=[jax.ShapeDtypeStruct((SUBLANES, D), F32), jax.ShapeDtypeStruct((2, SUBLANES, N_CHIP * nw), F32),
                   jax.ShapeDtypeStruct((SW_ROWS, SW_COLS), F32)],
        scratch_shapes=[pltpu.VMEM((N_DEV, SUBLANES, D), F32), pltpu.VMEM((N_CHIP, 2, SUBLANES, nw), F32),
                        pltpu.VMEM((2, SUBLANES, nw), F32), pltpu.VMEM((N_CHIP, SW_ROWS, pq), F32),
                        pltpu.VMEM((SW_ROWS, pq), F32),
                        pltpu.SemaphoreType.DMA((N_DEV - 1,)), pltpu.SemaphoreType.DMA((N_DEV - 1,)),
                        dma3, dma3, dma3, dma3],
        compiler_params=_cp(),
    )(*args)


def _wcast(ws):
    def body(*refs):
        n = len(refs) // 2
        for a in range(n):
            refs[n + a][...] = refs[a][...].astype(BF16)

    return pl.pallas_call(
        body, name="wcast", in_specs=[VMEM] * len(ws), out_specs=[VMEM] * len(ws),
        out_shape=[jax.ShapeDtypeStruct(w.shape, BF16) for w in ws], compiler_params=_cp(),
    )(*ws)


def _wcast_own_block(w, kidx, name, after=()):
    rr, cc = w.shape
    rb = min(rr, 256)

    def body(k_ref, w_ref, *rest):
        rest[-1][...] = w_ref[...].astype(BF16)

    order = list(after)
    return pl.pallas_call(
        body, name=name,
        grid_spec=pltpu.PrefetchScalarGridSpec(
            num_scalar_prefetch=1, grid=(rr // rb,),
            in_specs=[pl.BlockSpec((rb, cc), lambda j, k_ref: (j, 0))] + [ANY] * len(order),
            out_specs=pl.BlockSpec((None, rb, cc), lambda j, k_ref: (k_ref[0], j, 0))),
        out_shape=jax.ShapeDtypeStruct((N_CHIP, rr, cc), BF16),
        compiler_params=_cp(("parallel",)),
    )(kidx, w, *order)


def _wgather_copies(outs, rows, ssem, rsem, fssem, frsem):
    n = len(outs)
    x, y, c = _pos()
    chip = 2 * x + y
    sib = (x, y, 1 - c)
    flips = ((1, 0), (0, 1), (1, 1))

    def half(a, which):
        hr = rows[a] // 2
        return pl.ds(pl.multiple_of(which * hr, BF16_ROWS), hr)

    sends = []
    for a in range(n):
        mine = outs[a].at[chip, half(a, c), :]
        for q, (fx, fy) in enumerate(flips):
            cp = pltpu.make_async_remote_copy(
                src_ref=mine, dst_ref=mine, send_sem=ssem.at[3 * a + q], recv_sem=rsem.at[3 * a + q],
                device_id=(_flip(x, fx), _flip(y, fy), c), device_id_type=MESH)
            cp.start()
            sends.append(cp)
    passed = []
    for a in range(n):
        for q, (fx, fy) in enumerate(flips):
            src_chip = 2 * _flip(x, fx) + _flip(y, fy)
            landed = outs[a].at[src_chip, half(a, c), :]
            pltpu.make_async_remote_copy(
                src_ref=landed, dst_ref=landed, send_sem=ssem.at[3 * a + q], recv_sem=rsem.at[3 * a + q],
                device_id=sib, device_id_type=MESH).wait_recv()
            cp = pltpu.make_async_remote_copy(
                src_ref=landed, dst_ref=landed, send_sem=fssem.at[3 * a + q], recv_sem=frsem.at[3 * a + q],
                device_id=sib, device_id_type=MESH)
            cp.start()
            passed.append(cp)
    for a in range(n):
        for q, (fx, fy) in enumerate(flips):
            src_chip = 2 * _flip(x, fx) + _flip(y, fy)
            other = outs[a].at[src_chip, half(a, 1 - c), :]
            pltpu.make_async_remote_copy(
                src_ref=other, dst_ref=other, send_sem=fssem.at[3 * a + q], recv_sem=frsem.at[3 * a + q],
                device_id=sib, device_id_type=MESH).wait_recv()
    for cp in sends + passed:
        cp.wait_send()


def _wgather_sequencer(bufs, name, collective_id):
    n = len(bufs)
    refs = [jax.new_ref(b, memory_space=pltpu.MemorySpace.HBM) for b in bufs]
    dma = pltpu.SemaphoreType.DMA((3 * n,))

    @pl.kernel(mesh=plsc.ScalarSubcoreMesh(axis_name="sequencer", num_cores=1), name=name,
               scratch_types=(dma, dma, dma, dma), compiler_params=pltpu.CompilerParams(collective_id=collective_id))
    def launch(ssem, rsem, fssem, frsem):
        x, y, c = _pos()
        barrier = pltpu.get_barrier_semaphore()
        for peer in ((1 - x, y, c), (x, 1 - y, c), (1 - x, 1 - y, c), (x, y, 1 - c)):
            pl.semaphore_signal(barrier, inc=1, device_id=peer, device_id_type=MESH)
        pl.semaphore_wait(barrier, 4)
        _wgather_copies(refs, [b.shape[1] for b in bufs], ssem, rsem, fssem, frsem)

    launch()
    return [r[...] for r in refs]


def _norm_mod(x, g, sc, sh, name):
    s_len = x.shape[0]
    ts = _tile(s_len, TS_WGRAD)

    def body(x_ref, g_ref, sc_ref, sh_ref, h_ref):
        xv = x_ref[...]
        rinv = lax.rsqrt(jnp.mean(xv * xv, axis=-1, keepdims=True) + RMS_EPS)
        h_ref[...] = (xv * rinv * (g_ref[...] * (1.0 + sc_ref[...])) + sh_ref[...]).astype(BF16)

    row = pl.BlockSpec((ts, D), lambda i: (i, 0))
    vec = pl.BlockSpec((1, D), lambda i: (0, 0))
    return pl.pallas_call(
        body, name=name, grid=(s_len // ts,), in_specs=[row, vec, vec, vec], out_specs=row,
        out_shape=jax.ShapeDtypeStruct((s_len, D), BF16), compiler_params=_cp(("parallel",)),
    )(x, g, sc, sh)


def _l0_fwd(h0_all, x, w_in, gate, cw, cb, wa, ba, wx, bx, lam, sw, wo):
    s_len, nb = x.shape[0], w_in.shape[2]
    ts = _tile(s_len, TS_MIX)
    n_t = s_len // ts
    hl = SUBLANES

    def body(h0_ref, xb_ref, win_ref, gate_ref, cw_ref, cb_ref, wa_ref, ba_ref, wx_ref, bx_ref,
             lam_ref, sw_ref, wo_ref, x1_ref, h_ref, y_ref, xc_ref, cz_ref, r_ref, ig_ref, p_ref,
             pcur, pnext, cxa, czz, chh):
        i = pl.program_id(0)

        @pl.when(i == 0)
        def _():
            cxa[...] = jnp.zeros_like(cxa)
            czz[...] = jnp.zeros_like(czz)
            chh[...] = jnp.zeros_like(chh)
            pnext[...] = jnp.zeros_like(pnext)

        pcur[...] = pnext[...]
        h0 = h0_ref[...]

        def project(k, c0, cn):
            def emit():
                pk = jnp.dot(h0, win_ref[k, :, c0:c0 + cn], preferred_element_type=F32).astype(BF16)
                p_ref[:, k * nb + c0:k * nb + c0 + cn] = pk
                pnext[:, k * nb + c0:k * nb + c0 + cn] = pk
            return emit

        def mixer():
            piece = lambda k: pcur[:, k * D:(k + 1) * D].astype(F32)
            xa = piece(0)
            rows = _rows(ts, D)
            taps = _conv_taps(jnp.concatenate([cxa[...], xa], axis=0), hl, ts, 4)
            xc = cb_ref[...] + sum(cw_ref[k:k + 1, :] * taps[k] for k in range(4))
            xc_ref[...] = xc.astype(BF16)
            r, ig = _lru_gates(xc, wa_ref, ba_ref[...], wx_ref, bx_ref[...])
            r_ref[...] = r.astype(BF16)
            ig_ref[...] = ig.astype(BF16)
            a, m, _ = _lru_decay(r, _softplus_neg(lam_ref[...]), (rows == 0) & (i == 1))
            yield 0.26
            h = _run(_scan_fwd_steps(a, m * ig * xc, chh[hl - 1:hl, :]))
            yield 0.51
            gcp, v = piece(3), piece(4)
            z = gcp * v
            ztaps = _conv_taps(jnp.concatenate([czz[...], z], axis=0), hl, ts, 3)
            cz = sum(sw_ref[k:k + 1, :] * ztaps[k] for k in range(3))
            cz_ref[...] = cz.astype(BF16)
            yb = piece(2) * cz
            ga, gb = piece(1), piece(5)
            y = jnp.concatenate([h * (ga * _sigmoid(ga)), yb * (gb * _sigmoid(gb))], axis=1).astype(BF16)
            yield 0.76
            y_ref[...] = y
            x1_ref[...] = xb_ref[...] + gate_ref[...] * jnp.dot(y, wo_ref[...], preferred_element_type=F32)
            h_ref[...] = h.astype(BF16)
            cxa[...] = xa[ts - hl:, :]
            czz[...] = z[ts - hl:, :]
            chh[...] = jnp.where(i > 0, h[ts - hl:, :], 0.0)

        _paired(mixer(), [project(k, 0, nb) for k in range(N_CHIP)])

    def full(a):
        return pl.BlockSpec(a.shape, lambda i: (0,) * a.ndim)

    ahead = lambda w: pl.BlockSpec((ts, w), lambda i: (jnp.minimum(i, n_t - 1), 0))
    behind = lambda w: pl.BlockSpec((ts, w), lambda i: (jnp.maximum(i - 1, 0), 0))
    args = (h0_all, x, w_in, gate, cw, cb, wa, ba, wx, bx, lam, sw, wo)
    return pl.pallas_call(
        body, name="l0_fwd", grid=(n_t + 1,),
        in_specs=[ahead(D), behind(D)] + [full(a) for a in args[2:]],
        out_specs=[behind(D), behind(D), behind(2 * D)] + [behind(D)] * 4 + [ahead(N_CHIP * nb)],
        out_shape=[jax.ShapeDtypeStruct((s_len, D), F32), jax.ShapeDtypeStruct((s_len, D), BF16),
                   jax.ShapeDtypeStruct((s_len, 2 * D), BF16)] + [jax.ShapeDtypeStruct((s_len, D), BF16)] * 4
        + [jax.ShapeDtypeStruct((s_len, N_CHIP * nb), BF16)],
        scratch_shapes=[pltpu.VMEM((ts, N_CHIP * nb), BF16)] * 2 + [pltpu.VMEM((hl, D), F32)] * 3,
        compiler_params=_cp(("arbitrary",)),
    )(*args)


def _l1_fwd(x1, g, sc, sh, w_in, tgt, gate, wg, bg, scale, wo, gf):
    s_len, nb = x1.shape[0], w_in.shape[2]
    ts = _tile(s_len, TS_MIX)
    n_t = s_len // ts
    pw, gd, hl = 2 * D, POOL_GROUP_DIM, POOL_HALO

    def body(xa_ref, xb_ref, t_ref, g_ref, sc_ref, sh_ref, win_ref, gate_ref, wg_ref, bg_ref, scl_ref, wo_ref, gf_ref,
             d_ref, mx_ref, y_ref, dx_ref, loss_ref, dgf_ref, h1_ref, p_ref, pcur, pnext, cv):
        i = pl.program_id(0)

        @pl.when(i == 0)
        def _():
            cv[...] = jnp.zeros_like(cv)
            loss_ref[...] = jnp.zeros_like(loss_ref)
            dgf_ref[...] = jnp.zeros_like(dgf_ref)
            pnext[...] = jnp.zeros_like(pnext)

        pcur[...] = pnext[...]
        xv = xa_ref[...]
        rinv = lax.rsqrt(jnp.mean(xv * xv, axis=-1, keepdims=True) + RMS_EPS)
        h1 = (xv * rinv * (g_ref[...] * (1.0 + sc_ref[...])) + sh_ref[...]).astype(BF16)
        h1_ref[...] = h1

        def project(k):
            def emit():
                pk = jnp.dot(h1, win_ref[k], preferred_element_type=F32).astype(BF16)
                p_ref[:, k * nb:(k + 1) * nb] = pk
                pnext[:, k * nb:(k + 1) * nb] = pk
            return emit

        def mixer():
            v = pcur[:, 0:pw].astype(F32)
            sums = _window_sums(jnp.concatenate([cv[...], v], axis=0), _down)
            inv = _pool_inv_counts(jnp.maximum(i - 1, 0) * ts, ts)
            dd = [sums[k][hl:hl + ts] * inv[k] - v[:, k * gd:(k + 1) * gd] for k in range(4)]
            d_ref[...] = jnp.concatenate(dd, axis=1).astype(BF16)
            yield 0.26
            mixed = jnp.concatenate(
                [jnp.dot(dd[k].astype(BF16), wg_ref[k], preferred_element_type=F32) for k in range(4)], axis=1) + bg_ref[...]
            mx_ref[...] = mixed.astype(BF16)
            gg = pcur[:, pw:2 * pw].astype(F32)
            y = (mixed * scl_ref[...] * (gg * _sigmoid(gg))).astype(BF16)
            y_ref[...] = y
            yield 0.51
            x2 = xb_ref[...] + gate_ref[...] * jnp.dot(y, wo_ref[...], preferred_element_type=F32)
            yield 0.76
            r2 = lax.rsqrt(jnp.mean(x2 * x2, axis=-1, keepdims=True) + RMS_EPS)
            n2 = x2 * r2
            err = n2 * gf_ref[...] - t_ref[...]
            loss_ref[...] += jnp.where(i > 0, jnp.sum(err * err, axis=0, keepdims=True), 0.0)
            dyf = err * (1.0 / D)
            dgf_ref[...] += jnp.where(i > 0, jnp.sum(dyf * n2, axis=0, keepdims=True), 0.0)
            dn = dyf * gf_ref[...]
            dx_ref[...] = r2 * (dn - n2 * jnp.mean(dn * n2, axis=-1, keepdims=True))
            cv[...] = v[ts - hl:, :]

        _paired(mixer(), [project(k) for k in range(N_CHIP)])

    def full(a):
        return pl.BlockSpec(a.shape, lambda i: (0,) * a.ndim)

    ahead = lambda w: pl.BlockSpec((ts, w), lambda i: (jnp.minimum(i, n_t - 1), 0))
    behind = lambda w: pl.BlockSpec((ts, w), lambda i: (jnp.maximum(i - 1, 0), 0))
    acc = pl.BlockSpec((1, D), lambda i: (0, 0))
    args = (x1, x1, tgt, g, sc, sh, w_in, gate, wg, bg, scale, wo, gf)
    return pl.pallas_call(
        body, name="l1_fwd", grid=(n_t + 1,),
        in_specs=[ahead(D), behind(D), behind(D)] + [full(a) for a in args[3:]],
        out_specs=[behind(pw), behind(pw), behind(pw), behind(D), acc, acc, ahead(D), ahead(N_CHIP * nb)],
        out_shape=[jax.ShapeDtypeStruct((s_len, pw), BF16)] * 3 + [jax.ShapeDtypeStruct((s_len, D), F32)]
        + [jax.ShapeDtypeStruct((1, D), F32)] * 2
        + [jax.ShapeDtypeStruct((s_len, D), BF16), jax.ShapeDtypeStruct((s_len, N_CHIP * nb), BF16)],
        scratch_shapes=[pltpu.VMEM((ts, N_CHIP * nb), BF16)] * 2 + [pltpu.VMEM((hl, pw), F32)],
        compiler_params=_cp(("arbitrary",)),
    )(*args)


def _l1_bwd_mix(dx2, proj, mixed, y, dpool, gate, wg, scale, wo):
    s_len = dx2.shape[0]
    n_sub = 2
    ts = _tile(s_len, n_sub * TS_MIX)
    sub = ts // n_sub
    n_t = s_len // ts
    pw, gd, hl = 2 * D, POOL_GROUP_DIM, POOL_HALO

    def body(dx_ref, gg_ref, mx_ref, y_ref, d_ref, gate_ref, wg_ref, sc_ref, wo_ref,
             dp_ref, mt_ref, dwg_ref, dsc_ref, dbg_ref, cq):
        i = pl.program_id(0)

        @pl.when(i == 0)
        def _():
            cq[...] = jnp.zeros_like(cq)
            dsc_ref[...] = jnp.zeros_like(dsc_ref)
            dbg_ref[...] = jnp.zeros_like(dbg_ref)
            mt_ref[...] = jnp.zeros_like(mt_ref)
            dwg_ref[...] = jnp.zeros_like(dwg_ref)

        ahead_rows = {}

        def chain(j):
            rows = slice(j * sub, (j + 1) * sub)
            dxv = dx_ref[rows, :]
            dxb = dxv.astype(BF16)
            dy = lax.dot_general((gate_ref[...] * dxv).astype(BF16), wo_ref[...], NT, preferred_element_type=F32)
            for k in range(2):
                mt_ref[k] += lax.dot_general(y_ref[rows, k * gd:(k + 1) * gd], dxb, TN, preferred_element_type=F32)
            yield
            gg = gg_ref[rows, :].astype(F32)
            mixed = mx_ref[rows, :].astype(F32)
            s = _sigmoid(gg)
            sg = gg * s
            dym = dy * mixed
            dmixed = dy * sc_ref[...] * sg
            dsc_ref[...] += jnp.sum(dym * sg, axis=0, keepdims=True)
            dbg_ref[...] += jnp.sum(dmixed, axis=0, keepdims=True)
            dmb = dmixed.astype(BF16)
            dp_ref[rows, pw:2 * pw] = (dym * sc_ref[...] * (s + sg * (1.0 - s))).astype(BF16)
            yield
            inv = _pool_inv_counts((n_t - 1 - i) * ts + j * sub, sub)
            dd = []
            for k in range(4):
                dmk = dmb[:, k * gd:(k + 1) * gd]
                dd.append(lax.dot_general(dmk, wg_ref[k], NT, preferred_element_type=F32))
                dwg_ref[k] += lax.dot_general(d_ref[rows, k * gd:(k + 1) * gd], dmk, TN, preferred_element_type=F32)
            for k in range(2, 4):
                mt_ref[k] += lax.dot_general(y_ref[rows, k * gd:(k + 1) * gd], dxb, TN, preferred_element_type=F32)
            q = jnp.concatenate([dd[k] * inv[k] for k in range(4)], axis=1)
            ahead_rows[j] = q[0:hl, :]
            yield
            behind_q = cq[...] if j == n_sub - 1 else ahead_rows[j + 1]
            sums = _window_sums(jnp.concatenate([q, behind_q], axis=0), _up)
            dp_ref[rows, 0:pw] = jnp.concatenate([sums[k][0:sub] - dd[k] for k in range(4)], axis=1).astype(BF16)

        chains = [chain(j) for j in reversed(range(n_sub))]
        for _ in range(4):
            for ch in chains:
                next(ch, None)
        cq[...] = ahead_rows[0]

    def full(a):
        return pl.BlockSpec(a.shape, lambda i: (0,) * a.ndim)

    rev = lambda w, j=0: pl.BlockSpec((ts, w), lambda i: (n_t - 1 - i, j))
    acc = pl.BlockSpec((1, pw), lambda i: (0, 0))
    return pl.pallas_call(
        body, name="l1_bwd_mix", grid=(n_t,),
        in_specs=[rev(D), rev(pw, 1), rev(pw), rev(pw), rev(pw)] + [full(a) for a in (gate, wg, scale, wo)],
        out_specs=[rev(2 * pw), pl.BlockSpec((N_CHIP, gd, D), lambda i: (0, 0, 0)),
                   pl.BlockSpec((4, gd, gd), lambda i: (0, 0, 0)), acc, acc],
        out_shape=[jax.ShapeDtypeStruct((s_len, 2 * pw), BF16), jax.ShapeDtypeStruct((N_CHIP, gd, D), F32),
                   jax.ShapeDtypeStruct((4, gd, gd), F32),
                   jax.ShapeDtypeStruct((1, pw), F32), jax.ShapeDtypeStruct((1, pw), F32)],
        scratch_shapes=[pltpu.VMEM((hl, pw), F32)],
        compiler_params=_cp(("arbitrary",)),
    )(dx2, proj, mixed, y, dpool, gate, wg, scale, wo)


def _l0_bwd_mix(dx1, proj, hst, y, xc, cz, rg, ig_, gate, cw, wa, wx, lam, sw, wo):
    s_len = dx1.shape[0]
    ts = _tile(s_len, TS_MIX)
    n_t = s_len // ts
    hl, hb = SUBLANES, BF16_ROWS
    yb_w = 2 * D // N_CHIP

    def body(dx_ref, p_ref, h_ref, hh_ref, y_ref, xc_ref, cz_ref, r_ref, ig_ref, gate_ref, cw_ref, wa_ref, wx_ref,
             lam_ref, sw_ref, wo_ref, dp_ref, mt_ref, dwa_ref, dwx_ref, sm_ref, cg, cdxc, cdcz, ca):
        i = pl.program_id(0)
        ri = n_t - 1 - i

        @pl.when(i == 0)
        def _():
            cg[...] = jnp.zeros_like(cg)
            ca[...] = jnp.zeros_like(ca)
            cdxc[...] = jnp.zeros_like(cdxc)
            cdcz[...] = jnp.zeros_like(cdcz)
            sm_ref[...] = jnp.zeros_like(sm_ref)
            mt_ref[...] = jnp.zeros_like(mt_ref)
            dwa_ref[...] = jnp.zeros_like(dwa_ref)
            dwx_ref[...] = jnp.zeros_like(dwx_ref)

        dxb = dx_ref[...].astype(BF16)

        def wgrad_out(k):
            mt_ref[k] += lax.dot_general(y_ref[:, k * yb_w:(k + 1) * yb_w], dxb, TN, preferred_element_type=F32)

        wgrad_out(0)
        has_prev = (ri > 0).astype(F32)
        xa, ga, gbp, gcp, v, gb = [p_ref[:, k * D:(k + 1) * D].astype(F32) for k in range(6)]
        rows = _rows(ts, D)
        first = (rows == 0) & (ri == 0)
        xc = xc_ref[...].astype(F32)
        cz = cz_ref[...].astype(F32)
        r = r_ref[...].astype(F32)
        ig = ig_ref[...].astype(F32)
        sp = _softplus_neg(lam_ref[...])
        a, m, inv_m = _lru_decay(r, sp, first)
        z = gcp * v
        h = h_ref[...].astype(F32)
        hprev = _down(jnp.concatenate([hh_ref[...].astype(F32)[hb - hl:hb] * has_prev, h], axis=0), 1)[hl:hl + ts]
        dy = lax.dot_general((gate_ref[...] * dx_ref[...]).astype(BF16), wo_ref[...], NT, preferred_element_type=F32)
        dya_pre, dyb_pre = dy[:, 0:D], dy[:, D:2 * D]
        s_a, s_b = _sigmoid(ga), _sigmoid(gb)
        silu_a, silu_b = ga * s_a, gb * s_b
        dp_ref[:, D:2 * D] = (dya_pre * h * (s_a + silu_a * (1.0 - s_a))).astype(BF16)
        dp_ref[:, 5 * D:6 * D] = (dyb_pre * (gbp * cz) * (s_b + silu_b * (1.0 - s_b))).astype(BF16)
        dya = dya_pre * silu_a
        dyb = dyb_pre * silu_b
        wgrad_out(1)
        dp_ref[:, 2 * D:3 * D] = (dyb * cz).astype(BF16)
        dcz = dyb * gbp
        dcz_ext = jnp.concatenate([dcz, cdcz[...]], axis=0)
        dcz_taps = [_up(dcz_ext, 2 - k)[0:ts] for k in range(3)]
        for k in range(3):
            sm_ref[8 + k:9 + k, :] += jnp.sum(z * dcz_taps[k], axis=0, keepdims=True)
        dz = sum(sw_ref[k:k + 1, :] * dcz_taps[k] for k in range(3))
        dp_ref[:, 3 * D:4 * D] = (dz * v).astype(BF16)
        dp_ref[:, 4 * D:5 * D] = (dz * gcp).astype(BF16)
        cdcz[...] = dcz[0:hl, :]
        alpha = _up(jnp.concatenate([a, ca[...]], axis=0), 1)[0:ts]
        wgrad_out(2)
        dh = _run(_scan_rev_steps(alpha, dya, cg[0:1, :]))
        wgrad_out(3)
        cg[...] = dh[0:hl, :]
        ca[...] = a[0:hl, :]
        da = dh * hprev
        dhx = dh * xc
        dm = dhx * ig
        di = dhx * m
        dxc = dh * (m * ig)
        dl = a * (da - jnp.where(first, 0.0, dm * a * inv_m))
        dlr = dl * r
        sm_ref[7:8, :] += jnp.sum(dlr, axis=0, keepdims=True) * (-LRU_C)
        dpa = dlr * (sp * (-LRU_C)) * (1.0 - r)
        dpx = di * ig * (1.0 - ig)
        sm_ref[5:6, :] += jnp.sum(dpa, axis=0, keepdims=True)
        sm_ref[6:7, :] += jnp.sum(dpx, axis=0, keepdims=True)
        dpa_b, dpx_b, xc_b = dpa.astype(BF16), dpx.astype(BF16), xc.astype(BF16)
        back = []
        for hd in range(LRU_HEADS):
            sl = slice(hd * LRU_HEAD_DIM, (hd + 1) * LRU_HEAD_DIM)
            back.append(lax.dot_general(dpa_b[:, sl], wa_ref[hd], NT, preferred_element_type=F32)
                        + lax.dot_general(dpx_b[:, sl], wx_ref[hd], NT, preferred_element_type=F32))
            dwa_ref[hd] += lax.dot_general(xc_b[:, sl], dpa_b[:, sl], TN, preferred_element_type=F32)
            dwx_ref[hd] += lax.dot_general(xc_b[:, sl], dpx_b[:, sl], TN, preferred_element_type=F32)
        dxc = dxc + jnp.concatenate(back, axis=1)
        sm_ref[4:5, :] += jnp.sum(dxc, axis=0, keepdims=True)
        dxc_ext = jnp.concatenate([dxc, cdxc[...]], axis=0)
        dxc_taps = [_up(dxc_ext, 3 - k)[0:ts] for k in range(4)]
        for k in range(4):
            sm_ref[k:k + 1, :] += jnp.sum(xa * dxc_taps[k], axis=0, keepdims=True)
        dp_ref[:, 0:D] = sum(cw_ref[k:k + 1, :] * dxc_taps[k] for k in range(4)).astype(BF16)
        cdxc[...] = dxc[0:hl, :]

    def full(a):
        return pl.BlockSpec(a.shape, lambda i: (0,) * a.ndim)

    rev = lambda w: pl.BlockSpec((ts, w), lambda i: (n_t - 1 - i, 0))
    halo = lambda w: pl.BlockSpec((hb, w), lambda i: (jnp.maximum((n_t - 1 - i) * (ts // hb) - 1, 0), 0))
    return pl.pallas_call(
        body, name="l0_bwd_mix", grid=(n_t,),
        in_specs=[rev(D), rev(6 * D), rev(D), halo(D), rev(2 * D), rev(D), rev(D), rev(D), rev(D)]
        + [full(a) for a in (gate, cw, wa, wx, lam, sw, wo)],
        out_specs=[rev(6 * D), pl.BlockSpec((N_CHIP, yb_w, D), lambda i: (0, 0, 0)),
                   pl.BlockSpec(wa.shape, lambda i: (0, 0, 0)), pl.BlockSpec(wa.shape, lambda i: (0, 0, 0)),
                   pl.BlockSpec((2 * SUBLANES, D), lambda i: (0, 0))],
        out_shape=[jax.ShapeDtypeStruct((s_len, 6 * D), BF16), jax.ShapeDtypeStruct((N_CHIP, yb_w, D), F32),
                   jax.ShapeDtypeStruct(wa.shape, F32), jax.ShapeDtypeStruct(wa.shape, F32),
                   jax.ShapeDtypeStruct((2 * SUBLANES, D), F32)],
        scratch_shapes=[pltpu.VMEM((hl, D), F32)] * 4,
        compiler_params=_cp(("arbitrary",)),
    )(dx1, proj, hst, hst, y, xc, cz, rg, ig_, gate, cw, wa, wx, lam, sw, wo)


def _dgrad_norm(dproj, w, x, dres, g, sc, name, after=None):
    s_len, nb = x.shape[0], w.shape[2]
    ts = _tile(s_len, TS_DGRAD)
    order = [] if after is None else [after]

    def body(dp_ref, w_ref, x_ref, dr_ref, g_ref, sc_ref, *rest):
        dx_ref, s1_ref, s2_ref = rest[len(order):]

        @pl.when(pl.program_id(0) == 0)
        def _():
            s1_ref[...] = jnp.zeros_like(s1_ref)
            s2_ref[...] = jnp.zeros_like(s2_ref)

        dh = sum(lax.dot_general(dp_ref[:, k * nb:(k + 1) * nb], w_ref[k], NT, preferred_element_type=F32)
                 for k in range(N_CHIP))
        xv = x_ref[...]
        r = lax.rsqrt(jnp.mean(xv * xv, axis=-1, keepdims=True) + RMS_EPS)
        n = xv * r
        s1_ref[...] += jnp.sum(dh, axis=0, keepdims=True)
        s2_ref[...] += jnp.sum(dh * n, axis=0, keepdims=True)
        dn = dh * (g_ref[...] * (1.0 + sc_ref[...]))
        dx_ref[...] = dr_ref[...] + r * (dn - n * jnp.mean(dn * n, axis=-1, keepdims=True))

    row = lambda wd: pl.BlockSpec((ts, wd), lambda i: (i, 0))
    vec = pl.BlockSpec((1, D), lambda i: (0, 0))
    return pl.pallas_call(
        body, name=name, grid=(s_len // ts,),
        in_specs=[row(N_CHIP * nb), pl.BlockSpec(w.shape, lambda i: (0, 0, 0)), row(D), row(D), vec, vec]
        + [ANY] * len(order),
        out_specs=[row(D), vec, vec],
        out_shape=[jax.ShapeDtypeStruct((s_len, D), F32)] + [jax.ShapeDtypeStruct((1, D), F32)] * 2,
        compiler_params=_cp(("arbitrary",)),
    )(dproj, w, x, dres, g, sc, *order)


def _wgrad(a, b, groups, ka, nb, a_col, b_col, name, after=None):
    s_len = a.shape[0]
    ts = _tile(s_len, TS_WGRAD)
    n_s = s_len // ts
    order = [] if after is None else [after]

    def body(a_ref, b_ref, *rest):
        o_ref, wire_ref = rest[-2:]

        @pl.when(pl.program_id(1) == 0)
        def _():
            o_ref[...] = jnp.zeros_like(o_ref)

        o_ref[...] += lax.dot_general(a_ref[...].astype(BF16), b_ref[...].astype(BF16), TN, preferred_element_type=F32)

        @pl.when(pl.program_id(1) == n_s - 1)
        def _():
            wire_ref[...] = o_ref[...].astype(GRAD_WIRE_DTYPE)

    blk = pl.BlockSpec((None, ka, nb), lambda g, s: (g, 0, 0))
    return pl.pallas_call(
        body, name=name, grid=(groups, n_s),
        in_specs=[pl.BlockSpec((ts, ka), lambda g, s: (s, a_col(g))), pl.BlockSpec((ts, nb), lambda g, s: (s, b_col(g)))]
        + [ANY] * len(order),
        out_specs=[blk, blk],
        out_shape=[jax.ShapeDtypeStruct((groups, ka, nb), F32), jax.ShapeDtypeStruct((groups, ka, nb), GRAD_WIRE_DTYPE)],
        compiler_params=_cp(("parallel", "arbitrary")),
    )(a, b, *order)


def _wo_final(mt, wo, gate, name):
    rb = mt.shape[1]

    def body(m_ref, w_ref, gate_ref, dw_ref, wire_ref, dg_ref):
        @pl.when(pl.program_id(0) == 0)
        def _():
            dg_ref[...] = jnp.zeros_like(dg_ref)

        mv = m_ref[...]
        dw = mv * gate_ref[...]
        dw_ref[...] = dw
        wire_ref[...] = dw.astype(GRAD_WIRE_DTYPE)
        dg_ref[...] += jnp.sum(mv * w_ref[...].astype(F32), axis=0, keepdims=True)

    blk = pl.BlockSpec((None, rb, D), lambda k: (k, 0, 0))
    vec = pl.BlockSpec((1, D), lambda k: (0, 0))
    return pl.pallas_call(
        body, name=name, grid=(N_CHIP,), in_specs=[blk, blk, vec], out_specs=[blk, blk, vec],
        out_shape=[jax.ShapeDtypeStruct(mt.shape, F32), jax.ShapeDtypeStruct(mt.shape, GRAD_WIRE_DTYPE),
                   jax.ShapeDtypeStruct((1, D), F32)],
        compiler_params=_cp(("arbitrary",)),
    )(mt, wo, gate)


ROW_NORM_G, ROW_CONV_W, ROW_CONV_B, ROW_B_A, ROW_B_X, ROW_LAMBDA, ROW_SC_W, ROW_POOL_B, ROW_POOL_S, ROW_FINAL_G = (
    0, 2, 6, 7, 8, 9, 10, 13, 15, 17)
ROW_LOSS = 18
DMOD_W = 6 * D // SUBLANES


def _small_pack(s1_0, s2_0, s1_1, s2_1, sm0, dsc1, dbg1, dgf, losscols, dgate0, dgate1, norm_g, sc0, sc1, lam):
    def body(s1_0r, s2_0r, s1_1r, s2_1r, sm, dsc, dbg, dgfr, lcols, dg0, dg1, ng, sc0r, sc1r, lamr, buf, dmod):
        buf[...] = jnp.zeros_like(buf)
        buf[0:1, :] = s2_0r[...] * (1.0 + sc0r[...])
        buf[1:2, :] = s2_1r[...] * (1.0 + sc1r[...])
        buf[ROW_CONV_W:ROW_CONV_W + 4, :] = sm[0:4, :]
        buf[ROW_CONV_B:ROW_CONV_B + 1, :] = sm[4:5, :]
        buf[ROW_B_A:ROW_B_A + 1, :] = sm[5:6, :]
        buf[ROW_B_X:ROW_B_X + 1, :] = sm[6:7, :]
        buf[ROW_LAMBDA:ROW_LAMBDA + 1, :] = -sm[7:8, :] * _sigmoid(-lamr[...])
        buf[ROW_SC_W:ROW_SC_W + 3, :] = sm[8:11, :]
        for k in range(2):
            buf[ROW_POOL_B + k:ROW_POOL_B + k + 1, :] = dbg[:, k * D:(k + 1) * D]
            buf[ROW_POOL_S + k:ROW_POOL_S + k + 1, :] = dsc[:, k * D:(k + 1) * D]
        buf[ROW_FINAL_G:ROW_FINAL_G + 1, :] = dgfr[...]
        pieces = (s1_0r[...], s2_0r[...] * ng[0:1, :], dg0[...], s1_1r[...], s2_1r[...] * ng[1:2, :], dg1[...])
        flat = jnp.concatenate(pieces, axis=1)
        for r in range(SUBLANES):
            dmod[r:r + 1, :] = flat[:, r * DMOD_W:(r + 1) * DMOD_W]
        buf[ROW_LOSS:ROW_LOSS + 1, :] = jnp.broadcast_to(jnp.sum(lcols[...], axis=1, keepdims=True) * (0.5 / D), (1, D))

    args = (s1_0, s2_0, s1_1, s2_1, sm0, dsc1, dbg1, dgf, losscols, dgate0, dgate1, norm_g, sc0, sc1, lam)
    return pl.pallas_call(
        body, name="small_pack", in_specs=[VMEM] * len(args), out_specs=[VMEM] * 2,
        out_shape=[jax.ShapeDtypeStruct((SMALL_ROWS, D), F32), jax.ShapeDtypeStruct((SUBLANES, DMOD_W), F32)],
        compiler_params=_cp(),
    )(*args)


def _small_comm(buf_a, buf_b, dmod8):
    ra, rb = buf_a.shape[0] // N_DEV, buf_b.shape[0] // N_DEV
    wb = buf_b.shape[1]

    def body(a_ref, b_ref, dm_ref, oa_ref, ob_ref, odm_ref, ina, inb, dslot, sa, sb, s1, r1, s2, r2):
        x, y, c = _pos()
        me = 4 * x + 2 * y + c
        peers = []
        for r in range(1, N_DEV):
            fx, fy, fc = (r >> 2) & 1, (r >> 1) & 1, r & 1
            px, py, pc = _flip(x, fx), _flip(y, fy), _flip(c, fc)
            peers.append(((px, py, pc), 4 * px + 2 * py + pc))
        seg_a = lambda d: pl.ds(pl.multiple_of(d * ra, SUBLANES), ra)
        seg_b = lambda d: pl.ds(pl.multiple_of(d * rb, SUBLANES), rb)
        first = []
        for r, (peer, pid) in enumerate(peers):
            for k, (src, dst) in enumerate(((a_ref.at[seg_a(pid), :], ina.at[r]), (b_ref.at[seg_b(pid), :], inb.at[r]),
                                            (dm_ref, dslot.at[me]))):
                cp = pltpu.make_async_remote_copy(src_ref=src, dst_ref=dst, send_sem=s1.at[3 * r + k],
                                                  recv_sem=r1.at[3 * r + k], device_id=peer, device_id_type=MESH)
                cp.start()
                first.append(cp)
        dslot[me] = dm_ref[...]
        for cp in first:
            cp.wait()
        acc_a, acc_b = a_ref[seg_a(me), :], b_ref[seg_b(me), :]
        for r in range(N_DEV - 1):
            acc_a = acc_a + ina[r]
            acc_b = acc_b + inb[r]
        sa[...] = acc_a
        sb[...] = acc_b
        oa_ref[seg_a(me), :] = acc_a
        ob_ref[seg_b(me), :] = acc_b
        second = []
        for r, (peer, pid) in enumerate(peers):
            for k, (src, dst) in enumerate(((sa, oa_ref.at[seg_a(me), :]), (sb, ob_ref.at[seg_b(me), :]))):
                cp = pltpu.make_async_remote_copy(src_ref=src, dst_ref=dst, send_sem=s2.at[2 * r + k],
                                                  recv_sem=r2.at[2 * r + k], device_id=peer, device_id_type=MESH)
                cp.start()
                second.append(cp)
        odm_ref[...] = dslot[...]
        for cp in second:
            cp.wait()

    nrel = N_DEV - 1
    return pl.pallas_call(
        body, name="small_comm", in_specs=[VMEM] * 3, out_specs=[VMEM] * 3,
        out_shape=[jax.ShapeDtypeStruct(buf_a.shape, F32), jax.ShapeDtypeStruct(buf_b.shape, F32),
                   jax.ShapeDtypeStruct((N_DEV,) + dmod8.shape, F32)],
        scratch_shapes=[pltpu.VMEM((nrel, ra, D), F32), pltpu.VMEM((nrel, rb, wb), F32),
                        pltpu.VMEM((N_DEV,) + dmod8.shape, F32), pltpu.VMEM((ra, D), F32), pltpu.VMEM((rb, wb), F32),
                        pltpu.SemaphoreType.DMA((3 * nrel,)), pltpu.SemaphoreType.DMA((3 * nrel,)),
                        pltpu.SemaphoreType.DMA((2 * nrel,)), pltpu.SemaphoreType.DMA((2 * nrel,))],
        compiler_params=_cp(),
    )(buf_a, buf_b, dmod8)


def _adam(w, g, m, v):
    m2 = ADAM_B1 * m + (1.0 - ADAM_B1) * g
    v2 = ADAM_B2 * v + (1.0 - ADAM_B2) * (g * g)
    m_hat = m2 / (1.0 - ADAM_B1 ** ADAM_STEP)
    v_hat = v2 / (1.0 - ADAM_B2 ** ADAM_STEP)
    return -ADAM_LR * (m_hat / (jnp.sqrt(v_hat) + ADAM_EPS) + ADAM_WD * w), m2, v2


def _small_adam(red_a, red_b, dm_all, params):
    n = len(params)

    def body(*refs):
        ra, rb, dm = refs[:3]
        wmv = refs[3:3 + 3 * n]
        outs = refs[3 + 3 * n:]
        x, y, _ = _pos()
        chip = 2 * x + y

        def shard(row0, nrows, width):
            per_row = D // width
            cands = []
            for k in range(N_CHIP):
                if nrows == 1 or per_row >= N_CHIP:
                    cands.append(ra[row0:row0 + nrows, k * width:(k + 1) * width])
                else:
                    rr, cc = divmod(k * width, D)
                    cands.append(ra[row0 + rr:row0 + rr + 1, cc:cc + width])
            g = cands[0]
            for k in range(1, N_CHIP):
                g = jnp.where(chip == k, cands[k], g)
            return g

        dms = jnp.sum(dm[...], axis=0)
        hw = LRU_HEADS * LRU_HEAD_DIM
        grads = [
            ra[ROW_NORM_G:ROW_NORM_G + 2, :],
            None,
            shard(ROW_CONV_W, 4, D // N_CHIP),
            ra[ROW_CONV_B:ROW_CONV_B + 1, :],
            rb[0:hw, :],
            ra[ROW_B_A:ROW_B_A + 1, :],
            rb[hw:2 * hw, :],
            ra[ROW_B_X:ROW_B_X + 1, :],
            ra[ROW_LAMBDA:ROW_LAMBDA + 1, :],
            shard(ROW_SC_W, 3, D // N_CHIP),
            shard(ROW_POOL_B, 2, 2 * D // N_CHIP),
            shard(ROW_POOL_S, 2, 2 * D // N_CHIP),
            ra[ROW_FINAL_G:ROW_FINAL_G + 1, :],
        ]
        for p in range(n):
            w_ref, m_ref, v_ref = wmv[3 * p:3 * p + 3]
            g_out, d_out, m_out, v_out = outs[4 * p:4 * p + 4]
            if grads[p] is None:
                for r in range(SUBLANES):
                    l, cols = r // N_CHIP, slice((r % N_CHIP) * DMOD_W, (r % N_CHIP + 1) * DMOD_W)
                    g = dms[r:r + 1, :]
                    dl, m2, v2 = _adam(w_ref[l:l + 1, cols], g, m_ref[l:l + 1, cols], v_ref[l:l + 1, cols])
                    g_out[l:l + 1, cols] = g
                    d_out[l:l + 1, cols] = dl
                    m_out[l:l + 1, cols] = m2
                    v_out[l:l + 1, cols] = v2
            else:
                g = grads[p]
                dl, m2, v2 = _adam(w_ref[...], g, m_ref[...], v_ref[...])
                g_out[...] = g
                d_out[...] = dl
                m_out[...] = m2
                v_out[...] = v2

    flat = [a for p in params for a in p]
    return pl.pallas_call(
        body, name="small_adam", in_specs=[VMEM] * (3 + len(flat)), out_specs=[VMEM] * (4 * n),
        out_shape=[jax.ShapeDtypeStruct(p[0].shape, F32) for p in params for _ in range(4)],
        compiler_params=_cp(),
    )(red_a, red_b, dm_all, *flat)


def _modw_adam(ca_t, dm_sh, w, m, v):
    nw = w.shape[2]

    def body(c_ref, d_ref, w_ref, m_ref, v_ref, g_out, d_out, m_out, v_out):
        g = jnp.dot(c_ref[...], d_ref[...], precision=lax.Precision.HIGHEST, preferred_element_type=F32)
        dl, m2, v2 = _adam(w_ref[...], g, m_ref[...], v_ref[...])
        g_out[...] = g
        d_out[...] = dl
        m_out[...] = m2
        v_out[...] = v2

    blk = pl.BlockSpec((None, D, nw), lambda l: (l, 0, 0))
    return pl.pallas_call(
        body, name="modw_adam", grid=(2,),
        in_specs=[pl.BlockSpec((D, SUBLANES), lambda l: (0, 0)), pl.BlockSpec((None, SUBLANES, nw), lambda l: (l, 0, 0)),
                  blk, blk, blk],
        out_specs=[blk] * 4, out_shape=[jax.ShapeDtypeStruct(w.shape, F32)] * 4,
        compiler_params=_cp(("arbitrary",)),
    )(ca_t, dm_sh, w, m, v)


def _exchange(copies, name, out_type, n_sems, args, sequencer, after=None):
    order = [] if after is None else [after]
    n_in, n_out = len(args) + len(order), len(out_type)

    def body(*refs):
        barrier = pltpu.get_barrier_semaphore()
        peers = sequencer[1](*_pos())
        for peer in peers:
            pl.semaphore_signal(barrier, inc=1, device_id=peer, device_id_type=MESH)
        pl.semaphore_wait(barrier, len(peers))
        copies(refs[:n_in], refs[n_in:n_in + n_out], refs[n_in + n_out], refs[n_in + n_out + 1])

    sems = [pltpu.SemaphoreType.DMA((n_sems,))] * 2
    return pl.kernel(body, out_type, mesh=plsc.ScalarSubcoreMesh(axis_name="sequencer", num_cores=1), name=name,
                     scratch_types=sems, compiler_params=pltpu.CompilerParams(collective_id=sequencer[0]))(*args, *order)


def _sibling(x, y, c):
    return [(x, y, 1 - c)]


def _other_chips(x, y, c):
    return [(1 - x, y, c), (x, 1 - y, c), (1 - x, 1 - y, c)]


def _to_wire(g, name, after=None):
    _, rr, cc = g.shape
    rb = min(rr, 256)

    def body(g_ref, *rest):
        rest[-1][...] = g_ref[...].astype(GRAD_WIRE_DTYPE)

    order = [] if after is None else [after]
    blk = pl.BlockSpec((None, rb, cc), lambda k, j: (k, j, 0))
    return pl.pallas_call(
        body, name=name, grid=(N_CHIP, rr // rb), in_specs=[blk] + [ANY] * len(order), out_specs=blk,
        out_shape=jax.ShapeDtypeStruct(g.shape, GRAD_WIRE_DTYPE), compiler_params=_cp(("parallel", "parallel")),
    )(g, *order)


def _chip_scatter(ps, name, collective_id, after=None):
    n = len(ps)

    def copies(ins, outs, ssem, rsem):
        x, y, c = _pos()
        cps = []
        for a in range(n):
            for q, (fx, fy) in enumerate(((1, 0), (0, 1), (1, 1))):
                px, py = _flip(x, fx), _flip(y, fy)
                cp = pltpu.make_async_remote_copy(
                    src_ref=ins[a].at[2 * px + py], dst_ref=outs[a].at[q],
                    send_sem=ssem.at[3 * a + q], recv_sem=rsem.at[3 * a + q], device_id=(px, py, c), device_id_type=MESH)
                cp.start()
                cps.append(cp)
        for cp in cps:
            cp.wait()

    out_type = [jax.ShapeDtypeStruct((N_CHIP - 1,) + p.shape[1:], p.dtype) for p in ps]
    return _exchange(copies, name, out_type, 3 * n, ps, (collective_id, _other_chips), after)


def _add_owner(p, got, chipidx, name, after=None):
    _, hr, cc = p.shape
    rb = min(hr, 256)

    def body(k_ref, p_ref, r_ref, *rest):
        rest[-1][...] = ((p_ref[...].astype(F32) + r_ref[0].astype(F32)) + r_ref[1].astype(F32)) + r_ref[2].astype(F32)

    order = [] if after is None else [after]
    return pl.pallas_call(
        body, name=name,
        grid_spec=pltpu.PrefetchScalarGridSpec(
            num_scalar_prefetch=1, grid=(hr // rb,),
            in_specs=[pl.BlockSpec((None, rb, cc), lambda j, k_ref: (k_ref[0], j, 0)),
                      pl.BlockSpec((N_CHIP - 1, rb, cc), lambda j, k_ref: (0, j, 0))] + [ANY] * len(order),
            out_specs=pl.BlockSpec((rb, cc), lambda j, k_ref: (j, 0))),
        out_shape=jax.ShapeDtypeStruct((hr, cc), F32),
        compiler_params=_cp(("parallel",)),
    )(chipidx, p, got, *order)


def _sib_exchange(ts_, name, collective_id, after=None):
    n = len(ts_)

    def copies(ins, outs, ssem, rsem):
        x, y, c = _pos()
        cps = []
        for a in range(n):
            cp = pltpu.make_async_remote_copy(src_ref=ins[a], dst_ref=outs[a], send_sem=ssem.at[a],
                                              recv_sem=rsem.at[a], device_id=(x, y, 1 - c), device_id_type=MESH)
            cp.start()
            cps.append(cp)
        for cp in cps:
            cp.wait()

    out_type = [jax.ShapeDtypeStruct(t.shape, F32) for t in ts_]
    return _exchange(copies, name, out_type, n, ts_, (collective_id, _sibling), after)


def _adam_2d(w, g_own, g_sib, m, v, name):
    rr, cc = w.shape
    rb = min(rr, 256)

    def body(w_ref, go_ref, gs_ref, m_ref, v_ref, g_out, d_out, m_out, v_out):
        g = go_ref[...] + gs_ref[...]
        dl, m2, v2 = _adam(w_ref[...], g, m_ref[...], v_ref[...])
        g_out[...] = g
        d_out[...] = dl
        m_out[...] = m2
        v_out[...] = v2

    blk = pl.BlockSpec((rb, cc), lambda j: (j, 0))
    return pl.pallas_call(
        body, name=name, grid=(rr // rb,), in_specs=[blk] * 5, out_specs=[blk] * 4,
        out_shape=[jax.ShapeDtypeStruct((rr, cc), F32)] * 4, compiler_params=_cp(("parallel",)),
    )(w, g_own, g_sib, m, v)


def kernel(x, c, norm_g, mod_w, mod_b, hy_w_in, hy_conv_w, hy_conv_b, lru_w_a, lru_b_a, lru_w_x, lru_b_x, lru_lambda, sc_conv_w, hy_w_out, pool_w_in, pool_w_grp, pool_b_grp, pool_scale, pool_w_out, final_g, loss_target, m_norm_g, m_mod_w, m_mod_b, m_hy_w_in, m_hy_conv_w, m_hy_conv_b, m_lru_w_a, m_lru_b_a, m_lru_w_x, m_lru_b_x, m_lru_lambda, m_sc_conv_w, m_hy_w_out, m_pool_w_in, m_pool_w_grp, m_pool_b_grp, m_pool_scale, m_pool_w_out, m_final_g, v_norm_g, v_mod_w, v_mod_b, v_hy_w_in, v_hy_conv_w, v_hy_conv_b, v_lru_w_a, v_lru_b_a, v_lru_w_x, v_lru_b_x, v_lru_lambda, v_sc_conv_w, v_hy_w_out, v_pool_w_in, v_pool_w_grp, v_pool_b_grp, v_pool_scale, v_pool_w_out, v_final_g):
    ax, ay, ac = _pos()
    me = 4 * ax + 2 * ay + ac
    chip = 2 * ax + ay
    xs = x[0]
    tgt = loss_target[0]
    gd = POOL_GROUP_DIM
    kidx = chip.reshape(1).astype(jnp.int32)

    big = [hy_w_in[0], hy_w_out[0], pool_w_in[0], pool_w_grp[0].reshape(4 * 128, gd), pool_w_out[0]]
    w_in0, w_out0 = _wgather_sequencer(
        [_wcast_own_block(w, kidx, f"wcast_own_block_{a}") for a, w in enumerate(big[:2])], "wgather_l0", CIDS_WGATHER[0])

    ca_all, mod_all, small_w = _mod_fwd(jnp.broadcast_to(c, (SUBLANES, D)), mod_w, mod_b,
                                        hy_conv_w[0], sc_conv_w[0], pool_b_grp, pool_scale)
    mod_me = lax.dynamic_index_in_dim(mod_all, me, axis=1, keepdims=False)
    sh0, sc0, gt0 = (mod_me[0:1, k * D:(k + 1) * D] for k in range(3))
    sh1, sc1, gt1 = (mod_me[1:2, k * D:(k + 1) * D] for k in range(3))
    cw = small_w[SW_CONV:SW_CONV + 4, 0:D]
    sw = small_w[SW_SC:SW_SC + 3, 0:D]
    pool_b = small_w[SW_POOL_B:SW_POOL_B + 1, :]
    pool_s = small_w[SW_POOL_S:SW_POOL_S + 1, :]
    g0, g1, gf = norm_g[0:1], norm_g[1:2], final_g.reshape(1, D)
    cb, ba, bx, lam = hy_conv_b, lru_b_a, lru_b_x, lru_lambda

    h0 = _norm_mod(xs, g0, sc0, sh0, "l0_norm")
    wa_b, wx_b = _wcast([lru_w_a[0], lru_w_x[0]])
    w_in1, w_grp, w_out1 = _wgather_sequencer(
        [_wcast_own_block(w, kidx, f"wcast_own_block_{a + 2}", after=(w_out0, h0)) for a, w in enumerate(big[2:])],
        "wgather_l1", CIDS_WGATHER[1])
    w_grp =w_grp.reshape(N_CHIP, 4, 128, gd).transpose(1, 0, 2, 3).reshape(4, gd, gd)

    x1, hst, y0, xc0, cz0, rg0, ig0, proj0 = _l0_fwd(h0, xs, w_in0, gt0, cw, cb, wa_b, ba, wx_b, bx, lam, sw,
                                                     w_out0.reshape(2 * D, D))
    dpool, mixed, y1, dx2, losscols, dgf, h1, proj1 = _l1_fwd(x1, g1, sc1, sh1, w_in1, tgt, gt1, w_grp, pool_b, pool_s,
                                                              w_out1.reshape(2 * D, D), gf)

    def add_owners(grads, got, tag, ids, after):
        own = []
        for a, (g, r) in enumerate(zip(grads, got)):
            own.append(_add_owner(g, r, kidx, f"grad_add_owner_{tag}{a}", own[-1] if own else after))
        return own, _sib_exchange(own, f"grad_sib_exchange_{tag}", ids[1])

    dproj1, mt1, d_wgrp, dsc1, dbg1 = _l1_bwd_mix(dx2, proj1, mixed, y1, dpool, gt1, w_grp, pool_s,
                                                  w_out1.reshape(2 * D, D))
    d_win1, wire_win1 = _wgrad(h1, dproj1, N_CHIP, D, D, lambda g: 0, lambda g: g, "l1_wgrad_in")
    d_wout1, wire_wout1, dgate1 = _wo_final(mt1, w_out1, gt1, "l1_wo_final")
    d_wgrp = d_wgrp.reshape(4, N_CHIP, 128, gd).transpose(1, 0, 2, 3).reshape(N_CHIP, 4 * 128, gd)
    grads_l1 = [d_win1, d_wgrp, d_wout1]
    got_l1 = _chip_scatter([wire_win1, _to_wire(d_wgrp, "grad_to_wire_grp"), wire_wout1], "grad_chip_scatter_l1",
                           CIDS_L1[0])
    dx1, s1_1, s2_1 = _dgrad_norm(dproj1, w_in1, x1, dx2, g1, sc1, "l1_bwd_proj")

    dproj0, mt0, d_wa, d_wx, sm0 = _l0_bwd_mix(dx1, proj0, hst, y0, xc0, cz0, rg0, ig0, gt0, cw, wa_b, wx_b, lam, sw,
                                               w_out0.reshape(2 * D, D))
    sums_l1, sib_l1 = add_owners(grads_l1, got_l1, "l1", CIDS_L1, after=sm0)
    d_win0, wire_win0 = _wgrad(h0, dproj0, N_CHIP, D, 6 * D // N_CHIP, lambda g: 0, lambda g: g, "l0_wgrad_in",
                               after=sums_l1[-1])
    d_wout0, wire_wout0, dgate0 = _wo_final(mt0, w_out0, gt0, "l0_wo_final")
    grads_l0 = [d_win0, d_wout0]
    got_l0 = _chip_scatter([wire_win0, wire_wout0], "grad_chip_scatter_l0", CIDS_L0[0], after=sib_l1[0])
    grad_x, s1_0, s2_0 = _dgrad_norm(dproj0, w_in0, xs, dx1, g0, sc0, "l0_bwd_proj", after=wire_win0)
    sums_l0, sib_l0 = add_owners(grads_l0, got_l0, "l0", CIDS_L0, after=s1_0)

    buf_a, dmod8 = _small_pack(s1_0, s2_0, s1_1, s2_1, sm0, dsc1, dbg1, dgf, losscols, dgate0, dgate1,
                                      norm_g, sc0, sc1, lam)
    hw = LRU_HEADS * LRU_HEAD_DIM
    buf_b = jnp.concatenate([d_wa.reshape(hw, LRU_HEAD_DIM), d_wx.reshape(hw, LRU_HEAD_DIM)], axis=0)
    red_a, red_b, dm_all = _small_comm(buf_a, buf_b, dmod8)
    small = [(norm_g, m_norm_g, v_norm_g), (mod_b, m_mod_b, v_mod_b),
             (hy_conv_w[0], m_hy_conv_w[0], v_hy_conv_w[0]), (hy_conv_b, m_hy_conv_b, v_hy_conv_b),
             tuple(a.reshape(hw, LRU_HEAD_DIM) for a in (lru_w_a, m_lru_w_a, v_lru_w_a)),
             (lru_b_a, m_lru_b_a, v_lru_b_a),
             tuple(a.reshape(hw, LRU_HEAD_DIM) for a in (lru_w_x, m_lru_w_x, v_lru_w_x)),
             (lru_b_x, m_lru_b_x, v_lru_b_x), (lru_lambda, m_lru_lambda, v_lru_lambda),
             (sc_conv_w[0], m_sc_conv_w[0], v_sc_conv_w[0]), (pool_b_grp, m_pool_b_grp, v_pool_b_grp),
             (pool_scale, m_pool_scale, v_pool_scale),
             tuple(a.reshape(1, D) for a in (final_g, m_final_g, v_final_g))]
    small_names = ["norm_g", "mod_b", "hy_conv_w", "hy_conv_b", "lru_w_a", "lru_b_a", "lru_w_x", "lru_b_x",
                   "lru_lambda", "sc_conv_w", "pool_b_grp", "pool_scale", "final_g"]
    small_out = _small_adam(red_a, red_b, dm_all, small)
    res = {}
    shapes = dict(norm_g=norm_g, mod_b=mod_b, hy_conv_w=hy_conv_w, hy_conv_b=hy_conv_b, lru_w_a=lru_w_a, lru_b_a=lru_b_a,
                  lru_w_x=lru_w_x, lru_b_x=lru_b_x, lru_lambda=lru_lambda, sc_conv_w=sc_conv_w, pool_b_grp=pool_b_grp,
                  pool_scale=pool_scale, final_g=final_g)
    for p, nm in enumerate(small_names):
        res[nm] = tuple(o.reshape(shapes[nm].shape) for o in small_out[4 * p:4 * p + 4])

    nw = mod_w.shape[2]
    assert nw == DMOD_W
    dm_sh = jnp.stack([lax.dynamic_index_in_dim(dm_all, N_CHIP * l + chip, axis=1, keepdims=False) for l in range(2)])
    res["mod_w"] = tuple(_modw_adam(ca_all.T, dm_sh, mod_w, m_mod_w, v_mod_w))

    sums = list(sums_l0) + list(sums_l1)
    sib_sums = list(sib_l0) + list(sib_l1)
    big_names = ["hy_w_in", "hy_w_out", "pool_w_in", "pool_w_grp", "pool_w_out"]
    big_wmv = [(hy_w_in, m_hy_w_in, v_hy_w_in), (hy_w_out, m_hy_w_out, v_hy_w_out), (pool_w_in, m_pool_w_in, v_pool_w_in),
               (pool_w_grp, m_pool_w_grp, v_pool_w_grp), (pool_w_out, m_pool_w_out, v_pool_w_out)]
    for a, nm in enumerate(big_names):
        rr, cc = big[a].shape
        w, m, v = (t.reshape(rr, cc) for t in big_wmv[a])
        outs = _adam_2d(w, sums[a], sib_sums[a], m, v, f"adam_{nm}")
        res[nm] = tuple(o.reshape(big_wmv[a][0].shape) for o in outs)

    loss = red_a[ROW_LOSS, 0]
    order = ["norm_g", "mod_w", "mod_b", "hy_w_in", "hy_conv_w", "hy_conv_b", "lru_w_a", "lru_b_a", "lru_w_x", "lru_b_x",
             "lru_lambda", "sc_conv_w", "hy_w_out", "pool_w_in", "pool_w_grp", "pool_b_grp", "pool_scale", "pool_w_out",
             "final_g"]
    return (loss, grad_x[None], *[res[nm][0] for nm in order], *[res[nm][1] for nm in order],
            *[res[nm][2] for nm in order], *[res[nm][3] for nm in order])
```

```python
import jax
import jax.numpy as jnp
from jax import lax
from jax.experimental import pallas as pl
from jax.experimental.pallas import tpu as pltpu
from jax.experimental.pallas import tpu_sc as plsc

F32, BF16 = jnp.float32, jnp.bfloat16
D = 1024
RMS_EPS = 1e-6
SQRT_FLOOR = 1e-30
LRU_C = 8.0
LRU_HEADS, LRU_HEAD_DIM = 8, 128
POOL_WINDOWS = (2, 4, 8, 16)
POOL_GROUP_DIM = 512
ADAM_LR, ADAM_B1, ADAM_B2, ADAM_EPS, ADAM_WD, ADAM_STEP = 0.001, 0.9, 0.999, 1e-08, 0.01, 10
MESH = pl.DeviceIdType.MESH
CIDS_WGATHER = (1, 8)
CIDS_L1 = (2, 3)
CIDS_L0 = (4, 5)
N_DEV, N_CHIP = 8, 4
SUBLANES = 8
BF16_ROWS = 16
POOL_HALO = 16
TS_MIX, TS_WGRAD, TS_DGRAD = 256, 2048, 512
SMALL_ROWS = 64
GRAD_WIRE_DTYPE = BF16
ANY = pl.BlockSpec(memory_space=pl.ANY)
VMEM = pl.BlockSpec(memory_space=pltpu.VMEM)
NT = (((1,), (1,)), ((), ()))
TN = (((0,), (0,)), ((), ()))


def _cp(sem=None, vmem_mb=56):
    kw = dict(vmem_limit_bytes=vmem_mb * 2 ** 20)
    if sem is not None:
        kw["dimension_semantics"] = sem
    return pltpu.CompilerParams(**kw)


def _tile(n, t):
    return min(n, t)


def _pos():
    return lax.axis_index("x"), lax.axis_index("y"), lax.axis_index("c")


def _flip(v, f):
    return 1 - v if f else v


def _sigmoid(z):
    return 0.5 * jnp.tanh(0.5 * z) + 0.5


def _rows(n, c):
    return lax.broadcasted_iota(jnp.int32, (n, c), 0)


def _down(a, d):
    return a if d == 0 else pltpu.roll(a, d, 0)


def _up(a, d):
    return a if d == 0 else pltpu.roll(a, a.shape[0] - d, 0)


def _scan_fwd_steps(a, u, carry):
    n, c = a.shape
    sub = _rows(SUBLANES, c)
    out = []
    for k in range(n // SUBLANES):
        p = a[k * SUBLANES:(k + 1) * SUBLANES]
        g = u[k * SUBLANES:(k + 1) * SUBLANES]
        for d in (1, 2, 4):
            keep = sub >= d
            g = g + p * jnp.where(keep, pltpu.roll(g, d, 0), 0.0)
            p = p * jnp.where(keep, pltpu.roll(p, d, 0), 1.0)
        h = g + p * carry
        carry = h[SUBLANES - 1:SUBLANES, :]
        out.append(h)
        yield
    return jnp.concatenate(out, axis=0)


def _scan_rev_steps(alpha, b, carry):
    n, c = alpha.shape
    sub = _rows(SUBLANES, c)
    out = []
    for k in reversed(range(n // SUBLANES)):
        p = alpha[k * SUBLANES:(k + 1) * SUBLANES]
        g = b[k * SUBLANES:(k + 1) * SUBLANES]
        for d in (1, 2, 4):
            keep = sub < SUBLANES - d
            g = g + p * jnp.where(keep, pltpu.roll(g, SUBLANES - d, 0), 0.0)
            p = p * jnp.where(keep, pltpu.roll(p, SUBLANES - d, 0), 1.0)
        h = g + p * carry
        carry = h[0:1, :]
        out.append(h)
        yield
    return jnp.concatenate(out[::-1], axis=0)


def _run(steps):
    while True:
        try:
            next(steps)
        except StopIteration as done:
            return done.value


def _paired(progress, pieces):
    n, done = len(pieces), 1
    pieces[0]()
    for frac in progress:
        while done < n and done <= frac * n:
            pieces[done]()
            done += 1
    while done < n:
        pieces[done]()
        done += 1


def _conv_taps(ext, halo, n, width):
    return [_down(ext, width - 1 - k)[halo:halo + n] for k in range(width)]


def _lru_gates(xc, wa_ref, ba, wx_ref, bx):
    xb = xc.astype(BF16)
    pa, px = [], []
    for h in range(LRU_HEADS):
        xh = xb[:, h * LRU_HEAD_DIM:(h + 1) * LRU_HEAD_DIM]
        pa.append(jnp.dot(xh, wa_ref[h], preferred_element_type=F32))
        px.append(jnp.dot(xh, wx_ref[h], preferred_element_type=F32))
    r = _sigmoid(jnp.concatenate(pa, axis=1) + ba)
    ig = _sigmoid(jnp.concatenate(px, axis=1) + bx)
    return r, ig


def _softplus_neg(lam):
    return jnp.maximum(-lam, 0.0) + jnp.log1p(jnp.exp(-jnp.abs(lam)))


def _recip_1_to_2(d):
    r0 = pl.reciprocal(d, approx=True)
    return r0 * (2.0 - d * r0)


def _lru_decay(r, sp, first):
    big_l = (-LRU_C) * r * sp
    a = jnp.exp(big_l)
    th = jnp.tanh(big_l)
    q = (-2.0 * th) * _recip_1_to_2(1.0 - th)
    rs = lax.rsqrt(jnp.maximum(q, SQRT_FLOOR))
    return a, jnp.where(first, 1.0, q * rs), rs


def _pool_inv_counts(t0, n):
    t = (t0 + lax.broadcasted_iota(jnp.int32, (n, 1), 0) + 1).astype(F32)
    return [1.0 / jnp.minimum(t, float(w)) for w in POOL_WINDOWS]


def _window_sums(ext, shift):
    gd = POOL_GROUP_DIM
    out = []
    s = ext
    for k in range(len(POOL_WINDOWS)):
        s = s + shift(s, 2 ** k)
        out.append(s[:, 0:gd])
        if k + 1 < len(POOL_WINDOWS):
            s = s[:, gd:]
    return out


SW_ROWS, SW_COLS = 16, 2 * D
SW_CONV, SW_SC, SW_POOL_B, SW_POOL_S = 0, 4, 8, 9


def _mod_fwd(c8, mod_w, mod_b, conv_w, sc_w, pool_b, pool_s):
    nw = mod_w.shape[2]
    cq, pq = conv_w.shape[1], pool_b.shape[1]

    def body(c_ref, w_ref, b_ref, cw_ref, sw_ref, pb_ref, ps_ref, ca_ref, mod_ref, small_ref,
             cslot, mslot, msend, pslot, psend, s1, r1, s2, r2, s3, r3):
        x, y, c = _pos()
        me = 4 * x + 2 * y + c
        chip = 2 * x + y
        first = []
        for r in range(1, N_DEV):
            fx, fy, fc = (r >> 2) & 1, (r >> 1) & 1, r & 1
            cp = pltpu.make_async_remote_copy(
                src_ref=c_ref, dst_ref=cslot.at[me], send_sem=s1.at[r - 1], recv_sem=r1.at[r - 1],
                device_id=(_flip(x, fx), _flip(y, fy), _flip(c, fc)), device_id_type=MESH)
            cp.start()
            first.append(cp)
        cslot[me] = c_ref[...]
        for cp in first:
            cp.wait()
        rows = _rows(SUBLANES, D)
        call = jnp.zeros((SUBLANES, D), F32)
        for d in range(N_DEV):
            call = jnp.where(rows == d, cslot[d], call)
        ca = call * _sigmoid(call)
        ca_ref[...] = ca
        for l in range(2):
            msend[l] = jnp.dot(ca, w_ref[l], precision=lax.Precision.HIGHEST, preferred_element_type=F32)
        psend[...] = jnp.zeros_like(psend)
        psend[SW_CONV:SW_CONV + 4, 0:cq] = cw_ref[...]
        psend[SW_SC:SW_SC + 3, 0:cq] = sw_ref[...]
        psend[SW_POOL_B:SW_POOL_B + 1, :] = pb_ref[...]
        psend[SW_POOL_S:SW_POOL_S + 1, :] = ps_ref[...]
        second = []
        for q, (fx, fy) in enumerate(((1, 0), (0, 1), (1, 1))):
            peer = (_flip(x, fx), _flip(y, fy), c)
            for src, dst, ss, rs in ((msend, mslot, s2, r2), (psend, pslot, s3, r3)):
                cp = pltpu.make_async_remote_copy(src_ref=src, dst_ref=dst.at[chip], send_sem=ss.at[q], recv_sem=rs.at[q],
                                                  device_id=peer, device_id_type=MESH)
                cp.start()
                second.append(cp)
        mslot[chip] = msend[...]
        pslot[chip] = psend[...]
        for cp in second:
            cp.wait()
        small_ref[...] = jnp.zeros_like(small_ref)
        for j in range(N_CHIP):
            for l in range(2):
                mod_ref[l, :, j * nw:(j + 1) * nw] = mslot[j, l] + b_ref[l:l + 1, j * nw:(j + 1) * nw]
            small_ref[0:SUBLANES, j * cq:(j + 1) * cq] = pslot[j, 0:SUBLANES, 0:cq]
            small_ref[SUBLANES:SW_ROWS, j * pq:(j + 1) * pq] = pslot[j, SUBLANES:SW_ROWS, :]

    args = (c8, mod_w, mod_b, conv_w, sc_w, pool_b, pool_s)
    dma3 = pltpu.SemaphoreType.DMA((N_CHIP - 1,))
    return pl.pallas_call(
        body, name="mod_fwd",
        in_specs=[VMEM] * len(args), out_specs=[VMEM] * 3,
        out_shape=[jax.ShapeDtypeStruct((SUBLANES, D), F32), jax.ShapeDtypeStruct((2, SUBLANES, N_CHIP * nw), F32),
                   jax.ShapeDtypeStruct((SW_ROWS, SW_COLS), F32)],
        scratch_shapes=[pltpu.VMEM((N_DEV, SUBLANES, D), F32), pltpu.VMEM((N_CHIP, 2, SUBLANES, nw), F32),
                        pltpu.VMEM((2, SUBLANES, nw), F32), pltpu.VMEM((N_CHIP, SW_ROWS, pq), F32),
                        pltpu.VMEM((SW_ROWS, pq), F32),
                        pltpu.SemaphoreType.DMA((N_DEV - 1,)), pltpu.SemaphoreType.DMA((N_DEV - 1,)),
                        dma3, dma3, dma3, dma3],
        compiler_params=_cp(),
    )(*args)


def _wcast(ws):
    def body(*refs):
        n = len(refs) // 2
        for a in range(n):
            refs[n + a][...] = refs[a][...].astype(BF16)

    return pl.pallas_call(
        body, name="wcast", in_specs=[VMEM] * len(ws), out_specs=[VMEM] * len(ws),
        out_shape=[jax.ShapeDtypeStruct(w.shape, BF16) for w in ws], compiler_params=_cp(),
    )(*ws)


def _wcast_own_block(w, kidx, name, after=()):
    rr, cc = w.shape
    rb = min(rr, 256)

    def body(k_ref, w_ref, *rest):
        rest[-1][...] = w_ref[...].astype(BF16)

    order = list(after)
    return pl.pallas_call(
        body, name=name,
        grid_spec=pltpu.PrefetchScalarGridSpec(
            num_scalar_prefetch=1, grid=(rr // rb,),
            in_specs=[pl.BlockSpec((rb, cc), lambda j, k_ref: (j, 0))] + [ANY] * len(order),
            out_specs=pl.BlockSpec((None, rb, cc), lambda j, k_ref: (k_ref[0], j, 0))),
        out_shape=jax.ShapeDtypeStruct((N_CHIP, rr, cc), BF16),
        compiler_params=_cp(("parallel",)),
    )(kidx, w, *order)


def _wgather_copies(outs, rows, ssem, rsem, fssem, frsem):
    n = len(outs)
    x, y, c = _pos()
    chip = 2 * x + y
    sib = (x, y, 1 - c)
    flips = ((1, 0), (0, 1), (1, 1))

    def half(a, which):
        hr = rows[a] // 2
        return pl.ds(pl.multiple_of(which * hr, BF16_ROWS), hr)

    sends = []
    for a in range(n):
        mine = outs[a].at[chip, half(a, c), :]
        for q, (fx, fy) in enumerate(flips):
            cp = pltpu.make_async_remote_copy(
                src_ref=mine, dst_ref=mine, send_sem=ssem.at[3 * a + q], recv_sem=rsem.at[3 * a + q],
                device_id=(_flip(x, fx), _flip(y, fy), c), device_id_type=MESH)
            cp.start()
            sends.append(cp)
    passed = []
    for a in range(n):
        for q, (fx, fy) in enumerate(flips):
            src_chip = 2 * _flip(x, fx) + _flip(y, fy)
            landed = outs[a].at[src_chip, half(a, c), :]
            pltpu.make_async_remote_copy(
                src_ref=landed, dst_ref=landed, send_sem=ssem.at[3 * a + q], recv_sem=rsem.at[3 * a + q],
                device_id=sib, device_id_type=MESH).wait_recv()
            cp = pltpu.make_async_remote_copy(
                src_ref=landed, dst_ref=landed, send_sem=fssem.at[3 * a + q], recv_sem=frsem.at[3 * a + q],
                device_id=sib, device_id_type=MESH)
            cp.start()
            passed.append(cp)
    for a in range(n):
        for q, (fx, fy) in enumerate(flips):
            src_chip = 2 * _flip(x, fx) + _flip(y, fy)
            other = outs[a].at[src_chip, half(a, 1 - c), :]
            pltpu.make_async_remote_copy(
                src_ref=other, dst_ref=other, send_sem=fssem.at[3 * a + q], recv_sem=frsem.at[3 * a + q],
                device_id=sib, device_id_type=MESH).wait_recv()
    for cp in sends + passed:
        cp.wait_send()


def _wgather_sequencer(bufs, name, collective_id):
    n = len(bufs)
    refs = [jax.new_ref(b, memory_space=pltpu.MemorySpace.HBM) for b in bufs]
    dma = pltpu.SemaphoreType.DMA((3 * n,))

    @pl.kernel(mesh=plsc.ScalarSubcoreMesh(axis_name="sequencer", num_cores=1), name=name,
               scratch_types=(dma, dma, dma, dma), compiler_params=pltpu.CompilerParams(collective_id=collective_id))
    def launch(ssem, rsem, fssem, frsem):
        x, y, c = _pos()
        barrier = pltpu.get_barrier_semaphore()
        for peer in ((1 - x, y, c), (x, 1 - y, c), (1 - x, 1 - y, c), (x, y, 1 - c)):
            pl.semaphore_signal(barrier, inc=1, device_id=peer, device_id_type=MESH)
        pl.semaphore_wait(barrier, 4)
        _wgather_copies(refs, [b.shape[1] for b in bufs], ssem, rsem, fssem, frsem)

    launch()
    return [r[...] for r in refs]


def _norm_mod(x, g, sc, sh, name):
    s_len = x.shape[0]
    ts = _tile(s_len, TS_WGRAD)

    def body(x_ref, g_ref, sc_ref, sh_ref, h_ref):
        xv = x_ref[...]
        rinv = lax.rsqrt(jnp.mean(xv * xv, axis=-1, keepdims=True) + RMS_EPS)
        h_ref[...] = (xv * rinv * (g_ref[...] * (1.0 + sc_ref[...])) + sh_ref[...]).astype(BF16)

    row = pl.BlockSpec((ts, D), lambda i: (i, 0))
    vec = pl.BlockSpec((1, D), lambda i: (0, 0))
    return pl.pallas_call(
        body, name=name, grid=(s_len // ts,), in_specs=[row, vec, vec, vec], out_specs=row,
        out_shape=jax.ShapeDtypeStruct((s_len, D), BF16), compiler_params=_cp(("parallel",)),
    )(x, g, sc, sh)


def _l0_fwd(h0_all, x, w_in, gate, cw, cb, wa, ba, wx, bx, lam, sw, wo):
    s_len, nb = x.shape[0], w_in.shape[2]
    ts = _tile(s_len, TS_MIX)
    n_t = s_len // ts
    hl = SUBLANES

    def body(h0_ref, xb_ref, win_ref, gate_ref, cw_ref, cb_ref, wa_ref, ba_ref, wx_ref, bx_ref,
             lam_ref, sw_ref, wo_ref, x1_ref, h_ref, y_ref, xc_ref, cz_ref, r_ref, ig_ref, p_ref,
             pcur, pnext, cxa, czz, chh):
        i = pl.program_id(0)

        @pl.when(i == 0)
        def _():
            cxa[...] = jnp.zeros_like(cxa)
            czz[...] = jnp.zeros_like(czz)
            chh[...] = jnp.zeros_like(chh)
            pnext[...] = jnp.zeros_like(pnext)

        pcur[...] = pnext[...]
        h0 = h0_ref[...]

        def project(k, c0, cn):
            def emit():
                pk = jnp.dot(h0, win_ref[k, :, c0:c0 + cn], preferred_element_type=F32).astype(BF16)
                p_ref[:, k * nb + c0:k * nb + c0 + cn] = pk
                pnext[:, k * nb + c0:k * nb + c0 + cn] = pk
            return emit

        def mixer():
            piece = lambda k: pcur[:, k * D:(k + 1) * D].astype(F32)
            xa = piece(0)
            rows = _rows(ts, D)
            taps = _conv_taps(jnp.concatenate([cxa[...], xa], axis=0), hl, ts, 4)
            xc = cb_ref[...] + sum(cw_ref[k:k + 1, :] * taps[k] for k in range(4))
            xc_ref[...] = xc.astype(BF16)
            r, ig = _lru_gates(xc, wa_ref, ba_ref[...], wx_ref, bx_ref[...])
            r_ref[...] = r.astype(BF16)
            ig_ref[...] = ig.astype(BF16)
            a, m, _ = _lru_decay(r, _softplus_neg(lam_ref[...]), (rows == 0) & (i == 1))
            yield 0.26
            h = _run(_scan_fwd_steps(a, m * ig * xc, chh[hl - 1:hl, :]))
            yield 0.51
            gcp, v = piece(3), piece(4)
            z = gcp * v
            ztaps = _conv_taps(jnp.concatenate([czz[...], z], axis=0), hl, ts, 3)
            cz = sum(sw_ref[k:k + 1, :] * ztaps[k] for k in range(3))
            cz_ref[...] = cz.astype(BF16)
            yb = piece(2) * cz
            ga, gb = piece(1), piece(5)
            y = jnp.concatenate([h * (ga * _sigmoid(ga)), yb * (gb * _sigmoid(gb))], axis=1).astype(BF16)
            yield 0.76
            y_ref[...] = y
            x1_ref[...] = xb_ref[...] + gate_ref[...] * jnp.dot(y, wo_ref[...], preferred_element_type=F32)
            h_ref[...] = h.astype(BF16)
            cxa[...] = xa[ts - hl:, :]
            czz[...] = z[ts - hl:, :]
            chh[...] = jnp.where(i > 0, h[ts - hl:, :], 0.0)

        _paired(mixer(), [project(k, 0, nb) for k in range(N_CHIP)])

    def full(a):
        return pl.BlockSpec(a.shape, lambda i: (0,) * a.ndim)

    ahead = lambda w: pl.BlockSpec((ts, w), lambda i: (jnp.minimum(i, n_t - 1), 0))
    behind = lambda w: pl.BlockSpec((ts, w), lambda i: (jnp.maximum(i - 1, 0), 0))
    args = (h0_all, x, w_in, gate, cw, cb, wa, ba, wx, bx, lam, sw, wo)
    return pl.pallas_call(
        body, name="l0_fwd", grid=(n_t + 1,),
        in_specs=[ahead(D), behind(D)] + [full(a) for a in args[2:]],
        out_specs=[behind(D), behind(D), behind(2 * D)] + [behind(D)] * 4 + [ahead(N_CHIP * nb)],
        out_shape=[jax.ShapeDtypeStruct((s_len, D), F32), jax.ShapeDtypeStruct((s_len, D), BF16),
                   jax.ShapeDtypeStruct((s_len, 2 * D), BF16)] + [jax.ShapeDtypeStruct((s_len, D), BF16)] * 4
        + [jax.ShapeDtypeStruct((s_len, N_CHIP * nb), BF16)],
        scratch_shapes=[pltpu.VMEM((ts, N_CHIP * nb), BF16)] * 2 + [pltpu.VMEM((hl, D), F32)] * 3,
        compiler_params=_cp(("arbitrary",)),
    )(*args)


def _l1_fwd(x1, g, sc, sh, w_in, tgt, gate, wg, bg, scale, wo, gf):
    s_len, nb = x1.shape[0], w_in.shape[2]
    ts = _tile(s_len, TS_MIX)
    n_t = s_len // ts
    pw, gd, hl = 2 * D, POOL_GROUP_DIM, POOL_HALO

    def body(xa_ref, xb_ref, t_ref, g_ref, sc_ref, sh_ref, win_ref, gate_ref, wg_ref, bg_ref, scl_ref, wo_ref, gf_ref,
             d_ref, mx_ref, y_ref, dx_ref, loss_ref, dgf_ref, h1_ref, p_ref, pcur, pnext, cv):
        i = pl.program_id(0)

        @pl.when(i == 0)
        def _():
            cv[...] = jnp.zeros_like(cv)
            loss_ref[...] = jnp.zeros_like(loss_ref)
            dgf_ref[...] = jnp.zeros_like(dgf_ref)
            pnext[...] = jnp.zeros_like(pnext)

        pcur[...] = pnext[...]
        xv = xa_ref[...]
        rinv = lax.rsqrt(jnp.mean(xv * xv, axis=-1, keepdims=True) + RMS_EPS)
        h1 = (xv * rinv * (g_ref[...] * (1.0 + sc_ref[...])) + sh_ref[...]).astype(BF16)
        h1_ref[...] = h1

        def project(k):
            def emit():
                pk = jnp.dot(h1, win_ref[k], preferred_element_type=F32).astype(BF16)
                p_ref[:, k * nb:(k + 1) * nb] = pk
                pnext[:, k * nb:(k + 1) * nb] = pk
            return emit

        def mixer():
            v = pcur[:, 0:pw].astype(F32)
            sums = _window_sums(jnp.concatenate([cv[...], v], axis=0), _down)
            inv = _pool_inv_counts(jnp.maximum(i - 1, 0) * ts, ts)
            dd = [sums[k][hl:hl + ts] * inv[k] - v[:, k * gd:(k + 1) * gd] for k in range(4)]
            d_ref[...] = jnp.concatenate(dd, axis=1).astype(BF16)
            yield 0.26
            mixed = jnp.concatenate(
                [jnp.dot(dd[k].astype(BF16), wg_ref[k], preferred_element_type=F32) for k in range(4)], axis=1) + bg_ref[...]
            mx_ref[...] = mixed.astype(BF16)
            gg = pcur[:, pw:2 * pw].astype(F32)
            y = (mixed * scl_ref[...] * (gg * _sigmoid(gg))).astype(BF16)
            y_ref[...] = y
            yield 0.51
            x2 = xb_ref[...] + gate_ref[...] * jnp.dot(y, wo_ref[...], preferred_element_type=F32)
            yield 0.76
            r2 = lax.rsqrt(jnp.mean(x2 * x2, axis=-1, keepdims=True) + RMS_EPS)
            n2 = x2 * r2
            err = n2 * gf_ref[...] - t_ref[...]
            loss_ref[...] += jnp.where(i > 0, jnp.sum(err * err, axis=0, keepdims=True), 0.0)
            dyf = err * (1.0 / D)
            dgf_ref[...] += jnp.where(i > 0, jnp.sum(dyf * n2, axis=0, keepdims=True), 0.0)
            dn = dyf * gf_ref[...]
            dx_ref[...] = r2 * (dn - n2 * jnp.mean(dn * n2, axis=-1, keepdims=True))
            cv[...] = v[ts - hl:, :]

        _paired(mixer(), [project(k) for k in range(N_CHIP)])

    def full(a):
        return pl.BlockSpec(a.shape, lambda i: (0,) * a.ndim)

    ahead = lambda w: pl.BlockSpec((ts, w), lambda i: (jnp.minimum(i, n_t - 1), 0))
    behind = lambda w: pl.BlockSpec((ts, w), lambda i: (jnp.maximum(i - 1, 0), 0))
    acc = pl.BlockSpec((1, D), lambda i: (0, 0))
    args = (x1, x1, tgt, g, sc, sh, w_in, gate, wg, bg, scale, wo, gf)
    return pl.pallas_call(
        body, name="l1_fwd", grid=(n_t + 1,),
        in_specs=[ahead(D), behind(D), behind(D)] + [full(a) for a in args[3:]],
        out_specs=[behind(pw), behind(pw), behind(pw), behind(D), acc, acc, ahead(D), ahead(N_CHIP * nb)],
        out_shape=[jax.ShapeDtypeStruct((s_len, pw), BF16)] * 3 + [jax.ShapeDtypeStruct((s_len, D), F32)]
        + [jax.ShapeDtypeStruct((1, D), F32)] * 2
        + [jax.ShapeDtypeStruct((s_len, D), BF16), jax.ShapeDtypeStruct((s_len, N_CHIP * nb), BF16)],
        scratch_shapes=[pltpu.VMEM((ts, N_CHIP * nb), BF16)] * 2 + [pltpu.VMEM((hl, pw), F32)],
        compiler_params=_cp(("arbitrary",)),
    )(*args)


def _l1_bwd_mix(dx2, proj, mixed, y, dpool, gate, wg, scale, wo):
    s_len = dx2.shape[0]
    n_sub = 2
    ts = _tile(s_len, n_sub * TS_MIX)
    sub = ts // n_sub
    n_t = s_len // ts
    pw, gd, hl = 2 * D, POOL_GROUP_DIM, POOL_HALO

    def body(dx_ref, gg_ref, mx_ref, y_ref, d_ref, gate_ref, wg_ref, sc_ref, wo_ref,
             dp_ref, mt_ref, dwg_ref, dsc_ref, dbg_ref, cq):
        i = pl.program_id(0)

        @pl.when(i == 0)
        def _():
            cq[...] = jnp.zeros_like(cq)
            dsc_ref[...] = jnp.zeros_like(dsc_ref)
            dbg_ref[...] = jnp.zeros_like(dbg_ref)
            mt_ref[...] = jnp.zeros_like(mt_ref)
            dwg_ref[...] = jnp.zeros_like(dwg_ref)

        ahead_rows = {}

        def chain(j):
            rows = slice(j * sub, (j + 1) * sub)
            dxv = dx_ref[rows, :]
            dxb = dxv.astype(BF16)
            dy = lax.dot_general((gate_ref[...] * dxv).astype(BF16), wo_ref[...], NT, preferred_element_type=F32)
            for k in range(2):
                mt_ref[k] += lax.dot_general(y_ref[rows, k * gd:(k + 1) * gd], dxb, TN, preferred_element_type=F32)
            yield
            gg = gg_ref[rows, :].astype(F32)
            mixed = mx_ref[rows, :].astype(F32)
            s = _sigmoid(gg)
            sg = gg * s
            dym = dy * mixed
            dmixed = dy * sc_ref[...] * sg
            dsc_ref[...] += jnp.sum(dym * sg, axis=0, keepdims=True)
            dbg_ref[...] += jnp.sum(dmixed, axis=0, keepdims=True)
            dmb = dmixed.astype(BF16)
            dp_ref[rows, pw:2 * pw] = (dym * sc_ref[...] * (s + sg * (1.0 - s))).astype(BF16)
            yield
            inv = _pool_inv_counts((n_t - 1 - i) * ts + j * sub, sub)
            dd = []
            for k in range(4):
                dmk = dmb[:, k * gd:(k + 1) * gd]
                dd.append(lax.dot_general(dmk, wg_ref[k], NT, preferred_element_type=F32))
                dwg_ref[k] += lax.dot_general(d_ref[rows, k * gd:(k + 1) * gd], dmk, TN, preferred_element_type=F32)
            for k in range(2, 4):
                mt_ref[k] += lax.dot_general(y_ref[rows, k * gd:(k + 1) * gd], dxb, TN, preferred_element_type=F32)
            q = jnp.concatenate([dd[k] * inv[k] for k in range(4)], axis=1)
            ahead_rows[j] = q[0:hl, :]
            yield
            behind_q = cq[...] if j == n_sub - 1 else ahead_rows[j + 1]
            sums = _window_sums(jnp.concatenate([q, behind_q], axis=0), _up)
            dp_ref[rows, 0:pw] = jnp.concatenate([sums[k][0:sub] - dd[k] for k in range(4)], axis=1).astype(BF16)

        chains = [chain(j) for j in reversed(range(n_sub))]
        for _ in range(4):
            for ch in chains:
                next(ch, None)
        cq[...] = ahead_rows[0]

    def full(a):
        return pl.BlockSpec(a.shape, lambda i: (0,) * a.ndim)

    rev = lambda w, j=0: pl.BlockSpec((ts, w), lambda i: (n_t - 1 - i, j))
    acc = pl.BlockSpec((1, pw), lambda i: (0, 0))
    return pl.pallas_call(
        body, name="l1_bwd_mix", grid=(n_t,),
        in_specs=[rev(D), rev(pw, 1), rev(pw), rev(pw), rev(pw)] + [full(a) for a in (gate, wg, scale, wo)],
        out_specs=[rev(2 * pw), pl.BlockSpec((N_CHIP, gd, D), lambda i: (0, 0, 0)),
                   pl.BlockSpec((4, gd, gd), lambda i: (0, 0, 0)), acc, acc],
        out_shape=[jax.ShapeDtypeStruct((s_len, 2 * pw), BF16), jax.ShapeDtypeStruct((N_CHIP, gd, D), F32),
                   jax.ShapeDtypeStruct((4, gd, gd), F32),
                   jax.ShapeDtypeStruct((1, pw), F32), jax.ShapeDtypeStruct((1, pw), F32)],
        scratch_shapes=[pltpu.VMEM((hl, pw), F32)],
        compiler_params=_cp(("arbitrary",)),
    )(dx2, proj, mixed, y, dpool, gate, wg, scale, wo)


def _l0_bwd_mix(dx1, proj, hst, y, xc, cz, rg, ig_, gate, cw, wa, wx, lam, sw, wo):
    s_len = dx1.shape[0]
    ts = _tile(s_len, TS_MIX)
    n_t = s_len // ts
    hl, hb = SUBLANES, BF16_ROWS
    yb_w = 2 * D // N_CHIP

    def body(dx_ref, p_ref, h_ref, hh_ref, y_ref, xc_ref, cz_ref, r_ref, ig_ref, gate_ref, cw_ref, wa_ref, wx_ref,
             lam_ref, sw_ref, wo_ref, dp_ref, mt_ref, dwa_ref, dwx_ref, sm_ref, cg, cdxc, cdcz, ca):
        i = pl.program_id(0)
        ri = n_t - 1 - i

        @pl.when(i == 0)
        def _():
            cg[...] = jnp.zeros_like(cg)
            ca[...] = jnp.zeros_like(ca)
            cdxc[...] = jnp.zeros_like(cdxc)
            cdcz[...] = jnp.zeros_like(cdcz)
            sm_ref[...] = jnp.zeros_like(sm_ref)
            mt_ref[...] = jnp.zeros_like(mt_ref)
            dwa_ref[...] = jnp.zeros_like(dwa_ref)
            dwx_ref[...] = jnp.zeros_like(dwx_ref)

        dxb = dx_ref[...].astype(BF16)

        def wgrad_out(k):
            mt_ref[k] += lax.dot_general(y_ref[:, k * yb_w:(k + 1) * yb_w], dxb, TN, preferred_element_type=F32)

        wgrad_out(0)
        has_prev = (ri > 0).astype(F32)
        xa, ga, gbp, gcp, v, gb = [p_ref[:, k * D:(k + 1) * D].astype(F32) for k in range(6)]
        rows = _rows(ts, D)
        first = (rows == 0) & (ri == 0)
        xc = xc_ref[...].astype(F32)
        cz = cz_ref[...].astype(F32)
        r = r_ref[...].astype(F32)
        ig = ig_ref[...].astype(F32)
        sp = _softplus_neg(lam_ref[...])
        a, m, inv_m = _lru_decay(r, sp, first)
        z = gcp * v
        h = h_ref[...].astype(F32)
        hprev = _down(jnp.concatenate([hh_ref[...].astype(F32)[hb - hl:hb] * has_prev, h], axis=0), 1)[hl:hl + ts]
        dy = lax.dot_general((gate_ref[...] * dx_ref[...]).astype(BF16), wo_ref[...], NT, preferred_element_type=F32)
        dya_pre, dyb_pre = dy[:, 0:D], dy[:, D:2 * D]
        s_a, s_b = _sigmoid(ga), _sigmoid(gb)
        silu_a, silu_b = ga * s_a, gb * s_b
        dp_ref[:, D:2 * D] = (dya_pre * h * (s_a + silu_a * (1.0 - s_a))).astype(BF16)
        dp_ref[:, 5 * D:6 * D] = (dyb_pre * (gbp * cz) * (s_b + silu_b * (1.0 - s_b))).astype(BF16)
        dya = dya_pre * silu_a
        dyb = dyb_pre * silu_b
        wgrad_out(1)
        dp_ref[:, 2 * D:3 * D] = (dyb * cz).astype(BF16)
        dcz = dyb * gbp
        dcz_ext = jnp.concatenate([dcz, cdcz[...]], axis=0)
        dcz_taps = [_up(dcz_ext, 2 - k)[0:ts] for k in range(3)]
        for k in range(3):
            sm_ref[8 + k:9 + k, :] += jnp.sum(z * dcz_taps[k], axis=0, keepdims=True)
        dz = sum(sw_ref[k:k + 1, :] * dcz_taps[k] for k in range(3))
        dp_ref[:, 3 * D:4 * D] = (dz * v).astype(BF16)
        dp_ref[:, 4 * D:5 * D] = (dz * gcp).astype(BF16)
        cdcz[...] = dcz[0:hl, :]
        alpha = _up(jnp.concatenate([a, ca[...]], axis=0), 1)[0:ts]
        wgrad_out(2)
        dh = _run(_scan_rev_steps(alpha, dya, cg[0:1, :]))
        wgrad_out(3)
        cg[...] = dh[0:hl, :]
        ca[...] = a[0:hl, :]
        da = dh * hprev
        dhx = dh * xc
        dm = dhx * ig
        di = dhx * m
        dxc = dh * (m * ig)
        dl = a * (da - jnp.where(first, 0.0, dm * a * inv_m))
        dlr = dl * r
        sm_ref[7:8, :] += jnp.sum(dlr, axis=0, keepdims=True) * (-LRU_C)
        dpa = dlr * (sp * (-LRU_C)) * (1.0 - r)
        dpx = di * ig * (1.0 - ig)
        sm_ref[5:6, :] += jnp.sum(dpa, axis=0, keepdims=True)
        sm_ref[6:7, :] += jnp.sum(dpx, axis=0, keepdims=True)
        dpa_b, dpx_b, xc_b = dpa.astype(BF16), dpx.astype(BF16), xc.astype(BF16)
        back = []
        for hd in range(LRU_HEADS):
            sl = slice(hd * LRU_HEAD_DIM, (hd + 1) * LRU_HEAD_DIM)
            back.append(lax.dot_general(dpa_b[:, sl], wa_ref[hd], NT, preferred_element_type=F32)
                        + lax.dot_general(dpx_b[:, sl], wx_ref[hd], NT, preferred_element_type=F32))
            dwa_ref[hd] += lax.dot_general(xc_b[:, sl], dpa_b[:, sl], TN, preferred_element_type=F32)
            dwx_ref[hd] += lax.dot_general(xc_b[:, sl], dpx_b[:, sl], TN, preferred_element_type=F32)
        dxc = dxc + jnp.concatenate(back, axis=1)
        sm_ref[4:5, :] += jnp.sum(dxc, axis=0, keepdims=True)
        dxc_ext = jnp.concatenate([dxc, cdxc[...]], axis=0)
        dxc_taps = [_up(dxc_ext, 3 - k)[0:ts] for k in range(4)]
        for k in range(4):
            sm_ref[k:k + 1, :] += jnp.sum(xa * dxc_taps[k], axis=0, keepdims=True)
        dp_ref[:, 0:D] = sum(cw_ref[k:k + 1, :] * dxc_taps[k] for k in range(4)).astype(BF16)
        cdxc[...] = dxc[0:hl, :]

    def full(a):
        return pl.BlockSpec(a.shape, lambda i: (0,) * a.ndim)

    rev = lambda w: pl.BlockSpec((ts, w), lambda i: (n_t - 1 - i, 0))
    halo = lambda w: pl.BlockSpec((hb, w), lambda i: (jnp.maximum((n_t - 1 - i) * (ts // hb) - 1, 0), 0))
    return pl.pallas_call(
        body, name="l0_bwd_mix", grid=(n_t,),
        in_specs=[rev(D), rev(6 * D), rev(D), halo(D), rev(2 * D), rev(D), rev(D), rev(D), rev(D)]
        + [full(a) for a in (gate, cw, wa, wx, lam, sw, wo)],
        out_specs=[rev(6 * D), pl.BlockSpec((N_CHIP, yb_w, D), lambda i: (0, 0, 0)),
                   pl.BlockSpec(wa.shape, lambda i: (0, 0, 0)), pl.BlockSpec(wa.shape, lambda i: (0, 0, 0)),
                   pl.BlockSpec((2 * SUBLANES, D), lambda i: (0, 0))],
        out_shape=[jax.ShapeDtypeStruct((s_len, 6 * D), BF16), jax.ShapeDtypeStruct((N_CHIP, yb_w, D), F32),
                   jax.ShapeDtypeStruct(wa.shape, F32), jax.ShapeDtypeStruct(wa.shape, F32),
                   jax.ShapeDtypeStruct((2 * SUBLANES, D), F32)],
        scratch_shapes=[pltpu.VMEM((hl, D), F32)] * 4,
        compiler_params=_cp(("arbitrary",)),
    )(dx1, proj, hst, hst, y, xc, cz, rg, ig_, gate, cw, wa, wx, lam, sw, wo)


def _dgrad_norm(dproj, w, x, dres, g, sc, name, after=None):
    s_len, nb = x.shape[0], w.shape[2]
    ts = _tile(s_len, TS_DGRAD)
    order = [] if after is None else [after]

    def body(dp_ref, w_ref, x_ref, dr_ref, g_ref, sc_ref, *rest):
        dx_ref, s1_ref, s2_ref = rest[len(order):]

        @pl.when(pl.program_id(0) == 0)
        def _():
            s1_ref[...] = jnp.zeros_like(s1_ref)
            s2_ref[...] = jnp.zeros_like(s2_ref)

        dh = sum(lax.dot_general(dp_ref[:, k * nb:(k + 1) * nb], w_ref[k], NT, preferred_element_type=F32)
                 for k in range(N_CHIP))
        xv = x_ref[...]
        r = lax.rsqrt(jnp.mean(xv * xv, axis=-1, keepdims=True) + RMS_EPS)
        n = xv * r
        s1_ref[...] += jnp.sum(dh, axis=0, keepdims=True)
        s2_ref[...] += jnp.sum(dh * n, axis=0, keepdims=True)
        dn = dh * (g_ref[...] * (1.0 + sc_ref[...]))
        dx_ref[...] = dr_ref[...] + r * (dn - n * jnp.mean(dn * n, axis=-1, keepdims=True))

    row = lambda wd: pl.BlockSpec((ts, wd), lambda i: (i, 0))
    vec = pl.BlockSpec((1, D), lambda i: (0, 0))
    return pl.pallas_call(
        body, name=name, grid=(s_len // ts,),
        in_specs=[row(N_CHIP * nb), pl.BlockSpec(w.shape, lambda i: (0, 0, 0)), row(D), row(D), vec, vec]
        + [ANY] * len(order),
        out_specs=[row(D), vec, vec],
        out_shape=[jax.ShapeDtypeStruct((s_len, D), F32)] + [jax.ShapeDtypeStruct((1, D), F32)] * 2,
        compiler_params=_cp(("arbitrary",)),
    )(dproj, w, x, dres, g, sc, *order)


def _wgrad(a, b, groups, ka, nb, a_col, b_col, name, after=None):
    s_len = a.shape[0]
    ts = _tile(s_len, TS_WGRAD * (2 if ka * nb <= D * D else 1))
    n_s = s_len // ts
    order = [] if after is None else [after]

    def body(a_ref, b_ref, *rest):
        o_ref, wire_ref = rest[-2:]

        @pl.when(pl.program_id(1) == 0)
        def _():
            o_ref[...] = jnp.zeros_like(o_ref)

        o_ref[...] += lax.dot_general(a_ref[...].astype(BF16), b_ref[...].astype(BF16), TN, preferred_element_type=F32)

        @pl.when(pl.program_id(1) == n_s - 1)
        def _():
            wire_ref[...] = o_ref[...].astype(GRAD_WIRE_DTYPE)

    blk = pl.BlockSpec((None, ka, nb), lambda g, s: (g, 0, 0))
    return pl.pallas_call(
        body, name=name, grid=(groups, n_s),
        in_specs=[pl.BlockSpec((ts, ka), lambda g, s: (s, a_col(g))), pl.BlockSpec((ts, nb), lambda g, s: (s, b_col(g)))]
        + [ANY] * len(order),
        out_specs=[blk, blk],
        out_shape=[jax.ShapeDtypeStruct((groups, ka, nb), F32), jax.ShapeDtypeStruct((groups, ka, nb), GRAD_WIRE_DTYPE)],
        compiler_params=_cp(("parallel", "arbitrary")),
    )(a, b, *order)


def _wo_final(mt, wo, gate, name):
    rb = mt.shape[1]

    def body(m_ref, w_ref, gate_ref, dw_ref, wire_ref, dg_ref):
        @pl.when(pl.program_id(0) == 0)
        def _():
            dg_ref[...] = jnp.zeros_like(dg_ref)

        mv = m_ref[...]
        dw = mv * gate_ref[...]
        dw_ref[...] = dw
        wire_ref[...] = dw.astype(GRAD_WIRE_DTYPE)
        dg_ref[...] += jnp.sum(mv * w_ref[...].astype(F32), axis=0, keepdims=True)

    blk = pl.BlockSpec((None, rb, D), lambda k: (k, 0, 0))
    vec = pl.BlockSpec((1, D), lambda k: (0, 0))
    return pl.pallas_call(
        body, name=name, grid=(N_CHIP,), in_specs=[blk, blk, vec], out_specs=[blk, blk, vec],
        out_shape=[jax.ShapeDtypeStruct(mt.shape, F32), jax.ShapeDtypeStruct(mt.shape, GRAD_WIRE_DTYPE),
                   jax.ShapeDtypeStruct((1, D), F32)],
        compiler_params=_cp(("arbitrary",)),
    )(mt, wo, gate)


ROW_NORM_G, ROW_CONV_W, ROW_CONV_B, ROW_B_A, ROW_B_X, ROW_LAMBDA, ROW_SC_W, ROW_POOL_B, ROW_POOL_S, ROW_FINAL_G = (
    0, 2, 6, 7, 8, 9, 10, 13, 15, 17)
ROW_LOSS = 18
DMOD_W = 6 * D // SUBLANES


def _small_pack(s1_0, s2_0, s1_1, s2_1, sm0, dsc1, dbg1, dgf, losscols, dgate0, dgate1, norm_g, sc0, sc1, lam):
    def body(s1_0r, s2_0r, s1_1r, s2_1r, sm, dsc, dbg, dgfr, lcols, dg0, dg1, ng, sc0r, sc1r, lamr, buf, dmod):
        buf[...] = jnp.zeros_like(buf)
        buf[0:1, :] = s2_0r[...] * (1.0 + sc0r[...])
        buf[1:2, :] = s2_1r[...] * (1.0 + sc1r[...])
        buf[ROW_CONV_W:ROW_CONV_W + 4, :] = sm[0:4, :]
        buf[ROW_CONV_B:ROW_CONV_B + 1, :] = sm[4:5, :]
        buf[ROW_B_A:ROW_B_A + 1, :] = sm[5:6, :]
        buf[ROW_B_X:ROW_B_X + 1, :] = sm[6:7, :]
        buf[ROW_LAMBDA:ROW_LAMBDA + 1, :] = -sm[7:8, :] * _sigmoid(-lamr[...])
        buf[ROW_SC_W:ROW_SC_W + 3, :] = sm[8:11, :]
        for k in range(2):
            buf[ROW_POOL_B + k:ROW_POOL_B + k + 1, :] = dbg[:, k * D:(k + 1) * D]
            buf[ROW_POOL_S + k:ROW_POOL_S + k + 1, :] = dsc[:, k * D:(k + 1) * D]
        buf[ROW_FINAL_G:ROW_FINAL_G + 1, :] = dgfr[...]
        pieces = (s1_0r[...], s2_0r[...] * ng[0:1, :], dg0[...], s1_1r[...], s2_1r[...] * ng[1:2, :], dg1[...])
        flat = jnp.concatenate(pieces, axis=1)
        for r in range(SUBLANES):
            dmod[r:r + 1, :] = flat[:, r * DMOD_W:(r + 1) * DMOD_W]
        buf[ROW_LOSS:ROW_LOSS + 1, :] = jnp.broadcast_to(jnp.sum(lcols[...], axis=1, keepdims=True) * (0.5 / D), (1, D))

    args = (s1_0, s2_0, s1_1, s2_1, sm0, dsc1, dbg1, dgf, losscols, dgate0, dgate1, norm_g, sc0, sc1, lam)
    return pl.pallas_call(
        body, name="small_pack", in_specs=[VMEM] * len(args), out_specs=[VMEM] * 2,
        out_shape=[jax.ShapeDtypeStruct((SMALL_ROWS, D), F32), jax.ShapeDtypeStruct((SUBLANES, DMOD_W), F32)],
        compiler_params=_cp(),
    )(*args)


def _small_comm(buf_a, buf_b, dmod8):
    ra, rb = buf_a.shape[0] // N_DEV, buf_b.shape[0] // N_DEV
    wb = buf_b.shape[1]

    def body(a_ref, b_ref, dm_ref, oa_ref, ob_ref, odm_ref, ina, inb, dslot, sa, sb, s1, r1, s2, r2):
        x, y, c = _pos()
        me = 4 * x + 2 * y + c
        peers = []
        for r in range(1, N_DEV):
            fx, fy, fc = (r >> 2) & 1, (r >> 1) & 1, r & 1
            px, py, pc = _flip(x, fx), _flip(y, fy), _flip(c, fc)
            peers.append(((px, py, pc), 4 * px + 2 * py + pc))
        seg_a = lambda d: pl.ds(pl.multiple_of(d * ra, SUBLANES), ra)
        seg_b = lambda d: pl.ds(pl.multiple_of(d * rb, SUBLANES), rb)
        first = []
        for r, (peer, pid) in enumerate(peers):
            for k, (src, dst) in enumerate(((a_ref.at[seg_a(pid), :], ina.at[r]), (b_ref.at[seg_b(pid), :], inb.at[r]),
                                            (dm_ref, dslot.at[me]))):
                cp = pltpu.make_async_remote_copy(src_ref=src, dst_ref=dst, send_sem=s1.at[3 * r + k],
                                                  recv_sem=r1.at[3 * r + k], device_id=peer, device_id_type=MESH)
                cp.start()
                first.append(cp)
        dslot[me] = dm_ref[...]
        for cp in first:
            cp.wait()
        acc_a, acc_b = a_ref[seg_a(me), :], b_ref[seg_b(me), :]
        for r in range(N_DEV - 1):
            acc_a = acc_a + ina[r]
            acc_b = acc_b + inb[r]
        sa[...] = acc_a
        sb[...] = acc_b
        oa_ref[seg_a(me), :] = acc_a
        ob_ref[seg_b(me), :] = acc_b
        second = []
        for r, (peer, pid) in enumerate(peers):
            for k, (src, dst) in enumerate(((sa, oa_ref.at[seg_a(me), :]), (sb, ob_ref.at[seg_b(me), :]))):
                cp = pltpu.make_async_remote_copy(src_ref=src, dst_ref=dst, send_sem=s2.at[2 * r + k],
                                                  recv_sem=r2.at[2 * r + k], device_id=peer, device_id_type=MESH)
                cp.start()
                second.append(cp)
        odm_ref[...] = dslot[...]
        for cp in second:
            cp.wait()

    nrel = N_DEV - 1
    return pl.pallas_call(
        body, name="small_comm", in_specs=[VMEM] * 3, out_specs=[VMEM] * 3,
        out_shape=[jax.ShapeDtypeStruct(buf_a.shape, F32), jax.ShapeDtypeStruct(buf_b.shape, F32),
                   jax.ShapeDtypeStruct((N_DEV,) + dmod8.shape, F32)],
        scratch_shapes=[pltpu.VMEM((nrel, ra, D), F32), pltpu.VMEM((nrel, rb, wb), F32),
                        pltpu.VMEM((N_DEV,) + dmod8.shape, F32), pltpu.VMEM((ra, D), F32), pltpu.VMEM((rb, wb), F32),
                        pltpu.SemaphoreType.DMA((3 * nrel,)), pltpu.SemaphoreType.DMA((3 * nrel,)),
                        pltpu.SemaphoreType.DMA((2 * nrel,)), pltpu.SemaphoreType.DMA((2 * nrel,))],
        compiler_params=_cp(),
    )(buf_a, buf_b, dmod8)


def _adam(w, g, m, v):
    m2 = ADAM_B1 * m + (1.0 - ADAM_B1) * g
    v2 = ADAM_B2 * v + (1.0 - ADAM_B2) * (g * g)
    m_hat = m2 / (1.0 - ADAM_B1 ** ADAM_STEP)
    v_hat = v2 / (1.0 - ADAM_B2 ** ADAM_STEP)
    return -ADAM_LR * (m_hat / (jnp.sqrt(v_hat) + ADAM_EPS) + ADAM_WD * w), m2, v2


def _small_adam(red_a, red_b, dm_all, params):
    n = len(params)

    def body(*refs):
        ra, rb, dm = refs[:3]
        wmv = refs[3:3 + 3 * n]
        outs = refs[3 + 3 * n:]
        x, y, _ = _pos()
        chip = 2 * x + y

        def shard(row0, nrows, width):
            per_row = D // width
            cands = []
            for k in range(N_CHIP):
                if nrows == 1 or per_row >= N_CHIP:
                    cands.append(ra[row0:row0 + nrows, k * width:(k + 1) * width])
                else:
                    rr, cc = divmod(k * width, D)
                    cands.append(ra[row0 + rr:row0 + rr + 1, cc:cc + width])
            g = cands[0]
            for k in range(1, N_CHIP):
                g = jnp.where(chip == k, cands[k], g)
            return g

        dms = jnp.sum(dm[...], axis=0)
        hw = LRU_HEADS * LRU_HEAD_DIM
        grads = [
            ra[ROW_NORM_G:ROW_NORM_G + 2, :],
            None,
            shard(ROW_CONV_W, 4, D // N_CHIP),
            ra[ROW_CONV_B:ROW_CONV_B + 1, :],
            rb[0:hw, :],
            ra[ROW_B_A:ROW_B_A + 1, :],
            rb[hw:2 * hw, :],
            ra[ROW_B_X:ROW_B_X + 1, :],
            ra[ROW_LAMBDA:ROW_LAMBDA + 1, :],
            shard(ROW_SC_W, 3, D // N_CHIP),
            shard(ROW_POOL_B, 2, 2 * D // N_CHIP),
            shard(ROW_POOL_S, 2, 2 * D // N_CHIP),
            ra[ROW_FINAL_G:ROW_FINAL_G + 1, :],
        ]
        for p in range(n):
            w_ref, m_ref, v_ref = wmv[3 * p:3 * p + 3]
            g_out, d_out, m_out, v_out = outs[4 * p:4 * p + 4]
            if grads[p] is None:
                for r in range(SUBLANES):
                    l, cols = r // N_CHIP, slice((r % N_CHIP) * DMOD_W, (r % N_CHIP + 1) * DMOD_W)
                    g = dms[r:r + 1, :]
                    dl, m2, v2 = _adam(w_ref[l:l + 1, cols], g, m_ref[l:l + 1, cols], v_ref[l:l + 1, cols])
                    g_out[l:l + 1, cols] = g
                    d_out[l:l + 1, cols] = dl
                    m_out[l:l + 1, cols] = m2
                    v_out[l:l + 1, cols] = v2
            else:
                g = grads[p]
                dl, m2, v2 = _adam(w_ref[...], g, m_ref[...], v_ref[...])
                g_out[...] = g
                d_out[...] = dl
                m_out[...] = m2
                v_out[...] = v2

    flat = [a for p in params for a in p]
    return pl.pallas_call(
        body, name="small_adam", in_specs=[VMEM] * (3 + len(flat)), out_specs=[VMEM] * (4 * n),
        out_shape=[jax.ShapeDtypeStruct(p[0].shape, F32) for p in params for _ in range(4)],
        compiler_params=_cp(),
    )(red_a, red_b, dm_all, *flat)


def _modw_adam(ca_t, dm_sh, w, m, v):
    nw = w.shape[2]

    def body(c_ref, d_ref, w_ref, m_ref, v_ref, g_out, d_out, m_out, v_out):
        g = jnp.dot(c_ref[...], d_ref[...], precision=lax.Precision.HIGHEST, preferred_element_type=F32)
        dl, m2, v2 = _adam(w_ref[...], g, m_ref[...], v_ref[...])
        g_out[...] = g
        d_out[...] = dl
        m_out[...] = m2
        v_out[...] = v2

    blk = pl.BlockSpec((None, D, nw), lambda l: (l, 0, 0))
    return pl.pallas_call(
        body, name="modw_adam", grid=(2,),
        in_specs=[pl.BlockSpec((D, SUBLANES), lambda l: (0, 0)), pl.BlockSpec((None, SUBLANES, nw), lambda l: (l, 0, 0)),
                  blk, blk, blk],
        out_specs=[blk] * 4, out_shape=[jax.ShapeDtypeStruct(w.shape, F32)] * 4,
        compiler_params=_cp(("arbitrary",)),
    )(ca_t, dm_sh, w, m, v)


def _exchange(copies, name, out_type, n_sems, args, sequencer, after=None):
    order = [] if after is None else [after]
    n_in, n_out = len(args) + len(order), len(out_type)

    def body(*refs):
        barrier = pltpu.get_barrier_semaphore()
        peers = sequencer[1](*_pos())
        for peer in peers:
            pl.semaphore_signal(barrier, inc=1, device_id=peer, device_id_type=MESH)
        pl.semaphore_wait(barrier, len(peers))
        copies(refs[:n_in], refs[n_in:n_in + n_out], refs[n_in + n_out], refs[n_in + n_out + 1])

    sems = [pltpu.SemaphoreType.DMA((n_sems,))] * 2
    return pl.kernel(body, out_type, mesh=plsc.ScalarSubcoreMesh(axis_name="sequencer", num_cores=1), name=name,
                     scratch_types=sems, compiler_params=pltpu.CompilerParams(collective_id=sequencer[0]))(*args, *order)


def _sibling(x, y, c):
    return [(x, y, 1 - c)]


def _other_chips(x, y, c):
    return [(1 - x, y, c), (x, 1 - y, c), (1 - x, 1 - y, c)]


def _to_wire(g, name, after=None):
    _, rr, cc = g.shape
    rb = min(rr, 256)

    def body(g_ref, *rest):
        rest[-1][...] = g_ref[...].astype(GRAD_WIRE_DTYPE)

    order = [] if after is None else [after]
    blk = pl.BlockSpec((None, rb, cc), lambda k, j: (k, j, 0))
    return pl.pallas_call(
        body, name=name, grid=(N_CHIP, rr // rb), in_specs=[blk] + [ANY] * len(order), out_specs=blk,
        out_shape=jax.ShapeDtypeStruct(g.shape, GRAD_WIRE_DTYPE), compiler_params=_cp(("parallel", "parallel")),
    )(g, *order)


def _chip_scatter(ps, name, collective_id, after=None):
    n = len(ps)

    def copies(ins, outs, ssem, rsem):
        x, y, c = _pos()
        cps = []
        for a in range(n):
            for q, (fx, fy) in enumerate(((1, 0), (0, 1), (1, 1))):
                px, py = _flip(x, fx), _flip(y, fy)
                cp = pltpu.make_async_remote_copy(
                    src_ref=ins[a].at[2 * px + py], dst_ref=outs[a].at[q],
                    send_sem=ssem.at[3 * a + q], recv_sem=rsem.at[3 * a + q], device_id=(px, py, c), device_id_type=MESH)
                cp.start()
                cps.append(cp)
        for cp in cps:
            cp.wait()

    out_type = [jax.ShapeDtypeStruct((N_CHIP - 1,) + p.shape[1:], p.dtype) for p in ps]
    return _exchange(copies, name, out_type, 3 * n, ps, (collective_id, _other_chips), after)


def _add_owner(p, got, chipidx, name, after=None):
    _, hr, cc = p.shape
    rb = min(hr, 256)

    def body(k_ref, p_ref, r_ref, *rest):
        rest[-1][...] = ((p_ref[...].astype(F32) + r_ref[0].astype(F32)) + r_ref[1].astype(F32)) + r_ref[2].astype(F32)

    order = [] if after is None else [after]
    return pl.pallas_call(
        body, name=name,
        grid_spec=pltpu.PrefetchScalarGridSpec(
            num_scalar_prefetch=1, grid=(hr // rb,),
            in_specs=[pl.BlockSpec((None, rb, cc), lambda j, k_ref: (k_ref[0], j, 0)),
                      pl.BlockSpec((N_CHIP - 1, rb, cc), lambda j, k_ref: (0, j, 0))] + [ANY] * len(order),
            out_specs=pl.BlockSpec((rb, cc), lambda j, k_ref: (j, 0))),
        out_shape=jax.ShapeDtypeStruct((hr, cc), F32),
        compiler_params=_cp(("parallel",)),
    )(chipidx, p, got, *order)


def _sib_exchange(ts_, name, collective_id, after=None):
    n = len(ts_)

    def copies(ins, outs, ssem, rsem):
        x, y, c = _pos()
        cps = []
        for a in range(n):
            cp = pltpu.make_async_remote_copy(src_ref=ins[a], dst_ref=outs[a], send_sem=ssem.at[a],
                                              recv_sem=rsem.at[a], device_id=(x, y, 1 - c), device_id_type=MESH)
            cp.start()
            cps.append(cp)
        for cp in cps:
            cp.wait()

    out_type = [jax.ShapeDtypeStruct(t.shape, F32) for t in ts_]
    return _exchange(copies, name, out_type, n, ts_, (collective_id, _sibling), after)


def _adam_2d(w, g_own, g_sib, m, v, name):
    rr, cc = w.shape
    rb = min(rr, 256)

    def body(w_ref, go_ref, gs_ref, m_ref, v_ref, g_out, d_out, m_out, v_out):
        g = go_ref[...] + gs_ref[...]
        dl, m2, v2 = _adam(w_ref[...], g, m_ref[...], v_ref[...])
        g_out[...] = g
        d_out[...] = dl
        m_out[...] = m2
        v_out[...] = v2

    blk = pl.BlockSpec((rb, cc), lambda j: (j, 0))
    return pl.pallas_call(
        body, name=name, grid=(rr // rb,), in_specs=[blk] * 5, out_specs=[blk] * 4,
        out_shape=[jax.ShapeDtypeStruct((rr, cc), F32)] * 4, compiler_params=_cp(("parallel",)),
    )(w, g_own, g_sib, m, v)


def kernel(x, c, norm_g, mod_w, mod_b, hy_w_in, hy_conv_w, hy_conv_b, lru_w_a, lru_b_a, lru_w_x, lru_b_x, lru_lambda, sc_conv_w, hy_w_out, pool_w_in, pool_w_grp, pool_b_grp, pool_scale, pool_w_out, final_g, loss_target, m_norm_g, m_mod_w, m_mod_b, m_hy_w_in, m_hy_conv_w, m_hy_conv_b, m_lru_w_a, m_lru_b_a, m_lru_w_x, m_lru_b_x, m_lru_lambda, m_sc_conv_w, m_hy_w_out, m_pool_w_in, m_pool_w_grp, m_pool_b_grp, m_pool_scale, m_pool_w_out, m_final_g, v_norm_g, v_mod_w, v_mod_b, v_hy_w_in, v_hy_conv_w, v_hy_conv_b, v_lru_w_a, v_lru_b_a, v_lru_w_x, v_lru_b_x, v_lru_lambda, v_sc_conv_w, v_hy_w_out, v_pool_w_in, v_pool_w_grp, v_pool_b_grp, v_pool_scale, v_pool_w_out, v_final_g):
    ax, ay, ac = _pos()
    me = 4 * ax + 2 * ay + ac
    chip = 2 * ax + ay
    xs = x[0]
    tgt = loss_target[0]
    gd = POOL_GROUP_DIM
    kidx = chip.reshape(1).astype(jnp.int32)

    big = [hy_w_in[0], hy_w_out[0], pool_w_in[0], pool_w_grp[0].reshape(4 * 128, gd), pool_w_out[0]]
    w_in0, w_out0 = _wgather_sequencer(
        [_wcast_own_block(w, kidx, f"wcast_own_block_{a}") for a, w in enumerate(big[:2])], "wgather_l0", CIDS_WGATHER[0])

    ca_all, mod_all, small_w = _mod_fwd(jnp.broadcast_to(c, (SUBLANES, D)), mod_w, mod_b,
                                        hy_conv_w[0], sc_conv_w[0], pool_b_grp, pool_scale)
    mod_me = lax.dynamic_index_in_dim(mod_all, me, axis=1, keepdims=False)
    sh0, sc0, gt0 = (mod_me[0:1, k * D:(k + 1) * D] for k in range(3))
    sh1, sc1, gt1 = (mod_me[1:2, k * D:(k + 1) * D] for k in range(3))
    cw = small_w[SW_CONV:SW_CONV + 4, 0:D]
    sw = small_w[SW_SC:SW_SC + 3, 0:D]
    pool_b = small_w[SW_POOL_B:SW_POOL_B + 1, :]
    pool_s = small_w[SW_POOL_S:SW_POOL_S + 1, :]
    g0, g1, gf = norm_g[0:1], norm_g[1:2], final_g.reshape(1, D)
    cb, ba, bx, lam = hy_conv_b, lru_b_a, lru_b_x, lru_lambda

    h0 = _norm_mod(xs, g0, sc0, sh0, "l0_norm")
    wa_b, wx_b = _wcast([lru_w_a[0], lru_w_x[0]])
    w_in1, w_grp, w_out1 = _wgather_sequencer(
        [_wcast_own_block(w, kidx, f"wcast_own_block_{a + 2}", after=(w_out0, h0)) for a, w in enumerate(big[2:])],
        "wgather_l1", CIDS_WGATHER[1])
    w_grp =w_grp.reshape(N_CHIP, 4, 128, gd).transpose(1, 0, 2, 3).reshape(4, gd, gd)

    x1, hst, y0, xc0, cz0, rg0, ig0, proj0 = _l0_fwd(h0, xs, w_in0, gt0, cw, cb, wa_b, ba, wx_b, bx, lam, sw,
                                                     w_out0.reshape(2 * D, D))
    dpool, mixed, y1, dx2, losscols, dgf, h1, proj1 = _l1_fwd(x1, g1, sc1, sh1, w_in1, tgt, gt1, w_grp, pool_b, pool_s,
                                                              w_out1.reshape(2 * D, D), gf)

    def add_owners(grads, got, tag, ids, after):
        own = []
        for a, (g, r) in enumerate(zip(grads, got)):
            own.append(_add_owner(g, r, kidx, f"grad_add_owner_{tag}{a}", own[-1] if own else after))
        return own, _sib_exchange(own, f"grad_sib_exchange_{tag}", ids[1])

    dproj1, mt1, d_wgrp, dsc1, dbg1 = _l1_bwd_mix(dx2, proj1, mixed, y1, dpool, gt1, w_grp, pool_s,
                                                  w_out1.reshape(2 * D, D))
    d_win1, wire_win1 = _wgrad(h1, dproj1, N_CHIP, D, D, lambda g: 0, lambda g: g, "l1_wgrad_in")
    d_wout1, wire_wout1, dgate1 = _wo_final(mt1, w_out1, gt1, "l1_wo_final")
    d_wgrp = d_wgrp.reshape(4, N_CHIP, 128, gd).transpose(1, 0, 2, 3).reshape(N_CHIP, 4 * 128, gd)
    grads_l1 = [d_win1, d_wgrp, d_wout1]
    got_l1 = _chip_scatter([wire_win1, _to_wire(d_wgrp, "grad_to_wire_grp"), wire_wout1], "grad_chip_scatter_l1",
                           CIDS_L1[0])
    dx1, s1_1, s2_1 = _dgrad_norm(dproj1, w_in1, x1, dx2, g1, sc1, "l1_bwd_proj")

    dproj0, mt0, d_wa, d_wx, sm0 = _l0_bwd_mix(dx1, proj0, hst, y0, xc0, cz0, rg0, ig0, gt0, cw, wa_b, wx_b, lam, sw,
                                               w_out0.reshape(2 * D, D))
    sums_l1, sib_l1 = add_owners(grads_l1, got_l1, "l1", CIDS_L1, after=sm0)
    d_win0, wire_win0 = _wgrad(h0, dproj0, N_CHIP, D, 6 * D // N_CHIP, lambda g: 0, lambda g: g, "l0_wgrad_in",
                               after=sums_l1[-1])
    d_wout0, wire_wout0, dgate0 = _wo_final(mt0, w_out0, gt0, "l0_wo_final")
    grads_l0 = [d_win0, d_wout0]
    got_l0 = _chip_scatter([wire_win0, wire_wout0], "grad_chip_scatter_l0", CIDS_L0[0], after=sib_l1[0])
    grad_x, s1_0, s2_0 = _dgrad_norm(dproj0, w_in0, xs, dx1, g0, sc0, "l0_bwd_proj", after=wire_win0)
    sums_l0, sib_l0 = add_owners(grads_l0, got_l0, "l0", CIDS_L0, after=s1_0)

    buf_a, dmod8 = _small_pack(s1_0, s2_0, s1_1, s2_1, sm0, dsc1, dbg1, dgf, losscols, dgate0, dgate1,
                                      norm_g, sc0, sc1, lam)
    hw = LRU_HEADS * LRU_HEAD_DIM
    buf_b = jnp.concatenate([d_wa.reshape(hw, LRU_HEAD_DIM), d_wx.reshape(hw, LRU_HEAD_DIM)], axis=0)
    red_a, red_b, dm_all = _small_comm(buf_a, buf_b, dmod8)
    small = [(norm_g, m_norm_g, v_norm_g), (mod_b, m_mod_b, v_mod_b),
             (hy_conv_w[0], m_hy_conv_w[0], v_hy_conv_w[0]), (hy_conv_b, m_hy_conv_b, v_hy_conv_b),
             tuple(a.reshape(hw, LRU_HEAD_DIM) for a in (lru_w_a, m_lru_w_a, v_lru_w_a)),
             (lru_b_a, m_lru_b_a, v_lru_b_a),
             tuple(a.reshape(hw, LRU_HEAD_DIM) for a in (lru_w_x, m_lru_w_x, v_lru_w_x)),
             (lru_b_x, m_lru_b_x, v_lru_b_x), (lru_lambda, m_lru_lambda, v_lru_lambda),
             (sc_conv_w[0], m_sc_conv_w[0], v_sc_conv_w[0]), (pool_b_grp, m_pool_b_grp, v_pool_b_grp),
             (pool_scale, m_pool_scale, v_pool_scale),
             tuple(a.reshape(1, D) for a in (final_g, m_final_g, v_final_g))]
    small_names = ["norm_g", "mod_b", "hy_conv_w", "hy_conv_b", "lru_w_a", "lru_b_a", "lru_w_x", "lru_b_x",
                   "lru_lambda", "sc_conv_w", "pool_b_grp", "pool_scale", "final_g"]
    small_out = _small_adam(red_a, red_b, dm_all, small)
    res = {}
    shapes = dict(norm_g=norm_g, mod_b=mod_b, hy_conv_w=hy_conv_w, hy_conv_b=hy_conv_b, lru_w_a=lru_w_a, lru_b_a=lru_b_a,
                  lru_w_x=lru_w_x, lru_b_x=lru_b_x, lru_lambda=lru_lambda, sc_conv_w=sc_conv_w, pool_b_grp=pool_b_grp,
                  pool_scale=pool_scale, final_g=final_g)
    for p, nm in enumerate(small_names):
        res[nm] = tuple(o.reshape(shapes[nm].shape) for o in small_out[4 * p:4 * p + 4])

    nw = mod_w.shape[2]
    assert nw == DMOD_W
    dm_sh = jnp.stack([lax.dynamic_index_in_dim(dm_all, N_CHIP * l + chip, axis=1, keepdims=False) for l in range(2)])
    res["mod_w"] = tuple(_modw_adam(ca_all.T, dm_sh, mod_w, m_mod_w, v_mod_w))

    sums = list(sums_l0) + list(sums_l1)
    sib_sums = list(sib_l0) + list(sib_l1)
    big_names = ["hy_w_in", "hy_w_out", "pool_w_in", "pool_w_grp", "pool_w_out"]
    big_wmv = [(hy_w_in, m_hy_w_in, v_hy_w_in), (hy_w_out, m_hy_w_out, v_hy_w_out), (pool_w_in, m_pool_w_in, v_pool_w_in),
               (pool_w_grp, m_pool_w_grp, v_pool_w_grp), (pool_w_out, m_pool_w_out, v_pool_w_out)]
    for a, nm in enumerate(big_names):
        rr, cc = big[a].shape
        w, m, v = (t.reshape(rr, cc) for t in big_wmv[a])
        outs = _adam_2d(w, sums[a], sib_sums[a], m, v, f"adam_{nm}")
        res[nm] = tuple(o.reshape(big_wmv[a][0].shape) for o in outs)

    loss = red_a[ROW_LOSS, 0]
    order = ["norm_g", "mod_w", "mod_b", "hy_w_in", "hy_conv_w", "hy_conv_b", "lru_w_a", "lru_b_a", "lru_w_x", "lru_b_x",
             "lru_lambda", "sc_conv_w", "hy_w_out", "pool_w_in", "pool_w_grp", "pool_b_grp", "pool_scale", "pool_w_out",
             "final_g"]
    return (loss, grad_x[None], *[res[nm][0] for nm in order], *[res[nm][1] for nm in order],
            *[res[nm][2] for nm in order], *[res[nm][3] for nm in order])
```

```python
import jax
import jax.numpy as jnp
from jax import lax
from jax.experimental import pallas as pl
from jax.experimental.pallas import tpu as pltpu
from jax.experimental.pallas import tpu_sc as plsc

F32, BF16 = jnp.float32, jnp.bfloat16
D = 1024
RMS_EPS = 1e-6
SQRT_FLOOR = 1e-30
LRU_C = 8.0
LRU_HEADS, LRU_HEAD_DIM = 8, 128
POOL_WINDOWS = (2, 4, 8, 16)
POOL_GROUP_DIM = 512
ADAM_LR, ADAM_B1, ADAM_B2, ADAM_EPS, ADAM_WD, ADAM_STEP = 0.001, 0.9, 0.999, 1e-08, 0.01, 10
MESH = pl.DeviceIdType.MESH
CIDS_WGATHER = (1, 8)
CIDS_L1 = (2, 3)
CIDS_L0 = (4, 5)
N_DEV, N_CHIP = 8, 4
SUBLANES = 8
BF16_ROWS = 16
POOL_HALO = 16
TS_MIX, TS_WGRAD, TS_DGRAD = 256, 2048, 512
SMALL_ROWS = 64
GRAD_WIRE_DTYPE = BF16
ANY = pl.BlockSpec(memory_space=pl.ANY)
VMEM = pl.BlockSpec(memory_space=pltpu.VMEM)
NT = (((1,), (1,)), ((), ()))
TN = (((0,), (0,)), ((), ()))


def _cp(sem=None, vmem_mb=56):
    kw = dict(vmem_limit_bytes=vmem_mb * 2 ** 20)
    if sem is not None:
        kw["dimension_semantics"] = sem
    return pltpu.CompilerParams(**kw)


def _tile(n, t):
    return min(n, t)


def _pos():
    return lax.axis_index("x"), lax.axis_index("y"), lax.axis_index("c")


def _flip(v, f):
    return 1 - v if f else v


def _sigmoid(z):
    return 0.5 * jnp.tanh(0.5 * z) + 0.5


def _rows(n, c):
    return lax.broadcasted_iota(jnp.int32, (n, c), 0)


def _down(a, d):
    return a if d == 0 else pltpu.roll(a, d, 0)


def _up(a, d):
    return a if d == 0 else pltpu.roll(a, a.shape[0] - d, 0)


def _scan_fwd_steps(a, u, carry):
    n, c = a.shape
    sub = _rows(SUBLANES, c)
    out = []
    for k in range(n // SUBLANES):
        p = a[k * SUBLANES:(k + 1) * SUBLANES]
        g = u[k * SUBLANES:(k + 1) * SUBLANES]
        for d in (1, 2, 4):
            keep = sub >= d
            g = g + p * jnp.where(keep, pltpu.roll(g, d, 0), 0.0)
            p = p * jnp.where(keep, pltpu.roll(p, d, 0), 1.0)
        h = g + p * carry
        carry = h[SUBLANES - 1:SUBLANES, :]
        out.append(h)
        yield
    return jnp.concatenate(out, axis=0)


def _scan_rev_steps(alpha, b, carry):
    n, c = alpha.shape
    sub = _rows(SUBLANES, c)
    out = []
    for k in reversed(range(n // SUBLANES)):
        p = alpha[k * SUBLANES:(k + 1) * SUBLANES]
        g = b[k * SUBLANES:(k + 1) * SUBLANES]
        for d in (1, 2, 4):
            keep = sub < SUBLANES - d
            g = g + p * jnp.where(keep, pltpu.roll(g, SUBLANES - d, 0), 0.0)
            p = p * jnp.where(keep, pltpu.roll(p, SUBLANES - d, 0), 1.0)
        h = g + p * carry
        carry = h[0:1, :]
        out.append(h)
        yield
    return jnp.concatenate(out[::-1], axis=0)


def _run(steps):
    while True:
        try:
            next(steps)
        except StopIteration as done:
            return done.value


def _paired(progress, pieces):
    n, done = len(pieces), 1
    pieces[0]()
    for frac in progress:
        while done < n and done <= frac * n:
            pieces[done]()
            done += 1
    while done < n:
        pieces[done]()
        done += 1


def _conv_taps(ext, halo, n, width):
    return [_down(ext, width - 1 - k)[halo:halo + n] for k in range(width)]


def _lru_gates(xc, wa_ref, ba, wx_ref, bx):
    xb = xc.astype(BF16)
    pa, px = [], []
    for h in range(LRU_HEADS):
        xh = xb[:, h * LRU_HEAD_DIM:(h + 1) * LRU_HEAD_DIM]
        pa.append(jnp.dot(xh, wa_ref[h], preferred_element_type=F32))
        px.append(jnp.dot(xh, wx_ref[h], preferred_element_type=F32))
    r = _sigmoid(jnp.concatenate(pa, axis=1) + ba)
    ig = _sigmoid(jnp.concatenate(px, axis=1) + bx)
    return r, ig


def _softplus_neg(lam):
    return jnp.maximum(-lam, 0.0) + jnp.log1p(jnp.exp(-jnp.abs(lam)))


def _recip_1_to_2(d):
    r0 = pl.reciprocal(d, approx=True)
    return r0 * (2.0 - d * r0)


def _lru_decay(r, sp, first):
    big_l = (-LRU_C) * r * sp
    a = jnp.exp(big_l)
    th = jnp.tanh(big_l)
    q = (-2.0 * th) * _recip_1_to_2(1.0 - th)
    rs = lax.rsqrt(jnp.maximum(q, SQRT_FLOOR))
    return a, jnp.where(first, 1.0, q * rs), rs


def _pool_inv_counts(t0, n):
    t = (t0 + lax.broadcasted_iota(jnp.int32, (n, 1), 0) + 1).astype(F32)
    return [1.0 / jnp.minimum(t, float(w)) for w in POOL_WINDOWS]


def _window_sums(ext, shift):
    gd = POOL_GROUP_DIM
    out = []
    s = ext
    for k in range(len(POOL_WINDOWS)):
        s = s + shift(s, 2 ** k)
        out.append(s[:, 0:gd])
        if k + 1 < len(POOL_WINDOWS):
            s = s[:, gd:]
    return out


SW_ROWS, SW_COLS = 16, 2 * D
SW_CONV, SW_SC, SW_POOL_B, SW_POOL_S = 0, 4, 8, 9


def _mod_fwd(c8, mod_w, mod_b, conv_w, sc_w, pool_b, pool_s):
    nw = mod_w.shape[2]
    cq, pq = conv_w.shape[1], pool_b.shape[1]

    def body(c_ref, w_ref, b_ref, cw_ref, sw_ref, pb_ref, ps_ref, ca_ref, mod_ref, small_ref,
             cslot, mslot, msend, pslot, psend, s1, r1, s2, r2, s3, r3):
        x, y, c = _pos()
        me = 4 * x + 2 * y + c
        chip = 2 * x + y
        first = []
        for r in range(1, N_DEV):
            fx, fy, fc = (r >> 2) & 1, (r >> 1) & 1, r & 1
            cp = pltpu.make_async_remote_copy(
                src_ref=c_ref, dst_ref=cslot.at[me], send_sem=s1.at[r - 1], recv_sem=r1.at[r - 1],
                device_id=(_flip(x, fx), _flip(y, fy), _flip(c, fc)), device_id_type=MESH)
            cp.start()
            first.append(cp)
        cslot[me] = c_ref[...]
        for cp in first:
            cp.wait()
        rows = _rows(SUBLANES, D)
        call = jnp.zeros((SUBLANES, D), F32)
        for d in range(N_DEV):
            call = jnp.where(rows == d, cslot[d], call)
        ca = call * _sigmoid(call)
        ca_ref[...] = ca
        for l in range(2):
            msend[l] = jnp.dot(ca, w_ref[l], precision=lax.Precision.HIGHEST, preferred_element_type=F32)
        psend[...] = jnp.zeros_like(psend)
        psend[SW_CONV:SW_CONV + 4, 0:cq] = cw_ref[...]
        psend[SW_SC:SW_SC + 3, 0:cq] = sw_ref[...]
        psend[SW_POOL_B:SW_POOL_B + 1, :] = pb_ref[...]
        psend[SW_POOL_S:SW_POOL_S + 1, :] = ps_ref[...]
        second = []
        for q, (fx, fy) in enumerate(((1, 0), (0, 1), (1, 1))):
            peer = (_flip(x, fx), _flip(y, fy), c)
            for src, dst, ss, rs in ((msend, mslot, s2, r2), (psend, pslot, s3, r3)):
                cp = pltpu.make_async_remote_copy(src_ref=src, dst_ref=dst.at[chip], send_sem=ss.at[q], recv_sem=rs.at[q],
                                                  device_id=peer, device_id_type=MESH)
                cp.start()
                second.append(cp)
        mslot[chip] = msend[...]
        pslot[chip] = psend[...]
        for cp in second:
            cp.wait()
        small_ref[...] = jnp.zeros_like(small_ref)
        for j in range(N_CHIP):
            for l in range(2):
                mod_ref[l, :, j * nw:(j + 1) * nw] = mslot[j, l] + b_ref[l:l + 1, j * nw:(j + 1) * nw]
            small_ref[0:SUBLANES, j * cq:(j + 1) * cq] = pslot[j, 0:SUBLANES, 0:cq]
            small_ref[SUBLANES:SW_ROWS, j * pq:(j + 1) * pq] = pslot[j, SUBLANES:SW_ROWS, :]

    args = (c8, mod_w, mod_b, conv_w, sc_w, pool_b, pool_s)
    dma3 = pltpu.SemaphoreType.DMA((N_CHIP - 1,))
    return pl.pallas_call(
        body, name="mod_fwd",
        in_specs=[VMEM] * len(args), out_specs=[VMEM] * 3,
        out_shape=[jax.ShapeDtypeStruct((SUBLANES, D), F32), jax.ShapeDtypeStruct((2, SUBLANES, N_CHIP * nw), F32),
                   jax.ShapeDtypeStruct((SW_ROWS, SW_COLS), F32)],
        scratch_shapes=[pltpu.VMEM((N_DEV, SUBLANES, D), F32), pltpu.VMEM((N_CHIP, 2, SUBLANES, nw), F32),
                        pltpu.VMEM((2, SUBLANES, nw), F32), pltpu.VMEM((N_CHIP, SW_ROWS, pq), F32),
                        pltpu.VMEM((SW_ROWS, pq), F32),
                        pltpu.SemaphoreType.DMA((N_DEV - 1,)), pltpu.SemaphoreType.DMA((N_DEV - 1,)),
                        dma3, dma3, dma3, dma3],
        compiler_params=_cp(),
    )(*args)


def _wcast(ws):
    def body(*refs):
        n = len(refs) // 2
        for a in range(n):
            refs[n + a][...] = refs[a][...].astype(BF16)

    return pl.pallas_call(
        body, name="wcast", in_specs=[VMEM] * len(ws), out_specs=[VMEM] * len(ws),
        out_shape=[jax.ShapeDtypeStruct(w.shape, BF16) for w in ws], compiler_params=_cp(),
    )(*ws)


def _wcast_own_block(w, kidx, name, after=()):
    rr, cc = w.shape
    rb = min(rr, 256)

    def body(k_ref, w_ref, *rest):
        rest[-1][...] = w_ref[...].astype(BF16)

    order = list(after)
    return pl.pallas_call(
        body, name=name,
        grid_spec=pltpu.PrefetchScalarGridSpec(
            num_scalar_prefetch=1, grid=(rr // rb,),
            in_specs=[pl.BlockSpec((rb, cc), lambda j, k_ref: (j, 0))] + [ANY] * len(order),
            out_specs=pl.BlockSpec((None, rb, cc), lambda j, k_ref: (k_ref[0], j, 0))),
        out_shape=jax.ShapeDtypeStruct((N_CHIP, rr, cc), BF16),
        compiler_params=_cp(("parallel",)),
    )(kidx, w, *order)


def _wgather_copies(outs, rows, ssem, rsem, fssem, frsem):
    n = len(outs)
    x, y, c = _pos()
    chip = 2 * x + y
    sib = (x, y, 1 - c)
    flips = ((1, 0), (0, 1), (1, 1))

    def half(a, which):
        hr = rows[a] // 2
        return pl.ds(pl.multiple_of(which * hr, BF16_ROWS), hr)

    sends = []
    for a in range(n):
        mine = outs[a].at[chip, half(a, c), :]
        for q, (fx, fy) in enumerate(flips):
            cp = pltpu.make_async_remote_copy(
                src_ref=mine, dst_ref=mine, send_sem=ssem.at[3 * a + q], recv_sem=rsem.at[3 * a + q],
                device_id=(_flip(x, fx), _flip(y, fy), c), device_id_type=MESH)
            cp.start()
            sends.append(cp)
    passed = []
    for a in range(n):
        for q, (fx, fy) in enumerate(flips):
            src_chip = 2 * _flip(x, fx) + _flip(y, fy)
            landed = outs[a].at[src_chip, half(a, c), :]
            pltpu.make_async_remote_copy(
                src_ref=landed, dst_ref=landed, send_sem=ssem.at[3 * a + q], recv_sem=rsem.at[3 * a + q],
                device_id=sib, device_id_type=MESH).wait_recv()
            cp = pltpu.make_async_remote_copy(
                src_ref=landed, dst_ref=landed, send_sem=fssem.at[3 * a + q], recv_sem=frsem.at[3 * a + q],
                device_id=sib, device_id_type=MESH)
            cp.start()
            passed.append(cp)
    for a in range(n):
        for q, (fx, fy) in enumerate(flips):
            src_chip = 2 * _flip(x, fx) + _flip(y, fy)
            other = outs[a].at[src_chip, half(a, 1 - c), :]
            pltpu.make_async_remote_copy(
                src_ref=other, dst_ref=other, send_sem=fssem.at[3 * a + q], recv_sem=frsem.at[3 * a + q],
                device_id=sib, device_id_type=MESH).wait_recv()
    for cp in sends + passed:
        cp.wait_send()


def _wgather_sequencer(bufs, name, collective_id):
    n = len(bufs)
    refs = [jax.new_ref(b, memory_space=pltpu.MemorySpace.HBM) for b in bufs]
    dma = pltpu.SemaphoreType.DMA((3 * n,))

    @pl.kernel(mesh=plsc.ScalarSubcoreMesh(axis_name="sequencer", num_cores=1), name=name,
               scratch_types=(dma, dma, dma, dma), compiler_params=pltpu.CompilerParams(collective_id=collective_id))
    def launch(ssem, rsem, fssem, frsem):
        x, y, c = _pos()
        barrier = pltpu.get_barrier_semaphore()
        for peer in ((1 - x, y, c), (x, 1 - y, c), (1 - x, 1 - y, c), (x, y, 1 - c)):
            pl.semaphore_signal(barrier, inc=1, device_id=peer, device_id_type=MESH)
        pl.semaphore_wait(barrier, 4)
        _wgather_copies(refs, [b.shape[1] for b in bufs], ssem, rsem, fssem, frsem)

    launch()
    return [r[...] for r in refs]


def _norm_mod(x, g, sc, sh, name):
    s_len = x.shape[0]
    ts = _tile(s_len, TS_WGRAD)

    def body(x_ref, g_ref, sc_ref, sh_ref, h_ref):
        xv = x_ref[...]
        rinv = lax.rsqrt(jnp.mean(xv * xv, axis=-1, keepdims=True) + RMS_EPS)
        h_ref[...] = (xv * rinv * (g_ref[...] * (1.0 + sc_ref[...])) + sh_ref[...]).astype(BF16)

    row = pl.BlockSpec((ts, D), lambda i: (i, 0))
    vec = pl.BlockSpec((1, D), lambda i: (0, 0))
    return pl.pallas_call(
        body, name=name, grid=(s_len // ts,), in_specs=[row, vec, vec, vec], out_specs=row,
        out_shape=jax.ShapeDtypeStruct((s_len, D), BF16), compiler_params=_cp(("parallel",)),
    )(x, g, sc, sh)


def _l0_fwd(h0_all, x, w_in, gate, cw, cb, wa, ba, wx, bx, lam, sw, wo):
    s_len, nb = x.shape[0], w_in.shape[2]
    ts = _tile(s_len, TS_MIX)
    n_t = s_len // ts
    hl = SUBLANES

    def body(h0_ref, xb_ref, win_ref, gate_ref, cw_ref, cb_ref, wa_ref, ba_ref, wx_ref, bx_ref,
             lam_ref, sw_ref, wo_ref, x1_ref, h_ref, y_ref, xc_ref, cz_ref, r_ref, ig_ref, p_ref,
             pcur, pnext, cxa, czz, chh):
        i = pl.program_id(0)

        @pl.when(i == 0)
        def _():
            cxa[...] = jnp.zeros_like(cxa)
            czz[...] = jnp.zeros_like(czz)
            chh[...] = jnp.zeros_like(chh)
            pnext[...] = jnp.zeros_like(pnext)

        pcur[...] = pnext[...]
        h0 = h0_ref[...]

        def project(k, c0, cn):
            def emit():
                pk = jnp.dot(h0, win_ref[k, :, c0:c0 + cn], preferred_element_type=F32).astype(BF16)
                p_ref[:, k * nb + c0:k * nb + c0 + cn] = pk
                pnext[:, k * nb + c0:k * nb + c0 + cn] = pk
            return emit

        def mixer():
            piece = lambda k: pcur[:, k * D:(k + 1) * D].astype(F32)
            xa = piece(0)
            rows = _rows(ts, D)
            taps = _conv_taps(jnp.concatenate([cxa[...], xa], axis=0), hl, ts, 4)
            xc = cb_ref[...] + sum(cw_ref[k:k + 1, :] * taps[k] for k in range(4))
            xc_ref[...] = xc.astype(BF16)
            r, ig = _lru_gates(xc, wa_ref, ba_ref[...], wx_ref, bx_ref[...])
            r_ref[...] = r.astype(BF16)
            ig_ref[...] = ig.astype(BF16)
            a, m, _ = _lru_decay(r, _softplus_neg(lam_ref[...]), (rows == 0) & (i == 1))
            yield 0.26
            h = _run(_scan_fwd_steps(a, m * ig * xc, chh[hl - 1:hl, :]))
            yield 0.51
            gcp, v = piece(3), piece(4)
            z = gcp * v
            ztaps = _conv_taps(jnp.concatenate([czz[...], z], axis=0), hl, ts, 3)
            cz = sum(sw_ref[k:k + 1, :] * ztaps[k] for k in range(3))
            cz_ref[...] = cz.astype(BF16)
            yb = piece(2) * cz
            ga, gb = piece(1), piece(5)
            y = jnp.concatenate([h * (ga * _sigmoid(ga)), yb * (gb * _sigmoid(gb))], axis=1).astype(BF16)
            yield 0.76
            y_ref[...] = y
            x1_ref[...] = xb_ref[...] + gate_ref[...] * jnp.dot(y, wo_ref[...], preferred_element_type=F32)
            h_ref[...] = h.astype(BF16)
            cxa[...] = xa[ts - hl:, :]
            czz[...] = z[ts - hl:, :]
            chh[...] = jnp.where(i > 0, h[ts - hl:, :], 0.0)

        _paired(mixer(), [project(k, 0, nb) for k in range(N_CHIP)])

    def full(a):
        return pl.BlockSpec(a.shape, lambda i: (0,) * a.ndim)

    ahead = lambda w: pl.BlockSpec((ts, w), lambda i: (jnp.minimum(i, n_t - 1), 0))
    behind = lambda w: pl.BlockSpec((ts, w), lambda i: (jnp.maximum(i - 1, 0), 0))
    args = (h0_all, x, w_in, gate, cw, cb, wa, ba, wx, bx, lam, sw, wo)
    return pl.pallas_call(
        body, name="l0_fwd", grid=(n_t + 1,),
        in_specs=[ahead(D), behind(D)] + [full(a) for a in args[2:]],
        out_specs=[behind(D), behind(D), behind(2 * D)] + [behind(D)] * 4 + [ahead(N_CHIP * nb)],
        out_shape=[jax.ShapeDtypeStruct((s_len, D), F32), jax.ShapeDtypeStruct((s_len, D), BF16),
                   jax.ShapeDtypeStruct((s_len, 2 * D), BF16)] + [jax.ShapeDtypeStruct((s_len, D), BF16)] * 4
        + [jax.ShapeDtypeStruct((s_len, N_CHIP * nb), BF16)],
        scratch_shapes=[pltpu.VMEM((ts, N_CHIP * nb), BF16)] * 2 + [pltpu.VMEM((hl, D), F32)] * 3,
        compiler_params=_cp(("arbitrary",)),
    )(*args)


def _l1_fwd(x1, g, sc, sh, w_in, tgt, gate, wg, bg, scale, wo, gf):
    s_len, nb = x1.shape[0], w_in.shape[2]
    ts = _tile(s_len, TS_MIX)
    n_t = s_len // ts
    pw, gd, hl = 2 * D, POOL_GROUP_DIM, POOL_HALO

    def body(xa_ref, xb_ref, t_ref, g_ref, sc_ref, sh_ref, win_ref, gate_ref, wg_ref, bg_ref, scl_ref, wo_ref, gf_ref,
             d_ref, mx_ref, y_ref, dx_ref, loss_ref, dgf_ref, h1_ref, p_ref, pcur, pnext, cv):
        i = pl.program_id(0)

        @pl.when(i == 0)
        def _():
            cv[...] = jnp.zeros_like(cv)
            loss_ref[...] = jnp.zeros_like(loss_ref)
            dgf_ref[...] = jnp.zeros_like(dgf_ref)
            pnext[...] = jnp.zeros_like(pnext)

        pcur[...] = pnext[...]
        xv = xa_ref[...]
        rinv = lax.rsqrt(jnp.mean(xv * xv, axis=-1, keepdims=True) + RMS_EPS)
        h1 = (xv * rinv * (g_ref[...] * (1.0 + sc_ref[...])) + sh_ref[...]).astype(BF16)
        h1_ref[...] = h1

        def project(k):
            def emit():
                pk = jnp.dot(h1, win_ref[k], preferred_element_type=F32).astype(BF16)
                p_ref[:, k * nb:(k + 1) * nb] = pk
                pnext[:, k * nb:(k + 1) * nb] = pk
            return emit

        def mixer():
            v = pcur[:, 0:pw].astype(F32)
            sums = _window_sums(jnp.concatenate([cv[...], v], axis=0), _down)
            inv = _pool_inv_counts(jnp.maximum(i - 1, 0) * ts, ts)
            dd = [sums[k][hl:hl + ts] * inv[k] - v[:, k * gd:(k + 1) * gd] for k in range(4)]
            d_ref[...] = jnp.concatenate(dd, axis=1).astype(BF16)
            yield 0.26
            mixed = jnp.concatenate(
                [jnp.dot(dd[k].astype(BF16), wg_ref[k], preferred_element_type=F32) for k in range(4)], axis=1) + bg_ref[...]
            mx_ref[...] = mixed.astype(BF16)
            gg = pcur[:, pw:2 * pw].astype(F32)
            y = (mixed * scl_ref[...] * (gg * _sigmoid(gg))).astype(BF16)
            y_ref[...] = y
            yield 0.51
            x2 = xb_ref[...] + gate_ref[...] * jnp.dot(y, wo_ref[...], preferred_element_type=F32)
            yield 0.76
            r2 = lax.rsqrt(jnp.mean(x2 * x2, axis=-1, keepdims=True) + RMS_EPS)
            n2 = x2 * r2
            err = n2 * gf_ref[...] - t_ref[...]
            loss_ref[...] += jnp.where(i > 0, jnp.sum(err * err, axis=0, keepdims=True), 0.0)
            dyf = err * (1.0 / D)
            dgf_ref[...] += jnp.where(i > 0, jnp.sum(dyf * n2, axis=0, keepdims=True), 0.0)
            dn = dyf * gf_ref[...]
            dx_ref[...] = r2 * (dn - n2 * jnp.mean(dn * n2, axis=-1, keepdims=True))
            cv[...] = v[ts - hl:, :]

        _paired(mixer(), [project(k) for k in range(N_CHIP)])

    def full(a):
        return pl.BlockSpec(a.shape, lambda i: (0,) * a.ndim)

    ahead = lambda w: pl.BlockSpec((ts, w), lambda i: (jnp.minimum(i, n_t - 1), 0))
    behind = lambda w: pl.BlockSpec((ts, w), lambda i: (jnp.maximum(i - 1, 0), 0))
    acc = pl.BlockSpec((1, D), lambda i: (0, 0))
    args = (x1, x1, tgt, g, sc, sh, w_in, gate, wg, bg, scale, wo, gf)
    return pl.pallas_call(
        body, name="l1_fwd", grid=(n_t + 1,),
        in_specs=[ahead(D), behind(D), behind(D)] + [full(a) for a in args[3:]],
        out_specs=[behind(pw), behind(pw), behind(pw), behind(D), acc, acc, ahead(D), ahead(N_CHIP * nb)],
        out_shape=[jax.ShapeDtypeStruct((s_len, pw), BF16)] * 3 + [jax.ShapeDtypeStruct((s_len, D), F32)]
        + [jax.ShapeDtypeStruct((1, D), F32)] * 2
        + [jax.ShapeDtypeStruct((s_len, D), BF16), jax.ShapeDtypeStruct((s_len, N_CHIP * nb), BF16)],
        scratch_shapes=[pltpu.VMEM((ts, N_CHIP * nb), BF16)] * 2 + [pltpu.VMEM((hl, pw), F32)],
        compiler_params=_cp(("arbitrary",)),
    )(*args)


def _l1_bwd_mix(dx2, proj, mixed, y, dpool, gate, wg, scale, wo):
    s_len = dx2.shape[0]
    n_sub = 2
    ts = _tile(s_len, n_sub * TS_MIX)
    sub = ts // n_sub
    n_t = s_len // ts
    pw, gd, hl = 2 * D, POOL_GROUP_DIM, POOL_HALO

    def body(dx_ref, gg_ref, mx_ref, y_ref, d_ref, gate_ref, wg_ref, sc_ref, wo_ref,
             dp_ref, mt_ref, dwg_ref, dsc_ref, dbg_ref, cq):
        i = pl.program_id(0)

        @pl.when(i == 0)
        def _():
            cq[...] = jnp.zeros_like(cq)
            dsc_ref[...] = jnp.zeros_like(dsc_ref)
            dbg_ref[...] = jnp.zeros_like(dbg_ref)
            mt_ref[...] = jnp.zeros_like(mt_ref)
            dwg_ref[...] = jnp.zeros_like(dwg_ref)

        ahead_rows, dm_parts = {}, {}
        dxb_all = dx_ref[...].astype(BF16)

        def wgrad_out(k):
            mt_ref[k] += lax.dot_general(y_ref[:, k * gd:(k + 1) * gd], dxb_all, TN, preferred_element_type=F32)

        def chain(j):
            rows = slice(j * sub, (j + 1) * sub)
            dxv = dx_ref[rows, :]
            dy = lax.dot_general((gate_ref[...] * dxv).astype(BF16), wo_ref[...], NT, preferred_element_type=F32)
            yield
            gg = gg_ref[rows, :].astype(F32)
            mixed = mx_ref[rows, :].astype(F32)
            s = _sigmoid(gg)
            sg = gg * s
            dym = dy * mixed
            dmixed = dy * sc_ref[...] * sg
            dsc_ref[...] += jnp.sum(dym * sg, axis=0, keepdims=True)
            dbg_ref[...] += jnp.sum(dmixed, axis=0, keepdims=True)
            dmb = dmixed.astype(BF16)
            dm_parts[j] = dmb
            dp_ref[rows, pw:2 * pw] = (dym * sc_ref[...] * (s + sg * (1.0 - s))).astype(BF16)
            yield
            inv = _pool_inv_counts((n_t - 1 - i) * ts + j * sub, sub)
            dd = [lax.dot_general(dmb[:, k * gd:(k + 1) * gd], wg_ref[k], NT, preferred_element_type=F32)
                  for k in range(4)]
            q = jnp.concatenate([dd[k] * inv[k] for k in range(4)], axis=1)
            ahead_rows[j] = q[0:hl, :]
            yield
            behind_q = cq[...] if j == n_sub - 1 else ahead_rows[j + 1]
            sums = _window_sums(jnp.concatenate([q, behind_q], axis=0), _up)
            dp_ref[rows, 0:pw] = jnp.concatenate([sums[k][0:sub] - dd[k] for k in range(4)], axis=1).astype(BF16)

        chains = [chain(j) for j in reversed(range(n_sub))]
        for phase in range(4):
            for ch in chains:
                next(ch, None)
            wgrad_out(phase)
            if phase == 1:
                dmb_all = jnp.concatenate([dm_parts[j] for j in range(n_sub)], axis=0)
                for k in range(4):
                    cols = slice(k * gd, (k + 1) * gd)
                    dwg_ref[k] += lax.dot_general(d_ref[:, cols], dmb_all[:, cols], TN, preferred_element_type=F32)
        cq[...] = ahead_rows[0]

    def full(a):
        return pl.BlockSpec(a.shape, lambda i: (0,) * a.ndim)

    rev = lambda w, j=0: pl.BlockSpec((ts, w), lambda i: (n_t - 1 - i, j))
    acc = pl.BlockSpec((1, pw), lambda i: (0, 0))
    return pl.pallas_call(
        body, name="l1_bwd_mix", grid=(n_t,),
        in_specs=[rev(D), rev(pw, 1), rev(pw), rev(pw), rev(pw)] + [full(a) for a in (gate, wg, scale, wo)],
        out_specs=[rev(2 * pw), pl.BlockSpec((N_CHIP, gd, D), lambda i: (0, 0, 0)),
                   pl.BlockSpec((4, gd, gd), lambda i: (0, 0, 0)), acc, acc],
        out_shape=[jax.ShapeDtypeStruct((s_len, 2 * pw), BF16), jax.ShapeDtypeStruct((N_CHIP, gd, D), F32),
                   jax.ShapeDtypeStruct((4, gd, gd), F32),
                   jax.ShapeDtypeStruct((1, pw), F32), jax.ShapeDtypeStruct((1, pw), F32)],
        scratch_shapes=[pltpu.VMEM((hl, pw), F32)],
        compiler_params=_cp(("arbitrary",)),
    )(dx2, proj, mixed, y, dpool, gate, wg, scale, wo)


def _l0_bwd_mix(dx1, proj, hst, y, xc, cz, rg, ig_, gate, cw, wa, wx, lam, sw, wo):
    s_len = dx1.shape[0]
    ts = _tile(s_len, TS_MIX)
    n_t = s_len // ts
    hl, hb = SUBLANES, BF16_ROWS
    yb_w = 2 * D // N_CHIP

    def body(dx_ref, p_ref, h_ref, hh_ref, y_ref, xc_ref, cz_ref, r_ref, ig_ref, gate_ref, cw_ref, wa_ref, wx_ref,
             lam_ref, sw_ref, wo_ref, dp_ref, mt_ref, dwa_ref, dwx_ref, sm_ref, cg, cdxc, cdcz, ca):
        i = pl.program_id(0)
        ri = n_t - 1 - i

        @pl.when(i == 0)
        def _():
            cg[...] = jnp.zeros_like(cg)
            ca[...] = jnp.zeros_like(ca)
            cdxc[...] = jnp.zeros_like(cdxc)
            cdcz[...] = jnp.zeros_like(cdcz)
            sm_ref[...] = jnp.zeros_like(sm_ref)
            mt_ref[...] = jnp.zeros_like(mt_ref)
            dwa_ref[...] = jnp.zeros_like(dwa_ref)
            dwx_ref[...] = jnp.zeros_like(dwx_ref)

        dxb = dx_ref[...].astype(BF16)

        def wgrad_out(k):
            mt_ref[k] += lax.dot_general(y_ref[:, k * yb_w:(k + 1) * yb_w], dxb, TN, preferred_element_type=F32)

        wgrad_out(0)
        has_prev = (ri > 0).astype(F32)
        xa, ga, gbp, gcp, v, gb = [p_ref[:, k * D:(k + 1) * D].astype(F32) for k in range(6)]
        rows = _rows(ts, D)
        first = (rows == 0) & (ri == 0)
        xc = xc_ref[...].astype(F32)
        cz = cz_ref[...].astype(F32)
        r = r_ref[...].astype(F32)
        ig = ig_ref[...].astype(F32)
        sp = _softplus_neg(lam_ref[...])
        a, m, inv_m = _lru_decay(r, sp, first)
        z = gcp * v
        h = h_ref[...].astype(F32)
        hprev = _down(jnp.concatenate([hh_ref[...].astype(F32)[hb - hl:hb] * has_prev, h], axis=0), 1)[hl:hl + ts]
        dy = lax.dot_general((gate_ref[...] * dx_ref[...]).astype(BF16), wo_ref[...], NT, preferred_element_type=F32)
        dya_pre, dyb_pre = dy[:, 0:D], dy[:, D:2 * D]
        s_a, s_b = _sigmoid(ga), _sigmoid(gb)
        silu_a, silu_b = ga * s_a, gb * s_b
        dp_ref[:, D:2 * D] = (dya_pre * h * (s_a + silu_a * (1.0 - s_a))).astype(BF16)
        dp_ref[:, 5 * D:6 * D] = (dyb_pre * (gbp * cz) * (s_b + silu_b * (1.0 - s_b))).astype(BF16)
        dya = dya_pre * silu_a
        dyb = dyb_pre * silu_b
        wgrad_out(1)
        dp_ref[:, 2 * D:3 * D] = (dyb * cz).astype(BF16)
        dcz = dyb * gbp
        dcz_ext = jnp.concatenate([dcz, cdcz[...]], axis=0)
        dcz_taps = [_up(dcz_ext, 2 - k)[0:ts] for k in range(3)]
        for k in range(3):
            sm_ref[8 + k:9 + k, :] += jnp.sum(z * dcz_taps[k], axis=0, keepdims=True)
        dz = sum(sw_ref[k:k + 1, :] * dcz_taps[k] for k in range(3))
        dp_ref[:, 3 * D:4 * D] = (dz * v).astype(BF16)
        dp_ref[:, 4 * D:5 * D] = (dz * gcp).astype(BF16)
        cdcz[...] = dcz[0:hl, :]
        alpha = _up(jnp.concatenate([a, ca[...]], axis=0), 1)[0:ts]
        wgrad_out(2)
        dh = _run(_scan_rev_steps(alpha, dya, cg[0:1, :]))
        wgrad_out(3)
        cg[...] = dh[0:hl, :]
        ca[...] = a[0:hl, :]
        da = dh * hprev
        dhx = dh * xc
        dm = dhx * ig
        di = dhx * m
        dxc = dh * (m * ig)
        dl = a * (da - jnp.where(first, 0.0, dm * a * inv_m))
        dlr = dl * r
        sm_ref[7:8, :] += jnp.sum(dlr, axis=0, keepdims=True) * (-LRU_C)
        dpa = dlr * (sp * (-LRU_C)) * (1.0 - r)
        dpx = di * ig * (1.0 - ig)
        sm_ref[5:6, :] += jnp.sum(dpa, axis=0, keepdims=True)
        sm_ref[6:7, :] += jnp.sum(dpx, axis=0, keepdims=True)
        dpa_b, dpx_b, xc_b = dpa.astype(BF16), dpx.astype(BF16), xc.astype(BF16)
        back = []
        for hd in range(LRU_HEADS):
            sl = slice(hd * LRU_HEAD_DIM, (hd + 1) * LRU_HEAD_DIM)
            back.append(lax.dot_general(dpa_b[:, sl], wa_ref[hd], NT, preferred_element_type=F32)
                        + lax.dot_general(dpx_b[:, sl], wx_ref[hd], NT, preferred_element_type=F32))
            dwa_ref[hd] += lax.dot_general(xc_b[:, sl], dpa_b[:, sl], TN, preferred_element_type=F32)
            dwx_ref[hd] += lax.dot_general(xc_b[:, sl], dpx_b[:, sl], TN, preferred_element_type=F32)
        dxc = dxc + jnp.concatenate(back, axis=1)
        sm_ref[4:5, :] += jnp.sum(dxc, axis=0, keepdims=True)
        dxc_ext = jnp.concatenate([dxc, cdxc[...]], axis=0)
        dxc_taps = [_up(dxc_ext, 3 - k)[0:ts] for k in range(4)]
        for k in range(4):
            sm_ref[k:k + 1, :] += jnp.sum(xa * dxc_taps[k], axis=0, keepdims=True)
        dp_ref[:, 0:D] = sum(cw_ref[k:k + 1, :] * dxc_taps[k] for k in range(4)).astype(BF16)
        cdxc[...] = dxc[0:hl, :]

    def full(a):
        return pl.BlockSpec(a.shape, lambda i: (0,) * a.ndim)

    rev = lambda w: pl.BlockSpec((ts, w), lambda i: (n_t - 1 - i, 0))
    halo = lambda w: pl.BlockSpec((hb, w), lambda i: (jnp.maximum((n_t - 1 - i) * (ts // hb) - 1, 0), 0))
    return pl.pallas_call(
        body, name="l0_bwd_mix", grid=(n_t,),
        in_specs=[rev(D), rev(6 * D), rev(D), halo(D), rev(2 * D), rev(D), rev(D), rev(D), rev(D)]
        + [full(a) for a in (gate, cw, wa, wx, lam, sw, wo)],
        out_specs=[rev(6 * D), pl.BlockSpec((N_CHIP, yb_w, D), lambda i: (0, 0, 0)),
                   pl.BlockSpec(wa.shape, lambda i: (0, 0, 0)), pl.BlockSpec(wa.shape, lambda i: (0, 0, 0)),
                   pl.BlockSpec((2 * SUBLANES, D), lambda i: (0, 0))],
        out_shape=[jax.ShapeDtypeStruct((s_len, 6 * D), BF16), jax.ShapeDtypeStruct((N_CHIP, yb_w, D), F32),
                   jax.ShapeDtypeStruct(wa.shape, F32), jax.ShapeDtypeStruct(wa.shape, F32),
                   jax.ShapeDtypeStruct((2 * SUBLANES, D), F32)],
        scratch_shapes=[pltpu.VMEM((hl, D), F32)] * 4,
        compiler_params=_cp(("arbitrary",)),
    )(dx1, proj, hst, hst, y, xc, cz, rg, ig_, gate, cw, wa, wx, lam, sw, wo)


def _dgrad_norm(dproj, w, x, dres, g, sc, name, after=None):
    s_len, nb = x.shape[0], w.shape[2]
    ts = _tile(s_len, TS_DGRAD)
    order = [] if after is None else [after]

    def body(dp_ref, w_ref, x_ref, dr_ref, g_ref, sc_ref, *rest):
        dx_ref, s1_ref, s2_ref = rest[len(order):]

        @pl.when(pl.program_id(0) == 0)
        def _():
            s1_ref[...] = jnp.zeros_like(s1_ref)
            s2_ref[...] = jnp.zeros_like(s2_ref)

        dh = sum(lax.dot_general(dp_ref[:, k * nb:(k + 1) * nb], w_ref[k], NT, preferred_element_type=F32)
                 for k in range(N_CHIP))
        xv = x_ref[...]
        r = lax.rsqrt(jnp.mean(xv * xv, axis=-1, keepdims=True) + RMS_EPS)
        n = xv * r
        s1_ref[...] += jnp.sum(dh, axis=0, keepdims=True)
        s2_ref[...] += jnp.sum(dh * n, axis=0, keepdims=True)
        dn = dh * (g_ref[...] * (1.0 + sc_ref[...]))
        dx_ref[...] = dr_ref[...] + r * (dn - n * jnp.mean(dn * n, axis=-1, keepdims=True))

    row = lambda wd: pl.BlockSpec((ts, wd), lambda i: (i, 0))
    vec = pl.BlockSpec((1, D), lambda i: (0, 0))
    return pl.pallas_call(
        body, name=name, grid=(s_len // ts,),
        in_specs=[row(N_CHIP * nb), pl.BlockSpec(w.shape, lambda i: (0, 0, 0)), row(D), row(D), vec, vec]
        + [ANY] * len(order),
        out_specs=[row(D), vec, vec],
        out_shape=[jax.ShapeDtypeStruct((s_len, D), F32)] + [jax.ShapeDtypeStruct((1, D), F32)] * 2,
        compiler_params=_cp(("arbitrary",)),
    )(dproj, w, x, dres, g, sc, *order)


def _wgrad(a, b, groups, ka, nb, a_col, b_col, name, after=None):
    s_len = a.shape[0]
    ts = _tile(s_len, TS_WGRAD * (2 if ka * nb <= D * D else 1))
    n_s = s_len // ts
    order = [] if after is None else [after]

    def body(a_ref, b_ref, *rest):
        o_ref, wire_ref = rest[-2:]

        @pl.when(pl.program_id(1) == 0)
        def _():
            o_ref[...] = jnp.zeros_like(o_ref)

        o_ref[...] += lax.dot_general(a_ref[...].astype(BF16), b_ref[...].astype(BF16), TN, preferred_element_type=F32)

        @pl.when(pl.program_id(1) == n_s - 1)
        def _():
            wire_ref[...] = o_ref[...].astype(GRAD_WIRE_DTYPE)

    blk = pl.BlockSpec((None, ka, nb), lambda g, s: (g, 0, 0))
    return pl.pallas_call(
        body, name=name, grid=(groups, n_s),
        in_specs=[pl.BlockSpec((ts, ka), lambda g, s: (s, a_col(g))), pl.BlockSpec((ts, nb), lambda g, s: (s, b_col(g)))]
        + [ANY] * len(order),
        out_specs=[blk, blk],
        out_shape=[jax.ShapeDtypeStruct((groups, ka, nb), F32), jax.ShapeDtypeStruct((groups, ka, nb), GRAD_WIRE_DTYPE)],
        compiler_params=_cp(("parallel", "arbitrary")),
    )(a, b, *order)


def _wo_final(mt, wo, gate, name):
    rb = mt.shape[1]

    def body(m_ref, w_ref, gate_ref, dw_ref, wire_ref, dg_ref):
        @pl.when(pl.program_id(0) == 0)
        def _():
            dg_ref[...] = jnp.zeros_like(dg_ref)

        mv = m_ref[...]
        dw = mv * gate_ref[...]
        dw_ref[...] = dw
        wire_ref[...] = dw.astype(GRAD_WIRE_DTYPE)
        dg_ref[...] += jnp.sum(mv * w_ref[...].astype(F32), axis=0, keepdims=True)

    blk = pl.BlockSpec((None, rb, D), lambda k: (k, 0, 0))
    vec = pl.BlockSpec((1, D), lambda k: (0, 0))
    return pl.pallas_call(
        body, name=name, grid=(N_CHIP,), in_specs=[blk, blk, vec], out_specs=[blk, blk, vec],
        out_shape=[jax.ShapeDtypeStruct(mt.shape, F32), jax.ShapeDtypeStruct(mt.shape, GRAD_WIRE_DTYPE),
                   jax.ShapeDtypeStruct((1, D), F32)],
        compiler_params=_cp(("arbitrary",)),
    )(mt, wo, gate)


ROW_NORM_G, ROW_CONV_W, ROW_CONV_B, ROW_B_A, ROW_B_X, ROW_LAMBDA, ROW_SC_W, ROW_POOL_B, ROW_POOL_S, ROW_FINAL_G = (
    0, 2, 6, 7, 8, 9, 10, 13, 15, 17)
ROW_LOSS = 18
DMOD_W = 6 * D // SUBLANES


def _small_pack(s1_0, s2_0, s1_1, s2_1, sm0, dsc1, dbg1, dgf, losscols, dgate0, dgate1, norm_g, sc0, sc1, lam):
    def body(s1_0r, s2_0r, s1_1r, s2_1r, sm, dsc, dbg, dgfr, lcols, dg0, dg1, ng, sc0r, sc1r, lamr, buf, dmod):
        buf[...] = jnp.zeros_like(buf)
        buf[0:1, :] = s2_0r[...] * (1.0 + sc0r[...])
        buf[1:2, :] = s2_1r[...] * (1.0 + sc1r[...])
        buf[ROW_CONV_W:ROW_CONV_W + 4, :] = sm[0:4, :]
        buf[ROW_CONV_B:ROW_CONV_B + 1, :] = sm[4:5, :]
        buf[ROW_B_A:ROW_B_A + 1, :] = sm[5:6, :]
        buf[ROW_B_X:ROW_B_X + 1, :] = sm[6:7, :]
        buf[ROW_LAMBDA:ROW_LAMBDA + 1, :] = -sm[7:8, :] * _sigmoid(-lamr[...])
        buf[ROW_SC_W:ROW_SC_W + 3, :] = sm[8:11, :]
        for k in range(2):
            buf[ROW_POOL_B + k:ROW_POOL_B + k + 1, :] = dbg[:, k * D:(k + 1) * D]
            buf[ROW_POOL_S + k:ROW_POOL_S + k + 1, :] = dsc[:, k * D:(k + 1) * D]
        buf[ROW_FINAL_G:ROW_FINAL_G + 1, :] = dgfr[...]
        pieces = (s1_0r[...], s2_0r[...] * ng[0:1, :], dg0[...], s1_1r[...], s2_1r[...] * ng[1:2, :], dg1[...])
        flat = jnp.concatenate(pieces, axis=1)
        for r in range(SUBLANES):
            dmod[r:r + 1, :] = flat[:, r * DMOD_W:(r + 1) * DMOD_W]
        buf[ROW_LOSS:ROW_LOSS + 1, :] = jnp.broadcast_to(jnp.sum(lcols[...], axis=1, keepdims=True) * (0.5 / D), (1, D))

    args = (s1_0, s2_0, s1_1, s2_1, sm0, dsc1, dbg1, dgf, losscols, dgate0, dgate1, norm_g, sc0, sc1, lam)
    return pl.pallas_call(
        body, name="small_pack", in_specs=[VMEM] * len(args), out_specs=[VMEM] * 2,
        out_shape=[jax.ShapeDtypeStruct((SMALL_ROWS, D), F32), jax.ShapeDtypeStruct((SUBLANES, DMOD_W), F32)],
        compiler_params=_cp(),
    )(*args)


def _small_comm(buf_a, buf_b, dmod8):
    ra, rb = buf_a.shape[0] // N_DEV, buf_b.shape[0] // N_DEV
    wb = buf_b.shape[1]

    def body(a_ref, b_ref, dm_ref, oa_ref, ob_ref, odm_ref, ina, inb, dslot, sa, sb, s1, r1, s2, r2):
        x, y, c = _pos()
        me = 4 * x + 2 * y + c
        peers = []
        for r in range(1, N_DEV):
            fx, fy, fc = (r >> 2) & 1, (r >> 1) & 1, r & 1
            px, py, pc = _flip(x, fx), _flip(y, fy), _flip(c, fc)
            peers.append(((px, py, pc), 4 * px + 2 * py + pc))
        seg_a = lambda d: pl.ds(pl.multiple_of(d * ra, SUBLANES), ra)
        seg_b = lambda d: pl.ds(pl.multiple_of(d * rb, SUBLANES), rb)
        first = []
        for r, (peer, pid) in enumerate(peers):
            for k, (src, dst) in enumerate(((a_ref.at[seg_a(pid), :], ina.at[r]), (b_ref.at[seg_b(pid), :], inb.at[r]),
                                            (dm_ref, dslot.at[me]))):
                cp = pltpu.make_async_remote_copy(src_ref=src, dst_ref=dst, send_sem=s1.at[3 * r + k],
                                                  recv_sem=r1.at[3 * r + k], device_id=peer, device_id_type=MESH)
                cp.start()
                first.append(cp)
        dslot[me] = dm_ref[...]
        for cp in first:
            cp.wait()
        acc_a, acc_b = a_ref[seg_a(me), :], b_ref[seg_b(me), :]
        for r in range(N_DEV - 1):
            acc_a = acc_a + ina[r]
            acc_b = acc_b + inb[r]
        sa[...] = acc_a
        sb[...] = acc_b
        oa_ref[seg_a(me), :] = acc_a
        ob_ref[seg_b(me), :] = acc_b
        second = []
        for r, (peer, pid) in enumerate(peers):
            for k, (src, dst) in enumerate(((sa, oa_ref.at[seg_a(me), :]), (sb, ob_ref.at[seg_b(me), :]))):
                cp = pltpu.make_async_remote_copy(src_ref=src, dst_ref=dst, send_sem=s2.at[2 * r + k],
                                                  recv_sem=r2.at[2 * r + k], device_id=peer, device_id_type=MESH)
                cp.start()
                second.append(cp)
        odm_ref[...] = dslot[...]
        for cp in second:
            cp.wait()

    nrel = N_DEV - 1
    return pl.pallas_call(
        body, name="small_comm", in_specs=[VMEM] * 3, out_specs=[VMEM] * 3,
        out_shape=[jax.ShapeDtypeStruct(buf_a.shape, F32), jax.ShapeDtypeStruct(buf_b.shape, F32),
                   jax.ShapeDtypeStruct((N_DEV,) + dmod8.shape, F32)],
        scratch_shapes=[pltpu.VMEM((nrel, ra, D), F32), pltpu.VMEM((nrel, rb, wb), F32),
                        pltpu.VMEM((N_DEV,) + dmod8.shape, F32), pltpu.VMEM((ra, D), F32), pltpu.VMEM((rb, wb), F32),
                        pltpu.SemaphoreType.DMA((3 * nrel,)), pltpu.SemaphoreType.DMA((3 * nrel,)),
                        pltpu.SemaphoreType.DMA((2 * nrel,)), pltpu.SemaphoreType.DMA((2 * nrel,))],
        compiler_params=_cp(),
    )(buf_a, buf_b, dmod8)


def _adam(w, g, m, v):
    m2 = ADAM_B1 * m + (1.0 - ADAM_B1) * g
    v2 = ADAM_B2 * v + (1.0 - ADAM_B2) * (g * g)
    m_hat = m2 / (1.0 - ADAM_B1 ** ADAM_STEP)
    v_hat = v2 / (1.0 - ADAM_B2 ** ADAM_STEP)
    return -ADAM_LR * (m_hat / (jnp.sqrt(v_hat) + ADAM_EPS) + ADAM_WD * w), m2, v2


def _small_adam(red_a, red_b, dm_all, params):
    n = len(params)

    def body(*refs):
        ra, rb, dm = refs[:3]
        wmv = refs[3:3 + 3 * n]
        outs = refs[3 + 3 * n:]
        x, y, _ = _pos()
        chip = 2 * x + y

        def shard(row0, nrows, width):
            per_row = D // width
            cands = []
            for k in range(N_CHIP):
                if nrows == 1 or per_row >= N_CHIP:
                    cands.append(ra[row0:row0 + nrows, k * width:(k + 1) * width])
                else:
                    rr, cc = divmod(k * width, D)
                    cands.append(ra[row0 + rr:row0 + rr + 1, cc:cc + width])
            g = cands[0]
            for k in range(1, N_CHIP):
                g = jnp.where(chip == k, cands[k], g)
            return g

        dms = jnp.sum(dm[...], axis=0)
        hw = LRU_HEADS * LRU_HEAD_DIM
        grads = [
            ra[ROW_NORM_G:ROW_NORM_G + 2, :],
            None,
            shard(ROW_CONV_W, 4, D // N_CHIP),
            ra[ROW_CONV_B:ROW_CONV_B + 1, :],
            rb[0:hw, :],
            ra[ROW_B_A:ROW_B_A + 1, :],
            rb[hw:2 * hw, :],
            ra[ROW_B_X:ROW_B_X + 1, :],
            ra[ROW_LAMBDA:ROW_LAMBDA + 1, :],
            shard(ROW_SC_W, 3, D // N_CHIP),
            shard(ROW_POOL_B, 2, 2 * D // N_CHIP),
            shard(ROW_POOL_S, 2, 2 * D // N_CHIP),
            ra[ROW_FINAL_G:ROW_FINAL_G + 1, :],
        ]
        for p in range(n):
            w_ref, m_ref, v_ref = wmv[3 * p:3 * p + 3]
            g_out, d_out, m_out, v_out = outs[4 * p:4 * p + 4]
            if grads[p] is None:
                for r in range(SUBLANES):
                    l, cols = r // N_CHIP, slice((r % N_CHIP) * DMOD_W, (r % N_CHIP + 1) * DMOD_W)
                    g = dms[r:r + 1, :]
                    dl, m2, v2 = _adam(w_ref[l:l + 1, cols], g, m_ref[l:l + 1, cols], v_ref[l:l + 1, cols])
                    g_out[l:l + 1, cols] = g
                    d_out[l:l + 1, cols] = dl
                    m_out[l:l + 1, cols] = m2
                    v_out[l:l + 1, cols] = v2
            else:
                g = grads[p]
                dl, m2, v2 = _adam(w_ref[...], g, m_ref[...], v_ref[...])
                g_out[...] = g
                d_out[...] = dl
                m_out[...] = m2
                v_out[...] = v2

    flat = [a for p in params for a in p]
    return pl.pallas_call(
        body, name="small_adam", in_specs=[VMEM] * (3 + len(flat)), out_specs=[VMEM] * (4 * n),
        out_shape=[jax.ShapeDtypeStruct(p[0].shape, F32) for p in params for _ in range(4)],
        compiler_params=_cp(),
    )(red_a, red_b, dm_all, *flat)


def _modw_adam(ca_t, dm_sh, w, m, v):
    nw = w.shape[2]

    def body(c_ref, d_ref, w_ref, m_ref, v_ref, g_out, d_out, m_out, v_out):
        g = jnp.dot(c_ref[...], d_ref[...], precision=lax.Precision.HIGHEST, preferred_element_type=F32)
        dl, m2, v2 = _adam(w_ref[...], g, m_ref[...], v_ref[...])
        g_out[...] = g
        d_out[...] = dl
        m_out[...] = m2
        v_out[...] = v2

    blk = pl.BlockSpec((None, D, nw), lambda l: (l, 0, 0))
    return pl.pallas_call(
        body, name="modw_adam", grid=(2,),
        in_specs=[pl.BlockSpec((D, SUBLANES), lambda l: (0, 0)), pl.BlockSpec((None, SUBLANES, nw), lambda l: (l, 0, 0)),
                  blk, blk, blk],
        out_specs=[blk] * 4, out_shape=[jax.ShapeDtypeStruct(w.shape, F32)] * 4,
        compiler_params=_cp(("arbitrary",)),
    )(ca_t, dm_sh, w, m, v)


def _exchange(copies, name, out_type, n_sems, args, sequencer, after=None):
    order = [] if after is None else [after]
    n_in, n_out = len(args) + len(order), len(out_type)

    def body(*refs):
        barrier = pltpu.get_barrier_semaphore()
        peers = sequencer[1](*_pos())
        for peer in peers:
            pl.semaphore_signal(barrier, inc=1, device_id=peer, device_id_type=MESH)
        pl.semaphore_wait(barrier, len(peers))
        copies(refs[:n_in], refs[n_in:n_in + n_out], refs[n_in + n_out], refs[n_in + n_out + 1])

    sems = [pltpu.SemaphoreType.DMA((n_sems,))] * 2
    return pl.kernel(body, out_type, mesh=plsc.ScalarSubcoreMesh(axis_name="sequencer", num_cores=1), name=name,
                     scratch_types=sems, compiler_params=pltpu.CompilerParams(collective_id=sequencer[0]))(*args, *order)


def _sibling(x, y, c):
    return [(x, y, 1 - c)]


def _other_chips(x, y, c):
    return [(1 - x, y, c), (x, 1 - y, c), (1 - x, 1 - y, c)]


def _to_wire(g, name, after=None):
    _, rr, cc = g.shape
    rb = min(rr, 256)

    def body(g_ref, *rest):
        rest[-1][...] = g_ref[...].astype(GRAD_WIRE_DTYPE)

    order = [] if after is None else [after]
    blk = pl.BlockSpec((None, rb, cc), lambda k, j: (k, j, 0))
    return pl.pallas_call(
        body, name=name, grid=(N_CHIP, rr // rb), in_specs=[blk] + [ANY] * len(order), out_specs=blk,
        out_shape=jax.ShapeDtypeStruct(g.shape, GRAD_WIRE_DTYPE), compiler_params=_cp(("parallel", "parallel")),
    )(g, *order)


def _chip_scatter(ps, name, collective_id, after=None):
    n = len(ps)

    def copies(ins, outs, ssem, rsem):
        x, y, c = _pos()
        cps = []
        for a in range(n):
            for q, (fx, fy) in enumerate(((1, 0), (0, 1), (1, 1))):
                px, py = _flip(x, fx), _flip(y, fy)
                cp = pltpu.make_async_remote_copy(
                    src_ref=ins[a].at[2 * px + py], dst_ref=outs[a].at[q],
                    send_sem=ssem.at[3 * a + q], recv_sem=rsem.at[3 * a + q], device_id=(px, py, c), device_id_type=MESH)
                cp.start()
                cps.append(cp)
        for cp in cps:
            cp.wait()

    out_type = [jax.ShapeDtypeStruct((N_CHIP - 1,) + p.shape[1:], p.dtype) for p in ps]
    return _exchange(copies, name, out_type, 3 * n, ps, (collective_id, _other_chips), after)


def _add_owner(p, got, chipidx, name, after=None):
    _, hr, cc = p.shape
    rb = min(hr, 256)

    def body(k_ref, p_ref, r_ref, *rest):
        rest[-1][...] = ((p_ref[...].astype(F32) + r_ref[0].astype(F32)) + r_ref[1].astype(F32)) + r_ref[2].astype(F32)

    order = [] if after is None else [after]
    return pl.pallas_call(
        body, name=name,
        grid_spec=pltpu.PrefetchScalarGridSpec(
            num_scalar_prefetch=1, grid=(hr // rb,),
            in_specs=[pl.BlockSpec((None, rb, cc), lambda j, k_ref: (k_ref[0], j, 0)),
                      pl.BlockSpec((N_CHIP - 1, rb, cc), lambda j, k_ref: (0, j, 0))] + [ANY] * len(order),
            out_specs=pl.BlockSpec((rb, cc), lambda j, k_ref: (j, 0))),
        out_shape=jax.ShapeDtypeStruct((hr, cc), F32),
        compiler_params=_cp(("parallel",)),
    )(chipidx, p, got, *order)


def _sib_exchange(ts_, name, collective_id, after=None):
    n = len(ts_)

    def copies(ins, outs, ssem, rsem):
        x, y, c = _pos()
        cps = []
        for a in range(n):
            cp = pltpu.make_async_remote_copy(src_ref=ins[a], dst_ref=outs[a], send_sem=ssem.at[a],
                                              recv_sem=rsem.at[a], device_id=(x, y, 1 - c), device_id_type=MESH)
            cp.start()
            cps.append(cp)
        for cp in cps:
            cp.wait()

    out_type = [jax.ShapeDtypeStruct(t.shape, F32) for t in ts_]
    return _exchange(copies, name, out_type, n, ts_, (collective_id, _sibling), after)


def _adam_2d(w, g_own, g_sib, m, v, name):
    rr, cc = w.shape
    rb = min(rr, 256)

    def body(w_ref, go_ref, gs_ref, m_ref, v_ref, g_out, d_out, m_out, v_out):
        g = go_ref[...] + gs_ref[...]
        dl, m2, v2 = _adam(w_ref[...], g, m_ref[...], v_ref[...])
        g_out[...] = g
        d_out[...] = dl
        m_out[...] = m2
        v_out[...] = v2

    blk = pl.BlockSpec((rb, cc), lambda j: (j, 0))
    return pl.pallas_call(
        body, name=name, grid=(rr // rb,), in_specs=[blk] * 5, out_specs=[blk] * 4,
        out_shape=[jax.ShapeDtypeStruct((rr, cc), F32)] * 4, compiler_params=_cp(("parallel",)),
    )(w, g_own, g_sib, m, v)


def kernel(x, c, norm_g, mod_w, mod_b, hy_w_in, hy_conv_w, hy_conv_b, lru_w_a, lru_b_a, lru_w_x, lru_b_x, lru_lambda, sc_conv_w, hy_w_out, pool_w_in, pool_w_grp, pool_b_grp, pool_scale, pool_w_out, final_g, loss_target, m_norm_g, m_mod_w, m_mod_b, m_hy_w_in, m_hy_conv_w, m_hy_conv_b, m_lru_w_a, m_lru_b_a, m_lru_w_x, m_lru_b_x, m_lru_lambda, m_sc_conv_w, m_hy_w_out, m_pool_w_in, m_pool_w_grp, m_pool_b_grp, m_pool_scale, m_pool_w_out, m_final_g, v_norm_g, v_mod_w, v_mod_b, v_hy_w_in, v_hy_conv_w, v_hy_conv_b, v_lru_w_a, v_lru_b_a, v_lru_w_x, v_lru_b_x, v_lru_lambda, v_sc_conv_w, v_hy_w_out, v_pool_w_in, v_pool_w_grp, v_pool_b_grp, v_pool_scale, v_pool_w_out, v_final_g):
    ax, ay, ac = _pos()
    me = 4 * ax + 2 * ay + ac
    chip = 2 * ax + ay
    xs = x[0]
    tgt = loss_target[0]
    gd = POOL_GROUP_DIM
    kidx = chip.reshape(1).astype(jnp.int32)

    big = [hy_w_in[0], hy_w_out[0], pool_w_in[0], pool_w_grp[0].reshape(4 * 128, gd), pool_w_out[0]]
    w_in0, w_out0 = _wgather_sequencer(
        [_wcast_own_block(w, kidx, f"wcast_own_block_{a}") for a, w in enumerate(big[:2])], "wgather_l0", CIDS_WGATHER[0])

    ca_all, mod_all, small_w = _mod_fwd(jnp.broadcast_to(c, (SUBLANES, D)), mod_w, mod_b,
                                        hy_conv_w[0], sc_conv_w[0], pool_b_grp, pool_scale)
    mod_me = lax.dynamic_index_in_dim(mod_all, me, axis=1, keepdims=False)
    sh0, sc0, gt0 = (mod_me[0:1, k * D:(k + 1) * D] for k in range(3))
    sh1, sc1, gt1 = (mod_me[1:2, k * D:(k + 1) * D] for k in range(3))
    cw = small_w[SW_CONV:SW_CONV + 4, 0:D]
    sw = small_w[SW_SC:SW_SC + 3, 0:D]
    pool_b = small_w[SW_POOL_B:SW_POOL_B + 1, :]
    pool_s = small_w[SW_POOL_S:SW_POOL_S + 1, :]
    g0, g1, gf = norm_g[0:1], norm_g[1:2], final_g.reshape(1, D)
    cb, ba, bx, lam = hy_conv_b, lru_b_a, lru_b_x, lru_lambda

    h0 = _norm_mod(xs, g0, sc0, sh0, "l0_norm")
    wa_b, wx_b = _wcast([lru_w_a[0], lru_w_x[0]])
    w_in1, w_grp, w_out1 = _wgather_sequencer(
        [_wcast_own_block(w, kidx, f"wcast_own_block_{a + 2}", after=(w_out0, h0)) for a, w in enumerate(big[2:])],
        "wgather_l1", CIDS_WGATHER[1])
    w_grp =w_grp.reshape(N_CHIP, 4, 128, gd).transpose(1, 0, 2, 3).reshape(4, gd, gd)

    x1, hst, y0, xc0, cz0, rg0, ig0, proj0 = _l0_fwd(h0, xs, w_in0, gt0, cw, cb, wa_b, ba, wx_b, bx, lam, sw,
                                                     w_out0.reshape(2 * D, D))
    dpool, mixed, y1, dx2, losscols, dgf, h1, proj1 = _l1_fwd(x1, g1, sc1, sh1, w_in1, tgt, gt1, w_grp, pool_b, pool_s,
                                                              w_out1.reshape(2 * D, D), gf)

    def add_owners(grads, got, tag, ids, after):
        own = []
        for a, (g, r) in enumerate(zip(grads, got)):
            own.append(_add_owner(g, r, kidx, f"grad_add_owner_{tag}{a}", own[-1] if own else after))
        return own, _sib_exchange(own, f"grad_sib_exchange_{tag}", ids[1])

    dproj1, mt1, d_wgrp, dsc1, dbg1 = _l1_bwd_mix(dx2, proj1, mixed, y1, dpool, gt1, w_grp, pool_s,
                                                  w_out1.reshape(2 * D, D))
    d_win1, wire_win1 = _wgrad(h1, dproj1, N_CHIP, D, D, lambda g: 0, lambda g: g, "l1_wgrad_in")
    d_wout1, wire_wout1, dgate1 = _wo_final(mt1, w_out1, gt1, "l1_wo_final")
    d_wgrp = d_wgrp.reshape(4, N_CHIP, 128, gd).transpose(1, 0, 2, 3).reshape(N_CHIP, 4 * 128, gd)
    grads_l1 = [d_win1, d_wgrp, d_wout1]
    got_l1 = _chip_scatter([wire_win1, _to_wire(d_wgrp, "grad_to_wire_grp"), wire_wout1], "grad_chip_scatter_l1",
                           CIDS_L1[0])
    dx1, s1_1, s2_1 = _dgrad_norm(dproj1, w_in1, x1, dx2, g1, sc1, "l1_bwd_proj")

    dproj0, mt0, d_wa, d_wx, sm0 = _l0_bwd_mix(dx1, proj0, hst, y0, xc0, cz0, rg0, ig0, gt0, cw, wa_b, wx_b, lam, sw,
                                               w_out0.reshape(2 * D, D))
    sums_l1, sib_l1 = add_owners(grads_l1, got_l1, "l1", CIDS_L1, after=sm0)
    d_win0, wire_win0 = _wgrad(h0, dproj0, N_CHIP, D, 6 * D // N_CHIP, lambda g: 0, lambda g: g, "l0_wgrad_in",
                               after=sums_l1[-1])
    d_wout0, wire_wout0, dgate0 = _wo_final(mt0, w_out0, gt0, "l0_wo_final")
    grads_l0 = [d_win0, d_wout0]
    got_l0 = _chip_scatter([wire_win0, wire_wout0], "grad_chip_scatter_l0", CIDS_L0[0], after=sib_l1[0])
    grad_x, s1_0, s2_0 = _dgrad_norm(dproj0, w_in0, xs, dx1, g0, sc0, "l0_bwd_proj", after=wire_win0)
    sums_l0, sib_l0 = add_owners(grads_l0, got_l0, "l0", CIDS_L0, after=s1_0)

    buf_a, dmod8 = _small_pack(s1_0, s2_0, s1_1, s2_1, sm0, dsc1, dbg1, dgf, losscols, dgate0, dgate1,
                                      norm_g, sc0, sc1, lam)
    hw = LRU_HEADS * LRU_HEAD_DIM
    buf_b = jnp.concatenate([d_wa.reshape(hw, LRU_HEAD_DIM), d_wx.reshape(hw, LRU_HEAD_DIM)], axis=0)
    red_a, red_b, dm_all = _small_comm(buf_a, buf_b, dmod8)
    small = [(norm_g, m_norm_g, v_norm_g), (mod_b, m_mod_b, v_mod_b),
             (hy_conv_w[0], m_hy_conv_w[0], v_hy_conv_w[0]), (hy_conv_b, m_hy_conv_b, v_hy_conv_b),
             tuple(a.reshape(hw, LRU_HEAD_DIM) for a in (lru_w_a, m_lru_w_a, v_lru_w_a)),
             (lru_b_a, m_lru_b_a, v_lru_b_a),
             tuple(a.reshape(hw, LRU_HEAD_DIM) for a in (lru_w_x, m_lru_w_x, v_lru_w_x)),
             (lru_b_x, m_lru_b_x, v_lru_b_x), (lru_lambda, m_lru_lambda, v_lru_lambda),
             (sc_conv_w[0], m_sc_conv_w[0], v_sc_conv_w[0]), (pool_b_grp, m_pool_b_grp, v_pool_b_grp),
             (pool_scale, m_pool_scale, v_pool_scale),
             tuple(a.reshape(1, D) for a in (final_g, m_final_g, v_final_g))]
    small_names = ["norm_g", "mod_b", "hy_conv_w", "hy_conv_b", "lru_w_a", "lru_b_a", "lru_w_x", "lru_b_x",
                   "lru_lambda", "sc_conv_w", "pool_b_grp", "pool_scale", "final_g"]
    small_out = _small_adam(red_a, red_b, dm_all, small)
    res = {}
    shapes = dict(norm_g=norm_g, mod_b=mod_b, hy_conv_w=hy_conv_w, hy_conv_b=hy_conv_b, lru_w_a=lru_w_a, lru_b_a=lru_b_a,
                  lru_w_x=lru_w_x, lru_b_x=lru_b_x, lru_lambda=lru_lambda, sc_conv_w=sc_conv_w, pool_b_grp=pool_b_grp,
                  pool_scale=pool_scale, final_g=final_g)
    for p, nm in enumerate(small_names):
        res[nm] = tuple(o.reshape(shapes[nm].shape) for o in small_out[4 * p:4 * p + 4])

    nw = mod_w.shape[2]
    assert nw == DMOD_W
    dm_sh = jnp.stack([lax.dynamic_index_in_dim(dm_all, N_CHIP * l + chip, axis=1, keepdims=False) for l in range(2)])
    res["mod_w"] = tuple(_modw_adam(ca_all.T, dm_sh, mod_w, m_mod_w, v_mod_w))

    sums = list(sums_l0) + list(sums_l1)
    sib_sums = list(sib_l0) + list(sib_l1)
    big_names = ["hy_w_in", "hy_w_out", "pool_w_in", "pool_w_grp", "pool_w_out"]
    big_wmv = [(hy_w_in, m_hy_w_in, v_hy_w_in), (hy_w_out, m_hy_w_out, v_hy_w_out), (pool_w_in, m_pool_w_in, v_pool_w_in),
               (pool_w_grp, m_pool_w_grp, v_pool_w_grp), (pool_w_out, m_pool_w_out, v_pool_w_out)]
    for a, nm in enumerate(big_names):
        rr, cc = big[a].shape
        w, m, v = (t.reshape(rr, cc) for t in big_wmv[a])
        outs = _adam_2d(w, sums[a], sib_sums[a], m, v, f"adam_{nm}")
        res[nm] = tuple(o.reshape(big_wmv[a][0].shape) for o in outs)

    loss = red_a[ROW_LOSS, 0]
    order = ["norm_g", "mod_w", "mod_b", "hy_w_in", "hy_conv_w", "hy_conv_b", "lru_w_a", "lru_b_a", "lru_w_x", "lru_b_x",
             "lru_lambda", "sc_conv_w", "hy_w_out", "pool_w_in", "pool_w_grp", "pool_b_grp", "pool_scale", "pool_w_out",
             "final_g"]
    return (loss, grad_x[None], *[res[nm][0] for nm in order], *[res[nm][1] for nm in order],
            *[res[nm][2] for nm in order], *[res[nm][3] for nm in order])
```

```python
import jax
import jax.numpy as jnp
from jax import lax
from jax.experimental import pallas as pl
from jax.experimental.pallas import tpu as pltpu
from jax.experimental.pallas import tpu_sc as plsc

F32, BF16 = jnp.float32, jnp.bfloat16
D = 1024
RMS_EPS = 1e-6
SQRT_FLOOR = 1e-30
LRU_C = 8.0
LRU_HEADS, LRU_HEAD_DIM = 8, 128
POOL_WINDOWS = (2, 4, 8, 16)
POOL_GROUP_DIM = 512
ADAM_LR, ADAM_B1, ADAM_B2, ADAM_EPS, ADAM_WD, ADAM_STEP = 0.001, 0.9, 0.999, 1e-08, 0.01, 10
MESH = pl.DeviceIdType.MESH
CIDS_WGATHER = (1, 8)
CIDS_L1 = (2, 3)
CIDS_L0 = (4, 5)
N_DEV, N_CHIP = 8, 4
SUBLANES = 8
BF16_ROWS = 16
POOL_HALO = 16
TS_MIX, TS_WGRAD, TS_DGRAD = 256, 2048, 512
SMALL_ROWS = 64
GRAD_WIRE_DTYPE = BF16
ANY = pl.BlockSpec(memory_space=pl.ANY)
VMEM = pl.BlockSpec(memory_space=pltpu.VMEM)
NT = (((1,), (1,)), ((), ()))
TN = (((0,), (0,)), ((), ()))


def _cp(sem=None, vmem_mb=56):
    kw = dict(vmem_limit_bytes=vmem_mb * 2 ** 20)
    if sem is not None:
        kw["dimension_semantics"] = sem
    return pltpu.CompilerParams(**kw)


def _tile(n, t):
    return min(n, t)


def _pos():
    return lax.axis_index("x"), lax.axis_index("y"), lax.axis_index("c")


def _flip(v, f):
    return 1 - v if f else v


def _sigmoid(z):
    return 0.5 * jnp.tanh(0.5 * z) + 0.5


def _rows(n, c):
    return lax.broadcasted_iota(jnp.int32, (n, c), 0)


def _down(a, d):
    return a if d == 0 else pltpu.roll(a, d, 0)


def _up(a, d):
    return a if d == 0 else pltpu.roll(a, a.shape[0] - d, 0)


def _scan_fwd_steps(a, u, carry):
    n, c = a.shape
    sub = _rows(SUBLANES, c)
    out = []
    for k in range(n // SUBLANES):
        p = a[k * SUBLANES:(k + 1) * SUBLANES]
        g = u[k * SUBLANES:(k + 1) * SUBLANES]
        for d in (1, 2, 4):
            keep = sub >= d
            g = g + p * jnp.where(keep, pltpu.roll(g, d, 0), 0.0)
            p = p * jnp.where(keep, pltpu.roll(p, d, 0), 1.0)
        h = g + p * carry
        carry = h[SUBLANES - 1:SUBLANES, :]
        out.append(h)
        yield
    return jnp.concatenate(out, axis=0)


def _scan_rev_steps(alpha, b, carry):
    n, c = alpha.shape
    sub = _rows(SUBLANES, c)
    out = []
    for k in reversed(range(n // SUBLANES)):
        p = alpha[k * SUBLANES:(k + 1) * SUBLANES]
        g = b[k * SUBLANES:(k + 1) * SUBLANES]
        for d in (1, 2, 4):
            keep = sub < SUBLANES - d
            g = g + p * jnp.where(keep, pltpu.roll(g, SUBLANES - d, 0), 0.0)
            p = p * jnp.where(keep, pltpu.roll(p, SUBLANES - d, 0), 1.0)
        h = g + p * carry
        carry = h[0:1, :]
        out.append(h)
        yield
    return jnp.concatenate(out[::-1], axis=0)


def _run(steps):
    while True:
        try:
            next(steps)
        except StopIteration as done:
            return done.value


def _paired(progress, pieces):
    n, done = len(pieces), 1
    pieces[0]()
    for frac in progress:
        while done < n and done <= frac * n:
            pieces[done]()
            done += 1
    while done < n:
        pieces[done]()
        done += 1


def _conv_taps(ext, halo, n, width):
    return [_down(ext, width - 1 - k)[halo:halo + n] for k in range(width)]


def _lru_gates(xc, wa_ref, ba, wx_ref, bx):
    xb = xc.astype(BF16)
    pa, px = [], []
    for h in range(LRU_HEADS):
        xh = xb[:, h * LRU_HEAD_DIM:(h + 1) * LRU_HEAD_DIM]
        pa.append(jnp.dot(xh, wa_ref[h], preferred_element_type=F32))
        px.append(jnp.dot(xh, wx_ref[h], preferred_element_type=F32))
    r = _sigmoid(jnp.concatenate(pa, axis=1) + ba)
    ig = _sigmoid(jnp.concatenate(px, axis=1) + bx)
    return r, ig


def _softplus_neg(lam):
    return jnp.maximum(-lam, 0.0) + jnp.log1p(jnp.exp(-jnp.abs(lam)))


def _recip_1_to_2(d):
    r0 = pl.reciprocal(d, approx=True)
    return r0 * (2.0 - d * r0)


def _lru_decay(r, sp, first):
    big_l = (-LRU_C) * r * sp
    a = jnp.exp(big_l)
    th = jnp.tanh(big_l)
    q = (-2.0 * th) * _recip_1_to_2(1.0 - th)
    rs = lax.rsqrt(jnp.maximum(q, SQRT_FLOOR))
    return a, jnp.where(first, 1.0, q * rs), rs


def _pool_inv_counts(t0, n):
    t = (t0 + lax.broadcasted_iota(jnp.int32, (n, 1), 0) + 1).astype(F32)
    return [1.0 / jnp.minimum(t, float(w)) for w in POOL_WINDOWS]


def _window_sums(ext, shift):
    gd = POOL_GROUP_DIM
    out = []
    s = ext
    for k in range(len(POOL_WINDOWS)):
        s = s + shift(s, 2 ** k)
        out.append(s[:, 0:gd])
        if k + 1 < len(POOL_WINDOWS):
            s = s[:, gd:]
    return out


SW_ROWS, SW_COLS = 16, 2 * D
SW_CONV, SW_SC, SW_POOL_B, SW_POOL_S = 0, 4, 8, 9


def _mod_fwd(c8, mod_w, mod_b, conv_w, sc_w, pool_b, pool_s):
    nw = mod_w.shape[2]
    cq, pq = conv_w.shape[1], pool_b.shape[1]

    def body(c_ref, w_ref, b_ref, cw_ref, sw_ref, pb_ref, ps_ref, ca_ref, mod_ref, small_ref,
             cslot, mslot, msend, pslot, psend, s1, r1, s2, r2, s3, r3):
        x, y, c = _pos()
        me = 4 * x + 2 * y + c
        chip = 2 * x + y
        first = []
        for r in range(1, N_DEV):
            fx, fy, fc = (r >> 2) & 1, (r >> 1) & 1, r & 1
            cp = pltpu.make_async_remote_copy(
                src_ref=c_ref, dst_ref=cslot.at[me], send_sem=s1.at[r - 1], recv_sem=r1.at[r - 1],
                device_id=(_flip(x, fx), _flip(y, fy), _flip(c, fc)), device_id_type=MESH)
            cp.start()
            first.append(cp)
        cslot[me] = c_ref[...]
        for cp in first:
            cp.wait()
        rows = _rows(SUBLANES, D)
        call = jnp.zeros((SUBLANES, D), F32)
        for d in range(N_DEV):
            call = jnp.where(rows == d, cslot[d], call)
        ca = call * _sigmoid(call)
        ca_ref[...] = ca
        for l in range(2):
            msend[l] = jnp.dot(ca, w_ref[l], precision=lax.Precision.HIGHEST, preferred_element_type=F32)
        psend[...] = jnp.zeros_like(psend)
        psend[SW_CONV:SW_CONV + 4, 0:cq] = cw_ref[...]
        psend[SW_SC:SW_SC + 3, 0:cq] = sw_ref[...]
        psend[SW_POOL_B:SW_POOL_B + 1, :] = pb_ref[...]
        psend[SW_POOL_S:SW_POOL_S + 1, :] = ps_ref[...]
        second = []
        for q, (fx, fy) in enumerate(((1, 0), (0, 1), (1, 1))):
            peer = (_flip(x, fx), _flip(y, fy), c)
            for src, dst, ss, rs in ((msend, mslot, s2, r2), (psend, pslot, s3, r3)):
                cp = pltpu.make_async_remote_copy(src_ref=src, dst_ref=dst.at[chip], send_sem=ss.at[q], recv_sem=rs.at[q],
                                                  device_id=peer, device_id_type=MESH)
                cp.start()
                second.append(cp)
        mslot[chip] = msend[...]
        pslot[chip] = psend[...]
        for cp in second:
            cp.wait()
        small_ref[...] = jnp.zeros_like(small_ref)
        for j in range(N_CHIP):
            for l in range(2):
                mod_ref[l, :, j * nw:(j + 1) * nw] = mslot[j, l] + b_ref[l:l + 1, j * nw:(j + 1) * nw]
            small_ref[0:SUBLANES, j * cq:(j + 1) * cq] = pslot[j, 0:SUBLANES, 0:cq]
            small_ref[SUBLANES:SW_ROWS, j * pq:(j + 1) * pq] = pslot[j, SUBLANES:SW_ROWS, :]

    args = (c8, mod_w, mod_b, conv_w, sc_w, pool_b, pool_s)
    dma3 = pltpu.SemaphoreType.DMA((N_CHIP - 1,))
    return pl.pallas_call(
        body, name="mod_fwd",
        in_specs=[VMEM] * len(args), out_specs=[VMEM] * 3,
        out_shape=[jax.ShapeDtypeStruct((SUBLANES, D), F32), jax.ShapeDtypeStruct((2, SUBLANES, N_CHIP * nw), F32),
                   jax.ShapeDtypeStruct((SW_ROWS, SW_COLS), F32)],
        scratch_shapes=[pltpu.VMEM((N_DEV, SUBLANES, D), F32), pltpu.VMEM((N_CHIP, 2, SUBLANES, nw), F32),
                        pltpu.VMEM((2, SUBLANES, nw), F32), pltpu.VMEM((N_CHIP, SW_ROWS, pq), F32),
                        pltpu.VMEM((SW_ROWS, pq), F32),
                        pltpu.SemaphoreType.DMA((N_DEV - 1,)), pltpu.SemaphoreType.DMA((N_DEV - 1,)),
                        dma3, dma3, dma3, dma3],
        compiler_params=_cp(),
    )(*args)


def _wcast(ws):
    def body(*refs):
        n = len(refs) // 2
        for a in range(n):
            refs[n + a][...] = refs[a][...].astype(BF16)

    return pl.pallas_call(
        body, name="wcast", in_specs=[VMEM] * len(ws), out_specs=[VMEM] * len(ws),
        out_shape=[jax.ShapeDtypeStruct(w.shape, BF16) for w in ws], compiler_params=_cp(),
    )(*ws)


def _wcast_own_block(w, kidx, name, after=()):
    rr, cc = w.shape
    rb = min(rr, 256)

    def body(k_ref, w_ref, *rest):
        rest[-1][...] = w_ref[...].astype(BF16)

    order = list(after)
    return pl.pallas_call(
        body, name=name,
        grid_spec=pltpu.PrefetchScalarGridSpec(
            num_scalar_prefetch=1, grid=(rr // rb,),
            in_specs=[pl.BlockSpec((rb, cc), lambda j, k_ref: (j, 0))] + [ANY] * len(order),
            out_specs=pl.BlockSpec((None, rb, cc), lambda j, k_ref: (k_ref[0], j, 0))),
        out_shape=jax.ShapeDtypeStruct((N_CHIP, rr, cc), BF16),
        compiler_params=_cp(("parallel",)),
    )(kidx, w, *order)


def _wgather_copies(outs, rows, ssem, rsem, fssem, frsem):
    n = len(outs)
    x, y, c = _pos()
    chip = 2 * x + y
    sib = (x, y, 1 - c)
    flips = ((1, 0), (0, 1), (1, 1))

    def half(a, which):
        hr = rows[a] // 2
        return pl.ds(pl.multiple_of(which * hr, BF16_ROWS), hr)

    sends = []
    for a in range(n):
        mine = outs[a].at[chip, half(a, c), :]
        for q, (fx, fy) in enumerate(flips):
            cp = pltpu.make_async_remote_copy(
                src_ref=mine, dst_ref=mine, send_sem=ssem.at[3 * a + q], recv_sem=rsem.at[3 * a + q],
                device_id=(_flip(x, fx), _flip(y, fy), c), device_id_type=MESH)
            cp.start()
            sends.append(cp)
    passed = []
    for a in range(n):
        for q, (fx, fy) in enumerate(flips):
            src_chip = 2 * _flip(x, fx) + _flip(y, fy)
            landed = outs[a].at[src_chip, half(a, c), :]
            pltpu.make_async_remote_copy(
                src_ref=landed, dst_ref=landed, send_sem=ssem.at[3 * a + q], recv_sem=rsem.at[3 * a + q],
                device_id=sib, device_id_type=MESH).wait_recv()
            cp = pltpu.make_async_remote_copy(
                src_ref=landed, dst_ref=landed, send_sem=fssem.at[3 * a + q], recv_sem=frsem.at[3 * a + q],
                device_id=sib, device_id_type=MESH)
            cp.start()
            passed.append(cp)
    for a in range(n):
        for q, (fx, fy) in enumerate(flips):
            src_chip = 2 * _flip(x, fx) + _flip(y, fy)
            other = outs[a].at[src_chip, half(a, 1 - c), :]
            pltpu.make_async_remote_copy(
                src_ref=other, dst_ref=other, send_sem=fssem.at[3 * a + q], recv_sem=frsem.at[3 * a + q],
                device_id=sib, device_id_type=MESH).wait_recv()
    for cp in sends + passed:
        cp.wait_send()


def _wgather_sequencer(bufs, name, collective_id):
    n = len(bufs)
    refs = [jax.new_ref(b, memory_space=pltpu.MemorySpace.HBM) for b in bufs]
    dma = pltpu.SemaphoreType.DMA((3 * n,))

    @pl.kernel(mesh=plsc.ScalarSubcoreMesh(axis_name="sequencer", num_cores=1), name=name,
               scratch_types=(dma, dma, dma, dma), compiler_params=pltpu.CompilerParams(collective_id=collective_id))
    def launch(ssem, rsem, fssem, frsem):
        x, y, c = _pos()
        barrier = pltpu.get_barrier_semaphore()
        for peer in ((1 - x, y, c), (x, 1 - y, c), (1 - x, 1 - y, c), (x, y, 1 - c)):
            pl.semaphore_signal(barrier, inc=1, device_id=peer, device_id_type=MESH)
        pl.semaphore_wait(barrier, 4)
        _wgather_copies(refs, [b.shape[1] for b in bufs], ssem, rsem, fssem, frsem)

    launch()
    return [r[...] for r in refs]


def _norm_mod(x, g, sc, sh, name):
    s_len = x.shape[0]
    ts = _tile(s_len, TS_WGRAD)

    def body(x_ref, g_ref, sc_ref, sh_ref, h_ref):
        xv = x_ref[...]
        rinv = lax.rsqrt(jnp.mean(xv * xv, axis=-1, keepdims=True) + RMS_EPS)
        h_ref[...] = (xv * rinv * (g_ref[...] * (1.0 + sc_ref[...])) + sh_ref[...]).astype(BF16)

    row = pl.BlockSpec((ts, D), lambda i: (i, 0))
    vec = pl.BlockSpec((1, D), lambda i: (0, 0))
    return pl.pallas_call(
        body, name=name, grid=(s_len // ts,), in_specs=[row, vec, vec, vec], out_specs=row,
        out_shape=jax.ShapeDtypeStruct((s_len, D), BF16), compiler_params=_cp(("parallel",)),
    )(x, g, sc, sh)


def _l0_fwd(h0_all, x, w_in, gate, cw, cb, wa, ba, wx, bx, lam, sw, wo):
    s_len, nb = x.shape[0], w_in.shape[2]
    ts = _tile(s_len, TS_MIX)
    n_t = s_len // ts
    hl = SUBLANES

    def body(h0_ref, xb_ref, win_ref, gate_ref, cw_ref, cb_ref, wa_ref, ba_ref, wx_ref, bx_ref,
             lam_ref, sw_ref, wo_ref, x1_ref, h_ref, y_ref, xc_ref, cz_ref, r_ref, ig_ref, p_ref,
             pcur, pnext, cxa, czz, chh):
        i = pl.program_id(0)

        @pl.when(i == 0)
        def _():
            cxa[...] = jnp.zeros_like(cxa)
            czz[...] = jnp.zeros_like(czz)
            chh[...] = jnp.zeros_like(chh)
            pnext[...] = jnp.zeros_like(pnext)

        pcur[...] = pnext[...]
        h0 = h0_ref[...]

        def project(k, c0, cn):
            def emit():
                pk = jnp.dot(h0, win_ref[k, :, c0:c0 + cn], preferred_element_type=F32).astype(BF16)
                p_ref[:, k * nb + c0:k * nb + c0 + cn] = pk
                pnext[:, k * nb + c0:k * nb + c0 + cn] = pk
            return emit

        def mixer():
            piece = lambda k: pcur[:, k * D:(k + 1) * D].astype(F32)
            xa = piece(0)
            rows = _rows(ts, D)
            taps = _conv_taps(jnp.concatenate([cxa[...], xa], axis=0), hl, ts, 4)
            xc = cb_ref[...] + sum(cw_ref[k:k + 1, :] * taps[k] for k in range(4))
            xc_ref[...] = xc.astype(BF16)
            r, ig = _lru_gates(xc, wa_ref, ba_ref[...], wx_ref, bx_ref[...])
            r_ref[...] = r.astype(BF16)
            ig_ref[...] = ig.astype(BF16)
            a, m, _ = _lru_decay(r, _softplus_neg(lam_ref[...]), (rows == 0) & (i == 1))
            yield 0.26
            h = _run(_scan_fwd_steps(a, m * ig * xc, chh[hl - 1:hl, :]))
            yield 0.51
            gcp, v = piece(3), piece(4)
            z = gcp * v
            ztaps = _conv_taps(jnp.concatenate([czz[...], z], axis=0), hl, ts, 3)
            cz = sum(sw_ref[k:k + 1, :] * ztaps[k] for k in range(3))
            cz_ref[...] = cz.astype(BF16)
            yb = piece(2) * cz
            ga, gb = piece(1), piece(5)
            y = jnp.concatenate([h * (ga * _sigmoid(ga)), yb * (gb * _sigmoid(gb))], axis=1).astype(BF16)
            yield 0.76
            y_ref[...] = y
            x1_ref[...] = xb_ref[...] + gate_ref[...] * jnp.dot(y, wo_ref[...], preferred_element_type=F32)
            h_ref[...] = h.astype(BF16)
            cxa[...] = xa[ts - hl:, :]
            czz[...] = z[ts - hl:, :]
            chh[...] = jnp.where(i > 0, h[ts - hl:, :], 0.0)

        _paired(mixer(), [project(k, 0, nb) for k in range(N_CHIP)])

    def full(a):
        return pl.BlockSpec(a.shape, lambda i: (0,) * a.ndim)

    ahead = lambda w: pl.BlockSpec((ts, w), lambda i: (jnp.minimum(i, n_t - 1), 0))
    behind = lambda w: pl.BlockSpec((ts, w), lambda i: (jnp.maximum(i - 1, 0), 0))
    args = (h0_all, x, w_in, gate, cw, cb, wa, ba, wx, bx, lam, sw, wo)
    return pl.pallas_call(
        body, name="l0_fwd", grid=(n_t + 1,),
        in_specs=[ahead(D), behind(D)] + [full(a) for a in args[2:]],
        out_specs=[behind(D), behind(D), behind(2 * D)] + [behind(D)] * 4 + [ahead(N_CHIP * nb)],
        out_shape=[jax.ShapeDtypeStruct((s_len, D), F32), jax.ShapeDtypeStruct((s_len, D), BF16),
                   jax.ShapeDtypeStruct((s_len, 2 * D), BF16)] + [jax.ShapeDtypeStruct((s_len, D), BF16)] * 4
        + [jax.ShapeDtypeStruct((s_len, N_CHIP * nb), BF16)],
        scratch_shapes=[pltpu.VMEM((ts, N_CHIP * nb), BF16)] * 2 + [pltpu.VMEM((hl, D), F32)] * 3,
        compiler_params=_cp(("arbitrary",)),
    )(*args)


def _l1_fwd(x1, g, sc, sh, w_in, tgt, gate, wg, bg, scale, wo, gf):
    s_len, nb = x1.shape[0], w_in.shape[2]
    ts = _tile(s_len, TS_MIX)
    n_t = s_len // ts
    pw, gd, hl = 2 * D, POOL_GROUP_DIM, POOL_HALO

    def body(xa_ref, xb_ref, t_ref, g_ref, sc_ref, sh_ref, win_ref, gate_ref, wg_ref, bg_ref, scl_ref, wo_ref, gf_ref,
             d_ref, mx_ref, y_ref, dx_ref, loss_ref, dgf_ref, h1_ref, p_ref, pcur, pnext, cv):
        i = pl.program_id(0)

        @pl.when(i == 0)
        def _():
            cv[...] = jnp.zeros_like(cv)
            loss_ref[...] = jnp.zeros_like(loss_ref)
            dgf_ref[...] = jnp.zeros_like(dgf_ref)
            pnext[...] = jnp.zeros_like(pnext)

        pcur[...] = pnext[...]
        xv = xa_ref[...]
        rinv = lax.rsqrt(jnp.mean(xv * xv, axis=-1, keepdims=True) + RMS_EPS)
        h1 = (xv * rinv * (g_ref[...] * (1.0 + sc_ref[...])) + sh_ref[...]).astype(BF16)
        h1_ref[...] = h1

        def project(k):
            def emit():
                pk = jnp.dot(h1, win_ref[k], preferred_element_type=F32).astype(BF16)
                p_ref[:, k * nb:(k + 1) * nb] = pk
                pnext[:, k * nb:(k + 1) * nb] = pk
            return emit

        def mixer():
            v = pcur[:, 0:pw].astype(F32)
            sums = _window_sums(jnp.concatenate([cv[...], v], axis=0), _down)
            inv = _pool_inv_counts(jnp.maximum(i - 1, 0) * ts, ts)
            dd = [sums[k][hl:hl + ts] * inv[k] - v[:, k * gd:(k + 1) * gd] for k in range(4)]
            d_ref[...] = jnp.concatenate(dd, axis=1).astype(BF16)
            yield 0.26
            mixed = jnp.concatenate(
                [jnp.dot(dd[k].astype(BF16), wg_ref[k], preferred_element_type=F32) for k in range(4)], axis=1) + bg_ref[...]
            mx_ref[...] = mixed.astype(BF16)
            gg = pcur[:, pw:2 * pw].astype(F32)
            y = (mixed * scl_ref[...] * (gg * _sigmoid(gg))).astype(BF16)
            y_ref[...] = y
            yield 0.51
            x2 = xb_ref[...] + gate_ref[...] * jnp.dot(y, wo_ref[...], preferred_element_type=F32)
            yield 0.76
            r2 = lax.rsqrt(jnp.mean(x2 * x2, axis=-1, keepdims=True) + RMS_EPS)
            n2 = x2 * r2
            err = n2 * gf_ref[...] - t_ref[...]
            loss_ref[...] += jnp.where(i > 0, jnp.sum(err * err, axis=0, keepdims=True), 0.0)
            dyf = err * (1.0 / D)
            dgf_ref[...] += jnp.where(i > 0, jnp.sum(dyf * n2, axis=0, keepdims=True), 0.0)
            dn = dyf * gf_ref[...]
            dx_ref[...] = r2 * (dn - n2 * jnp.mean(dn * n2, axis=-1, keepdims=True))
            cv[...] = v[ts - hl:, :]

        _paired(mixer(), [project(k) for k in range(N_CHIP)])

    def full(a):
        return pl.BlockSpec(a.shape, lambda i: (0,) * a.ndim)

    ahead = lambda w: pl.BlockSpec((ts, w), lambda i: (jnp.minimum(i, n_t - 1), 0))
    behind = lambda w: pl.BlockSpec((ts, w), lambda i: (jnp.maximum(i - 1, 0), 0))
    acc = pl.BlockSpec((1, D), lambda i: (0, 0))
    args = (x1, x1, tgt, g, sc, sh, w_in, gate, wg, bg, scale, wo, gf)
    return pl.pallas_call(
        body, name="l1_fwd", grid=(n_t + 1,),
        in_specs=[ahead(D), behind(D), behind(D)] + [full(a) for a in args[3:]],
        out_specs=[behind(pw), behind(pw), behind(pw), behind(D), acc, acc, ahead(D), ahead(N_CHIP * nb)],
        out_shape=[jax.ShapeDtypeStruct((s_len, pw), BF16)] * 3 + [jax.ShapeDtypeStruct((s_len, D), F32)]
        + [jax.ShapeDtypeStruct((1, D), F32)] * 2
        + [jax.ShapeDtypeStruct((s_len, D), BF16), jax.ShapeDtypeStruct((s_len, N_CHIP * nb), BF16)],
        scratch_shapes=[pltpu.VMEM((ts, N_CHIP * nb), BF16)] * 2 + [pltpu.VMEM((hl, pw), F32)],
        compiler_params=_cp(("arbitrary",)),
    )(*args)


def _l1_bwd_mix(dx2, proj, mixed, y, dpool, gate, wg, scale, wo):
    s_len = dx2.shape[0]
    n_sub = 2
    ts = _tile(s_len, n_sub * TS_MIX)
    sub = ts // n_sub
    n_t = s_len // ts
    pw, gd, hl = 2 * D, POOL_GROUP_DIM, POOL_HALO

    def body(dx_ref, gg_ref, mx_ref, y_ref, d_ref, gate_ref, wg_ref, sc_ref, wo_ref,
             dp_ref, mt_ref, dwg_ref, dsc_ref, dbg_ref, cq):
        i = pl.program_id(0)

        @pl.when(i == 0)
        def _():
            cq[...] = jnp.zeros_like(cq)
            dsc_ref[...] = jnp.zeros_like(dsc_ref)
            dbg_ref[...] = jnp.zeros_like(dbg_ref)
            mt_ref[...] = jnp.zeros_like(mt_ref)
            dwg_ref[...] = jnp.zeros_like(dwg_ref)

        ahead_rows, dm_parts = {}, {}
        dxb_all = dx_ref[...].astype(BF16)

        def wgrad_out(k):
            mt_ref[k] += lax.dot_general(y_ref[:, k * gd:(k + 1) * gd], dxb_all, TN, preferred_element_type=F32)

        def chain(j):
            rows = slice(j * sub, (j + 1) * sub)
            dxv = dx_ref[rows, :]
            dy = lax.dot_general((gate_ref[...] * dxv).astype(BF16), wo_ref[...], NT, preferred_element_type=F32)
            yield
            gg = gg_ref[rows, :].astype(F32)
            mixed = mx_ref[rows, :].astype(F32)
            s = _sigmoid(gg)
            sg = gg * s
            dym = dy * mixed
            dmixed = dy * sc_ref[...] * sg
            dsc_ref[...] += jnp.sum(dym * sg, axis=0, keepdims=True)
            dbg_ref[...] += jnp.sum(dmixed, axis=0, keepdims=True)
            dmb = dmixed.astype(BF16)
            dm_parts[j] = dmb
            dp_ref[rows, pw:2 * pw] = (dym * sc_ref[...] * (s + sg * (1.0 - s))).astype(BF16)
            yield
            inv = _pool_inv_counts((n_t - 1 - i) * ts + j * sub, sub)
            dd = [lax.dot_general(dmb[:, k * gd:(k + 1) * gd], wg_ref[k], NT, preferred_element_type=F32)
                  for k in range(4)]
            q = jnp.concatenate([dd[k] * inv[k] for k in range(4)], axis=1)
            ahead_rows[j] = q[0:hl, :]
            yield
            behind_q = cq[...] if j == n_sub - 1 else ahead_rows[j + 1]
            sums = _window_sums(jnp.concatenate([q, behind_q], axis=0), _up)
            dp_ref[rows, 0:pw] = jnp.concatenate([sums[k][0:sub] - dd[k] for k in range(4)], axis=1).astype(BF16)

        chains = [chain(j) for j in reversed(range(n_sub))]
        for phase in range(4):
            for ch in chains:
                next(ch, None)
            wgrad_out(phase)
            if phase == 1:
                dmb_all = jnp.concatenate([dm_parts[j] for j in range(n_sub)], axis=0)
                for k in range(4):
                    cols = slice(k * gd, (k + 1) * gd)
                    dwg_ref[k] += lax.dot_general(d_ref[:, cols], dmb_all[:, cols], TN, preferred_element_type=F32)
        cq[...] = ahead_rows[0]

    def full(a):
        return pl.BlockSpec(a.shape, lambda i: (0,) * a.ndim)

    rev = lambda w, j=0: pl.BlockSpec((ts, w), lambda i: (n_t - 1 - i, j))
    acc = pl.BlockSpec((1, pw), lambda i: (0, 0))
    return pl.pallas_call(
        body, name="l1_bwd_mix", grid=(n_t,),
        in_specs=[rev(D), rev(pw, 1), rev(pw), rev(pw), rev(pw)] + [full(a) for a in (gate, wg, scale, wo)],
        out_specs=[rev(2 * pw), pl.BlockSpec((N_CHIP, gd, D), lambda i: (0, 0, 0)),
                   pl.BlockSpec((4, gd, gd), lambda i: (0, 0, 0)), acc, acc],
        out_shape=[jax.ShapeDtypeStruct((s_len, 2 * pw), BF16), jax.ShapeDtypeStruct((N_CHIP, gd, D), F32),
                   jax.ShapeDtypeStruct((4, gd, gd), F32),
                   jax.ShapeDtypeStruct((1, pw), F32), jax.ShapeDtypeStruct((1, pw), F32)],
        scratch_shapes=[pltpu.VMEM((hl, pw), F32)],
        compiler_params=_cp(("arbitrary",)),
    )(dx2, proj, mixed, y, dpool, gate, wg, scale, wo)


def _l0_bwd_mix(dx1, proj, hst, y, xc, cz, rg, ig_, gate, cw, wa, wx, lam, sw, wo):
    s_len = dx1.shape[0]
    ts = _tile(s_len, TS_MIX)
    n_t = s_len // ts
    hl, hb = SUBLANES, BF16_ROWS
    yb_w = 2 * D // N_CHIP
    assert n_t % 2 == 0

    def body(dx_ref, p_ref, h_ref, hh_ref, y_ref, xc_ref, cz_ref, r_ref, ig_ref, gate_ref, cw_ref, wa_ref, wx_ref,
             lam_ref, sw_ref, wo_ref, dp_ref, mt_ref, dwa_ref, dwx_ref, sm_ref, cg, cdxc, cdcz, ca, yhold, dxhold):
        i = pl.program_id(0)
        ri = n_t - 1 - i

        @pl.when(i == 0)
        def _():
            cg[...] = jnp.zeros_like(cg)
            ca[...] = jnp.zeros_like(ca)
            cdxc[...] = jnp.zeros_like(cdxc)
            cdcz[...] = jnp.zeros_like(cdcz)
            sm_ref[...] = jnp.zeros_like(sm_ref)
            mt_ref[...] = jnp.zeros_like(mt_ref)
            dwa_ref[...] = jnp.zeros_like(dwa_ref)
            dwx_ref[...] = jnp.zeros_like(dwx_ref)

        dxb = dx_ref[...].astype(BF16)

        @pl.when(i % 2 == 0)
        def _():
            yhold[...] = y_ref[...]
            dxhold[...] = dxb

        @pl.when(i % 2 == 1)
        def _():
            dx2t = jnp.concatenate([dxhold[...], dxb], axis=0)
            for k in range(N_CHIP):
                cols = slice(k * yb_w, (k + 1) * yb_w)
                y2t = jnp.concatenate([yhold[:, cols], y_ref[:, cols]], axis=0)
                mt_ref[k] += lax.dot_general(y2t, dx2t, TN, preferred_element_type=F32)

        has_prev = (ri > 0).astype(F32)
        xa, ga, gbp, gcp, v, gb = [p_ref[:, k * D:(k + 1) * D].astype(F32) for k in range(6)]
        rows = _rows(ts, D)
        first = (rows == 0) & (ri == 0)
        xc = xc_ref[...].astype(F32)
        cz = cz_ref[...].astype(F32)
        r = r_ref[...].astype(F32)
        ig = ig_ref[...].astype(F32)
        sp = _softplus_neg(lam_ref[...])
        a, m, inv_m = _lru_decay(r, sp, first)
        z = gcp * v
        h = h_ref[...].astype(F32)
        hprev = _down(jnp.concatenate([hh_ref[...].astype(F32)[hb - hl:hb] * has_prev, h], axis=0), 1)[hl:hl + ts]
        dy = lax.dot_general((gate_ref[...] * dx_ref[...]).astype(BF16), wo_ref[...], NT, preferred_element_type=F32)
        dya_pre, dyb_pre = dy[:, 0:D], dy[:, D:2 * D]
        s_a, s_b = _sigmoid(ga), _sigmoid(gb)
        silu_a, silu_b = ga * s_a, gb * s_b
        dp_ref[:, D:2 * D] = (dya_pre * h * (s_a + silu_a * (1.0 - s_a))).astype(BF16)
        dp_ref[:, 5 * D:6 * D] = (dyb_pre * (gbp * cz) * (s_b + silu_b * (1.0 - s_b))).astype(BF16)
        dya = dya_pre * silu_a
        dyb = dyb_pre * silu_b
        dp_ref[:, 2 * D:3 * D] = (dyb * cz).astype(BF16)
        dcz = dyb * gbp
        dcz_ext = jnp.concatenate([dcz, cdcz[...]], axis=0)
        dcz_taps = [_up(dcz_ext, 2 - k)[0:ts] for k in range(3)]
        for k in range(3):
            sm_ref[8 + k:9 + k, :] += jnp.sum(z * dcz_taps[k], axis=0, keepdims=True)
        dz = sum(sw_ref[k:k + 1, :] * dcz_taps[k] for k in range(3))
        dp_ref[:, 3 * D:4 * D] = (dz * v).astype(BF16)
        dp_ref[:, 4 * D:5 * D] = (dz * gcp).astype(BF16)
        cdcz[...] = dcz[0:hl, :]
        alpha = _up(jnp.concatenate([a, ca[...]], axis=0), 1)[0:ts]
        dh = _run(_scan_rev_steps(alpha, dya, cg[0:1, :]))
        cg[...] = dh[0:hl, :]
        ca[...] = a[0:hl, :]
        da = dh * hprev
        dhx = dh * xc
        dm = dhx * ig
        di = dhx * m
        dxc = dh * (m * ig)
        dl = a * (da - jnp.where(first, 0.0, dm * a * inv_m))
        dlr = dl * r
        sm_ref[7:8, :] += jnp.sum(dlr, axis=0, keepdims=True) * (-LRU_C)
        dpa = dlr * (sp * (-LRU_C)) * (1.0 - r)
        dpx = di * ig * (1.0 - ig)
        sm_ref[5:6, :] += jnp.sum(dpa, axis=0, keepdims=True)
        sm_ref[6:7, :] += jnp.sum(dpx, axis=0, keepdims=True)
        dpa_b, dpx_b, xc_b = dpa.astype(BF16), dpx.astype(BF16), xc.astype(BF16)
        back = []
        for hd in range(LRU_HEADS):
            sl = slice(hd * LRU_HEAD_DIM, (hd + 1) * LRU_HEAD_DIM)
            back.append(lax.dot_general(dpa_b[:, sl], wa_ref[hd], NT, preferred_element_type=F32)
                        + lax.dot_general(dpx_b[:, sl], wx_ref[hd], NT, preferred_element_type=F32))
            dwa_ref[hd] += lax.dot_general(xc_b[:, sl], dpa_b[:, sl], TN, preferred_element_type=F32)
            dwx_ref[hd] += lax.dot_general(xc_b[:, sl], dpx_b[:, sl], TN, preferred_element_type=F32)
        dxc = dxc + jnp.concatenate(back, axis=1)
        sm_ref[4:5, :] += jnp.sum(dxc, axis=0, keepdims=True)
        dxc_ext = jnp.concatenate([dxc, cdxc[...]], axis=0)
        dxc_taps = [_up(dxc_ext, 3 - k)[0:ts] for k in range(4)]
        for k in range(4):
            sm_ref[k:k + 1, :] += jnp.sum(xa * dxc_taps[k], axis=0, keepdims=True)
        dp_ref[:, 0:D] = sum(cw_ref[k:k + 1, :] * dxc_taps[k] for k in range(4)).astype(BF16)
        cdxc[...] = dxc[0:hl, :]

    def full(a):
        return pl.BlockSpec(a.shape, lambda i: (0,) * a.ndim)

    rev = lambda w: pl.BlockSpec((ts, w), lambda i: (n_t - 1 - i, 0))
    halo = lambda w: pl.BlockSpec((hb, w), lambda i: (jnp.maximum((n_t - 1 - i) * (ts // hb) - 1, 0), 0))
    return pl.pallas_call(
        body, name="l0_bwd_mix", grid=(n_t,),
        in_specs=[rev(D), rev(6 * D), rev(D), halo(D), rev(2 * D), rev(D), rev(D), rev(D), rev(D)]
        + [full(a) for a in (gate, cw, wa, wx, lam, sw, wo)],
        out_specs=[rev(6 * D), pl.BlockSpec((N_CHIP, yb_w, D), lambda i: (0, 0, 0)),
                   pl.BlockSpec(wa.shape, lambda i: (0, 0, 0)), pl.BlockSpec(wa.shape, lambda i: (0, 0, 0)),
                   pl.BlockSpec((2 * SUBLANES, D), lambda i: (0, 0))],
        out_shape=[jax.ShapeDtypeStruct((s_len, 6 * D), BF16), jax.ShapeDtypeStruct((N_CHIP, yb_w, D), F32),
                   jax.ShapeDtypeStruct(wa.shape, F32), jax.ShapeDtypeStruct(wa.shape, F32),
                   jax.ShapeDtypeStruct((2 * SUBLANES, D), F32)],
        scratch_shapes=[pltpu.VMEM((hl, D), F32)] * 4 + [pltpu.VMEM((ts, 2 * D), BF16), pltpu.VMEM((ts, D), BF16)],
        compiler_params=_cp(("arbitrary",)),
    )(dx1, proj, hst, hst, y, xc, cz, rg, ig_, gate, cw, wa, wx, lam, sw, wo)


def _dgrad_norm(dproj, w, x, dres, g, sc, name, after=None):
    s_len, nb = x.shape[0], w.shape[2]
    ts = _tile(s_len, TS_DGRAD)
    order = [] if after is None else [after]

    def body(dp_ref, w_ref, x_ref, dr_ref, g_ref, sc_ref, *rest):
        dx_ref, s1_ref, s2_ref = rest[len(order):]

        @pl.when(pl.program_id(0) == 0)
        def _():
            s1_ref[...] = jnp.zeros_like(s1_ref)
            s2_ref[...] = jnp.zeros_like(s2_ref)

        dh = sum(lax.dot_general(dp_ref[:, k * nb:(k + 1) * nb], w_ref[k], NT, preferred_element_type=F32)
                 for k in range(N_CHIP))
        xv = x_ref[...]
        r = lax.rsqrt(jnp.mean(xv * xv, axis=-1, keepdims=True) + RMS_EPS)
        n = xv * r
        s1_ref[...] += jnp.sum(dh, axis=0, keepdims=True)
        s2_ref[...] += jnp.sum(dh * n, axis=0, keepdims=True)
        dn = dh * (g_ref[...] * (1.0 + sc_ref[...]))
        dx_ref[...] = dr_ref[...] + r * (dn - n * jnp.mean(dn * n, axis=-1, keepdims=True))

    row = lambda wd: pl.BlockSpec((ts, wd), lambda i: (i, 0))
    vec = pl.BlockSpec((1, D), lambda i: (0, 0))
    return pl.pallas_call(
        body, name=name, grid=(s_len // ts,),
        in_specs=[row(N_CHIP * nb), pl.BlockSpec(w.shape, lambda i: (0, 0, 0)), row(D), row(D), vec, vec]
        + [ANY] * len(order),
        out_specs=[row(D), vec, vec],
        out_shape=[jax.ShapeDtypeStruct((s_len, D), F32)] + [jax.ShapeDtypeStruct((1, D), F32)] * 2,
        compiler_params=_cp(("arbitrary",)),
    )(dproj, w, x, dres, g, sc, *order)


def _wgrad(a, b, groups, ka, nb, a_col, b_col, name, after=None):
    s_len = a.shape[0]
    ts = _tile(s_len, TS_WGRAD * (2 if ka * nb <= D * D else 1))
    n_s = s_len // ts
    order = [] if after is None else [after]

    def body(a_ref, b_ref, *rest):
        o_ref, wire_ref = rest[-2:]

        @pl.when(pl.program_id(1) == 0)
        def _():
            o_ref[...] = jnp.zeros_like(o_ref)

        o_ref[...] += lax.dot_general(a_ref[...].astype(BF16), b_ref[...].astype(BF16), TN, preferred_element_type=F32)

        @pl.when(pl.program_id(1) == n_s - 1)
        def _():
            wire_ref[...] = o_ref[...].astype(GRAD_WIRE_DTYPE)

    blk = pl.BlockSpec((None, ka, nb), lambda g, s: (g, 0, 0))
    return pl.pallas_call(
        body, name=name, grid=(groups, n_s),
        in_specs=[pl.BlockSpec((ts, ka), lambda g, s: (s, a_col(g))), pl.BlockSpec((ts, nb), lambda g, s: (s, b_col(g)))]
        + [ANY] * len(order),
        out_specs=[blk, blk],
        out_shape=[jax.ShapeDtypeStruct((groups, ka, nb), F32), jax.ShapeDtypeStruct((groups, ka, nb), GRAD_WIRE_DTYPE)],
        compiler_params=_cp(("parallel", "arbitrary")),
    )(a, b, *order)


def _wo_final(mt, wo, gate, name):
    rb = mt.shape[1]

    def body(m_ref, w_ref, gate_ref, dw_ref, wire_ref, dg_ref):
        @pl.when(pl.program_id(0) == 0)
        def _():
            dg_ref[...] = jnp.zeros_like(dg_ref)

        mv = m_ref[...]
        dw = mv * gate_ref[...]
        dw_ref[...] = dw
        wire_ref[...] = dw.astype(GRAD_WIRE_DTYPE)
        dg_ref[...] += jnp.sum(mv * w_ref[...].astype(F32), axis=0, keepdims=True)

    blk = pl.BlockSpec((None, rb, D), lambda k: (k, 0, 0))
    vec = pl.BlockSpec((1, D), lambda k: (0, 0))
    return pl.pallas_call(
        body, name=name, grid=(N_CHIP,), in_specs=[blk, blk, vec], out_specs=[blk, blk, vec],
        out_shape=[jax.ShapeDtypeStruct(mt.shape, F32), jax.ShapeDtypeStruct(mt.shape, GRAD_WIRE_DTYPE),
                   jax.ShapeDtypeStruct((1, D), F32)],
        compiler_params=_cp(("arbitrary",)),
    )(mt, wo, gate)


ROW_NORM_G, ROW_CONV_W, ROW_CONV_B, ROW_B_A, ROW_B_X, ROW_LAMBDA, ROW_SC_W, ROW_POOL_B, ROW_POOL_S, ROW_FINAL_G = (
    0, 2, 6, 7, 8, 9, 10, 13, 15, 17)
ROW_LOSS = 18
DMOD_W = 6 * D // SUBLANES


def _small_pack(s1_0, s2_0, s1_1, s2_1, sm0, dsc1, dbg1, dgf, losscols, dgate0, dgate1, norm_g, sc0, sc1, lam):
    def body(s1_0r, s2_0r, s1_1r, s2_1r, sm, dsc, dbg, dgfr, lcols, dg0, dg1, ng, sc0r, sc1r, lamr, buf, dmod):
        buf[...] = jnp.zeros_like(buf)
        buf[0:1, :] = s2_0r[...] * (1.0 + sc0r[...])
        buf[1:2, :] = s2_1r[...] * (1.0 + sc1r[...])
        buf[ROW_CONV_W:ROW_CONV_W + 4, :] = sm[0:4, :]
        buf[ROW_CONV_B:ROW_CONV_B + 1, :] = sm[4:5, :]
        buf[ROW_B_A:ROW_B_A + 1, :] = sm[5:6, :]
        buf[ROW_B_X:ROW_B_X + 1, :] = sm[6:7, :]
        buf[ROW_LAMBDA:ROW_LAMBDA + 1, :] = -sm[7:8, :] * _sigmoid(-lamr[...])
        buf[ROW_SC_W:ROW_SC_W + 3, :] = sm[8:11, :]
        for k in range(2):
            buf[ROW_POOL_B + k:ROW_POOL_B + k + 1, :] = dbg[:, k * D:(k + 1) * D]
            buf[ROW_POOL_S + k:ROW_POOL_S + k + 1, :] = dsc[:, k * D:(k + 1) * D]
        buf[ROW_FINAL_G:ROW_FINAL_G + 1, :] = dgfr[...]
        pieces = (s1_0r[...], s2_0r[...] * ng[0:1, :], dg0[...], s1_1r[...], s2_1r[...] * ng[1:2, :], dg1[...])
        flat = jnp.concatenate(pieces, axis=1)
        for r in range(SUBLANES):
            dmod[r:r + 1, :] = flat[:, r * DMOD_W:(r + 1) * DMOD_W]
        buf[ROW_LOSS:ROW_LOSS + 1, :] = jnp.broadcast_to(jnp.sum(lcols[...], axis=1, keepdims=True) * (0.5 / D), (1, D))

    args = (s1_0, s2_0, s1_1, s2_1, sm0, dsc1, dbg1, dgf, losscols, dgate0, dgate1, norm_g, sc0, sc1, lam)
    return pl.pallas_call(
        body, name="small_pack", in_specs=[VMEM] * len(args), out_specs=[VMEM] * 2,
        out_shape=[jax.ShapeDtypeStruct((SMALL_ROWS, D), F32), jax.ShapeDtypeStruct((SUBLANES, DMOD_W), F32)],
        compiler_params=_cp(),
    )(*args)


def _small_comm(buf_a, buf_b, dmod8):
    ra, rb = buf_a.shape[0] // N_DEV, buf_b.shape[0] // N_DEV
    wb = buf_b.shape[1]

    def body(a_ref, b_ref, dm_ref, oa_ref, ob_ref, odm_ref, ina, inb, dslot, sa, sb, s1, r1, s2, r2):
        x, y, c = _pos()
        me = 4 * x + 2 * y + c
        peers = []
        for r in range(1, N_DEV):
            fx, fy, fc = (r >> 2) & 1, (r >> 1) & 1, r & 1
            px, py, pc = _flip(x, fx), _flip(y, fy), _flip(c, fc)
            peers.append(((px, py, pc), 4 * px + 2 * py + pc))
        seg_a = lambda d: pl.ds(pl.multiple_of(d * ra, SUBLANES), ra)
        seg_b = lambda d: pl.ds(pl.multiple_of(d * rb, SUBLANES), rb)
        first = []
        for r, (peer, pid) in enumerate(peers):
            for k, (src, dst) in enumerate(((a_ref.at[seg_a(pid), :], ina.at[r]), (b_ref.at[seg_b(pid), :], inb.at[r]),
                                            (dm_ref, dslot.at[me]))):
                cp = pltpu.make_async_remote_copy(src_ref=src, dst_ref=dst, send_sem=s1.at[3 * r + k],
                                                  recv_sem=r1.at[3 * r + k], device_id=peer, device_id_type=MESH)
                cp.start()
                first.append(cp)
        dslot[me] = dm_ref[...]
        for cp in first:
            cp.wait()
        acc_a, acc_b = a_ref[seg_a(me), :], b_ref[seg_b(me), :]
        for r in range(N_DEV - 1):
            acc_a = acc_a + ina[r]
            acc_b = acc_b + inb[r]
        sa[...] = acc_a
        sb[...] = acc_b
        oa_ref[seg_a(me), :] = acc_a
        ob_ref[seg_b(me), :] = acc_b
        second = []
        for r, (peer, pid) in enumerate(peers):
            for k, (src, dst) in enumerate(((sa, oa_ref.at[seg_a(me), :]), (sb, ob_ref.at[seg_b(me), :]))):
                cp = pltpu.make_async_remote_copy(src_ref=src, dst_ref=dst, send_sem=s2.at[2 * r + k],
                                                  recv_sem=r2.at[2 * r + k], device_id=peer, device_id_type=MESH)
                cp.start()
                second.append(cp)
        odm_ref[...] = dslot[...]
        for cp in second:
            cp.wait()

    nrel = N_DEV - 1
    return pl.pallas_call(
        body, name="small_comm", in_specs=[VMEM] * 3, out_specs=[VMEM] * 3,
        out_shape=[jax.ShapeDtypeStruct(buf_a.shape, F32), jax.ShapeDtypeStruct(buf_b.shape, F32),
                   jax.ShapeDtypeStruct((N_DEV,) + dmod8.shape, F32)],
        scratch_shapes=[pltpu.VMEM((nrel, ra, D), F32), pltpu.VMEM((nrel, rb, wb), F32),
                        pltpu.VMEM((N_DEV,) + dmod8.shape, F32), pltpu.VMEM((ra, D), F32), pltpu.VMEM((rb, wb), F32),
                        pltpu.SemaphoreType.DMA((3 * nrel,)), pltpu.SemaphoreType.DMA((3 * nrel,)),
                        pltpu.SemaphoreType.DMA((2 * nrel,)), pltpu.SemaphoreType.DMA((2 * nrel,))],
        compiler_params=_cp(),
    )(buf_a, buf_b, dmod8)


def _adam(w, g, m, v):
    m2 = ADAM_B1 * m + (1.0 - ADAM_B1) * g
    v2 = ADAM_B2 * v + (1.0 - ADAM_B2) * (g * g)
    m_hat = m2 / (1.0 - ADAM_B1 ** ADAM_STEP)
    v_hat = v2 / (1.0 - ADAM_B2 ** ADAM_STEP)
    return -ADAM_LR * (m_hat / (jnp.sqrt(v_hat) + ADAM_EPS) + ADAM_WD * w), m2, v2


def _small_adam(red_a, red_b, dm_all, params):
    n = len(params)

    def body(*refs):
        ra, rb, dm = refs[:3]
        wmv = refs[3:3 + 3 * n]
        outs = refs[3 + 3 * n:]
        x, y, _ = _pos()
        chip = 2 * x + y

        def shard(row0, nrows, width):
            per_row = D // width
            cands = []
            for k in range(N_CHIP):
                if nrows == 1 or per_row >= N_CHIP:
                    cands.append(ra[row0:row0 + nrows, k * width:(k + 1) * width])
                else:
                    rr, cc = divmod(k * width, D)
                    cands.append(ra[row0 + rr:row0 + rr + 1, cc:cc + width])
            g = cands[0]
            for k in range(1, N_CHIP):
                g = jnp.where(chip == k, cands[k], g)
            return g

        dms = jnp.sum(dm[...], axis=0)
        hw = LRU_HEADS * LRU_HEAD_DIM
        grads = [
            ra[ROW_NORM_G:ROW_NORM_G + 2, :],
            None,
            shard(ROW_CONV_W, 4, D // N_CHIP),
            ra[ROW_CONV_B:ROW_CONV_B + 1, :],
            rb[0:hw, :],
            ra[ROW_B_A:ROW_B_A + 1, :],
            rb[hw:2 * hw, :],
            ra[ROW_B_X:ROW_B_X + 1, :],
            ra[ROW_LAMBDA:ROW_LAMBDA + 1, :],
            shard(ROW_SC_W, 3, D // N_CHIP),
            shard(ROW_POOL_B, 2, 2 * D // N_CHIP),
            shard(ROW_POOL_S, 2, 2 * D // N_CHIP),
            ra[ROW_FINAL_G:ROW_FINAL_G + 1, :],
        ]
        for p in range(n):
            w_ref, m_ref, v_ref = wmv[3 * p:3 * p + 3]
            g_out, d_out, m_out, v_out = outs[4 * p:4 * p + 4]
            if grads[p] is None:
                for r in range(SUBLANES):
                    l, cols = r // N_CHIP, slice((r % N_CHIP) * DMOD_W, (r % N_CHIP + 1) * DMOD_W)
                    g = dms[r:r + 1, :]
                    dl, m2, v2 = _adam(w_ref[l:l + 1, cols], g, m_ref[l:l + 1, cols], v_ref[l:l + 1, cols])
                    g_out[l:l + 1, cols] = g
                    d_out[l:l + 1, cols] = dl
                    m_out[l:l + 1, cols] = m2
                    v_out[l:l + 1, cols] = v2
            else:
                g = grads[p]
                dl, m2, v2 = _adam(w_ref[...], g, m_ref[...], v_ref[...])
                g_out[...] = g
                d_out[...] = dl
                m_out[...] = m2
                v_out[...] = v2

    flat = [a for p in params for a in p]
    return pl.pallas_call(
        body, name="small_adam", in_specs=[VMEM] * (3 + len(flat)), out_specs=[VMEM] * (4 * n),
        out_shape=[jax.ShapeDtypeStruct(p[0].shape, F32) for p in params for _ in range(4)],
        compiler_params=_cp(),
    )(red_a, red_b, dm_all, *flat)


def _modw_adam(ca_t, dm_sh, w, m, v):
    nw = w.shape[2]

    def body(c_ref, d_ref, w_ref, m_ref, v_ref, g_out, d_out, m_out, v_out):
        g = jnp.dot(c_ref[...], d_ref[...], precision=lax.Precision.HIGHEST, preferred_element_type=F32)
        dl, m2, v2 = _adam(w_ref[...], g, m_ref[...], v_ref[...])
        g_out[...] = g
        d_out[...] = dl
        m_out[...] = m2
        v_out[...] = v2

    blk = pl.BlockSpec((None, D, nw), lambda l: (l, 0, 0))
    return pl.pallas_call(
        body, name="modw_adam", grid=(2,),
        in_specs=[pl.BlockSpec((D, SUBLANES), lambda l: (0, 0)), pl.BlockSpec((None, SUBLANES, nw), lambda l: (l, 0, 0)),
                  blk, blk, blk],
        out_specs=[blk] * 4, out_shape=[jax.ShapeDtypeStruct(w.shape, F32)] * 4,
        compiler_params=_cp(("arbitrary",)),
    )(ca_t, dm_sh, w, m, v)


def _exchange(copies, name, out_type, n_sems, args, sequencer, after=None):
    order = [] if after is None else [after]
    n_in, n_out = len(args) + len(order), len(out_type)

    def body(*refs):
        barrier = pltpu.get_barrier_semaphore()
        peers = sequencer[1](*_pos())
        for peer in peers:
            pl.semaphore_signal(barrier, inc=1, device_id=peer, device_id_type=MESH)
        pl.semaphore_wait(barrier, len(peers))
        copies(refs[:n_in], refs[n_in:n_in + n_out], refs[n_in + n_out], refs[n_in + n_out + 1])

    sems = [pltpu.SemaphoreType.DMA((n_sems,))] * 2
    return pl.kernel(body, out_type, mesh=plsc.ScalarSubcoreMesh(axis_name="sequencer", num_cores=1), name=name,
                     scratch_types=sems, compiler_params=pltpu.CompilerParams(collective_id=sequencer[0]))(*args, *order)


def _sibling(x, y, c):
    return [(x, y, 1 - c)]


def _other_chips(x, y, c):
    return [(1 - x, y, c), (x, 1 - y, c), (1 - x, 1 - y, c)]


def _to_wire(g, name, after=None):
    _, rr, cc = g.shape
    rb = min(rr, 256)

    def body(g_ref, *rest):
        rest[-1][...] = g_ref[...].astype(GRAD_WIRE_DTYPE)

    order = [] if after is None else [after]
    blk = pl.BlockSpec((None, rb, cc), lambda k, j: (k, j, 0))
    return pl.pallas_call(
        body, name=name, grid=(N_CHIP, rr // rb), in_specs=[blk] + [ANY] * len(order), out_specs=blk,
        out_shape=jax.ShapeDtypeStruct(g.shape, GRAD_WIRE_DTYPE), compiler_params=_cp(("parallel", "parallel")),
    )(g, *order)


def _chip_scatter(ps, name, collective_id, after=None):
    n = len(ps)

    def copies(ins, outs, ssem, rsem):
        x, y, c = _pos()
        cps = []
        for a in range(n):
            for q, (fx, fy) in enumerate(((1, 0), (0, 1), (1, 1))):
                px, py = _flip(x, fx), _flip(y, fy)
                cp = pltpu.make_async_remote_copy(
                    src_ref=ins[a].at[2 * px + py], dst_ref=outs[a].at[q],
                    send_sem=ssem.at[3 * a + q], recv_sem=rsem.at[3 * a + q], device_id=(px, py, c), device_id_type=MESH)
                cp.start()
                cps.append(cp)
        for cp in cps:
            cp.wait()

    out_type = [jax.ShapeDtypeStruct((N_CHIP - 1,) + p.shape[1:], p.dtype) for p in ps]
    return _exchange(copies, name, out_type, 3 * n, ps, (collective_id, _other_chips), after)


def _add_owner(p, got, chipidx, name, after=None):
    _, hr, cc = p.shape
    rb = min(hr, 256)

    def body(k_ref, p_ref, r_ref, *rest):
        rest[-1][...] = ((p_ref[...].astype(F32) + r_ref[0].astype(F32)) + r_ref[1].astype(F32)) + r_ref[2].astype(F32)

    order = [] if after is None else [after]
    return pl.pallas_call(
        body, name=name,
        grid_spec=pltpu.PrefetchScalarGridSpec(
            num_scalar_prefetch=1, grid=(hr // rb,),
            in_specs=[pl.BlockSpec((None, rb, cc), lambda j, k_ref: (k_ref[0], j, 0)),
                      pl.BlockSpec((N_CHIP - 1, rb, cc), lambda j, k_ref: (0, j, 0))] + [ANY] * len(order),
            out_specs=pl.BlockSpec((rb, cc), lambda j, k_ref: (j, 0))),
        out_shape=jax.ShapeDtypeStruct((hr, cc), F32),
        compiler_params=_cp(("parallel",)),
    )(chipidx, p, got, *order)


def _sib_exchange(ts_, name, collective_id, after=None):
    n = len(ts_)

    def copies(ins, outs, ssem, rsem):
        x, y, c = _pos()
        cps = []
        for a in range(n):
            cp = pltpu.make_async_remote_copy(src_ref=ins[a], dst_ref=outs[a], send_sem=ssem.at[a],
                                              recv_sem=rsem.at[a], device_id=(x, y, 1 - c), device_id_type=MESH)
            cp.start()
            cps.append(cp)
        for cp in cps:
            cp.wait()

    out_type = [jax.ShapeDtypeStruct(t.shape, F32) for t in ts_]
    return _exchange(copies, name, out_type, n, ts_, (collective_id, _sibling), after)


def _adam_2d(w, g_own, g_sib, m, v, name):
    rr, cc = w.shape
    rb = min(rr, 256)

    def body(w_ref, go_ref, gs_ref, m_ref, v_ref, g_out, d_out, m_out, v_out):
        g = go_ref[...] + gs_ref[...]
        dl, m2, v2 = _adam(w_ref[...], g, m_ref[...], v_ref[...])
        g_out[...] = g
        d_out[...] = dl
        m_out[...] = m2
        v_out[...] = v2

    blk = pl.BlockSpec((rb, cc), lambda j: (j, 0))
    return pl.pallas_call(
        body, name=name, grid=(rr // rb,), in_specs=[blk] * 5, out_specs=[blk] * 4,
        out_shape=[jax.ShapeDtypeStruct((rr, cc), F32)] * 4, compiler_params=_cp(("parallel",)),
    )(w, g_own, g_sib, m, v)


def kernel(x, c, norm_g, mod_w, mod_b, hy_w_in, hy_conv_w, hy_conv_b, lru_w_a, lru_b_a, lru_w_x, lru_b_x, lru_lambda, sc_conv_w, hy_w_out, pool_w_in, pool_w_grp, pool_b_grp, pool_scale, pool_w_out, final_g, loss_target, m_norm_g, m_mod_w, m_mod_b, m_hy_w_in, m_hy_conv_w, m_hy_conv_b, m_lru_w_a, m_lru_b_a, m_lru_w_x, m_lru_b_x, m_lru_lambda, m_sc_conv_w, m_hy_w_out, m_pool_w_in, m_pool_w_grp, m_pool_b_grp, m_pool_scale, m_pool_w_out, m_final_g, v_norm_g, v_mod_w, v_mod_b, v_hy_w_in, v_hy_conv_w, v_hy_conv_b, v_lru_w_a, v_lru_b_a, v_lru_w_x, v_lru_b_x, v_lru_lambda, v_sc_conv_w, v_hy_w_out, v_pool_w_in, v_pool_w_grp, v_pool_b_grp, v_pool_scale, v_pool_w_out, v_final_g):
    ax, ay, ac = _pos()
    me = 4 * ax + 2 * ay + ac
    chip = 2 * ax + ay
    xs = x[0]
    tgt = loss_target[0]
    gd = POOL_GROUP_DIM
    kidx = chip.reshape(1).astype(jnp.int32)

    big = [hy_w_in[0], hy_w_out[0], pool_w_in[0], pool_w_grp[0].reshape(4 * 128, gd), pool_w_out[0]]
    w_in0, w_out0 = _wgather_sequencer(
        [_wcast_own_block(w, kidx, f"wcast_own_block_{a}") for a, w in enumerate(big[:2])], "wgather_l0", CIDS_WGATHER[0])

    ca_all, mod_all, small_w = _mod_fwd(jnp.broadcast_to(c, (SUBLANES, D)), mod_w, mod_b,
                                        hy_conv_w[0], sc_conv_w[0], pool_b_grp, pool_scale)
    mod_me = lax.dynamic_index_in_dim(mod_all, me, axis=1, keepdims=False)
    sh0, sc0, gt0 = (mod_me[0:1, k * D:(k + 1) * D] for k in range(3))
    sh1, sc1, gt1 = (mod_me[1:2, k * D:(k + 1) * D] for k in range(3))
    cw = small_w[SW_CONV:SW_CONV + 4, 0:D]
    sw = small_w[SW_SC:SW_SC + 3, 0:D]
    pool_b = small_w[SW_POOL_B:SW_POOL_B + 1, :]
    pool_s = small_w[SW_POOL_S:SW_POOL_S + 1, :]
    g0, g1, gf = norm_g[0:1], norm_g[1:2], final_g.reshape(1, D)
    cb, ba, bx, lam = hy_conv_b, lru_b_a, lru_b_x, lru_lambda

    h0 = _norm_mod(xs, g0, sc0, sh0, "l0_norm")
    wa_b, wx_b = _wcast([lru_w_a[0], lru_w_x[0]])
    w_in1, w_grp, w_out1 = _wgather_sequencer(
        [_wcast_own_block(w, kidx, f"wcast_own_block_{a + 2}", after=(w_out0, h0)) for a, w in enumerate(big[2:])],
        "wgather_l1", CIDS_WGATHER[1])
    w_grp =w_grp.reshape(N_CHIP, 4, 128, gd).transpose(1, 0, 2, 3).reshape(4, gd, gd)

    x1, hst, y0, xc0, cz0, rg0, ig0, proj0 = _l0_fwd(h0, xs, w_in0, gt0, cw, cb, wa_b, ba, wx_b, bx, lam, sw,
                                                     w_out0.reshape(2 * D, D))
    dpool, mixed, y1, dx2, losscols, dgf, h1, proj1 = _l1_fwd(x1, g1, sc1, sh1, w_in1, tgt, gt1, w_grp, pool_b, pool_s,
                                                              w_out1.reshape(2 * D, D), gf)

    def add_owners(grads, got, tag, ids, after):
        own = []
        for a, (g, r) in enumerate(zip(grads, got)):
            own.append(_add_owner(g, r, kidx, f"grad_add_owner_{tag}{a}", own[-1] if own else after))
        return own, _sib_exchange(own, f"grad_sib_exchange_{tag}", ids[1])

    dproj1, mt1, d_wgrp, dsc1, dbg1 = _l1_bwd_mix(dx2, proj1, mixed, y1, dpool, gt1, w_grp, pool_s,
                                                  w_out1.reshape(2 * D, D))
    d_win1, wire_win1 = _wgrad(h1, dproj1, N_CHIP, D, D, lambda g: 0, lambda g: g, "l1_wgrad_in")
    d_wout1, wire_wout1, dgate1 = _wo_final(mt1, w_out1, gt1, "l1_wo_final")
    d_wgrp = d_wgrp.reshape(4, N_CHIP, 128, gd).transpose(1, 0, 2, 3).reshape(N_CHIP, 4 * 128, gd)
    grads_l1 = [d_win1, d_wgrp, d_wout1]
    got_l1 = _chip_scatter([wire_win1, _to_wire(d_wgrp, "grad_to_wire_grp"), wire_wout1], "grad_chip_scatter_l1",
                           CIDS_L1[0])
    dx1, s1_1, s2_1 = _dgrad_norm(dproj1, w_in1, x1, dx2, g1, sc1, "l1_bwd_proj")

    dproj0, mt0, d_wa, d_wx, sm0 = _l0_bwd_mix(dx1, proj0, hst, y0, xc0, cz0, rg0, ig0, gt0, cw, wa_b, wx_b, lam, sw,
                                               w_out0.reshape(2 * D, D))
    sums_l1, sib_l1 = add_owners(grads_l1, got_l1, "l1", CIDS_L1, after=sm0)
    d_win0, wire_win0 = _wgrad(h0, dproj0, N_CHIP, D, 6 * D // N_CHIP, lambda g: 0, lambda g: g, "l0_wgrad_in",
                               after=sums_l1[-1])
    d_wout0, wire_wout0, dgate0 = _wo_final(mt0, w_out0, gt0, "l0_wo_final")
    grads_l0 = [d_win0, d_wout0]
    got_l0 = _chip_scatter([wire_win0, wire_wout0], "grad_chip_scatter_l0", CIDS_L0[0], after=sib_l1[0])
    grad_x, s1_0, s2_0 = _dgrad_norm(dproj0, w_in0, xs, dx1, g0, sc0, "l0_bwd_proj", after=wire_win0)
    sums_l0, sib_l0 = add_owners(grads_l0, got_l0, "l0", CIDS_L0, after=s1_0)

    buf_a, dmod8 = _small_pack(s1_0, s2_0, s1_1, s2_1, sm0, dsc1, dbg1, dgf, losscols, dgate0, dgate1,
                                      norm_g, sc0, sc1, lam)
    hw = LRU_HEADS * LRU_HEAD_DIM
    buf_b = jnp.concatenate([d_wa.reshape(hw, LRU_HEAD_DIM), d_wx.reshape(hw, LRU_HEAD_DIM)], axis=0)
    red_a, red_b, dm_all = _small_comm(buf_a, buf_b, dmod8)
    small = [(norm_g, m_norm_g, v_norm_g), (mod_b, m_mod_b, v_mod_b),
             (hy_conv_w[0], m_hy_conv_w[0], v_hy_conv_w[0]), (hy_conv_b, m_hy_conv_b, v_hy_conv_b),
             tuple(a.reshape(hw, LRU_HEAD_DIM) for a in (lru_w_a, m_lru_w_a, v_lru_w_a)),
             (lru_b_a, m_lru_b_a, v_lru_b_a),
             tuple(a.reshape(hw, LRU_HEAD_DIM) for a in (lru_w_x, m_lru_w_x, v_lru_w_x)),
             (lru_b_x, m_lru_b_x, v_lru_b_x), (lru_lambda, m_lru_lambda, v_lru_lambda),
             (sc_conv_w[0], m_sc_conv_w[0], v_sc_conv_w[0]), (pool_b_grp, m_pool_b_grp, v_pool_b_grp),
             (pool_scale, m_pool_scale, v_pool_scale),
             tuple(a.reshape(1, D) for a in (final_g, m_final_g, v_final_g))]
    small_names = ["norm_g", "mod_b", "hy_conv_w", "hy_conv_b", "lru_w_a", "lru_b_a", "lru_w_x", "lru_b_x",
                   "lru_lambda", "sc_conv_w", "pool_b_grp", "pool_scale", "final_g"]
    small_out = _small_adam(red_a, red_b, dm_all, small)
    res = {}
    shapes = dict(norm_g=norm_g, mod_b=mod_b, hy_conv_w=hy_conv_w, hy_conv_b=hy_conv_b, lru_w_a=lru_w_a, lru_b_a=lru_b_a,
                  lru_w_x=lru_w_x, lru_b_x=lru_b_x, lru_lambda=lru_lambda, sc_conv_w=sc_conv_w, pool_b_grp=pool_b_grp,
                  pool_scale=pool_scale, final_g=final_g)
    for p, nm in enumerate(small_names):
        res[nm] = tuple(o.reshape(shapes[nm].shape) for o in small_out[4 * p:4 * p + 4])

    nw = mod_w.shape[2]
    assert nw == DMOD_W
    dm_sh = jnp.stack([lax.dynamic_index_in_dim(dm_all, N_CHIP * l + chip, axis=1, keepdims=False) for l in range(2)])
    res["mod_w"] = tuple(_modw_adam(ca_all.T, dm_sh, mod_w, m_mod_w, v_mod_w))

    sums = list(sums_l0) + list(sums_l1)
    sib_sums = list(sib_l0) + list(sib_l1)
    big_names = ["hy_w_in", "hy_w_out", "pool_w_in", "pool_w_grp", "pool_w_out"]
    big_wmv = [(hy_w_in, m_hy_w_in, v_hy_w_in), (hy_w_out, m_hy_w_out, v_hy_w_out), (pool_w_in, m_pool_w_in, v_pool_w_in),
               (pool_w_grp, m_pool_w_grp, v_pool_w_grp), (pool_w_out, m_pool_w_out, v_pool_w_out)]
    for a, nm in enumerate(big_names):
        rr, cc = big[a].shape
        w, m, v = (t.reshape(rr, cc) for t in big_wmv[a])
        outs = _adam_2d(w, sums[a], sib_sums[a], m, v, f"adam_{nm}")
        res[nm] = tuple(o.reshape(big_wmv[a][0].shape) for o in outs)

    loss = red_a[ROW_LOSS, 0]
    order = ["norm_g", "mod_w", "mod_b", "hy_w_in", "hy_conv_w", "hy_conv_b", "lru_w_a", "lru_b_a", "lru_w_x", "lru_b_x",
             "lru_lambda", "sc_conv_w", "hy_w_out", "pool_w_in", "pool_w_grp", "pool_b_grp", "pool_scale", "pool_w_out",
             "final_g"]
    return (loss, grad_x[None], *[res[nm][0] for nm in order], *[res[nm][1] for nm in order],
            *[res[nm][2] for nm in order], *[res[nm][3] for nm in order])
```

```python
import jax
import jax.numpy as jnp
from jax import lax
from jax.experimental import pallas as pl
from jax.experimental.pallas import tpu as pltpu
from jax.experimental.pallas import tpu_sc as plsc

F32, BF16 = jnp.float32, jnp.bfloat16
D = 1024
RMS_EPS = 1e-6
SQRT_FLOOR = 1e-30
LRU_C = 8.0
LRU_HEADS, LRU_HEAD_DIM = 8, 128
POOL_WINDOWS = (2, 4, 8, 16)
POOL_GROUP_DIM = 512
ADAM_LR, ADAM_B1, ADAM_B2, ADAM_EPS, ADAM_WD, ADAM_STEP = 0.001, 0.9, 0.999, 1e-08, 0.01, 10
MESH = pl.DeviceIdType.MESH
CIDS_WGATHER = (1, 8)
CIDS_L1 = (2, 3)
CIDS_L0 = (4, 5)
N_DEV, N_CHIP = 8, 4
SUBLANES = 8
BF16_ROWS = 16
POOL_HALO = 16
TS_MIX, TS_WGRAD, TS_DGRAD = 256, 2048, 512
SMALL_ROWS = 64
GRAD_WIRE_DTYPE = BF16
ANY = pl.BlockSpec(memory_space=pl.ANY)
VMEM = pl.BlockSpec(memory_space=pltpu.VMEM)
NT = (((1,), (1,)), ((), ()))
TN = (((0,), (0,)), ((), ()))


def _cp(sem=None, vmem_mb=56):
    kw = dict(vmem_limit_bytes=vmem_mb * 2 ** 20)
    if sem is not None:
        kw["dimension_semantics"] = sem
    return pltpu.CompilerParams(**kw)


def _tile(n, t):
    return min(n, t)


def _pos():
    return lax.axis_index("x"), lax.axis_index("y"), lax.axis_index("c")


def _flip(v, f):
    return 1 - v if f else v


def _sigmoid(z):
    return 0.5 * jnp.tanh(0.5 * z) + 0.5


def _rows(n, c):
    return lax.broadcasted_iota(jnp.int32, (n, c), 0)


def _down(a, d):
    return a if d == 0 else pltpu.roll(a, d, 0)


def _up(a, d):
    return a if d == 0 else pltpu.roll(a, a.shape[0] - d, 0)


def _scan_fwd_steps(a, u, carry):
    n, c = a.shape
    sub = _rows(SUBLANES, c)
    out = []
    for k in range(n // SUBLANES):
        p = a[k * SUBLANES:(k + 1) * SUBLANES]
        g = u[k * SUBLANES:(k + 1) * SUBLANES]
        for d in (1, 2, 4):
            keep = sub >= d
            g = g + p * jnp.where(keep, pltpu.roll(g, d, 0), 0.0)
            p = p * jnp.where(keep, pltpu.roll(p, d, 0), 1.0)
        h = g + p * carry
        carry = h[SUBLANES - 1:SUBLANES, :]
        out.append(h)
        yield
    return jnp.concatenate(out, axis=0)


def _scan_rev_steps(alpha, b, carry):
    n, c = alpha.shape
    sub = _rows(SUBLANES, c)
    out = []
    for k in reversed(range(n // SUBLANES)):
        p = alpha[k * SUBLANES:(k + 1) * SUBLANES]
        g = b[k * SUBLANES:(k + 1) * SUBLANES]
        for d in (1, 2, 4):
            keep = sub < SUBLANES - d
            g = g + p * jnp.where(keep, pltpu.roll(g, SUBLANES - d, 0), 0.0)
            p = p * jnp.where(keep, pltpu.roll(p, SUBLANES - d, 0), 1.0)
        h = g + p * carry
        carry = h[0:1, :]
        out.append(h)
        yield
    return jnp.concatenate(out[::-1], axis=0)


def _run(steps):
    while True:
        try:
            next(steps)
        except StopIteration as done:
            return done.value


def _paired(progress, pieces):
    n, done = len(pieces), 1
    pieces[0]()
    for frac in progress:
        while done < n and done <= frac * n:
            pieces[done]()
            done += 1
    while done < n:
        pieces[done]()
        done += 1


def _conv_taps(ext, halo, n, width):
    return [_down(ext, width - 1 - k)[halo:halo + n] for k in range(width)]


def _lru_gates(xc, wa_ref, ba, wx_ref, bx):
    xb = xc.astype(BF16)
    pa, px = [], []
    for h in range(LRU_HEADS):
        xh = xb[:, h * LRU_HEAD_DIM:(h + 1) * LRU_HEAD_DIM]
        pa.append(jnp.dot(xh, wa_ref[h], preferred_element_type=F32))
        px.append(jnp.dot(xh, wx_ref[h], preferred_element_type=F32))
    r = _sigmoid(jnp.concatenate(pa, axis=1) + ba)
    ig = _sigmoid(jnp.concatenate(px, axis=1) + bx)
    return r, ig


def _softplus_neg(lam):
    return jnp.maximum(-lam, 0.0) + jnp.log1p(jnp.exp(-jnp.abs(lam)))


def _recip_1_to_2(d):
    r0 = pl.reciprocal(d, approx=True)
    return r0 * (2.0 - d * r0)


def _lru_decay(r, sp, first):
    big_l = (-LRU_C) * r * sp
    a = jnp.exp(big_l)
    th = jnp.tanh(big_l)
    q = (-2.0 * th) * _recip_1_to_2(1.0 - th)
    rs = lax.rsqrt(jnp.maximum(q, SQRT_FLOOR))
    return a, jnp.where(first, 1.0, q * rs), rs


def _pool_inv_counts(t0, n):
    t = (t0 + lax.broadcasted_iota(jnp.int32, (n, 1), 0) + 1).astype(F32)
    return [1.0 / jnp.minimum(t, float(w)) for w in POOL_WINDOWS]


def _window_sums(ext, shift):
    gd = POOL_GROUP_DIM
    out = []
    s = ext
    for k in range(len(POOL_WINDOWS)):
        s = s + shift(s, 2 ** k)
        out.append(s[:, 0:gd])
        if k + 1 < len(POOL_WINDOWS):
            s = s[:, gd:]
    return out


SW_ROWS, SW_COLS = 16, 2 * D
SW_CONV, SW_SC, SW_POOL_B, SW_POOL_S = 0, 4, 8, 9


def _mod_fwd(c8, mod_w, mod_b, conv_w, sc_w, pool_b, pool_s):
    nw = mod_w.shape[2]
    cq, pq = conv_w.shape[1], pool_b.shape[1]

    def body(c_ref, w_ref, b_ref, cw_ref, sw_ref, pb_ref, ps_ref, ca_ref, mod_ref, small_ref,
             cslot, mslot, msend, pslot, psend, s1, r1, s2, r2, s3, r3):
        x, y, c = _pos()
        me = 4 * x + 2 * y + c
        chip = 2 * x + y
        first = []
        for r in range(1, N_DEV):
            fx, fy, fc = (r >> 2) & 1, (r >> 1) & 1, r & 1
            cp = pltpu.make_async_remote_copy(
                src_ref=c_ref, dst_ref=cslot.at[me], send_sem=s1.at[r - 1], recv_sem=r1.at[r - 1],
                device_id=(_flip(x, fx), _flip(y, fy), _flip(c, fc)), device_id_type=MESH)
            cp.start()
            first.append(cp)
        cslot[me] = c_ref[...]
        for cp in first:
            cp.wait()
        rows = _rows(SUBLANES, D)
        call = jnp.zeros((SUBLANES, D), F32)
        for d in range(N_DEV):
            call = jnp.where(rows == d, cslot[d], call)
        ca = call * _sigmoid(call)
        ca_ref[...] = ca
        for l in range(2):
            msend[l] = jnp.dot(ca, w_ref[l], precision=lax.Precision.HIGHEST, preferred_element_type=F32)
        psend[...] = jnp.zeros_like(psend)
        psend[SW_CONV:SW_CONV + 4, 0:cq] = cw_ref[...]
        psend[SW_SC:SW_SC + 3, 0:cq] = sw_ref[...]
        psend[SW_POOL_B:SW_POOL_B + 1, :] = pb_ref[...]
        psend[SW_POOL_S:SW_POOL_S + 1, :] = ps_ref[...]
        second = []
        for q, (fx, fy) in enumerate(((1, 0), (0, 1), (1, 1))):
            peer = (_flip(x, fx), _flip(y, fy), c)
            for src, dst, ss, rs in ((msend, mslot, s2, r2), (psend, pslot, s3, r3)):
                cp = pltpu.make_async_remote_copy(src_ref=src, dst_ref=dst.at[chip], send_sem=ss.at[q], recv_sem=rs.at[q],
                                                  device_id=peer, device_id_type=MESH)
                cp.start()
                second.append(cp)
        mslot[chip] = msend[...]
        pslot[chip] = psend[...]
        for cp in second:
            cp.wait()
        small_ref[...] = jnp.zeros_like(small_ref)
        for j in range(N_CHIP):
            for l in range(2):
                mod_ref[l, :, j * nw:(j + 1) * nw] = mslot[j, l] + b_ref[l:l + 1, j * nw:(j + 1) * nw]
            small_ref[0:SUBLANES, j * cq:(j + 1) * cq] = pslot[j, 0:SUBLANES, 0:cq]
            small_ref[SUBLANES:SW_ROWS, j * pq:(j + 1) * pq] = pslot[j, SUBLANES:SW_ROWS, :]

    args = (c8, mod_w, mod_b, conv_w, sc_w, pool_b, pool_s)
    dma3 = pltpu.SemaphoreType.DMA((N_CHIP - 1,))
    return pl.pallas_call(
        body, name="mod_fwd",
        in_specs=[VMEM] * len(args), out_specs=[VMEM] * 3,
        out_shape=[jax.ShapeDtypeStruct((SUBLANES, D), F32), jax.ShapeDtypeStruct((2, SUBLANES, N_CHIP * nw), F32),
                   jax.ShapeDtypeStruct((SW_ROWS, SW_COLS), F32)],
        scratch_shapes=[pltpu.VMEM((N_DEV, SUBLANES, D), F32), pltpu.VMEM((N_CHIP, 2, SUBLANES, nw), F32),
                        pltpu.VMEM((2, SUBLANES, nw), F32), pltpu.VMEM((N_CHIP, SW_ROWS, pq), F32),
                        pltpu.VMEM((SW_ROWS, pq), F32),
                        pltpu.SemaphoreType.DMA((N_DEV - 1,)), pltpu.SemaphoreType.DMA((N_DEV - 1,)),
                        dma3, dma3, dma3, dma3],
        compiler_params=_cp(),
    )(*args)


def _wcast(ws):
    def body(*refs):
        n = len(refs) // 2
        for a in range(n):
            refs[n + a][...] = refs[a][...].astype(BF16)

    return pl.pallas_call(
        body, name="wcast", in_specs=[VMEM] * len(ws), out_specs=[VMEM] * len(ws),
        out_shape=[jax.ShapeDtypeStruct(w.shape, BF16) for w in ws], compiler_params=_cp(),
    )(*ws)


def _wcast_own_block(w, kidx, name, after=()):
    rr, cc = w.shape
    rb = min(rr, 256)

    def body(k_ref, w_ref, *rest):
        rest[-1][...] = w_ref[...].astype(BF16)

    order = list(after)
    return pl.pallas_call(
        body, name=name,
        grid_spec=pltpu.PrefetchScalarGridSpec(
            num_scalar_prefetch=1, grid=(rr // rb,),
            in_specs=[pl.BlockSpec((rb, cc), lambda j, k_ref: (j, 0))] + [ANY] * len(order),
            out_specs=pl.BlockSpec((None, rb, cc), lambda j, k_ref: (k_ref[0], j, 0))),
        out_shape=jax.ShapeDtypeStruct((N_CHIP, rr, cc), BF16),
        compiler_params=_cp(("parallel",)),
    )(kidx, w, *order)


def _wgather_copies(outs, rows, ssem, rsem, fssem, frsem):
    n = len(outs)
    x, y, c = _pos()
    chip = 2 * x + y
    sib = (x, y, 1 - c)
    flips = ((1, 0), (0, 1), (1, 1))

    def half(a, which):
        hr = rows[a] // 2
        return pl.ds(pl.multiple_of(which * hr, BF16_ROWS), hr)

    sends = []
    for a in range(n):
        mine = outs[a].at[chip, half(a, c), :]
        for q, (fx, fy) in enumerate(flips):
            cp = pltpu.make_async_remote_copy(
                src_ref=mine, dst_ref=mine, send_sem=ssem.at[3 * a + q], recv_sem=rsem.at[3 * a + q],
                device_id=(_flip(x, fx), _flip(y, fy), c), device_id_type=MESH)
            cp.start()
            sends.append(cp)
    passed = []
    for a in range(n):
        for q, (fx, fy) in enumerate(flips):
            src_chip = 2 * _flip(x, fx) + _flip(y, fy)
            landed = outs[a].at[src_chip, half(a, c), :]
            pltpu.make_async_remote_copy(
                src_ref=landed, dst_ref=landed, send_sem=ssem.at[3 * a + q], recv_sem=rsem.at[3 * a + q],
                device_id=sib, device_id_type=MESH).wait_recv()
            cp = pltpu.make_async_remote_copy(
                src_ref=landed, dst_ref=landed, send_sem=fssem.at[3 * a + q], recv_sem=frsem.at[3 * a + q],
                device_id=sib, device_id_type=MESH)
            cp.start()
            passed.append(cp)
    for a in range(n):
        for q, (fx, fy) in enumerate(flips):
            src_chip = 2 * _flip(x, fx) + _flip(y, fy)
            other = outs[a].at[src_chip, half(a, 1 - c), :]
            pltpu.make_async_remote_copy(
                src_ref=other, dst_ref=other, send_sem=fssem.at[3 * a + q], recv_sem=frsem.at[3 * a + q],
                device_id=sib, device_id_type=MESH).wait_recv()
    for cp in sends + passed:
        cp.wait_send()


def _wgather_sequencer(bufs, name, collective_id):
    n = len(bufs)
    refs = [jax.new_ref(b, memory_space=pltpu.MemorySpace.HBM) for b in bufs]
    dma = pltpu.SemaphoreType.DMA((3 * n,))

    @pl.kernel(mesh=plsc.ScalarSubcoreMesh(axis_name="sequencer", num_cores=1), name=name,
               scratch_types=(dma, dma, dma, dma), compiler_params=pltpu.CompilerParams(collective_id=collective_id))
    def launch(ssem, rsem, fssem, frsem):
        x, y, c = _pos()
        barrier = pltpu.get_barrier_semaphore()
        for peer in ((1 - x, y, c), (x, 1 - y, c), (1 - x, 1 - y, c), (x, y, 1 - c)):
            pl.semaphore_signal(barrier, inc=1, device_id=peer, device_id_type=MESH)
        pl.semaphore_wait(barrier, 4)
        _wgather_copies(refs, [b.shape[1] for b in bufs], ssem, rsem, fssem, frsem)

    launch()
    return [r[...] for r in refs]


def _norm_mod(x, g, sc, sh, name):
    s_len = x.shape[0]
    ts = _tile(s_len, TS_WGRAD)

    def body(x_ref, g_ref, sc_ref, sh_ref, h_ref):
        xv = x_ref[...]
        rinv = lax.rsqrt(jnp.mean(xv * xv, axis=-1, keepdims=True) + RMS_EPS)
        h_ref[...] = (xv * rinv * (g_ref[...] * (1.0 + sc_ref[...])) + sh_ref[...]).astype(BF16)

    row = pl.BlockSpec((ts, D), lambda i: (i, 0))
    vec = pl.BlockSpec((1, D), lambda i: (0, 0))
    return pl.pallas_call(
        body, name=name, grid=(s_len // ts,), in_specs=[row, vec, vec, vec], out_specs=row,
        out_shape=jax.ShapeDtypeStruct((s_len, D), BF16), compiler_params=_cp(("parallel",)),
    )(x, g, sc, sh)


def _l0_fwd(h0_all, x, w_in, gate, cw, cb, wa, ba, wx, bx, lam, sw, wo):
    s_len, nb = x.shape[0], w_in.shape[2]
    ts = _tile(s_len, TS_MIX)
    n_t = s_len // ts
    hl = SUBLANES

    def body(h0_ref, xb_ref, win_ref, gate_ref, cw_ref, cb_ref, wa_ref, ba_ref, wx_ref, bx_ref,
             lam_ref, sw_ref, wo_ref, x1_ref, h_ref, y_ref, xc_ref, cz_ref, r_ref, ig_ref, p_ref,
             pcur, pnext, cxa, czz, chh):
        i = pl.program_id(0)

        @pl.when(i == 0)
        def _():
            cxa[...] = jnp.zeros_like(cxa)
            czz[...] = jnp.zeros_like(czz)
            chh[...] = jnp.zeros_like(chh)
            pnext[...] = jnp.zeros_like(pnext)

        pcur[...] = pnext[...]
        h0 = h0_ref[...]

        def project(k, c0, cn):
            def emit():
                pk = jnp.dot(h0, win_ref[k, :, c0:c0 + cn], preferred_element_type=F32).astype(BF16)
                p_ref[:, k * nb + c0:k * nb + c0 + cn] = pk
                pnext[:, k * nb + c0:k * nb + c0 + cn] = pk
            return emit

        def mixer():
            piece = lambda k: pcur[:, k * D:(k + 1) * D].astype(F32)
            xa = piece(0)
            rows = _rows(ts, D)
            taps = _conv_taps(jnp.concatenate([cxa[...], xa], axis=0), hl, ts, 4)
            xc = cb_ref[...] + sum(cw_ref[k:k + 1, :] * taps[k] for k in range(4))
            xc_ref[...] = xc.astype(BF16)
            r, ig = _lru_gates(xc, wa_ref, ba_ref[...], wx_ref, bx_ref[...])
            r_ref[...] = r.astype(BF16)
            ig_ref[...] = ig.astype(BF16)
            a, m, _ = _lru_decay(r, _softplus_neg(lam_ref[...]), (rows == 0) & (i == 1))
            yield 0.26
            h = _run(_scan_fwd_steps(a, m * ig * xc, chh[hl - 1:hl, :]))
            yield 0.51
            gcp, v = piece(3), piece(4)
            z = gcp * v
            ztaps = _conv_taps(jnp.concatenate([czz[...], z], axis=0), hl, ts, 3)
            cz = sum(sw_ref[k:k + 1, :] * ztaps[k] for k in range(3))
            cz_ref[...] = cz.astype(BF16)
            yb = piece(2) * cz
            ga, gb = piece(1), piece(5)
            y = jnp.concatenate([h * (ga * _sigmoid(ga)), yb * (gb * _sigmoid(gb))], axis=1).astype(BF16)
            yield 0.76
            y_ref[...] = y
            x1_ref[...] = xb_ref[...] + gate_ref[...] * jnp.dot(y, wo_ref[...], preferred_element_type=F32)
            h_ref[...] = h.astype(BF16)
            cxa[...] = xa[ts - hl:, :]
            czz[...] = z[ts - hl:, :]
            chh[...] = jnp.where(i > 0, h[ts - hl:, :], 0.0)

        _paired(mixer(), [project(k, 0, nb) for k in range(N_CHIP)])

    def full(a):
        return pl.BlockSpec(a.shape, lambda i: (0,) * a.ndim)

    ahead = lambda w: pl.BlockSpec((ts, w), lambda i: (jnp.minimum(i, n_t - 1), 0))
    behind = lambda w: pl.BlockSpec((ts, w), lambda i: (jnp.maximum(i - 1, 0), 0))
    args = (h0_all, x, w_in, gate, cw, cb, wa, ba, wx, bx, lam, sw, wo)
    return pl.pallas_call(
        body, name="l0_fwd", grid=(n_t + 1,),
        in_specs=[ahead(D), behind(D)] + [full(a) for a in args[2:]],
        out_specs=[behind(D), behind(D), behind(2 * D)] + [behind(D)] * 4 + [ahead(N_CHIP * nb)],
        out_shape=[jax.ShapeDtypeStruct((s_len, D), F32), jax.ShapeDtypeStruct((s_len, D), BF16),
                   jax.ShapeDtypeStruct((s_len, 2 * D), BF16)] + [jax.ShapeDtypeStruct((s_len, D), BF16)] * 4
        + [jax.ShapeDtypeStruct((s_len, N_CHIP * nb), BF16)],
        scratch_shapes=[pltpu.VMEM((ts, N_CHIP * nb), BF16)] * 2 + [pltpu.VMEM((hl, D), F32)] * 3,
        compiler_params=_cp(("arbitrary",)),
    )(*args)


def _l1_fwd(x1, g, sc, sh, w_in, tgt, gate, wg, bg, scale, wo, gf):
    s_len, nb = x1.shape[0], w_in.shape[2]
    ts = _tile(s_len, TS_MIX)
    n_t = s_len // ts
    pw, gd, hl = 2 * D, POOL_GROUP_DIM, POOL_HALO

    def body(xa_ref, xb_ref, t_ref, g_ref, sc_ref, sh_ref, win_ref, gate_ref, wg_ref, bg_ref, scl_ref, wo_ref, gf_ref,
             d_ref, mx_ref, y_ref, dx_ref, loss_ref, dgf_ref, h1_ref, p_ref, pcur, pnext, cv):
        i = pl.program_id(0)

        @pl.when(i == 0)
        def _():
            cv[...] = jnp.zeros_like(cv)
            loss_ref[...] = jnp.zeros_like(loss_ref)
            dgf_ref[...] = jnp.zeros_like(dgf_ref)
            pnext[...] = jnp.zeros_like(pnext)

        pcur[...] = pnext[...]
        xv = xa_ref[...]
        rinv = lax.rsqrt(jnp.mean(xv * xv, axis=-1, keepdims=True) + RMS_EPS)
        h1 = (xv * rinv * (g_ref[...] * (1.0 + sc_ref[...])) + sh_ref[...]).astype(BF16)
        h1_ref[...] = h1

        def project(k):
            def emit():
                pk = jnp.dot(h1, win_ref[k], preferred_element_type=F32).astype(BF16)
                p_ref[:, k * nb:(k + 1) * nb] = pk
                pnext[:, k * nb:(k + 1) * nb] = pk
            return emit

        def mixer():
            v = pcur[:, 0:pw].astype(F32)
            sums = _window_sums(jnp.concatenate([cv[...], v], axis=0), _down)
            inv = _pool_inv_counts(jnp.maximum(i - 1, 0) * ts, ts)
            dd = [sums[k][hl:hl + ts] * inv[k] - v[:, k * gd:(k + 1) * gd] for k in range(4)]
            d_ref[...] = jnp.concatenate(dd, axis=1).astype(BF16)
            yield 0.26
            mixed = jnp.concatenate(
                [jnp.dot(dd[k].astype(BF16), wg_ref[k], preferred_element_type=F32) for k in range(4)], axis=1) + bg_ref[...]
            mx_ref[...] = mixed.astype(BF16)
            gg = pcur[:, pw:2 * pw].astype(F32)
            y = (mixed * scl_ref[...] * (gg * _sigmoid(gg))).astype(BF16)
            y_ref[...] = y
            yield 0.51
            x2 = xb_ref[...] + gate_ref[...] * jnp.dot(y, wo_ref[...], preferred_element_type=F32)
            yield 0.76
            r2 = lax.rsqrt(jnp.mean(x2 * x2, axis=-1, keepdims=True) + RMS_EPS)
            n2 = x2 * r2
            err = n2 * gf_ref[...] - t_ref[...]
            loss_ref[...] += jnp.where(i > 0, jnp.sum(err * err, axis=0, keepdims=True), 0.0)
            dyf = err * (1.0 / D)
            dgf_ref[...] += jnp.where(i > 0, jnp.sum(dyf * n2, axis=0, keepdims=True), 0.0)
            dn = dyf * gf_ref[...]
            dx_ref[...] = r2 * (dn - n2 * jnp.mean(dn * n2, axis=-1, keepdims=True))
            cv[...] = v[ts - hl:, :]

        _paired(mixer(), [project(k) for k in range(N_CHIP)])

    def full(a):
        return pl.BlockSpec(a.shape, lambda i: (0,) * a.ndim)

    ahead = lambda w: pl.BlockSpec((ts, w), lambda i: (jnp.minimum(i, n_t - 1), 0))
    behind = lambda w: pl.BlockSpec((ts, w), lambda i: (jnp.maximum(i - 1, 0), 0))
    acc = pl.BlockSpec((1, D), lambda i: (0, 0))
    args = (x1, x1, tgt, g, sc, sh, w_in, gate, wg, bg, scale, wo, gf)
    return pl.pallas_call(
        body, name="l1_fwd", grid=(n_t + 1,),
        in_specs=[ahead(D), behind(D), behind(D)] + [full(a) for a in args[3:]],
        out_specs=[behind(pw), behind(pw), behind(pw), behind(D), acc, acc, ahead(D), ahead(N_CHIP * nb)],
        out_shape=[jax.ShapeDtypeStruct((s_len, pw), BF16)] * 3 + [jax.ShapeDtypeStruct((s_len, D), F32)]
        + [jax.ShapeDtypeStruct((1, D), F32)] * 2
        + [jax.ShapeDtypeStruct((s_len, D), BF16), jax.ShapeDtypeStruct((s_len, N_CHIP * nb), BF16)],
        scratch_shapes=[pltpu.VMEM((ts, N_CHIP * nb), BF16)] * 2 + [pltpu.VMEM((hl, pw), F32)],
        compiler_params=_cp(("arbitrary",)),
    )(*args)


def _l1_bwd_mix(dx2, proj, mixed, y, dpool, gate, wg, scale, wo):
    s_len = dx2.shape[0]
    n_sub = 2
    ts = _tile(s_len, n_sub * TS_MIX)
    sub = ts // n_sub
    n_t = s_len // ts
    pw, gd, hl = 2 * D, POOL_GROUP_DIM, POOL_HALO

    def body(dx_ref, gg_ref, mx_ref, y_ref, d_ref, gate_ref, wg_ref, sc_ref, wo_ref,
             dp_ref, mt_ref, dwg_ref, dsc_ref, dbg_ref, cq):
        i = pl.program_id(0)

        @pl.when(i == 0)
        def _():
            cq[...] = jnp.zeros_like(cq)
            dsc_ref[...] = jnp.zeros_like(dsc_ref)
            dbg_ref[...] = jnp.zeros_like(dbg_ref)
            mt_ref[...] = jnp.zeros_like(mt_ref)
            dwg_ref[...] = jnp.zeros_like(dwg_ref)

        ahead_rows, dm_parts = {}, {}
        dxb_all = dx_ref[...].astype(BF16)

        def wgrad_out(k):
            mt_ref[k] += lax.dot_general(y_ref[:, k * gd:(k + 1) * gd], dxb_all, TN, preferred_element_type=F32)

        def chain(j):
            rows = slice(j * sub, (j + 1) * sub)
            dxv = dx_ref[rows, :]
            dy = lax.dot_general((gate_ref[...] * dxv).astype(BF16), wo_ref[...], NT, preferred_element_type=F32)
            yield
            gg = gg_ref[rows, :].astype(F32)
            mixed = mx_ref[rows, :].astype(F32)
            s = _sigmoid(gg)
            sg = gg * s
            dym = dy * mixed
            dmixed = dy * sc_ref[...] * sg
            dsc_ref[...] += jnp.sum(dym * sg, axis=0, keepdims=True)
            dbg_ref[...] += jnp.sum(dmixed, axis=0, keepdims=True)
            dmb = dmixed.astype(BF16)
            dm_parts[j] = dmb
            dp_ref[rows, pw:2 * pw] = (dym * sc_ref[...] * (s + sg * (1.0 - s))).astype(BF16)
            yield
            inv = _pool_inv_counts((n_t - 1 - i) * ts + j * sub, sub)
            dd = [lax.dot_general(dmb[:, k * gd:(k + 1) * gd], wg_ref[k], NT, preferred_element_type=F32)
                  for k in range(4)]
            q = jnp.concatenate([dd[k] * inv[k] for k in range(4)], axis=1)
            ahead_rows[j] = q[0:hl, :]
            yield
            behind_q = cq[...] if j == n_sub - 1 else ahead_rows[j + 1]
            sums = _window_sums(jnp.concatenate([q, behind_q], axis=0), _up)
            dp_ref[rows, 0:pw] = jnp.concatenate([sums[k][0:sub] - dd[k] for k in range(4)], axis=1).astype(BF16)

        chains = [chain(j) for j in reversed(range(n_sub))]
        for phase in range(4):
            for ch in chains:
                next(ch, None)
            wgrad_out(phase)
            if phase == 1:
                dmb_all = jnp.concatenate([dm_parts[j] for j in range(n_sub)], axis=0)
                for k in range(4):
                    cols = slice(k * gd, (k + 1) * gd)
                    dwg_ref[k] += lax.dot_general(d_ref[:, cols], dmb_all[:, cols], TN, preferred_element_type=F32)
        cq[...] = ahead_rows[0]

    def full(a):
        return pl.BlockSpec(a.shape, lambda i: (0,) * a.ndim)

    rev = lambda w, j=0: pl.BlockSpec((ts, w), lambda i: (n_t - 1 - i, j))
    acc = pl.BlockSpec((1, pw), lambda i: (0, 0))
    return pl.pallas_call(
        body, name="l1_bwd_mix", grid=(n_t,),
        in_specs=[rev(D), rev(pw, 1), rev(pw), rev(pw), rev(pw)] + [full(a) for a in (gate, wg, scale, wo)],
        out_specs=[rev(2 * pw), pl.BlockSpec((N_CHIP, gd, D), lambda i: (0, 0, 0)),
                   pl.BlockSpec((4, gd, gd), lambda i: (0, 0, 0)), acc, acc],
        out_shape=[jax.ShapeDtypeStruct((s_len, 2 * pw), BF16), jax.ShapeDtypeStruct((N_CHIP, gd, D), F32),
                   jax.ShapeDtypeStruct((4, gd, gd), F32),
                   jax.ShapeDtypeStruct((1, pw), F32), jax.ShapeDtypeStruct((1, pw), F32)],
        scratch_shapes=[pltpu.VMEM((hl, pw), F32)],
        compiler_params=_cp(("arbitrary",)),
    )(dx2, proj, mixed, y, dpool, gate, wg, scale, wo)


def _l0_bwd_mix(dx1, proj, hst, y, xc, cz, rg, ig_, gate, cw, wa, wx, lam, sw, wo):
    s_len = dx1.shape[0]
    ts = _tile(s_len, TS_MIX)
    n_t = s_len // ts
    hl, hb = SUBLANES, BF16_ROWS
    yb_w = 2 * D // N_CHIP

    def body(dx_ref, p_ref, h_ref, hh_ref, y_ref, xc_ref, cz_ref, r_ref, ig_ref, gate_ref, cw_ref, wa_ref, wx_ref,
             lam_ref, sw_ref, wo_ref, dp_ref, mt_ref, dwa_ref, dwx_ref, sm_ref, cg, cdxc, cdcz, ca):
        i = pl.program_id(0)
        ri = n_t - 1 - i

        @pl.when(i == 0)
        def _():
            cg[...] = jnp.zeros_like(cg)
            ca[...] = jnp.zeros_like(ca)
            cdxc[...] = jnp.zeros_like(cdxc)
            cdcz[...] = jnp.zeros_like(cdcz)
            sm_ref[...] = jnp.zeros_like(sm_ref)
            mt_ref[...] = jnp.zeros_like(mt_ref)
            dwa_ref[...] = jnp.zeros_like(dwa_ref)
            dwx_ref[...] = jnp.zeros_like(dwx_ref)

        dxb = dx_ref[...].astype(BF16)
        has_prev = (ri > 0).astype(F32)
        dy = lax.dot_general((gate_ref[...] * dx_ref[...]).astype(BF16), wo_ref[...], NT, preferred_element_type=F32)
        n_half = 2
        wd = D // n_half
        heads_half = LRU_HEADS // n_half
        for c in range(n_half):
            cs = slice(c * wd, (c + 1) * wd)
            col = lambda k: slice(k * D + c * wd, k * D + (c + 1) * wd)
            mt_ref[2 * c] += lax.dot_general(y_ref[:, 2 * c * yb_w:(2 * c + 1) * yb_w], dxb, TN, preferred_element_type=F32)
            xa, ga, gbp, gcp, v, gb = [p_ref[:, col(k)].astype(F32) for k in range(6)]
            rows = _rows(ts, wd)
            first = (rows == 0) & (ri == 0)
            xc = xc_ref[:, cs].astype(F32)
            cz = cz_ref[:, cs].astype(F32)
            r = r_ref[:, cs].astype(F32)
            ig = ig_ref[:, cs].astype(F32)
            sp = _softplus_neg(lam_ref[:, cs])
            a, m, inv_m = _lru_decay(r, sp, first)
            z = gcp * v
            h = h_ref[:, cs].astype(F32)
            hprev = _down(jnp.concatenate([hh_ref[:, cs].astype(F32)[hb - hl:hb] * has_prev, h], axis=0), 1)[hl:hl + ts]
            dya_pre, dyb_pre = dy[:, cs], dy[:, D + c * wd:D + (c + 1) * wd]
            s_a, s_b = _sigmoid(ga), _sigmoid(gb)
            silu_a, silu_b = ga * s_a, gb * s_b
            dp_ref[:, col(1)] = (dya_pre * h * (s_a + silu_a * (1.0 - s_a))).astype(BF16)
            dp_ref[:, col(5)] = (dyb_pre * (gbp * cz) * (s_b + silu_b * (1.0 - s_b))).astype(BF16)
            dya = dya_pre * silu_a
            dyb = dyb_pre * silu_b
            dp_ref[:, col(2)] = (dyb * cz).astype(BF16)
            dcz = dyb * gbp
            dcz_ext = jnp.concatenate([dcz, cdcz[:, cs]], axis=0)
            dcz_taps = [_up(dcz_ext, 2 - k)[0:ts] for k in range(3)]
            for k in range(3):
                sm_ref[8 + k:9 + k, cs] += jnp.sum(z * dcz_taps[k], axis=0, keepdims=True)
            dz = sum(sw_ref[k:k + 1, cs] * dcz_taps[k] for k in range(3))
            dp_ref[:, col(3)] = (dz * v).astype(BF16)
            dp_ref[:, col(4)] = (dz * gcp).astype(BF16)
            cdcz[:, cs] = dcz[0:hl, :]
            alpha = _up(jnp.concatenate([a, ca[:, cs]], axis=0), 1)[0:ts]
            mt_ref[2 * c + 1] += lax.dot_general(y_ref[:, (2 * c + 1) * yb_w:(2 * c + 2) * yb_w], dxb, TN,
                                                 preferred_element_type=F32)
            dh = _run(_scan_rev_steps(alpha, dya, cg[0:1, cs]))
            cg[:, cs] = dh[0:hl, :]
            ca[:, cs] = a[0:hl, :]
            da = dh * hprev
            dhx = dh * xc
            dm = dhx * ig
            di = dhx * m
            dxc = dh * (m * ig)
            dl = a * (da - jnp.where(first, 0.0, dm * a * inv_m))
            dlr = dl * r
            sm_ref[7:8, cs] += jnp.sum(dlr, axis=0, keepdims=True) * (-LRU_C)
            dpa = dlr * (sp * (-LRU_C)) * (1.0 - r)
            dpx = di * ig * (1.0 - ig)
            sm_ref[5:6, cs] += jnp.sum(dpa, axis=0, keepdims=True)
            sm_ref[6:7, cs] += jnp.sum(dpx, axis=0, keepdims=True)
            dpa_b, dpx_b, xc_b = dpa.astype(BF16), dpx.astype(BF16), xc.astype(BF16)
            back = []
            for j in range(heads_half):
                hd = c * heads_half + j
                sl = slice(j * LRU_HEAD_DIM, (j + 1) * LRU_HEAD_DIM)
                back.append(lax.dot_general(dpa_b[:, sl], wa_ref[hd], NT, preferred_element_type=F32)
                            + lax.dot_general(dpx_b[:, sl], wx_ref[hd], NT, preferred_element_type=F32))
                dwa_ref[hd] += lax.dot_general(xc_b[:, sl], dpa_b[:, sl], TN, preferred_element_type=F32)
                dwx_ref[hd] += lax.dot_general(xc_b[:, sl], dpx_b[:, sl], TN, preferred_element_type=F32)
            dxc = dxc + jnp.concatenate(back, axis=1)
            sm_ref[4:5, cs] += jnp.sum(dxc, axis=0, keepdims=True)
            dxc_ext = jnp.concatenate([dxc, cdxc[:, cs]], axis=0)
            dxc_taps = [_up(dxc_ext, 3 - k)[0:ts] for k in range(4)]
            for k in range(4):
                sm_ref[k:k + 1, cs] += jnp.sum(xa * dxc_taps[k], axis=0, keepdims=True)
            dp_ref[:, col(0)] = sum(cw_ref[k:k + 1, cs] * dxc_taps[k] for k in range(4)).astype(BF16)
            cdxc[:, cs] = dxc[0:hl, :]

    def full(a):
        return pl.BlockSpec(a.shape, lambda i: (0,) * a.ndim)

    rev = lambda w: pl.BlockSpec((ts, w), lambda i: (n_t - 1 - i, 0))
    halo = lambda w: pl.BlockSpec((hb, w), lambda i: (jnp.maximum((n_t - 1 - i) * (ts // hb) - 1, 0), 0))
    return pl.pallas_call(
        body, name="l0_bwd_mix", grid=(n_t,),
        in_specs=[rev(D), rev(6 * D), rev(D), halo(D), rev(2 * D), rev(D), rev(D), rev(D), rev(D)]
        + [full(a) for a in (gate, cw, wa, wx, lam, sw, wo)],
        out_specs=[rev(6 * D), pl.BlockSpec((N_CHIP, yb_w, D), lambda i: (0, 0, 0)),
                   pl.BlockSpec(wa.shape, lambda i: (0, 0, 0)), pl.BlockSpec(wa.shape, lambda i: (0, 0, 0)),
                   pl.BlockSpec((2 * SUBLANES, D), lambda i: (0, 0))],
        out_shape=[jax.ShapeDtypeStruct((s_len, 6 * D), BF16), jax.ShapeDtypeStruct((N_CHIP, yb_w, D), F32),
                   jax.ShapeDtypeStruct(wa.shape, F32), jax.ShapeDtypeStruct(wa.shape, F32),
                   jax.ShapeDtypeStruct((2 * SUBLANES, D), F32)],
        scratch_shapes=[pltpu.VMEM((hl, D), F32)] * 4,
        compiler_params=_cp(("arbitrary",)),
    )(dx1, proj, hst, hst, y, xc, cz, rg, ig_, gate, cw, wa, wx, lam, sw, wo)


def _dgrad_norm(dproj, w, x, dres, g, sc, name, after=None):
    s_len, nb = x.shape[0], w.shape[2]
    ts = _tile(s_len, TS_DGRAD)
    order = [] if after is None else [after]

    def body(dp_ref, w_ref, x_ref, dr_ref, g_ref, sc_ref, *rest):
        dx_ref, s1_ref, s2_ref = rest[len(order):]

        @pl.when(pl.program_id(0) == 0)
        def _():
            s1_ref[...] = jnp.zeros_like(s1_ref)
            s2_ref[...] = jnp.zeros_like(s2_ref)

        dh = sum(lax.dot_general(dp_ref[:, k * nb:(k + 1) * nb], w_ref[k], NT, preferred_element_type=F32)
                 for k in range(N_CHIP))
        xv = x_ref[...]
        r = lax.rsqrt(jnp.mean(xv * xv, axis=-1, keepdims=True) + RMS_EPS)
        n = xv * r
        s1_ref[...] += jnp.sum(dh, axis=0, keepdims=True)
        s2_ref[...] += jnp.sum(dh * n, axis=0, keepdims=True)
        dn = dh * (g_ref[...] * (1.0 + sc_ref[...]))
        dx_ref[...] = dr_ref[...] + r * (dn - n * jnp.mean(dn * n, axis=-1, keepdims=True))

    row = lambda wd: pl.BlockSpec((ts, wd), lambda i: (i, 0))
    vec = pl.BlockSpec((1, D), lambda i: (0, 0))
    return pl.pallas_call(
        body, name=name, grid=(s_len // ts,),
        in_specs=[row(N_CHIP * nb), pl.BlockSpec(w.shape, lambda i: (0, 0, 0)), row(D), row(D), vec, vec]
        + [ANY] * len(order),
        out_specs=[row(D), vec, vec],
        out_shape=[jax.ShapeDtypeStruct((s_len, D), F32)] + [jax.ShapeDtypeStruct((1, D), F32)] * 2,
        compiler_params=_cp(("arbitrary",)),
    )(dproj, w, x, dres, g, sc, *order)


def _wgrad(a, b, groups, ka, nb, a_col, b_col, name, after=None):
    s_len = a.shape[0]
    ts = _tile(s_len, TS_WGRAD * (2 if ka * nb <= D * D else 1))
    n_s = s_len // ts
    order = [] if after is None else [after]

    def body(a_ref, b_ref, *rest):
        o_ref, wire_ref = rest[-2:]

        @pl.when(pl.program_id(1) == 0)
        def _():
            o_ref[...] = jnp.zeros_like(o_ref)

        o_ref[...] += lax.dot_general(a_ref[...].astype(BF16), b_ref[...].astype(BF16), TN, preferred_element_type=F32)

        @pl.when(pl.program_id(1) == n_s - 1)
        def _():
            wire_ref[...] = o_ref[...].astype(GRAD_WIRE_DTYPE)

    blk = pl.BlockSpec((None, ka, nb), lambda g, s: (g, 0, 0))
    return pl.pallas_call(
        body, name=name, grid=(groups, n_s),
        in_specs=[pl.BlockSpec((ts, ka), lambda g, s: (s, a_col(g))), pl.BlockSpec((ts, nb), lambda g, s: (s, b_col(g)))]
        + [ANY] * len(order),
        out_specs=[blk, blk],
        out_shape=[jax.ShapeDtypeStruct((groups, ka, nb), F32), jax.ShapeDtypeStruct((groups, ka, nb), GRAD_WIRE_DTYPE)],
        compiler_params=_cp(("parallel", "arbitrary")),
    )(a, b, *order)


def _wo_final(mt, wo, gate, name):
    rb = mt.shape[1]

    def body(m_ref, w_ref, gate_ref, dw_ref, wire_ref, dg_ref):
        @pl.when(pl.program_id(0) == 0)
        def _():
            dg_ref[...] = jnp.zeros_like(dg_ref)

        mv = m_ref[...]
        dw = mv * gate_ref[...]
        dw_ref[...] = dw
        wire_ref[...] = dw.astype(GRAD_WIRE_DTYPE)
        dg_ref[...] += jnp.sum(mv * w_ref[...].astype(F32), axis=0, keepdims=True)

    blk = pl.BlockSpec((None, rb, D), lambda k: (k, 0, 0))
    vec = pl.BlockSpec((1, D), lambda k: (0, 0))
    return pl.pallas_call(
        body, name=name, grid=(N_CHIP,), in_specs=[blk, blk, vec], out_specs=[blk, blk, vec],
        out_shape=[jax.ShapeDtypeStruct(mt.shape, F32), jax.ShapeDtypeStruct(mt.shape, GRAD_WIRE_DTYPE),
                   jax.ShapeDtypeStruct((1, D), F32)],
        compiler_params=_cp(("arbitrary",)),
    )(mt, wo, gate)


ROW_NORM_G, ROW_CONV_W, ROW_CONV_B, ROW_B_A, ROW_B_X, ROW_LAMBDA, ROW_SC_W, ROW_POOL_B, ROW_POOL_S, ROW_FINAL_G = (
    0, 2, 6, 7, 8, 9, 10, 13, 15, 17)
ROW_LOSS = 18
DMOD_W = 6 * D // SUBLANES


def _small_pack(s1_0, s2_0, s1_1, s2_1, sm0, dsc1, dbg1, dgf, losscols, dgate0, dgate1, norm_g, sc0, sc1, lam):
    def body(s1_0r, s2_0r, s1_1r, s2_1r, sm, dsc, dbg, dgfr, lcols, dg0, dg1, ng, sc0r, sc1r, lamr, buf, dmod):
        buf[...] = jnp.zeros_like(buf)
        buf[0:1, :] = s2_0r[...] * (1.0 + sc0r[...])
        buf[1:2, :] = s2_1r[...] * (1.0 + sc1r[...])
        buf[ROW_CONV_W:ROW_CONV_W + 4, :] = sm[0:4, :]
        buf[ROW_CONV_B:ROW_CONV_B + 1, :] = sm[4:5, :]
        buf[ROW_B_A:ROW_B_A + 1, :] = sm[5:6, :]
        buf[ROW_B_X:ROW_B_X + 1, :] = sm[6:7, :]
        buf[ROW_LAMBDA:ROW_LAMBDA + 1, :] = -sm[7:8, :] * _sigmoid(-lamr[...])
        buf[ROW_SC_W:ROW_SC_W + 3, :] = sm[8:11, :]
        for k in range(2):
            buf[ROW_POOL_B + k:ROW_POOL_B + k + 1, :] = dbg[:, k * D:(k + 1) * D]
            buf[ROW_POOL_S + k:ROW_POOL_S + k + 1, :] = dsc[:, k * D:(k + 1) * D]
        buf[ROW_FINAL_G:ROW_FINAL_G + 1, :] = dgfr[...]
        pieces = (s1_0r[...], s2_0r[...] * ng[0:1, :], dg0[...], s1_1r[...], s2_1r[...] * ng[1:2, :], dg1[...])
        flat = jnp.concatenate(pieces, axis=1)
        for r in range(SUBLANES):
            dmod[r:r + 1, :] = flat[:, r * DMOD_W:(r + 1) * DMOD_W]
        buf[ROW_LOSS:ROW_LOSS + 1, :] = jnp.broadcast_to(jnp.sum(lcols[...], axis=1, keepdims=True) * (0.5 / D), (1, D))

    args = (s1_0, s2_0, s1_1, s2_1, sm0, dsc1, dbg1, dgf, losscols, dgate0, dgate1, norm_g, sc0, sc1, lam)
    return pl.pallas_call(
        body, name="small_pack", in_specs=[VMEM] * len(args), out_specs=[VMEM] * 2,
        out_shape=[jax.ShapeDtypeStruct((SMALL_ROWS, D), F32), jax.ShapeDtypeStruct((SUBLANES, DMOD_W), F32)],
        compiler_params=_cp(),
    )(*args)


def _small_comm(buf_a, buf_b, dmod8):
    ra, rb = buf_a.shape[0] // N_DEV, buf_b.shape[0] // N_DEV
    wb = buf_b.shape[1]

    def body(a_ref, b_ref, dm_ref, oa_ref, ob_ref, odm_ref, ina, inb, dslot, sa, sb, s1, r1, s2, r2):
        x, y, c = _pos()
        me = 4 * x + 2 * y + c
        peers = []
        for r in range(1, N_DEV):
            fx, fy, fc = (r >> 2) & 1, (r >> 1) & 1, r & 1
            px, py, pc = _flip(x, fx), _flip(y, fy), _flip(c, fc)
            peers.append(((px, py, pc), 4 * px + 2 * py + pc))
        seg_a = lambda d: pl.ds(pl.multiple_of(d * ra, SUBLANES), ra)
        seg_b = lambda d: pl.ds(pl.multiple_of(d * rb, SUBLANES), rb)
        first = []
        for r, (peer, pid) in enumerate(peers):
            for k, (src, dst) in enumerate(((a_ref.at[seg_a(pid), :], ina.at[r]), (b_ref.at[seg_b(pid), :], inb.at[r]),
                                            (dm_ref, dslot.at[me]))):
                cp = pltpu.make_async_remote_copy(src_ref=src, dst_ref=dst, send_sem=s1.at[3 * r + k],
                                                  recv_sem=r1.at[3 * r + k], device_id=peer, device_id_type=MESH)
                cp.start()
                first.append(cp)
        dslot[me] = dm_ref[...]
        for cp in first:
            cp.wait()
        acc_a, acc_b = a_ref[seg_a(me), :], b_ref[seg_b(me), :]
        for r in range(N_DEV - 1):
            acc_a = acc_a + ina[r]
            acc_b = acc_b + inb[r]
        sa[...] = acc_a
        sb[...] = acc_b
        oa_ref[seg_a(me), :] = acc_a
        ob_ref[seg_b(me), :] = acc_b
        second = []
        for r, (peer, pid) in enumerate(peers):
            for k, (src, dst) in enumerate(((sa, oa_ref.at[seg_a(me), :]), (sb, ob_ref.at[seg_b(me), :]))):
                cp = pltpu.make_async_remote_copy(src_ref=src, dst_ref=dst, send_sem=s2.at[2 * r + k],
                                                  recv_sem=r2.at[2 * r + k], device_id=peer, device_id_type=MESH)
                cp.start()
                second.append(cp)
        odm_ref[...] = dslot[...]
        for cp in second:
            cp.wait()

    nrel = N_DEV - 1
    return pl.pallas_call(
        body, name="small_comm", in_specs=[VMEM] * 3, out_specs=[VMEM] * 3,
        out_shape=[jax.ShapeDtypeStruct(buf_a.shape, F32), jax.ShapeDtypeStruct(buf_b.shape, F32),
                   jax.ShapeDtypeStruct((N_DEV,) + dmod8.shape, F32)],
        scratch_shapes=[pltpu.VMEM((nrel, ra, D), F32), pltpu.VMEM((nrel, rb, wb), F32),
                        pltpu.VMEM((N_DEV,) + dmod8.shape, F32), pltpu.VMEM((ra, D), F32), pltpu.VMEM((rb, wb), F32),
                        pltpu.SemaphoreType.DMA((3 * nrel,)), pltpu.SemaphoreType.DMA((3 * nrel,)),
                        pltpu.SemaphoreType.DMA((2 * nrel,)), pltpu.SemaphoreType.DMA((2 * nrel,))],
        compiler_params=_cp(),
    )(buf_a, buf_b, dmod8)


def _adam(w, g, m, v):
    m2 = ADAM_B1 * m + (1.0 - ADAM_B1) * g
    v2 = ADAM_B2 * v + (1.0 - ADAM_B2) * (g * g)
    m_hat = m2 / (1.0 - ADAM_B1 ** ADAM_STEP)
    v_hat = v2 / (1.0 - ADAM_B2 ** ADAM_STEP)
    return -ADAM_LR * (m_hat / (jnp.sqrt(v_hat) + ADAM_EPS) + ADAM_WD * w), m2, v2


def _small_adam(red_a, red_b, dm_all, params):
    n = len(params)

    def body(*refs):
        ra, rb, dm = refs[:3]
        wmv = refs[3:3 + 3 * n]
        outs = refs[3 + 3 * n:]
        x, y, _ = _pos()
        chip = 2 * x + y

        def shard(row0, nrows, width):
            per_row = D // width
            cands = []
            for k in range(N_CHIP):
                if nrows == 1 or per_row >= N_CHIP:
                    cands.append(ra[row0:row0 + nrows, k * width:(k + 1) * width])
                else:
                    rr, cc = divmod(k * width, D)
                    cands.append(ra[row0 + rr:row0 + rr + 1, cc:cc + width])
            g = cands[0]
            for k in range(1, N_CHIP):
                g = jnp.where(chip == k, cands[k], g)
            return g

        dms = jnp.sum(dm[...], axis=0)
        hw = LRU_HEADS * LRU_HEAD_DIM
        grads = [
            ra[ROW_NORM_G:ROW_NORM_G + 2, :],
            None,
            shard(ROW_CONV_W, 4, D // N_CHIP),
            ra[ROW_CONV_B:ROW_CONV_B + 1, :],
            rb[0:hw, :],
            ra[ROW_B_A:ROW_B_A + 1, :],
            rb[hw:2 * hw, :],
            ra[ROW_B_X:ROW_B_X + 1, :],
            ra[ROW_LAMBDA:ROW_LAMBDA + 1, :],
            shard(ROW_SC_W, 3, D // N_CHIP),
            shard(ROW_POOL_B, 2, 2 * D // N_CHIP),
            shard(ROW_POOL_S, 2, 2 * D // N_CHIP),
            ra[ROW_FINAL_G:ROW_FINAL_G + 1, :],
        ]
        for p in range(n):
            w_ref, m_ref, v_ref = wmv[3 * p:3 * p + 3]
            g_out, d_out, m_out, v_out = outs[4 * p:4 * p + 4]
            if grads[p] is None:
                for r in range(SUBLANES):
                    l, cols = r // N_CHIP, slice((r % N_CHIP) * DMOD_W, (r % N_CHIP + 1) * DMOD_W)
                    g = dms[r:r + 1, :]
                    dl, m2, v2 = _adam(w_ref[l:l + 1, cols], g, m_ref[l:l + 1, cols], v_ref[l:l + 1, cols])
                    g_out[l:l + 1, cols] = g
                    d_out[l:l + 1, cols] = dl
                    m_out[l:l + 1, cols] = m2
                    v_out[l:l + 1, cols] = v2
            else:
                g = grads[p]
                dl, m2, v2 = _adam(w_ref[...], g, m_ref[...], v_ref[...])
                g_out[...] = g
                d_out[...] = dl
                m_out[...] = m2
                v_out[...] = v2

    flat = [a for p in params for a in p]
    return pl.pallas_call(
        body, name="small_adam", in_specs=[VMEM] * (3 + len(flat)), out_specs=[VMEM] * (4 * n),
        out_shape=[jax.ShapeDtypeStruct(p[0].shape, F32) for p in params for _ in range(4)],
        compiler_params=_cp(),
    )(red_a, red_b, dm_all, *flat)


def _modw_adam(ca_t, dm_sh, w, m, v):
    nw = w.shape[2]

    def body(c_ref, d_ref, w_ref, m_ref, v_ref, g_out, d_out, m_out, v_out):
        g = jnp.dot(c_ref[...], d_ref[...], precision=lax.Precision.HIGHEST, preferred_element_type=F32)
        dl, m2, v2 = _adam(w_ref[...], g, m_ref[...], v_ref[...])
        g_out[...] = g
        d_out[...] = dl
        m_out[...] = m2
        v_out[...] = v2

    blk = pl.BlockSpec((None, D, nw), lambda l: (l, 0, 0))
    return pl.pallas_call(
        body, name="modw_adam", grid=(2,),
        in_specs=[pl.BlockSpec((D, SUBLANES), lambda l: (0, 0)), pl.BlockSpec((None, SUBLANES, nw), lambda l: (l, 0, 0)),
                  blk, blk, blk],
        out_specs=[blk] * 4, out_shape=[jax.ShapeDtypeStruct(w.shape, F32)] * 4,
        compiler_params=_cp(("arbitrary",)),
    )(ca_t, dm_sh, w, m, v)


def _exchange(copies, name, out_type, n_sems, args, sequencer, after=None):
    order = [] if after is None else [after]
    n_in, n_out = len(args) + len(order), len(out_type)

    def body(*refs):
        barrier = pltpu.get_barrier_semaphore()
        peers = sequencer[1](*_pos())
        for peer in peers:
            pl.semaphore_signal(barrier, inc=1, device_id=peer, device_id_type=MESH)
        pl.semaphore_wait(barrier, len(peers))
        copies(refs[:n_in], refs[n_in:n_in + n_out], refs[n_in + n_out], refs[n_in + n_out + 1])

    sems = [pltpu.SemaphoreType.DMA((n_sems,))] * 2
    return pl.kernel(body, out_type, mesh=plsc.ScalarSubcoreMesh(axis_name="sequencer", num_cores=1), name=name,
                     scratch_types=sems, compiler_params=pltpu.CompilerParams(collective_id=sequencer[0]))(*args, *order)


def _sibling(x, y, c):
    return [(x, y, 1 - c)]


def _other_chips(x, y, c):
    return [(1 - x, y, c), (x, 1 - y, c), (1 - x, 1 - y, c)]


def _to_wire(g, name, after=None):
    _, rr, cc = g.shape
    rb = min(rr, 256)

    def body(g_ref, *rest):
        rest[-1][...] = g_ref[...].astype(GRAD_WIRE_DTYPE)

    order = [] if after is None else [after]
    blk = pl.BlockSpec((None, rb, cc), lambda k, j: (k, j, 0))
    return pl.pallas_call(
        body, name=name, grid=(N_CHIP, rr // rb), in_specs=[blk] + [ANY] * len(order), out_specs=blk,
        out_shape=jax.ShapeDtypeStruct(g.shape, GRAD_WIRE_DTYPE), compiler_params=_cp(("parallel", "parallel")),
    )(g, *order)


def _chip_scatter(ps, name, collective_id, after=None):
    n = len(ps)

    def copies(ins, outs, ssem, rsem):
        x, y, c = _pos()
        cps = []
        for a in range(n):
            for q, (fx, fy) in enumerate(((1, 0), (0, 1), (1, 1))):
                px, py = _flip(x, fx), _flip(y, fy)
                cp = pltpu.make_async_remote_copy(
                    src_ref=ins[a].at[2 * px + py], dst_ref=outs[a].at[q],
                    send_sem=ssem.at[3 * a + q], recv_sem=rsem.at[3 * a + q], device_id=(px, py, c), device_id_type=MESH)
                cp.start()
                cps.append(cp)
        for cp in cps:
            cp.wait()

    out_type = [jax.ShapeDtypeStruct((N_CHIP - 1,) + p.shape[1:], p.dtype) for p in ps]
    return _exchange(copies, name, out_type, 3 * n, ps, (collective_id, _other_chips), after)


def _add_owner(p, got, chipidx, name, after=None):
    _, hr, cc = p.shape
    rb = min(hr, 256)

    def body(k_ref, p_ref, r_ref, *rest):
        rest[-1][...] = ((p_ref[...].astype(F32) + r_ref[0].astype(F32)) + r_ref[1].astype(F32)) + r_ref[2].astype(F32)

    order = [] if after is None else [after]
    return pl.pallas_call(
        body, name=name,
        grid_spec=pltpu.PrefetchScalarGridSpec(
            num_scalar_prefetch=1, grid=(hr // rb,),
            in_specs=[pl.BlockSpec((None, rb, cc), lambda j, k_ref: (k_ref[0], j, 0)),
                      pl.BlockSpec((N_CHIP - 1, rb, cc), lambda j, k_ref: (0, j, 0))] + [ANY] * len(order),
            out_specs=pl.BlockSpec((rb, cc), lambda j, k_ref: (j, 0))),
        out_shape=jax.ShapeDtypeStruct((hr, cc), F32),
        compiler_params=_cp(("parallel",)),
    )(chipidx, p, got, *order)


def _sib_exchange(ts_, name, collective_id, after=None):
    n = len(ts_)

    def copies(ins, outs, ssem, rsem):
        x, y, c = _pos()
        cps = []
        for a in range(n):
            cp = pltpu.make_async_remote_copy(src_ref=ins[a], dst_ref=outs[a], send_sem=ssem.at[a],
                                              recv_sem=rsem.at[a], device_id=(x, y, 1 - c), device_id_type=MESH)
            cp.start()
            cps.append(cp)
        for cp in cps:
            cp.wait()

    out_type = [jax.ShapeDtypeStruct(t.shape, F32) for t in ts_]
    return _exchange(copies, name, out_type, n, ts_, (collective_id, _sibling), after)


def _adam_2d(w, g_own, g_sib, m, v, name):
    rr, cc = w.shape
    rb = min(rr, 256)

    def body(w_ref, go_ref, gs_ref, m_ref, v_ref, g_out, d_out, m_out, v_out):
        g = go_ref[...] + gs_ref[...]
        dl, m2, v2 = _adam(w_ref[...], g, m_ref[...], v_ref[...])
        g_out[...] = g
        d_out[...] = dl
        m_out[...] = m2
        v_out[...] = v2

    blk = pl.BlockSpec((rb, cc), lambda j: (j, 0))
    return pl.pallas_call(
        body, name=name, grid=(rr // rb,), in_specs=[blk] * 5, out_specs=[blk] * 4,
        out_shape=[jax.ShapeDtypeStruct((rr, cc), F32)] * 4, compiler_params=_cp(("parallel",)),
    )(w, g_own, g_sib, m, v)


def kernel(x, c, norm_g, mod_w, mod_b, hy_w_in, hy_conv_w, hy_conv_b, lru_w_a, lru_b_a, lru_w_x, lru_b_x, lru_lambda, sc_conv_w, hy_w_out, pool_w_in, pool_w_grp, pool_b_grp, pool_scale, pool_w_out, final_g, loss_target, m_norm_g, m_mod_w, m_mod_b, m_hy_w_in, m_hy_conv_w, m_hy_conv_b, m_lru_w_a, m_lru_b_a, m_lru_w_x, m_lru_b_x, m_lru_lambda, m_sc_conv_w, m_hy_w_out, m_pool_w_in, m_pool_w_grp, m_pool_b_grp, m_pool_scale, m_pool_w_out, m_final_g, v_norm_g, v_mod_w, v_mod_b, v_hy_w_in, v_hy_conv_w, v_hy_conv_b, v_lru_w_a, v_lru_b_a, v_lru_w_x, v_lru_b_x, v_lru_lambda, v_sc_conv_w, v_hy_w_out, v_pool_w_in, v_pool_w_grp, v_pool_b_grp, v_pool_scale, v_pool_w_out, v_final_g):
    ax, ay, ac = _pos()
    me = 4 * ax + 2 * ay + ac
    chip = 2 * ax + ay
    xs = x[0]
    tgt = loss_target[0]
    gd = POOL_GROUP_DIM
    kidx = chip.reshape(1).astype(jnp.int32)

    big = [hy_w_in[0], hy_w_out[0], pool_w_in[0], pool_w_grp[0].reshape(4 * 128, gd), pool_w_out[0]]
    w_in0, w_out0 = _wgather_sequencer(
        [_wcast_own_block(w, kidx, f"wcast_own_block_{a}") for a, w in enumerate(big[:2])], "wgather_l0", CIDS_WGATHER[0])

    ca_all, mod_all, small_w = _mod_fwd(jnp.broadcast_to(c, (SUBLANES, D)), mod_w, mod_b,
                                        hy_conv_w[0], sc_conv_w[0], pool_b_grp, pool_scale)
    mod_me = lax.dynamic_index_in_dim(mod_all, me, axis=1, keepdims=False)
    sh0, sc0, gt0 = (mod_me[0:1, k * D:(k + 1) * D] for k in range(3))
    sh1, sc1, gt1 = (mod_me[1:2, k * D:(k + 1) * D] for k in range(3))
    cw = small_w[SW_CONV:SW_CONV + 4, 0:D]
    sw = small_w[SW_SC:SW_SC + 3, 0:D]
    pool_b = small_w[SW_POOL_B:SW_POOL_B + 1, :]
    pool_s = small_w[SW_POOL_S:SW_POOL_S + 1, :]
    g0, g1, gf = norm_g[0:1], norm_g[1:2], final_g.reshape(1, D)
    cb, ba, bx, lam = hy_conv_b, lru_b_a, lru_b_x, lru_lambda

    h0 = _norm_mod(xs, g0, sc0, sh0, "l0_norm")
    wa_b, wx_b = _wcast([lru_w_a[0], lru_w_x[0]])
    w_in1, w_grp, w_out1 = _wgather_sequencer(
        [_wcast_own_block(w, kidx, f"wcast_own_block_{a + 2}", after=(w_out0, h0)) for a, w in enumerate(big[2:])],
        "wgather_l1", CIDS_WGATHER[1])
    w_grp =w_grp.reshape(N_CHIP, 4, 128, gd).transpose(1, 0, 2, 3).reshape(4, gd, gd)

    x1, hst, y0, xc0, cz0, rg0, ig0, proj0 = _l0_fwd(h0, xs, w_in0, gt0, cw, cb, wa_b, ba, wx_b, bx, lam, sw,
                                                     w_out0.reshape(2 * D, D))
    dpool, mixed, y1, dx2, losscols, dgf, h1, proj1 = _l1_fwd(x1, g1, sc1, sh1, w_in1, tgt, gt1, w_grp, pool_b, pool_s,
                                                              w_out1.reshape(2 * D, D), gf)

    def add_owners(grads, got, tag, ids, after):
        own = []
        for a, (g, r) in enumerate(zip(grads, got)):
            own.append(_add_owner(g, r, kidx, f"grad_add_owner_{tag}{a}", own[-1] if own else after))
        return own, _sib_exchange(own, f"grad_sib_exchange_{tag}", ids[1])

    dproj1, mt1, d_wgrp, dsc1, dbg1 = _l1_bwd_mix(dx2, proj1, mixed, y1, dpool, gt1, w_grp, pool_s,
                                                  w_out1.reshape(2 * D, D))
    d_win1, wire_win1 = _wgrad(h1, dproj1, N_CHIP, D, D, lambda g: 0, lambda g: g, "l1_wgrad_in")
    d_wout1, wire_wout1, dgate1 = _wo_final(mt1, w_out1, gt1, "l1_wo_final")
    d_wgrp = d_wgrp.reshape(4, N_CHIP, 128, gd).transpose(1, 0, 2, 3).reshape(N_CHIP, 4 * 128, gd)
    grads_l1 = [d_win1, d_wgrp, d_wout1]
    got_l1 = _chip_scatter([wire_win1, _to_wire(d_wgrp, "grad_to_wire_grp"), wire_wout1], "grad_chip_scatter_l1",
                           CIDS_L1[0])
    dx1, s1_1, s2_1 = _dgrad_norm(dproj1, w_in1, x1, dx2, g1, sc1, "l1_bwd_proj")

    dproj0, mt0, d_wa, d_wx, sm0 = _l0_bwd_mix(dx1, proj0, hst, y0, xc0, cz0, rg0, ig0, gt0, cw, wa_b, wx_b, lam, sw,
                                               w_out0.reshape(2 * D, D))
    sums_l1, sib_l1 = add_owners(grads_l1, got_l1, "l1", CIDS_L1, after=sm0)
    d_win0, wire_win0 = _wgrad(h0, dproj0, N_CHIP, D, 6 * D // N_CHIP, lambda g: 0, lambda g: g, "l0_wgrad_in",
                               after=sums_l1[-1])
    d_wout0, wire_wout0, dgate0 = _wo_final(mt0, w_out0, gt0, "l0_wo_final")
    grads_l0 = [d_win0, d_wout0]
    got_l0 = _chip_scatter([wire_win0, wire_wout0], "grad_chip_scatter_l0", CIDS_L0[0], after=sib_l1[0])
    grad_x, s1_0, s2_0 = _dgrad_norm(dproj0, w_in0, xs, dx1, g0, sc0, "l0_bwd_proj", after=wire_win0)
    sums_l0, sib_l0 = add_owners(grads_l0, got_l0, "l0", CIDS_L0, after=s1_0)

    buf_a, dmod8 = _small_pack(s1_0, s2_0, s1_1, s2_1, sm0, dsc1, dbg1, dgf, losscols, dgate0, dgate1,
                                      norm_g, sc0, sc1, lam)
    hw = LRU_HEADS * LRU_HEAD_DIM
    buf_b = jnp.concatenate([d_wa.reshape(hw, LRU_HEAD_DIM), d_wx.reshape(hw, LRU_HEAD_DIM)], axis=0)
    red_a, red_b, dm_all = _small_comm(buf_a, buf_b, dmod8)
    small = [(norm_g, m_norm_g, v_norm_g), (mod_b, m_mod_b, v_mod_b),
             (hy_conv_w[0], m_hy_conv_w[0], v_hy_conv_w[0]), (hy_conv_b, m_hy_conv_b, v_hy_conv_b),
             tuple(a.reshape(hw, LRU_HEAD_DIM) for a in (lru_w_a, m_lru_w_a, v_lru_w_a)),
             (lru_b_a, m_lru_b_a, v_lru_b_a),
             tuple(a.reshape(hw, LRU_HEAD_DIM) for a in (lru_w_x, m_lru_w_x, v_lru_w_x)),
             (lru_b_x, m_lru_b_x, v_lru_b_x), (lru_lambda, m_lru_lambda, v_lru_lambda),
             (sc_conv_w[0], m_sc_conv_w[0], v_sc_conv_w[0]), (pool_b_grp, m_pool_b_grp, v_pool_b_grp),
             (pool_scale, m_pool_scale, v_pool_scale),
             tuple(a.reshape(1, D) for a in (final_g, m_final_g, v_final_g))]
    small_names = ["norm_g", "mod_b", "hy_conv_w", "hy_conv_b", "lru_w_a", "lru_b_a", "lru_w_x", "lru_b_x",
                   "lru_lambda", "sc_conv_w", "pool_b_grp", "pool_scale", "final_g"]
    small_out = _small_adam(red_a, red_b, dm_all, small)
    res = {}
    shapes = dict(norm_g=norm_g, mod_b=mod_b, hy_conv_w=hy_conv_w, hy_conv_b=hy_conv_b, lru_w_a=lru_w_a, lru_b_a=lru_b_a,
                  lru_w_x=lru_w_x, lru_b_x=lru_b_x, lru_lambda=lru_lambda, sc_conv_w=sc_conv_w, pool_b_grp=pool_b_grp,
                  pool_scale=pool_scale, final_g=final_g)
    for p, nm in enumerate(small_names):
        res[nm] = tuple(o.reshape(shapes[nm].shape) for o in small_out[4 * p:4 * p + 4])

    nw = mod_w.shape[2]
    assert nw == DMOD_W
    dm_sh = jnp.stack([lax.dynamic_index_in_dim(dm_all, N_CHIP * l + chip, axis=1, keepdims=False) for l in range(2)])
    res["mod_w"] = tuple(_modw_adam(ca_all.T, dm_sh, mod_w, m_mod_w, v_mod_w))

    sums = list(sums_l0) + list(sums_l1)
    sib_sums = list(sib_l0) + list(sib_l1)
    big_names = ["hy_w_in", "hy_w_out", "pool_w_in", "pool_w_grp", "pool_w_out"]
    big_wmv = [(hy_w_in, m_hy_w_in, v_hy_w_in), (hy_w_out, m_hy_w_out, v_hy_w_out), (pool_w_in, m_pool_w_in, v_pool_w_in),
               (pool_w_grp, m_pool_w_grp, v_pool_w_grp), (pool_w_out, m_pool_w_out, v_pool_w_out)]
    for a, nm in enumerate(big_names):
        rr, cc = big[a].shape
        w, m, v = (t.reshape(rr, cc) for t in big_wmv[a])
        outs = _adam_2d(w, sums[a], sib_sums[a], m, v, f"adam_{nm}")
        res[nm] = tuple(o.reshape(big_wmv[a][0].shape) for o in outs)

    loss = red_a[ROW_LOSS, 0]
    order = ["norm_g", "mod_w", "mod_b", "hy_w_in", "hy_conv_w", "hy_conv_b", "lru_w_a", "lru_b_a", "lru_w_x", "lru_b_x",
             "lru_lambda", "sc_conv_w", "hy_w_out", "pool_w_in", "pool_w_grp", "pool_b_grp", "pool_scale", "pool_w_out",
             "final_g"]
    return (loss, grad_x[None], *[res[nm][0] for nm in order], *[res[nm][1] for nm in order],
            *[res[nm][2] for nm in order], *[res[nm][3] for nm in order])
```
